```python
import jax, jax.numpy as jnp
from jax import lax
import numpy as np

D_MODEL = 1024
BATCH = 32
SEQ = 2048
DEPTH = 1

CHUNK = 64
Q_BLOCK = 128
N_MEM = 256
EPS = 1e-6

HG_HEADS = 4
HG_DK = 128
HG_DV = 128
HG_WIDTH = HG_HEADS * HG_DK
FOX_HEADS = 8
FOX_DH = 64
FOX_WIDTH = FOX_HEADS * FOX_DH
MEM_HEADS = 4
MEM_DH = 128
MEM_WIDTH = MEM_HEADS * MEM_DH
N_BRANCH = 3
D_FF = 2816
CONV_W = 3

IN_SPLITS = (HG_WIDTH, HG_WIDTH, HG_WIDTH, HG_WIDTH,
             FOX_WIDTH, FOX_WIDTH, FOX_WIDTH, FOX_HEADS,
             MEM_WIDTH, N_BRANCH * D_MODEL)
IN_COLS = 4 * HG_WIDTH + 3 * FOX_WIDTH + FOX_HEADS + MEM_WIDTH + N_BRANCH * D_MODEL

kernel_name = "hybrid_hgrn2_fox_memory_convglu"


def _split_points():
    pts, acc = [], 0
    for s in IN_SPLITS[:-1]:
        acc += s
        pts.append(acc)
    return pts


def _rmsnorm(x, g):
    xf = x.astype(jnp.float32)
    return xf * lax.rsqrt(jnp.mean(xf * xf, axis=-1, keepdims=True) + EPS) * g.astype(jnp.float32)


def _hgrn2_mixer(q, f_logit, i, g_out, lb, norm_g):
    B, T, _ = q.shape
    n = T // CHUNK

    def heads(z):
        return z.reshape(B, n, CHUNK, HG_HEADS, -1).transpose(1, 0, 3, 2, 4)

    f = lb + (1.0 - lb) * jax.nn.sigmoid(f_logit.astype(jnp.float32))
    qh = heads(jax.nn.silu(q.astype(jnp.float32)))
    kh = heads(1.0 - f)
    ih = heads(i.astype(jnp.float32))
    G = jnp.cumsum(heads(jnp.log(f)), axis=3)
    causal = jnp.tril(jnp.ones((CHUNK, CHUNK), dtype=bool))[:, :, None]

    def step(S, inp):
        qc, kc, ic, Gc = inp
        diff = Gc[:, :, :, None, :] - Gc[:, :, None, :, :]
        decay = jnp.exp(jnp.where(causal, diff, -jnp.inf))
        A = jnp.einsum('bhtc,bhsc,bhtsc->bhts', qc, kc, decay)
        o = (jnp.einsum('bhts,bhsv->bhtv', A, ic)
             + jnp.einsum('bhtc,bhcv->bhtv', qc * jnp.exp(Gc), S))
        G_last = Gc[:, :, -1, :]
        S_new = (S * jnp.exp(G_last)[..., None]
                 + jnp.einsum('bhsc,bhsv->bhcv', kc * jnp.exp(G_last[:, :, None, :] - Gc), ic))
        return S_new, o

    S0 = jnp.zeros((B, HG_HEADS, HG_DK, HG_DV), jnp.float32)
    _, o = lax.scan(step, S0, (qh, kh, ih, G))
    o = o.transpose(1, 0, 3, 2, 4).reshape(B, T, HG_HEADS, HG_DV)
    o = _rmsnorm(o, norm_g).reshape(B, T, HG_WIDTH)
    return o * jax.nn.silu(g_out.astype(jnp.float32))


def _fox_mixer(q, k, v, f_logit, f_bias, q_g, k_g):
    B, T, _ = q.shape
    qh = _rmsnorm(q.reshape(B, T, FOX_HEADS, FOX_DH), q_g).transpose(0, 2, 1, 3)
    kh = _rmsnorm(k.reshape(B, T, FOX_HEADS, FOX_DH), k_g).transpose(0, 2, 1, 3)
    vh = v.astype(jnp.float32).reshape(B, T, FOX_HEADS, FOX_DH).transpose(0, 2, 1, 3)
    log_f = jax.nn.log_sigmoid(f_logit.astype(jnp.float32) + f_bias.astype(jnp.float32))
    Fc = jnp.cumsum(log_f, axis=1).transpose(0, 2, 1)
    scale = FOX_DH ** -0.5
    outs = []
    for blk in range(T // Q_BLOCK):
        lo, hi = blk * Q_BLOCK, (blk + 1) * Q_BLOCK
        s = (jnp.einsum('bhqd,bhkd->bhqk', qh[:, :, lo:hi], kh[:, :, :hi]) * scale
             + Fc[:, :, lo:hi, None] - Fc[:, :, None, :hi])
        mask = (lo + jnp.arange(Q_BLOCK))[:, None] >= jnp.arange(hi)[None, :]
        p = jax.nn.softmax(jnp.where(mask, s, -jnp.inf), axis=-1)
        outs.append(jnp.einsum('bhqk,bhkd->bhqd', p, vh[:, :, :hi]))
    o = jnp.concatenate(outs, axis=2)
    return o.transpose(0, 2, 1, 3).reshape(B, T, FOX_WIDTH)


def _memory_mixer(q, mem_kv, q_g, k_g):
    B, T, _ = q.shape
    M = mem_kv.shape[1]
    qh = _rmsnorm(q.reshape(B, T, MEM_HEADS, MEM_DH), q_g)
    mk, mv = jnp.split(mem_kv, 2, axis=-1)
    kh = _rmsnorm(mk.reshape(B, M, MEM_HEADS, MEM_DH), k_g)
    vh = mv.astype(jnp.float32).reshape(B, M, MEM_HEADS, MEM_DH)
    s = jnp.einsum('bthd,bmhd->bhtm', qh, kh) * (MEM_DH ** -0.5)
    p = jax.nn.softmax(s, axis=-1)
    return jnp.einsum('bhtm,bmhd->bthd', p, vh).reshape(B, T, MEM_WIDTH)


def _conv_glu_ffn(h, w_up, conv_w, conv_b, w_down):
    T = h.shape[1]
    a, v = jnp.split(h @ w_up.astype(jnp.float32), 2, axis=-1)
    ap = jnp.pad(a, ((0, 0), (CONV_W - 1, 0), (0, 0)))
    a = sum(ap[:, j:j + T] * conv_w[j].astype(jnp.float32) for j in range(CONV_W)) + conv_b.astype(jnp.float32)
    return (jax.nn.gelu(a, approximate=False) * v) @ w_down.astype(jnp.float32)


def _fwd_setup_inputs(seed: int = 0) -> dict:
    key = jax.random.key(seed)
    ks = jax.random.split(key, 24)
    f32 = jnp.float32
    L = DEPTH

    def nrm(k, shape, fan_in):
        return jax.random.normal(k, shape, f32) * (fan_in ** -0.5)

    def gain(k, shape):
        return 1.0 + 0.02 * jax.random.normal(k, shape, f32)

    return {
        "x": jax.random.normal(ks[0], (BATCH, SEQ, D_MODEL), f32),
        "mem": jax.random.normal(ks[1], (BATCH, N_MEM, D_MODEL), f32),
        "norm_mix_g": gain(ks[2], (L, D_MODEL)),
        "norm_mem_g": gain(ks[3], (L, D_MODEL)),
        "w_in": nrm(ks[4], (L, D_MODEL, IN_COLS), D_MODEL),
        "hgrn_lb_logits": 0.1 * jax.random.normal(ks[5], (L + 1, HG_WIDTH), f32),
        "hgrn_norm_g": gain(ks[6], (L, HG_DV)),
        "fox_f_bias": 1.0 + 0.1 * jax.random.normal(ks[7], (L, FOX_HEADS), f32),
        "fox_q_norm_g": gain(ks[8], (L, FOX_DH)),
        "fox_k_norm_g": gain(ks[9], (L, FOX_DH)),
        "mem_kv_w": nrm(ks[10], (L, D_MODEL, 2 * MEM_WIDTH), D_MODEL),
        "mem_q_norm_g": gain(ks[11], (L, MEM_DH)),
        "mem_k_norm_g": gain(ks[12], (L, MEM_DH)),
        "w_br_hgrn": nrm(ks[13], (L, HG_WIDTH, D_MODEL), HG_WIDTH),
        "w_br_fox": nrm(ks[14], (L, FOX_WIDTH, D_MODEL), FOX_WIDTH),
        "w_br_mem": nrm(ks[15], (L, MEM_WIDTH, D_MODEL), MEM_WIDTH),
        "w_out": nrm(ks[16], (L, D_MODEL, D_MODEL), D_MODEL),
        "norm_ffn_g": gain(ks[17], (L, D_MODEL)),
        "ffn_w_up": nrm(ks[18], (L, D_MODEL, 2 * D_FF), D_MODEL),
        "ffn_conv_w": nrm(ks[19], (L, CONV_W, D_FF), CONV_W),
        "ffn_conv_b": 0.02 * jax.random.normal(ks[20], (L, D_FF), f32),
        "ffn_w_down": nrm(ks[21], (L, D_FF, D_MODEL), D_FF),
    }


def _fwd_reference(x, mem, norm_mix_g, norm_mem_g, w_in, hgrn_lb_logits, hgrn_norm_g, fox_f_bias,
              fox_q_norm_g, fox_k_norm_g, mem_kv_w, mem_q_norm_g, mem_k_norm_g,
              w_br_hgrn, w_br_fox, w_br_mem, w_out, norm_ffn_g, ffn_w_up, ffn_conv_w,
              ffn_conv_b, ffn_w_down):
    B, T, _ = x.shape
    lower_bounds = jnp.cumsum(jax.nn.softmax(hgrn_lb_logits.astype(jnp.float32), axis=0), axis=0)
    pts = _split_points()
    for l in range(DEPTH):
        h = _rmsnorm(x, norm_mix_g[l])
        z = h @ w_in[l].astype(jnp.float32)
        (hq, hf, hi, hg, fq, fk, fv, ff, mq, gate_logits) = jnp.split(z, pts, axis=-1)
        y_a = _hgrn2_mixer(hq, hf, hi, hg, lower_bounds[l], hgrn_norm_g[l])
        y_b = _fox_mixer(fq, fk, fv, ff, fox_f_bias[l], fox_q_norm_g[l], fox_k_norm_g[l])
        mem_kv = _rmsnorm(mem, norm_mem_g[l]) @ mem_kv_w[l].astype(jnp.float32)
        y_c = _memory_mixer(mq, mem_kv, mem_q_norm_g[l], mem_k_norm_g[l])
        gates = jax.nn.sigmoid(gate_logits).reshape(B, T, N_BRANCH, D_MODEL)
        merged = (gates[:, :, 0] * (y_a @ w_br_hgrn[l].astype(jnp.float32))
                  + gates[:, :, 1] * (y_b @ w_br_fox[l].astype(jnp.float32))
                  + gates[:, :, 2] * (y_c @ w_br_mem[l].astype(jnp.float32)))
        x = x + (merged @ w_out[l].astype(jnp.float32)).astype(x.dtype)
        h2 = _rmsnorm(x, norm_ffn_g[l])
        x = x + _conv_glu_ffn(h2, ffn_w_up[l], ffn_conv_w[l], ffn_conv_b[l], ffn_w_down[l]).astype(x.dtype)
    return x


import jax as _jax
import jax.numpy as _jnp

TWIN_FORMAT = 'train_step'
FWD_PARAMS = ['x', 'mem', 'norm_mix_g', 'norm_mem_g', 'w_in', 'hgrn_lb_logits', 'hgrn_norm_g', 'fox_f_bias', 'fox_q_norm_g', 'fox_k_norm_g', 'mem_kv_w', 'mem_q_norm_g', 'mem_k_norm_g', 'w_br_hgrn', 'w_br_fox', 'w_br_mem', 'w_out', 'norm_ffn_g', 'ffn_w_up', 'ffn_conv_w', 'ffn_conv_b', 'ffn_w_down']
TWIN_WEIGHTS = ['norm_mix_g', 'norm_mem_g', 'w_in', 'hgrn_lb_logits', 'hgrn_norm_g', 'fox_f_bias', 'fox_q_norm_g', 'fox_k_norm_g', 'mem_kv_w', 'mem_q_norm_g', 'mem_k_norm_g', 'w_br_hgrn', 'w_br_fox', 'w_br_mem', 'w_out', 'norm_ffn_g', 'ffn_w_up', 'ffn_conv_w', 'ffn_conv_b', 'ffn_w_down']
TWIN_DIFF_INPUT = 'x'
TWIN_INPUTS = ['x', 'mem', 'norm_mix_g', 'norm_mem_g', 'w_in', 'hgrn_lb_logits', 'hgrn_norm_g', 'fox_f_bias', 'fox_q_norm_g', 'fox_k_norm_g', 'mem_kv_w', 'mem_q_norm_g', 'mem_k_norm_g', 'w_br_hgrn', 'w_br_fox', 'w_br_mem', 'w_out', 'norm_ffn_g', 'ffn_w_up', 'ffn_conv_w', 'ffn_conv_b', 'ffn_w_down', 'loss_target', 'm_norm_mix_g', 'm_norm_mem_g', 'm_w_in', 'm_hgrn_lb_logits', 'm_hgrn_norm_g', 'm_fox_f_bias', 'm_fox_q_norm_g', 'm_fox_k_norm_g', 'm_mem_kv_w', 'm_mem_q_norm_g', 'm_mem_k_norm_g', 'm_w_br_hgrn', 'm_w_br_fox', 'm_w_br_mem', 'm_w_out', 'm_norm_ffn_g', 'm_ffn_w_up', 'm_ffn_conv_w', 'm_ffn_conv_b', 'm_ffn_w_down', 'v_norm_mix_g', 'v_norm_mem_g', 'v_w_in', 'v_hgrn_lb_logits', 'v_hgrn_norm_g', 'v_fox_f_bias', 'v_fox_q_norm_g', 'v_fox_k_norm_g', 'v_mem_kv_w', 'v_mem_q_norm_g', 'v_mem_k_norm_g', 'v_w_br_hgrn', 'v_w_br_fox', 'v_w_br_mem', 'v_w_out', 'v_norm_ffn_g', 'v_ffn_w_up', 'v_ffn_conv_w', 'v_ffn_conv_b', 'v_ffn_w_down']
TWIN_OUTPUTS = ['loss', 'grad_x', 'grad_norm_mix_g', 'grad_norm_mem_g', 'grad_w_in', 'grad_hgrn_lb_logits', 'grad_hgrn_norm_g', 'grad_fox_f_bias', 'grad_fox_q_norm_g', 'grad_fox_k_norm_g', 'grad_mem_kv_w', 'grad_mem_q_norm_g', 'grad_mem_k_norm_g', 'grad_w_br_hgrn', 'grad_w_br_fox', 'grad_w_br_mem', 'grad_w_out', 'grad_norm_ffn_g', 'grad_ffn_w_up', 'grad_ffn_conv_w', 'grad_ffn_conv_b', 'grad_ffn_w_down', 'delta_norm_mix_g', 'delta_norm_mem_g', 'delta_w_in', 'delta_hgrn_lb_logits', 'delta_hgrn_norm_g', 'delta_fox_f_bias', 'delta_fox_q_norm_g', 'delta_fox_k_norm_g', 'delta_mem_kv_w', 'delta_mem_q_norm_g', 'delta_mem_k_norm_g', 'delta_w_br_hgrn', 'delta_w_br_fox', 'delta_w_br_mem', 'delta_w_out', 'delta_norm_ffn_g', 'delta_ffn_w_up', 'delta_ffn_conv_w', 'delta_ffn_conv_b', 'delta_ffn_w_down', 'new_m_norm_mix_g', 'new_m_norm_mem_g', 'new_m_w_in', 'new_m_hgrn_lb_logits', 'new_m_hgrn_norm_g', 'new_m_fox_f_bias', 'new_m_fox_q_norm_g', 'new_m_fox_k_norm_g', 'new_m_mem_kv_w', 'new_m_mem_q_norm_g', 'new_m_mem_k_norm_g', 'new_m_w_br_hgrn', 'new_m_w_br_fox', 'new_m_w_br_mem', 'new_m_w_out', 'new_m_norm_ffn_g', 'new_m_ffn_w_up', 'new_m_ffn_conv_w', 'new_m_ffn_conv_b', 'new_m_ffn_w_down', 'new_v_norm_mix_g', 'new_v_norm_mem_g', 'new_v_w_in', 'new_v_hgrn_lb_logits', 'new_v_hgrn_norm_g', 'new_v_fox_f_bias', 'new_v_fox_q_norm_g', 'new_v_fox_k_norm_g', 'new_v_mem_kv_w', 'new_v_mem_q_norm_g', 'new_v_mem_k_norm_g', 'new_v_w_br_hgrn', 'new_v_w_br_fox', 'new_v_w_br_mem', 'new_v_w_out', 'new_v_norm_ffn_g', 'new_v_ffn_w_up', 'new_v_ffn_conv_w', 'new_v_ffn_conv_b', 'new_v_ffn_w_down']
TWIN_LEAF_KINDS = {'loss': 'loss', 'grad_x': 'grad_x', 'grad_norm_mix_g': 'grad_w', 'grad_norm_mem_g': 'grad_w', 'grad_w_in': 'grad_w', 'grad_hgrn_lb_logits': 'grad_w', 'grad_hgrn_norm_g': 'grad_w', 'grad_fox_f_bias': 'grad_w', 'grad_fox_q_norm_g': 'grad_w', 'grad_fox_k_norm_g': 'grad_w', 'grad_mem_kv_w': 'grad_w', 'grad_mem_q_norm_g': 'grad_w', 'grad_mem_k_norm_g': 'grad_w', 'grad_w_br_hgrn': 'grad_w', 'grad_w_br_fox': 'grad_w', 'grad_w_br_mem': 'grad_w', 'grad_w_out': 'grad_w', 'grad_norm_ffn_g': 'grad_w', 'grad_ffn_w_up': 'grad_w', 'grad_ffn_conv_w': 'grad_w', 'grad_ffn_conv_b': 'grad_w', 'grad_ffn_w_down': 'grad_w', 'delta_norm_mix_g': 'delta_w', 'delta_norm_mem_g': 'delta_w', 'delta_w_in': 'delta_w', 'delta_hgrn_lb_logits': 'delta_w', 'delta_hgrn_norm_g': 'delta_w', 'delta_fox_f_bias': 'delta_w', 'delta_fox_q_norm_g': 'delta_w', 'delta_fox_k_norm_g': 'delta_w', 'delta_mem_kv_w': 'delta_w', 'delta_mem_q_norm_g': 'delta_w', 'delta_mem_k_norm_g': 'delta_w', 'delta_w_br_hgrn': 'delta_w', 'delta_w_br_fox': 'delta_w', 'delta_w_br_mem': 'delta_w', 'delta_w_out': 'delta_w', 'delta_norm_ffn_g': 'delta_w', 'delta_ffn_w_up': 'delta_w', 'delta_ffn_conv_w': 'delta_w', 'delta_ffn_conv_b': 'delta_w', 'delta_ffn_w_down': 'delta_w', 'new_m_norm_mix_g': 'new_m', 'new_m_norm_mem_g': 'new_m', 'new_m_w_in': 'new_m', 'new_m_hgrn_lb_logits': 'new_m', 'new_m_hgrn_norm_g': 'new_m', 'new_m_fox_f_bias': 'new_m', 'new_m_fox_q_norm_g': 'new_m', 'new_m_fox_k_norm_g': 'new_m', 'new_m_mem_kv_w': 'new_m', 'new_m_mem_q_norm_g': 'new_m', 'new_m_mem_k_norm_g': 'new_m', 'new_m_w_br_hgrn': 'new_m', 'new_m_w_br_fox': 'new_m', 'new_m_w_br_mem': 'new_m', 'new_m_w_out': 'new_m', 'new_m_norm_ffn_g': 'new_m', 'new_m_ffn_w_up': 'new_m', 'new_m_ffn_conv_w': 'new_m', 'new_m_ffn_conv_b': 'new_m', 'new_m_ffn_w_down': 'new_m', 'new_v_norm_mix_g': 'new_v', 'new_v_norm_mem_g': 'new_v', 'new_v_w_in': 'new_v', 'new_v_hgrn_lb_logits': 'new_v', 'new_v_hgrn_norm_g': 'new_v', 'new_v_fox_f_bias': 'new_v', 'new_v_fox_q_norm_g': 'new_v', 'new_v_fox_k_norm_g': 'new_v', 'new_v_mem_kv_w': 'new_v', 'new_v_mem_q_norm_g': 'new_v', 'new_v_mem_k_norm_g': 'new_v', 'new_v_w_br_hgrn': 'new_v', 'new_v_w_br_fox': 'new_v', 'new_v_w_br_mem': 'new_v', 'new_v_w_out': 'new_v', 'new_v_norm_ffn_g': 'new_v', 'new_v_ffn_w_up': 'new_v', 'new_v_ffn_conv_w': 'new_v', 'new_v_ffn_conv_b': 'new_v', 'new_v_ffn_w_down': 'new_v'}


def _forward(args):
    return _fwd_reference(*[args[k] for k in FWD_PARAMS])


def _output_shape():
    out = _jax.eval_shape(lambda: _forward(_fwd_setup_inputs(0)))
    return out.shape, out.dtype

N_MICROBATCH = 1
ADAM_LR = 0.001
ADAM_B1 = 0.9
ADAM_B2 = 0.999
ADAM_EPS = 1e-08
ADAM_WD = 0.01
ADAM_STEP = 10
PER_EXAMPLE_BATCH_AXIS = {'x': 0, 'mem': 0, 'loss_target': 0}
SHARED_INPUTS = []
_WEIGHT_DTYPES = {'norm_mix_g': _jnp.float32, 'norm_mem_g': _jnp.float32, 'w_in': _jnp.float32, 'hgrn_lb_logits': _jnp.float32, 'hgrn_norm_g': _jnp.float32, 'fox_f_bias': _jnp.float32, 'fox_q_norm_g': _jnp.float32, 'fox_k_norm_g': _jnp.float32, 'mem_kv_w': _jnp.float32, 'mem_q_norm_g': _jnp.float32, 'mem_k_norm_g': _jnp.float32, 'w_br_hgrn': _jnp.float32, 'w_br_fox': _jnp.float32, 'w_br_mem': _jnp.float32, 'w_out': _jnp.float32, 'norm_ffn_g': _jnp.float32, 'ffn_w_up': _jnp.float32, 'ffn_conv_w': _jnp.float32, 'ffn_conv_b': _jnp.float32, 'ffn_w_down': _jnp.float32}
MOMENT_SCALE = {'norm_mix_g': 1.490215e+01, 'norm_mem_g': 2.184070e-01, 'w_in': 2.102831e-01, 'hgrn_lb_logits': 3.359355e-02, 'hgrn_norm_g': 5.203239e+01, 'fox_f_bias': 1.716631e+02, 'fox_q_norm_g': 1.791737e+01, 'fox_k_norm_g': 1.796412e+01, 'mem_kv_w': 1.650886e-01, 'mem_q_norm_g': 1.236319e+00, 'mem_k_norm_g': 1.230699e+00, 'w_br_hgrn': 2.886982e-01, 'w_br_fox': 3.108331e-01, 'w_br_mem': 1.833595e-01, 'w_out': 4.528855e-01, 'norm_ffn_g': 5.776803e+01, 'ffn_w_up': 3.368803e-01, 'ffn_conv_w': 6.367743e+00, 'ffn_conv_b': 8.306776e+00, 'ffn_w_down': 5.671510e-01}


def _to_microbatches(a, axis):
    t = _jnp.moveaxis(a, axis, 0)
    t = t.reshape((N_MICROBATCH, t.shape[0] // N_MICROBATCH) + t.shape[1:])
    return _jnp.moveaxis(t, 1, axis + 1)


def setup_inputs(seed: int = 0) -> dict:
    inp = _fwd_setup_inputs(seed)
    key = _jax.random.fold_in(_jax.random.key(seed), 7919)
    shape, _ = _output_shape()
    out = dict(inp)
    out["loss_target"] = _jax.random.normal(_jax.random.fold_in(key, 0), shape, _jnp.float32)
    for i, name in enumerate(TWIN_WEIGHTS):
        w = inp[name].astype(_jnp.float32)
        if MOMENT_SCALE is None:
            s = _jnp.sqrt(_jnp.mean(_jnp.square(w)) + 1e-30)
        else:
            s = MOMENT_SCALE[name]
        km, kv = _jax.random.split(_jax.random.fold_in(key, i + 1))
        out[name] = w
        out["m_" + name] = s * _jax.random.normal(km, w.shape, _jnp.float32)
        out["v_" + name] = (s * s) * _jax.random.uniform(kv, w.shape, _jnp.float32, 0.5, 1.5)
    if N_MICROBATCH > 1:
        for name, axis in PER_EXAMPLE_BATCH_AXIS.items():
            out[name] = _to_microbatches(out[name], axis)
    return {'x': out['x'], 'mem': out['mem'], 'norm_mix_g': out['norm_mix_g'], 'norm_mem_g': out['norm_mem_g'], 'w_in': out['w_in'], 'hgrn_lb_logits': out['hgrn_lb_logits'], 'hgrn_norm_g': out['hgrn_norm_g'], 'fox_f_bias': out['fox_f_bias'], 'fox_q_norm_g': out['fox_q_norm_g'], 'fox_k_norm_g': out['fox_k_norm_g'], 'mem_kv_w': out['mem_kv_w'], 'mem_q_norm_g': out['mem_q_norm_g'], 'mem_k_norm_g': out['mem_k_norm_g'], 'w_br_hgrn': out['w_br_hgrn'], 'w_br_fox': out['w_br_fox'], 'w_br_mem': out['w_br_mem'], 'w_out': out['w_out'], 'norm_ffn_g': out['norm_ffn_g'], 'ffn_w_up': out['ffn_w_up'], 'ffn_conv_w': out['ffn_conv_w'], 'ffn_conv_b': out['ffn_conv_b'], 'ffn_w_down': out['ffn_w_down'], 'loss_target': out['loss_target'], 'm_norm_mix_g': out['m_norm_mix_g'], 'm_norm_mem_g': out['m_norm_mem_g'], 'm_w_in': out['m_w_in'], 'm_hgrn_lb_logits': out['m_hgrn_lb_logits'], 'm_hgrn_norm_g': out['m_hgrn_norm_g'], 'm_fox_f_bias': out['m_fox_f_bias'], 'm_fox_q_norm_g': out['m_fox_q_norm_g'], 'm_fox_k_norm_g': out['m_fox_k_norm_g'], 'm_mem_kv_w': out['m_mem_kv_w'], 'm_mem_q_norm_g': out['m_mem_q_norm_g'], 'm_mem_k_norm_g': out['m_mem_k_norm_g'], 'm_w_br_hgrn': out['m_w_br_hgrn'], 'm_w_br_fox': out['m_w_br_fox'], 'm_w_br_mem': out['m_w_br_mem'], 'm_w_out': out['m_w_out'], 'm_norm_ffn_g': out['m_norm_ffn_g'], 'm_ffn_w_up': out['m_ffn_w_up'], 'm_ffn_conv_w': out['m_ffn_conv_w'], 'm_ffn_conv_b': out['m_ffn_conv_b'], 'm_ffn_w_down': out['m_ffn_w_down'], 'v_norm_mix_g': out['v_norm_mix_g'], 'v_norm_mem_g': out['v_norm_mem_g'], 'v_w_in': out['v_w_in'], 'v_hgrn_lb_logits': out['v_hgrn_lb_logits'], 'v_hgrn_norm_g': out['v_hgrn_norm_g'], 'v_fox_f_bias': out['v_fox_f_bias'], 'v_fox_q_norm_g': out['v_fox_q_norm_g'], 'v_fox_k_norm_g': out['v_fox_k_norm_g'], 'v_mem_kv_w': out['v_mem_kv_w'], 'v_mem_q_norm_g': out['v_mem_q_norm_g'], 'v_mem_k_norm_g': out['v_mem_k_norm_g'], 'v_w_br_hgrn': out['v_w_br_hgrn'], 'v_w_br_fox': out['v_w_br_fox'], 'v_w_br_mem': out['v_w_br_mem'], 'v_w_out': out['v_w_out'], 'v_norm_ffn_g': out['v_norm_ffn_g'], 'v_ffn_w_up': out['v_ffn_w_up'], 'v_ffn_conv_w': out['v_ffn_conv_w'], 'v_ffn_conv_b': out['v_ffn_conv_b'], 'v_ffn_w_down': out['v_ffn_w_down']}


def _loss(weights, diff, rest, loss_target):
    with _jax.named_scope("forward"):
        args = {**rest, TWIN_DIFF_INPUT: diff, **{k: w.astype(_WEIGHT_DTYPES[k]) for k, w in weights.items()}}
        y = _forward(args)
    with _jax.named_scope("loss_head"):
        err = _jnp.square(y.astype(_jnp.float32) - loss_target)
        return 0.5 * _jnp.sum(_jnp.mean(err, axis=-1)) if err.ndim else 0.5 * err


def _adamw(w, g, m, v):
    m = ADAM_B1 * m + (1.0 - ADAM_B1) * g
    v = ADAM_B2 * v + (1.0 - ADAM_B2) * _jnp.square(g)
    m_hat = m / (1.0 - ADAM_B1 ** ADAM_STEP)
    v_hat = v / (1.0 - ADAM_B2 ** ADAM_STEP)
    delta = -ADAM_LR * (m_hat / (_jnp.sqrt(v_hat) + ADAM_EPS) + ADAM_WD * w)
    return delta, m, v


def reference(x, mem, norm_mix_g, norm_mem_g, w_in, hgrn_lb_logits, hgrn_norm_g, fox_f_bias, fox_q_norm_g, fox_k_norm_g, mem_kv_w, mem_q_norm_g, mem_k_norm_g, w_br_hgrn, w_br_fox, w_br_mem, w_out, norm_ffn_g, ffn_w_up, ffn_conv_w, ffn_conv_b, ffn_w_down, loss_target, m_norm_mix_g, m_norm_mem_g, m_w_in, m_hgrn_lb_logits, m_hgrn_norm_g, m_fox_f_bias, m_fox_q_norm_g, m_fox_k_norm_g, m_mem_kv_w, m_mem_q_norm_g, m_mem_k_norm_g, m_w_br_hgrn, m_w_br_fox, m_w_br_mem, m_w_out, m_norm_ffn_g, m_ffn_w_up, m_ffn_conv_w, m_ffn_conv_b, m_ffn_w_down, v_norm_mix_g, v_norm_mem_g, v_w_in, v_hgrn_lb_logits, v_hgrn_norm_g, v_fox_f_bias, v_fox_q_norm_g, v_fox_k_norm_g, v_mem_kv_w, v_mem_q_norm_g, v_mem_k_norm_g, v_w_br_hgrn, v_w_br_fox, v_w_br_mem, v_w_out, v_norm_ffn_g, v_ffn_w_up, v_ffn_conv_w, v_ffn_conv_b, v_ffn_w_down):
    given = dict(x=x, mem=mem, norm_mix_g=norm_mix_g, norm_mem_g=norm_mem_g, w_in=w_in, hgrn_lb_logits=hgrn_lb_logits, hgrn_norm_g=hgrn_norm_g, fox_f_bias=fox_f_bias, fox_q_norm_g=fox_q_norm_g, fox_k_norm_g=fox_k_norm_g, mem_kv_w=mem_kv_w, mem_q_norm_g=mem_q_norm_g, mem_k_norm_g=mem_k_norm_g, w_br_hgrn=w_br_hgrn, w_br_fox=w_br_fox, w_br_mem=w_br_mem, w_out=w_out, norm_ffn_g=norm_ffn_g, ffn_w_up=ffn_w_up, ffn_conv_w=ffn_conv_w, ffn_conv_b=ffn_conv_b, ffn_w_down=ffn_w_down, loss_target=loss_target, m_norm_mix_g=m_norm_mix_g, m_norm_mem_g=m_norm_mem_g, m_w_in=m_w_in, m_hgrn_lb_logits=m_hgrn_lb_logits, m_hgrn_norm_g=m_hgrn_norm_g, m_fox_f_bias=m_fox_f_bias, m_fox_q_norm_g=m_fox_q_norm_g, m_fox_k_norm_g=m_fox_k_norm_g, m_mem_kv_w=m_mem_kv_w, m_mem_q_norm_g=m_mem_q_norm_g, m_mem_k_norm_g=m_mem_k_norm_g, m_w_br_hgrn=m_w_br_hgrn, m_w_br_fox=m_w_br_fox, m_w_br_mem=m_w_br_mem, m_w_out=m_w_out, m_norm_ffn_g=m_norm_ffn_g, m_ffn_w_up=m_ffn_w_up, m_ffn_conv_w=m_ffn_conv_w, m_ffn_conv_b=m_ffn_conv_b, m_ffn_w_down=m_ffn_w_down, v_norm_mix_g=v_norm_mix_g, v_norm_mem_g=v_norm_mem_g, v_w_in=v_w_in, v_hgrn_lb_logits=v_hgrn_lb_logits, v_hgrn_norm_g=v_hgrn_norm_g, v_fox_f_bias=v_fox_f_bias, v_fox_q_norm_g=v_fox_q_norm_g, v_fox_k_norm_g=v_fox_k_norm_g, v_mem_kv_w=v_mem_kv_w, v_mem_q_norm_g=v_mem_q_norm_g, v_mem_k_norm_g=v_mem_k_norm_g, v_w_br_hgrn=v_w_br_hgrn, v_w_br_fox=v_w_br_fox, v_w_br_mem=v_w_br_mem, v_w_out=v_w_out, v_norm_ffn_g=v_norm_ffn_g, v_ffn_w_up=v_ffn_w_up, v_ffn_conv_w=v_ffn_conv_w, v_ffn_conv_b=v_ffn_conv_b, v_ffn_w_down=v_ffn_w_down)
    weights = {n: given[n] for n in TWIN_WEIGHTS}
    shared = {n: given[n] for n in SHARED_INPUTS}
    per_example = {n: given[n] for n in ['x', 'mem']}
    grad_fn = _jax.value_and_grad(_loss, argnums=(0, 1))

    def one_microbatch(ex, loss_target):
        ex = dict(ex)
        diff = ex.pop(TWIN_DIFF_INPUT)
        return grad_fn(weights, diff, {**shared, **ex}, loss_target)

    if N_MICROBATCH == 1:
        loss, (grad_w, grad_x) = one_microbatch(per_example, given["loss_target"])
    else:
        def body(carry, xs):
            loss_sum, grad_sum = carry
            l_k, (gw_k, gx_k) = one_microbatch(xs[0], xs[1])
            with _jax.named_scope("update"):
                return (loss_sum + l_k, _jax.tree.map(_jnp.add, grad_sum, gw_k)), gx_k

        init = (_jnp.zeros((), _jnp.float32), _jax.tree.map(_jnp.zeros_like, weights))
        (loss, grad_w), grad_x = _jax.lax.scan(body, init, (per_example, given["loss_target"]))
    with _jax.named_scope("update"):
        delta_w, new_m, new_v = {}, {}, {}
        for n in TWIN_WEIGHTS:
            delta_w[n], new_m[n], new_v[n] = _adamw(weights[n], grad_w[n], given["m_" + n], given["v_" + n])
    return (loss, grad_x, *[grad_w[n] for n in TWIN_WEIGHTS], *[delta_w[n] for n in TWIN_WEIGHTS],
            *[new_m[n] for n in TWIN_WEIGHTS], *[new_v[n] for n in TWIN_WEIGHTS])
```

```python
import functools
import math

import jax
import jax.numpy as jnp
from jax import lax
from jax.experimental import pallas as pl
from jax.experimental.pallas import tpu as pltpu

F32, BF16 = jnp.float32, jnp.bfloat16
S = jax.ShapeDtypeStruct
HI = lax.Precision.HIGHEST
MESH = pl.DeviceIdType.MESH

N_DEV = 8
EPS = 1e-6
LANE = 128
CHUNK = 64
SUB = 16
HG_H, HG_D = 4, 128
FOX_H, FOX_D = 8, 64
FOX_P = FOX_H // 2
MEM_H, MEM_D = 4, 128
NEG = -1e30
VMEM_LIMIT = 56 * 2**20

ADAM_LR, ADAM_B1, ADAM_B2, ADAM_EPS, ADAM_WD, ADAM_STEP = 0.001, 0.9, 0.999, 1e-08, 0.01, 10

C_FOX, C_MQ, C_HG, C_GATE, C_FF, C_END = 0, 1536, 2048, 4096, 7168, 7296


def _cp(sem=None):
    return pltpu.CompilerParams(dimension_semantics=sem, vmem_limit_bytes=VMEM_LIMIT)


def _dot(a, b, dims, prec=None):
    return lax.dot_general(a, b, (dims, ((), ())), preferred_element_type=F32, precision=prec)


def _nn(a, b, prec=None):
    return _dot(a, b, ((1,), (0,)), prec)


def _nt(a, b, prec=None):
    return _dot(a, b, ((1,), (1,)), prec)


def _tn(a, b, prec=None):
    return _dot(a, b, ((0,), (0,)), prec)


def _iota(shape, dim):
    return lax.broadcasted_iota(jnp.int32, shape, dim)


def _rowsum8(x):
    r, d = x.shape
    return jnp.sum(x.reshape(r // 8, 8, d), axis=0)


def _rmsnorm_cast(x, g, name, tm=512):
    n, d = x.shape

    def body(x_ref, g_ref, o_ref):
        v = x_ref[...]
        r = lax.rsqrt(jnp.mean(v * v, axis=-1, keepdims=True) + EPS)
        o_ref[...] = (v * r * g_ref[...]).astype(BF16)

    return pl.pallas_call(
        body, name=name, grid=(n // tm,),
        in_specs=[pl.BlockSpec((tm, d), lambda i: (i, 0)), pl.BlockSpec((1, d), lambda i: (0, 0))],
        out_specs=pl.BlockSpec((tm, d), lambda i: (i, 0)), out_shape=S((n, d), BF16), compiler_params=_cp(("parallel",)),
    )(x, g)


def _rmsnorm_bwd(dh, x, g, resid, name, tm=512):
    n, d = x.shape
    has_res = resid is not None

    def body(*refs):
        if has_res:
            dh_ref, x_ref, g_ref, r_ref, dx_ref, dg_ref = refs
        else:
            dh_ref, x_ref, g_ref, dx_ref, dg_ref = refs
        v = x_ref[...]
        dhv = dh_ref[...].astype(F32)
        r = lax.rsqrt(jnp.mean(v * v, axis=-1, keepdims=True) + EPS)
        xh = v * r
        u = dhv * g_ref[...]
        dx = r * (u - xh * jnp.mean(u * xh, axis=-1, keepdims=True))
        if has_res:
            dx = dx + r_ref[...]
        dx_ref[...] = dx

        @pl.when(pl.program_id(0) == 0)
        def _():
            dg_ref[...] = jnp.zeros_like(dg_ref)

        dg_ref[...] += _rowsum8(dhv * xh)

    tile = pl.BlockSpec((tm, d), lambda i: (i, 0))
    ins = [tile, tile, pl.BlockSpec((1, d), lambda i: (0, 0))] + ([tile] if has_res else [])
    args = (dh, x, g) + ((resid,) if has_res else ())
    return pl.pallas_call(
        body, name=name, grid=(n // tm,), in_specs=ins,
        out_specs=[tile, pl.BlockSpec((8, d), lambda i: (0, 0))],
        out_shape=[S((n, d), F32), S((8, d), F32)], compiler_params=_cp(("arbitrary",)),
    )(*args)


def _mm_nn(a, b, out_dtype, name, tm, tn, b_col0=0, n_out=None):
    m, k = a.shape
    n_out = b.shape[1] if n_out is None else n_out
    jb = b_col0 // tn
    assert b_col0 % tn == 0 and n_out % tn == 0 and m % tm == 0

    def body(a_ref, b_ref, o_ref):
        o_ref[...] = _nn(a_ref[...].astype(BF16), b_ref[...].astype(BF16)).astype(out_dtype)

    return pl.pallas_call(
        body, name=name, grid=(m // tm, n_out // tn),
        in_specs=[pl.BlockSpec((tm, k), lambda i, j: (i, 0)), pl.BlockSpec((k, tn), lambda i, j: (0, j + jb))],
        out_specs=pl.BlockSpec((tm, tn), lambda i, j: (i, j)), out_shape=S((m, n_out), out_dtype),
        compiler_params=_cp(("parallel", "parallel")),
    )(a, b)


def _mm_nt(dy, w, name, tm, tr, w_col0=0, acc=None):
    m, r = dy.shape
    k = w.shape[0]
    jb = w_col0 // tr
    nr = r // tr
    assert w_col0 % tr == 0 and r % tr == 0 and m % tm == 0
    has_acc = acc is not None

    def body(*refs):
        if has_acc:
            dy_ref, w_ref, acc_ref, o_ref = refs
        else:
            dy_ref, w_ref, o_ref = refs
        part = _nt(dy_ref[...].astype(BF16), w_ref[...].astype(BF16))

        @pl.when(pl.program_id(1) == 0)
        def _():
            o_ref[...] = part + acc_ref[...] if has_acc else part

        @pl.when(pl.program_id(1) > 0)
        def _():
            o_ref[...] += part

    out_tile = pl.BlockSpec((tm, k), lambda i, j: (i, 0))
    ins = [pl.BlockSpec((tm, tr), lambda i, j: (i, j)), pl.BlockSpec((k, tr), lambda i, j: (0, j + jb))]
    args = (dy, w)
    if has_acc:
        ins.append(out_tile)
        args = args + (acc,)
    return pl.pallas_call(
        body, name=name, grid=(m // tm, nr), in_specs=ins, out_specs=out_tile, out_shape=S((m, k), F32),
        input_output_aliases=({2: 0} if has_acc else {}), compiler_params=_cp(("parallel", "arbitrary")),
    )(*args)


def _mm_tn(x, dy, name, tm, tn):
    m, k = x.shape
    n = dy.shape[1]
    assert m % tm == 0 and n % tn == 0

    def body(x_ref, dy_ref, o_ref):
        part = _tn(x_ref[...].astype(BF16), dy_ref[...].astype(BF16))

        @pl.when(pl.program_id(1) == 0)
        def _():
            o_ref[...] = part

        @pl.when(pl.program_id(1) > 0)
        def _():
            o_ref[...] += part

    return pl.pallas_call(
        body, name=name, grid=(n // tn, m // tm),
        in_specs=[pl.BlockSpec((tm, k), lambda j, i: (i, 0)), pl.BlockSpec((tm, tn), lambda j, i: (i, j))],
        out_specs=pl.BlockSpec((k, tn), lambda j, i: (0, j)), out_shape=S((k, n), F32),
        compiler_params=_cp(("parallel", "arbitrary")),
    )(x, dy)


def _lower_bound(logits):
    e = jnp.exp(logits - jnp.max(logits, axis=0, keepdims=True))
    return e[0:1, :] / jnp.sum(e, axis=0, keepdims=True)


def _hg_gates(fl, lb):
    sig = jax.nn.sigmoid(fl)
    f = lb + (1.0 - lb) * sig
    k = (1.0 - lb) * (1.0 - sig)
    return sig, f, k, jnp.log(f)


def _silu_and_grad(x):
    s = jax.nn.sigmoid(x)
    return x * s, s * (1.0 + x * (1.0 - s))


def _hg_rowblocks(G):
    return [None] + [G[SUB * i - 1:SUB * i, :] for i in range(1, CHUNK // SUB)]


def _hg_intra_A(qs, k, G):
    refs = _hg_rowblocks(G)
    cols = _iota((SUB, CHUNK), 1)
    rows = _iota((SUB, CHUNK), 0)
    blocks = []
    for i in range(CHUNK // SUB):
        lo = SUB * i
        qb, Gb = qs[lo:lo + SUB, :], G[lo:lo + SUB, :]
        diag = jnp.zeros((SUB, CHUNK), F32)
        for s in range(SUB):
            e = jnp.exp(jnp.minimum(Gb - G[lo + s:lo + s + 1, :], 0.0))
            col = jnp.sum(qb * k[lo + s:lo + s + 1, :] * e, axis=-1, keepdims=True)
            diag = jnp.where(cols == lo + s, col, diag)
        a = jnp.where((cols >= lo) & (cols <= rows + lo), diag, 0.0)
        if i > 0:
            qr = qb * jnp.exp(Gb - refs[i])
            kr = k * jnp.exp(jnp.minimum(refs[i] - G, 0.0))
            a = jnp.where(cols < lo, _nt(qr, kr, HI), a)
        blocks.append(a)
    return jnp.concatenate(blocks, axis=0)


def _hg_intra_bwd(dA, qs, k, G):
    refs = _hg_rowblocks(G)
    cols = _iota((SUB, CHUNK), 1)
    rows16 = _iota((SUB, HG_D), 0)
    dk = jnp.zeros((CHUNK, HG_D), F32)
    dq_blocks, dk_diag_blocks = [], []
    for i in range(CHUNK // SUB):
        lo = SUB * i
        qb, Gb = qs[lo:lo + SUB, :], G[lo:lo + SUB, :]
        dAb = dA[lo:lo + SUB, :]
        dq = jnp.zeros((SUB, HG_D), F32)
        dkb = jnp.zeros((SUB, HG_D), F32)
        for s in range(SUB):
            e = jnp.exp(jnp.minimum(Gb - G[lo + s:lo + s + 1, :], 0.0))
            e = jnp.where(rows16 >= s, e, 0.0)
            dcol = jnp.sum(jnp.where(cols == lo + s, dAb, 0.0), axis=-1, keepdims=True)
            w = dcol * e
            dq = dq + w * k[lo + s:lo + s + 1, :]
            dkb = jnp.where(rows16 == s, jnp.sum(w * qb, axis=0, keepdims=True), dkb)
        if i > 0:
            e1 = jnp.exp(Gb - refs[i])
            e2 = jnp.exp(jnp.minimum(refs[i] - G, 0.0))
            dA_off = jnp.where(cols < lo, dAb, 0.0)
            dq = dq + _nn(dA_off, k * e2, HI) * e1
            dk = dk + _tn(dA_off, qb * e1, HI) * e2
        dq_blocks.append(dq)
        dk_diag_blocks.append(dkb)
    return jnp.concatenate(dq_blocks, axis=0), dk + jnp.concatenate(dk_diag_blocks, axis=0)


def _tri(n, upper=False):
    r, c = _iota((n, n), 0), _iota((n, n), 1)
    return jnp.where((c >= r) if upper else (r >= c), 1.0, 0.0).astype(F32)


def _hgrn_fwd(z, lb, gn, B, T):
    N = B * T
    NC = T // CHUNK

    def body(z_ref, lb_ref, gn_ref, y_ref, o_ref, st_ref, s_scr):
        lb_v = _lower_bound(lb_ref[...])
        tri = _tri(CHUNK)
        s_scr[...] = jnp.zeros_like(s_scr)

        def chunk(c, carry):
            r = pl.ds(pl.multiple_of(c * CHUNK, CHUNK), CHUNK)
            ql, fl, il, gl = (z_ref[r, LANE * j:LANE * (j + 1)] for j in range(4))
            _, _, k, logf = _hg_gates(fl, lb_v)
            G = _nn(tri, logf, HI)
            qs = ql * jax.nn.sigmoid(ql)
            st = s_scr[...]
            st_ref[c] = st
            A = _hg_intra_A(qs, k, G)
            o = _nn(A, il, HI) + _nt(qs * jnp.exp(G), st, HI)
            g_last = G[CHUNK - 1:CHUNK, :]
            s_scr[...] = st * jnp.exp(g_last) + _tn(il, k * jnp.exp(g_last - G), HI)
            o_ref[r, :] = o
            rstd = lax.rsqrt(jnp.mean(o * o, axis=-1, keepdims=True) + EPS)
            y_ref[r, :] = (o * rstd * gn_ref[...] * (gl * jax.nn.sigmoid(gl))).astype(BF16)
            return carry

        lax.fori_loop(0, NC, chunk, 0)

    cb = C_HG // 512
    return pl.pallas_call(
        body, name="hgrn_fwd", grid=(B, HG_H),
        in_specs=[pl.BlockSpec((T, 512), lambda b, h: (b, cb + h)), pl.BlockSpec((lb.shape[0], LANE), lambda b, h: (0, h)),
                  pl.BlockSpec((1, LANE), lambda b, h: (0, 0))],
        out_specs=[pl.BlockSpec((T, LANE), lambda b, h: (b, h)), pl.BlockSpec((T, LANE), lambda b, h: (b, h)),
                   pl.BlockSpec((NC, HG_D, HG_D), lambda b, h: (b * HG_H + h, 0, 0))],
        out_shape=[S((N, 512), BF16), S((N, 512), F32), S((B * HG_H * NC, HG_D, HG_D), F32)],
        scratch_shapes=[pltpu.VMEM((HG_D, HG_D), F32)], compiler_params=_cp(("parallel", "parallel")),
    )(z, lb, gn)


def _hgrn_bwd(z, o_raw, states, dy, lb, gn, B, T):
    N = B * T
    NC = T // CHUNK

    def body(z_ref, o_ref, st_ref, dy_ref, lb_ref, gn_ref, dz_ref, dlb_ref, dgn_ref, ds_scr, racc, dlb_acc, dgn_acc):
        lb_v = _lower_bound(lb_ref[...])
        gn_v = gn_ref[...]
        tri, triu = _tri(CHUNK), _tri(CHUNK, upper=True)
        cmask = _iota((CHUNK, CHUNK), 0) >= _iota((CHUNK, CHUNK), 1)
        for ref in (ds_scr, racc, dlb_acc, dgn_acc):
            ref[...] = jnp.zeros_like(ref)

        def chunk(ci, carry):
            c = NC - 1 - ci
            r = pl.ds(pl.multiple_of(c * CHUNK, CHUNK), CHUNK)
            ql, fl, il, gl = (z_ref[r, LANE * j:LANE * (j + 1)] for j in range(4))
            sig, f, k, logf = _hg_gates(fl, lb_v)
            G = _nn(tri, logf, HI)
            qs, dsilu_q = _silu_and_grad(ql)
            gs, dsilu_g = _silu_and_grad(gl)
            o = o_ref[r, :]
            dyv = dy_ref[r, :]
            rstd = lax.rsqrt(jnp.mean(o * o, axis=-1, keepdims=True) + EPS)
            oh = o * rstd
            dgl = dyv * oh * gn_v * dsilu_g
            dn = dyv * gs
            dgn_acc[...] += _rowsum8(dn * oh)
            u = dn * gn_v
            do = rstd * (u - oh * jnp.mean(u * oh, axis=-1, keepdims=True))
            st = st_ref[c]
            dst = ds_scr[...]
            eG = jnp.exp(G)
            g_last = G[CHUNK - 1:CHUNK, :]
            eL = jnp.exp(g_last - G)
            A = _hg_intra_A(qs, k, G)
            dA = jnp.where(cmask, _nt(do, il, HI), 0.0)
            di = _tn(A, do, HI) + _nt(k * eL, dst, HI)
            dq_in, dk_in = _hg_intra_bwd(dA, qs, k, G)
            dq = dq_in + _nn(do, st, HI) * eG
            dk = dk_in + _nn(il, dst, HI) * eL
            ds_scr[...] = dst * jnp.exp(g_last) + _tn(do, qs * eG, HI)
            dd = qs * dq - k * dk
            dlogf = _nn(triu, dd, HI) + racc[...]
            racc[...] += jnp.sum(dd, axis=0, keepdims=True)
            df = dlogf / f - dk
            dlb_acc[...] += _rowsum8(df * (1.0 - sig))
            dz_ref[r, 0:LANE] = (dq * dsilu_q).astype(BF16)
            dz_ref[r, LANE:2 * LANE] = (df * (1.0 - lb_v) * sig * (1.0 - sig)).astype(BF16)
            dz_ref[r, 2 * LANE:3 * LANE] = di.astype(BF16)
            dz_ref[r, 3 * LANE:4 * LANE] = dgl.astype(BF16)
            return carry

        lax.fori_loop(0, NC, chunk, 0)
        dlb_ref[...] = dlb_acc[...]
        dgn_ref[...] = dgn_acc[...]

    cb = C_HG // 512
    part = pl.BlockSpec((8, LANE), lambda b, h: (b * HG_H + h, 0))
    return pl.pallas_call(
        body, name="hgrn_bwd", grid=(B, HG_H),
        in_specs=[pl.BlockSpec((T, 512), lambda b, h: (b, cb + h)), pl.BlockSpec((T, LANE), lambda b, h: (b, h)),
                  pl.BlockSpec((NC, HG_D, HG_D), lambda b, h: (b * HG_H + h, 0, 0)),
                  pl.BlockSpec((T, LANE), lambda b, h: (b, h)), pl.BlockSpec((lb.shape[0], LANE), lambda b, h: (0, h)),
                  pl.BlockSpec((1, LANE), lambda b, h: (0, 0))],
        out_specs=[pl.BlockSpec((T, 512), lambda b, h: (b, h)), part, part],
        out_shape=[S((N, 2048), BF16), S((B * HG_H * 8, LANE), F32), S((B * HG_H * 8, LANE), F32)],
        scratch_shapes=[pltpu.VMEM((HG_D, HG_D), F32), pltpu.VMEM((1, LANE), F32), pltpu.VMEM((8, LANE), F32),
                        pltpu.VMEM((8, LANE), F32)],
        compiler_params=_cp(("parallel", "parallel")),
    )(z, o_raw, states, dy, lb, gn)


def _pair_mean(x, lo_half):
    a = jnp.sum(jnp.where(lo_half, x, 0.0), axis=-1, keepdims=True)
    b = jnp.sum(jnp.where(lo_half, 0.0, x), axis=-1, keepdims=True)
    return jnp.where(lo_half, a, b) * (1.0 / FOX_D)


def _fox_gate_fwd(z, bias, B, T):
    N = B * T
    tb = LANE

    def body(z_ref, b_ref, fc_ref, fct_ref):
        tri = _tri(tb)

        def step(i, carry):
            r = pl.ds(pl.multiple_of(i * tb, tb), tb)
            cs = _nn(tri, jax.nn.log_sigmoid(z_ref[r, :] + b_ref[...]), HI) + carry
            fc_ref[r, :] = cs
            fct_ref[0, :, r] = cs.T[0:8, :]
            return cs[tb - 1:tb, :]

        lax.fori_loop(0, T // tb, step, jnp.zeros((1, LANE), F32))

    return pl.pallas_call(
        body, name="fox_gate_fwd", grid=(B,),
        in_specs=[pl.BlockSpec((T, LANE), lambda b: (b, C_FF // LANE)), pl.BlockSpec((1, LANE), lambda b: (0, 0))],
        out_specs=[pl.BlockSpec((T, LANE), lambda b: (b, 0)), pl.BlockSpec((1, 8, T), lambda b: (b, 0, 0))],
        out_shape=[S((N, LANE), F32), S((B, 8, T), F32)], compiler_params=_cp(("parallel",)),
    )(z, bias)


def _fox_gate_bwd(dfc, z, bias, B, T):
    N = B * T
    tb = LANE
    nt = T // tb

    def body(d_ref, z_ref, b_ref, dz_ref, db_ref):
        triu = _tri(tb, upper=True)
        db_ref[...] = jnp.zeros_like(db_ref)

        def step(ii, carry):
            r = pl.ds(pl.multiple_of((nt - 1 - ii) * tb, tb), tb)
            d = d_ref[r, 0:LANE]
            for p in range(1, FOX_P):
                d = d + d_ref[r, LANE * p:LANE * (p + 1)]
            rc = _nn(triu, d, HI) + carry
            dff = rc * jax.nn.sigmoid(-(z_ref[r, :] + b_ref[...]))
            dz_ref[r, :] = dff.astype(BF16)
            db_ref[...] += _rowsum8(dff)
            return carry + jnp.sum(d, axis=0, keepdims=True)

        lax.fori_loop(0, nt, step, jnp.zeros((1, LANE), F32))

    return pl.pallas_call(
        body, name="fox_gate_bwd", grid=(B,),
        in_specs=[pl.BlockSpec((T, 512), lambda b: (b, 0)), pl.BlockSpec((T, LANE), lambda b: (b, C_FF // LANE)),
                  pl.BlockSpec((1, LANE), lambda b: (0, 0))],
        out_specs=[pl.BlockSpec((T, LANE), lambda b: (b, 0)), pl.BlockSpec((8, LANE), lambda b: (b, 0))],
        out_shape=[S((N, LANE), BF16), S((B * 8, LANE), F32)], compiler_params=_cp(("parallel",)),
    )(dfc, z, bias)


def _fox_prep(z_ref, gq, gk, r, lo_half):
    q, k, v = z_ref[r, 0:LANE], z_ref[r, LANE:2 * LANE], z_ref[r, 2 * LANE:3 * LANE]
    rq = lax.rsqrt(_pair_mean(q * q, lo_half) + EPS)
    rk = lax.rsqrt(_pair_mean(k * k, lo_half) + EPS)
    qh, kh = q * rq, k * rk
    return qh * gq * (FOX_D ** -0.5), kh * gk, v, qh, kh, rq, rk


def _fox_fwd(z, fc, fct, gq, gk, B, T, tq=256):
    N = B * T
    NQ = T // tq

    def body(z_ref, fc_ref, fct_ref, gq_ref, gk_ref, y_ref, lse_ref, qn_s, kn_s, v_s):
        p, qi = pl.program_id(1), pl.program_id(2)
        lo_half = _iota((1, LANE), 1) < FOX_D

        @pl.when(qi == 0)
        def _():
            def prep(i, carry):
                r = pl.ds(pl.multiple_of(i * tq, tq), tq)
                qn, kn, v = _fox_prep(z_ref, gq_ref[...], gk_ref[...], r, lo_half)[:3]
                qn_s[r, :], kn_s[r, :], v_s[r, :] = qn.astype(BF16), kn.astype(BF16), v.astype(BF16)
                return carry
            lax.fori_loop(0, NQ, prep, 0)

        rq = pl.ds(pl.multiple_of(qi * tq, tq), tq)
        qn = qn_s[rq, :]
        fcq = fc_ref[rq, :]
        lane = _iota((tq, LANE), 1)
        pos_q = qi * tq + _iota((tq, tq), 0)
        outs, lses = [], []
        for hh in range(2):
            hm = lo_half if hh == 0 else jnp.logical_not(lo_half)
            qh = jnp.where(hm, qn, jnp.zeros_like(qn))
            fq = jnp.sum(jnp.where(lane == 2 * p + hh, fcq, 0.0), axis=-1, keepdims=True)

            def kv(j, carry):
                m, l, acc = carry
                rk = pl.ds(pl.multiple_of(j * tq, tq), tq)
                s = _nt(qh, kn_s[rk, :]) + fq - fct_ref[0, pl.ds(2 * p + hh, 1), rk]
                s = jnp.where(pos_q >= j * tq + _iota((tq, tq), 1), s, NEG)
                m_new = jnp.maximum(m, jnp.max(s, axis=-1, keepdims=True))
                pe = jnp.exp(s - m_new)
                alpha = jnp.exp(m - m_new)
                return (m_new, alpha * l + jnp.sum(pe, axis=-1, keepdims=True),
                        alpha * acc + _nn(pe.astype(BF16), v_s[rk, :]))

            m, l, acc = lax.fori_loop(0, qi + 1, kv, (jnp.full((tq, 1), NEG, F32), jnp.zeros((tq, 1), F32),
                                                     jnp.zeros((tq, LANE), F32)))
            outs.append(acc / l)
            lses.append(m + jnp.log(l))
        y_ref[...] = jnp.where(lo_half, outs[0], outs[1]).astype(BF16)
        lse_ref[...] = jnp.where(lo_half, lses[0], lses[1])

    vec = pl.BlockSpec((1, LANE), lambda b, p, q: (0, 0))
    tile = pl.BlockSpec((tq, LANE), lambda b, p, q: (b * NQ + q, p))
    return pl.pallas_call(
        body, name="fox_fwd", grid=(B, FOX_P, NQ),
        in_specs=[pl.BlockSpec((T, 384), lambda b, p, q: (b, p)), pl.BlockSpec((T, LANE), lambda b, p, q: (b, 0)),
                  pl.BlockSpec((1, 8, T), lambda b, p, q: (b, 0, 0)), vec, vec],
        out_specs=[tile, tile], out_shape=[S((N, 512), BF16), S((N, 512), F32)],
        scratch_shapes=[pltpu.VMEM((T, LANE), BF16)] * 3,
        compiler_params=_cp(("parallel", "parallel", "arbitrary")),
    )(z, fc, fct, gq, gk)


def _fox_bwd(z, dy, y, lse, fc, fct, gq, gk, B, T, tq=256):
    N = B * T
    NQ = T // tq

    def body(z_ref, dy_ref, y_ref, lse_ref, fc_ref, fct_ref, gq_ref, gk_ref, dz_ref, dfc_ref, dgq_ref, dgk_ref,
             qn_s, kn_s, v_s, do_s, delta_s, dq_s, dfk_s):
        p, kj = pl.program_id(1), pl.program_id(2)
        lo_half = _iota((1, LANE), 1) < FOX_D
        lane = _iota((tq, LANE), 1)
        gq_v, gk_v = gq_ref[...], gk_ref[...]

        @pl.when(kj == 0)
        def _():
            def prep(i, carry):
                r = pl.ds(pl.multiple_of(i * tq, tq), tq)
                qn, kn, v = _fox_prep(z_ref, gq_v, gk_v, r, lo_half)[:3]
                qn_s[r, :], kn_s[r, :], v_s[r, :] = qn.astype(BF16), kn.astype(BF16), v.astype(BF16)
                do = dy_ref[r, :]
                do_s[r, :] = do.astype(BF16)
                delta_s[r, :] = _pair_mean(do * y_ref[r, :].astype(F32), lo_half) * float(FOX_D)
                return carry
            lax.fori_loop(0, NQ, prep, 0)
            dq_s[...] = jnp.zeros_like(dq_s)
            dgq_ref[...] = jnp.zeros_like(dgq_ref)
            dgk_ref[...] = jnp.zeros_like(dgk_ref)

        rk = pl.ds(pl.multiple_of(kj * tq, tq), tq)
        kn, vv = kn_s[rk, :], v_s[rk, :]
        pos_k = kj * tq + _iota((tq, tq), 1)
        dks, dvs = [], []
        for hh in range(2):
            hm = lo_half if hh == 0 else jnp.logical_not(lo_half)
            c0 = 0 if hh == 0 else FOX_D
            kmask = jnp.where(hm, kn, jnp.zeros_like(kn))
            kaug = jnp.where(hm, kn, jnp.ones_like(kn))
            vmask = jnp.where(hm, vv, jnp.zeros_like(vv))
            fk = fct_ref[0, pl.ds(2 * p + hh, 1), rk]

            def qloop(i, carry):
                dk_acc, dv_acc = carry
                ri = pl.ds(pl.multiple_of(i * tq, tq), tq)
                qn = qn_s[ri, :]
                do = do_s[ri, :]
                fq = jnp.sum(jnp.where(lane == 2 * p + hh, fc_ref[ri, :], 0.0), axis=-1, keepdims=True)
                s = _nt(qn, kmask) + fq - fk - lse_ref[ri, c0:c0 + 1]
                pr = jnp.where(i * tq + _iota((tq, tq), 0) >= pos_k, jnp.exp(s), 0.0)
                dp = _nt(do, vmask)
                ds = (pr * (dp - delta_s[ri, c0:c0 + 1])).astype(BF16)
                dq_s[hh, ri, :] += _nn(ds, kaug)
                return (dk_acc + _tn(ds, jnp.where(hm, qn, jnp.ones_like(qn))), dv_acc + _tn(pr.astype(BF16), do))

            dk_h, dv_h = lax.fori_loop(kj, NQ, qloop, (jnp.zeros((tq, LANE), F32), jnp.zeros((tq, LANE), F32)))
            dks.append(dk_h)
            dvs.append(dv_h)

        dkn = jnp.where(lo_half, dks[0], dks[1])
        _, _, _, _, kh, _, rkk = _fox_prep(z_ref, gq_v, gk_v, rk, lo_half)
        u = dkn * gk_v
        dz_ref[rk, LANE:2 * LANE] = (rkk * (u - kh * _pair_mean(u * kh, lo_half))).astype(BF16)
        dz_ref[rk, 2 * LANE:3 * LANE] = jnp.where(lo_half, dvs[0], dvs[1]).astype(BF16)
        dgk_ref[...] += _rowsum8(dkn * kh)
        dfk_s[rk, :] = jnp.where(lane == 2 * p, -dks[0][:, FOX_D:FOX_D + 1],
                                 jnp.where(lane == 2 * p + 1, -dks[1][:, 0:1], 0.0))

        @pl.when(kj == NQ - 1)
        def _():
            def fin(i, carry):
                r = pl.ds(pl.multiple_of(i * tq, tq), tq)
                d0, d1 = dq_s[0, r, :], dq_s[1, r, :]
                dqn = jnp.where(lo_half, d0, d1)
                _, _, _, qh, _, rqq, _ = _fox_prep(z_ref, gq_v, gk_v, r, lo_half)
                u = dqn * gq_v * (FOX_D ** -0.5)
                dz_ref[r, 0:LANE] = (rqq * (u - qh * _pair_mean(u * qh, lo_half))).astype(BF16)
                dgq_ref[...] += _rowsum8(dqn * qh) * (FOX_D ** -0.5)
                dfc_ref[r, :] = dfk_s[r, :] + jnp.where(lane == 2 * p, d0[:, FOX_D:FOX_D + 1],
                                                        jnp.where(lane == 2 * p + 1, d1[:, 0:1], 0.0))
                return carry
            lax.fori_loop(0, NQ, fin, 0)

    vec = pl.BlockSpec((1, LANE), lambda b, p, k: (0, 0))
    col = pl.BlockSpec((T, LANE), lambda b, p, k: (b, p))
    part = pl.BlockSpec((8, LANE), lambda b, p, k: (b * FOX_P + p, 0))
    return pl.pallas_call(
        body, name="fox_bwd", grid=(B, FOX_P, NQ),
        in_specs=[pl.BlockSpec((T, 384), lambda b, p, k: (b, p)), col, col, col,
                  pl.BlockSpec((T, LANE), lambda b, p, k: (b, 0)), pl.BlockSpec((1, 8, T), lambda b, p, k: (b, 0, 0)),
                  vec, vec],
        out_specs=[pl.BlockSpec((T, 384), lambda b, p, k: (b, p)), col, part, part],
        out_shape=[S((N, 1536), BF16), S((N, 512), F32), S((B * FOX_P * 8, LANE), F32), S((B * FOX_P * 8, LANE), F32)],
        scratch_shapes=[pltpu.VMEM((T, LANE), BF16)] * 4 + [pltpu.VMEM((T, LANE), F32), pltpu.VMEM((2, T, LANE), F32),
                                                            pltpu.VMEM((T, LANE), F32)],
        compiler_params=_cp(("parallel", "parallel", "arbitrary")),
    )(z, dy, y, lse, fc, fct, gq, gk)


def _mem_scores(z_ref, kv_ref, gq, gk, h):
    c = slice(MEM_D * h, MEM_D * (h + 1))
    q, k = z_ref[:, c], kv_ref[:, c]
    rq = lax.rsqrt(jnp.mean(q * q, axis=-1, keepdims=True) + EPS)
    rk = lax.rsqrt(jnp.mean(k * k, axis=-1, keepdims=True) + EPS)
    qh, kh = q * rq, k * rk
    qn = (qh * gq * (MEM_D ** -0.5)).astype(BF16)
    kn = (kh * gk).astype(BF16)
    s = _nt(qn, kn)
    pe = jnp.exp(s - jnp.max(s, axis=-1, keepdims=True))
    pn = pe / jnp.sum(pe, axis=-1, keepdims=True)
    return pn, qn, kn, qh, kh, rq, rk


def _mem_fwd(z, memkv, gq, gk, B, T, M, tq=512):
    N = B * T
    NQ = T // tq
    W = MEM_H * MEM_D

    def body(z_ref, kv_ref, gq_ref, gk_ref, y_ref):
        for h in range(MEM_H):
            pn = _mem_scores(z_ref, kv_ref, gq_ref[...], gk_ref[...], h)[0]
            v = kv_ref[:, W + MEM_D * h:W + MEM_D * (h + 1)].astype(BF16)
            y_ref[:, MEM_D * h:MEM_D * (h + 1)] = _nn(pn.astype(BF16), v).astype(BF16)

    vec = pl.BlockSpec((1, LANE), lambda b, q: (0, 0))
    return pl.pallas_call(
        body, name="mem_fwd", grid=(B, NQ),
        in_specs=[pl.BlockSpec((tq, W), lambda b, q: (b * NQ + q, C_MQ // W)),
                  pl.BlockSpec((M, 2 * W), lambda b, q: (b, 0)), vec, vec],
        out_specs=pl.BlockSpec((tq, W), lambda b, q: (b * NQ + q, 0)), out_shape=S((N, W), BF16),
        compiler_params=_cp(("parallel", "parallel")),
    )(z, memkv, gq, gk)


def _mem_bwd(z, memkv, dy, gq, gk, B, T, M, tq=512):
    N = B * T
    NQ = T // tq
    W = MEM_H * MEM_D

    def body(z_ref, kv_ref, dy_ref, gq_ref, gk_ref, dz_ref, dkv_ref, dgq_ref, dgk_ref, acc):
        qi = pl.program_id(1)
        gq_v, gk_v = gq_ref[...], gk_ref[...]

        @pl.when(qi == 0)
        def _():
            acc[...] = jnp.zeros_like(acc)
            dgq_ref[...] = jnp.zeros_like(dgq_ref)
            dgk_ref[...] = jnp.zeros_like(dgk_ref)

        for h in range(MEM_H):
            c = slice(MEM_D * h, MEM_D * (h + 1))
            cv = slice(W + MEM_D * h, W + MEM_D * (h + 1))
            pn, qn, kn, qh, _, rq, _ = _mem_scores(z_ref, kv_ref, gq_v, gk_v, h)
            do = dy_ref[:, c].astype(BF16)
            dp = _nt(do, kv_ref[:, cv].astype(BF16))
            ds = (pn * (dp - jnp.sum(dp * pn, axis=-1, keepdims=True))).astype(BF16)
            dqn = _nn(ds, kn)
            acc[:, c] += _tn(ds, qn)
            acc[:, cv] += _tn(pn.astype(BF16), do)
            u = dqn * gq_v * (MEM_D ** -0.5)
            dz_ref[:, c] = (rq * (u - qh * jnp.mean(u * qh, axis=-1, keepdims=True))).astype(BF16)
            dgq_ref[...] += _rowsum8(dqn * qh) * (MEM_D ** -0.5)

        @pl.when(qi == NQ - 1)
        def _():
            for h in range(MEM_H):
                c = slice(MEM_D * h, MEM_D * (h + 1))
                cv = slice(W + MEM_D * h, W + MEM_D * (h + 1))
                k = kv_ref[:, c]
                rk = lax.rsqrt(jnp.mean(k * k, axis=-1, keepdims=True) + EPS)
                kh = k * rk
                dkn = acc[:, c]
                u = dkn * gk_v
                dkv_ref[:, c] = (rk * (u - kh * jnp.mean(u * kh, axis=-1, keepdims=True))).astype(BF16)
                dkv_ref[:, cv] = acc[:, cv].astype(BF16)
                dgk_ref[...] += _rowsum8(dkn * kh)

    vec = pl.BlockSpec((1, LANE), lambda b, q: (0, 0))
    part = pl.BlockSpec((8, LANE), lambda b, q: (b, 0))
    return pl.pallas_call(
        body, name="mem_bwd", grid=(B, NQ),
        in_specs=[pl.BlockSpec((tq, W), lambda b, q: (b * NQ + q, C_MQ // W)),
                  pl.BlockSpec((M, 2 * W), lambda b, q: (b, 0)), pl.BlockSpec((tq, W), lambda b, q: (b * NQ + q, 0)),
                  vec, vec],
        out_specs=[pl.BlockSpec((tq, W), lambda b, q: (b * NQ + q, 0)), pl.BlockSpec((M, 2 * W), lambda b, q: (b, 0)),
                   part, part],
        out_shape=[S((N, W), BF16), S((B * M, 2 * W), BF16), S((B * 8, LANE), F32), S((B * 8, LANE), F32)],
        scratch_shapes=[pltpu.VMEM((M, 2 * W), F32)], compiler_params=_cp(("parallel", "arbitrary")),
    )(z, memkv, dy, gq, gk)


def _merge_fwd(ya, yb, yc, z, x, wa, wb, wc, wo, tm=256):
    n, d = x.shape
    wdt = ya.shape[1]
    gb = C_GATE // d

    def body(ya_ref, yb_ref, yc_ref, g0_ref, g1_ref, g2_ref, x_ref, wa_ref, wb_ref, wc_ref, wo_ref,
             x1_ref, mg_ref, ua_ref, ub_ref, uc_ref):
        merged = jnp.zeros((tm, d), F32)
        for y_ref, g_ref, w_ref, u_ref in ((ya_ref, g0_ref, wa_ref, ua_ref), (yb_ref, g1_ref, wb_ref, ub_ref),
                                           (yc_ref, g2_ref, wc_ref, uc_ref)):
            u = _nn(y_ref[...], w_ref[...])
            u_ref[...] = u.astype(BF16)
            merged = merged + jax.nn.sigmoid(g_ref[...]) * u
        mb = merged.astype(BF16)
        mg_ref[...] = mb
        x1_ref[...] = x_ref[...] + _nn(mb, wo_ref[...])

    yt = pl.BlockSpec((tm, wdt), lambda i: (i, 0))
    xt = pl.BlockSpec((tm, d), lambda i: (i, 0))
    wbr = pl.BlockSpec((wdt, d), lambda i: (0, 0))
    gates = [pl.BlockSpec((tm, d), functools.partial(lambda i, k: (i, gb + k), k=k)) for k in range(3)]
    return pl.pallas_call(
        body, name="merge_fwd", grid=(n // tm,),
        in_specs=[yt, yt, yt] + gates + [xt, wbr, wbr, wbr, pl.BlockSpec((d, d), lambda i: (0, 0))],
        out_specs=[xt] * 5, out_shape=[S((n, d), F32)] + [S((n, d), BF16)] * 4, compiler_params=_cp(("parallel",)),
    )(ya, yb, yc, z, z, z, x, wa, wb, wc, wo)


def _merge_bwd(dx1, z, ua, ub, uc, wa, wb, wc, wo, tm=256):
    n, d = dx1.shape
    wdt = wa.shape[0]
    gb = C_GATE // d

    def body(dx_ref, g0_ref, g1_ref, g2_ref, ua_ref, ub_ref, uc_ref, wa_ref, wb_ref, wc_ref, wo_ref,
             dg_ref, dya_ref, dyb_ref, dyc_ref, dua_ref, dub_ref, duc_ref):
        dm = _nt(dx_ref[...].astype(BF16), wo_ref[...])
        for k, (g_ref, u_ref, w_ref, dy_ref, du_ref) in enumerate((
                (g0_ref, ua_ref, wa_ref, dya_ref, dua_ref), (g1_ref, ub_ref, wb_ref, dyb_ref, dub_ref),
                (g2_ref, uc_ref, wc_ref, dyc_ref, duc_ref))):
            g = jax.nn.sigmoid(g_ref[...])
            du = (dm * g).astype(BF16)
            du_ref[...] = du
            dg_ref[:, d * k:d * (k + 1)] = (dm * u_ref[...].astype(F32) * g * (1.0 - g)).astype(BF16)
            dy_ref[...] = _nt(du, w_ref[...])

    yt = pl.BlockSpec((tm, wdt), lambda i: (i, 0))
    xt = pl.BlockSpec((tm, d), lambda i: (i, 0))
    wbr = pl.BlockSpec((wdt, d), lambda i: (0, 0))
    gates = [pl.BlockSpec((tm, d), functools.partial(lambda i, k: (i, gb + k), k=k)) for k in range(3)]
    return pl.pallas_call(
        body, name="merge_bwd", grid=(n // tm,),
        in_specs=[xt] + gates + [xt, xt, xt, wbr, wbr, wbr, pl.BlockSpec((d, d), lambda i: (0, 0))],
        out_specs=[pl.BlockSpec((tm, 3 * d), lambda i: (i, 0)), yt, yt, yt, xt, xt, xt],
        out_shape=[S((n, 3 * d), BF16)] + [S((n, wdt), F32)] * 3 + [S((n, d), BF16)] * 3,
        compiler_params=_cp(("parallel",)),
    )(dx1, z, z, z, ua, ub, uc, wa, wb, wc, wo)


FFN_TN = 1408
INV_SQRT2 = 0.7071067811865476
INV_SQRT_2PI = 0.3989422804014327


def _conv_shifted(a, prev, first, tm):
    row = _iota(a.shape, 0)
    p7 = jnp.where(first, 0.0, prev[7:8, :])
    p6 = jnp.where(first, 0.0, prev[6:7, :])
    a1 = jnp.where(row == 0, p7, pltpu.roll(a, 1, 0))
    a2 = jnp.where(row == 0, p6, jnp.where(row == 1, p7, pltpu.roll(a, 2, 0)))
    return a1, a2


def _ffn_act_fwd(up, cw, cb, B, T, tm=256):
    N = B * T
    dff = cw.shape[1]
    NT, NJ, tn = T // tm, dff // FFN_TN, FFN_TN

    def body(a_ref, v_ref, cw_ref, cb_ref, y_ref, carry):
        t = pl.program_id(2)
        a = a_ref[...]
        a1, a2 = _conv_shifted(a, carry[...], t == 0, tm)
        w = cw_ref[...]
        ac = w[0:1, :] * a2 + w[1:2, :] * a1 + w[2:3, :] * a + cb_ref[...]
        y_ref[...] = (0.5 * ac * (1.0 + lax.erf(ac * INV_SQRT2)) * v_ref[...]).astype(BF16)
        carry[...] = a[tm - 8:tm, :]

    return pl.pallas_call(
        body, name="ffn_act_fwd", grid=(B, NJ, NT),
        in_specs=[pl.BlockSpec((tm, tn), lambda b, j, t: (b * NT + t, j)),
                  pl.BlockSpec((tm, tn), lambda b, j, t: (b * NT + t, NJ + j)),
                  pl.BlockSpec((3, tn), lambda b, j, t: (0, j)), pl.BlockSpec((1, tn), lambda b, j, t: (0, j))],
        out_specs=pl.BlockSpec((tm, tn), lambda b, j, t: (b * NT + t, j)), out_shape=S((N, dff), BF16),
        scratch_shapes=[pltpu.VMEM((8, tn), F32)], compiler_params=_cp(("parallel", "parallel", "arbitrary")),
    )(up, up, cw, cb)


def _ffn_down_loss(y, wd, x1, tgt, tm=256):
    n, d = x1.shape
    kf = y.shape[1]

    def body(y_ref, w_ref, x_ref, t_ref, dx_ref, ls_ref):
        err = x_ref[...] + _nn(y_ref[...], w_ref[...]) - t_ref[...]
        dx_ref[...] = err * (1.0 / d)

        @pl.when(pl.program_id(0) == 0)
        def _():
            ls_ref[...] = jnp.zeros_like(ls_ref)

        ls_ref[...] += _rowsum8(err * err) * (0.5 / d)

    xt = pl.BlockSpec((tm, d), lambda i: (i, 0))
    return pl.pallas_call(
        body, name="ffn_down_loss", grid=(n // tm,),
        in_specs=[pl.BlockSpec((tm, kf), lambda i: (i, 0)), pl.BlockSpec((kf, d), lambda i: (0, 0)), xt, xt],
        out_specs=[xt, pl.BlockSpec((8, d), lambda i: (0, 0))], out_shape=[S((n, d), F32), S((8, d), F32)],
        compiler_params=_cp(("arbitrary",)),
    )(y, wd, x1, tgt)


def _ffn_act_bwd1(dx2, wd, up, cw, cb, B, T, tm=256):
    N = B * T
    d = dx2.shape[1]
    dff = cw.shape[1]
    NT, NJ, tn = T // tm, dff // FFN_TN, FFN_TN

    def body(dx_ref, w_ref, a_ref, v_ref, cw_ref, cb_ref, dac_ref, dv_ref, dcw_ref, dcb_ref, carry):
        b, t = pl.program_id(1), pl.program_id(2)
        a = a_ref[...]
        a1, a2 = _conv_shifted(a, carry[...], t == 0, tm)
        carry[...] = a[tm - 8:tm, :]
        w = cw_ref[...]
        ac = w[0:1, :] * a2 + w[1:2, :] * a1 + w[2:3, :] * a + cb_ref[...]
        dy = _nt(dx_ref[...].astype(BF16), w_ref[...])
        cdf = 0.5 * (1.0 + lax.erf(ac * INV_SQRT2))
        dv_ref[...] = (dy * ac * cdf).astype(BF16)
        dac = dy * v_ref[...] * (cdf + ac * jnp.exp(-0.5 * ac * ac) * INV_SQRT_2PI)
        dac_ref[...] = dac

        @pl.when((b == 0) & (t == 0))
        def _():
            dcw_ref[...] = jnp.zeros_like(dcw_ref)
            dcb_ref[...] = jnp.zeros_like(dcb_ref)

        dcw_ref[0:8, :] += _rowsum8(dac * a2)
        dcw_ref[8:16, :] += _rowsum8(dac * a1)
        dcw_ref[16:24, :] += _rowsum8(dac * a)
        dcb_ref[...] += _rowsum8(dac)

    return pl.pallas_call(
        body, name="ffn_act_bwd1", grid=(NJ, B, NT),
        in_specs=[pl.BlockSpec((tm, d), lambda j, b, t: (b * NT + t, 0)), pl.BlockSpec((tn, d), lambda j, b, t: (j, 0)),
                  pl.BlockSpec((tm, tn), lambda j, b, t: (b * NT + t, j)),
                  pl.BlockSpec((tm, tn), lambda j, b, t: (b * NT + t, NJ + j)),
                  pl.BlockSpec((3, tn), lambda j, b, t: (0, j)), pl.BlockSpec((1, tn), lambda j, b, t: (0, j))],
        out_specs=[pl.BlockSpec((tm, tn), lambda j, b, t: (b * NT + t, j)),
                   pl.BlockSpec((tm, tn), lambda j, b, t: (b * NT + t, j)),
                   pl.BlockSpec((24, tn), lambda j, b, t: (0, j)), pl.BlockSpec((8, tn), lambda j, b, t: (0, j))],
        out_shape=[S((N, dff), F32), S((N, dff), BF16), S((24, dff), F32), S((8, dff), F32)],
        scratch_shapes=[pltpu.VMEM((8, tn), F32)], compiler_params=_cp(("parallel", "arbitrary", "arbitrary")),
    )(dx2, wd, up, up, cw, cb)


def _ffn_act_bwd2(dac, cw, B, T, tm=256):
    N = B * T
    dff = cw.shape[1]
    NT, NJ, tn = T // tm, dff // FFN_TN, FFN_TN
    last8 = N // 8 - 1

    def body(d_ref, nx_ref, cw_ref, da_ref):
        t = pl.program_id(2)
        dd = d_ref[...]
        row = _iota(dd.shape, 0)
        last = t == NT - 1
        n0 = jnp.where(last, 0.0, nx_ref[0:1, :])
        n1 = jnp.where(last, 0.0, nx_ref[1:2, :])
        d1 = jnp.where(row == tm - 1, n0, pltpu.roll(dd, tm - 1, 0))
        d2 = jnp.where(row == tm - 1, n1, jnp.where(row == tm - 2, n0, pltpu.roll(dd, tm - 2, 0)))
        w = cw_ref[...]
        da_ref[...] = (w[2:3, :] * dd + w[1:2, :] * d1 + w[0:1, :] * d2).astype(BF16)

    return pl.pallas_call(
        body, name="ffn_act_bwd2", grid=(B, NJ, NT),
        in_specs=[pl.BlockSpec((tm, tn), lambda b, j, t: (b * NT + t, j)),
                  pl.BlockSpec((8, tn), lambda b, j, t: (jnp.minimum((b * NT + t + 1) * (tm // 8), last8), j)),
                  pl.BlockSpec((3, tn), lambda b, j, t: (0, j))],
        out_specs=pl.BlockSpec((tm, tn), lambda b, j, t: (b * NT + t, j)), out_shape=S((N, dff), BF16),
        compiler_params=_cp(("parallel", "parallel", "parallel")),
    )(dac, dac, cw)


def _small_reduce(lbl, dg_mix, dg_mem, dlb_p, dgn_p, dfb_p, dgq_p, dgk_p, dmq_p, dmk_p, dg_ffn, dcb_p, loss_p, dcw_p):
    d, dff = dg_mix.shape[1], dcb_p.shape[1]
    nbh = dlb_p.shape[0] // (8 * HG_H)

    def colsum(ref):
        return jnp.sum(ref[...], axis=0, keepdims=True)

    def body(lbl_ref, mix_ref, mem_ref, dlb_ref, dgn_ref, dfb_ref, dgq_ref, dgk_ref, dmq_ref, dmk_ref, ffn_ref, dcb_ref,
             ls_ref, dcw_ref, o_mix, o_mem, o_lb, o_hgn, o_fb, o_fq, o_fk, o_mq, o_mk, o_ffn, o_cb, o_loss, o_cw):
        o_mix[...], o_mem[...], o_ffn[...], o_cb[...] = colsum(mix_ref), colsum(mem_ref), colsum(ffn_ref), colsum(dcb_ref)
        for j in range(3):
            o_cw[j:j + 1, :] = jnp.sum(dcw_ref[8 * j:8 * (j + 1), :], axis=0, keepdims=True)
        o_hgn[...], o_fb[...], o_mq[...], o_mk[...] = colsum(dgn_ref), colsum(dfb_ref), colsum(dmq_ref), colsum(dmk_ref)
        for src, dst in ((dgq_ref, o_fq), (dgk_ref, o_fk)):
            v = colsum(src)
            dst[...] = v + pltpu.roll(v, FOX_D, 1)
        o_loss[...] = jnp.zeros((1, LANE), F32) + jnp.sum(colsum(ls_ref), axis=-1, keepdims=True)
        logits = lbl_ref[...]
        e = jnp.exp(logits - jnp.max(logits, axis=0, keepdims=True))
        pr = e / jnp.sum(e, axis=0, keepdims=True)
        rows = _iota((8, LANE), 0)
        for h in range(HG_H):
            acc = jnp.zeros((8, LANE), F32)
            for b in range(nbh):
                acc = acc + dlb_ref[8 * (b * HG_H + h):8 * (b * HG_H + h + 1), :]
            dlb = jnp.sum(acc, axis=0, keepdims=True)
            c = slice(LANE * h, LANE * (h + 1))
            p0 = pr[0:1, c]
            first = _iota((logits.shape[0], LANE), 0) == 0
            o_lb[:, c] = pr[:, c] * (jnp.where(first, 1.0, 0.0) - p0) * dlb

    outs = [S((1, d), F32), S((1, d), F32), S(lbl.shape, F32)] + [S((1, LANE), F32)] * 6 + \
           [S((1, d), F32), S((1, dff), F32), S((1, LANE), F32), S((3, dff), F32)]
    return pl.pallas_call(body, name="small_reduce", out_shape=outs, compiler_params=_cp())(
        lbl, dg_mix, dg_mem, dlb_p, dgn_p, dfb_p, dgq_p, dgk_p, dmq_p, dmk_p, dg_ffn, dcb_p, loss_p, dcw_p)


def _in_col_pieces():
    hw, fw = HG_H * HG_D, FOX_H * FOX_D
    fox0, ff0 = 4 * hw, 4 * hw + 3 * fw
    mq0 = ff0 + FOX_H
    gate0 = mq0 + MEM_H * MEM_D
    pieces = []
    for p in range(FOX_P):
        pieces += [(fox0 + j * fw + LANE * p, LANE) for j in range(3)]
    pieces.append((mq0, MEM_H * MEM_D))
    for h in range(HG_H):
        pieces += [(j * hw + HG_D * h, HG_D) for j in range(4)]
    pieces.append((gate0, C_FF - C_GATE))
    pieces.append((ff0, FOX_H))
    return pieces


def _perm_cols(w):
    parts = [w[:, s:s + n] for s, n in _in_col_pieces()]
    parts.append(jnp.zeros((w.shape[0], C_END - C_FF - FOX_H), w.dtype))
    return jnp.concatenate(parts, axis=1)


def _unperm_cols(g):
    new_start, placed = 0, []
    for s, n in _in_col_pieces():
        placed.append((s, new_start, n))
        new_start += n
    return jnp.concatenate([g[:, ns:ns + n] for _, ns, n in sorted(placed)], axis=1)


def _local_step(x2, mem2, tgt, sm, W, B, T, M):
    fbias = jnp.pad(sm["fox_f_bias"], ((0, 0), (0, LANE - FOX_H)))
    gq2 = jnp.concatenate([sm["fox_q_norm_g"]] * 2, axis=1)
    gk2 = jnp.concatenate([sm["fox_k_norm_g"]] * 2, axis=1)
    lbl = sm["hgrn_lb_logits"]
    h = _rmsnorm_cast(x2, sm["norm_mix_g"], "norm_mix")
    z = _mm_nn(h, W["w_in"], F32, "proj_in", 512, 2432)
    memn = _rmsnorm_cast(mem2, sm["norm_mem_g"], "norm_mem", tm=256)
    memkv = _mm_nn(memn, W["mem_kv_w"], F32, "proj_memkv", 256, 512)
    ya, o_raw, states = _hgrn_fwd(z, lbl, sm["hgrn_norm_g"], B, T)
    fc, fct = _fox_gate_fwd(z, fbias, B, T)
    yb, lse = _fox_fwd(z, fc, fct, gq2, gk2, B, T)
    yc = _mem_fwd(z, memkv, sm["mem_q_norm_g"], sm["mem_k_norm_g"], B, T, M)
    x1, merged, ua, ub, uc = _merge_fwd(ya, yb, yc, z, x2, W["w_br_hgrn"], W["w_br_fox"], W["w_br_mem"], W["w_out"])
    h2 = _rmsnorm_cast(x1, sm["norm_ffn_g"], "norm_ffn")
    up = _mm_nn(h2, W["ffn_w_up"], F32, "ffn_up", 512, FFN_TN)
    yf = _ffn_act_fwd(up, W["ffn_conv_w"], sm["ffn_conv_b"], B, T)
    dx2, loss_p = _ffn_down_loss(yf, W["ffn_w_down"], x1, tgt)
    dff = W["ffn_conv_w"].shape[1]
    dac, dv, dcw_p, dcb_p = _ffn_act_bwd1(dx2, W["ffn_w_down"], up, W["ffn_conv_w"], sm["ffn_conv_b"], B, T)
    da = _ffn_act_bwd2(dac, W["ffn_conv_w"], B, T)
    g = {}
    g["ffn_w_down"] = _mm_tn(yf, dx2, "g_w_down", 512, 512)
    dh2 = _mm_nt(da, W["ffn_w_up"], "dh2_a", 512, FFN_TN)
    dh2 = _mm_nt(dv, W["ffn_w_up"], "dh2_v", 512, FFN_TN, w_col0=dff, acc=dh2)
    g["ffn_w_up"] = jnp.concatenate([_mm_tn(h2, da, "g_w_up_a", 512, FFN_TN), _mm_tn(h2, dv, "g_w_up_v", 512, FFN_TN)], axis=1)
    dx1, dg_ffn = _rmsnorm_bwd(dh2, x1, sm["norm_ffn_g"], dx2, "norm_ffn_bwd")
    g["w_out"] = _mm_tn(merged, dx1, "g_w_out", 512, 512)
    dgate, dya, dyb, dyc, dua, dub, duc = _merge_bwd(dx1, z, ua, ub, uc, W["w_br_hgrn"], W["w_br_fox"], W["w_br_mem"],
                                                    W["w_out"])
    g["w_br_hgrn"] = _mm_tn(ya, dua, "g_w_br_hgrn", 512, 512)
    g["w_br_fox"] = _mm_tn(yb, dub, "g_w_br_fox", 512, 512)
    g["w_br_mem"] = _mm_tn(yc, duc, "g_w_br_mem", 512, 512)
    dz_hg, dlb_p, dgn_p = _hgrn_bwd(z, o_raw, states, dya, lbl, sm["hgrn_norm_g"], B, T)
    dz_fox, dfc, dgq_p, dgk_p = _fox_bwd(z, dyb, yb, lse, fc, fct, gq2, gk2, B, T)
    dz_ff, dfb_p = _fox_gate_bwd(dfc, z, fbias, B, T)
    dz_mq, dkv, dmq_p, dmk_p = _mem_bwd(z, memkv, dyc, sm["mem_q_norm_g"], sm["mem_k_norm_g"], B, T, M)
    g["mem_kv_w"] = _mm_tn(memn, dkv, "g_mem_kv_w", 256, 512)
    dmemn = _mm_nt(dkv, W["mem_kv_w"], "d_memn", 256, 512)
    _, dg_mem = _rmsnorm_bwd(dmemn, mem2, sm["norm_mem_g"], None, "norm_mem_bwd", tm=256)
    segs = ((dz_fox, C_FOX, 512), (dz_mq, C_MQ, 512), (dz_hg, C_HG, 512), (dgate, C_GATE, 512), (dz_ff, C_FF, LANE))
    dh = None
    gw = []
    for i, (dzs, c0, tr) in enumerate(segs):
        dh = _mm_nt(dzs, W["w_in"], "dh_%d" % i, 512, tr, w_col0=c0, acc=dh)
        gw.append(_mm_tn(h, dzs, "g_w_in_%d" % i, 512, min(512, dzs.shape[1])))
    g["w_in"] = jnp.concatenate(gw, axis=1)
    grad_x, dg_mix = _rmsnorm_bwd(dh, x2, sm["norm_mix_g"], dx1, "norm_mix_bwd")
    small = _small_reduce(lbl, dg_mix, dg_mem, dlb_p, dgn_p, dfb_p, dgq_p, dgk_p, dmq_p, dmk_p, dg_ffn, dcb_p, loss_p,
                          dcw_p)
    names = ("norm_mix_g", "norm_mem_g", "hgrn_lb_logits", "hgrn_norm_g", "fox_f_bias", "fox_q_norm_g", "fox_k_norm_g",
             "mem_q_norm_g", "mem_k_norm_g", "norm_ffn_g", "ffn_conv_b", "loss", "ffn_conv_w")
    g.update(dict(zip(names, small)))
    return grad_x, g


ANY = pl.BlockSpec(memory_space=pl.ANY)


def _position():
    return lax.axis_index("x"), lax.axis_index("y"), lax.axis_index("c")


def _all_gather(block, name):
    def body(x_ref, out_ref, send_sems, recv_sems, local_sem):
        x, y, c = _position()
        me, sibling = (x, y, c), (x, y, 1 - c)
        chips = [(1 - x, y), (x, 1 - y), (1 - x, 1 - y)]

        def slot(px, py, pc):
            return out_ref.at[4 * px + 2 * py + pc]

        def copy(k, blk, to, src=None):
            return pltpu.make_async_remote_copy(
                src_ref=slot(*blk) if src is None else src, dst_ref=slot(*blk), send_sem=send_sems.at[k],
                recv_sem=recv_sems.at[k], device_id=to, device_id_type=MESH)

        mine = pltpu.make_async_copy(x_ref, slot(*me), local_sem)
        mine.start()
        first = [copy(0, me, sibling, src=x_ref)]
        first += [copy(1 + j, me, (*chip, c), src=x_ref) for j, chip in enumerate(chips)]
        for cp in first:
            cp.start()
        passed = [copy(4 + j, (*chip, c), sibling) for j, chip in enumerate(chips)]
        for j, chip in enumerate(chips):
            copy(1 + j, (*chip, c), me).wait_recv()
            passed[j].start()
        copy(0, sibling, me).wait_recv()
        for j, chip in enumerate(chips):
            copy(4 + j, (*chip, 1 - c), me).wait_recv()
        for cp in first + passed:
            cp.wait_send()
        mine.wait()

    return pl.pallas_call(
        body, name=name, out_shape=S((N_DEV,) + block.shape, block.dtype), in_specs=[ANY], out_specs=ANY,
        scratch_shapes=[pltpu.SemaphoreType.DMA((7,)), pltpu.SemaphoreType.DMA((7,)), pltpu.SemaphoreType.DMA],
    )(block)


def _swap_with_sibling(pk):
    _, r, l = pk.shape

    def body(pk_ref, out_ref, send_sems, recv_sems):
        x, y, c = _position()
        copies = [pltpu.make_async_remote_copy(
            src_ref=pk_ref.at[2 * k + 1 - c], dst_ref=out_ref.at[k], send_sem=send_sems.at[k], recv_sem=recv_sems.at[k],
            device_id=(x, y, 1 - c), device_id_type=MESH) for k in range(4)]
        for cp in copies:
            cp.start()
        for cp in copies:
            cp.wait()

    return pl.pallas_call(
        body, name="rs_sibling", out_shape=S((4, r, l), pk.dtype), in_specs=[ANY], out_specs=ANY,
        scratch_shapes=[pltpu.SemaphoreType.DMA((4,)), pltpu.SemaphoreType.DMA((4,))],
    )(pk)


def _swap_between_chips(pb):
    def body(pb_ref, out_ref, send_sems, recv_sems, local_sem):
        x, y, c = _position()
        me = 2 * x + y
        chips = [(1 - x, y), (x, 1 - y), (1 - x, 1 - y)]
        local = pltpu.make_async_copy(pb_ref.at[me], out_ref.at[me], local_sem)
        local.start()
        sends = [pltpu.make_async_remote_copy(
            src_ref=pb_ref.at[2 * cx + cy], dst_ref=out_ref.at[me], send_sem=send_sems.at[j], recv_sem=recv_sems.at[j],
            device_id=(cx, cy, c), device_id_type=MESH) for j, (cx, cy) in enumerate(chips)]
        for cp in sends:
            cp.start()
        for j, (cx, cy) in enumerate(chips):
            pltpu.make_async_remote_copy(
                src_ref=pb_ref.at[me], dst_ref=out_ref.at[2 * cx + cy], send_sem=send_sems.at[j],
                recv_sem=recv_sems.at[j], device_id=(cx, cy, c), device_id_type=MESH).wait_recv()
        for cp in sends:
            cp.wait_send()
        local.wait()

    return pl.pallas_call(
        body, name="rs_chips", out_shape=S(pb.shape, pb.dtype), in_specs=[ANY], out_specs=ANY,
        scratch_shapes=[pltpu.SemaphoreType.DMA((3,)), pltpu.SemaphoreType.DMA((3,)), pltpu.SemaphoreType.DMA],
    )(pb)


PACK_TR = 1024


def _pair_sum_cast(pk, recv, core):
    _, r, l = pk.shape

    def body(c_ref, a_ref, b_ref, o_ref):
        o_ref[...] = (a_ref[...] + b_ref[...]).astype(BF16)

    return pl.pallas_call(
        body, name="rs_pair_sum",
        grid_spec=pltpu.PrefetchScalarGridSpec(
            num_scalar_prefetch=1, grid=(4, r // PACK_TR),
            in_specs=[pl.BlockSpec((None, PACK_TR, l), lambda k, i, c: (2 * k + c[0], i, 0)),
                      pl.BlockSpec((None, PACK_TR, l), lambda k, i, c: (k, i, 0))],
            out_specs=pl.BlockSpec((None, PACK_TR, l), lambda k, i, c: (k, i, 0))),
        out_shape=S((4, r, l), BF16), compiler_params=_cp(("parallel", "parallel")),
    )(core, pk, recv)


def _final_sum(pk, recv_sib, recv_chips, slot, chip):
    _, r, l = pk.shape

    def body(s_ref, k_ref, a_ref, b_ref, rc_ref, o_ref):
        base = a_ref[...] + b_ref[...]
        acc = jnp.zeros_like(base)
        for j in range(4):
            acc = acc + jnp.where(k_ref[0] == j, base, rc_ref[j].astype(F32))
        o_ref[...] = acc

    return pl.pallas_call(
        body, name="rs_final_sum",
        grid_spec=pltpu.PrefetchScalarGridSpec(
            num_scalar_prefetch=2, grid=(r // PACK_TR,),
            in_specs=[pl.BlockSpec((None, PACK_TR, l), lambda i, s, k: (s[0], i, 0)),
                      pl.BlockSpec((None, PACK_TR, l), lambda i, s, k: (k[0], i, 0)),
                      pl.BlockSpec((4, PACK_TR, l), lambda i, s, k: (0, i, 0))],
            out_specs=pl.BlockSpec((PACK_TR, l), lambda i, s, k: (i, 0))),
        out_shape=S((r, l), F32), compiler_params=_cp(("parallel",)),
    )(slot, chip, pk, recv_sib, recv_chips)


def _adamw_math(w, g, m, v):
    m = ADAM_B1 * m + (1.0 - ADAM_B1) * g
    v = ADAM_B2 * v + (1.0 - ADAM_B2) * (g * g)
    m_hat = m / (1.0 - ADAM_B1 ** ADAM_STEP)
    v_hat = v / (1.0 - ADAM_B2 ** ADAM_STEP)
    return -ADAM_LR * (m_hat / (jnp.sqrt(v_hat) + ADAM_EPS) + ADAM_WD * w), m, v


def _adamw(w, g, m, v, name):
    r, c = w.shape
    tr = 256 if r % 256 == 0 else r

    def body(w_ref, g_ref, m_ref, v_ref, d_ref, nm_ref, nv_ref):
        d_ref[...], nm_ref[...], nv_ref[...] = _adamw_math(w_ref[...], g_ref[...], m_ref[...], v_ref[...])

    tile = pl.BlockSpec((tr, c), lambda i: (i, 0))
    return pl.pallas_call(
        body, name=name, grid=(r // tr,), in_specs=[tile] * 4, out_specs=[tile] * 3, out_shape=[S((r, c), F32)] * 3,
        compiler_params=_cp(("parallel",)),
    )(w, g, m, v)


def _small_update(gathered, w, m, v):
    def body(ga_ref, w_ref, m_ref, v_ref, g_ref, d_ref, nm_ref, nv_ref):
        g = ga_ref[0]
        for k in range(1, N_DEV):
            g = g + ga_ref[k]
        g_ref[...] = g
        d_ref[...], nm_ref[...], nv_ref[...] = _adamw_math(w_ref[...], g, m_ref[...], v_ref[...])

    return pl.pallas_call(body, name="small_update", out_shape=[S(w.shape, F32)] * 4, compiler_params=_cp())(
        gathered, w, m, v)


BIG = (("w_in", "col"), ("mem_kv_w", "row"), ("w_br_hgrn", "col"), ("w_br_fox", "col"), ("w_br_mem", "col"),
       ("w_out", "row"), ("ffn_w_up", "col"), ("ffn_conv_w", "col"), ("ffn_w_down", "row"))
SMALL = ("norm_mix_g", "norm_mem_g", "hgrn_lb_logits", "hgrn_norm_g", "fox_f_bias", "fox_q_norm_g", "fox_k_norm_g",
         "mem_q_norm_g", "mem_k_norm_g", "norm_ffn_g", "ffn_conv_b")


def _rows_of(n_elems):
    return -(-n_elems // LANE)


def _to_rows(a, lead=0):
    flat = a.reshape(a.shape[:lead] + (-1,))
    pad = (-flat.shape[-1]) % LANE
    if pad:
        flat = jnp.pad(flat, [(0, 0)] * lead + [(0, pad)])
    return flat.reshape(a.shape[:lead] + (-1, LANE))


def _stack_rows(parts, lead, total_rows):
    buf = jnp.concatenate(parts, axis=lead)
    pad = total_rows - buf.shape[lead]
    return jnp.pad(buf, [(0, 0)] * lead + [(0, pad), (0, 0)])


def _round_up(n, k):
    return -(-n // k) * k


def _from_rows(rows, shape, lead=0):
    n = math.prod(shape)
    return rows.reshape(rows.shape[:lead] + (-1,))[..., :n].reshape(rows.shape[:lead] + tuple(shape))


def _blocks_to_full(blocks, kind):
    n, a, b = blocks.shape
    return blocks.transpose(1, 0, 2).reshape(a, n * b) if kind == "col" else blocks.reshape(n * a, b)


def _full_to_blocks(full, kind):
    a, b = full.shape
    return full.reshape(a, N_DEV, b // N_DEV).transpose(1, 0, 2) if kind == "col" else full.reshape(N_DEV, a // N_DEV, b)


def kernel(x, mem, norm_mix_g, norm_mem_g, w_in, hgrn_lb_logits, hgrn_norm_g, fox_f_bias, fox_q_norm_g, fox_k_norm_g, mem_kv_w, mem_q_norm_g, mem_k_norm_g, w_br_hgrn, w_br_fox, w_br_mem, w_out, norm_ffn_g, ffn_w_up, ffn_conv_w, ffn_conv_b, ffn_w_down, loss_target, m_norm_mix_g, m_norm_mem_g, m_w_in, m_hgrn_lb_logits, m_hgrn_norm_g, m_fox_f_bias, m_fox_q_norm_g, m_fox_k_norm_g, m_mem_kv_w, m_mem_q_norm_g, m_mem_k_norm_g, m_w_br_hgrn, m_w_br_fox, m_w_br_mem, m_w_out, m_norm_ffn_g, m_ffn_w_up, m_ffn_conv_w, m_ffn_conv_b, m_ffn_w_down, v_norm_mix_g, v_norm_mem_g, v_w_in, v_hgrn_lb_logits, v_hgrn_norm_g, v_fox_f_bias, v_fox_q_norm_g, v_fox_k_norm_g, v_mem_kv_w, v_mem_q_norm_g, v_mem_k_norm_g, v_w_br_hgrn, v_w_br_fox, v_w_br_mem, v_w_out, v_norm_ffn_g, v_ffn_w_up, v_ffn_conv_w, v_ffn_conv_b, v_ffn_w_down):
    given = dict(locals())
    order = ("norm_mix_g", "norm_mem_g", "w_in", "hgrn_lb_logits", "hgrn_norm_g", "fox_f_bias", "fox_q_norm_g",
             "fox_k_norm_g", "mem_kv_w", "mem_q_norm_g", "mem_k_norm_g", "w_br_hgrn", "w_br_fox", "w_br_mem", "w_out",
             "norm_ffn_g", "ffn_w_up", "ffn_conv_w", "ffn_conv_b", "ffn_w_down")
    B, T, D = x.shape
    M = mem.shape[1]
    big_names = [n for n, _ in BIG]
    shard = {n: given[n][0] if n in big_names else given[n] for n in order}
    mom = {n: (given["m_" + n][0], given["v_" + n][0]) if n in big_names else (given["m_" + n], given["v_" + n])
           for n in order}
    shard["hgrn_lb_logits"] = hgrn_lb_logits
    for n in ("norm_mix_g", "norm_mem_g", "hgrn_norm_g", "fox_f_bias", "fox_q_norm_g", "fox_k_norm_g", "mem_q_norm_g",
              "mem_k_norm_g", "norm_ffn_g", "ffn_conv_b"):
        shard[n] = given[n].reshape(1, -1)

    parts, layout, row0 = [], {}, 0
    for n, _ in BIG:
        a = shard[n]
        a = lax.bitcast_convert_type(a, BF16) if n == "ffn_conv_w" else a.astype(BF16)
        rows = _to_rows(a)
        layout[n] = (row0, rows.shape[0])
        row0 += rows.shape[0]
        parts.append(rows)
    r_pack = _round_up(row0, PACK_TR)
    gathered = _all_gather(_stack_rows(parts, 0, r_pack), "ag_weights")
    W = {}
    for n, kind in BIG:
        r0, nr = layout[n]
        seg = gathered[:, r0:r0 + nr]
        if n == "ffn_conv_w":
            blocks = lax.bitcast_convert_type(_from_rows(seg, shard[n].shape + (2,), lead=1), F32)
        else:
            blocks = _from_rows(seg, shard[n].shape, lead=1)
        W[n] = _blocks_to_full(blocks, kind)
    W["w_in"] = _perm_cols(W["w_in"])

    sm = {n: shard[n] for n in SMALL}
    grad_x, g = _local_step(x.reshape(B * T, D), mem.reshape(B * M, D), loss_target.reshape(B * T, D), sm, W, B, T, M)
    g["w_in"] = _unperm_cols(g["w_in"])

    xi, yi, ci = _position()
    gparts, glayout, row0 = [], {}, 0
    for n, kind in BIG:
        rows = _to_rows(_full_to_blocks(g[n], kind), lead=1)
        glayout[n] = (row0, rows.shape[1])
        row0 += rows.shape[1]
        gparts.append(rows)
    pk = _stack_rows(gparts, 1, _round_up(row0, PACK_TR))
    core = ci.astype(jnp.int32).reshape(1)
    chip = (2 * xi + yi).astype(jnp.int32).reshape(1)
    recv_sib = _swap_with_sibling(pk)
    recv_chips = _swap_between_chips(_pair_sum_cast(pk, recv_sib, core))
    g_sum = _final_sum(pk, recv_sib, recv_chips, 2 * chip + core, chip)

    sg = {n: g[n] for n in SMALL}
    sg["fox_f_bias"] = g["fox_f_bias"][:, :FOX_H]
    sg["fox_q_norm_g"] = g["fox_q_norm_g"][:, :FOX_D]
    sg["fox_k_norm_g"] = g["fox_k_norm_g"][:, :FOX_D]
    slayout, row0 = {}, 0
    for n in SMALL:
        nr = _rows_of(shard[n].size)
        slayout[n] = (row0, nr)
        row0 += nr
    loss_row = row0
    r_small = _round_up(row0 + 1, 8)

    def pack_small(d, with_loss=None):
        rows = [_to_rows(d[n]) for n in SMALL]
        rows.append(with_loss if with_loss is not None else jnp.zeros((1, LANE), F32))
        return _stack_rows(rows, 0, r_small)

    sgath = _all_gather(pack_small(sg, g["loss"]), "ag_small")
    s_g, s_d, s_m, s_v = _small_update(sgath, pack_small(shard), pack_small({n: mom[n][0].reshape(shard[n].shape) for n in SMALL}),
                                       pack_small({n: mom[n][1].reshape(shard[n].shape) for n in SMALL}))
    loss = s_g[loss_row, 0]

    grads, deltas, new_m, new_v = {}, {}, {}, {}
    for n, _ in BIG:
        r0, nr = glayout[n]
        gn = _from_rows(g_sum[r0:r0 + nr], shard[n].shape)
        d, nm, nv = _adamw(shard[n], gn, mom[n][0], mom[n][1], "adamw_" + n)
        grads[n], deltas[n], new_m[n], new_v[n] = (a[None] for a in (gn, d, nm, nv))
    for n in SMALL:
        r0, nr = slayout[n]
        for dst, src in ((grads, s_g), (deltas, s_d), (new_m, s_m), (new_v, s_v)):
            dst[n] = _from_rows(src[r0:r0 + nr], given[n].shape)
    return (loss, grad_x.reshape(B, T, D), *[grads[n] for n in order], *[deltas[n] for n in order],
            *[new_m[n] for n in order], *[new_v[n] for n in order])
```

```python
import functools
import math

import jax
import jax.numpy as jnp
from jax import lax
from jax.experimental import pallas as pl
from jax.experimental.pallas import tpu as pltpu

F32, BF16 = jnp.float32, jnp.bfloat16
S = jax.ShapeDtypeStruct
MESH = pl.DeviceIdType.MESH

N_DEV = 8
EPS = 1e-6
LANE = 128
CHUNK = 64
SUB = 16
HG_H, HG_D = 4, 128
HG_GROUP = 2
FOX_H, FOX_D = 8, 64
FOX_P = FOX_H // 2
MEM_H, MEM_D = 4, 128
NEG = -1e30
VMEM_LIMIT = 56 * 2**20

ADAM_LR, ADAM_B1, ADAM_B2, ADAM_EPS, ADAM_WD, ADAM_STEP = 0.001, 0.9, 0.999, 1e-08, 0.01, 10

C_FOX, C_MQ, C_HG, C_GATE, C_FF, C_END = 0, 1536, 2048, 4096, 7168, 7296


def _cp(sem=None):
    return pltpu.CompilerParams(dimension_semantics=sem, vmem_limit_bytes=VMEM_LIMIT)


def _dot(a, b, dims, prec=None):
    return lax.dot_general(a, b, (dims, ((), ())), preferred_element_type=F32, precision=prec)


def _nn(a, b, prec=None):
    return _dot(a, b, ((1,), (0,)), prec)


def _nt(a, b, prec=None):
    return _dot(a, b, ((1,), (1,)), prec)


def _tn(a, b, prec=None):
    return _dot(a, b, ((0,), (0,)), prec)


def _b(x):
    return x.astype(BF16)


def _mm3(fn, a, b):
    ah, bh = _b(a), _b(b)
    return fn(ah, bh) + fn(ah, _b(b - bh.astype(F32))) + fn(_b(a - ah.astype(F32)), bh)


def _iota(shape, dim):
    return lax.broadcasted_iota(jnp.int32, shape, dim)


def _rowsum8(x):
    r, d = x.shape
    return jnp.sum(x.reshape(r // 8, 8, d), axis=0)


def _rmsnorm_cast(x, g, name, tm=512):
    n, d = x.shape

    def body(x_ref, g_ref, o_ref):
        v = x_ref[...]
        r = lax.rsqrt(jnp.mean(v * v, axis=-1, keepdims=True) + EPS)
        o_ref[...] = (v * r * g_ref[...]).astype(BF16)

    return pl.pallas_call(
        body, name=name, grid=(n // tm,),
        in_specs=[pl.BlockSpec((tm, d), lambda i: (i, 0)), pl.BlockSpec((1, d), lambda i: (0, 0))],
        out_specs=pl.BlockSpec((tm, d), lambda i: (i, 0)), out_shape=S((n, d), BF16), compiler_params=_cp(("parallel",)),
    )(x, g)


def _rmsnorm_bwd(dh, x, g, resid, name, tm=512):
    n, d = x.shape
    has_res = resid is not None

    def body(*refs):
        if has_res:
            dh_ref, x_ref, g_ref, r_ref, dx_ref, dg_ref = refs
        else:
            dh_ref, x_ref, g_ref, dx_ref, dg_ref = refs
        v = x_ref[...]
        dhv = dh_ref[...].astype(F32)
        r = lax.rsqrt(jnp.mean(v * v, axis=-1, keepdims=True) + EPS)
        xh = v * r
        u = dhv * g_ref[...]
        dx = r * (u - xh * jnp.mean(u * xh, axis=-1, keepdims=True))
        if has_res:
            dx = dx + r_ref[...]
        dx_ref[...] = dx

        @pl.when(pl.program_id(0) == 0)
        def _():
            dg_ref[...] = jnp.zeros_like(dg_ref)

        dg_ref[...] += _rowsum8(dhv * xh)

    tile = pl.BlockSpec((tm, d), lambda i: (i, 0))
    ins = [tile, tile, pl.BlockSpec((1, d), lambda i: (0, 0))] + ([tile] if has_res else [])
    args = (dh, x, g) + ((resid,) if has_res else ())
    return pl.pallas_call(
        body, name=name, grid=(n // tm,), in_specs=ins,
        out_specs=[tile, pl.BlockSpec((8, d), lambda i: (0, 0))],
        out_shape=[S((n, d), F32), S((8, d), F32)], compiler_params=_cp(("arbitrary",)),
    )(*args)


def _mm_nn(a, b, out_dtype, name, tm, tn, b_col0=0, n_out=None):
    m, k = a.shape
    n_out = b.shape[1] if n_out is None else n_out
    jb = b_col0 // tn
    assert b_col0 % tn == 0 and n_out % tn == 0 and m % tm == 0

    def body(a_ref, b_ref, o_ref):
        o_ref[...] = _nn(a_ref[...].astype(BF16), b_ref[...].astype(BF16)).astype(out_dtype)

    return pl.pallas_call(
        body, name=name, grid=(m // tm, n_out // tn),
        in_specs=[pl.BlockSpec((tm, k), lambda i, j: (i, 0)), pl.BlockSpec((k, tn), lambda i, j: (0, j + jb))],
        out_specs=pl.BlockSpec((tm, tn), lambda i, j: (i, j)), out_shape=S((m, n_out), out_dtype),
        compiler_params=_cp(("parallel", "parallel")),
    )(a, b)


def _mm_nt(dy, w, name, tm, tr, w_col0=0, acc=None):
    m, r = dy.shape
    k = w.shape[0]
    jb = w_col0 // tr
    nr = r // tr
    assert w_col0 % tr == 0 and r % tr == 0 and m % tm == 0
    has_acc = acc is not None

    def body(*refs):
        if has_acc:
            dy_ref, w_ref, acc_ref, o_ref = refs
        else:
            dy_ref, w_ref, o_ref = refs
        part = _nt(dy_ref[...].astype(BF16), w_ref[...].astype(BF16))

        @pl.when(pl.program_id(1) == 0)
        def _():
            o_ref[...] = part + acc_ref[...] if has_acc else part

        @pl.when(pl.program_id(1) > 0)
        def _():
            o_ref[...] += part

    out_tile = pl.BlockSpec((tm, k), lambda i, j: (i, 0))
    ins = [pl.BlockSpec((tm, tr), lambda i, j: (i, j)), pl.BlockSpec((k, tr), lambda i, j: (0, j + jb))]
    args = (dy, w)
    if has_acc:
        ins.append(out_tile)
        args = args + (acc,)
    return pl.pallas_call(
        body, name=name, grid=(m // tm, nr), in_specs=ins, out_specs=out_tile, out_shape=S((m, k), F32),
        input_output_aliases=({2: 0} if has_acc else {}), compiler_params=_cp(("parallel", "arbitrary")),
    )(*args)


def _mm_tn(x, dy, name, tm, tn):
    m, k = x.shape
    n = dy.shape[1]
    assert m % tm == 0 and n % tn == 0

    def body(x_ref, dy_ref, o_ref):
        part = _tn(x_ref[...].astype(BF16), dy_ref[...].astype(BF16))

        @pl.when(pl.program_id(1) == 0)
        def _():
            o_ref[...] = part

        @pl.when(pl.program_id(1) > 0)
        def _():
            o_ref[...] += part

    return pl.pallas_call(
        body, name=name, grid=(n // tn, m // tm),
        in_specs=[pl.BlockSpec((tm, k), lambda j, i: (i, 0)), pl.BlockSpec((tm, tn), lambda j, i: (i, j))],
        out_specs=pl.BlockSpec((k, tn), lambda j, i: (0, j)), out_shape=S((k, n), F32),
        compiler_params=_cp(("parallel", "arbitrary")),
    )(x, dy)


def _lower_bound(logits):
    e = jnp.exp(logits - jnp.max(logits, axis=0, keepdims=True))
    return e[0:1, :] / jnp.sum(e, axis=0, keepdims=True)


def _hg_gates(fl, lb):
    sig = jax.nn.sigmoid(fl)
    f = lb + (1.0 - lb) * sig
    k = (1.0 - lb) * (1.0 - sig)
    return sig, f, k, jnp.log(f)


def _silu_and_grad(x):
    s = jax.nn.sigmoid(x)
    return x * s, s * (1.0 + x * (1.0 - s))


def _hg_rowblocks(G):
    return [None] + [G[SUB * i - 1:SUB * i, :] for i in range(1, CHUNK // SUB)]


def _hg_intra_A(qs, k, G):
    refs = _hg_rowblocks(G)
    cols = _iota((SUB, CHUNK), 1)
    rows = _iota((SUB, CHUNK), 0)
    blocks = []
    for i in range(CHUNK // SUB):
        lo = SUB * i
        qb, Gb = qs[lo:lo + SUB, :], G[lo:lo + SUB, :]
        diag = jnp.zeros((SUB, CHUNK), F32)
        for s in range(SUB):
            e = jnp.exp(jnp.minimum(Gb - G[lo + s:lo + s + 1, :], 0.0))
            col = jnp.sum(qb * k[lo + s:lo + s + 1, :] * e, axis=-1, keepdims=True)
            diag = jnp.where(cols == lo + s, col, diag)
        a = jnp.where((cols >= lo) & (cols <= rows + lo), diag, 0.0)
        if i > 0:
            qr = qb * jnp.exp(Gb - refs[i])
            kr = k * jnp.exp(jnp.minimum(refs[i] - G, 0.0))
            a = jnp.where(cols < lo, _nt(_b(qr), _b(kr)), a)
        blocks.append(a)
    return jnp.concatenate(blocks, axis=0)


def _hg_intra_bwd(dA, qs, k, G):
    refs = _hg_rowblocks(G)
    cols = _iota((SUB, CHUNK), 1)
    rows16 = _iota((SUB, HG_D), 0)
    dk = jnp.zeros((CHUNK, HG_D), F32)
    dq_blocks, dk_diag_blocks = [], []
    for i in range(CHUNK // SUB):
        lo = SUB * i
        qb, Gb = qs[lo:lo + SUB, :], G[lo:lo + SUB, :]
        dAb = dA[lo:lo + SUB, :]
        dq = jnp.zeros((SUB, HG_D), F32)
        dkb = jnp.zeros((SUB, HG_D), F32)
        for s in range(SUB):
            e = jnp.exp(jnp.minimum(Gb - G[lo + s:lo + s + 1, :], 0.0))
            e = jnp.where(rows16 >= s, e, 0.0)
            dcol = jnp.sum(jnp.where(cols == lo + s, dAb, 0.0), axis=-1, keepdims=True)
            w = dcol * e
            dq = dq + w * k[lo + s:lo + s + 1, :]
            dkb = jnp.where(rows16 == s, jnp.sum(w * qb, axis=0, keepdims=True), dkb)
        if i > 0:
            e1 = jnp.exp(Gb - refs[i])
            e2 = jnp.exp(jnp.minimum(refs[i] - G, 0.0))
            dA_off = jnp.where(cols < lo, dAb, 0.0)
            dq = dq + _mm3(_nn, dA_off, k * e2) * e1
            dk = dk + _mm3(_tn, dA_off, qb * e1) * e2
        dq_blocks.append(dq)
        dk_diag_blocks.append(dkb)
    return jnp.concatenate(dq_blocks, axis=0), dk + jnp.concatenate(dk_diag_blocks, axis=0)


def _tri(n, upper=False):
    r, c = _iota((n, n), 0), _iota((n, n), 1)
    return jnp.where((c >= r) if upper else (r >= c), 1.0, 0.0).astype(BF16)


def _prefix_mm(tri, x):
    hi = x.astype(BF16)
    r1 = x - hi.astype(F32)
    mid = r1.astype(BF16)
    lo = (r1 - mid.astype(F32)).astype(BF16)
    return _nn(tri, hi) + _nn(tri, mid) + _nn(tri, lo)


def _hgrn_fwd(z, lb, gn, B, T):
    N = B * T
    NC = T // CHUNK

    def body(z_ref, lb_ref, gn_ref, y_ref, o_ref, st_ref, s_scr):
        lbs = _lower_bound(lb_ref[...])
        tri = _tri(CHUNK)
        s_scr[...] = jnp.zeros_like(s_scr)

        def chunk(c, carry):
            r = pl.ds(pl.multiple_of(c * CHUNK, CHUNK), CHUNK)
            for hh in range(HG_GROUP):
                zc, oc = 4 * LANE * hh, LANE * hh
                ql, fl, il, gl = (z_ref[r, zc + LANE * j:zc + LANE * (j + 1)] for j in range(4))
                _, _, k, logf = _hg_gates(fl, lbs[:, oc:oc + LANE])
                G = _prefix_mm(tri, logf)
                qs = ql * jax.nn.sigmoid(ql)
                st = s_scr[hh]
                st_ref[hh * NC + c] = st
                A = _hg_intra_A(qs, k, G)
                ib = _b(il)
                o = _nn(_b(A), ib) + _nt(_b(qs * jnp.exp(G)), _b(st))
                g_last = G[CHUNK - 1:CHUNK, :]
                s_scr[hh] = st * jnp.exp(g_last) + _mm3(_tn, il, k * jnp.exp(g_last - G))
                o_ref[r, oc:oc + LANE] = o
                rstd = lax.rsqrt(jnp.mean(o * o, axis=-1, keepdims=True) + EPS)
                y_ref[r, oc:oc + LANE] = (o * rstd * gn_ref[...] * (gl * jax.nn.sigmoid(gl))).astype(BF16)
            return carry

        lax.fori_loop(0, NC, chunk, 0)

    gw = HG_GROUP * LANE
    cb = C_HG // (4 * gw)
    ng = HG_H // HG_GROUP
    return pl.pallas_call(
        body, name="hgrn_fwd", grid=(B, ng),
        in_specs=[pl.BlockSpec((T, 4 * gw), lambda b, h: (b, cb + h)), pl.BlockSpec((lb.shape[0], gw), lambda b, h: (0, h)),
                  pl.BlockSpec((1, LANE), lambda b, h: (0, 0))],
        out_specs=[pl.BlockSpec((T, gw), lambda b, h: (b, h)), pl.BlockSpec((T, gw), lambda b, h: (b, h)),
                   pl.BlockSpec((HG_GROUP * NC, HG_D, HG_D), lambda b, h: (b * ng + h, 0, 0))],
        out_shape=[S((N, 512), BF16), S((N, 512), F32), S((B * HG_H * NC, HG_D, HG_D), F32)],
        scratch_shapes=[pltpu.VMEM((HG_GROUP, HG_D, HG_D), F32)], compiler_params=_cp(("parallel", "parallel")),
    )(z, lb, gn)


def _hgrn_bwd(z, o_raw, states, dy, lb, gn, B, T):
    N = B * T
    NC = T // CHUNK

    def body(z_ref, o_ref, st_ref, dy_ref, lb_ref, gn_ref, dz_ref, dlb_ref, dgn_ref, ds_scr, racc, dgn_acc):
        lbs = _lower_bound(lb_ref[...])
        gn_v = gn_ref[...]
        tri, triu = _tri(CHUNK), _tri(CHUNK, upper=True)
        cmask = _iota((CHUNK, CHUNK), 0) >= _iota((CHUNK, CHUNK), 1)
        for ref in (ds_scr, racc, dgn_acc, dlb_ref):
            ref[...] = jnp.zeros_like(ref)

        def chunk(ci, carry):
            c = NC - 1 - ci
            r = pl.ds(pl.multiple_of(c * CHUNK, CHUNK), CHUNK)
            for hh in range(HG_GROUP):
                zc, oc = 4 * LANE * hh, LANE * hh
                lb_v = lbs[:, oc:oc + LANE]
                ql, fl, il, gl = (z_ref[r, zc + LANE * j:zc + LANE * (j + 1)] for j in range(4))
                sig, f, k, logf = _hg_gates(fl, lb_v)
                G = _prefix_mm(tri, logf)
                qs, dsilu_q = _silu_and_grad(ql)
                gs, dsilu_g = _silu_and_grad(gl)
                o = o_ref[r, oc:oc + LANE]
                dyv = dy_ref[r, oc:oc + LANE]
                rstd = lax.rsqrt(jnp.mean(o * o, axis=-1, keepdims=True) + EPS)
                oh = o * rstd
                dgl = dyv * oh * gn_v * dsilu_g
                dn = dyv * gs
                dgn_acc[...] += _rowsum8(dn * oh)
                u = dn * gn_v
                do = rstd * (u - oh * jnp.mean(u * oh, axis=-1, keepdims=True))
                st = st_ref[hh * NC + c]
                dst = ds_scr[hh]
                eG = jnp.exp(G)
                g_last = G[CHUNK - 1:CHUNK, :]
                eL = jnp.exp(g_last - G)
                A = _hg_intra_A(qs, k, G)
                dA = jnp.where(cmask, _mm3(_nt, do, il), 0.0)
                di = _tn(_b(A), _b(do)) + _nt(_b(k * eL), _b(dst))
                dq_in, dk_in = _hg_intra_bwd(dA, qs, k, G)
                dq = dq_in + _mm3(_nn, do, st) * eG
                dk = dk_in + _mm3(_nn, il, dst) * eL
                ds_scr[hh] = dst * jnp.exp(g_last) + _mm3(_tn, do, qs * eG)
                dd = qs * dq - k * dk
                dlogf = _prefix_mm(triu, dd) + racc[hh]
                racc[hh] += jnp.sum(dd, axis=0, keepdims=True)
                df = dlogf / f - dk
                dlb_ref[8 * hh:8 * (hh + 1), :] += _rowsum8(df * (1.0 - sig))
                dz_ref[r, zc:zc + LANE] = (dq * dsilu_q).astype(BF16)
                dz_ref[r, zc + LANE:zc + 2 * LANE] = (df * (1.0 - lb_v) * sig * (1.0 - sig)).astype(BF16)
                dz_ref[r, zc + 2 * LANE:zc + 3 * LANE] = di.astype(BF16)
                dz_ref[r, zc + 3 * LANE:zc + 4 * LANE] = dgl.astype(BF16)
            return carry

        lax.fori_loop(0, NC, chunk, 0)
        dgn_ref[...] = dgn_acc[...]

    gw = HG_GROUP * LANE
    cb = C_HG // (4 * gw)
    ng = HG_H // HG_GROUP
    col = pl.BlockSpec((T, gw), lambda b, h: (b, h))
    return pl.pallas_call(
        body, name="hgrn_bwd", grid=(B, ng),
        in_specs=[pl.BlockSpec((T, 4 * gw), lambda b, h: (b, cb + h)), col,
                  pl.BlockSpec((HG_GROUP * NC, HG_D, HG_D), lambda b, h: (b * ng + h, 0, 0)), col,
                  pl.BlockSpec((lb.shape[0], gw), lambda b, h: (0, h)), pl.BlockSpec((1, LANE), lambda b, h: (0, 0))],
        out_specs=[pl.BlockSpec((T, 4 * gw), lambda b, h: (b, h)),
                   pl.BlockSpec((8 * HG_GROUP, LANE), lambda b, h: (b * ng + h, 0)),
                   pl.BlockSpec((8, LANE), lambda b, h: (b * ng + h, 0))],
        out_shape=[S((N, 2048), BF16), S((B * HG_H * 8, LANE), F32), S((B * ng * 8, LANE), F32)],
        scratch_shapes=[pltpu.VMEM((HG_GROUP, HG_D, HG_D), F32), pltpu.VMEM((HG_GROUP, 1, LANE), F32),
                        pltpu.VMEM((8, LANE), F32)],
        compiler_params=_cp(("parallel", "parallel")),
    )(z, o_raw, states, dy, lb, gn)


def _pair_mean(x, lo_half):
    a = jnp.sum(jnp.where(lo_half, x, 0.0), axis=-1, keepdims=True)
    b = jnp.sum(jnp.where(lo_half, 0.0, x), axis=-1, keepdims=True)
    return jnp.where(lo_half, a, b) * (1.0 / FOX_D)


def _fox_gate_fwd(z, bias, B, T):
    N = B * T
    tb = LANE

    def body(z_ref, b_ref, fc_ref, fct_ref):
        tri = _tri(tb)

        def step(i, carry):
            r = pl.ds(pl.multiple_of(i * tb, tb), tb)
            cs = _prefix_mm(tri, jax.nn.log_sigmoid(z_ref[r, :] + b_ref[...])) + carry
            fc_ref[r, :] = cs
            fct_ref[0, :, r] = cs.T[0:8, :]
            return cs[tb - 1:tb, :]

        lax.fori_loop(0, T // tb, step, jnp.zeros((1, LANE), F32))

    return pl.pallas_call(
        body, name="fox_gate_fwd", grid=(B,),
        in_specs=[pl.BlockSpec((T, LANE), lambda b: (b, C_FF // LANE)), pl.BlockSpec((1, LANE), lambda b: (0, 0))],
        out_specs=[pl.BlockSpec((T, LANE), lambda b: (b, 0)), pl.BlockSpec((1, 8, T), lambda b: (b, 0, 0))],
        out_shape=[S((N, LANE), F32), S((B, 8, T), F32)], compiler_params=_cp(("parallel",)),
    )(z, bias)


def _fox_gate_bwd(dfc, z, bias, B, T):
    N = B * T
    tb = LANE
    nt = T // tb

    def body(d_ref, z_ref, b_ref, dz_ref, db_ref):
        triu = _tri(tb, upper=True)
        db_ref[...] = jnp.zeros_like(db_ref)

        def step(ii, carry):
            r = pl.ds(pl.multiple_of((nt - 1 - ii) * tb, tb), tb)
            d = d_ref[r, 0:LANE]
            for p in range(1, FOX_P):
                d = d + d_ref[r, LANE * p:LANE * (p + 1)]
            rc = _prefix_mm(triu, d) + carry
            dff = rc * jax.nn.sigmoid(-(z_ref[r, :] + b_ref[...]))
            dz_ref[r, :] = dff.astype(BF16)
            db_ref[...] += _rowsum8(dff)
            return carry + jnp.sum(d, axis=0, keepdims=True)

        lax.fori_loop(0, nt, step, jnp.zeros((1, LANE), F32))

    return pl.pallas_call(
        body, name="fox_gate_bwd", grid=(B,),
        in_specs=[pl.BlockSpec((T, 512), lambda b: (b, 0)), pl.BlockSpec((T, LANE), lambda b: (b, C_FF // LANE)),
                  pl.BlockSpec((1, LANE), lambda b: (0, 0))],
        out_specs=[pl.BlockSpec((T, LANE), lambda b: (b, 0)), pl.BlockSpec((8, LANE), lambda b: (b, 0))],
        out_shape=[S((N, LANE), BF16), S((B * 8, LANE), F32)], compiler_params=_cp(("parallel",)),
    )(dfc, z, bias)


def _fox_prep(z_ref, gq, gk, r, lo_half):
    q, k, v = z_ref[r, 0:LANE], z_ref[r, LANE:2 * LANE], z_ref[r, 2 * LANE:3 * LANE]
    rq = lax.rsqrt(_pair_mean(q * q, lo_half) + EPS)
    rk = lax.rsqrt(_pair_mean(k * k, lo_half) + EPS)
    qh, kh = q * rq, k * rk
    return qh * gq * (FOX_D ** -0.5), kh * gk, v, qh, kh, rq, rk


def _fox_fwd(z, fc, fct, gq, gk, B, T, tq=512):
    N = B * T
    NQ = T // tq

    def body(z_ref, fc_ref, fct_ref, gq_ref, gk_ref, y_ref, lse_ref, qn_s, kn_s, v_s):
        p, qi = pl.program_id(1), pl.program_id(2)
        lo_half = _iota((1, LANE), 1) < FOX_D

        @pl.when(qi == 0)
        def _():
            def prep(i, carry):
                r = pl.ds(pl.multiple_of(i * tq, tq), tq)
                qn, kn, v = _fox_prep(z_ref, gq_ref[...], gk_ref[...], r, lo_half)[:3]
                qn_s[r, :], kn_s[r, :], v_s[r, :] = qn.astype(BF16), kn.astype(BF16), v.astype(BF16)
                return carry
            lax.fori_loop(0, NQ, prep, 0)

        rq = pl.ds(pl.multiple_of(qi * tq, tq), tq)
        qn = qn_s[rq, :]
        fcq = fc_ref[rq, :]
        lane = _iota((tq, LANE), 1)
        causal = _iota((tq, tq), 0) >= _iota((tq, tq), 1)
        qhs = [jnp.where(lo_half, qn, jnp.zeros_like(qn)), jnp.where(lo_half, jnp.zeros_like(qn), qn)]
        fqs = [jnp.sum(jnp.where(lane == 2 * p + hh, fcq, 0.0), axis=-1, keepdims=True) for hh in range(2)]

        def kv(j, carry, diagonal):
            rk = pl.ds(pl.multiple_of(j * tq, tq), tq)
            kj, vj = kn_s[rk, :], v_s[rk, :]
            new = []
            for hh in range(2):
                m, l, acc = carry[hh]
                s = _nt(qhs[hh], kj) + fqs[hh] - fct_ref[0, pl.ds(2 * p + hh, 1), rk]
                if diagonal:
                    s = jnp.where(causal, s, NEG)
                m_new = jnp.maximum(m, jnp.max(s, axis=-1, keepdims=True))
                pe = jnp.exp(s - m_new)
                alpha = jnp.exp(m - m_new)
                new.append((m_new, alpha * l + jnp.sum(pe, axis=-1, keepdims=True),
                            alpha * acc + _nn(pe.astype(BF16), vj)))
            return tuple(new)

        init = tuple((jnp.full((tq, 1), NEG, F32), jnp.zeros((tq, 1), F32), jnp.zeros((tq, LANE), F32)) for _ in range(2))
        carry = lax.fori_loop(0, qi, functools.partial(kv, diagonal=False), init)
        (m0, l0, a0), (m1, l1, a1) = kv(qi, carry, True)
        y_ref[...] = jnp.where(lo_half, a0 / l0, a1 / l1).astype(BF16)
        lse_ref[...] = jnp.where(lo_half, m0 + jnp.log(l0), m1 + jnp.log(l1))

    vec = pl.BlockSpec((1, LANE), lambda b, p, q: (0, 0))
    tile = pl.BlockSpec((tq, LANE), lambda b, p, q: (b * NQ + q, p))
    return pl.pallas_call(
        body, name="fox_fwd", grid=(B, FOX_P, NQ),
        in_specs=[pl.BlockSpec((T, 384), lambda b, p, q: (b, p)), pl.BlockSpec((T, LANE), lambda b, p, q: (b, 0)),
                  pl.BlockSpec((1, 8, T), lambda b, p, q: (b, 0, 0)), vec, vec],
        out_specs=[tile, tile], out_shape=[S((N, 512), BF16), S((N, 512), F32)],
        scratch_shapes=[pltpu.VMEM((T, LANE), BF16)] * 3,
        compiler_params=_cp(("parallel", "parallel", "arbitrary")),
    )(z, fc, fct, gq, gk)


def _fox_bwd(z, dy, y, lse, fc, fct, gq, gk, B, T, tq=512):
    N = B * T
    NQ = T // tq

    def body(z_ref, dy_ref, y_ref, lse_ref, fc_ref, fct_ref, gq_ref, gk_ref, dz_ref, dfc_ref, dgq_ref, dgk_ref,
             qn_s, kn_s, v_s, do_s, delta_s, dq_s, dfk_s):
        p, kj = pl.program_id(1), pl.program_id(2)
        lo_half = _iota((1, LANE), 1) < FOX_D
        lane = _iota((tq, LANE), 1)
        gq_v, gk_v = gq_ref[...], gk_ref[...]

        @pl.when(kj == 0)
        def _():
            def prep(i, carry):
                r = pl.ds(pl.multiple_of(i * tq, tq), tq)
                qn, kn, v = _fox_prep(z_ref, gq_v, gk_v, r, lo_half)[:3]
                qn_s[r, :], kn_s[r, :], v_s[r, :] = qn.astype(BF16), kn.astype(BF16), v.astype(BF16)
                do = dy_ref[r, :]
                do_s[r, :] = do.astype(BF16)
                delta_s[r, :] = _pair_mean(do * y_ref[r, :].astype(F32), lo_half) * float(FOX_D)
                return carry
            lax.fori_loop(0, NQ, prep, 0)
            dq_s[...] = jnp.zeros_like(dq_s)
            dgq_ref[...] = jnp.zeros_like(dgq_ref)
            dgk_ref[...] = jnp.zeros_like(dgk_ref)

        rk = pl.ds(pl.multiple_of(kj * tq, tq), tq)
        kn, vv = kn_s[rk, :], v_s[rk, :]
        causal = _iota((tq, tq), 0) >= _iota((tq, tq), 1)
        zero, one = jnp.zeros_like(kn), jnp.ones_like(kn)
        hms = [lo_half, jnp.logical_not(lo_half)]
        kmasks = [jnp.where(hm, kn, zero) for hm in hms]
        kaugs = [jnp.where(hm, kn, one) for hm in hms]
        vmasks = [jnp.where(hm, vv, zero) for hm in hms]
        fks = [fct_ref[0, pl.ds(2 * p + hh, 1), rk] for hh in range(2)]

        def qloop(i, carry, diagonal):
            ri = pl.ds(pl.multiple_of(i * tq, tq), tq)
            qn = qn_s[ri, :]
            do = do_s[ri, :]
            fcq = fc_ref[ri, :]
            new = []
            for hh in range(2):
                dk_acc, dv_acc = carry[hh]
                c0 = FOX_D * hh
                fq = jnp.sum(jnp.where(lane == 2 * p + hh, fcq, 0.0), axis=-1, keepdims=True)
                pr = jnp.exp(_nt(qn, kmasks[hh]) + fq - fks[hh] - lse_ref[ri, c0:c0 + 1])
                if diagonal:
                    pr = jnp.where(causal, pr, 0.0)
                ds = (pr * (_nt(do, vmasks[hh]) - delta_s[ri, c0:c0 + 1])).astype(BF16)
                dq_s[hh, ri, :] += _nn(ds, kaugs[hh])
                new.append((dk_acc + _tn(ds, jnp.where(hms[hh], qn, one)), dv_acc + _tn(pr.astype(BF16), do)))
            return tuple(new)

        init = tuple((jnp.zeros((tq, LANE), F32), jnp.zeros((tq, LANE), F32)) for _ in range(2))
        carry = qloop(kj, init, True)
        (dk0, dv0), (dk1, dv1) = lax.fori_loop(kj + 1, NQ, functools.partial(qloop, diagonal=False), carry)
        dks, dvs = [dk0, dk1], [dv0, dv1]

        dkn = jnp.where(lo_half, dks[0], dks[1])
        _, _, _, _, kh, _, rkk = _fox_prep(z_ref, gq_v, gk_v, rk, lo_half)
        u = dkn * gk_v
        dz_ref[rk, LANE:2 * LANE] = (rkk * (u - kh * _pair_mean(u * kh, lo_half))).astype(BF16)
        dz_ref[rk, 2 * LANE:3 * LANE] = jnp.where(lo_half, dvs[0], dvs[1]).astype(BF16)
        dgk_ref[...] += _rowsum8(dkn * kh)
        dfk_s[rk, :] = jnp.where(lane == 2 * p, -dks[0][:, FOX_D:FOX_D + 1],
                                 jnp.where(lane == 2 * p + 1, -dks[1][:, 0:1], 0.0))

        @pl.when(kj == NQ - 1)
        def _():
            def fin(i, carry):
                r = pl.ds(pl.multiple_of(i * tq, tq), tq)
                d0, d1 = dq_s[0, r, :], dq_s[1, r, :]
                dqn = jnp.where(lo_half, d0, d1)
                _, _, _, qh, _, rqq, _ = _fox_prep(z_ref, gq_v, gk_v, r, lo_half)
                u = dqn * gq_v * (FOX_D ** -0.5)
                dz_ref[r, 0:LANE] = (rqq * (u - qh * _pair_mean(u * qh, lo_half))).astype(BF16)
                dgq_ref[...] += _rowsum8(dqn * qh) * (FOX_D ** -0.5)
                dfc_ref[r, :] = dfk_s[r, :] + jnp.where(lane == 2 * p, d0[:, FOX_D:FOX_D + 1],
                                                        jnp.where(lane == 2 * p + 1, d1[:, 0:1], 0.0))
                return carry
            lax.fori_loop(0, NQ, fin, 0)

    vec = pl.BlockSpec((1, LANE), lambda b, p, k: (0, 0))
    col = pl.BlockSpec((T, LANE), lambda b, p, k: (b, p))
    part = pl.BlockSpec((8, LANE), lambda b, p, k: (b * FOX_P + p, 0))
    return pl.pallas_call(
        body, name="fox_bwd", grid=(B, FOX_P, NQ),
        in_specs=[pl.BlockSpec((T, 384), lambda b, p, k: (b, p)), col, col, col,
                  pl.BlockSpec((T, LANE), lambda b, p, k: (b, 0)), pl.BlockSpec((1, 8, T), lambda b, p, k: (b, 0, 0)),
                  vec, vec],
        out_specs=[pl.BlockSpec((T, 384), lambda b, p, k: (b, p)), col, part, part],
        out_shape=[S((N, 1536), BF16), S((N, 512), F32), S((B * FOX_P * 8, LANE), F32), S((B * FOX_P * 8, LANE), F32)],
        scratch_shapes=[pltpu.VMEM((T, LANE), BF16)] * 4 + [pltpu.VMEM((T, LANE), F32), pltpu.VMEM((2, T, LANE), F32),
                                                            pltpu.VMEM((T, LANE), F32)],
        compiler_params=_cp(("parallel", "parallel", "arbitrary")),
    )(z, dy, y, lse, fc, fct, gq, gk)


def _mem_scores(z_ref, kv_ref, gq, gk, h):
    c = slice(MEM_D * h, MEM_D * (h + 1))
    q, k = z_ref[:, c], kv_ref[:, c]
    rq = lax.rsqrt(jnp.mean(q * q, axis=-1, keepdims=True) + EPS)
    rk = lax.rsqrt(jnp.mean(k * k, axis=-1, keepdims=True) + EPS)
    qh, kh = q * rq, k * rk
    qn = (qh * gq * (MEM_D ** -0.5)).astype(BF16)
    kn = (kh * gk).astype(BF16)
    s = _nt(qn, kn)
    pe = jnp.exp(s - jnp.max(s, axis=-1, keepdims=True))
    pn = pe / jnp.sum(pe, axis=-1, keepdims=True)
    return pn, qn, kn, qh, kh, rq, rk


def _mem_fwd(z, memkv, gq, gk, B, T, M, tq=512):
    N = B * T
    NQ = T // tq
    W = MEM_H * MEM_D

    def body(z_ref, kv_ref, gq_ref, gk_ref, y_ref):
        for h in range(MEM_H):
            pn = _mem_scores(z_ref, kv_ref, gq_ref[...], gk_ref[...], h)[0]
            v = kv_ref[:, W + MEM_D * h:W + MEM_D * (h + 1)].astype(BF16)
            y_ref[:, MEM_D * h:MEM_D * (h + 1)] = _nn(pn.astype(BF16), v).astype(BF16)

    vec = pl.BlockSpec((1, LANE), lambda b, q: (0, 0))
    return pl.pallas_call(
        body, name="mem_fwd", grid=(B, NQ),
        in_specs=[pl.BlockSpec((tq, W), lambda b, q: (b * NQ + q, C_MQ // W)),
                  pl.BlockSpec((M, 2 * W), lambda b, q: (b, 0)), vec, vec],
        out_specs=pl.BlockSpec((tq, W), lambda b, q: (b * NQ + q, 0)), out_shape=S((N, W), BF16),
        compiler_params=_cp(("parallel", "parallel")),
    )(z, memkv, gq, gk)


def _mem_bwd(z, memkv, dy, gq, gk, B, T, M, tq=512):
    N = B * T
    NQ = T // tq
    W = MEM_H * MEM_D

    def body(z_ref, kv_ref, dy_ref, gq_ref, gk_ref, dz_ref, dkv_ref, dgq_ref, dgk_ref, acc):
        qi = pl.program_id(1)
        gq_v, gk_v = gq_ref[...], gk_ref[...]

        @pl.when(qi == 0)
        def _():
            acc[...] = jnp.zeros_like(acc)
            dgq_ref[...] = jnp.zeros_like(dgq_ref)
            dgk_ref[...] = jnp.zeros_like(dgk_ref)

        for h in range(MEM_H):
            c = slice(MEM_D * h, MEM_D * (h + 1))
            cv = slice(W + MEM_D * h, W + MEM_D * (h + 1))
            pn, qn, kn, qh, _, rq, _ = _mem_scores(z_ref, kv_ref, gq_v, gk_v, h)
            do = dy_ref[:, c].astype(BF16)
            dp = _nt(do, kv_ref[:, cv].astype(BF16))
            ds = (pn * (dp - jnp.sum(dp * pn, axis=-1, keepdims=True))).astype(BF16)
            dqn = _nn(ds, kn)
            acc[:, c] += _tn(ds, qn)
            acc[:, cv] += _tn(pn.astype(BF16), do)
            u = dqn * gq_v * (MEM_D ** -0.5)
            dz_ref[:, c] = (rq * (u - qh * jnp.mean(u * qh, axis=-1, keepdims=True))).astype(BF16)
            dgq_ref[...] += _rowsum8(dqn * qh) * (MEM_D ** -0.5)

        @pl.when(qi == NQ - 1)
        def _():
            for h in range(MEM_H):
                c = slice(MEM_D * h, MEM_D * (h + 1))
                cv = slice(W + MEM_D * h, W + MEM_D * (h + 1))
                k = kv_ref[:, c]
                rk = lax.rsqrt(jnp.mean(k * k, axis=-1, keepdims=True) + EPS)
                kh = k * rk
                dkn = acc[:, c]
                u = dkn * gk_v
                dkv_ref[:, c] = (rk * (u - kh * jnp.mean(u * kh, axis=-1, keepdims=True))).astype(BF16)
                dkv_ref[:, cv] = acc[:, cv].astype(BF16)
                dgk_ref[...] += _rowsum8(dkn * kh)

    vec = pl.BlockSpec((1, LANE), lambda b, q: (0, 0))
    part = pl.BlockSpec((8, LANE), lambda b, q: (b, 0))
    return pl.pallas_call(
        body, name="mem_bwd", grid=(B, NQ),
        in_specs=[pl.BlockSpec((tq, W), lambda b, q: (b * NQ + q, C_MQ // W)),
                  pl.BlockSpec((M, 2 * W), lambda b, q: (b, 0)), pl.BlockSpec((tq, W), lambda b, q: (b * NQ + q, 0)),
                  vec, vec],
        out_specs=[pl.BlockSpec((tq, W), lambda b, q: (b * NQ + q, 0)), pl.BlockSpec((M, 2 * W), lambda b, q: (b, 0)),
                   part, part],
        out_shape=[S((N, W), BF16), S((B * M, 2 * W), BF16), S((B * 8, LANE), F32), S((B * 8, LANE), F32)],
        scratch_shapes=[pltpu.VMEM((M, 2 * W), F32)], compiler_params=_cp(("parallel", "arbitrary")),
    )(z, memkv, dy, gq, gk)


def _merge_fwd(ya, yb, yc, z, x, wa, wb, wc, wo, tm=256):
    n, d = x.shape
    wdt = ya.shape[1]
    gb = C_GATE // d

    def body(ya_ref, yb_ref, yc_ref, g0_ref, g1_ref, g2_ref, x_ref, wa_ref, wb_ref, wc_ref, wo_ref,
             x1_ref, mg_ref, ua_ref, ub_ref, uc_ref):
        merged = jnp.zeros((tm, d), F32)
        for y_ref, g_ref, w_ref, u_ref in ((ya_ref, g0_ref, wa_ref, ua_ref), (yb_ref, g1_ref, wb_ref, ub_ref),
                                           (yc_ref, g2_ref, wc_ref, uc_ref)):
            u = _nn(y_ref[...], w_ref[...])
            u_ref[...] = u.astype(BF16)
            merged = merged + jax.nn.sigmoid(g_ref[...]) * u
        mb = merged.astype(BF16)
        mg_ref[...] = mb
        x1_ref[...] = x_ref[...] + _nn(mb, wo_ref[...])

    yt = pl.BlockSpec((tm, wdt), lambda i: (i, 0))
    xt = pl.BlockSpec((tm, d), lambda i: (i, 0))
    wbr = pl.BlockSpec((wdt, d), lambda i: (0, 0))
    gates = [pl.BlockSpec((tm, d), functools.partial(lambda i, k: (i, gb + k), k=k)) for k in range(3)]
    return pl.pallas_call(
        body, name="merge_fwd", grid=(n // tm,),
        in_specs=[yt, yt, yt] + gates + [xt, wbr, wbr, wbr, pl.BlockSpec((d, d), lambda i: (0, 0))],
        out_specs=[xt] * 5, out_shape=[S((n, d), F32)] + [S((n, d), BF16)] * 4, compiler_params=_cp(("parallel",)),
    )(ya, yb, yc, z, z, z, x, wa, wb, wc, wo)


def _merge_bwd(dx1, z, ua, ub, uc, wa, wb, wc, wo, tm=256):
    n, d = dx1.shape
    wdt = wa.shape[0]
    gb = C_GATE // d

    def body(dx_ref, g0_ref, g1_ref, g2_ref, ua_ref, ub_ref, uc_ref, wa_ref, wb_ref, wc_ref, wo_ref,
             dg_ref, dya_ref, dyb_ref, dyc_ref, dua_ref, dub_ref, duc_ref):
        dm = _nt(dx_ref[...].astype(BF16), wo_ref[...])
        for k, (g_ref, u_ref, w_ref, dy_ref, du_ref) in enumerate((
                (g0_ref, ua_ref, wa_ref, dya_ref, dua_ref), (g1_ref, ub_ref, wb_ref, dyb_ref, dub_ref),
                (g2_ref, uc_ref, wc_ref, dyc_ref, duc_ref))):
            g = jax.nn.sigmoid(g_ref[...])
            du = (dm * g).astype(BF16)
            du_ref[...] = du
            dg_ref[:, d * k:d * (k + 1)] = (dm * u_ref[...].astype(F32) * g * (1.0 - g)).astype(BF16)
            dy_ref[...] = _nt(du, w_ref[...])

    yt = pl.BlockSpec((tm, wdt), lambda i: (i, 0))
    xt = pl.BlockSpec((tm, d), lambda i: (i, 0))
    wbr = pl.BlockSpec((wdt, d), lambda i: (0, 0))
    gates = [pl.BlockSpec((tm, d), functools.partial(lambda i, k: (i, gb + k), k=k)) for k in range(3)]
    return pl.pallas_call(
        body, name="merge_bwd", grid=(n // tm,),
        in_specs=[xt] + gates + [xt, xt, xt, wbr, wbr, wbr, pl.BlockSpec((d, d), lambda i: (0, 0))],
        out_specs=[pl.BlockSpec((tm, 3 * d), lambda i: (i, 0)), yt, yt, yt, xt, xt, xt],
        out_shape=[S((n, 3 * d), BF16)] + [S((n, wdt), F32)] * 3 + [S((n, d), BF16)] * 3,
        compiler_params=_cp(("parallel",)),
    )(dx1, z, z, z, ua, ub, uc, wa, wb, wc, wo)


FFN_TN = 1408
INV_SQRT2 = 0.7071067811865476
INV_SQRT_2PI = 0.3989422804014327


def _conv_shifted(a, prev, first, tm):
    row = _iota(a.shape, 0)
    p7 = jnp.where(first, 0.0, prev[7:8, :])
    p6 = jnp.where(first, 0.0, prev[6:7, :])
    a1 = jnp.where(row == 0, p7, pltpu.roll(a, 1, 0))
    a2 = jnp.where(row == 0, p6, jnp.where(row == 1, p7, pltpu.roll(a, 2, 0)))
    return a1, a2


def _ffn_act_fwd(up, cw, cb, B, T, tm=256):
    N = B * T
    dff = cw.shape[1]
    NT, NJ, tn = T // tm, dff // FFN_TN, FFN_TN

    def body(a_ref, v_ref, cw_ref, cb_ref, y_ref, carry):
        t = pl.program_id(2)
        a = a_ref[...]
        a1, a2 = _conv_shifted(a, carry[...], t == 0, tm)
        w = cw_ref[...]
        ac = w[0:1, :] * a2 + w[1:2, :] * a1 + w[2:3, :] * a + cb_ref[...]
        y_ref[...] = (0.5 * ac * (1.0 + lax.erf(ac * INV_SQRT2)) * v_ref[...]).astype(BF16)
        carry[...] = a[tm - 8:tm, :]

    return pl.pallas_call(
        body, name="ffn_act_fwd", grid=(B, NJ, NT),
        in_specs=[pl.BlockSpec((tm, tn), lambda b, j, t: (b * NT + t, j)),
                  pl.BlockSpec((tm, tn), lambda b, j, t: (b * NT + t, NJ + j)),
                  pl.BlockSpec((3, tn), lambda b, j, t: (0, j)), pl.BlockSpec((1, tn), lambda b, j, t: (0, j))],
        out_specs=pl.BlockSpec((tm, tn), lambda b, j, t: (b * NT + t, j)), out_shape=S((N, dff), BF16),
        scratch_shapes=[pltpu.VMEM((8, tn), F32)], compiler_params=_cp(("parallel", "parallel", "arbitrary")),
    )(up, up, cw, cb)


def _ffn_down_loss(y, wd, x1, tgt, tm=256):
    n, d = x1.shape
    kf = y.shape[1]

    def body(y_ref, w_ref, x_ref, t_ref, dx_ref, ls_ref):
        err = x_ref[...] + _nn(y_ref[...], w_ref[...]) - t_ref[...]
        dx_ref[...] = err * (1.0 / d)

        @pl.when(pl.program_id(0) == 0)
        def _():
            ls_ref[...] = jnp.zeros_like(ls_ref)

        ls_ref[...] += _rowsum8(err * err) * (0.5 / d)

    xt = pl.BlockSpec((tm, d), lambda i: (i, 0))
    return pl.pallas_call(
        body, name="ffn_down_loss", grid=(n // tm,),
        in_specs=[pl.BlockSpec((tm, kf), lambda i: (i, 0)), pl.BlockSpec((kf, d), lambda i: (0, 0)), xt, xt],
        out_specs=[xt, pl.BlockSpec((8, d), lambda i: (0, 0))], out_shape=[S((n, d), F32), S((8, d), F32)],
        compiler_params=_cp(("arbitrary",)),
    )(y, wd, x1, tgt)


def _ffn_act_bwd1(dx2, wd, up, cw, cb, B, T, tm=256):
    N = B * T
    d = dx2.shape[1]
    dff = cw.shape[1]
    NT, NJ, tn = T // tm, dff // FFN_TN, FFN_TN

    def body(dx_ref, w_ref, a_ref, v_ref, cw_ref, cb_ref, dac_ref, dv_ref, dcw_ref, dcb_ref, carry):
        b, t = pl.program_id(1), pl.program_id(2)
        a = a_ref[...]
        a1, a2 = _conv_shifted(a, carry[...], t == 0, tm)
        carry[...] = a[tm - 8:tm, :]
        w = cw_ref[...]
        ac = w[0:1, :] * a2 + w[1:2, :] * a1 + w[2:3, :] * a + cb_ref[...]
        dy = _nt(dx_ref[...].astype(BF16), w_ref[...])
        cdf = 0.5 * (1.0 + lax.erf(ac * INV_SQRT2))
        dv_ref[...] = (dy * ac * cdf).astype(BF16)
        dac = dy * v_ref[...] * (cdf + ac * jnp.exp(-0.5 * ac * ac) * INV_SQRT_2PI)
        dac_ref[...] = dac

        @pl.when((b == 0) & (t == 0))
        def _():
            dcw_ref[...] = jnp.zeros_like(dcw_ref)
            dcb_ref[...] = jnp.zeros_like(dcb_ref)

        dcw_ref[0:8, :] += _rowsum8(dac * a2)
        dcw_ref[8:16, :] += _rowsum8(dac * a1)
        dcw_ref[16:24, :] += _rowsum8(dac * a)
        dcb_ref[...] += _rowsum8(dac)

    return pl.pallas_call(
        body, name="ffn_act_bwd1", grid=(NJ, B, NT),
        in_specs=[pl.BlockSpec((tm, d), lambda j, b, t: (b * NT + t, 0)), pl.BlockSpec((tn, d), lambda j, b, t: (j, 0)),
                  pl.BlockSpec((tm, tn), lambda j, b, t: (b * NT + t, j)),
                  pl.BlockSpec((tm, tn), lambda j, b, t: (b * NT + t, NJ + j)),
                  pl.BlockSpec((3, tn), lambda j, b, t: (0, j)), pl.BlockSpec((1, tn), lambda j, b, t: (0, j))],
        out_specs=[pl.BlockSpec((tm, tn), lambda j, b, t: (b * NT + t, j)),
                   pl.BlockSpec((tm, tn), lambda j, b, t: (b * NT + t, j)),
                   pl.BlockSpec((24, tn), lambda j, b, t: (0, j)), pl.BlockSpec((8, tn), lambda j, b, t: (0, j))],
        out_shape=[S((N, dff), F32), S((N, dff), BF16), S((24, dff), F32), S((8, dff), F32)],
        scratch_shapes=[pltpu.VMEM((8, tn), F32)], compiler_params=_cp(("parallel", "arbitrary", "arbitrary")),
    )(dx2, wd, up, up, cw, cb)


def _ffn_act_bwd2(dac, cw, B, T, tm=256):
    N = B * T
    dff = cw.shape[1]
    NT, NJ, tn = T // tm, dff // FFN_TN, FFN_TN
    last8 = N // 8 - 1

    def body(d_ref, nx_ref, cw_ref, da_ref):
        t = pl.program_id(2)
        dd = d_ref[...]
        row = _iota(dd.shape, 0)
        last = t == NT - 1
        n0 = jnp.where(last, 0.0, nx_ref[0:1, :])
        n1 = jnp.where(last, 0.0, nx_ref[1:2, :])
        d1 = jnp.where(row == tm - 1, n0, pltpu.roll(dd, tm - 1, 0))
        d2 = jnp.where(row == tm - 1, n1, jnp.where(row == tm - 2, n0, pltpu.roll(dd, tm - 2, 0)))
        w = cw_ref[...]
        da_ref[...] = (w[2:3, :] * dd + w[1:2, :] * d1 + w[0:1, :] * d2).astype(BF16)

    return pl.pallas_call(
        body, name="ffn_act_bwd2", grid=(B, NJ, NT),
        in_specs=[pl.BlockSpec((tm, tn), lambda b, j, t: (b * NT + t, j)),
                  pl.BlockSpec((8, tn), lambda b, j, t: (jnp.minimum((b * NT + t + 1) * (tm // 8), last8), j)),
                  pl.BlockSpec((3, tn), lambda b, j, t: (0, j))],
        out_specs=pl.BlockSpec((tm, tn), lambda b, j, t: (b * NT + t, j)), out_shape=S((N, dff), BF16),
        compiler_params=_cp(("parallel", "parallel", "parallel")),
    )(dac, dac, cw)


def _small_reduce(lbl, dg_mix, dg_mem, dlb_p, dgn_p, dfb_p, dgq_p, dgk_p, dmq_p, dmk_p, dg_ffn, dcb_p, loss_p, dcw_p):
    d, dff = dg_mix.shape[1], dcb_p.shape[1]
    nbh = dlb_p.shape[0] // (8 * HG_H)

    def colsum(ref):
        return jnp.sum(ref[...], axis=0, keepdims=True)

    def body(lbl_ref, mix_ref, mem_ref, dlb_ref, dgn_ref, dfb_ref, dgq_ref, dgk_ref, dmq_ref, dmk_ref, ffn_ref, dcb_ref,
             ls_ref, dcw_ref, o_mix, o_mem, o_lb, o_hgn, o_fb, o_fq, o_fk, o_mq, o_mk, o_ffn, o_cb, o_loss, o_cw):
        o_mix[...], o_mem[...], o_ffn[...], o_cb[...] = colsum(mix_ref), colsum(mem_ref), colsum(ffn_ref), colsum(dcb_ref)
        for j in range(3):
            o_cw[j:j + 1, :] = jnp.sum(dcw_ref[8 * j:8 * (j + 1), :], axis=0, keepdims=True)
        o_hgn[...], o_fb[...], o_mq[...], o_mk[...] = colsum(dgn_ref), colsum(dfb_ref), colsum(dmq_ref), colsum(dmk_ref)
        for src, dst in ((dgq_ref, o_fq), (dgk_ref, o_fk)):
            v = colsum(src)
            dst[...] = v + pltpu.roll(v, FOX_D, 1)
        o_loss[...] = jnp.zeros((1, LANE), F32) + jnp.sum(colsum(ls_ref), axis=-1, keepdims=True)
        logits = lbl_ref[...]
        e = jnp.exp(logits - jnp.max(logits, axis=0, keepdims=True))
        pr = e / jnp.sum(e, axis=0, keepdims=True)
        rows = _iota((8, LANE), 0)
        for h in range(HG_H):
            acc = jnp.zeros((8, LANE), F32)
            for b in range(nbh):
                acc = acc + dlb_ref[8 * (b * HG_H + h):8 * (b * HG_H + h + 1), :]
            dlb = jnp.sum(acc, axis=0, keepdims=True)
            c = slice(LANE * h, LANE * (h + 1))
            p0 = pr[0:1, c]
            first = _iota((logits.shape[0], LANE), 0) == 0
            o_lb[:, c] = pr[:, c] * (jnp.where(first, 1.0, 0.0) - p0) * dlb

    outs = [S((1, d), F32), S((1, d), F32), S(lbl.shape, F32)] + [S((1, LANE), F32)] * 6 + \
           [S((1, d), F32), S((1, dff), F32), S((1, LANE), F32), S((3, dff), F32)]
    return pl.pallas_call(body, name="small_reduce", out_shape=outs, compiler_params=_cp())(
        lbl, dg_mix, dg_mem, dlb_p, dgn_p, dfb_p, dgq_p, dgk_p, dmq_p, dmk_p, dg_ffn, dcb_p, loss_p, dcw_p)


def _in_col_pieces():
    hw, fw = HG_H * HG_D, FOX_H * FOX_D
    fox0, ff0 = 4 * hw, 4 * hw + 3 * fw
    mq0 = ff0 + FOX_H
    gate0 = mq0 + MEM_H * MEM_D
    pieces = []
    for p in range(FOX_P):
        pieces += [(fox0 + j * fw + LANE * p, LANE) for j in range(3)]
    pieces.append((mq0, MEM_H * MEM_D))
    for h in range(HG_H):
        pieces += [(j * hw + HG_D * h, HG_D) for j in range(4)]
    pieces.append((gate0, C_FF - C_GATE))
    pieces.append((ff0, FOX_H))
    return pieces


def _perm_cols(w):
    parts = [w[:, s:s + n] for s, n in _in_col_pieces()]
    parts.append(jnp.zeros((w.shape[0], C_END - C_FF - FOX_H), w.dtype))
    return jnp.concatenate(parts, axis=1)


def _unperm_cols(g):
    new_start, placed = 0, []
    for s, n in _in_col_pieces():
        placed.append((s, new_start, n))
        new_start += n
    return jnp.concatenate([g[:, ns:ns + n] for _, ns, n in sorted(placed)], axis=1)


def _local_step(x2, mem2, tgt, sm, W, B, T, M):
    fbias = jnp.pad(sm["fox_f_bias"], ((0, 0), (0, LANE - FOX_H)))
    gq2 = jnp.concatenate([sm["fox_q_norm_g"]] * 2, axis=1)
    gk2 = jnp.concatenate([sm["fox_k_norm_g"]] * 2, axis=1)
    lbl = sm["hgrn_lb_logits"]
    h = _rmsnorm_cast(x2, sm["norm_mix_g"], "norm_mix")
    z = _mm_nn(h, W["w_in"], F32, "proj_in", 512, 2432)
    memn = _rmsnorm_cast(mem2, sm["norm_mem_g"], "norm_mem", tm=256)
    memkv = _mm_nn(memn, W["mem_kv_w"], F32, "proj_memkv", 256, 512)
    ya, o_raw, states = _hgrn_fwd(z, lbl, sm["hgrn_norm_g"], B, T)
    fc, fct = _fox_gate_fwd(z, fbias, B, T)
    yb, lse = _fox_fwd(z, fc, fct, gq2, gk2, B, T)
    yc = _mem_fwd(z, memkv, sm["mem_q_norm_g"], sm["mem_k_norm_g"], B, T, M)
    x1, merged, ua, ub, uc = _merge_fwd(ya, yb, yc, z, x2, W["w_br_hgrn"], W["w_br_fox"], W["w_br_mem"], W["w_out"])
    h2 = _rmsnorm_cast(x1, sm["norm_ffn_g"], "norm_ffn")
    up = _mm_nn(h2, W["ffn_w_up"], F32, "ffn_up", 512, FFN_TN)
    yf = _ffn_act_fwd(up, W["ffn_conv_w"], sm["ffn_conv_b"], B, T)
    dx2, loss_p = _ffn_down_loss(yf, W["ffn_w_down"], x1, tgt)
    dff = W["ffn_conv_w"].shape[1]
    dac, dv, dcw_p, dcb_p = _ffn_act_bwd1(dx2, W["ffn_w_down"], up, W["ffn_conv_w"], sm["ffn_conv_b"], B, T)
    da = _ffn_act_bwd2(dac, W["ffn_conv_w"], B, T)
    g = {}
    g["ffn_w_down"] = _mm_tn(yf, dx2, "g_w_down", 512, 512)
    dh2 = _mm_nt(da, W["ffn_w_up"], "dh2_a", 512, FFN_TN)
    dh2 = _mm_nt(dv, W["ffn_w_up"], "dh2_v", 512, FFN_TN, w_col0=dff, acc=dh2)
    g["ffn_w_up"] = jnp.concatenate([_mm_tn(h2, da, "g_w_up_a", 512, FFN_TN), _mm_tn(h2, dv, "g_w_up_v", 512, FFN_TN)], axis=1)
    dx1, dg_ffn = _rmsnorm_bwd(dh2, x1, sm["norm_ffn_g"], dx2, "norm_ffn_bwd")
    g["w_out"] = _mm_tn(merged, dx1, "g_w_out", 512, 512)
    dgate, dya, dyb, dyc, dua, dub, duc = _merge_bwd(dx1, z, ua, ub, uc, W["w_br_hgrn"], W["w_br_fox"], W["w_br_mem"],
                                                    W["w_out"])
    g["w_br_hgrn"] = _mm_tn(ya, dua, "g_w_br_hgrn", 512, 512)
    g["w_br_fox"] = _mm_tn(yb, dub, "g_w_br_fox", 512, 512)
    g["w_br_mem"] = _mm_tn(yc, duc, "g_w_br_mem", 512, 512)
    dz_hg, dlb_p, dgn_p = _hgrn_bwd(z, o_raw, states, dya, lbl, sm["hgrn_norm_g"], B, T)
    dz_fox, dfc, dgq_p, dgk_p = _fox_bwd(z, dyb, yb, lse, fc, fct, gq2, gk2, B, T)
    dz_ff, dfb_p = _fox_gate_bwd(dfc, z, fbias, B, T)
    dz_mq, dkv, dmq_p, dmk_p = _mem_bwd(z, memkv, dyc, sm["mem_q_norm_g"], sm["mem_k_norm_g"], B, T, M)
    g["mem_kv_w"] = _mm_tn(memn, dkv, "g_mem_kv_w", 256, 512)
    dmemn = _mm_nt(dkv, W["mem_kv_w"], "d_memn", 256, 512)
    _, dg_mem = _rmsnorm_bwd(dmemn, mem2, sm["norm_mem_g"], None, "norm_mem_bwd", tm=256)
    segs = ((dz_fox, C_FOX, 512), (dz_mq, C_MQ, 512), (dz_hg, C_HG, 512), (dgate, C_GATE, 512), (dz_ff, C_FF, LANE))
    dh = None
    gw = []
    for i, (dzs, c0, tr) in enumerate(segs):
        dh = _mm_nt(dzs, W["w_in"], "dh_%d" % i, 512, tr, w_col0=c0, acc=dh)
        gw.append(_mm_tn(h, dzs, "g_w_in_%d" % i, 512, min(512, dzs.shape[1])))
    g["w_in"] = jnp.concatenate(gw, axis=1)
    grad_x, dg_mix = _rmsnorm_bwd(dh, x2, sm["norm_mix_g"], dx1, "norm_mix_bwd")
    small = _small_reduce(lbl, dg_mix, dg_mem, dlb_p, dgn_p, dfb_p, dgq_p, dgk_p, dmq_p, dmk_p, dg_ffn, dcb_p, loss_p,
                          dcw_p)
    names = ("norm_mix_g", "norm_mem_g", "hgrn_lb_logits", "hgrn_norm_g", "fox_f_bias", "fox_q_norm_g", "fox_k_norm_g",
             "mem_q_norm_g", "mem_k_norm_g", "norm_ffn_g", "ffn_conv_b", "loss", "ffn_conv_w")
    g.update(dict(zip(names, small)))
    return grad_x, g


ANY = pl.BlockSpec(memory_space=pl.ANY)


def _position():
    return lax.axis_index("x"), lax.axis_index("y"), lax.axis_index("c")


def _all_gather(block, name):
    def body(x_ref, out_ref, send_sems, recv_sems, local_sem):
        x, y, c = _position()
        me, sibling = (x, y, c), (x, y, 1 - c)
        chips = [(1 - x, y), (x, 1 - y), (1 - x, 1 - y)]

        def slot(px, py, pc):
            return out_ref.at[4 * px + 2 * py + pc]

        def copy(k, blk, to, src=None):
            return pltpu.make_async_remote_copy(
                src_ref=slot(*blk) if src is None else src, dst_ref=slot(*blk), send_sem=send_sems.at[k],
                recv_sem=recv_sems.at[k], device_id=to, device_id_type=MESH)

        mine = pltpu.make_async_copy(x_ref, slot(*me), local_sem)
        mine.start()
        first = [copy(0, me, sibling, src=x_ref)]
        first += [copy(1 + j, me, (*chip, c), src=x_ref) for j, chip in enumerate(chips)]
        for cp in first:
            cp.start()
        passed = [copy(4 + j, (*chip, c), sibling) for j, chip in enumerate(chips)]
        for j, chip in enumerate(chips):
            copy(1 + j, (*chip, c), me).wait_recv()
            passed[j].start()
        copy(0, sibling, me).wait_recv()
        for j, chip in enumerate(chips):
            copy(4 + j, (*chip, 1 - c), me).wait_recv()
        for cp in first + passed:
            cp.wait_send()
        mine.wait()

    return pl.pallas_call(
        body, name=name, out_shape=S((N_DEV,) + block.shape, block.dtype), in_specs=[ANY], out_specs=ANY,
        scratch_shapes=[pltpu.SemaphoreType.DMA((7,)), pltpu.SemaphoreType.DMA((7,)), pltpu.SemaphoreType.DMA],
    )(block)


def _swap_with_sibling(pk):
    _, r, l = pk.shape

    def body(pk_ref, out_ref, send_sems, recv_sems):
        x, y, c = _position()
        copies = [pltpu.make_async_remote_copy(
            src_ref=pk_ref.at[2 * k + 1 - c], dst_ref=out_ref.at[k], send_sem=send_sems.at[k], recv_sem=recv_sems.at[k],
            device_id=(x, y, 1 - c), device_id_type=MESH) for k in range(4)]
        for cp in copies:
            cp.start()
        for cp in copies:
            cp.wait()

    return pl.pallas_call(
        body, name="rs_sibling", out_shape=S((4, r, l), pk.dtype), in_specs=[ANY], out_specs=ANY,
        scratch_shapes=[pltpu.SemaphoreType.DMA((4,)), pltpu.SemaphoreType.DMA((4,))],
    )(pk)


def _swap_between_chips(pb):
    def body(pb_ref, out_ref, send_sems, recv_sems, local_sem):
        x, y, c = _position()
        me = 2 * x + y
        chips = [(1 - x, y), (x, 1 - y), (1 - x, 1 - y)]
        local = pltpu.make_async_copy(pb_ref.at[me], out_ref.at[me], local_sem)
        local.start()
        sends = [pltpu.make_async_remote_copy(
            src_ref=pb_ref.at[2 * cx + cy], dst_ref=out_ref.at[me], send_sem=send_sems.at[j], recv_sem=recv_sems.at[j],
            device_id=(cx, cy, c), device_id_type=MESH) for j, (cx, cy) in enumerate(chips)]
        for cp in sends:
            cp.start()
        for j, (cx, cy) in enumerate(chips):
            pltpu.make_async_remote_copy(
                src_ref=pb_ref.at[me], dst_ref=out_ref.at[2 * cx + cy], send_sem=send_sems.at[j],
                recv_sem=recv_sems.at[j], device_id=(cx, cy, c), device_id_type=MESH).wait_recv()
        for cp in sends:
            cp.wait_send()
        local.wait()

    return pl.pallas_call(
        body, name="rs_chips", out_shape=S(pb.shape, pb.dtype), in_specs=[ANY], out_specs=ANY,
        scratch_shapes=[pltpu.SemaphoreType.DMA((3,)), pltpu.SemaphoreType.DMA((3,)), pltpu.SemaphoreType.DMA],
    )(pb)


PACK_TR = 1024


def _pair_sum_cast(pk, recv, core):
    _, r, l = pk.shape

    def body(c_ref, a_ref, b_ref, o_ref):
        o_ref[...] = (a_ref[...] + b_ref[...]).astype(BF16)

    return pl.pallas_call(
        body, name="rs_pair_sum",
        grid_spec=pltpu.PrefetchScalarGridSpec(
            num_scalar_prefetch=1, grid=(4, r // PACK_TR),
            in_specs=[pl.BlockSpec((None, PACK_TR, l), lambda k, i, c: (2 * k + c[0], i, 0)),
                      pl.BlockSpec((None, PACK_TR, l), lambda k, i, c: (k, i, 0))],
            out_specs=pl.BlockSpec((None, PACK_TR, l), lambda k, i, c: (k, i, 0))),
        out_shape=S((4, r, l), BF16), compiler_params=_cp(("parallel", "parallel")),
    )(core, pk, recv)


def _final_sum(pk, recv_sib, recv_chips, slot, chip):
    _, r, l = pk.shape

    def body(s_ref, k_ref, a_ref, b_ref, rc_ref, o_ref):
        base = a_ref[...] + b_ref[...]
        acc = jnp.zeros_like(base)
        for j in range(4):
            acc = acc + jnp.where(k_ref[0] == j, base, rc_ref[j].astype(F32))
        o_ref[...] = acc

    return pl.pallas_call(
        body, name="rs_final_sum",
        grid_spec=pltpu.PrefetchScalarGridSpec(
            num_scalar_prefetch=2, grid=(r // PACK_TR,),
            in_specs=[pl.BlockSpec((None, PACK_TR, l), lambda i, s, k: (s[0], i, 0)),
                      pl.BlockSpec((None, PACK_TR, l), lambda i, s, k: (k[0], i, 0)),
                      pl.BlockSpec((4, PACK_TR, l), lambda i, s, k: (0, i, 0))],
            out_specs=pl.BlockSpec((PACK_TR, l), lambda i, s, k: (i, 0))),
        out_shape=S((r, l), F32), compiler_params=_cp(("parallel",)),
    )(slot, chip, pk, recv_sib, recv_chips)


def _adamw_math(w, g, m, v):
    m = ADAM_B1 * m + (1.0 - ADAM_B1) * g
    v = ADAM_B2 * v + (1.0 - ADAM_B2) * (g * g)
    m_hat = m / (1.0 - ADAM_B1 ** ADAM_STEP)
    v_hat = v / (1.0 - ADAM_B2 ** ADAM_STEP)
    return -ADAM_LR * (m_hat / (jnp.sqrt(v_hat) + ADAM_EPS) + ADAM_WD * w), m, v


def _adamw(w, g, m, v, name):
    r, c = w.shape
    tr = 256 if r % 256 == 0 else r

    def body(w_ref, g_ref, m_ref, v_ref, d_ref, nm_ref, nv_ref):
        d_ref[...], nm_ref[...], nv_ref[...] = _adamw_math(w_ref[...], g_ref[...], m_ref[...], v_ref[...])

    tile = pl.BlockSpec((tr, c), lambda i: (i, 0))
    return pl.pallas_call(
        body, name=name, grid=(r // tr,), in_specs=[tile] * 4, out_specs=[tile] * 3, out_shape=[S((r, c), F32)] * 3,
        compiler_params=_cp(("parallel",)),
    )(w, g, m, v)


def _small_update(gathered, w, m, v):
    def body(ga_ref, w_ref, m_ref, v_ref, g_ref, d_ref, nm_ref, nv_ref):
        g = ga_ref[0]
        for k in range(1, N_DEV):
            g = g + ga_ref[k]
        g_ref[...] = g
        d_ref[...], nm_ref[...], nv_ref[...] = _adamw_math(w_ref[...], g, m_ref[...], v_ref[...])

    return pl.pallas_call(body, name="small_update", out_shape=[S(w.shape, F32)] * 4, compiler_params=_cp())(
        gathered, w, m, v)


BIG = (("w_in", "col"), ("mem_kv_w", "row"), ("w_br_hgrn", "col"), ("w_br_fox", "col"), ("w_br_mem", "col"),
       ("w_out", "row"), ("ffn_w_up", "col"), ("ffn_conv_w", "col"), ("ffn_w_down", "row"))
SMALL = ("norm_mix_g", "norm_mem_g", "hgrn_lb_logits", "hgrn_norm_g", "fox_f_bias", "fox_q_norm_g", "fox_k_norm_g",
         "mem_q_norm_g", "mem_k_norm_g", "norm_ffn_g", "ffn_conv_b")


def _rows_of(n_elems):
    return -(-n_elems // LANE)


def _to_rows(a, lead=0):
    flat = a.reshape(a.shape[:lead] + (-1,))
    pad = (-flat.shape[-1]) % LANE
    if pad:
        flat = jnp.pad(flat, [(0, 0)] * lead + [(0, pad)])
    return flat.reshape(a.shape[:lead] + (-1, LANE))


def _stack_rows(parts, lead, total_rows):
    buf = jnp.concatenate(parts, axis=lead)
    pad = total_rows - buf.shape[lead]
    return jnp.pad(buf, [(0, 0)] * lead + [(0, pad), (0, 0)])


def _round_up(n, k):
    return -(-n // k) * k


def _from_rows(rows, shape, lead=0):
    n = math.prod(shape)
    return rows.reshape(rows.shape[:lead] + (-1,))[..., :n].reshape(rows.shape[:lead] + tuple(shape))


def _blocks_to_full(blocks, kind):
    n, a, b = blocks.shape
    return blocks.transpose(1, 0, 2).reshape(a, n * b) if kind == "col" else blocks.reshape(n * a, b)


def _full_to_blocks(full, kind):
    a, b = full.shape
    return full.reshape(a, N_DEV, b // N_DEV).transpose(1, 0, 2) if kind == "col" else full.reshape(N_DEV, a // N_DEV, b)


def kernel(x, mem, norm_mix_g, norm_mem_g, w_in, hgrn_lb_logits, hgrn_norm_g, fox_f_bias, fox_q_norm_g, fox_k_norm_g, mem_kv_w, mem_q_norm_g, mem_k_norm_g, w_br_hgrn, w_br_fox, w_br_mem, w_out, norm_ffn_g, ffn_w_up, ffn_conv_w, ffn_conv_b, ffn_w_down, loss_target, m_norm_mix_g, m_norm_mem_g, m_w_in, m_hgrn_lb_logits, m_hgrn_norm_g, m_fox_f_bias, m_fox_q_norm_g, m_fox_k_norm_g, m_mem_kv_w, m_mem_q_norm_g, m_mem_k_norm_g, m_w_br_hgrn, m_w_br_fox, m_w_br_mem, m_w_out, m_norm_ffn_g, m_ffn_w_up, m_ffn_conv_w, m_ffn_conv_b, m_ffn_w_down, v_norm_mix_g, v_norm_mem_g, v_w_in, v_hgrn_lb_logits, v_hgrn_norm_g, v_fox_f_bias, v_fox_q_norm_g, v_fox_k_norm_g, v_mem_kv_w, v_mem_q_norm_g, v_mem_k_norm_g, v_w_br_hgrn, v_w_br_fox, v_w_br_mem, v_w_out, v_norm_ffn_g, v_ffn_w_up, v_ffn_conv_w, v_ffn_conv_b, v_ffn_w_down):
    given = dict(locals())
    order = ("norm_mix_g", "norm_mem_g", "w_in", "hgrn_lb_logits", "hgrn_norm_g", "fox_f_bias", "fox_q_norm_g",
             "fox_k_norm_g", "mem_kv_w", "mem_q_norm_g", "mem_k_norm_g", "w_br_hgrn", "w_br_fox", "w_br_mem", "w_out",
             "norm_ffn_g", "ffn_w_up", "ffn_conv_w", "ffn_conv_b", "ffn_w_down")
    B, T, D = x.shape
    M = mem.shape[1]
    big_names = [n for n, _ in BIG]
    shard = {n: given[n][0] if n in big_names else given[n] for n in order}
    mom = {n: (given["m_" + n][0], given["v_" + n][0]) if n in big_names else (given["m_" + n], given["v_" + n])
           for n in order}
    shard["hgrn_lb_logits"] = hgrn_lb_logits
    for n in ("norm_mix_g", "norm_mem_g", "hgrn_norm_g", "fox_f_bias", "fox_q_norm_g", "fox_k_norm_g", "mem_q_norm_g",
              "mem_k_norm_g", "norm_ffn_g", "ffn_conv_b"):
        shard[n] = given[n].reshape(1, -1)

    parts, layout, row0 = [], {}, 0
    for n, _ in BIG:
        a = shard[n]
        a = lax.bitcast_convert_type(a, BF16) if n == "ffn_conv_w" else a.astype(BF16)
        rows = _to_rows(a)
        layout[n] = (row0, rows.shape[0])
        row0 += rows.shape[0]
        parts.append(rows)
    r_pack = _round_up(row0, PACK_TR)
    gathered = _all_gather(_stack_rows(parts, 0, r_pack), "ag_weights")
    W = {}
    for n, kind in BIG:
        r0, nr = layout[n]
        seg = gathered[:, r0:r0 + nr]
        if n == "ffn_conv_w":
            blocks = lax.bitcast_convert_type(_from_rows(seg, shard[n].shape + (2,), lead=1), F32)
        else:
            blocks = _from_rows(seg, shard[n].shape, lead=1)
        W[n] = _blocks_to_full(blocks, kind)
    W["w_in"] = _perm_cols(W["w_in"])

    sm = {n: shard[n] for n in SMALL}
    grad_x, g = _local_step(x.reshape(B * T, D), mem.reshape(B * M, D), loss_target.reshape(B * T, D), sm, W, B, T, M)
    g["w_in"] = _unperm_cols(g["w_in"])

    xi, yi, ci = _position()
    gparts, glayout, row0 = [], {}, 0
    for n, kind in BIG:
        rows = _to_rows(_full_to_blocks(g[n], kind), lead=1)
        glayout[n] = (row0, rows.shape[1])
        row0 += rows.shape[1]
        gparts.append(rows)
    pk = _stack_rows(gparts, 1, _round_up(row0, PACK_TR))
    core = ci.astype(jnp.int32).reshape(1)
    chip = (2 * xi + yi).astype(jnp.int32).reshape(1)
    recv_sib = _swap_with_sibling(pk)
    recv_chips = _swap_between_chips(_pair_sum_cast(pk, recv_sib, core))
    g_sum = _final_sum(pk, recv_sib, recv_chips, 2 * chip + core, chip)

    sg = {n: g[n] for n in SMALL}
    sg["fox_f_bias"] = g["fox_f_bias"][:, :FOX_H]
    sg["fox_q_norm_g"] = g["fox_q_norm_g"][:, :FOX_D]
    sg["fox_k_norm_g"] = g["fox_k_norm_g"][:, :FOX_D]
    slayout, row0 = {}, 0
    for n in SMALL:
        nr = _rows_of(shard[n].size)
        slayout[n] = (row0, nr)
        row0 += nr
    loss_row = row0
    r_small = _round_up(row0 + 1, 8)

    def pack_small(d, with_loss=None):
        rows = [_to_rows(d[n]) for n in SMALL]
        rows.append(with_loss if with_loss is not None else jnp.zeros((1, LANE), F32))
        return _stack_rows(rows, 0, r_small)

    sgath = _all_gather(pack_small(sg, g["loss"]), "ag_small")
    s_g, s_d, s_m, s_v = _small_update(sgath, pack_small(shard), pack_small({n: mom[n][0].reshape(shard[n].shape) for n in SMALL}),
                                       pack_small({n: mom[n][1].reshape(shard[n].shape) for n in SMALL}))
    loss = s_g[loss_row, 0]

    grads, deltas, new_m, new_v = {}, {}, {}, {}
    for n, _ in BIG:
        r0, nr = glayout[n]
        gn = _from_rows(g_sum[r0:r0 + nr], shard[n].shape)
        d, nm, nv = _adamw(shard[n], gn, mom[n][0], mom[n][1], "adamw_" + n)
        grads[n], deltas[n], new_m[n], new_v[n] = (a[None] for a in (gn, d, nm, nv))
    for n in SMALL:
        r0, nr = slayout[n]
        for dst, src in ((grads, s_g), (deltas, s_d), (new_m, s_m), (new_v, s_v)):
            dst[n] = _from_rows(src[r0:r0 + nr], given[n].shape)
    return (loss, grad_x.reshape(B, T, D), *[grads[n] for n in order], *[deltas[n] for n in order],
            *[new_m[n] for n in order], *[new_v[n] for n in order])
```

```python
import functools
import math

import jax
import jax.numpy as jnp
from jax import lax
from jax.experimental import pallas as pl
from jax.experimental.pallas import tpu as pltpu

F32, BF16 = jnp.float32, jnp.bfloat16
S = jax.ShapeDtypeStruct
MESH = pl.DeviceIdType.MESH

N_DEV = 8
EPS = 1e-6
LANE = 128
CHUNK = 64
SUB = 16
HG_H, HG_D = 4, 128
HG_GROUP = 2
FOX_H, FOX_D = 8, 64
FOX_P = FOX_H // 2
MEM_H, MEM_D = 4, 128
NEG = -1e30
VMEM_LIMIT = 56 * 2**20

ADAM_LR, ADAM_B1, ADAM_B2, ADAM_EPS, ADAM_WD, ADAM_STEP = 0.001, 0.9, 0.999, 1e-08, 0.01, 10

C_FOX, C_MQ, C_HG, C_GATE, C_FF, C_END = 0, 1536, 2048, 4096, 7168, 7296


def _cp(sem=None):
    return pltpu.CompilerParams(dimension_semantics=sem, vmem_limit_bytes=VMEM_LIMIT)


def _dot(a, b, dims, prec=None):
    return lax.dot_general(a, b, (dims, ((), ())), preferred_element_type=F32, precision=prec)


def _nn(a, b, prec=None):
    return _dot(a, b, ((1,), (0,)), prec)


def _nt(a, b, prec=None):
    return _dot(a, b, ((1,), (1,)), prec)


def _tn(a, b, prec=None):
    return _dot(a, b, ((0,), (0,)), prec)


def _b(x):
    return x.astype(BF16)


def _mm3(fn, a, b):
    ah, bh = _b(a), _b(b)
    return fn(ah, bh) + fn(ah, _b(b - bh.astype(F32))) + fn(_b(a - ah.astype(F32)), bh)


def _iota(shape, dim):
    return lax.broadcasted_iota(jnp.int32, shape, dim)


def _rowsum8(x):
    r, d = x.shape
    return jnp.sum(x.reshape(r // 8, 8, d), axis=0)


def _rmsnorm_cast(x, g, name, tm=512):
    n, d = x.shape

    def body(x_ref, g_ref, o_ref):
        v = x_ref[...]
        r = lax.rsqrt(jnp.mean(v * v, axis=-1, keepdims=True) + EPS)
        o_ref[...] = (v * r * g_ref[...]).astype(BF16)

    return pl.pallas_call(
        body, name=name, grid=(n // tm,),
        in_specs=[pl.BlockSpec((tm, d), lambda i: (i, 0)), pl.BlockSpec((1, d), lambda i: (0, 0))],
        out_specs=pl.BlockSpec((tm, d), lambda i: (i, 0)), out_shape=S((n, d), BF16), compiler_params=_cp(("parallel",)),
    )(x, g)


def _rmsnorm_bwd(dh, x, g, resid, name, tm=512):
    n, d = x.shape
    has_res = resid is not None

    def body(*refs):
        if has_res:
            dh_ref, x_ref, g_ref, r_ref, dx_ref, dg_ref = refs
        else:
            dh_ref, x_ref, g_ref, dx_ref, dg_ref = refs
        v = x_ref[...]
        dhv = dh_ref[...].astype(F32)
        r = lax.rsqrt(jnp.mean(v * v, axis=-1, keepdims=True) + EPS)
        xh = v * r
        u = dhv * g_ref[...]
        dx = r * (u - xh * jnp.mean(u * xh, axis=-1, keepdims=True))
        if has_res:
            dx = dx + r_ref[...]
        dx_ref[...] = dx

        @pl.when(pl.program_id(0) == 0)
        def _():
            dg_ref[...] = jnp.zeros_like(dg_ref)

        dg_ref[...] += _rowsum8(dhv * xh)

    tile = pl.BlockSpec((tm, d), lambda i: (i, 0))
    ins = [tile, tile, pl.BlockSpec((1, d), lambda i: (0, 0))] + ([tile] if has_res else [])
    args = (dh, x, g) + ((resid,) if has_res else ())
    return pl.pallas_call(
        body, name=name, grid=(n // tm,), in_specs=ins,
        out_specs=[tile, pl.BlockSpec((8, d), lambda i: (0, 0))],
        out_shape=[S((n, d), F32), S((8, d), F32)], compiler_params=_cp(("arbitrary",)),
    )(*args)


def _mm_nn(a, b, out_dtype, name, tm, tn, b_col0=0, n_out=None):
    m, k = a.shape
    n_out = b.shape[1] if n_out is None else n_out
    jb = b_col0 // tn
    assert b_col0 % tn == 0 and n_out % tn == 0 and m % tm == 0

    def body(a_ref, b_ref, o_ref):
        o_ref[...] = _nn(a_ref[...].astype(BF16), b_ref[...].astype(BF16)).astype(out_dtype)

    return pl.pallas_call(
        body, name=name, grid=(m // tm, n_out // tn),
        in_specs=[pl.BlockSpec((tm, k), lambda i, j: (i, 0)), pl.BlockSpec((k, tn), lambda i, j: (0, j + jb))],
        out_specs=pl.BlockSpec((tm, tn), lambda i, j: (i, j)), out_shape=S((m, n_out), out_dtype),
        compiler_params=_cp(("parallel", "parallel")),
    )(a, b)


def _mm_nt(dy, w, name, tm, tr, w_col0=0, acc=None):
    m, r = dy.shape
    k = w.shape[0]
    jb = w_col0 // tr
    nr = r // tr
    assert w_col0 % tr == 0 and r % tr == 0 and m % tm == 0
    has_acc = acc is not None

    def body(*refs):
        if has_acc:
            dy_ref, w_ref, acc_ref, o_ref = refs
        else:
            dy_ref, w_ref, o_ref = refs
        part = _nt(dy_ref[...].astype(BF16), w_ref[...].astype(BF16))

        @pl.when(pl.program_id(1) == 0)
        def _():
            o_ref[...] = part + acc_ref[...] if has_acc else part

        @pl.when(pl.program_id(1) > 0)
        def _():
            o_ref[...] += part

    out_tile = pl.BlockSpec((tm, k), lambda i, j: (i, 0))
    ins = [pl.BlockSpec((tm, tr), lambda i, j: (i, j)), pl.BlockSpec((k, tr), lambda i, j: (0, j + jb))]
    args = (dy, w)
    if has_acc:
        ins.append(out_tile)
        args = args + (acc,)
    return pl.pallas_call(
        body, name=name, grid=(m // tm, nr), in_specs=ins, out_specs=out_tile, out_shape=S((m, k), F32),
        input_output_aliases=({2: 0} if has_acc else {}), compiler_params=_cp(("parallel", "arbitrary")),
    )(*args)


def _mm_tn(x, dy, name, tm, tn):
    m, k = x.shape
    n = dy.shape[1]
    assert m % tm == 0 and n % tn == 0

    def body(x_ref, dy_ref, o_ref):
        part = _tn(x_ref[...].astype(BF16), dy_ref[...].astype(BF16))

        @pl.when(pl.program_id(1) == 0)
        def _():
            o_ref[...] = part

        @pl.when(pl.program_id(1) > 0)
        def _():
            o_ref[...] += part

    return pl.pallas_call(
        body, name=name, grid=(n // tn, m // tm),
        in_specs=[pl.BlockSpec((tm, k), lambda j, i: (i, 0)), pl.BlockSpec((tm, tn), lambda j, i: (i, j))],
        out_specs=pl.BlockSpec((k, tn), lambda j, i: (0, j)), out_shape=S((k, n), F32),
        compiler_params=_cp(("parallel", "arbitrary")),
    )(x, dy)


def _lower_bound(logits):
    e = jnp.exp(logits - jnp.max(logits, axis=0, keepdims=True))
    return e[0:1, :] / jnp.sum(e, axis=0, keepdims=True)


def _hg_gates(fl, lb):
    sig = jax.nn.sigmoid(fl)
    f = lb + (1.0 - lb) * sig
    k = (1.0 - lb) * (1.0 - sig)
    return sig, f, k, jnp.log(f)


def _silu_and_grad(x):
    s = jax.nn.sigmoid(x)
    return x * s, s * (1.0 + x * (1.0 - s))


def _hg_rowblocks(G):
    return [None] + [G[SUB * i - 1:SUB * i, :] for i in range(1, CHUNK // SUB)]


def _hg_intra_A(qs, k, G):
    refs = _hg_rowblocks(G)
    cols = _iota((SUB, CHUNK), 1)
    rows = _iota((SUB, CHUNK), 0)
    blocks = []
    for i in range(CHUNK // SUB):
        lo = SUB * i
        qb, Gb = qs[lo:lo + SUB, :], G[lo:lo + SUB, :]
        diag = jnp.zeros((SUB, CHUNK), F32)
        for s in range(SUB):
            e = jnp.exp(jnp.minimum(Gb - G[lo + s:lo + s + 1, :], 0.0))
            col = jnp.sum(qb * k[lo + s:lo + s + 1, :] * e, axis=-1, keepdims=True)
            diag = jnp.where(cols == lo + s, col, diag)
        a = jnp.where((cols >= lo) & (cols <= rows + lo), diag, 0.0)
        if i > 0:
            qr = qb * jnp.exp(Gb - refs[i])
            kr = k * jnp.exp(jnp.minimum(refs[i] - G, 0.0))
            a = jnp.where(cols < lo, _nt(_b(qr), _b(kr)), a)
        blocks.append(a)
    return jnp.concatenate(blocks, axis=0)


def _hg_intra_bwd(dA, qs, k, G):
    refs = _hg_rowblocks(G)
    cols = _iota((SUB, CHUNK), 1)
    rows16 = _iota((SUB, HG_D), 0)
    dk = jnp.zeros((CHUNK, HG_D), F32)
    dq_blocks, dk_diag_blocks = [], []
    for i in range(CHUNK // SUB):
        lo = SUB * i
        qb, Gb = qs[lo:lo + SUB, :], G[lo:lo + SUB, :]
        dAb = dA[lo:lo + SUB, :]
        dq = jnp.zeros((SUB, HG_D), F32)
        dkb = jnp.zeros((SUB, HG_D), F32)
        for s in range(SUB):
            e = jnp.exp(jnp.minimum(Gb - G[lo + s:lo + s + 1, :], 0.0))
            e = jnp.where(rows16 >= s, e, 0.0)
            dcol = jnp.sum(jnp.where(cols == lo + s, dAb, 0.0), axis=-1, keepdims=True)
            w = dcol * e
            dq = dq + w * k[lo + s:lo + s + 1, :]
            dkb = jnp.where(rows16 == s, jnp.sum(w * qb, axis=0, keepdims=True), dkb)
        if i > 0:
            e1 = jnp.exp(Gb - refs[i])
            e2 = jnp.exp(jnp.minimum(refs[i] - G, 0.0))
            dA_off = jnp.where(cols < lo, dAb, 0.0)
            dq = dq + _mm3(_nn, dA_off, k * e2) * e1
            dk = dk + _mm3(_tn, dA_off, qb * e1) * e2
        dq_blocks.append(dq)
        dk_diag_blocks.append(dkb)
    return jnp.concatenate(dq_blocks, axis=0), dk + jnp.concatenate(dk_diag_blocks, axis=0)


def _tri(n, upper=False):
    r, c = _iota((n, n), 0), _iota((n, n), 1)
    return jnp.where((c >= r) if upper else (r >= c), 1.0, 0.0).astype(BF16)


def _prefix_mm(tri, x):
    hi = x.astype(BF16)
    r1 = x - hi.astype(F32)
    mid = r1.astype(BF16)
    lo = (r1 - mid.astype(F32)).astype(BF16)
    return _nn(tri, hi) + _nn(tri, mid) + _nn(tri, lo)


def _hgrn_fwd(z, lb, gn, B, T):
    N = B * T
    NC = T // CHUNK

    def body(z_ref, lb_ref, gn_ref, y_ref, o_ref, st_ref, s_scr):
        lbs = _lower_bound(lb_ref[...])
        tri = _tri(CHUNK)
        s_scr[...] = jnp.zeros_like(s_scr)

        def chunk(c, carry):
            r = pl.ds(pl.multiple_of(c * CHUNK, CHUNK), CHUNK)
            for hh in range(HG_GROUP):
                zc, oc = 4 * LANE * hh, LANE * hh
                ql, fl, il, gl = (z_ref[r, zc + LANE * j:zc + LANE * (j + 1)] for j in range(4))
                _, _, k, logf = _hg_gates(fl, lbs[:, oc:oc + LANE])
                G = _prefix_mm(tri, logf)
                qs = ql * jax.nn.sigmoid(ql)
                st = s_scr[hh]
                st_ref[hh * NC + c] = st
                A = _hg_intra_A(qs, k, G)
                ib = _b(il)
                o = _nn(_b(A), ib) + _nt(_b(qs * jnp.exp(G)), _b(st))
                g_last = G[CHUNK - 1:CHUNK, :]
                s_scr[hh] = st * jnp.exp(g_last) + _mm3(_tn, il, k * jnp.exp(g_last - G))
                o_ref[r, oc:oc + LANE] = o
                rstd = lax.rsqrt(jnp.mean(o * o, axis=-1, keepdims=True) + EPS)
                y_ref[r, oc:oc + LANE] = (o * rstd * gn_ref[...] * (gl * jax.nn.sigmoid(gl))).astype(BF16)
            return carry

        lax.fori_loop(0, NC, chunk, 0)

    gw = HG_GROUP * LANE
    cb = C_HG // (4 * gw)
    ng = HG_H // HG_GROUP
    return pl.pallas_call(
        body, name="hgrn_fwd", grid=(B, ng),
        in_specs=[pl.BlockSpec((T, 4 * gw), lambda b, h: (b, cb + h)), pl.BlockSpec((lb.shape[0], gw), lambda b, h: (0, h)),
                  pl.BlockSpec((1, LANE), lambda b, h: (0, 0))],
        out_specs=[pl.BlockSpec((T, gw), lambda b, h: (b, h)), pl.BlockSpec((T, gw), lambda b, h: (b, h)),
                   pl.BlockSpec((HG_GROUP * NC, HG_D, HG_D), lambda b, h: (b * ng + h, 0, 0))],
        out_shape=[S((N, 512), BF16), S((N, 512), F32), S((B * HG_H * NC, HG_D, HG_D), F32)],
        scratch_shapes=[pltpu.VMEM((HG_GROUP, HG_D, HG_D), F32)], compiler_params=_cp(("parallel", "parallel")),
    )(z, lb, gn)


def _hgrn_bwd(z, o_raw, states, dy, lb, gn, B, T):
    N = B * T
    NC = T // CHUNK

    def body(z_ref, o_ref, st_ref, dy_ref, lb_ref, gn_ref, dz_ref, dlb_ref, dgn_ref, ds_scr, racc, dgn_acc):
        lbs = _lower_bound(lb_ref[...])
        gn_v = gn_ref[...]
        tri, triu = _tri(CHUNK), _tri(CHUNK, upper=True)
        cmask = _iota((CHUNK, CHUNK), 0) >= _iota((CHUNK, CHUNK), 1)
        for ref in (ds_scr, racc, dgn_acc, dlb_ref):
            ref[...] = jnp.zeros_like(ref)

        def chunk(ci, carry):
            c = NC - 1 - ci
            r = pl.ds(pl.multiple_of(c * CHUNK, CHUNK), CHUNK)
            for hh in range(HG_GROUP):
                zc, oc = 4 * LANE * hh, LANE * hh
                lb_v = lbs[:, oc:oc + LANE]
                ql, fl, il, gl = (z_ref[r, zc + LANE * j:zc + LANE * (j + 1)] for j in range(4))
                sig, f, k, logf = _hg_gates(fl, lb_v)
                G = _prefix_mm(tri, logf)
                qs, dsilu_q = _silu_and_grad(ql)
                gs, dsilu_g = _silu_and_grad(gl)
                o = o_ref[r, oc:oc + LANE]
                dyv = dy_ref[r, oc:oc + LANE]
                rstd = lax.rsqrt(jnp.mean(o * o, axis=-1, keepdims=True) + EPS)
                oh = o * rstd
                dgl = dyv * oh * gn_v * dsilu_g
                dn = dyv * gs
                dgn_acc[...] += _rowsum8(dn * oh)
                u = dn * gn_v
                do = rstd * (u - oh * jnp.mean(u * oh, axis=-1, keepdims=True))
                st = st_ref[hh * NC + c]
                dst = ds_scr[hh]
                eG = jnp.exp(G)
                g_last = G[CHUNK - 1:CHUNK, :]
                eL = jnp.exp(g_last - G)
                A = _hg_intra_A(qs, k, G)
                dA = jnp.where(cmask, _mm3(_nt, do, il), 0.0)
                di = _tn(_b(A), _b(do)) + _nt(_b(k * eL), _b(dst))
                dq_in, dk_in = _hg_intra_bwd(dA, qs, k, G)
                dq = dq_in + _mm3(_nn, do, st) * eG
                dk = dk_in + _mm3(_nn, il, dst) * eL
                ds_scr[hh] = dst * jnp.exp(g_last) + _mm3(_tn, do, qs * eG)
                dd = qs * dq - k * dk
                dlogf = _prefix_mm(triu, dd) + racc[hh]
                racc[hh] += jnp.sum(dd, axis=0, keepdims=True)
                df = dlogf / f - dk
                dlb_ref[8 * hh:8 * (hh + 1), :] += _rowsum8(df * (1.0 - sig))
                dz_ref[r, zc:zc + LANE] = (dq * dsilu_q).astype(BF16)
                dz_ref[r, zc + LANE:zc + 2 * LANE] = (df * (1.0 - lb_v) * sig * (1.0 - sig)).astype(BF16)
                dz_ref[r, zc + 2 * LANE:zc + 3 * LANE] = di.astype(BF16)
                dz_ref[r, zc + 3 * LANE:zc + 4 * LANE] = dgl.astype(BF16)
            return carry

        lax.fori_loop(0, NC, chunk, 0)
        dgn_ref[...] = dgn_acc[...]

    gw = HG_GROUP * LANE
    cb = C_HG // (4 * gw)
    ng = HG_H // HG_GROUP
    col = pl.BlockSpec((T, gw), lambda b, h: (b, h))
    return pl.pallas_call(
        body, name="hgrn_bwd", grid=(B, ng),
        in_specs=[pl.BlockSpec((T, 4 * gw), lambda b, h: (b, cb + h)), col,
                  pl.BlockSpec((HG_GROUP * NC, HG_D, HG_D), lambda b, h: (b * ng + h, 0, 0)), col,
                  pl.BlockSpec((lb.shape[0], gw), lambda b, h: (0, h)), pl.BlockSpec((1, LANE), lambda b, h: (0, 0))],
        out_specs=[pl.BlockSpec((T, 4 * gw), lambda b, h: (b, h)),
                   pl.BlockSpec((8 * HG_GROUP, LANE), lambda b, h: (b * ng + h, 0)),
                   pl.BlockSpec((8, LANE), lambda b, h: (b * ng + h, 0))],
        out_shape=[S((N, 2048), BF16), S((B * HG_H * 8, LANE), F32), S((B * ng * 8, LANE), F32)],
        scratch_shapes=[pltpu.VMEM((HG_GROUP, HG_D, HG_D), F32), pltpu.VMEM((HG_GROUP, 1, LANE), F32),
                        pltpu.VMEM((8, LANE), F32)],
        compiler_params=_cp(("parallel", "parallel")),
    )(z, o_raw, states, dy, lb, gn)


def _pair_mean(x, lo_half):
    a = jnp.sum(jnp.where(lo_half, x, 0.0), axis=-1, keepdims=True)
    b = jnp.sum(jnp.where(lo_half, 0.0, x), axis=-1, keepdims=True)
    return jnp.where(lo_half, a, b) * (1.0 / FOX_D)


def _fox_gate_fwd(z, bias, B, T):
    N = B * T
    tb = LANE

    def body(z_ref, b_ref, fc_ref, fct_ref):
        tri = _tri(tb)

        def step(i, carry):
            r = pl.ds(pl.multiple_of(i * tb, tb), tb)
            cs = _prefix_mm(tri, jax.nn.log_sigmoid(z_ref[r, :] + b_ref[...])) + carry
            fc_ref[r, :] = cs
            fct_ref[0, :, r] = cs.T[0:8, :]
            return cs[tb - 1:tb, :]

        lax.fori_loop(0, T // tb, step, jnp.zeros((1, LANE), F32))

    return pl.pallas_call(
        body, name="fox_gate_fwd", grid=(B,),
        in_specs=[pl.BlockSpec((T, LANE), lambda b: (b, C_FF // LANE)), pl.BlockSpec((1, LANE), lambda b: (0, 0))],
        out_specs=[pl.BlockSpec((T, LANE), lambda b: (b, 0)), pl.BlockSpec((1, 8, T), lambda b: (b, 0, 0))],
        out_shape=[S((N, LANE), F32), S((B, 8, T), F32)], compiler_params=_cp(("parallel",)),
    )(z, bias)


def _fox_gate_bwd(dfc, z, bias, B, T):
    N = B * T
    tb = LANE
    nt = T // tb

    def body(d_ref, z_ref, b_ref, dz_ref, db_ref):
        triu = _tri(tb, upper=True)
        db_ref[...] = jnp.zeros_like(db_ref)

        def step(ii, carry):
            r = pl.ds(pl.multiple_of((nt - 1 - ii) * tb, tb), tb)
            d = d_ref[r, 0:LANE]
            for p in range(1, FOX_P):
                d = d + d_ref[r, LANE * p:LANE * (p + 1)]
            rc = _prefix_mm(triu, d) + carry
            dff = rc * jax.nn.sigmoid(-(z_ref[r, :] + b_ref[...]))
            dz_ref[r, :] = dff.astype(BF16)
            db_ref[...] += _rowsum8(dff)
            return carry + jnp.sum(d, axis=0, keepdims=True)

        lax.fori_loop(0, nt, step, jnp.zeros((1, LANE), F32))

    return pl.pallas_call(
        body, name="fox_gate_bwd", grid=(B,),
        in_specs=[pl.BlockSpec((T, 512), lambda b: (b, 0)), pl.BlockSpec((T, LANE), lambda b: (b, C_FF // LANE)),
                  pl.BlockSpec((1, LANE), lambda b: (0, 0))],
        out_specs=[pl.BlockSpec((T, LANE), lambda b: (b, 0)), pl.BlockSpec((8, LANE), lambda b: (b, 0))],
        out_shape=[S((N, LANE), BF16), S((B * 8, LANE), F32)], compiler_params=_cp(("parallel",)),
    )(dfc, z, bias)


def _fox_prep(z_ref, gq, gk, r, lo_half):
    q, k, v = z_ref[r, 0:LANE], z_ref[r, LANE:2 * LANE], z_ref[r, 2 * LANE:3 * LANE]
    rq = lax.rsqrt(_pair_mean(q * q, lo_half) + EPS)
    rk = lax.rsqrt(_pair_mean(k * k, lo_half) + EPS)
    qh, kh = q * rq, k * rk
    return qh * gq * (FOX_D ** -0.5), kh * gk, v, qh, kh, rq, rk


def _fox_fwd(z, fc, fct, gq, gk, B, T, tq=512):
    N = B * T
    NQ = T // tq

    def body(z_ref, fc_ref, fct_ref, gq_ref, gk_ref, y_ref, lse_ref, qn_s, kn_s, v_s):
        p, qi = pl.program_id(1), pl.program_id(2)
        lo_half = _iota((1, LANE), 1) < FOX_D

        @pl.when(qi == 0)
        def _():
            def prep(i, carry):
                r = pl.ds(pl.multiple_of(i * tq, tq), tq)
                qn, kn, v = _fox_prep(z_ref, gq_ref[...], gk_ref[...], r, lo_half)[:3]
                qn_s[r, :], kn_s[r, :], v_s[r, :] = qn.astype(BF16), kn.astype(BF16), v.astype(BF16)
                return carry
            lax.fori_loop(0, NQ, prep, 0)

        rq = pl.ds(pl.multiple_of(qi * tq, tq), tq)
        qn = qn_s[rq, :]
        fcq = fc_ref[rq, :]
        lane = _iota((tq, LANE), 1)
        causal = _iota((tq, tq), 0) >= _iota((tq, tq), 1)
        qhs = [jnp.where(lo_half, qn, jnp.zeros_like(qn)), jnp.where(lo_half, jnp.zeros_like(qn), qn)]
        fqs = [jnp.sum(jnp.where(lane == 2 * p + hh, fcq, 0.0), axis=-1, keepdims=True) for hh in range(2)]

        def kv(j, carry, diagonal):
            rk = pl.ds(pl.multiple_of(j * tq, tq), tq)
            kj, vj = kn_s[rk, :], v_s[rk, :]
            new = []
            for hh in range(2):
                m, l, acc = carry[hh]
                s = _nt(qhs[hh], kj) + fqs[hh] - fct_ref[0, pl.ds(2 * p + hh, 1), rk]
                if diagonal:
                    s = jnp.where(causal, s, NEG)
                m_new = jnp.maximum(m, jnp.max(s, axis=-1, keepdims=True))
                pe = jnp.exp(s - m_new)
                alpha = jnp.exp(m - m_new)
                new.append((m_new, alpha * l + jnp.sum(pe, axis=-1, keepdims=True),
                            alpha * acc + _nn(pe.astype(BF16), vj)))
            return tuple(new)

        init = tuple((jnp.full((tq, 1), NEG, F32), jnp.zeros((tq, 1), F32), jnp.zeros((tq, LANE), F32)) for _ in range(2))
        carry = lax.fori_loop(0, qi, functools.partial(kv, diagonal=False), init)
        (m0, l0, a0), (m1, l1, a1) = kv(qi, carry, True)
        y_ref[...] = jnp.where(lo_half, a0 / l0, a1 / l1).astype(BF16)
        lse_ref[...] = jnp.where(lo_half, m0 + jnp.log(l0), m1 + jnp.log(l1))

    vec = pl.BlockSpec((1, LANE), lambda b, p, q: (0, 0))
    tile = pl.BlockSpec((tq, LANE), lambda b, p, q: (b * NQ + q, p))
    return pl.pallas_call(
        body, name="fox_fwd", grid=(B, FOX_P, NQ),
        in_specs=[pl.BlockSpec((T, 384), lambda b, p, q: (b, p)), pl.BlockSpec((T, LANE), lambda b, p, q: (b, 0)),
                  pl.BlockSpec((1, 8, T), lambda b, p, q: (b, 0, 0)), vec, vec],
        out_specs=[tile, tile], out_shape=[S((N, 512), BF16), S((N, 512), F32)],
        scratch_shapes=[pltpu.VMEM((T, LANE), BF16)] * 3,
        compiler_params=_cp(("parallel", "parallel", "arbitrary")),
    )(z, fc, fct, gq, gk)


def _fox_bwd(z, dy, y, lse, fc, fct, gq, gk, B, T, tq=512):
    N = B * T
    NQ = T // tq

    def body(z_ref, dy_ref, y_ref, lse_ref, fc_ref, fct_ref, gq_ref, gk_ref, dz_ref, dfc_ref, dgq_ref, dgk_ref,
             qn_s, kn_s, v_s, do_s, delta_s, dq_s, dfk_s):
        p, kj = pl.program_id(1), pl.program_id(2)
        lo_half = _iota((1, LANE), 1) < FOX_D
        lane = _iota((tq, LANE), 1)
        gq_v, gk_v = gq_ref[...], gk_ref[...]

        @pl.when(kj == 0)
        def _():
            def prep(i, carry):
                r = pl.ds(pl.multiple_of(i * tq, tq), tq)
                qn, kn, v = _fox_prep(z_ref, gq_v, gk_v, r, lo_half)[:3]
                qn_s[r, :], kn_s[r, :], v_s[r, :] = qn.astype(BF16), kn.astype(BF16), v.astype(BF16)
                do = dy_ref[r, :]
                do_s[r, :] = do.astype(BF16)
                delta_s[r, :] = _pair_mean(do * y_ref[r, :].astype(F32), lo_half) * float(FOX_D)
                return carry
            lax.fori_loop(0, NQ, prep, 0)
            dq_s[...] = jnp.zeros_like(dq_s)
            dgq_ref[...] = jnp.zeros_like(dgq_ref)
            dgk_ref[...] = jnp.zeros_like(dgk_ref)

        rk = pl.ds(pl.multiple_of(kj * tq, tq), tq)
        kn, vv = kn_s[rk, :], v_s[rk, :]
        causal = _iota((tq, tq), 0) >= _iota((tq, tq), 1)
        zero, one = jnp.zeros_like(kn), jnp.ones_like(kn)
        hms = [lo_half, jnp.logical_not(lo_half)]
        kmasks = [jnp.where(hm, kn, zero) for hm in hms]
        kaugs = [jnp.where(hm, kn, one) for hm in hms]
        vmasks = [jnp.where(hm, vv, zero) for hm in hms]
        fks = [fct_ref[0, pl.ds(2 * p + hh, 1), rk] for hh in range(2)]

        def qloop(i, carry, diagonal):
            ri = pl.ds(pl.multiple_of(i * tq, tq), tq)
            qn = qn_s[ri, :]
            do = do_s[ri, :]
            fcq = fc_ref[ri, :]
            new = []
            for hh in range(2):
                dk_acc, dv_acc = carry[hh]
                c0 = FOX_D * hh
                fq = jnp.sum(jnp.where(lane == 2 * p + hh, fcq, 0.0), axis=-1, keepdims=True)
                pr = jnp.exp(_nt(qn, kmasks[hh]) + fq - fks[hh] - lse_ref[ri, c0:c0 + 1])
                if diagonal:
                    pr = jnp.where(causal, pr, 0.0)
                ds = (pr * (_nt(do, vmasks[hh]) - delta_s[ri, c0:c0 + 1])).astype(BF16)
                dq_s[hh, ri, :] += _nn(ds, kaugs[hh])
                new.append((dk_acc + _tn(ds, jnp.where(hms[hh], qn, one)), dv_acc + _tn(pr.astype(BF16), do)))
            return tuple(new)

        init = tuple((jnp.zeros((tq, LANE), F32), jnp.zeros((tq, LANE), F32)) for _ in range(2))
        carry = qloop(kj, init, True)
        (dk0, dv0), (dk1, dv1) = lax.fori_loop(kj + 1, NQ, functools.partial(qloop, diagonal=False), carry)
        dks, dvs = [dk0, dk1], [dv0, dv1]

        dkn = jnp.where(lo_half, dks[0], dks[1])
        _, _, _, _, kh, _, rkk = _fox_prep(z_ref, gq_v, gk_v, rk, lo_half)
        u = dkn * gk_v
        dz_ref[rk, LANE:2 * LANE] = (rkk * (u - kh * _pair_mean(u * kh, lo_half))).astype(BF16)
        dz_ref[rk, 2 * LANE:3 * LANE] = jnp.where(lo_half, dvs[0], dvs[1]).astype(BF16)
        dgk_ref[...] += _rowsum8(dkn * kh)
        dfk_s[rk, :] = jnp.where(lane == 2 * p, -dks[0][:, FOX_D:FOX_D + 1],
                                 jnp.where(lane == 2 * p + 1, -dks[1][:, 0:1], 0.0))

        @pl.when(kj == NQ - 1)
        def _():
            def fin(i, carry):
                r = pl.ds(pl.multiple_of(i * tq, tq), tq)
                d0, d1 = dq_s[0, r, :], dq_s[1, r, :]
                dqn = jnp.where(lo_half, d0, d1)
                _, _, _, qh, _, rqq, _ = _fox_prep(z_ref, gq_v, gk_v, r, lo_half)
                u = dqn * gq_v * (FOX_D ** -0.5)
                dz_ref[r, 0:LANE] = (rqq * (u - qh * _pair_mean(u * qh, lo_half))).astype(BF16)
                dgq_ref[...] += _rowsum8(dqn * qh) * (FOX_D ** -0.5)
                dfc_ref[r, :] = dfk_s[r, :] + jnp.where(lane == 2 * p, d0[:, FOX_D:FOX_D + 1],
                                                        jnp.where(lane == 2 * p + 1, d1[:, 0:1], 0.0))
                return carry
            lax.fori_loop(0, NQ, fin, 0)

    vec = pl.BlockSpec((1, LANE), lambda b, p, k: (0, 0))
    col = pl.BlockSpec((T, LANE), lambda b, p, k: (b, p))
    part = pl.BlockSpec((8, LANE), lambda b, p, k: (b * FOX_P + p, 0))
    return pl.pallas_call(
        body, name="fox_bwd", grid=(B, FOX_P, NQ),
        in_specs=[pl.BlockSpec((T, 384), lambda b, p, k: (b, p)), col, col, col,
                  pl.BlockSpec((T, LANE), lambda b, p, k: (b, 0)), pl.BlockSpec((1, 8, T), lambda b, p, k: (b, 0, 0)),
                  vec, vec],
        out_specs=[pl.BlockSpec((T, 384), lambda b, p, k: (b, p)), col, part, part],
        out_shape=[S((N, 1536), BF16), S((N, 512), F32), S((B * FOX_P * 8, LANE), F32), S((B * FOX_P * 8, LANE), F32)],
        scratch_shapes=[pltpu.VMEM((T, LANE), BF16)] * 4 + [pltpu.VMEM((T, LANE), F32), pltpu.VMEM((2, T, LANE), F32),
                                                            pltpu.VMEM((T, LANE), F32)],
        compiler_params=_cp(("parallel", "parallel", "arbitrary")),
    )(z, dy, y, lse, fc, fct, gq, gk)


def _mem_scores(z_ref, kv_ref, gq, gk, h):
    c = slice(MEM_D * h, MEM_D * (h + 1))
    q, k = z_ref[:, c], kv_ref[:, c]
    rq = lax.rsqrt(jnp.mean(q * q, axis=-1, keepdims=True) + EPS)
    rk = lax.rsqrt(jnp.mean(k * k, axis=-1, keepdims=True) + EPS)
    qh, kh = q * rq, k * rk
    qn = (qh * gq * (MEM_D ** -0.5)).astype(BF16)
    kn = (kh * gk).astype(BF16)
    s = _nt(qn, kn)
    pe = jnp.exp(s - jnp.max(s, axis=-1, keepdims=True))
    pn = pe / jnp.sum(pe, axis=-1, keepdims=True)
    return pn, qn, kn, qh, kh, rq, rk


def _mem_fwd(z, memkv, gq, gk, B, T, M, tq=512):
    N = B * T
    NQ = T // tq
    W = MEM_H * MEM_D

    def body(z_ref, kv_ref, gq_ref, gk_ref, y_ref):
        for h in range(MEM_H):
            pn = _mem_scores(z_ref, kv_ref, gq_ref[...], gk_ref[...], h)[0]
            v = kv_ref[:, W + MEM_D * h:W + MEM_D * (h + 1)].astype(BF16)
            y_ref[:, MEM_D * h:MEM_D * (h + 1)] = _nn(pn.astype(BF16), v).astype(BF16)

    vec = pl.BlockSpec((1, LANE), lambda b, q: (0, 0))
    return pl.pallas_call(
        body, name="mem_fwd", grid=(B, NQ),
        in_specs=[pl.BlockSpec((tq, W), lambda b, q: (b * NQ + q, C_MQ // W)),
                  pl.BlockSpec((M, 2 * W), lambda b, q: (b, 0)), vec, vec],
        out_specs=pl.BlockSpec((tq, W), lambda b, q: (b * NQ + q, 0)), out_shape=S((N, W), BF16),
        compiler_params=_cp(("parallel", "parallel")),
    )(z, memkv, gq, gk)


def _mem_bwd(z, memkv, dy, gq, gk, B, T, M, tq=512):
    N = B * T
    NQ = T // tq
    W = MEM_H * MEM_D

    def body(z_ref, kv_ref, dy_ref, gq_ref, gk_ref, dz_ref, dkv_ref, dgq_ref, dgk_ref, acc):
        qi = pl.program_id(1)
        gq_v, gk_v = gq_ref[...], gk_ref[...]

        @pl.when(qi == 0)
        def _():
            acc[...] = jnp.zeros_like(acc)
            dgq_ref[...] = jnp.zeros_like(dgq_ref)
            dgk_ref[...] = jnp.zeros_like(dgk_ref)

        for h in range(MEM_H):
            c = slice(MEM_D * h, MEM_D * (h + 1))
            cv = slice(W + MEM_D * h, W + MEM_D * (h + 1))
            pn, qn, kn, qh, _, rq, _ = _mem_scores(z_ref, kv_ref, gq_v, gk_v, h)
            do = dy_ref[:, c].astype(BF16)
            dp = _nt(do, kv_ref[:, cv].astype(BF16))
            ds = (pn * (dp - jnp.sum(dp * pn, axis=-1, keepdims=True))).astype(BF16)
            dqn = _nn(ds, kn)
            acc[:, c] += _tn(ds, qn)
            acc[:, cv] += _tn(pn.astype(BF16), do)
            u = dqn * gq_v * (MEM_D ** -0.5)
            dz_ref[:, c] = (rq * (u - qh * jnp.mean(u * qh, axis=-1, keepdims=True))).astype(BF16)
            dgq_ref[...] += _rowsum8(dqn * qh) * (MEM_D ** -0.5)

        @pl.when(qi == NQ - 1)
        def _():
            for h in range(MEM_H):
                c = slice(MEM_D * h, MEM_D * (h + 1))
                cv = slice(W + MEM_D * h, W + MEM_D * (h + 1))
                k = kv_ref[:, c]
                rk = lax.rsqrt(jnp.mean(k * k, axis=-1, keepdims=True) + EPS)
                kh = k * rk
                dkn = acc[:, c]
                u = dkn * gk_v
                dkv_ref[:, c] = (rk * (u - kh * jnp.mean(u * kh, axis=-1, keepdims=True))).astype(BF16)
                dkv_ref[:, cv] = acc[:, cv].astype(BF16)
                dgk_ref[...] += _rowsum8(dkn * kh)

    vec = pl.BlockSpec((1, LANE), lambda b, q: (0, 0))
    part = pl.BlockSpec((8, LANE), lambda b, q: (b, 0))
    return pl.pallas_call(
        body, name="mem_bwd", grid=(B, NQ),
        in_specs=[pl.BlockSpec((tq, W), lambda b, q: (b * NQ + q, C_MQ // W)),
                  pl.BlockSpec((M, 2 * W), lambda b, q: (b, 0)), pl.BlockSpec((tq, W), lambda b, q: (b * NQ + q, 0)),
                  vec, vec],
        out_specs=[pl.BlockSpec((tq, W), lambda b, q: (b * NQ + q, 0)), pl.BlockSpec((M, 2 * W), lambda b, q: (b, 0)),
                   part, part],
        out_shape=[S((N, W), BF16), S((B * M, 2 * W), BF16), S((B * 8, LANE), F32), S((B * 8, LANE), F32)],
        scratch_shapes=[pltpu.VMEM((M, 2 * W), F32)], compiler_params=_cp(("parallel", "arbitrary")),
    )(z, memkv, dy, gq, gk)


def _merge_fwd(ya, yb, yc, z, x, wa, wb, wc, wo, tm=256):
    n, d = x.shape
    wdt = ya.shape[1]
    gb = C_GATE // d

    def body(ya_ref, yb_ref, yc_ref, g0_ref, g1_ref, g2_ref, x_ref, wa_ref, wb_ref, wc_ref, wo_ref,
             x1_ref, mg_ref, ua_ref, ub_ref, uc_ref):
        merged = jnp.zeros((tm, d), F32)
        for y_ref, g_ref, w_ref, u_ref in ((ya_ref, g0_ref, wa_ref, ua_ref), (yb_ref, g1_ref, wb_ref, ub_ref),
                                           (yc_ref, g2_ref, wc_ref, uc_ref)):
            u = _nn(y_ref[...], w_ref[...])
            u_ref[...] = u.astype(BF16)
            merged = merged + jax.nn.sigmoid(g_ref[...]) * u
        mb = merged.astype(BF16)
        mg_ref[...] = mb
        x1_ref[...] = x_ref[...] + _nn(mb, wo_ref[...])

    yt = pl.BlockSpec((tm, wdt), lambda i: (i, 0))
    xt = pl.BlockSpec((tm, d), lambda i: (i, 0))
    wbr = pl.BlockSpec((wdt, d), lambda i: (0, 0))
    gates = [pl.BlockSpec((tm, d), functools.partial(lambda i, k: (i, gb + k), k=k)) for k in range(3)]
    return pl.pallas_call(
        body, name="merge_fwd", grid=(n // tm,),
        in_specs=[yt, yt, yt] + gates + [xt, wbr, wbr, wbr, pl.BlockSpec((d, d), lambda i: (0, 0))],
        out_specs=[xt] * 5, out_shape=[S((n, d), F32)] + [S((n, d), BF16)] * 4, compiler_params=_cp(("parallel",)),
    )(ya, yb, yc, z, z, z, x, wa, wb, wc, wo)


def _merge_bwd(dx1, z, ua, ub, uc, wa, wb, wc, wo, tm=256):
    n, d = dx1.shape
    wdt = wa.shape[0]
    gb = C_GATE // d

    def body(dx_ref, g0_ref, g1_ref, g2_ref, ua_ref, ub_ref, uc_ref, wa_ref, wb_ref, wc_ref, wo_ref,
             dg_ref, dya_ref, dyb_ref, dyc_ref, dua_ref, dub_ref, duc_ref):
        dm = _nt(dx_ref[...].astype(BF16), wo_ref[...])
        for k, (g_ref, u_ref, w_ref, dy_ref, du_ref) in enumerate((
                (g0_ref, ua_ref, wa_ref, dya_ref, dua_ref), (g1_ref, ub_ref, wb_ref, dyb_ref, dub_ref),
                (g2_ref, uc_ref, wc_ref, dyc_ref, duc_ref))):
            g = jax.nn.sigmoid(g_ref[...])
            du = (dm * g).astype(BF16)
            du_ref[...] = du
            dg_ref[:, d * k:d * (k + 1)] = (dm * u_ref[...].astype(F32) * g * (1.0 - g)).astype(BF16)
            dy_ref[...] = _nt(du, w_ref[...])

    yt = pl.BlockSpec((tm, wdt), lambda i: (i, 0))
    xt = pl.BlockSpec((tm, d), lambda i: (i, 0))
    wbr = pl.BlockSpec((wdt, d), lambda i: (0, 0))
    gates = [pl.BlockSpec((tm, d), functools.partial(lambda i, k: (i, gb + k), k=k)) for k in range(3)]
    return pl.pallas_call(
        body, name="merge_bwd", grid=(n // tm,),
        in_specs=[xt] + gates + [xt, xt, xt, wbr, wbr, wbr, pl.BlockSpec((d, d), lambda i: (0, 0))],
        out_specs=[pl.BlockSpec((tm, 3 * d), lambda i: (i, 0)), yt, yt, yt, xt, xt, xt],
        out_shape=[S((n, 3 * d), BF16)] + [S((n, wdt), F32)] * 3 + [S((n, d), BF16)] * 3,
        compiler_params=_cp(("parallel",)),
    )(dx1, z, z, z, ua, ub, uc, wa, wb, wc, wo)


FFN_TN = 1408
INV_SQRT2 = 0.7071067811865476
INV_SQRT_2PI = 0.3989422804014327


def _conv_shifted(a, prev, first, tm):
    row = _iota(a.shape, 0)
    p7 = jnp.where(first, 0.0, prev[7:8, :])
    p6 = jnp.where(first, 0.0, prev[6:7, :])
    a1 = jnp.where(row == 0, p7, pltpu.roll(a, 1, 0))
    a2 = jnp.where(row == 0, p6, jnp.where(row == 1, p7, pltpu.roll(a, 2, 0)))
    return a1, a2


def _ffn_act_fwd(up, cw, cb, B, T, tm=256):
    N = B * T
    dff = cw.shape[1]
    NT, NJ, tn = T // tm, dff // FFN_TN, FFN_TN

    def body(a_ref, v_ref, cw_ref, cb_ref, y_ref, carry):
        t = pl.program_id(2)
        a = a_ref[...]
        a1, a2 = _conv_shifted(a, carry[...], t == 0, tm)
        w = cw_ref[...]
        ac = w[0:1, :] * a2 + w[1:2, :] * a1 + w[2:3, :] * a + cb_ref[...]
        y_ref[...] = (0.5 * ac * (1.0 + lax.erf(ac * INV_SQRT2)) * v_ref[...]).astype(BF16)
        carry[...] = a[tm - 8:tm, :]

    return pl.pallas_call(
        body, name="ffn_act_fwd", grid=(B, NJ, NT),
        in_specs=[pl.BlockSpec((tm, tn), lambda b, j, t: (b * NT + t, j)),
                  pl.BlockSpec((tm, tn), lambda b, j, t: (b * NT + t, NJ + j)),
                  pl.BlockSpec((3, tn), lambda b, j, t: (0, j)), pl.BlockSpec((1, tn), lambda b, j, t: (0, j))],
        out_specs=pl.BlockSpec((tm, tn), lambda b, j, t: (b * NT + t, j)), out_shape=S((N, dff), BF16),
        scratch_shapes=[pltpu.VMEM((8, tn), F32)], compiler_params=_cp(("parallel", "parallel", "arbitrary")),
    )(up, up, cw, cb)


def _ffn_down_loss(y, wd, x1, tgt, tm=256):
    n, d = x1.shape
    kf = y.shape[1]

    def body(y_ref, w_ref, x_ref, t_ref, dx_ref, ls_ref):
        err = x_ref[...] + _nn(y_ref[...], w_ref[...]) - t_ref[...]
        dx_ref[...] = err * (1.0 / d)

        @pl.when(pl.program_id(0) == 0)
        def _():
            ls_ref[...] = jnp.zeros_like(ls_ref)

        ls_ref[...] += _rowsum8(err * err) * (0.5 / d)

    xt = pl.BlockSpec((tm, d), lambda i: (i, 0))
    return pl.pallas_call(
        body, name="ffn_down_loss", grid=(n // tm,),
        in_specs=[pl.BlockSpec((tm, kf), lambda i: (i, 0)), pl.BlockSpec((kf, d), lambda i: (0, 0)), xt, xt],
        out_specs=[xt, pl.BlockSpec((8, d), lambda i: (0, 0))], out_shape=[S((n, d), F32), S((8, d), F32)],
        compiler_params=_cp(("arbitrary",)),
    )(y, wd, x1, tgt)


def _ffn_act_bwd1(dx2, wd, up, cw, cb, B, T, tm=256):
    N = B * T
    d = dx2.shape[1]
    dff = cw.shape[1]
    NT, NJ, tn = T // tm, dff // FFN_TN, FFN_TN

    def body(dx_ref, w_ref, a_ref, v_ref, cw_ref, cb_ref, dac_ref, dv_ref, dcw_ref, dcb_ref, carry):
        b, t = pl.program_id(1), pl.program_id(2)
        a = a_ref[...]
        a1, a2 = _conv_shifted(a, carry[...], t == 0, tm)
        carry[...] = a[tm - 8:tm, :]
        w = cw_ref[...]
        ac = w[0:1, :] * a2 + w[1:2, :] * a1 + w[2:3, :] * a + cb_ref[...]
        dy = _nt(dx_ref[...].astype(BF16), w_ref[...])
        cdf = 0.5 * (1.0 + lax.erf(ac * INV_SQRT2))
        dv_ref[...] = (dy * ac * cdf).astype(BF16)
        dac = dy * v_ref[...] * (cdf + ac * jnp.exp(-0.5 * ac * ac) * INV_SQRT_2PI)
        dac_ref[...] = dac

        @pl.when((b == 0) & (t == 0))
        def _():
            dcw_ref[...] = jnp.zeros_like(dcw_ref)
            dcb_ref[...] = jnp.zeros_like(dcb_ref)

        dcw_ref[0:8, :] += _rowsum8(dac * a2)
        dcw_ref[8:16, :] += _rowsum8(dac * a1)
        dcw_ref[16:24, :] += _rowsum8(dac * a)
        dcb_ref[...] += _rowsum8(dac)

    return pl.pallas_call(
        body, name="ffn_act_bwd1", grid=(NJ, B, NT),
        in_specs=[pl.BlockSpec((tm, d), lambda j, b, t: (b * NT + t, 0)), pl.BlockSpec((tn, d), lambda j, b, t: (j, 0)),
                  pl.BlockSpec((tm, tn), lambda j, b, t: (b * NT + t, j)),
                  pl.BlockSpec((tm, tn), lambda j, b, t: (b * NT + t, NJ + j)),
                  pl.BlockSpec((3, tn), lambda j, b, t: (0, j)), pl.BlockSpec((1, tn), lambda j, b, t: (0, j))],
        out_specs=[pl.BlockSpec((tm, tn), lambda j, b, t: (b * NT + t, j)),
                   pl.BlockSpec((tm, tn), lambda j, b, t: (b * NT + t, j)),
                   pl.BlockSpec((24, tn), lambda j, b, t: (0, j)), pl.BlockSpec((8, tn), lambda j, b, t: (0, j))],
        out_shape=[S((N, dff), F32), S((N, dff), BF16), S((24, dff), F32), S((8, dff), F32)],
        scratch_shapes=[pltpu.VMEM((8, tn), F32)], compiler_params=_cp(("parallel", "arbitrary", "arbitrary")),
    )(dx2, wd, up, up, cw, cb)


def _ffn_act_bwd2(dac, cw, B, T, tm=256):
    N = B * T
    dff = cw.shape[1]
    NT, NJ, tn = T // tm, dff // FFN_TN, FFN_TN
    last8 = N // 8 - 1

    def body(d_ref, nx_ref, cw_ref, da_ref):
        t = pl.program_id(2)
        dd = d_ref[...]
        row = _iota(dd.shape, 0)
        last = t == NT - 1
        n0 = jnp.where(last, 0.0, nx_ref[0:1, :])
        n1 = jnp.where(last, 0.0, nx_ref[1:2, :])
        d1 = jnp.where(row == tm - 1, n0, pltpu.roll(dd, tm - 1, 0))
        d2 = jnp.where(row == tm - 1, n1, jnp.where(row == tm - 2, n0, pltpu.roll(dd, tm - 2, 0)))
        w = cw_ref[...]
        da_ref[...] = (w[2:3, :] * dd + w[1:2, :] * d1 + w[0:1, :] * d2).astype(BF16)

    return pl.pallas_call(
        body, name="ffn_act_bwd2", grid=(B, NJ, NT),
        in_specs=[pl.BlockSpec((tm, tn), lambda b, j, t: (b * NT + t, j)),
                  pl.BlockSpec((8, tn), lambda b, j, t: (jnp.minimum((b * NT + t + 1) * (tm // 8), last8), j)),
                  pl.BlockSpec((3, tn), lambda b, j, t: (0, j))],
        out_specs=pl.BlockSpec((tm, tn), lambda b, j, t: (b * NT + t, j)), out_shape=S((N, dff), BF16),
        compiler_params=_cp(("parallel", "parallel", "parallel")),
    )(dac, dac, cw)


def _small_reduce(lbl, dg_mix, dg_mem, dlb_p, dgn_p, dfb_p, dgq_p, dgk_p, dmq_p, dmk_p, dg_ffn, dcb_p, loss_p, dcw_p):
    d, dff = dg_mix.shape[1], dcb_p.shape[1]
    nbh = dlb_p.shape[0] // (8 * HG_H)

    def colsum(ref):
        return jnp.sum(ref[...], axis=0, keepdims=True)

    def body(lbl_ref, mix_ref, mem_ref, dlb_ref, dgn_ref, dfb_ref, dgq_ref, dgk_ref, dmq_ref, dmk_ref, ffn_ref, dcb_ref,
             ls_ref, dcw_ref, o_mix, o_mem, o_lb, o_hgn, o_fb, o_fq, o_fk, o_mq, o_mk, o_ffn, o_cb, o_loss, o_cw):
        o_mix[...], o_mem[...], o_ffn[...], o_cb[...] = colsum(mix_ref), colsum(mem_ref), colsum(ffn_ref), colsum(dcb_ref)
        for j in range(3):
            o_cw[j:j + 1, :] = jnp.sum(dcw_ref[8 * j:8 * (j + 1), :], axis=0, keepdims=True)
        o_hgn[...], o_fb[...], o_mq[...], o_mk[...] = colsum(dgn_ref), colsum(dfb_ref), colsum(dmq_ref), colsum(dmk_ref)
        for src, dst in ((dgq_ref, o_fq), (dgk_ref, o_fk)):
            v = colsum(src)
            dst[...] = v + pltpu.roll(v, FOX_D, 1)
        o_loss[...] = jnp.zeros((1, LANE), F32) + jnp.sum(colsum(ls_ref), axis=-1, keepdims=True)
        logits = lbl_ref[...]
        e = jnp.exp(logits - jnp.max(logits, axis=0, keepdims=True))
        pr = e / jnp.sum(e, axis=0, keepdims=True)
        rows = _iota((8, LANE), 0)
        for h in range(HG_H):
            acc = jnp.zeros((8, LANE), F32)
            for b in range(nbh):
                acc = acc + dlb_ref[8 * (b * HG_H + h):8 * (b * HG_H + h + 1), :]
            dlb = jnp.sum(acc, axis=0, keepdims=True)
            c = slice(LANE * h, LANE * (h + 1))
            p0 = pr[0:1, c]
            first = _iota((logits.shape[0], LANE), 0) == 0
            o_lb[:, c] = pr[:, c] * (jnp.where(first, 1.0, 0.0) - p0) * dlb

    outs = [S((1, d), F32), S((1, d), F32), S(lbl.shape, F32)] + [S((1, LANE), F32)] * 6 + \
           [S((1, d), F32), S((1, dff), F32), S((1, LANE), F32), S((3, dff), F32)]
    return pl.pallas_call(body, name="small_reduce", out_shape=outs, compiler_params=_cp())(
        lbl, dg_mix, dg_mem, dlb_p, dgn_p, dfb_p, dgq_p, dgk_p, dmq_p, dmk_p, dg_ffn, dcb_p, loss_p, dcw_p)


def _in_col_pieces():
    hw, fw = HG_H * HG_D, FOX_H * FOX_D
    fox0, ff0 = 4 * hw, 4 * hw + 3 * fw
    mq0 = ff0 + FOX_H
    gate0 = mq0 + MEM_H * MEM_D
    pieces = []
    for p in range(FOX_P):
        pieces += [(fox0 + j * fw + LANE * p, LANE) for j in range(3)]
    pieces.append((mq0, MEM_H * MEM_D))
    for h in range(HG_H):
        pieces += [(j * hw + HG_D * h, HG_D) for j in range(4)]
    pieces.append((gate0, C_FF - C_GATE))
    pieces.append((ff0, FOX_H))
    return pieces


def _perm_cols(w):
    parts = [w[:, s:s + n] for s, n in _in_col_pieces()]
    parts.append(jnp.zeros((w.shape[0], C_END - C_FF - FOX_H), w.dtype))
    return jnp.concatenate(parts, axis=1)


def _unperm_cols(segs):
    starts = [0]
    for a in segs:
        starts.append(starts[-1] + a.shape[1])

    def piece(ns, n):
        i = max(j for j in range(len(segs)) if starts[j] <= ns)
        return segs[i][:, ns - starts[i]:ns - starts[i] + n]

    new_start, placed = 0, []
    for s, n in _in_col_pieces():
        placed.append((s, new_start, n))
        new_start += n
    return jnp.concatenate([piece(ns, n) for _, ns, n in sorted(placed)], axis=1)


def _local_step(x2, mem2, tgt, sm, W, B, T, M):
    fbias = jnp.pad(sm["fox_f_bias"], ((0, 0), (0, LANE - FOX_H)))
    gq2 = jnp.concatenate([sm["fox_q_norm_g"]] * 2, axis=1)
    gk2 = jnp.concatenate([sm["fox_k_norm_g"]] * 2, axis=1)
    lbl = sm["hgrn_lb_logits"]
    h = _rmsnorm_cast(x2, sm["norm_mix_g"], "norm_mix")
    z = _mm_nn(h, W["w_in"], F32, "proj_in", 512, 2432)
    memn = _rmsnorm_cast(mem2, sm["norm_mem_g"], "norm_mem", tm=256)
    memkv = _mm_nn(memn, W["mem_kv_w"], F32, "proj_memkv", 256, 512)
    ya, o_raw, states = _hgrn_fwd(z, lbl, sm["hgrn_norm_g"], B, T)
    fc, fct = _fox_gate_fwd(z, fbias, B, T)
    yb, lse = _fox_fwd(z, fc, fct, gq2, gk2, B, T)
    yc = _mem_fwd(z, memkv, sm["mem_q_norm_g"], sm["mem_k_norm_g"], B, T, M)
    x1, merged, ua, ub, uc = _merge_fwd(ya, yb, yc, z, x2, W["w_br_hgrn"], W["w_br_fox"], W["w_br_mem"], W["w_out"])
    h2 = _rmsnorm_cast(x1, sm["norm_ffn_g"], "norm_ffn")
    up = _mm_nn(h2, W["ffn_w_up"], F32, "ffn_up", 512, FFN_TN)
    yf = _ffn_act_fwd(up, W["ffn_conv_w"], sm["ffn_conv_b"], B, T)
    dx2, loss_p = _ffn_down_loss(yf, W["ffn_w_down"], x1, tgt)
    dff = W["ffn_conv_w"].shape[1]
    dac, dv, dcw_p, dcb_p = _ffn_act_bwd1(dx2, W["ffn_w_down"], up, W["ffn_conv_w"], sm["ffn_conv_b"], B, T)
    da = _ffn_act_bwd2(dac, W["ffn_conv_w"], B, T)
    g = {}
    g["ffn_w_down"] = _mm_tn(yf, dx2, "g_w_down", 512, 512)
    dh2 = _mm_nt(da, W["ffn_w_up"], "dh2_a", 512, FFN_TN)
    dh2 = _mm_nt(dv, W["ffn_w_up"], "dh2_v", 512, FFN_TN, w_col0=dff, acc=dh2)
    g["ffn_w_up"] = [_mm_tn(h2, da, "g_w_up_a", 512, FFN_TN), _mm_tn(h2, dv, "g_w_up_v", 512, FFN_TN)]
    dx1, dg_ffn = _rmsnorm_bwd(dh2, x1, sm["norm_ffn_g"], dx2, "norm_ffn_bwd")
    g["w_out"] = _mm_tn(merged, dx1, "g_w_out", 512, 512)
    dgate, dya, dyb, dyc, dua, dub, duc = _merge_bwd(dx1, z, ua, ub, uc, W["w_br_hgrn"], W["w_br_fox"], W["w_br_mem"],
                                                    W["w_out"])
    g["w_br_hgrn"] = _mm_tn(ya, dua, "g_w_br_hgrn", 512, 512)
    g["w_br_fox"] = _mm_tn(yb, dub, "g_w_br_fox", 512, 512)
    g["w_br_mem"] = _mm_tn(yc, duc, "g_w_br_mem", 512, 512)
    dz_hg, dlb_p, dgn_p = _hgrn_bwd(z, o_raw, states, dya, lbl, sm["hgrn_norm_g"], B, T)
    dz_fox, dfc, dgq_p, dgk_p = _fox_bwd(z, dyb, yb, lse, fc, fct, gq2, gk2, B, T)
    dz_ff, dfb_p = _fox_gate_bwd(dfc, z, fbias, B, T)
    dz_mq, dkv, dmq_p, dmk_p = _mem_bwd(z, memkv, dyc, sm["mem_q_norm_g"], sm["mem_k_norm_g"], B, T, M)
    g["mem_kv_w"] = _mm_tn(memn, dkv, "g_mem_kv_w", 256, 512)
    dmemn = _mm_nt(dkv, W["mem_kv_w"], "d_memn", 256, 512)
    _, dg_mem = _rmsnorm_bwd(dmemn, mem2, sm["norm_mem_g"], None, "norm_mem_bwd", tm=256)
    segs = ((dz_fox, C_FOX, 512), (dz_mq, C_MQ, 512), (dz_hg, C_HG, 512), (dgate, C_GATE, 512), (dz_ff, C_FF, LANE))
    dh = None
    gw = []
    for i, (dzs, c0, tr) in enumerate(segs):
        dh = _mm_nt(dzs, W["w_in"], "dh_%d" % i, 512, tr, w_col0=c0, acc=dh)
        gw.append(_mm_tn(h, dzs, "g_w_in_%d" % i, 512, min(512, dzs.shape[1])))
    g["w_in"] = gw
    grad_x, dg_mix = _rmsnorm_bwd(dh, x2, sm["norm_mix_g"], dx1, "norm_mix_bwd")
    small = _small_reduce(lbl, dg_mix, dg_mem, dlb_p, dgn_p, dfb_p, dgq_p, dgk_p, dmq_p, dmk_p, dg_ffn, dcb_p, loss_p,
                          dcw_p)
    names = ("norm_mix_g", "norm_mem_g", "hgrn_lb_logits", "hgrn_norm_g", "fox_f_bias", "fox_q_norm_g", "fox_k_norm_g",
             "mem_q_norm_g", "mem_k_norm_g", "norm_ffn_g", "ffn_conv_b", "loss", "ffn_conv_w")
    g.update(dict(zip(names, small)))
    return grad_x, g


ANY = pl.BlockSpec(memory_space=pl.ANY)


def _position():
    return lax.axis_index("x"), lax.axis_index("y"), lax.axis_index("c")


def _all_gather(blocks, name):
    nb = len(blocks)

    def body(*refs):
        x_refs, out_refs = refs[:nb], refs[nb:2 * nb]
        send_sems, recv_sems, local_sems = refs[2 * nb:]
        x, y, c = _position()
        me, sibling = (x, y, c), (x, y, 1 - c)
        chips = [(1 - x, y), (x, 1 - y), (1 - x, 1 - y)]

        def copy(i, k, blk, to, own=False):
            px, py, pc = blk
            slot = out_refs[i].at[4 * px + 2 * py + pc]
            return pltpu.make_async_remote_copy(
                src_ref=x_refs[i] if own else slot, dst_ref=slot, send_sem=send_sems.at[7 * i + k],
                recv_sem=recv_sems.at[7 * i + k], device_id=to, device_id_type=MESH)

        mine = [pltpu.make_async_copy(x_refs[i], out_refs[i].at[4 * x + 2 * y + c], local_sems.at[i]) for i in range(nb)]
        first = []
        for i in range(nb):
            mine[i].start()
            first.append(copy(i, 0, me, sibling, own=True))
            first += [copy(i, 1 + j, me, (*chip, c), own=True) for j, chip in enumerate(chips)]
        for cp in first:
            cp.start()
        passed = []
        for i in range(nb):
            for j, chip in enumerate(chips):
                copy(i, 1 + j, (*chip, c), me).wait_recv()
                passed.append(copy(i, 4 + j, (*chip, c), sibling))
                passed[-1].start()
        for i in range(nb):
            copy(i, 0, sibling, me).wait_recv()
            for j, chip in enumerate(chips):
                copy(i, 4 + j, (*chip, 1 - c), me).wait_recv()
        for cp in first + passed:
            cp.wait_send()
        for cp in mine:
            cp.wait()

    return pl.pallas_call(
        body, name=name, out_shape=[S((N_DEV,) + b.shape, b.dtype) for b in blocks], in_specs=[ANY] * nb,
        out_specs=[ANY] * nb,
        scratch_shapes=[pltpu.SemaphoreType.DMA((7 * nb,)), pltpu.SemaphoreType.DMA((7 * nb,)),
                        pltpu.SemaphoreType.DMA((nb,))],
    )(*blocks)


def _swap_with_sibling(pks):
    nb = len(pks)

    def body(*refs):
        pk_refs, out_refs = refs[:nb], refs[nb:2 * nb]
        send_sems, recv_sems = refs[2 * nb:]
        x, y, c = _position()
        copies = [pltpu.make_async_remote_copy(
            src_ref=pk_refs[i].at[2 * k + 1 - c], dst_ref=out_refs[i].at[k], send_sem=send_sems.at[4 * i + k],
            recv_sem=recv_sems.at[4 * i + k], device_id=(x, y, 1 - c), device_id_type=MESH)
            for i in range(nb) for k in range(4)]
        for cp in copies:
            cp.start()
        for cp in copies:
            cp.wait()

    return pl.pallas_call(
        body, name="rs_sibling", out_shape=[S((4,) + p.shape[1:], p.dtype) for p in pks], in_specs=[ANY] * nb,
        out_specs=[ANY] * nb, scratch_shapes=[pltpu.SemaphoreType.DMA((4 * nb,)), pltpu.SemaphoreType.DMA((4 * nb,))],
    )(*pks)


def _swap_between_chips(pbs):
    nb = len(pbs)

    def body(*refs):
        pb_refs, out_refs = refs[:nb], refs[nb:2 * nb]
        send_sems, recv_sems, local_sems = refs[2 * nb:]
        x, y, c = _position()
        me = 2 * x + y
        chips = [(1 - x, y), (x, 1 - y), (1 - x, 1 - y)]
        local = [pltpu.make_async_copy(pb_refs[i].at[me], out_refs[i].at[me], local_sems.at[i]) for i in range(nb)]
        for cp in local:
            cp.start()
        sends = [pltpu.make_async_remote_copy(
            src_ref=pb_refs[i].at[2 * cx + cy], dst_ref=out_refs[i].at[me], send_sem=send_sems.at[3 * i + j],
            recv_sem=recv_sems.at[3 * i + j], device_id=(cx, cy, c), device_id_type=MESH)
            for i in range(nb) for j, (cx, cy) in enumerate(chips)]
        for cp in sends:
            cp.start()
        for i in range(nb):
            for j, (cx, cy) in enumerate(chips):
                pltpu.make_async_remote_copy(
                    src_ref=pb_refs[i].at[me], dst_ref=out_refs[i].at[2 * cx + cy], send_sem=send_sems.at[3 * i + j],
                    recv_sem=recv_sems.at[3 * i + j], device_id=(cx, cy, c), device_id_type=MESH).wait_recv()
        for cp in sends:
            cp.wait_send()
        for cp in local:
            cp.wait()

    return pl.pallas_call(
        body, name="rs_chips", out_shape=[S(p.shape, p.dtype) for p in pbs], in_specs=[ANY] * nb, out_specs=[ANY] * nb,
        scratch_shapes=[pltpu.SemaphoreType.DMA((3 * nb,)), pltpu.SemaphoreType.DMA((3 * nb,)),
                        pltpu.SemaphoreType.DMA((nb,))],
    )(*pbs)


def _row_tile(r):
    return max(t for t in range(16, min(r, 512) + 1, 16) if r % t == 0)


def _pair_sum_cast(pk, recv, core, name):
    _, r, l = pk.shape
    tr = _row_tile(r)

    def body(c_ref, a_ref, b_ref, o_ref):
        o_ref[...] = (a_ref[...] + b_ref[...]).astype(BF16)

    return pl.pallas_call(
        body, name=name,
        grid_spec=pltpu.PrefetchScalarGridSpec(
            num_scalar_prefetch=1, grid=(4, r // tr),
            in_specs=[pl.BlockSpec((None, tr, l), lambda k, i, c: (2 * k + c[0], i, 0)),
                      pl.BlockSpec((None, tr, l), lambda k, i, c: (k, i, 0))],
            out_specs=pl.BlockSpec((None, tr, l), lambda k, i, c: (k, i, 0))),
        out_shape=S((4, r, l), BF16), compiler_params=_cp(("parallel", "parallel")),
    )(core, pk, recv)


def _final_sum(pk, recv_sib, recv_chips, slot, chip, name):
    _, r, l = pk.shape
    tr = _row_tile(r)

    def body(s_ref, k_ref, a_ref, b_ref, rc_ref, o_ref):
        base = a_ref[...] + b_ref[...]
        acc = jnp.zeros_like(base)
        for j in range(4):
            acc = acc + jnp.where(k_ref[0] == j, base, rc_ref[j].astype(F32))
        o_ref[...] = acc

    return pl.pallas_call(
        body, name=name,
        grid_spec=pltpu.PrefetchScalarGridSpec(
            num_scalar_prefetch=2, grid=(r // tr,),
            in_specs=[pl.BlockSpec((None, tr, l), lambda i, s, k: (s[0], i, 0)),
                      pl.BlockSpec((None, tr, l), lambda i, s, k: (k[0], i, 0)),
                      pl.BlockSpec((4, tr, l), lambda i, s, k: (0, i, 0))],
            out_specs=pl.BlockSpec((tr, l), lambda i, s, k: (i, 0))),
        out_shape=S((r, l), F32), compiler_params=_cp(("parallel",)),
    )(slot, chip, pk, recv_sib, recv_chips)


def _adamw_math(w, g, m, v):
    m = ADAM_B1 * m + (1.0 - ADAM_B1) * g
    v = ADAM_B2 * v + (1.0 - ADAM_B2) * (g * g)
    m_hat = m / (1.0 - ADAM_B1 ** ADAM_STEP)
    v_hat = v / (1.0 - ADAM_B2 ** ADAM_STEP)
    return -ADAM_LR * (m_hat / (jnp.sqrt(v_hat) + ADAM_EPS) + ADAM_WD * w), m, v


def _adamw(w, g, m, v, name):
    r, c = w.shape
    tr = 256 if r % 256 == 0 else r

    def body(w_ref, g_ref, m_ref, v_ref, d_ref, nm_ref, nv_ref):
        d_ref[...], nm_ref[...], nv_ref[...] = _adamw_math(w_ref[...], g_ref[...], m_ref[...], v_ref[...])

    tile = pl.BlockSpec((tr, c), lambda i: (i, 0))
    return pl.pallas_call(
        body, name=name, grid=(r // tr,), in_specs=[tile] * 4, out_specs=[tile] * 3, out_shape=[S((r, c), F32)] * 3,
        compiler_params=_cp(("parallel",)),
    )(w, g, m, v)


def _small_update(gathered, w, m, v):
    def body(ga_ref, w_ref, m_ref, v_ref, g_ref, d_ref, nm_ref, nv_ref):
        g = ga_ref[0]
        for k in range(1, N_DEV):
            g = g + ga_ref[k]
        g_ref[...] = g
        d_ref[...], nm_ref[...], nv_ref[...] = _adamw_math(w_ref[...], g, m_ref[...], v_ref[...])

    return pl.pallas_call(body, name="small_update", out_shape=[S(w.shape, F32)] * 4, compiler_params=_cp())(
        gathered, w, m, v)


BIG = ("w_in", "mem_kv_w", "w_br_hgrn", "w_br_fox", "w_br_mem", "w_out", "ffn_w_up", "ffn_conv_w", "ffn_w_down")
GROUP_ROWS = ("mem_kv_w", "w_out", "ffn_w_down")
GROUP_LANE = ("w_br_hgrn", "w_br_fox", "w_br_mem")
LANE_GROUP_ROWS = 224
SMALL = ("norm_mix_g", "norm_mem_g", "hgrn_lb_logits", "hgrn_norm_g", "fox_f_bias", "fox_q_norm_g", "fox_k_norm_g",
         "mem_q_norm_g", "mem_k_norm_g", "norm_ffn_g", "ffn_conv_b")


def _rows_of(n_elems):
    return -(-n_elems // LANE)


def _to_rows(a, lead=0):
    flat = a.reshape(a.shape[:lead] + (-1,))
    pad = (-flat.shape[-1]) % LANE
    if pad:
        flat = jnp.pad(flat, [(0, 0)] * lead + [(0, pad)])
    return flat.reshape(a.shape[:lead] + (-1, LANE))


def _stack_rows(parts, lead, total_rows):
    buf = jnp.concatenate(parts, axis=lead)
    pad = total_rows - buf.shape[lead]
    return jnp.pad(buf, [(0, 0)] * lead + [(0, pad), (0, 0)])


def _round_up(n, k):
    return -(-n // k) * k


def _from_rows(rows, shape, lead=0):
    n = math.prod(shape)
    return rows.reshape(rows.shape[:lead] + (-1,))[..., :n].reshape(rows.shape[:lead] + tuple(shape))


def _blocks_to_full(blocks, kind):
    n, a, b = blocks.shape
    return blocks.transpose(1, 0, 2).reshape(a, n * b) if kind == "col" else blocks.reshape(n * a, b)


def _full_to_blocks(full, kind, n=N_DEV):
    a, b = full.shape
    return full.reshape(a, n, b // n).transpose(1, 0, 2) if kind == "col" else full.reshape(n, a // n, b)


def _lane_group_rows(shard):
    n_lane = sum(shard[n].shape[0] for n in GROUP_LANE)
    n_cw = shard["ffn_conv_w"].size
    return n_lane, _rows_of(3 * n_cw), _rows_of(n_cw), _round_up(n_lane + _rows_of(3 * n_cw), LANE_GROUP_ROWS)


def _split_bf16x3(x):
    hi = x.astype(BF16)
    r1 = x - hi.astype(F32)
    mid = r1.astype(BF16)
    return jnp.stack([hi, mid, (r1 - mid.astype(F32)).astype(BF16)])


def _pack_weights(shard):
    r_lane = _lane_group_rows(shard)[3]
    lane_rows = [shard[n].astype(BF16) for n in GROUP_LANE] + [_to_rows(_split_bf16x3(shard["ffn_conv_w"]))]
    return [shard["w_in"].astype(BF16), jnp.concatenate([shard[n].astype(BF16) for n in GROUP_ROWS], axis=0),
            shard["ffn_w_up"].astype(BF16), _stack_rows(lane_rows, 0, r_lane)]


def _unpack_weights(gathered, shard):
    ga, gb, gc, gd = gathered
    n_lane, r_words, _, _ = _lane_group_rows(shard)
    W = {"w_in": _perm_cols(_blocks_to_full(ga, "col")), "ffn_w_up": _blocks_to_full(gc, "col")}
    r0 = 0
    for n in GROUP_ROWS:
        W[n] = _blocks_to_full(gb[:, r0:r0 + shard[n].shape[0]], "row")
        r0 += shard[n].shape[0]
    r0 = 0
    for n in GROUP_LANE:
        W[n] = _blocks_to_full(gd[:, r0:r0 + shard[n].shape[0]], "col")
        r0 += shard[n].shape[0]
    cw = _from_rows(gd[:, n_lane:n_lane + r_words], (3,) + shard["ffn_conv_w"].shape, lead=1).astype(F32)
    W["ffn_conv_w"] = _blocks_to_full(cw[:, 0] + cw[:, 1] + cw[:, 2], "col")
    return W


def _pack_grads(g, shard):
    r_lane = _lane_group_rows(shard)[3]
    cw_rows = _to_rows(_full_to_blocks(g["ffn_conv_w"], "col"), lead=1)
    return [_full_to_blocks(_unperm_cols(g["w_in"]), "col"),
            jnp.concatenate([_full_to_blocks(g[n], "row") for n in GROUP_ROWS], axis=1),
            jnp.concatenate([_full_to_blocks(h, "col", N_DEV // 2) for h in g["ffn_w_up"]], axis=0),
            _stack_rows([_full_to_blocks(g[n], "col") for n in GROUP_LANE] + [cw_rows], 1, r_lane)]


def _unpack_grads(sums, shard):
    n_lane, _, r_vals, _ = _lane_group_rows(shard)
    g_shard = {"w_in": sums[0], "ffn_w_up": sums[2]}
    r0 = 0
    for n in GROUP_ROWS:
        g_shard[n] = sums[1][r0:r0 + shard[n].shape[0]]
        r0 += shard[n].shape[0]
    r0 = 0
    for n in GROUP_LANE:
        g_shard[n] = sums[3][r0:r0 + shard[n].shape[0]]
        r0 += shard[n].shape[0]
    g_shard["ffn_conv_w"] = _from_rows(sums[3][n_lane:n_lane + r_vals], shard["ffn_conv_w"].shape)
    return g_shard


def kernel(x, mem, norm_mix_g, norm_mem_g, w_in, hgrn_lb_logits, hgrn_norm_g, fox_f_bias, fox_q_norm_g, fox_k_norm_g, mem_kv_w, mem_q_norm_g, mem_k_norm_g, w_br_hgrn, w_br_fox, w_br_mem, w_out, norm_ffn_g, ffn_w_up, ffn_conv_w, ffn_conv_b, ffn_w_down, loss_target, m_norm_mix_g, m_norm_mem_g, m_w_in, m_hgrn_lb_logits, m_hgrn_norm_g, m_fox_f_bias, m_fox_q_norm_g, m_fox_k_norm_g, m_mem_kv_w, m_mem_q_norm_g, m_mem_k_norm_g, m_w_br_hgrn, m_w_br_fox, m_w_br_mem, m_w_out, m_norm_ffn_g, m_ffn_w_up, m_ffn_conv_w, m_ffn_conv_b, m_ffn_w_down, v_norm_mix_g, v_norm_mem_g, v_w_in, v_hgrn_lb_logits, v_hgrn_norm_g, v_fox_f_bias, v_fox_q_norm_g, v_fox_k_norm_g, v_mem_kv_w, v_mem_q_norm_g, v_mem_k_norm_g, v_w_br_hgrn, v_w_br_fox, v_w_br_mem, v_w_out, v_norm_ffn_g, v_ffn_w_up, v_ffn_conv_w, v_ffn_conv_b, v_ffn_w_down):
    given = dict(locals())
    order = ("norm_mix_g", "norm_mem_g", "w_in", "hgrn_lb_logits", "hgrn_norm_g", "fox_f_bias", "fox_q_norm_g",
             "fox_k_norm_g", "mem_kv_w", "mem_q_norm_g", "mem_k_norm_g", "w_br_hgrn", "w_br_fox", "w_br_mem", "w_out",
             "norm_ffn_g", "ffn_w_up", "ffn_conv_w", "ffn_conv_b", "ffn_w_down")
    B, T, D = x.shape
    M = mem.shape[1]
    shard = {n: given[n][0] if n in BIG else given[n] for n in order}
    mom = {n: (given["m_" + n][0], given["v_" + n][0]) if n in BIG else (given["m_" + n], given["v_" + n])
           for n in order}
    shard["hgrn_lb_logits"] = hgrn_lb_logits
    for n in ("norm_mix_g", "norm_mem_g", "hgrn_norm_g", "fox_f_bias", "fox_q_norm_g", "fox_k_norm_g", "mem_q_norm_g",
              "mem_k_norm_g", "norm_ffn_g", "ffn_conv_b"):
        shard[n] = given[n].reshape(1, -1)

    W = _unpack_weights(_all_gather(_pack_weights(shard), "ag_weights"), shard)

    sm = {n: shard[n] for n in SMALL}
    grad_x, g = _local_step(x.reshape(B * T, D), mem.reshape(B * M, D), loss_target.reshape(B * T, D), sm, W, B, T, M)

    xi, yi, ci = _position()
    pks = _pack_grads(g, shard)
    core = ci.astype(jnp.int32).reshape(1)
    chip = (2 * xi + yi).astype(jnp.int32).reshape(1)
    recv_sib = _swap_with_sibling(pks)
    pair = [_pair_sum_cast(p, r, core, "rs_pair_sum_%d" % i) for i, (p, r) in enumerate(zip(pks, recv_sib))]
    recv_chips = _swap_between_chips(pair)
    g_shard = _unpack_grads([_final_sum(p, rs, rc, 2 * chip + core, chip, "rs_final_sum_%d" % i)
                             for i, (p, rs, rc) in enumerate(zip(pks, recv_sib, recv_chips))], shard)

    sg = {n: g[n] for n in SMALL}
    sg["fox_f_bias"] = g["fox_f_bias"][:, :FOX_H]
    sg["fox_q_norm_g"] = g["fox_q_norm_g"][:, :FOX_D]
    sg["fox_k_norm_g"] = g["fox_k_norm_g"][:, :FOX_D]
    slayout, row0 = {}, 0
    for n in SMALL:
        nr = _rows_of(shard[n].size)
        slayout[n] = (row0, nr)
        row0 += nr
    loss_row = row0
    r_small = _round_up(row0 + 1, 8)

    def pack_small(d, with_loss=None):
        rows = [_to_rows(d[n]) for n in SMALL]
        rows.append(with_loss if with_loss is not None else jnp.zeros((1, LANE), F32))
        return _stack_rows(rows, 0, r_small)

    sgath, = _all_gather([pack_small(sg, g["loss"])], "ag_small")
    s_g, s_d, s_m, s_v = _small_update(sgath, pack_small(shard), pack_small({n: mom[n][0].reshape(shard[n].shape) for n in SMALL}),
                                       pack_small({n: mom[n][1].reshape(shard[n].shape) for n in SMALL}))
    loss = s_g[loss_row, 0]

    grads, deltas, new_m, new_v = {}, {}, {}, {}
    for n in BIG:
        gn = g_shard[n]
        d, nm, nv = _adamw(shard[n], gn, mom[n][0], mom[n][1], "adamw_" + n)
        grads[n], deltas[n], new_m[n], new_v[n] = (a[None] for a in (gn, d, nm, nv))
    for n in SMALL:
        r0, nr = slayout[n]
        for dst, src in ((grads, s_g), (deltas, s_d), (new_m, s_m), (new_v, s_v)):
            dst[n] = _from_rows(src[r0:r0 + nr], given[n].shape)
    return (loss, grad_x.reshape(B, T, D), *[grads[n] for n in order], *[deltas[n] for n in order],
            *[new_m[n] for n in order], *[new_v[n] for n in order])
```

```python
import functools
import math

import jax
import jax.numpy as jnp
from jax import lax
from jax.experimental import pallas as pl
from jax.experimental.pallas import tpu as pltpu

F32, BF16 = jnp.float32, jnp.bfloat16
S = jax.ShapeDtypeStruct
MESH = pl.DeviceIdType.MESH

N_DEV = 8
EPS = 1e-6
LANE = 128
CHUNK = 64
SUB = 16
HG_H, HG_D = 4, 128
HG_GROUP = 2
FOX_H, FOX_D = 8, 64
FOX_P = FOX_H // 2
MEM_H, MEM_D = 4, 128
NEG = -1e30
VMEM_LIMIT = 56 * 2**20

ADAM_LR, ADAM_B1, ADAM_B2, ADAM_EPS, ADAM_WD, ADAM_STEP = 0.001, 0.9, 0.999, 1e-08, 0.01, 10

C_FOX, C_MQ, C_HG, C_GATE, C_FF, C_END = 0, 1536, 2048, 4096, 7168, 7296


def _cp(sem=None):
    return pltpu.CompilerParams(dimension_semantics=sem, vmem_limit_bytes=VMEM_LIMIT)


def _dot(a, b, dims, prec=None):
    return lax.dot_general(a, b, (dims, ((), ())), preferred_element_type=F32, precision=prec)


def _nn(a, b, prec=None):
    return _dot(a, b, ((1,), (0,)), prec)


def _nt(a, b, prec=None):
    return _dot(a, b, ((1,), (1,)), prec)


def _tn(a, b, prec=None):
    return _dot(a, b, ((0,), (0,)), prec)


def _b(x):
    return x.astype(BF16)


def _mm3(fn, a, b):
    ah, bh = _b(a), _b(b)
    return fn(ah, bh) + fn(ah, _b(b - bh.astype(F32))) + fn(_b(a - ah.astype(F32)), bh)


def _iota(shape, dim):
    return lax.broadcasted_iota(jnp.int32, shape, dim)


def _rowsum8(x):
    r, d = x.shape
    return jnp.sum(x.reshape(r // 8, 8, d), axis=0)


def _rmsnorm_cast(x, g, name, tm=512):
    n, d = x.shape

    def body(x_ref, g_ref, o_ref):
        v = x_ref[...]
        r = lax.rsqrt(jnp.mean(v * v, axis=-1, keepdims=True) + EPS)
        o_ref[...] = (v * r * g_ref[...]).astype(BF16)

    return pl.pallas_call(
        body, name=name, grid=(n // tm,),
        in_specs=[pl.BlockSpec((tm, d), lambda i: (i, 0)), pl.BlockSpec((1, d), lambda i: (0, 0))],
        out_specs=pl.BlockSpec((tm, d), lambda i: (i, 0)), out_shape=S((n, d), BF16), compiler_params=_cp(("parallel",)),
    )(x, g)


def _rmsnorm_bwd(dh, x, g, resid, name, tm=512):
    n, d = x.shape
    has_res = resid is not None

    def body(*refs):
        if has_res:
            dh_ref, x_ref, g_ref, r_ref, dx_ref, dg_ref = refs
        else:
            dh_ref, x_ref, g_ref, dx_ref, dg_ref = refs
        v = x_ref[...]
        dhv = dh_ref[...].astype(F32)
        r = lax.rsqrt(jnp.mean(v * v, axis=-1, keepdims=True) + EPS)
        xh = v * r
        u = dhv * g_ref[...]
        dx = r * (u - xh * jnp.mean(u * xh, axis=-1, keepdims=True))
        if has_res:
            dx = dx + r_ref[...]
        dx_ref[...] = dx

        @pl.when(pl.program_id(0) == 0)
        def _():
            dg_ref[...] = jnp.zeros_like(dg_ref)

        dg_ref[...] += _rowsum8(dhv * xh)

    tile = pl.BlockSpec((tm, d), lambda i: (i, 0))
    ins = [tile, tile, pl.BlockSpec((1, d), lambda i: (0, 0))] + ([tile] if has_res else [])
    args = (dh, x, g) + ((resid,) if has_res else ())
    return pl.pallas_call(
        body, name=name, grid=(n // tm,), in_specs=ins,
        out_specs=[tile, pl.BlockSpec((8, d), lambda i: (0, 0))],
        out_shape=[S((n, d), F32), S((8, d), F32)], compiler_params=_cp(("arbitrary",)),
    )(*args)


def _mm_nn(a, b, out_dtype, name, tm, tn, b_col0=0, n_out=None):
    m, k = a.shape
    n_out = b.shape[1] if n_out is None else n_out
    jb = b_col0 // tn
    assert b_col0 % tn == 0 and n_out % tn == 0 and m % tm == 0

    def body(a_ref, b_ref, o_ref):
        o_ref[...] = _nn(a_ref[...].astype(BF16), b_ref[...].astype(BF16)).astype(out_dtype)

    return pl.pallas_call(
        body, name=name, grid=(m // tm, n_out // tn),
        in_specs=[pl.BlockSpec((tm, k), lambda i, j: (i, 0)), pl.BlockSpec((k, tn), lambda i, j: (0, j + jb))],
        out_specs=pl.BlockSpec((tm, tn), lambda i, j: (i, j)), out_shape=S((m, n_out), out_dtype),
        compiler_params=_cp(("parallel", "parallel")),
    )(a, b)


def _mm_nt(dy, w, name, tm, tr, w_col0=0, acc=None):
    m, r = dy.shape
    k = w.shape[0]
    jb = w_col0 // tr
    nr = r // tr
    assert w_col0 % tr == 0 and r % tr == 0 and m % tm == 0
    has_acc = acc is not None

    def body(*refs):
        if has_acc:
            dy_ref, w_ref, acc_ref, o_ref = refs
        else:
            dy_ref, w_ref, o_ref = refs
        part = _nt(dy_ref[...].astype(BF16), w_ref[...].astype(BF16))

        @pl.when(pl.program_id(1) == 0)
        def _():
            o_ref[...] = part + acc_ref[...] if has_acc else part

        @pl.when(pl.program_id(1) > 0)
        def _():
            o_ref[...] += part

    out_tile = pl.BlockSpec((tm, k), lambda i, j: (i, 0))
    ins = [pl.BlockSpec((tm, tr), lambda i, j: (i, j)), pl.BlockSpec((k, tr), lambda i, j: (0, j + jb))]
    args = (dy, w)
    if has_acc:
        ins.append(out_tile)
        args = args + (acc,)
    return pl.pallas_call(
        body, name=name, grid=(m // tm, nr), in_specs=ins, out_specs=out_tile, out_shape=S((m, k), F32),
        input_output_aliases=({2: 0} if has_acc else {}), compiler_params=_cp(("parallel", "arbitrary")),
    )(*args)


def _mm_nt_sum(parts, w, name, tm):
    m = parts[0][0].shape[0]
    k = w.shape[0]
    assert m % tm == 0 and all(c % n == 0 and o % n == 0 for _, c, n, o in parts)
    np_ = len(parts)

    def body(*refs):
        o_ref = refs[2 * np_]
        acc = _nt(refs[0][...].astype(BF16), refs[np_][...].astype(BF16))
        for i in range(1, np_):
            acc = acc + _nt(refs[i][...].astype(BF16), refs[np_ + i][...].astype(BF16))
        o_ref[...] = acc

    dy_specs = [pl.BlockSpec((tm, n), functools.partial(lambda i, j: (i, j), j=c // n)) for _, c, n, _ in parts]
    w_specs = [pl.BlockSpec((k, n), functools.partial(lambda i, j: (0, j), j=o // n)) for _, _, n, o in parts]
    return pl.pallas_call(
        body, name=name, grid=(m // tm,), in_specs=dy_specs + w_specs, out_specs=pl.BlockSpec((tm, k), lambda i: (i, 0)),
        out_shape=S((m, k), F32), compiler_params=_cp(("parallel",)),
    )(*([p[0] for p in parts] + [w] * np_))


def _mm_tn(x, dy, name, tm, tn):
    m, k = x.shape
    n = dy.shape[1]
    tm = min(tm, m)
    assert m % tm == 0 and n % tn == 0

    def body(x_ref, dy_ref, o_ref):
        part = _tn(x_ref[...].astype(BF16), dy_ref[...].astype(BF16))

        @pl.when(pl.program_id(1) == 0)
        def _():
            o_ref[...] = part

        @pl.when(pl.program_id(1) > 0)
        def _():
            o_ref[...] += part

    return pl.pallas_call(
        body, name=name, grid=(n // tn, m // tm),
        in_specs=[pl.BlockSpec((tm, k), lambda j, i: (i, 0)), pl.BlockSpec((tm, tn), lambda j, i: (i, j))],
        out_specs=pl.BlockSpec((k, tn), lambda j, i: (0, j)), out_shape=S((k, n), F32),
        compiler_params=_cp(("parallel", "arbitrary")),
    )(x, dy)


def _lower_bound(logits):
    e = jnp.exp(logits - jnp.max(logits, axis=0, keepdims=True))
    return e[0:1, :] / jnp.sum(e, axis=0, keepdims=True)


def _hg_gates(fl, lb):
    sig = jax.nn.sigmoid(fl)
    f = lb + (1.0 - lb) * sig
    k = (1.0 - lb) * (1.0 - sig)
    return sig, f, k, jnp.log(f)


def _silu_and_grad(x):
    s = jax.nn.sigmoid(x)
    return x * s, s * (1.0 + x * (1.0 - s))


def _hg_rowblocks(G):
    return [None] + [G[SUB * i - 1:SUB * i, :] for i in range(1, CHUNK // SUB)]


def _hg_intra_A(qs, k, G):
    refs = _hg_rowblocks(G)
    cols = _iota((SUB, CHUNK), 1)
    rows = _iota((SUB, CHUNK), 0)
    blocks = []
    for i in range(CHUNK // SUB):
        lo = SUB * i
        qb, Gb = qs[lo:lo + SUB, :], G[lo:lo + SUB, :]
        diag = jnp.zeros((SUB, CHUNK), F32)
        for s in range(SUB):
            e = jnp.exp(jnp.minimum(Gb - G[lo + s:lo + s + 1, :], 0.0))
            col = jnp.sum(qb * k[lo + s:lo + s + 1, :] * e, axis=-1, keepdims=True)
            diag = jnp.where(cols == lo + s, col, diag)
        a = jnp.where((cols >= lo) & (cols <= rows + lo), diag, 0.0)
        if i > 0:
            qr = qb * jnp.exp(Gb - refs[i])
            kr = k * jnp.exp(jnp.minimum(refs[i] - G, 0.0))
            a = jnp.where(cols < lo, _nt(_b(qr), _b(kr)), a)
        blocks.append(a)
    return jnp.concatenate(blocks, axis=0)


def _hg_intra_bwd(dA, qs, k, G):
    refs = _hg_rowblocks(G)
    cols = _iota((SUB, CHUNK), 1)
    rows16 = _iota((SUB, HG_D), 0)
    dk = jnp.zeros((CHUNK, HG_D), F32)
    dq_blocks, dk_diag_blocks = [], []
    for i in range(CHUNK // SUB):
        lo = SUB * i
        qb, Gb = qs[lo:lo + SUB, :], G[lo:lo + SUB, :]
        dAb = dA[lo:lo + SUB, :]
        dq = jnp.zeros((SUB, HG_D), F32)
        dkb = jnp.zeros((SUB, HG_D), F32)
        for s in range(SUB):
            e = jnp.exp(jnp.minimum(Gb - G[lo + s:lo + s + 1, :], 0.0))
            e = jnp.where(rows16 >= s, e, 0.0)
            dcol = jnp.sum(jnp.where(cols == lo + s, dAb, 0.0), axis=-1, keepdims=True)
            w = dcol * e
            dq = dq + w * k[lo + s:lo + s + 1, :]
            dkb = jnp.where(rows16 == s, jnp.sum(w * qb, axis=0, keepdims=True), dkb)
        if i > 0:
            e1 = jnp.exp(Gb - refs[i])
            e2 = jnp.exp(jnp.minimum(refs[i] - G, 0.0))
            dA_off = jnp.where(cols < lo, dAb, 0.0)
            dq = dq + _mm3(_nn, dA_off, k * e2) * e1
            dk = dk + _mm3(_tn, dA_off, qb * e1) * e2
        dq_blocks.append(dq)
        dk_diag_blocks.append(dkb)
    return jnp.concatenate(dq_blocks, axis=0), dk + jnp.concatenate(dk_diag_blocks, axis=0)


def _tri(n, upper=False):
    r, c = _iota((n, n), 0), _iota((n, n), 1)
    return jnp.where((c >= r) if upper else (r >= c), 1.0, 0.0).astype(BF16)


def _prefix_mm(tri, x):
    hi = x.astype(BF16)
    r1 = x - hi.astype(F32)
    mid = r1.astype(BF16)
    lo = (r1 - mid.astype(F32)).astype(BF16)
    return _nn(tri, hi) + _nn(tri, mid) + _nn(tri, lo)


def _hgrn_fwd(z, lb, gn, B, T):
    N = B * T
    NC = T // CHUNK

    def body(z_ref, lb_ref, gn_ref, y_ref, o_ref, st_ref, s_scr):
        lbs = _lower_bound(lb_ref[...])
        tri = _tri(CHUNK)
        s_scr[...] = jnp.zeros_like(s_scr)

        def chunk(c, carry):
            r = pl.ds(pl.multiple_of(c * CHUNK, CHUNK), CHUNK)
            for hh in range(HG_GROUP):
                zc, oc = 4 * LANE * hh, LANE * hh
                ql, fl, il, gl = (z_ref[r, zc + LANE * j:zc + LANE * (j + 1)] for j in range(4))
                _, _, k, logf = _hg_gates(fl, lbs[:, oc:oc + LANE])
                G = _prefix_mm(tri, logf)
                qs = ql * jax.nn.sigmoid(ql)
                st = s_scr[hh]
                st_ref[hh * NC + c] = st
                A = _hg_intra_A(qs, k, G)
                ib = _b(il)
                o = _nn(_b(A), ib) + _nt(_b(qs * jnp.exp(G)), _b(st))
                g_last = G[CHUNK - 1:CHUNK, :]
                s_scr[hh] = st * jnp.exp(g_last) + _mm3(_tn, il, k * jnp.exp(g_last - G))
                o_ref[r, oc:oc + LANE] = o
                rstd = lax.rsqrt(jnp.mean(o * o, axis=-1, keepdims=True) + EPS)
                y_ref[r, oc:oc + LANE] = (o * rstd * gn_ref[...] * (gl * jax.nn.sigmoid(gl))).astype(BF16)
            return carry

        lax.fori_loop(0, NC, chunk, 0)

    gw = HG_GROUP * LANE
    cb = C_HG // (4 * gw)
    ng = HG_H // HG_GROUP
    return pl.pallas_call(
        body, name="hgrn_fwd", grid=(B, ng),
        in_specs=[pl.BlockSpec((T, 4 * gw), lambda b, h: (b, cb + h)), pl.BlockSpec((lb.shape[0], gw), lambda b, h: (0, h)),
                  pl.BlockSpec((1, LANE), lambda b, h: (0, 0))],
        out_specs=[pl.BlockSpec((T, gw), lambda b, h: (b, h)), pl.BlockSpec((T, gw), lambda b, h: (b, h)),
                   pl.BlockSpec((HG_GROUP * NC, HG_D, HG_D), lambda b, h: (b * ng + h, 0, 0))],
        out_shape=[S((N, 512), BF16), S((N, 512), F32), S((B * HG_H * NC, HG_D, HG_D), F32)],
        scratch_shapes=[pltpu.VMEM((HG_GROUP, HG_D, HG_D), F32)], compiler_params=_cp(("parallel", "parallel")),
    )(z, lb, gn)


def _hgrn_bwd(z, o_raw, states, dy, lb, gn, B, T):
    N = B * T
    NC = T // CHUNK

    def body(z_ref, o_ref, st_ref, dy_ref, lb_ref, gn_ref, dz_ref, dlb_ref, dgn_ref, ds_scr, racc, dgn_acc):
        lbs = _lower_bound(lb_ref[...])
        gn_v = gn_ref[...]
        tri, triu = _tri(CHUNK), _tri(CHUNK, upper=True)
        cmask = _iota((CHUNK, CHUNK), 0) >= _iota((CHUNK, CHUNK), 1)
        for ref in (ds_scr, racc, dgn_acc, dlb_ref):
            ref[...] = jnp.zeros_like(ref)

        def chunk(ci, carry):
            c = NC - 1 - ci
            r = pl.ds(pl.multiple_of(c * CHUNK, CHUNK), CHUNK)
            for hh in range(HG_GROUP):
                zc, oc = 4 * LANE * hh, LANE * hh
                lb_v = lbs[:, oc:oc + LANE]
                ql, fl, il, gl = (z_ref[r, zc + LANE * j:zc + LANE * (j + 1)] for j in range(4))
                sig, f, k, logf = _hg_gates(fl, lb_v)
                G = _prefix_mm(tri, logf)
                qs, dsilu_q = _silu_and_grad(ql)
                gs, dsilu_g = _silu_and_grad(gl)
                o = o_ref[r, oc:oc + LANE]
                dyv = dy_ref[r, oc:oc + LANE]
                rstd = lax.rsqrt(jnp.mean(o * o, axis=-1, keepdims=True) + EPS)
                oh = o * rstd
                dgl = dyv * oh * gn_v * dsilu_g
                dn = dyv * gs
                dgn_acc[...] += _rowsum8(dn * oh)
                u = dn * gn_v
                do = rstd * (u - oh * jnp.mean(u * oh, axis=-1, keepdims=True))
                st = st_ref[hh * NC + c]
                dst = ds_scr[hh]
                eG = jnp.exp(G)
                g_last = G[CHUNK - 1:CHUNK, :]
                eL = jnp.exp(g_last - G)
                A = _hg_intra_A(qs, k, G)
                dA = jnp.where(cmask, _mm3(_nt, do, il), 0.0)
                di = _tn(_b(A), _b(do)) + _nt(_b(k * eL), _b(dst))
                dq_in, dk_in = _hg_intra_bwd(dA, qs, k, G)
                dq = dq_in + _mm3(_nn, do, st) * eG
                dk = dk_in + _mm3(_nn, il, dst) * eL
                ds_scr[hh] = dst * jnp.exp(g_last) + _mm3(_tn, do, qs * eG)
                dd = qs * dq - k * dk
                dlogf = _prefix_mm(triu, dd) + racc[hh]
                racc[hh] += jnp.sum(dd, axis=0, keepdims=True)
                df = dlogf / f - dk
                dlb_ref[8 * hh:8 * (hh + 1), :] += _rowsum8(df * (1.0 - sig))
                dz_ref[r, zc:zc + LANE] = (dq * dsilu_q).astype(BF16)
                dz_ref[r, zc + LANE:zc + 2 * LANE] = (df * (1.0 - lb_v) * sig * (1.0 - sig)).astype(BF16)
                dz_ref[r, zc + 2 * LANE:zc + 3 * LANE] = di.astype(BF16)
                dz_ref[r, zc + 3 * LANE:zc + 4 * LANE] = dgl.astype(BF16)
            return carry

        lax.fori_loop(0, NC, chunk, 0)
        dgn_ref[...] = dgn_acc[...]

    gw = HG_GROUP * LANE
    cb = C_HG // (4 * gw)
    ng = HG_H // HG_GROUP
    col = pl.BlockSpec((T, gw), lambda b, h: (b, h))
    return pl.pallas_call(
        body, name="hgrn_bwd", grid=(B, ng),
        in_specs=[pl.BlockSpec((T, 4 * gw), lambda b, h: (b, cb + h)), col,
                  pl.BlockSpec((HG_GROUP * NC, HG_D, HG_D), lambda b, h: (b * ng + h, 0, 0)), col,
                  pl.BlockSpec((lb.shape[0], gw), lambda b, h: (0, h)), pl.BlockSpec((1, LANE), lambda b, h: (0, 0))],
        out_specs=[pl.BlockSpec((T, 4 * gw), lambda b, h: (b, h)),
                   pl.BlockSpec((8 * HG_GROUP, LANE), lambda b, h: (b * ng + h, 0)),
                   pl.BlockSpec((8, LANE), lambda b, h: (b * ng + h, 0))],
        out_shape=[S((N, 2048), BF16), S((B * HG_H * 8, LANE), F32), S((B * ng * 8, LANE), F32)],
        scratch_shapes=[pltpu.VMEM((HG_GROUP, HG_D, HG_D), F32), pltpu.VMEM((HG_GROUP, 1, LANE), F32),
                        pltpu.VMEM((8, LANE), F32)],
        compiler_params=_cp(("parallel", "parallel")),
    )(z, o_raw, states, dy, lb, gn)


def _pair_mean(x, lo_half):
    a = jnp.sum(jnp.where(lo_half, x, 0.0), axis=-1, keepdims=True)
    b = jnp.sum(jnp.where(lo_half, 0.0, x), axis=-1, keepdims=True)
    return jnp.where(lo_half, a, b) * (1.0 / FOX_D)


def _fox_gate_fwd(z, bias, B, T):
    N = B * T
    tb = LANE

    def body(z_ref, b_ref, fc_ref, fct_ref):
        tri = _tri(tb)

        def step(i, carry):
            r = pl.ds(pl.multiple_of(i * tb, tb), tb)
            cs = _prefix_mm(tri, jax.nn.log_sigmoid(z_ref[r, :] + b_ref[...])) + carry
            fc_ref[r, :] = cs
            fct_ref[0, :, r] = cs.T[0:8, :]
            return cs[tb - 1:tb, :]

        lax.fori_loop(0, T // tb, step, jnp.zeros((1, LANE), F32))

    return pl.pallas_call(
        body, name="fox_gate_fwd", grid=(B,),
        in_specs=[pl.BlockSpec((T, LANE), lambda b: (b, C_FF // LANE)), pl.BlockSpec((1, LANE), lambda b: (0, 0))],
        out_specs=[pl.BlockSpec((T, LANE), lambda b: (b, 0)), pl.BlockSpec((1, 8, T), lambda b: (b, 0, 0))],
        out_shape=[S((N, LANE), F32), S((B, 8, T), F32)], compiler_params=_cp(("parallel",)),
    )(z, bias)


def _fox_gate_bwd(dfc, z, bias, B, T):
    N = B * T
    tb = LANE
    nt = T // tb

    def body(d_ref, z_ref, b_ref, dz_ref, db_ref):
        triu = _tri(tb, upper=True)
        db_ref[...] = jnp.zeros_like(db_ref)

        def step(ii, carry):
            r = pl.ds(pl.multiple_of((nt - 1 - ii) * tb, tb), tb)
            d = d_ref[r, 0:LANE]
            for p in range(1, FOX_P):
                d = d + d_ref[r, LANE * p:LANE * (p + 1)]
            rc = _prefix_mm(triu, d) + carry
            dff = rc * jax.nn.sigmoid(-(z_ref[r, :] + b_ref[...]))
            dz_ref[r, :] = dff.astype(BF16)
            db_ref[...] += _rowsum8(dff)
            return carry + jnp.sum(d, axis=0, keepdims=True)

        lax.fori_loop(0, nt, step, jnp.zeros((1, LANE), F32))

    return pl.pallas_call(
        body, name="fox_gate_bwd", grid=(B,),
        in_specs=[pl.BlockSpec((T, 512), lambda b: (b, 0)), pl.BlockSpec((T, LANE), lambda b: (b, C_FF // LANE)),
                  pl.BlockSpec((1, LANE), lambda b: (0, 0))],
        out_specs=[pl.BlockSpec((T, LANE), lambda b: (b, 0)), pl.BlockSpec((8, LANE), lambda b: (b, 0))],
        out_shape=[S((N, LANE), BF16), S((B * 8, LANE), F32)], compiler_params=_cp(("parallel",)),
    )(dfc, z, bias)


def _fox_prep(z_ref, gq, gk, r, lo_half):
    q, k, v = z_ref[r, 0:LANE], z_ref[r, LANE:2 * LANE], z_ref[r, 2 * LANE:3 * LANE]
    rq = lax.rsqrt(_pair_mean(q * q, lo_half) + EPS)
    rk = lax.rsqrt(_pair_mean(k * k, lo_half) + EPS)
    qh, kh = q * rq, k * rk
    return qh * gq * (FOX_D ** -0.5), kh * gk, v, qh, kh, rq, rk


def _fox_fwd(z, fc, fct, gq, gk, B, T, tq=512):
    N = B * T
    NQ = T // tq

    def body(z_ref, fc_ref, fct_ref, gq_ref, gk_ref, y_ref, lse_ref, qn_s, kn_s, v_s):
        p, qi = pl.program_id(1), pl.program_id(2)
        lo_half = _iota((1, LANE), 1) < FOX_D

        @pl.when(qi == 0)
        def _():
            def prep(i, carry):
                r = pl.ds(pl.multiple_of(i * tq, tq), tq)
                qn, kn, v = _fox_prep(z_ref, gq_ref[...], gk_ref[...], r, lo_half)[:3]
                qn_s[r, :], kn_s[r, :], v_s[r, :] = qn.astype(BF16), kn.astype(BF16), v.astype(BF16)
                return carry
            lax.fori_loop(0, NQ, prep, 0)

        rq = pl.ds(pl.multiple_of(qi * tq, tq), tq)
        qn = qn_s[rq, :]
        fcq = fc_ref[rq, :]
        lane = _iota((tq, LANE), 1)
        causal = _iota((tq, tq), 0) >= _iota((tq, tq), 1)
        qhs = [jnp.where(lo_half, qn, jnp.zeros_like(qn)), jnp.where(lo_half, jnp.zeros_like(qn), qn)]
        fqs = [jnp.sum(jnp.where(lane == 2 * p + hh, fcq, 0.0), axis=-1, keepdims=True) for hh in range(2)]

        def kv(j, carry, diagonal):
            rk = pl.ds(pl.multiple_of(j * tq, tq), tq)
            kj, vj = kn_s[rk, :], v_s[rk, :]
            new = []
            for hh in range(2):
                m, l, acc = carry[hh]
                s = _nt(qhs[hh], kj) + fqs[hh] - fct_ref[0, pl.ds(2 * p + hh, 1), rk]
                if diagonal:
                    s = jnp.where(causal, s, NEG)
                m_new = jnp.maximum(m, jnp.max(s, axis=-1, keepdims=True))
                pe = jnp.exp(s - m_new)
                alpha = jnp.exp(m - m_new)
                new.append((m_new, alpha * l + jnp.sum(pe, axis=-1, keepdims=True),
                            alpha * acc + _nn(pe.astype(BF16), vj)))
            return tuple(new)

        init = tuple((jnp.full((tq, 1), NEG, F32), jnp.zeros((tq, 1), F32), jnp.zeros((tq, LANE), F32)) for _ in range(2))
        carry = lax.fori_loop(0, qi, functools.partial(kv, diagonal=False), init)
        (m0, l0, a0), (m1, l1, a1) = kv(qi, carry, True)
        y_ref[...] = jnp.where(lo_half, a0 / l0, a1 / l1).astype(BF16)
        lse_ref[...] = jnp.where(lo_half, m0 + jnp.log(l0), m1 + jnp.log(l1))

    vec = pl.BlockSpec((1, LANE), lambda b, p, q: (0, 0))
    tile = pl.BlockSpec((tq, LANE), lambda b, p, q: (b * NQ + q, p))
    return pl.pallas_call(
        body, name="fox_fwd", grid=(B, FOX_P, NQ),
        in_specs=[pl.BlockSpec((T, 384), lambda b, p, q: (b, p)), pl.BlockSpec((T, LANE), lambda b, p, q: (b, 0)),
                  pl.BlockSpec((1, 8, T), lambda b, p, q: (b, 0, 0)), vec, vec],
        out_specs=[tile, tile], out_shape=[S((N, 512), BF16), S((N, 512), F32)],
        scratch_shapes=[pltpu.VMEM((T, LANE), BF16)] * 3,
        compiler_params=_cp(("parallel", "parallel", "arbitrary")),
    )(z, fc, fct, gq, gk)


def _fox_bwd(z, dy, y, lse, fc, fct, gq, gk, B, T, tq=512):
    N = B * T
    NQ = T // tq

    def body(z_ref, dy_ref, y_ref, lse_ref, fc_ref, fct_ref, gq_ref, gk_ref, dz_ref, dfc_ref, dgq_ref, dgk_ref,
             qn_s, kn_s, v_s, do_s, delta_s, dq_s, dfk_s):
        p, kj = pl.program_id(1), pl.program_id(2)
        lo_half = _iota((1, LANE), 1) < FOX_D
        lane = _iota((tq, LANE), 1)
        gq_v, gk_v = gq_ref[...], gk_ref[...]

        @pl.when(kj == 0)
        def _():
            def prep(i, carry):
                r = pl.ds(pl.multiple_of(i * tq, tq), tq)
                qn, kn, v = _fox_prep(z_ref, gq_v, gk_v, r, lo_half)[:3]
                qn_s[r, :], kn_s[r, :], v_s[r, :] = qn.astype(BF16), kn.astype(BF16), v.astype(BF16)
                do = dy_ref[r, :]
                do_s[r, :] = do.astype(BF16)
                delta_s[r, :] = _pair_mean(do * y_ref[r, :].astype(F32), lo_half) * float(FOX_D)
                return carry
            lax.fori_loop(0, NQ, prep, 0)
            dq_s[...] = jnp.zeros_like(dq_s)
            dgq_ref[...] = jnp.zeros_like(dgq_ref)
            dgk_ref[...] = jnp.zeros_like(dgk_ref)

        rk = pl.ds(pl.multiple_of(kj * tq, tq), tq)
        kn, vv = kn_s[rk, :], v_s[rk, :]
        causal = _iota((tq, tq), 0) >= _iota((tq, tq), 1)
        zero, one = jnp.zeros_like(kn), jnp.ones_like(kn)
        hms = [lo_half, jnp.logical_not(lo_half)]
        kmasks = [jnp.where(hm, kn, zero) for hm in hms]
        kaugs = [jnp.where(hm, kn, one) for hm in hms]
        vmasks = [jnp.where(hm, vv, zero) for hm in hms]
        fks = [fct_ref[0, pl.ds(2 * p + hh, 1), rk] for hh in range(2)]

        def qloop(i, carry, diagonal):
            ri = pl.ds(pl.multiple_of(i * tq, tq), tq)
            qn = qn_s[ri, :]
            do = do_s[ri, :]
            fcq = fc_ref[ri, :]
            new = []
            for hh in range(2):
                dk_acc, dv_acc = carry[hh]
                c0 = FOX_D * hh
                fq = jnp.sum(jnp.where(lane == 2 * p + hh, fcq, 0.0), axis=-1, keepdims=True)
                pr = jnp.exp(_nt(qn, kmasks[hh]) + fq - fks[hh] - lse_ref[ri, c0:c0 + 1])
                if diagonal:
                    pr = jnp.where(causal, pr, 0.0)
                ds = (pr * (_nt(do, vmasks[hh]) - delta_s[ri, c0:c0 + 1])).astype(BF16)
                dq_s[hh, ri, :] += _nn(ds, kaugs[hh])
                new.append((dk_acc + _tn(ds, jnp.where(hms[hh], qn, one)), dv_acc + _tn(pr.astype(BF16), do)))
            return tuple(new)

        init = tuple((jnp.zeros((tq, LANE), F32), jnp.zeros((tq, LANE), F32)) for _ in range(2))
        carry = qloop(kj, init, True)
        (dk0, dv0), (dk1, dv1) = lax.fori_loop(kj + 1, NQ, functools.partial(qloop, diagonal=False), carry)
        dks, dvs = [dk0, dk1], [dv0, dv1]

        dkn = jnp.where(lo_half, dks[0], dks[1])
        _, _, _, _, kh, _, rkk = _fox_prep(z_ref, gq_v, gk_v, rk, lo_half)
        u = dkn * gk_v
        dz_ref[rk, LANE:2 * LANE] = (rkk * (u - kh * _pair_mean(u * kh, lo_half))).astype(BF16)
        dz_ref[rk, 2 * LANE:3 * LANE] = jnp.where(lo_half, dvs[0], dvs[1]).astype(BF16)
        dgk_ref[...] += _rowsum8(dkn * kh)
        dfk_s[rk, :] = jnp.where(lane == 2 * p, -dks[0][:, FOX_D:FOX_D + 1],
                                 jnp.where(lane == 2 * p + 1, -dks[1][:, 0:1], 0.0))

        @pl.when(kj == NQ - 1)
        def _():
            def fin(i, carry):
                r = pl.ds(pl.multiple_of(i * tq, tq), tq)
                d0, d1 = dq_s[0, r, :], dq_s[1, r, :]
                dqn = jnp.where(lo_half, d0, d1)
                _, _, _, qh, _, rqq, _ = _fox_prep(z_ref, gq_v, gk_v, r, lo_half)
                u = dqn * gq_v * (FOX_D ** -0.5)
                dz_ref[r, 0:LANE] = (rqq * (u - qh * _pair_mean(u * qh, lo_half))).astype(BF16)
                dgq_ref[...] += _rowsum8(dqn * qh) * (FOX_D ** -0.5)
                dfc_ref[r, :] = dfk_s[r, :] + jnp.where(lane == 2 * p, d0[:, FOX_D:FOX_D + 1],
                                                        jnp.where(lane == 2 * p + 1, d1[:, 0:1], 0.0))
                return carry
            lax.fori_loop(0, NQ, fin, 0)

    vec = pl.BlockSpec((1, LANE), lambda b, p, k: (0, 0))
    col = pl.BlockSpec((T, LANE), lambda b, p, k: (b, p))
    part = pl.BlockSpec((8, LANE), lambda b, p, k: (b * FOX_P + p, 0))
    return pl.pallas_call(
        body, name="fox_bwd", grid=(B, FOX_P, NQ),
        in_specs=[pl.BlockSpec((T, 384), lambda b, p, k: (b, p)), col, col, col,
                  pl.BlockSpec((T, LANE), lambda b, p, k: (b, 0)), pl.BlockSpec((1, 8, T), lambda b, p, k: (b, 0, 0)),
                  vec, vec],
        out_specs=[pl.BlockSpec((T, 384), lambda b, p, k: (b, p)), col, part, part],
        out_shape=[S((N, 1536), BF16), S((N, 512), F32), S((B * FOX_P * 8, LANE), F32), S((B * FOX_P * 8, LANE), F32)],
        scratch_shapes=[pltpu.VMEM((T, LANE), BF16)] * 4 + [pltpu.VMEM((T, LANE), F32), pltpu.VMEM((2, T, LANE), F32),
                                                            pltpu.VMEM((T, LANE), F32)],
        compiler_params=_cp(("parallel", "parallel", "arbitrary")),
    )(z, dy, y, lse, fc, fct, gq, gk)


def _mem_scores(z_ref, kv_ref, gq, gk, h):
    c = slice(MEM_D * h, MEM_D * (h + 1))
    q, k = z_ref[:, c], kv_ref[:, c]
    rq = lax.rsqrt(jnp.mean(q * q, axis=-1, keepdims=True) + EPS)
    rk = lax.rsqrt(jnp.mean(k * k, axis=-1, keepdims=True) + EPS)
    qh, kh = q * rq, k * rk
    qn = (qh * gq * (MEM_D ** -0.5)).astype(BF16)
    kn = (kh * gk).astype(BF16)
    s = _nt(qn, kn)
    pe = jnp.exp(s - jnp.max(s, axis=-1, keepdims=True))
    pn = pe / jnp.sum(pe, axis=-1, keepdims=True)
    return pn, qn, kn, qh, kh, rq, rk


def _mem_fwd(z, memkv, gq, gk, B, T, M, tq=512):
    N = B * T
    NQ = T // tq
    W = MEM_H * MEM_D

    def body(z_ref, kv_ref, gq_ref, gk_ref, y_ref):
        for h in range(MEM_H):
            pn = _mem_scores(z_ref, kv_ref, gq_ref[...], gk_ref[...], h)[0]
            v = kv_ref[:, W + MEM_D * h:W + MEM_D * (h + 1)].astype(BF16)
            y_ref[:, MEM_D * h:MEM_D * (h + 1)] = _nn(pn.astype(BF16), v).astype(BF16)

    vec = pl.BlockSpec((1, LANE), lambda b, q: (0, 0))
    return pl.pallas_call(
        body, name="mem_fwd", grid=(B, NQ),
        in_specs=[pl.BlockSpec((tq, W), lambda b, q: (b * NQ + q, C_MQ // W)),
                  pl.BlockSpec((M, 2 * W), lambda b, q: (b, 0)), vec, vec],
        out_specs=pl.BlockSpec((tq, W), lambda b, q: (b * NQ + q, 0)), out_shape=S((N, W), BF16),
        compiler_params=_cp(("parallel", "parallel")),
    )(z, memkv, gq, gk)


def _mem_bwd(z, memkv, dy, gq, gk, B, T, M, tq=512):
    N = B * T
    NQ = T // tq
    W = MEM_H * MEM_D

    def body(z_ref, kv_ref, dy_ref, gq_ref, gk_ref, dz_ref, dkv_ref, dgq_ref, dgk_ref, acc):
        qi = pl.program_id(1)
        gq_v, gk_v = gq_ref[...], gk_ref[...]

        @pl.when(qi == 0)
        def _():
            acc[...] = jnp.zeros_like(acc)
            dgq_ref[...] = jnp.zeros_like(dgq_ref)
            dgk_ref[...] = jnp.zeros_like(dgk_ref)

        for h in range(MEM_H):
            c = slice(MEM_D * h, MEM_D * (h + 1))
            cv = slice(W + MEM_D * h, W + MEM_D * (h + 1))
            pn, qn, kn, qh, _, rq, _ = _mem_scores(z_ref, kv_ref, gq_v, gk_v, h)
            do = dy_ref[:, c].astype(BF16)
            dp = _nt(do, kv_ref[:, cv].astype(BF16))
            ds = (pn * (dp - jnp.sum(dp * pn, axis=-1, keepdims=True))).astype(BF16)
            dqn = _nn(ds, kn)
            acc[:, c] += _tn(ds, qn)
            acc[:, cv] += _tn(pn.astype(BF16), do)
            u = dqn * gq_v * (MEM_D ** -0.5)
            dz_ref[:, c] = (rq * (u - qh * jnp.mean(u * qh, axis=-1, keepdims=True))).astype(BF16)
            dgq_ref[...] += _rowsum8(dqn * qh) * (MEM_D ** -0.5)

        @pl.when(qi == NQ - 1)
        def _():
            for h in range(MEM_H):
                c = slice(MEM_D * h, MEM_D * (h + 1))
                cv = slice(W + MEM_D * h, W + MEM_D * (h + 1))
                k = kv_ref[:, c]
                rk = lax.rsqrt(jnp.mean(k * k, axis=-1, keepdims=True) + EPS)
                kh = k * rk
                dkn = acc[:, c]
                u = dkn * gk_v
                dkv_ref[:, c] = (rk * (u - kh * jnp.mean(u * kh, axis=-1, keepdims=True))).astype(BF16)
                dkv_ref[:, cv] = acc[:, cv].astype(BF16)
                dgk_ref[...] += _rowsum8(dkn * kh)

    vec = pl.BlockSpec((1, LANE), lambda b, q: (0, 0))
    part = pl.BlockSpec((8, LANE), lambda b, q: (b, 0))
    return pl.pallas_call(
        body, name="mem_bwd", grid=(B, NQ),
        in_specs=[pl.BlockSpec((tq, W), lambda b, q: (b * NQ + q, C_MQ // W)),
                  pl.BlockSpec((M, 2 * W), lambda b, q: (b, 0)), pl.BlockSpec((tq, W), lambda b, q: (b * NQ + q, 0)),
                  vec, vec],
        out_specs=[pl.BlockSpec((tq, W), lambda b, q: (b * NQ + q, 0)), pl.BlockSpec((M, 2 * W), lambda b, q: (b, 0)),
                   part, part],
        out_shape=[S((N, W), BF16), S((B * M, 2 * W), BF16), S((B * 8, LANE), F32), S((B * 8, LANE), F32)],
        scratch_shapes=[pltpu.VMEM((M, 2 * W), F32)], compiler_params=_cp(("parallel", "arbitrary")),
    )(z, memkv, dy, gq, gk)


def _merge_fwd(ya, yb, yc, z, x, wa, wb, wc, wo, tm=256):
    n, d = x.shape
    wdt = ya.shape[1]
    gb = C_GATE // d

    def body(ya_ref, yb_ref, yc_ref, g0_ref, g1_ref, g2_ref, x_ref, wa_ref, wb_ref, wc_ref, wo_ref,
             x1_ref, mg_ref, ua_ref, ub_ref, uc_ref):
        merged = jnp.zeros((tm, d), F32)
        for y_ref, g_ref, w_ref, u_ref in ((ya_ref, g0_ref, wa_ref, ua_ref), (yb_ref, g1_ref, wb_ref, ub_ref),
                                           (yc_ref, g2_ref, wc_ref, uc_ref)):
            u = _nn(y_ref[...], w_ref[...])
            u_ref[...] = u.astype(BF16)
            merged = merged + jax.nn.sigmoid(g_ref[...]) * u
        mb = merged.astype(BF16)
        mg_ref[...] = mb
        x1_ref[...] = x_ref[...] + _nn(mb, wo_ref[...])

    yt = pl.BlockSpec((tm, wdt), lambda i: (i, 0))
    xt = pl.BlockSpec((tm, d), lambda i: (i, 0))
    wbr = pl.BlockSpec((wdt, d), lambda i: (0, 0))
    gates = [pl.BlockSpec((tm, d), functools.partial(lambda i, k: (i, gb + k), k=k)) for k in range(3)]
    return pl.pallas_call(
        body, name="merge_fwd", grid=(n // tm,),
        in_specs=[yt, yt, yt] + gates + [xt, wbr, wbr, wbr, pl.BlockSpec((d, d), lambda i: (0, 0))],
        out_specs=[xt] * 5, out_shape=[S((n, d), F32)] + [S((n, d), BF16)] * 4, compiler_params=_cp(("parallel",)),
    )(ya, yb, yc, z, z, z, x, wa, wb, wc, wo)


def _merge_bwd(dx1, z, ua, ub, uc, wa, wb, wc, wo, tm=256):
    n, d = dx1.shape
    wdt = wa.shape[0]
    gb = C_GATE // d

    def body(dx_ref, g0_ref, g1_ref, g2_ref, ua_ref, ub_ref, uc_ref, wa_ref, wb_ref, wc_ref, wo_ref,
             dg_ref, dya_ref, dyb_ref, dyc_ref, dua_ref, dub_ref, duc_ref):
        dm = _nt(dx_ref[...].astype(BF16), wo_ref[...])
        for k, (g_ref, u_ref, w_ref, dy_ref, du_ref) in enumerate((
                (g0_ref, ua_ref, wa_ref, dya_ref, dua_ref), (g1_ref, ub_ref, wb_ref, dyb_ref, dub_ref),
                (g2_ref, uc_ref, wc_ref, dyc_ref, duc_ref))):
            g = jax.nn.sigmoid(g_ref[...])
            du = (dm * g).astype(BF16)
            du_ref[...] = du
            dg_ref[:, d * k:d * (k + 1)] = (dm * u_ref[...].astype(F32) * g * (1.0 - g)).astype(BF16)
            dy_ref[...] = _nt(du, w_ref[...])

    yt = pl.BlockSpec((tm, wdt), lambda i: (i, 0))
    xt = pl.BlockSpec((tm, d), lambda i: (i, 0))
    wbr = pl.BlockSpec((wdt, d), lambda i: (0, 0))
    gates = [pl.BlockSpec((tm, d), functools.partial(lambda i, k: (i, gb + k), k=k)) for k in range(3)]
    return pl.pallas_call(
        body, name="merge_bwd", grid=(n // tm,),
        in_specs=[xt] + gates + [xt, xt, xt, wbr, wbr, wbr, pl.BlockSpec((d, d), lambda i: (0, 0))],
        out_specs=[pl.BlockSpec((tm, 3 * d), lambda i: (i, 0)), yt, yt, yt, xt, xt, xt],
        out_shape=[S((n, 3 * d), BF16)] + [S((n, wdt), F32)] * 3 + [S((n, d), BF16)] * 3,
        compiler_params=_cp(("parallel",)),
    )(dx1, z, z, z, ua, ub, uc, wa, wb, wc, wo)


FFN_TN = 1408
TN_TM = 2048
INV_SQRT2 = 0.7071067811865476
INV_SQRT_2PI = 0.3989422804014327


def _conv_shifted(a, prev, first, tm):
    row = _iota(a.shape, 0)
    p7 = jnp.where(first, 0.0, prev[7:8, :])
    p6 = jnp.where(first, 0.0, prev[6:7, :])
    a1 = jnp.where(row == 0, p7, pltpu.roll(a, 1, 0))
    a2 = jnp.where(row == 0, p6, jnp.where(row == 1, p7, pltpu.roll(a, 2, 0)))
    return a1, a2


def _ffn_act_fwd(up, cw, cb, B, T, tm=256):
    N = B * T
    dff = cw.shape[1]
    NT, NJ, tn = T // tm, dff // FFN_TN, FFN_TN

    def body(a_ref, v_ref, cw_ref, cb_ref, y_ref, carry):
        t = pl.program_id(2)
        a = a_ref[...].astype(F32)
        a1, a2 = _conv_shifted(a, carry[...], t == 0, tm)
        w = cw_ref[...]
        ac = w[0:1, :] * a2 + w[1:2, :] * a1 + w[2:3, :] * a + cb_ref[...]
        y_ref[...] = (0.5 * ac * (1.0 + lax.erf(ac * INV_SQRT2)) * v_ref[...].astype(F32)).astype(BF16)
        carry[...] = a[tm - 8:tm, :]

    return pl.pallas_call(
        body, name="ffn_act_fwd", grid=(B, NJ, NT),
        in_specs=[pl.BlockSpec((tm, tn), lambda b, j, t: (b * NT + t, j)),
                  pl.BlockSpec((tm, tn), lambda b, j, t: (b * NT + t, NJ + j)),
                  pl.BlockSpec((3, tn), lambda b, j, t: (0, j)), pl.BlockSpec((1, tn), lambda b, j, t: (0, j))],
        out_specs=pl.BlockSpec((tm, tn), lambda b, j, t: (b * NT + t, j)), out_shape=S((N, dff), BF16),
        scratch_shapes=[pltpu.VMEM((8, tn), F32)], compiler_params=_cp(("parallel", "parallel", "arbitrary")),
    )(up, up, cw, cb)


def _ffn_down_loss(y, wd, x1, tgt, tm=256):
    n, d = x1.shape
    kf = y.shape[1]

    def body(y_ref, w_ref, x_ref, t_ref, dx_ref, ls_ref):
        err = x_ref[...] + _nn(y_ref[...], w_ref[...]) - t_ref[...]
        dx_ref[...] = err * (1.0 / d)

        @pl.when(pl.program_id(0) == 0)
        def _():
            ls_ref[...] = jnp.zeros_like(ls_ref)

        ls_ref[...] += _rowsum8(err * err) * (0.5 / d)

    xt = pl.BlockSpec((tm, d), lambda i: (i, 0))
    return pl.pallas_call(
        body, name="ffn_down_loss", grid=(n // tm,),
        in_specs=[pl.BlockSpec((tm, kf), lambda i: (i, 0)), pl.BlockSpec((kf, d), lambda i: (0, 0)), xt, xt],
        out_specs=[xt, pl.BlockSpec((8, d), lambda i: (0, 0))], out_shape=[S((n, d), F32), S((8, d), F32)],
        compiler_params=_cp(("arbitrary",)),
    )(y, wd, x1, tgt)


def _ffn_act_bwd1(dx2, wd, up, cw, cb, B, T, tm=256):
    N = B * T
    d = dx2.shape[1]
    dff = cw.shape[1]
    NT, NJ, tn = T // tm, dff // FFN_TN, FFN_TN

    def body(dx_ref, w_ref, a_ref, v_ref, cw_ref, cb_ref, dac_ref, dv_ref, dcw_ref, dcb_ref, carry):
        b, t = pl.program_id(1), pl.program_id(2)
        a = a_ref[...].astype(F32)
        a1, a2 = _conv_shifted(a, carry[...], t == 0, tm)
        carry[...] = a[tm - 8:tm, :]
        w = cw_ref[...]
        ac = w[0:1, :] * a2 + w[1:2, :] * a1 + w[2:3, :] * a + cb_ref[...]
        dy = _nt(dx_ref[...].astype(BF16), w_ref[...])
        cdf = 0.5 * (1.0 + lax.erf(ac * INV_SQRT2))
        dv_ref[...] = (dy * ac * cdf).astype(BF16)
        dac = dy * v_ref[...].astype(F32) * (cdf + ac * jnp.exp(-0.5 * ac * ac) * INV_SQRT_2PI)
        dac_ref[...] = dac

        @pl.when((b == 0) & (t == 0))
        def _():
            dcw_ref[...] = jnp.zeros_like(dcw_ref)
            dcb_ref[...] = jnp.zeros_like(dcb_ref)

        dcw_ref[0:8, :] += _rowsum8(dac * a2)
        dcw_ref[8:16, :] += _rowsum8(dac * a1)
        dcw_ref[16:24, :] += _rowsum8(dac * a)
        dcb_ref[...] += _rowsum8(dac)

    return pl.pallas_call(
        body, name="ffn_act_bwd1", grid=(NJ, B, NT),
        in_specs=[pl.BlockSpec((tm, d), lambda j, b, t: (b * NT + t, 0)), pl.BlockSpec((tn, d), lambda j, b, t: (j, 0)),
                  pl.BlockSpec((tm, tn), lambda j, b, t: (b * NT + t, j)),
                  pl.BlockSpec((tm, tn), lambda j, b, t: (b * NT + t, NJ + j)),
                  pl.BlockSpec((3, tn), lambda j, b, t: (0, j)), pl.BlockSpec((1, tn), lambda j, b, t: (0, j))],
        out_specs=[pl.BlockSpec((tm, tn), lambda j, b, t: (b * NT + t, j)),
                   pl.BlockSpec((tm, tn), lambda j, b, t: (b * NT + t, j)),
                   pl.BlockSpec((24, tn), lambda j, b, t: (0, j)), pl.BlockSpec((8, tn), lambda j, b, t: (0, j))],
        out_shape=[S((N, dff), F32), S((N, dff), BF16), S((24, dff), F32), S((8, dff), F32)],
        scratch_shapes=[pltpu.VMEM((8, tn), F32)], compiler_params=_cp(("parallel", "arbitrary", "arbitrary")),
    )(dx2, wd, up, up, cw, cb)


def _ffn_act_bwd2(dac, cw, B, T, tm=256):
    N = B * T
    dff = cw.shape[1]
    NT, NJ, tn = T // tm, dff // FFN_TN, FFN_TN
    last8 = N // 8 - 1

    def body(d_ref, nx_ref, cw_ref, da_ref):
        t = pl.program_id(2)
        dd = d_ref[...]
        row = _iota(dd.shape, 0)
        last = t == NT - 1
        n0 = jnp.where(last, 0.0, nx_ref[0:1, :])
        n1 = jnp.where(last, 0.0, nx_ref[1:2, :])
        d1 = jnp.where(row == tm - 1, n0, pltpu.roll(dd, tm - 1, 0))
        d2 = jnp.where(row == tm - 1, n1, jnp.where(row == tm - 2, n0, pltpu.roll(dd, tm - 2, 0)))
        w = cw_ref[...]
        da_ref[...] = (w[2:3, :] * dd + w[1:2, :] * d1 + w[0:1, :] * d2).astype(BF16)

    return pl.pallas_call(
        body, name="ffn_act_bwd2", grid=(B, NJ, NT),
        in_specs=[pl.BlockSpec((tm, tn), lambda b, j, t: (b * NT + t, j)),
                  pl.BlockSpec((8, tn), lambda b, j, t: (jnp.minimum((b * NT + t + 1) * (tm // 8), last8), j)),
                  pl.BlockSpec((3, tn), lambda b, j, t: (0, j))],
        out_specs=pl.BlockSpec((tm, tn), lambda b, j, t: (b * NT + t, j)), out_shape=S((N, dff), BF16),
        compiler_params=_cp(("parallel", "parallel", "parallel")),
    )(dac, dac, cw)


def _small_reduce(lbl, dg_mix, dg_mem, dlb_p, dgn_p, dfb_p, dgq_p, dgk_p, dmq_p, dmk_p, dg_ffn, dcb_p, loss_p, dcw_p):
    d, dff = dg_mix.shape[1], dcb_p.shape[1]
    nbh = dlb_p.shape[0] // (8 * HG_H)

    def colsum(ref):
        return jnp.sum(ref[...], axis=0, keepdims=True)

    def body(lbl_ref, mix_ref, mem_ref, dlb_ref, dgn_ref, dfb_ref, dgq_ref, dgk_ref, dmq_ref, dmk_ref, ffn_ref, dcb_ref,
             ls_ref, dcw_ref, o_mix, o_mem, o_lb, o_hgn, o_fb, o_fq, o_fk, o_mq, o_mk, o_ffn, o_cb, o_loss, o_cw):
        o_mix[...], o_mem[...], o_ffn[...], o_cb[...] = colsum(mix_ref), colsum(mem_ref), colsum(ffn_ref), colsum(dcb_ref)
        for j in range(3):
            o_cw[j:j + 1, :] = jnp.sum(dcw_ref[8 * j:8 * (j + 1), :], axis=0, keepdims=True)
        o_hgn[...], o_fb[...], o_mq[...], o_mk[...] = colsum(dgn_ref), colsum(dfb_ref), colsum(dmq_ref), colsum(dmk_ref)
        for src, dst in ((dgq_ref, o_fq), (dgk_ref, o_fk)):
            v = colsum(src)
            dst[...] = v + pltpu.roll(v, FOX_D, 1)
        o_loss[...] = jnp.zeros((1, LANE), F32) + jnp.sum(colsum(ls_ref), axis=-1, keepdims=True)
        logits = lbl_ref[...]
        e = jnp.exp(logits - jnp.max(logits, axis=0, keepdims=True))
        pr = e / jnp.sum(e, axis=0, keepdims=True)
        rows = _iota((8, LANE), 0)
        for h in range(HG_H):
            acc = jnp.zeros((8, LANE), F32)
            for b in range(nbh):
                acc = acc + dlb_ref[8 * (b * HG_H + h):8 * (b * HG_H + h + 1), :]
            dlb = jnp.sum(acc, axis=0, keepdims=True)
            c = slice(LANE * h, LANE * (h + 1))
            p0 = pr[0:1, c]
            first = _iota((logits.shape[0], LANE), 0) == 0
            o_lb[:, c] = pr[:, c] * (jnp.where(first, 1.0, 0.0) - p0) * dlb

    outs = [S((1, d), F32), S((1, d), F32), S(lbl.shape, F32)] + [S((1, LANE), F32)] * 6 + \
           [S((1, d), F32), S((1, dff), F32), S((1, LANE), F32), S((3, dff), F32)]
    return pl.pallas_call(body, name="small_reduce", out_shape=outs, compiler_params=_cp())(
        lbl, dg_mix, dg_mem, dlb_p, dgn_p, dfb_p, dgq_p, dgk_p, dmq_p, dmk_p, dg_ffn, dcb_p, loss_p, dcw_p)


def _in_col_pieces():
    hw, fw = HG_H * HG_D, FOX_H * FOX_D
    fox0, ff0 = 4 * hw, 4 * hw + 3 * fw
    mq0 = ff0 + FOX_H
    gate0 = mq0 + MEM_H * MEM_D
    pieces = []
    for p in range(FOX_P):
        pieces += [(fox0 + j * fw + LANE * p, LANE) for j in range(3)]
    pieces.append((mq0, MEM_H * MEM_D))
    for h in range(HG_H):
        pieces += [(j * hw + HG_D * h, HG_D) for j in range(4)]
    pieces.append((gate0, C_FF - C_GATE))
    pieces.append((ff0, FOX_H))
    return pieces


def _perm_cols(w):
    parts = [w[:, s:s + n] for s, n in _in_col_pieces()]
    parts.append(jnp.zeros((w.shape[0], C_END - C_FF - FOX_H), w.dtype))
    return jnp.concatenate(parts, axis=1)


def _unperm_cols(segs):
    starts = [0]
    for a in segs:
        starts.append(starts[-1] + a.shape[1])

    def piece(ns, n):
        i = max(j for j in range(len(segs)) if starts[j] <= ns)
        return segs[i][:, ns - starts[i]:ns - starts[i] + n]

    new_start, placed = 0, []
    for s, n in _in_col_pieces():
        placed.append((s, new_start, n))
        new_start += n
    return jnp.concatenate([piece(ns, n) for _, ns, n in sorted(placed)], axis=1)


def _local_step(x2, mem2, tgt, sm, W, B, T, M):
    fbias = jnp.pad(sm["fox_f_bias"], ((0, 0), (0, LANE - FOX_H)))
    gq2 = jnp.concatenate([sm["fox_q_norm_g"]] * 2, axis=1)
    gk2 = jnp.concatenate([sm["fox_k_norm_g"]] * 2, axis=1)
    lbl = sm["hgrn_lb_logits"]
    h = _rmsnorm_cast(x2, sm["norm_mix_g"], "norm_mix")
    z = _mm_nn(h, W["w_in"], F32, "proj_in", 512, 2432)
    memn = _rmsnorm_cast(mem2, sm["norm_mem_g"], "norm_mem", tm=256)
    memkv = _mm_nn(memn, W["mem_kv_w"], F32, "proj_memkv", 256, 512)
    ya, o_raw, states = _hgrn_fwd(z, lbl, sm["hgrn_norm_g"], B, T)
    fc, fct = _fox_gate_fwd(z, fbias, B, T)
    yb, lse = _fox_fwd(z, fc, fct, gq2, gk2, B, T)
    yc = _mem_fwd(z, memkv, sm["mem_q_norm_g"], sm["mem_k_norm_g"], B, T, M)
    x1, merged, ua, ub, uc = _merge_fwd(ya, yb, yc, z, x2, W["w_br_hgrn"], W["w_br_fox"], W["w_br_mem"], W["w_out"])
    h2 = _rmsnorm_cast(x1, sm["norm_ffn_g"], "norm_ffn")
    up = _mm_nn(h2, W["ffn_w_up"], BF16, "ffn_up", 512, FFN_TN)
    yf = _ffn_act_fwd(up, W["ffn_conv_w"], sm["ffn_conv_b"], B, T)
    dx2, loss_p = _ffn_down_loss(yf, W["ffn_w_down"], x1, tgt)
    dff = W["ffn_conv_w"].shape[1]
    dac, dv, dcw_p, dcb_p = _ffn_act_bwd1(dx2, W["ffn_w_down"], up, W["ffn_conv_w"], sm["ffn_conv_b"], B, T)
    da = _ffn_act_bwd2(dac, W["ffn_conv_w"], B, T)
    g = {}
    g["ffn_w_down"] = _mm_tn(yf, dx2, "g_w_down", TN_TM, 512)
    dh2 = _mm_nt_sum([(da, 0, dff, 0), (dv, 0, dff, dff)], W["ffn_w_up"], "dh2", 256)
    g["ffn_w_up"] = [_mm_tn(h2, da, "g_w_up_a", TN_TM, FFN_TN), _mm_tn(h2, dv, "g_w_up_v", TN_TM, FFN_TN)]
    dx1, dg_ffn = _rmsnorm_bwd(dh2, x1, sm["norm_ffn_g"], dx2, "norm_ffn_bwd")
    g["w_out"] = _mm_tn(merged, dx1, "g_w_out", TN_TM, 512)
    dgate, dya, dyb, dyc, dua, dub, duc = _merge_bwd(dx1, z, ua, ub, uc, W["w_br_hgrn"], W["w_br_fox"], W["w_br_mem"],
                                                    W["w_out"])
    g["w_br_hgrn"] = _mm_tn(ya, dua, "g_w_br_hgrn", TN_TM, 512)
    g["w_br_fox"] = _mm_tn(yb, dub, "g_w_br_fox", TN_TM, 512)
    g["w_br_mem"] = _mm_tn(yc, duc, "g_w_br_mem", TN_TM, 512)
    dz_hg, dlb_p, dgn_p = _hgrn_bwd(z, o_raw, states, dya, lbl, sm["hgrn_norm_g"], B, T)
    dz_fox, dfc, dgq_p, dgk_p = _fox_bwd(z, dyb, yb, lse, fc, fct, gq2, gk2, B, T)
    dz_ff, dfb_p = _fox_gate_bwd(dfc, z, fbias, B, T)
    dz_mq, dkv, dmq_p, dmk_p = _mem_bwd(z, memkv, dyc, sm["mem_q_norm_g"], sm["mem_k_norm_g"], B, T, M)
    g["mem_kv_w"] = _mm_tn(memn, dkv, "g_mem_kv_w", 256, 512)
    dmemn = _mm_nt(dkv, W["mem_kv_w"], "d_memn", 256, 512)
    _, dg_mem = _rmsnorm_bwd(dmemn, mem2, sm["norm_mem_g"], None, "norm_mem_bwd", tm=256)
    d = x2.shape[1]
    parts = [(dz_fox, 0, C_MQ - C_FOX, C_FOX), (dz_mq, 0, C_HG - C_MQ, C_MQ), (dz_hg, 0, C_GATE - C_HG, C_HG)]
    parts += [(dgate, d * k, d, C_GATE + d * k) for k in range(3)] + [(dz_ff, 0, C_END - C_FF, C_FF)]
    dh = _mm_nt_sum(parts, W["w_in"], "dh", 256)
    g["w_in"] = [_mm_tn(h, dzs, "g_w_in_%d" % i, TN_TM, min(512, dzs.shape[1]))
                 for i, dzs in enumerate((dz_fox, dz_mq, dz_hg, dgate, dz_ff))]
    grad_x, dg_mix = _rmsnorm_bwd(dh, x2, sm["norm_mix_g"], dx1, "norm_mix_bwd")
    small = _small_reduce(lbl, dg_mix, dg_mem, dlb_p, dgn_p, dfb_p, dgq_p, dgk_p, dmq_p, dmk_p, dg_ffn, dcb_p, loss_p,
                          dcw_p)
    names = ("norm_mix_g", "norm_mem_g", "hgrn_lb_logits", "hgrn_norm_g", "fox_f_bias", "fox_q_norm_g", "fox_k_norm_g",
             "mem_q_norm_g", "mem_k_norm_g", "norm_ffn_g", "ffn_conv_b", "loss", "ffn_conv_w")
    g.update(dict(zip(names, small)))
    return grad_x, g


ANY = pl.BlockSpec(memory_space=pl.ANY)


def _position():
    return lax.axis_index("x"), lax.axis_index("y"), lax.axis_index("c")


def _all_gather(blocks, name):
    nb = len(blocks)

    def body(*refs):
        x_refs, out_refs = refs[:nb], refs[nb:2 * nb]
        send_sems, recv_sems, local_sems = refs[2 * nb:]
        x, y, c = _position()
        me, sibling = (x, y, c), (x, y, 1 - c)
        chips = [(1 - x, y), (x, 1 - y), (1 - x, 1 - y)]

        def copy(i, k, blk, to, own=False):
            px, py, pc = blk
            slot = out_refs[i].at[4 * px + 2 * py + pc]
            return pltpu.make_async_remote_copy(
                src_ref=x_refs[i] if own else slot, dst_ref=slot, send_sem=send_sems.at[7 * i + k],
                recv_sem=recv_sems.at[7 * i + k], device_id=to, device_id_type=MESH)

        mine = [pltpu.make_async_copy(x_refs[i], out_refs[i].at[4 * x + 2 * y + c], local_sems.at[i]) for i in range(nb)]
        first = []
        for i in range(nb):
            mine[i].start()
            first.append(copy(i, 0, me, sibling, own=True))
            first += [copy(i, 1 + j, me, (*chip, c), own=True) for j, chip in enumerate(chips)]
        for cp in first:
            cp.start()
        passed = []
        for i in range(nb):
            for j, chip in enumerate(chips):
                copy(i, 1 + j, (*chip, c), me).wait_recv()
                passed.append(copy(i, 4 + j, (*chip, c), sibling))
                passed[-1].start()
        for i in range(nb):
            copy(i, 0, sibling, me).wait_recv()
            for j, chip in enumerate(chips):
                copy(i, 4 + j, (*chip, 1 - c), me).wait_recv()
        for cp in first + passed:
            cp.wait_send()
        for cp in mine:
            cp.wait()

    return pl.pallas_call(
        body, name=name, out_shape=[S((N_DEV,) + b.shape, b.dtype) for b in blocks], in_specs=[ANY] * nb,
        out_specs=[ANY] * nb,
        scratch_shapes=[pltpu.SemaphoreType.DMA((7 * nb,)), pltpu.SemaphoreType.DMA((7 * nb,)),
                        pltpu.SemaphoreType.DMA((nb,))],
    )(*blocks)


def _swap_with_sibling(pks):
    nb = len(pks)

    def body(*refs):
        pk_refs, out_refs = refs[:nb], refs[nb:2 * nb]
        send_sems, recv_sems = refs[2 * nb:]
        x, y, c = _position()
        copies = [pltpu.make_async_remote_copy(
            src_ref=pk_refs[i].at[2 * k + 1 - c], dst_ref=out_refs[i].at[k], send_sem=send_sems.at[4 * i + k],
            recv_sem=recv_sems.at[4 * i + k], device_id=(x, y, 1 - c), device_id_type=MESH)
            for i in range(nb) for k in range(4)]
        for cp in copies:
            cp.start()
        for cp in copies:
            cp.wait()

    return pl.pallas_call(
        body, name="rs_sibling", out_shape=[S((4,) + p.shape[1:], p.dtype) for p in pks], in_specs=[ANY] * nb,
        out_specs=[ANY] * nb, scratch_shapes=[pltpu.SemaphoreType.DMA((4 * nb,)), pltpu.SemaphoreType.DMA((4 * nb,))],
    )(*pks)


def _swap_between_chips(pbs):
    nb = len(pbs)

    def body(*refs):
        pb_refs, out_refs = refs[:nb], refs[nb:2 * nb]
        send_sems, recv_sems, local_sems = refs[2 * nb:]
        x, y, c = _position()
        me = 2 * x + y
        chips = [(1 - x, y), (x, 1 - y), (1 - x, 1 - y)]
        local = [pltpu.make_async_copy(pb_refs[i].at[me], out_refs[i].at[me], local_sems.at[i]) for i in range(nb)]
        for cp in local:
            cp.start()
        sends = [pltpu.make_async_remote_copy(
            src_ref=pb_refs[i].at[2 * cx + cy], dst_ref=out_refs[i].at[me], send_sem=send_sems.at[3 * i + j],
            recv_sem=recv_sems.at[3 * i + j], device_id=(cx, cy, c), device_id_type=MESH)
            for i in range(nb) for j, (cx, cy) in enumerate(chips)]
        for cp in sends:
            cp.start()
        for i in range(nb):
            for j, (cx, cy) in enumerate(chips):
                pltpu.make_async_remote_copy(
                    src_ref=pb_refs[i].at[me], dst_ref=out_refs[i].at[2 * cx + cy], send_sem=send_sems.at[3 * i + j],
                    recv_sem=recv_sems.at[3 * i + j], device_id=(cx, cy, c), device_id_type=MESH).wait_recv()
        for cp in sends:
            cp.wait_send()
        for cp in local:
            cp.wait()

    return pl.pallas_call(
        body, name="rs_chips", out_shape=[S(p.shape, p.dtype) for p in pbs], in_specs=[ANY] * nb, out_specs=[ANY] * nb,
        scratch_shapes=[pltpu.SemaphoreType.DMA((3 * nb,)), pltpu.SemaphoreType.DMA((3 * nb,)),
                        pltpu.SemaphoreType.DMA((nb,))],
    )(*pbs)


def _row_tile(r):
    return max(t for t in range(16, min(r, 512) + 1, 16) if r % t == 0)


def _pair_sum_cast(pk, recv, core, name):
    _, r, l = pk.shape
    tr = _row_tile(r)

    def body(c_ref, a_ref, b_ref, o_ref):
        o_ref[...] = (a_ref[...] + b_ref[...]).astype(BF16)

    return pl.pallas_call(
        body, name=name,
        grid_spec=pltpu.PrefetchScalarGridSpec(
            num_scalar_prefetch=1, grid=(4, r // tr),
            in_specs=[pl.BlockSpec((None, tr, l), lambda k, i, c: (2 * k + c[0], i, 0)),
                      pl.BlockSpec((None, tr, l), lambda k, i, c: (k, i, 0))],
            out_specs=pl.BlockSpec((None, tr, l), lambda k, i, c: (k, i, 0))),
        out_shape=S((4, r, l), BF16), compiler_params=_cp(("parallel", "parallel")),
    )(core, pk, recv)


def _final_sum(pk, recv_sib, recv_chips, slot, chip, name):
    _, r, l = pk.shape
    tr = _row_tile(r)

    def body(s_ref, k_ref, a_ref, b_ref, rc_ref, o_ref):
        base = a_ref[...] + b_ref[...]
        acc = jnp.zeros_like(base)
        for j in range(4):
            acc = acc + jnp.where(k_ref[0] == j, base, rc_ref[j].astype(F32))
        o_ref[...] = acc

    return pl.pallas_call(
        body, name=name,
        grid_spec=pltpu.PrefetchScalarGridSpec(
            num_scalar_prefetch=2, grid=(r // tr,),
            in_specs=[pl.BlockSpec((None, tr, l), lambda i, s, k: (s[0], i, 0)),
                      pl.BlockSpec((None, tr, l), lambda i, s, k: (k[0], i, 0)),
                      pl.BlockSpec((4, tr, l), lambda i, s, k: (0, i, 0))],
            out_specs=pl.BlockSpec((tr, l), lambda i, s, k: (i, 0))),
        out_shape=S((r, l), F32), compiler_params=_cp(("parallel",)),
    )(slot, chip, pk, recv_sib, recv_chips)


def _adamw_math(w, g, m, v):
    m = ADAM_B1 * m + (1.0 - ADAM_B1) * g
    v = ADAM_B2 * v + (1.0 - ADAM_B2) * (g * g)
    m_hat = m / (1.0 - ADAM_B1 ** ADAM_STEP)
    v_hat = v / (1.0 - ADAM_B2 ** ADAM_STEP)
    return -ADAM_LR * (m_hat / (jnp.sqrt(v_hat) + ADAM_EPS) + ADAM_WD * w), m, v


def _adamw(w, g, m, v, name):
    r, c = w.shape
    tr = 256 if r % 256 == 0 else r

    def body(w_ref, g_ref, m_ref, v_ref, d_ref, nm_ref, nv_ref):
        d_ref[...], nm_ref[...], nv_ref[...] = _adamw_math(w_ref[...], g_ref[...], m_ref[...], v_ref[...])

    tile = pl.BlockSpec((tr, c), lambda i: (i, 0))
    return pl.pallas_call(
        body, name=name, grid=(r // tr,), in_specs=[tile] * 4, out_specs=[tile] * 3, out_shape=[S((r, c), F32)] * 3,
        compiler_params=_cp(("parallel",)),
    )(w, g, m, v)


def _small_update(gathered, w, m, v):
    def body(ga_ref, w_ref, m_ref, v_ref, g_ref, d_ref, nm_ref, nv_ref):
        g = ga_ref[0]
        for k in range(1, N_DEV):
            g = g + ga_ref[k]
        g_ref[...] = g
        d_ref[...], nm_ref[...], nv_ref[...] = _adamw_math(w_ref[...], g, m_ref[...], v_ref[...])

    return pl.pallas_call(body, name="small_update", out_shape=[S(w.shape, F32)] * 4, compiler_params=_cp())(
        gathered, w, m, v)


BIG = ("w_in", "mem_kv_w", "w_br_hgrn", "w_br_fox", "w_br_mem", "w_out", "ffn_w_up", "ffn_conv_w", "ffn_w_down")
GROUP_ROWS = ("mem_kv_w", "w_out", "ffn_w_down")
GROUP_LANE = ("w_br_hgrn", "w_br_fox", "w_br_mem")
LANE_GROUP_ROWS = 224
SMALL = ("norm_mix_g", "norm_mem_g", "hgrn_lb_logits", "hgrn_norm_g", "fox_f_bias", "fox_q_norm_g", "fox_k_norm_g",
         "mem_q_norm_g", "mem_k_norm_g", "norm_ffn_g", "ffn_conv_b")


def _rows_of(n_elems):
    return -(-n_elems // LANE)


def _to_rows(a, lead=0):
    flat = a.reshape(a.shape[:lead] + (-1,))
    pad = (-flat.shape[-1]) % LANE
    if pad:
        flat = jnp.pad(flat, [(0, 0)] * lead + [(0, pad)])
    return flat.reshape(a.shape[:lead] + (-1, LANE))


def _stack_rows(parts, lead, total_rows):
    buf = jnp.concatenate(parts, axis=lead)
    pad = total_rows - buf.shape[lead]
    return jnp.pad(buf, [(0, 0)] * lead + [(0, pad), (0, 0)])


def _round_up(n, k):
    return -(-n // k) * k


def _from_rows(rows, shape, lead=0):
    n = math.prod(shape)
    return rows.reshape(rows.shape[:lead] + (-1,))[..., :n].reshape(rows.shape[:lead] + tuple(shape))


def _blocks_to_full(blocks, kind):
    n, a, b = blocks.shape
    return blocks.transpose(1, 0, 2).reshape(a, n * b) if kind == "col" else blocks.reshape(n * a, b)


def _full_to_blocks(full, kind, n=N_DEV):
    a, b = full.shape
    return full.reshape(a, n, b // n).transpose(1, 0, 2) if kind == "col" else full.reshape(n, a // n, b)


def _lane_group_rows(shard):
    n_lane = sum(shard[n].shape[0] for n in GROUP_LANE)
    n_cw = shard["ffn_conv_w"].size
    return n_lane, _rows_of(3 * n_cw), _rows_of(n_cw), _round_up(n_lane + _rows_of(3 * n_cw), LANE_GROUP_ROWS)


def _split_bf16x3(x):
    hi = x.astype(BF16)
    r1 = x - hi.astype(F32)
    mid = r1.astype(BF16)
    return jnp.stack([hi, mid, (r1 - mid.astype(F32)).astype(BF16)])


def _pack_weights(shard):
    r_lane = _lane_group_rows(shard)[3]
    lane_rows = [shard[n].astype(BF16) for n in GROUP_LANE] + [_to_rows(_split_bf16x3(shard["ffn_conv_w"]))]
    return [shard["w_in"].astype(BF16), jnp.concatenate([shard[n].astype(BF16) for n in GROUP_ROWS], axis=0),
            shard["ffn_w_up"].astype(BF16), _stack_rows(lane_rows, 0, r_lane)]


def _unpack_weights(gathered, shard):
    ga, gb, gc, gd = gathered
    n_lane, r_words, _, _ = _lane_group_rows(shard)
    W = {"w_in": _perm_cols(_blocks_to_full(ga, "col")), "ffn_w_up": _blocks_to_full(gc, "col")}
    r0 = 0
    for n in GROUP_ROWS:
        W[n] = _blocks_to_full(gb[:, r0:r0 + shard[n].shape[0]], "row")
        r0 += shard[n].shape[0]
    r0 = 0
    for n in GROUP_LANE:
        W[n] = _blocks_to_full(gd[:, r0:r0 + shard[n].shape[0]], "col")
        r0 += shard[n].shape[0]
    cw = _from_rows(gd[:, n_lane:n_lane + r_words], (3,) + shard["ffn_conv_w"].shape, lead=1).astype(F32)
    W["ffn_conv_w"] = _blocks_to_full(cw[:, 0] + cw[:, 1] + cw[:, 2], "col")
    return W


def _pack_grads(g, shard):
    r_lane = _lane_group_rows(shard)[3]
    cw_rows = _to_rows(_full_to_blocks(g["ffn_conv_w"], "col"), lead=1)
    return [_full_to_blocks(_unperm_cols(g["w_in"]), "col"),
            jnp.concatenate([_full_to_blocks(g[n], "row") for n in GROUP_ROWS], axis=1),
            jnp.concatenate([_full_to_blocks(h, "col", N_DEV // 2) for h in g["ffn_w_up"]], axis=0),
            _stack_rows([_full_to_blocks(g[n], "col") for n in GROUP_LANE] + [cw_rows], 1, r_lane)]


def _unpack_grads(sums, shard):
    n_lane, _, r_vals, _ = _lane_group_rows(shard)
    g_shard = {"w_in": sums[0], "ffn_w_up": sums[2]}
    r0 = 0
    for n in GROUP_ROWS:
        g_shard[n] = sums[1][r0:r0 + shard[n].shape[0]]
        r0 += shard[n].shape[0]
    r0 = 0
    for n in GROUP_LANE:
        g_shard[n] = sums[3][r0:r0 + shard[n].shape[0]]
        r0 += shard[n].shape[0]
    g_shard["ffn_conv_w"] = _from_rows(sums[3][n_lane:n_lane + r_vals], shard["ffn_conv_w"].shape)
    return g_shard


def kernel(x, mem, norm_mix_g, norm_mem_g, w_in, hgrn_lb_logits, hgrn_norm_g, fox_f_bias, fox_q_norm_g, fox_k_norm_g, mem_kv_w, mem_q_norm_g, mem_k_norm_g, w_br_hgrn, w_br_fox, w_br_mem, w_out, norm_ffn_g, ffn_w_up, ffn_conv_w, ffn_conv_b, ffn_w_down, loss_target, m_norm_mix_g, m_norm_mem_g, m_w_in, m_hgrn_lb_logits, m_hgrn_norm_g, m_fox_f_bias, m_fox_q_norm_g, m_fox_k_norm_g, m_mem_kv_w, m_mem_q_norm_g, m_mem_k_norm_g, m_w_br_hgrn, m_w_br_fox, m_w_br_mem, m_w_out, m_norm_ffn_g, m_ffn_w_up, m_ffn_conv_w, m_ffn_conv_b, m_ffn_w_down, v_norm_mix_g, v_norm_mem_g, v_w_in, v_hgrn_lb_logits, v_hgrn_norm_g, v_fox_f_bias, v_fox_q_norm_g, v_fox_k_norm_g, v_mem_kv_w, v_mem_q_norm_g, v_mem_k_norm_g, v_w_br_hgrn, v_w_br_fox, v_w_br_mem, v_w_out, v_norm_ffn_g, v_ffn_w_up, v_ffn_conv_w, v_ffn_conv_b, v_ffn_w_down):
    given = dict(locals())
    order = ("norm_mix_g", "norm_mem_g", "w_in", "hgrn_lb_logits", "hgrn_norm_g", "fox_f_bias", "fox_q_norm_g",
             "fox_k_norm_g", "mem_kv_w", "mem_q_norm_g", "mem_k_norm_g", "w_br_hgrn", "w_br_fox", "w_br_mem", "w_out",
             "norm_ffn_g", "ffn_w_up", "ffn_conv_w", "ffn_conv_b", "ffn_w_down")
    B, T, D = x.shape
    M = mem.shape[1]
    shard = {n: given[n][0] if n in BIG else given[n] for n in order}
    mom = {n: (given["m_" + n][0], given["v_" + n][0]) if n in BIG else (given["m_" + n], given["v_" + n])
           for n in order}
    shard["hgrn_lb_logits"] = hgrn_lb_logits
    for n in ("norm_mix_g", "norm_mem_g", "hgrn_norm_g", "fox_f_bias", "fox_q_norm_g", "fox_k_norm_g", "mem_q_norm_g",
              "mem_k_norm_g", "norm_ffn_g", "ffn_conv_b"):
        shard[n] = given[n].reshape(1, -1)

    W = _unpack_weights(_all_gather(_pack_weights(shard), "ag_weights"), shard)

    sm = {n: shard[n] for n in SMALL}
    grad_x, g = _local_step(x.reshape(B * T, D), mem.reshape(B * M, D), loss_target.reshape(B * T, D), sm, W, B, T, M)

    xi, yi, ci = _position()
    pks = _pack_grads(g, shard)
    core = ci.astype(jnp.int32).reshape(1)
    chip = (2 * xi + yi).astype(jnp.int32).reshape(1)
    recv_sib = _swap_with_sibling(pks)
    pair = [_pair_sum_cast(p, r, core, "rs_pair_sum_%d" % i) for i, (p, r) in enumerate(zip(pks, recv_sib))]
    recv_chips = _swap_between_chips(pair)
    g_shard = _unpack_grads([_final_sum(p, rs, rc, 2 * chip + core, chip, "rs_final_sum_%d" % i)
                             for i, (p, rs, rc) in enumerate(zip(pks, recv_sib, recv_chips))], shard)

    sg = {n: g[n] for n in SMALL}
    sg["fox_f_bias"] = g["fox_f_bias"][:, :FOX_H]
    sg["fox_q_norm_g"] = g["fox_q_norm_g"][:, :FOX_D]
    sg["fox_k_norm_g"] = g["fox_k_norm_g"][:, :FOX_D]
    slayout, row0 = {}, 0
    for n in SMALL:
        nr = _rows_of(shard[n].size)
        slayout[n] = (row0, nr)
        row0 += nr
    loss_row = row0
    r_small = _round_up(row0 + 1, 8)

    def pack_small(d, with_loss=None):
        rows = [_to_rows(d[n]) for n in SMALL]
        rows.append(with_loss if with_loss is not None else jnp.zeros((1, LANE), F32))
        return _stack_rows(rows, 0, r_small)

    sgath, = _all_gather([pack_small(sg, g["loss"])], "ag_small")
    s_g, s_d, s_m, s_v = _small_update(sgath, pack_small(shard), pack_small({n: mom[n][0].reshape(shard[n].shape) for n in SMALL}),
                                       pack_small({n: mom[n][1].reshape(shard[n].shape) for n in SMALL}))
    loss = s_g[loss_row, 0]

    grads, deltas, new_m, new_v = {}, {}, {}, {}
    for n in BIG:
        gn = g_shard[n]
        d, nm, nv = _adamw(shard[n], gn, mom[n][0], mom[n][1], "adamw_" + n)
        grads[n], deltas[n], new_m[n], new_v[n] = (a[None] for a in (gn, d, nm, nv))
    for n in SMALL:
        r0, nr = slayout[n]
        for dst, src in ((grads, s_g), (deltas, s_d), (new_m, s_m), (new_v, s_v)):
            dst[n] = _from_rows(src[r0:r0 + nr], given[n].shape)
    return (loss, grad_x.reshape(B, T, D), *[grads[n] for n in order], *[deltas[n] for n in order],
            *[new_m[n] for n in order], *[new_v[n] for n in order])
```

```python
import functools
import math

import jax
import jax.numpy as jnp
from jax import lax
from jax.experimental import pallas as pl
from jax.experimental.pallas import tpu as pltpu

F32, BF16 = jnp.float32, jnp.bfloat16
S = jax.ShapeDtypeStruct
MESH = pl.DeviceIdType.MESH

N_DEV = 8
EPS = 1e-6
LANE = 128
CHUNK = 64
SUB = 16
HG_H, HG_D = 4, 128
HG_GROUP = 2
FOX_H, FOX_D = 8, 64
FOX_P = FOX_H // 2
MEM_H, MEM_D = 4, 128
NEG = -1e30
VMEM_LIMIT = 56 * 2**20

ADAM_LR, ADAM_B1, ADAM_B2, ADAM_EPS, ADAM_WD, ADAM_STEP = 0.001, 0.9, 0.999, 1e-08, 0.01, 10

C_FOX, C_MQ, C_HG, C_GATE, C_FF, C_END = 0, 1536, 2048, 4096, 7168, 7296


def _cp(sem=None):
    return pltpu.CompilerParams(dimension_semantics=sem, vmem_limit_bytes=VMEM_LIMIT)


def _dot(a, b, dims, prec=None):
    return lax.dot_general(a, b, (dims, ((), ())), preferred_element_type=F32, precision=prec)


def _nn(a, b, prec=None):
    return _dot(a, b, ((1,), (0,)), prec)


def _nt(a, b, prec=None):
    return _dot(a, b, ((1,), (1,)), prec)


def _tn(a, b, prec=None):
    return _dot(a, b, ((0,), (0,)), prec)


def _b(x):
    return x.astype(BF16)


def _mm3(fn, a, b):
    ah, bh = _b(a), _b(b)
    return fn(ah, bh) + fn(ah, _b(b - bh.astype(F32))) + fn(_b(a - ah.astype(F32)), bh)


def _iota(shape, dim):
    return lax.broadcasted_iota(jnp.int32, shape, dim)


def _rowsum8(x):
    r, d = x.shape
    return jnp.sum(x.reshape(r // 8, 8, d), axis=0)


def _rmsnorm_cast(x, g, name, tm=512):
    n, d = x.shape

    def body(x_ref, g_ref, o_ref):
        v = x_ref[...]
        r = lax.rsqrt(jnp.mean(v * v, axis=-1, keepdims=True) + EPS)
        o_ref[...] = (v * r * g_ref[...]).astype(BF16)

    return pl.pallas_call(
        body, name=name, grid=(n // tm,),
        in_specs=[pl.BlockSpec((tm, d), lambda i: (i, 0)), pl.BlockSpec((1, d), lambda i: (0, 0))],
        out_specs=pl.BlockSpec((tm, d), lambda i: (i, 0)), out_shape=S((n, d), BF16), compiler_params=_cp(("parallel",)),
    )(x, g)


def _rmsnorm_bwd(dh, x, g, resid, name, tm=512):
    n, d = x.shape
    has_res = resid is not None

    def body(*refs):
        if has_res:
            dh_ref, x_ref, g_ref, r_ref, dx_ref, dg_ref = refs
        else:
            dh_ref, x_ref, g_ref, dx_ref, dg_ref = refs
        v = x_ref[...]
        dhv = dh_ref[...].astype(F32)
        r = lax.rsqrt(jnp.mean(v * v, axis=-1, keepdims=True) + EPS)
        xh = v * r
        u = dhv * g_ref[...]
        dx = r * (u - xh * jnp.mean(u * xh, axis=-1, keepdims=True))
        if has_res:
            dx = dx + r_ref[...]
        dx_ref[...] = dx

        @pl.when(pl.program_id(0) == 0)
        def _():
            dg_ref[...] = jnp.zeros_like(dg_ref)

        dg_ref[...] += _rowsum8(dhv * xh)

    tile = pl.BlockSpec((tm, d), lambda i: (i, 0))
    ins = [tile, tile, pl.BlockSpec((1, d), lambda i: (0, 0))] + ([tile] if has_res else [])
    args = (dh, x, g) + ((resid,) if has_res else ())
    return pl.pallas_call(
        body, name=name, grid=(n // tm,), in_specs=ins,
        out_specs=[tile, pl.BlockSpec((8, d), lambda i: (0, 0))],
        out_shape=[S((n, d), F32), S((8, d), F32)], compiler_params=_cp(("arbitrary",)),
    )(*args)


def _mm_nn(a, b, out_dtype, name, tm, tn, b_col0=0, n_out=None):
    m, k = a.shape
    n_out = b.shape[1] if n_out is None else n_out
    jb = b_col0 // tn
    assert b_col0 % tn == 0 and n_out % tn == 0 and m % tm == 0

    def body(a_ref, b_ref, o_ref):
        o_ref[...] = _nn(a_ref[...].astype(BF16), b_ref[...].astype(BF16)).astype(out_dtype)

    return pl.pallas_call(
        body, name=name, grid=(m // tm, n_out // tn),
        in_specs=[pl.BlockSpec((tm, k), lambda i, j: (i, 0)), pl.BlockSpec((k, tn), lambda i, j: (0, j + jb))],
        out_specs=pl.BlockSpec((tm, tn), lambda i, j: (i, j)), out_shape=S((m, n_out), out_dtype),
        compiler_params=_cp(("parallel", "parallel")),
    )(a, b)


def _mm_nt(dy, w, name, tm, tr, w_col0=0, acc=None):
    m, r = dy.shape
    k = w.shape[0]
    jb = w_col0 // tr
    nr = r // tr
    assert w_col0 % tr == 0 and r % tr == 0 and m % tm == 0
    has_acc = acc is not None

    def body(*refs):
        if has_acc:
            dy_ref, w_ref, acc_ref, o_ref = refs
        else:
            dy_ref, w_ref, o_ref = refs
        part = _nt(dy_ref[...].astype(BF16), w_ref[...].astype(BF16))

        @pl.when(pl.program_id(1) == 0)
        def _():
            o_ref[...] = part + acc_ref[...] if has_acc else part

        @pl.when(pl.program_id(1) > 0)
        def _():
            o_ref[...] += part

    out_tile = pl.BlockSpec((tm, k), lambda i, j: (i, 0))
    ins = [pl.BlockSpec((tm, tr), lambda i, j: (i, j)), pl.BlockSpec((k, tr), lambda i, j: (0, j + jb))]
    args = (dy, w)
    if has_acc:
        ins.append(out_tile)
        args = args + (acc,)
    return pl.pallas_call(
        body, name=name, grid=(m // tm, nr), in_specs=ins, out_specs=out_tile, out_shape=S((m, k), F32),
        input_output_aliases=({2: 0} if has_acc else {}), compiler_params=_cp(("parallel", "arbitrary")),
    )(*args)


def _mm_nt_sum(parts, w, name, tm, swap=()):
    m = parts[0][0].shape[0]
    k = w.shape[0]
    assert m % tm == 0 and all(c % n == 0 and o % n == 0 for _, c, n, o in parts)
    np_ = len(parts)
    nsw = len(swap)
    n_steps = m // tm

    def body(*refs):
        o_ref = refs[2 * np_ + nsw]
        if nsw:
            start, finish = _chip_swap_phases(refs[2 * np_:2 * np_ + nsw], refs[2 * np_ + nsw + 1:2 * np_ + 2 * nsw + 1],
                                              *refs[2 * np_ + 2 * nsw + 1:])
            pl.when(pl.program_id(0) == 0)(start)
        acc = _nt(refs[0][...].astype(BF16), refs[np_][...].astype(BF16))
        for i in range(1, np_):
            acc = acc + _nt(refs[i][...].astype(BF16), refs[np_ + i][...].astype(BF16))
        o_ref[...] = acc
        if nsw:
            pl.when(pl.program_id(0) == n_steps - 1)(finish)

    dy_specs = [pl.BlockSpec((tm, n), functools.partial(lambda i, j: (i, j), j=c // n)) for _, c, n, _ in parts]
    w_specs = [pl.BlockSpec((k, n), functools.partial(lambda i, j: (0, j), j=o // n)) for _, _, n, o in parts]
    out = pl.pallas_call(
        body, name=name, grid=(n_steps,), in_specs=dy_specs + w_specs + [ANY] * nsw,
        out_specs=[pl.BlockSpec((tm, k), lambda i: (i, 0))] + [ANY] * nsw,
        out_shape=[S((m, k), F32)] + [S(p.shape, p.dtype) for p in swap],
        scratch_shapes=_chip_swap_sems(nsw) if nsw else [],
        compiler_params=_cp(("arbitrary",) if nsw else ("parallel",)),
    )(*([p[0] for p in parts] + [w] * np_ + list(swap)))
    return (out[0], out[1:]) if nsw else out[0]


def _mm_tn(x, dy, name, tm, tn):
    m, k = x.shape
    n = dy.shape[1]
    tm = min(tm, m)
    assert m % tm == 0 and n % tn == 0

    def body(x_ref, dy_ref, o_ref):
        part = _tn(x_ref[...].astype(BF16), dy_ref[...].astype(BF16))

        @pl.when(pl.program_id(1) == 0)
        def _():
            o_ref[...] = part

        @pl.when(pl.program_id(1) > 0)
        def _():
            o_ref[...] += part

    return pl.pallas_call(
        body, name=name, grid=(n // tn, m // tm),
        in_specs=[pl.BlockSpec((tm, k), lambda j, i: (i, 0)), pl.BlockSpec((tm, tn), lambda j, i: (i, j))],
        out_specs=pl.BlockSpec((k, tn), lambda j, i: (0, j)), out_shape=S((k, n), F32),
        compiler_params=_cp(("parallel", "arbitrary")),
    )(x, dy)


def _lower_bound(logits):
    e = jnp.exp(logits - jnp.max(logits, axis=0, keepdims=True))
    return e[0:1, :] / jnp.sum(e, axis=0, keepdims=True)


def _hg_gates(fl, lb):
    sig = jax.nn.sigmoid(fl)
    f = lb + (1.0 - lb) * sig
    k = (1.0 - lb) * (1.0 - sig)
    return sig, f, k, jnp.log(f)


def _silu_and_grad(x):
    s = jax.nn.sigmoid(x)
    return x * s, s * (1.0 + x * (1.0 - s))


def _hg_rowblocks(G):
    return [None] + [G[SUB * i - 1:SUB * i, :] for i in range(1, CHUNK // SUB)]


def _hg_intra_A(qs, k, G):
    refs = _hg_rowblocks(G)
    cols = _iota((SUB, CHUNK), 1)
    rows = _iota((SUB, CHUNK), 0)
    blocks = []
    for i in range(CHUNK // SUB):
        lo = SUB * i
        qb, Gb = qs[lo:lo + SUB, :], G[lo:lo + SUB, :]
        diag = jnp.zeros((SUB, CHUNK), F32)
        for s in range(SUB):
            e = jnp.exp(jnp.minimum(Gb - G[lo + s:lo + s + 1, :], 0.0))
            col = jnp.sum(qb * k[lo + s:lo + s + 1, :] * e, axis=-1, keepdims=True)
            diag = jnp.where(cols == lo + s, col, diag)
        a = jnp.where((cols >= lo) & (cols <= rows + lo), diag, 0.0)
        if i > 0:
            qr = qb * jnp.exp(Gb - refs[i])
            kr = k * jnp.exp(jnp.minimum(refs[i] - G, 0.0))
            a = jnp.where(cols < lo, _nt(_b(qr), _b(kr)), a)
        blocks.append(a)
    return jnp.concatenate(blocks, axis=0)


def _hg_intra_bwd(dA, qs, k, G):
    refs = _hg_rowblocks(G)
    cols = _iota((SUB, CHUNK), 1)
    rows16 = _iota((SUB, HG_D), 0)
    dk = jnp.zeros((CHUNK, HG_D), F32)
    dq_blocks, dk_diag_blocks = [], []
    for i in range(CHUNK // SUB):
        lo = SUB * i
        qb, Gb = qs[lo:lo + SUB, :], G[lo:lo + SUB, :]
        dAb = dA[lo:lo + SUB, :]
        dq = jnp.zeros((SUB, HG_D), F32)
        dkb = jnp.zeros((SUB, HG_D), F32)
        for s in range(SUB):
            e = jnp.exp(jnp.minimum(Gb - G[lo + s:lo + s + 1, :], 0.0))
            e = jnp.where(rows16 >= s, e, 0.0)
            dcol = jnp.sum(jnp.where(cols == lo + s, dAb, 0.0), axis=-1, keepdims=True)
            w = dcol * e
            dq = dq + w * k[lo + s:lo + s + 1, :]
            dkb = jnp.where(rows16 == s, jnp.sum(w * qb, axis=0, keepdims=True), dkb)
        if i > 0:
            e1 = jnp.exp(Gb - refs[i])
            e2 = jnp.exp(jnp.minimum(refs[i] - G, 0.0))
            dA_off = jnp.where(cols < lo, dAb, 0.0)
            dq = dq + _mm3(_nn, dA_off, k * e2) * e1
            dk = dk + _mm3(_tn, dA_off, qb * e1) * e2
        dq_blocks.append(dq)
        dk_diag_blocks.append(dkb)
    return jnp.concatenate(dq_blocks, axis=0), dk + jnp.concatenate(dk_diag_blocks, axis=0)


def _tri(n, upper=False):
    r, c = _iota((n, n), 0), _iota((n, n), 1)
    return jnp.where((c >= r) if upper else (r >= c), 1.0, 0.0).astype(BF16)


def _prefix_mm(tri, x):
    hi = x.astype(BF16)
    r1 = x - hi.astype(F32)
    mid = r1.astype(BF16)
    lo = (r1 - mid.astype(F32)).astype(BF16)
    return _nn(tri, hi) + _nn(tri, mid) + _nn(tri, lo)


def _hgrn_fwd(z, lb, gn, B, T, gather=()):
    N = B * T
    NC = T // CHUNK
    nga = len(gather)
    ng = HG_H // HG_GROUP
    n_steps = B * ng

    def body(*refs):
        z_ref, lb_ref, gn_ref = refs[:3]
        y_ref, o_ref, st_ref = refs[3 + nga:6 + nga]
        s_scr = refs[6 + 2 * nga]
        if nga:
            step = pl.program_id(0) * ng + pl.program_id(1)
            start, forward, finish = _gather_phases(refs[3:3 + nga], refs[6 + nga:6 + 2 * nga], *refs[7 + 2 * nga:])
            pl.when(step == 0)(start)
            pl.when(step == n_steps // 2)(forward)
        lbs = _lower_bound(lb_ref[...])
        tri = _tri(CHUNK)
        s_scr[...] = jnp.zeros_like(s_scr)

        def chunk(c, carry):
            r = pl.ds(pl.multiple_of(c * CHUNK, CHUNK), CHUNK)
            for hh in range(HG_GROUP):
                zc, oc = 4 * LANE * hh, LANE * hh
                ql, fl, il, gl = (z_ref[r, zc + LANE * j:zc + LANE * (j + 1)] for j in range(4))
                _, _, k, logf = _hg_gates(fl, lbs[:, oc:oc + LANE])
                G = _prefix_mm(tri, logf)
                qs = ql * jax.nn.sigmoid(ql)
                st = s_scr[hh]
                st_ref[hh * NC + c] = st
                A = _hg_intra_A(qs, k, G)
                ib = _b(il)
                o = _nn(_b(A), ib) + _nt(_b(qs * jnp.exp(G)), _b(st))
                g_last = G[CHUNK - 1:CHUNK, :]
                s_scr[hh] = st * jnp.exp(g_last) + _mm3(_tn, il, k * jnp.exp(g_last - G))
                o_ref[r, oc:oc + LANE] = o
                rstd = lax.rsqrt(jnp.mean(o * o, axis=-1, keepdims=True) + EPS)
                y_ref[r, oc:oc + LANE] = (o * rstd * gn_ref[...] * (gl * jax.nn.sigmoid(gl))).astype(BF16)
            return carry

        lax.fori_loop(0, NC, chunk, 0)
        if nga:
            pl.when(step == n_steps - 1)(finish)

    gw = HG_GROUP * LANE
    cb = C_HG // (4 * gw)
    sem = ("arbitrary", "arbitrary") if nga else ("parallel", "parallel")
    return pl.pallas_call(
        body, name="hgrn_fwd", grid=(B, ng),
        in_specs=[pl.BlockSpec((T, 4 * gw), lambda b, h: (b, cb + h)), pl.BlockSpec((lb.shape[0], gw), lambda b, h: (0, h)),
                  pl.BlockSpec((1, LANE), lambda b, h: (0, 0))] + [ANY] * nga,
        out_specs=[pl.BlockSpec((T, gw), lambda b, h: (b, h)), pl.BlockSpec((T, gw), lambda b, h: (b, h)),
                   pl.BlockSpec((HG_GROUP * NC, HG_D, HG_D), lambda b, h: (b * ng + h, 0, 0))] + [ANY] * nga,
        out_shape=[S((N, 512), BF16), S((N, 512), F32), S((B * HG_H * NC, HG_D, HG_D), F32)] + _gather_shapes(gather),
        scratch_shapes=[pltpu.VMEM((HG_GROUP, HG_D, HG_D), F32)] + (_gather_sems(nga) if nga else []),
        compiler_params=_cp(sem),
    )(z, lb, gn, *gather)


def _hgrn_bwd(z, o_raw, states, dy, lb, gn, B, T, swap=()):
    N = B * T
    NC = T // CHUNK
    nsw = len(swap)
    ng = HG_H // HG_GROUP
    n_steps = B * ng

    def body(*refs):
        z_ref, o_ref, st_ref, dy_ref, lb_ref, gn_ref = refs[:6]
        dz_ref, dlb_ref, dgn_ref = refs[6 + nsw:9 + nsw]
        ds_scr, racc, dgn_acc = refs[9 + 2 * nsw:12 + 2 * nsw]
        if nsw:
            step = pl.program_id(0) * ng + pl.program_id(1)
            start, finish = _chip_swap_phases(refs[6:6 + nsw], refs[9 + nsw:9 + 2 * nsw], *refs[12 + 2 * nsw:])
            pl.when(step == 0)(start)
        lbs = _lower_bound(lb_ref[...])
        gn_v = gn_ref[...]
        tri, triu = _tri(CHUNK), _tri(CHUNK, upper=True)
        cmask = _iota((CHUNK, CHUNK), 0) >= _iota((CHUNK, CHUNK), 1)
        for ref in (ds_scr, racc, dgn_acc, dlb_ref):
            ref[...] = jnp.zeros_like(ref)

        def chunk(ci, carry):
            c = NC - 1 - ci
            r = pl.ds(pl.multiple_of(c * CHUNK, CHUNK), CHUNK)
            for hh in range(HG_GROUP):
                zc, oc = 4 * LANE * hh, LANE * hh
                lb_v = lbs[:, oc:oc + LANE]
                ql, fl, il, gl = (z_ref[r, zc + LANE * j:zc + LANE * (j + 1)] for j in range(4))
                sig, f, k, logf = _hg_gates(fl, lb_v)
                G = _prefix_mm(tri, logf)
                qs, dsilu_q = _silu_and_grad(ql)
                gs, dsilu_g = _silu_and_grad(gl)
                o = o_ref[r, oc:oc + LANE]
                dyv = dy_ref[r, oc:oc + LANE]
                rstd = lax.rsqrt(jnp.mean(o * o, axis=-1, keepdims=True) + EPS)
                oh = o * rstd
                dgl = dyv * oh * gn_v * dsilu_g
                dn = dyv * gs
                dgn_acc[...] += _rowsum8(dn * oh)
                u = dn * gn_v
                do = rstd * (u - oh * jnp.mean(u * oh, axis=-1, keepdims=True))
                st = st_ref[hh * NC + c]
                dst = ds_scr[hh]
                eG = jnp.exp(G)
                g_last = G[CHUNK - 1:CHUNK, :]
                eL = jnp.exp(g_last - G)
                A = _hg_intra_A(qs, k, G)
                dA = jnp.where(cmask, _mm3(_nt, do, il), 0.0)
                di = _tn(_b(A), _b(do)) + _nt(_b(k * eL), _b(dst))
                dq_in, dk_in = _hg_intra_bwd(dA, qs, k, G)
                dq = dq_in + _mm3(_nn, do, st) * eG
                dk = dk_in + _mm3(_nn, il, dst) * eL
                ds_scr[hh] = dst * jnp.exp(g_last) + _mm3(_tn, do, qs * eG)
                dd = qs * dq - k * dk
                dlogf = _prefix_mm(triu, dd) + racc[hh]
                racc[hh] += jnp.sum(dd, axis=0, keepdims=True)
                df = dlogf / f - dk
                dlb_ref[8 * hh:8 * (hh + 1), :] += _rowsum8(df * (1.0 - sig))
                dz_ref[r, zc:zc + LANE] = (dq * dsilu_q).astype(BF16)
                dz_ref[r, zc + LANE:zc + 2 * LANE] = (df * (1.0 - lb_v) * sig * (1.0 - sig)).astype(BF16)
                dz_ref[r, zc + 2 * LANE:zc + 3 * LANE] = di.astype(BF16)
                dz_ref[r, zc + 3 * LANE:zc + 4 * LANE] = dgl.astype(BF16)
            return carry

        lax.fori_loop(0, NC, chunk, 0)
        dgn_ref[...] = dgn_acc[...]
        if nsw:
            pl.when(step == n_steps - 1)(finish)

    gw = HG_GROUP * LANE
    cb = C_HG // (4 * gw)
    col = pl.BlockSpec((T, gw), lambda b, h: (b, h))
    sem = ("arbitrary", "arbitrary") if nsw else ("parallel", "parallel")
    return pl.pallas_call(
        body, name="hgrn_bwd", grid=(B, ng),
        in_specs=[pl.BlockSpec((T, 4 * gw), lambda b, h: (b, cb + h)), col,
                  pl.BlockSpec((HG_GROUP * NC, HG_D, HG_D), lambda b, h: (b * ng + h, 0, 0)), col,
                  pl.BlockSpec((lb.shape[0], gw), lambda b, h: (0, h)), pl.BlockSpec((1, LANE), lambda b, h: (0, 0))]
        + [ANY] * nsw,
        out_specs=[pl.BlockSpec((T, 4 * gw), lambda b, h: (b, h)),
                   pl.BlockSpec((8 * HG_GROUP, LANE), lambda b, h: (b * ng + h, 0)),
                   pl.BlockSpec((8, LANE), lambda b, h: (b * ng + h, 0))] + [ANY] * nsw,
        out_shape=[S((N, 2048), BF16), S((B * HG_H * 8, LANE), F32), S((B * ng * 8, LANE), F32)]
        + [S(p.shape, p.dtype) for p in swap],
        scratch_shapes=[pltpu.VMEM((HG_GROUP, HG_D, HG_D), F32), pltpu.VMEM((HG_GROUP, 1, LANE), F32),
                        pltpu.VMEM((8, LANE), F32)] + (_chip_swap_sems(nsw) if nsw else []),
        compiler_params=_cp(sem),
    )(z, o_raw, states, dy, lb, gn, *swap)


def _pair_mean(x, lo_half):
    a = jnp.sum(jnp.where(lo_half, x, 0.0), axis=-1, keepdims=True)
    b = jnp.sum(jnp.where(lo_half, 0.0, x), axis=-1, keepdims=True)
    return jnp.where(lo_half, a, b) * (1.0 / FOX_D)


def _fox_gate_fwd(z, bias, B, T):
    N = B * T
    tb = LANE

    def body(z_ref, b_ref, fc_ref, fct_ref):
        tri = _tri(tb)

        def step(i, carry):
            r = pl.ds(pl.multiple_of(i * tb, tb), tb)
            cs = _prefix_mm(tri, jax.nn.log_sigmoid(z_ref[r, :] + b_ref[...])) + carry
            fc_ref[r, :] = cs
            fct_ref[0, :, r] = cs.T[0:8, :]
            return cs[tb - 1:tb, :]

        lax.fori_loop(0, T // tb, step, jnp.zeros((1, LANE), F32))

    return pl.pallas_call(
        body, name="fox_gate_fwd", grid=(B,),
        in_specs=[pl.BlockSpec((T, LANE), lambda b: (b, C_FF // LANE)), pl.BlockSpec((1, LANE), lambda b: (0, 0))],
        out_specs=[pl.BlockSpec((T, LANE), lambda b: (b, 0)), pl.BlockSpec((1, 8, T), lambda b: (b, 0, 0))],
        out_shape=[S((N, LANE), F32), S((B, 8, T), F32)], compiler_params=_cp(("parallel",)),
    )(z, bias)


def _fox_gate_bwd(dfc, z, bias, B, T):
    N = B * T
    tb = LANE
    nt = T // tb

    def body(d_ref, z_ref, b_ref, dz_ref, db_ref):
        triu = _tri(tb, upper=True)
        db_ref[...] = jnp.zeros_like(db_ref)

        def step(ii, carry):
            r = pl.ds(pl.multiple_of((nt - 1 - ii) * tb, tb), tb)
            d = d_ref[r, 0:LANE]
            for p in range(1, FOX_P):
                d = d + d_ref[r, LANE * p:LANE * (p + 1)]
            rc = _prefix_mm(triu, d) + carry
            dff = rc * jax.nn.sigmoid(-(z_ref[r, :] + b_ref[...]))
            dz_ref[r, :] = dff.astype(BF16)
            db_ref[...] += _rowsum8(dff)
            return carry + jnp.sum(d, axis=0, keepdims=True)

        lax.fori_loop(0, nt, step, jnp.zeros((1, LANE), F32))

    return pl.pallas_call(
        body, name="fox_gate_bwd", grid=(B,),
        in_specs=[pl.BlockSpec((T, 512), lambda b: (b, 0)), pl.BlockSpec((T, LANE), lambda b: (b, C_FF // LANE)),
                  pl.BlockSpec((1, LANE), lambda b: (0, 0))],
        out_specs=[pl.BlockSpec((T, LANE), lambda b: (b, 0)), pl.BlockSpec((8, LANE), lambda b: (b, 0))],
        out_shape=[S((N, LANE), BF16), S((B * 8, LANE), F32)], compiler_params=_cp(("parallel",)),
    )(dfc, z, bias)


def _fox_prep(z_ref, gq, gk, r, lo_half):
    q, k, v = z_ref[r, 0:LANE], z_ref[r, LANE:2 * LANE], z_ref[r, 2 * LANE:3 * LANE]
    rq = lax.rsqrt(_pair_mean(q * q, lo_half) + EPS)
    rk = lax.rsqrt(_pair_mean(k * k, lo_half) + EPS)
    qh, kh = q * rq, k * rk
    return qh * gq * (FOX_D ** -0.5), kh * gk, v, qh, kh, rq, rk


def _fox_fwd(z, fc, fct, gq, gk, B, T, tq=512):
    N = B * T
    NQ = T // tq

    def body(z_ref, fc_ref, fct_ref, gq_ref, gk_ref, y_ref, lse_ref, qn_s, kn_s, v_s):
        p, qi = pl.program_id(1), pl.program_id(2)
        lo_half = _iota((1, LANE), 1) < FOX_D

        @pl.when(qi == 0)
        def _():
            def prep(i, carry):
                r = pl.ds(pl.multiple_of(i * tq, tq), tq)
                qn, kn, v = _fox_prep(z_ref, gq_ref[...], gk_ref[...], r, lo_half)[:3]
                qn_s[r, :], kn_s[r, :], v_s[r, :] = qn.astype(BF16), kn.astype(BF16), v.astype(BF16)
                return carry
            lax.fori_loop(0, NQ, prep, 0)

        rq = pl.ds(pl.multiple_of(qi * tq, tq), tq)
        qn = qn_s[rq, :]
        fcq = fc_ref[rq, :]
        lane = _iota((tq, LANE), 1)
        causal = _iota((tq, tq), 0) >= _iota((tq, tq), 1)
        qhs = [jnp.where(lo_half, qn, jnp.zeros_like(qn)), jnp.where(lo_half, jnp.zeros_like(qn), qn)]
        fqs = [jnp.sum(jnp.where(lane == 2 * p + hh, fcq, 0.0), axis=-1, keepdims=True) for hh in range(2)]

        def kv(j, carry, diagonal):
            rk = pl.ds(pl.multiple_of(j * tq, tq), tq)
            kj, vj = kn_s[rk, :], v_s[rk, :]
            new = []
            for hh in range(2):
                m, l, acc = carry[hh]
                s = _nt(qhs[hh], kj) + fqs[hh] - fct_ref[0, pl.ds(2 * p + hh, 1), rk]
                if diagonal:
                    s = jnp.where(causal, s, NEG)
                m_new = jnp.maximum(m, jnp.max(s, axis=-1, keepdims=True))
                pe = jnp.exp(s - m_new)
                alpha = jnp.exp(m - m_new)
                new.append((m_new, alpha * l + jnp.sum(pe, axis=-1, keepdims=True),
                            alpha * acc + _nn(pe.astype(BF16), vj)))
            return tuple(new)

        init = tuple((jnp.full((tq, 1), NEG, F32), jnp.zeros((tq, 1), F32), jnp.zeros((tq, LANE), F32)) for _ in range(2))
        carry = lax.fori_loop(0, qi, functools.partial(kv, diagonal=False), init)
        (m0, l0, a0), (m1, l1, a1) = kv(qi, carry, True)
        y_ref[...] = jnp.where(lo_half, a0 / l0, a1 / l1).astype(BF16)
        lse_ref[...] = jnp.where(lo_half, m0 + jnp.log(l0), m1 + jnp.log(l1))

    vec = pl.BlockSpec((1, LANE), lambda b, p, q: (0, 0))
    tile = pl.BlockSpec((tq, LANE), lambda b, p, q: (b * NQ + q, p))
    return pl.pallas_call(
        body, name="fox_fwd", grid=(B, FOX_P, NQ),
        in_specs=[pl.BlockSpec((T, 384), lambda b, p, q: (b, p)), pl.BlockSpec((T, LANE), lambda b, p, q: (b, 0)),
                  pl.BlockSpec((1, 8, T), lambda b, p, q: (b, 0, 0)), vec, vec],
        out_specs=[tile, tile], out_shape=[S((N, 512), BF16), S((N, 512), F32)],
        scratch_shapes=[pltpu.VMEM((T, LANE), BF16)] * 3,
        compiler_params=_cp(("parallel", "parallel", "arbitrary")),
    )(z, fc, fct, gq, gk)


def _fox_bwd(z, dy, y, lse, fc, fct, gq, gk, B, T, tq=512):
    N = B * T
    NQ = T // tq

    def body(z_ref, dy_ref, y_ref, lse_ref, fc_ref, fct_ref, gq_ref, gk_ref, dz_ref, dfc_ref, dgq_ref, dgk_ref,
             qn_s, kn_s, v_s, do_s, delta_s, dq_s, dfk_s):
        p, kj = pl.program_id(1), pl.program_id(2)
        lo_half = _iota((1, LANE), 1) < FOX_D
        lane = _iota((tq, LANE), 1)
        gq_v, gk_v = gq_ref[...], gk_ref[...]

        @pl.when(kj == 0)
        def _():
            def prep(i, carry):
                r = pl.ds(pl.multiple_of(i * tq, tq), tq)
                qn, kn, v = _fox_prep(z_ref, gq_v, gk_v, r, lo_half)[:3]
                qn_s[r, :], kn_s[r, :], v_s[r, :] = qn.astype(BF16), kn.astype(BF16), v.astype(BF16)
                do = dy_ref[r, :]
                do_s[r, :] = do.astype(BF16)
                delta_s[r, :] = _pair_mean(do * y_ref[r, :].astype(F32), lo_half) * float(FOX_D)
                return carry
            lax.fori_loop(0, NQ, prep, 0)
            dq_s[...] = jnp.zeros_like(dq_s)
            dgq_ref[...] = jnp.zeros_like(dgq_ref)
            dgk_ref[...] = jnp.zeros_like(dgk_ref)

        rk = pl.ds(pl.multiple_of(kj * tq, tq), tq)
        kn, vv = kn_s[rk, :], v_s[rk, :]
        causal = _iota((tq, tq), 0) >= _iota((tq, tq), 1)
        zero, one = jnp.zeros_like(kn), jnp.ones_like(kn)
        hms = [lo_half, jnp.logical_not(lo_half)]
        kmasks = [jnp.where(hm, kn, zero) for hm in hms]
        kaugs = [jnp.where(hm, kn, one) for hm in hms]
        vmasks = [jnp.where(hm, vv, zero) for hm in hms]
        fks = [fct_ref[0, pl.ds(2 * p + hh, 1), rk] for hh in range(2)]

        def qloop(i, carry, diagonal):
            ri = pl.ds(pl.multiple_of(i * tq, tq), tq)
            qn = qn_s[ri, :]
            do = do_s[ri, :]
            fcq = fc_ref[ri, :]
            new = []
            for hh in range(2):
                dk_acc, dv_acc = carry[hh]
                c0 = FOX_D * hh
                fq = jnp.sum(jnp.where(lane == 2 * p + hh, fcq, 0.0), axis=-1, keepdims=True)
                pr = jnp.exp(_nt(qn, kmasks[hh]) + fq - fks[hh] - lse_ref[ri, c0:c0 + 1])
                if diagonal:
                    pr = jnp.where(causal, pr, 0.0)
                ds = (pr * (_nt(do, vmasks[hh]) - delta_s[ri, c0:c0 + 1])).astype(BF16)
                dq_s[hh, ri, :] += _nn(ds, kaugs[hh])
                new.append((dk_acc + _tn(jnp.where(hms[hh], qn, one), ds), dv_acc + _tn(do, pr.astype(BF16))))
            return tuple(new)

        init = tuple((jnp.zeros((LANE, tq), F32), jnp.zeros((LANE, tq), F32)) for _ in range(2))
        carry = qloop(kj, init, True)
        (dk0, dv0), (dk1, dv1) = lax.fori_loop(kj + 1, NQ, functools.partial(qloop, diagonal=False), carry)
        dks, dvs = [dk0.T, dk1.T], [dv0.T, dv1.T]

        dkn = jnp.where(lo_half, dks[0], dks[1])
        _, _, _, _, kh, _, rkk = _fox_prep(z_ref, gq_v, gk_v, rk, lo_half)
        u = dkn * gk_v
        dz_ref[rk, LANE:2 * LANE] = (rkk * (u - kh * _pair_mean(u * kh, lo_half))).astype(BF16)
        dz_ref[rk, 2 * LANE:3 * LANE] = jnp.where(lo_half, dvs[0], dvs[1]).astype(BF16)
        dgk_ref[...] += _rowsum8(dkn * kh)
        dfk_s[rk, :] = jnp.where(lane == 2 * p, -dks[0][:, FOX_D:FOX_D + 1],
                                 jnp.where(lane == 2 * p + 1, -dks[1][:, 0:1], 0.0))

        @pl.when(kj == NQ - 1)
        def _():
            def fin(i, carry):
                r = pl.ds(pl.multiple_of(i * tq, tq), tq)
                d0, d1 = dq_s[0, r, :], dq_s[1, r, :]
                dqn = jnp.where(lo_half, d0, d1)
                _, _, _, qh, _, rqq, _ = _fox_prep(z_ref, gq_v, gk_v, r, lo_half)
                u = dqn * gq_v * (FOX_D ** -0.5)
                dz_ref[r, 0:LANE] = (rqq * (u - qh * _pair_mean(u * qh, lo_half))).astype(BF16)
                dgq_ref[...] += _rowsum8(dqn * qh) * (FOX_D ** -0.5)
                dfc_ref[r, :] = dfk_s[r, :] + jnp.where(lane == 2 * p, d0[:, FOX_D:FOX_D + 1],
                                                        jnp.where(lane == 2 * p + 1, d1[:, 0:1], 0.0))
                return carry
            lax.fori_loop(0, NQ, fin, 0)

    vec = pl.BlockSpec((1, LANE), lambda b, p, k: (0, 0))
    col = pl.BlockSpec((T, LANE), lambda b, p, k: (b, p))
    part = pl.BlockSpec((8, LANE), lambda b, p, k: (b * FOX_P + p, 0))
    return pl.pallas_call(
        body, name="fox_bwd", grid=(B, FOX_P, NQ),
        in_specs=[pl.BlockSpec((T, 384), lambda b, p, k: (b, p)), col, col, col,
                  pl.BlockSpec((T, LANE), lambda b, p, k: (b, 0)), pl.BlockSpec((1, 8, T), lambda b, p, k: (b, 0, 0)),
                  vec, vec],
        out_specs=[pl.BlockSpec((T, 384), lambda b, p, k: (b, p)), col, part, part],
        out_shape=[S((N, 1536), BF16), S((N, 512), F32), S((B * FOX_P * 8, LANE), F32), S((B * FOX_P * 8, LANE), F32)],
        scratch_shapes=[pltpu.VMEM((T, LANE), BF16)] * 4 + [pltpu.VMEM((T, LANE), F32), pltpu.VMEM((2, T, LANE), F32),
                                                            pltpu.VMEM((T, LANE), F32)],
        compiler_params=_cp(("parallel", "parallel", "arbitrary")),
    )(z, dy, y, lse, fc, fct, gq, gk)


def _mem_scores(z_ref, kv_ref, gq, gk, h):
    c = slice(MEM_D * h, MEM_D * (h + 1))
    q, k = z_ref[:, c], kv_ref[:, c]
    rq = lax.rsqrt(jnp.mean(q * q, axis=-1, keepdims=True) + EPS)
    rk = lax.rsqrt(jnp.mean(k * k, axis=-1, keepdims=True) + EPS)
    qh, kh = q * rq, k * rk
    qn = (qh * gq * (MEM_D ** -0.5)).astype(BF16)
    kn = (kh * gk).astype(BF16)
    s = _nt(qn, kn)
    pe = jnp.exp(s - jnp.max(s, axis=-1, keepdims=True))
    pn = pe / jnp.sum(pe, axis=-1, keepdims=True)
    return pn, qn, kn, qh, kh, rq, rk


def _mem_fwd(z, memkv, gq, gk, B, T, M, tq=512):
    N = B * T
    NQ = T // tq
    W = MEM_H * MEM_D

    def body(z_ref, kv_ref, gq_ref, gk_ref, y_ref):
        for h in range(MEM_H):
            pn = _mem_scores(z_ref, kv_ref, gq_ref[...], gk_ref[...], h)[0]
            v = kv_ref[:, W + MEM_D * h:W + MEM_D * (h + 1)].astype(BF16)
            y_ref[:, MEM_D * h:MEM_D * (h + 1)] = _nn(pn.astype(BF16), v).astype(BF16)

    vec = pl.BlockSpec((1, LANE), lambda b, q: (0, 0))
    return pl.pallas_call(
        body, name="mem_fwd", grid=(B, NQ),
        in_specs=[pl.BlockSpec((tq, W), lambda b, q: (b * NQ + q, C_MQ // W)),
                  pl.BlockSpec((M, 2 * W), lambda b, q: (b, 0)), vec, vec],
        out_specs=pl.BlockSpec((tq, W), lambda b, q: (b * NQ + q, 0)), out_shape=S((N, W), BF16),
        compiler_params=_cp(("parallel", "parallel")),
    )(z, memkv, gq, gk)


def _mem_bwd(z, memkv, dy, gq, gk, B, T, M, tq=512):
    N = B * T
    NQ = T // tq
    W = MEM_H * MEM_D

    def body(z_ref, kv_ref, dy_ref, gq_ref, gk_ref, dz_ref, dkv_ref, dgq_ref, dgk_ref, acc):
        qi = pl.program_id(1)
        gq_v, gk_v = gq_ref[...], gk_ref[...]

        @pl.when(qi == 0)
        def _():
            acc[...] = jnp.zeros_like(acc)
            dgq_ref[...] = jnp.zeros_like(dgq_ref)
            dgk_ref[...] = jnp.zeros_like(dgk_ref)

        for h in range(MEM_H):
            c = slice(MEM_D * h, MEM_D * (h + 1))
            cv = slice(W + MEM_D * h, W + MEM_D * (h + 1))
            pn, qn, kn, qh, _, rq, _ = _mem_scores(z_ref, kv_ref, gq_v, gk_v, h)
            do = dy_ref[:, c].astype(BF16)
            dp = _nt(do, kv_ref[:, cv].astype(BF16))
            ds = (pn * (dp - jnp.sum(dp * pn, axis=-1, keepdims=True))).astype(BF16)
            dqn = _nn(ds, kn)
            acc[:, c] += _tn(ds, qn)
            acc[:, cv] += _tn(pn.astype(BF16), do)
            u = dqn * gq_v * (MEM_D ** -0.5)
            dz_ref[:, c] = (rq * (u - qh * jnp.mean(u * qh, axis=-1, keepdims=True))).astype(BF16)
            dgq_ref[...] += _rowsum8(dqn * qh) * (MEM_D ** -0.5)

        @pl.when(qi == NQ - 1)
        def _():
            for h in range(MEM_H):
                c = slice(MEM_D * h, MEM_D * (h + 1))
                cv = slice(W + MEM_D * h, W + MEM_D * (h + 1))
                k = kv_ref[:, c]
                rk = lax.rsqrt(jnp.mean(k * k, axis=-1, keepdims=True) + EPS)
                kh = k * rk
                dkn = acc[:, c]
                u = dkn * gk_v
                dkv_ref[:, c] = (rk * (u - kh * jnp.mean(u * kh, axis=-1, keepdims=True))).astype(BF16)
                dkv_ref[:, cv] = acc[:, cv].astype(BF16)
                dgk_ref[...] += _rowsum8(dkn * kh)

    vec = pl.BlockSpec((1, LANE), lambda b, q: (0, 0))
    part = pl.BlockSpec((8, LANE), lambda b, q: (b, 0))
    return pl.pallas_call(
        body, name="mem_bwd", grid=(B, NQ),
        in_specs=[pl.BlockSpec((tq, W), lambda b, q: (b * NQ + q, C_MQ // W)),
                  pl.BlockSpec((M, 2 * W), lambda b, q: (b, 0)), pl.BlockSpec((tq, W), lambda b, q: (b * NQ + q, 0)),
                  vec, vec],
        out_specs=[pl.BlockSpec((tq, W), lambda b, q: (b * NQ + q, 0)), pl.BlockSpec((M, 2 * W), lambda b, q: (b, 0)),
                   part, part],
        out_shape=[S((N, W), BF16), S((B * M, 2 * W), BF16), S((B * 8, LANE), F32), S((B * 8, LANE), F32)],
        scratch_shapes=[pltpu.VMEM((M, 2 * W), F32)], compiler_params=_cp(("parallel", "arbitrary")),
    )(z, memkv, dy, gq, gk)


def _merge_fwd(ya, yb, yc, z, x, wa, wb, wc, wo, tm=256):
    n, d = x.shape
    wdt = ya.shape[1]
    gb = C_GATE // d

    def body(ya_ref, yb_ref, yc_ref, g0_ref, g1_ref, g2_ref, x_ref, wa_ref, wb_ref, wc_ref, wo_ref,
             x1_ref, mg_ref, ua_ref, ub_ref, uc_ref):
        merged = jnp.zeros((tm, d), F32)
        for y_ref, g_ref, w_ref, u_ref in ((ya_ref, g0_ref, wa_ref, ua_ref), (yb_ref, g1_ref, wb_ref, ub_ref),
                                           (yc_ref, g2_ref, wc_ref, uc_ref)):
            u = _nn(y_ref[...], w_ref[...])
            u_ref[...] = u.astype(BF16)
            merged = merged + jax.nn.sigmoid(g_ref[...]) * u
        mb = merged.astype(BF16)
        mg_ref[...] = mb
        x1_ref[...] = x_ref[...] + _nn(mb, wo_ref[...])

    yt = pl.BlockSpec((tm, wdt), lambda i: (i, 0))
    xt = pl.BlockSpec((tm, d), lambda i: (i, 0))
    wbr = pl.BlockSpec((wdt, d), lambda i: (0, 0))
    gates = [pl.BlockSpec((tm, d), functools.partial(lambda i, k: (i, gb + k), k=k)) for k in range(3)]
    return pl.pallas_call(
        body, name="merge_fwd", grid=(n // tm,),
        in_specs=[yt, yt, yt] + gates + [xt, wbr, wbr, wbr, pl.BlockSpec((d, d), lambda i: (0, 0))],
        out_specs=[xt] * 5, out_shape=[S((n, d), F32)] + [S((n, d), BF16)] * 4, compiler_params=_cp(("parallel",)),
    )(ya, yb, yc, z, z, z, x, wa, wb, wc, wo)


def _merge_bwd(dx1, z, ua, ub, uc, wa, wb, wc, wo, tm=256):
    n, d = dx1.shape
    wdt = wa.shape[0]
    gb = C_GATE // d

    def body(dx_ref, g0_ref, g1_ref, g2_ref, ua_ref, ub_ref, uc_ref, wa_ref, wb_ref, wc_ref, wo_ref,
             dg_ref, dya_ref, dyb_ref, dyc_ref, dua_ref, dub_ref, duc_ref):
        dm = _nt(dx_ref[...].astype(BF16), wo_ref[...])
        for k, (g_ref, u_ref, w_ref, dy_ref, du_ref) in enumerate((
                (g0_ref, ua_ref, wa_ref, dya_ref, dua_ref), (g1_ref, ub_ref, wb_ref, dyb_ref, dub_ref),
                (g2_ref, uc_ref, wc_ref, dyc_ref, duc_ref))):
            g = jax.nn.sigmoid(g_ref[...])
            du = (dm * g).astype(BF16)
            du_ref[...] = du
            dg_ref[:, d * k:d * (k + 1)] = (dm * u_ref[...].astype(F32) * g * (1.0 - g)).astype(BF16)
            dy_ref[...] = _nt(du, w_ref[...])

    yt = pl.BlockSpec((tm, wdt), lambda i: (i, 0))
    xt = pl.BlockSpec((tm, d), lambda i: (i, 0))
    wbr = pl.BlockSpec((wdt, d), lambda i: (0, 0))
    gates = [pl.BlockSpec((tm, d), functools.partial(lambda i, k: (i, gb + k), k=k)) for k in range(3)]
    return pl.pallas_call(
        body, name="merge_bwd", grid=(n // tm,),
        in_specs=[xt] + gates + [xt, xt, xt, wbr, wbr, wbr, pl.BlockSpec((d, d), lambda i: (0, 0))],
        out_specs=[pl.BlockSpec((tm, 3 * d), lambda i: (i, 0)), yt, yt, yt, xt, xt, xt],
        out_shape=[S((n, 3 * d), BF16)] + [S((n, wdt), F32)] * 3 + [S((n, d), BF16)] * 3,
        compiler_params=_cp(("parallel",)),
    )(dx1, z, z, z, ua, ub, uc, wa, wb, wc, wo)


FFN_TN = 1408
TN_TM = 2048
INV_SQRT2 = 0.7071067811865476
INV_SQRT_2PI = 0.3989422804014327


def _conv_shifted(a, prev, first, tm):
    row = _iota(a.shape, 0)
    p7 = jnp.where(first, 0.0, prev[7:8, :])
    p6 = jnp.where(first, 0.0, prev[6:7, :])
    a1 = jnp.where(row == 0, p7, pltpu.roll(a, 1, 0))
    a2 = jnp.where(row == 0, p6, jnp.where(row == 1, p7, pltpu.roll(a, 2, 0)))
    return a1, a2


def _ffn_act_fwd(up, cw, cb, B, T, tm=256):
    N = B * T
    dff = cw.shape[1]
    NT, NJ, tn = T // tm, dff // FFN_TN, FFN_TN

    def body(a_ref, v_ref, cw_ref, cb_ref, y_ref, carry):
        t = pl.program_id(2)
        a = a_ref[...].astype(F32)
        a1, a2 = _conv_shifted(a, carry[...], t == 0, tm)
        w = cw_ref[...]
        ac = w[0:1, :] * a2 + w[1:2, :] * a1 + w[2:3, :] * a + cb_ref[...]
        y_ref[...] = (0.5 * ac * (1.0 + lax.erf(ac * INV_SQRT2)) * v_ref[...].astype(F32)).astype(BF16)
        carry[...] = a[tm - 8:tm, :]

    return pl.pallas_call(
        body, name="ffn_act_fwd", grid=(B, NJ, NT),
        in_specs=[pl.BlockSpec((tm, tn), lambda b, j, t: (b * NT + t, j)),
                  pl.BlockSpec((tm, tn), lambda b, j, t: (b * NT + t, NJ + j)),
                  pl.BlockSpec((3, tn), lambda b, j, t: (0, j)), pl.BlockSpec((1, tn), lambda b, j, t: (0, j))],
        out_specs=pl.BlockSpec((tm, tn), lambda b, j, t: (b * NT + t, j)), out_shape=S((N, dff), BF16),
        scratch_shapes=[pltpu.VMEM((8, tn), F32)], compiler_params=_cp(("parallel", "parallel", "arbitrary")),
    )(up, up, cw, cb)


def _ffn_down_loss(y, wd, x1, tgt, tm=256):
    n, d = x1.shape
    kf = y.shape[1]

    def body(y_ref, w_ref, x_ref, t_ref, dx_ref, ls_ref):
        err = x_ref[...] + _nn(y_ref[...], w_ref[...]) - t_ref[...]
        dx_ref[...] = err * (1.0 / d)

        @pl.when(pl.program_id(0) == 0)
        def _():
            ls_ref[...] = jnp.zeros_like(ls_ref)

        ls_ref[...] += _rowsum8(err * err) * (0.5 / d)

    xt = pl.BlockSpec((tm, d), lambda i: (i, 0))
    return pl.pallas_call(
        body, name="ffn_down_loss", grid=(n // tm,),
        in_specs=[pl.BlockSpec((tm, kf), lambda i: (i, 0)), pl.BlockSpec((kf, d), lambda i: (0, 0)), xt, xt],
        out_specs=[xt, pl.BlockSpec((8, d), lambda i: (0, 0))], out_shape=[S((n, d), F32), S((8, d), F32)],
        compiler_params=_cp(("arbitrary",)),
    )(y, wd, x1, tgt)


def _ffn_act_bwd1(dx2, wd, up, cw, cb, B, T, tm=256):
    N = B * T
    d = dx2.shape[1]
    dff = cw.shape[1]
    NT, NJ, tn = T // tm, dff // FFN_TN, FFN_TN

    def body(dx_ref, w_ref, a_ref, v_ref, cw_ref, cb_ref, dac_ref, dv_ref, dcw_ref, dcb_ref, carry):
        b, t = pl.program_id(1), pl.program_id(2)
        a = a_ref[...].astype(F32)
        a1, a2 = _conv_shifted(a, carry[...], t == 0, tm)
        carry[...] = a[tm - 8:tm, :]
        w = cw_ref[...]
        ac = w[0:1, :] * a2 + w[1:2, :] * a1 + w[2:3, :] * a + cb_ref[...]
        dy = _nt(dx_ref[...].astype(BF16), w_ref[...])
        cdf = 0.5 * (1.0 + lax.erf(ac * INV_SQRT2))
        dv_ref[...] = (dy * ac * cdf).astype(BF16)
        dac = dy * v_ref[...].astype(F32) * (cdf + ac * jnp.exp(-0.5 * ac * ac) * INV_SQRT_2PI)
        dac_ref[...] = dac

        @pl.when((b == 0) & (t == 0))
        def _():
            dcw_ref[...] = jnp.zeros_like(dcw_ref)
            dcb_ref[...] = jnp.zeros_like(dcb_ref)

        dcw_ref[0:8, :] += _rowsum8(dac * a2)
        dcw_ref[8:16, :] += _rowsum8(dac * a1)
        dcw_ref[16:24, :] += _rowsum8(dac * a)
        dcb_ref[...] += _rowsum8(dac)

    return pl.pallas_call(
        body, name="ffn_act_bwd1", grid=(NJ, B, NT),
        in_specs=[pl.BlockSpec((tm, d), lambda j, b, t: (b * NT + t, 0)), pl.BlockSpec((tn, d), lambda j, b, t: (j, 0)),
                  pl.BlockSpec((tm, tn), lambda j, b, t: (b * NT + t, j)),
                  pl.BlockSpec((tm, tn), lambda j, b, t: (b * NT + t, NJ + j)),
                  pl.BlockSpec((3, tn), lambda j, b, t: (0, j)), pl.BlockSpec((1, tn), lambda j, b, t: (0, j))],
        out_specs=[pl.BlockSpec((tm, tn), lambda j, b, t: (b * NT + t, j)),
                   pl.BlockSpec((tm, tn), lambda j, b, t: (b * NT + t, j)),
                   pl.BlockSpec((24, tn), lambda j, b, t: (0, j)), pl.BlockSpec((8, tn), lambda j, b, t: (0, j))],
        out_shape=[S((N, dff), F32), S((N, dff), BF16), S((24, dff), F32), S((8, dff), F32)],
        scratch_shapes=[pltpu.VMEM((8, tn), F32)], compiler_params=_cp(("parallel", "arbitrary", "arbitrary")),
    )(dx2, wd, up, up, cw, cb)


def _ffn_act_bwd2(dac, cw, B, T, tm=256):
    N = B * T
    dff = cw.shape[1]
    NT, NJ, tn = T // tm, dff // FFN_TN, FFN_TN
    last8 = N // 8 - 1

    def body(d_ref, nx_ref, cw_ref, da_ref):
        t = pl.program_id(2)
        dd = d_ref[...]
        row = _iota(dd.shape, 0)
        last = t == NT - 1
        n0 = jnp.where(last, 0.0, nx_ref[0:1, :])
        n1 = jnp.where(last, 0.0, nx_ref[1:2, :])
        d1 = jnp.where(row == tm - 1, n0, pltpu.roll(dd, tm - 1, 0))
        d2 = jnp.where(row == tm - 1, n1, jnp.where(row == tm - 2, n0, pltpu.roll(dd, tm - 2, 0)))
        w = cw_ref[...]
        da_ref[...] = (w[2:3, :] * dd + w[1:2, :] * d1 + w[0:1, :] * d2).astype(BF16)

    return pl.pallas_call(
        body, name="ffn_act_bwd2", grid=(B, NJ, NT),
        in_specs=[pl.BlockSpec((tm, tn), lambda b, j, t: (b * NT + t, j)),
                  pl.BlockSpec((8, tn), lambda b, j, t: (jnp.minimum((b * NT + t + 1) * (tm // 8), last8), j)),
                  pl.BlockSpec((3, tn), lambda b, j, t: (0, j))],
        out_specs=pl.BlockSpec((tm, tn), lambda b, j, t: (b * NT + t, j)), out_shape=S((N, dff), BF16),
        compiler_params=_cp(("parallel", "parallel", "parallel")),
    )(dac, dac, cw)


def _fold_rows(p, name):
    r, c = p.shape[0] // 8, p.shape[1]

    def body(p_ref, o_ref):
        for j in range(r):
            o_ref[j:j + 1, :] = jnp.sum(p_ref[8 * j:8 * (j + 1), :], axis=0, keepdims=True)

    return pl.pallas_call(body, name=name, out_shape=S((r, c), F32), compiler_params=_cp())(p)


def _small_reduce(lbl, dg_mix, dg_mem, dlb_p, dgn_p, dfb_p, dgq_p, dgk_p, dmq_p, dmk_p, dg_ffn, dcb_p, loss_p):
    d, dff = dg_mix.shape[1], dcb_p.shape[1]
    nbh = dlb_p.shape[0] // (8 * HG_H)

    def colsum(ref):
        return jnp.sum(ref[...], axis=0, keepdims=True)

    def body(lbl_ref, mix_ref, mem_ref, dlb_ref, dgn_ref, dfb_ref, dgq_ref, dgk_ref, dmq_ref, dmk_ref, ffn_ref, dcb_ref,
             ls_ref, o_mix, o_mem, o_lb, o_hgn, o_fb, o_fq, o_fk, o_mq, o_mk, o_ffn, o_cb, o_loss):
        o_mix[...], o_mem[...], o_ffn[...], o_cb[...] = colsum(mix_ref), colsum(mem_ref), colsum(ffn_ref), colsum(dcb_ref)
        o_hgn[...], o_fb[...], o_mq[...], o_mk[...] = colsum(dgn_ref), colsum(dfb_ref), colsum(dmq_ref), colsum(dmk_ref)
        for src, dst in ((dgq_ref, o_fq), (dgk_ref, o_fk)):
            v = colsum(src)
            dst[...] = v + pltpu.roll(v, FOX_D, 1)
        o_loss[...] = jnp.zeros((1, LANE), F32) + jnp.sum(colsum(ls_ref), axis=-1, keepdims=True)
        logits = lbl_ref[...]
        e = jnp.exp(logits - jnp.max(logits, axis=0, keepdims=True))
        pr = e / jnp.sum(e, axis=0, keepdims=True)
        rows = _iota((8, LANE), 0)
        for h in range(HG_H):
            acc = jnp.zeros((8, LANE), F32)
            for b in range(nbh):
                acc = acc + dlb_ref[8 * (b * HG_H + h):8 * (b * HG_H + h + 1), :]
            dlb = jnp.sum(acc, axis=0, keepdims=True)
            c = slice(LANE * h, LANE * (h + 1))
            p0 = pr[0:1, c]
            first = _iota((logits.shape[0], LANE), 0) == 0
            o_lb[:, c] = pr[:, c] * (jnp.where(first, 1.0, 0.0) - p0) * dlb

    outs = [S((1, d), F32), S((1, d), F32), S(lbl.shape, F32)] + [S((1, LANE), F32)] * 6 + \
           [S((1, d), F32), S((1, dff), F32), S((1, LANE), F32)]
    return pl.pallas_call(body, name="small_reduce", out_shape=outs, compiler_params=_cp())(
        lbl, dg_mix, dg_mem, dlb_p, dgn_p, dfb_p, dgq_p, dgk_p, dmq_p, dmk_p, dg_ffn, dcb_p, loss_p)


def _in_col_pieces():
    hw, fw = HG_H * HG_D, FOX_H * FOX_D
    fox0, ff0 = 4 * hw, 4 * hw + 3 * fw
    mq0 = ff0 + FOX_H
    gate0 = mq0 + MEM_H * MEM_D
    pieces = []
    for p in range(FOX_P):
        pieces += [(fox0 + j * fw + LANE * p, LANE) for j in range(3)]
    pieces.append((mq0, MEM_H * MEM_D))
    for h in range(HG_H):
        pieces += [(j * hw + HG_D * h, HG_D) for j in range(4)]
    pieces.append((gate0, C_FF - C_GATE))
    pieces.append((ff0, FOX_H))
    return pieces


def _perm_cols(w):
    parts = [w[:, s:s + n] for s, n in _in_col_pieces()]
    parts.append(jnp.zeros((w.shape[0], C_END - C_FF - FOX_H), w.dtype))
    return jnp.concatenate(parts, axis=1)


def _unperm_cols(segs):
    starts = [0]
    for a in segs:
        starts.append(starts[-1] + a.shape[1])

    def piece(ns, n):
        i = max(j for j in range(len(segs)) if starts[j] <= ns)
        return segs[i][:, ns - starts[i]:ns - starts[i] + n]

    new_start, placed = 0, []
    for s, n in _in_col_pieces():
        placed.append((s, new_start, n))
        new_start += n
    return jnp.concatenate([piece(ns, n) for _, ns, n in sorted(placed)], axis=1)


def _local_step(x2, mem2, tgt, sm, W, B, T, M, ex=None):
    fbias = jnp.pad(sm["fox_f_bias"], ((0, 0), (0, LANE - FOX_H)))
    gq2 = jnp.concatenate([sm["fox_q_norm_g"]] * 2, axis=1)
    gk2 = jnp.concatenate([sm["fox_k_norm_g"]] * 2, axis=1)
    lbl = sm["hgrn_lb_logits"]
    h = _rmsnorm_cast(x2, sm["norm_mix_g"], "norm_mix")
    z = _mm_nn(h, W["w_in"], F32, "proj_in", 512, 2432)
    memn = _rmsnorm_cast(mem2, sm["norm_mem_g"], "norm_mem", tm=256)
    memkv = _mm_nn(memn, W["mem_kv_w"], F32, "proj_memkv", 256, 512)
    ya, o_raw, states, *late = _hgrn_fwd(z, lbl, sm["hgrn_norm_g"], B, T, gather=ex.late_blocks() if ex else ())
    if ex:
        W = {**W, **ex.unpack_late(late)}
    fc, fct = _fox_gate_fwd(z, fbias, B, T)
    yb, lse = _fox_fwd(z, fc, fct, gq2, gk2, B, T)
    yc = _mem_fwd(z, memkv, sm["mem_q_norm_g"], sm["mem_k_norm_g"], B, T, M)
    x1, merged, ua, ub, uc = _merge_fwd(ya, yb, yc, z, x2, W["w_br_hgrn"], W["w_br_fox"], W["w_br_mem"], W["w_out"])
    h2 = _rmsnorm_cast(x1, sm["norm_ffn_g"], "norm_ffn")
    up = _mm_nn(h2, W["ffn_w_up"], BF16, "ffn_up", 512, FFN_TN)
    yf = _ffn_act_fwd(up, W["ffn_conv_w"], sm["ffn_conv_b"], B, T)
    dx2, loss_p = _ffn_down_loss(yf, W["ffn_w_down"], x1, tgt)
    dff = W["ffn_conv_w"].shape[1]
    dac, dv, dcw_p, dcb_p = _ffn_act_bwd1(dx2, W["ffn_w_down"], up, W["ffn_conv_w"], sm["ffn_conv_b"], B, T)
    da = _ffn_act_bwd2(dac, W["ffn_conv_w"], B, T)
    g = {"ffn_conv_w": _fold_rows(dcw_p, "g_conv_w")}
    g["ffn_w_down"] = _mm_tn(yf, dx2, "g_w_down", TN_TM, 512)
    dh2 = _mm_nt_sum([(da, 0, dff, 0), (dv, 0, dff, dff)], W["ffn_w_up"], "dh2", 256)
    g["ffn_w_up"] = [_mm_tn(h2, da, "g_w_up_a", TN_TM, FFN_TN), _mm_tn(h2, dv, "g_w_up_v", TN_TM, FFN_TN)]
    dx1, dg_ffn = _rmsnorm_bwd(dh2, x1, sm["norm_ffn_g"], dx2, "norm_ffn_bwd")
    g["w_out"] = _mm_tn(merged, dx1, "g_w_out", TN_TM, 512)
    dgate, dya, dyb, dyc, dua, dub, duc = _merge_bwd(dx1, z, ua, ub, uc, W["w_br_hgrn"], W["w_br_fox"], W["w_br_mem"],
                                                    W["w_out"])
    g["w_br_hgrn"] = _mm_tn(ya, dua, "g_w_br_hgrn", TN_TM, 512)
    g["w_br_fox"] = _mm_tn(yb, dub, "g_w_br_fox", TN_TM, 512)
    g["w_br_mem"] = _mm_tn(yc, duc, "g_w_br_mem", TN_TM, 512)
    if ex:
        early_pk = ex.early_grads(g)
        early_sib, early_pair = ex.sibling_round(early_pk, "early")
    dz_hg, dlb_p, dgn_p, *early_chips = _hgrn_bwd(z, o_raw, states, dya, lbl, sm["hgrn_norm_g"], B, T,
                                                  swap=early_pair if ex else ())
    dz_fox, dfc, dgq_p, dgk_p = _fox_bwd(z, dyb, yb, lse, fc, fct, gq2, gk2, B, T)
    dz_ff, dfb_p = _fox_gate_bwd(dfc, z, fbias, B, T)
    dz_mq, dkv, dmq_p, dmk_p = _mem_bwd(z, memkv, dyc, sm["mem_q_norm_g"], sm["mem_k_norm_g"], B, T, M)
    g["mem_kv_w"] = _mm_tn(memn, dkv, "g_mem_kv_w", 256, 512)
    dmemn = _mm_nt(dkv, W["mem_kv_w"], "d_memn", 256, 512)
    _, dg_mem = _rmsnorm_bwd(dmemn, mem2, sm["norm_mem_g"], None, "norm_mem_bwd", tm=256)
    d = x2.shape[1]
    parts = [(dz_fox, 0, C_MQ - C_FOX, C_FOX), (dz_mq, 0, C_HG - C_MQ, C_MQ), (dz_hg, 0, C_GATE - C_HG, C_HG)]
    parts += [(dgate, d * k, d, C_GATE + d * k) for k in range(3)] + [(dz_ff, 0, C_END - C_FF, C_FF)]
    g["w_in"] = [_mm_tn(h, dzs, "g_w_in_%d" % i, TN_TM, min(512, dzs.shape[1]))
                 for i, dzs in enumerate((dz_fox, dz_mq, dz_hg, dgate, dz_ff))]
    sums = None
    if ex:
        last_pk = ex.last_grads(g)
        last_sib, last_pair = ex.sibling_round(last_pk, "last")
        dh, last_chips = _mm_nt_sum(parts, W["w_in"], "dh", 256, swap=last_pair)
        sums = (ex.final_sums(early_pk, early_sib, early_chips, "early"),
                ex.final_sums(last_pk, last_sib, last_chips, "last"))
    else:
        dh = _mm_nt_sum(parts, W["w_in"], "dh", 256)
    grad_x, dg_mix = _rmsnorm_bwd(dh, x2, sm["norm_mix_g"], dx1, "norm_mix_bwd")
    small = _small_reduce(lbl, dg_mix, dg_mem, dlb_p, dgn_p, dfb_p, dgq_p, dgk_p, dmq_p, dmk_p, dg_ffn, dcb_p, loss_p)
    names = ("norm_mix_g", "norm_mem_g", "hgrn_lb_logits", "hgrn_norm_g", "fox_f_bias", "fox_q_norm_g", "fox_k_norm_g",
             "mem_q_norm_g", "mem_k_norm_g", "norm_ffn_g", "ffn_conv_b", "loss")
    g.update(dict(zip(names, small)))
    return grad_x, g, sums


ANY = pl.BlockSpec(memory_space=pl.ANY)


def _position():
    return lax.axis_index("x"), lax.axis_index("y"), lax.axis_index("c")


def _all_gather(blocks, name):
    nb = len(blocks)

    def body(*refs):
        start, forward, finish = _gather_phases(refs[:nb], refs[nb:2 * nb], *refs[2 * nb:])
        start()
        forward()
        finish()

    return pl.pallas_call(
        body, name=name, out_shape=_gather_shapes(blocks), in_specs=[ANY] * nb, out_specs=[ANY] * nb,
        scratch_shapes=_gather_sems(nb),
    )(*blocks)


def _gather_shapes(blocks):
    return [S((N_DEV,) + b.shape, b.dtype) for b in blocks]


def _gather_sems(nb):
    return [pltpu.SemaphoreType.DMA((7 * nb,)), pltpu.SemaphoreType.DMA((7 * nb,)), pltpu.SemaphoreType.DMA((nb,))]


def _gather_phases(x_refs, out_refs, send_sems, recv_sems, local_sems):
    nb = len(x_refs)
    x, y, c = _position()
    me, sibling = (x, y, c), (x, y, 1 - c)
    chips = [(1 - x, y), (x, 1 - y), (1 - x, 1 - y)]

    def copy(i, k, blk, to, own=False):
        px, py, pc = blk
        slot = out_refs[i].at[4 * px + 2 * py + pc]
        return pltpu.make_async_remote_copy(
            src_ref=x_refs[i] if own else slot, dst_ref=slot, send_sem=send_sems.at[7 * i + k],
            recv_sem=recv_sems.at[7 * i + k], device_id=to, device_id_type=MESH)

    def mine(i):
        return pltpu.make_async_copy(x_refs[i], out_refs[i].at[4 * x + 2 * y + c], local_sems.at[i])

    def first(i):
        return [copy(i, 0, me, sibling, own=True)] + [copy(i, 1 + j, me, (*chip, c), own=True)
                                                     for j, chip in enumerate(chips)]

    def passed(i, j):
        return copy(i, 4 + j, (*chips[j], c), sibling)

    def start():
        for i in range(nb):
            mine(i).start()
            for cp in first(i):
                cp.start()

    def forward():
        for i in range(nb):
            for j, chip in enumerate(chips):
                copy(i, 1 + j, (*chip, c), me).wait_recv()
                passed(i, j).start()

    def finish():
        for i in range(nb):
            copy(i, 0, sibling, me).wait_recv()
            for j, chip in enumerate(chips):
                copy(i, 4 + j, (*chip, 1 - c), me).wait_recv()
        for i in range(nb):
            for cp in first(i) + [passed(i, j) for j in range(3)]:
                cp.wait_send()
            mine(i).wait()

    return start, forward, finish


def _swap_with_sibling(pks, name):
    nb = len(pks)

    def body(*refs):
        pk_refs, out_refs = refs[:nb], refs[nb:2 * nb]
        send_sems, recv_sems = refs[2 * nb:]
        x, y, c = _position()
        copies = [pltpu.make_async_remote_copy(
            src_ref=pk_refs[i].at[2 * k + 1 - c], dst_ref=out_refs[i].at[k], send_sem=send_sems.at[4 * i + k],
            recv_sem=recv_sems.at[4 * i + k], device_id=(x, y, 1 - c), device_id_type=MESH)
            for i in range(nb) for k in range(4)]
        for cp in copies:
            cp.start()
        for cp in copies:
            cp.wait()

    return pl.pallas_call(
        body, name=name, out_shape=[S((4,) + p.shape[1:], p.dtype) for p in pks], in_specs=[ANY] * nb,
        out_specs=[ANY] * nb, scratch_shapes=[pltpu.SemaphoreType.DMA((4 * nb,)), pltpu.SemaphoreType.DMA((4 * nb,))],
    )(*pks)


def _swap_between_chips(pbs, name):
    nb = len(pbs)

    def body(*refs):
        start, finish = _chip_swap_phases(refs[:nb], refs[nb:2 * nb], *refs[2 * nb:])
        start()
        finish()

    return pl.pallas_call(
        body, name=name, out_shape=[S(p.shape, p.dtype) for p in pbs], in_specs=[ANY] * nb, out_specs=[ANY] * nb,
        scratch_shapes=_chip_swap_sems(nb),
    )(*pbs)


def _chip_swap_sems(nb):
    return [pltpu.SemaphoreType.DMA((3 * nb,)), pltpu.SemaphoreType.DMA((3 * nb,)), pltpu.SemaphoreType.DMA((nb,))]


def _chip_swap_phases(pb_refs, out_refs, send_sems, recv_sems, local_sems):
    nb = len(pb_refs)
    x, y, c = _position()
    me = 2 * x + y
    chips = [(1 - x, y), (x, 1 - y), (1 - x, 1 - y)]

    def local(i):
        return pltpu.make_async_copy(pb_refs[i].at[me], out_refs[i].at[me], local_sems.at[i])

    def send(i, j):
        cx, cy = chips[j]
        return pltpu.make_async_remote_copy(
            src_ref=pb_refs[i].at[2 * cx + cy], dst_ref=out_refs[i].at[me], send_sem=send_sems.at[3 * i + j],
            recv_sem=recv_sems.at[3 * i + j], device_id=(cx, cy, c), device_id_type=MESH)

    def arrival(i, j):
        cx, cy = chips[j]
        return pltpu.make_async_remote_copy(
            src_ref=pb_refs[i].at[me], dst_ref=out_refs[i].at[2 * cx + cy], send_sem=send_sems.at[3 * i + j],
            recv_sem=recv_sems.at[3 * i + j], device_id=(cx, cy, c), device_id_type=MESH)

    def start():
        for i in range(nb):
            local(i).start()
            for j in range(3):
                send(i, j).start()

    def finish():
        for i in range(nb):
            for j in range(3):
                arrival(i, j).wait_recv()
        for i in range(nb):
            for j in range(3):
                send(i, j).wait_send()
            local(i).wait()

    return start, finish


def _row_tile(r):
    return max(t for t in range(16, min(r, 512) + 1, 16) if r % t == 0)


def _pair_sum_cast(pk, recv, core, name):
    _, r, l = pk.shape
    tr = _row_tile(r)

    def body(c_ref, a_ref, b_ref, o_ref):
        o_ref[...] = (a_ref[...] + b_ref[...]).astype(BF16)

    return pl.pallas_call(
        body, name=name,
        grid_spec=pltpu.PrefetchScalarGridSpec(
            num_scalar_prefetch=1, grid=(4, r // tr),
            in_specs=[pl.BlockSpec((None, tr, l), lambda k, i, c: (2 * k + c[0], i, 0)),
                      pl.BlockSpec((None, tr, l), lambda k, i, c: (k, i, 0))],
            out_specs=pl.BlockSpec((None, tr, l), lambda k, i, c: (k, i, 0))),
        out_shape=S((4, r, l), BF16), compiler_params=_cp(("parallel", "parallel")),
    )(core, pk, recv)


def _final_sum(pk, recv_sib, recv_chips, slot, chip, name):
    _, r, l = pk.shape
    tr = _row_tile(r)

    def body(s_ref, k_ref, a_ref, b_ref, rc_ref, o_ref):
        base = a_ref[...] + b_ref[...]
        acc = jnp.zeros_like(base)
        for j in range(4):
            acc = acc + jnp.where(k_ref[0] == j, base, rc_ref[j].astype(F32))
        o_ref[...] = acc

    return pl.pallas_call(
        body, name=name,
        grid_spec=pltpu.PrefetchScalarGridSpec(
            num_scalar_prefetch=2, grid=(r // tr,),
            in_specs=[pl.BlockSpec((None, tr, l), lambda i, s, k: (s[0], i, 0)),
                      pl.BlockSpec((None, tr, l), lambda i, s, k: (k[0], i, 0)),
                      pl.BlockSpec((4, tr, l), lambda i, s, k: (0, i, 0))],
            out_specs=pl.BlockSpec((tr, l), lambda i, s, k: (i, 0))),
        out_shape=S((r, l), F32), compiler_params=_cp(("parallel",)),
    )(slot, chip, pk, recv_sib, recv_chips)


def _adamw_math(w, g, m, v):
    m = ADAM_B1 * m + (1.0 - ADAM_B1) * g
    v = ADAM_B2 * v + (1.0 - ADAM_B2) * (g * g)
    m_hat = m / (1.0 - ADAM_B1 ** ADAM_STEP)
    v_hat = v / (1.0 - ADAM_B2 ** ADAM_STEP)
    return -ADAM_LR * (m_hat / (jnp.sqrt(v_hat) + ADAM_EPS) + ADAM_WD * w), m, v


def _adamw(w, g, m, v, name):
    r, c = w.shape
    tr = 256 if r % 256 == 0 else r

    def body(w_ref, g_ref, m_ref, v_ref, d_ref, nm_ref, nv_ref):
        d_ref[...], nm_ref[...], nv_ref[...] = _adamw_math(w_ref[...], g_ref[...], m_ref[...], v_ref[...])

    tile = pl.BlockSpec((tr, c), lambda i: (i, 0))
    return pl.pallas_call(
        body, name=name, grid=(r // tr,), in_specs=[tile] * 4, out_specs=[tile] * 3, out_shape=[S((r, c), F32)] * 3,
        compiler_params=_cp(("parallel",)),
    )(w, g, m, v)


def _small_update(gathered, w, m, v):
    def body(ga_ref, w_ref, m_ref, v_ref, g_ref, d_ref, nm_ref, nv_ref):
        g = ga_ref[0]
        for k in range(1, N_DEV):
            g = g + ga_ref[k]
        g_ref[...] = g
        d_ref[...], nm_ref[...], nv_ref[...] = _adamw_math(w_ref[...], g, m_ref[...], v_ref[...])

    return pl.pallas_call(body, name="small_update", out_shape=[S(w.shape, F32)] * 4, compiler_params=_cp())(
        gathered, w, m, v)


BIG = ("w_in", "mem_kv_w", "w_br_hgrn", "w_br_fox", "w_br_mem", "w_out", "ffn_w_up", "ffn_conv_w", "ffn_w_down")
GROUP_ROWS = ("w_out", "ffn_w_down")
GROUP_LANE = ("w_br_hgrn", "w_br_fox", "w_br_mem")
LANE_GROUP_ROWS = 224
SMALL = ("norm_mix_g", "norm_mem_g", "hgrn_lb_logits", "hgrn_norm_g", "fox_f_bias", "fox_q_norm_g", "fox_k_norm_g",
         "mem_q_norm_g", "mem_k_norm_g", "norm_ffn_g", "ffn_conv_b")


def _rows_of(n_elems):
    return -(-n_elems // LANE)


def _to_rows(a, lead=0):
    flat = a.reshape(a.shape[:lead] + (-1,))
    pad = (-flat.shape[-1]) % LANE
    if pad:
        flat = jnp.pad(flat, [(0, 0)] * lead + [(0, pad)])
    return flat.reshape(a.shape[:lead] + (-1, LANE))


def _stack_rows(parts, lead, total_rows):
    buf = jnp.concatenate(parts, axis=lead)
    pad = total_rows - buf.shape[lead]
    return jnp.pad(buf, [(0, 0)] * lead + [(0, pad), (0, 0)])


def _round_up(n, k):
    return -(-n // k) * k


def _from_rows(rows, shape, lead=0):
    n = math.prod(shape)
    return rows.reshape(rows.shape[:lead] + (-1,))[..., :n].reshape(rows.shape[:lead] + tuple(shape))


def _blocks_to_full(blocks, kind):
    n, a, b = blocks.shape
    return blocks.transpose(1, 0, 2).reshape(a, n * b) if kind == "col" else blocks.reshape(n * a, b)


def _full_to_blocks(full, kind, n=N_DEV):
    a, b = full.shape
    return full.reshape(a, n, b // n).transpose(1, 0, 2) if kind == "col" else full.reshape(n, a // n, b)


def _lane_group_rows(shard):
    n_lane = sum(shard[n].shape[0] for n in GROUP_LANE)
    n_cw = shard["ffn_conv_w"].size
    return n_lane, _rows_of(3 * n_cw), _rows_of(n_cw), _round_up(n_lane + _rows_of(3 * n_cw), LANE_GROUP_ROWS)


def _split_bf16x3(x):
    hi = x.astype(BF16)
    r1 = x - hi.astype(F32)
    mid = r1.astype(BF16)
    return jnp.stack([hi, mid, (r1 - mid.astype(F32)).astype(BF16)])


class _Exchange:
    def __init__(self, shard):
        self.shard = shard
        xi, yi, ci = _position()
        self.core = ci.astype(jnp.int32).reshape(1)
        self.chip = (2 * xi + yi).astype(jnp.int32).reshape(1)
        self.n_lane, self.r_pieces, self.r_vals, self.r_lane = _lane_group_rows(shard)

    def first_blocks(self):
        return [self.shard["w_in"].astype(BF16), self.shard["mem_kv_w"].astype(BF16)]

    def unpack_first(self, gathered):
        return {"w_in": _perm_cols(_blocks_to_full(gathered[0], "col")), "mem_kv_w": _blocks_to_full(gathered[1], "row")}

    def late_blocks(self):
        sh = self.shard
        lane_rows = [sh[n].astype(BF16) for n in GROUP_LANE] + [_to_rows(_split_bf16x3(sh["ffn_conv_w"]))]
        return [jnp.concatenate([sh[n].astype(BF16) for n in GROUP_ROWS], axis=0), sh["ffn_w_up"].astype(BF16),
                _stack_rows(lane_rows, 0, self.r_lane)]

    def unpack_late(self, gathered):
        gb, gc, gd = gathered
        sh = self.shard
        W = {"ffn_w_up": _blocks_to_full(gc, "col")}
        r0 = 0
        for n in GROUP_ROWS:
            W[n] = _blocks_to_full(gb[:, r0:r0 + sh[n].shape[0]], "row")
            r0 += sh[n].shape[0]
        r0 = 0
        for n in GROUP_LANE:
            W[n] = _blocks_to_full(gd[:, r0:r0 + sh[n].shape[0]], "col")
            r0 += sh[n].shape[0]
        cw = _from_rows(gd[:, self.n_lane:self.n_lane + self.r_pieces], (3,) + sh["ffn_conv_w"].shape, lead=1).astype(F32)
        W["ffn_conv_w"] = _blocks_to_full(cw[:, 0] + cw[:, 1] + cw[:, 2], "col")
        return W

    def early_grads(self, g):
        cw_rows = _to_rows(_full_to_blocks(g["ffn_conv_w"], "col"), lead=1)
        return [jnp.concatenate([_full_to_blocks(g[n], "row") for n in GROUP_ROWS], axis=1),
                jnp.concatenate([_full_to_blocks(h, "col", N_DEV // 2) for h in g["ffn_w_up"]], axis=0),
                _stack_rows([_full_to_blocks(g[n], "col") for n in GROUP_LANE] + [cw_rows], 1, self.r_lane)]

    def last_grads(self, g):
        return [_full_to_blocks(_unperm_cols(g["w_in"]), "col"), _full_to_blocks(g["mem_kv_w"], "row")]

    def sibling_round(self, pks, tag):
        recv = _swap_with_sibling(pks, "rs_sibling_" + tag)
        pair = [_pair_sum_cast(p, r, self.core, "rs_pair_sum_%s%d" % (tag, i)) for i, (p, r) in enumerate(zip(pks, recv))]
        return recv, pair

    def final_sums(self, pks, recv_sib, recv_chips, tag):
        return [_final_sum(p, rs, rc, 2 * self.chip + self.core, self.chip, "rs_final_sum_%s%d" % (tag, i))
                for i, (p, rs, rc) in enumerate(zip(pks, recv_sib, recv_chips))]

    def unpack_grads(self, early, last):
        sh = self.shard
        g_shard = {"w_in": last[0], "mem_kv_w": last[1], "ffn_w_up": early[1]}
        r0 = 0
        for n in GROUP_ROWS:
            g_shard[n] = early[0][r0:r0 + sh[n].shape[0]]
            r0 += sh[n].shape[0]
        r0 = 0
        for n in GROUP_LANE:
            g_shard[n] = early[2][r0:r0 + sh[n].shape[0]]
            r0 += sh[n].shape[0]
        g_shard["ffn_conv_w"] = _from_rows(early[2][self.n_lane:self.n_lane + self.r_vals], sh["ffn_conv_w"].shape)
        return g_shard


def kernel(x, mem, norm_mix_g, norm_mem_g, w_in, hgrn_lb_logits, hgrn_norm_g, fox_f_bias, fox_q_norm_g, fox_k_norm_g, mem_kv_w, mem_q_norm_g, mem_k_norm_g, w_br_hgrn, w_br_fox, w_br_mem, w_out, norm_ffn_g, ffn_w_up, ffn_conv_w, ffn_conv_b, ffn_w_down, loss_target, m_norm_mix_g, m_norm_mem_g, m_w_in, m_hgrn_lb_logits, m_hgrn_norm_g, m_fox_f_bias, m_fox_q_norm_g, m_fox_k_norm_g, m_mem_kv_w, m_mem_q_norm_g, m_mem_k_norm_g, m_w_br_hgrn, m_w_br_fox, m_w_br_mem, m_w_out, m_norm_ffn_g, m_ffn_w_up, m_ffn_conv_w, m_ffn_conv_b, m_ffn_w_down, v_norm_mix_g, v_norm_mem_g, v_w_in, v_hgrn_lb_logits, v_hgrn_norm_g, v_fox_f_bias, v_fox_q_norm_g, v_fox_k_norm_g, v_mem_kv_w, v_mem_q_norm_g, v_mem_k_norm_g, v_w_br_hgrn, v_w_br_fox, v_w_br_mem, v_w_out, v_norm_ffn_g, v_ffn_w_up, v_ffn_conv_w, v_ffn_conv_b, v_ffn_w_down):
    given = dict(locals())
    order = ("norm_mix_g", "norm_mem_g", "w_in", "hgrn_lb_logits", "hgrn_norm_g", "fox_f_bias", "fox_q_norm_g",
             "fox_k_norm_g", "mem_kv_w", "mem_q_norm_g", "mem_k_norm_g", "w_br_hgrn", "w_br_fox", "w_br_mem", "w_out",
             "norm_ffn_g", "ffn_w_up", "ffn_conv_w", "ffn_conv_b", "ffn_w_down")
    B, T, D = x.shape
    M = mem.shape[1]
    shard = {n: given[n][0] if n in BIG else given[n] for n in order}
    mom = {n: (given["m_" + n][0], given["v_" + n][0]) if n in BIG else (given["m_" + n], given["v_" + n])
           for n in order}
    shard["hgrn_lb_logits"] = hgrn_lb_logits
    for n in ("norm_mix_g", "norm_mem_g", "hgrn_norm_g", "fox_f_bias", "fox_q_norm_g", "fox_k_norm_g", "mem_q_norm_g",
              "mem_k_norm_g", "norm_ffn_g", "ffn_conv_b"):
        shard[n] = given[n].reshape(1, -1)

    ex = _Exchange(shard)
    W = ex.unpack_first(_all_gather(ex.first_blocks(), "ag_first"))

    sm = {n: shard[n] for n in SMALL}
    grad_x, g, sums = _local_step(x.reshape(B * T, D), mem.reshape(B * M, D), loss_target.reshape(B * T, D), sm, W,
                                  B, T, M, ex)
    g_shard = ex.unpack_grads(*sums)

    sg = {n: g[n] for n in SMALL}
    sg["fox_f_bias"] = g["fox_f_bias"][:, :FOX_H]
    sg["fox_q_norm_g"] = g["fox_q_norm_g"][:, :FOX_D]
    sg["fox_k_norm_g"] = g["fox_k_norm_g"][:, :FOX_D]
    slayout, row0 = {}, 0
    for n in SMALL:
        nr = _rows_of(shard[n].size)
        slayout[n] = (row0, nr)
        row0 += nr
    loss_row = row0
    r_small = _round_up(row0 + 1, 8)

    def pack_small(d, with_loss=None):
        rows = [_to_rows(d[n]) for n in SMALL]
        rows.append(with_loss if with_loss is not None else jnp.zeros((1, LANE), F32))
        return _stack_rows(rows, 0, r_small)

    sgath, = _all_gather([pack_small(sg, g["loss"])], "ag_small")
    s_g, s_d, s_m, s_v = _small_update(sgath, pack_small(shard), pack_small({n: mom[n][0].reshape(shard[n].shape) for n in SMALL}),
                                       pack_small({n: mom[n][1].reshape(shard[n].shape) for n in SMALL}))
    loss = s_g[loss_row, 0]

    grads, deltas, new_m, new_v = {}, {}, {}, {}
    for n in BIG:
        gn = g_shard[n]
        d, nm, nv = _adamw(shard[n], gn, mom[n][0], mom[n][1], "adamw_" + n)
        grads[n], deltas[n], new_m[n], new_v[n] = (a[None] for a in (gn, d, nm, nv))
    for n in SMALL:
        r0, nr = slayout[n]
        for dst, src in ((grads, s_g), (deltas, s_d), (new_m, s_m), (new_v, s_v)):
            dst[n] = _from_rows(src[r0:r0 + nr], given[n].shape)
    return (loss, grad_x.reshape(B, T, D), *[grads[n] for n in order], *[deltas[n] for n in order],
            *[new_m[n] for n in order], *[new_v[n] for n in order])
```

```python
import functools
import math

import jax
import jax.numpy as jnp
from jax import lax
from jax.experimental import pallas as pl
from jax.experimental.pallas import tpu as pltpu

F32, BF16 = jnp.float32, jnp.bfloat16
S = jax.ShapeDtypeStruct
MESH = pl.DeviceIdType.MESH

N_DEV = 8
EPS = 1e-6
LANE = 128
CHUNK = 64
SUB = 16
HG_H, HG_D = 4, 128
HG_GROUP = 2
FOX_H, FOX_D = 8, 64
FOX_P = FOX_H // 2
MEM_H, MEM_D = 4, 128
NEG = -1e30
VMEM_LIMIT = 56 * 2**20

ADAM_LR, ADAM_B1, ADAM_B2, ADAM_EPS, ADAM_WD, ADAM_STEP = 0.001, 0.9, 0.999, 1e-08, 0.01, 10

C_FOX, C_MQ, C_HG, C_GATE, C_FF, C_END = 0, 1536, 2048, 4096, 7168, 7296


def _cp(sem=None):
    return pltpu.CompilerParams(dimension_semantics=sem, vmem_limit_bytes=VMEM_LIMIT)


def _dot(a, b, dims, prec=None):
    return lax.dot_general(a, b, (dims, ((), ())), preferred_element_type=F32, precision=prec)


def _nn(a, b, prec=None):
    return _dot(a, b, ((1,), (0,)), prec)


def _nt(a, b, prec=None):
    return _dot(a, b, ((1,), (1,)), prec)


def _tn(a, b, prec=None):
    return _dot(a, b, ((0,), (0,)), prec)


def _b(x):
    return x.astype(BF16)


def _mm3(fn, a, b):
    ah, bh = _b(a), _b(b)
    return fn(ah, bh) + fn(ah, _b(b - bh.astype(F32))) + fn(_b(a - ah.astype(F32)), bh)


def _iota(shape, dim):
    return lax.broadcasted_iota(jnp.int32, shape, dim)


def _rowsum8(x):
    r, d = x.shape
    return jnp.sum(x.reshape(r // 8, 8, d), axis=0)


def _rmsnorm_cast(x, g, name, tm=512):
    n, d = x.shape

    def body(x_ref, g_ref, o_ref):
        v = x_ref[...]
        r = lax.rsqrt(jnp.mean(v * v, axis=-1, keepdims=True) + EPS)
        o_ref[...] = (v * r * g_ref[...]).astype(BF16)

    return pl.pallas_call(
        body, name=name, grid=(n // tm,),
        in_specs=[pl.BlockSpec((tm, d), lambda i: (i, 0)), pl.BlockSpec((1, d), lambda i: (0, 0))],
        out_specs=pl.BlockSpec((tm, d), lambda i: (i, 0)), out_shape=S((n, d), BF16), compiler_params=_cp(("parallel",)),
    )(x, g)


def _rmsnorm_bwd(dh, x, g, resid, name, tm=512):
    n, d = x.shape
    has_res = resid is not None

    def body(*refs):
        if has_res:
            dh_ref, x_ref, g_ref, r_ref, dx_ref, dg_ref = refs
        else:
            dh_ref, x_ref, g_ref, dx_ref, dg_ref = refs
        v = x_ref[...]
        dhv = dh_ref[...].astype(F32)
        r = lax.rsqrt(jnp.mean(v * v, axis=-1, keepdims=True) + EPS)
        xh = v * r
        u = dhv * g_ref[...]
        dx = r * (u - xh * jnp.mean(u * xh, axis=-1, keepdims=True))
        if has_res:
            dx = dx + r_ref[...]
        dx_ref[...] = dx

        @pl.when(pl.program_id(0) == 0)
        def _():
            dg_ref[...] = jnp.zeros_like(dg_ref)

        dg_ref[...] += _rowsum8(dhv * xh)

    tile = pl.BlockSpec((tm, d), lambda i: (i, 0))
    ins = [tile, tile, pl.BlockSpec((1, d), lambda i: (0, 0))] + ([tile] if has_res else [])
    args = (dh, x, g) + ((resid,) if has_res else ())
    return pl.pallas_call(
        body, name=name, grid=(n // tm,), in_specs=ins,
        out_specs=[tile, pl.BlockSpec((8, d), lambda i: (0, 0))],
        out_shape=[S((n, d), F32), S((8, d), F32)], compiler_params=_cp(("arbitrary",)),
    )(*args)


def _mm_nn(a, b, out_dtype, name, tm, tn, b_col0=0, n_out=None):
    m, k = a.shape
    n_out = b.shape[1] if n_out is None else n_out
    jb = b_col0 // tn
    assert b_col0 % tn == 0 and n_out % tn == 0 and m % tm == 0

    def body(a_ref, b_ref, o_ref):
        o_ref[...] = _nn(a_ref[...].astype(BF16), b_ref[...].astype(BF16)).astype(out_dtype)

    return pl.pallas_call(
        body, name=name, grid=(m // tm, n_out // tn),
        in_specs=[pl.BlockSpec((tm, k), lambda i, j: (i, 0)), pl.BlockSpec((k, tn), lambda i, j: (0, j + jb))],
        out_specs=pl.BlockSpec((tm, tn), lambda i, j: (i, j)), out_shape=S((m, n_out), out_dtype),
        compiler_params=_cp(("parallel", "parallel")),
    )(a, b)


def _mm_nt(dy, w, name, tm, tr, w_col0=0, acc=None):
    m, r = dy.shape
    k = w.shape[0]
    jb = w_col0 // tr
    nr = r // tr
    assert w_col0 % tr == 0 and r % tr == 0 and m % tm == 0
    has_acc = acc is not None

    def body(*refs):
        if has_acc:
            dy_ref, w_ref, acc_ref, o_ref = refs
        else:
            dy_ref, w_ref, o_ref = refs
        part = _nt(dy_ref[...].astype(BF16), w_ref[...].astype(BF16))

        @pl.when(pl.program_id(1) == 0)
        def _():
            o_ref[...] = part + acc_ref[...] if has_acc else part

        @pl.when(pl.program_id(1) > 0)
        def _():
            o_ref[...] += part

    out_tile = pl.BlockSpec((tm, k), lambda i, j: (i, 0))
    ins = [pl.BlockSpec((tm, tr), lambda i, j: (i, j)), pl.BlockSpec((k, tr), lambda i, j: (0, j + jb))]
    args = (dy, w)
    if has_acc:
        ins.append(out_tile)
        args = args + (acc,)
    return pl.pallas_call(
        body, name=name, grid=(m // tm, nr), in_specs=ins, out_specs=out_tile, out_shape=S((m, k), F32),
        input_output_aliases=({2: 0} if has_acc else {}), compiler_params=_cp(("parallel", "arbitrary")),
    )(*args)


def _mm_nt_sum(parts, w, name, tm, swap=()):
    m = parts[0][0].shape[0]
    k = w.shape[0]
    assert m % tm == 0 and all(c % n == 0 and o % n == 0 for _, c, n, o in parts)
    np_ = len(parts)
    nsw = len(swap)
    n_steps = m // tm

    def body(*refs):
        o_ref = refs[2 * np_ + nsw]
        if nsw:
            start, finish = _chip_swap_phases(refs[2 * np_:2 * np_ + nsw], refs[2 * np_ + nsw + 1:2 * np_ + 2 * nsw + 1],
                                              *refs[2 * np_ + 2 * nsw + 1:])
            pl.when(pl.program_id(0) == 0)(start)
        acc = _nt(refs[0][...].astype(BF16), refs[np_][...].astype(BF16))
        for i in range(1, np_):
            acc = acc + _nt(refs[i][...].astype(BF16), refs[np_ + i][...].astype(BF16))
        o_ref[...] = acc
        if nsw:
            pl.when(pl.program_id(0) == n_steps - 1)(finish)

    dy_specs = [pl.BlockSpec((tm, n), functools.partial(lambda i, j: (i, j), j=c // n)) for _, c, n, _ in parts]
    w_specs = [pl.BlockSpec((k, n), functools.partial(lambda i, j: (0, j), j=o // n)) for _, _, n, o in parts]
    out = pl.pallas_call(
        body, name=name, grid=(n_steps,), in_specs=dy_specs + w_specs + [ANY] * nsw,
        out_specs=[pl.BlockSpec((tm, k), lambda i: (i, 0))] + [ANY] * nsw,
        out_shape=[S((m, k), F32)] + [S(p.shape, p.dtype) for p in swap],
        scratch_shapes=_chip_swap_sems(nsw) if nsw else [],
        compiler_params=_cp(("arbitrary",) if nsw else ("parallel",)),
    )(*([p[0] for p in parts] + [w] * np_ + list(swap)))
    return (out[0], out[1:]) if nsw else out[0]


def _mm_tn(x, dy, name, tm, tn):
    m, k = x.shape
    n = dy.shape[1]
    tm = min(tm, m)
    assert m % tm == 0 and n % tn == 0

    def body(x_ref, dy_ref, o_ref):
        part = _tn(x_ref[...].astype(BF16), dy_ref[...].astype(BF16))

        @pl.when(pl.program_id(1) == 0)
        def _():
            o_ref[...] = part

        @pl.when(pl.program_id(1) > 0)
        def _():
            o_ref[...] += part

    return pl.pallas_call(
        body, name=name, grid=(n // tn, m // tm),
        in_specs=[pl.BlockSpec((tm, k), lambda j, i: (i, 0)), pl.BlockSpec((tm, tn), lambda j, i: (i, j))],
        out_specs=pl.BlockSpec((k, tn), lambda j, i: (0, j)), out_shape=S((k, n), F32),
        compiler_params=_cp(("parallel", "arbitrary")),
    )(x, dy)


def _lower_bound(logits):
    e = jnp.exp(logits - jnp.max(logits, axis=0, keepdims=True))
    return e[0:1, :] / jnp.sum(e, axis=0, keepdims=True)


def _hg_gates(fl, lb):
    sig = jax.nn.sigmoid(fl)
    f = lb + (1.0 - lb) * sig
    k = (1.0 - lb) * (1.0 - sig)
    return sig, f, k, jnp.log(f)


def _silu_and_grad(x):
    s = jax.nn.sigmoid(x)
    return x * s, s * (1.0 + x * (1.0 - s))


def _hg_rowblocks(G):
    return [None] + [G[SUB * i - 1:SUB * i, :] for i in range(1, CHUNK // SUB)]


def _hg_intra_A(qs, k, G):
    refs = _hg_rowblocks(G)
    cols = _iota((SUB, CHUNK), 1)
    rows = _iota((SUB, CHUNK), 0)
    blocks = []
    for i in range(CHUNK // SUB):
        lo = SUB * i
        qb, Gb = qs[lo:lo + SUB, :], G[lo:lo + SUB, :]
        diag = jnp.zeros((SUB, CHUNK), F32)
        for s in range(SUB):
            e = jnp.exp(jnp.minimum(Gb - G[lo + s:lo + s + 1, :], 0.0))
            col = jnp.sum(qb * k[lo + s:lo + s + 1, :] * e, axis=-1, keepdims=True)
            diag = jnp.where(cols == lo + s, col, diag)
        a = jnp.where((cols >= lo) & (cols <= rows + lo), diag, 0.0)
        if i > 0:
            qr = qb * jnp.exp(Gb - refs[i])
            kr = k * jnp.exp(jnp.minimum(refs[i] - G, 0.0))
            a = jnp.where(cols < lo, _nt(_b(qr), _b(kr)), a)
        blocks.append(a)
    return jnp.concatenate(blocks, axis=0)


def _hg_intra_bwd(dA, qs, k, G):
    refs = _hg_rowblocks(G)
    cols = _iota((SUB, CHUNK), 1)
    rows16 = _iota((SUB, HG_D), 0)
    dk = jnp.zeros((CHUNK, HG_D), F32)
    dq_blocks, dk_diag_blocks = [], []
    for i in range(CHUNK // SUB):
        lo = SUB * i
        qb, Gb = qs[lo:lo + SUB, :], G[lo:lo + SUB, :]
        dAb = dA[lo:lo + SUB, :]
        dq = jnp.zeros((SUB, HG_D), F32)
        dkb = jnp.zeros((SUB, HG_D), F32)
        for s in range(SUB):
            e = jnp.exp(jnp.minimum(Gb - G[lo + s:lo + s + 1, :], 0.0))
            e = jnp.where(rows16 >= s, e, 0.0)
            dcol = jnp.sum(jnp.where(cols == lo + s, dAb, 0.0), axis=-1, keepdims=True)
            w = dcol * e
            dq = dq + w * k[lo + s:lo + s + 1, :]
            dkb = jnp.where(rows16 == s, jnp.sum(w * qb, axis=0, keepdims=True), dkb)
        if i > 0:
            e1 = jnp.exp(Gb - refs[i])
            e2 = jnp.exp(jnp.minimum(refs[i] - G, 0.0))
            dA_off = jnp.where(cols < lo, dAb, 0.0)
            dq = dq + _mm3(_nn, dA_off, k * e2) * e1
            dk = dk + _mm3(_tn, dA_off, qb * e1) * e2
        dq_blocks.append(dq)
        dk_diag_blocks.append(dkb)
    return jnp.concatenate(dq_blocks, axis=0), dk + jnp.concatenate(dk_diag_blocks, axis=0)


def _tri(n, upper=False):
    r, c = _iota((n, n), 0), _iota((n, n), 1)
    return jnp.where((c >= r) if upper else (r >= c), 1.0, 0.0).astype(BF16)


def _prefix_mm(tri, x):
    hi = x.astype(BF16)
    r1 = x - hi.astype(F32)
    mid = r1.astype(BF16)
    lo = (r1 - mid.astype(F32)).astype(BF16)
    return _nn(tri, hi) + _nn(tri, mid) + _nn(tri, lo)


def _hgrn_fwd(z, lb, gn, B, T):
    N = B * T
    NC = T // CHUNK
    ng = HG_H // HG_GROUP

    def body(z_ref, lb_ref, gn_ref, y_ref, o_ref, st_ref, s_scr):
        lbs = _lower_bound(lb_ref[...])
        tri = _tri(CHUNK)
        s_scr[...] = jnp.zeros_like(s_scr)

        def chunk(c, carry):
            r = pl.ds(pl.multiple_of(c * CHUNK, CHUNK), CHUNK)
            for hh in range(HG_GROUP):
                zc, oc = 4 * LANE * hh, LANE * hh
                ql, fl, il, gl = (z_ref[r, zc + LANE * j:zc + LANE * (j + 1)] for j in range(4))
                _, _, k, logf = _hg_gates(fl, lbs[:, oc:oc + LANE])
                G = _prefix_mm(tri, logf)
                qs = ql * jax.nn.sigmoid(ql)
                st = s_scr[hh]
                st_ref[hh * NC + c] = st
                g_last = G[CHUNK - 1:CHUNK, :]
                A = _hg_intra_A(qs, k, G)
                o = _nn(_b(A), _b(il)) + _nt(_b(qs * jnp.exp(G)), _b(st))
                s_scr[hh] = st * jnp.exp(g_last) + _mm3(_tn, il, k * jnp.exp(g_last - G))
                o_ref[r, oc:oc + LANE] = o
                rstd = lax.rsqrt(jnp.mean(o * o, axis=-1, keepdims=True) + EPS)
                y_ref[r, oc:oc + LANE] = (o * rstd * gn_ref[...] * (gl * jax.nn.sigmoid(gl))).astype(BF16)
            return carry

        lax.fori_loop(0, NC, chunk, 0)

    gw = HG_GROUP * LANE
    cb = C_HG // (4 * gw)
    return pl.pallas_call(
        body, name="hgrn_fwd", grid=(B, ng),
        in_specs=[pl.BlockSpec((T, 4 * gw), lambda b, h: (b, cb + h)), pl.BlockSpec((lb.shape[0], gw), lambda b, h: (0, h)),
                  pl.BlockSpec((1, LANE), lambda b, h: (0, 0))],
        out_specs=[pl.BlockSpec((T, gw), lambda b, h: (b, h)), pl.BlockSpec((T, gw), lambda b, h: (b, h)),
                   pl.BlockSpec((HG_GROUP * NC, HG_D, HG_D), lambda b, h: (b * ng + h, 0, 0))],
        out_shape=[S((N, 512), BF16), S((N, 512), F32), S((B * HG_H * NC, HG_D, HG_D), F32)],
        scratch_shapes=[pltpu.VMEM((HG_GROUP, HG_D, HG_D), F32)], compiler_params=_cp(("parallel", "parallel")),
    )(z, lb, gn)


def _hgrn_bwd(z, o_raw, states, dy, lb, gn, B, T):
    N = B * T
    NC = T // CHUNK
    ng = HG_H // HG_GROUP

    def body(z_ref, o_ref, st_ref, dy_ref, lb_ref, gn_ref, dz_ref, dlb_ref, dgn_ref, ds_scr, racc, dgn_acc):
        lbs = _lower_bound(lb_ref[...])
        gn_v = gn_ref[...]
        tri, triu = _tri(CHUNK), _tri(CHUNK, upper=True)
        cmask = _iota((CHUNK, CHUNK), 0) >= _iota((CHUNK, CHUNK), 1)
        for ref in (ds_scr, racc, dgn_acc, dlb_ref):
            ref[...] = jnp.zeros_like(ref)

        def chunk(ci, carry):
            c = NC - 1 - ci
            r = pl.ds(pl.multiple_of(c * CHUNK, CHUNK), CHUNK)
            for hh in range(HG_GROUP):
                zc, oc = 4 * LANE * hh, LANE * hh
                lb_v = lbs[:, oc:oc + LANE]
                ql, fl, il, gl = (z_ref[r, zc + LANE * j:zc + LANE * (j + 1)] for j in range(4))
                sig, f, k, logf = _hg_gates(fl, lb_v)
                G = _prefix_mm(tri, logf)
                qs, dsilu_q = _silu_and_grad(ql)
                gs, dsilu_g = _silu_and_grad(gl)
                o = o_ref[r, oc:oc + LANE]
                dyv = dy_ref[r, oc:oc + LANE]
                rstd = lax.rsqrt(jnp.mean(o * o, axis=-1, keepdims=True) + EPS)
                oh = o * rstd
                dgl = dyv * oh * gn_v * dsilu_g
                dn = dyv * gs
                dgn_acc[...] += _rowsum8(dn * oh)
                u = dn * gn_v
                do = rstd * (u - oh * jnp.mean(u * oh, axis=-1, keepdims=True))
                st = st_ref[hh * NC + c]
                dst = ds_scr[hh]
                eG = jnp.exp(G)
                g_last = G[CHUNK - 1:CHUNK, :]
                eL = jnp.exp(g_last - G)
                dA = jnp.where(cmask, _mm3(_nt, do, il), 0.0)
                A, (dq_in, dk_in) = _hg_intra_A(qs, k, G), _hg_intra_bwd(dA, qs, k, G)
                di = _tn(_b(A), _b(do)) + _nt(_b(k * eL), _b(dst))
                dq = dq_in + _mm3(_nn, do, st) * eG
                dk = dk_in + _mm3(_nn, il, dst) * eL
                ds_scr[hh] = dst * jnp.exp(g_last) + _mm3(_tn, do, qs * eG)
                dd = qs * dq - k * dk
                dlogf = _prefix_mm(triu, dd) + racc[hh]
                racc[hh] += jnp.sum(dd, axis=0, keepdims=True)
                df = dlogf / f - dk
                dlb_ref[8 * hh:8 * (hh + 1), :] += _rowsum8(df * (1.0 - sig))
                dz_ref[r, zc:zc + LANE] = (dq * dsilu_q).astype(BF16)
                dz_ref[r, zc + LANE:zc + 2 * LANE] = (df * (1.0 - lb_v) * sig * (1.0 - sig)).astype(BF16)
                dz_ref[r, zc + 2 * LANE:zc + 3 * LANE] = di.astype(BF16)
                dz_ref[r, zc + 3 * LANE:zc + 4 * LANE] = dgl.astype(BF16)
            return carry

        lax.fori_loop(0, NC, chunk, 0)
        dgn_ref[...] = dgn_acc[...]

    gw = HG_GROUP * LANE
    cb = C_HG // (4 * gw)
    col = pl.BlockSpec((T, gw), lambda b, h: (b, h))
    return pl.pallas_call(
        body, name="hgrn_bwd", grid=(B, ng),
        in_specs=[pl.BlockSpec((T, 4 * gw), lambda b, h: (b, cb + h)), col,
                  pl.BlockSpec((HG_GROUP * NC, HG_D, HG_D), lambda b, h: (b * ng + h, 0, 0)), col,
                  pl.BlockSpec((lb.shape[0], gw), lambda b, h: (0, h)), pl.BlockSpec((1, LANE), lambda b, h: (0, 0))],
        out_specs=[pl.BlockSpec((T, 4 * gw), lambda b, h: (b, h)),
                   pl.BlockSpec((8 * HG_GROUP, LANE), lambda b, h: (b * ng + h, 0)),
                   pl.BlockSpec((8, LANE), lambda b, h: (b * ng + h, 0))],
        out_shape=[S((N, 2048), BF16), S((B * HG_H * 8, LANE), F32), S((B * ng * 8, LANE), F32)],
        scratch_shapes=[pltpu.VMEM((HG_GROUP, HG_D, HG_D), F32), pltpu.VMEM((HG_GROUP, 1, LANE), F32),
                        pltpu.VMEM((8, LANE), F32)],
        compiler_params=_cp(("parallel", "parallel")),
    )(z, o_raw, states, dy, lb, gn)


def _pair_mean(x, lo_half):
    a = jnp.sum(jnp.where(lo_half, x, 0.0), axis=-1, keepdims=True)
    b = jnp.sum(jnp.where(lo_half, 0.0, x), axis=-1, keepdims=True)
    return jnp.where(lo_half, a, b) * (1.0 / FOX_D)


def _fox_gate_fwd(z, bias, B, T):
    N = B * T
    tb = LANE

    def body(z_ref, b_ref, fc_ref, fct_ref):
        tri = _tri(tb)

        def step(i, carry):
            r = pl.ds(pl.multiple_of(i * tb, tb), tb)
            cs = _prefix_mm(tri, jax.nn.log_sigmoid(z_ref[r, :] + b_ref[...])) + carry
            fc_ref[r, :] = cs
            fct_ref[0, :, r] = cs.T[0:8, :]
            return cs[tb - 1:tb, :]

        lax.fori_loop(0, T // tb, step, jnp.zeros((1, LANE), F32))

    return pl.pallas_call(
        body, name="fox_gate_fwd", grid=(B,),
        in_specs=[pl.BlockSpec((T, LANE), lambda b: (b, C_FF // LANE)), pl.BlockSpec((1, LANE), lambda b: (0, 0))],
        out_specs=[pl.BlockSpec((T, LANE), lambda b: (b, 0)), pl.BlockSpec((1, 8, T), lambda b: (b, 0, 0))],
        out_shape=[S((N, LANE), F32), S((B, 8, T), F32)], compiler_params=_cp(("parallel",)),
    )(z, bias)


def _fox_gate_bwd(dfc, z, bias, B, T):
    N = B * T
    tb = LANE
    nt = T // tb

    def body(d_ref, z_ref, b_ref, dz_ref, db_ref):
        triu = _tri(tb, upper=True)
        db_ref[...] = jnp.zeros_like(db_ref)

        def step(ii, carry):
            r = pl.ds(pl.multiple_of((nt - 1 - ii) * tb, tb), tb)
            d = d_ref[r, 0:LANE]
            for p in range(1, FOX_P):
                d = d + d_ref[r, LANE * p:LANE * (p + 1)]
            rc = _prefix_mm(triu, d) + carry
            dff = rc * jax.nn.sigmoid(-(z_ref[r, :] + b_ref[...]))
            dz_ref[r, :] = dff.astype(BF16)
            db_ref[...] += _rowsum8(dff)
            return carry + jnp.sum(d, axis=0, keepdims=True)

        lax.fori_loop(0, nt, step, jnp.zeros((1, LANE), F32))

    return pl.pallas_call(
        body, name="fox_gate_bwd", grid=(B,),
        in_specs=[pl.BlockSpec((T, 512), lambda b: (b, 0)), pl.BlockSpec((T, LANE), lambda b: (b, C_FF // LANE)),
                  pl.BlockSpec((1, LANE), lambda b: (0, 0))],
        out_specs=[pl.BlockSpec((T, LANE), lambda b: (b, 0)), pl.BlockSpec((8, LANE), lambda b: (b, 0))],
        out_shape=[S((N, LANE), BF16), S((B * 8, LANE), F32)], compiler_params=_cp(("parallel",)),
    )(dfc, z, bias)


def _fox_prep(z_ref, gq, gk, r, lo_half):
    q, k, v = z_ref[r, 0:LANE], z_ref[r, LANE:2 * LANE], z_ref[r, 2 * LANE:3 * LANE]
    rq = lax.rsqrt(_pair_mean(q * q, lo_half) + EPS)
    rk = lax.rsqrt(_pair_mean(k * k, lo_half) + EPS)
    qh, kh = q * rq, k * rk
    return qh * gq * (FOX_D ** -0.5), kh * gk, v, qh, kh, rq, rk


def _fox_fwd(z, fc, fct, gq, gk, B, T, tq=512, gather=()):
    N = B * T
    NQ = T // tq
    nga = len(gather)

    def body(z_ref, fc_ref, fct_ref, gq_ref, gk_ref, y_ref, lse_ref, qn_s, kn_s, v_s):
        p, qi = pl.program_id(1), pl.program_id(2)
        lo_half = _iota((1, LANE), 1) < FOX_D

        @pl.when(qi == 0)
        def _():
            def prep(i, carry):
                r = pl.ds(pl.multiple_of(i * tq, tq), tq)
                qn, kn, v = _fox_prep(z_ref, gq_ref[...], gk_ref[...], r, lo_half)[:3]
                qn_s[r, :], kn_s[r, :], v_s[r, :] = qn.astype(BF16), kn.astype(BF16), v.astype(BF16)
                return carry
            lax.fori_loop(0, NQ, prep, 0)

        rq = pl.ds(pl.multiple_of(qi * tq, tq), tq)
        qn = qn_s[rq, :]
        fcq = fc_ref[rq, :]
        lane = _iota((tq, LANE), 1)
        causal = _iota((tq, tq), 0) >= _iota((tq, tq), 1)
        qhs = [jnp.where(lo_half, qn, jnp.zeros_like(qn)), jnp.where(lo_half, jnp.zeros_like(qn), qn)]
        fqs = [jnp.sum(jnp.where(lane == 2 * p + hh, fcq, 0.0), axis=-1, keepdims=True) for hh in range(2)]

        def kv(j, carry, diagonal):
            rk = pl.ds(pl.multiple_of(j * tq, tq), tq)
            kj, vj = kn_s[rk, :], v_s[rk, :]
            new = []
            for hh in range(2):
                m, l, acc = carry[hh]
                s = _nt(qhs[hh], kj) + fqs[hh] - fct_ref[0, pl.ds(2 * p + hh, 1), rk]
                if diagonal:
                    s = jnp.where(causal, s, NEG)
                m_new = jnp.maximum(m, jnp.max(s, axis=-1, keepdims=True))
                pe = jnp.exp(s - m_new)
                alpha = jnp.exp(m - m_new)
                new.append((m_new, alpha * l + jnp.sum(pe, axis=-1, keepdims=True),
                            alpha * acc + _nn(pe.astype(BF16), vj)))
            return tuple(new)

        init = tuple((jnp.full((tq, 1), NEG, F32), jnp.zeros((tq, 1), F32), jnp.zeros((tq, LANE), F32)) for _ in range(2))
        carry = lax.fori_loop(0, qi, functools.partial(kv, diagonal=False), init)
        (m0, l0, a0), (m1, l1, a1) = kv(qi, carry, True)
        y_ref[...] = jnp.where(lo_half, a0 / l0, a1 / l1).astype(BF16)
        lse_ref[...] = jnp.where(lo_half, m0 + jnp.log(l0), m1 + jnp.log(l1))

    vec = pl.BlockSpec((1, LANE), lambda b, p, q: (0, 0))
    tile = pl.BlockSpec((tq, LANE), lambda b, p, q: (b * NQ + q, p))
    if nga:
        body = _hosting(body, 5, 2, 3, nga, _gather_phases, (B, FOX_P, NQ))
    return pl.pallas_call(
        body, name="fox_fwd", grid=(B, FOX_P, NQ),
        in_specs=[pl.BlockSpec((T, 384), lambda b, p, q: (b, p)), pl.BlockSpec((T, LANE), lambda b, p, q: (b, 0)),
                  pl.BlockSpec((1, 8, T), lambda b, p, q: (b, 0, 0)), vec, vec] + [ANY] * nga,
        out_specs=[tile, tile] + [ANY] * nga, out_shape=[S((N, 512), BF16), S((N, 512), F32)] + _gather_shapes(gather),
        scratch_shapes=[pltpu.VMEM((T, LANE), BF16)] * 3 + (_gather_sems(nga) if nga else []),
        compiler_params=_cp(("arbitrary",) * 3 if nga else ("parallel", "parallel", "arbitrary")),
    )(z, fc, fct, gq, gk, *gather)


def _fox_bwd(z, dy, y, lse, fc, fct, gq, gk, B, T, tq=512, swap=()):
    N = B * T
    NQ = T // tq
    nsw = len(swap)

    def body(z_ref, dy_ref, y_ref, lse_ref, fc_ref, fct_ref, gq_ref, gk_ref, dz_ref, dfc_ref, dgq_ref, dgk_ref,
             qn_s, kn_s, v_s, do_s, delta_s, dq_s, dfk_s):
        p, kj = pl.program_id(1), pl.program_id(2)
        lo_half = _iota((1, LANE), 1) < FOX_D
        lane = _iota((tq, LANE), 1)
        gq_v, gk_v = gq_ref[...], gk_ref[...]

        @pl.when(kj == 0)
        def _():
            def prep(i, carry):
                r = pl.ds(pl.multiple_of(i * tq, tq), tq)
                qn, kn, v = _fox_prep(z_ref, gq_v, gk_v, r, lo_half)[:3]
                qn_s[r, :], kn_s[r, :], v_s[r, :] = qn.astype(BF16), kn.astype(BF16), v.astype(BF16)
                do = dy_ref[r, :]
                do_s[r, :] = do.astype(BF16)
                delta_s[r, :] = _pair_mean(do * y_ref[r, :].astype(F32), lo_half) * float(FOX_D)
                return carry
            lax.fori_loop(0, NQ, prep, 0)
            dq_s[...] = jnp.zeros_like(dq_s)
            dgq_ref[...] = jnp.zeros_like(dgq_ref)
            dgk_ref[...] = jnp.zeros_like(dgk_ref)

        rk = pl.ds(pl.multiple_of(kj * tq, tq), tq)
        kn, vv = kn_s[rk, :], v_s[rk, :]
        causal = _iota((tq, tq), 0) >= _iota((tq, tq), 1)
        zero, one = jnp.zeros_like(kn), jnp.ones_like(kn)
        hms = [lo_half, jnp.logical_not(lo_half)]
        kmasks = [jnp.where(hm, kn, zero) for hm in hms]
        kaugs = [jnp.where(hm, kn, one) for hm in hms]
        vmasks = [jnp.where(hm, vv, zero) for hm in hms]
        fks = [fct_ref[0, pl.ds(2 * p + hh, 1), rk] for hh in range(2)]

        def qloop(i, carry, diagonal):
            ri = pl.ds(pl.multiple_of(i * tq, tq), tq)
            qn = qn_s[ri, :]
            do = do_s[ri, :]
            fcq = fc_ref[ri, :]
            new = []
            for hh in range(2):
                dk_acc, dv_acc = carry[hh]
                c0 = FOX_D * hh
                fq = jnp.sum(jnp.where(lane == 2 * p + hh, fcq, 0.0), axis=-1, keepdims=True)
                pr = jnp.exp(_nt(qn, kmasks[hh]) + fq - fks[hh] - lse_ref[ri, c0:c0 + 1])
                if diagonal:
                    pr = jnp.where(causal, pr, 0.0)
                ds = (pr * (_nt(do, vmasks[hh]) - delta_s[ri, c0:c0 + 1])).astype(BF16)
                dq_s[hh, ri, :] += _nn(ds, kaugs[hh])
                new.append((dk_acc + _tn(jnp.where(hms[hh], qn, one), ds), dv_acc + _tn(do, pr.astype(BF16))))
            return tuple(new)

        init = tuple((jnp.zeros((LANE, tq), F32), jnp.zeros((LANE, tq), F32)) for _ in range(2))
        carry = qloop(kj, init, True)
        (dk0, dv0), (dk1, dv1) = lax.fori_loop(kj + 1, NQ, functools.partial(qloop, diagonal=False), carry)
        dks, dvs = [dk0.T, dk1.T], [dv0.T, dv1.T]

        dkn = jnp.where(lo_half, dks[0], dks[1])
        _, _, _, _, kh, _, rkk = _fox_prep(z_ref, gq_v, gk_v, rk, lo_half)
        u = dkn * gk_v
        dz_ref[rk, LANE:2 * LANE] = (rkk * (u - kh * _pair_mean(u * kh, lo_half))).astype(BF16)
        dz_ref[rk, 2 * LANE:3 * LANE] = jnp.where(lo_half, dvs[0], dvs[1]).astype(BF16)
        dgk_ref[...] += _rowsum8(dkn * kh)
        dfk_s[rk, :] = jnp.where(lane == 2 * p, -dks[0][:, FOX_D:FOX_D + 1],
                                 jnp.where(lane == 2 * p + 1, -dks[1][:, 0:1], 0.0))

        @pl.when(kj == NQ - 1)
        def _():
            def fin(i, carry):
                r = pl.ds(pl.multiple_of(i * tq, tq), tq)
                d0, d1 = dq_s[0, r, :], dq_s[1, r, :]
                dqn = jnp.where(lo_half, d0, d1)
                _, _, _, qh, _, rqq, _ = _fox_prep(z_ref, gq_v, gk_v, r, lo_half)
                u = dqn * gq_v * (FOX_D ** -0.5)
                dz_ref[r, 0:LANE] = (rqq * (u - qh * _pair_mean(u * qh, lo_half))).astype(BF16)
                dgq_ref[...] += _rowsum8(dqn * qh) * (FOX_D ** -0.5)
                dfc_ref[r, :] = dfk_s[r, :] + jnp.where(lane == 2 * p, d0[:, FOX_D:FOX_D + 1],
                                                        jnp.where(lane == 2 * p + 1, d1[:, 0:1], 0.0))
                return carry
            lax.fori_loop(0, NQ, fin, 0)

    vec = pl.BlockSpec((1, LANE), lambda b, p, k: (0, 0))
    col = pl.BlockSpec((T, LANE), lambda b, p, k: (b, p))
    part = pl.BlockSpec((8, LANE), lambda b, p, k: (b * FOX_P + p, 0))
    if nsw:
        body = _hosting(body, 8, 4, 7, nsw, _chip_swap_phases, (B, FOX_P, NQ))
    return pl.pallas_call(
        body, name="fox_bwd", grid=(B, FOX_P, NQ),
        in_specs=[pl.BlockSpec((T, 384), lambda b, p, k: (b, p)), col, col, col,
                  pl.BlockSpec((T, LANE), lambda b, p, k: (b, 0)), pl.BlockSpec((1, 8, T), lambda b, p, k: (b, 0, 0)),
                  vec, vec] + [ANY] * nsw,
        out_specs=[pl.BlockSpec((T, 384), lambda b, p, k: (b, p)), col, part, part] + [ANY] * nsw,
        out_shape=[S((N, 1536), BF16), S((N, 512), F32), S((B * FOX_P * 8, LANE), F32), S((B * FOX_P * 8, LANE), F32)]
        + [S(p.shape, p.dtype) for p in swap],
        scratch_shapes=[pltpu.VMEM((T, LANE), BF16)] * 4 + [pltpu.VMEM((T, LANE), F32), pltpu.VMEM((2, T, LANE), F32),
                                                            pltpu.VMEM((T, LANE), F32)]
        + (_chip_swap_sems(nsw) if nsw else []),
        compiler_params=_cp(("arbitrary",) * 3 if nsw else ("parallel", "parallel", "arbitrary")),
    )(z, dy, y, lse, fc, fct, gq, gk, *swap)


def _mem_scores(z_ref, kv_ref, gq, gk, h):
    c = slice(MEM_D * h, MEM_D * (h + 1))
    q, k = z_ref[:, c], kv_ref[:, c]
    rq = lax.rsqrt(jnp.mean(q * q, axis=-1, keepdims=True) + EPS)
    rk = lax.rsqrt(jnp.mean(k * k, axis=-1, keepdims=True) + EPS)
    qh, kh = q * rq, k * rk
    qn = (qh * gq * (MEM_D ** -0.5)).astype(BF16)
    kn = (kh * gk).astype(BF16)
    s = _nt(qn, kn)
    pe = jnp.exp(s - jnp.max(s, axis=-1, keepdims=True))
    pn = pe / jnp.sum(pe, axis=-1, keepdims=True)
    return pn, qn, kn, qh, kh, rq, rk


def _mem_fwd(z, memkv, gq, gk, B, T, M, tq=512):
    N = B * T
    NQ = T // tq
    W = MEM_H * MEM_D

    def body(z_ref, kv_ref, gq_ref, gk_ref, y_ref):
        for h in range(MEM_H):
            pn = _mem_scores(z_ref, kv_ref, gq_ref[...], gk_ref[...], h)[0]
            v = kv_ref[:, W + MEM_D * h:W + MEM_D * (h + 1)].astype(BF16)
            y_ref[:, MEM_D * h:MEM_D * (h + 1)] = _nn(pn.astype(BF16), v).astype(BF16)

    vec = pl.BlockSpec((1, LANE), lambda b, q: (0, 0))
    return pl.pallas_call(
        body, name="mem_fwd", grid=(B, NQ),
        in_specs=[pl.BlockSpec((tq, W), lambda b, q: (b * NQ + q, C_MQ // W)),
                  pl.BlockSpec((M, 2 * W), lambda b, q: (b, 0)), vec, vec],
        out_specs=pl.BlockSpec((tq, W), lambda b, q: (b * NQ + q, 0)), out_shape=S((N, W), BF16),
        compiler_params=_cp(("parallel", "parallel")),
    )(z, memkv, gq, gk)


def _mem_bwd(z, memkv, dy, gq, gk, B, T, M, tq=512):
    N = B * T
    NQ = T // tq
    W = MEM_H * MEM_D

    def body(z_ref, kv_ref, dy_ref, gq_ref, gk_ref, dz_ref, dkv_ref, dgq_ref, dgk_ref, acc):
        qi = pl.program_id(1)
        gq_v, gk_v = gq_ref[...], gk_ref[...]

        @pl.when(qi == 0)
        def _():
            acc[...] = jnp.zeros_like(acc)
            dgq_ref[...] = jnp.zeros_like(dgq_ref)
            dgk_ref[...] = jnp.zeros_like(dgk_ref)

        for h in range(MEM_H):
            c = slice(MEM_D * h, MEM_D * (h + 1))
            cv = slice(W + MEM_D * h, W + MEM_D * (h + 1))
            pn, qn, kn, qh, _, rq, _ = _mem_scores(z_ref, kv_ref, gq_v, gk_v, h)
            do = dy_ref[:, c].astype(BF16)
            dp = _nt(do, kv_ref[:, cv].astype(BF16))
            ds = (pn * (dp - jnp.sum(dp * pn, axis=-1, keepdims=True))).astype(BF16)
            dqn = _nn(ds, kn)
            acc[:, c] += _tn(ds, qn)
            acc[:, cv] += _tn(pn.astype(BF16), do)
            u = dqn * gq_v * (MEM_D ** -0.5)
            dz_ref[:, c] = (rq * (u - qh * jnp.mean(u * qh, axis=-1, keepdims=True))).astype(BF16)
            dgq_ref[...] += _rowsum8(dqn * qh) * (MEM_D ** -0.5)

        @pl.when(qi == NQ - 1)
        def _():
            for h in range(MEM_H):
                c = slice(MEM_D * h, MEM_D * (h + 1))
                cv = slice(W + MEM_D * h, W + MEM_D * (h + 1))
                k = kv_ref[:, c]
                rk = lax.rsqrt(jnp.mean(k * k, axis=-1, keepdims=True) + EPS)
                kh = k * rk
                dkn = acc[:, c]
                u = dkn * gk_v
                dkv_ref[:, c] = (rk * (u - kh * jnp.mean(u * kh, axis=-1, keepdims=True))).astype(BF16)
                dkv_ref[:, cv] = acc[:, cv].astype(BF16)
                dgk_ref[...] += _rowsum8(dkn * kh)

    vec = pl.BlockSpec((1, LANE), lambda b, q: (0, 0))
    part = pl.BlockSpec((8, LANE), lambda b, q: (b, 0))
    return pl.pallas_call(
        body, name="mem_bwd", grid=(B, NQ),
        in_specs=[pl.BlockSpec((tq, W), lambda b, q: (b * NQ + q, C_MQ // W)),
                  pl.BlockSpec((M, 2 * W), lambda b, q: (b, 0)), pl.BlockSpec((tq, W), lambda b, q: (b * NQ + q, 0)),
                  vec, vec],
        out_specs=[pl.BlockSpec((tq, W), lambda b, q: (b * NQ + q, 0)), pl.BlockSpec((M, 2 * W), lambda b, q: (b, 0)),
                   part, part],
        out_shape=[S((N, W), BF16), S((B * M, 2 * W), BF16), S((B * 8, LANE), F32), S((B * 8, LANE), F32)],
        scratch_shapes=[pltpu.VMEM((M, 2 * W), F32)], compiler_params=_cp(("parallel", "arbitrary")),
    )(z, memkv, dy, gq, gk)


def _merge_fwd(ya, yb, yc, z, x, wa, wb, wc, wo, tm=256):
    n, d = x.shape
    wdt = ya.shape[1]
    gb = C_GATE // d

    def body(ya_ref, yb_ref, yc_ref, g0_ref, g1_ref, g2_ref, x_ref, wa_ref, wb_ref, wc_ref, wo_ref,
             x1_ref, mg_ref, ua_ref, ub_ref, uc_ref):
        merged = jnp.zeros((tm, d), F32)
        for y_ref, g_ref, w_ref, u_ref in ((ya_ref, g0_ref, wa_ref, ua_ref), (yb_ref, g1_ref, wb_ref, ub_ref),
                                           (yc_ref, g2_ref, wc_ref, uc_ref)):
            u = _nn(y_ref[...], w_ref[...])
            u_ref[...] = u.astype(BF16)
            merged = merged + jax.nn.sigmoid(g_ref[...]) * u
        mb = merged.astype(BF16)
        mg_ref[...] = mb
        x1_ref[...] = x_ref[...] + _nn(mb, wo_ref[...])

    yt = pl.BlockSpec((tm, wdt), lambda i: (i, 0))
    xt = pl.BlockSpec((tm, d), lambda i: (i, 0))
    wbr = pl.BlockSpec((wdt, d), lambda i: (0, 0))
    gates = [pl.BlockSpec((tm, d), functools.partial(lambda i, k: (i, gb + k), k=k)) for k in range(3)]
    return pl.pallas_call(
        body, name="merge_fwd", grid=(n // tm,),
        in_specs=[yt, yt, yt] + gates + [xt, wbr, wbr, wbr, pl.BlockSpec((d, d), lambda i: (0, 0))],
        out_specs=[xt] * 5, out_shape=[S((n, d), F32)] + [S((n, d), BF16)] * 4, compiler_params=_cp(("parallel",)),
    )(ya, yb, yc, z, z, z, x, wa, wb, wc, wo)


def _merge_bwd(dx1, z, ua, ub, uc, wa, wb, wc, wo, tm=256):
    n, d = dx1.shape
    wdt = wa.shape[0]
    gb = C_GATE // d

    def body(dx_ref, g0_ref, g1_ref, g2_ref, ua_ref, ub_ref, uc_ref, wa_ref, wb_ref, wc_ref, wo_ref,
             dg_ref, dya_ref, dyb_ref, dyc_ref, dua_ref, dub_ref, duc_ref):
        dm = _nt(dx_ref[...].astype(BF16), wo_ref[...])
        for k, (g_ref, u_ref, w_ref, dy_ref, du_ref) in enumerate((
                (g0_ref, ua_ref, wa_ref, dya_ref, dua_ref), (g1_ref, ub_ref, wb_ref, dyb_ref, dub_ref),
                (g2_ref, uc_ref, wc_ref, dyc_ref, duc_ref))):
            g = jax.nn.sigmoid(g_ref[...])
            du = (dm * g).astype(BF16)
            du_ref[...] = du
            dg_ref[:, d * k:d * (k + 1)] = (dm * u_ref[...].astype(F32) * g * (1.0 - g)).astype(BF16)
            dy_ref[...] = _nt(du, w_ref[...])

    yt = pl.BlockSpec((tm, wdt), lambda i: (i, 0))
    xt = pl.BlockSpec((tm, d), lambda i: (i, 0))
    wbr = pl.BlockSpec((wdt, d), lambda i: (0, 0))
    gates = [pl.BlockSpec((tm, d), functools.partial(lambda i, k: (i, gb + k), k=k)) for k in range(3)]
    return pl.pallas_call(
        body, name="merge_bwd", grid=(n // tm,),
        in_specs=[xt] + gates + [xt, xt, xt, wbr, wbr, wbr, pl.BlockSpec((d, d), lambda i: (0, 0))],
        out_specs=[pl.BlockSpec((tm, 3 * d), lambda i: (i, 0)), yt, yt, yt, xt, xt, xt],
        out_shape=[S((n, 3 * d), BF16)] + [S((n, wdt), F32)] * 3 + [S((n, d), BF16)] * 3,
        compiler_params=_cp(("parallel",)),
    )(dx1, z, z, z, ua, ub, uc, wa, wb, wc, wo)


FFN_TN = 1408
TN_TM = 2048
INV_SQRT2 = 0.7071067811865476
INV_SQRT_2PI = 0.3989422804014327


def _conv_shifted(a, prev, first, tm):
    row = _iota(a.shape, 0)
    p7 = jnp.where(first, 0.0, prev[7:8, :])
    p6 = jnp.where(first, 0.0, prev[6:7, :])
    a1 = jnp.where(row == 0, p7, pltpu.roll(a, 1, 0))
    a2 = jnp.where(row == 0, p6, jnp.where(row == 1, p7, pltpu.roll(a, 2, 0)))
    return a1, a2


def _ffn_act_fwd(up, cw, cb, B, T, tm=256):
    N = B * T
    dff = cw.shape[1]
    NT, NJ, tn = T // tm, dff // FFN_TN, FFN_TN

    def body(a_ref, v_ref, cw_ref, cb_ref, y_ref, c_ref, carry):
        t = pl.program_id(2)
        a = a_ref[...].astype(F32)
        a1, a2 = _conv_shifted(a, carry[...], t == 0, tm)
        w = cw_ref[...]
        ac = w[0:1, :] * a2 + w[1:2, :] * a1 + w[2:3, :] * a + cb_ref[...]
        cdf = 0.5 * (1.0 + lax.erf(ac * INV_SQRT2))
        y_ref[...] = (ac * cdf * v_ref[...].astype(F32)).astype(BF16)
        c_ref[...] = cdf.astype(BF16)
        carry[...] = a[tm - 8:tm, :]

    return pl.pallas_call(
        body, name="ffn_act_fwd", grid=(B, NJ, NT),
        in_specs=[pl.BlockSpec((tm, tn), lambda b, j, t: (b * NT + t, j)),
                  pl.BlockSpec((tm, tn), lambda b, j, t: (b * NT + t, NJ + j)),
                  pl.BlockSpec((3, tn), lambda b, j, t: (0, j)), pl.BlockSpec((1, tn), lambda b, j, t: (0, j))],
        out_specs=[pl.BlockSpec((tm, tn), lambda b, j, t: (b * NT + t, j))] * 2, out_shape=[S((N, dff), BF16)] * 2,
        scratch_shapes=[pltpu.VMEM((8, tn), F32)], compiler_params=_cp(("parallel", "parallel", "arbitrary")),
    )(up, up, cw, cb)


def _ffn_down_loss(y, wd, x1, tgt, tm=256):
    n, d = x1.shape
    kf = y.shape[1]

    def body(y_ref, w_ref, x_ref, t_ref, dx_ref, ls_ref):
        err = x_ref[...] + _nn(y_ref[...], w_ref[...]) - t_ref[...]
        dx_ref[...] = err * (1.0 / d)

        @pl.when(pl.program_id(0) == 0)
        def _():
            ls_ref[...] = jnp.zeros_like(ls_ref)

        ls_ref[...] += _rowsum8(err * err) * (0.5 / d)

    xt = pl.BlockSpec((tm, d), lambda i: (i, 0))
    return pl.pallas_call(
        body, name="ffn_down_loss", grid=(n // tm,),
        in_specs=[pl.BlockSpec((tm, kf), lambda i: (i, 0)), pl.BlockSpec((kf, d), lambda i: (0, 0)), xt, xt],
        out_specs=[xt, pl.BlockSpec((8, d), lambda i: (0, 0))], out_shape=[S((n, d), F32), S((8, d), F32)],
        compiler_params=_cp(("arbitrary",)),
    )(y, wd, x1, tgt)


def _ffn_act_bwd1(dx2, wd, up, cdf, cw, cb, B, T, tm=256):
    N = B * T
    d = dx2.shape[1]
    dff = cw.shape[1]
    NT, NJ, tn = T // tm, dff // FFN_TN, FFN_TN

    def body(dx_ref, w_ref, a_ref, v_ref, c_ref, cw_ref, cb_ref, dac_ref, dv_ref, dcw_ref, dcb_ref, carry):
        b, t = pl.program_id(1), pl.program_id(2)
        a = a_ref[...].astype(F32)
        a1, a2 = _conv_shifted(a, carry[...], t == 0, tm)
        carry[...] = a[tm - 8:tm, :]
        w = cw_ref[...]
        ac = w[0:1, :] * a2 + w[1:2, :] * a1 + w[2:3, :] * a + cb_ref[...]
        dy = _nt(dx_ref[...].astype(BF16), w_ref[...])
        cdf = c_ref[...].astype(F32)
        dv_ref[...] = (dy * ac * cdf).astype(BF16)
        dac = dy * v_ref[...].astype(F32) * (cdf + ac * jnp.exp(-0.5 * ac * ac) * INV_SQRT_2PI)
        dac_ref[...] = dac

        @pl.when((b == 0) & (t == 0))
        def _():
            dcw_ref[...] = jnp.zeros_like(dcw_ref)
            dcb_ref[...] = jnp.zeros_like(dcb_ref)

        dcw_ref[0:8, :] += _rowsum8(dac * a2)
        dcw_ref[8:16, :] += _rowsum8(dac * a1)
        dcw_ref[16:24, :] += _rowsum8(dac * a)
        dcb_ref[...] += _rowsum8(dac)

    return pl.pallas_call(
        body, name="ffn_act_bwd1", grid=(NJ, B, NT),
        in_specs=[pl.BlockSpec((tm, d), lambda j, b, t: (b * NT + t, 0)), pl.BlockSpec((tn, d), lambda j, b, t: (j, 0)),
                  pl.BlockSpec((tm, tn), lambda j, b, t: (b * NT + t, j)),
                  pl.BlockSpec((tm, tn), lambda j, b, t: (b * NT + t, NJ + j)),
                  pl.BlockSpec((tm, tn), lambda j, b, t: (b * NT + t, j)),
                  pl.BlockSpec((3, tn), lambda j, b, t: (0, j)), pl.BlockSpec((1, tn), lambda j, b, t: (0, j))],
        out_specs=[pl.BlockSpec((tm, tn), lambda j, b, t: (b * NT + t, j)),
                   pl.BlockSpec((tm, tn), lambda j, b, t: (b * NT + t, j)),
                   pl.BlockSpec((24, tn), lambda j, b, t: (0, j)), pl.BlockSpec((8, tn), lambda j, b, t: (0, j))],
        out_shape=[S((N, dff), F32), S((N, dff), BF16), S((24, dff), F32), S((8, dff), F32)],
        scratch_shapes=[pltpu.VMEM((8, tn), F32)], compiler_params=_cp(("parallel", "arbitrary", "arbitrary")),
    )(dx2, wd, up, up, cdf, cw, cb)


def _ffn_act_bwd2(dac, cw, B, T, tm=256):
    N = B * T
    dff = cw.shape[1]
    NT, NJ, tn = T // tm, dff // FFN_TN, FFN_TN
    last8 = N // 8 - 1

    def body(d_ref, nx_ref, cw_ref, da_ref):
        t = pl.program_id(2)
        dd = d_ref[...]
        row = _iota(dd.shape, 0)
        last = t == NT - 1
        n0 = jnp.where(last, 0.0, nx_ref[0:1, :])
        n1 = jnp.where(last, 0.0, nx_ref[1:2, :])
        d1 = jnp.where(row == tm - 1, n0, pltpu.roll(dd, tm - 1, 0))
        d2 = jnp.where(row == tm - 1, n1, jnp.where(row == tm - 2, n0, pltpu.roll(dd, tm - 2, 0)))
        w = cw_ref[...]
        da_ref[...] = (w[2:3, :] * dd + w[1:2, :] * d1 + w[0:1, :] * d2).astype(BF16)

    return pl.pallas_call(
        body, name="ffn_act_bwd2", grid=(B, NJ, NT),
        in_specs=[pl.BlockSpec((tm, tn), lambda b, j, t: (b * NT + t, j)),
                  pl.BlockSpec((8, tn), lambda b, j, t: (jnp.minimum((b * NT + t + 1) * (tm // 8), last8), j)),
                  pl.BlockSpec((3, tn), lambda b, j, t: (0, j))],
        out_specs=pl.BlockSpec((tm, tn), lambda b, j, t: (b * NT + t, j)), out_shape=S((N, dff), BF16),
        compiler_params=_cp(("parallel", "parallel", "parallel")),
    )(dac, dac, cw)


def _fold_rows(p, name):
    r, c = p.shape[0] // 8, p.shape[1]

    def body(p_ref, o_ref):
        for j in range(r):
            o_ref[j:j + 1, :] = jnp.sum(p_ref[8 * j:8 * (j + 1), :], axis=0, keepdims=True)

    return pl.pallas_call(body, name=name, out_shape=S((r, c), F32), compiler_params=_cp())(p)


def _small_reduce(lbl, dg_mix, dg_mem, dlb_p, dgn_p, dfb_p, dgq_p, dgk_p, dmq_p, dmk_p, dg_ffn, dcb_p, loss_p):
    d, dff = dg_mix.shape[1], dcb_p.shape[1]
    nbh = dlb_p.shape[0] // (8 * HG_H)

    def colsum(ref):
        return jnp.sum(ref[...], axis=0, keepdims=True)

    def body(lbl_ref, mix_ref, mem_ref, dlb_ref, dgn_ref, dfb_ref, dgq_ref, dgk_ref, dmq_ref, dmk_ref, ffn_ref, dcb_ref,
             ls_ref, o_mix, o_mem, o_lb, o_hgn, o_fb, o_fq, o_fk, o_mq, o_mk, o_ffn, o_cb, o_loss):
        o_mix[...], o_mem[...], o_ffn[...], o_cb[...] = colsum(mix_ref), colsum(mem_ref), colsum(ffn_ref), colsum(dcb_ref)
        o_hgn[...], o_fb[...], o_mq[...], o_mk[...] = colsum(dgn_ref), colsum(dfb_ref), colsum(dmq_ref), colsum(dmk_ref)
        for src, dst in ((dgq_ref, o_fq), (dgk_ref, o_fk)):
            v = colsum(src)
            dst[...] = v + pltpu.roll(v, FOX_D, 1)
        o_loss[...] = jnp.zeros((1, LANE), F32) + jnp.sum(colsum(ls_ref), axis=-1, keepdims=True)
        logits = lbl_ref[...]
        e = jnp.exp(logits - jnp.max(logits, axis=0, keepdims=True))
        pr = e / jnp.sum(e, axis=0, keepdims=True)
        rows = _iota((8, LANE), 0)
        for h in range(HG_H):
            acc = jnp.zeros((8, LANE), F32)
            for b in range(nbh):
                acc = acc + dlb_ref[8 * (b * HG_H + h):8 * (b * HG_H + h + 1), :]
            dlb = jnp.sum(acc, axis=0, keepdims=True)
            c = slice(LANE * h, LANE * (h + 1))
            p0 = pr[0:1, c]
            first = _iota((logits.shape[0], LANE), 0) == 0
            o_lb[:, c] = pr[:, c] * (jnp.where(first, 1.0, 0.0) - p0) * dlb

    outs = [S((1, d), F32), S((1, d), F32), S(lbl.shape, F32)] + [S((1, LANE), F32)] * 6 + \
           [S((1, d), F32), S((1, dff), F32), S((1, LANE), F32)]
    return pl.pallas_call(body, name="small_reduce", out_shape=outs, compiler_params=_cp())(
        lbl, dg_mix, dg_mem, dlb_p, dgn_p, dfb_p, dgq_p, dgk_p, dmq_p, dmk_p, dg_ffn, dcb_p, loss_p)


def _in_col_pieces():
    hw, fw = HG_H * HG_D, FOX_H * FOX_D
    fox0, ff0 = 4 * hw, 4 * hw + 3 * fw
    mq0 = ff0 + FOX_H
    gate0 = mq0 + MEM_H * MEM_D
    pieces = []
    for p in range(FOX_P):
        pieces += [(fox0 + j * fw + LANE * p, LANE) for j in range(3)]
    pieces.append((mq0, MEM_H * MEM_D))
    for h in range(HG_H):
        pieces += [(j * hw + HG_D * h, HG_D) for j in range(4)]
    pieces.append((gate0, C_FF - C_GATE))
    pieces.append((ff0, FOX_H))
    return pieces


def _perm_from_blocks(blocks):
    n_blk, _, c = blocks.shape
    parts = []
    for s, n in _in_col_pieces():
        lo = s
        while lo < s + n:
            d = lo // c
            hi = min(s + n, (d + 1) * c)
            parts.append(blocks[d][:, lo - d * c:hi - d * c])
            lo = hi
    parts.append(jnp.zeros((blocks.shape[1], C_END - C_FF - FOX_H), blocks.dtype))
    return jnp.concatenate(parts, axis=1)


def _unperm_blocks(segs, n_blk):
    starts = [0]
    for a in segs:
        starts.append(starts[-1] + a.shape[1])
    new_start, placed = 0, []
    for s, n in _in_col_pieces():
        placed.append((s, new_start, n))
        new_start += n
    placed.sort()
    c = sum(n for _, _, n in placed) // n_blk
    blocks = []
    for d in range(n_blk):
        parts = []
        for s, ns, n in placed:
            lo, hi = max(s, d * c), min(s + n, (d + 1) * c)
            if lo < hi:
                i = max(j for j in range(len(segs)) if starts[j] <= ns)
                parts.append(segs[i][:, ns + lo - s - starts[i]:ns + hi - s - starts[i]])
        blocks.append(jnp.concatenate(parts, axis=1))
    return jnp.stack(blocks)


def _local_step(x2, mem2, tgt, sm, W, B, T, M, ex=None):
    fbias = jnp.pad(sm["fox_f_bias"], ((0, 0), (0, LANE - FOX_H)))
    gq2 = jnp.concatenate([sm["fox_q_norm_g"]] * 2, axis=1)
    gk2 = jnp.concatenate([sm["fox_k_norm_g"]] * 2, axis=1)
    lbl = sm["hgrn_lb_logits"]
    h = _rmsnorm_cast(x2, sm["norm_mix_g"], "norm_mix")
    z = _mm_nn(h, W["w_in"], F32, "proj_in", 512, 2432)
    memn = _rmsnorm_cast(mem2, sm["norm_mem_g"], "norm_mem", tm=256)
    memkv = _mm_nn(memn, W["mem_kv_w"], F32, "proj_memkv", 256, 512)
    ya, o_raw, states = _hgrn_fwd(z, lbl, sm["hgrn_norm_g"], B, T)
    fc, fct = _fox_gate_fwd(z, fbias, B, T)
    yb, lse, *late = _fox_fwd(z, fc, fct, gq2, gk2, B, T, gather=ex.late_blocks() if ex else ())
    if ex:
        W = {**W, **ex.unpack_late(late)}
    yc = _mem_fwd(z, memkv, sm["mem_q_norm_g"], sm["mem_k_norm_g"], B, T, M)
    x1, merged, ua, ub, uc = _merge_fwd(ya, yb, yc, z, x2, W["w_br_hgrn"], W["w_br_fox"], W["w_br_mem"], W["w_out"])
    h2 = _rmsnorm_cast(x1, sm["norm_ffn_g"], "norm_ffn")
    up = _mm_nn(h2, W["ffn_w_up"], BF16, "ffn_up", 512, FFN_TN)
    yf, cdf = _ffn_act_fwd(up, W["ffn_conv_w"], sm["ffn_conv_b"], B, T)
    dx2, loss_p = _ffn_down_loss(yf, W["ffn_w_down"], x1, tgt)
    dff = W["ffn_conv_w"].shape[1]
    dac, dv, dcw_p, dcb_p = _ffn_act_bwd1(dx2, W["ffn_w_down"], up, cdf, W["ffn_conv_w"], sm["ffn_conv_b"], B, T)
    da = _ffn_act_bwd2(dac, W["ffn_conv_w"], B, T)
    g = {"ffn_conv_w": _fold_rows(dcw_p, "g_conv_w")}
    g["ffn_w_down"] = _mm_tn(yf, dx2, "g_w_down", TN_TM, 512)
    dh2 = _mm_nt_sum([(da, 0, dff, 0), (dv, 0, dff, dff)], W["ffn_w_up"], "dh2", 256)
    g["ffn_w_up"] = [_mm_tn(h2, da, "g_w_up_a", TN_TM, FFN_TN), _mm_tn(h2, dv, "g_w_up_v", TN_TM, FFN_TN)]
    dx1, dg_ffn = _rmsnorm_bwd(dh2, x1, sm["norm_ffn_g"], dx2, "norm_ffn_bwd")
    g["w_out"] = _mm_tn(merged, dx1, "g_w_out", TN_TM, 512)
    dgate, dya, dyb, dyc, dua, dub, duc = _merge_bwd(dx1, z, ua, ub, uc, W["w_br_hgrn"], W["w_br_fox"], W["w_br_mem"],
                                                    W["w_out"])
    g["w_br_hgrn"] = _mm_tn(ya, dua, "g_w_br_hgrn", TN_TM, 512)
    g["w_br_fox"] = _mm_tn(yb, dub, "g_w_br_fox", TN_TM, 512)
    g["w_br_mem"] = _mm_tn(yc, duc, "g_w_br_mem", TN_TM, 512)
    if ex:
        early_pk = ex.early_grads(g)
        early_sib, early_pair = ex.sibling_round(early_pk, "early")
    dz_fox, dfc, dgq_p, dgk_p, *early_chips = _fox_bwd(z, dyb, yb, lse, fc, fct, gq2, gk2, B, T,
                                                       swap=early_pair if ex else ())
    dz_hg, dlb_p, dgn_p = _hgrn_bwd(z, o_raw, states, dya, lbl, sm["hgrn_norm_g"], B, T)
    dz_ff, dfb_p = _fox_gate_bwd(dfc, z, fbias, B, T)
    dz_mq, dkv, dmq_p, dmk_p = _mem_bwd(z, memkv, dyc, sm["mem_q_norm_g"], sm["mem_k_norm_g"], B, T, M)
    g["mem_kv_w"] = _mm_tn(memn, dkv, "g_mem_kv_w", 256, 512)
    dmemn = _mm_nt(dkv, W["mem_kv_w"], "d_memn", 256, 512)
    _, dg_mem = _rmsnorm_bwd(dmemn, mem2, sm["norm_mem_g"], None, "norm_mem_bwd", tm=256)
    d = x2.shape[1]
    parts = [(dz_fox, 0, C_MQ - C_FOX, C_FOX), (dz_mq, 0, C_HG - C_MQ, C_MQ), (dz_hg, 0, C_GATE - C_HG, C_HG)]
    parts += [(dgate, d * k, d, C_GATE + d * k) for k in range(3)] + [(dz_ff, 0, C_END - C_FF, C_FF)]
    g["w_in"] = [_mm_tn(h, dzs, "g_w_in_%d" % i, TN_TM, min(512, dzs.shape[1]))
                 for i, dzs in enumerate((dz_fox, dz_mq, dz_hg, dgate, dz_ff))]
    sums = None
    if ex:
        last_pk = ex.last_grads(g)
        last_sib, last_pair = ex.sibling_round(last_pk, "last")
        dh, last_chips = _mm_nt_sum(parts, W["w_in"], "dh", 256, swap=last_pair)
        sums = (ex.final_sums(early_pk, early_sib, early_chips, "early"),
                ex.final_sums(last_pk, last_sib, last_chips, "last"))
    else:
        dh = _mm_nt_sum(parts, W["w_in"], "dh", 256)
    grad_x, dg_mix = _rmsnorm_bwd(dh, x2, sm["norm_mix_g"], dx1, "norm_mix_bwd")
    small = _small_reduce(lbl, dg_mix, dg_mem, dlb_p, dgn_p, dfb_p, dgq_p, dgk_p, dmq_p, dmk_p, dg_ffn, dcb_p, loss_p)
    names = ("norm_mix_g", "norm_mem_g", "hgrn_lb_logits", "hgrn_norm_g", "fox_f_bias", "fox_q_norm_g", "fox_k_norm_g",
             "mem_q_norm_g", "mem_k_norm_g", "norm_ffn_g", "ffn_conv_b", "loss")
    g.update(dict(zip(names, small)))
    return grad_x, g, sums


ANY = pl.BlockSpec(memory_space=pl.ANY)


def _position():
    return lax.axis_index("x"), lax.axis_index("y"), lax.axis_index("c")


def _all_gather(blocks, name):
    nb = len(blocks)

    def body(*refs):
        start, forward, finish = _gather_phases(refs[:nb], refs[nb:2 * nb], *refs[2 * nb:])
        start()
        forward()
        finish()

    return pl.pallas_call(
        body, name=name, out_shape=_gather_shapes(blocks), in_specs=[ANY] * nb, out_specs=[ANY] * nb,
        scratch_shapes=_gather_sems(nb),
    )(*blocks)


def _hosting(body, n_in, n_out, n_scratch, n_x, make_phases, grid):
    n_steps = math.prod(grid)

    def hosted(*refs):
        a = n_in + n_x
        b = a + n_out + n_x
        ins, xs = refs[:n_in], refs[n_in:a]
        outs, x_outs = refs[a:a + n_out], refs[a + n_out:b]
        scratch, sems = refs[b:b + n_scratch], refs[b + n_scratch:]
        step = 0
        for ax, n in enumerate(grid):
            step = step * n + pl.program_id(ax)
        phases = make_phases(xs, x_outs, *sems)
        pl.when(step == 0)(phases[0])
        for ph in phases[1:-1]:
            pl.when(step == n_steps // 2)(ph)
        body(*ins, *outs, *scratch)
        pl.when(step == n_steps - 1)(phases[-1])

    return hosted


def _gather_shapes(blocks):
    return [S((N_DEV,) + b.shape, b.dtype) for b in blocks]


def _gather_sems(nb):
    return [pltpu.SemaphoreType.DMA((7 * nb,)), pltpu.SemaphoreType.DMA((7 * nb,)), pltpu.SemaphoreType.DMA((nb,))]


def _gather_phases(x_refs, out_refs, send_sems, recv_sems, local_sems):
    nb = len(x_refs)
    x, y, c = _position()
    me, sibling = (x, y, c), (x, y, 1 - c)
    chips = [(1 - x, y), (x, 1 - y), (1 - x, 1 - y)]

    def copy(i, k, blk, to, own=False):
        px, py, pc = blk
        slot = out_refs[i].at[4 * px + 2 * py + pc]
        return pltpu.make_async_remote_copy(
            src_ref=x_refs[i] if own else slot, dst_ref=slot, send_sem=send_sems.at[7 * i + k],
            recv_sem=recv_sems.at[7 * i + k], device_id=to, device_id_type=MESH)

    def mine(i):
        return pltpu.make_async_copy(x_refs[i], out_refs[i].at[4 * x + 2 * y + c], local_sems.at[i])

    def first(i):
        return [copy(i, 0, me, sibling, own=True)] + [copy(i, 1 + j, me, (*chip, c), own=True)
                                                     for j, chip in enumerate(chips)]

    def passed(i, j):
        return copy(i, 4 + j, (*chips[j], c), sibling)

    def start():
        for i in range(nb):
            mine(i).start()
            for cp in first(i):
                cp.start()

    def forward():
        for i in range(nb):
            for j, chip in enumerate(chips):
                copy(i, 1 + j, (*chip, c), me).wait_recv()
                passed(i, j).start()

    def finish():
        for i in range(nb):
            copy(i, 0, sibling, me).wait_recv()
            for j, chip in enumerate(chips):
                copy(i, 4 + j, (*chip, 1 - c), me).wait_recv()
        for i in range(nb):
            for cp in first(i) + [passed(i, j) for j in range(3)]:
                cp.wait_send()
            mine(i).wait()

    return start, forward, finish


def _swap_with_sibling(pks, name):
    nb = len(pks)

    def body(*refs):
        pk_refs, out_refs = refs[:nb], refs[nb:2 * nb]
        send_sems, recv_sems = refs[2 * nb:]
        x, y, c = _position()
        copies = [pltpu.make_async_remote_copy(
            src_ref=pk_refs[i].at[2 * k + 1 - c], dst_ref=out_refs[i].at[k], send_sem=send_sems.at[4 * i + k],
            recv_sem=recv_sems.at[4 * i + k], device_id=(x, y, 1 - c), device_id_type=MESH)
            for i in range(nb) for k in range(4)]
        for cp in copies:
            cp.start()
        for cp in copies:
            cp.wait()

    return pl.pallas_call(
        body, name=name, out_shape=[S((4,) + p.shape[1:], p.dtype) for p in pks], in_specs=[ANY] * nb,
        out_specs=[ANY] * nb, scratch_shapes=[pltpu.SemaphoreType.DMA((4 * nb,)), pltpu.SemaphoreType.DMA((4 * nb,))],
    )(*pks)


def _swap_between_chips(pbs, name):
    nb = len(pbs)

    def body(*refs):
        start, finish = _chip_swap_phases(refs[:nb], refs[nb:2 * nb], *refs[2 * nb:])
        start()
        finish()

    return pl.pallas_call(
        body, name=name, out_shape=[S(p.shape, p.dtype) for p in pbs], in_specs=[ANY] * nb, out_specs=[ANY] * nb,
        scratch_shapes=_chip_swap_sems(nb),
    )(*pbs)


def _chip_swap_sems(nb):
    return [pltpu.SemaphoreType.DMA((3 * nb,)), pltpu.SemaphoreType.DMA((3 * nb,)), pltpu.SemaphoreType.DMA((nb,))]


def _chip_swap_phases(pb_refs, out_refs, send_sems, recv_sems, local_sems):
    nb = len(pb_refs)
    x, y, c = _position()
    me = 2 * x + y
    chips = [(1 - x, y), (x, 1 - y), (1 - x, 1 - y)]

    def local(i):
        return pltpu.make_async_copy(pb_refs[i].at[me], out_refs[i].at[me], local_sems.at[i])

    def send(i, j):
        cx, cy = chips[j]
        return pltpu.make_async_remote_copy(
            src_ref=pb_refs[i].at[2 * cx + cy], dst_ref=out_refs[i].at[me], send_sem=send_sems.at[3 * i + j],
            recv_sem=recv_sems.at[3 * i + j], device_id=(cx, cy, c), device_id_type=MESH)

    def arrival(i, j):
        cx, cy = chips[j]
        return pltpu.make_async_remote_copy(
            src_ref=pb_refs[i].at[me], dst_ref=out_refs[i].at[2 * cx + cy], send_sem=send_sems.at[3 * i + j],
            recv_sem=recv_sems.at[3 * i + j], device_id=(cx, cy, c), device_id_type=MESH)

    def start():
        for i in range(nb):
            local(i).start()
            for j in range(3):
                send(i, j).start()

    def finish():
        for i in range(nb):
            for j in range(3):
                arrival(i, j).wait_recv()
        for i in range(nb):
            for j in range(3):
                send(i, j).wait_send()
            local(i).wait()

    return start, finish


def _row_tile(r):
    return max(t for t in range(16, min(r, 512) + 1, 16) if r % t == 0)


def _pair_sum_cast(pk, recv, core, name):
    _, r, l = pk.shape
    tr = _row_tile(r)

    def body(c_ref, a_ref, b_ref, o_ref):
        o_ref[...] = (a_ref[...] + b_ref[...]).astype(BF16)

    return pl.pallas_call(
        body, name=name,
        grid_spec=pltpu.PrefetchScalarGridSpec(
            num_scalar_prefetch=1, grid=(4, r // tr),
            in_specs=[pl.BlockSpec((None, tr, l), lambda k, i, c: (2 * k + c[0], i, 0)),
                      pl.BlockSpec((None, tr, l), lambda k, i, c: (k, i, 0))],
            out_specs=pl.BlockSpec((None, tr, l), lambda k, i, c: (k, i, 0))),
        out_shape=S((4, r, l), BF16), compiler_params=_cp(("parallel", "parallel")),
    )(core, pk, recv)


def _final_sum(pk, recv_sib, recv_chips, slot, chip, name):
    _, r, l = pk.shape
    tr = _row_tile(r)

    def body(s_ref, k_ref, a_ref, b_ref, rc_ref, o_ref):
        base = a_ref[...] + b_ref[...]
        acc = jnp.zeros_like(base)
        for j in range(4):
            acc = acc + jnp.where(k_ref[0] == j, base, rc_ref[j].astype(F32))
        o_ref[...] = acc

    return pl.pallas_call(
        body, name=name,
        grid_spec=pltpu.PrefetchScalarGridSpec(
            num_scalar_prefetch=2, grid=(r // tr,),
            in_specs=[pl.BlockSpec((None, tr, l), lambda i, s, k: (s[0], i, 0)),
                      pl.BlockSpec((None, tr, l), lambda i, s, k: (k[0], i, 0)),
                      pl.BlockSpec((4, tr, l), lambda i, s, k: (0, i, 0))],
            out_specs=pl.BlockSpec((tr, l), lambda i, s, k: (i, 0))),
        out_shape=S((r, l), F32), compiler_params=_cp(("parallel",)),
    )(slot, chip, pk, recv_sib, recv_chips)


def _adamw_math(w, g, m, v):
    m = ADAM_B1 * m + (1.0 - ADAM_B1) * g
    v = ADAM_B2 * v + (1.0 - ADAM_B2) * (g * g)
    m_hat = m / (1.0 - ADAM_B1 ** ADAM_STEP)
    v_hat = v / (1.0 - ADAM_B2 ** ADAM_STEP)
    return -ADAM_LR * (m_hat / (jnp.sqrt(v_hat) + ADAM_EPS) + ADAM_WD * w), m, v


def _adamw(w, g, m, v, name):
    r, c = w.shape
    tr = 256 if r % 256 == 0 else r

    def body(w_ref, g_ref, m_ref, v_ref, d_ref, nm_ref, nv_ref):
        d_ref[...], nm_ref[...], nv_ref[...] = _adamw_math(w_ref[...], g_ref[...], m_ref[...], v_ref[...])

    tile = pl.BlockSpec((tr, c), lambda i: (i, 0))
    return pl.pallas_call(
        body, name=name, grid=(r // tr,), in_specs=[tile] * 4, out_specs=[tile] * 3, out_shape=[S((r, c), F32)] * 3,
        compiler_params=_cp(("parallel",)),
    )(w, g, m, v)


def _small_update(gathered, w, m, v):
    def body(ga_ref, w_ref, m_ref, v_ref, g_ref, d_ref, nm_ref, nv_ref):
        g = ga_ref[0]
        for k in range(1, N_DEV):
            g = g + ga_ref[k]
        g_ref[...] = g
        d_ref[...], nm_ref[...], nv_ref[...] = _adamw_math(w_ref[...], g, m_ref[...], v_ref[...])

    return pl.pallas_call(body, name="small_update", out_shape=[S(w.shape, F32)] * 4, compiler_params=_cp())(
        gathered, w, m, v)


BIG = ("w_in", "mem_kv_w", "w_br_hgrn", "w_br_fox", "w_br_mem", "w_out", "ffn_w_up", "ffn_conv_w", "ffn_w_down")
GROUP_ROWS = ("w_out", "ffn_w_down")
GROUP_LANE = ("w_br_hgrn", "w_br_fox", "w_br_mem")
LANE_GROUP_ROWS = 224
SMALL = ("norm_mix_g", "norm_mem_g", "hgrn_lb_logits", "hgrn_norm_g", "fox_f_bias", "fox_q_norm_g", "fox_k_norm_g",
         "mem_q_norm_g", "mem_k_norm_g", "norm_ffn_g", "ffn_conv_b")


def _rows_of(n_elems):
    return -(-n_elems // LANE)


def _to_rows(a, lead=0):
    flat = a.reshape(a.shape[:lead] + (-1,))
    pad = (-flat.shape[-1]) % LANE
    if pad:
        flat = jnp.pad(flat, [(0, 0)] * lead + [(0, pad)])
    return flat.reshape(a.shape[:lead] + (-1, LANE))


def _stack_rows(parts, lead, total_rows):
    buf = jnp.concatenate(parts, axis=lead)
    pad = total_rows - buf.shape[lead]
    return jnp.pad(buf, [(0, 0)] * lead + [(0, pad), (0, 0)])


def _round_up(n, k):
    return -(-n // k) * k


def _from_rows(rows, shape, lead=0):
    n = math.prod(shape)
    return rows.reshape(rows.shape[:lead] + (-1,))[..., :n].reshape(rows.shape[:lead] + tuple(shape))


def _blocks_to_full(blocks, kind):
    n, a, b = blocks.shape
    return blocks.transpose(1, 0, 2).reshape(a, n * b) if kind == "col" else blocks.reshape(n * a, b)


def _full_to_blocks(full, kind, n=N_DEV):
    a, b = full.shape
    return full.reshape(a, n, b // n).transpose(1, 0, 2) if kind == "col" else full.reshape(n, a // n, b)


def _lane_group_rows(shard):
    n_lane = sum(shard[n].shape[0] for n in GROUP_LANE)
    n_cw = shard["ffn_conv_w"].size
    return n_lane, _rows_of(3 * n_cw), _rows_of(n_cw), _round_up(n_lane + _rows_of(3 * n_cw), LANE_GROUP_ROWS)


def _split_bf16x3(x):
    hi = x.astype(BF16)
    r1 = x - hi.astype(F32)
    mid = r1.astype(BF16)
    return jnp.stack([hi, mid, (r1 - mid.astype(F32)).astype(BF16)])


class _Exchange:
    def __init__(self, shard):
        self.shard = shard
        xi, yi, ci = _position()
        self.core = ci.astype(jnp.int32).reshape(1)
        self.chip = (2 * xi + yi).astype(jnp.int32).reshape(1)
        self.n_lane, self.r_pieces, self.r_vals, self.r_lane = _lane_group_rows(shard)

    def first_blocks(self):
        return [self.shard["w_in"].astype(BF16), self.shard["mem_kv_w"].astype(BF16)]

    def unpack_first(self, gathered):
        return {"w_in": _perm_from_blocks(gathered[0]), "mem_kv_w": _blocks_to_full(gathered[1], "row")}

    def late_blocks(self):
        sh = self.shard
        lane_rows = [sh[n].astype(BF16) for n in GROUP_LANE] + [_to_rows(_split_bf16x3(sh["ffn_conv_w"]))]
        return [jnp.concatenate([sh[n].astype(BF16) for n in GROUP_ROWS], axis=0), sh["ffn_w_up"].astype(BF16),
                _stack_rows(lane_rows, 0, self.r_lane)]

    def unpack_late(self, gathered):
        gb, gc, gd = gathered
        sh = self.shard
        W = {"ffn_w_up": _blocks_to_full(gc, "col")}
        r0 = 0
        for n in GROUP_ROWS:
            W[n] = _blocks_to_full(gb[:, r0:r0 + sh[n].shape[0]], "row")
            r0 += sh[n].shape[0]
        r0 = 0
        for n in GROUP_LANE:
            W[n] = _blocks_to_full(gd[:, r0:r0 + sh[n].shape[0]], "col")
            r0 += sh[n].shape[0]
        cw = _from_rows(gd[:, self.n_lane:self.n_lane + self.r_pieces], (3,) + sh["ffn_conv_w"].shape, lead=1).astype(F32)
        W["ffn_conv_w"] = _blocks_to_full(cw[:, 0] + cw[:, 1] + cw[:, 2], "col")
        return W

    def early_grads(self, g):
        cw_rows = _to_rows(_full_to_blocks(g["ffn_conv_w"], "col"), lead=1)
        return [jnp.concatenate([_full_to_blocks(g[n], "row") for n in GROUP_ROWS], axis=1),
                jnp.concatenate([_full_to_blocks(h, "col", N_DEV // 2) for h in g["ffn_w_up"]], axis=0),
                _stack_rows([_full_to_blocks(g[n], "col") for n in GROUP_LANE] + [cw_rows], 1, self.r_lane)]

    def last_grads(self, g):
        return [_unperm_blocks(g["w_in"], N_DEV), _full_to_blocks(g["mem_kv_w"], "row")]

    def sibling_round(self, pks, tag):
        recv = _swap_with_sibling(pks, "rs_sibling_" + tag)
        pair = [_pair_sum_cast(p, r, self.core, "rs_pair_sum_%s%d" % (tag, i)) for i, (p, r) in enumerate(zip(pks, recv))]
        return recv, pair

    def final_sums(self, pks, recv_sib, recv_chips, tag):
        return [_final_sum(p, rs, rc, 2 * self.chip + self.core, self.chip, "rs_final_sum_%s%d" % (tag, i))
                for i, (p, rs, rc) in enumerate(zip(pks, recv_sib, recv_chips))]

    def unpack_grads(self, early, last):
        sh = self.shard
        g_shard = {"w_in": last[0], "mem_kv_w": last[1], "ffn_w_up": early[1]}
        r0 = 0
        for n in GROUP_ROWS:
            g_shard[n] = early[0][r0:r0 + sh[n].shape[0]]
            r0 += sh[n].shape[0]
        r0 = 0
        for n in GROUP_LANE:
            g_shard[n] = early[2][r0:r0 + sh[n].shape[0]]
            r0 += sh[n].shape[0]
        g_shard["ffn_conv_w"] = _from_rows(early[2][self.n_lane:self.n_lane + self.r_vals], sh["ffn_conv_w"].shape)
        return g_shard


def kernel(x, mem, norm_mix_g, norm_mem_g, w_in, hgrn_lb_logits, hgrn_norm_g, fox_f_bias, fox_q_norm_g, fox_k_norm_g, mem_kv_w, mem_q_norm_g, mem_k_norm_g, w_br_hgrn, w_br_fox, w_br_mem, w_out, norm_ffn_g, ffn_w_up, ffn_conv_w, ffn_conv_b, ffn_w_down, loss_target, m_norm_mix_g, m_norm_mem_g, m_w_in, m_hgrn_lb_logits, m_hgrn_norm_g, m_fox_f_bias, m_fox_q_norm_g, m_fox_k_norm_g, m_mem_kv_w, m_mem_q_norm_g, m_mem_k_norm_g, m_w_br_hgrn, m_w_br_fox, m_w_br_mem, m_w_out, m_norm_ffn_g, m_ffn_w_up, m_ffn_conv_w, m_ffn_conv_b, m_ffn_w_down, v_norm_mix_g, v_norm_mem_g, v_w_in, v_hgrn_lb_logits, v_hgrn_norm_g, v_fox_f_bias, v_fox_q_norm_g, v_fox_k_norm_g, v_mem_kv_w, v_mem_q_norm_g, v_mem_k_norm_g, v_w_br_hgrn, v_w_br_fox, v_w_br_mem, v_w_out, v_norm_ffn_g, v_ffn_w_up, v_ffn_conv_w, v_ffn_conv_b, v_ffn_w_down):
    given = dict(locals())
    order = ("norm_mix_g", "norm_mem_g", "w_in", "hgrn_lb_logits", "hgrn_norm_g", "fox_f_bias", "fox_q_norm_g",
             "fox_k_norm_g", "mem_kv_w", "mem_q_norm_g", "mem_k_norm_g", "w_br_hgrn", "w_br_fox", "w_br_mem", "w_out",
             "norm_ffn_g", "ffn_w_up", "ffn_conv_w", "ffn_conv_b", "ffn_w_down")
    B, T, D = x.shape
    M = mem.shape[1]
    shard = {n: given[n][0] if n in BIG else given[n] for n in order}
    mom = {n: (given["m_" + n][0], given["v_" + n][0]) if n in BIG else (given["m_" + n], given["v_" + n])
           for n in order}
    shard["hgrn_lb_logits"] = hgrn_lb_logits
    for n in ("norm_mix_g", "norm_mem_g", "hgrn_norm_g", "fox_f_bias", "fox_q_norm_g", "fox_k_norm_g", "mem_q_norm_g",
              "mem_k_norm_g", "norm_ffn_g", "ffn_conv_b"):
        shard[n] = given[n].reshape(1, -1)

    ex = _Exchange(shard)
    W = ex.unpack_first(_all_gather(ex.first_blocks(), "ag_first"))

    sm = {n: shard[n] for n in SMALL}
    grad_x, g, sums = _local_step(x.reshape(B * T, D), mem.reshape(B * M, D), loss_target.reshape(B * T, D), sm, W,
                                  B, T, M, ex)
    g_shard = ex.unpack_grads(*sums)

    sg = {n: g[n] for n in SMALL}
    sg["fox_f_bias"] = g["fox_f_bias"][:, :FOX_H]
    sg["fox_q_norm_g"] = g["fox_q_norm_g"][:, :FOX_D]
    sg["fox_k_norm_g"] = g["fox_k_norm_g"][:, :FOX_D]
    slayout, row0 = {}, 0
    for n in SMALL:
        nr = _rows_of(shard[n].size)
        slayout[n] = (row0, nr)
        row0 += nr
    loss_row = row0
    r_small = _round_up(row0 + 1, 8)

    def pack_small(d, with_loss=None):
        rows = [_to_rows(d[n]) for n in SMALL]
        rows.append(with_loss if with_loss is not None else jnp.zeros((1, LANE), F32))
        return _stack_rows(rows, 0, r_small)

    sgath, = _all_gather([pack_small(sg, g["loss"])], "ag_small")
    s_g, s_d, s_m, s_v = _small_update(sgath, pack_small(shard), pack_small({n: mom[n][0].reshape(shard[n].shape) for n in SMALL}),
                                       pack_small({n: mom[n][1].reshape(shard[n].shape) for n in SMALL}))
    loss = s_g[loss_row, 0]

    grads, deltas, new_m, new_v = {}, {}, {}, {}
    for n in BIG:
        gn = g_shard[n]
        d, nm, nv = _adamw(shard[n], gn, mom[n][0], mom[n][1], "adamw_" + n)
        grads[n], deltas[n], new_m[n], new_v[n] = (a[None] for a in (gn, d, nm, nv))
    for n in SMALL:
        r0, nr = slayout[n]
        for dst, src in ((grads, s_g), (deltas, s_d), (new_m, s_m), (new_v, s_v)):
            dst[n] = _from_rows(src[r0:r0 + nr], given[n].shape)
    return (loss, grad_x.reshape(B, T, D), *[grads[n] for n in order], *[deltas[n] for n in order],
            *[new_m[n] for n in order], *[new_v[n] for n in order])
```

```python
import functools
import math

import jax
import jax.numpy as jnp
from jax import lax
from jax.experimental import pallas as pl
from jax.experimental.pallas import tpu as pltpu

F32, BF16 = jnp.float32, jnp.bfloat16
S = jax.ShapeDtypeStruct
MESH = pl.DeviceIdType.MESH

N_DEV = 8
EPS = 1e-6
LANE = 128
CHUNK = 64
SUB = 16
HG_H, HG_D = 4, 128
HG_GROUP = 2
FOX_H, FOX_D = 8, 64
FOX_P = FOX_H // 2
MEM_H, MEM_D = 4, 128
NEG = -1e30
VMEM_LIMIT = 56 * 2**20

ADAM_LR, ADAM_B1, ADAM_B2, ADAM_EPS, ADAM_WD, ADAM_STEP = 0.001, 0.9, 0.999, 1e-08, 0.01, 10

C_FOX, C_MQ, C_HG, C_GATE, C_FF, C_END = 0, 1536, 2048, 4096, 7168, 7296


def _cp(sem=None):
    return pltpu.CompilerParams(dimension_semantics=sem, vmem_limit_bytes=VMEM_LIMIT)


def _dot(a, b, dims, prec=None):
    return lax.dot_general(a, b, (dims, ((), ())), preferred_element_type=F32, precision=prec)


def _nn(a, b, prec=None):
    return _dot(a, b, ((1,), (0,)), prec)


def _nt(a, b, prec=None):
    return _dot(a, b, ((1,), (1,)), prec)


def _tn(a, b, prec=None):
    return _dot(a, b, ((0,), (0,)), prec)


def _b(x):
    return x.astype(BF16)


def _mm3(fn, a, b):
    ah, bh = _b(a), _b(b)
    return fn(ah, bh) + fn(ah, _b(b - bh.astype(F32))) + fn(_b(a - ah.astype(F32)), bh)


def _iota(shape, dim):
    return lax.broadcasted_iota(jnp.int32, shape, dim)


def _rowsum8(x):
    r, d = x.shape
    return jnp.sum(x.reshape(r // 8, 8, d), axis=0)


def _rmsnorm_cast(x, g, name, tm=512):
    n, d = x.shape

    def body(x_ref, g_ref, o_ref):
        v = x_ref[...]
        r = lax.rsqrt(jnp.mean(v * v, axis=-1, keepdims=True) + EPS)
        o_ref[...] = (v * r * g_ref[...]).astype(BF16)

    return pl.pallas_call(
        body, name=name, grid=(n // tm,),
        in_specs=[pl.BlockSpec((tm, d), lambda i: (i, 0)), pl.BlockSpec((1, d), lambda i: (0, 0))],
        out_specs=pl.BlockSpec((tm, d), lambda i: (i, 0)), out_shape=S((n, d), BF16), compiler_params=_cp(("parallel",)),
    )(x, g)


def _rmsnorm_bwd(dh, x, g, resid, name, tm=512):
    n, d = x.shape
    has_res = resid is not None

    def body(*refs):
        if has_res:
            dh_ref, x_ref, g_ref, r_ref, dx_ref, dg_ref = refs
        else:
            dh_ref, x_ref, g_ref, dx_ref, dg_ref = refs
        v = x_ref[...]
        dhv = dh_ref[...].astype(F32)
        r = lax.rsqrt(jnp.mean(v * v, axis=-1, keepdims=True) + EPS)
        xh = v * r
        u = dhv * g_ref[...]
        dx = r * (u - xh * jnp.mean(u * xh, axis=-1, keepdims=True))
        if has_res:
            dx = dx + r_ref[...]
        dx_ref[...] = dx

        @pl.when(pl.program_id(0) == 0)
        def _():
            dg_ref[...] = jnp.zeros_like(dg_ref)

        dg_ref[...] += _rowsum8(dhv * xh)

    tile = pl.BlockSpec((tm, d), lambda i: (i, 0))
    ins = [tile, tile, pl.BlockSpec((1, d), lambda i: (0, 0))] + ([tile] if has_res else [])
    args = (dh, x, g) + ((resid,) if has_res else ())
    return pl.pallas_call(
        body, name=name, grid=(n // tm,), in_specs=ins,
        out_specs=[tile, pl.BlockSpec((8, d), lambda i: (0, 0))],
        out_shape=[S((n, d), F32), S((8, d), F32)], compiler_params=_cp(("arbitrary",)),
    )(*args)


def _mm_nn(a, b, out_dtype, name, tm, tn, b_col0=0, n_out=None):
    m, k = a.shape
    n_out = b.shape[1] if n_out is None else n_out
    jb = b_col0 // tn
    assert b_col0 % tn == 0 and n_out % tn == 0 and m % tm == 0

    def body(a_ref, b_ref, o_ref):
        o_ref[...] = _nn(a_ref[...].astype(BF16), b_ref[...].astype(BF16)).astype(out_dtype)

    return pl.pallas_call(
        body, name=name, grid=(m // tm, n_out // tn),
        in_specs=[pl.BlockSpec((tm, k), lambda i, j: (i, 0)), pl.BlockSpec((k, tn), lambda i, j: (0, j + jb))],
        out_specs=pl.BlockSpec((tm, tn), lambda i, j: (i, j)), out_shape=S((m, n_out), out_dtype),
        compiler_params=_cp(("parallel", "parallel")),
    )(a, b)


def _mm_nt(dy, w, name, tm, tr, w_col0=0, acc=None):
    m, r = dy.shape
    k = w.shape[0]
    jb = w_col0 // tr
    nr = r // tr
    assert w_col0 % tr == 0 and r % tr == 0 and m % tm == 0
    has_acc = acc is not None

    def body(*refs):
        if has_acc:
            dy_ref, w_ref, acc_ref, o_ref = refs
        else:
            dy_ref, w_ref, o_ref = refs
        part = _nt(dy_ref[...].astype(BF16), w_ref[...].astype(BF16))

        @pl.when(pl.program_id(1) == 0)
        def _():
            o_ref[...] = part + acc_ref[...] if has_acc else part

        @pl.when(pl.program_id(1) > 0)
        def _():
            o_ref[...] += part

    out_tile = pl.BlockSpec((tm, k), lambda i, j: (i, 0))
    ins = [pl.BlockSpec((tm, tr), lambda i, j: (i, j)), pl.BlockSpec((k, tr), lambda i, j: (0, j + jb))]
    args = (dy, w)
    if has_acc:
        ins.append(out_tile)
        args = args + (acc,)
    return pl.pallas_call(
        body, name=name, grid=(m // tm, nr), in_specs=ins, out_specs=out_tile, out_shape=S((m, k), F32),
        input_output_aliases=({2: 0} if has_acc else {}), compiler_params=_cp(("parallel", "arbitrary")),
    )(*args)


def _mm_nt_sum(parts, w, name, tm, swap=()):
    m = parts[0][0].shape[0]
    k = w.shape[0]
    assert m % tm == 0 and all(c % n == 0 and o % n == 0 for _, c, n, o in parts)
    np_ = len(parts)
    nsw = len(swap)
    n_steps = m // tm

    def body(*refs):
        o_ref = refs[2 * np_ + nsw]
        if nsw:
            start, finish = _chip_swap_phases(refs[2 * np_:2 * np_ + nsw], refs[2 * np_ + nsw + 1:2 * np_ + 2 * nsw + 1],
                                              *refs[2 * np_ + 2 * nsw + 1:])
            pl.when(pl.program_id(0) == 0)(start)
        acc = _nt(refs[0][...].astype(BF16), refs[np_][...].astype(BF16))
        for i in range(1, np_):
            acc = acc + _nt(refs[i][...].astype(BF16), refs[np_ + i][...].astype(BF16))
        o_ref[...] = acc
        if nsw:
            pl.when(pl.program_id(0) == n_steps - 1)(finish)

    dy_specs = [pl.BlockSpec((tm, n), functools.partial(lambda i, j: (i, j), j=c // n)) for _, c, n, _ in parts]
    w_specs = [pl.BlockSpec((k, n), functools.partial(lambda i, j: (0, j), j=o // n)) for _, _, n, o in parts]
    out = pl.pallas_call(
        body, name=name, grid=(n_steps,), in_specs=dy_specs + w_specs + [ANY] * nsw,
        out_specs=[pl.BlockSpec((tm, k), lambda i: (i, 0))] + [ANY] * nsw,
        out_shape=[S((m, k), F32)] + [S(p.shape, p.dtype) for p in swap],
        scratch_shapes=_chip_swap_sems(nsw) if nsw else [],
        compiler_params=_cp(("arbitrary",) if nsw else ("parallel",)),
    )(*([p[0] for p in parts] + [w] * np_ + list(swap)))
    return (out[0], out[1:]) if nsw else out[0]


def _mm_tn(x, dy, name, tm, tn):
    m, k = x.shape
    n = dy.shape[1]
    tm = min(tm, m)
    assert m % tm == 0 and n % tn == 0

    def body(x_ref, dy_ref, o_ref):
        part = _tn(x_ref[...].astype(BF16), dy_ref[...].astype(BF16))

        @pl.when(pl.program_id(1) == 0)
        def _():
            o_ref[...] = part

        @pl.when(pl.program_id(1) > 0)
        def _():
            o_ref[...] += part

    return pl.pallas_call(
        body, name=name, grid=(n // tn, m // tm),
        in_specs=[pl.BlockSpec((tm, k), lambda j, i: (i, 0)), pl.BlockSpec((tm, tn), lambda j, i: (i, j))],
        out_specs=pl.BlockSpec((k, tn), lambda j, i: (0, j)), out_shape=S((k, n), F32),
        compiler_params=_cp(("parallel", "arbitrary")),
    )(x, dy)


def _lower_bound(logits):
    e = jnp.exp(logits - jnp.max(logits, axis=0, keepdims=True))
    return e[0:1, :] / jnp.sum(e, axis=0, keepdims=True)


def _hg_gates(fl, lb):
    sig = jax.nn.sigmoid(fl)
    f = lb + (1.0 - lb) * sig
    k = (1.0 - lb) * (1.0 - sig)
    return sig, f, k, jnp.log(f)


def _silu_and_grad(x):
    s = jax.nn.sigmoid(x)
    return x * s, s * (1.0 + x * (1.0 - s))


def _hg_rowblocks(G):
    return [None] + [G[SUB * i - 1:SUB * i, :] for i in range(1, CHUNK // SUB)]


def _hg_intra_A(qs, k, G):
    refs = _hg_rowblocks(G)
    cols = _iota((SUB, CHUNK), 1)
    rows = _iota((SUB, CHUNK), 0)
    blocks = []
    for i in range(CHUNK // SUB):
        lo = SUB * i
        qb, Gb = qs[lo:lo + SUB, :], G[lo:lo + SUB, :]
        diag = jnp.zeros((SUB, CHUNK), F32)
        for s in range(SUB):
            e = jnp.exp(jnp.minimum(Gb - G[lo + s:lo + s + 1, :], 0.0))
            col = jnp.sum(qb * k[lo + s:lo + s + 1, :] * e, axis=-1, keepdims=True)
            diag = jnp.where(cols == lo + s, col, diag)
        a = jnp.where((cols >= lo) & (cols <= rows + lo), diag, 0.0)
        if i > 0:
            qr = qb * jnp.exp(Gb - refs[i])
            kr = k * jnp.exp(jnp.minimum(refs[i] - G, 0.0))
            a = jnp.where(cols < lo, _nt(_b(qr), _b(kr)), a)
        blocks.append(a)
    return jnp.concatenate(blocks, axis=0)


def _hg_intra_bwd(dA, qs, k, G):
    refs = _hg_rowblocks(G)
    cols = _iota((SUB, CHUNK), 1)
    rows16 = _iota((SUB, HG_D), 0)
    dk = jnp.zeros((CHUNK, HG_D), F32)
    dq_blocks, dk_diag_blocks = [], []
    for i in range(CHUNK // SUB):
        lo = SUB * i
        qb, Gb = qs[lo:lo + SUB, :], G[lo:lo + SUB, :]
        dAb = dA[lo:lo + SUB, :]
        dq = jnp.zeros((SUB, HG_D), F32)
        dkb = jnp.zeros((SUB, HG_D), F32)
        for s in range(SUB):
            e = jnp.exp(jnp.minimum(Gb - G[lo + s:lo + s + 1, :], 0.0))
            e = jnp.where(rows16 >= s, e, 0.0)
            dcol = jnp.sum(jnp.where(cols == lo + s, dAb, 0.0), axis=-1, keepdims=True)
            w = dcol * e
            dq = dq + w * k[lo + s:lo + s + 1, :]
            dkb = jnp.where(rows16 == s, jnp.sum(w * qb, axis=0, keepdims=True), dkb)
        if i > 0:
            e1 = jnp.exp(Gb - refs[i])
            e2 = jnp.exp(jnp.minimum(refs[i] - G, 0.0))
            dA_off = jnp.where(cols < lo, dAb, 0.0)
            dq = dq + _mm3(_nn, dA_off, k * e2) * e1
            dk = dk + _mm3(_tn, dA_off, qb * e1) * e2
        dq_blocks.append(dq)
        dk_diag_blocks.append(dkb)
    return jnp.concatenate(dq_blocks, axis=0), dk + jnp.concatenate(dk_diag_blocks, axis=0)


def _tri(n, upper=False):
    r, c = _iota((n, n), 0), _iota((n, n), 1)
    return jnp.where((c >= r) if upper else (r >= c), 1.0, 0.0).astype(BF16)


def _prefix_mm(tri, x):
    hi = x.astype(BF16)
    r1 = x - hi.astype(F32)
    mid = r1.astype(BF16)
    lo = (r1 - mid.astype(F32)).astype(BF16)
    return _nn(tri, hi) + _nn(tri, mid) + _nn(tri, lo)


def _hgrn_fwd(z, lb, gn, B, T):
    N = B * T
    NC = T // CHUNK
    ng = HG_H // HG_GROUP

    def body(z_ref, lb_ref, gn_ref, y_ref, o_ref, st_ref, s_scr):
        lbs = _lower_bound(lb_ref[...])
        tri = _tri(CHUNK)
        s_scr[...] = jnp.zeros_like(s_scr)

        def chunk(c, carry):
            r = pl.ds(pl.multiple_of(c * CHUNK, CHUNK), CHUNK)
            for hh in range(HG_GROUP):
                zc, oc = 4 * LANE * hh, LANE * hh
                ql, fl, il, gl = (z_ref[r, zc + LANE * j:zc + LANE * (j + 1)] for j in range(4))
                _, _, k, logf = _hg_gates(fl, lbs[:, oc:oc + LANE])
                G = _prefix_mm(tri, logf)
                qs = ql * jax.nn.sigmoid(ql)
                st = s_scr[hh]
                st_ref[hh * NC + c] = st
                g_last = G[CHUNK - 1:CHUNK, :]
                A = _hg_intra_A(qs, k, G)
                o = _nn(_b(A), _b(il)) + _nt(_b(qs * jnp.exp(G)), _b(st))
                s_scr[hh] = st * jnp.exp(g_last) + _mm3(_tn, il, k * jnp.exp(g_last - G))
                o_ref[r, oc:oc + LANE] = o
                rstd = lax.rsqrt(jnp.mean(o * o, axis=-1, keepdims=True) + EPS)
                y_ref[r, oc:oc + LANE] = (o * rstd * gn_ref[...] * (gl * jax.nn.sigmoid(gl))).astype(BF16)
            return carry

        lax.fori_loop(0, NC, chunk, 0)

    gw = HG_GROUP * LANE
    cb = C_HG // (4 * gw)
    return pl.pallas_call(
        body, name="hgrn_fwd", grid=(B, ng),
        in_specs=[pl.BlockSpec((T, 4 * gw), lambda b, h: (b, cb + h)), pl.BlockSpec((lb.shape[0], gw), lambda b, h: (0, h)),
                  pl.BlockSpec((1, LANE), lambda b, h: (0, 0))],
        out_specs=[pl.BlockSpec((T, gw), lambda b, h: (b, h)), pl.BlockSpec((T, gw), lambda b, h: (b, h)),
                   pl.BlockSpec((HG_GROUP * NC, HG_D, HG_D), lambda b, h: (b * ng + h, 0, 0))],
        out_shape=[S((N, 512), BF16), S((N, 512), F32), S((B * HG_H * NC, HG_D, HG_D), F32)],
        scratch_shapes=[pltpu.VMEM((HG_GROUP, HG_D, HG_D), F32)], compiler_params=_cp(("parallel", "parallel")),
    )(z, lb, gn)


def _hgrn_bwd(z, o_raw, states, dy, lb, gn, B, T, swap_sibling=()):
    N = B * T
    NC = T // CHUNK
    ng = HG_H // HG_GROUP
    nsw = len(swap_sibling)

    def body(z_ref, o_ref, st_ref, dy_ref, lb_ref, gn_ref, dz_ref, dlb_ref, dgn_ref, ds_scr, racc, dgn_acc):
        lbs = _lower_bound(lb_ref[...])
        gn_v = gn_ref[...]
        tri, triu = _tri(CHUNK), _tri(CHUNK, upper=True)
        cmask = _iota((CHUNK, CHUNK), 0) >= _iota((CHUNK, CHUNK), 1)
        for ref in (ds_scr, racc, dgn_acc, dlb_ref):
            ref[...] = jnp.zeros_like(ref)

        def chunk(ci, carry):
            c = NC - 1 - ci
            r = pl.ds(pl.multiple_of(c * CHUNK, CHUNK), CHUNK)
            for hh in range(HG_GROUP):
                zc, oc = 4 * LANE * hh, LANE * hh
                lb_v = lbs[:, oc:oc + LANE]
                ql, fl, il, gl = (z_ref[r, zc + LANE * j:zc + LANE * (j + 1)] for j in range(4))
                sig, f, k, logf = _hg_gates(fl, lb_v)
                G = _prefix_mm(tri, logf)
                qs, dsilu_q = _silu_and_grad(ql)
                gs, dsilu_g = _silu_and_grad(gl)
                o = o_ref[r, oc:oc + LANE]
                dyv = dy_ref[r, oc:oc + LANE]
                rstd = lax.rsqrt(jnp.mean(o * o, axis=-1, keepdims=True) + EPS)
                oh = o * rstd
                dgl = dyv * oh * gn_v * dsilu_g
                dn = dyv * gs
                dgn_acc[...] += _rowsum8(dn * oh)
                u = dn * gn_v
                do = rstd * (u - oh * jnp.mean(u * oh, axis=-1, keepdims=True))
                st = st_ref[hh * NC + c]
                dst = ds_scr[hh]
                eG = jnp.exp(G)
                g_last = G[CHUNK - 1:CHUNK, :]
                eL = jnp.exp(g_last - G)
                dA = jnp.where(cmask, _mm3(_nt, do, il), 0.0)
                A, (dq_in, dk_in) = _hg_intra_A(qs, k, G), _hg_intra_bwd(dA, qs, k, G)
                di = _tn(_b(A), _b(do)) + _nt(_b(k * eL), _b(dst))
                dq = dq_in + _mm3(_nn, do, st) * eG
                dk = dk_in + _mm3(_nn, il, dst) * eL
                ds_scr[hh] = dst * jnp.exp(g_last) + _mm3(_tn, do, qs * eG)
                dd = qs * dq - k * dk
                dlogf = _prefix_mm(triu, dd) + racc[hh]
                racc[hh] += jnp.sum(dd, axis=0, keepdims=True)
                df = dlogf / f - dk
                dlb_ref[8 * hh:8 * (hh + 1), :] += _rowsum8(df * (1.0 - sig))
                dz_ref[r, zc:zc + LANE] = (dq * dsilu_q).astype(BF16)
                dz_ref[r, zc + LANE:zc + 2 * LANE] = (df * (1.0 - lb_v) * sig * (1.0 - sig)).astype(BF16)
                dz_ref[r, zc + 2 * LANE:zc + 3 * LANE] = di.astype(BF16)
                dz_ref[r, zc + 3 * LANE:zc + 4 * LANE] = dgl.astype(BF16)
            return carry

        lax.fori_loop(0, NC, chunk, 0)
        dgn_ref[...] = dgn_acc[...]

    gw = HG_GROUP * LANE
    cb = C_HG // (4 * gw)
    col = pl.BlockSpec((T, gw), lambda b, h: (b, h))
    if nsw:
        body = _hosting(body, 6, 3, 3, nsw, _sibling_swap_phases, (B, ng))
    return pl.pallas_call(
        body, name="hgrn_bwd", grid=(B, ng),
        in_specs=[pl.BlockSpec((T, 4 * gw), lambda b, h: (b, cb + h)), col,
                  pl.BlockSpec((HG_GROUP * NC, HG_D, HG_D), lambda b, h: (b * ng + h, 0, 0)), col,
                  pl.BlockSpec((lb.shape[0], gw), lambda b, h: (0, h)), pl.BlockSpec((1, LANE), lambda b, h: (0, 0))]
        + [ANY] * nsw,
        out_specs=[pl.BlockSpec((T, 4 * gw), lambda b, h: (b, h)),
                   pl.BlockSpec((8 * HG_GROUP, LANE), lambda b, h: (b * ng + h, 0)),
                   pl.BlockSpec((8, LANE), lambda b, h: (b * ng + h, 0))] + [ANY] * nsw,
        out_shape=[S((N, 2048), BF16), S((B * HG_H * 8, LANE), F32), S((B * ng * 8, LANE), F32)]
        + _sibling_swap_shapes(swap_sibling),
        scratch_shapes=[pltpu.VMEM((HG_GROUP, HG_D, HG_D), F32), pltpu.VMEM((HG_GROUP, 1, LANE), F32),
                        pltpu.VMEM((8, LANE), F32)] + (_sibling_swap_sems(nsw) if nsw else []),
        compiler_params=_cp(("arbitrary", "arbitrary") if nsw else ("parallel", "parallel")),
    )(z, o_raw, states, dy, lb, gn, *swap_sibling)


def _pair_mean(x, lo_half):
    a = jnp.sum(jnp.where(lo_half, x, 0.0), axis=-1, keepdims=True)
    b = jnp.sum(jnp.where(lo_half, 0.0, x), axis=-1, keepdims=True)
    return jnp.where(lo_half, a, b) * (1.0 / FOX_D)


def _fox_gate_fwd(z, bias, B, T):
    N = B * T
    tb = LANE

    def body(z_ref, b_ref, fc_ref, fct_ref):
        tri = _tri(tb)

        def step(i, carry):
            r = pl.ds(pl.multiple_of(i * tb, tb), tb)
            cs = _prefix_mm(tri, jax.nn.log_sigmoid(z_ref[r, :] + b_ref[...])) + carry
            fc_ref[r, :] = cs
            fct_ref[0, :, r] = cs.T[0:8, :]
            return cs[tb - 1:tb, :]

        lax.fori_loop(0, T // tb, step, jnp.zeros((1, LANE), F32))

    return pl.pallas_call(
        body, name="fox_gate_fwd", grid=(B,),
        in_specs=[pl.BlockSpec((T, LANE), lambda b: (b, C_FF // LANE)), pl.BlockSpec((1, LANE), lambda b: (0, 0))],
        out_specs=[pl.BlockSpec((T, LANE), lambda b: (b, 0)), pl.BlockSpec((1, 8, T), lambda b: (b, 0, 0))],
        out_shape=[S((N, LANE), F32), S((B, 8, T), F32)], compiler_params=_cp(("parallel",)),
    )(z, bias)


def _fox_gate_bwd(dfc, z, bias, B, T):
    N = B * T
    tb = LANE
    nt = T // tb

    def body(d_ref, z_ref, b_ref, dz_ref, db_ref):
        triu = _tri(tb, upper=True)
        db_ref[...] = jnp.zeros_like(db_ref)

        def step(ii, carry):
            r = pl.ds(pl.multiple_of((nt - 1 - ii) * tb, tb), tb)
            d = d_ref[r, 0:LANE]
            for p in range(1, FOX_P):
                d = d + d_ref[r, LANE * p:LANE * (p + 1)]
            rc = _prefix_mm(triu, d) + carry
            dff = rc * jax.nn.sigmoid(-(z_ref[r, :] + b_ref[...]))
            dz_ref[r, :] = dff.astype(BF16)
            db_ref[...] += _rowsum8(dff)
            return carry + jnp.sum(d, axis=0, keepdims=True)

        lax.fori_loop(0, nt, step, jnp.zeros((1, LANE), F32))

    return pl.pallas_call(
        body, name="fox_gate_bwd", grid=(B,),
        in_specs=[pl.BlockSpec((T, 512), lambda b: (b, 0)), pl.BlockSpec((T, LANE), lambda b: (b, C_FF // LANE)),
                  pl.BlockSpec((1, LANE), lambda b: (0, 0))],
        out_specs=[pl.BlockSpec((T, LANE), lambda b: (b, 0)), pl.BlockSpec((8, LANE), lambda b: (b, 0))],
        out_shape=[S((N, LANE), BF16), S((B * 8, LANE), F32)], compiler_params=_cp(("parallel",)),
    )(dfc, z, bias)


def _fox_prep(z_ref, gq, gk, r, lo_half):
    q, k, v = z_ref[r, 0:LANE], z_ref[r, LANE:2 * LANE], z_ref[r, 2 * LANE:3 * LANE]
    rq = lax.rsqrt(_pair_mean(q * q, lo_half) + EPS)
    rk = lax.rsqrt(_pair_mean(k * k, lo_half) + EPS)
    qh, kh = q * rq, k * rk
    return qh * gq * (FOX_D ** -0.5), kh * gk, v, qh, kh, rq, rk


def _fox_fwd(z, fc, fct, gq, gk, B, T, tq=512, gather=()):
    N = B * T
    NQ = T // tq
    nga = len(gather)

    def body(z_ref, fc_ref, fct_ref, gq_ref, gk_ref, y_ref, lse_ref, qn_s, kn_s, v_s):
        p, qi = pl.program_id(1), pl.program_id(2)
        lo_half = _iota((1, LANE), 1) < FOX_D

        @pl.when(qi == 0)
        def _():
            def prep(i, carry):
                r = pl.ds(pl.multiple_of(i * tq, tq), tq)
                qn, kn, v = _fox_prep(z_ref, gq_ref[...], gk_ref[...], r, lo_half)[:3]
                qn_s[r, :], kn_s[r, :], v_s[r, :] = qn.astype(BF16), kn.astype(BF16), v.astype(BF16)
                return carry
            lax.fori_loop(0, NQ, prep, 0)

        rq = pl.ds(pl.multiple_of(qi * tq, tq), tq)
        qn = qn_s[rq, :]
        fcq = fc_ref[rq, :]
        lane = _iota((tq, LANE), 1)
        causal = _iota((tq, tq), 0) >= _iota((tq, tq), 1)
        qhs = [jnp.where(lo_half, qn, jnp.zeros_like(qn)), jnp.where(lo_half, jnp.zeros_like(qn), qn)]
        fqs = [jnp.sum(jnp.where(lane == 2 * p + hh, fcq, 0.0), axis=-1, keepdims=True) for hh in range(2)]

        def kv(j, carry, diagonal):
            rk = pl.ds(pl.multiple_of(j * tq, tq), tq)
            kj, vj = kn_s[rk, :], v_s[rk, :]
            new = []
            for hh in range(2):
                m, l, acc = carry[hh]
                s = _nt(qhs[hh], kj) + fqs[hh] - fct_ref[0, pl.ds(2 * p + hh, 1), rk]
                if diagonal:
                    s = jnp.where(causal, s, NEG)
                m_new = jnp.maximum(m, jnp.max(s, axis=-1, keepdims=True))
                pe = jnp.exp(s - m_new)
                alpha = jnp.exp(m - m_new)
                new.append((m_new, alpha * l + jnp.sum(pe, axis=-1, keepdims=True),
                            alpha * acc + _nn(pe.astype(BF16), vj)))
            return tuple(new)

        init = tuple((jnp.full((tq, 1), NEG, F32), jnp.zeros((tq, 1), F32), jnp.zeros((tq, LANE), F32)) for _ in range(2))
        carry = lax.fori_loop(0, qi, functools.partial(kv, diagonal=False), init)
        (m0, l0, a0), (m1, l1, a1) = kv(qi, carry, True)
        y_ref[...] = jnp.where(lo_half, a0 / l0, a1 / l1).astype(BF16)
        lse_ref[...] = jnp.where(lo_half, m0 + jnp.log(l0), m1 + jnp.log(l1))

    vec = pl.BlockSpec((1, LANE), lambda b, p, q: (0, 0))
    tile = pl.BlockSpec((tq, LANE), lambda b, p, q: (b * NQ + q, p))
    if nga:
        body = _hosting(body, 5, 2, 3, nga, _gather_phases, (B, FOX_P, NQ))
    return pl.pallas_call(
        body, name="fox_fwd", grid=(B, FOX_P, NQ),
        in_specs=[pl.BlockSpec((T, 384), lambda b, p, q: (b, p)), pl.BlockSpec((T, LANE), lambda b, p, q: (b, 0)),
                  pl.BlockSpec((1, 8, T), lambda b, p, q: (b, 0, 0)), vec, vec] + [ANY] * nga,
        out_specs=[tile, tile] + [ANY] * nga, out_shape=[S((N, 512), BF16), S((N, 512), F32)] + _gather_shapes(gather),
        scratch_shapes=[pltpu.VMEM((T, LANE), BF16)] * 3 + (_gather_sems(nga) if nga else []),
        compiler_params=_cp(("arbitrary",) * 3 if nga else ("parallel", "parallel", "arbitrary")),
    )(z, fc, fct, gq, gk, *gather)


def _fox_bwd(z, dy, y, lse, fc, fct, gq, gk, B, T, tq=512, swap=()):
    N = B * T
    NQ = T // tq
    nsw = len(swap)

    def body(z_ref, dy_ref, y_ref, lse_ref, fc_ref, fct_ref, gq_ref, gk_ref, dz_ref, dfc_ref, dgq_ref, dgk_ref,
             qn_s, kn_s, v_s, do_s, delta_s, dq_s, dfk_s):
        p, kj = pl.program_id(1), pl.program_id(2)
        lo_half = _iota((1, LANE), 1) < FOX_D
        lane = _iota((tq, LANE), 1)
        gq_v, gk_v = gq_ref[...], gk_ref[...]

        @pl.when(kj == 0)
        def _():
            def prep(i, carry):
                r = pl.ds(pl.multiple_of(i * tq, tq), tq)
                qn, kn, v = _fox_prep(z_ref, gq_v, gk_v, r, lo_half)[:3]
                qn_s[r, :], kn_s[r, :], v_s[r, :] = qn.astype(BF16), kn.astype(BF16), v.astype(BF16)
                do = dy_ref[r, :]
                do_s[r, :] = do.astype(BF16)
                delta_s[r, :] = _pair_mean(do * y_ref[r, :].astype(F32), lo_half) * float(FOX_D)
                return carry
            lax.fori_loop(0, NQ, prep, 0)
            dq_s[...] = jnp.zeros_like(dq_s)
            dgq_ref[...] = jnp.zeros_like(dgq_ref)
            dgk_ref[...] = jnp.zeros_like(dgk_ref)

        rk = pl.ds(pl.multiple_of(kj * tq, tq), tq)
        kn, vv = kn_s[rk, :], v_s[rk, :]
        causal = _iota((tq, tq), 0) >= _iota((tq, tq), 1)
        zero, one = jnp.zeros_like(kn), jnp.ones_like(kn)
        hms = [lo_half, jnp.logical_not(lo_half)]
        kmasks = [jnp.where(hm, kn, zero) for hm in hms]
        kaugs = [jnp.where(hm, kn, one) for hm in hms]
        vmasks = [jnp.where(hm, vv, zero) for hm in hms]
        fks = [fct_ref[0, pl.ds(2 * p + hh, 1), rk] for hh in range(2)]

        def qloop(i, carry, diagonal):
            ri = pl.ds(pl.multiple_of(i * tq, tq), tq)
            qn = qn_s[ri, :]
            do = do_s[ri, :]
            fcq = fc_ref[ri, :]
            new = []
            for hh in range(2):
                dk_acc, dv_acc = carry[hh]
                c0 = FOX_D * hh
                fq = jnp.sum(jnp.where(lane == 2 * p + hh, fcq, 0.0), axis=-1, keepdims=True)
                pr = jnp.exp(_nt(qn, kmasks[hh]) + fq - fks[hh] - lse_ref[ri, c0:c0 + 1])
                if diagonal:
                    pr = jnp.where(causal, pr, 0.0)
                ds = (pr * (_nt(do, vmasks[hh]) - delta_s[ri, c0:c0 + 1])).astype(BF16)
                dq_s[hh, ri, :] += _nn(ds, kaugs[hh])
                new.append((dk_acc + _tn(jnp.where(hms[hh], qn, one), ds), dv_acc + _tn(do, pr.astype(BF16))))
            return tuple(new)

        init = tuple((jnp.zeros((LANE, tq), F32), jnp.zeros((LANE, tq), F32)) for _ in range(2))
        carry = qloop(kj, init, True)
        (dk0, dv0), (dk1, dv1) = lax.fori_loop(kj + 1, NQ, functools.partial(qloop, diagonal=False), carry)
        dks, dvs = [dk0.T, dk1.T], [dv0.T, dv1.T]

        dkn = jnp.where(lo_half, dks[0], dks[1])
        _, _, _, _, kh, _, rkk = _fox_prep(z_ref, gq_v, gk_v, rk, lo_half)
        u = dkn * gk_v
        dz_ref[rk, LANE:2 * LANE] = (rkk * (u - kh * _pair_mean(u * kh, lo_half))).astype(BF16)
        dz_ref[rk, 2 * LANE:3 * LANE] = jnp.where(lo_half, dvs[0], dvs[1]).astype(BF16)
        dgk_ref[...] += _rowsum8(dkn * kh)
        dfk_s[rk, :] = jnp.where(lane == 2 * p, -dks[0][:, FOX_D:FOX_D + 1],
                                 jnp.where(lane == 2 * p + 1, -dks[1][:, 0:1], 0.0))

        @pl.when(kj == NQ - 1)
        def _():
            def fin(i, carry):
                r = pl.ds(pl.multiple_of(i * tq, tq), tq)
                d0, d1 = dq_s[0, r, :], dq_s[1, r, :]
                dqn = jnp.where(lo_half, d0, d1)
                _, _, _, qh, _, rqq, _ = _fox_prep(z_ref, gq_v, gk_v, r, lo_half)
                u = dqn * gq_v * (FOX_D ** -0.5)
                dz_ref[r, 0:LANE] = (rqq * (u - qh * _pair_mean(u * qh, lo_half))).astype(BF16)
                dgq_ref[...] += _rowsum8(dqn * qh) * (FOX_D ** -0.5)
                dfc_ref[r, :] = dfk_s[r, :] + jnp.where(lane == 2 * p, d0[:, FOX_D:FOX_D + 1],
                                                        jnp.where(lane == 2 * p + 1, d1[:, 0:1], 0.0))
                return carry
            lax.fori_loop(0, NQ, fin, 0)

    vec = pl.BlockSpec((1, LANE), lambda b, p, k: (0, 0))
    col = pl.BlockSpec((T, LANE), lambda b, p, k: (b, p))
    part = pl.BlockSpec((8, LANE), lambda b, p, k: (b * FOX_P + p, 0))
    if nsw:
        body = _hosting(body, 8, 4, 7, nsw, _chip_swap_phases, (B, FOX_P, NQ))
    return pl.pallas_call(
        body, name="fox_bwd", grid=(B, FOX_P, NQ),
        in_specs=[pl.BlockSpec((T, 384), lambda b, p, k: (b, p)), col, col, col,
                  pl.BlockSpec((T, LANE), lambda b, p, k: (b, 0)), pl.BlockSpec((1, 8, T), lambda b, p, k: (b, 0, 0)),
                  vec, vec] + [ANY] * nsw,
        out_specs=[pl.BlockSpec((T, 384), lambda b, p, k: (b, p)), col, part, part] + [ANY] * nsw,
        out_shape=[S((N, 1536), BF16), S((N, 512), F32), S((B * FOX_P * 8, LANE), F32), S((B * FOX_P * 8, LANE), F32)]
        + [S(p.shape, p.dtype) for p in swap],
        scratch_shapes=[pltpu.VMEM((T, LANE), BF16)] * 4 + [pltpu.VMEM((T, LANE), F32), pltpu.VMEM((2, T, LANE), F32),
                                                            pltpu.VMEM((T, LANE), F32)]
        + (_chip_swap_sems(nsw) if nsw else []),
        compiler_params=_cp(("arbitrary",) * 3 if nsw else ("parallel", "parallel", "arbitrary")),
    )(z, dy, y, lse, fc, fct, gq, gk, *swap)


def _mem_scores(z_ref, kv_ref, gq, gk, h):
    c = slice(MEM_D * h, MEM_D * (h + 1))
    q, k = z_ref[:, c], kv_ref[:, c]
    rq = lax.rsqrt(jnp.mean(q * q, axis=-1, keepdims=True) + EPS)
    rk = lax.rsqrt(jnp.mean(k * k, axis=-1, keepdims=True) + EPS)
    qh, kh = q * rq, k * rk
    qn = (qh * gq * (MEM_D ** -0.5)).astype(BF16)
    kn = (kh * gk).astype(BF16)
    s = _nt(qn, kn)
    pe = jnp.exp(s - jnp.max(s, axis=-1, keepdims=True))
    pn = pe / jnp.sum(pe, axis=-1, keepdims=True)
    return pn, qn, kn, qh, kh, rq, rk


def _mem_fwd(z, memkv, gq, gk, B, T, M, tq=512):
    N = B * T
    NQ = T // tq
    W = MEM_H * MEM_D

    def body(z_ref, kv_ref, gq_ref, gk_ref, y_ref):
        for h in range(MEM_H):
            pn = _mem_scores(z_ref, kv_ref, gq_ref[...], gk_ref[...], h)[0]
            v = kv_ref[:, W + MEM_D * h:W + MEM_D * (h + 1)].astype(BF16)
            y_ref[:, MEM_D * h:MEM_D * (h + 1)] = _nn(pn.astype(BF16), v).astype(BF16)

    vec = pl.BlockSpec((1, LANE), lambda b, q: (0, 0))
    return pl.pallas_call(
        body, name="mem_fwd", grid=(B, NQ),
        in_specs=[pl.BlockSpec((tq, W), lambda b, q: (b * NQ + q, C_MQ // W)),
                  pl.BlockSpec((M, 2 * W), lambda b, q: (b, 0)), vec, vec],
        out_specs=pl.BlockSpec((tq, W), lambda b, q: (b * NQ + q, 0)), out_shape=S((N, W), BF16),
        compiler_params=_cp(("parallel", "parallel")),
    )(z, memkv, gq, gk)


def _mem_bwd(z, memkv, dy, gq, gk, B, T, M, tq=512):
    N = B * T
    NQ = T // tq
    W = MEM_H * MEM_D

    def body(z_ref, kv_ref, dy_ref, gq_ref, gk_ref, dz_ref, dkv_ref, dgq_ref, dgk_ref, acc):
        qi = pl.program_id(1)
        gq_v, gk_v = gq_ref[...], gk_ref[...]

        @pl.when(qi == 0)
        def _():
            acc[...] = jnp.zeros_like(acc)
            dgq_ref[...] = jnp.zeros_like(dgq_ref)
            dgk_ref[...] = jnp.zeros_like(dgk_ref)

        for h in range(MEM_H):
            c = slice(MEM_D * h, MEM_D * (h + 1))
            cv = slice(W + MEM_D * h, W + MEM_D * (h + 1))
            pn, qn, kn, qh, _, rq, _ = _mem_scores(z_ref, kv_ref, gq_v, gk_v, h)
            do = dy_ref[:, c].astype(BF16)
            dp = _nt(do, kv_ref[:, cv].astype(BF16))
            ds = (pn * (dp - jnp.sum(dp * pn, axis=-1, keepdims=True))).astype(BF16)
            dqn = _nn(ds, kn)
            acc[:, c] += _tn(ds, qn)
            acc[:, cv] += _tn(pn.astype(BF16), do)
            u = dqn * gq_v * (MEM_D ** -0.5)
            dz_ref[:, c] = (rq * (u - qh * jnp.mean(u * qh, axis=-1, keepdims=True))).astype(BF16)
            dgq_ref[...] += _rowsum8(dqn * qh) * (MEM_D ** -0.5)

        @pl.when(qi == NQ - 1)
        def _():
            for h in range(MEM_H):
                c = slice(MEM_D * h, MEM_D * (h + 1))
                cv = slice(W + MEM_D * h, W + MEM_D * (h + 1))
                k = kv_ref[:, c]
                rk = lax.rsqrt(jnp.mean(k * k, axis=-1, keepdims=True) + EPS)
                kh = k * rk
                dkn = acc[:, c]
                u = dkn * gk_v
                dkv_ref[:, c] = (rk * (u - kh * jnp.mean(u * kh, axis=-1, keepdims=True))).astype(BF16)
                dkv_ref[:, cv] = acc[:, cv].astype(BF16)
                dgk_ref[...] += _rowsum8(dkn * kh)

    vec = pl.BlockSpec((1, LANE), lambda b, q: (0, 0))
    part = pl.BlockSpec((8, LANE), lambda b, q: (b, 0))
    return pl.pallas_call(
        body, name="mem_bwd", grid=(B, NQ),
        in_specs=[pl.BlockSpec((tq, W), lambda b, q: (b * NQ + q, C_MQ // W)),
                  pl.BlockSpec((M, 2 * W), lambda b, q: (b, 0)), pl.BlockSpec((tq, W), lambda b, q: (b * NQ + q, 0)),
                  vec, vec],
        out_specs=[pl.BlockSpec((tq, W), lambda b, q: (b * NQ + q, 0)), pl.BlockSpec((M, 2 * W), lambda b, q: (b, 0)),
                   part, part],
        out_shape=[S((N, W), BF16), S((B * M, 2 * W), BF16), S((B * 8, LANE), F32), S((B * 8, LANE), F32)],
        scratch_shapes=[pltpu.VMEM((M, 2 * W), F32)], compiler_params=_cp(("parallel", "arbitrary")),
    )(z, memkv, dy, gq, gk)


def _merge_fwd(ya, yb, yc, z, x, wa, wb, wc, wo, tm=256):
    n, d = x.shape
    wdt = ya.shape[1]
    gb = C_GATE // d

    def body(ya_ref, yb_ref, yc_ref, g0_ref, g1_ref, g2_ref, x_ref, wa_ref, wb_ref, wc_ref, wo_ref,
             x1_ref, mg_ref, ua_ref, ub_ref, uc_ref):
        merged = jnp.zeros((tm, d), F32)
        for y_ref, g_ref, w_ref, u_ref in ((ya_ref, g0_ref, wa_ref, ua_ref), (yb_ref, g1_ref, wb_ref, ub_ref),
                                           (yc_ref, g2_ref, wc_ref, uc_ref)):
            u = _nn(y_ref[...], w_ref[...])
            u_ref[...] = u.astype(BF16)
            merged = merged + jax.nn.sigmoid(g_ref[...]) * u
        mb = merged.astype(BF16)
        mg_ref[...] = mb
        x1_ref[...] = x_ref[...] + _nn(mb, wo_ref[...])

    yt = pl.BlockSpec((tm, wdt), lambda i: (i, 0))
    xt = pl.BlockSpec((tm, d), lambda i: (i, 0))
    wbr = pl.BlockSpec((wdt, d), lambda i: (0, 0))
    gates = [pl.BlockSpec((tm, d), functools.partial(lambda i, k: (i, gb + k), k=k)) for k in range(3)]
    return pl.pallas_call(
        body, name="merge_fwd", grid=(n // tm,),
        in_specs=[yt, yt, yt] + gates + [xt, wbr, wbr, wbr, pl.BlockSpec((d, d), lambda i: (0, 0))],
        out_specs=[xt] * 5, out_shape=[S((n, d), F32)] + [S((n, d), BF16)] * 4, compiler_params=_cp(("parallel",)),
    )(ya, yb, yc, z, z, z, x, wa, wb, wc, wo)


def _merge_bwd(dx1, z, ua, ub, uc, wa, wb, wc, wo, tm=256):
    n, d = dx1.shape
    wdt = wa.shape[0]
    gb = C_GATE // d

    def body(dx_ref, g0_ref, g1_ref, g2_ref, ua_ref, ub_ref, uc_ref, wa_ref, wb_ref, wc_ref, wo_ref,
             dg_ref, dya_ref, dyb_ref, dyc_ref, dua_ref, dub_ref, duc_ref):
        dm = _nt(dx_ref[...].astype(BF16), wo_ref[...])
        for k, (g_ref, u_ref, w_ref, dy_ref, du_ref) in enumerate((
                (g0_ref, ua_ref, wa_ref, dya_ref, dua_ref), (g1_ref, ub_ref, wb_ref, dyb_ref, dub_ref),
                (g2_ref, uc_ref, wc_ref, dyc_ref, duc_ref))):
            g = jax.nn.sigmoid(g_ref[...])
            du = (dm * g).astype(BF16)
            du_ref[...] = du
            dg_ref[:, d * k:d * (k + 1)] = (dm * u_ref[...].astype(F32) * g * (1.0 - g)).astype(BF16)
            dy_ref[...] = _nt(du, w_ref[...])

    yt = pl.BlockSpec((tm, wdt), lambda i: (i, 0))
    xt = pl.BlockSpec((tm, d), lambda i: (i, 0))
    wbr = pl.BlockSpec((wdt, d), lambda i: (0, 0))
    gates = [pl.BlockSpec((tm, d), functools.partial(lambda i, k: (i, gb + k), k=k)) for k in range(3)]
    return pl.pallas_call(
        body, name="merge_bwd", grid=(n // tm,),
        in_specs=[xt] + gates + [xt, xt, xt, wbr, wbr, wbr, pl.BlockSpec((d, d), lambda i: (0, 0))],
        out_specs=[pl.BlockSpec((tm, 3 * d), lambda i: (i, 0)), yt, yt, yt, xt, xt, xt],
        out_shape=[S((n, 3 * d), BF16)] + [S((n, wdt), F32)] * 3 + [S((n, d), BF16)] * 3,
        compiler_params=_cp(("parallel",)),
    )(dx1, z, z, z, ua, ub, uc, wa, wb, wc, wo)


FFN_TN = 1408
TN_TM = 2048
INV_SQRT2 = 0.7071067811865476
INV_SQRT_2PI = 0.3989422804014327


def _conv_shifted(a, prev, first, tm):
    row = _iota(a.shape, 0)
    p7 = jnp.where(first, 0.0, prev[7:8, :])
    p6 = jnp.where(first, 0.0, prev[6:7, :])
    a1 = jnp.where(row == 0, p7, pltpu.roll(a, 1, 0))
    a2 = jnp.where(row == 0, p6, jnp.where(row == 1, p7, pltpu.roll(a, 2, 0)))
    return a1, a2


def _ffn_act_fwd(up, cw, cb, B, T, tm=256):
    N = B * T
    dff = cw.shape[1]
    NT, NJ, tn = T // tm, dff // FFN_TN, FFN_TN

    def body(a_ref, v_ref, cw_ref, cb_ref, y_ref, c_ref, carry):
        t = pl.program_id(2)
        a = a_ref[...].astype(F32)
        a1, a2 = _conv_shifted(a, carry[...], t == 0, tm)
        w = cw_ref[...]
        ac = w[0:1, :] * a2 + w[1:2, :] * a1 + w[2:3, :] * a + cb_ref[...]
        cdf = 0.5 * (1.0 + lax.erf(ac * INV_SQRT2))
        y_ref[...] = (ac * cdf * v_ref[...].astype(F32)).astype(BF16)
        c_ref[...] = cdf.astype(BF16)
        carry[...] = a[tm - 8:tm, :]

    return pl.pallas_call(
        body, name="ffn_act_fwd", grid=(B, NJ, NT),
        in_specs=[pl.BlockSpec((tm, tn), lambda b, j, t: (b * NT + t, j)),
                  pl.BlockSpec((tm, tn), lambda b, j, t: (b * NT + t, NJ + j)),
                  pl.BlockSpec((3, tn), lambda b, j, t: (0, j)), pl.BlockSpec((1, tn), lambda b, j, t: (0, j))],
        out_specs=[pl.BlockSpec((tm, tn), lambda b, j, t: (b * NT + t, j))] * 2, out_shape=[S((N, dff), BF16)] * 2,
        scratch_shapes=[pltpu.VMEM((8, tn), F32)], compiler_params=_cp(("parallel", "parallel", "arbitrary")),
    )(up, up, cw, cb)


def _ffn_down_loss(y, wd, x1, tgt, tm=256):
    n, d = x1.shape
    kf = y.shape[1]

    def body(y_ref, w_ref, x_ref, t_ref, dx_ref, ls_ref):
        err = x_ref[...] + _nn(y_ref[...], w_ref[...]) - t_ref[...]
        dx_ref[...] = err * (1.0 / d)

        @pl.when(pl.program_id(0) == 0)
        def _():
            ls_ref[...] = jnp.zeros_like(ls_ref)

        ls_ref[...] += _rowsum8(err * err) * (0.5 / d)

    xt = pl.BlockSpec((tm, d), lambda i: (i, 0))
    return pl.pallas_call(
        body, name="ffn_down_loss", grid=(n // tm,),
        in_specs=[pl.BlockSpec((tm, kf), lambda i: (i, 0)), pl.BlockSpec((kf, d), lambda i: (0, 0)), xt, xt],
        out_specs=[xt, pl.BlockSpec((8, d), lambda i: (0, 0))], out_shape=[S((n, d), F32), S((8, d), F32)],
        compiler_params=_cp(("arbitrary",)),
    )(y, wd, x1, tgt)


def _ffn_act_bwd1(dx2, wd, up, cdf, cw, cb, B, T, tm=256):
    N = B * T
    d = dx2.shape[1]
    dff = cw.shape[1]
    NT, NJ, tn = T // tm, dff // FFN_TN, FFN_TN

    def body(dx_ref, w_ref, a_ref, v_ref, c_ref, cw_ref, cb_ref, dac_ref, dv_ref, dcw_ref, dcb_ref, carry):
        b, t = pl.program_id(1), pl.program_id(2)
        a = a_ref[...].astype(F32)
        a1, a2 = _conv_shifted(a, carry[...], t == 0, tm)
        carry[...] = a[tm - 8:tm, :]
        w = cw_ref[...]
        ac = w[0:1, :] * a2 + w[1:2, :] * a1 + w[2:3, :] * a + cb_ref[...]
        dy = _nt(dx_ref[...].astype(BF16), w_ref[...])
        cdf = c_ref[...].astype(F32)
        dv_ref[...] = (dy * ac * cdf).astype(BF16)
        dac = dy * v_ref[...].astype(F32) * (cdf + ac * jnp.exp(-0.5 * ac * ac) * INV_SQRT_2PI)
        dac_ref[...] = dac

        @pl.when((b == 0) & (t == 0))
        def _():
            dcw_ref[...] = jnp.zeros_like(dcw_ref)
            dcb_ref[...] = jnp.zeros_like(dcb_ref)

        dcw_ref[0:8, :] += _rowsum8(dac * a2)
        dcw_ref[8:16, :] += _rowsum8(dac * a1)
        dcw_ref[16:24, :] += _rowsum8(dac * a)
        dcb_ref[...] += _rowsum8(dac)

    return pl.pallas_call(
        body, name="ffn_act_bwd1", grid=(NJ, B, NT),
        in_specs=[pl.BlockSpec((tm, d), lambda j, b, t: (b * NT + t, 0)), pl.BlockSpec((tn, d), lambda j, b, t: (j, 0)),
                  pl.BlockSpec((tm, tn), lambda j, b, t: (b * NT + t, j)),
                  pl.BlockSpec((tm, tn), lambda j, b, t: (b * NT + t, NJ + j)),
                  pl.BlockSpec((tm, tn), lambda j, b, t: (b * NT + t, j)),
                  pl.BlockSpec((3, tn), lambda j, b, t: (0, j)), pl.BlockSpec((1, tn), lambda j, b, t: (0, j))],
        out_specs=[pl.BlockSpec((tm, tn), lambda j, b, t: (b * NT + t, j)),
                   pl.BlockSpec((tm, tn), lambda j, b, t: (b * NT + t, j)),
                   pl.BlockSpec((24, tn), lambda j, b, t: (0, j)), pl.BlockSpec((8, tn), lambda j, b, t: (0, j))],
        out_shape=[S((N, dff), F32), S((N, dff), BF16), S((24, dff), F32), S((8, dff), F32)],
        scratch_shapes=[pltpu.VMEM((8, tn), F32)], compiler_params=_cp(("parallel", "arbitrary", "arbitrary")),
    )(dx2, wd, up, up, cdf, cw, cb)


def _ffn_act_bwd2(dac, cw, B, T, tm=256):
    N = B * T
    dff = cw.shape[1]
    NT, NJ, tn = T // tm, dff // FFN_TN, FFN_TN
    last8 = N // 8 - 1

    def body(d_ref, nx_ref, cw_ref, da_ref):
        t = pl.program_id(2)
        dd = d_ref[...]
        row = _iota(dd.shape, 0)
        last = t == NT - 1
        n0 = jnp.where(last, 0.0, nx_ref[0:1, :])
        n1 = jnp.where(last, 0.0, nx_ref[1:2, :])
        d1 = jnp.where(row == tm - 1, n0, pltpu.roll(dd, tm - 1, 0))
        d2 = jnp.where(row == tm - 1, n1, jnp.where(row == tm - 2, n0, pltpu.roll(dd, tm - 2, 0)))
        w = cw_ref[...]
        da_ref[...] = (w[2:3, :] * dd + w[1:2, :] * d1 + w[0:1, :] * d2).astype(BF16)

    return pl.pallas_call(
        body, name="ffn_act_bwd2", grid=(B, NJ, NT),
        in_specs=[pl.BlockSpec((tm, tn), lambda b, j, t: (b * NT + t, j)),
                  pl.BlockSpec((8, tn), lambda b, j, t: (jnp.minimum((b * NT + t + 1) * (tm // 8), last8), j)),
                  pl.BlockSpec((3, tn), lambda b, j, t: (0, j))],
        out_specs=pl.BlockSpec((tm, tn), lambda b, j, t: (b * NT + t, j)), out_shape=S((N, dff), BF16),
        compiler_params=_cp(("parallel", "parallel", "parallel")),
    )(dac, dac, cw)


def _fold_rows(p, name):
    r, c = p.shape[0] // 8, p.shape[1]

    def body(p_ref, o_ref):
        for j in range(r):
            o_ref[j:j + 1, :] = jnp.sum(p_ref[8 * j:8 * (j + 1), :], axis=0, keepdims=True)

    return pl.pallas_call(body, name=name, out_shape=S((r, c), F32), compiler_params=_cp())(p)


def _small_reduce(lbl, dg_mix, dg_mem, dlb_p, dgn_p, dfb_p, dgq_p, dgk_p, dmq_p, dmk_p, dg_ffn, dcb_p, loss_p):
    d, dff = dg_mix.shape[1], dcb_p.shape[1]
    nbh = dlb_p.shape[0] // (8 * HG_H)

    def colsum(ref):
        return jnp.sum(ref[...], axis=0, keepdims=True)

    def body(lbl_ref, mix_ref, mem_ref, dlb_ref, dgn_ref, dfb_ref, dgq_ref, dgk_ref, dmq_ref, dmk_ref, ffn_ref, dcb_ref,
             ls_ref, o_mix, o_mem, o_lb, o_hgn, o_fb, o_fq, o_fk, o_mq, o_mk, o_ffn, o_cb, o_loss):
        o_mix[...], o_mem[...], o_ffn[...], o_cb[...] = colsum(mix_ref), colsum(mem_ref), colsum(ffn_ref), colsum(dcb_ref)
        o_hgn[...], o_fb[...], o_mq[...], o_mk[...] = colsum(dgn_ref), colsum(dfb_ref), colsum(dmq_ref), colsum(dmk_ref)
        for src, dst in ((dgq_ref, o_fq), (dgk_ref, o_fk)):
            v = colsum(src)
            dst[...] = v + pltpu.roll(v, FOX_D, 1)
        o_loss[...] = jnp.zeros((1, LANE), F32) + jnp.sum(colsum(ls_ref), axis=-1, keepdims=True)
        logits = lbl_ref[...]
        e = jnp.exp(logits - jnp.max(logits, axis=0, keepdims=True))
        pr = e / jnp.sum(e, axis=0, keepdims=True)
        rows = _iota((8, LANE), 0)
        for h in range(HG_H):
            acc = jnp.zeros((8, LANE), F32)
            for b in range(nbh):
                acc = acc + dlb_ref[8 * (b * HG_H + h):8 * (b * HG_H + h + 1), :]
            dlb = jnp.sum(acc, axis=0, keepdims=True)
            c = slice(LANE * h, LANE * (h + 1))
            p0 = pr[0:1, c]
            first = _iota((logits.shape[0], LANE), 0) == 0
            o_lb[:, c] = pr[:, c] * (jnp.where(first, 1.0, 0.0) - p0) * dlb

    outs = [S((1, d), F32), S((1, d), F32), S(lbl.shape, F32)] + [S((1, LANE), F32)] * 6 + \
           [S((1, d), F32), S((1, dff), F32), S((1, LANE), F32)]
    return pl.pallas_call(body, name="small_reduce", out_shape=outs, compiler_params=_cp())(
        lbl, dg_mix, dg_mem, dlb_p, dgn_p, dfb_p, dgq_p, dgk_p, dmq_p, dmk_p, dg_ffn, dcb_p, loss_p)


def _in_col_pieces():
    hw, fw = HG_H * HG_D, FOX_H * FOX_D
    fox0, ff0 = 4 * hw, 4 * hw + 3 * fw
    mq0 = ff0 + FOX_H
    gate0 = mq0 + MEM_H * MEM_D
    pieces = []
    for p in range(FOX_P):
        pieces += [(fox0 + j * fw + LANE * p, LANE) for j in range(3)]
    pieces.append((mq0, MEM_H * MEM_D))
    for h in range(HG_H):
        pieces += [(j * hw + HG_D * h, HG_D) for j in range(4)]
    pieces.append((gate0, C_FF - C_GATE))
    pieces.append((ff0, FOX_H))
    return pieces


def _perm_from_blocks(blocks):
    n_blk, _, c = blocks.shape
    parts = []
    for s, n in _in_col_pieces():
        lo = s
        while lo < s + n:
            d = lo // c
            hi = min(s + n, (d + 1) * c)
            parts.append(blocks[d][:, lo - d * c:hi - d * c])
            lo = hi
    parts.append(jnp.zeros((blocks.shape[1], C_END - C_FF - FOX_H), blocks.dtype))
    return jnp.concatenate(parts, axis=1)


def _unperm_blocks(segs, n_blk):
    starts = [0]
    for a in segs:
        starts.append(starts[-1] + a.shape[1])
    new_start, placed = 0, []
    for s, n in _in_col_pieces():
        placed.append((s, new_start, n))
        new_start += n
    placed.sort()
    c = sum(n for _, _, n in placed) // n_blk
    blocks = []
    for d in range(n_blk):
        parts = []
        for s, ns, n in placed:
            lo, hi = max(s, d * c), min(s + n, (d + 1) * c)
            if lo < hi:
                i = max(j for j in range(len(segs)) if starts[j] <= ns)
                parts.append(segs[i][:, ns + lo - s - starts[i]:ns + hi - s - starts[i]])
        blocks.append(jnp.concatenate(parts, axis=1))
    return jnp.stack(blocks)


def _local_step(x2, mem2, tgt, sm, W, B, T, M, ex=None):
    fbias = jnp.pad(sm["fox_f_bias"], ((0, 0), (0, LANE - FOX_H)))
    gq2 = jnp.concatenate([sm["fox_q_norm_g"]] * 2, axis=1)
    gk2 = jnp.concatenate([sm["fox_k_norm_g"]] * 2, axis=1)
    lbl = sm["hgrn_lb_logits"]
    h = _rmsnorm_cast(x2, sm["norm_mix_g"], "norm_mix")
    z = _mm_nn(h, W["w_in"], F32, "proj_in", 512, 2432)
    memn = _rmsnorm_cast(mem2, sm["norm_mem_g"], "norm_mem", tm=256)
    memkv = _mm_nn(memn, W["mem_kv_w"], F32, "proj_memkv", 256, 512)
    ya, o_raw, states = _hgrn_fwd(z, lbl, sm["hgrn_norm_g"], B, T)
    fc, fct = _fox_gate_fwd(z, fbias, B, T)
    yb, lse, *late = _fox_fwd(z, fc, fct, gq2, gk2, B, T, gather=ex.late_blocks() if ex else ())
    if ex:
        W = {**W, **ex.unpack_late(late)}
    yc = _mem_fwd(z, memkv, sm["mem_q_norm_g"], sm["mem_k_norm_g"], B, T, M)
    x1, merged, ua, ub, uc = _merge_fwd(ya, yb, yc, z, x2, W["w_br_hgrn"], W["w_br_fox"], W["w_br_mem"], W["w_out"])
    h2 = _rmsnorm_cast(x1, sm["norm_ffn_g"], "norm_ffn")
    up = _mm_nn(h2, W["ffn_w_up"], BF16, "ffn_up", 512, FFN_TN)
    yf, cdf = _ffn_act_fwd(up, W["ffn_conv_w"], sm["ffn_conv_b"], B, T)
    dx2, loss_p = _ffn_down_loss(yf, W["ffn_w_down"], x1, tgt)
    dff = W["ffn_conv_w"].shape[1]
    dac, dv, dcw_p, dcb_p = _ffn_act_bwd1(dx2, W["ffn_w_down"], up, cdf, W["ffn_conv_w"], sm["ffn_conv_b"], B, T)
    da = _ffn_act_bwd2(dac, W["ffn_conv_w"], B, T)
    g = {"ffn_conv_w": _fold_rows(dcw_p, "g_conv_w")}
    g["ffn_w_down"] = _mm_tn(yf, dx2, "g_w_down", TN_TM, 512)
    dh2 = _mm_nt_sum([(da, 0, dff, 0), (dv, 0, dff, dff)], W["ffn_w_up"], "dh2", 256)
    g["ffn_w_up"] = [_mm_tn(h2, da, "g_w_up_a", TN_TM, FFN_TN), _mm_tn(h2, dv, "g_w_up_v", TN_TM, FFN_TN)]
    dx1, dg_ffn = _rmsnorm_bwd(dh2, x1, sm["norm_ffn_g"], dx2, "norm_ffn_bwd")
    g["w_out"] = _mm_tn(merged, dx1, "g_w_out", TN_TM, 512)
    dgate, dya, dyb, dyc, dua, dub, duc = _merge_bwd(dx1, z, ua, ub, uc, W["w_br_hgrn"], W["w_br_fox"], W["w_br_mem"],
                                                    W["w_out"])
    g["w_br_hgrn"] = _mm_tn(ya, dua, "g_w_br_hgrn", TN_TM, 512)
    g["w_br_fox"] = _mm_tn(yb, dub, "g_w_br_fox", TN_TM, 512)
    g["w_br_mem"] = _mm_tn(yc, duc, "g_w_br_mem", TN_TM, 512)
    early_pk = ex.early_grads(g) if ex else ()
    dz_hg, dlb_p, dgn_p, *early_sib = _hgrn_bwd(z, o_raw, states, dya, lbl, sm["hgrn_norm_g"], B, T,
                                                swap_sibling=early_pk)
    dz_fox, dfc, dgq_p, dgk_p, *early_chips = _fox_bwd(z, dyb, yb, lse, fc, fct, gq2, gk2, B, T,
                                                       swap=ex.pair_sums(early_pk, early_sib, "early") if ex else ())
    dz_ff, dfb_p = _fox_gate_bwd(dfc, z, fbias, B, T)
    dz_mq, dkv, dmq_p, dmk_p = _mem_bwd(z, memkv, dyc, sm["mem_q_norm_g"], sm["mem_k_norm_g"], B, T, M)
    g["mem_kv_w"] = _mm_tn(memn, dkv, "g_mem_kv_w", 256, 512)
    dmemn = _mm_nt(dkv, W["mem_kv_w"], "d_memn", 256, 512)
    _, dg_mem = _rmsnorm_bwd(dmemn, mem2, sm["norm_mem_g"], None, "norm_mem_bwd", tm=256)
    d = x2.shape[1]
    parts = [(dz_fox, 0, C_MQ - C_FOX, C_FOX), (dz_mq, 0, C_HG - C_MQ, C_MQ), (dz_hg, 0, C_GATE - C_HG, C_HG)]
    parts += [(dgate, d * k, d, C_GATE + d * k) for k in range(3)] + [(dz_ff, 0, C_END - C_FF, C_FF)]
    g["w_in"] = [_mm_tn(h, dzs, "g_w_in_%d" % i, TN_TM, min(512, dzs.shape[1]))
                 for i, dzs in enumerate((dz_fox, dz_mq, dz_hg, dgate, dz_ff))]
    sums = None
    if ex:
        last_pk = ex.last_grads(g)
        last_sib = _swap_with_sibling(last_pk, "rs_sibling_last")
        dh, last_chips = _mm_nt_sum(parts, W["w_in"], "dh", 256, swap=ex.pair_sums(last_pk, last_sib, "last"))
        sums = (ex.final_sums(early_pk, early_sib, early_chips, "early"),
                ex.final_sums(last_pk, last_sib, last_chips, "last"))
    else:
        dh = _mm_nt_sum(parts, W["w_in"], "dh", 256)
    grad_x, dg_mix = _rmsnorm_bwd(dh, x2, sm["norm_mix_g"], dx1, "norm_mix_bwd")
    small = _small_reduce(lbl, dg_mix, dg_mem, dlb_p, dgn_p, dfb_p, dgq_p, dgk_p, dmq_p, dmk_p, dg_ffn, dcb_p, loss_p)
    names = ("norm_mix_g", "norm_mem_g", "hgrn_lb_logits", "hgrn_norm_g", "fox_f_bias", "fox_q_norm_g", "fox_k_norm_g",
             "mem_q_norm_g", "mem_k_norm_g", "norm_ffn_g", "ffn_conv_b", "loss")
    g.update(dict(zip(names, small)))
    return grad_x, g, sums


ANY = pl.BlockSpec(memory_space=pl.ANY)


def _position():
    return lax.axis_index("x"), lax.axis_index("y"), lax.axis_index("c")


def _all_gather(blocks, name):
    nb = len(blocks)

    def body(*refs):
        start, forward, finish = _gather_phases(refs[:nb], refs[nb:2 * nb], *refs[2 * nb:])
        start()
        forward()
        finish()

    return pl.pallas_call(
        body, name=name, out_shape=_gather_shapes(blocks), in_specs=[ANY] * nb, out_specs=[ANY] * nb,
        scratch_shapes=_gather_sems(nb),
    )(*blocks)


def _hosting(body, n_in, n_out, n_scratch, n_x, make_phases, grid):
    n_steps = math.prod(grid)

    def hosted(*refs):
        a = n_in + n_x
        b = a + n_out + n_x
        ins, xs = refs[:n_in], refs[n_in:a]
        outs, x_outs = refs[a:a + n_out], refs[a + n_out:b]
        scratch, sems = refs[b:b + n_scratch], refs[b + n_scratch:]
        step = 0
        for ax, n in enumerate(grid):
            step = step * n + pl.program_id(ax)
        phases = make_phases(xs, x_outs, *sems)
        pl.when(step == 0)(phases[0])
        for ph in phases[1:-1]:
            pl.when(step == n_steps // 2)(ph)
        body(*ins, *outs, *scratch)
        pl.when(step == n_steps - 1)(phases[-1])

    return hosted


def _gather_shapes(blocks):
    return [S((N_DEV,) + b.shape, b.dtype) for b in blocks]


def _gather_sems(nb):
    return [pltpu.SemaphoreType.DMA((7 * nb,)), pltpu.SemaphoreType.DMA((7 * nb,)), pltpu.SemaphoreType.DMA((nb,))]


def _gather_phases(x_refs, out_refs, send_sems, recv_sems, local_sems):
    nb = len(x_refs)
    x, y, c = _position()
    me, sibling = (x, y, c), (x, y, 1 - c)
    chips = [(1 - x, y), (x, 1 - y), (1 - x, 1 - y)]

    def copy(i, k, blk, to, own=False):
        px, py, pc = blk
        slot = out_refs[i].at[4 * px + 2 * py + pc]
        return pltpu.make_async_remote_copy(
            src_ref=x_refs[i] if own else slot, dst_ref=slot, send_sem=send_sems.at[7 * i + k],
            recv_sem=recv_sems.at[7 * i + k], device_id=to, device_id_type=MESH)

    def mine(i):
        return pltpu.make_async_copy(x_refs[i], out_refs[i].at[4 * x + 2 * y + c], local_sems.at[i])

    def first(i):
        return [copy(i, 0, me, sibling, own=True)] + [copy(i, 1 + j, me, (*chip, c), own=True)
                                                     for j, chip in enumerate(chips)]

    def passed(i, j):
        return copy(i, 4 + j, (*chips[j], c), sibling)

    def start():
        for i in range(nb):
            mine(i).start()
            for cp in first(i):
                cp.start()

    def forward():
        for i in range(nb):
            for j, chip in enumerate(chips):
                copy(i, 1 + j, (*chip, c), me).wait_recv()
                passed(i, j).start()

    def finish():
        for i in range(nb):
            copy(i, 0, sibling, me).wait_recv()
            for j, chip in enumerate(chips):
                copy(i, 4 + j, (*chip, 1 - c), me).wait_recv()
        for i in range(nb):
            for cp in first(i) + [passed(i, j) for j in range(3)]:
                cp.wait_send()
            mine(i).wait()

    return start, forward, finish


def _swap_with_sibling(pks, name):
    nb = len(pks)

    def body(*refs):
        start, finish = _sibling_swap_phases(refs[:nb], refs[nb:2 * nb], *refs[2 * nb:])
        start()
        finish()

    return pl.pallas_call(
        body, name=name, out_shape=_sibling_swap_shapes(pks), in_specs=[ANY] * nb, out_specs=[ANY] * nb,
        scratch_shapes=_sibling_swap_sems(nb),
    )(*pks)


def _sibling_swap_shapes(pks):
    return [S((4,) + p.shape[1:], p.dtype) for p in pks]


def _sibling_swap_sems(nb):
    return [pltpu.SemaphoreType.DMA((4 * nb,)), pltpu.SemaphoreType.DMA((4 * nb,))]


def _sibling_swap_phases(pk_refs, out_refs, send_sems, recv_sems):
    nb = len(pk_refs)
    x, y, c = _position()

    def copies():
        return [pltpu.make_async_remote_copy(
            src_ref=pk_refs[i].at[2 * k + 1 - c], dst_ref=out_refs[i].at[k], send_sem=send_sems.at[4 * i + k],
            recv_sem=recv_sems.at[4 * i + k], device_id=(x, y, 1 - c), device_id_type=MESH)
            for i in range(nb) for k in range(4)]

    def start():
        for cp in copies():
            cp.start()

    def finish():
        for cp in copies():
            cp.wait()

    return start, finish


def _swap_between_chips(pbs, name):
    nb = len(pbs)

    def body(*refs):
        start, finish = _chip_swap_phases(refs[:nb], refs[nb:2 * nb], *refs[2 * nb:])
        start()
        finish()

    return pl.pallas_call(
        body, name=name, out_shape=[S(p.shape, p.dtype) for p in pbs], in_specs=[ANY] * nb, out_specs=[ANY] * nb,
        scratch_shapes=_chip_swap_sems(nb),
    )(*pbs)


def _chip_swap_sems(nb):
    return [pltpu.SemaphoreType.DMA((3 * nb,)), pltpu.SemaphoreType.DMA((3 * nb,)), pltpu.SemaphoreType.DMA((nb,))]


def _chip_swap_phases(pb_refs, out_refs, send_sems, recv_sems, local_sems):
    nb = len(pb_refs)
    x, y, c = _position()
    me = 2 * x + y
    chips = [(1 - x, y), (x, 1 - y), (1 - x, 1 - y)]

    def local(i):
        return pltpu.make_async_copy(pb_refs[i].at[me], out_refs[i].at[me], local_sems.at[i])

    def send(i, j):
        cx, cy = chips[j]
        return pltpu.make_async_remote_copy(
            src_ref=pb_refs[i].at[2 * cx + cy], dst_ref=out_refs[i].at[me], send_sem=send_sems.at[3 * i + j],
            recv_sem=recv_sems.at[3 * i + j], device_id=(cx, cy, c), device_id_type=MESH)

    def arrival(i, j):
        cx, cy = chips[j]
        return pltpu.make_async_remote_copy(
            src_ref=pb_refs[i].at[me], dst_ref=out_refs[i].at[2 * cx + cy], send_sem=send_sems.at[3 * i + j],
            recv_sem=recv_sems.at[3 * i + j], device_id=(cx, cy, c), device_id_type=MESH)

    def start():
        for i in range(nb):
            local(i).start()
            for j in range(3):
                send(i, j).start()

    def finish():
        for i in range(nb):
            for j in range(3):
                arrival(i, j).wait_recv()
        for i in range(nb):
            for j in range(3):
                send(i, j).wait_send()
            local(i).wait()

    return start, finish


def _row_tile(r):
    return max(t for t in range(16, min(r, 512) + 1, 16) if r % t == 0)


def _pair_sum_cast(pk, recv, core, name):
    _, r, l = pk.shape
    tr = _row_tile(r)

    def body(c_ref, a_ref, b_ref, o_ref):
        o_ref[...] = (a_ref[...] + b_ref[...]).astype(BF16)

    return pl.pallas_call(
        body, name=name,
        grid_spec=pltpu.PrefetchScalarGridSpec(
            num_scalar_prefetch=1, grid=(4, r // tr),
            in_specs=[pl.BlockSpec((None, tr, l), lambda k, i, c: (2 * k + c[0], i, 0)),
                      pl.BlockSpec((None, tr, l), lambda k, i, c: (k, i, 0))],
            out_specs=pl.BlockSpec((None, tr, l), lambda k, i, c: (k, i, 0))),
        out_shape=S((4, r, l), BF16), compiler_params=_cp(("parallel", "parallel")),
    )(core, pk, recv)


def _final_sum(pk, recv_sib, recv_chips, slot, chip, name):
    _, r, l = pk.shape
    tr = _row_tile(r)

    def body(s_ref, k_ref, a_ref, b_ref, rc_ref, o_ref):
        base = a_ref[...] + b_ref[...]
        acc = jnp.zeros_like(base)
        for j in range(4):
            acc = acc + jnp.where(k_ref[0] == j, base, rc_ref[j].astype(F32))
        o_ref[...] = acc

    return pl.pallas_call(
        body, name=name,
        grid_spec=pltpu.PrefetchScalarGridSpec(
            num_scalar_prefetch=2, grid=(r // tr,),
            in_specs=[pl.BlockSpec((None, tr, l), lambda i, s, k: (s[0], i, 0)),
                      pl.BlockSpec((None, tr, l), lambda i, s, k: (k[0], i, 0)),
                      pl.BlockSpec((4, tr, l), lambda i, s, k: (0, i, 0))],
            out_specs=pl.BlockSpec((tr, l), lambda i, s, k: (i, 0))),
        out_shape=S((r, l), F32), compiler_params=_cp(("parallel",)),
    )(slot, chip, pk, recv_sib, recv_chips)


def _adamw_math(w, g, m, v):
    m = ADAM_B1 * m + (1.0 - ADAM_B1) * g
    v = ADAM_B2 * v + (1.0 - ADAM_B2) * (g * g)
    m_hat = m / (1.0 - ADAM_B1 ** ADAM_STEP)
    v_hat = v / (1.0 - ADAM_B2 ** ADAM_STEP)
    return -ADAM_LR * (m_hat / (jnp.sqrt(v_hat) + ADAM_EPS) + ADAM_WD * w), m, v


def _adamw(w, g, m, v, name):
    r, c = w.shape
    tr = 256 if r % 256 == 0 else r

    def body(w_ref, g_ref, m_ref, v_ref, d_ref, nm_ref, nv_ref):
        d_ref[...], nm_ref[...], nv_ref[...] = _adamw_math(w_ref[...], g_ref[...], m_ref[...], v_ref[...])

    tile = pl.BlockSpec((tr, c), lambda i: (i, 0))
    return pl.pallas_call(
        body, name=name, grid=(r // tr,), in_specs=[tile] * 4, out_specs=[tile] * 3, out_shape=[S((r, c), F32)] * 3,
        compiler_params=_cp(("parallel",)),
    )(w, g, m, v)


def _small_update(gathered, w, m, v):
    def body(ga_ref, w_ref, m_ref, v_ref, g_ref, d_ref, nm_ref, nv_ref):
        g = ga_ref[0]
        for k in range(1, N_DEV):
            g = g + ga_ref[k]
        g_ref[...] = g
        d_ref[...], nm_ref[...], nv_ref[...] = _adamw_math(w_ref[...], g, m_ref[...], v_ref[...])

    return pl.pallas_call(body, name="small_update", out_shape=[S(w.shape, F32)] * 4, compiler_params=_cp())(
        gathered, w, m, v)


BIG = ("w_in", "mem_kv_w", "w_br_hgrn", "w_br_fox", "w_br_mem", "w_out", "ffn_w_up", "ffn_conv_w", "ffn_w_down")
GROUP_ROWS = ("w_out", "ffn_w_down")
GROUP_LANE = ("w_br_hgrn", "w_br_fox", "w_br_mem")
LANE_GROUP_ROWS = 224
SMALL = ("norm_mix_g", "norm_mem_g", "hgrn_lb_logits", "hgrn_norm_g", "fox_f_bias", "fox_q_norm_g", "fox_k_norm_g",
         "mem_q_norm_g", "mem_k_norm_g", "norm_ffn_g", "ffn_conv_b")


def _rows_of(n_elems):
    return -(-n_elems // LANE)


def _to_rows(a, lead=0):
    flat = a.reshape(a.shape[:lead] + (-1,))
    pad = (-flat.shape[-1]) % LANE
    if pad:
        flat = jnp.pad(flat, [(0, 0)] * lead + [(0, pad)])
    return flat.reshape(a.shape[:lead] + (-1, LANE))


def _stack_rows(parts, lead, total_rows):
    buf = jnp.concatenate(parts, axis=lead)
    pad = total_rows - buf.shape[lead]
    return jnp.pad(buf, [(0, 0)] * lead + [(0, pad), (0, 0)])


def _round_up(n, k):
    return -(-n // k) * k


def _from_rows(rows, shape, lead=0):
    n = math.prod(shape)
    return rows.reshape(rows.shape[:lead] + (-1,))[..., :n].reshape(rows.shape[:lead] + tuple(shape))


def _blocks_to_full(blocks, kind):
    n, a, b = blocks.shape
    return blocks.transpose(1, 0, 2).reshape(a, n * b) if kind == "col" else blocks.reshape(n * a, b)


def _full_to_blocks(full, kind, n=N_DEV):
    a, b = full.shape
    return full.reshape(a, n, b // n).transpose(1, 0, 2) if kind == "col" else full.reshape(n, a // n, b)


def _lane_group_rows(shard):
    n_lane = sum(shard[n].shape[0] for n in GROUP_LANE)
    n_cw = shard["ffn_conv_w"].size
    return n_lane, _rows_of(3 * n_cw), _rows_of(n_cw), _round_up(n_lane + _rows_of(3 * n_cw), LANE_GROUP_ROWS)


def _split_bf16x3(x):
    hi = x.astype(BF16)
    r1 = x - hi.astype(F32)
    mid = r1.astype(BF16)
    return jnp.stack([hi, mid, (r1 - mid.astype(F32)).astype(BF16)])


class _Exchange:
    def __init__(self, shard):
        self.shard = shard
        xi, yi, ci = _position()
        self.core = ci.astype(jnp.int32).reshape(1)
        self.chip = (2 * xi + yi).astype(jnp.int32).reshape(1)
        self.n_lane, self.r_pieces, self.r_vals, self.r_lane = _lane_group_rows(shard)

    def first_blocks(self):
        return [self.shard["w_in"].astype(BF16), self.shard["mem_kv_w"].astype(BF16)]

    def unpack_first(self, gathered):
        return {"w_in": _perm_from_blocks(gathered[0]), "mem_kv_w": _blocks_to_full(gathered[1], "row")}

    def late_blocks(self):
        sh = self.shard
        lane_rows = [sh[n].astype(BF16) for n in GROUP_LANE] + [_to_rows(_split_bf16x3(sh["ffn_conv_w"]))]
        return [sh[n].astype(BF16) for n in GROUP_ROWS] + [sh["ffn_w_up"].astype(BF16),
                                                           _stack_rows(lane_rows, 0, self.r_lane)]

    def unpack_late(self, gathered):
        *rows, gc, gd = gathered
        sh = self.shard
        W = {"ffn_w_up": _blocks_to_full(gc, "col")}
        for n, blocks in zip(GROUP_ROWS, rows):
            W[n] = _blocks_to_full(blocks, "row")
        r0 = 0
        for n in GROUP_LANE:
            W[n] = _blocks_to_full(gd[:, r0:r0 + sh[n].shape[0]], "col")
            r0 += sh[n].shape[0]
        cw = _from_rows(gd[:, self.n_lane:self.n_lane + self.r_pieces], (3,) + sh["ffn_conv_w"].shape, lead=1).astype(F32)
        W["ffn_conv_w"] = _blocks_to_full(cw[:, 0] + cw[:, 1] + cw[:, 2], "col")
        return W

    def early_grads(self, g):
        cw_rows = _to_rows(_full_to_blocks(g["ffn_conv_w"], "col"), lead=1)
        return [_full_to_blocks(g[n], "row") for n in GROUP_ROWS] + [
            jnp.concatenate([_full_to_blocks(h, "col", N_DEV // 2) for h in g["ffn_w_up"]], axis=0),
            _stack_rows([_full_to_blocks(g[n], "col") for n in GROUP_LANE] + [cw_rows], 1, self.r_lane)]

    def last_grads(self, g):
        return [_unperm_blocks(g["w_in"], N_DEV), _full_to_blocks(g["mem_kv_w"], "row")]

    def pair_sums(self, pks, recv_sib, tag):
        return [_pair_sum_cast(p, r, self.core, "rs_pair_sum_%s%d" % (tag, i))
                for i, (p, r) in enumerate(zip(pks, recv_sib))]

    def final_sums(self, pks, recv_sib, recv_chips, tag):
        return [_final_sum(p, rs, rc, 2 * self.chip + self.core, self.chip, "rs_final_sum_%s%d" % (tag, i))
                for i, (p, rs, rc) in enumerate(zip(pks, recv_sib, recv_chips))]

    def unpack_grads(self, early, last):
        sh = self.shard
        *rows, g_up, g_lane = early
        g_shard = {"w_in": last[0], "mem_kv_w": last[1], "ffn_w_up": g_up, **dict(zip(GROUP_ROWS, rows))}
        r0 = 0
        for n in GROUP_LANE:
            g_shard[n] = g_lane[r0:r0 + sh[n].shape[0]]
            r0 += sh[n].shape[0]
        g_shard["ffn_conv_w"] = _from_rows(g_lane[self.n_lane:self.n_lane + self.r_vals], sh["ffn_conv_w"].shape)
        return g_shard


def kernel(x, mem, norm_mix_g, norm_mem_g, w_in, hgrn_lb_logits, hgrn_norm_g, fox_f_bias, fox_q_norm_g, fox_k_norm_g, mem_kv_w, mem_q_norm_g, mem_k_norm_g, w_br_hgrn, w_br_fox, w_br_mem, w_out, norm_ffn_g, ffn_w_up, ffn_conv_w, ffn_conv_b, ffn_w_down, loss_target, m_norm_mix_g, m_norm_mem_g, m_w_in, m_hgrn_lb_logits, m_hgrn_norm_g, m_fox_f_bias, m_fox_q_norm_g, m_fox_k_norm_g, m_mem_kv_w, m_mem_q_norm_g, m_mem_k_norm_g, m_w_br_hgrn, m_w_br_fox, m_w_br_mem, m_w_out, m_norm_ffn_g, m_ffn_w_up, m_ffn_conv_w, m_ffn_conv_b, m_ffn_w_down, v_norm_mix_g, v_norm_mem_g, v_w_in, v_hgrn_lb_logits, v_hgrn_norm_g, v_fox_f_bias, v_fox_q_norm_g, v_fox_k_norm_g, v_mem_kv_w, v_mem_q_norm_g, v_mem_k_norm_g, v_w_br_hgrn, v_w_br_fox, v_w_br_mem, v_w_out, v_norm_ffn_g, v_ffn_w_up, v_ffn_conv_w, v_ffn_conv_b, v_ffn_w_down):
    given = dict(locals())
    order = ("norm_mix_g", "norm_mem_g", "w_in", "hgrn_lb_logits", "hgrn_norm_g", "fox_f_bias", "fox_q_norm_g",
             "fox_k_norm_g", "mem_kv_w", "mem_q_norm_g", "mem_k_norm_g", "w_br_hgrn", "w_br_fox", "w_br_mem", "w_out",
             "norm_ffn_g", "ffn_w_up", "ffn_conv_w", "ffn_conv_b", "ffn_w_down")
    B, T, D = x.shape
    M = mem.shape[1]
    shard = {n: given[n][0] if n in BIG else given[n] for n in order}
    mom = {n: (given["m_" + n][0], given["v_" + n][0]) if n in BIG else (given["m_" + n], given["v_" + n])
           for n in order}
    shard["hgrn_lb_logits"] = hgrn_lb_logits
    for n in ("norm_mix_g", "norm_mem_g", "hgrn_norm_g", "fox_f_bias", "fox_q_norm_g", "fox_k_norm_g", "mem_q_norm_g",
              "mem_k_norm_g", "norm_ffn_g", "ffn_conv_b"):
        shard[n] = given[n].reshape(1, -1)

    ex = _Exchange(shard)
    W = ex.unpack_first(_all_gather(ex.first_blocks(), "ag_first"))

    sm = {n: shard[n] for n in SMALL}
    grad_x, g, sums = _local_step(x.reshape(B * T, D), mem.reshape(B * M, D), loss_target.reshape(B * T, D), sm, W,
                                  B, T, M, ex)
    g_shard = ex.unpack_grads(*sums)

    sg = {n: g[n] for n in SMALL}
    sg["fox_f_bias"] = g["fox_f_bias"][:, :FOX_H]
    sg["fox_q_norm_g"] = g["fox_q_norm_g"][:, :FOX_D]
    sg["fox_k_norm_g"] = g["fox_k_norm_g"][:, :FOX_D]
    slayout, row0 = {}, 0
    for n in SMALL:
        nr = _rows_of(shard[n].size)
        slayout[n] = (row0, nr)
        row0 += nr
    loss_row = row0
    r_small = _round_up(row0 + 1, 8)

    def pack_small(d, with_loss=None):
        rows = [_to_rows(d[n]) for n in SMALL]
        rows.append(with_loss if with_loss is not None else jnp.zeros((1, LANE), F32))
        return _stack_rows(rows, 0, r_small)

    sgath, = _all_gather([pack_small(sg, g["loss"])], "ag_small")
    s_g, s_d, s_m, s_v = _small_update(sgath, pack_small(shard), pack_small({n: mom[n][0].reshape(shard[n].shape) for n in SMALL}),
                                       pack_small({n: mom[n][1].reshape(shard[n].shape) for n in SMALL}))
    loss = s_g[loss_row, 0]

    grads, deltas, new_m, new_v = {}, {}, {}, {}
    for n in BIG:
        gn = g_shard[n]
        d, nm, nv = _adamw(shard[n], gn, mom[n][0], mom[n][1], "adamw_" + n)
        grads[n], deltas[n], new_m[n], new_v[n] = (a[None] for a in (gn, d, nm, nv))
    for n in SMALL:
        r0, nr = slayout[n]
        for dst, src in ((grads, s_g), (deltas, s_d), (new_m, s_m), (new_v, s_v)):
            dst[n] = _from_rows(src[r0:r0 + nr], given[n].shape)
    return (loss, grad_x.reshape(B, T, D), *[grads[n] for n in order], *[deltas[n] for n in order],
            *[new_m[n] for n in order], *[new_v[n] for n in order])
```

```python
import functools
import math

import jax
import jax.numpy as jnp
from jax import lax
from jax.experimental import pallas as pl
from jax.experimental.pallas import tpu as pltpu

F32, BF16 = jnp.float32, jnp.bfloat16
S = jax.ShapeDtypeStruct
MESH = pl.DeviceIdType.MESH

N_DEV = 8
EPS = 1e-6
LANE = 128
CHUNK = 64
SUB = 16
HG_H, HG_D = 4, 128
HG_GROUP = 2
HG_MILD = 60.0
FOX_H, FOX_D = 8, 64
FOX_P = FOX_H // 2
MEM_H, MEM_D = 4, 128
NEG = -1e30
VMEM_LIMIT = 56 * 2**20

ADAM_LR, ADAM_B1, ADAM_B2, ADAM_EPS, ADAM_WD, ADAM_STEP = 0.001, 0.9, 0.999, 1e-08, 0.01, 10

C_FOX, C_MQ, C_HG, C_GATE, C_FF, C_END = 0, 1536, 2048, 4096, 7168, 7296


def _cp(sem=None):
    return pltpu.CompilerParams(dimension_semantics=sem, vmem_limit_bytes=VMEM_LIMIT)


def _dot(a, b, dims, prec=None):
    return lax.dot_general(a, b, (dims, ((), ())), preferred_element_type=F32, precision=prec)


def _nn(a, b, prec=None):
    return _dot(a, b, ((1,), (0,)), prec)


def _nt(a, b, prec=None):
    return _dot(a, b, ((1,), (1,)), prec)


def _tn(a, b, prec=None):
    return _dot(a, b, ((0,), (0,)), prec)


def _b(x):
    return x.astype(BF16)


def _mm3(fn, a, b):
    ah, bh = _b(a), _b(b)
    return fn(ah, bh) + fn(ah, _b(b - bh.astype(F32))) + fn(_b(a - ah.astype(F32)), bh)


def _iota(shape, dim):
    return lax.broadcasted_iota(jnp.int32, shape, dim)


def _rowsum8(x):
    r, d = x.shape
    return jnp.sum(x.reshape(r // 8, 8, d), axis=0)


def _rmsnorm_cast(x, g, name, tm=512):
    n, d = x.shape

    def body(x_ref, g_ref, o_ref):
        v = x_ref[...]
        r = lax.rsqrt(jnp.mean(v * v, axis=-1, keepdims=True) + EPS)
        o_ref[...] = (v * r * g_ref[...]).astype(BF16)

    return pl.pallas_call(
        body, name=name, grid=(n // tm,),
        in_specs=[pl.BlockSpec((tm, d), lambda i: (i, 0)), pl.BlockSpec((1, d), lambda i: (0, 0))],
        out_specs=pl.BlockSpec((tm, d), lambda i: (i, 0)), out_shape=S((n, d), BF16), compiler_params=_cp(("parallel",)),
    )(x, g)


def _rmsnorm_bwd(dh, x, g, resid, name, tm=512):
    n, d = x.shape
    has_res = resid is not None

    def body(*refs):
        if has_res:
            dh_ref, x_ref, g_ref, r_ref, dx_ref, dg_ref = refs
        else:
            dh_ref, x_ref, g_ref, dx_ref, dg_ref = refs
        v = x_ref[...]
        dhv = dh_ref[...].astype(F32)
        r = lax.rsqrt(jnp.mean(v * v, axis=-1, keepdims=True) + EPS)
        xh = v * r
        u = dhv * g_ref[...]
        dx = r * (u - xh * jnp.mean(u * xh, axis=-1, keepdims=True))
        if has_res:
            dx = dx + r_ref[...]
        dx_ref[...] = dx

        @pl.when(pl.program_id(0) == 0)
        def _():
            dg_ref[...] = jnp.zeros_like(dg_ref)

        dg_ref[...] += _rowsum8(dhv * xh)

    tile = pl.BlockSpec((tm, d), lambda i: (i, 0))
    ins = [tile, tile, pl.BlockSpec((1, d), lambda i: (0, 0))] + ([tile] if has_res else [])
    args = (dh, x, g) + ((resid,) if has_res else ())
    return pl.pallas_call(
        body, name=name, grid=(n // tm,), in_specs=ins,
        out_specs=[tile, pl.BlockSpec((8, d), lambda i: (0, 0))],
        out_shape=[S((n, d), F32), S((8, d), F32)], compiler_params=_cp(("arbitrary",)),
    )(*args)


def _mm_nn(a, b, out_dtype, name, tm, tn, b_col0=0, n_out=None):
    m, k = a.shape
    n_out = b.shape[1] if n_out is None else n_out
    jb = b_col0 // tn
    assert b_col0 % tn == 0 and n_out % tn == 0 and m % tm == 0

    def body(a_ref, b_ref, o_ref):
        o_ref[...] = _nn(a_ref[...].astype(BF16), b_ref[...].astype(BF16)).astype(out_dtype)

    return pl.pallas_call(
        body, name=name, grid=(m // tm, n_out // tn),
        in_specs=[pl.BlockSpec((tm, k), lambda i, j: (i, 0)), pl.BlockSpec((k, tn), lambda i, j: (0, j + jb))],
        out_specs=pl.BlockSpec((tm, tn), lambda i, j: (i, j)), out_shape=S((m, n_out), out_dtype),
        compiler_params=_cp(("parallel", "parallel")),
    )(a, b)


def _mm_nt(dy, w, name, tm, tr, w_col0=0, acc=None):
    m, r = dy.shape
    k = w.shape[0]
    jb = w_col0 // tr
    nr = r // tr
    assert w_col0 % tr == 0 and r % tr == 0 and m % tm == 0
    has_acc = acc is not None

    def body(*refs):
        if has_acc:
            dy_ref, w_ref, acc_ref, o_ref = refs
        else:
            dy_ref, w_ref, o_ref = refs
        part = _nt(dy_ref[...].astype(BF16), w_ref[...].astype(BF16))

        @pl.when(pl.program_id(1) == 0)
        def _():
            o_ref[...] = part + acc_ref[...] if has_acc else part

        @pl.when(pl.program_id(1) > 0)
        def _():
            o_ref[...] += part

    out_tile = pl.BlockSpec((tm, k), lambda i, j: (i, 0))
    ins = [pl.BlockSpec((tm, tr), lambda i, j: (i, j)), pl.BlockSpec((k, tr), lambda i, j: (0, j + jb))]
    args = (dy, w)
    if has_acc:
        ins.append(out_tile)
        args = args + (acc,)
    return pl.pallas_call(
        body, name=name, grid=(m // tm, nr), in_specs=ins, out_specs=out_tile, out_shape=S((m, k), F32),
        input_output_aliases=({2: 0} if has_acc else {}), compiler_params=_cp(("parallel", "arbitrary")),
    )(*args)


def _mm_nt_sum(parts, w, name, tm, swap=()):
    m = parts[0][0].shape[0]
    k = w.shape[0]
    assert m % tm == 0 and all(c % n == 0 and o % n == 0 for _, c, n, o in parts)
    np_ = len(parts)
    nsw = len(swap)
    n_steps = m // tm

    def body(*refs):
        o_ref = refs[2 * np_ + nsw]
        if nsw:
            start, finish = _chip_swap_phases(refs[2 * np_:2 * np_ + nsw], refs[2 * np_ + nsw + 1:2 * np_ + 2 * nsw + 1],
                                              *refs[2 * np_ + 2 * nsw + 1:])
            pl.when(pl.program_id(0) == 0)(start)
        acc = _nt(refs[0][...].astype(BF16), refs[np_][...].astype(BF16))
        for i in range(1, np_):
            acc = acc + _nt(refs[i][...].astype(BF16), refs[np_ + i][...].astype(BF16))
        o_ref[...] = acc
        if nsw:
            pl.when(pl.program_id(0) == n_steps - 1)(finish)

    dy_specs = [pl.BlockSpec((tm, n), functools.partial(lambda i, j: (i, j), j=c // n)) for _, c, n, _ in parts]
    w_specs = [pl.BlockSpec((k, n), functools.partial(lambda i, j: (0, j), j=o // n)) for _, _, n, o in parts]
    out = pl.pallas_call(
        body, name=name, grid=(n_steps,), in_specs=dy_specs + w_specs + [ANY] * nsw,
        out_specs=[pl.BlockSpec((tm, k), lambda i: (i, 0))] + [ANY] * nsw,
        out_shape=[S((m, k), F32)] + [S(p.shape, p.dtype) for p in swap],
        scratch_shapes=_chip_swap_sems(nsw) if nsw else [],
        compiler_params=_cp(("arbitrary",) if nsw else ("parallel",)),
    )(*([p[0] for p in parts] + [w] * np_ + list(swap)))
    return (out[0], out[1:]) if nsw else out[0]


def _mm_tn(x, dy, name, tm, tn):
    m, k = x.shape
    n = dy.shape[1]
    tm = min(tm, m)
    assert m % tm == 0 and n % tn == 0

    def body(x_ref, dy_ref, o_ref):
        part = _tn(x_ref[...].astype(BF16), dy_ref[...].astype(BF16))

        @pl.when(pl.program_id(1) == 0)
        def _():
            o_ref[...] = part

        @pl.when(pl.program_id(1) > 0)
        def _():
            o_ref[...] += part

    return pl.pallas_call(
        body, name=name, grid=(n // tn, m // tm),
        in_specs=[pl.BlockSpec((tm, k), lambda j, i: (i, 0)), pl.BlockSpec((tm, tn), lambda j, i: (i, j))],
        out_specs=pl.BlockSpec((k, tn), lambda j, i: (0, j)), out_shape=S((k, n), F32),
        compiler_params=_cp(("parallel", "arbitrary")),
    )(x, dy)


def _lower_bound(logits):
    e = jnp.exp(logits - jnp.max(logits, axis=0, keepdims=True))
    return e[0:1, :] / jnp.sum(e, axis=0, keepdims=True)


def _hg_gates(fl, lb):
    sig = jax.nn.sigmoid(fl)
    f = lb + (1.0 - lb) * sig
    k = (1.0 - lb) * (1.0 - sig)
    return sig, f, k, jnp.log(f)


def _silu_and_grad(x):
    s = jax.nn.sigmoid(x)
    return x * s, s * (1.0 + x * (1.0 - s))


def _hg_rowblocks(G):
    return [None] + [G[SUB * i - 1:SUB * i, :] for i in range(1, CHUNK // SUB)]


def _hg_intra_A(qs, k, G):
    refs = _hg_rowblocks(G)
    cols = _iota((SUB, CHUNK), 1)
    rows = _iota((SUB, CHUNK), 0)
    blocks = []
    for i in range(CHUNK // SUB):
        lo = SUB * i
        qb, Gb = qs[lo:lo + SUB, :], G[lo:lo + SUB, :]
        diag = jnp.zeros((SUB, CHUNK), F32)
        for s in range(SUB):
            e = jnp.exp(jnp.minimum(Gb - G[lo + s:lo + s + 1, :], 0.0))
            col = jnp.sum(qb * k[lo + s:lo + s + 1, :] * e, axis=-1, keepdims=True)
            diag = jnp.where(cols == lo + s, col, diag)
        a = jnp.where((cols >= lo) & (cols <= rows + lo), diag, 0.0)
        if i > 0:
            qr = qb * jnp.exp(Gb - refs[i])
            kr = k * jnp.exp(jnp.minimum(refs[i] - G, 0.0))
            a = jnp.where(cols < lo, _nt(_b(qr), _b(kr)), a)
        blocks.append(a)
    return jnp.concatenate(blocks, axis=0)


def _hg_intra_bwd(dA, qs, k, G):
    refs = _hg_rowblocks(G)
    cols = _iota((SUB, CHUNK), 1)
    rows16 = _iota((SUB, HG_D), 0)
    dk = jnp.zeros((CHUNK, HG_D), F32)
    dq_blocks, dk_diag_blocks = [], []
    for i in range(CHUNK // SUB):
        lo = SUB * i
        qb, Gb = qs[lo:lo + SUB, :], G[lo:lo + SUB, :]
        dAb = dA[lo:lo + SUB, :]
        dq = jnp.zeros((SUB, HG_D), F32)
        dkb = jnp.zeros((SUB, HG_D), F32)
        for s in range(SUB):
            e = jnp.exp(jnp.minimum(Gb - G[lo + s:lo + s + 1, :], 0.0))
            e = jnp.where(rows16 >= s, e, 0.0)
            dcol = jnp.sum(jnp.where(cols == lo + s, dAb, 0.0), axis=-1, keepdims=True)
            w = dcol * e
            dq = dq + w * k[lo + s:lo + s + 1, :]
            dkb = jnp.where(rows16 == s, jnp.sum(w * qb, axis=0, keepdims=True), dkb)
        if i > 0:
            e1 = jnp.exp(Gb - refs[i])
            e2 = jnp.exp(jnp.minimum(refs[i] - G, 0.0))
            dA_off = jnp.where(cols < lo, dAb, 0.0)
            dq = dq + _mm3(_nn, dA_off, k * e2) * e1
            dk = dk + _mm3(_tn, dA_off, qb * e1) * e2
        dq_blocks.append(dq)
        dk_diag_blocks.append(dkb)
    return jnp.concatenate(dq_blocks, axis=0), dk + jnp.concatenate(dk_diag_blocks, axis=0)


def _hg_mild_decay(Gs):
    worst = Gs[0][0:1, :] - Gs[0][CHUNK - 1:CHUNK, :]
    for G in Gs[1:]:
        worst = jnp.maximum(worst, G[0:1, :] - G[CHUNK - 1:CHUNK, :])
    return jnp.max(worst) <= HG_MILD


def _hg_intra_A_mild(qs, kL, G):
    qL = qs * jnp.exp(G - G[CHUNK - 1:CHUNK, :])
    cmask = _iota((CHUNK, CHUNK), 0) >= _iota((CHUNK, CHUNK), 1)
    return jnp.where(cmask, _nt(_b(qL), _b(kL)), 0.0)


def _hg_intra_mild_bwd(dA, qs, k, eL, G):
    e1 = jnp.exp(G - G[CHUNK - 1:CHUNK, :])
    qL, kL = qs * e1, k * eL
    cmask = _iota((CHUNK, CHUNK), 0) >= _iota((CHUNK, CHUNK), 1)
    A = jnp.where(cmask, _nt(_b(qL), _b(kL)), 0.0)
    return A, _mm3(_nn, dA, kL) * e1, _mm3(_tn, dA, qL) * eL


def _tri(n, upper=False):
    r, c = _iota((n, n), 0), _iota((n, n), 1)
    return jnp.where((c >= r) if upper else (r >= c), 1.0, 0.0).astype(BF16)


def _prefix_mm(tri, x):
    hi = x.astype(BF16)
    r1 = x - hi.astype(F32)
    mid = r1.astype(BF16)
    lo = (r1 - mid.astype(F32)).astype(BF16)
    return _nn(tri, hi) + _nn(tri, mid) + _nn(tri, lo)


def _hgrn_fwd(z, lb, gn, B, T):
    N = B * T
    NC = T // CHUNK
    ng = HG_H // HG_GROUP

    def body(z_ref, lb_ref, gn_ref, y_ref, o_ref, st_ref, s_scr):
        lbs = _lower_bound(lb_ref[...])
        tri = _tri(CHUNK)
        s_scr[...] = jnp.zeros_like(s_scr)

        def chunk(c, carry):
            r = pl.ds(pl.multiple_of(c * CHUNK, CHUNK), CHUNK)
            pre = []
            for hh in range(HG_GROUP):
                zc, oc = 4 * LANE * hh, LANE * hh
                ql, fl, il, gl = (z_ref[r, zc + LANE * j:zc + LANE * (j + 1)] for j in range(4))
                _, _, k, logf = _hg_gates(fl, lbs[:, oc:oc + LANE])
                pre.append((oc, il, gl, k, _prefix_mm(tri, logf), ql * jax.nn.sigmoid(ql)))

            def rest(mild):
                for hh, (oc, il, gl, k, G, qs) in enumerate(pre):
                    st = s_scr[hh]
                    st_ref[hh * NC + c] = st
                    g_last = G[CHUNK - 1:CHUNK, :]
                    kL = k * jnp.exp(g_last - G)
                    A = _hg_intra_A_mild(qs, kL, G) if mild else _hg_intra_A(qs, k, G)
                    o = _nn(_b(A), _b(il)) + _nt(_b(qs * jnp.exp(G)), _b(st))
                    s_scr[hh] = st * jnp.exp(g_last) + _mm3(_tn, il, kL)
                    o_ref[r, oc:oc + LANE] = o
                    rstd = lax.rsqrt(jnp.mean(o * o, axis=-1, keepdims=True) + EPS)
                    y_ref[r, oc:oc + LANE] = (o * rstd * gn_ref[...] * (gl * jax.nn.sigmoid(gl))).astype(BF16)

            lax.cond(_hg_mild_decay([p[4] for p in pre]), functools.partial(rest, True), functools.partial(rest, False))
            return carry

        lax.fori_loop(0, NC, chunk, 0)

    gw = HG_GROUP * LANE
    cb = C_HG // (4 * gw)
    return pl.pallas_call(
        body, name="hgrn_fwd", grid=(B, ng),
        in_specs=[pl.BlockSpec((T, 4 * gw), lambda b, h: (b, cb + h)), pl.BlockSpec((lb.shape[0], gw), lambda b, h: (0, h)),
                  pl.BlockSpec((1, LANE), lambda b, h: (0, 0))],
        out_specs=[pl.BlockSpec((T, gw), lambda b, h: (b, h)), pl.BlockSpec((T, gw), lambda b, h: (b, h)),
                   pl.BlockSpec((HG_GROUP * NC, HG_D, HG_D), lambda b, h: (b * ng + h, 0, 0))],
        out_shape=[S((N, 512), BF16), S((N, 512), F32), S((B * HG_H * NC, HG_D, HG_D), F32)],
        scratch_shapes=[pltpu.VMEM((HG_GROUP, HG_D, HG_D), F32)], compiler_params=_cp(("parallel", "parallel")),
    )(z, lb, gn)


def _hgrn_bwd(z, o_raw, states, dy, lb, gn, B, T, swap_sibling=()):
    N = B * T
    NC = T // CHUNK
    ng = HG_H // HG_GROUP
    nsw = len(swap_sibling)

    def body(z_ref, o_ref, st_ref, dy_ref, lb_ref, gn_ref, dz_ref, dlb_ref, dgn_ref, ds_scr, racc, dgn_acc):
        lbs = _lower_bound(lb_ref[...])
        gn_v = gn_ref[...]
        tri, triu = _tri(CHUNK), _tri(CHUNK, upper=True)
        cmask = _iota((CHUNK, CHUNK), 0) >= _iota((CHUNK, CHUNK), 1)
        for ref in (ds_scr, racc, dgn_acc, dlb_ref):
            ref[...] = jnp.zeros_like(ref)

        def chunk(ci, carry):
            c = NC - 1 - ci
            r = pl.ds(pl.multiple_of(c * CHUNK, CHUNK), CHUNK)
            pre = []
            for hh in range(HG_GROUP):
                zc, oc = 4 * LANE * hh, LANE * hh
                lb_v = lbs[:, oc:oc + LANE]
                ql, fl, il, gl = (z_ref[r, zc + LANE * j:zc + LANE * (j + 1)] for j in range(4))
                sig, f, k, logf = _hg_gates(fl, lb_v)
                pre.append((zc, oc, lb_v, ql, il, gl, sig, f, k, _prefix_mm(tri, logf)))

            def rest(mild):
                for hh, (zc, oc, lb_v, ql, il, gl, sig, f, k, G) in enumerate(pre):
                    qs, dsilu_q = _silu_and_grad(ql)
                    gs, dsilu_g = _silu_and_grad(gl)
                    o = o_ref[r, oc:oc + LANE]
                    dyv = dy_ref[r, oc:oc + LANE]
                    rstd = lax.rsqrt(jnp.mean(o * o, axis=-1, keepdims=True) + EPS)
                    oh = o * rstd
                    dgl = dyv * oh * gn_v * dsilu_g
                    dn = dyv * gs
                    dgn_acc[...] += _rowsum8(dn * oh)
                    u = dn * gn_v
                    do = rstd * (u - oh * jnp.mean(u * oh, axis=-1, keepdims=True))
                    st = st_ref[hh * NC + c]
                    dst = ds_scr[hh]
                    eG = jnp.exp(G)
                    g_last = G[CHUNK - 1:CHUNK, :]
                    eL = jnp.exp(g_last - G)
                    dA = jnp.where(cmask, _mm3(_nt, do, il), 0.0)
                    if mild:
                        A, dq_in, dk_in = _hg_intra_mild_bwd(dA, qs, k, eL, G)
                    else:
                        A, (dq_in, dk_in) = _hg_intra_A(qs, k, G), _hg_intra_bwd(dA, qs, k, G)
                    di = _tn(_b(A), _b(do)) + _nt(_b(k * eL), _b(dst))
                    dq = dq_in + _mm3(_nn, do, st) * eG
                    dk = dk_in + _mm3(_nn, il, dst) * eL
                    ds_scr[hh] = dst * jnp.exp(g_last) + _mm3(_tn, do, qs * eG)
                    dd = qs * dq - k * dk
                    dlogf = _prefix_mm(triu, dd) + racc[hh]
                    racc[hh] += jnp.sum(dd, axis=0, keepdims=True)
                    df = dlogf / f - dk
                    dlb_ref[8 * hh:8 * (hh + 1), :] += _rowsum8(df * (1.0 - sig))
                    dz_ref[r, zc:zc + LANE] = (dq * dsilu_q).astype(BF16)
                    dz_ref[r, zc + LANE:zc + 2 * LANE] = (df * (1.0 - lb_v) * sig * (1.0 - sig)).astype(BF16)
                    dz_ref[r, zc + 2 * LANE:zc + 3 * LANE] = di.astype(BF16)
                    dz_ref[r, zc + 3 * LANE:zc + 4 * LANE] = dgl.astype(BF16)

            lax.cond(_hg_mild_decay([p[9] for p in pre]), functools.partial(rest, True), functools.partial(rest, False))
            return carry

        lax.fori_loop(0, NC, chunk, 0)
        dgn_ref[...] = dgn_acc[...]

    gw = HG_GROUP * LANE
    cb = C_HG // (4 * gw)
    col = pl.BlockSpec((T, gw), lambda b, h: (b, h))
    if nsw:
        body = _hosting(body, 6, 3, 3, nsw, _sibling_swap_phases, (B, ng))
    return pl.pallas_call(
        body, name="hgrn_bwd", grid=(B, ng),
        in_specs=[pl.BlockSpec((T, 4 * gw), lambda b, h: (b, cb + h)), col,
                  pl.BlockSpec((HG_GROUP * NC, HG_D, HG_D), lambda b, h: (b * ng + h, 0, 0)), col,
                  pl.BlockSpec((lb.shape[0], gw), lambda b, h: (0, h)), pl.BlockSpec((1, LANE), lambda b, h: (0, 0))]
        + [ANY] * nsw,
        out_specs=[pl.BlockSpec((T, 4 * gw), lambda b, h: (b, h)),
                   pl.BlockSpec((8 * HG_GROUP, LANE), lambda b, h: (b * ng + h, 0)),
                   pl.BlockSpec((8, LANE), lambda b, h: (b * ng + h, 0))] + [ANY] * nsw,
        out_shape=[S((N, 2048), BF16), S((B * HG_H * 8, LANE), F32), S((B * ng * 8, LANE), F32)]
        + _sibling_swap_shapes(swap_sibling),
        scratch_shapes=[pltpu.VMEM((HG_GROUP, HG_D, HG_D), F32), pltpu.VMEM((HG_GROUP, 1, LANE), F32),
                        pltpu.VMEM((8, LANE), F32)] + (_sibling_swap_sems(nsw) if nsw else []),
        compiler_params=_cp(("arbitrary", "arbitrary") if nsw else ("parallel", "parallel")),
    )(z, o_raw, states, dy, lb, gn, *swap_sibling)


def _pair_mean(x, lo_half):
    a = jnp.sum(jnp.where(lo_half, x, 0.0), axis=-1, keepdims=True)
    b = jnp.sum(jnp.where(lo_half, 0.0, x), axis=-1, keepdims=True)
    return jnp.where(lo_half, a, b) * (1.0 / FOX_D)


def _fox_gate_fwd(z, bias, B, T):
    N = B * T
    tb = LANE

    def body(z_ref, b_ref, fc_ref, fct_ref):
        tri = _tri(tb)

        def step(i, carry):
            r = pl.ds(pl.multiple_of(i * tb, tb), tb)
            cs = _prefix_mm(tri, jax.nn.log_sigmoid(z_ref[r, :] + b_ref[...])) + carry
            fc_ref[r, :] = cs
            fct_ref[0, :, r] = cs.T[0:8, :]
            return cs[tb - 1:tb, :]

        lax.fori_loop(0, T // tb, step, jnp.zeros((1, LANE), F32))

    return pl.pallas_call(
        body, name="fox_gate_fwd", grid=(B,),
        in_specs=[pl.BlockSpec((T, LANE), lambda b: (b, C_FF // LANE)), pl.BlockSpec((1, LANE), lambda b: (0, 0))],
        out_specs=[pl.BlockSpec((T, LANE), lambda b: (b, 0)), pl.BlockSpec((1, 8, T), lambda b: (b, 0, 0))],
        out_shape=[S((N, LANE), F32), S((B, 8, T), F32)], compiler_params=_cp(("parallel",)),
    )(z, bias)


def _fox_gate_bwd(dfc, z, bias, B, T):
    N = B * T
    tb = LANE
    nt = T // tb

    def body(d_ref, z_ref, b_ref, dz_ref, db_ref):
        triu = _tri(tb, upper=True)
        db_ref[...] = jnp.zeros_like(db_ref)

        def step(ii, carry):
            r = pl.ds(pl.multiple_of((nt - 1 - ii) * tb, tb), tb)
            d = d_ref[r, 0:LANE]
            for p in range(1, FOX_P):
                d = d + d_ref[r, LANE * p:LANE * (p + 1)]
            rc = _prefix_mm(triu, d) + carry
            dff = rc * jax.nn.sigmoid(-(z_ref[r, :] + b_ref[...]))
            dz_ref[r, :] = dff.astype(BF16)
            db_ref[...] += _rowsum8(dff)
            return carry + jnp.sum(d, axis=0, keepdims=True)

        lax.fori_loop(0, nt, step, jnp.zeros((1, LANE), F32))

    return pl.pallas_call(
        body, name="fox_gate_bwd", grid=(B,),
        in_specs=[pl.BlockSpec((T, 512), lambda b: (b, 0)), pl.BlockSpec((T, LANE), lambda b: (b, C_FF // LANE)),
                  pl.BlockSpec((1, LANE), lambda b: (0, 0))],
        out_specs=[pl.BlockSpec((T, LANE), lambda b: (b, 0)), pl.BlockSpec((8, LANE), lambda b: (b, 0))],
        out_shape=[S((N, LANE), BF16), S((B * 8, LANE), F32)], compiler_params=_cp(("parallel",)),
    )(dfc, z, bias)


def _fox_prep(z_ref, gq, gk, r, lo_half):
    q, k, v = z_ref[r, 0:LANE], z_ref[r, LANE:2 * LANE], z_ref[r, 2 * LANE:3 * LANE]
    rq = lax.rsqrt(_pair_mean(q * q, lo_half) + EPS)
    rk = lax.rsqrt(_pair_mean(k * k, lo_half) + EPS)
    qh, kh = q * rq, k * rk
    return qh * gq * (FOX_D ** -0.5), kh * gk, v, qh, kh, rq, rk


def _fox_fwd(z, fc, fct, gq, gk, B, T, tq=512, gather=()):
    N = B * T
    NQ = T // tq
    nga = len(gather)

    def body(z_ref, fc_ref, fct_ref, gq_ref, gk_ref, y_ref, lse_ref, qn_s, kn_s, v_s):
        p, qi = pl.program_id(1), pl.program_id(2)
        lo_half = _iota((1, LANE), 1) < FOX_D

        @pl.when(qi == 0)
        def _():
            def prep(i, carry):
                r = pl.ds(pl.multiple_of(i * tq, tq), tq)
                qn, kn, v = _fox_prep(z_ref, gq_ref[...], gk_ref[...], r, lo_half)[:3]
                qn_s[r, :], kn_s[r, :], v_s[r, :] = qn.astype(BF16), kn.astype(BF16), v.astype(BF16)
                return carry
            lax.fori_loop(0, NQ, prep, 0)

        rq = pl.ds(pl.multiple_of(qi * tq, tq), tq)
        qn = qn_s[rq, :]
        fcq = fc_ref[rq, :]
        lane = _iota((tq, LANE), 1)
        causal = _iota((tq, tq), 0) >= _iota((tq, tq), 1)
        qhs = [jnp.where(lo_half, qn, jnp.zeros_like(qn)), jnp.where(lo_half, jnp.zeros_like(qn), qn)]
        fqs = [jnp.sum(jnp.where(lane == 2 * p + hh, fcq, 0.0), axis=-1, keepdims=True) for hh in range(2)]

        def kv(j, carry, diagonal):
            rk = pl.ds(pl.multiple_of(j * tq, tq), tq)
            kj, vj = kn_s[rk, :], v_s[rk, :]
            new = []
            for hh in range(2):
                m, l, acc = carry[hh]
                s = _nt(qhs[hh], kj) + fqs[hh] - fct_ref[0, pl.ds(2 * p + hh, 1), rk]
                if diagonal:
                    s = jnp.where(causal, s, NEG)
                m_new = jnp.maximum(m, jnp.max(s, axis=-1, keepdims=True))
                pe = jnp.exp(s - m_new)
                alpha = jnp.exp(m - m_new)
                new.append((m_new, alpha * l + jnp.sum(pe, axis=-1, keepdims=True),
                            alpha * acc + _nn(pe.astype(BF16), vj)))
            return tuple(new)

        init = tuple((jnp.full((tq, 1), NEG, F32), jnp.zeros((tq, 1), F32), jnp.zeros((tq, LANE), F32)) for _ in range(2))
        carry = lax.fori_loop(0, qi, functools.partial(kv, diagonal=False), init)
        (m0, l0, a0), (m1, l1, a1) = kv(qi, carry, True)
        y_ref[...] = jnp.where(lo_half, a0 / l0, a1 / l1).astype(BF16)
        lse_ref[...] = jnp.where(lo_half, m0 + jnp.log(l0), m1 + jnp.log(l1))

    vec = pl.BlockSpec((1, LANE), lambda b, p, q: (0, 0))
    tile = pl.BlockSpec((tq, LANE), lambda b, p, q: (b * NQ + q, p))
    if nga:
        body = _hosting(body, 5, 2, 3, nga, _gather_phases, (B, FOX_P, NQ))
    return pl.pallas_call(
        body, name="fox_fwd", grid=(B, FOX_P, NQ),
        in_specs=[pl.BlockSpec((T, 384), lambda b, p, q: (b, p)), pl.BlockSpec((T, LANE), lambda b, p, q: (b, 0)),
                  pl.BlockSpec((1, 8, T), lambda b, p, q: (b, 0, 0)), vec, vec] + [ANY] * nga,
        out_specs=[tile, tile] + [ANY] * nga, out_shape=[S((N, 512), BF16), S((N, 512), F32)] + _gather_shapes(gather),
        scratch_shapes=[pltpu.VMEM((T, LANE), BF16)] * 3 + (_gather_sems(nga) if nga else []),
        compiler_params=_cp(("arbitrary",) * 3 if nga else ("parallel", "parallel", "arbitrary")),
    )(z, fc, fct, gq, gk, *gather)


def _fox_bwd(z, dy, y, lse, fc, fct, gq, gk, B, T, tq=512, swap=()):
    N = B * T
    NQ = T // tq
    nsw = len(swap)

    def body(z_ref, dy_ref, y_ref, lse_ref, fc_ref, fct_ref, gq_ref, gk_ref, dz_ref, dfc_ref, dgq_ref, dgk_ref,
             qn_s, kn_s, v_s, do_s, delta_s, dq_s, dfk_s):
        p, kj = pl.program_id(1), pl.program_id(2)
        lo_half = _iota((1, LANE), 1) < FOX_D
        lane = _iota((tq, LANE), 1)
        gq_v, gk_v = gq_ref[...], gk_ref[...]

        @pl.when(kj == 0)
        def _():
            def prep(i, carry):
                r = pl.ds(pl.multiple_of(i * tq, tq), tq)
                qn, kn, v = _fox_prep(z_ref, gq_v, gk_v, r, lo_half)[:3]
                qn_s[r, :], kn_s[r, :], v_s[r, :] = qn.astype(BF16), kn.astype(BF16), v.astype(BF16)
                do = dy_ref[r, :]
                do_s[r, :] = do.astype(BF16)
                delta_s[r, :] = _pair_mean(do * y_ref[r, :].astype(F32), lo_half) * float(FOX_D)
                return carry
            lax.fori_loop(0, NQ, prep, 0)
            dq_s[...] = jnp.zeros_like(dq_s)
            dgq_ref[...] = jnp.zeros_like(dgq_ref)
            dgk_ref[...] = jnp.zeros_like(dgk_ref)

        rk = pl.ds(pl.multiple_of(kj * tq, tq), tq)
        kn, vv = kn_s[rk, :], v_s[rk, :]
        causal = _iota((tq, tq), 0) >= _iota((tq, tq), 1)
        zero, one = jnp.zeros_like(kn), jnp.ones_like(kn)
        hms = [lo_half, jnp.logical_not(lo_half)]
        kmasks = [jnp.where(hm, kn, zero) for hm in hms]
        kaugs = [jnp.where(hm, kn, one) for hm in hms]
        vmasks = [jnp.where(hm, vv, zero) for hm in hms]
        fks = [fct_ref[0, pl.ds(2 * p + hh, 1), rk] for hh in range(2)]

        def qloop(i, carry, diagonal):
            ri = pl.ds(pl.multiple_of(i * tq, tq), tq)
            qn = qn_s[ri, :]
            do = do_s[ri, :]
            fcq = fc_ref[ri, :]
            new = []
            for hh in range(2):
                dk_acc, dv_acc = carry[hh]
                c0 = FOX_D * hh
                fq = jnp.sum(jnp.where(lane == 2 * p + hh, fcq, 0.0), axis=-1, keepdims=True)
                pr = jnp.exp(_nt(qn, kmasks[hh]) + fq - fks[hh] - lse_ref[ri, c0:c0 + 1])
                if diagonal:
                    pr = jnp.where(causal, pr, 0.0)
                ds = (pr * (_nt(do, vmasks[hh]) - delta_s[ri, c0:c0 + 1])).astype(BF16)
                dq_s[hh, ri, :] += _nn(ds, kaugs[hh])
                new.append((dk_acc + _tn(jnp.where(hms[hh], qn, one), ds), dv_acc + _tn(do, pr.astype(BF16))))
            return tuple(new)

        init = tuple((jnp.zeros((LANE, tq), F32), jnp.zeros((LANE, tq), F32)) for _ in range(2))
        carry = qloop(kj, init, True)
        (dk0, dv0), (dk1, dv1) = lax.fori_loop(kj + 1, NQ, functools.partial(qloop, diagonal=False), carry)
        dks, dvs = [dk0.T, dk1.T], [dv0.T, dv1.T]

        dkn = jnp.where(lo_half, dks[0], dks[1])
        _, _, _, _, kh, _, rkk = _fox_prep(z_ref, gq_v, gk_v, rk, lo_half)
        u = dkn * gk_v
        dz_ref[rk, LANE:2 * LANE] = (rkk * (u - kh * _pair_mean(u * kh, lo_half))).astype(BF16)
        dz_ref[rk, 2 * LANE:3 * LANE] = jnp.where(lo_half, dvs[0], dvs[1]).astype(BF16)
        dgk_ref[...] += _rowsum8(dkn * kh)
        dfk_s[rk, :] = jnp.where(lane == 2 * p, -dks[0][:, FOX_D:FOX_D + 1],
                                 jnp.where(lane == 2 * p + 1, -dks[1][:, 0:1], 0.0))

        @pl.when(kj == NQ - 1)
        def _():
            def fin(i, carry):
                r = pl.ds(pl.multiple_of(i * tq, tq), tq)
                d0, d1 = dq_s[0, r, :], dq_s[1, r, :]
                dqn = jnp.where(lo_half, d0, d1)
                _, _, _, qh, _, rqq, _ = _fox_prep(z_ref, gq_v, gk_v, r, lo_half)
                u = dqn * gq_v * (FOX_D ** -0.5)
                dz_ref[r, 0:LANE] = (rqq * (u - qh * _pair_mean(u * qh, lo_half))).astype(BF16)
                dgq_ref[...] += _rowsum8(dqn * qh) * (FOX_D ** -0.5)
                dfc_ref[r, :] = dfk_s[r, :] + jnp.where(lane == 2 * p, d0[:, FOX_D:FOX_D + 1],
                                                        jnp.where(lane == 2 * p + 1, d1[:, 0:1], 0.0))
                return carry
            lax.fori_loop(0, NQ, fin, 0)

    vec = pl.BlockSpec((1, LANE), lambda b, p, k: (0, 0))
    col = pl.BlockSpec((T, LANE), lambda b, p, k: (b, p))
    part = pl.BlockSpec((8, LANE), lambda b, p, k: (b * FOX_P + p, 0))
    if nsw:
        body = _hosting(body, 8, 4, 7, nsw, _chip_swap_phases, (B, FOX_P, NQ))
    return pl.pallas_call(
        body, name="fox_bwd", grid=(B, FOX_P, NQ),
        in_specs=[pl.BlockSpec((T, 384), lambda b, p, k: (b, p)), col, col, col,
                  pl.BlockSpec((T, LANE), lambda b, p, k: (b, 0)), pl.BlockSpec((1, 8, T), lambda b, p, k: (b, 0, 0)),
                  vec, vec] + [ANY] * nsw,
        out_specs=[pl.BlockSpec((T, 384), lambda b, p, k: (b, p)), col, part, part] + [ANY] * nsw,
        out_shape=[S((N, 1536), BF16), S((N, 512), F32), S((B * FOX_P * 8, LANE), F32), S((B * FOX_P * 8, LANE), F32)]
        + [S(p.shape, p.dtype) for p in swap],
        scratch_shapes=[pltpu.VMEM((T, LANE), BF16)] * 4 + [pltpu.VMEM((T, LANE), F32), pltpu.VMEM((2, T, LANE), F32),
                                                            pltpu.VMEM((T, LANE), F32)]
        + (_chip_swap_sems(nsw) if nsw else []),
        compiler_params=_cp(("arbitrary",) * 3 if nsw else ("parallel", "parallel", "arbitrary")),
    )(z, dy, y, lse, fc, fct, gq, gk, *swap)


def _mem_scores(z_ref, kv_ref, gq, gk, h):
    c = slice(MEM_D * h, MEM_D * (h + 1))
    q, k = z_ref[:, c], kv_ref[:, c]
    rq = lax.rsqrt(jnp.mean(q * q, axis=-1, keepdims=True) + EPS)
    rk = lax.rsqrt(jnp.mean(k * k, axis=-1, keepdims=True) + EPS)
    qh, kh = q * rq, k * rk
    qn = (qh * gq * (MEM_D ** -0.5)).astype(BF16)
    kn = (kh * gk).astype(BF16)
    s = _nt(qn, kn)
    pe = jnp.exp(s - jnp.max(s, axis=-1, keepdims=True))
    pn = pe / jnp.sum(pe, axis=-1, keepdims=True)
    return pn, qn, kn, qh, kh, rq, rk


def _mem_fwd(z, memkv, gq, gk, B, T, M, tq=512):
    N = B * T
    NQ = T // tq
    W = MEM_H * MEM_D

    def body(z_ref, kv_ref, gq_ref, gk_ref, y_ref):
        for h in range(MEM_H):
            pn = _mem_scores(z_ref, kv_ref, gq_ref[...], gk_ref[...], h)[0]
            v = kv_ref[:, W + MEM_D * h:W + MEM_D * (h + 1)].astype(BF16)
            y_ref[:, MEM_D * h:MEM_D * (h + 1)] = _nn(pn.astype(BF16), v).astype(BF16)

    vec = pl.BlockSpec((1, LANE), lambda b, q: (0, 0))
    return pl.pallas_call(
        body, name="mem_fwd", grid=(B, NQ),
        in_specs=[pl.BlockSpec((tq, W), lambda b, q: (b * NQ + q, C_MQ // W)),
                  pl.BlockSpec((M, 2 * W), lambda b, q: (b, 0)), vec, vec],
        out_specs=pl.BlockSpec((tq, W), lambda b, q: (b * NQ + q, 0)), out_shape=S((N, W), BF16),
        compiler_params=_cp(("parallel", "parallel")),
    )(z, memkv, gq, gk)


def _mem_bwd(z, memkv, dy, gq, gk, B, T, M, tq=512):
    N = B * T
    NQ = T // tq
    W = MEM_H * MEM_D

    def body(z_ref, kv_ref, dy_ref, gq_ref, gk_ref, dz_ref, dkv_ref, dgq_ref, dgk_ref, acc):
        qi = pl.program_id(1)
        gq_v, gk_v = gq_ref[...], gk_ref[...]

        @pl.when(qi == 0)
        def _():
            acc[...] = jnp.zeros_like(acc)
            dgq_ref[...] = jnp.zeros_like(dgq_ref)
            dgk_ref[...] = jnp.zeros_like(dgk_ref)

        for h in range(MEM_H):
            c = slice(MEM_D * h, MEM_D * (h + 1))
            cv = slice(W + MEM_D * h, W + MEM_D * (h + 1))
            pn, qn, kn, qh, _, rq, _ = _mem_scores(z_ref, kv_ref, gq_v, gk_v, h)
            do = dy_ref[:, c].astype(BF16)
            dp = _nt(do, kv_ref[:, cv].astype(BF16))
            ds = (pn * (dp - jnp.sum(dp * pn, axis=-1, keepdims=True))).astype(BF16)
            dqn = _nn(ds, kn)
            acc[:, c] += _tn(ds, qn)
            acc[:, cv] += _tn(pn.astype(BF16), do)
            u = dqn * gq_v * (MEM_D ** -0.5)
            dz_ref[:, c] = (rq * (u - qh * jnp.mean(u * qh, axis=-1, keepdims=True))).astype(BF16)
            dgq_ref[...] += _rowsum8(dqn * qh) * (MEM_D ** -0.5)

        @pl.when(qi == NQ - 1)
        def _():
            for h in range(MEM_H):
                c = slice(MEM_D * h, MEM_D * (h + 1))
                cv = slice(W + MEM_D * h, W + MEM_D * (h + 1))
                k = kv_ref[:, c]
                rk = lax.rsqrt(jnp.mean(k * k, axis=-1, keepdims=True) + EPS)
                kh = k * rk
                dkn = acc[:, c]
                u = dkn * gk_v
                dkv_ref[:, c] = (rk * (u - kh * jnp.mean(u * kh, axis=-1, keepdims=True))).astype(BF16)
                dkv_ref[:, cv] = acc[:, cv].astype(BF16)
                dgk_ref[...] += _rowsum8(dkn * kh)

    vec = pl.BlockSpec((1, LANE), lambda b, q: (0, 0))
    part = pl.BlockSpec((8, LANE), lambda b, q: (b, 0))
    return pl.pallas_call(
        body, name="mem_bwd", grid=(B, NQ),
        in_specs=[pl.BlockSpec((tq, W), lambda b, q: (b * NQ + q, C_MQ // W)),
                  pl.BlockSpec((M, 2 * W), lambda b, q: (b, 0)), pl.BlockSpec((tq, W), lambda b, q: (b * NQ + q, 0)),
                  vec, vec],
        out_specs=[pl.BlockSpec((tq, W), lambda b, q: (b * NQ + q, 0)), pl.BlockSpec((M, 2 * W), lambda b, q: (b, 0)),
                   part, part],
        out_shape=[S((N, W), BF16), S((B * M, 2 * W), BF16), S((B * 8, LANE), F32), S((B * 8, LANE), F32)],
        scratch_shapes=[pltpu.VMEM((M, 2 * W), F32)], compiler_params=_cp(("parallel", "arbitrary")),
    )(z, memkv, dy, gq, gk)


def _merge_fwd(ya, yb, yc, z, x, wa, wb, wc, wo, tm=256):
    n, d = x.shape
    wdt = ya.shape[1]
    gb = C_GATE // d

    def body(ya_ref, yb_ref, yc_ref, g0_ref, g1_ref, g2_ref, x_ref, wa_ref, wb_ref, wc_ref, wo_ref,
             x1_ref, mg_ref, ua_ref, ub_ref, uc_ref):
        merged = jnp.zeros((tm, d), F32)
        for y_ref, g_ref, w_ref, u_ref in ((ya_ref, g0_ref, wa_ref, ua_ref), (yb_ref, g1_ref, wb_ref, ub_ref),
                                           (yc_ref, g2_ref, wc_ref, uc_ref)):
            u = _nn(y_ref[...], w_ref[...])
            u_ref[...] = u.astype(BF16)
            merged = merged + jax.nn.sigmoid(g_ref[...]) * u
        mb = merged.astype(BF16)
        mg_ref[...] = mb
        x1_ref[...] = x_ref[...] + _nn(mb, wo_ref[...])

    yt = pl.BlockSpec((tm, wdt), lambda i: (i, 0))
    xt = pl.BlockSpec((tm, d), lambda i: (i, 0))
    wbr = pl.BlockSpec((wdt, d), lambda i: (0, 0))
    gates = [pl.BlockSpec((tm, d), functools.partial(lambda i, k: (i, gb + k), k=k)) for k in range(3)]
    return pl.pallas_call(
        body, name="merge_fwd", grid=(n // tm,),
        in_specs=[yt, yt, yt] + gates + [xt, wbr, wbr, wbr, pl.BlockSpec((d, d), lambda i: (0, 0))],
        out_specs=[xt] * 5, out_shape=[S((n, d), F32)] + [S((n, d), BF16)] * 4, compiler_params=_cp(("parallel",)),
    )(ya, yb, yc, z, z, z, x, wa, wb, wc, wo)


def _merge_bwd(dx1, z, ua, ub, uc, wa, wb, wc, wo, tm=256):
    n, d = dx1.shape
    wdt = wa.shape[0]
    gb = C_GATE // d

    def body(dx_ref, g0_ref, g1_ref, g2_ref, ua_ref, ub_ref, uc_ref, wa_ref, wb_ref, wc_ref, wo_ref,
             dg_ref, dya_ref, dyb_ref, dyc_ref, dua_ref, dub_ref, duc_ref):
        dm = _nt(dx_ref[...].astype(BF16), wo_ref[...])
        for k, (g_ref, u_ref, w_ref, dy_ref, du_ref) in enumerate((
                (g0_ref, ua_ref, wa_ref, dya_ref, dua_ref), (g1_ref, ub_ref, wb_ref, dyb_ref, dub_ref),
                (g2_ref, uc_ref, wc_ref, dyc_ref, duc_ref))):
            g = jax.nn.sigmoid(g_ref[...])
            du = (dm * g).astype(BF16)
            du_ref[...] = du
            dg_ref[:, d * k:d * (k + 1)] = (dm * u_ref[...].astype(F32) * g * (1.0 - g)).astype(BF16)
            dy_ref[...] = _nt(du, w_ref[...])

    yt = pl.BlockSpec((tm, wdt), lambda i: (i, 0))
    xt = pl.BlockSpec((tm, d), lambda i: (i, 0))
    wbr = pl.BlockSpec((wdt, d), lambda i: (0, 0))
    gates = [pl.BlockSpec((tm, d), functools.partial(lambda i, k: (i, gb + k), k=k)) for k in range(3)]
    return pl.pallas_call(
        body, name="merge_bwd", grid=(n // tm,),
        in_specs=[xt] + gates + [xt, xt, xt, wbr, wbr, wbr, pl.BlockSpec((d, d), lambda i: (0, 0))],
        out_specs=[pl.BlockSpec((tm, 3 * d), lambda i: (i, 0)), yt, yt, yt, xt, xt, xt],
        out_shape=[S((n, 3 * d), BF16)] + [S((n, wdt), F32)] * 3 + [S((n, d), BF16)] * 3,
        compiler_params=_cp(("parallel",)),
    )(dx1, z, z, z, ua, ub, uc, wa, wb, wc, wo)


FFN_TN = 1408
TN_TM = 2048
INV_SQRT2 = 0.7071067811865476
INV_SQRT_2PI = 0.3989422804014327


def _conv_shifted(a, prev, first, tm):
    row = _iota(a.shape, 0)
    p7 = jnp.where(first, 0.0, prev[7:8, :])
    p6 = jnp.where(first, 0.0, prev[6:7, :])
    a1 = jnp.where(row == 0, p7, pltpu.roll(a, 1, 0))
    a2 = jnp.where(row == 0, p6, jnp.where(row == 1, p7, pltpu.roll(a, 2, 0)))
    return a1, a2


def _ffn_act_fwd(up, cw, cb, B, T, tm=256):
    N = B * T
    dff = cw.shape[1]
    NT, NJ, tn = T // tm, dff // FFN_TN, FFN_TN

    def body(a_ref, v_ref, cw_ref, cb_ref, y_ref, c_ref, carry):
        t = pl.program_id(2)
        a = a_ref[...].astype(F32)
        a1, a2 = _conv_shifted(a, carry[...], t == 0, tm)
        w = cw_ref[...]
        ac = w[0:1, :] * a2 + w[1:2, :] * a1 + w[2:3, :] * a + cb_ref[...]
        cdf = 0.5 * (1.0 + lax.erf(ac * INV_SQRT2))
        y_ref[...] = (ac * cdf * v_ref[...].astype(F32)).astype(BF16)
        c_ref[...] = cdf.astype(BF16)
        carry[...] = a[tm - 8:tm, :]

    return pl.pallas_call(
        body, name="ffn_act_fwd", grid=(B, NJ, NT),
        in_specs=[pl.BlockSpec((tm, tn), lambda b, j, t: (b * NT + t, j)),
                  pl.BlockSpec((tm, tn), lambda b, j, t: (b * NT + t, NJ + j)),
                  pl.BlockSpec((3, tn), lambda b, j, t: (0, j)), pl.BlockSpec((1, tn), lambda b, j, t: (0, j))],
        out_specs=[pl.BlockSpec((tm, tn), lambda b, j, t: (b * NT + t, j))] * 2, out_shape=[S((N, dff), BF16)] * 2,
        scratch_shapes=[pltpu.VMEM((8, tn), F32)], compiler_params=_cp(("parallel", "parallel", "arbitrary")),
    )(up, up, cw, cb)


def _ffn_down_loss(y, wd, x1, tgt, tm=256):
    n, d = x1.shape
    kf = y.shape[1]

    def body(y_ref, w_ref, x_ref, t_ref, dx_ref, ls_ref):
        err = x_ref[...] + _nn(y_ref[...], w_ref[...]) - t_ref[...]
        dx_ref[...] = err * (1.0 / d)

        @pl.when(pl.program_id(0) == 0)
        def _():
            ls_ref[...] = jnp.zeros_like(ls_ref)

        ls_ref[...] += _rowsum8(err * err) * (0.5 / d)

    xt = pl.BlockSpec((tm, d), lambda i: (i, 0))
    return pl.pallas_call(
        body, name="ffn_down_loss", grid=(n // tm,),
        in_specs=[pl.BlockSpec((tm, kf), lambda i: (i, 0)), pl.BlockSpec((kf, d), lambda i: (0, 0)), xt, xt],
        out_specs=[xt, pl.BlockSpec((8, d), lambda i: (0, 0))], out_shape=[S((n, d), F32), S((8, d), F32)],
        compiler_params=_cp(("arbitrary",)),
    )(y, wd, x1, tgt)


def _ffn_act_bwd1(dx2, wd, up, cdf, cw, cb, B, T, tm=256):
    N = B * T
    d = dx2.shape[1]
    dff = cw.shape[1]
    NT, NJ, tn = T // tm, dff // FFN_TN, FFN_TN

    def body(dx_ref, w_ref, a_ref, v_ref, c_ref, cw_ref, cb_ref, dac_ref, dv_ref, dcw_ref, dcb_ref, carry):
        b, t = pl.program_id(1), pl.program_id(2)
        a = a_ref[...].astype(F32)
        a1, a2 = _conv_shifted(a, carry[...], t == 0, tm)
        carry[...] = a[tm - 8:tm, :]
        w = cw_ref[...]
        ac = w[0:1, :] * a2 + w[1:2, :] * a1 + w[2:3, :] * a + cb_ref[...]
        dy = _nt(dx_ref[...].astype(BF16), w_ref[...])
        cdf = c_ref[...].astype(F32)
        dv_ref[...] = (dy * ac * cdf).astype(BF16)
        dac = dy * v_ref[...].astype(F32) * (cdf + ac * jnp.exp(-0.5 * ac * ac) * INV_SQRT_2PI)
        dac_ref[...] = dac

        @pl.when((b == 0) & (t == 0))
        def _():
            dcw_ref[...] = jnp.zeros_like(dcw_ref)
            dcb_ref[...] = jnp.zeros_like(dcb_ref)

        dcw_ref[0:8, :] += _rowsum8(dac * a2)
        dcw_ref[8:16, :] += _rowsum8(dac * a1)
        dcw_ref[16:24, :] += _rowsum8(dac * a)
        dcb_ref[...] += _rowsum8(dac)

    return pl.pallas_call(
        body, name="ffn_act_bwd1", grid=(NJ, B, NT),
        in_specs=[pl.BlockSpec((tm, d), lambda j, b, t: (b * NT + t, 0)), pl.BlockSpec((tn, d), lambda j, b, t: (j, 0)),
                  pl.BlockSpec((tm, tn), lambda j, b, t: (b * NT + t, j)),
                  pl.BlockSpec((tm, tn), lambda j, b, t: (b * NT + t, NJ + j)),
                  pl.BlockSpec((tm, tn), lambda j, b, t: (b * NT + t, j)),
                  pl.BlockSpec((3, tn), lambda j, b, t: (0, j)), pl.BlockSpec((1, tn), lambda j, b, t: (0, j))],
        out_specs=[pl.BlockSpec((tm, tn), lambda j, b, t: (b * NT + t, j)),
                   pl.BlockSpec((tm, tn), lambda j, b, t: (b * NT + t, j)),
                   pl.BlockSpec((24, tn), lambda j, b, t: (0, j)), pl.BlockSpec((8, tn), lambda j, b, t: (0, j))],
        out_shape=[S((N, dff), F32), S((N, dff), BF16), S((24, dff), F32), S((8, dff), F32)],
        scratch_shapes=[pltpu.VMEM((8, tn), F32)], compiler_params=_cp(("parallel", "arbitrary", "arbitrary")),
    )(dx2, wd, up, up, cdf, cw, cb)


def _ffn_act_bwd2(dac, cw, B, T, tm=256):
    N = B * T
    dff = cw.shape[1]
    NT, NJ, tn = T // tm, dff // FFN_TN, FFN_TN
    last8 = N // 8 - 1

    def body(d_ref, nx_ref, cw_ref, da_ref):
        t = pl.program_id(2)
        dd = d_ref[...]
        row = _iota(dd.shape, 0)
        last = t == NT - 1
        n0 = jnp.where(last, 0.0, nx_ref[0:1, :])
        n1 = jnp.where(last, 0.0, nx_ref[1:2, :])
        d1 = jnp.where(row == tm - 1, n0, pltpu.roll(dd, tm - 1, 0))
        d2 = jnp.where(row == tm - 1, n1, jnp.where(row == tm - 2, n0, pltpu.roll(dd, tm - 2, 0)))
        w = cw_ref[...]
        da_ref[...] = (w[2:3, :] * dd + w[1:2, :] * d1 + w[0:1, :] * d2).astype(BF16)

    return pl.pallas_call(
        body, name="ffn_act_bwd2", grid=(B, NJ, NT),
        in_specs=[pl.BlockSpec((tm, tn), lambda b, j, t: (b * NT + t, j)),
                  pl.BlockSpec((8, tn), lambda b, j, t: (jnp.minimum((b * NT + t + 1) * (tm // 8), last8), j)),
                  pl.BlockSpec((3, tn), lambda b, j, t: (0, j))],
        out_specs=pl.BlockSpec((tm, tn), lambda b, j, t: (b * NT + t, j)), out_shape=S((N, dff), BF16),
        compiler_params=_cp(("parallel", "parallel", "parallel")),
    )(dac, dac, cw)


def _fold_rows(p, name):
    r, c = p.shape[0] // 8, p.shape[1]

    def body(p_ref, o_ref):
        for j in range(r):
            o_ref[j:j + 1, :] = jnp.sum(p_ref[8 * j:8 * (j + 1), :], axis=0, keepdims=True)

    return pl.pallas_call(body, name=name, out_shape=S((r, c), F32), compiler_params=_cp())(p)


def _small_reduce(lbl, dg_mix, dg_mem, dlb_p, dgn_p, dfb_p, dgq_p, dgk_p, dmq_p, dmk_p, dg_ffn, dcb_p, loss_p):
    d, dff = dg_mix.shape[1], dcb_p.shape[1]
    nbh = dlb_p.shape[0] // (8 * HG_H)

    def colsum(ref):
        return jnp.sum(ref[...], axis=0, keepdims=True)

    def body(lbl_ref, mix_ref, mem_ref, dlb_ref, dgn_ref, dfb_ref, dgq_ref, dgk_ref, dmq_ref, dmk_ref, ffn_ref, dcb_ref,
             ls_ref, o_mix, o_mem, o_lb, o_hgn, o_fb, o_fq, o_fk, o_mq, o_mk, o_ffn, o_cb, o_loss):
        o_mix[...], o_mem[...], o_ffn[...], o_cb[...] = colsum(mix_ref), colsum(mem_ref), colsum(ffn_ref), colsum(dcb_ref)
        o_hgn[...], o_fb[...], o_mq[...], o_mk[...] = colsum(dgn_ref), colsum(dfb_ref), colsum(dmq_ref), colsum(dmk_ref)
        for src, dst in ((dgq_ref, o_fq), (dgk_ref, o_fk)):
            v = colsum(src)
            dst[...] = v + pltpu.roll(v, FOX_D, 1)
        o_loss[...] = jnp.zeros((1, LANE), F32) + jnp.sum(colsum(ls_ref), axis=-1, keepdims=True)
        logits = lbl_ref[...]
        e = jnp.exp(logits - jnp.max(logits, axis=0, keepdims=True))
        pr = e / jnp.sum(e, axis=0, keepdims=True)
        rows = _iota((8, LANE), 0)
        for h in range(HG_H):
            acc = jnp.zeros((8, LANE), F32)
            for b in range(nbh):
                acc = acc + dlb_ref[8 * (b * HG_H + h):8 * (b * HG_H + h + 1), :]
            dlb = jnp.sum(acc, axis=0, keepdims=True)
            c = slice(LANE * h, LANE * (h + 1))
            p0 = pr[0:1, c]
            first = _iota((logits.shape[0], LANE), 0) == 0
            o_lb[:, c] = pr[:, c] * (jnp.where(first, 1.0, 0.0) - p0) * dlb

    outs = [S((1, d), F32), S((1, d), F32), S(lbl.shape, F32)] + [S((1, LANE), F32)] * 6 + \
           [S((1, d), F32), S((1, dff), F32), S((1, LANE), F32)]
    return pl.pallas_call(body, name="small_reduce", out_shape=outs, compiler_params=_cp())(
        lbl, dg_mix, dg_mem, dlb_p, dgn_p, dfb_p, dgq_p, dgk_p, dmq_p, dmk_p, dg_ffn, dcb_p, loss_p)


def _in_col_pieces():
    hw, fw = HG_H * HG_D, FOX_H * FOX_D
    fox0, ff0 = 4 * hw, 4 * hw + 3 * fw
    mq0 = ff0 + FOX_H
    gate0 = mq0 + MEM_H * MEM_D
    pieces = []
    for p in range(FOX_P):
        pieces += [(fox0 + j * fw + LANE * p, LANE) for j in range(3)]
    pieces.append((mq0, MEM_H * MEM_D))
    for h in range(HG_H):
        pieces += [(j * hw + HG_D * h, HG_D) for j in range(4)]
    pieces.append((gate0, C_FF - C_GATE))
    pieces.append((ff0, FOX_H))
    return pieces


def _perm_from_blocks(blocks):
    n_blk, _, c = blocks.shape
    parts = []
    for s, n in _in_col_pieces():
        lo = s
        while lo < s + n:
            d = lo // c
            hi = min(s + n, (d + 1) * c)
            parts.append(blocks[d][:, lo - d * c:hi - d * c])
            lo = hi
    parts.append(jnp.zeros((blocks.shape[1], C_END - C_FF - FOX_H), blocks.dtype))
    return jnp.concatenate(parts, axis=1)


def _unperm_blocks(segs, n_blk):
    starts = [0]
    for a in segs:
        starts.append(starts[-1] + a.shape[1])
    new_start, placed = 0, []
    for s, n in _in_col_pieces():
        placed.append((s, new_start, n))
        new_start += n
    placed.sort()
    c = sum(n for _, _, n in placed) // n_blk
    blocks = []
    for d in range(n_blk):
        parts = []
        for s, ns, n in placed:
            lo, hi = max(s, d * c), min(s + n, (d + 1) * c)
            if lo < hi:
                i = max(j for j in range(len(segs)) if starts[j] <= ns)
                parts.append(segs[i][:, ns + lo - s - starts[i]:ns + hi - s - starts[i]])
        blocks.append(jnp.concatenate(parts, axis=1))
    return jnp.stack(blocks)


def _local_step(x2, mem2, tgt, sm, W, B, T, M, ex=None):
    fbias = jnp.pad(sm["fox_f_bias"], ((0, 0), (0, LANE - FOX_H)))
    gq2 = jnp.concatenate([sm["fox_q_norm_g"]] * 2, axis=1)
    gk2 = jnp.concatenate([sm["fox_k_norm_g"]] * 2, axis=1)
    lbl = sm["hgrn_lb_logits"]
    h = _rmsnorm_cast(x2, sm["norm_mix_g"], "norm_mix")
    z = _mm_nn(h, W["w_in"], F32, "proj_in", 512, 2432)
    memn = _rmsnorm_cast(mem2, sm["norm_mem_g"], "norm_mem", tm=256)
    memkv = _mm_nn(memn, W["mem_kv_w"], F32, "proj_memkv", 256, 512)
    ya, o_raw, states = _hgrn_fwd(z, lbl, sm["hgrn_norm_g"], B, T)
    fc, fct = _fox_gate_fwd(z, fbias, B, T)
    yb, lse, *late = _fox_fwd(z, fc, fct, gq2, gk2, B, T, gather=ex.late_blocks() if ex else ())
    if ex:
        W = {**W, **ex.unpack_late(late)}
    yc = _mem_fwd(z, memkv, sm["mem_q_norm_g"], sm["mem_k_norm_g"], B, T, M)
    x1, merged, ua, ub, uc = _merge_fwd(ya, yb, yc, z, x2, W["w_br_hgrn"], W["w_br_fox"], W["w_br_mem"], W["w_out"])
    h2 = _rmsnorm_cast(x1, sm["norm_ffn_g"], "norm_ffn")
    up = _mm_nn(h2, W["ffn_w_up"], BF16, "ffn_up", 512, FFN_TN)
    yf, cdf = _ffn_act_fwd(up, W["ffn_conv_w"], sm["ffn_conv_b"], B, T)
    dx2, loss_p = _ffn_down_loss(yf, W["ffn_w_down"], x1, tgt)
    dff = W["ffn_conv_w"].shape[1]
    dac, dv, dcw_p, dcb_p = _ffn_act_bwd1(dx2, W["ffn_w_down"], up, cdf, W["ffn_conv_w"], sm["ffn_conv_b"], B, T)
    da = _ffn_act_bwd2(dac, W["ffn_conv_w"], B, T)
    g = {"ffn_conv_w": _fold_rows(dcw_p, "g_conv_w")}
    g["ffn_w_down"] = _mm_tn(yf, dx2, "g_w_down", TN_TM, 512)
    dh2 = _mm_nt_sum([(da, 0, dff, 0), (dv, 0, dff, dff)], W["ffn_w_up"], "dh2", 256)
    g["ffn_w_up"] = [_mm_tn(h2, da, "g_w_up_a", TN_TM, FFN_TN), _mm_tn(h2, dv, "g_w_up_v", TN_TM, FFN_TN)]
    dx1, dg_ffn = _rmsnorm_bwd(dh2, x1, sm["norm_ffn_g"], dx2, "norm_ffn_bwd")
    g["w_out"] = _mm_tn(merged, dx1, "g_w_out", TN_TM, 512)
    dgate, dya, dyb, dyc, dua, dub, duc = _merge_bwd(dx1, z, ua, ub, uc, W["w_br_hgrn"], W["w_br_fox"], W["w_br_mem"],
                                                    W["w_out"])
    g["w_br_hgrn"] = _mm_tn(ya, dua, "g_w_br_hgrn", TN_TM, 512)
    g["w_br_fox"] = _mm_tn(yb, dub, "g_w_br_fox", TN_TM, 512)
    g["w_br_mem"] = _mm_tn(yc, duc, "g_w_br_mem", TN_TM, 512)
    early_pk = ex.early_grads(g) if ex else ()
    dz_hg, dlb_p, dgn_p, *early_sib = _hgrn_bwd(z, o_raw, states, dya, lbl, sm["hgrn_norm_g"], B, T,
                                                swap_sibling=early_pk)
    dz_fox, dfc, dgq_p, dgk_p, *early_chips = _fox_bwd(z, dyb, yb, lse, fc, fct, gq2, gk2, B, T,
                                                       swap=ex.pair_sums(early_pk, early_sib, "early") if ex else ())
    dz_ff, dfb_p = _fox_gate_bwd(dfc, z, fbias, B, T)
    dz_mq, dkv, dmq_p, dmk_p = _mem_bwd(z, memkv, dyc, sm["mem_q_norm_g"], sm["mem_k_norm_g"], B, T, M)
    g["mem_kv_w"] = _mm_tn(memn, dkv, "g_mem_kv_w", 256, 512)
    dmemn = _mm_nt(dkv, W["mem_kv_w"], "d_memn", 256, 512)
    _, dg_mem = _rmsnorm_bwd(dmemn, mem2, sm["norm_mem_g"], None, "norm_mem_bwd", tm=256)
    d = x2.shape[1]
    parts = [(dz_fox, 0, C_MQ - C_FOX, C_FOX), (dz_mq, 0, C_HG - C_MQ, C_MQ), (dz_hg, 0, C_GATE - C_HG, C_HG)]
    parts += [(dgate, d * k, d, C_GATE + d * k) for k in range(3)] + [(dz_ff, 0, C_END - C_FF, C_FF)]
    g["w_in"] = [_mm_tn(h, dzs, "g_w_in_%d" % i, TN_TM, min(512, dzs.shape[1]))
                 for i, dzs in enumerate((dz_fox, dz_mq, dz_hg, dgate, dz_ff))]
    sums = None
    if ex:
        last_pk = ex.last_grads(g)
        last_sib = _swap_with_sibling(last_pk, "rs_sibling_last")
        dh, last_chips = _mm_nt_sum(parts, W["w_in"], "dh", 256, swap=ex.pair_sums(last_pk, last_sib, "last"))
        sums = (ex.final_sums(early_pk, early_sib, early_chips, "early"),
                ex.final_sums(last_pk, last_sib, last_chips, "last"))
    else:
        dh = _mm_nt_sum(parts, W["w_in"], "dh", 256)
    grad_x, dg_mix = _rmsnorm_bwd(dh, x2, sm["norm_mix_g"], dx1, "norm_mix_bwd")
    small = _small_reduce(lbl, dg_mix, dg_mem, dlb_p, dgn_p, dfb_p, dgq_p, dgk_p, dmq_p, dmk_p, dg_ffn, dcb_p, loss_p)
    names = ("norm_mix_g", "norm_mem_g", "hgrn_lb_logits", "hgrn_norm_g", "fox_f_bias", "fox_q_norm_g", "fox_k_norm_g",
             "mem_q_norm_g", "mem_k_norm_g", "norm_ffn_g", "ffn_conv_b", "loss")
    g.update(dict(zip(names, small)))
    return grad_x, g, sums


ANY = pl.BlockSpec(memory_space=pl.ANY)


def _position():
    return lax.axis_index("x"), lax.axis_index("y"), lax.axis_index("c")


def _all_gather(blocks, name):
    nb = len(blocks)

    def body(*refs):
        start, forward, finish = _gather_phases(refs[:nb], refs[nb:2 * nb], *refs[2 * nb:])
        start()
        forward()
        finish()

    return pl.pallas_call(
        body, name=name, out_shape=_gather_shapes(blocks), in_specs=[ANY] * nb, out_specs=[ANY] * nb,
        scratch_shapes=_gather_sems(nb),
    )(*blocks)


def _hosting(body, n_in, n_out, n_scratch, n_x, make_phases, grid):
    n_steps = math.prod(grid)

    def hosted(*refs):
        a = n_in + n_x
        b = a + n_out + n_x
        ins, xs = refs[:n_in], refs[n_in:a]
        outs, x_outs = refs[a:a + n_out], refs[a + n_out:b]
        scratch, sems = refs[b:b + n_scratch], refs[b + n_scratch:]
        step = 0
        for ax, n in enumerate(grid):
            step = step * n + pl.program_id(ax)
        phases = make_phases(xs, x_outs, *sems)
        pl.when(step == 0)(phases[0])
        for ph in phases[1:-1]:
            pl.when(step == n_steps // 2)(ph)
        body(*ins, *outs, *scratch)
        pl.when(step == n_steps - 1)(phases[-1])

    return hosted


def _gather_shapes(blocks):
    return [S((N_DEV,) + b.shape, b.dtype) for b in blocks]


def _gather_sems(nb):
    return [pltpu.SemaphoreType.DMA((7 * nb,)), pltpu.SemaphoreType.DMA((7 * nb,)), pltpu.SemaphoreType.DMA((nb,))]


def _gather_phases(x_refs, out_refs, send_sems, recv_sems, local_sems):
    nb = len(x_refs)
    x, y, c = _position()
    me, sibling = (x, y, c), (x, y, 1 - c)
    chips = [(1 - x, y), (x, 1 - y), (1 - x, 1 - y)]

    def copy(i, k, blk, to, own=False):
        px, py, pc = blk
        slot = out_refs[i].at[4 * px + 2 * py + pc]
        return pltpu.make_async_remote_copy(
            src_ref=x_refs[i] if own else slot, dst_ref=slot, send_sem=send_sems.at[7 * i + k],
            recv_sem=recv_sems.at[7 * i + k], device_id=to, device_id_type=MESH)

    def mine(i):
        return pltpu.make_async_copy(x_refs[i], out_refs[i].at[4 * x + 2 * y + c], local_sems.at[i])

    def first(i):
        return [copy(i, 0, me, sibling, own=True)] + [copy(i, 1 + j, me, (*chip, c), own=True)
                                                     for j, chip in enumerate(chips)]

    def passed(i, j):
        return copy(i, 4 + j, (*chips[j], c), sibling)

    def start():
        for i in range(nb):
            mine(i).start()
            for cp in first(i):
                cp.start()

    def forward():
        for i in range(nb):
            for j, chip in enumerate(chips):
                copy(i, 1 + j, (*chip, c), me).wait_recv()
                passed(i, j).start()

    def finish():
        for i in range(nb):
            copy(i, 0, sibling, me).wait_recv()
            for j, chip in enumerate(chips):
                copy(i, 4 + j, (*chip, 1 - c), me).wait_recv()
        for i in range(nb):
            for cp in first(i) + [passed(i, j) for j in range(3)]:
                cp.wait_send()
            mine(i).wait()

    return start, forward, finish


def _swap_with_sibling(pks, name):
    nb = len(pks)

    def body(*refs):
        start, finish = _sibling_swap_phases(refs[:nb], refs[nb:2 * nb], *refs[2 * nb:])
        start()
        finish()

    return pl.pallas_call(
        body, name=name, out_shape=_sibling_swap_shapes(pks), in_specs=[ANY] * nb, out_specs=[ANY] * nb,
        scratch_shapes=_sibling_swap_sems(nb),
    )(*pks)


def _sibling_swap_shapes(pks):
    return [S((4,) + p.shape[1:], p.dtype) for p in pks]


def _sibling_swap_sems(nb):
    return [pltpu.SemaphoreType.DMA((4 * nb,)), pltpu.SemaphoreType.DMA((4 * nb,))]


def _sibling_swap_phases(pk_refs, out_refs, send_sems, recv_sems):
    nb = len(pk_refs)
    x, y, c = _position()

    def copies():
        return [pltpu.make_async_remote_copy(
            src_ref=pk_refs[i].at[2 * k + 1 - c], dst_ref=out_refs[i].at[k], send_sem=send_sems.at[4 * i + k],
            recv_sem=recv_sems.at[4 * i + k], device_id=(x, y, 1 - c), device_id_type=MESH)
            for i in range(nb) for k in range(4)]

    def start():
        for cp in copies():
            cp.start()

    def finish():
        for cp in copies():
            cp.wait()

    return start, finish


def _swap_between_chips(pbs, name):
    nb = len(pbs)

    def body(*refs):
        start, finish = _chip_swap_phases(refs[:nb], refs[nb:2 * nb], *refs[2 * nb:])
        start()
        finish()

    return pl.pallas_call(
        body, name=name, out_shape=[S(p.shape, p.dtype) for p in pbs], in_specs=[ANY] * nb, out_specs=[ANY] * nb,
        scratch_shapes=_chip_swap_sems(nb),
    )(*pbs)


def _chip_swap_sems(nb):
    return [pltpu.SemaphoreType.DMA((3 * nb,)), pltpu.SemaphoreType.DMA((3 * nb,)), pltpu.SemaphoreType.DMA((nb,))]


def _chip_swap_phases(pb_refs, out_refs, send_sems, recv_sems, local_sems):
    nb = len(pb_refs)
    x, y, c = _position()
    me = 2 * x + y
    chips = [(1 - x, y), (x, 1 - y), (1 - x, 1 - y)]

    def local(i):
        return pltpu.make_async_copy(pb_refs[i].at[me], out_refs[i].at[me], local_sems.at[i])

    def send(i, j):
        cx, cy = chips[j]
        return pltpu.make_async_remote_copy(
            src_ref=pb_refs[i].at[2 * cx + cy], dst_ref=out_refs[i].at[me], send_sem=send_sems.at[3 * i + j],
            recv_sem=recv_sems.at[3 * i + j], device_id=(cx, cy, c), device_id_type=MESH)

    def arrival(i, j):
        cx, cy = chips[j]
        return pltpu.make_async_remote_copy(
            src_ref=pb_refs[i].at[me], dst_ref=out_refs[i].at[2 * cx + cy], send_sem=send_sems.at[3 * i + j],
            recv_sem=recv_sems.at[3 * i + j], device_id=(cx, cy, c), device_id_type=MESH)

    def start():
        for i in range(nb):
            local(i).start()
            for j in range(3):
                send(i, j).start()

    def finish():
        for i in range(nb):
            for j in range(3):
                arrival(i, j).wait_recv()
        for i in range(nb):
            for j in range(3):
                send(i, j).wait_send()
            local(i).wait()

    return start, finish


def _row_tile(r):
    return max(t for t in range(16, min(r, 512) + 1, 16) if r % t == 0)


def _pair_sum_cast(pk, recv, core, name):
    _, r, l = pk.shape
    tr = _row_tile(r)

    def body(c_ref, a_ref, b_ref, o_ref):
        o_ref[...] = (a_ref[...] + b_ref[...]).astype(BF16)

    return pl.pallas_call(
        body, name=name,
        grid_spec=pltpu.PrefetchScalarGridSpec(
            num_scalar_prefetch=1, grid=(4, r // tr),
            in_specs=[pl.BlockSpec((None, tr, l), lambda k, i, c: (2 * k + c[0], i, 0)),
                      pl.BlockSpec((None, tr, l), lambda k, i, c: (k, i, 0))],
            out_specs=pl.BlockSpec((None, tr, l), lambda k, i, c: (k, i, 0))),
        out_shape=S((4, r, l), BF16), compiler_params=_cp(("parallel", "parallel")),
    )(core, pk, recv)


def _final_sum(pk, recv_sib, recv_chips, slot, chip, name):
    _, r, l = pk.shape
    tr = _row_tile(r)

    def body(s_ref, k_ref, a_ref, b_ref, rc_ref, o_ref):
        base = a_ref[...] + b_ref[...]
        acc = jnp.zeros_like(base)
        for j in range(4):
            acc = acc + jnp.where(k_ref[0] == j, base, rc_ref[j].astype(F32))
        o_ref[...] = acc

    return pl.pallas_call(
        body, name=name,
        grid_spec=pltpu.PrefetchScalarGridSpec(
            num_scalar_prefetch=2, grid=(r // tr,),
            in_specs=[pl.BlockSpec((None, tr, l), lambda i, s, k: (s[0], i, 0)),
                      pl.BlockSpec((None, tr, l), lambda i, s, k: (k[0], i, 0)),
                      pl.BlockSpec((4, tr, l), lambda i, s, k: (0, i, 0))],
            out_specs=pl.BlockSpec((tr, l), lambda i, s, k: (i, 0))),
        out_shape=S((r, l), F32), compiler_params=_cp(("parallel",)),
    )(slot, chip, pk, recv_sib, recv_chips)


def _adamw_math(w, g, m, v):
    m = ADAM_B1 * m + (1.0 - ADAM_B1) * g
    v = ADAM_B2 * v + (1.0 - ADAM_B2) * (g * g)
    m_hat = m / (1.0 - ADAM_B1 ** ADAM_STEP)
    v_hat = v / (1.0 - ADAM_B2 ** ADAM_STEP)
    return -ADAM_LR * (m_hat / (jnp.sqrt(v_hat) + ADAM_EPS) + ADAM_WD * w), m, v


def _adamw(w, g, m, v, name):
    r, c = w.shape
    tr = 256 if r % 256 == 0 else r

    def body(w_ref, g_ref, m_ref, v_ref, d_ref, nm_ref, nv_ref):
        d_ref[...], nm_ref[...], nv_ref[...] = _adamw_math(w_ref[...], g_ref[...], m_ref[...], v_ref[...])

    tile = pl.BlockSpec((tr, c), lambda i: (i, 0))
    return pl.pallas_call(
        body, name=name, grid=(r // tr,), in_specs=[tile] * 4, out_specs=[tile] * 3, out_shape=[S((r, c), F32)] * 3,
        compiler_params=_cp(("parallel",)),
    )(w, g, m, v)


def _small_update(gathered, w, m, v):
    def body(ga_ref, w_ref, m_ref, v_ref, g_ref, d_ref, nm_ref, nv_ref):
        g = ga_ref[0]
        for k in range(1, N_DEV):
            g = g + ga_ref[k]
        g_ref[...] = g
        d_ref[...], nm_ref[...], nv_ref[...] = _adamw_math(w_ref[...], g, m_ref[...], v_ref[...])

    return pl.pallas_call(body, name="small_update", out_shape=[S(w.shape, F32)] * 4, compiler_params=_cp())(
        gathered, w, m, v)


BIG = ("w_in", "mem_kv_w", "w_br_hgrn", "w_br_fox", "w_br_mem", "w_out", "ffn_w_up", "ffn_conv_w", "ffn_w_down")
GROUP_ROWS = ("w_out", "ffn_w_down")
GROUP_LANE = ("w_br_hgrn", "w_br_fox", "w_br_mem")
LANE_GROUP_ROWS = 224
SMALL = ("norm_mix_g", "norm_mem_g", "hgrn_lb_logits", "hgrn_norm_g", "fox_f_bias", "fox_q_norm_g", "fox_k_norm_g",
         "mem_q_norm_g", "mem_k_norm_g", "norm_ffn_g", "ffn_conv_b")


def _rows_of(n_elems):
    return -(-n_elems // LANE)


def _to_rows(a, lead=0):
    flat = a.reshape(a.shape[:lead] + (-1,))
    pad = (-flat.shape[-1]) % LANE
    if pad:
        flat = jnp.pad(flat, [(0, 0)] * lead + [(0, pad)])
    return flat.reshape(a.shape[:lead] + (-1, LANE))


def _stack_rows(parts, lead, total_rows):
    buf = jnp.concatenate(parts, axis=lead)
    pad = total_rows - buf.shape[lead]
    return jnp.pad(buf, [(0, 0)] * lead + [(0, pad), (0, 0)])


def _round_up(n, k):
    return -(-n // k) * k


def _from_rows(rows, shape, lead=0):
    n = math.prod(shape)
    return rows.reshape(rows.shape[:lead] + (-1,))[..., :n].reshape(rows.shape[:lead] + tuple(shape))


def _blocks_to_full(blocks, kind):
    n, a, b = blocks.shape
    return blocks.transpose(1, 0, 2).reshape(a, n * b) if kind == "col" else blocks.reshape(n * a, b)


def _full_to_blocks(full, kind, n=N_DEV):
    a, b = full.shape
    return full.reshape(a, n, b // n).transpose(1, 0, 2) if kind == "col" else full.reshape(n, a // n, b)


def _lane_group_rows(shard):
    n_lane = sum(shard[n].shape[0] for n in GROUP_LANE)
    n_cw = shard["ffn_conv_w"].size
    return n_lane, _rows_of(3 * n_cw), _rows_of(n_cw), _round_up(n_lane + _rows_of(3 * n_cw), LANE_GROUP_ROWS)


def _split_bf16x3(x):
    hi = x.astype(BF16)
    r1 = x - hi.astype(F32)
    mid = r1.astype(BF16)
    return jnp.stack([hi, mid, (r1 - mid.astype(F32)).astype(BF16)])


class _Exchange:
    def __init__(self, shard):
        self.shard = shard
        xi, yi, ci = _position()
        self.core = ci.astype(jnp.int32).reshape(1)
        self.chip = (2 * xi + yi).astype(jnp.int32).reshape(1)
        self.n_lane, self.r_pieces, self.r_vals, self.r_lane = _lane_group_rows(shard)

    def first_blocks(self):
        return [self.shard["w_in"].astype(BF16), self.shard["mem_kv_w"].astype(BF16)]

    def unpack_first(self, gathered):
        return {"w_in": _perm_from_blocks(gathered[0]), "mem_kv_w": _blocks_to_full(gathered[1], "row")}

    def late_blocks(self):
        sh = self.shard
        lane_rows = [sh[n].astype(BF16) for n in GROUP_LANE] + [_to_rows(_split_bf16x3(sh["ffn_conv_w"]))]
        return [sh[n].astype(BF16) for n in GROUP_ROWS] + [sh["ffn_w_up"].astype(BF16),
                                                           _stack_rows(lane_rows, 0, self.r_lane)]

    def unpack_late(self, gathered):
        *rows, gc, gd = gathered
        sh = self.shard
        W = {"ffn_w_up": _blocks_to_full(gc, "col")}
        for n, blocks in zip(GROUP_ROWS, rows):
            W[n] = _blocks_to_full(blocks, "row")
        r0 = 0
        for n in GROUP_LANE:
            W[n] = _blocks_to_full(gd[:, r0:r0 + sh[n].shape[0]], "col")
            r0 += sh[n].shape[0]
        cw = _from_rows(gd[:, self.n_lane:self.n_lane + self.r_pieces], (3,) + sh["ffn_conv_w"].shape, lead=1).astype(F32)
        W["ffn_conv_w"] = _blocks_to_full(cw[:, 0] + cw[:, 1] + cw[:, 2], "col")
        return W

    def early_grads(self, g):
        cw_rows = _to_rows(_full_to_blocks(g["ffn_conv_w"], "col"), lead=1)
        return [_full_to_blocks(g[n], "row") for n in GROUP_ROWS] + [
            jnp.concatenate([_full_to_blocks(h, "col", N_DEV // 2) for h in g["ffn_w_up"]], axis=0),
            _stack_rows([_full_to_blocks(g[n], "col") for n in GROUP_LANE] + [cw_rows], 1, self.r_lane)]

    def last_grads(self, g):
        return [_unperm_blocks(g["w_in"], N_DEV), _full_to_blocks(g["mem_kv_w"], "row")]

    def pair_sums(self, pks, recv_sib, tag):
        return [_pair_sum_cast(p, r, self.core, "rs_pair_sum_%s%d" % (tag, i))
                for i, (p, r) in enumerate(zip(pks, recv_sib))]

    def final_sums(self, pks, recv_sib, recv_chips, tag):
        return [_final_sum(p, rs, rc, 2 * self.chip + self.core, self.chip, "rs_final_sum_%s%d" % (tag, i))
                for i, (p, rs, rc) in enumerate(zip(pks, recv_sib, recv_chips))]

    def unpack_grads(self, early, last):
        sh = self.shard
        *rows, g_up, g_lane = early
        g_shard = {"w_in": last[0], "mem_kv_w": last[1], "ffn_w_up": g_up, **dict(zip(GROUP_ROWS, rows))}
        r0 = 0
        for n in GROUP_LANE:
            g_shard[n] = g_lane[r0:r0 + sh[n].shape[0]]
            r0 += sh[n].shape[0]
        g_shard["ffn_conv_w"] = _from_rows(g_lane[self.n_lane:self.n_lane + self.r_vals], sh["ffn_conv_w"].shape)
        return g_shard


def kernel(x, mem, norm_mix_g, norm_mem_g, w_in, hgrn_lb_logits, hgrn_norm_g, fox_f_bias, fox_q_norm_g, fox_k_norm_g, mem_kv_w, mem_q_norm_g, mem_k_norm_g, w_br_hgrn, w_br_fox, w_br_mem, w_out, norm_ffn_g, ffn_w_up, ffn_conv_w, ffn_conv_b, ffn_w_down, loss_target, m_norm_mix_g, m_norm_mem_g, m_w_in, m_hgrn_lb_logits, m_hgrn_norm_g, m_fox_f_bias, m_fox_q_norm_g, m_fox_k_norm_g, m_mem_kv_w, m_mem_q_norm_g, m_mem_k_norm_g, m_w_br_hgrn, m_w_br_fox, m_w_br_mem, m_w_out, m_norm_ffn_g, m_ffn_w_up, m_ffn_conv_w, m_ffn_conv_b, m_ffn_w_down, v_norm_mix_g, v_norm_mem_g, v_w_in, v_hgrn_lb_logits, v_hgrn_norm_g, v_fox_f_bias, v_fox_q_norm_g, v_fox_k_norm_g, v_mem_kv_w, v_mem_q_norm_g, v_mem_k_norm_g, v_w_br_hgrn, v_w_br_fox, v_w_br_mem, v_w_out, v_norm_ffn_g, v_ffn_w_up, v_ffn_conv_w, v_ffn_conv_b, v_ffn_w_down):
    given = dict(locals())
    order = ("norm_mix_g", "norm_mem_g", "w_in", "hgrn_lb_logits", "hgrn_norm_g", "fox_f_bias", "fox_q_norm_g",
             "fox_k_norm_g", "mem_kv_w", "mem_q_norm_g", "mem_k_norm_g", "w_br_hgrn", "w_br_fox", "w_br_mem", "w_out",
             "norm_ffn_g", "ffn_w_up", "ffn_conv_w", "ffn_conv_b", "ffn_w_down")
    B, T, D = x.shape
    M = mem.shape[1]
    shard = {n: given[n][0] if n in BIG else given[n] for n in order}
    mom = {n: (given["m_" + n][0], given["v_" + n][0]) if n in BIG else (given["m_" + n], given["v_" + n])
           for n in order}
    shard["hgrn_lb_logits"] = hgrn_lb_logits
    for n in ("norm_mix_g", "norm_mem_g", "hgrn_norm_g", "fox_f_bias", "fox_q_norm_g", "fox_k_norm_g", "mem_q_norm_g",
              "mem_k_norm_g", "norm_ffn_g", "ffn_conv_b"):
        shard[n] = given[n].reshape(1, -1)

    ex = _Exchange(shard)
    W = ex.unpack_first(_all_gather(ex.first_blocks(), "ag_first"))

    sm = {n: shard[n] for n in SMALL}
    grad_x, g, sums = _local_step(x.reshape(B * T, D), mem.reshape(B * M, D), loss_target.reshape(B * T, D), sm, W,
                                  B, T, M, ex)
    g_shard = ex.unpack_grads(*sums)

    sg = {n: g[n] for n in SMALL}
    sg["fox_f_bias"] = g["fox_f_bias"][:, :FOX_H]
    sg["fox_q_norm_g"] = g["fox_q_norm_g"][:, :FOX_D]
    sg["fox_k_norm_g"] = g["fox_k_norm_g"][:, :FOX_D]
    slayout, row0 = {}, 0
    for n in SMALL:
        nr = _rows_of(shard[n].size)
        slayout[n] = (row0, nr)
        row0 += nr
    loss_row = row0
    r_small = _round_up(row0 + 1, 8)

    def pack_small(d, with_loss=None):
        rows = [_to_rows(d[n]) for n in SMALL]
        rows.append(with_loss if with_loss is not None else jnp.zeros((1, LANE), F32))
        return _stack_rows(rows, 0, r_small)

    sgath, = _all_gather([pack_small(sg, g["loss"])], "ag_small")
    s_g, s_d, s_m, s_v = _small_update(sgath, pack_small(shard), pack_small({n: mom[n][0].reshape(shard[n].shape) for n in SMALL}),
                                       pack_small({n: mom[n][1].reshape(shard[n].shape) for n in SMALL}))
    loss = s_g[loss_row, 0]

    grads, deltas, new_m, new_v = {}, {}, {}, {}
    for n in BIG:
        gn = g_shard[n]
        d, nm, nv = _adamw(shard[n], gn, mom[n][0], mom[n][1], "adamw_" + n)
        grads[n], deltas[n], new_m[n], new_v[n] = (a[None] for a in (gn, d, nm, nv))
    for n in SMALL:
        r0, nr = slayout[n]
        for dst, src in ((grads, s_g), (deltas, s_d), (new_m, s_m), (new_v, s_v)):
            dst[n] = _from_rows(src[r0:r0 + nr], given[n].shape)
    return (loss, grad_x.reshape(B, T, D), *[grads[n] for n in order], *[deltas[n] for n in order],
            *[new_m[n] for n in order], *[new_v[n] for n in order])
```

```python
import functools
import math

import jax
import jax.numpy as jnp
from jax import lax
from jax.experimental import pallas as pl
from jax.experimental.pallas import tpu as pltpu

F32, BF16 = jnp.float32, jnp.bfloat16
S = jax.ShapeDtypeStruct
MESH = pl.DeviceIdType.MESH

N_DEV = 8
EPS = 1e-6
LANE = 128
CHUNK = 64
SUB = 16
HG_H, HG_D = 4, 128
HG_GROUP = 2
FOX_H, FOX_D = 8, 64
FOX_P = FOX_H // 2
MEM_H, MEM_D = 4, 128
NEG = -1e30
VMEM_LIMIT = 56 * 2**20

ADAM_LR, ADAM_B1, ADAM_B2, ADAM_EPS, ADAM_WD, ADAM_STEP = 0.001, 0.9, 0.999, 1e-08, 0.01, 10

C_FOX, C_MQ, C_HG, C_GATE, C_FF, C_END = 0, 1536, 2048, 4096, 7168, 7296


def _cp(sem=None):
    return pltpu.CompilerParams(dimension_semantics=sem, vmem_limit_bytes=VMEM_LIMIT)


def _dot(a, b, dims, prec=None):
    return lax.dot_general(a, b, (dims, ((), ())), preferred_element_type=F32, precision=prec)


def _nn(a, b, prec=None):
    return _dot(a, b, ((1,), (0,)), prec)


def _nt(a, b, prec=None):
    return _dot(a, b, ((1,), (1,)), prec)


def _tn(a, b, prec=None):
    return _dot(a, b, ((0,), (0,)), prec)


def _b(x):
    return x.astype(BF16)


def _mm3(fn, a, b):
    ah, bh = _b(a), _b(b)
    return fn(ah, bh) + fn(ah, _b(b - bh.astype(F32))) + fn(_b(a - ah.astype(F32)), bh)


def _iota(shape, dim):
    return lax.broadcasted_iota(jnp.int32, shape, dim)


def _rowsum8(x):
    r, d = x.shape
    return jnp.sum(x.reshape(r // 8, 8, d), axis=0)


def _rmsnorm_cast(x, g, name, tm=512):
    n, d = x.shape

    def body(x_ref, g_ref, o_ref):
        v = x_ref[...]
        r = lax.rsqrt(jnp.mean(v * v, axis=-1, keepdims=True) + EPS)
        o_ref[...] = (v * r * g_ref[...]).astype(BF16)

    return pl.pallas_call(
        body, name=name, grid=(n // tm,),
        in_specs=[pl.BlockSpec((tm, d), lambda i: (i, 0)), pl.BlockSpec((1, d), lambda i: (0, 0))],
        out_specs=pl.BlockSpec((tm, d), lambda i: (i, 0)), out_shape=S((n, d), BF16), compiler_params=_cp(("parallel",)),
    )(x, g)


def _rmsnorm_bwd(dh, x, g, resid, name, tm=512):
    n, d = x.shape
    has_res = resid is not None

    def body(*refs):
        if has_res:
            dh_ref, x_ref, g_ref, r_ref, dx_ref, dg_ref = refs
        else:
            dh_ref, x_ref, g_ref, dx_ref, dg_ref = refs
        v = x_ref[...]
        dhv = dh_ref[...].astype(F32)
        r = lax.rsqrt(jnp.mean(v * v, axis=-1, keepdims=True) + EPS)
        xh = v * r
        u = dhv * g_ref[...]
        dx = r * (u - xh * jnp.mean(u * xh, axis=-1, keepdims=True))
        if has_res:
            dx = dx + r_ref[...]
        dx_ref[...] = dx

        @pl.when(pl.program_id(0) == 0)
        def _():
            dg_ref[...] = jnp.zeros_like(dg_ref)

        dg_ref[...] += _rowsum8(dhv * xh)

    tile = pl.BlockSpec((tm, d), lambda i: (i, 0))
    ins = [tile, tile, pl.BlockSpec((1, d), lambda i: (0, 0))] + ([tile] if has_res else [])
    args = (dh, x, g) + ((resid,) if has_res else ())
    return pl.pallas_call(
        body, name=name, grid=(n // tm,), in_specs=ins,
        out_specs=[tile, pl.BlockSpec((8, d), lambda i: (0, 0))],
        out_shape=[S((n, d), F32), S((8, d), F32)], compiler_params=_cp(("arbitrary",)),
    )(*args)


def _mm_nn(a, b, out_dtype, name, tm, tn):
    m, k = a.shape
    n = b.shape[1]
    assert n % tn == 0 and m % tm == 0

    def body(a_ref, b_ref, o_ref):
        o_ref[...] = _nn(a_ref[...].astype(BF16), b_ref[...].astype(BF16)).astype(out_dtype)

    return pl.pallas_call(
        body, name=name, grid=(n // tn, m // tm),
        in_specs=[pl.BlockSpec((tm, k), lambda j, i: (i, 0)), pl.BlockSpec((k, tn), lambda j, i: (0, j))],
        out_specs=pl.BlockSpec((tm, tn), lambda j, i: (i, j)), out_shape=S((m, n), out_dtype),
        compiler_params=_cp(("parallel", "parallel")),
    )(a, b)


def _mm_nt(dy, w, name, tm, tr, w_col0=0, acc=None):
    m, r = dy.shape
    k = w.shape[0]
    jb = w_col0 // tr
    nr = r // tr
    assert w_col0 % tr == 0 and r % tr == 0 and m % tm == 0
    has_acc = acc is not None

    def body(*refs):
        if has_acc:
            dy_ref, w_ref, acc_ref, o_ref = refs
        else:
            dy_ref, w_ref, o_ref = refs
        part = _nt(dy_ref[...].astype(BF16), w_ref[...].astype(BF16))

        @pl.when(pl.program_id(1) == 0)
        def _():
            o_ref[...] = part + acc_ref[...] if has_acc else part

        @pl.when(pl.program_id(1) > 0)
        def _():
            o_ref[...] += part

    out_tile = pl.BlockSpec((tm, k), lambda i, j: (i, 0))
    ins = [pl.BlockSpec((tm, tr), lambda i, j: (i, j)), pl.BlockSpec((k, tr), lambda i, j: (0, j + jb))]
    args = (dy, w)
    if has_acc:
        ins.append(out_tile)
        args = args + (acc,)
    return pl.pallas_call(
        body, name=name, grid=(m // tm, nr), in_specs=ins, out_specs=out_tile, out_shape=S((m, k), F32),
        input_output_aliases=({2: 0} if has_acc else {}), compiler_params=_cp(("parallel", "arbitrary")),
    )(*args)


def _mm_nt_sum(parts, w, name, tm, swap=()):
    m = parts[0][0].shape[0]
    k = w.shape[0]
    assert m % tm == 0 and all(c % n == 0 and o % n == 0 for _, c, n, o in parts)
    np_ = len(parts)
    nsw = len(swap)
    n_steps = m // tm

    def body(*refs):
        o_ref = refs[2 * np_ + nsw]
        if nsw:
            start, finish = _chip_swap_phases(refs[2 * np_:2 * np_ + nsw], refs[2 * np_ + nsw + 1:2 * np_ + 2 * nsw + 1],
                                              *refs[2 * np_ + 2 * nsw + 1:])
            pl.when(pl.program_id(0) == 0)(start)
        acc = _nt(refs[0][...].astype(BF16), refs[np_][...].astype(BF16))
        for i in range(1, np_):
            acc = acc + _nt(refs[i][...].astype(BF16), refs[np_ + i][...].astype(BF16))
        o_ref[...] = acc
        if nsw:
            pl.when(pl.program_id(0) == n_steps - 1)(finish)

    dy_specs = [pl.BlockSpec((tm, n), functools.partial(lambda i, j: (i, j), j=c // n)) for _, c, n, _ in parts]
    w_specs = [pl.BlockSpec((k, n), functools.partial(lambda i, j: (0, j), j=o // n)) for _, _, n, o in parts]
    out = pl.pallas_call(
        body, name=name, grid=(n_steps,), in_specs=dy_specs + w_specs + [ANY] * nsw,
        out_specs=[pl.BlockSpec((tm, k), lambda i: (i, 0))] + [ANY] * nsw,
        out_shape=[S((m, k), F32)] + [S(p.shape, p.dtype) for p in swap],
        scratch_shapes=_chip_swap_sems(nsw) if nsw else [],
        compiler_params=_cp(("arbitrary",) if nsw else ("parallel",)),
    )(*([p[0] for p in parts] + [w] * np_ + list(swap)))
    return (out[0], out[1:]) if nsw else out[0]


def _mm_tn(x, dy, name, tm, tn):
    m, k = x.shape
    n = dy.shape[1]
    tm = min(tm, m)
    assert m % tm == 0 and n % tn == 0

    def body(x_ref, dy_ref, o_ref):
        part = _tn(x_ref[...].astype(BF16), dy_ref[...].astype(BF16))

        @pl.when(pl.program_id(1) == 0)
        def _():
            o_ref[...] = part

        @pl.when(pl.program_id(1) > 0)
        def _():
            o_ref[...] += part

    return pl.pallas_call(
        body, name=name, grid=(n // tn, m // tm),
        in_specs=[pl.BlockSpec((tm, k), lambda j, i: (i, 0)), pl.BlockSpec((tm, tn), lambda j, i: (i, j))],
        out_specs=pl.BlockSpec((k, tn), lambda j, i: (0, j)), out_shape=S((k, n), F32),
        compiler_params=_cp(("parallel", "arbitrary")),
    )(x, dy)


def _lower_bound(logits):
    e = jnp.exp(logits - jnp.max(logits, axis=0, keepdims=True))
    return e[0:1, :] / jnp.sum(e, axis=0, keepdims=True)


def _hg_gates(fl, lb):
    sig = jax.nn.sigmoid(fl)
    f = lb + (1.0 - lb) * sig
    k = (1.0 - lb) * (1.0 - sig)
    return sig, f, k, jnp.log(f)


def _silu_and_grad(x):
    s = jax.nn.sigmoid(x)
    return x * s, s * (1.0 + x * (1.0 - s))


def _hg_rowblocks(G):
    return [None] + [G[SUB * i - 1:SUB * i, :] for i in range(1, CHUNK // SUB)]


def _hg_intra_A(qs, k, G):
    refs = _hg_rowblocks(G)
    cols = _iota((SUB, CHUNK), 1)
    rows = _iota((SUB, CHUNK), 0)
    blocks = []
    for i in range(CHUNK // SUB):
        lo = SUB * i
        qb, Gb = qs[lo:lo + SUB, :], G[lo:lo + SUB, :]
        diag = jnp.zeros((SUB, CHUNK), F32)
        for s in range(SUB):
            e = jnp.exp(jnp.minimum(Gb - G[lo + s:lo + s + 1, :], 0.0))
            col = jnp.sum(qb * k[lo + s:lo + s + 1, :] * e, axis=-1, keepdims=True)
            diag = jnp.where(cols == lo + s, col, diag)
        a = jnp.where((cols >= lo) & (cols <= rows + lo), diag, 0.0)
        if i > 0:
            qr = qb * jnp.exp(Gb - refs[i])
            kr = k * jnp.exp(jnp.minimum(refs[i] - G, 0.0))
            a = jnp.where(cols < lo, _nt(_b(qr), _b(kr)), a)
        blocks.append(a)
    return jnp.concatenate(blocks, axis=0)


def _hg_intra_bwd(dA, qs, k, G):
    refs = _hg_rowblocks(G)
    cols = _iota((SUB, CHUNK), 1)
    rows16 = _iota((SUB, HG_D), 0)
    dk = jnp.zeros((CHUNK, HG_D), F32)
    dq_blocks, dk_diag_blocks = [], []
    for i in range(CHUNK // SUB):
        lo = SUB * i
        qb, Gb = qs[lo:lo + SUB, :], G[lo:lo + SUB, :]
        dAb = dA[lo:lo + SUB, :]
        dq = jnp.zeros((SUB, HG_D), F32)
        dkb = jnp.zeros((SUB, HG_D), F32)
        for s in range(SUB):
            e = jnp.exp(jnp.minimum(Gb - G[lo + s:lo + s + 1, :], 0.0))
            e = jnp.where(rows16 >= s, e, 0.0)
            dcol = jnp.sum(jnp.where(cols == lo + s, dAb, 0.0), axis=-1, keepdims=True)
            w = dcol * e
            dq = dq + w * k[lo + s:lo + s + 1, :]
            dkb = jnp.where(rows16 == s, jnp.sum(w * qb, axis=0, keepdims=True), dkb)
        if i > 0:
            e1 = jnp.exp(Gb - refs[i])
            e2 = jnp.exp(jnp.minimum(refs[i] - G, 0.0))
            dA_off = jnp.where(cols < lo, dAb, 0.0)
            dq = dq + _mm3(_nn, dA_off, k * e2) * e1
            dk = dk + _mm3(_tn, dA_off, qb * e1) * e2
        dq_blocks.append(dq)
        dk_diag_blocks.append(dkb)
    return jnp.concatenate(dq_blocks, axis=0), dk + jnp.concatenate(dk_diag_blocks, axis=0)


def _tri(n, upper=False):
    r, c = _iota((n, n), 0), _iota((n, n), 1)
    return jnp.where((c >= r) if upper else (r >= c), 1.0, 0.0).astype(BF16)


def _prefix_mm(tri, x):
    hi = x.astype(BF16)
    r1 = x - hi.astype(F32)
    mid = r1.astype(BF16)
    lo = (r1 - mid.astype(F32)).astype(BF16)
    return _nn(tri, hi) + _nn(tri, mid) + _nn(tri, lo)


def _hgrn_fwd(z, lb, gn, B, T):
    N = B * T
    NC = T // CHUNK
    ng = HG_H // HG_GROUP

    def body(z_ref, lb_ref, gn_ref, y_ref, o_ref, st_ref, s_scr):
        lbs = _lower_bound(lb_ref[...])
        tri = _tri(CHUNK)
        s_scr[...] = jnp.zeros_like(s_scr)

        def chunk(c, carry):
            r = pl.ds(pl.multiple_of(c * CHUNK, CHUNK), CHUNK)
            for hh in range(HG_GROUP):
                zc, oc = 4 * LANE * hh, LANE * hh
                ql, fl, il, gl = (z_ref[r, zc + LANE * j:zc + LANE * (j + 1)].astype(F32) for j in range(4))
                _, _, k, logf = _hg_gates(fl, lbs[:, oc:oc + LANE])
                G = _prefix_mm(tri, logf)
                qs = ql * jax.nn.sigmoid(ql)
                st = s_scr[hh]
                st_ref[hh * NC + c] = st
                g_last = G[CHUNK - 1:CHUNK, :]
                A = _hg_intra_A(qs, k, G)
                o = _nn(_b(A), _b(il)) + _nt(_b(qs * jnp.exp(G)), _b(st))
                s_scr[hh] = st * jnp.exp(g_last) + _mm3(_tn, il, k * jnp.exp(g_last - G))
                o_ref[r, oc:oc + LANE] = o
                rstd = lax.rsqrt(jnp.mean(o * o, axis=-1, keepdims=True) + EPS)
                y_ref[r, oc:oc + LANE] = (o * rstd * gn_ref[...] * (gl * jax.nn.sigmoid(gl))).astype(BF16)
            return carry

        lax.fori_loop(0, NC, chunk, 0)

    gw = HG_GROUP * LANE
    cb = C_HG // (4 * gw)
    return pl.pallas_call(
        body, name="hgrn_fwd", grid=(B, ng),
        in_specs=[pl.BlockSpec((T, 4 * gw), lambda b, h: (b, cb + h)), pl.BlockSpec((lb.shape[0], gw), lambda b, h: (0, h)),
                  pl.BlockSpec((1, LANE), lambda b, h: (0, 0))],
        out_specs=[pl.BlockSpec((T, gw), lambda b, h: (b, h)), pl.BlockSpec((T, gw), lambda b, h: (b, h)),
                   pl.BlockSpec((HG_GROUP * NC, HG_D, HG_D), lambda b, h: (b * ng + h, 0, 0))],
        out_shape=[S((N, 512), BF16), S((N, 512), F32), S((B * HG_H * NC, HG_D, HG_D), F32)],
        scratch_shapes=[pltpu.VMEM((HG_GROUP, HG_D, HG_D), F32)], compiler_params=_cp(("parallel", "parallel")),
    )(z, lb, gn)


def _hgrn_bwd(z, o_raw, states, dy, lb, gn, B, T, swap_sibling=()):
    N = B * T
    NC = T // CHUNK
    ng = HG_H // HG_GROUP
    nsw = len(swap_sibling)

    def body(z_ref, o_ref, st_ref, dy_ref, lb_ref, gn_ref, dz_ref, dlb_ref, dgn_ref, ds_scr, racc, dgn_acc):
        lbs = _lower_bound(lb_ref[...])
        gn_v = gn_ref[...]
        tri, triu = _tri(CHUNK), _tri(CHUNK, upper=True)
        cmask = _iota((CHUNK, CHUNK), 0) >= _iota((CHUNK, CHUNK), 1)
        for ref in (ds_scr, racc, dgn_acc, dlb_ref):
            ref[...] = jnp.zeros_like(ref)

        def chunk(ci, carry):
            c = NC - 1 - ci
            r = pl.ds(pl.multiple_of(c * CHUNK, CHUNK), CHUNK)
            for hh in range(HG_GROUP):
                zc, oc = 4 * LANE * hh, LANE * hh
                lb_v = lbs[:, oc:oc + LANE]
                ql, fl, il, gl = (z_ref[r, zc + LANE * j:zc + LANE * (j + 1)].astype(F32) for j in range(4))
                sig, f, k, logf = _hg_gates(fl, lb_v)
                G = _prefix_mm(tri, logf)
                qs, dsilu_q = _silu_and_grad(ql)
                gs, dsilu_g = _silu_and_grad(gl)
                o = o_ref[r, oc:oc + LANE]
                dyv = dy_ref[r, oc:oc + LANE]
                rstd = lax.rsqrt(jnp.mean(o * o, axis=-1, keepdims=True) + EPS)
                oh = o * rstd
                dgl = dyv * oh * gn_v * dsilu_g
                dn = dyv * gs
                dgn_acc[...] += _rowsum8(dn * oh)
                u = dn * gn_v
                do = rstd * (u - oh * jnp.mean(u * oh, axis=-1, keepdims=True))
                st = st_ref[hh * NC + c]
                dst = ds_scr[hh]
                eG = jnp.exp(G)
                g_last = G[CHUNK - 1:CHUNK, :]
                eL = jnp.exp(g_last - G)
                dA = jnp.where(cmask, _mm3(_nt, do, il), 0.0)
                A, (dq_in, dk_in) = _hg_intra_A(qs, k, G), _hg_intra_bwd(dA, qs, k, G)
                di = _tn(_b(A), _b(do)) + _nt(_b(k * eL), _b(dst))
                dq = dq_in + _mm3(_nn, do, st) * eG
                dk = dk_in + _mm3(_nn, il, dst) * eL
                ds_scr[hh] = dst * jnp.exp(g_last) + _mm3(_tn, do, qs * eG)
                dd = qs * dq - k * dk
                dlogf = _prefix_mm(triu, dd) + racc[hh]
                racc[hh] += jnp.sum(dd, axis=0, keepdims=True)
                df = dlogf / f - dk
                dlb_ref[8 * hh:8 * (hh + 1), :] += _rowsum8(df * (1.0 - sig))
                dz_ref[r, zc:zc + LANE] = (dq * dsilu_q).astype(BF16)
                dz_ref[r, zc + LANE:zc + 2 * LANE] = (df * (1.0 - lb_v) * sig * (1.0 - sig)).astype(BF16)
                dz_ref[r, zc + 2 * LANE:zc + 3 * LANE] = di.astype(BF16)
                dz_ref[r, zc + 3 * LANE:zc + 4 * LANE] = dgl.astype(BF16)
            return carry

        lax.fori_loop(0, NC, chunk, 0)
        dgn_ref[...] = dgn_acc[...]

    gw = HG_GROUP * LANE
    cb = C_HG // (4 * gw)
    col = pl.BlockSpec((T, gw), lambda b, h: (b, h))
    if nsw:
        body = _hosting(body, 6, 3, 3, nsw, _sibling_swap_phases, (B, ng))
    return pl.pallas_call(
        body, name="hgrn_bwd", grid=(B, ng),
        in_specs=[pl.BlockSpec((T, 4 * gw), lambda b, h: (b, cb + h)), col,
                  pl.BlockSpec((HG_GROUP * NC, HG_D, HG_D), lambda b, h: (b * ng + h, 0, 0)), col,
                  pl.BlockSpec((lb.shape[0], gw), lambda b, h: (0, h)), pl.BlockSpec((1, LANE), lambda b, h: (0, 0))]
        + [ANY] * nsw,
        out_specs=[pl.BlockSpec((T, 4 * gw), lambda b, h: (b, h)),
                   pl.BlockSpec((8 * HG_GROUP, LANE), lambda b, h: (b * ng + h, 0)),
                   pl.BlockSpec((8, LANE), lambda b, h: (b * ng + h, 0))] + [ANY] * nsw,
        out_shape=[S((N, 2048), BF16), S((B * HG_H * 8, LANE), F32), S((B * ng * 8, LANE), F32)]
        + _sibling_swap_shapes(swap_sibling),
        scratch_shapes=[pltpu.VMEM((HG_GROUP, HG_D, HG_D), F32), pltpu.VMEM((HG_GROUP, 1, LANE), F32),
                        pltpu.VMEM((8, LANE), F32)] + (_sibling_swap_sems(nsw) if nsw else []),
        compiler_params=_cp(("arbitrary", "arbitrary") if nsw else ("parallel", "parallel")),
    )(z, o_raw, states, dy, lb, gn, *swap_sibling)


def _pair_mean(x, lo_half):
    a = jnp.sum(jnp.where(lo_half, x, 0.0), axis=-1, keepdims=True)
    b = jnp.sum(jnp.where(lo_half, 0.0, x), axis=-1, keepdims=True)
    return jnp.where(lo_half, a, b) * (1.0 / FOX_D)


def _fox_gate_fwd(z, bias, B, T):
    N = B * T
    tb = LANE

    def body(z_ref, b_ref, fc_ref, fct_ref):
        tri = _tri(tb)

        def step(i, carry):
            r = pl.ds(pl.multiple_of(i * tb, tb), tb)
            cs = _prefix_mm(tri, jax.nn.log_sigmoid(z_ref[r, :].astype(F32) + b_ref[...])) + carry
            fc_ref[r, :] = cs
            fct_ref[0, :, r] = cs.T[0:8, :]
            return cs[tb - 1:tb, :]

        lax.fori_loop(0, T // tb, step, jnp.zeros((1, LANE), F32))

    return pl.pallas_call(
        body, name="fox_gate_fwd", grid=(B,),
        in_specs=[pl.BlockSpec((T, LANE), lambda b: (b, C_FF // LANE)), pl.BlockSpec((1, LANE), lambda b: (0, 0))],
        out_specs=[pl.BlockSpec((T, LANE), lambda b: (b, 0)), pl.BlockSpec((1, 8, T), lambda b: (b, 0, 0))],
        out_shape=[S((N, LANE), F32), S((B, 8, T), F32)], compiler_params=_cp(("parallel",)),
    )(z, bias)


def _fox_gate_bwd(dfc, z, bias, B, T):
    N = B * T
    tb = LANE
    nt = T // tb

    def body(d_ref, z_ref, b_ref, dz_ref, db_ref):
        triu = _tri(tb, upper=True)
        db_ref[...] = jnp.zeros_like(db_ref)

        def step(ii, carry):
            r = pl.ds(pl.multiple_of((nt - 1 - ii) * tb, tb), tb)
            d = d_ref[r, 0:LANE]
            for p in range(1, FOX_P):
                d = d + d_ref[r, LANE * p:LANE * (p + 1)]
            rc = _prefix_mm(triu, d) + carry
            dff = rc * jax.nn.sigmoid(-(z_ref[r, :].astype(F32) + b_ref[...]))
            dz_ref[r, :] = dff.astype(BF16)
            db_ref[...] += _rowsum8(dff)
            return carry + jnp.sum(d, axis=0, keepdims=True)

        lax.fori_loop(0, nt, step, jnp.zeros((1, LANE), F32))

    return pl.pallas_call(
        body, name="fox_gate_bwd", grid=(B,),
        in_specs=[pl.BlockSpec((T, 512), lambda b: (b, 0)), pl.BlockSpec((T, LANE), lambda b: (b, C_FF // LANE)),
                  pl.BlockSpec((1, LANE), lambda b: (0, 0))],
        out_specs=[pl.BlockSpec((T, LANE), lambda b: (b, 0)), pl.BlockSpec((8, LANE), lambda b: (b, 0))],
        out_shape=[S((N, LANE), BF16), S((B * 8, LANE), F32)], compiler_params=_cp(("parallel",)),
    )(dfc, z, bias)


def _fox_prep(z_ref, gq, gk, r, lo_half):
    q, k, v = (z_ref[r, LANE * j:LANE * (j + 1)].astype(F32) for j in range(3))
    rq = lax.rsqrt(_pair_mean(q * q, lo_half) + EPS)
    rk = lax.rsqrt(_pair_mean(k * k, lo_half) + EPS)
    qh, kh = q * rq, k * rk
    return qh * gq * (FOX_D ** -0.5), kh * gk, v, qh, kh, rq, rk


def _fox_fwd(z, fc, fct, gq, gk, B, T, tq=512, gather=()):
    N = B * T
    NQ = T // tq
    nga = len(gather)

    def body(z_ref, fc_ref, fct_ref, gq_ref, gk_ref, y_ref, lse_ref, qn_s, kn_s, v_s):
        p, qi = pl.program_id(1), pl.program_id(2)
        lo_half = _iota((1, LANE), 1) < FOX_D

        @pl.when(qi == 0)
        def _():
            def prep(i, carry):
                r = pl.ds(pl.multiple_of(i * tq, tq), tq)
                qn, kn, v = _fox_prep(z_ref, gq_ref[...], gk_ref[...], r, lo_half)[:3]
                qn_s[r, :], kn_s[r, :], v_s[r, :] = qn.astype(BF16), kn.astype(BF16), v.astype(BF16)
                return carry
            lax.fori_loop(0, NQ, prep, 0)

        rq = pl.ds(pl.multiple_of(qi * tq, tq), tq)
        qn = qn_s[rq, :]
        fcq = fc_ref[rq, :]
        lane = _iota((tq, LANE), 1)
        causal = _iota((tq, tq), 0) >= _iota((tq, tq), 1)
        qhs = [jnp.where(lo_half, qn, jnp.zeros_like(qn)), jnp.where(lo_half, jnp.zeros_like(qn), qn)]
        fqs = [jnp.sum(jnp.where(lane == 2 * p + hh, fcq, 0.0), axis=-1, keepdims=True) for hh in range(2)]

        def kv(j, carry, diagonal):
            rk = pl.ds(pl.multiple_of(j * tq, tq), tq)
            kj, vj = kn_s[rk, :], v_s[rk, :]
            new = []
            for hh in range(2):
                m, l, acc = carry[hh]
                s = _nt(qhs[hh], kj) + fqs[hh] - fct_ref[0, pl.ds(2 * p + hh, 1), rk]
                if diagonal:
                    s = jnp.where(causal, s, NEG)
                m_new = jnp.maximum(m, jnp.max(s, axis=-1, keepdims=True))
                pe = jnp.exp(s - m_new)
                alpha = jnp.exp(m - m_new)
                new.append((m_new, alpha * l + jnp.sum(pe, axis=-1, keepdims=True),
                            alpha * acc + _nn(pe.astype(BF16), vj)))
            return tuple(new)

        init = tuple((jnp.full((tq, 1), NEG, F32), jnp.zeros((tq, 1), F32), jnp.zeros((tq, LANE), F32)) for _ in range(2))
        carry = lax.fori_loop(0, qi, functools.partial(kv, diagonal=False), init)
        (m0, l0, a0), (m1, l1, a1) = kv(qi, carry, True)
        y_ref[...] = jnp.where(lo_half, a0 / l0, a1 / l1).astype(BF16)
        lse_ref[...] = jnp.where(lo_half, m0 + jnp.log(l0), m1 + jnp.log(l1))

    vec = pl.BlockSpec((1, LANE), lambda b, p, q: (0, 0))
    tile = pl.BlockSpec((tq, LANE), lambda b, p, q: (b * NQ + q, p))
    if nga:
        body = _hosting(body, 5, 2, 3, nga, _gather_phases, (B, FOX_P, NQ))
    return pl.pallas_call(
        body, name="fox_fwd", grid=(B, FOX_P, NQ),
        in_specs=[pl.BlockSpec((T, 384), lambda b, p, q: (b, p)), pl.BlockSpec((T, LANE), lambda b, p, q: (b, 0)),
                  pl.BlockSpec((1, 8, T), lambda b, p, q: (b, 0, 0)), vec, vec] + [ANY] * nga,
        out_specs=[tile, tile] + [ANY] * nga, out_shape=[S((N, 512), BF16), S((N, 512), F32)] + _gather_shapes(gather),
        scratch_shapes=[pltpu.VMEM((T, LANE), BF16)] * 3 + (_gather_sems(nga) if nga else []),
        compiler_params=_cp(("arbitrary",) * 3 if nga else ("parallel", "parallel", "arbitrary")),
    )(z, fc, fct, gq, gk, *gather)


def _fox_bwd(z, dy, y, lse, fc, fct, gq, gk, B, T, tq=512, swap=()):
    N = B * T
    NQ = T // tq
    nsw = len(swap)

    def body(z_ref, dy_ref, y_ref, lse_ref, fc_ref, fct_ref, gq_ref, gk_ref, dz_ref, dfc_ref, dgq_ref, dgk_ref,
             qn_s, kn_s, v_s, do_s, delta_s, dq_s, dfk_s):
        p, kj = pl.program_id(1), pl.program_id(2)
        lo_half = _iota((1, LANE), 1) < FOX_D
        lane = _iota((tq, LANE), 1)
        gq_v, gk_v = gq_ref[...], gk_ref[...]

        @pl.when(kj == 0)
        def _():
            def prep(i, carry):
                r = pl.ds(pl.multiple_of(i * tq, tq), tq)
                qn, kn, v = _fox_prep(z_ref, gq_v, gk_v, r, lo_half)[:3]
                qn_s[r, :], kn_s[r, :], v_s[r, :] = qn.astype(BF16), kn.astype(BF16), v.astype(BF16)
                do = dy_ref[r, :]
                do_s[r, :] = do.astype(BF16)
                delta_s[r, :] = _pair_mean(do * y_ref[r, :].astype(F32), lo_half) * float(FOX_D)
                return carry
            lax.fori_loop(0, NQ, prep, 0)
            dq_s[...] = jnp.zeros_like(dq_s)
            dgq_ref[...] = jnp.zeros_like(dgq_ref)
            dgk_ref[...] = jnp.zeros_like(dgk_ref)

        rk = pl.ds(pl.multiple_of(kj * tq, tq), tq)
        kn, vv = kn_s[rk, :], v_s[rk, :]
        causal = _iota((tq, tq), 0) >= _iota((tq, tq), 1)
        zero, one = jnp.zeros_like(kn), jnp.ones_like(kn)
        hms = [lo_half, jnp.logical_not(lo_half)]
        kmasks = [jnp.where(hm, kn, zero) for hm in hms]
        kaugs = [jnp.where(hm, kn, one) for hm in hms]
        vmasks = [jnp.where(hm, vv, zero) for hm in hms]
        fks = [fct_ref[0, pl.ds(2 * p + hh, 1), rk] for hh in range(2)]

        def qloop(i, carry, diagonal):
            ri = pl.ds(pl.multiple_of(i * tq, tq), tq)
            qn = qn_s[ri, :]
            do = do_s[ri, :]
            fcq = fc_ref[ri, :]
            new = []
            for hh in range(2):
                dk_acc, dv_acc = carry[hh]
                c0 = FOX_D * hh
                fq = jnp.sum(jnp.where(lane == 2 * p + hh, fcq, 0.0), axis=-1, keepdims=True)
                pr = jnp.exp(_nt(qn, kmasks[hh]) + fq - fks[hh] - lse_ref[ri, c0:c0 + 1])
                if diagonal:
                    pr = jnp.where(causal, pr, 0.0)
                ds = (pr * (_nt(do, vmasks[hh]) - delta_s[ri, c0:c0 + 1])).astype(BF16)
                dq_s[hh, ri, :] += _nn(ds, kaugs[hh])
                new.append((dk_acc + _tn(jnp.where(hms[hh], qn, one), ds), dv_acc + _tn(do, pr.astype(BF16))))
            return tuple(new)

        init = tuple((jnp.zeros((LANE, tq), F32), jnp.zeros((LANE, tq), F32)) for _ in range(2))
        carry = qloop(kj, init, True)
        (dk0, dv0), (dk1, dv1) = lax.fori_loop(kj + 1, NQ, functools.partial(qloop, diagonal=False), carry)
        dks, dvs = [dk0.T, dk1.T], [dv0.T, dv1.T]

        dkn = jnp.where(lo_half, dks[0], dks[1])
        _, _, _, _, kh, _, rkk = _fox_prep(z_ref, gq_v, gk_v, rk, lo_half)
        u = dkn * gk_v
        dz_ref[rk, LANE:2 * LANE] = (rkk * (u - kh * _pair_mean(u * kh, lo_half))).astype(BF16)
        dz_ref[rk, 2 * LANE:3 * LANE] = jnp.where(lo_half, dvs[0], dvs[1]).astype(BF16)
        dgk_ref[...] += _rowsum8(dkn * kh)
        dfk_s[rk, :] = jnp.where(lane == 2 * p, -dks[0][:, FOX_D:FOX_D + 1],
                                 jnp.where(lane == 2 * p + 1, -dks[1][:, 0:1], 0.0))

        @pl.when(kj == NQ - 1)
        def _():
            def fin(i, carry):
                r = pl.ds(pl.multiple_of(i * tq, tq), tq)
                d0, d1 = dq_s[0, r, :], dq_s[1, r, :]
                dqn = jnp.where(lo_half, d0, d1)
                _, _, _, qh, _, rqq, _ = _fox_prep(z_ref, gq_v, gk_v, r, lo_half)
                u = dqn * gq_v * (FOX_D ** -0.5)
                dz_ref[r, 0:LANE] = (rqq * (u - qh * _pair_mean(u * qh, lo_half))).astype(BF16)
                dgq_ref[...] += _rowsum8(dqn * qh) * (FOX_D ** -0.5)
                dfc_ref[r, :] = dfk_s[r, :] + jnp.where(lane == 2 * p, d0[:, FOX_D:FOX_D + 1],
                                                        jnp.where(lane == 2 * p + 1, d1[:, 0:1], 0.0))
                return carry
            lax.fori_loop(0, NQ, fin, 0)

    vec = pl.BlockSpec((1, LANE), lambda b, p, k: (0, 0))
    col = pl.BlockSpec((T, LANE), lambda b, p, k: (b, p))
    part = pl.BlockSpec((8, LANE), lambda b, p, k: (b * FOX_P + p, 0))
    if nsw:
        body = _hosting(body, 8, 4, 7, nsw, _chip_swap_phases, (B, FOX_P, NQ))
    return pl.pallas_call(
        body, name="fox_bwd", grid=(B, FOX_P, NQ),
        in_specs=[pl.BlockSpec((T, 384), lambda b, p, k: (b, p)), col, col, col,
                  pl.BlockSpec((T, LANE), lambda b, p, k: (b, 0)), pl.BlockSpec((1, 8, T), lambda b, p, k: (b, 0, 0)),
                  vec, vec] + [ANY] * nsw,
        out_specs=[pl.BlockSpec((T, 384), lambda b, p, k: (b, p)), col, part, part] + [ANY] * nsw,
        out_shape=[S((N, 1536), BF16), S((N, 512), F32), S((B * FOX_P * 8, LANE), F32), S((B * FOX_P * 8, LANE), F32)]
        + [S(p.shape, p.dtype) for p in swap],
        scratch_shapes=[pltpu.VMEM((T, LANE), BF16)] * 4 + [pltpu.VMEM((T, LANE), F32), pltpu.VMEM((2, T, LANE), F32),
                                                            pltpu.VMEM((T, LANE), F32)]
        + (_chip_swap_sems(nsw) if nsw else []),
        compiler_params=_cp(("arbitrary",) * 3 if nsw else ("parallel", "parallel", "arbitrary")),
    )(z, dy, y, lse, fc, fct, gq, gk, *swap)


def _mem_scores(z_ref, kv_ref, gq, gk, h):
    c = slice(MEM_D * h, MEM_D * (h + 1))
    q, k = z_ref[:, c].astype(F32), kv_ref[:, c]
    rq = lax.rsqrt(jnp.mean(q * q, axis=-1, keepdims=True) + EPS)
    rk = lax.rsqrt(jnp.mean(k * k, axis=-1, keepdims=True) + EPS)
    qh, kh = q * rq, k * rk
    qn = (qh * gq * (MEM_D ** -0.5)).astype(BF16)
    kn = (kh * gk).astype(BF16)
    s = _nt(qn, kn)
    pe = jnp.exp(s - jnp.max(s, axis=-1, keepdims=True))
    pn = pe / jnp.sum(pe, axis=-1, keepdims=True)
    return pn, qn, kn, qh, kh, rq, rk


def _mem_fwd(z, memkv, gq, gk, B, T, M, tq=512):
    N = B * T
    NQ = T // tq
    W = MEM_H * MEM_D

    def body(z_ref, kv_ref, gq_ref, gk_ref, y_ref):
        for h in range(MEM_H):
            pn = _mem_scores(z_ref, kv_ref, gq_ref[...], gk_ref[...], h)[0]
            v = kv_ref[:, W + MEM_D * h:W + MEM_D * (h + 1)].astype(BF16)
            y_ref[:, MEM_D * h:MEM_D * (h + 1)] = _nn(pn.astype(BF16), v).astype(BF16)

    vec = pl.BlockSpec((1, LANE), lambda b, q: (0, 0))
    return pl.pallas_call(
        body, name="mem_fwd", grid=(B, NQ),
        in_specs=[pl.BlockSpec((tq, W), lambda b, q: (b * NQ + q, C_MQ // W)),
                  pl.BlockSpec((M, 2 * W), lambda b, q: (b, 0)), vec, vec],
        out_specs=pl.BlockSpec((tq, W), lambda b, q: (b * NQ + q, 0)), out_shape=S((N, W), BF16),
        compiler_params=_cp(("parallel", "parallel")),
    )(z, memkv, gq, gk)


def _mem_bwd(z, memkv, dy, gq, gk, B, T, M, tq=512):
    N = B * T
    NQ = T // tq
    W = MEM_H * MEM_D

    def body(z_ref, kv_ref, dy_ref, gq_ref, gk_ref, dz_ref, dkv_ref, dgq_ref, dgk_ref, acc):
        qi = pl.program_id(1)
        gq_v, gk_v = gq_ref[...], gk_ref[...]

        @pl.when(qi == 0)
        def _():
            acc[...] = jnp.zeros_like(acc)
            dgq_ref[...] = jnp.zeros_like(dgq_ref)
            dgk_ref[...] = jnp.zeros_like(dgk_ref)

        for h in range(MEM_H):
            c = slice(MEM_D * h, MEM_D * (h + 1))
            cv = slice(W + MEM_D * h, W + MEM_D * (h + 1))
            pn, qn, kn, qh, _, rq, _ = _mem_scores(z_ref, kv_ref, gq_v, gk_v, h)
            do = dy_ref[:, c].astype(BF16)
            dp = _nt(do, kv_ref[:, cv].astype(BF16))
            ds = (pn * (dp - jnp.sum(dp * pn, axis=-1, keepdims=True))).astype(BF16)
            dqn = _nn(ds, kn)
            acc[:, c] += _tn(ds, qn)
            acc[:, cv] += _tn(pn.astype(BF16), do)
            u = dqn * gq_v * (MEM_D ** -0.5)
            dz_ref[:, c] = (rq * (u - qh * jnp.mean(u * qh, axis=-1, keepdims=True))).astype(BF16)
            dgq_ref[...] += _rowsum8(dqn * qh) * (MEM_D ** -0.5)

        @pl.when(qi == NQ - 1)
        def _():
            for h in range(MEM_H):
                c = slice(MEM_D * h, MEM_D * (h + 1))
                cv = slice(W + MEM_D * h, W + MEM_D * (h + 1))
                k = kv_ref[:, c]
                rk = lax.rsqrt(jnp.mean(k * k, axis=-1, keepdims=True) + EPS)
                kh = k * rk
                dkn = acc[:, c]
                u = dkn * gk_v
                dkv_ref[:, c] = (rk * (u - kh * jnp.mean(u * kh, axis=-1, keepdims=True))).astype(BF16)
                dkv_ref[:, cv] = acc[:, cv].astype(BF16)
                dgk_ref[...] += _rowsum8(dkn * kh)

    vec = pl.BlockSpec((1, LANE), lambda b, q: (0, 0))
    part = pl.BlockSpec((8, LANE), lambda b, q: (b, 0))
    return pl.pallas_call(
        body, name="mem_bwd", grid=(B, NQ),
        in_specs=[pl.BlockSpec((tq, W), lambda b, q: (b * NQ + q, C_MQ // W)),
                  pl.BlockSpec((M, 2 * W), lambda b, q: (b, 0)), pl.BlockSpec((tq, W), lambda b, q: (b * NQ + q, 0)),
                  vec, vec],
        out_specs=[pl.BlockSpec((tq, W), lambda b, q: (b * NQ + q, 0)), pl.BlockSpec((M, 2 * W), lambda b, q: (b, 0)),
                   part, part],
        out_shape=[S((N, W), BF16), S((B * M, 2 * W), BF16), S((B * 8, LANE), F32), S((B * 8, LANE), F32)],
        scratch_shapes=[pltpu.VMEM((M, 2 * W), F32)], compiler_params=_cp(("parallel", "arbitrary")),
    )(z, memkv, dy, gq, gk)


def _merge_fwd(ya, yb, yc, z, x, wa, wb, wc, wo, tm=256):
    n, d = x.shape
    wdt = ya.shape[1]
    gb = C_GATE // d

    def body(ya_ref, yb_ref, yc_ref, g0_ref, g1_ref, g2_ref, x_ref, wa_ref, wb_ref, wc_ref, wo_ref,
             x1_ref, mg_ref, ua_ref, ub_ref, uc_ref):
        merged = jnp.zeros((tm, d), F32)
        for y_ref, g_ref, w_ref, u_ref in ((ya_ref, g0_ref, wa_ref, ua_ref), (yb_ref, g1_ref, wb_ref, ub_ref),
                                           (yc_ref, g2_ref, wc_ref, uc_ref)):
            u = _nn(y_ref[...], w_ref[...])
            u_ref[...] = u.astype(BF16)
            merged = merged + jax.nn.sigmoid(g_ref[...].astype(F32)) * u
        mb = merged.astype(BF16)
        mg_ref[...] = mb
        x1_ref[...] = x_ref[...] + _nn(mb, wo_ref[...])

    yt = pl.BlockSpec((tm, wdt), lambda i: (i, 0))
    xt = pl.BlockSpec((tm, d), lambda i: (i, 0))
    wbr = pl.BlockSpec((wdt, d), lambda i: (0, 0))
    gates = [pl.BlockSpec((tm, d), functools.partial(lambda i, k: (i, gb + k), k=k)) for k in range(3)]
    return pl.pallas_call(
        body, name="merge_fwd", grid=(n // tm,),
        in_specs=[yt, yt, yt] + gates + [xt, wbr, wbr, wbr, pl.BlockSpec((d, d), lambda i: (0, 0))],
        out_specs=[xt] * 5, out_shape=[S((n, d), F32)] + [S((n, d), BF16)] * 4, compiler_params=_cp(("parallel",)),
    )(ya, yb, yc, z, z, z, x, wa, wb, wc, wo)


def _merge_bwd(dx1, z, ua, ub, uc, wa, wb, wc, wo, tm=256):
    n, d = dx1.shape
    wdt = wa.shape[0]
    gb = C_GATE // d

    def body(dx_ref, g0_ref, g1_ref, g2_ref, ua_ref, ub_ref, uc_ref, wa_ref, wb_ref, wc_ref, wo_ref,
             dg_ref, dya_ref, dyb_ref, dyc_ref, dua_ref, dub_ref, duc_ref):
        dm = _nt(dx_ref[...].astype(BF16), wo_ref[...])
        for k, (g_ref, u_ref, w_ref, dy_ref, du_ref) in enumerate((
                (g0_ref, ua_ref, wa_ref, dya_ref, dua_ref), (g1_ref, ub_ref, wb_ref, dyb_ref, dub_ref),
                (g2_ref, uc_ref, wc_ref, dyc_ref, duc_ref))):
            g = jax.nn.sigmoid(g_ref[...].astype(F32))
            du = (dm * g).astype(BF16)
            du_ref[...] = du
            dg_ref[:, d * k:d * (k + 1)] = (dm * u_ref[...].astype(F32) * g * (1.0 - g)).astype(BF16)
            dy_ref[...] = _nt(du, w_ref[...])

    yt = pl.BlockSpec((tm, wdt), lambda i: (i, 0))
    xt = pl.BlockSpec((tm, d), lambda i: (i, 0))
    wbr = pl.BlockSpec((wdt, d), lambda i: (0, 0))
    gates = [pl.BlockSpec((tm, d), functools.partial(lambda i, k: (i, gb + k), k=k)) for k in range(3)]
    return pl.pallas_call(
        body, name="merge_bwd", grid=(n // tm,),
        in_specs=[xt] + gates + [xt, xt, xt, wbr, wbr, wbr, pl.BlockSpec((d, d), lambda i: (0, 0))],
        out_specs=[pl.BlockSpec((tm, 3 * d), lambda i: (i, 0)), yt, yt, yt, xt, xt, xt],
        out_shape=[S((n, 3 * d), BF16)] + [S((n, wdt), F32)] * 3 + [S((n, d), BF16)] * 3,
        compiler_params=_cp(("parallel",)),
    )(dx1, z, z, z, ua, ub, uc, wa, wb, wc, wo)


FFN_TN = 1408
TN_TM = 2048
INV_SQRT2 = 0.7071067811865476
INV_SQRT_2PI = 0.3989422804014327


def _conv_shifted(a, prev, first, tm):
    row = _iota(a.shape, 0)
    p7 = jnp.where(first, 0.0, prev[7:8, :])
    p6 = jnp.where(first, 0.0, prev[6:7, :])
    a1 = jnp.where(row == 0, p7, pltpu.roll(a, 1, 0))
    a2 = jnp.where(row == 0, p6, jnp.where(row == 1, p7, pltpu.roll(a, 2, 0)))
    return a1, a2


def _ffn_act_fwd(up, cw, cb, B, T, tm=256):
    N = B * T
    dff = cw.shape[1]
    NT, NJ, tn = T // tm, dff // FFN_TN, FFN_TN

    def body(a_ref, v_ref, cw_ref, cb_ref, y_ref, c_ref, carry):
        t = pl.program_id(2)
        a = a_ref[...].astype(F32)
        a1, a2 = _conv_shifted(a, carry[...], t == 0, tm)
        w = cw_ref[...]
        ac = w[0:1, :] * a2 + w[1:2, :] * a1 + w[2:3, :] * a + cb_ref[...]
        cdf = 0.5 * (1.0 + lax.erf(ac * INV_SQRT2))
        y_ref[...] = (ac * cdf * v_ref[...].astype(F32)).astype(BF16)
        c_ref[...] = cdf.astype(BF16)
        carry[...] = a[tm - 8:tm, :]

    return pl.pallas_call(
        body, name="ffn_act_fwd", grid=(B, NJ, NT),
        in_specs=[pl.BlockSpec((tm, tn), lambda b, j, t: (b * NT + t, j)),
                  pl.BlockSpec((tm, tn), lambda b, j, t: (b * NT + t, NJ + j)),
                  pl.BlockSpec((3, tn), lambda b, j, t: (0, j)), pl.BlockSpec((1, tn), lambda b, j, t: (0, j))],
        out_specs=[pl.BlockSpec((tm, tn), lambda b, j, t: (b * NT + t, j))] * 2, out_shape=[S((N, dff), BF16)] * 2,
        scratch_shapes=[pltpu.VMEM((8, tn), F32)], compiler_params=_cp(("parallel", "parallel", "arbitrary")),
    )(up, up, cw, cb)


def _ffn_down_loss(y, wd, x1, tgt, tm=256):
    n, d = x1.shape
    kf = y.shape[1]

    def body(y_ref, w_ref, x_ref, t_ref, dx_ref, ls_ref):
        err = x_ref[...] + _nn(y_ref[...], w_ref[...]) - t_ref[...]
        dx_ref[...] = err * (1.0 / d)

        @pl.when(pl.program_id(0) == 0)
        def _():
            ls_ref[...] = jnp.zeros_like(ls_ref)

        ls_ref[...] += _rowsum8(err * err) * (0.5 / d)

    xt = pl.BlockSpec((tm, d), lambda i: (i, 0))
    return pl.pallas_call(
        body, name="ffn_down_loss", grid=(n // tm,),
        in_specs=[pl.BlockSpec((tm, kf), lambda i: (i, 0)), pl.BlockSpec((kf, d), lambda i: (0, 0)), xt, xt],
        out_specs=[xt, pl.BlockSpec((8, d), lambda i: (0, 0))], out_shape=[S((n, d), F32), S((8, d), F32)],
        compiler_params=_cp(("arbitrary",)),
    )(y, wd, x1, tgt)


def _ffn_act_bwd1(dx2, wd, up, cdf, cw, cb, B, T, tm=256):
    N = B * T
    d = dx2.shape[1]
    dff = cw.shape[1]
    NT, NJ, tn = T // tm, dff // FFN_TN, FFN_TN

    def body(dx_ref, w_ref, a_ref, v_ref, c_ref, cw_ref, cb_ref, dac_ref, dv_ref, dcw_ref, dcb_ref, carry):
        b, t = pl.program_id(1), pl.program_id(2)
        a = a_ref[...].astype(F32)
        a1, a2 = _conv_shifted(a, carry[...], t == 0, tm)
        carry[...] = a[tm - 8:tm, :]
        w = cw_ref[...]
        ac = w[0:1, :] * a2 + w[1:2, :] * a1 + w[2:3, :] * a + cb_ref[...]
        dy = _nt(dx_ref[...].astype(BF16), w_ref[...])
        cdf = c_ref[...].astype(F32)
        dv_ref[...] = (dy * ac * cdf).astype(BF16)
        dac = dy * v_ref[...].astype(F32) * (cdf + ac * jnp.exp(-0.5 * ac * ac) * INV_SQRT_2PI)
        dac_ref[...] = dac

        @pl.when((b == 0) & (t == 0))
        def _():
            dcw_ref[...] = jnp.zeros_like(dcw_ref)
            dcb_ref[...] = jnp.zeros_like(dcb_ref)

        dcw_ref[0:8, :] += _rowsum8(dac * a2)
        dcw_ref[8:16, :] += _rowsum8(dac * a1)
        dcw_ref[16:24, :] += _rowsum8(dac * a)
        dcb_ref[...] += _rowsum8(dac)

    return pl.pallas_call(
        body, name="ffn_act_bwd1", grid=(NJ, B, NT),
        in_specs=[pl.BlockSpec((tm, d), lambda j, b, t: (b * NT + t, 0)), pl.BlockSpec((tn, d), lambda j, b, t: (j, 0)),
                  pl.BlockSpec((tm, tn), lambda j, b, t: (b * NT + t, j)),
                  pl.BlockSpec((tm, tn), lambda j, b, t: (b * NT + t, NJ + j)),
                  pl.BlockSpec((tm, tn), lambda j, b, t: (b * NT + t, j)),
                  pl.BlockSpec((3, tn), lambda j, b, t: (0, j)), pl.BlockSpec((1, tn), lambda j, b, t: (0, j))],
        out_specs=[pl.BlockSpec((tm, tn), lambda j, b, t: (b * NT + t, j)),
                   pl.BlockSpec((tm, tn), lambda j, b, t: (b * NT + t, j)),
                   pl.BlockSpec((24, tn), lambda j, b, t: (0, j)), pl.BlockSpec((8, tn), lambda j, b, t: (0, j))],
        out_shape=[S((N, dff), F32), S((N, dff), BF16), S((24, dff), F32), S((8, dff), F32)],
        scratch_shapes=[pltpu.VMEM((8, tn), F32)], compiler_params=_cp(("parallel", "arbitrary", "arbitrary")),
    )(dx2, wd, up, up, cdf, cw, cb)


def _ffn_act_bwd2(dac, cw, B, T, tm=256):
    N = B * T
    dff = cw.shape[1]
    NT, NJ, tn = T // tm, dff // FFN_TN, FFN_TN
    last8 = N // 8 - 1

    def body(d_ref, nx_ref, cw_ref, da_ref):
        t = pl.program_id(2)
        dd = d_ref[...]
        row = _iota(dd.shape, 0)
        last = t == NT - 1
        n0 = jnp.where(last, 0.0, nx_ref[0:1, :])
        n1 = jnp.where(last, 0.0, nx_ref[1:2, :])
        d1 = jnp.where(row == tm - 1, n0, pltpu.roll(dd, tm - 1, 0))
        d2 = jnp.where(row == tm - 1, n1, jnp.where(row == tm - 2, n0, pltpu.roll(dd, tm - 2, 0)))
        w = cw_ref[...]
        da_ref[...] = (w[2:3, :] * dd + w[1:2, :] * d1 + w[0:1, :] * d2).astype(BF16)

    return pl.pallas_call(
        body, name="ffn_act_bwd2", grid=(B, NJ, NT),
        in_specs=[pl.BlockSpec((tm, tn), lambda b, j, t: (b * NT + t, j)),
                  pl.BlockSpec((8, tn), lambda b, j, t: (jnp.minimum((b * NT + t + 1) * (tm // 8), last8), j)),
                  pl.BlockSpec((3, tn), lambda b, j, t: (0, j))],
        out_specs=pl.BlockSpec((tm, tn), lambda b, j, t: (b * NT + t, j)), out_shape=S((N, dff), BF16),
        compiler_params=_cp(("parallel", "parallel", "parallel")),
    )(dac, dac, cw)


def _fold_rows(p, name):
    r, c = p.shape[0] // 8, p.shape[1]

    def body(p_ref, o_ref):
        for j in range(r):
            o_ref[j:j + 1, :] = jnp.sum(p_ref[8 * j:8 * (j + 1), :], axis=0, keepdims=True)

    return pl.pallas_call(body, name=name, out_shape=S((r, c), F32), compiler_params=_cp())(p)


def _small_reduce(lbl, dg_mix, dg_mem, dlb_p, dgn_p, dfb_p, dgq_p, dgk_p, dmq_p, dmk_p, dg_ffn, dcb_p, loss_p):
    d, dff = dg_mix.shape[1], dcb_p.shape[1]
    nbh = dlb_p.shape[0] // (8 * HG_H)

    def colsum(ref):
        return jnp.sum(ref[...], axis=0, keepdims=True)

    def body(lbl_ref, mix_ref, mem_ref, dlb_ref, dgn_ref, dfb_ref, dgq_ref, dgk_ref, dmq_ref, dmk_ref, ffn_ref, dcb_ref,
             ls_ref, o_mix, o_mem, o_lb, o_hgn, o_fb, o_fq, o_fk, o_mq, o_mk, o_ffn, o_cb, o_loss):
        o_mix[...], o_mem[...], o_ffn[...], o_cb[...] = colsum(mix_ref), colsum(mem_ref), colsum(ffn_ref), colsum(dcb_ref)
        o_hgn[...], o_fb[...], o_mq[...], o_mk[...] = colsum(dgn_ref), colsum(dfb_ref), colsum(dmq_ref), colsum(dmk_ref)
        for src, dst in ((dgq_ref, o_fq), (dgk_ref, o_fk)):
            v = colsum(src)
            dst[...] = v + pltpu.roll(v, FOX_D, 1)
        o_loss[...] = jnp.zeros((1, LANE), F32) + jnp.sum(colsum(ls_ref), axis=-1, keepdims=True)
        logits = lbl_ref[...]
        e = jnp.exp(logits - jnp.max(logits, axis=0, keepdims=True))
        pr = e / jnp.sum(e, axis=0, keepdims=True)
        rows = _iota((8, LANE), 0)
        for h in range(HG_H):
            acc = jnp.zeros((8, LANE), F32)
            for b in range(nbh):
                acc = acc + dlb_ref[8 * (b * HG_H + h):8 * (b * HG_H + h + 1), :]
            dlb = jnp.sum(acc, axis=0, keepdims=True)
            c = slice(LANE * h, LANE * (h + 1))
            p0 = pr[0:1, c]
            first = _iota((logits.shape[0], LANE), 0) == 0
            o_lb[:, c] = pr[:, c] * (jnp.where(first, 1.0, 0.0) - p0) * dlb

    outs = [S((1, d), F32), S((1, d), F32), S(lbl.shape, F32)] + [S((1, LANE), F32)] * 6 + \
           [S((1, d), F32), S((1, dff), F32), S((1, LANE), F32)]
    return pl.pallas_call(body, name="small_reduce", out_shape=outs, compiler_params=_cp())(
        lbl, dg_mix, dg_mem, dlb_p, dgn_p, dfb_p, dgq_p, dgk_p, dmq_p, dmk_p, dg_ffn, dcb_p, loss_p)


def _in_col_pieces():
    hw, fw = HG_H * HG_D, FOX_H * FOX_D
    fox0, ff0 = 4 * hw, 4 * hw + 3 * fw
    mq0 = ff0 + FOX_H
    gate0 = mq0 + MEM_H * MEM_D
    pieces = []
    for p in range(FOX_P):
        pieces += [(fox0 + j * fw + LANE * p, LANE) for j in range(3)]
    pieces.append((mq0, MEM_H * MEM_D))
    for h in range(HG_H):
        pieces += [(j * hw + HG_D * h, HG_D) for j in range(4)]
    pieces.append((gate0, C_FF - C_GATE))
    pieces.append((ff0, FOX_H))
    return pieces


def _perm_from_blocks(blocks):
    n_blk, _, c = blocks.shape
    parts = []
    for s, n in _in_col_pieces():
        lo = s
        while lo < s + n:
            d = lo // c
            hi = min(s + n, (d + 1) * c)
            parts.append(blocks[d][:, lo - d * c:hi - d * c])
            lo = hi
    parts.append(jnp.zeros((blocks.shape[1], C_END - C_FF - FOX_H), blocks.dtype))
    return jnp.concatenate(parts, axis=1)


def _unperm_blocks(segs, n_blk):
    starts = [0]
    for a in segs:
        starts.append(starts[-1] + a.shape[1])
    new_start, placed = 0, []
    for s, n in _in_col_pieces():
        placed.append((s, new_start, n))
        new_start += n
    placed.sort()
    c = sum(n for _, _, n in placed) // n_blk
    blocks = []
    for d in range(n_blk):
        parts = []
        for s, ns, n in placed:
            lo, hi = max(s, d * c), min(s + n, (d + 1) * c)
            if lo < hi:
                i = max(j for j in range(len(segs)) if starts[j] <= ns)
                parts.append(segs[i][:, ns + lo - s - starts[i]:ns + hi - s - starts[i]])
        blocks.append(jnp.concatenate(parts, axis=1))
    return jnp.stack(blocks)


def _local_step(x2, mem2, tgt, sm, W, B, T, M, ex=None):
    fbias = jnp.pad(sm["fox_f_bias"], ((0, 0), (0, LANE - FOX_H)))
    gq2 = jnp.concatenate([sm["fox_q_norm_g"]] * 2, axis=1)
    gk2 = jnp.concatenate([sm["fox_k_norm_g"]] * 2, axis=1)
    lbl = sm["hgrn_lb_logits"]
    h = _rmsnorm_cast(x2, sm["norm_mix_g"], "norm_mix")
    z = _mm_nn(h, W["w_in"], BF16, "proj_in", 512, 2432)
    memn = _rmsnorm_cast(mem2, sm["norm_mem_g"], "norm_mem", tm=256)
    memkv = _mm_nn(memn, W["mem_kv_w"], F32, "proj_memkv", 256, 512)
    ya, o_raw, states = _hgrn_fwd(z, lbl, sm["hgrn_norm_g"], B, T)
    fc, fct = _fox_gate_fwd(z, fbias, B, T)
    yb, lse, *late = _fox_fwd(z, fc, fct, gq2, gk2, B, T, gather=ex.late_blocks() if ex else ())
    if ex:
        W = {**W, **ex.unpack_late(late)}
    yc = _mem_fwd(z, memkv, sm["mem_q_norm_g"], sm["mem_k_norm_g"], B, T, M)
    x1, merged, ua, ub, uc = _merge_fwd(ya, yb, yc, z, x2, W["w_br_hgrn"], W["w_br_fox"], W["w_br_mem"], W["w_out"])
    h2 = _rmsnorm_cast(x1, sm["norm_ffn_g"], "norm_ffn")
    up = _mm_nn(h2, W["ffn_w_up"], BF16, "ffn_up", 512, FFN_TN)
    yf, cdf = _ffn_act_fwd(up, W["ffn_conv_w"], sm["ffn_conv_b"], B, T)
    dx2, loss_p = _ffn_down_loss(yf, W["ffn_w_down"], x1, tgt)
    dff = W["ffn_conv_w"].shape[1]
    dac, dv, dcw_p, dcb_p = _ffn_act_bwd1(dx2, W["ffn_w_down"], up, cdf, W["ffn_conv_w"], sm["ffn_conv_b"], B, T)
    da = _ffn_act_bwd2(dac, W["ffn_conv_w"], B, T)
    g = {"ffn_conv_w": _fold_rows(dcw_p, "g_conv_w")}
    g["ffn_w_down"] = _mm_tn(yf, dx2, "g_w_down", TN_TM, 512)
    dh2 = _mm_nt_sum([(da, 0, dff, 0), (dv, 0, dff, dff)], W["ffn_w_up"], "dh2", 256)
    g["ffn_w_up"] = [_mm_tn(h2, da, "g_w_up_a", TN_TM, FFN_TN), _mm_tn(h2, dv, "g_w_up_v", TN_TM, FFN_TN)]
    dx1, dg_ffn = _rmsnorm_bwd(dh2, x1, sm["norm_ffn_g"], dx2, "norm_ffn_bwd")
    g["w_out"] = _mm_tn(merged, dx1, "g_w_out", TN_TM, 512)
    dgate, dya, dyb, dyc, dua, dub, duc = _merge_bwd(dx1, z, ua, ub, uc, W["w_br_hgrn"], W["w_br_fox"], W["w_br_mem"],
                                                    W["w_out"])
    g["w_br_hgrn"] = _mm_tn(ya, dua, "g_w_br_hgrn", TN_TM, 512)
    g["w_br_fox"] = _mm_tn(yb, dub, "g_w_br_fox", TN_TM, 512)
    g["w_br_mem"] = _mm_tn(yc, duc, "g_w_br_mem", TN_TM, 512)
    early_pk = ex.early_grads(g) if ex else ()
    dz_hg, dlb_p, dgn_p, *early_sib = _hgrn_bwd(z, o_raw, states, dya, lbl, sm["hgrn_norm_g"], B, T,
                                                swap_sibling=early_pk)
    dz_fox, dfc, dgq_p, dgk_p, *early_chips = _fox_bwd(z, dyb, yb, lse, fc, fct, gq2, gk2, B, T,
                                                       swap=ex.pair_sums(early_pk, early_sib, "early") if ex else ())
    dz_ff, dfb_p = _fox_gate_bwd(dfc, z, fbias, B, T)
    dz_mq, dkv, dmq_p, dmk_p = _mem_bwd(z, memkv, dyc, sm["mem_q_norm_g"], sm["mem_k_norm_g"], B, T, M)
    g["mem_kv_w"] = _mm_tn(memn, dkv, "g_mem_kv_w", 256, 512)
    dmemn = _mm_nt(dkv, W["mem_kv_w"], "d_memn", 256, 512)
    _, dg_mem = _rmsnorm_bwd(dmemn, mem2, sm["norm_mem_g"], None, "norm_mem_bwd", tm=256)
    d = x2.shape[1]
    parts = [(dz_fox, 0, C_MQ - C_FOX, C_FOX), (dz_mq, 0, C_HG - C_MQ, C_MQ), (dz_hg, 0, C_GATE - C_HG, C_HG)]
    parts += [(dgate, d * k, d, C_GATE + d * k) for k in range(3)] + [(dz_ff, 0, C_END - C_FF, C_FF)]
    g["w_in"] = [_mm_tn(h, dzs, "g_w_in_%d" % i, TN_TM, min(512, dzs.shape[1]))
                 for i, dzs in enumerate((dz_fox, dz_mq, dz_hg, dgate, dz_ff))]
    sums = None
    if ex:
        last_pk = ex.last_grads(g)
        last_sib = _swap_with_sibling(last_pk, "rs_sibling_last")
        dh, last_chips = _mm_nt_sum(parts, W["w_in"], "dh", 256, swap=ex.pair_sums(last_pk, last_sib, "last"))
        sums = (ex.final_sums(early_pk, early_sib, early_chips, "early"),
                ex.final_sums(last_pk, last_sib, last_chips, "last"))
    else:
        dh = _mm_nt_sum(parts, W["w_in"], "dh", 256)
    grad_x, dg_mix = _rmsnorm_bwd(dh, x2, sm["norm_mix_g"], dx1, "norm_mix_bwd")
    small = _small_reduce(lbl, dg_mix, dg_mem, dlb_p, dgn_p, dfb_p, dgq_p, dgk_p, dmq_p, dmk_p, dg_ffn, dcb_p, loss_p)
    names = ("norm_mix_g", "norm_mem_g", "hgrn_lb_logits", "hgrn_norm_g", "fox_f_bias", "fox_q_norm_g", "fox_k_norm_g",
             "mem_q_norm_g", "mem_k_norm_g", "norm_ffn_g", "ffn_conv_b", "loss")
    g.update(dict(zip(names, small)))
    return grad_x, g, sums


ANY = pl.BlockSpec(memory_space=pl.ANY)


def _position():
    return lax.axis_index("x"), lax.axis_index("y"), lax.axis_index("c")


def _all_gather(blocks, name):
    nb = len(blocks)

    def body(*refs):
        start, forward, finish = _gather_phases(refs[:nb], refs[nb:2 * nb], *refs[2 * nb:])
        start()
        forward()
        finish()

    return pl.pallas_call(
        body, name=name, out_shape=_gather_shapes(blocks), in_specs=[ANY] * nb, out_specs=[ANY] * nb,
        scratch_shapes=_gather_sems(nb),
    )(*blocks)


def _hosting(body, n_in, n_out, n_scratch, n_x, make_phases, grid):
    n_steps = math.prod(grid)

    def hosted(*refs):
        a = n_in + n_x
        b = a + n_out + n_x
        ins, xs = refs[:n_in], refs[n_in:a]
        outs, x_outs = refs[a:a + n_out], refs[a + n_out:b]
        scratch, sems = refs[b:b + n_scratch], refs[b + n_scratch:]
        step = 0
        for ax, n in enumerate(grid):
            step = step * n + pl.program_id(ax)
        phases = make_phases(xs, x_outs, *sems)
        pl.when(step == 0)(phases[0])
        for ph in phases[1:-1]:
            pl.when(step == n_steps // 2)(ph)
        body(*ins, *outs, *scratch)
        pl.when(step == n_steps - 1)(phases[-1])

    return hosted


def _gather_shapes(blocks):
    return [S((N_DEV,) + b.shape, b.dtype) for b in blocks]


def _gather_sems(nb):
    return [pltpu.SemaphoreType.DMA((7 * nb,)), pltpu.SemaphoreType.DMA((7 * nb,)), pltpu.SemaphoreType.DMA((nb,))]


def _gather_phases(x_refs, out_refs, send_sems, recv_sems, local_sems):
    nb = len(x_refs)
    x, y, c = _position()
    me, sibling = (x, y, c), (x, y, 1 - c)
    chips = [(1 - x, y), (x, 1 - y), (1 - x, 1 - y)]

    def copy(i, k, blk, to, own=False):
        px, py, pc = blk
        slot = out_refs[i].at[4 * px + 2 * py + pc]
        return pltpu.make_async_remote_copy(
            src_ref=x_refs[i] if own else slot, dst_ref=slot, send_sem=send_sems.at[7 * i + k],
            recv_sem=recv_sems.at[7 * i + k], device_id=to, device_id_type=MESH)

    def mine(i):
        return pltpu.make_async_copy(x_refs[i], out_refs[i].at[4 * x + 2 * y + c], local_sems.at[i])

    def first(i):
        return [copy(i, 0, me, sibling, own=True)] + [copy(i, 1 + j, me, (*chip, c), own=True)
                                                     for j, chip in enumerate(chips)]

    def passed(i, j):
        return copy(i, 4 + j, (*chips[j], c), sibling)

    def start():
        for i in range(nb):
            mine(i).start()
            for cp in first(i):
                cp.start()

    def forward():
        for i in range(nb):
            for j, chip in enumerate(chips):
                copy(i, 1 + j, (*chip, c), me).wait_recv()
                passed(i, j).start()

    def finish():
        for i in range(nb):
            copy(i, 0, sibling, me).wait_recv()
            for j, chip in enumerate(chips):
                copy(i, 4 + j, (*chip, 1 - c), me).wait_recv()
        for i in range(nb):
            for cp in first(i) + [passed(i, j) for j in range(3)]:
                cp.wait_send()
            mine(i).wait()

    return start, forward, finish


def _swap_with_sibling(pks, name):
    nb = len(pks)

    def body(*refs):
        start, finish = _sibling_swap_phases(refs[:nb], refs[nb:2 * nb], *refs[2 * nb:])
        start()
        finish()

    return pl.pallas_call(
        body, name=name, out_shape=_sibling_swap_shapes(pks), in_specs=[ANY] * nb, out_specs=[ANY] * nb,
        scratch_shapes=_sibling_swap_sems(nb),
    )(*pks)


def _sibling_swap_shapes(pks):
    return [S((4,) + p.shape[1:], p.dtype) for p in pks]


def _sibling_swap_sems(nb):
    return [pltpu.SemaphoreType.DMA((4 * nb,)), pltpu.SemaphoreType.DMA((4 * nb,))]


def _sibling_swap_phases(pk_refs, out_refs, send_sems, recv_sems):
    nb = len(pk_refs)
    x, y, c = _position()

    def copies():
        return [pltpu.make_async_remote_copy(
            src_ref=pk_refs[i].at[2 * k + 1 - c], dst_ref=out_refs[i].at[k], send_sem=send_sems.at[4 * i + k],
            recv_sem=recv_sems.at[4 * i + k], device_id=(x, y, 1 - c), device_id_type=MESH)
            for i in range(nb) for k in range(4)]

    def start():
        for cp in copies():
            cp.start()

    def finish():
        for cp in copies():
            cp.wait()

    return start, finish


def _swap_between_chips(pbs, name):
    nb = len(pbs)

    def body(*refs):
        start, finish = _chip_swap_phases(refs[:nb], refs[nb:2 * nb], *refs[2 * nb:])
        start()
        finish()

    return pl.pallas_call(
        body, name=name, out_shape=[S(p.shape, p.dtype) for p in pbs], in_specs=[ANY] * nb, out_specs=[ANY] * nb,
        scratch_shapes=_chip_swap_sems(nb),
    )(*pbs)


def _chip_swap_sems(nb):
    return [pltpu.SemaphoreType.DMA((3 * nb,)), pltpu.SemaphoreType.DMA((3 * nb,)), pltpu.SemaphoreType.DMA((nb,))]


def _chip_swap_phases(pb_refs, out_refs, send_sems, recv_sems, local_sems):
    nb = len(pb_refs)
    x, y, c = _position()
    me = 2 * x + y
    chips = [(1 - x, y), (x, 1 - y), (1 - x, 1 - y)]

    def local(i):
        return pltpu.make_async_copy(pb_refs[i].at[me], out_refs[i].at[me], local_sems.at[i])

    def send(i, j):
        cx, cy = chips[j]
        return pltpu.make_async_remote_copy(
            src_ref=pb_refs[i].at[2 * cx + cy], dst_ref=out_refs[i].at[me], send_sem=send_sems.at[3 * i + j],
            recv_sem=recv_sems.at[3 * i + j], device_id=(cx, cy, c), device_id_type=MESH)

    def arrival(i, j):
        cx, cy = chips[j]
        return pltpu.make_async_remote_copy(
            src_ref=pb_refs[i].at[me], dst_ref=out_refs[i].at[2 * cx + cy], send_sem=send_sems.at[3 * i + j],
            recv_sem=recv_sems.at[3 * i + j], device_id=(cx, cy, c), device_id_type=MESH)

    def start():
        for i in range(nb):
            local(i).start()
            for j in range(3):
                send(i, j).start()

    def finish():
        for i in range(nb):
            for j in range(3):
                arrival(i, j).wait_recv()
        for i in range(nb):
            for j in range(3):
                send(i, j).wait_send()
            local(i).wait()

    return start, finish


def _row_tile(r):
    return max(t for t in range(16, min(r, 512) + 1, 16) if r % t == 0)


def _pair_sum_cast(pk, recv, core, name):
    _, r, l = pk.shape
    tr = _row_tile(r)

    def body(c_ref, a_ref, b_ref, o_ref):
        o_ref[...] = (a_ref[...] + b_ref[...]).astype(BF16)

    return pl.pallas_call(
        body, name=name,
        grid_spec=pltpu.PrefetchScalarGridSpec(
            num_scalar_prefetch=1, grid=(4, r // tr),
            in_specs=[pl.BlockSpec((None, tr, l), lambda k, i, c: (2 * k + c[0], i, 0)),
                      pl.BlockSpec((None, tr, l), lambda k, i, c: (k, i, 0))],
            out_specs=pl.BlockSpec((None, tr, l), lambda k, i, c: (k, i, 0))),
        out_shape=S((4, r, l), BF16), compiler_params=_cp(("parallel", "parallel")),
    )(core, pk, recv)


def _final_sum(pk, recv_sib, recv_chips, slot, chip, name):
    _, r, l = pk.shape
    tr = _row_tile(r)

    def body(s_ref, k_ref, a_ref, b_ref, rc_ref, o_ref):
        base = a_ref[...] + b_ref[...]
        acc = jnp.zeros_like(base)
        for j in range(4):
            acc = acc + jnp.where(k_ref[0] == j, base, rc_ref[j].astype(F32))
        o_ref[...] = acc

    return pl.pallas_call(
        body, name=name,
        grid_spec=pltpu.PrefetchScalarGridSpec(
            num_scalar_prefetch=2, grid=(r // tr,),
            in_specs=[pl.BlockSpec((None, tr, l), lambda i, s, k: (s[0], i, 0)),
                      pl.BlockSpec((None, tr, l), lambda i, s, k: (k[0], i, 0)),
                      pl.BlockSpec((4, tr, l), lambda i, s, k: (0, i, 0))],
            out_specs=pl.BlockSpec((tr, l), lambda i, s, k: (i, 0))),
        out_shape=S((r, l), F32), compiler_params=_cp(("parallel",)),
    )(slot, chip, pk, recv_sib, recv_chips)


def _adamw_math(w, g, m, v):
    m = ADAM_B1 * m + (1.0 - ADAM_B1) * g
    v = ADAM_B2 * v + (1.0 - ADAM_B2) * (g * g)
    m_hat = m / (1.0 - ADAM_B1 ** ADAM_STEP)
    v_hat = v / (1.0 - ADAM_B2 ** ADAM_STEP)
    return -ADAM_LR * (m_hat / (jnp.sqrt(v_hat) + ADAM_EPS) + ADAM_WD * w), m, v


def _adamw(w, g, m, v, name):
    r, c = w.shape
    tr = 256 if r % 256 == 0 else r

    def body(w_ref, g_ref, m_ref, v_ref, d_ref, nm_ref, nv_ref):
        d_ref[...], nm_ref[...], nv_ref[...] = _adamw_math(w_ref[...], g_ref[...], m_ref[...], v_ref[...])

    tile = pl.BlockSpec((tr, c), lambda i: (i, 0))
    return pl.pallas_call(
        body, name=name, grid=(r // tr,), in_specs=[tile] * 4, out_specs=[tile] * 3, out_shape=[S((r, c), F32)] * 3,
        compiler_params=_cp(("parallel",)),
    )(w, g, m, v)


def _small_update(gathered, w, m, v):
    def body(ga_ref, w_ref, m_ref, v_ref, g_ref, d_ref, nm_ref, nv_ref):
        g = ga_ref[0]
        for k in range(1, N_DEV):
            g = g + ga_ref[k]
        g_ref[...] = g
        d_ref[...], nm_ref[...], nv_ref[...] = _adamw_math(w_ref[...], g, m_ref[...], v_ref[...])

    return pl.pallas_call(body, name="small_update", out_shape=[S(w.shape, F32)] * 4, compiler_params=_cp())(
        gathered, w, m, v)


BIG = ("w_in", "mem_kv_w", "w_br_hgrn", "w_br_fox", "w_br_mem", "w_out", "ffn_w_up", "ffn_conv_w", "ffn_w_down")
GROUP_ROWS = ("w_out", "ffn_w_down")
GROUP_LANE = ("w_br_hgrn", "w_br_fox", "w_br_mem")
LANE_GROUP_ROWS = 224
SMALL = ("norm_mix_g", "norm_mem_g", "hgrn_lb_logits", "hgrn_norm_g", "fox_f_bias", "fox_q_norm_g", "fox_k_norm_g",
         "mem_q_norm_g", "mem_k_norm_g", "norm_ffn_g", "ffn_conv_b")


def _rows_of(n_elems):
    return -(-n_elems // LANE)


def _to_rows(a, lead=0):
    flat = a.reshape(a.shape[:lead] + (-1,))
    pad = (-flat.shape[-1]) % LANE
    if pad:
        flat = jnp.pad(flat, [(0, 0)] * lead + [(0, pad)])
    return flat.reshape(a.shape[:lead] + (-1, LANE))


def _stack_rows(parts, lead, total_rows):
    buf = jnp.concatenate(parts, axis=lead)
    pad = total_rows - buf.shape[lead]
    return jnp.pad(buf, [(0, 0)] * lead + [(0, pad), (0, 0)])


def _round_up(n, k):
    return -(-n // k) * k


def _from_rows(rows, shape, lead=0):
    n = math.prod(shape)
    return rows.reshape(rows.shape[:lead] + (-1,))[..., :n].reshape(rows.shape[:lead] + tuple(shape))


def _blocks_to_full(blocks, kind):
    n, a, b = blocks.shape
    return blocks.transpose(1, 0, 2).reshape(a, n * b) if kind == "col" else blocks.reshape(n * a, b)


def _full_to_blocks(full, kind, n=N_DEV):
    a, b = full.shape
    return full.reshape(a, n, b // n).transpose(1, 0, 2) if kind == "col" else full.reshape(n, a // n, b)


def _lane_group_rows(shard):
    n_lane = sum(shard[n].shape[0] for n in GROUP_LANE)
    n_cw = shard["ffn_conv_w"].size
    return n_lane, _rows_of(3 * n_cw), _rows_of(n_cw), _round_up(n_lane + _rows_of(3 * n_cw), LANE_GROUP_ROWS)


def _split_bf16x3(x):
    hi = x.astype(BF16)
    r1 = x - hi.astype(F32)
    mid = r1.astype(BF16)
    return jnp.stack([hi, mid, (r1 - mid.astype(F32)).astype(BF16)])


class _Exchange:
    def __init__(self, shard):
        self.shard = shard
        xi, yi, ci = _position()
        self.core = ci.astype(jnp.int32).reshape(1)
        self.chip = (2 * xi + yi).astype(jnp.int32).reshape(1)
        self.n_lane, self.r_pieces, self.r_vals, self.r_lane = _lane_group_rows(shard)

    def first_blocks(self):
        return [self.shard["w_in"].astype(BF16), self.shard["mem_kv_w"].astype(BF16)]

    def unpack_first(self, gathered):
        return {"w_in": _perm_from_blocks(gathered[0]), "mem_kv_w": _blocks_to_full(gathered[1], "row")}

    def late_blocks(self):
        sh = self.shard
        lane_rows = [sh[n].astype(BF16) for n in GROUP_LANE] + [_to_rows(_split_bf16x3(sh["ffn_conv_w"]))]
        return [sh[n].astype(BF16) for n in GROUP_ROWS] + [sh["ffn_w_up"].astype(BF16),
                                                           _stack_rows(lane_rows, 0, self.r_lane)]

    def unpack_late(self, gathered):
        *rows, gc, gd = gathered
        sh = self.shard
        W = {"ffn_w_up": _blocks_to_full(gc, "col")}
        for n, blocks in zip(GROUP_ROWS, rows):
            W[n] = _blocks_to_full(blocks, "row")
        r0 = 0
        for n in GROUP_LANE:
            W[n] = _blocks_to_full(gd[:, r0:r0 + sh[n].shape[0]], "col")
            r0 += sh[n].shape[0]
        cw = _from_rows(gd[:, self.n_lane:self.n_lane + self.r_pieces], (3,) + sh["ffn_conv_w"].shape, lead=1).astype(F32)
        W["ffn_conv_w"] = _blocks_to_full(cw[:, 0] + cw[:, 1] + cw[:, 2], "col")
        return W

    def early_grads(self, g):
        cw_rows = _to_rows(_full_to_blocks(g["ffn_conv_w"], "col"), lead=1)
        return [_full_to_blocks(g[n], "row") for n in GROUP_ROWS] + [
            jnp.concatenate([_full_to_blocks(h, "col", N_DEV // 2) for h in g["ffn_w_up"]], axis=0),
            _stack_rows([_full_to_blocks(g[n], "col") for n in GROUP_LANE] + [cw_rows], 1, self.r_lane)]

    def last_grads(self, g):
        return [_unperm_blocks(g["w_in"], N_DEV), _full_to_blocks(g["mem_kv_w"], "row")]

    def pair_sums(self, pks, recv_sib, tag):
        return [_pair_sum_cast(p, r, self.core, "rs_pair_sum_%s%d" % (tag, i))
                for i, (p, r) in enumerate(zip(pks, recv_sib))]

    def final_sums(self, pks, recv_sib, recv_chips, tag):
        return [_final_sum(p, rs, rc, 2 * self.chip + self.core, self.chip, "rs_final_sum_%s%d" % (tag, i))
                for i, (p, rs, rc) in enumerate(zip(pks, recv_sib, recv_chips))]

    def unpack_grads(self, early, last):
        sh = self.shard
        *rows, g_up, g_lane = early
        g_shard = {"w_in": last[0], "mem_kv_w": last[1], "ffn_w_up": g_up, **dict(zip(GROUP_ROWS, rows))}
        r0 = 0
        for n in GROUP_LANE:
            g_shard[n] = g_lane[r0:r0 + sh[n].shape[0]]
            r0 += sh[n].shape[0]
        g_shard["ffn_conv_w"] = _from_rows(g_lane[self.n_lane:self.n_lane + self.r_vals], sh["ffn_conv_w"].shape)
        return g_shard


def kernel(x, mem, norm_mix_g, norm_mem_g, w_in, hgrn_lb_logits, hgrn_norm_g, fox_f_bias, fox_q_norm_g, fox_k_norm_g, mem_kv_w, mem_q_norm_g, mem_k_norm_g, w_br_hgrn, w_br_fox, w_br_mem, w_out, norm_ffn_g, ffn_w_up, ffn_conv_w, ffn_conv_b, ffn_w_down, loss_target, m_norm_mix_g, m_norm_mem_g, m_w_in, m_hgrn_lb_logits, m_hgrn_norm_g, m_fox_f_bias, m_fox_q_norm_g, m_fox_k_norm_g, m_mem_kv_w, m_mem_q_norm_g, m_mem_k_norm_g, m_w_br_hgrn, m_w_br_fox, m_w_br_mem, m_w_out, m_norm_ffn_g, m_ffn_w_up, m_ffn_conv_w, m_ffn_conv_b, m_ffn_w_down, v_norm_mix_g, v_norm_mem_g, v_w_in, v_hgrn_lb_logits, v_hgrn_norm_g, v_fox_f_bias, v_fox_q_norm_g, v_fox_k_norm_g, v_mem_kv_w, v_mem_q_norm_g, v_mem_k_norm_g, v_w_br_hgrn, v_w_br_fox, v_w_br_mem, v_w_out, v_norm_ffn_g, v_ffn_w_up, v_ffn_conv_w, v_ffn_conv_b, v_ffn_w_down):
    given = dict(locals())
    order = ("norm_mix_g", "norm_mem_g", "w_in", "hgrn_lb_logits", "hgrn_norm_g", "fox_f_bias", "fox_q_norm_g",
             "fox_k_norm_g", "mem_kv_w", "mem_q_norm_g", "mem_k_norm_g", "w_br_hgrn", "w_br_fox", "w_br_mem", "w_out",
             "norm_ffn_g", "ffn_w_up", "ffn_conv_w", "ffn_conv_b", "ffn_w_down")
    B, T, D = x.shape
    M = mem.shape[1]
    shard = {n: given[n][0] if n in BIG else given[n] for n in order}
    mom = {n: (given["m_" + n][0], given["v_" + n][0]) if n in BIG else (given["m_" + n], given["v_" + n])
           for n in order}
    shard["hgrn_lb_logits"] = hgrn_lb_logits
    for n in ("norm_mix_g", "norm_mem_g", "hgrn_norm_g", "fox_f_bias", "fox_q_norm_g", "fox_k_norm_g", "mem_q_norm_g",
              "mem_k_norm_g", "norm_ffn_g", "ffn_conv_b"):
        shard[n] = given[n].reshape(1, -1)

    ex = _Exchange(shard)
    W = ex.unpack_first(_all_gather(ex.first_blocks(), "ag_first"))

    sm = {n: shard[n] for n in SMALL}
    grad_x, g, sums = _local_step(x.reshape(B * T, D), mem.reshape(B * M, D), loss_target.reshape(B * T, D), sm, W,
                                  B, T, M, ex)
    g_shard = ex.unpack_grads(*sums)

    sg = {n: g[n] for n in SMALL}
    sg["fox_f_bias"] = g["fox_f_bias"][:, :FOX_H]
    sg["fox_q_norm_g"] = g["fox_q_norm_g"][:, :FOX_D]
    sg["fox_k_norm_g"] = g["fox_k_norm_g"][:, :FOX_D]
    slayout, row0 = {}, 0
    for n in SMALL:
        nr = _rows_of(shard[n].size)
        slayout[n] = (row0, nr)
        row0 += nr
    loss_row = row0
    r_small = _round_up(row0 + 1, 8)

    def pack_small(d, with_loss=None):
        rows = [_to_rows(d[n]) for n in SMALL]
        rows.append(with_loss if with_loss is not None else jnp.zeros((1, LANE), F32))
        return _stack_rows(rows, 0, r_small)

    sgath, = _all_gather([pack_small(sg, g["loss"])], "ag_small")
    s_g, s_d, s_m, s_v = _small_update(sgath, pack_small(shard), pack_small({n: mom[n][0].reshape(shard[n].shape) for n in SMALL}),
                                       pack_small({n: mom[n][1].reshape(shard[n].shape) for n in SMALL}))
    loss = s_g[loss_row, 0]

    grads, deltas, new_m, new_v = {}, {}, {}, {}
    for n in BIG:
        gn = g_shard[n]
        d, nm, nv = _adamw(shard[n], gn, mom[n][0], mom[n][1], "adamw_" + n)
        grads[n], deltas[n], new_m[n], new_v[n] = (a[None] for a in (gn, d, nm, nv))
    for n in SMALL:
        r0, nr = slayout[n]
        for dst, src in ((grads, s_g), (deltas, s_d), (new_m, s_m), (new_v, s_v)):
            dst[n] = _from_rows(src[r0:r0 + nr], given[n].shape)
    return (loss, grad_x.reshape(B, T, D), *[grads[n] for n in order], *[deltas[n] for n in order],
            *[new_m[n] for n in order], *[new_v[n] for n in order])
```

```python
import functools
import math

import jax
import jax.numpy as jnp
from jax import lax
from jax.experimental import pallas as pl
from jax.experimental.pallas import tpu as pltpu

F32, BF16 = jnp.float32, jnp.bfloat16
S = jax.ShapeDtypeStruct
MESH = pl.DeviceIdType.MESH

N_DEV = 8
EPS = 1e-6
LANE = 128
CHUNK = 64
SUB = 16
HG_H, HG_D = 4, 128
HG_GROUP_FWD = 4
HG_GROUP = 2
FOX_H, FOX_D = 8, 64
FOX_P = FOX_H // 2
MEM_H, MEM_D = 4, 128
NEG = -1e30
VMEM_LIMIT = 56 * 2**20

ADAM_LR, ADAM_B1, ADAM_B2, ADAM_EPS, ADAM_WD, ADAM_STEP = 0.001, 0.9, 0.999, 1e-08, 0.01, 10

C_FOX, C_MQ, C_HG, C_GATE, C_FF, C_END = 0, 1536, 2048, 4096, 7168, 7296


def _cp(sem=None):
    return pltpu.CompilerParams(dimension_semantics=sem, vmem_limit_bytes=VMEM_LIMIT)


def _dot(a, b, dims, prec=None):
    return lax.dot_general(a, b, (dims, ((), ())), preferred_element_type=F32, precision=prec)


def _nn(a, b, prec=None):
    return _dot(a, b, ((1,), (0,)), prec)


def _nt(a, b, prec=None):
    return _dot(a, b, ((1,), (1,)), prec)


def _tn(a, b, prec=None):
    return _dot(a, b, ((0,), (0,)), prec)


def _b(x):
    return x.astype(BF16)


def _mm3(fn, a, b):
    ah, bh = _b(a), _b(b)
    return fn(ah, bh) + fn(ah, _b(b - bh.astype(F32))) + fn(_b(a - ah.astype(F32)), bh)


def _iota(shape, dim):
    return lax.broadcasted_iota(jnp.int32, shape, dim)


def _rowsum8(x):
    r, d = x.shape
    return jnp.sum(x.reshape(r // 8, 8, d), axis=0)


def _rmsnorm_cast(x, g, name, tm=512):
    n, d = x.shape

    def body(x_ref, g_ref, o_ref):
        v = x_ref[...]
        r = lax.rsqrt(jnp.mean(v * v, axis=-1, keepdims=True) + EPS)
        o_ref[...] = (v * r * g_ref[...]).astype(BF16)

    return pl.pallas_call(
        body, name=name, grid=(n // tm,),
        in_specs=[pl.BlockSpec((tm, d), lambda i: (i, 0)), pl.BlockSpec((1, d), lambda i: (0, 0))],
        out_specs=pl.BlockSpec((tm, d), lambda i: (i, 0)), out_shape=S((n, d), BF16), compiler_params=_cp(("parallel",)),
    )(x, g)


def _rmsnorm_bwd(dh, x, g, resid, name, tm=512):
    n, d = x.shape
    has_res = resid is not None

    def body(*refs):
        if has_res:
            dh_ref, x_ref, g_ref, r_ref, dx_ref, dg_ref = refs
        else:
            dh_ref, x_ref, g_ref, dx_ref, dg_ref = refs
        v = x_ref[...]
        dhv = dh_ref[...].astype(F32)
        r = lax.rsqrt(jnp.mean(v * v, axis=-1, keepdims=True) + EPS)
        xh = v * r
        u = dhv * g_ref[...]
        dx = r * (u - xh * jnp.mean(u * xh, axis=-1, keepdims=True))
        if has_res:
            dx = dx + r_ref[...]
        dx_ref[...] = dx

        @pl.when(pl.program_id(0) == 0)
        def _():
            dg_ref[...] = jnp.zeros_like(dg_ref)

        dg_ref[...] += _rowsum8(dhv * xh)

    tile = pl.BlockSpec((tm, d), lambda i: (i, 0))
    ins = [tile, tile, pl.BlockSpec((1, d), lambda i: (0, 0))] + ([tile] if has_res else [])
    args = (dh, x, g) + ((resid,) if has_res else ())
    return pl.pallas_call(
        body, name=name, grid=(n // tm,), in_specs=ins,
        out_specs=[tile, pl.BlockSpec((8, d), lambda i: (0, 0))],
        out_shape=[S((n, d), F32), S((8, d), F32)], compiler_params=_cp(("arbitrary",)),
    )(*args)


def _mm_nn(a, b, out_dtype, name, tm, tn):
    m, k = a.shape
    n = b.shape[1]
    assert n % tn == 0 and m % tm == 0

    def body(a_ref, b_ref, o_ref):
        o_ref[...] = _nn(a_ref[...].astype(BF16), b_ref[...].astype(BF16)).astype(out_dtype)

    return pl.pallas_call(
        body, name=name, grid=(n // tn, m // tm),
        in_specs=[pl.BlockSpec((tm, k), lambda j, i: (i, 0)), pl.BlockSpec((k, tn), lambda j, i: (0, j))],
        out_specs=pl.BlockSpec((tm, tn), lambda j, i: (i, j)), out_shape=S((m, n), out_dtype),
        compiler_params=_cp(("parallel", "parallel")),
    )(a, b)


def _mm_nt(dy, w, name, tm, tr, w_col0=0, acc=None):
    m, r = dy.shape
    k = w.shape[0]
    jb = w_col0 // tr
    nr = r // tr
    assert w_col0 % tr == 0 and r % tr == 0 and m % tm == 0
    has_acc = acc is not None

    def body(*refs):
        if has_acc:
            dy_ref, w_ref, acc_ref, o_ref = refs
        else:
            dy_ref, w_ref, o_ref = refs
        part = _nt(dy_ref[...].astype(BF16), w_ref[...].astype(BF16))

        @pl.when(pl.program_id(1) == 0)
        def _():
            o_ref[...] = part + acc_ref[...] if has_acc else part

        @pl.when(pl.program_id(1) > 0)
        def _():
            o_ref[...] += part

    out_tile = pl.BlockSpec((tm, k), lambda i, j: (i, 0))
    ins = [pl.BlockSpec((tm, tr), lambda i, j: (i, j)), pl.BlockSpec((k, tr), lambda i, j: (0, j + jb))]
    args = (dy, w)
    if has_acc:
        ins.append(out_tile)
        args = args + (acc,)
    return pl.pallas_call(
        body, name=name, grid=(m // tm, nr), in_specs=ins, out_specs=out_tile, out_shape=S((m, k), F32),
        input_output_aliases=({2: 0} if has_acc else {}), compiler_params=_cp(("parallel", "arbitrary")),
    )(*args)


def _mm_nt_sum(parts, w, name, tm, swap=()):
    m = parts[0][0].shape[0]
    k = w.shape[0]
    assert m % tm == 0 and all(c % n == 0 and o % n == 0 for _, c, n, o in parts)
    np_ = len(parts)
    nsw = len(swap)
    n_steps = m // tm

    def body(*refs):
        o_ref = refs[2 * np_ + nsw]
        if nsw:
            start, finish = _chip_swap_phases(refs[2 * np_:2 * np_ + nsw], refs[2 * np_ + nsw + 1:2 * np_ + 2 * nsw + 1],
                                              *refs[2 * np_ + 2 * nsw + 1:])
            pl.when(pl.program_id(0) == 0)(start)
        acc = _nt(refs[0][...].astype(BF16), refs[np_][...].astype(BF16))
        for i in range(1, np_):
            acc = acc + _nt(refs[i][...].astype(BF16), refs[np_ + i][...].astype(BF16))
        o_ref[...] = acc
        if nsw:
            pl.when(pl.program_id(0) == n_steps - 1)(finish)

    dy_specs = [pl.BlockSpec((tm, n), functools.partial(lambda i, j: (i, j), j=c // n)) for _, c, n, _ in parts]
    w_specs = [pl.BlockSpec((k, n), functools.partial(lambda i, j: (0, j), j=o // n)) for _, _, n, o in parts]
    out = pl.pallas_call(
        body, name=name, grid=(n_steps,), in_specs=dy_specs + w_specs + [ANY] * nsw,
        out_specs=[pl.BlockSpec((tm, k), lambda i: (i, 0))] + [ANY] * nsw,
        out_shape=[S((m, k), F32)] + [S(p.shape, p.dtype) for p in swap],
        scratch_shapes=_chip_swap_sems(nsw) if nsw else [],
        compiler_params=_cp(("arbitrary",) if nsw else ("parallel",)),
    )(*([p[0] for p in parts] + [w] * np_ + list(swap)))
    return (out[0], out[1:]) if nsw else out[0]


def _mm_tn(x, dy, name, tm, tn):
    m, k = x.shape
    n = dy.shape[1]
    tm = min(tm, m)
    assert m % tm == 0 and n % tn == 0

    def body(x_ref, dy_ref, o_ref):
        part = _tn(x_ref[...].astype(BF16), dy_ref[...].astype(BF16))

        @pl.when(pl.program_id(1) == 0)
        def _():
            o_ref[...] = part

        @pl.when(pl.program_id(1) > 0)
        def _():
            o_ref[...] += part

    return pl.pallas_call(
        body, name=name, grid=(n // tn, m // tm),
        in_specs=[pl.BlockSpec((tm, k), lambda j, i: (i, 0)), pl.BlockSpec((tm, tn), lambda j, i: (i, j))],
        out_specs=pl.BlockSpec((k, tn), lambda j, i: (0, j)), out_shape=S((k, n), F32),
        compiler_params=_cp(("parallel", "arbitrary")),
    )(x, dy)


def _lower_bound(logits):
    e = jnp.exp(logits - jnp.max(logits, axis=0, keepdims=True))
    return e[0:1, :] / jnp.sum(e, axis=0, keepdims=True)


def _hg_gates(fl, lb):
    sig = jax.nn.sigmoid(fl)
    f = lb + (1.0 - lb) * sig
    k = (1.0 - lb) * (1.0 - sig)
    return sig, f, k, jnp.log(f)


def _silu_and_grad(x):
    s = jax.nn.sigmoid(x)
    return x * s, s * (1.0 + x * (1.0 - s))


def _hg_rowblocks(G):
    return [None] + [G[SUB * i - 1:SUB * i, :] for i in range(1, CHUNK // SUB)]


def _hg_intra_A(qs, k, G):
    refs = _hg_rowblocks(G)
    cols = _iota((SUB, CHUNK), 1)
    rows = _iota((SUB, CHUNK), 0)
    blocks = []
    for i in range(CHUNK // SUB):
        lo = SUB * i
        qb, Gb = qs[lo:lo + SUB, :], G[lo:lo + SUB, :]
        diag = jnp.zeros((SUB, CHUNK), F32)
        for s in range(SUB):
            e = jnp.exp(jnp.minimum(Gb - G[lo + s:lo + s + 1, :], 0.0))
            col = jnp.sum(qb * k[lo + s:lo + s + 1, :] * e, axis=-1, keepdims=True)
            diag = jnp.where(cols == lo + s, col, diag)
        a = jnp.where((cols >= lo) & (cols <= rows + lo), diag, 0.0)
        if i > 0:
            qr = qb * jnp.exp(Gb - refs[i])
            kr = k * jnp.exp(jnp.minimum(refs[i] - G, 0.0))
            a = jnp.where(cols < lo, _nt(_b(qr), _b(kr)), a)
        blocks.append(a)
    return jnp.concatenate(blocks, axis=0)


def _hg_intra_bwd(dA, qs, k, G):
    refs = _hg_rowblocks(G)
    cols = _iota((SUB, CHUNK), 1)
    rows16 = _iota((SUB, HG_D), 0)
    dk = jnp.zeros((CHUNK, HG_D), F32)
    dq_blocks, dk_diag_blocks = [], []
    for i in range(CHUNK // SUB):
        lo = SUB * i
        qb, Gb = qs[lo:lo + SUB, :], G[lo:lo + SUB, :]
        dAb = dA[lo:lo + SUB, :]
        dq = jnp.zeros((SUB, HG_D), F32)
        dkb = jnp.zeros((SUB, HG_D), F32)
        for s in range(SUB):
            e = jnp.exp(jnp.minimum(Gb - G[lo + s:lo + s + 1, :], 0.0))
            e = jnp.where(rows16 >= s, e, 0.0)
            dcol = jnp.sum(jnp.where(cols == lo + s, dAb, 0.0), axis=-1, keepdims=True)
            w = dcol * e
            dq = dq + w * k[lo + s:lo + s + 1, :]
            dkb = jnp.where(rows16 == s, jnp.sum(w * qb, axis=0, keepdims=True), dkb)
        if i > 0:
            e1 = jnp.exp(Gb - refs[i])
            e2 = jnp.exp(jnp.minimum(refs[i] - G, 0.0))
            dA_off = jnp.where(cols < lo, dAb, 0.0)
            dq = dq + _mm3(_nn, dA_off, k * e2) * e1
            dk = dk + _mm3(_tn, dA_off, qb * e1) * e2
        dq_blocks.append(dq)
        dk_diag_blocks.append(dkb)
    return jnp.concatenate(dq_blocks, axis=0), dk + jnp.concatenate(dk_diag_blocks, axis=0)


def _tri(n, upper=False):
    r, c = _iota((n, n), 0), _iota((n, n), 1)
    return jnp.where((c >= r) if upper else (r >= c), 1.0, 0.0).astype(BF16)


def _prefix_mm(tri, x):
    hi = x.astype(BF16)
    r1 = x - hi.astype(F32)
    mid = r1.astype(BF16)
    lo = (r1 - mid.astype(F32)).astype(BF16)
    return _nn(tri, hi) + _nn(tri, mid) + _nn(tri, lo)


def _hgrn_fwd(z, lb, gn, B, T):
    N = B * T
    NC = T // CHUNK
    ng = HG_H // HG_GROUP_FWD

    def body(z_ref, lb_ref, gn_ref, y_ref, o_ref, st_ref, s_scr):
        lbs = _lower_bound(lb_ref[...])
        tri = _tri(CHUNK)
        s_scr[...] = jnp.zeros_like(s_scr)

        def chunk(c, carry):
            r = pl.ds(pl.multiple_of(c * CHUNK, CHUNK), CHUNK)
            for hh in range(HG_GROUP_FWD):
                zc, oc = 4 * LANE * hh, LANE * hh
                ql, fl, il, gl = (z_ref[r, zc + LANE * j:zc + LANE * (j + 1)].astype(F32) for j in range(4))
                _, _, k, logf = _hg_gates(fl, lbs[:, oc:oc + LANE])
                G = _prefix_mm(tri, logf)
                qs = ql * jax.nn.sigmoid(ql)
                st = s_scr[hh]
                st_ref[hh * NC + c] = st
                g_last = G[CHUNK - 1:CHUNK, :]
                A = _hg_intra_A(qs, k, G)
                o = _nn(_b(A), _b(il)) + _nt(_b(qs * jnp.exp(G)), _b(st))
                s_scr[hh] = st * jnp.exp(g_last) + _mm3(_tn, il, k * jnp.exp(g_last - G))
                o_ref[r, oc:oc + LANE] = o
                rstd = lax.rsqrt(jnp.mean(o * o, axis=-1, keepdims=True) + EPS)
                y_ref[r, oc:oc + LANE] = (o * rstd * gn_ref[...] * (gl * jax.nn.sigmoid(gl))).astype(BF16)
            return carry

        lax.fori_loop(0, NC, chunk, 0)

    gw = HG_GROUP_FWD * LANE
    cb = C_HG // (4 * gw)
    return pl.pallas_call(
        body, name="hgrn_fwd", grid=(B, ng),
        in_specs=[pl.BlockSpec((T, 4 * gw), lambda b, h: (b, cb + h)), pl.BlockSpec((lb.shape[0], gw), lambda b, h: (0, h)),
                  pl.BlockSpec((1, LANE), lambda b, h: (0, 0))],
        out_specs=[pl.BlockSpec((T, gw), lambda b, h: (b, h)), pl.BlockSpec((T, gw), lambda b, h: (b, h)),
                   pl.BlockSpec((HG_GROUP_FWD * NC, HG_D, HG_D), lambda b, h: (b * ng + h, 0, 0))],
        out_shape=[S((N, 512), BF16), S((N, 512), F32), S((B * HG_H * NC, HG_D, HG_D), F32)],
        scratch_shapes=[pltpu.VMEM((HG_GROUP_FWD, HG_D, HG_D), F32)], compiler_params=_cp(("parallel", "parallel")),
    )(z, lb, gn)


def _hgrn_bwd(z, o_raw, states, dy, lb, gn, B, T, swap_sibling=()):
    N = B * T
    NC = T // CHUNK
    ng = HG_H // HG_GROUP
    nsw = len(swap_sibling)

    def body(z_ref, o_ref, st_ref, dy_ref, lb_ref, gn_ref, dz_ref, dlb_ref, dgn_ref, ds_scr, racc, dgn_acc):
        lbs = _lower_bound(lb_ref[...])
        gn_v = gn_ref[...]
        tri, triu = _tri(CHUNK), _tri(CHUNK, upper=True)
        cmask = _iota((CHUNK, CHUNK), 0) >= _iota((CHUNK, CHUNK), 1)
        for ref in (ds_scr, racc, dgn_acc, dlb_ref):
            ref[...] = jnp.zeros_like(ref)

        def chunk(ci, carry):
            c = NC - 1 - ci
            r = pl.ds(pl.multiple_of(c * CHUNK, CHUNK), CHUNK)
            for hh in range(HG_GROUP):
                zc, oc = 4 * LANE * hh, LANE * hh
                lb_v = lbs[:, oc:oc + LANE]
                ql, fl, il, gl = (z_ref[r, zc + LANE * j:zc + LANE * (j + 1)].astype(F32) for j in range(4))
                sig, f, k, logf = _hg_gates(fl, lb_v)
                G = _prefix_mm(tri, logf)
                qs, dsilu_q = _silu_and_grad(ql)
                gs, dsilu_g = _silu_and_grad(gl)
                o = o_ref[r, oc:oc + LANE]
                dyv = dy_ref[r, oc:oc + LANE]
                rstd = lax.rsqrt(jnp.mean(o * o, axis=-1, keepdims=True) + EPS)
                oh = o * rstd
                dgl = dyv * oh * gn_v * dsilu_g
                dn = dyv * gs
                dgn_acc[...] += _rowsum8(dn * oh)
                u = dn * gn_v
                do = rstd * (u - oh * jnp.mean(u * oh, axis=-1, keepdims=True))
                st = st_ref[hh * NC + c]
                dst = ds_scr[hh]
                eG = jnp.exp(G)
                g_last = G[CHUNK - 1:CHUNK, :]
                eL = jnp.exp(g_last - G)
                dA = jnp.where(cmask, _mm3(_nt, do, il), 0.0)
                A, (dq_in, dk_in) = _hg_intra_A(qs, k, G), _hg_intra_bwd(dA, qs, k, G)
                di = _tn(_b(A), _b(do)) + _nt(_b(k * eL), _b(dst))
                dq = dq_in + _mm3(_nn, do, st) * eG
                dk = dk_in + _mm3(_nn, il, dst) * eL
                ds_scr[hh] = dst * jnp.exp(g_last) + _mm3(_tn, do, qs * eG)
                dd = qs * dq - k * dk
                dlogf = _prefix_mm(triu, dd) + racc[hh]
                racc[hh] += jnp.sum(dd, axis=0, keepdims=True)
                df = dlogf / f - dk
                dlb_ref[8 * hh:8 * (hh + 1), :] += _rowsum8(df * (1.0 - sig))
                dz_ref[r, zc:zc + LANE] = (dq * dsilu_q).astype(BF16)
                dz_ref[r, zc + LANE:zc + 2 * LANE] = (df * (1.0 - lb_v) * sig * (1.0 - sig)).astype(BF16)
                dz_ref[r, zc + 2 * LANE:zc + 3 * LANE] = di.astype(BF16)
                dz_ref[r, zc + 3 * LANE:zc + 4 * LANE] = dgl.astype(BF16)
            return carry

        lax.fori_loop(0, NC, chunk, 0)
        dgn_ref[...] = dgn_acc[...]

    gw = HG_GROUP * LANE
    cb = C_HG // (4 * gw)
    col = pl.BlockSpec((T, gw), lambda b, h: (b, h))
    if nsw:
        body = _hosting(body, 6, 3, 3, nsw, _sibling_swap_phases, (B, ng))
    return pl.pallas_call(
        body, name="hgrn_bwd", grid=(B, ng),
        in_specs=[pl.BlockSpec((T, 4 * gw), lambda b, h: (b, cb + h)), col,
                  pl.BlockSpec((HG_GROUP * NC, HG_D, HG_D), lambda b, h: (b * ng + h, 0, 0)), col,
                  pl.BlockSpec((lb.shape[0], gw), lambda b, h: (0, h)), pl.BlockSpec((1, LANE), lambda b, h: (0, 0))]
        + [ANY] * nsw,
        out_specs=[pl.BlockSpec((T, 4 * gw), lambda b, h: (b, h)),
                   pl.BlockSpec((8 * HG_GROUP, LANE), lambda b, h: (b * ng + h, 0)),
                   pl.BlockSpec((8, LANE), lambda b, h: (b * ng + h, 0))] + [ANY] * nsw,
        out_shape=[S((N, 2048), BF16), S((B * HG_H * 8, LANE), F32), S((B * ng * 8, LANE), F32)]
        + _sibling_swap_shapes(swap_sibling),
        scratch_shapes=[pltpu.VMEM((HG_GROUP, HG_D, HG_D), F32), pltpu.VMEM((HG_GROUP, 1, LANE), F32),
                        pltpu.VMEM((8, LANE), F32)] + (_sibling_swap_sems(nsw) if nsw else []),
        compiler_params=_cp(("arbitrary", "arbitrary") if nsw else ("parallel", "parallel")),
    )(z, o_raw, states, dy, lb, gn, *swap_sibling)


def _pair_mean(x, lo_half):
    a = jnp.sum(jnp.where(lo_half, x, 0.0), axis=-1, keepdims=True)
    b = jnp.sum(jnp.where(lo_half, 0.0, x), axis=-1, keepdims=True)
    return jnp.where(lo_half, a, b) * (1.0 / FOX_D)


def _fox_gate_fwd(z, bias, B, T):
    N = B * T
    tb = LANE

    def body(z_ref, b_ref, fc_ref, fct_ref):
        tri = _tri(tb)

        def step(i, carry):
            r = pl.ds(pl.multiple_of(i * tb, tb), tb)
            cs = _prefix_mm(tri, jax.nn.log_sigmoid(z_ref[r, :].astype(F32) + b_ref[...])) + carry
            fc_ref[r, :] = cs
            fct_ref[0, :, r] = cs.T[0:8, :]
            return cs[tb - 1:tb, :]

        lax.fori_loop(0, T // tb, step, jnp.zeros((1, LANE), F32))

    return pl.pallas_call(
        body, name="fox_gate_fwd", grid=(B,),
        in_specs=[pl.BlockSpec((T, LANE), lambda b: (b, C_FF // LANE)), pl.BlockSpec((1, LANE), lambda b: (0, 0))],
        out_specs=[pl.BlockSpec((T, LANE), lambda b: (b, 0)), pl.BlockSpec((1, 8, T), lambda b: (b, 0, 0))],
        out_shape=[S((N, LANE), F32), S((B, 8, T), F32)], compiler_params=_cp(("parallel",)),
    )(z, bias)


def _fox_gate_bwd(dfc, z, bias, B, T):
    N = B * T
    tb = LANE
    nt = T // tb

    def body(d_ref, z_ref, b_ref, dz_ref, db_ref):
        triu = _tri(tb, upper=True)
        db_ref[...] = jnp.zeros_like(db_ref)

        def step(ii, carry):
            r = pl.ds(pl.multiple_of((nt - 1 - ii) * tb, tb), tb)
            d = d_ref[r, 0:LANE]
            for p in range(1, FOX_P):
                d = d + d_ref[r, LANE * p:LANE * (p + 1)]
            rc = _prefix_mm(triu, d) + carry
            dff = rc * jax.nn.sigmoid(-(z_ref[r, :].astype(F32) + b_ref[...]))
            dz_ref[r, :] = dff.astype(BF16)
            db_ref[...] += _rowsum8(dff)
            return carry + jnp.sum(d, axis=0, keepdims=True)

        lax.fori_loop(0, nt, step, jnp.zeros((1, LANE), F32))

    return pl.pallas_call(
        body, name="fox_gate_bwd", grid=(B,),
        in_specs=[pl.BlockSpec((T, 512), lambda b: (b, 0)), pl.BlockSpec((T, LANE), lambda b: (b, C_FF // LANE)),
                  pl.BlockSpec((1, LANE), lambda b: (0, 0))],
        out_specs=[pl.BlockSpec((T, LANE), lambda b: (b, 0)), pl.BlockSpec((8, LANE), lambda b: (b, 0))],
        out_shape=[S((N, LANE), BF16), S((B * 8, LANE), F32)], compiler_params=_cp(("parallel",)),
    )(dfc, z, bias)


def _fox_prep(z_ref, gq, gk, r, lo_half):
    q, k, v = (z_ref[r, LANE * j:LANE * (j + 1)].astype(F32) for j in range(3))
    rq = lax.rsqrt(_pair_mean(q * q, lo_half) + EPS)
    rk = lax.rsqrt(_pair_mean(k * k, lo_half) + EPS)
    qh, kh = q * rq, k * rk
    return qh * gq * (FOX_D ** -0.5), kh * gk, v, qh, kh, rq, rk


def _fox_fwd(z, fc, fct, gq, gk, B, T, tq=512, gather=()):
    N = B * T
    NQ = T // tq
    nga = len(gather)

    def body(z_ref, fc_ref, fct_ref, gq_ref, gk_ref, y_ref, lse_ref, qn_s, kn_s, v_s):
        p, qi = pl.program_id(1), pl.program_id(2)
        lo_half = _iota((1, LANE), 1) < FOX_D

        @pl.when(qi == 0)
        def _():
            def prep(i, carry):
                r = pl.ds(pl.multiple_of(i * tq, tq), tq)
                qn, kn, v = _fox_prep(z_ref, gq_ref[...], gk_ref[...], r, lo_half)[:3]
                qn_s[r, :], kn_s[r, :], v_s[r, :] = qn.astype(BF16), kn.astype(BF16), v.astype(BF16)
                return carry
            lax.fori_loop(0, NQ, prep, 0)

        rq = pl.ds(pl.multiple_of(qi * tq, tq), tq)
        qn = qn_s[rq, :]
        fcq = fc_ref[rq, :]
        lane = _iota((tq, LANE), 1)
        causal = _iota((tq, tq), 0) >= _iota((tq, tq), 1)
        qhs = [jnp.where(lo_half, qn, jnp.zeros_like(qn)), jnp.where(lo_half, jnp.zeros_like(qn), qn)]
        fqs = [jnp.sum(jnp.where(lane == 2 * p + hh, fcq, 0.0), axis=-1, keepdims=True) for hh in range(2)]

        def kv(j, carry, diagonal):
            rk = pl.ds(pl.multiple_of(j * tq, tq), tq)
            kj, vj = kn_s[rk, :], v_s[rk, :]
            one = jnp.ones_like(vj)
            new = []
            for hh in range(2):
                m, acc = carry[hh]
                s = _nt(qhs[hh], kj) + fqs[hh] - fct_ref[0, pl.ds(2 * p + hh, 1), rk]
                if diagonal:
                    s = jnp.where(causal, s, NEG)
                m_new = jnp.maximum(m, jnp.max(s, axis=-1, keepdims=True))
                pe = jnp.exp(s - m_new)
                v_aug = jnp.where(lo_half if hh == 0 else jnp.logical_not(lo_half), vj, one)
                new.append((m_new, jnp.exp(m - m_new) * acc + _nn(pe.astype(BF16), v_aug)))
            return tuple(new)

        init = tuple((jnp.full((tq, 1), NEG, F32), jnp.zeros((tq, LANE), F32)) for _ in range(2))
        carry = lax.fori_loop(0, qi, functools.partial(kv, diagonal=False), init)
        (m0, a0), (m1, a1) = kv(qi, carry, True)
        l0, l1 = a0[:, FOX_D:FOX_D + 1], a1[:, 0:1]
        y_ref[...] = jnp.where(lo_half, a0 / l0, a1 / l1).astype(BF16)
        lse_ref[...] = jnp.where(lo_half, m0 + jnp.log(l0), m1 + jnp.log(l1))

    vec = pl.BlockSpec((1, LANE), lambda b, p, q: (0, 0))
    tile = pl.BlockSpec((tq, LANE), lambda b, p, q: (b * NQ + q, p))
    if nga:
        body = _hosting(body, 5, 2, 3, nga, _gather_phases, (B, FOX_P, NQ))
    return pl.pallas_call(
        body, name="fox_fwd", grid=(B, FOX_P, NQ),
        in_specs=[pl.BlockSpec((T, 384), lambda b, p, q: (b, p)), pl.BlockSpec((T, LANE), lambda b, p, q: (b, 0)),
                  pl.BlockSpec((1, 8, T), lambda b, p, q: (b, 0, 0)), vec, vec] + [ANY] * nga,
        out_specs=[tile, tile] + [ANY] * nga, out_shape=[S((N, 512), BF16), S((N, 512), F32)] + _gather_shapes(gather),
        scratch_shapes=[pltpu.VMEM((T, LANE), BF16)] * 3 + (_gather_sems(nga) if nga else []),
        compiler_params=_cp(("arbitrary",) * 3 if nga else ("parallel", "parallel", "arbitrary")),
    )(z, fc, fct, gq, gk, *gather)


def _fox_bwd(z, dy, y, lse, fc, fct, gq, gk, B, T, tq=512, swap=()):
    N = B * T
    NQ = T // tq
    nsw = len(swap)

    def body(z_ref, dy_ref, y_ref, lse_ref, fc_ref, fct_ref, gq_ref, gk_ref, dz_ref, dfc_ref, dgq_ref, dgk_ref,
             qn_s, kn_s, v_s, do_s, delta_s, dq_s, dfk_s):
        p, kj = pl.program_id(1), pl.program_id(2)
        lo_half = _iota((1, LANE), 1) < FOX_D
        lane = _iota((tq, LANE), 1)
        gq_v, gk_v = gq_ref[...], gk_ref[...]

        @pl.when(kj == 0)
        def _():
            def prep(i, carry):
                r = pl.ds(pl.multiple_of(i * tq, tq), tq)
                qn, kn, v = _fox_prep(z_ref, gq_v, gk_v, r, lo_half)[:3]
                qn_s[r, :], kn_s[r, :], v_s[r, :] = qn.astype(BF16), kn.astype(BF16), v.astype(BF16)
                do = dy_ref[r, :]
                do_s[r, :] = do.astype(BF16)
                delta_s[r, :] = _pair_mean(do * y_ref[r, :].astype(F32), lo_half) * float(FOX_D)
                return carry
            lax.fori_loop(0, NQ, prep, 0)
            dq_s[...] = jnp.zeros_like(dq_s)
            dgq_ref[...] = jnp.zeros_like(dgq_ref)
            dgk_ref[...] = jnp.zeros_like(dgk_ref)

        rk = pl.ds(pl.multiple_of(kj * tq, tq), tq)
        kn, vv = kn_s[rk, :], v_s[rk, :]
        causal = _iota((tq, tq), 0) >= _iota((tq, tq), 1)
        zero, one = jnp.zeros_like(kn), jnp.ones_like(kn)
        hms = [lo_half, jnp.logical_not(lo_half)]
        kmasks = [jnp.where(hm, kn, zero) for hm in hms]
        kaugs = [jnp.where(hm, kn, one) for hm in hms]
        vmasks = [jnp.where(hm, vv, zero) for hm in hms]
        fks = [fct_ref[0, pl.ds(2 * p + hh, 1), rk] for hh in range(2)]

        def qloop(i, carry, diagonal):
            ri = pl.ds(pl.multiple_of(i * tq, tq), tq)
            qn = qn_s[ri, :]
            do = do_s[ri, :]
            fcq = fc_ref[ri, :]
            new = []
            for hh in range(2):
                dk_acc, dv_acc = carry[hh]
                c0 = FOX_D * hh
                fq = jnp.sum(jnp.where(lane == 2 * p + hh, fcq, 0.0), axis=-1, keepdims=True)
                pr = jnp.exp(_nt(qn, kmasks[hh]) + fq - fks[hh] - lse_ref[ri, c0:c0 + 1])
                if diagonal:
                    pr = jnp.where(causal, pr, 0.0)
                ds = (pr * (_nt(do, vmasks[hh]) - delta_s[ri, c0:c0 + 1])).astype(BF16)
                dq_s[hh, ri, :] += _nn(ds, kaugs[hh])
                new.append((dk_acc + _tn(jnp.where(hms[hh], qn, one), ds), dv_acc + _tn(do, pr.astype(BF16))))
            return tuple(new)

        init = tuple((jnp.zeros((LANE, tq), F32), jnp.zeros((LANE, tq), F32)) for _ in range(2))
        carry = qloop(kj, init, True)
        (dk0, dv0), (dk1, dv1) = lax.fori_loop(kj + 1, NQ, functools.partial(qloop, diagonal=False), carry)
        dks, dvs = [dk0.T, dk1.T], [dv0.T, dv1.T]

        dkn = jnp.where(lo_half, dks[0], dks[1])
        _, _, _, _, kh, _, rkk = _fox_prep(z_ref, gq_v, gk_v, rk, lo_half)
        u = dkn * gk_v
        dz_ref[rk, LANE:2 * LANE] = (rkk * (u - kh * _pair_mean(u * kh, lo_half))).astype(BF16)
        dz_ref[rk, 2 * LANE:3 * LANE] = jnp.where(lo_half, dvs[0], dvs[1]).astype(BF16)
        dgk_ref[...] += _rowsum8(dkn * kh)
        dfk_s[rk, :] = jnp.where(lane == 2 * p, -dks[0][:, FOX_D:FOX_D + 1],
                                 jnp.where(lane == 2 * p + 1, -dks[1][:, 0:1], 0.0))

        @pl.when(kj == NQ - 1)
        def _():
            def fin(i, carry):
                r = pl.ds(pl.multiple_of(i * tq, tq), tq)
                d0, d1 = dq_s[0, r, :], dq_s[1, r, :]
                dqn = jnp.where(lo_half, d0, d1)
                _, _, _, qh, _, rqq, _ = _fox_prep(z_ref, gq_v, gk_v, r, lo_half)
                u = dqn * gq_v * (FOX_D ** -0.5)
                dz_ref[r, 0:LANE] = (rqq * (u - qh * _pair_mean(u * qh, lo_half))).astype(BF16)
                dgq_ref[...] += _rowsum8(dqn * qh) * (FOX_D ** -0.5)
                dfc_ref[r, :] = dfk_s[r, :] + jnp.where(lane == 2 * p, d0[:, FOX_D:FOX_D + 1],
                                                        jnp.where(lane == 2 * p + 1, d1[:, 0:1], 0.0))
                return carry
            lax.fori_loop(0, NQ, fin, 0)

    vec = pl.BlockSpec((1, LANE), lambda b, p, k: (0, 0))
    col = pl.BlockSpec((T, LANE), lambda b, p, k: (b, p))
    part = pl.BlockSpec((8, LANE), lambda b, p, k: (b * FOX_P + p, 0))
    if nsw:
        body = _hosting(body, 8, 4, 7, nsw, _chip_swap_phases, (B, FOX_P, NQ))
    return pl.pallas_call(
        body, name="fox_bwd", grid=(B, FOX_P, NQ),
        in_specs=[pl.BlockSpec((T, 384), lambda b, p, k: (b, p)), col, col, col,
                  pl.BlockSpec((T, LANE), lambda b, p, k: (b, 0)), pl.BlockSpec((1, 8, T), lambda b, p, k: (b, 0, 0)),
                  vec, vec] + [ANY] * nsw,
        out_specs=[pl.BlockSpec((T, 384), lambda b, p, k: (b, p)), col, part, part] + [ANY] * nsw,
        out_shape=[S((N, 1536), BF16), S((N, 512), F32), S((B * FOX_P * 8, LANE), F32), S((B * FOX_P * 8, LANE), F32)]
        + [S(p.shape, p.dtype) for p in swap],
        scratch_shapes=[pltpu.VMEM((T, LANE), BF16)] * 4 + [pltpu.VMEM((T, LANE), F32), pltpu.VMEM((2, T, LANE), F32),
                                                            pltpu.VMEM((T, LANE), F32)]
        + (_chip_swap_sems(nsw) if nsw else []),
        compiler_params=_cp(("arbitrary",) * 3 if nsw else ("parallel", "parallel", "arbitrary")),
    )(z, dy, y, lse, fc, fct, gq, gk, *swap)


def _mem_scores(z_ref, kv_ref, gq, gk, h):
    c = slice(MEM_D * h, MEM_D * (h + 1))
    q, k = z_ref[:, c].astype(F32), kv_ref[:, c]
    rq = lax.rsqrt(jnp.mean(q * q, axis=-1, keepdims=True) + EPS)
    rk = lax.rsqrt(jnp.mean(k * k, axis=-1, keepdims=True) + EPS)
    qh, kh = q * rq, k * rk
    qn = (qh * gq * (MEM_D ** -0.5)).astype(BF16)
    kn = (kh * gk).astype(BF16)
    s = _nt(qn, kn)
    pe = jnp.exp(s - jnp.max(s, axis=-1, keepdims=True))
    pn = pe / jnp.sum(pe, axis=-1, keepdims=True)
    return pn, qn, kn, qh, kh, rq, rk


def _mem_fwd(z, memkv, gq, gk, B, T, M, tq=512):
    N = B * T
    NQ = T // tq
    W = MEM_H * MEM_D

    def body(z_ref, kv_ref, gq_ref, gk_ref, y_ref):
        for h in range(MEM_H):
            pn = _mem_scores(z_ref, kv_ref, gq_ref[...], gk_ref[...], h)[0]
            v = kv_ref[:, W + MEM_D * h:W + MEM_D * (h + 1)].astype(BF16)
            y_ref[:, MEM_D * h:MEM_D * (h + 1)] = _nn(pn.astype(BF16), v).astype(BF16)

    vec = pl.BlockSpec((1, LANE), lambda b, q: (0, 0))
    return pl.pallas_call(
        body, name="mem_fwd", grid=(B, NQ),
        in_specs=[pl.BlockSpec((tq, W), lambda b, q: (b * NQ + q, C_MQ // W)),
                  pl.BlockSpec((M, 2 * W), lambda b, q: (b, 0)), vec, vec],
        out_specs=pl.BlockSpec((tq, W), lambda b, q: (b * NQ + q, 0)), out_shape=S((N, W), BF16),
        compiler_params=_cp(("parallel", "parallel")),
    )(z, memkv, gq, gk)


def _mem_bwd(z, memkv, dy, gq, gk, B, T, M, tq=512):
    N = B * T
    NQ = T // tq
    W = MEM_H * MEM_D

    def body(z_ref, kv_ref, dy_ref, gq_ref, gk_ref, dz_ref, dkv_ref, dgq_ref, dgk_ref, acc):
        qi = pl.program_id(1)
        gq_v, gk_v = gq_ref[...], gk_ref[...]

        @pl.when(qi == 0)
        def _():
            acc[...] = jnp.zeros_like(acc)
            dgq_ref[...] = jnp.zeros_like(dgq_ref)
            dgk_ref[...] = jnp.zeros_like(dgk_ref)

        for h in range(MEM_H):
            c = slice(MEM_D * h, MEM_D * (h + 1))
            cv = slice(W + MEM_D * h, W + MEM_D * (h + 1))
            pn, qn, kn, qh, _, rq, _ = _mem_scores(z_ref, kv_ref, gq_v, gk_v, h)
            do = dy_ref[:, c].astype(BF16)
            dp = _nt(do, kv_ref[:, cv].astype(BF16))
            ds = (pn * (dp - jnp.sum(dp * pn, axis=-1, keepdims=True))).astype(BF16)
            dqn = _nn(ds, kn)
            acc[:, c] += _tn(ds, qn)
            acc[:, cv] += _tn(pn.astype(BF16), do)
            u = dqn * gq_v * (MEM_D ** -0.5)
            dz_ref[:, c] = (rq * (u - qh * jnp.mean(u * qh, axis=-1, keepdims=True))).astype(BF16)
            dgq_ref[...] += _rowsum8(dqn * qh) * (MEM_D ** -0.5)

        @pl.when(qi == NQ - 1)
        def _():
            for h in range(MEM_H):
                c = slice(MEM_D * h, MEM_D * (h + 1))
                cv = slice(W + MEM_D * h, W + MEM_D * (h + 1))
                k = kv_ref[:, c]
                rk = lax.rsqrt(jnp.mean(k * k, axis=-1, keepdims=True) + EPS)
                kh = k * rk
                dkn = acc[:, c]
                u = dkn * gk_v
                dkv_ref[:, c] = (rk * (u - kh * jnp.mean(u * kh, axis=-1, keepdims=True))).astype(BF16)
                dkv_ref[:, cv] = acc[:, cv].astype(BF16)
                dgk_ref[...] += _rowsum8(dkn * kh)

    vec = pl.BlockSpec((1, LANE), lambda b, q: (0, 0))
    part = pl.BlockSpec((8, LANE), lambda b, q: (b, 0))
    return pl.pallas_call(
        body, name="mem_bwd", grid=(B, NQ),
        in_specs=[pl.BlockSpec((tq, W), lambda b, q: (b * NQ + q, C_MQ // W)),
                  pl.BlockSpec((M, 2 * W), lambda b, q: (b, 0)), pl.BlockSpec((tq, W), lambda b, q: (b * NQ + q, 0)),
                  vec, vec],
        out_specs=[pl.BlockSpec((tq, W), lambda b, q: (b * NQ + q, 0)), pl.BlockSpec((M, 2 * W), lambda b, q: (b, 0)),
                   part, part],
        out_shape=[S((N, W), BF16), S((B * M, 2 * W), BF16), S((B * 8, LANE), F32), S((B * 8, LANE), F32)],
        scratch_shapes=[pltpu.VMEM((M, 2 * W), F32)], compiler_params=_cp(("parallel", "arbitrary")),
    )(z, memkv, dy, gq, gk)


def _merge_fwd(ya, yb, yc, z, x, wa, wb, wc, wo, tm=256):
    n, d = x.shape
    wdt = ya.shape[1]
    gb = C_GATE // d

    def body(ya_ref, yb_ref, yc_ref, g0_ref, g1_ref, g2_ref, x_ref, wa_ref, wb_ref, wc_ref, wo_ref,
             x1_ref, mg_ref, ua_ref, ub_ref, uc_ref):
        merged = jnp.zeros((tm, d), F32)
        for y_ref, g_ref, w_ref, u_ref in ((ya_ref, g0_ref, wa_ref, ua_ref), (yb_ref, g1_ref, wb_ref, ub_ref),
                                           (yc_ref, g2_ref, wc_ref, uc_ref)):
            u = _nn(y_ref[...], w_ref[...])
            u_ref[...] = u.astype(BF16)
            merged = merged + jax.nn.sigmoid(g_ref[...].astype(F32)) * u
        mb = merged.astype(BF16)
        mg_ref[...] = mb
        x1_ref[...] = x_ref[...] + _nn(mb, wo_ref[...])

    yt = pl.BlockSpec((tm, wdt), lambda i: (i, 0))
    xt = pl.BlockSpec((tm, d), lambda i: (i, 0))
    wbr = pl.BlockSpec((wdt, d), lambda i: (0, 0))
    gates = [pl.BlockSpec((tm, d), functools.partial(lambda i, k: (i, gb + k), k=k)) for k in range(3)]
    return pl.pallas_call(
        body, name="merge_fwd", grid=(n // tm,),
        in_specs=[yt, yt, yt] + gates + [xt, wbr, wbr, wbr, pl.BlockSpec((d, d), lambda i: (0, 0))],
        out_specs=[xt] * 5, out_shape=[S((n, d), F32)] + [S((n, d), BF16)] * 4, compiler_params=_cp(("parallel",)),
    )(ya, yb, yc, z, z, z, x, wa, wb, wc, wo)


def _merge_bwd(dx1, z, ua, ub, uc, wa, wb, wc, wo, tm=256):
    n, d = dx1.shape
    wdt = wa.shape[0]
    gb = C_GATE // d

    def body(dx_ref, g0_ref, g1_ref, g2_ref, ua_ref, ub_ref, uc_ref, wa_ref, wb_ref, wc_ref, wo_ref,
             dg_ref, dya_ref, dyb_ref, dyc_ref, dua_ref, dub_ref, duc_ref):
        dm = _nt(dx_ref[...].astype(BF16), wo_ref[...])
        for k, (g_ref, u_ref, w_ref, dy_ref, du_ref) in enumerate((
                (g0_ref, ua_ref, wa_ref, dya_ref, dua_ref), (g1_ref, ub_ref, wb_ref, dyb_ref, dub_ref),
                (g2_ref, uc_ref, wc_ref, dyc_ref, duc_ref))):
            g = jax.nn.sigmoid(g_ref[...].astype(F32))
            du = (dm * g).astype(BF16)
            du_ref[...] = du
            dg_ref[:, d * k:d * (k + 1)] = (dm * u_ref[...].astype(F32) * g * (1.0 - g)).astype(BF16)
            dy_ref[...] = _nt(du, w_ref[...])

    yt = pl.BlockSpec((tm, wdt), lambda i: (i, 0))
    xt = pl.BlockSpec((tm, d), lambda i: (i, 0))
    wbr = pl.BlockSpec((wdt, d), lambda i: (0, 0))
    gates = [pl.BlockSpec((tm, d), functools.partial(lambda i, k: (i, gb + k), k=k)) for k in range(3)]
    return pl.pallas_call(
        body, name="merge_bwd", grid=(n // tm,),
        in_specs=[xt] + gates + [xt, xt, xt, wbr, wbr, wbr, pl.BlockSpec((d, d), lambda i: (0, 0))],
        out_specs=[pl.BlockSpec((tm, 3 * d), lambda i: (i, 0)), yt, yt, yt, xt, xt, xt],
        out_shape=[S((n, 3 * d), BF16)] + [S((n, wdt), F32)] * 3 + [S((n, d), BF16)] * 3,
        compiler_params=_cp(("parallel",)),
    )(dx1, z, z, z, ua, ub, uc, wa, wb, wc, wo)


FFN_TN = 1408
TN_TM = 2048
INV_SQRT2 = 0.7071067811865476
INV_SQRT_2PI = 0.3989422804014327


def _conv_shifted(a, prev, first, tm):
    row = _iota(a.shape, 0)
    p7 = jnp.where(first, 0.0, prev[7:8, :])
    p6 = jnp.where(first, 0.0, prev[6:7, :])
    a1 = jnp.where(row == 0, p7, pltpu.roll(a, 1, 0))
    a2 = jnp.where(row == 0, p6, jnp.where(row == 1, p7, pltpu.roll(a, 2, 0)))
    return a1, a2


def _ffn_act_fwd(up, cw, cb, B, T, tm=256):
    N = B * T
    dff = cw.shape[1]
    NT, NJ, tn = T // tm, dff // FFN_TN, FFN_TN

    def body(a_ref, v_ref, cw_ref, cb_ref, y_ref, c_ref, carry):
        t = pl.program_id(2)
        a = a_ref[...].astype(F32)
        a1, a2 = _conv_shifted(a, carry[...], t == 0, tm)
        w = cw_ref[...]
        ac = w[0:1, :] * a2 + w[1:2, :] * a1 + w[2:3, :] * a + cb_ref[...]
        cdf = 0.5 * (1.0 + lax.erf(ac * INV_SQRT2))
        y_ref[...] = (ac * cdf * v_ref[...].astype(F32)).astype(BF16)
        c_ref[...] = cdf.astype(BF16)
        carry[...] = a[tm - 8:tm, :]

    return pl.pallas_call(
        body, name="ffn_act_fwd", grid=(B, NJ, NT),
        in_specs=[pl.BlockSpec((tm, tn), lambda b, j, t: (b * NT + t, j)),
                  pl.BlockSpec((tm, tn), lambda b, j, t: (b * NT + t, NJ + j)),
                  pl.BlockSpec((3, tn), lambda b, j, t: (0, j)), pl.BlockSpec((1, tn), lambda b, j, t: (0, j))],
        out_specs=[pl.BlockSpec((tm, tn), lambda b, j, t: (b * NT + t, j))] * 2, out_shape=[S((N, dff), BF16)] * 2,
        scratch_shapes=[pltpu.VMEM((8, tn), F32)], compiler_params=_cp(("parallel", "parallel", "arbitrary")),
    )(up, up, cw, cb)


def _ffn_down_loss(y, wd, x1, tgt, tm=256):
    n, d = x1.shape
    kf = y.shape[1]

    def body(y_ref, w_ref, x_ref, t_ref, dx_ref, ls_ref):
        err = x_ref[...] + _nn(y_ref[...], w_ref[...]) - t_ref[...]
        dx_ref[...] = err * (1.0 / d)

        @pl.when(pl.program_id(0) == 0)
        def _():
            ls_ref[...] = jnp.zeros_like(ls_ref)

        ls_ref[...] += _rowsum8(err * err) * (0.5 / d)

    xt = pl.BlockSpec((tm, d), lambda i: (i, 0))
    return pl.pallas_call(
        body, name="ffn_down_loss", grid=(n // tm,),
        in_specs=[pl.BlockSpec((tm, kf), lambda i: (i, 0)), pl.BlockSpec((kf, d), lambda i: (0, 0)), xt, xt],
        out_specs=[xt, pl.BlockSpec((8, d), lambda i: (0, 0))], out_shape=[S((n, d), F32), S((8, d), F32)],
        compiler_params=_cp(("arbitrary",)),
    )(y, wd, x1, tgt)


def _ffn_act_bwd1(dx2, wd, up, cdf, cw, cb, B, T, tm=256):
    N = B * T
    d = dx2.shape[1]
    dff = cw.shape[1]
    NT, NJ, tn = T // tm, dff // FFN_TN, FFN_TN

    def body(dx_ref, w_ref, a_ref, v_ref, c_ref, cw_ref, cb_ref, dac_ref, dv_ref, dcw_ref, dcb_ref, carry):
        b, t = pl.program_id(1), pl.program_id(2)
        a = a_ref[...].astype(F32)
        a1, a2 = _conv_shifted(a, carry[...], t == 0, tm)
        carry[...] = a[tm - 8:tm, :]
        w = cw_ref[...]
        ac = w[0:1, :] * a2 + w[1:2, :] * a1 + w[2:3, :] * a + cb_ref[...]
        dy = _nt(dx_ref[...].astype(BF16), w_ref[...])
        cdf = c_ref[...].astype(F32)
        dv_ref[...] = (dy * ac * cdf).astype(BF16)
        dac = dy * v_ref[...].astype(F32) * (cdf + ac * jnp.exp(-0.5 * ac * ac) * INV_SQRT_2PI)
        dac_ref[...] = dac

        @pl.when((b == 0) & (t == 0))
        def _():
            dcw_ref[...] = jnp.zeros_like(dcw_ref)
            dcb_ref[...] = jnp.zeros_like(dcb_ref)

        dcw_ref[0:8, :] += _rowsum8(dac * a2)
        dcw_ref[8:16, :] += _rowsum8(dac * a1)
        dcw_ref[16:24, :] += _rowsum8(dac * a)
        dcb_ref[...] += _rowsum8(dac)

    return pl.pallas_call(
        body, name="ffn_act_bwd1", grid=(NJ, B, NT),
        in_specs=[pl.BlockSpec((tm, d), lambda j, b, t: (b * NT + t, 0)), pl.BlockSpec((tn, d), lambda j, b, t: (j, 0)),
                  pl.BlockSpec((tm, tn), lambda j, b, t: (b * NT + t, j)),
                  pl.BlockSpec((tm, tn), lambda j, b, t: (b * NT + t, NJ + j)),
                  pl.BlockSpec((tm, tn), lambda j, b, t: (b * NT + t, j)),
                  pl.BlockSpec((3, tn), lambda j, b, t: (0, j)), pl.BlockSpec((1, tn), lambda j, b, t: (0, j))],
        out_specs=[pl.BlockSpec((tm, tn), lambda j, b, t: (b * NT + t, j)),
                   pl.BlockSpec((tm, tn), lambda j, b, t: (b * NT + t, j)),
                   pl.BlockSpec((24, tn), lambda j, b, t: (0, j)), pl.BlockSpec((8, tn), lambda j, b, t: (0, j))],
        out_shape=[S((N, dff), F32), S((N, dff), BF16), S((24, dff), F32), S((8, dff), F32)],
        scratch_shapes=[pltpu.VMEM((8, tn), F32)], compiler_params=_cp(("parallel", "arbitrary", "arbitrary")),
    )(dx2, wd, up, up, cdf, cw, cb)


def _ffn_act_bwd2(dac, cw, B, T, tm=256):
    N = B * T
    dff = cw.shape[1]
    NT, NJ, tn = T // tm, dff // FFN_TN, FFN_TN
    last8 = N // 8 - 1

    def body(d_ref, nx_ref, cw_ref, da_ref):
        t = pl.program_id(2)
        dd = d_ref[...]
        row = _iota(dd.shape, 0)
        last = t == NT - 1
        n0 = jnp.where(last, 0.0, nx_ref[0:1, :])
        n1 = jnp.where(last, 0.0, nx_ref[1:2, :])
        d1 = jnp.where(row == tm - 1, n0, pltpu.roll(dd, tm - 1, 0))
        d2 = jnp.where(row == tm - 1, n1, jnp.where(row == tm - 2, n0, pltpu.roll(dd, tm - 2, 0)))
        w = cw_ref[...]
        da_ref[...] = (w[2:3, :] * dd + w[1:2, :] * d1 + w[0:1, :] * d2).astype(BF16)

    return pl.pallas_call(
        body, name="ffn_act_bwd2", grid=(B, NJ, NT),
        in_specs=[pl.BlockSpec((tm, tn), lambda b, j, t: (b * NT + t, j)),
                  pl.BlockSpec((8, tn), lambda b, j, t: (jnp.minimum((b * NT + t + 1) * (tm // 8), last8), j)),
                  pl.BlockSpec((3, tn), lambda b, j, t: (0, j))],
        out_specs=pl.BlockSpec((tm, tn), lambda b, j, t: (b * NT + t, j)), out_shape=S((N, dff), BF16),
        compiler_params=_cp(("parallel", "parallel", "parallel")),
    )(dac, dac, cw)


def _fold_rows(p, name):
    r, c = p.shape[0] // 8, p.shape[1]

    def body(p_ref, o_ref):
        for j in range(r):
            o_ref[j:j + 1, :] = jnp.sum(p_ref[8 * j:8 * (j + 1), :], axis=0, keepdims=True)

    return pl.pallas_call(body, name=name, out_shape=S((r, c), F32), compiler_params=_cp())(p)


def _small_reduce(lbl, dg_mix, dg_mem, dlb_p, dgn_p, dfb_p, dgq_p, dgk_p, dmq_p, dmk_p, dg_ffn, dcb_p, loss_p):
    d, dff = dg_mix.shape[1], dcb_p.shape[1]
    nbh = dlb_p.shape[0] // (8 * HG_H)

    def colsum(ref):
        return jnp.sum(ref[...], axis=0, keepdims=True)

    def body(lbl_ref, mix_ref, mem_ref, dlb_ref, dgn_ref, dfb_ref, dgq_ref, dgk_ref, dmq_ref, dmk_ref, ffn_ref, dcb_ref,
             ls_ref, o_mix, o_mem, o_lb, o_hgn, o_fb, o_fq, o_fk, o_mq, o_mk, o_ffn, o_cb, o_loss):
        o_mix[...], o_mem[...], o_ffn[...], o_cb[...] = colsum(mix_ref), colsum(mem_ref), colsum(ffn_ref), colsum(dcb_ref)
        o_hgn[...], o_fb[...], o_mq[...], o_mk[...] = colsum(dgn_ref), colsum(dfb_ref), colsum(dmq_ref), colsum(dmk_ref)
        for src, dst in ((dgq_ref, o_fq), (dgk_ref, o_fk)):
            v = colsum(src)
            dst[...] = v + pltpu.roll(v, FOX_D, 1)
        o_loss[...] = jnp.zeros((1, LANE), F32) + jnp.sum(colsum(ls_ref), axis=-1, keepdims=True)
        logits = lbl_ref[...]
        e = jnp.exp(logits - jnp.max(logits, axis=0, keepdims=True))
        pr = e / jnp.sum(e, axis=0, keepdims=True)
        rows = _iota((8, LANE), 0)
        for h in range(HG_H):
            acc = jnp.zeros((8, LANE), F32)
            for b in range(nbh):
                acc = acc + dlb_ref[8 * (b * HG_H + h):8 * (b * HG_H + h + 1), :]
            dlb = jnp.sum(acc, axis=0, keepdims=True)
            c = slice(LANE * h, LANE * (h + 1))
            p0 = pr[0:1, c]
            first = _iota((logits.shape[0], LANE), 0) == 0
            o_lb[:, c] = pr[:, c] * (jnp.where(first, 1.0, 0.0) - p0) * dlb

    outs = [S((1, d), F32), S((1, d), F32), S(lbl.shape, F32)] + [S((1, LANE), F32)] * 6 + \
           [S((1, d), F32), S((1, dff), F32), S((1, LANE), F32)]
    return pl.pallas_call(body, name="small_reduce", out_shape=outs, compiler_params=_cp())(
        lbl, dg_mix, dg_mem, dlb_p, dgn_p, dfb_p, dgq_p, dgk_p, dmq_p, dmk_p, dg_ffn, dcb_p, loss_p)


def _in_col_pieces():
    hw, fw = HG_H * HG_D, FOX_H * FOX_D
    fox0, ff0 = 4 * hw, 4 * hw + 3 * fw
    mq0 = ff0 + FOX_H
    gate0 = mq0 + MEM_H * MEM_D
    pieces = []
    for p in range(FOX_P):
        pieces += [(fox0 + j * fw + LANE * p, LANE) for j in range(3)]
    pieces.append((mq0, MEM_H * MEM_D))
    for h in range(HG_H):
        pieces += [(j * hw + HG_D * h, HG_D) for j in range(4)]
    pieces.append((gate0, C_FF - C_GATE))
    pieces.append((ff0, FOX_H))
    return pieces


def _perm_from_blocks(blocks):
    n_blk, _, c = blocks.shape
    parts = []
    for s, n in _in_col_pieces():
        lo = s
        while lo < s + n:
            d = lo // c
            hi = min(s + n, (d + 1) * c)
            parts.append(blocks[d][:, lo - d * c:hi - d * c])
            lo = hi
    parts.append(jnp.zeros((blocks.shape[1], C_END - C_FF - FOX_H), blocks.dtype))
    return jnp.concatenate(parts, axis=1)


def _unperm_blocks(segs, n_blk):
    starts = [0]
    for a in segs:
        starts.append(starts[-1] + a.shape[1])
    new_start, placed = 0, []
    for s, n in _in_col_pieces():
        placed.append((s, new_start, n))
        new_start += n
    placed.sort()
    c = sum(n for _, _, n in placed) // n_blk
    blocks = []
    for d in range(n_blk):
        parts = []
        for s, ns, n in placed:
            lo, hi = max(s, d * c), min(s + n, (d + 1) * c)
            if lo < hi:
                i = max(j for j in range(len(segs)) if starts[j] <= ns)
                parts.append(segs[i][:, ns + lo - s - starts[i]:ns + hi - s - starts[i]])
        blocks.append(jnp.concatenate(parts, axis=1))
    return jnp.stack(blocks)


def _local_step(x2, mem2, tgt, sm, W, B, T, M, ex=None):
    fbias = jnp.pad(sm["fox_f_bias"], ((0, 0), (0, LANE - FOX_H)))
    gq2 = jnp.concatenate([sm["fox_q_norm_g"]] * 2, axis=1)
    gk2 = jnp.concatenate([sm["fox_k_norm_g"]] * 2, axis=1)
    lbl = sm["hgrn_lb_logits"]
    h = _rmsnorm_cast(x2, sm["norm_mix_g"], "norm_mix")
    z = _mm_nn(h, W["w_in"], BF16, "proj_in", 512, 2432)
    memn = _rmsnorm_cast(mem2, sm["norm_mem_g"], "norm_mem", tm=256)
    memkv = _mm_nn(memn, W["mem_kv_w"], F32, "proj_memkv", 256, 512)
    ya, o_raw, states = _hgrn_fwd(z, lbl, sm["hgrn_norm_g"], B, T)
    fc, fct = _fox_gate_fwd(z, fbias, B, T)
    yb, lse, *late = _fox_fwd(z, fc, fct, gq2, gk2, B, T, gather=ex.late_blocks() if ex else ())
    if ex:
        W = {**W, **ex.unpack_late(late)}
    yc = _mem_fwd(z, memkv, sm["mem_q_norm_g"], sm["mem_k_norm_g"], B, T, M)
    x1, merged, ua, ub, uc = _merge_fwd(ya, yb, yc, z, x2, W["w_br_hgrn"], W["w_br_fox"], W["w_br_mem"], W["w_out"])
    h2 = _rmsnorm_cast(x1, sm["norm_ffn_g"], "norm_ffn")
    up = _mm_nn(h2, W["ffn_w_up"], BF16, "ffn_up", 512, FFN_TN)
    yf, cdf = _ffn_act_fwd(up, W["ffn_conv_w"], sm["ffn_conv_b"], B, T)
    dx2, loss_p = _ffn_down_loss(yf, W["ffn_w_down"], x1, tgt)
    dff = W["ffn_conv_w"].shape[1]
    dac, dv, dcw_p, dcb_p = _ffn_act_bwd1(dx2, W["ffn_w_down"], up, cdf, W["ffn_conv_w"], sm["ffn_conv_b"], B, T)
    da = _ffn_act_bwd2(dac, W["ffn_conv_w"], B, T)
    g = {"ffn_conv_w": _fold_rows(dcw_p, "g_conv_w")}
    g["ffn_w_down"] = _mm_tn(yf, dx2, "g_w_down", TN_TM, 512)
    dh2 = _mm_nt_sum([(da, 0, dff, 0), (dv, 0, dff, dff)], W["ffn_w_up"], "dh2", 256)
    g["ffn_w_up"] = [_mm_tn(h2, da, "g_w_up_a", TN_TM, FFN_TN), _mm_tn(h2, dv, "g_w_up_v", TN_TM, FFN_TN)]
    dx1, dg_ffn = _rmsnorm_bwd(dh2, x1, sm["norm_ffn_g"], dx2, "norm_ffn_bwd")
    g["w_out"] = _mm_tn(merged, dx1, "g_w_out", TN_TM, 512)
    dgate, dya, dyb, dyc, dua, dub, duc = _merge_bwd(dx1, z, ua, ub, uc, W["w_br_hgrn"], W["w_br_fox"], W["w_br_mem"],
                                                    W["w_out"])
    g["w_br_hgrn"] = _mm_tn(ya, dua, "g_w_br_hgrn", TN_TM, 512)
    g["w_br_fox"] = _mm_tn(yb, dub, "g_w_br_fox", TN_TM, 512)
    g["w_br_mem"] = _mm_tn(yc, duc, "g_w_br_mem", TN_TM, 512)
    early_pk = ex.early_grads(g) if ex else ()
    dz_hg, dlb_p, dgn_p, *early_sib = _hgrn_bwd(z, o_raw, states, dya, lbl, sm["hgrn_norm_g"], B, T,
                                                swap_sibling=early_pk)
    dz_fox, dfc, dgq_p, dgk_p, *early_chips = _fox_bwd(z, dyb, yb, lse, fc, fct, gq2, gk2, B, T,
                                                       swap=ex.pair_sums(early_pk, early_sib, "early") if ex else ())
    dz_ff, dfb_p = _fox_gate_bwd(dfc, z, fbias, B, T)
    dz_mq, dkv, dmq_p, dmk_p = _mem_bwd(z, memkv, dyc, sm["mem_q_norm_g"], sm["mem_k_norm_g"], B, T, M)
    g["mem_kv_w"] = _mm_tn(memn, dkv, "g_mem_kv_w", 256, 512)
    dmemn = _mm_nt(dkv, W["mem_kv_w"], "d_memn", 256, 512)
    _, dg_mem = _rmsnorm_bwd(dmemn, mem2, sm["norm_mem_g"], None, "norm_mem_bwd", tm=256)
    d = x2.shape[1]
    parts = [(dz_fox, 0, C_MQ - C_FOX, C_FOX), (dz_mq, 0, C_HG - C_MQ, C_MQ), (dz_hg, 0, C_GATE - C_HG, C_HG)]
    parts += [(dgate, d * k, d, C_GATE + d * k) for k in range(3)] + [(dz_ff, 0, C_END - C_FF, C_FF)]
    g["w_in"] = [_mm_tn(h, dzs, "g_w_in_%d" % i, TN_TM, min(512, dzs.shape[1]))
                 for i, dzs in enumerate((dz_fox, dz_mq, dz_hg, dgate, dz_ff))]
    sums = None
    if ex:
        last_pk = ex.last_grads(g)
        last_sib = _swap_with_sibling(last_pk, "rs_sibling_last")
        dh, last_chips = _mm_nt_sum(parts, W["w_in"], "dh", 256, swap=ex.pair_sums(last_pk, last_sib, "last"))
        sums = (ex.final_sums(early_pk, early_sib, early_chips, "early"),
                ex.final_sums(last_pk, last_sib, last_chips, "last"))
    else:
        dh = _mm_nt_sum(parts, W["w_in"], "dh", 256)
    grad_x, dg_mix = _rmsnorm_bwd(dh, x2, sm["norm_mix_g"], dx1, "norm_mix_bwd")
    small = _small_reduce(lbl, dg_mix, dg_mem, dlb_p, dgn_p, dfb_p, dgq_p, dgk_p, dmq_p, dmk_p, dg_ffn, dcb_p, loss_p)
    names = ("norm_mix_g", "norm_mem_g", "hgrn_lb_logits", "hgrn_norm_g", "fox_f_bias", "fox_q_norm_g", "fox_k_norm_g",
             "mem_q_norm_g", "mem_k_norm_g", "norm_ffn_g", "ffn_conv_b", "loss")
    g.update(dict(zip(names, small)))
    return grad_x, g, sums


ANY = pl.BlockSpec(memory_space=pl.ANY)


def _position():
    return lax.axis_index("x"), lax.axis_index("y"), lax.axis_index("c")


def _all_gather(blocks, name):
    nb = len(blocks)

    def body(*refs):
        start, forward, finish = _gather_phases(refs[:nb], refs[nb:2 * nb], *refs[2 * nb:])
        start()
        forward()
        finish()

    return pl.pallas_call(
        body, name=name, out_shape=_gather_shapes(blocks), in_specs=[ANY] * nb, out_specs=[ANY] * nb,
        scratch_shapes=_gather_sems(nb),
    )(*blocks)


def _hosting(body, n_in, n_out, n_scratch, n_x, make_phases, grid):
    n_steps = math.prod(grid)

    def hosted(*refs):
        a = n_in + n_x
        b = a + n_out + n_x
        ins, xs = refs[:n_in], refs[n_in:a]
        outs, x_outs = refs[a:a + n_out], refs[a + n_out:b]
        scratch, sems = refs[b:b + n_scratch], refs[b + n_scratch:]
        step = 0
        for ax, n in enumerate(grid):
            step = step * n + pl.program_id(ax)
        phases = make_phases(xs, x_outs, *sems)
        pl.when(step == 0)(phases[0])
        for ph in phases[1:-1]:
            pl.when(step == n_steps // 2)(ph)
        body(*ins, *outs, *scratch)
        pl.when(step == n_steps - 1)(phases[-1])

    return hosted


def _gather_shapes(blocks):
    return [S((N_DEV,) + b.shape, b.dtype) for b in blocks]


def _gather_sems(nb):
    return [pltpu.SemaphoreType.DMA((7 * nb,)), pltpu.SemaphoreType.DMA((7 * nb,)), pltpu.SemaphoreType.DMA((nb,))]


def _gather_phases(x_refs, out_refs, send_sems, recv_sems, local_sems):
    nb = len(x_refs)
    x, y, c = _position()
    me, sibling = (x, y, c), (x, y, 1 - c)
    chips = [(1 - x, y), (x, 1 - y), (1 - x, 1 - y)]

    def copy(i, k, blk, to, own=False):
        px, py, pc = blk
        slot = out_refs[i].at[4 * px + 2 * py + pc]
        return pltpu.make_async_remote_copy(
            src_ref=x_refs[i] if own else slot, dst_ref=slot, send_sem=send_sems.at[7 * i + k],
            recv_sem=recv_sems.at[7 * i + k], device_id=to, device_id_type=MESH)

    def mine(i):
        return pltpu.make_async_copy(x_refs[i], out_refs[i].at[4 * x + 2 * y + c], local_sems.at[i])

    def first(i):
        return [copy(i, 0, me, sibling, own=True)] + [copy(i, 1 + j, me, (*chip, c), own=True)
                                                     for j, chip in enumerate(chips)]

    def passed(i, j):
        return copy(i, 4 + j, (*chips[j], c), sibling)

    def start():
        for i in range(nb):
            mine(i).start()
            for cp in first(i):
                cp.start()

    def forward():
        for i in range(nb):
            for j, chip in enumerate(chips):
                copy(i, 1 + j, (*chip, c), me).wait_recv()
                passed(i, j).start()

    def finish():
        for i in range(nb):
            copy(i, 0, sibling, me).wait_recv()
            for j, chip in enumerate(chips):
                copy(i, 4 + j, (*chip, 1 - c), me).wait_recv()
        for i in range(nb):
            for cp in first(i) + [passed(i, j) for j in range(3)]:
                cp.wait_send()
            mine(i).wait()

    return start, forward, finish


def _swap_with_sibling(pks, name):
    nb = len(pks)

    def body(*refs):
        start, finish = _sibling_swap_phases(refs[:nb], refs[nb:2 * nb], *refs[2 * nb:])
        start()
        finish()

    return pl.pallas_call(
        body, name=name, out_shape=_sibling_swap_shapes(pks), in_specs=[ANY] * nb, out_specs=[ANY] * nb,
        scratch_shapes=_sibling_swap_sems(nb),
    )(*pks)


def _sibling_swap_shapes(pks):
    return [S((4,) + p.shape[1:], p.dtype) for p in pks]


def _sibling_swap_sems(nb):
    return [pltpu.SemaphoreType.DMA((4 * nb,)), pltpu.SemaphoreType.DMA((4 * nb,))]


def _sibling_swap_phases(pk_refs, out_refs, send_sems, recv_sems):
    nb = len(pk_refs)
    x, y, c = _position()

    def copies():
        return [pltpu.make_async_remote_copy(
            src_ref=pk_refs[i].at[2 * k + 1 - c], dst_ref=out_refs[i].at[k], send_sem=send_sems.at[4 * i + k],
            recv_sem=recv_sems.at[4 * i + k], device_id=(x, y, 1 - c), device_id_type=MESH)
            for i in range(nb) for k in range(4)]

    def start():
        for cp in copies():
            cp.start()

    def finish():
        for cp in copies():
            cp.wait()

    return start, finish


def _swap_between_chips(pbs, name):
    nb = len(pbs)

    def body(*refs):
        start, finish = _chip_swap_phases(refs[:nb], refs[nb:2 * nb], *refs[2 * nb:])
        start()
        finish()

    return pl.pallas_call(
        body, name=name, out_shape=[S(p.shape, p.dtype) for p in pbs], in_specs=[ANY] * nb, out_specs=[ANY] * nb,
        scratch_shapes=_chip_swap_sems(nb),
    )(*pbs)


def _chip_swap_sems(nb):
    return [pltpu.SemaphoreType.DMA((3 * nb,)), pltpu.SemaphoreType.DMA((3 * nb,)), pltpu.SemaphoreType.DMA((nb,))]


def _chip_swap_phases(pb_refs, out_refs, send_sems, recv_sems, local_sems):
    nb = len(pb_refs)
    x, y, c = _position()
    me = 2 * x + y
    chips = [(1 - x, y), (x, 1 - y), (1 - x, 1 - y)]

    def local(i):
        return pltpu.make_async_copy(pb_refs[i].at[me], out_refs[i].at[me], local_sems.at[i])

    def send(i, j):
        cx, cy = chips[j]
        return pltpu.make_async_remote_copy(
            src_ref=pb_refs[i].at[2 * cx + cy], dst_ref=out_refs[i].at[me], send_sem=send_sems.at[3 * i + j],
            recv_sem=recv_sems.at[3 * i + j], device_id=(cx, cy, c), device_id_type=MESH)

    def arrival(i, j):
        cx, cy = chips[j]
        return pltpu.make_async_remote_copy(
            src_ref=pb_refs[i].at[me], dst_ref=out_refs[i].at[2 * cx + cy], send_sem=send_sems.at[3 * i + j],
            recv_sem=recv_sems.at[3 * i + j], device_id=(cx, cy, c), device_id_type=MESH)

    def start():
        for i in range(nb):
            local(i).start()
            for j in range(3):
                send(i, j).start()

    def finish():
        for i in range(nb):
            for j in range(3):
                arrival(i, j).wait_recv()
        for i in range(nb):
            for j in range(3):
                send(i, j).wait_send()
            local(i).wait()

    return start, finish


def _row_tile(r):
    return max(t for t in range(16, min(r, 512) + 1, 16) if r % t == 0)


def _pair_sum_cast(pk, recv, core, name):
    _, r, l = pk.shape
    tr = _row_tile(r)

    def body(c_ref, a_ref, b_ref, o_ref):
        o_ref[...] = (a_ref[...] + b_ref[...]).astype(BF16)

    return pl.pallas_call(
        body, name=name,
        grid_spec=pltpu.PrefetchScalarGridSpec(
            num_scalar_prefetch=1, grid=(4, r // tr),
            in_specs=[pl.BlockSpec((None, tr, l), lambda k, i, c: (2 * k + c[0], i, 0)),
                      pl.BlockSpec((None, tr, l), lambda k, i, c: (k, i, 0))],
            out_specs=pl.BlockSpec((None, tr, l), lambda k, i, c: (k, i, 0))),
        out_shape=S((4, r, l), BF16), compiler_params=_cp(("parallel", "parallel")),
    )(core, pk, recv)


def _final_sum(pk, recv_sib, recv_chips, slot, chip, name):
    _, r, l = pk.shape
    tr = _row_tile(r)

    def body(s_ref, k_ref, a_ref, b_ref, rc_ref, o_ref):
        base = a_ref[...] + b_ref[...]
        acc = jnp.zeros_like(base)
        for j in range(4):
            acc = acc + jnp.where(k_ref[0] == j, base, rc_ref[j].astype(F32))
        o_ref[...] = acc

    return pl.pallas_call(
        body, name=name,
        grid_spec=pltpu.PrefetchScalarGridSpec(
            num_scalar_prefetch=2, grid=(r // tr,),
            in_specs=[pl.BlockSpec((None, tr, l), lambda i, s, k: (s[0], i, 0)),
                      pl.BlockSpec((None, tr, l), lambda i, s, k: (k[0], i, 0)),
                      pl.BlockSpec((4, tr, l), lambda i, s, k: (0, i, 0))],
            out_specs=pl.BlockSpec((tr, l), lambda i, s, k: (i, 0))),
        out_shape=S((r, l), F32), compiler_params=_cp(("parallel",)),
    )(slot, chip, pk, recv_sib, recv_chips)


def _adamw_math(w, g, m, v):
    m = ADAM_B1 * m + (1.0 - ADAM_B1) * g
    v = ADAM_B2 * v + (1.0 - ADAM_B2) * (g * g)
    m_hat = m / (1.0 - ADAM_B1 ** ADAM_STEP)
    v_hat = v / (1.0 - ADAM_B2 ** ADAM_STEP)
    return -ADAM_LR * (m_hat / (jnp.sqrt(v_hat) + ADAM_EPS) + ADAM_WD * w), m, v


def _adamw(w, g, m, v, name):
    r, c = w.shape
    tr = 256 if r % 256 == 0 else r

    def body(w_ref, g_ref, m_ref, v_ref, d_ref, nm_ref, nv_ref):
        d_ref[...], nm_ref[...], nv_ref[...] = _adamw_math(w_ref[...], g_ref[...], m_ref[...], v_ref[...])

    tile = pl.BlockSpec((tr, c), lambda i: (i, 0))
    return pl.pallas_call(
        body, name=name, grid=(r // tr,), in_specs=[tile] * 4, out_specs=[tile] * 3, out_shape=[S((r, c), F32)] * 3,
        compiler_params=_cp(("parallel",)),
    )(w, g, m, v)


def _small_update(gathered, w, m, v):
    def body(ga_ref, w_ref, m_ref, v_ref, g_ref, d_ref, nm_ref, nv_ref):
        g = ga_ref[0]
        for k in range(1, N_DEV):
            g = g + ga_ref[k]
        g_ref[...] = g
        d_ref[...], nm_ref[...], nv_ref[...] = _adamw_math(w_ref[...], g, m_ref[...], v_ref[...])

    return pl.pallas_call(body, name="small_update", out_shape=[S(w.shape, F32)] * 4, compiler_params=_cp())(
        gathered, w, m, v)


BIG = ("w_in", "mem_kv_w", "w_br_hgrn", "w_br_fox", "w_br_mem", "w_out", "ffn_w_up", "ffn_conv_w", "ffn_w_down")
GROUP_ROWS = ("w_out", "ffn_w_down")
GROUP_LANE = ("w_br_hgrn", "w_br_fox", "w_br_mem")
LANE_GROUP_ROWS = 224
SMALL = ("norm_mix_g", "norm_mem_g", "hgrn_lb_logits", "hgrn_norm_g", "fox_f_bias", "fox_q_norm_g", "fox_k_norm_g",
         "mem_q_norm_g", "mem_k_norm_g", "norm_ffn_g", "ffn_conv_b")


def _rows_of(n_elems):
    return -(-n_elems // LANE)


def _to_rows(a, lead=0):
    flat = a.reshape(a.shape[:lead] + (-1,))
    pad = (-flat.shape[-1]) % LANE
    if pad:
        flat = jnp.pad(flat, [(0, 0)] * lead + [(0, pad)])
    return flat.reshape(a.shape[:lead] + (-1, LANE))


def _stack_rows(parts, lead, total_rows):
    buf = jnp.concatenate(parts, axis=lead)
    pad = total_rows - buf.shape[lead]
    return jnp.pad(buf, [(0, 0)] * lead + [(0, pad), (0, 0)])


def _round_up(n, k):
    return -(-n // k) * k


def _from_rows(rows, shape, lead=0):
    n = math.prod(shape)
    return rows.reshape(rows.shape[:lead] + (-1,))[..., :n].reshape(rows.shape[:lead] + tuple(shape))


def _blocks_to_full(blocks, kind):
    n, a, b = blocks.shape
    return blocks.transpose(1, 0, 2).reshape(a, n * b) if kind == "col" else blocks.reshape(n * a, b)


def _full_to_blocks(full, kind, n=N_DEV):
    a, b = full.shape
    return full.reshape(a, n, b // n).transpose(1, 0, 2) if kind == "col" else full.reshape(n, a // n, b)


def _lane_group_rows(shard):
    n_lane = sum(shard[n].shape[0] for n in GROUP_LANE)
    n_cw = shard["ffn_conv_w"].size
    return n_lane, _rows_of(3 * n_cw), _rows_of(n_cw), _round_up(n_lane + _rows_of(3 * n_cw), LANE_GROUP_ROWS)


def _split_bf16x3(x):
    hi = x.astype(BF16)
    r1 = x - hi.astype(F32)
    mid = r1.astype(BF16)
    return jnp.stack([hi, mid, (r1 - mid.astype(F32)).astype(BF16)])


class _Exchange:
    def __init__(self, shard):
        self.shard = shard
        xi, yi, ci = _position()
        self.core = ci.astype(jnp.int32).reshape(1)
        self.chip = (2 * xi + yi).astype(jnp.int32).reshape(1)
        self.n_lane, self.r_pieces, self.r_vals, self.r_lane = _lane_group_rows(shard)

    def first_blocks(self):
        return [self.shard["w_in"].astype(BF16), self.shard["mem_kv_w"].astype(BF16)]

    def unpack_first(self, gathered):
        return {"w_in": _perm_from_blocks(gathered[0]), "mem_kv_w": _blocks_to_full(gathered[1], "row")}

    def late_blocks(self):
        sh = self.shard
        lane_rows = [sh[n].astype(BF16) for n in GROUP_LANE] + [_to_rows(_split_bf16x3(sh["ffn_conv_w"]))]
        return [sh[n].astype(BF16) for n in GROUP_ROWS] + [sh["ffn_w_up"].astype(BF16),
                                                           _stack_rows(lane_rows, 0, self.r_lane)]

    def unpack_late(self, gathered):
        *rows, gc, gd = gathered
        sh = self.shard
        W = {"ffn_w_up": _blocks_to_full(gc, "col")}
        for n, blocks in zip(GROUP_ROWS, rows):
            W[n] = _blocks_to_full(blocks, "row")
        r0 = 0
        for n in GROUP_LANE:
            W[n] = _blocks_to_full(gd[:, r0:r0 + sh[n].shape[0]], "col")
            r0 += sh[n].shape[0]
        cw = _from_rows(gd[:, self.n_lane:self.n_lane + self.r_pieces], (3,) + sh["ffn_conv_w"].shape, lead=1).astype(F32)
        W["ffn_conv_w"] = _blocks_to_full(cw[:, 0] + cw[:, 1] + cw[:, 2], "col")
        return W

    def early_grads(self, g):
        cw_rows = _to_rows(_full_to_blocks(g["ffn_conv_w"], "col"), lead=1)
        return [_full_to_blocks(g[n], "row") for n in GROUP_ROWS] + [
            jnp.concatenate([_full_to_blocks(h, "col", N_DEV // 2) for h in g["ffn_w_up"]], axis=0),
            _stack_rows([_full_to_blocks(g[n], "col") for n in GROUP_LANE] + [cw_rows], 1, self.r_lane)]

    def last_grads(self, g):
        return [_unperm_blocks(g["w_in"], N_DEV), _full_to_blocks(g["mem_kv_w"], "row")]

    def pair_sums(self, pks, recv_sib, tag):
        return [_pair_sum_cast(p, r, self.core, "rs_pair_sum_%s%d" % (tag, i))
                for i, (p, r) in enumerate(zip(pks, recv_sib))]

    def final_sums(self, pks, recv_sib, recv_chips, tag):
        return [_final_sum(p, rs, rc, 2 * self.chip + self.core, self.chip, "rs_final_sum_%s%d" % (tag, i))
                for i, (p, rs, rc) in enumerate(zip(pks, recv_sib, recv_chips))]

    def unpack_grads(self, early, last):
        sh = self.shard
        *rows, g_up, g_lane = early
        g_shard = {"w_in": last[0], "mem_kv_w": last[1], "ffn_w_up": g_up, **dict(zip(GROUP_ROWS, rows))}
        r0 = 0
        for n in GROUP_LANE:
            g_shard[n] = g_lane[r0:r0 + sh[n].shape[0]]
            r0 += sh[n].shape[0]
        g_shard["ffn_conv_w"] = _from_rows(g_lane[self.n_lane:self.n_lane + self.r_vals], sh["ffn_conv_w"].shape)
        return g_shard


def kernel(x, mem, norm_mix_g, norm_mem_g, w_in, hgrn_lb_logits, hgrn_norm_g, fox_f_bias, fox_q_norm_g, fox_k_norm_g, mem_kv_w, mem_q_norm_g, mem_k_norm_g, w_br_hgrn, w_br_fox, w_br_mem, w_out, norm_ffn_g, ffn_w_up, ffn_conv_w, ffn_conv_b, ffn_w_down, loss_target, m_norm_mix_g, m_norm_mem_g, m_w_in, m_hgrn_lb_logits, m_hgrn_norm_g, m_fox_f_bias, m_fox_q_norm_g, m_fox_k_norm_g, m_mem_kv_w, m_mem_q_norm_g, m_mem_k_norm_g, m_w_br_hgrn, m_w_br_fox, m_w_br_mem, m_w_out, m_norm_ffn_g, m_ffn_w_up, m_ffn_conv_w, m_ffn_conv_b, m_ffn_w_down, v_norm_mix_g, v_norm_mem_g, v_w_in, v_hgrn_lb_logits, v_hgrn_norm_g, v_fox_f_bias, v_fox_q_norm_g, v_fox_k_norm_g, v_mem_kv_w, v_mem_q_norm_g, v_mem_k_norm_g, v_w_br_hgrn, v_w_br_fox, v_w_br_mem, v_w_out, v_norm_ffn_g, v_ffn_w_up, v_ffn_conv_w, v_ffn_conv_b, v_ffn_w_down):
    given = dict(locals())
    order = ("norm_mix_g", "norm_mem_g", "w_in", "hgrn_lb_logits", "hgrn_norm_g", "fox_f_bias", "fox_q_norm_g",
             "fox_k_norm_g", "mem_kv_w", "mem_q_norm_g", "mem_k_norm_g", "w_br_hgrn", "w_br_fox", "w_br_mem", "w_out",
             "norm_ffn_g", "ffn_w_up", "ffn_conv_w", "ffn_conv_b", "ffn_w_down")
    B, T, D = x.shape
    M = mem.shape[1]
    shard = {n: given[n][0] if n in BIG else given[n] for n in order}
    mom = {n: (given["m_" + n][0], given["v_" + n][0]) if n in BIG else (given["m_" + n], given["v_" + n])
           for n in order}
    shard["hgrn_lb_logits"] = hgrn_lb_logits
    for n in ("norm_mix_g", "norm_mem_g", "hgrn_norm_g", "fox_f_bias", "fox_q_norm_g", "fox_k_norm_g", "mem_q_norm_g",
              "mem_k_norm_g", "norm_ffn_g", "ffn_conv_b"):
        shard[n] = given[n].reshape(1, -1)

    ex = _Exchange(shard)
    W = ex.unpack_first(_all_gather(ex.first_blocks(), "ag_first"))

    sm = {n: shard[n] for n in SMALL}
    grad_x, g, sums = _local_step(x.reshape(B * T, D), mem.reshape(B * M, D), loss_target.reshape(B * T, D), sm, W,
                                  B, T, M, ex)
    g_shard = ex.unpack_grads(*sums)

    sg = {n: g[n] for n in SMALL}
    sg["fox_f_bias"] = g["fox_f_bias"][:, :FOX_H]
    sg["fox_q_norm_g"] = g["fox_q_norm_g"][:, :FOX_D]
    sg["fox_k_norm_g"] = g["fox_k_norm_g"][:, :FOX_D]
    slayout, row0 = {}, 0
    for n in SMALL:
        nr = _rows_of(shard[n].size)
        slayout[n] = (row0, nr)
        row0 += nr
    loss_row = row0
    r_small = _round_up(row0 + 1, 8)

    def pack_small(d, with_loss=None):
        rows = [_to_rows(d[n]) for n in SMALL]
        rows.append(with_loss if with_loss is not None else jnp.zeros((1, LANE), F32))
        return _stack_rows(rows, 0, r_small)

    sgath, = _all_gather([pack_small(sg, g["loss"])], "ag_small")
    s_g, s_d, s_m, s_v = _small_update(sgath, pack_small(shard), pack_small({n: mom[n][0].reshape(shard[n].shape) for n in SMALL}),
                                       pack_small({n: mom[n][1].reshape(shard[n].shape) for n in SMALL}))
    loss = s_g[loss_row, 0]

    grads, deltas, new_m, new_v = {}, {}, {}, {}
    for n in BIG:
        gn = g_shard[n]
        d, nm, nv = _adamw(shard[n], gn, mom[n][0], mom[n][1], "adamw_" + n)
        grads[n], deltas[n], new_m[n], new_v[n] = (a[None] for a in (gn, d, nm, nv))
    for n in SMALL:
        r0, nr = slayout[n]
        for dst, src in ((grads, s_g), (deltas, s_d), (new_m, s_m), (new_v, s_v)):
            dst[n] = _from_rows(src[r0:r0 + nr], given[n].shape)
    return (loss, grad_x.reshape(B, T, D), *[grads[n] for n in order], *[deltas[n] for n in order],
            *[new_m[n] for n in order], *[new_v[n] for n in order])
```

```python
import functools
import math

import jax
import jax.numpy as jnp
from jax import lax
from jax.experimental import pallas as pl
from jax.experimental.pallas import tpu as pltpu

F32, BF16 = jnp.float32, jnp.bfloat16
S = jax.ShapeDtypeStruct
MESH = pl.DeviceIdType.MESH

N_DEV = 8
EPS = 1e-6
LANE = 128
CHUNK = 64
SUB = 16
HG_H, HG_D = 4, 128
HG_GROUP_FWD = 4
HG_GROUP = 2
FOX_H, FOX_D = 8, 64
FOX_P = FOX_H // 2
MEM_H, MEM_D = 4, 128
NEG = -1e30
VMEM_LIMIT = 56 * 2**20

ADAM_LR, ADAM_B1, ADAM_B2, ADAM_EPS, ADAM_WD, ADAM_STEP = 0.001, 0.9, 0.999, 1e-08, 0.01, 10

C_FOX, C_MQ, C_HG, C_GATE, C_FF, C_END = 0, 1536, 2048, 4096, 7168, 7296


def _cp(sem=None):
    return pltpu.CompilerParams(dimension_semantics=sem, vmem_limit_bytes=VMEM_LIMIT)


def _dot(a, b, dims, prec=None):
    return lax.dot_general(a, b, (dims, ((), ())), preferred_element_type=F32, precision=prec)


def _nn(a, b, prec=None):
    return _dot(a, b, ((1,), (0,)), prec)


def _nt(a, b, prec=None):
    return _dot(a, b, ((1,), (1,)), prec)


def _tn(a, b, prec=None):
    return _dot(a, b, ((0,), (0,)), prec)


def _b(x):
    return x.astype(BF16)


def _mm3(fn, a, b):
    ah, bh = _b(a), _b(b)
    return fn(ah, bh) + fn(ah, _b(b - bh.astype(F32))) + fn(_b(a - ah.astype(F32)), bh)


def _iota(shape, dim):
    return lax.broadcasted_iota(jnp.int32, shape, dim)


def _rowsum8(x):
    r, d = x.shape
    return jnp.sum(x.reshape(r // 8, 8, d), axis=0)


def _rmsnorm_cast(x, g, name, tm=512):
    n, d = x.shape

    def body(x_ref, g_ref, o_ref):
        v = x_ref[...]
        r = lax.rsqrt(jnp.mean(v * v, axis=-1, keepdims=True) + EPS)
        o_ref[...] = (v * r * g_ref[...]).astype(BF16)

    return pl.pallas_call(
        body, name=name, grid=(n // tm,),
        in_specs=[pl.BlockSpec((tm, d), lambda i: (i, 0)), pl.BlockSpec((1, d), lambda i: (0, 0))],
        out_specs=pl.BlockSpec((tm, d), lambda i: (i, 0)), out_shape=S((n, d), BF16), compiler_params=_cp(("parallel",)),
    )(x, g)


def _rmsnorm_bwd(dh, x, g, resid, name, tm=512):
    n, d = x.shape
    has_res = resid is not None

    def body(*refs):
        if has_res:
            dh_ref, x_ref, g_ref, r_ref, dx_ref, dg_ref = refs
        else:
            dh_ref, x_ref, g_ref, dx_ref, dg_ref = refs
        v = x_ref[...]
        dhv = dh_ref[...].astype(F32)
        r = lax.rsqrt(jnp.mean(v * v, axis=-1, keepdims=True) + EPS)
        xh = v * r
        u = dhv * g_ref[...]
        dx = r * (u - xh * jnp.mean(u * xh, axis=-1, keepdims=True))
        if has_res:
            dx = dx + r_ref[...]
        dx_ref[...] = dx

        @pl.when(pl.program_id(0) == 0)
        def _():
            dg_ref[...] = jnp.zeros_like(dg_ref)

        dg_ref[...] += _rowsum8(dhv * xh)

    tile = pl.BlockSpec((tm, d), lambda i: (i, 0))
    ins = [tile, tile, pl.BlockSpec((1, d), lambda i: (0, 0))] + ([tile] if has_res else [])
    args = (dh, x, g) + ((resid,) if has_res else ())
    return pl.pallas_call(
        body, name=name, grid=(n // tm,), in_specs=ins,
        out_specs=[tile, pl.BlockSpec((8, d), lambda i: (0, 0))],
        out_shape=[S((n, d), F32), S((8, d), F32)], compiler_params=_cp(("arbitrary",)),
    )(*args)


def _mm_nn(a, b, out_dtype, name, tm, tn):
    m, k = a.shape
    n = b.shape[1]
    assert n % tn == 0 and m % tm == 0

    def body(a_ref, b_ref, o_ref):
        o_ref[...] = _nn(a_ref[...].astype(BF16), b_ref[...].astype(BF16)).astype(out_dtype)

    return pl.pallas_call(
        body, name=name, grid=(n // tn, m // tm),
        in_specs=[pl.BlockSpec((tm, k), lambda j, i: (i, 0)), pl.BlockSpec((k, tn), lambda j, i: (0, j))],
        out_specs=pl.BlockSpec((tm, tn), lambda j, i: (i, j)), out_shape=S((m, n), out_dtype),
        compiler_params=_cp(("parallel", "parallel")),
    )(a, b)


def _mm_nt_sum(parts, w, name, tm, swap=()):
    m = parts[0][0].shape[0]
    k = w.shape[0]
    assert m % tm == 0 and all(c % n == 0 and o % n == 0 for _, c, n, o in parts)
    np_ = len(parts)
    nsw = len(swap)
    n_steps = m // tm

    def body(*refs):
        o_ref = refs[2 * np_ + nsw]
        if nsw:
            start, finish = _chip_swap_phases(refs[2 * np_:2 * np_ + nsw], refs[2 * np_ + nsw + 1:2 * np_ + 2 * nsw + 1],
                                              *refs[2 * np_ + 2 * nsw + 1:])
            pl.when(pl.program_id(0) == 0)(start)
        acc = _nt(refs[0][...].astype(BF16), refs[np_][...].astype(BF16))
        for i in range(1, np_):
            acc = acc + _nt(refs[i][...].astype(BF16), refs[np_ + i][...].astype(BF16))
        o_ref[...] = acc
        if nsw:
            pl.when(pl.program_id(0) == n_steps - 1)(finish)

    dy_specs = [pl.BlockSpec((tm, n), functools.partial(lambda i, j: (i, j), j=c // n)) for _, c, n, _ in parts]
    w_specs = [pl.BlockSpec((k, n), functools.partial(lambda i, j: (0, j), j=o // n)) for _, _, n, o in parts]
    out = pl.pallas_call(
        body, name=name, grid=(n_steps,), in_specs=dy_specs + w_specs + [ANY] * nsw,
        out_specs=[pl.BlockSpec((tm, k), lambda i: (i, 0))] + [ANY] * nsw,
        out_shape=[S((m, k), F32)] + [S(p.shape, p.dtype) for p in swap],
        scratch_shapes=_chip_swap_sems(nsw) if nsw else [],
        compiler_params=_cp(("arbitrary",) if nsw else ("parallel",)),
    )(*([p[0] for p in parts] + [w] * np_ + list(swap)))
    return (out[0], out[1:]) if nsw else out[0]


def _mm_tn(x, dy, name, tm, tn):
    m, k = x.shape
    n = dy.shape[1]
    tm = min(tm, m)
    assert m % tm == 0 and n % tn == 0

    def body(x_ref, dy_ref, o_ref):
        part = _tn(x_ref[...].astype(BF16), dy_ref[...].astype(BF16))

        @pl.when(pl.program_id(1) == 0)
        def _():
            o_ref[...] = part

        @pl.when(pl.program_id(1) > 0)
        def _():
            o_ref[...] += part

    return pl.pallas_call(
        body, name=name, grid=(n // tn, m // tm),
        in_specs=[pl.BlockSpec((tm, k), lambda j, i: (i, 0)), pl.BlockSpec((tm, tn), lambda j, i: (i, j))],
        out_specs=pl.BlockSpec((k, tn), lambda j, i: (0, j)), out_shape=S((k, n), F32),
        compiler_params=_cp(("parallel", "arbitrary")),
    )(x, dy)


def _lower_bound(logits):
    e = jnp.exp(logits - jnp.max(logits, axis=0, keepdims=True))
    return e[0:1, :] / jnp.sum(e, axis=0, keepdims=True)


def _hg_gates(fl, lb):
    sig = jax.nn.sigmoid(fl)
    f = lb + (1.0 - lb) * sig
    k = (1.0 - lb) * (1.0 - sig)
    return sig, f, k, jnp.log(f)


def _silu_and_grad(x):
    s = jax.nn.sigmoid(x)
    return x * s, s * (1.0 + x * (1.0 - s))


def _hg_rowblocks(G):
    return [None] + [G[SUB * i - 1:SUB * i, :] for i in range(1, CHUNK // SUB)]


def _hg_intra_A(qs, k, G):
    refs = _hg_rowblocks(G)
    cols = _iota((SUB, LANE), 1)
    rows = _iota((SUB, LANE), 0)
    no_keys = jnp.zeros((LANE - CHUNK, HG_D), BF16)
    blocks = []
    for i in range(CHUNK // SUB):
        lo = SUB * i
        qb, Gb = qs[lo:lo + SUB, :], G[lo:lo + SUB, :]
        diag = jnp.zeros((SUB, LANE), F32)
        for s in range(SUB):
            e = jnp.exp(jnp.minimum(Gb - G[lo + s:lo + s + 1, :], 0.0))
            col = jnp.sum(qb * k[lo + s:lo + s + 1, :] * e, axis=-1, keepdims=True)
            diag = jnp.where(cols == lo + s, col, diag)
        a = jnp.where((cols >= lo) & (cols <= rows + lo), diag, 0.0)
        if i > 0:
            qr = qb * jnp.exp(Gb - refs[i])
            kr = k * jnp.exp(jnp.minimum(refs[i] - G, 0.0))
            a = jnp.where(cols < lo, _nt(_b(qr), jnp.concatenate([_b(kr), no_keys], axis=0)), a)
        blocks.append(a)
    return jnp.concatenate(blocks, axis=0)


def _hg_intra_bwd(dA, qs, k, G):
    refs = _hg_rowblocks(G)
    cols = _iota((SUB, CHUNK), 1)
    rows16 = _iota((SUB, HG_D), 0)
    dk = jnp.zeros((CHUNK, HG_D), F32)
    dq_blocks, dk_diag_blocks = [], []
    for i in range(CHUNK // SUB):
        lo = SUB * i
        qb, Gb = qs[lo:lo + SUB, :], G[lo:lo + SUB, :]
        dAb = dA[lo:lo + SUB, :]
        dq = jnp.zeros((SUB, HG_D), F32)
        dkb = jnp.zeros((SUB, HG_D), F32)
        for s in range(SUB):
            e = jnp.exp(jnp.minimum(Gb - G[lo + s:lo + s + 1, :], 0.0))
            e = jnp.where(rows16 >= s, e, 0.0)
            dcol = jnp.sum(jnp.where(cols == lo + s, dAb, 0.0), axis=-1, keepdims=True)
            w = dcol * e
            dq = dq + w * k[lo + s:lo + s + 1, :]
            dkb = jnp.where(rows16 == s, jnp.sum(w * qb, axis=0, keepdims=True), dkb)
        if i > 0:
            e1 = jnp.exp(Gb - refs[i])
            e2 = jnp.exp(jnp.minimum(refs[i] - G, 0.0))
            dA_off = jnp.where(cols < lo, dAb, 0.0)
            dq = dq + _mm3(_nn, dA_off, k * e2) * e1
            dk = dk + _mm3(_tn, dA_off, qb * e1) * e2
        dq_blocks.append(dq)
        dk_diag_blocks.append(dkb)
    return jnp.concatenate(dq_blocks, axis=0), dk + jnp.concatenate(dk_diag_blocks, axis=0)


def _tri(n, upper=False):
    r, c = _iota((n, n), 0), _iota((n, n), 1)
    return jnp.where((c >= r) if upper else (r >= c), 1.0, 0.0).astype(BF16)


def _prefix_mm(tri, x):
    hi = x.astype(BF16)
    r1 = x - hi.astype(F32)
    mid = r1.astype(BF16)
    lo = (r1 - mid.astype(F32)).astype(BF16)
    return _nn(tri, hi) + _nn(tri, mid) + _nn(tri, lo)


def _hgrn_fwd(z, lb, gn, B, T):
    N = B * T
    NC = T // CHUNK
    ng = HG_H // HG_GROUP_FWD

    def body(z_ref, lb_ref, gn_ref, y_ref, o_ref, st_ref, a_ref, s_scr):
        lbs = _lower_bound(lb_ref[...])
        tri = _tri(CHUNK)
        s_scr[...] = jnp.zeros_like(s_scr)

        def chunk(c, carry):
            r = pl.ds(pl.multiple_of(c * CHUNK, CHUNK), CHUNK)
            for hh in range(HG_GROUP_FWD):
                zc, oc = 4 * LANE * hh, LANE * hh
                ql, fl, il, gl = (z_ref[r, zc + LANE * j:zc + LANE * (j + 1)].astype(F32) for j in range(4))
                _, _, k, logf = _hg_gates(fl, lbs[:, oc:oc + LANE])
                G = _prefix_mm(tri, logf)
                qs = ql * jax.nn.sigmoid(ql)
                st = s_scr[hh]
                st_ref[hh * NC + c] = st
                g_last = G[CHUNK - 1:CHUNK, :]
                A = _b(_hg_intra_A(qs, k, G))
                a_ref[r, oc:oc + LANE] = A
                o = _nn(A[:, 0:CHUNK], _b(il)) + _nt(_b(qs * jnp.exp(G)), _b(st))
                s_scr[hh] = st * jnp.exp(g_last) + _mm3(_tn, il, k * jnp.exp(g_last - G))
                o_ref[r, oc:oc + LANE] = o
                rstd = lax.rsqrt(jnp.mean(o * o, axis=-1, keepdims=True) + EPS)
                y_ref[r, oc:oc + LANE] = (o * rstd * gn_ref[...] * (gl * jax.nn.sigmoid(gl))).astype(BF16)
            return carry

        lax.fori_loop(0, NC, chunk, 0)

    gw = HG_GROUP_FWD * LANE
    cb = C_HG // (4 * gw)
    return pl.pallas_call(
        body, name="hgrn_fwd", grid=(B, ng),
        in_specs=[pl.BlockSpec((T, 4 * gw), lambda b, h: (b, cb + h)), pl.BlockSpec((lb.shape[0], gw), lambda b, h: (0, h)),
                  pl.BlockSpec((1, LANE), lambda b, h: (0, 0))],
        out_specs=[pl.BlockSpec((T, gw), lambda b, h: (b, h)), pl.BlockSpec((T, gw), lambda b, h: (b, h)),
                   pl.BlockSpec((HG_GROUP_FWD * NC, HG_D, HG_D), lambda b, h: (b * ng + h, 0, 0)),
                   pl.BlockSpec((T, gw), lambda b, h: (b, h))],
        out_shape=[S((N, 512), BF16), S((N, 512), F32), S((B * HG_H * NC, HG_D, HG_D), F32), S((N, 512), BF16)],
        scratch_shapes=[pltpu.VMEM((HG_GROUP_FWD, HG_D, HG_D), F32)], compiler_params=_cp(("parallel", "parallel")),
    )(z, lb, gn)


def _hgrn_bwd(z, o_raw, states, a_mat, dy, lb, gn, B, T, swap_sibling=()):
    N = B * T
    NC = T // CHUNK
    ng = HG_H // HG_GROUP
    nsw = len(swap_sibling)

    def body(z_ref, o_ref, st_ref, a_ref, dy_ref, lb_ref, gn_ref, dz_ref, dlb_ref, dgn_ref, ds_scr, racc, dgn_acc):
        lbs = _lower_bound(lb_ref[...])
        gn_v = gn_ref[...]
        tri, triu = _tri(CHUNK), _tri(CHUNK, upper=True)
        cmask = _iota((CHUNK, CHUNK), 0) >= _iota((CHUNK, CHUNK), 1)
        for ref in (ds_scr, racc, dgn_acc, dlb_ref):
            ref[...] = jnp.zeros_like(ref)

        def chunk(ci, carry):
            c = NC - 1 - ci
            r = pl.ds(pl.multiple_of(c * CHUNK, CHUNK), CHUNK)
            for hh in range(HG_GROUP):
                zc, oc = 4 * LANE * hh, LANE * hh
                lb_v = lbs[:, oc:oc + LANE]
                ql, fl, il, gl = (z_ref[r, zc + LANE * j:zc + LANE * (j + 1)].astype(F32) for j in range(4))
                sig, f, k, logf = _hg_gates(fl, lb_v)
                G = _prefix_mm(tri, logf)
                qs, dsilu_q = _silu_and_grad(ql)
                gs, dsilu_g = _silu_and_grad(gl)
                o = o_ref[r, oc:oc + LANE]
                dyv = dy_ref[r, oc:oc + LANE]
                rstd = lax.rsqrt(jnp.mean(o * o, axis=-1, keepdims=True) + EPS)
                oh = o * rstd
                dgl = dyv * oh * gn_v * dsilu_g
                dn = dyv * gs
                dgn_acc[...] += _rowsum8(dn * oh)
                u = dn * gn_v
                do = rstd * (u - oh * jnp.mean(u * oh, axis=-1, keepdims=True))
                st = st_ref[hh * NC + c]
                dst = ds_scr[hh]
                eG = jnp.exp(G)
                g_last = G[CHUNK - 1:CHUNK, :]
                eL = jnp.exp(g_last - G)
                dA = jnp.where(cmask, _mm3(_nt, do, il), 0.0)
                dq_in, dk_in = _hg_intra_bwd(dA, qs, k, G)
                di = _tn(a_ref[r, oc:oc + LANE][:, 0:CHUNK], _b(do)) + _nt(_b(k * eL), _b(dst))
                dq = dq_in + _mm3(_nn, do, st) * eG
                dk = dk_in + _mm3(_nn, il, dst) * eL
                ds_scr[hh] = dst * jnp.exp(g_last) + _mm3(_tn, do, qs * eG)
                dd = qs * dq - k * dk
                dlogf = _prefix_mm(triu, dd) + racc[hh]
                racc[hh] += jnp.sum(dd, axis=0, keepdims=True)
                df = dlogf / f - dk
                dlb_ref[8 * hh:8 * (hh + 1), :] += _rowsum8(df * (1.0 - sig))
                dz_ref[r, zc:zc + LANE] = (dq * dsilu_q).astype(BF16)
                dz_ref[r, zc + LANE:zc + 2 * LANE] = (df * (1.0 - lb_v) * sig * (1.0 - sig)).astype(BF16)
                dz_ref[r, zc + 2 * LANE:zc + 3 * LANE] = di.astype(BF16)
                dz_ref[r, zc + 3 * LANE:zc + 4 * LANE] = dgl.astype(BF16)
            return carry

        lax.fori_loop(0, NC, chunk, 0)
        dgn_ref[...] = dgn_acc[...]

    gw = HG_GROUP * LANE
    cb = C_HG // (4 * gw)
    col = pl.BlockSpec((T, gw), lambda b, h: (b, h))
    if nsw:
        body = _hosting(body, 7, 3, 3, nsw, _sibling_swap_phases, (B, ng))
    return pl.pallas_call(
        body, name="hgrn_bwd", grid=(B, ng),
        in_specs=[pl.BlockSpec((T, 4 * gw), lambda b, h: (b, cb + h)), col,
                  pl.BlockSpec((HG_GROUP * NC, HG_D, HG_D), lambda b, h: (b * ng + h, 0, 0)), col, col,
                  pl.BlockSpec((lb.shape[0], gw), lambda b, h: (0, h)), pl.BlockSpec((1, LANE), lambda b, h: (0, 0))]
        + [ANY] * nsw,
        out_specs=[pl.BlockSpec((T, 4 * gw), lambda b, h: (b, h)),
                   pl.BlockSpec((8 * HG_GROUP, LANE), lambda b, h: (b * ng + h, 0)),
                   pl.BlockSpec((8, LANE), lambda b, h: (b * ng + h, 0))] + [ANY] * nsw,
        out_shape=[S((N, 2048), BF16), S((B * HG_H * 8, LANE), F32), S((B * ng * 8, LANE), F32)]
        + _sibling_swap_shapes(swap_sibling),
        scratch_shapes=[pltpu.VMEM((HG_GROUP, HG_D, HG_D), F32), pltpu.VMEM((HG_GROUP, 1, LANE), F32),
                        pltpu.VMEM((8, LANE), F32)] + (_sibling_swap_sems(nsw) if nsw else []),
        compiler_params=_cp(("arbitrary", "arbitrary") if nsw else ("parallel", "parallel")),
    )(z, o_raw, states, a_mat, dy, lb, gn, *swap_sibling)


def _pair_mean(x, lo_half):
    a = jnp.sum(jnp.where(lo_half, x, 0.0), axis=-1, keepdims=True)
    b = jnp.sum(jnp.where(lo_half, 0.0, x), axis=-1, keepdims=True)
    return jnp.where(lo_half, a, b) * (1.0 / FOX_D)


def _fox_gate_fwd(z, bias, B, T):
    N = B * T
    tb = LANE

    def body(z_ref, b_ref, fc_ref, fct_ref):
        tri = _tri(tb)

        def step(i, carry):
            r = pl.ds(pl.multiple_of(i * tb, tb), tb)
            cs = _prefix_mm(tri, jax.nn.log_sigmoid(z_ref[r, :].astype(F32) + b_ref[...])) + carry
            fc_ref[r, :] = cs
            fct_ref[0, :, r] = cs.T[0:8, :]
            return cs[tb - 1:tb, :]

        lax.fori_loop(0, T // tb, step, jnp.zeros((1, LANE), F32))

    return pl.pallas_call(
        body, name="fox_gate_fwd", grid=(B,),
        in_specs=[pl.BlockSpec((T, LANE), lambda b: (b, C_FF // LANE)), pl.BlockSpec((1, LANE), lambda b: (0, 0))],
        out_specs=[pl.BlockSpec((T, LANE), lambda b: (b, 0)), pl.BlockSpec((1, 8, T), lambda b: (b, 0, 0))],
        out_shape=[S((N, LANE), F32), S((B, 8, T), F32)], compiler_params=_cp(("parallel",)),
    )(z, bias)


def _fox_gate_bwd(dfc, z, bias, B, T):
    N = B * T
    tb = LANE
    nt = T // tb

    def body(d_ref, z_ref, b_ref, dz_ref, db_ref):
        triu = _tri(tb, upper=True)
        db_ref[...] = jnp.zeros_like(db_ref)

        def step(ii, carry):
            r = pl.ds(pl.multiple_of((nt - 1 - ii) * tb, tb), tb)
            d = d_ref[r, 0:LANE]
            for p in range(1, FOX_P):
                d = d + d_ref[r, LANE * p:LANE * (p + 1)]
            rc = _prefix_mm(triu, d) + carry
            dff = rc * jax.nn.sigmoid(-(z_ref[r, :].astype(F32) + b_ref[...]))
            dz_ref[r, :] = dff.astype(BF16)
            db_ref[...] += _rowsum8(dff)
            return carry + jnp.sum(d, axis=0, keepdims=True)

        lax.fori_loop(0, nt, step, jnp.zeros((1, LANE), F32))

    return pl.pallas_call(
        body, name="fox_gate_bwd", grid=(B,),
        in_specs=[pl.BlockSpec((T, 512), lambda b: (b, 0)), pl.BlockSpec((T, LANE), lambda b: (b, C_FF // LANE)),
                  pl.BlockSpec((1, LANE), lambda b: (0, 0))],
        out_specs=[pl.BlockSpec((T, LANE), lambda b: (b, 0)), pl.BlockSpec((8, LANE), lambda b: (b, 0))],
        out_shape=[S((N, LANE), BF16), S((B * 8, LANE), F32)], compiler_params=_cp(("parallel",)),
    )(dfc, z, bias)


def _fox_prep(z_ref, gq, gk, r, lo_half):
    q, k, v = (z_ref[r, LANE * j:LANE * (j + 1)].astype(F32) for j in range(3))
    rq = lax.rsqrt(_pair_mean(q * q, lo_half) + EPS)
    rk = lax.rsqrt(_pair_mean(k * k, lo_half) + EPS)
    qh, kh = q * rq, k * rk
    return qh * gq * (FOX_D ** -0.5), kh * gk, v, qh, kh, rq, rk


def _fox_fwd(z, fc, fct, gq, gk, B, T, tq=512, gather=()):
    N = B * T
    NQ = T // tq
    nga = len(gather)

    def body(z_ref, fc_ref, fct_ref, gq_ref, gk_ref, y_ref, lse_ref, qn_s, kn_s, v_s):
        p, qi = pl.program_id(1), pl.program_id(2)
        lo_half = _iota((1, LANE), 1) < FOX_D

        @pl.when(qi == 0)
        def _():
            def prep(i, carry):
                r = pl.ds(pl.multiple_of(i * tq, tq), tq)
                qn, kn, v = _fox_prep(z_ref, gq_ref[...], gk_ref[...], r, lo_half)[:3]
                qn_s[r, :], kn_s[r, :], v_s[r, :] = qn.astype(BF16), kn.astype(BF16), v.astype(BF16)
                return carry
            lax.fori_loop(0, NQ, prep, 0)

        rq = pl.ds(pl.multiple_of(qi * tq, tq), tq)
        qn = qn_s[rq, :]
        fcq = fc_ref[rq, :]
        lane = _iota((tq, LANE), 1)
        causal = _iota((tq, tq), 0) >= _iota((tq, tq), 1)
        qhs = [jnp.where(lo_half, qn, jnp.zeros_like(qn)), jnp.where(lo_half, jnp.zeros_like(qn), qn)]
        fqs = [jnp.sum(jnp.where(lane == 2 * p + hh, fcq, 0.0), axis=-1, keepdims=True) for hh in range(2)]

        def kv(j, carry, diagonal):
            rk = pl.ds(pl.multiple_of(j * tq, tq), tq)
            kj, vj = kn_s[rk, :], v_s[rk, :]
            one = jnp.ones_like(vj)
            new = []
            for hh in range(2):
                m, acc = carry[hh]
                s = _nt(qhs[hh], kj) + fqs[hh] - fct_ref[0, pl.ds(2 * p + hh, 1), rk]
                if diagonal:
                    s = jnp.where(causal, s, NEG)
                m_new = jnp.maximum(m, jnp.max(s, axis=-1, keepdims=True))
                pe = jnp.exp(s - m_new)
                v_aug = jnp.where(lo_half if hh == 0 else jnp.logical_not(lo_half), vj, one)
                new.append((m_new, jnp.exp(m - m_new) * acc + _nn(pe.astype(BF16), v_aug)))
            return tuple(new)

        init = tuple((jnp.full((tq, 1), NEG, F32), jnp.zeros((tq, LANE), F32)) for _ in range(2))
        carry = lax.fori_loop(0, qi, functools.partial(kv, diagonal=False), init)
        (m0, a0), (m1, a1) = kv(qi, carry, True)
        l0, l1 = a0[:, FOX_D:FOX_D + 1], a1[:, 0:1]
        y_ref[...] = jnp.where(lo_half, a0 / l0, a1 / l1).astype(BF16)
        lse_ref[...] = jnp.where(lo_half, m0 + jnp.log(l0), m1 + jnp.log(l1))

    vec = pl.BlockSpec((1, LANE), lambda b, p, q: (0, 0))
    tile = pl.BlockSpec((tq, LANE), lambda b, p, q: (b * NQ + q, p))
    if nga:
        body = _hosting(body, 5, 2, 3, nga, _gather_phases, (B, FOX_P, NQ))
    return pl.pallas_call(
        body, name="fox_fwd", grid=(B, FOX_P, NQ),
        in_specs=[pl.BlockSpec((T, 384), lambda b, p, q: (b, p)), pl.BlockSpec((T, LANE), lambda b, p, q: (b, 0)),
                  pl.BlockSpec((1, 8, T), lambda b, p, q: (b, 0, 0)), vec, vec] + [ANY] * nga,
        out_specs=[tile, tile] + [ANY] * nga, out_shape=[S((N, 512), BF16), S((N, 512), F32)] + _gather_shapes(gather),
        scratch_shapes=[pltpu.VMEM((T, LANE), BF16)] * 3 + (_gather_sems(nga) if nga else []),
        compiler_params=_cp(("arbitrary",) * 3 if nga else ("parallel", "parallel", "arbitrary")),
    )(z, fc, fct, gq, gk, *gather)


def _fox_bwd(z, dy, y, lse, fc, fct, gq, gk, B, T, tq=512, swap=()):
    N = B * T
    NQ = T // tq
    nsw = len(swap)

    def body(z_ref, dy_ref, y_ref, lse_ref, fc_ref, fct_ref, gq_ref, gk_ref, dz_ref, dfc_ref, dgq_ref, dgk_ref,
             qn_s, kn_s, v_s, do_s, delta_s, dq_s, dfk_s):
        p, kj = pl.program_id(1), pl.program_id(2)
        lo_half = _iota((1, LANE), 1) < FOX_D
        lane = _iota((tq, LANE), 1)
        gq_v, gk_v = gq_ref[...], gk_ref[...]

        @pl.when(kj == 0)
        def _():
            def prep(i, carry):
                r = pl.ds(pl.multiple_of(i * tq, tq), tq)
                qn, kn, v = _fox_prep(z_ref, gq_v, gk_v, r, lo_half)[:3]
                qn_s[r, :], kn_s[r, :], v_s[r, :] = qn.astype(BF16), kn.astype(BF16), v.astype(BF16)
                do = dy_ref[r, :]
                do_s[r, :] = do.astype(BF16)
                delta_s[r, :] = _pair_mean(do * y_ref[r, :].astype(F32), lo_half) * float(FOX_D)
                return carry
            lax.fori_loop(0, NQ, prep, 0)
            dq_s[...] = jnp.zeros_like(dq_s)
            dgq_ref[...] = jnp.zeros_like(dgq_ref)
            dgk_ref[...] = jnp.zeros_like(dgk_ref)

        rk = pl.ds(pl.multiple_of(kj * tq, tq), tq)
        kn, vv = kn_s[rk, :], v_s[rk, :]
        causal = _iota((tq, tq), 0) >= _iota((tq, tq), 1)
        zero, one = jnp.zeros_like(kn), jnp.ones_like(kn)
        hms = [lo_half, jnp.logical_not(lo_half)]
        kmasks = [jnp.where(hm, kn, zero) for hm in hms]
        kaugs = [jnp.where(hm, kn, one) for hm in hms]
        vmasks = [jnp.where(hm, vv, zero) for hm in hms]
        fks = [fct_ref[0, pl.ds(2 * p + hh, 1), rk] for hh in range(2)]

        def qloop(i, carry, diagonal):
            ri = pl.ds(pl.multiple_of(i * tq, tq), tq)
            qn = qn_s[ri, :]
            do = do_s[ri, :]
            fcq = fc_ref[ri, :]
            new = []
            for hh in range(2):
                dk_acc, dv_acc = carry[hh]
                c0 = FOX_D * hh
                fq = jnp.sum(jnp.where(lane == 2 * p + hh, fcq, 0.0), axis=-1, keepdims=True)
                pr = jnp.exp(_nt(qn, kmasks[hh]) + fq - fks[hh] - lse_ref[ri, c0:c0 + 1])
                if diagonal:
                    pr = jnp.where(causal, pr, 0.0)
                ds = (pr * (_nt(do, vmasks[hh]) - delta_s[ri, c0:c0 + 1])).astype(BF16)
                dq_s[hh, ri, :] += _nn(ds, kaugs[hh])
                new.append((dk_acc + _tn(jnp.where(hms[hh], qn, one), ds), dv_acc + _tn(do, pr.astype(BF16))))
            return tuple(new)

        init = tuple((jnp.zeros((LANE, tq), F32), jnp.zeros((LANE, tq), F32)) for _ in range(2))
        carry = qloop(kj, init, True)
        (dk0, dv0), (dk1, dv1) = lax.fori_loop(kj + 1, NQ, functools.partial(qloop, diagonal=False), carry)
        dks, dvs = [dk0.T, dk1.T], [dv0.T, dv1.T]

        dkn = jnp.where(lo_half, dks[0], dks[1])
        _, _, _, _, kh, _, rkk = _fox_prep(z_ref, gq_v, gk_v, rk, lo_half)
        u = dkn * gk_v
        dz_ref[rk, LANE:2 * LANE] = (rkk * (u - kh * _pair_mean(u * kh, lo_half))).astype(BF16)
        dz_ref[rk, 2 * LANE:3 * LANE] = jnp.where(lo_half, dvs[0], dvs[1]).astype(BF16)
        dgk_ref[...] += _rowsum8(dkn * kh)
        dfk_s[rk, :] = jnp.where(lane == 2 * p, -dks[0][:, FOX_D:FOX_D + 1],
                                 jnp.where(lane == 2 * p + 1, -dks[1][:, 0:1], 0.0))

        @pl.when(kj == NQ - 1)
        def _():
            def fin(i, carry):
                r = pl.ds(pl.multiple_of(i * tq, tq), tq)
                d0, d1 = dq_s[0, r, :], dq_s[1, r, :]
                dqn = jnp.where(lo_half, d0, d1)
                _, _, _, qh, _, rqq, _ = _fox_prep(z_ref, gq_v, gk_v, r, lo_half)
                u = dqn * gq_v * (FOX_D ** -0.5)
                dz_ref[r, 0:LANE] = (rqq * (u - qh * _pair_mean(u * qh, lo_half))).astype(BF16)
                dgq_ref[...] += _rowsum8(dqn * qh) * (FOX_D ** -0.5)
                dfc_ref[r, :] = dfk_s[r, :] + jnp.where(lane == 2 * p, d0[:, FOX_D:FOX_D + 1],
                                                        jnp.where(lane == 2 * p + 1, d1[:, 0:1], 0.0))
                return carry
            lax.fori_loop(0, NQ, fin, 0)

    vec = pl.BlockSpec((1, LANE), lambda b, p, k: (0, 0))
    col = pl.BlockSpec((T, LANE), lambda b, p, k: (b, p))
    part = pl.BlockSpec((8, LANE), lambda b, p, k: (b * FOX_P + p, 0))
    if nsw:
        body = _hosting(body, 8, 4, 7, nsw, _chip_swap_phases, (B, FOX_P, NQ))
    return pl.pallas_call(
        body, name="fox_bwd", grid=(B, FOX_P, NQ),
        in_specs=[pl.BlockSpec((T, 384), lambda b, p, k: (b, p)), col, col, col,
                  pl.BlockSpec((T, LANE), lambda b, p, k: (b, 0)), pl.BlockSpec((1, 8, T), lambda b, p, k: (b, 0, 0)),
                  vec, vec] + [ANY] * nsw,
        out_specs=[pl.BlockSpec((T, 384), lambda b, p, k: (b, p)), col, part, part] + [ANY] * nsw,
        out_shape=[S((N, 1536), BF16), S((N, 512), F32), S((B * FOX_P * 8, LANE), F32), S((B * FOX_P * 8, LANE), F32)]
        + [S(p.shape, p.dtype) for p in swap],
        scratch_shapes=[pltpu.VMEM((T, LANE), BF16)] * 4 + [pltpu.VMEM((T, LANE), F32), pltpu.VMEM((2, T, LANE), F32),
                                                            pltpu.VMEM((T, LANE), F32)]
        + (_chip_swap_sems(nsw) if nsw else []),
        compiler_params=_cp(("arbitrary",) * 3 if nsw else ("parallel", "parallel", "arbitrary")),
    )(z, dy, y, lse, fc, fct, gq, gk, *swap)


def _mem_scores(z_ref, kv_ref, gq, gk, h):
    c = slice(MEM_D * h, MEM_D * (h + 1))
    q, k = z_ref[:, c].astype(F32), kv_ref[:, c]
    rq = lax.rsqrt(jnp.mean(q * q, axis=-1, keepdims=True) + EPS)
    rk = lax.rsqrt(jnp.mean(k * k, axis=-1, keepdims=True) + EPS)
    qh, kh = q * rq, k * rk
    qn = (qh * gq * (MEM_D ** -0.5)).astype(BF16)
    kn = (kh * gk).astype(BF16)
    s = _nt(qn, kn)
    pe = jnp.exp(s - jnp.max(s, axis=-1, keepdims=True))
    pn = pe / jnp.sum(pe, axis=-1, keepdims=True)
    return pn, qn, kn, qh, kh, rq, rk


def _mem_fwd(z, memkv, gq, gk, B, T, M, tq=512):
    N = B * T
    NQ = T // tq
    W = MEM_H * MEM_D

    def body(z_ref, kv_ref, gq_ref, gk_ref, y_ref):
        for h in range(MEM_H):
            pn = _mem_scores(z_ref, kv_ref, gq_ref[...], gk_ref[...], h)[0]
            v = kv_ref[:, W + MEM_D * h:W + MEM_D * (h + 1)].astype(BF16)
            y_ref[:, MEM_D * h:MEM_D * (h + 1)] = _nn(pn.astype(BF16), v).astype(BF16)

    vec = pl.BlockSpec((1, LANE), lambda b, q: (0, 0))
    return pl.pallas_call(
        body, name="mem_fwd", grid=(B, NQ),
        in_specs=[pl.BlockSpec((tq, W), lambda b, q: (b * NQ + q, C_MQ // W)),
                  pl.BlockSpec((M, 2 * W), lambda b, q: (b, 0)), vec, vec],
        out_specs=pl.BlockSpec((tq, W), lambda b, q: (b * NQ + q, 0)), out_shape=S((N, W), BF16),
        compiler_params=_cp(("parallel", "parallel")),
    )(z, memkv, gq, gk)


def _mem_bwd(z, memkv, dy, gq, gk, B, T, M, tq=512):
    N = B * T
    NQ = T // tq
    W = MEM_H * MEM_D

    def body(z_ref, kv_ref, dy_ref, gq_ref, gk_ref, dz_ref, dkv_ref, dgq_ref, dgk_ref, acc):
        qi = pl.program_id(1)
        gq_v, gk_v = gq_ref[...], gk_ref[...]

        @pl.when(qi == 0)
        def _():
            acc[...] = jnp.zeros_like(acc)
            dgq_ref[...] = jnp.zeros_like(dgq_ref)
            dgk_ref[...] = jnp.zeros_like(dgk_ref)

        for h in range(MEM_H):
            c = slice(MEM_D * h, MEM_D * (h + 1))
            cv = slice(W + MEM_D * h, W + MEM_D * (h + 1))
            pn, qn, kn, qh, _, rq, _ = _mem_scores(z_ref, kv_ref, gq_v, gk_v, h)
            do = dy_ref[:, c].astype(BF16)
            dp = _nt(do, kv_ref[:, cv].astype(BF16))
            ds = (pn * (dp - jnp.sum(dp * pn, axis=-1, keepdims=True))).astype(BF16)
            dqn = _nn(ds, kn)
            acc[:, c] += _tn(ds, qn)
            acc[:, cv] += _tn(pn.astype(BF16), do)
            u = dqn * gq_v * (MEM_D ** -0.5)
            dz_ref[:, c] = (rq * (u - qh * jnp.mean(u * qh, axis=-1, keepdims=True))).astype(BF16)
            dgq_ref[...] += _rowsum8(dqn * qh) * (MEM_D ** -0.5)

        @pl.when(qi == NQ - 1)
        def _():
            for h in range(MEM_H):
                c = slice(MEM_D * h, MEM_D * (h + 1))
                cv = slice(W + MEM_D * h, W + MEM_D * (h + 1))
                k = kv_ref[:, c]
                rk = lax.rsqrt(jnp.mean(k * k, axis=-1, keepdims=True) + EPS)
                kh = k * rk
                dkn = acc[:, c]
                u = dkn * gk_v
                dkv_ref[:, c] = (rk * (u - kh * jnp.mean(u * kh, axis=-1, keepdims=True))).astype(BF16)
                dkv_ref[:, cv] = acc[:, cv].astype(BF16)
                dgk_ref[...] += _rowsum8(dkn * kh)

    vec = pl.BlockSpec((1, LANE), lambda b, q: (0, 0))
    part = pl.BlockSpec((8, LANE), lambda b, q: (b, 0))
    return pl.pallas_call(
        body, name="mem_bwd", grid=(B, NQ),
        in_specs=[pl.BlockSpec((tq, W), lambda b, q: (b * NQ + q, C_MQ // W)),
                  pl.BlockSpec((M, 2 * W), lambda b, q: (b, 0)), pl.BlockSpec((tq, W), lambda b, q: (b * NQ + q, 0)),
                  vec, vec],
        out_specs=[pl.BlockSpec((tq, W), lambda b, q: (b * NQ + q, 0)), pl.BlockSpec((M, 2 * W), lambda b, q: (b, 0)),
                   part, part],
        out_shape=[S((N, W), BF16), S((B * M, 2 * W), BF16), S((B * 8, LANE), F32), S((B * 8, LANE), F32)],
        scratch_shapes=[pltpu.VMEM((M, 2 * W), F32)], compiler_params=_cp(("parallel", "arbitrary")),
    )(z, memkv, dy, gq, gk)


def _merge_fwd(ya, yb, yc, z, x, wa, wb, wc, wo, tm=256):
    n, d = x.shape
    wdt = ya.shape[1]
    gb = C_GATE // d

    def body(ya_ref, yb_ref, yc_ref, g0_ref, g1_ref, g2_ref, x_ref, wa_ref, wb_ref, wc_ref, wo_ref,
             x1_ref, mg_ref, ua_ref, ub_ref, uc_ref):
        merged = jnp.zeros((tm, d), F32)
        for y_ref, g_ref, w_ref, u_ref in ((ya_ref, g0_ref, wa_ref, ua_ref), (yb_ref, g1_ref, wb_ref, ub_ref),
                                           (yc_ref, g2_ref, wc_ref, uc_ref)):
            u = _nn(y_ref[...], w_ref[...])
            u_ref[...] = u.astype(BF16)
            merged = merged + jax.nn.sigmoid(g_ref[...].astype(F32)) * u
        mb = merged.astype(BF16)
        mg_ref[...] = mb
        x1_ref[...] = x_ref[...] + _nn(mb, wo_ref[...])

    yt = pl.BlockSpec((tm, wdt), lambda i: (i, 0))
    xt = pl.BlockSpec((tm, d), lambda i: (i, 0))
    wbr = pl.BlockSpec((wdt, d), lambda i: (0, 0))
    gates = [pl.BlockSpec((tm, d), functools.partial(lambda i, k: (i, gb + k), k=k)) for k in range(3)]
    return pl.pallas_call(
        body, name="merge_fwd", grid=(n // tm,),
        in_specs=[yt, yt, yt] + gates + [xt, wbr, wbr, wbr, pl.BlockSpec((d, d), lambda i: (0, 0))],
        out_specs=[xt] * 5, out_shape=[S((n, d), F32)] + [S((n, d), BF16)] * 4, compiler_params=_cp(("parallel",)),
    )(ya, yb, yc, z, z, z, x, wa, wb, wc, wo)


def _merge_bwd(dx1, z, ua, ub, uc, wa, wb, wc, wo, tm=256):
    n, d = dx1.shape
    wdt = wa.shape[0]
    gb = C_GATE // d

    def body(dx_ref, g0_ref, g1_ref, g2_ref, ua_ref, ub_ref, uc_ref, wa_ref, wb_ref, wc_ref, wo_ref,
             dg_ref, dya_ref, dyb_ref, dyc_ref, dua_ref, dub_ref, duc_ref):
        dm = _nt(dx_ref[...].astype(BF16), wo_ref[...])
        for k, (g_ref, u_ref, w_ref, dy_ref, du_ref) in enumerate((
                (g0_ref, ua_ref, wa_ref, dya_ref, dua_ref), (g1_ref, ub_ref, wb_ref, dyb_ref, dub_ref),
                (g2_ref, uc_ref, wc_ref, dyc_ref, duc_ref))):
            g = jax.nn.sigmoid(g_ref[...].astype(F32))
            du = (dm * g).astype(BF16)
            du_ref[...] = du
            dg_ref[:, d * k:d * (k + 1)] = (dm * u_ref[...].astype(F32) * g * (1.0 - g)).astype(BF16)
            dy_ref[...] = _nt(du, w_ref[...])

    yt = pl.BlockSpec((tm, wdt), lambda i: (i, 0))
    xt = pl.BlockSpec((tm, d), lambda i: (i, 0))
    wbr = pl.BlockSpec((wdt, d), lambda i: (0, 0))
    gates = [pl.BlockSpec((tm, d), functools.partial(lambda i, k: (i, gb + k), k=k)) for k in range(3)]
    return pl.pallas_call(
        body, name="merge_bwd", grid=(n // tm,),
        in_specs=[xt] + gates + [xt, xt, xt, wbr, wbr, wbr, pl.BlockSpec((d, d), lambda i: (0, 0))],
        out_specs=[pl.BlockSpec((tm, 3 * d), lambda i: (i, 0)), yt, yt, yt, xt, xt, xt],
        out_shape=[S((n, 3 * d), BF16)] + [S((n, wdt), F32)] * 3 + [S((n, d), BF16)] * 3,
        compiler_params=_cp(("parallel",)),
    )(dx1, z, z, z, ua, ub, uc, wa, wb, wc, wo)


FFN_TN = 1408
TN_TM = 2048
INV_SQRT2 = 0.7071067811865476
INV_SQRT_2PI = 0.3989422804014327


def _conv_shifted(a, prev, first, tm):
    row = _iota(a.shape, 0)
    p7 = jnp.where(first, 0.0, prev[7:8, :])
    p6 = jnp.where(first, 0.0, prev[6:7, :])
    a1 = jnp.where(row == 0, p7, pltpu.roll(a, 1, 0))
    a2 = jnp.where(row == 0, p6, jnp.where(row == 1, p7, pltpu.roll(a, 2, 0)))
    return a1, a2


def _ffn_act_fwd(up, cw, cb, B, T, tm=256):
    N = B * T
    dff = cw.shape[1]
    NT, NJ, tn = T // tm, dff // FFN_TN, FFN_TN

    def body(a_ref, v_ref, cw_ref, cb_ref, y_ref, c_ref, carry):
        t = pl.program_id(2)
        a = a_ref[...].astype(F32)
        a1, a2 = _conv_shifted(a, carry[...], t == 0, tm)
        w = cw_ref[...]
        ac = w[0:1, :] * a2 + w[1:2, :] * a1 + w[2:3, :] * a + cb_ref[...]
        cdf = 0.5 * (1.0 + lax.erf(ac * INV_SQRT2))
        y_ref[...] = (ac * cdf * v_ref[...].astype(F32)).astype(BF16)
        c_ref[...] = cdf.astype(BF16)
        carry[...] = a[tm - 8:tm, :]

    return pl.pallas_call(
        body, name="ffn_act_fwd", grid=(B, NJ, NT),
        in_specs=[pl.BlockSpec((tm, tn), lambda b, j, t: (b * NT + t, j)),
                  pl.BlockSpec((tm, tn), lambda b, j, t: (b * NT + t, NJ + j)),
                  pl.BlockSpec((3, tn), lambda b, j, t: (0, j)), pl.BlockSpec((1, tn), lambda b, j, t: (0, j))],
        out_specs=[pl.BlockSpec((tm, tn), lambda b, j, t: (b * NT + t, j))] * 2, out_shape=[S((N, dff), BF16)] * 2,
        scratch_shapes=[pltpu.VMEM((8, tn), F32)], compiler_params=_cp(("parallel", "parallel", "arbitrary")),
    )(up, up, cw, cb)


def _ffn_down_loss(y, wd, x1, tgt, tm=256):
    n, d = x1.shape
    kf = y.shape[1]

    def body(y_ref, w_ref, x_ref, t_ref, dx_ref, ls_ref):
        err = x_ref[...] + _nn(y_ref[...], w_ref[...]) - t_ref[...]
        dx_ref[...] = err * (1.0 / d)

        @pl.when(pl.program_id(0) == 0)
        def _():
            ls_ref[...] = jnp.zeros_like(ls_ref)

        ls_ref[...] += _rowsum8(err * err) * (0.5 / d)

    xt = pl.BlockSpec((tm, d), lambda i: (i, 0))
    return pl.pallas_call(
        body, name="ffn_down_loss", grid=(n // tm,),
        in_specs=[pl.BlockSpec((tm, kf), lambda i: (i, 0)), pl.BlockSpec((kf, d), lambda i: (0, 0)), xt, xt],
        out_specs=[xt, pl.BlockSpec((8, d), lambda i: (0, 0))], out_shape=[S((n, d), F32), S((8, d), F32)],
        compiler_params=_cp(("arbitrary",)),
    )(y, wd, x1, tgt)


def _ffn_act_bwd1(dx2, wd, up, cdf, cw, cb, B, T, tm=256):
    N = B * T
    d = dx2.shape[1]
    dff = cw.shape[1]
    NT, NJ, tn = T // tm, dff // FFN_TN, FFN_TN

    def body(dx_ref, w_ref, a_ref, v_ref, c_ref, cw_ref, cb_ref, dac_ref, dv_ref, dcw_ref, dcb_ref, carry):
        b, t = pl.program_id(1), pl.program_id(2)
        a = a_ref[...].astype(F32)
        a1, a2 = _conv_shifted(a, carry[...], t == 0, tm)
        carry[...] = a[tm - 8:tm, :]
        w = cw_ref[...]
        ac = w[0:1, :] * a2 + w[1:2, :] * a1 + w[2:3, :] * a + cb_ref[...]
        dy = _nt(dx_ref[...].astype(BF16), w_ref[...])
        cdf = c_ref[...].astype(F32)
        dv_ref[...] = (dy * ac * cdf).astype(BF16)
        dac = dy * v_ref[...].astype(F32) * (cdf + ac * jnp.exp(-0.5 * ac * ac) * INV_SQRT_2PI)
        dac_ref[...] = dac

        @pl.when((b == 0) & (t == 0))
        def _():
            dcw_ref[...] = jnp.zeros_like(dcw_ref)
            dcb_ref[...] = jnp.zeros_like(dcb_ref)

        dcw_ref[0:8, :] += _rowsum8(dac * a2)
        dcw_ref[8:16, :] += _rowsum8(dac * a1)
        dcw_ref[16:24, :] += _rowsum8(dac * a)
        dcb_ref[...] += _rowsum8(dac)

    return pl.pallas_call(
        body, name="ffn_act_bwd1", grid=(NJ, B, NT),
        in_specs=[pl.BlockSpec((tm, d), lambda j, b, t: (b * NT + t, 0)), pl.BlockSpec((tn, d), lambda j, b, t: (j, 0)),
                  pl.BlockSpec((tm, tn), lambda j, b, t: (b * NT + t, j)),
                  pl.BlockSpec((tm, tn), lambda j, b, t: (b * NT + t, NJ + j)),
                  pl.BlockSpec((tm, tn), lambda j, b, t: (b * NT + t, j)),
                  pl.BlockSpec((3, tn), lambda j, b, t: (0, j)), pl.BlockSpec((1, tn), lambda j, b, t: (0, j))],
        out_specs=[pl.BlockSpec((tm, tn), lambda j, b, t: (b * NT + t, j)),
                   pl.BlockSpec((tm, tn), lambda j, b, t: (b * NT + t, j)),
                   pl.BlockSpec((24, tn), lambda j, b, t: (0, j)), pl.BlockSpec((8, tn), lambda j, b, t: (0, j))],
        out_shape=[S((N, dff), F32), S((N, dff), BF16), S((24, dff), F32), S((8, dff), F32)],
        scratch_shapes=[pltpu.VMEM((8, tn), F32)], compiler_params=_cp(("parallel", "arbitrary", "arbitrary")),
    )(dx2, wd, up, up, cdf, cw, cb)


def _ffn_act_bwd2(dac, cw, B, T, tm=256):
    N = B * T
    dff = cw.shape[1]
    NT, NJ, tn = T // tm, dff // FFN_TN, FFN_TN
    last8 = N // 8 - 1

    def body(d_ref, nx_ref, cw_ref, da_ref):
        t = pl.program_id(2)
        dd = d_ref[...]
        row = _iota(dd.shape, 0)
        last = t == NT - 1
        n0 = jnp.where(last, 0.0, nx_ref[0:1, :])
        n1 = jnp.where(last, 0.0, nx_ref[1:2, :])
        d1 = jnp.where(row == tm - 1, n0, pltpu.roll(dd, tm - 1, 0))
        d2 = jnp.where(row == tm - 1, n1, jnp.where(row == tm - 2, n0, pltpu.roll(dd, tm - 2, 0)))
        w = cw_ref[...]
        da_ref[...] = (w[2:3, :] * dd + w[1:2, :] * d1 + w[0:1, :] * d2).astype(BF16)

    return pl.pallas_call(
        body, name="ffn_act_bwd2", grid=(B, NJ, NT),
        in_specs=[pl.BlockSpec((tm, tn), lambda b, j, t: (b * NT + t, j)),
                  pl.BlockSpec((8, tn), lambda b, j, t: (jnp.minimum((b * NT + t + 1) * (tm // 8), last8), j)),
                  pl.BlockSpec((3, tn), lambda b, j, t: (0, j))],
        out_specs=pl.BlockSpec((tm, tn), lambda b, j, t: (b * NT + t, j)), out_shape=S((N, dff), BF16),
        compiler_params=_cp(("parallel", "parallel", "parallel")),
    )(dac, dac, cw)


def _fold_rows(p, name):
    r, c = p.shape[0] // 8, p.shape[1]

    def body(p_ref, o_ref):
        for j in range(r):
            o_ref[j:j + 1, :] = jnp.sum(p_ref[8 * j:8 * (j + 1), :], axis=0, keepdims=True)

    return pl.pallas_call(body, name=name, out_shape=S((r, c), F32), compiler_params=_cp())(p)


def _small_reduce(lbl, dg_mix, dg_mem, dlb_p, dgn_p, dfb_p, dgq_p, dgk_p, dmq_p, dmk_p, dg_ffn, dcb_p, loss_p):
    d, dff = dg_mix.shape[1], dcb_p.shape[1]
    nbh = dlb_p.shape[0] // (8 * HG_H)

    def colsum(ref):
        return jnp.sum(ref[...], axis=0, keepdims=True)

    def body(lbl_ref, mix_ref, mem_ref, dlb_ref, dgn_ref, dfb_ref, dgq_ref, dgk_ref, dmq_ref, dmk_ref, ffn_ref, dcb_ref,
             ls_ref, o_mix, o_mem, o_lb, o_hgn, o_fb, o_fq, o_fk, o_mq, o_mk, o_ffn, o_cb, o_loss):
        o_mix[...], o_mem[...], o_ffn[...], o_cb[...] = colsum(mix_ref), colsum(mem_ref), colsum(ffn_ref), colsum(dcb_ref)
        o_hgn[...], o_fb[...], o_mq[...], o_mk[...] = colsum(dgn_ref), colsum(dfb_ref), colsum(dmq_ref), colsum(dmk_ref)
        for src, dst in ((dgq_ref, o_fq), (dgk_ref, o_fk)):
            v = colsum(src)
            dst[...] = v + pltpu.roll(v, FOX_D, 1)
        o_loss[...] = jnp.zeros((1, LANE), F32) + jnp.sum(colsum(ls_ref), axis=-1, keepdims=True)
        logits = lbl_ref[...]
        e = jnp.exp(logits - jnp.max(logits, axis=0, keepdims=True))
        pr = e / jnp.sum(e, axis=0, keepdims=True)
        rows = _iota((8, LANE), 0)
        for h in range(HG_H):
            acc = jnp.zeros((8, LANE), F32)
            for b in range(nbh):
                acc = acc + dlb_ref[8 * (b * HG_H + h):8 * (b * HG_H + h + 1), :]
            dlb = jnp.sum(acc, axis=0, keepdims=True)
            c = slice(LANE * h, LANE * (h + 1))
            p0 = pr[0:1, c]
            first = _iota((logits.shape[0], LANE), 0) == 0
            o_lb[:, c] = pr[:, c] * (jnp.where(first, 1.0, 0.0) - p0) * dlb

    outs = [S((1, d), F32), S((1, d), F32), S(lbl.shape, F32)] + [S((1, LANE), F32)] * 6 + \
           [S((1, d), F32), S((1, dff), F32), S((1, LANE), F32)]
    return pl.pallas_call(body, name="small_reduce", out_shape=outs, compiler_params=_cp())(
        lbl, dg_mix, dg_mem, dlb_p, dgn_p, dfb_p, dgq_p, dgk_p, dmq_p, dmk_p, dg_ffn, dcb_p, loss_p)


def _in_col_pieces():
    hw, fw = HG_H * HG_D, FOX_H * FOX_D
    fox0, ff0 = 4 * hw, 4 * hw + 3 * fw
    mq0 = ff0 + FOX_H
    gate0 = mq0 + MEM_H * MEM_D
    pieces = []
    for p in range(FOX_P):
        pieces += [(fox0 + j * fw + LANE * p, LANE) for j in range(3)]
    pieces.append((mq0, MEM_H * MEM_D))
    for h in range(HG_H):
        pieces += [(j * hw + HG_D * h, HG_D) for j in range(4)]
    pieces.append((gate0, C_FF - C_GATE))
    pieces.append((ff0, FOX_H))
    return pieces


def _perm_from_blocks(blocks):
    n_blk, _, c = blocks.shape
    parts = []
    for s, n in _in_col_pieces():
        lo = s
        while lo < s + n:
            d = lo // c
            hi = min(s + n, (d + 1) * c)
            parts.append(blocks[d][:, lo - d * c:hi - d * c])
            lo = hi
    parts.append(jnp.zeros((blocks.shape[1], C_END - C_FF - FOX_H), blocks.dtype))
    return jnp.concatenate(parts, axis=1)


def _unperm_blocks(segs, n_blk):
    starts = [0]
    for a in segs:
        starts.append(starts[-1] + a.shape[1])
    new_start, placed = 0, []
    for s, n in _in_col_pieces():
        placed.append((s, new_start, n))
        new_start += n
    placed.sort()
    c = sum(n for _, _, n in placed) // n_blk
    blocks = []
    for d in range(n_blk):
        parts = []
        for s, ns, n in placed:
            lo, hi = max(s, d * c), min(s + n, (d + 1) * c)
            if lo < hi:
                i = max(j for j in range(len(segs)) if starts[j] <= ns)
                parts.append(segs[i][:, ns + lo - s - starts[i]:ns + hi - s - starts[i]])
        blocks.append(jnp.concatenate(parts, axis=1))
    return jnp.stack(blocks)


def _local_step(x2, mem2, tgt, sm, W, B, T, M, ex=None):
    fbias = jnp.pad(sm["fox_f_bias"], ((0, 0), (0, LANE - FOX_H)))
    gq2 = jnp.concatenate([sm["fox_q_norm_g"]] * 2, axis=1)
    gk2 = jnp.concatenate([sm["fox_k_norm_g"]] * 2, axis=1)
    lbl = sm["hgrn_lb_logits"]
    h = _rmsnorm_cast(x2, sm["norm_mix_g"], "norm_mix")
    z = _mm_nn(h, W["w_in"], BF16, "proj_in", 512, 2432)
    memn = _rmsnorm_cast(mem2, sm["norm_mem_g"], "norm_mem", tm=256)
    memkv = _mm_nn(memn, W["mem_kv_w"], F32, "proj_memkv", 256, 512)
    ya, o_raw, states, a_mat = _hgrn_fwd(z, lbl, sm["hgrn_norm_g"], B, T)
    fc, fct = _fox_gate_fwd(z, fbias, B, T)
    yb, lse, *late = _fox_fwd(z, fc, fct, gq2, gk2, B, T, gather=ex.late_blocks() if ex else ())
    if ex:
        W = {**W, **ex.unpack_late(late)}
    yc = _mem_fwd(z, memkv, sm["mem_q_norm_g"], sm["mem_k_norm_g"], B, T, M)
    x1, merged, ua, ub, uc = _merge_fwd(ya, yb, yc, z, x2, W["w_br_hgrn"], W["w_br_fox"], W["w_br_mem"], W["w_out"])
    h2 = _rmsnorm_cast(x1, sm["norm_ffn_g"], "norm_ffn")
    up = _mm_nn(h2, W["ffn_w_up"], BF16, "ffn_up", 512, FFN_TN)
    yf, cdf = _ffn_act_fwd(up, W["ffn_conv_w"], sm["ffn_conv_b"], B, T)
    dx2, loss_p = _ffn_down_loss(yf, W["ffn_w_down"], x1, tgt)
    dff = W["ffn_conv_w"].shape[1]
    dac, dv, dcw_p, dcb_p = _ffn_act_bwd1(dx2, W["ffn_w_down"], up, cdf, W["ffn_conv_w"], sm["ffn_conv_b"], B, T)
    da = _ffn_act_bwd2(dac, W["ffn_conv_w"], B, T)
    g = {"ffn_conv_w": _fold_rows(dcw_p, "g_conv_w")}
    g["ffn_w_down"] = _mm_tn(yf, dx2, "g_w_down", TN_TM, 512)
    dh2 = _mm_nt_sum([(da, 0, dff, 0), (dv, 0, dff, dff)], W["ffn_w_up"], "dh2", 256)
    g["ffn_w_up"] = [_mm_tn(h2, da, "g_w_up_a", TN_TM, FFN_TN), _mm_tn(h2, dv, "g_w_up_v", TN_TM, FFN_TN)]
    dx1, dg_ffn = _rmsnorm_bwd(dh2, x1, sm["norm_ffn_g"], dx2, "norm_ffn_bwd")
    g["w_out"] = _mm_tn(merged, dx1, "g_w_out", TN_TM, 512)
    dgate, dya, dyb, dyc, dua, dub, duc = _merge_bwd(dx1, z, ua, ub, uc, W["w_br_hgrn"], W["w_br_fox"], W["w_br_mem"],
                                                    W["w_out"])
    g["w_br_hgrn"] = _mm_tn(ya, dua, "g_w_br_hgrn", TN_TM, 512)
    g["w_br_fox"] = _mm_tn(yb, dub, "g_w_br_fox", TN_TM, 512)
    g["w_br_mem"] = _mm_tn(yc, duc, "g_w_br_mem", TN_TM, 512)
    early_pk = ex.early_grads(g) if ex else ()
    dz_hg, dlb_p, dgn_p, *early_sib = _hgrn_bwd(z, o_raw, states, a_mat, dya, lbl, sm["hgrn_norm_g"], B, T,
                                                swap_sibling=early_pk)
    dz_fox, dfc, dgq_p, dgk_p, *early_chips = _fox_bwd(z, dyb, yb, lse, fc, fct, gq2, gk2, B, T,
                                                       swap=ex.pair_sums(early_pk, early_sib, "early") if ex else ())
    dz_ff, dfb_p = _fox_gate_bwd(dfc, z, fbias, B, T)
    dz_mq, dkv, dmq_p, dmk_p = _mem_bwd(z, memkv, dyc, sm["mem_q_norm_g"], sm["mem_k_norm_g"], B, T, M)
    g["mem_kv_w"] = _mm_tn(memn, dkv, "g_mem_kv_w", 256, 512)
    dmemn = _mm_nt_sum([(dkv, 0, dkv.shape[1], 0)], W["mem_kv_w"], "d_memn", 256)
    _, dg_mem = _rmsnorm_bwd(dmemn, mem2, sm["norm_mem_g"], None, "norm_mem_bwd", tm=256)
    d = x2.shape[1]
    parts = [(dz_fox, 0, C_MQ - C_FOX, C_FOX), (dz_mq, 0, C_HG - C_MQ, C_MQ), (dz_hg, 0, C_GATE - C_HG, C_HG)]
    parts += [(dgate, d * k, d, C_GATE + d * k) for k in range(3)] + [(dz_ff, 0, C_END - C_FF, C_FF)]
    g["w_in"] = [_mm_tn(h, dzs, "g_w_in_%d" % i, 2 * TN_TM, min(512, dzs.shape[1]))
                 for i, dzs in enumerate((dz_fox, dz_mq, dz_hg, dgate, dz_ff))]
    sums = None
    if ex:
        last_pk = ex.last_grads(g)
        last_sib = _swap_with_sibling(last_pk, "rs_sibling_last")
        dh, last_chips = _mm_nt_sum(parts, W["w_in"], "dh", 256, swap=ex.pair_sums(last_pk, last_sib, "last"))
        sums = (ex.final_sums(early_pk, early_sib, early_chips, "early"),
                ex.final_sums(last_pk, last_sib, last_chips, "last"))
    else:
        dh = _mm_nt_sum(parts, W["w_in"], "dh", 256)
    grad_x, dg_mix = _rmsnorm_bwd(dh, x2, sm["norm_mix_g"], dx1, "norm_mix_bwd")
    small = _small_reduce(lbl, dg_mix, dg_mem, dlb_p, dgn_p, dfb_p, dgq_p, dgk_p, dmq_p, dmk_p, dg_ffn, dcb_p, loss_p)
    names = ("norm_mix_g", "norm_mem_g", "hgrn_lb_logits", "hgrn_norm_g", "fox_f_bias", "fox_q_norm_g", "fox_k_norm_g",
             "mem_q_norm_g", "mem_k_norm_g", "norm_ffn_g", "ffn_conv_b", "loss")
    g.update(dict(zip(names, small)))
    return grad_x, g, sums


ANY = pl.BlockSpec(memory_space=pl.ANY)


def _position():
    return lax.axis_index("x"), lax.axis_index("y"), lax.axis_index("c")


def _all_gather(blocks, name):
    nb = len(blocks)

    def body(*refs):
        start, forward, finish = _gather_phases(refs[:nb], refs[nb:2 * nb], *refs[2 * nb:])
        start()
        forward()
        finish()

    return pl.pallas_call(
        body, name=name, out_shape=_gather_shapes(blocks), in_specs=[ANY] * nb, out_specs=[ANY] * nb,
        scratch_shapes=_gather_sems(nb),
    )(*blocks)


def _hosting(body, n_in, n_out, n_scratch, n_x, make_phases, grid):
    n_steps = math.prod(grid)

    def hosted(*refs):
        a = n_in + n_x
        b = a + n_out + n_x
        ins, xs = refs[:n_in], refs[n_in:a]
        outs, x_outs = refs[a:a + n_out], refs[a + n_out:b]
        scratch, sems = refs[b:b + n_scratch], refs[b + n_scratch:]
        step = 0
        for ax, n in enumerate(grid):
            step = step * n + pl.program_id(ax)
        phases = make_phases(xs, x_outs, *sems)
        pl.when(step == 0)(phases[0])
        for ph in phases[1:-1]:
            pl.when(step == n_steps // 2)(ph)
        body(*ins, *outs, *scratch)
        pl.when(step == n_steps - 1)(phases[-1])

    return hosted


def _gather_shapes(blocks):
    return [S((N_DEV,) + b.shape, b.dtype) for b in blocks]


def _gather_sems(nb):
    return [pltpu.SemaphoreType.DMA((7 * nb,)), pltpu.SemaphoreType.DMA((7 * nb,)), pltpu.SemaphoreType.DMA((nb,))]


def _gather_phases(x_refs, out_refs, send_sems, recv_sems, local_sems):
    nb = len(x_refs)
    x, y, c = _position()
    me, sibling = (x, y, c), (x, y, 1 - c)
    chips = [(1 - x, y), (x, 1 - y), (1 - x, 1 - y)]

    def copy(i, k, blk, to, own=False):
        px, py, pc = blk
        slot = out_refs[i].at[4 * px + 2 * py + pc]
        return pltpu.make_async_remote_copy(
            src_ref=x_refs[i] if own else slot, dst_ref=slot, send_sem=send_sems.at[7 * i + k],
            recv_sem=recv_sems.at[7 * i + k], device_id=to, device_id_type=MESH)

    def mine(i):
        return pltpu.make_async_copy(x_refs[i], out_refs[i].at[4 * x + 2 * y + c], local_sems.at[i])

    def first(i):
        return [copy(i, 0, me, sibling, own=True)] + [copy(i, 1 + j, me, (*chip, c), own=True)
                                                     for j, chip in enumerate(chips)]

    def passed(i, j):
        return copy(i, 4 + j, (*chips[j], c), sibling)

    def start():
        for i in range(nb):
            mine(i).start()
            for cp in first(i):
                cp.start()

    def forward():
        for i in range(nb):
            for j, chip in enumerate(chips):
                copy(i, 1 + j, (*chip, c), me).wait_recv()
                passed(i, j).start()

    def finish():
        for i in range(nb):
            copy(i, 0, sibling, me).wait_recv()
            for j, chip in enumerate(chips):
                copy(i, 4 + j, (*chip, 1 - c), me).wait_recv()
        for i in range(nb):
            for cp in first(i) + [passed(i, j) for j in range(3)]:
                cp.wait_send()
            mine(i).wait()

    return start, forward, finish


def _swap_with_sibling(pks, name):
    nb = len(pks)

    def body(*refs):
        start, finish = _sibling_swap_phases(refs[:nb], refs[nb:2 * nb], *refs[2 * nb:])
        start()
        finish()

    return pl.pallas_call(
        body, name=name, out_shape=_sibling_swap_shapes(pks), in_specs=[ANY] * nb, out_specs=[ANY] * nb,
        scratch_shapes=_sibling_swap_sems(nb),
    )(*pks)


def _sibling_swap_shapes(pks):
    return [S((4,) + p.shape[1:], p.dtype) for p in pks]


def _sibling_swap_sems(nb):
    return [pltpu.SemaphoreType.DMA((4 * nb,)), pltpu.SemaphoreType.DMA((4 * nb,))]


def _sibling_swap_phases(pk_refs, out_refs, send_sems, recv_sems):
    nb = len(pk_refs)
    x, y, c = _position()

    def copies():
        return [pltpu.make_async_remote_copy(
            src_ref=pk_refs[i].at[2 * k + 1 - c], dst_ref=out_refs[i].at[k], send_sem=send_sems.at[4 * i + k],
            recv_sem=recv_sems.at[4 * i + k], device_id=(x, y, 1 - c), device_id_type=MESH)
            for i in range(nb) for k in range(4)]

    def start():
        for cp in copies():
            cp.start()

    def finish():
        for cp in copies():
            cp.wait()

    return start, finish


def _swap_between_chips(pbs, name):
    nb = len(pbs)

    def body(*refs):
        start, finish = _chip_swap_phases(refs[:nb], refs[nb:2 * nb], *refs[2 * nb:])
        start()
        finish()

    return pl.pallas_call(
        body, name=name, out_shape=[S(p.shape, p.dtype) for p in pbs], in_specs=[ANY] * nb, out_specs=[ANY] * nb,
        scratch_shapes=_chip_swap_sems(nb),
    )(*pbs)


def _chip_swap_sems(nb):
    return [pltpu.SemaphoreType.DMA((3 * nb,)), pltpu.SemaphoreType.DMA((3 * nb,)), pltpu.SemaphoreType.DMA((nb,))]


def _chip_swap_phases(pb_refs, out_refs, send_sems, recv_sems, local_sems):
    nb = len(pb_refs)
    x, y, c = _position()
    me = 2 * x + y
    chips = [(1 - x, y), (x, 1 - y), (1 - x, 1 - y)]

    def local(i):
        return pltpu.make_async_copy(pb_refs[i].at[me], out_refs[i].at[me], local_sems.at[i])

    def send(i, j):
        cx, cy = chips[j]
        return pltpu.make_async_remote_copy(
            src_ref=pb_refs[i].at[2 * cx + cy], dst_ref=out_refs[i].at[me], send_sem=send_sems.at[3 * i + j],
            recv_sem=recv_sems.at[3 * i + j], device_id=(cx, cy, c), device_id_type=MESH)

    def arrival(i, j):
        cx, cy = chips[j]
        return pltpu.make_async_remote_copy(
            src_ref=pb_refs[i].at[me], dst_ref=out_refs[i].at[2 * cx + cy], send_sem=send_sems.at[3 * i + j],
            recv_sem=recv_sems.at[3 * i + j], device_id=(cx, cy, c), device_id_type=MESH)

    def start():
        for i in range(nb):
            local(i).start()
            for j in range(3):
                send(i, j).start()

    def finish():
        for i in range(nb):
            for j in range(3):
                arrival(i, j).wait_recv()
        for i in range(nb):
            for j in range(3):
                send(i, j).wait_send()
            local(i).wait()

    return start, finish


def _row_tile(r):
    return max(t for t in range(16, min(r, 512) + 1, 16) if r % t == 0)


def _pair_sum_cast(pk, recv, core, name):
    _, r, l = pk.shape
    tr = _row_tile(r)

    def body(c_ref, a_ref, b_ref, o_ref):
        o_ref[...] = (a_ref[...] + b_ref[...]).astype(BF16)

    return pl.pallas_call(
        body, name=name,
        grid_spec=pltpu.PrefetchScalarGridSpec(
            num_scalar_prefetch=1, grid=(4, r // tr),
            in_specs=[pl.BlockSpec((None, tr, l), lambda k, i, c: (2 * k + c[0], i, 0)),
                      pl.BlockSpec((None, tr, l), lambda k, i, c: (k, i, 0))],
            out_specs=pl.BlockSpec((None, tr, l), lambda k, i, c: (k, i, 0))),
        out_shape=S((4, r, l), BF16), compiler_params=_cp(("parallel", "parallel")),
    )(core, pk, recv)


def _final_sum(pk, recv_sib, recv_chips, slot, chip, name):
    _, r, l = pk.shape
    tr = _row_tile(r)

    def body(s_ref, k_ref, a_ref, b_ref, rc_ref, o_ref):
        base = a_ref[...] + b_ref[...]
        acc = jnp.zeros_like(base)
        for j in range(4):
            acc = acc + jnp.where(k_ref[0] == j, base, rc_ref[j].astype(F32))
        o_ref[...] = acc

    return pl.pallas_call(
        body, name=name,
        grid_spec=pltpu.PrefetchScalarGridSpec(
            num_scalar_prefetch=2, grid=(r // tr,),
            in_specs=[pl.BlockSpec((None, tr, l), lambda i, s, k: (s[0], i, 0)),
                      pl.BlockSpec((None, tr, l), lambda i, s, k: (k[0], i, 0)),
                      pl.BlockSpec((4, tr, l), lambda i, s, k: (0, i, 0))],
            out_specs=pl.BlockSpec((tr, l), lambda i, s, k: (i, 0))),
        out_shape=S((r, l), F32), compiler_params=_cp(("parallel",)),
    )(slot, chip, pk, recv_sib, recv_chips)


def _adamw_math(w, g, m, v):
    m = ADAM_B1 * m + (1.0 - ADAM_B1) * g
    v = ADAM_B2 * v + (1.0 - ADAM_B2) * (g * g)
    m_hat = m / (1.0 - ADAM_B1 ** ADAM_STEP)
    v_hat = v / (1.0 - ADAM_B2 ** ADAM_STEP)
    return -ADAM_LR * (m_hat / (jnp.sqrt(v_hat) + ADAM_EPS) + ADAM_WD * w), m, v


def _adamw(w, g, m, v, name):
    r, c = w.shape
    tr = 256 if r % 256 == 0 else r

    def body(w_ref, g_ref, m_ref, v_ref, d_ref, nm_ref, nv_ref):
        d_ref[...], nm_ref[...], nv_ref[...] = _adamw_math(w_ref[...], g_ref[...], m_ref[...], v_ref[...])

    tile = pl.BlockSpec((tr, c), lambda i: (i, 0))
    return pl.pallas_call(
        body, name=name, grid=(r // tr,), in_specs=[tile] * 4, out_specs=[tile] * 3, out_shape=[S((r, c), F32)] * 3,
        compiler_params=_cp(("parallel",)),
    )(w, g, m, v)


def _small_update(gathered, w, m, v):
    def body(ga_ref, w_ref, m_ref, v_ref, g_ref, d_ref, nm_ref, nv_ref):
        g = ga_ref[0]
        for k in range(1, N_DEV):
            g = g + ga_ref[k]
        g_ref[...] = g
        d_ref[...], nm_ref[...], nv_ref[...] = _adamw_math(w_ref[...], g, m_ref[...], v_ref[...])

    return pl.pallas_call(body, name="small_update", out_shape=[S(w.shape, F32)] * 4, compiler_params=_cp())(
        gathered, w, m, v)


BIG = ("w_in", "mem_kv_w", "w_br_hgrn", "w_br_fox", "w_br_mem", "w_out", "ffn_w_up", "ffn_conv_w", "ffn_w_down")
GROUP_ROWS = ("w_out", "ffn_w_down")
GROUP_LANE = ("w_br_hgrn", "w_br_fox", "w_br_mem")
LANE_GROUP_ROWS = 224
SMALL = ("norm_mix_g", "norm_mem_g", "hgrn_lb_logits", "hgrn_norm_g", "fox_f_bias", "fox_q_norm_g", "fox_k_norm_g",
         "mem_q_norm_g", "mem_k_norm_g", "norm_ffn_g", "ffn_conv_b")


def _rows_of(n_elems):
    return -(-n_elems // LANE)


def _to_rows(a, lead=0):
    flat = a.reshape(a.shape[:lead] + (-1,))
    pad = (-flat.shape[-1]) % LANE
    if pad:
        flat = jnp.pad(flat, [(0, 0)] * lead + [(0, pad)])
    return flat.reshape(a.shape[:lead] + (-1, LANE))


def _stack_rows(parts, lead, total_rows):
    buf = jnp.concatenate(parts, axis=lead)
    pad = total_rows - buf.shape[lead]
    return jnp.pad(buf, [(0, 0)] * lead + [(0, pad), (0, 0)])


def _round_up(n, k):
    return -(-n // k) * k


def _from_rows(rows, shape, lead=0):
    n = math.prod(shape)
    return rows.reshape(rows.shape[:lead] + (-1,))[..., :n].reshape(rows.shape[:lead] + tuple(shape))


def _blocks_to_full(blocks, kind):
    n, a, b = blocks.shape
    return blocks.transpose(1, 0, 2).reshape(a, n * b) if kind == "col" else blocks.reshape(n * a, b)


def _full_to_blocks(full, kind, n=N_DEV):
    a, b = full.shape
    return full.reshape(a, n, b // n).transpose(1, 0, 2) if kind == "col" else full.reshape(n, a // n, b)


def _lane_group_rows(shard):
    n_lane = sum(shard[n].shape[0] for n in GROUP_LANE)
    n_cw = shard["ffn_conv_w"].size
    return n_lane, _rows_of(3 * n_cw), _rows_of(n_cw), _round_up(n_lane + _rows_of(3 * n_cw), LANE_GROUP_ROWS)


def _split_bf16x3(x):
    hi = x.astype(BF16)
    r1 = x - hi.astype(F32)
    mid = r1.astype(BF16)
    return jnp.stack([hi, mid, (r1 - mid.astype(F32)).astype(BF16)])


class _Exchange:
    def __init__(self, shard):
        self.shard = shard
        xi, yi, ci = _position()
        self.core = ci.astype(jnp.int32).reshape(1)
        self.chip = (2 * xi + yi).astype(jnp.int32).reshape(1)
        self.n_lane, self.r_pieces, self.r_vals, self.r_lane = _lane_group_rows(shard)

    def first_blocks(self):
        return [self.shard["w_in"].astype(BF16), self.shard["mem_kv_w"].astype(BF16)]

    def unpack_first(self, gathered):
        return {"w_in": _perm_from_blocks(gathered[0]), "mem_kv_w": _blocks_to_full(gathered[1], "row")}

    def late_blocks(self):
        sh = self.shard
        lane_rows = [sh[n].astype(BF16) for n in GROUP_LANE] + [_to_rows(_split_bf16x3(sh["ffn_conv_w"]))]
        return [sh[n].astype(BF16) for n in GROUP_ROWS] + [sh["ffn_w_up"].astype(BF16),
                                                           _stack_rows(lane_rows, 0, self.r_lane)]

    def unpack_late(self, gathered):
        *rows, gc, gd = gathered
        sh = self.shard
        W = {"ffn_w_up": _blocks_to_full(gc, "col")}
        for n, blocks in zip(GROUP_ROWS, rows):
            W[n] = _blocks_to_full(blocks, "row")
        r0 = 0
        for n in GROUP_LANE:
            W[n] = _blocks_to_full(gd[:, r0:r0 + sh[n].shape[0]], "col")
            r0 += sh[n].shape[0]
        cw = _from_rows(gd[:, self.n_lane:self.n_lane + self.r_pieces], (3,) + sh["ffn_conv_w"].shape, lead=1).astype(F32)
        W["ffn_conv_w"] = _blocks_to_full(cw[:, 0] + cw[:, 1] + cw[:, 2], "col")
        return W

    def early_grads(self, g):
        cw_rows = _to_rows(_full_to_blocks(g["ffn_conv_w"], "col"), lead=1)
        return [_full_to_blocks(g[n], "row") for n in GROUP_ROWS] + [
            jnp.concatenate([_full_to_blocks(h, "col", N_DEV // 2) for h in g["ffn_w_up"]], axis=0),
            _stack_rows([_full_to_blocks(g[n], "col") for n in GROUP_LANE] + [cw_rows], 1, self.r_lane)]

    def last_grads(self, g):
        return [_unperm_blocks(g["w_in"], N_DEV), _full_to_blocks(g["mem_kv_w"], "row")]

    def pair_sums(self, pks, recv_sib, tag):
        return [_pair_sum_cast(p, r, self.core, "rs_pair_sum_%s%d" % (tag, i))
                for i, (p, r) in enumerate(zip(pks, recv_sib))]

    def final_sums(self, pks, recv_sib, recv_chips, tag):
        return [_final_sum(p, rs, rc, 2 * self.chip + self.core, self.chip, "rs_final_sum_%s%d" % (tag, i))
                for i, (p, rs, rc) in enumerate(zip(pks, recv_sib, recv_chips))]

    def unpack_grads(self, early, last):
        sh = self.shard
        *rows, g_up, g_lane = early
        g_shard = {"w_in": last[0], "mem_kv_w": last[1], "ffn_w_up": g_up, **dict(zip(GROUP_ROWS, rows))}
        r0 = 0
        for n in GROUP_LANE:
            g_shard[n] = g_lane[r0:r0 + sh[n].shape[0]]
            r0 += sh[n].shape[0]
        g_shard["ffn_conv_w"] = _from_rows(g_lane[self.n_lane:self.n_lane + self.r_vals], sh["ffn_conv_w"].shape)
        return g_shard


def kernel(x, mem, norm_mix_g, norm_mem_g, w_in, hgrn_lb_logits, hgrn_norm_g, fox_f_bias, fox_q_norm_g, fox_k_norm_g, mem_kv_w, mem_q_norm_g, mem_k_norm_g, w_br_hgrn, w_br_fox, w_br_mem, w_out, norm_ffn_g, ffn_w_up, ffn_conv_w, ffn_conv_b, ffn_w_down, loss_target, m_norm_mix_g, m_norm_mem_g, m_w_in, m_hgrn_lb_logits, m_hgrn_norm_g, m_fox_f_bias, m_fox_q_norm_g, m_fox_k_norm_g, m_mem_kv_w, m_mem_q_norm_g, m_mem_k_norm_g, m_w_br_hgrn, m_w_br_fox, m_w_br_mem, m_w_out, m_norm_ffn_g, m_ffn_w_up, m_ffn_conv_w, m_ffn_conv_b, m_ffn_w_down, v_norm_mix_g, v_norm_mem_g, v_w_in, v_hgrn_lb_logits, v_hgrn_norm_g, v_fox_f_bias, v_fox_q_norm_g, v_fox_k_norm_g, v_mem_kv_w, v_mem_q_norm_g, v_mem_k_norm_g, v_w_br_hgrn, v_w_br_fox, v_w_br_mem, v_w_out, v_norm_ffn_g, v_ffn_w_up, v_ffn_conv_w, v_ffn_conv_b, v_ffn_w_down):
    given = dict(locals())
    order = ("norm_mix_g", "norm_mem_g", "w_in", "hgrn_lb_logits", "hgrn_norm_g", "fox_f_bias", "fox_q_norm_g",
             "fox_k_norm_g", "mem_kv_w", "mem_q_norm_g", "mem_k_norm_g", "w_br_hgrn", "w_br_fox", "w_br_mem", "w_out",
             "norm_ffn_g", "ffn_w_up", "ffn_conv_w", "ffn_conv_b", "ffn_w_down")
    B, T, D = x.shape
    M = mem.shape[1]
    shard = {n: given[n][0] if n in BIG else given[n] for n in order}
    mom = {n: (given["m_" + n][0], given["v_" + n][0]) if n in BIG else (given["m_" + n], given["v_" + n])
           for n in order}
    shard["hgrn_lb_logits"] = hgrn_lb_logits
    for n in ("norm_mix_g", "norm_mem_g", "hgrn_norm_g", "fox_f_bias", "fox_q_norm_g", "fox_k_norm_g", "mem_q_norm_g",
              "mem_k_norm_g", "norm_ffn_g", "ffn_conv_b"):
        shard[n] = given[n].reshape(1, -1)

    ex = _Exchange(shard)
    W = ex.unpack_first(_all_gather(ex.first_blocks(), "ag_first"))

    sm = {n: shard[n] for n in SMALL}
    grad_x, g, sums = _local_step(x.reshape(B * T, D), mem.reshape(B * M, D), loss_target.reshape(B * T, D), sm, W,
                                  B, T, M, ex)
    g_shard = ex.unpack_grads(*sums)

    sg = {n: g[n] for n in SMALL}
    sg["fox_f_bias"] = g["fox_f_bias"][:, :FOX_H]
    sg["fox_q_norm_g"] = g["fox_q_norm_g"][:, :FOX_D]
    sg["fox_k_norm_g"] = g["fox_k_norm_g"][:, :FOX_D]
    slayout, row0 = {}, 0
    for n in SMALL:
        nr = _rows_of(shard[n].size)
        slayout[n] = (row0, nr)
        row0 += nr
    loss_row = row0
    r_small = _round_up(row0 + 1, 8)

    def pack_small(d, with_loss=None):
        rows = [_to_rows(d[n]) for n in SMALL]
        rows.append(with_loss if with_loss is not None else jnp.zeros((1, LANE), F32))
        return _stack_rows(rows, 0, r_small)

    sgath, = _all_gather([pack_small(sg, g["loss"])], "ag_small")
    s_g, s_d, s_m, s_v = _small_update(sgath, pack_small(shard), pack_small({n: mom[n][0].reshape(shard[n].shape) for n in SMALL}),
                                       pack_small({n: mom[n][1].reshape(shard[n].shape) for n in SMALL}))
    loss = s_g[loss_row, 0]

    grads, deltas, new_m, new_v = {}, {}, {}, {}
    for n in BIG:
        gn = g_shard[n]
        d, nm, nv = _adamw(shard[n], gn, mom[n][0], mom[n][1], "adamw_" + n)
        grads[n], deltas[n], new_m[n], new_v[n] = (a[None] for a in (gn, d, nm, nv))
    for n in SMALL:
        r0, nr = slayout[n]
        for dst, src in ((grads, s_g), (deltas, s_d), (new_m, s_m), (new_v, s_v)):
            dst[n] = _from_rows(src[r0:r0 + nr], given[n].shape)
    return (loss, grad_x.reshape(B, T, D), *[grads[n] for n in order], *[deltas[n] for n in order],
            *[new_m[n] for n in order], *[new_v[n] for n in order])
```

```python
import functools
import math

import jax
import jax.numpy as jnp
from jax import lax
from jax.experimental import pallas as pl
from jax.experimental.pallas import tpu as pltpu

F32, BF16 = jnp.float32, jnp.bfloat16
S = jax.ShapeDtypeStruct
MESH = pl.DeviceIdType.MESH

N_DEV = 8
EPS = 1e-6
LANE = 128
CHUNK = 64
SUB = 16
HG_H, HG_D = 4, 128
HG_GROUP_FWD = 4
HG_GROUP = 2
FOX_H, FOX_D = 8, 64
FOX_P = FOX_H // 2
MEM_H, MEM_D = 4, 128
NEG = -1e30
VMEM_LIMIT = 56 * 2**20

ADAM_LR, ADAM_B1, ADAM_B2, ADAM_EPS, ADAM_WD, ADAM_STEP = 0.001, 0.9, 0.999, 1e-08, 0.01, 10

C_FOX, C_MQ, C_HG, C_GATE, C_FF, C_END = 0, 1536, 2048, 4096, 7168, 7296


def _cp(sem=None):
    return pltpu.CompilerParams(dimension_semantics=sem, vmem_limit_bytes=VMEM_LIMIT)


def _dot(a, b, dims, prec=None):
    return lax.dot_general(a, b, (dims, ((), ())), preferred_element_type=F32, precision=prec)


def _nn(a, b, prec=None):
    return _dot(a, b, ((1,), (0,)), prec)


def _nt(a, b, prec=None):
    return _dot(a, b, ((1,), (1,)), prec)


def _tn(a, b, prec=None):
    return _dot(a, b, ((0,), (0,)), prec)


def _b(x):
    return x.astype(BF16)


def _mm3(fn, a, b):
    ah, bh = _b(a), _b(b)
    return fn(ah, bh) + fn(ah, _b(b - bh.astype(F32))) + fn(_b(a - ah.astype(F32)), bh)


def _iota(shape, dim):
    return lax.broadcasted_iota(jnp.int32, shape, dim)


def _rowsum8(x):
    r, d = x.shape
    return jnp.sum(x.reshape(r // 8, 8, d), axis=0)


def _rmsnorm_cast(x, g, name, tm=512):
    n, d = x.shape

    def body(x_ref, g_ref, o_ref):
        v = x_ref[...]
        r = lax.rsqrt(jnp.mean(v * v, axis=-1, keepdims=True) + EPS)
        o_ref[...] = (v * r * g_ref[...]).astype(BF16)

    return pl.pallas_call(
        body, name=name, grid=(n // tm,),
        in_specs=[pl.BlockSpec((tm, d), lambda i: (i, 0)), pl.BlockSpec((1, d), lambda i: (0, 0))],
        out_specs=pl.BlockSpec((tm, d), lambda i: (i, 0)), out_shape=S((n, d), BF16), compiler_params=_cp(("parallel",)),
    )(x, g)


def _rmsnorm_bwd(dh, x, g, resid, name, tm=512):
    n, d = x.shape
    has_res = resid is not None

    def body(*refs):
        if has_res:
            dh_ref, x_ref, g_ref, r_ref, dx_ref, dg_ref = refs
        else:
            dh_ref, x_ref, g_ref, dx_ref, dg_ref = refs
        v = x_ref[...]
        dhv = dh_ref[...].astype(F32)
        r = lax.rsqrt(jnp.mean(v * v, axis=-1, keepdims=True) + EPS)
        xh = v * r
        u = dhv * g_ref[...]
        dx = r * (u - xh * jnp.mean(u * xh, axis=-1, keepdims=True))
        if has_res:
            dx = dx + r_ref[...]
        dx_ref[...] = dx

        @pl.when(pl.program_id(0) == 0)
        def _():
            dg_ref[...] = jnp.zeros_like(dg_ref)

        dg_ref[...] += _rowsum8(dhv * xh)

    tile = pl.BlockSpec((tm, d), lambda i: (i, 0))
    ins = [tile, tile, pl.BlockSpec((1, d), lambda i: (0, 0))] + ([tile] if has_res else [])
    args = (dh, x, g) + ((resid,) if has_res else ())
    return pl.pallas_call(
        body, name=name, grid=(n // tm,), in_specs=ins,
        out_specs=[tile, pl.BlockSpec((8, d), lambda i: (0, 0))],
        out_shape=[S((n, d), F32), S((8, d), F32)], compiler_params=_cp(("arbitrary",)),
    )(*args)


def _mm_nn(a, b, out_dtype, name, tm, tn):
    m, k = a.shape
    n = b.shape[1]
    assert n % tn == 0 and m % tm == 0

    def body(a_ref, b_ref, o_ref):
        o_ref[...] = _nn(a_ref[...].astype(BF16), b_ref[...].astype(BF16)).astype(out_dtype)

    return pl.pallas_call(
        body, name=name, grid=(n // tn, m // tm),
        in_specs=[pl.BlockSpec((tm, k), lambda j, i: (i, 0)), pl.BlockSpec((k, tn), lambda j, i: (0, j))],
        out_specs=pl.BlockSpec((tm, tn), lambda j, i: (i, j)), out_shape=S((m, n), out_dtype),
        compiler_params=_cp(("parallel", "parallel")),
    )(a, b)


def _mm_nt_sum(parts, w, name, tm, swap=()):
    m = parts[0][0].shape[0]
    k = w.shape[0]
    assert m % tm == 0 and all(c % n == 0 and o % n == 0 for _, c, n, o in parts)
    np_ = len(parts)
    nsw = len(swap)
    n_steps = m // tm

    def body(*refs):
        o_ref = refs[2 * np_ + nsw]
        if nsw:
            start, finish = _chip_swap_phases(refs[2 * np_:2 * np_ + nsw], refs[2 * np_ + nsw + 1:2 * np_ + 2 * nsw + 1],
                                              *refs[2 * np_ + 2 * nsw + 1:])
            pl.when(pl.program_id(0) == 0)(start)
        acc = _nt(refs[0][...].astype(BF16), refs[np_][...].astype(BF16))
        for i in range(1, np_):
            acc = acc + _nt(refs[i][...].astype(BF16), refs[np_ + i][...].astype(BF16))
        o_ref[...] = acc
        if nsw:
            pl.when(pl.program_id(0) == n_steps - 1)(finish)

    dy_specs = [pl.BlockSpec((tm, n), functools.partial(lambda i, j: (i, j), j=c // n)) for _, c, n, _ in parts]
    w_specs = [pl.BlockSpec((k, n), functools.partial(lambda i, j: (0, j), j=o // n)) for _, _, n, o in parts]
    out = pl.pallas_call(
        body, name=name, grid=(n_steps,), in_specs=dy_specs + w_specs + [ANY] * nsw,
        out_specs=[pl.BlockSpec((tm, k), lambda i: (i, 0))] + [ANY] * nsw,
        out_shape=[S((m, k), F32)] + [S(p.shape, p.dtype) for p in swap],
        scratch_shapes=_chip_swap_sems(nsw) if nsw else [],
        compiler_params=_cp(("arbitrary",) if nsw else ("parallel",)),
    )(*([p[0] for p in parts] + [w] * np_ + list(swap)))
    return (out[0], out[1:]) if nsw else out[0]


def _mm_tn(x, dy, name, tm, tn):
    m, k = x.shape
    n = dy.shape[1]
    tm = min(tm, m)
    assert m % tm == 0 and n % tn == 0

    def body(x_ref, dy_ref, o_ref):
        part = _tn(x_ref[...].astype(BF16), dy_ref[...].astype(BF16))

        @pl.when(pl.program_id(1) == 0)
        def _():
            o_ref[...] = part

        @pl.when(pl.program_id(1) > 0)
        def _():
            o_ref[...] += part

    return pl.pallas_call(
        body, name=name, grid=(n // tn, m // tm),
        in_specs=[pl.BlockSpec((tm, k), lambda j, i: (i, 0)), pl.BlockSpec((tm, tn), lambda j, i: (i, j))],
        out_specs=pl.BlockSpec((k, tn), lambda j, i: (0, j)), out_shape=S((k, n), F32),
        compiler_params=_cp(("parallel", "arbitrary")),
    )(x, dy)


def _lower_bound(logits):
    e = jnp.exp(logits - jnp.max(logits, axis=0, keepdims=True))
    return e[0:1, :] / jnp.sum(e, axis=0, keepdims=True)


def _hg_gates(fl, lb):
    sig = jax.nn.sigmoid(fl)
    f = lb + (1.0 - lb) * sig
    k = (1.0 - lb) * (1.0 - sig)
    return sig, f, k, jnp.log(f)


def _silu_and_grad(x):
    s = jax.nn.sigmoid(x)
    return x * s, s * (1.0 + x * (1.0 - s))


def _hg_rowblocks(G):
    return [None] + [G[SUB * i - 1:SUB * i, :] for i in range(1, CHUNK // SUB)]


def _hg_intra_A(qs, k, G):
    refs = _hg_rowblocks(G)
    cols = _iota((SUB, LANE), 1)
    rows = _iota((SUB, LANE), 0)
    no_keys = jnp.zeros((LANE - CHUNK, HG_D), BF16)
    blocks = []
    for i in range(CHUNK // SUB):
        lo = SUB * i
        qb, Gb = qs[lo:lo + SUB, :], G[lo:lo + SUB, :]
        diag = jnp.zeros((SUB, LANE), F32)
        for s in range(SUB):
            e = jnp.exp(jnp.minimum(Gb - G[lo + s:lo + s + 1, :], 0.0))
            col = jnp.sum(qb * k[lo + s:lo + s + 1, :] * e, axis=-1, keepdims=True)
            diag = jnp.where(cols == lo + s, col, diag)
        a = jnp.where((cols >= lo) & (cols <= rows + lo), diag, 0.0)
        if i > 0:
            qr = qb * jnp.exp(Gb - refs[i])
            kr = k * jnp.exp(jnp.minimum(refs[i] - G, 0.0))
            a = jnp.where(cols < lo, _nt(_b(qr), jnp.concatenate([_b(kr), no_keys], axis=0)), a)
        blocks.append(a)
    return jnp.concatenate(blocks, axis=0)


def _hg_intra_bwd(dA, qs, k, G):
    refs = _hg_rowblocks(G)
    cols = _iota((SUB, CHUNK), 1)
    rows16 = _iota((SUB, HG_D), 0)
    dk = jnp.zeros((CHUNK, HG_D), F32)
    dq_blocks, dk_diag_blocks = [], []
    for i in range(CHUNK // SUB):
        lo = SUB * i
        qb, Gb = qs[lo:lo + SUB, :], G[lo:lo + SUB, :]
        dAb = dA[lo:lo + SUB, :]
        dq = jnp.zeros((SUB, HG_D), F32)
        dkb = jnp.zeros((SUB, HG_D), F32)
        for s in range(SUB):
            e = jnp.exp(jnp.minimum(Gb - G[lo + s:lo + s + 1, :], 0.0))
            e = jnp.where(rows16 >= s, e, 0.0)
            dcol = jnp.sum(jnp.where(cols == lo + s, dAb, 0.0), axis=-1, keepdims=True)
            w = dcol * e
            dq = dq + w * k[lo + s:lo + s + 1, :]
            dkb = jnp.where(rows16 == s, jnp.sum(w * qb, axis=0, keepdims=True), dkb)
        if i > 0:
            e1 = jnp.exp(Gb - refs[i])
            e2 = jnp.exp(jnp.minimum(refs[i] - G, 0.0))
            dA_off = jnp.where(cols < lo, dAb, 0.0)
            dq = dq + _mm3(_nn, dA_off, k * e2) * e1
            dk = dk + _mm3(_tn, dA_off, qb * e1) * e2
        dq_blocks.append(dq)
        dk_diag_blocks.append(dkb)
    return jnp.concatenate(dq_blocks, axis=0), dk + jnp.concatenate(dk_diag_blocks, axis=0)


def _tri(n, upper=False):
    r, c = _iota((n, n), 0), _iota((n, n), 1)
    return jnp.where((c >= r) if upper else (r >= c), 1.0, 0.0).astype(BF16)


def _prefix_mm(tri, x):
    hi = x.astype(BF16)
    r1 = x - hi.astype(F32)
    mid = r1.astype(BF16)
    lo = (r1 - mid.astype(F32)).astype(BF16)
    return _nn(tri, hi) + _nn(tri, mid) + _nn(tri, lo)


def _hgrn_fwd(z, lb, gn, B, T):
    N = B * T
    NC = T // CHUNK
    ng = HG_H // HG_GROUP_FWD

    def body(z_ref, lb_ref, gn_ref, y_ref, o_ref, st_ref, a_ref, s_scr):
        lbs = _lower_bound(lb_ref[...])
        tri = _tri(CHUNK)
        s_scr[...] = jnp.zeros_like(s_scr)

        def chunk(c, carry):
            r = pl.ds(pl.multiple_of(c * CHUNK, CHUNK), CHUNK)
            for hh in range(HG_GROUP_FWD):
                zc, oc = 4 * LANE * hh, LANE * hh
                ql, fl, il, gl = (z_ref[r, zc + LANE * j:zc + LANE * (j + 1)].astype(F32) for j in range(4))
                _, _, k, logf = _hg_gates(fl, lbs[:, oc:oc + LANE])
                G = _prefix_mm(tri, logf)
                qs = ql * jax.nn.sigmoid(ql)
                st = s_scr[hh]
                st_ref[hh * NC + c] = st
                g_last = G[CHUNK - 1:CHUNK, :]
                A = _b(_hg_intra_A(qs, k, G))
                a_ref[r, oc:oc + LANE] = A
                o = _nn(A[:, 0:CHUNK], _b(il)) + _nt(_b(qs * jnp.exp(G)), _b(st))
                s_scr[hh] = st * jnp.exp(g_last) + _mm3(_tn, il, k * jnp.exp(g_last - G))
                o_ref[r, oc:oc + LANE] = o
                rstd = lax.rsqrt(jnp.mean(o * o, axis=-1, keepdims=True) + EPS)
                y_ref[r, oc:oc + LANE] = (o * rstd * gn_ref[...] * (gl * jax.nn.sigmoid(gl))).astype(BF16)
            return carry

        lax.fori_loop(0, NC, chunk, 0)

    gw = HG_GROUP_FWD * LANE
    cb = C_HG // (4 * gw)
    return pl.pallas_call(
        body, name="hgrn_fwd", grid=(B, ng),
        in_specs=[pl.BlockSpec((T, 4 * gw), lambda b, h: (b, cb + h)), pl.BlockSpec((lb.shape[0], gw), lambda b, h: (0, h)),
                  pl.BlockSpec((1, LANE), lambda b, h: (0, 0))],
        out_specs=[pl.BlockSpec((T, gw), lambda b, h: (b, h)), pl.BlockSpec((T, gw), lambda b, h: (b, h)),
                   pl.BlockSpec((HG_GROUP_FWD * NC, HG_D, HG_D), lambda b, h: (b * ng + h, 0, 0)),
                   pl.BlockSpec((T, gw), lambda b, h: (b, h))],
        out_shape=[S((N, 512), BF16), S((N, 512), F32), S((B * HG_H * NC, HG_D, HG_D), F32), S((N, 512), BF16)],
        scratch_shapes=[pltpu.VMEM((HG_GROUP_FWD, HG_D, HG_D), F32)], compiler_params=_cp(("parallel", "parallel")),
    )(z, lb, gn)


def _hgrn_bwd(z, o_raw, states, a_mat, dy, lb, gn, B, T, swap_sibling=()):
    N = B * T
    NC = T // CHUNK
    ng = HG_H // HG_GROUP
    nsw = len(swap_sibling)

    def body(z_ref, o_ref, st_ref, a_ref, dy_ref, lb_ref, gn_ref, dz_ref, dlb_ref, dgn_ref, ds_scr, racc, dgn_acc):
        lbs = _lower_bound(lb_ref[...])
        gn_v = gn_ref[...]
        tri, triu = _tri(CHUNK), _tri(CHUNK, upper=True)
        cmask = _iota((CHUNK, CHUNK), 0) >= _iota((CHUNK, CHUNK), 1)
        for ref in (ds_scr, racc, dgn_acc, dlb_ref):
            ref[...] = jnp.zeros_like(ref)

        def chunk(ci, carry):
            c = NC - 1 - ci
            r = pl.ds(pl.multiple_of(c * CHUNK, CHUNK), CHUNK)
            for hh in range(HG_GROUP):
                zc, oc = 4 * LANE * hh, LANE * hh
                lb_v = lbs[:, oc:oc + LANE]
                ql, fl, il, gl = (z_ref[r, zc + LANE * j:zc + LANE * (j + 1)].astype(F32) for j in range(4))
                sig, f, k, logf = _hg_gates(fl, lb_v)
                G = _prefix_mm(tri, logf)
                qs, dsilu_q = _silu_and_grad(ql)
                gs, dsilu_g = _silu_and_grad(gl)
                o = o_ref[r, oc:oc + LANE]
                dyv = dy_ref[r, oc:oc + LANE]
                rstd = lax.rsqrt(jnp.mean(o * o, axis=-1, keepdims=True) + EPS)
                oh = o * rstd
                dgl = dyv * oh * gn_v * dsilu_g
                dn = dyv * gs
                dgn_acc[...] += _rowsum8(dn * oh)
                u = dn * gn_v
                do = rstd * (u - oh * jnp.mean(u * oh, axis=-1, keepdims=True))
                st = st_ref[hh * NC + c]
                dst = ds_scr[hh]
                eG = jnp.exp(G)
                g_last = G[CHUNK - 1:CHUNK, :]
                eL = jnp.exp(g_last - G)
                dA = jnp.where(cmask, _mm3(_nt, do, il), 0.0)
                dq_in, dk_in = _hg_intra_bwd(dA, qs, k, G)
                di = _tn(a_ref[r, oc:oc + LANE][:, 0:CHUNK], _b(do)) + _nt(_b(k * eL), _b(dst))
                dq = dq_in + _mm3(_nn, do, st) * eG
                dk = dk_in + _mm3(_nn, il, dst) * eL
                ds_scr[hh] = dst * jnp.exp(g_last) + _mm3(_tn, do, qs * eG)
                dd = qs * dq - k * dk
                dlogf = _prefix_mm(triu, dd) + racc[hh]
                racc[hh] += jnp.sum(dd, axis=0, keepdims=True)
                df = dlogf / f - dk
                dlb_ref[8 * hh:8 * (hh + 1), :] += _rowsum8(df * (1.0 - sig))
                dz_ref[r, zc:zc + LANE] = (dq * dsilu_q).astype(BF16)
                dz_ref[r, zc + LANE:zc + 2 * LANE] = (df * (1.0 - lb_v) * sig * (1.0 - sig)).astype(BF16)
                dz_ref[r, zc + 2 * LANE:zc + 3 * LANE] = di.astype(BF16)
                dz_ref[r, zc + 3 * LANE:zc + 4 * LANE] = dgl.astype(BF16)
            return carry

        lax.fori_loop(0, NC, chunk, 0)
        dgn_ref[...] = dgn_acc[...]

    gw = HG_GROUP * LANE
    cb = C_HG // (4 * gw)
    col = pl.BlockSpec((T, gw), lambda b, h: (b, h))
    if nsw:
        body = _hosting(body, 7, 3, 3, nsw, _sibling_swap_phases, (B, ng))
    return pl.pallas_call(
        body, name="hgrn_bwd", grid=(B, ng),
        in_specs=[pl.BlockSpec((T, 4 * gw), lambda b, h: (b, cb + h)), col,
                  pl.BlockSpec((HG_GROUP * NC, HG_D, HG_D), lambda b, h: (b * ng + h, 0, 0)), col, col,
                  pl.BlockSpec((lb.shape[0], gw), lambda b, h: (0, h)), pl.BlockSpec((1, LANE), lambda b, h: (0, 0))]
        + [ANY] * nsw,
        out_specs=[pl.BlockSpec((T, 4 * gw), lambda b, h: (b, h)),
                   pl.BlockSpec((8 * HG_GROUP, LANE), lambda b, h: (b * ng + h, 0)),
                   pl.BlockSpec((8, LANE), lambda b, h: (b * ng + h, 0))] + [ANY] * nsw,
        out_shape=[S((N, 2048), BF16), S((B * HG_H * 8, LANE), F32), S((B * ng * 8, LANE), F32)]
        + _sibling_swap_shapes(swap_sibling),
        scratch_shapes=[pltpu.VMEM((HG_GROUP, HG_D, HG_D), F32), pltpu.VMEM((HG_GROUP, 1, LANE), F32),
                        pltpu.VMEM((8, LANE), F32)] + (_sibling_swap_sems(nsw) if nsw else []),
        compiler_params=_cp(("arbitrary", "arbitrary") if nsw else ("parallel", "parallel")),
    )(z, o_raw, states, a_mat, dy, lb, gn, *swap_sibling)


def _pair_mean(x, lo_half):
    a = jnp.sum(jnp.where(lo_half, x, 0.0), axis=-1, keepdims=True)
    b = jnp.sum(jnp.where(lo_half, 0.0, x), axis=-1, keepdims=True)
    return jnp.where(lo_half, a, b) * (1.0 / FOX_D)


def _fox_gate_fwd(z, bias, B, T):
    N = B * T
    tb = LANE

    def body(z_ref, b_ref, fc_ref, fct_ref):
        tri = _tri(tb)

        def step(i, carry):
            r = pl.ds(pl.multiple_of(i * tb, tb), tb)
            cs = _prefix_mm(tri, jax.nn.log_sigmoid(z_ref[r, :].astype(F32) + b_ref[...])) + carry
            fc_ref[r, :] = cs
            fct_ref[0, :, r] = cs.T[0:8, :]
            return cs[tb - 1:tb, :]

        lax.fori_loop(0, T // tb, step, jnp.zeros((1, LANE), F32))

    return pl.pallas_call(
        body, name="fox_gate_fwd", grid=(B,),
        in_specs=[pl.BlockSpec((T, LANE), lambda b: (b, C_FF // LANE)), pl.BlockSpec((1, LANE), lambda b: (0, 0))],
        out_specs=[pl.BlockSpec((T, LANE), lambda b: (b, 0)), pl.BlockSpec((1, 8, T), lambda b: (b, 0, 0))],
        out_shape=[S((N, LANE), F32), S((B, 8, T), F32)], compiler_params=_cp(("parallel",)),
    )(z, bias)


def _fox_gate_bwd(dfc, z, bias, B, T):
    N = B * T
    tb = LANE
    nt = T // tb

    def body(d_ref, z_ref, b_ref, dz_ref, db_ref):
        triu = _tri(tb, upper=True)
        db_ref[...] = jnp.zeros_like(db_ref)

        def step(ii, carry):
            r = pl.ds(pl.multiple_of((nt - 1 - ii) * tb, tb), tb)
            d = d_ref[r, 0:LANE]
            for p in range(1, FOX_P):
                d = d + d_ref[r, LANE * p:LANE * (p + 1)]
            rc = _prefix_mm(triu, d) + carry
            dff = rc * jax.nn.sigmoid(-(z_ref[r, :].astype(F32) + b_ref[...]))
            dz_ref[r, :] = dff.astype(BF16)
            db_ref[...] += _rowsum8(dff)
            return carry + jnp.sum(d, axis=0, keepdims=True)

        lax.fori_loop(0, nt, step, jnp.zeros((1, LANE), F32))

    return pl.pallas_call(
        body, name="fox_gate_bwd", grid=(B,),
        in_specs=[pl.BlockSpec((T, 512), lambda b: (b, 0)), pl.BlockSpec((T, LANE), lambda b: (b, C_FF // LANE)),
                  pl.BlockSpec((1, LANE), lambda b: (0, 0))],
        out_specs=[pl.BlockSpec((T, LANE), lambda b: (b, 0)), pl.BlockSpec((8, LANE), lambda b: (b, 0))],
        out_shape=[S((N, LANE), BF16), S((B * 8, LANE), F32)], compiler_params=_cp(("parallel",)),
    )(dfc, z, bias)


def _fox_prep(z_ref, gq, gk, r, lo_half):
    q, k, v = (z_ref[r, LANE * j:LANE * (j + 1)].astype(F32) for j in range(3))
    rq = lax.rsqrt(_pair_mean(q * q, lo_half) + EPS)
    rk = lax.rsqrt(_pair_mean(k * k, lo_half) + EPS)
    qh, kh = q * rq, k * rk
    return qh * gq * (FOX_D ** -0.5), kh * gk, v, qh, kh, rq, rk


def _fox_fwd(z, fc, fct, gq, gk, B, T, tq=512, gather=()):
    N = B * T
    NQ = T // tq
    nga = len(gather)

    def body(z_ref, fc_ref, fct_ref, gq_ref, gk_ref, y_ref, lse_ref, qn_s, kn_s, v_s):
        p, qi = pl.program_id(1), pl.program_id(2)
        lo_half = _iota((1, LANE), 1) < FOX_D

        @pl.when(qi == 0)
        def _():
            def prep(i, carry):
                r = pl.ds(pl.multiple_of(i * tq, tq), tq)
                qn, kn, v = _fox_prep(z_ref, gq_ref[...], gk_ref[...], r, lo_half)[:3]
                qn_s[r, :], kn_s[r, :], v_s[r, :] = qn.astype(BF16), kn.astype(BF16), v.astype(BF16)
                return carry
            lax.fori_loop(0, NQ, prep, 0)

        rq = pl.ds(pl.multiple_of(qi * tq, tq), tq)
        qn = qn_s[rq, :]
        fcq = fc_ref[rq, :]
        lane = _iota((tq, LANE), 1)
        causal = _iota((tq, tq), 0) >= _iota((tq, tq), 1)
        qhs = [jnp.where(lo_half, qn, jnp.zeros_like(qn)), jnp.where(lo_half, jnp.zeros_like(qn), qn)]
        fqs = [jnp.sum(jnp.where(lane == 2 * p + hh, fcq, 0.0), axis=-1, keepdims=True) for hh in range(2)]

        def kv(j, carry, diagonal):
            rk = pl.ds(pl.multiple_of(j * tq, tq), tq)
            kj, vj = kn_s[rk, :], v_s[rk, :]
            one = jnp.ones_like(vj)
            new = []
            for hh in range(2):
                m, acc = carry[hh]
                s = _nt(qhs[hh], kj) + fqs[hh] - fct_ref[0, pl.ds(2 * p + hh, 1), rk]
                if diagonal:
                    s = jnp.where(causal, s, NEG)
                m_new = jnp.maximum(m, jnp.max(s, axis=-1, keepdims=True))
                pe = jnp.exp(s - m_new)
                v_aug = jnp.where(lo_half if hh == 0 else jnp.logical_not(lo_half), vj, one)
                new.append((m_new, jnp.exp(m - m_new) * acc + _nn(pe.astype(BF16), v_aug)))
            return tuple(new)

        init = tuple((jnp.full((tq, 1), NEG, F32), jnp.zeros((tq, LANE), F32)) for _ in range(2))
        carry = lax.fori_loop(0, qi, functools.partial(kv, diagonal=False), init)
        (m0, a0), (m1, a1) = kv(qi, carry, True)
        l0, l1 = a0[:, FOX_D:FOX_D + 1], a1[:, 0:1]
        y_ref[...] = jnp.where(lo_half, a0 / l0, a1 / l1).astype(BF16)
        lse_ref[...] = jnp.where(lo_half, m0 + jnp.log(l0), m1 + jnp.log(l1))

    vec = pl.BlockSpec((1, LANE), lambda b, p, q: (0, 0))
    tile = pl.BlockSpec((tq, LANE), lambda b, p, q: (b * NQ + q, p))
    if nga:
        body = _hosting(body, 5, 2, 3, nga, _gather_phases, (B, FOX_P, NQ))
    return pl.pallas_call(
        body, name="fox_fwd", grid=(B, FOX_P, NQ),
        in_specs=[pl.BlockSpec((T, 384), lambda b, p, q: (b, p)), pl.BlockSpec((T, LANE), lambda b, p, q: (b, 0)),
                  pl.BlockSpec((1, 8, T), lambda b, p, q: (b, 0, 0)), vec, vec] + [ANY] * nga,
        out_specs=[tile, tile] + [ANY] * nga, out_shape=[S((N, 512), BF16), S((N, 512), F32)] + _gather_shapes(gather),
        scratch_shapes=[pltpu.VMEM((T, LANE), BF16)] * 3 + (_gather_sems(nga) if nga else []),
        compiler_params=_cp(("arbitrary",) * 3 if nga else ("parallel", "parallel", "arbitrary")),
    )(z, fc, fct, gq, gk, *gather)


def _fox_bwd(z, dy, y, lse, fc, fct, gq, gk, B, T, tq=512, swap=()):
    N = B * T
    NQ = T // tq
    nsw = len(swap)

    def body(z_ref, dy_ref, y_ref, lse_ref, fc_ref, fct_ref, gq_ref, gk_ref, dz_ref, dfc_ref, dgq_ref, dgk_ref,
             qn_s, kn_s, v_s, do_s, delta_s, dq_s, dfk_s):
        p, kj = pl.program_id(1), pl.program_id(2)
        lo_half = _iota((1, LANE), 1) < FOX_D
        lane = _iota((tq, LANE), 1)
        gq_v, gk_v = gq_ref[...], gk_ref[...]

        @pl.when(kj == 0)
        def _():
            def prep(i, carry):
                r = pl.ds(pl.multiple_of(i * tq, tq), tq)
                qn, kn, v = _fox_prep(z_ref, gq_v, gk_v, r, lo_half)[:3]
                qn_s[r, :], kn_s[r, :], v_s[r, :] = qn.astype(BF16), kn.astype(BF16), v.astype(BF16)
                do = dy_ref[r, :]
                do_s[r, :] = do.astype(BF16)
                delta_s[r, :] = _pair_mean(do * y_ref[r, :].astype(F32), lo_half) * float(FOX_D)
                return carry
            lax.fori_loop(0, NQ, prep, 0)
            dq_s[...] = jnp.zeros_like(dq_s)
            dgq_ref[...] = jnp.zeros_like(dgq_ref)
            dgk_ref[...] = jnp.zeros_like(dgk_ref)

        rk = pl.ds(pl.multiple_of(kj * tq, tq), tq)
        kn, vv = kn_s[rk, :], v_s[rk, :]
        causal = _iota((tq, tq), 0) >= _iota((tq, tq), 1)
        zero, one = jnp.zeros_like(kn), jnp.ones_like(kn)
        hms = [lo_half, jnp.logical_not(lo_half)]
        kmasks = [jnp.where(hm, kn, zero) for hm in hms]
        kaugs = [jnp.where(hm, kn, one) for hm in hms]
        vmasks = [jnp.where(hm, vv, zero) for hm in hms]
        fks = [fct_ref[0, pl.ds(2 * p + hh, 1), rk] for hh in range(2)]

        def qloop(i, carry, diagonal):
            ri = pl.ds(pl.multiple_of(i * tq, tq), tq)
            qn = qn_s[ri, :]
            do = do_s[ri, :]
            fcq = fc_ref[ri, :]
            new = []
            for hh in range(2):
                dk_acc, dv_acc = carry[hh]
                c0 = FOX_D * hh
                fq = jnp.sum(jnp.where(lane == 2 * p + hh, fcq, 0.0), axis=-1, keepdims=True)
                pr = jnp.exp(_nt(qn, kmasks[hh]) + fq - fks[hh] - lse_ref[ri, c0:c0 + 1])
                if diagonal:
                    pr = jnp.where(causal, pr, 0.0)
                ds = (pr * (_nt(do, vmasks[hh]) - delta_s[ri, c0:c0 + 1])).astype(BF16)
                dq_s[hh, ri, :] += _nn(ds, kaugs[hh])
                new.append((dk_acc + _tn(jnp.where(hms[hh], qn, one), ds), dv_acc + _tn(do, pr.astype(BF16))))
            return tuple(new)

        init = tuple((jnp.zeros((LANE, tq), F32), jnp.zeros((LANE, tq), F32)) for _ in range(2))
        carry = qloop(kj, init, True)
        (dk0, dv0), (dk1, dv1) = lax.fori_loop(kj + 1, NQ, functools.partial(qloop, diagonal=False), carry)
        dks, dvs = [dk0.T, dk1.T], [dv0.T, dv1.T]

        dkn = jnp.where(lo_half, dks[0], dks[1])
        _, _, _, _, kh, _, rkk = _fox_prep(z_ref, gq_v, gk_v, rk, lo_half)
        u = dkn * gk_v
        dz_ref[rk, LANE:2 * LANE] = (rkk * (u - kh * _pair_mean(u * kh, lo_half))).astype(BF16)
        dz_ref[rk, 2 * LANE:3 * LANE] = jnp.where(lo_half, dvs[0], dvs[1]).astype(BF16)
        dgk_ref[...] += _rowsum8(dkn * kh)
        dfk_s[rk, :] = jnp.where(lane == 2 * p, -dks[0][:, FOX_D:FOX_D + 1],
                                 jnp.where(lane == 2 * p + 1, -dks[1][:, 0:1], 0.0))

        @pl.when(kj == NQ - 1)
        def _():
            def fin(i, carry):
                r = pl.ds(pl.multiple_of(i * tq, tq), tq)
                d0, d1 = dq_s[0, r, :], dq_s[1, r, :]
                dqn = jnp.where(lo_half, d0, d1)
                _, _, _, qh, _, rqq, _ = _fox_prep(z_ref, gq_v, gk_v, r, lo_half)
                u = dqn * gq_v * (FOX_D ** -0.5)
                dz_ref[r, 0:LANE] = (rqq * (u - qh * _pair_mean(u * qh, lo_half))).astype(BF16)
                dgq_ref[...] += _rowsum8(dqn * qh) * (FOX_D ** -0.5)
                dfc_ref[r, :] = dfk_s[r, :] + jnp.where(lane == 2 * p, d0[:, FOX_D:FOX_D + 1],
                                                        jnp.where(lane == 2 * p + 1, d1[:, 0:1], 0.0))
                return carry
            lax.fori_loop(0, NQ, fin, 0)

    vec = pl.BlockSpec((1, LANE), lambda b, p, k: (0, 0))
    col = pl.BlockSpec((T, LANE), lambda b, p, k: (b, p))
    part = pl.BlockSpec((8, LANE), lambda b, p, k: (b * FOX_P + p, 0))
    if nsw:
        body = _hosting(body, 8, 4, 7, nsw, _chip_swap_phases, (B, FOX_P, NQ))
    return pl.pallas_call(
        body, name="fox_bwd", grid=(B, FOX_P, NQ),
        in_specs=[pl.BlockSpec((T, 384), lambda b, p, k: (b, p)), col, col, col,
                  pl.BlockSpec((T, LANE), lambda b, p, k: (b, 0)), pl.BlockSpec((1, 8, T), lambda b, p, k: (b, 0, 0)),
                  vec, vec] + [ANY] * nsw,
        out_specs=[pl.BlockSpec((T, 384), lambda b, p, k: (b, p)), col, part, part] + [ANY] * nsw,
        out_shape=[S((N, 1536), BF16), S((N, 512), F32), S((B * FOX_P * 8, LANE), F32), S((B * FOX_P * 8, LANE), F32)]
        + [S(p.shape, p.dtype) for p in swap],
        scratch_shapes=[pltpu.VMEM((T, LANE), BF16)] * 4 + [pltpu.VMEM((T, LANE), F32), pltpu.VMEM((2, T, LANE), F32),
                                                            pltpu.VMEM((T, LANE), F32)]
        + (_chip_swap_sems(nsw) if nsw else []),
        compiler_params=_cp(("arbitrary",) * 3 if nsw else ("parallel", "parallel", "arbitrary")),
    )(z, dy, y, lse, fc, fct, gq, gk, *swap)


def _mem_scores(z_ref, kv_ref, gq, gk, h):
    c = slice(MEM_D * h, MEM_D * (h + 1))
    q, k = z_ref[:, c].astype(F32), kv_ref[:, c]
    rq = lax.rsqrt(jnp.mean(q * q, axis=-1, keepdims=True) + EPS)
    rk = lax.rsqrt(jnp.mean(k * k, axis=-1, keepdims=True) + EPS)
    qh, kh = q * rq, k * rk
    qn = (qh * gq * (MEM_D ** -0.5)).astype(BF16)
    kn = (kh * gk).astype(BF16)
    s = _nt(qn, kn)
    pe = jnp.exp(s - jnp.max(s, axis=-1, keepdims=True))
    pn = pe / jnp.sum(pe, axis=-1, keepdims=True)
    return pn, qn, kn, qh, kh, rq, rk


def _mem_fwd(z, memkv, gq, gk, B, T, M, tq=512):
    N = B * T
    NQ = T // tq
    W = MEM_H * MEM_D

    def body(z_ref, kv_ref, gq_ref, gk_ref, y_ref):
        for h in range(MEM_H):
            pn = _mem_scores(z_ref, kv_ref, gq_ref[...], gk_ref[...], h)[0]
            v = kv_ref[:, W + MEM_D * h:W + MEM_D * (h + 1)].astype(BF16)
            y_ref[:, MEM_D * h:MEM_D * (h + 1)] = _nn(pn.astype(BF16), v).astype(BF16)

    vec = pl.BlockSpec((1, LANE), lambda b, q: (0, 0))
    return pl.pallas_call(
        body, name="mem_fwd", grid=(B, NQ),
        in_specs=[pl.BlockSpec((tq, W), lambda b, q: (b * NQ + q, C_MQ // W)),
                  pl.BlockSpec((M, 2 * W), lambda b, q: (b, 0)), vec, vec],
        out_specs=pl.BlockSpec((tq, W), lambda b, q: (b * NQ + q, 0)), out_shape=S((N, W), BF16),
        compiler_params=_cp(("parallel", "parallel")),
    )(z, memkv, gq, gk)


def _mem_bwd(z, memkv, dy, gq, gk, B, T, M, tq=512):
    N = B * T
    NQ = T // tq
    W = MEM_H * MEM_D

    def body(z_ref, kv_ref, dy_ref, gq_ref, gk_ref, dz_ref, dkv_ref, dgq_ref, dgk_ref, acc):
        qi = pl.program_id(1)
        gq_v, gk_v = gq_ref[...], gk_ref[...]

        @pl.when(qi == 0)
        def _():
            acc[...] = jnp.zeros_like(acc)
            dgq_ref[...] = jnp.zeros_like(dgq_ref)
            dgk_ref[...] = jnp.zeros_like(dgk_ref)

        for h in range(MEM_H):
            c = slice(MEM_D * h, MEM_D * (h + 1))
            cv = slice(W + MEM_D * h, W + MEM_D * (h + 1))
            pn, qn, kn, qh, _, rq, _ = _mem_scores(z_ref, kv_ref, gq_v, gk_v, h)
            do = dy_ref[:, c].astype(BF16)
            dp = _nt(do, kv_ref[:, cv].astype(BF16))
            ds = (pn * (dp - jnp.sum(dp * pn, axis=-1, keepdims=True))).astype(BF16)
            dqn = _nn(ds, kn)
            acc[:, c] += _tn(ds, qn)
            acc[:, cv] += _tn(pn.astype(BF16), do)
            u = dqn * gq_v * (MEM_D ** -0.5)
            dz_ref[:, c] = (rq * (u - qh * jnp.mean(u * qh, axis=-1, keepdims=True))).astype(BF16)
            dgq_ref[...] += _rowsum8(dqn * qh) * (MEM_D ** -0.5)

        @pl.when(qi == NQ - 1)
        def _():
            for h in range(MEM_H):
                c = slice(MEM_D * h, MEM_D * (h + 1))
                cv = slice(W + MEM_D * h, W + MEM_D * (h + 1))
                k = kv_ref[:, c]
                rk = lax.rsqrt(jnp.mean(k * k, axis=-1, keepdims=True) + EPS)
                kh = k * rk
                dkn = acc[:, c]
                u = dkn * gk_v
                dkv_ref[:, c] = (rk * (u - kh * jnp.mean(u * kh, axis=-1, keepdims=True))).astype(BF16)
                dkv_ref[:, cv] = acc[:, cv].astype(BF16)
                dgk_ref[...] += _rowsum8(dkn * kh)

    vec = pl.BlockSpec((1, LANE), lambda b, q: (0, 0))
    part = pl.BlockSpec((8, LANE), lambda b, q: (b, 0))
    return pl.pallas_call(
        body, name="mem_bwd", grid=(B, NQ),
        in_specs=[pl.BlockSpec((tq, W), lambda b, q: (b * NQ + q, C_MQ // W)),
                  pl.BlockSpec((M, 2 * W), lambda b, q: (b, 0)), pl.BlockSpec((tq, W), lambda b, q: (b * NQ + q, 0)),
                  vec, vec],
        out_specs=[pl.BlockSpec((tq, W), lambda b, q: (b * NQ + q, 0)), pl.BlockSpec((M, 2 * W), lambda b, q: (b, 0)),
                   part, part],
        out_shape=[S((N, W), BF16), S((B * M, 2 * W), BF16), S((B * 8, LANE), F32), S((B * 8, LANE), F32)],
        scratch_shapes=[pltpu.VMEM((M, 2 * W), F32)], compiler_params=_cp(("parallel", "arbitrary")),
    )(z, memkv, dy, gq, gk)


def _merge_fwd(ya, yb, yc, z, x, wa, wb, wc, wo, g_next, tm=512):
    n, d = x.shape
    wdt = ya.shape[1]
    gb = C_GATE // d

    def body(ya_ref, yb_ref, yc_ref, g0_ref, g1_ref, g2_ref, x_ref, wa_ref, wb_ref, wc_ref, wo_ref, gn_ref,
             x1_ref, mg_ref, ua_ref, ub_ref, uc_ref, h_ref):
        merged = jnp.zeros((tm, d), F32)
        for y_ref, g_ref, w_ref, u_ref in ((ya_ref, g0_ref, wa_ref, ua_ref), (yb_ref, g1_ref, wb_ref, ub_ref),
                                           (yc_ref, g2_ref, wc_ref, uc_ref)):
            u = _nn(y_ref[...], w_ref[...])
            u_ref[...] = u.astype(BF16)
            merged = merged + jax.nn.sigmoid(g_ref[...].astype(F32)) * u
        mb = merged.astype(BF16)
        mg_ref[...] = mb
        x1 = x_ref[...] + _nn(mb, wo_ref[...])
        x1_ref[...] = x1
        h_ref[...] = (x1 * lax.rsqrt(jnp.mean(x1 * x1, axis=-1, keepdims=True) + EPS) * gn_ref[...]).astype(BF16)

    yt = pl.BlockSpec((tm, wdt), lambda i: (i, 0))
    xt = pl.BlockSpec((tm, d), lambda i: (i, 0))
    wbr = pl.BlockSpec((wdt, d), lambda i: (0, 0))
    gates = [pl.BlockSpec((tm, d), functools.partial(lambda i, k: (i, gb + k), k=k)) for k in range(3)]
    return pl.pallas_call(
        body, name="merge_fwd", grid=(n // tm,),
        in_specs=[yt, yt, yt] + gates + [xt, wbr, wbr, wbr, pl.BlockSpec((d, d), lambda i: (0, 0)),
                                         pl.BlockSpec((1, d), lambda i: (0, 0))],
        out_specs=[xt] * 6, out_shape=[S((n, d), F32)] + [S((n, d), BF16)] * 5, compiler_params=_cp(("parallel",)),
    )(ya, yb, yc, z, z, z, x, wa, wb, wc, wo, g_next)


def _merge_bwd(dx1, z, ua, ub, uc, wa, wb, wc, wo, tm=512):
    n, d = dx1.shape
    wdt = wa.shape[0]
    gb = C_GATE // d

    def body(dx_ref, g0_ref, g1_ref, g2_ref, ua_ref, ub_ref, uc_ref, wa_ref, wb_ref, wc_ref, wo_ref,
             dg_ref, dya_ref, dyb_ref, dyc_ref, dua_ref, dub_ref, duc_ref):
        dm = _nt(dx_ref[...].astype(BF16), wo_ref[...])
        for k, (g_ref, u_ref, w_ref, dy_ref, du_ref) in enumerate((
                (g0_ref, ua_ref, wa_ref, dya_ref, dua_ref), (g1_ref, ub_ref, wb_ref, dyb_ref, dub_ref),
                (g2_ref, uc_ref, wc_ref, dyc_ref, duc_ref))):
            g = jax.nn.sigmoid(g_ref[...].astype(F32))
            du = (dm * g).astype(BF16)
            du_ref[...] = du
            dg_ref[:, d * k:d * (k + 1)] = (dm * u_ref[...].astype(F32) * g * (1.0 - g)).astype(BF16)
            dy_ref[...] = _nt(du, w_ref[...])

    yt = pl.BlockSpec((tm, wdt), lambda i: (i, 0))
    xt = pl.BlockSpec((tm, d), lambda i: (i, 0))
    wbr = pl.BlockSpec((wdt, d), lambda i: (0, 0))
    gates = [pl.BlockSpec((tm, d), functools.partial(lambda i, k: (i, gb + k), k=k)) for k in range(3)]
    return pl.pallas_call(
        body, name="merge_bwd", grid=(n // tm,),
        in_specs=[xt] + gates + [xt, xt, xt, wbr, wbr, wbr, pl.BlockSpec((d, d), lambda i: (0, 0))],
        out_specs=[pl.BlockSpec((tm, 3 * d), lambda i: (i, 0)), yt, yt, yt, xt, xt, xt],
        out_shape=[S((n, 3 * d), BF16)] + [S((n, wdt), F32)] * 3 + [S((n, d), BF16)] * 3,
        compiler_params=_cp(("parallel",)),
    )(dx1, z, z, z, ua, ub, uc, wa, wb, wc, wo)


FFN_TN = 1408
TN_TM = 2048
INV_SQRT2 = 0.7071067811865476
INV_SQRT_2PI = 0.3989422804014327


def _conv_shifted(a, prev, first, tm):
    row = _iota(a.shape, 0)
    p7 = jnp.where(first, 0.0, prev[7:8, :])
    p6 = jnp.where(first, 0.0, prev[6:7, :])
    a1 = jnp.where(row == 0, p7, pltpu.roll(a, 1, 0))
    a2 = jnp.where(row == 0, p6, jnp.where(row == 1, p7, pltpu.roll(a, 2, 0)))
    return a1, a2


def _ffn_act_fwd(up, cw, cb, B, T, tm=512):
    N = B * T
    dff = cw.shape[1]
    NT, NJ, tn = T // tm, dff // FFN_TN, FFN_TN

    def body(a_ref, v_ref, cw_ref, cb_ref, y_ref, c_ref, carry):
        t = pl.program_id(2)
        a = a_ref[...].astype(F32)
        a1, a2 = _conv_shifted(a, carry[...], t == 0, tm)
        w = cw_ref[...]
        ac = w[0:1, :] * a2 + w[1:2, :] * a1 + w[2:3, :] * a + cb_ref[...]
        cdf = 0.5 * (1.0 + lax.erf(ac * INV_SQRT2))
        y_ref[...] = (ac * cdf * v_ref[...].astype(F32)).astype(BF16)
        c_ref[...] = cdf.astype(BF16)
        carry[...] = a[tm - 8:tm, :]

    return pl.pallas_call(
        body, name="ffn_act_fwd", grid=(B, NJ, NT),
        in_specs=[pl.BlockSpec((tm, tn), lambda b, j, t: (b * NT + t, j)),
                  pl.BlockSpec((tm, tn), lambda b, j, t: (b * NT + t, NJ + j)),
                  pl.BlockSpec((3, tn), lambda b, j, t: (0, j)), pl.BlockSpec((1, tn), lambda b, j, t: (0, j))],
        out_specs=[pl.BlockSpec((tm, tn), lambda b, j, t: (b * NT + t, j))] * 2, out_shape=[S((N, dff), BF16)] * 2,
        scratch_shapes=[pltpu.VMEM((8, tn), F32)], compiler_params=_cp(("parallel", "parallel", "arbitrary")),
    )(up, up, cw, cb)


def _ffn_down_loss(y, wd, x1, tgt, tm=256):
    n, d = x1.shape
    kf = y.shape[1]

    def body(y_ref, w_ref, x_ref, t_ref, dx_ref, ls_ref):
        err = x_ref[...] + _nn(y_ref[...], w_ref[...]) - t_ref[...]
        dx_ref[...] = err * (1.0 / d)

        @pl.when(pl.program_id(0) == 0)
        def _():
            ls_ref[...] = jnp.zeros_like(ls_ref)

        ls_ref[...] += _rowsum8(err * err) * (0.5 / d)

    xt = pl.BlockSpec((tm, d), lambda i: (i, 0))
    return pl.pallas_call(
        body, name="ffn_down_loss", grid=(n // tm,),
        in_specs=[pl.BlockSpec((tm, kf), lambda i: (i, 0)), pl.BlockSpec((kf, d), lambda i: (0, 0)), xt, xt],
        out_specs=[xt, pl.BlockSpec((8, d), lambda i: (0, 0))], out_shape=[S((n, d), F32), S((8, d), F32)],
        compiler_params=_cp(("arbitrary",)),
    )(y, wd, x1, tgt)


def _ffn_act_bwd1(dx2, wd, up, cdf, cw, cb, B, T, tm=512):
    N = B * T
    d = dx2.shape[1]
    dff = cw.shape[1]
    NT, NJ, tn = T // tm, dff // FFN_TN, FFN_TN

    def body(dx_ref, w_ref, a_ref, v_ref, c_ref, cw_ref, cb_ref, dac_ref, dv_ref, dcw_ref, dcb_ref, carry):
        b, t = pl.program_id(1), pl.program_id(2)
        a = a_ref[...].astype(F32)
        a1, a2 = _conv_shifted(a, carry[...], t == 0, tm)
        carry[...] = a[tm - 8:tm, :]
        w = cw_ref[...]
        ac = w[0:1, :] * a2 + w[1:2, :] * a1 + w[2:3, :] * a + cb_ref[...]
        dy = _nt(dx_ref[...].astype(BF16), w_ref[...])
        cdf = c_ref[...].astype(F32)
        dv_ref[...] = (dy * ac * cdf).astype(BF16)
        dac = dy * v_ref[...].astype(F32) * (cdf + ac * jnp.exp(-0.5 * ac * ac) * INV_SQRT_2PI)
        dac_ref[...] = dac

        @pl.when((b == 0) & (t == 0))
        def _():
            dcw_ref[...] = jnp.zeros_like(dcw_ref)
            dcb_ref[...] = jnp.zeros_like(dcb_ref)

        dcw_ref[0:8, :] += _rowsum8(dac * a2)
        dcw_ref[8:16, :] += _rowsum8(dac * a1)
        dcw_ref[16:24, :] += _rowsum8(dac * a)
        dcb_ref[...] += _rowsum8(dac)

    return pl.pallas_call(
        body, name="ffn_act_bwd1", grid=(NJ, B, NT),
        in_specs=[pl.BlockSpec((tm, d), lambda j, b, t: (b * NT + t, 0)), pl.BlockSpec((tn, d), lambda j, b, t: (j, 0)),
                  pl.BlockSpec((tm, tn), lambda j, b, t: (b * NT + t, j)),
                  pl.BlockSpec((tm, tn), lambda j, b, t: (b * NT + t, NJ + j)),
                  pl.BlockSpec((tm, tn), lambda j, b, t: (b * NT + t, j)),
                  pl.BlockSpec((3, tn), lambda j, b, t: (0, j)), pl.BlockSpec((1, tn), lambda j, b, t: (0, j))],
        out_specs=[pl.BlockSpec((tm, tn), lambda j, b, t: (b * NT + t, j)),
                   pl.BlockSpec((tm, tn), lambda j, b, t: (b * NT + t, j)),
                   pl.BlockSpec((24, tn), lambda j, b, t: (0, j)), pl.BlockSpec((8, tn), lambda j, b, t: (0, j))],
        out_shape=[S((N, dff), F32), S((N, dff), BF16), S((24, dff), F32), S((8, dff), F32)],
        scratch_shapes=[pltpu.VMEM((8, tn), F32)], compiler_params=_cp(("parallel", "arbitrary", "arbitrary")),
    )(dx2, wd, up, up, cdf, cw, cb)


def _ffn_act_bwd2(dac, cw, B, T, tm=512):
    N = B * T
    dff = cw.shape[1]
    NT, NJ, tn = T // tm, dff // FFN_TN, FFN_TN
    last8 = N // 8 - 1

    def body(d_ref, nx_ref, cw_ref, da_ref):
        t = pl.program_id(2)
        dd = d_ref[...]
        row = _iota(dd.shape, 0)
        last = t == NT - 1
        n0 = jnp.where(last, 0.0, nx_ref[0:1, :])
        n1 = jnp.where(last, 0.0, nx_ref[1:2, :])
        d1 = jnp.where(row == tm - 1, n0, pltpu.roll(dd, tm - 1, 0))
        d2 = jnp.where(row == tm - 1, n1, jnp.where(row == tm - 2, n0, pltpu.roll(dd, tm - 2, 0)))
        w = cw_ref[...]
        da_ref[...] = (w[2:3, :] * dd + w[1:2, :] * d1 + w[0:1, :] * d2).astype(BF16)

    return pl.pallas_call(
        body, name="ffn_act_bwd2", grid=(B, NJ, NT),
        in_specs=[pl.BlockSpec((tm, tn), lambda b, j, t: (b * NT + t, j)),
                  pl.BlockSpec((8, tn), lambda b, j, t: (jnp.minimum((b * NT + t + 1) * (tm // 8), last8), j)),
                  pl.BlockSpec((3, tn), lambda b, j, t: (0, j))],
        out_specs=pl.BlockSpec((tm, tn), lambda b, j, t: (b * NT + t, j)), out_shape=S((N, dff), BF16),
        compiler_params=_cp(("parallel", "parallel", "parallel")),
    )(dac, dac, cw)


def _fold_rows(p, name):
    r, c = p.shape[0] // 8, p.shape[1]

    def body(p_ref, o_ref):
        for j in range(r):
            o_ref[j:j + 1, :] = jnp.sum(p_ref[8 * j:8 * (j + 1), :], axis=0, keepdims=True)

    return pl.pallas_call(body, name=name, out_shape=S((r, c), F32), compiler_params=_cp())(p)


def _small_reduce(lbl, dg_mix, dg_mem, dlb_p, dgn_p, dfb_p, dgq_p, dgk_p, dmq_p, dmk_p, dg_ffn, dcb_p, loss_p):
    d, dff = dg_mix.shape[1], dcb_p.shape[1]
    nbh = dlb_p.shape[0] // (8 * HG_H)

    def colsum(ref):
        return jnp.sum(ref[...], axis=0, keepdims=True)

    def body(lbl_ref, mix_ref, mem_ref, dlb_ref, dgn_ref, dfb_ref, dgq_ref, dgk_ref, dmq_ref, dmk_ref, ffn_ref, dcb_ref,
             ls_ref, o_mix, o_mem, o_lb, o_hgn, o_fb, o_fq, o_fk, o_mq, o_mk, o_ffn, o_cb, o_loss):
        o_mix[...], o_mem[...], o_ffn[...], o_cb[...] = colsum(mix_ref), colsum(mem_ref), colsum(ffn_ref), colsum(dcb_ref)
        o_hgn[...], o_fb[...], o_mq[...], o_mk[...] = colsum(dgn_ref), colsum(dfb_ref), colsum(dmq_ref), colsum(dmk_ref)
        for src, dst in ((dgq_ref, o_fq), (dgk_ref, o_fk)):
            v = colsum(src)
            dst[...] = v + pltpu.roll(v, FOX_D, 1)
        o_loss[...] = jnp.zeros((1, LANE), F32) + jnp.sum(colsum(ls_ref), axis=-1, keepdims=True)
        logits = lbl_ref[...]
        e = jnp.exp(logits - jnp.max(logits, axis=0, keepdims=True))
        pr = e / jnp.sum(e, axis=0, keepdims=True)
        rows = _iota((8, LANE), 0)
        for h in range(HG_H):
            acc = jnp.zeros((8, LANE), F32)
            for b in range(nbh):
                acc = acc + dlb_ref[8 * (b * HG_H + h):8 * (b * HG_H + h + 1), :]
            dlb = jnp.sum(acc, axis=0, keepdims=True)
            c = slice(LANE * h, LANE * (h + 1))
            p0 = pr[0:1, c]
            first = _iota((logits.shape[0], LANE), 0) == 0
            o_lb[:, c] = pr[:, c] * (jnp.where(first, 1.0, 0.0) - p0) * dlb

    outs = [S((1, d), F32), S((1, d), F32), S(lbl.shape, F32)] + [S((1, LANE), F32)] * 6 + \
           [S((1, d), F32), S((1, dff), F32), S((1, LANE), F32)]
    return pl.pallas_call(body, name="small_reduce", out_shape=outs, compiler_params=_cp())(
        lbl, dg_mix, dg_mem, dlb_p, dgn_p, dfb_p, dgq_p, dgk_p, dmq_p, dmk_p, dg_ffn, dcb_p, loss_p)


def _in_col_pieces():
    hw, fw = HG_H * HG_D, FOX_H * FOX_D
    fox0, ff0 = 4 * hw, 4 * hw + 3 * fw
    mq0 = ff0 + FOX_H
    gate0 = mq0 + MEM_H * MEM_D
    pieces = []
    for p in range(FOX_P):
        pieces += [(fox0 + j * fw + LANE * p, LANE) for j in range(3)]
    pieces.append((mq0, MEM_H * MEM_D))
    for h in range(HG_H):
        pieces += [(j * hw + HG_D * h, HG_D) for j in range(4)]
    pieces.append((gate0, C_FF - C_GATE))
    pieces.append((ff0, FOX_H))
    return pieces


def _perm_from_blocks(blocks):
    n_blk, _, c = blocks.shape
    parts = []
    for s, n in _in_col_pieces():
        lo = s
        while lo < s + n:
            d = lo // c
            hi = min(s + n, (d + 1) * c)
            parts.append(blocks[d][:, lo - d * c:hi - d * c])
            lo = hi
    parts.append(jnp.zeros((blocks.shape[1], C_END - C_FF - FOX_H), blocks.dtype))
    return jnp.concatenate(parts, axis=1)


def _unperm_blocks(segs, n_blk):
    starts = [0]
    for a in segs:
        starts.append(starts[-1] + a.shape[1])
    new_start, placed = 0, []
    for s, n in _in_col_pieces():
        placed.append((s, new_start, n))
        new_start += n
    placed.sort()
    c = sum(n for _, _, n in placed) // n_blk
    blocks = []
    for d in range(n_blk):
        parts = []
        for s, ns, n in placed:
            lo, hi = max(s, d * c), min(s + n, (d + 1) * c)
            if lo < hi:
                i = max(j for j in range(len(segs)) if starts[j] <= ns)
                parts.append(segs[i][:, ns + lo - s - starts[i]:ns + hi - s - starts[i]])
        blocks.append(jnp.concatenate(parts, axis=1))
    return jnp.stack(blocks)


def _local_step(x2, mem2, tgt, sm, W, B, T, M, ex=None):
    fbias = jnp.pad(sm["fox_f_bias"], ((0, 0), (0, LANE - FOX_H)))
    gq2 = jnp.concatenate([sm["fox_q_norm_g"]] * 2, axis=1)
    gk2 = jnp.concatenate([sm["fox_k_norm_g"]] * 2, axis=1)
    lbl = sm["hgrn_lb_logits"]
    h = _rmsnorm_cast(x2, sm["norm_mix_g"], "norm_mix")
    z = _mm_nn(h, W["w_in"], BF16, "proj_in", 512, 2432)
    memn = _rmsnorm_cast(mem2, sm["norm_mem_g"], "norm_mem", tm=256)
    memkv = _mm_nn(memn, W["mem_kv_w"], F32, "proj_memkv", 256, 512)
    ya, o_raw, states, a_mat = _hgrn_fwd(z, lbl, sm["hgrn_norm_g"], B, T)
    fc, fct = _fox_gate_fwd(z, fbias, B, T)
    yb, lse, *late = _fox_fwd(z, fc, fct, gq2, gk2, B, T, gather=ex.late_blocks() if ex else ())
    if ex:
        W = {**W, **ex.unpack_late(late)}
    yc = _mem_fwd(z, memkv, sm["mem_q_norm_g"], sm["mem_k_norm_g"], B, T, M)
    x1, merged, ua, ub, uc, h2 = _merge_fwd(ya, yb, yc, z, x2, W["w_br_hgrn"], W["w_br_fox"], W["w_br_mem"], W["w_out"],
                                            sm["norm_ffn_g"])
    up = _mm_nn(h2, W["ffn_w_up"], BF16, "ffn_up", 512, FFN_TN)
    yf, cdf = _ffn_act_fwd(up, W["ffn_conv_w"], sm["ffn_conv_b"], B, T)
    dx2, loss_p = _ffn_down_loss(yf, W["ffn_w_down"], x1, tgt)
    dff = W["ffn_conv_w"].shape[1]
    dac, dv, dcw_p, dcb_p = _ffn_act_bwd1(dx2, W["ffn_w_down"], up, cdf, W["ffn_conv_w"], sm["ffn_conv_b"], B, T)
    da = _ffn_act_bwd2(dac, W["ffn_conv_w"], B, T)
    g = {"ffn_conv_w": _fold_rows(dcw_p, "g_conv_w")}
    g["ffn_w_down"] = _mm_tn(yf, dx2, "g_w_down", TN_TM, 512)
    dh2 = _mm_nt_sum([(da, 0, dff, 0), (dv, 0, dff, dff)], W["ffn_w_up"], "dh2", 256)
    g["ffn_w_up"] = [_mm_tn(h2, da, "g_w_up_a", TN_TM, FFN_TN), _mm_tn(h2, dv, "g_w_up_v", TN_TM, FFN_TN)]
    dx1, dg_ffn = _rmsnorm_bwd(dh2, x1, sm["norm_ffn_g"], dx2, "norm_ffn_bwd")
    g["w_out"] = _mm_tn(merged, dx1, "g_w_out", TN_TM, 512)
    dgate, dya, dyb, dyc, dua, dub, duc = _merge_bwd(dx1, z, ua, ub, uc, W["w_br_hgrn"], W["w_br_fox"], W["w_br_mem"],
                                                    W["w_out"])
    g["w_br_hgrn"] = _mm_tn(ya, dua, "g_w_br_hgrn", TN_TM, 512)
    g["w_br_fox"] = _mm_tn(yb, dub, "g_w_br_fox", TN_TM, 512)
    g["w_br_mem"] = _mm_tn(yc, duc, "g_w_br_mem", TN_TM, 512)
    early_pk = ex.early_grads(g) if ex else ()
    dz_hg, dlb_p, dgn_p, *early_sib = _hgrn_bwd(z, o_raw, states, a_mat, dya, lbl, sm["hgrn_norm_g"], B, T,
                                                swap_sibling=early_pk)
    dz_fox, dfc, dgq_p, dgk_p, *early_chips = _fox_bwd(z, dyb, yb, lse, fc, fct, gq2, gk2, B, T,
                                                       swap=ex.pair_sums(early_pk, early_sib, "early") if ex else ())
    dz_ff, dfb_p = _fox_gate_bwd(dfc, z, fbias, B, T)
    dz_mq, dkv, dmq_p, dmk_p = _mem_bwd(z, memkv, dyc, sm["mem_q_norm_g"], sm["mem_k_norm_g"], B, T, M)
    g["mem_kv_w"] = _mm_tn(memn, dkv, "g_mem_kv_w", 256, 512)
    dmemn = _mm_nt_sum([(dkv, 0, dkv.shape[1], 0)], W["mem_kv_w"], "d_memn", 256)
    _, dg_mem = _rmsnorm_bwd(dmemn, mem2, sm["norm_mem_g"], None, "norm_mem_bwd", tm=256)
    d = x2.shape[1]
    parts = [(dz_fox, 0, C_MQ - C_FOX, C_FOX), (dz_mq, 0, C_HG - C_MQ, C_MQ), (dz_hg, 0, C_GATE - C_HG, C_HG)]
    parts += [(dgate, d * k, d, C_GATE + d * k) for k in range(3)] + [(dz_ff, 0, C_END - C_FF, C_FF)]
    g["w_in"] = [_mm_tn(h, dzs, "g_w_in_%d" % i, 2 * TN_TM, min(512, dzs.shape[1]))
                 for i, dzs in enumerate((dz_fox, dz_mq, dz_hg, dgate, dz_ff))]
    sums = None
    if ex:
        last_pk = ex.last_grads(g)
        last_sib = _swap_with_sibling(last_pk, "rs_sibling_last")
        dh, last_chips = _mm_nt_sum(parts, W["w_in"], "dh", 256, swap=ex.pair_sums(last_pk, last_sib, "last"))
        sums = (ex.final_sums(early_pk, early_sib, early_chips, "early"),
                ex.final_sums(last_pk, last_sib, last_chips, "last"))
    else:
        dh = _mm_nt_sum(parts, W["w_in"], "dh", 256)
    grad_x, dg_mix = _rmsnorm_bwd(dh, x2, sm["norm_mix_g"], dx1, "norm_mix_bwd")
    small = _small_reduce(lbl, dg_mix, dg_mem, dlb_p, dgn_p, dfb_p, dgq_p, dgk_p, dmq_p, dmk_p, dg_ffn, dcb_p, loss_p)
    names = ("norm_mix_g", "norm_mem_g", "hgrn_lb_logits", "hgrn_norm_g", "fox_f_bias", "fox_q_norm_g", "fox_k_norm_g",
             "mem_q_norm_g", "mem_k_norm_g", "norm_ffn_g", "ffn_conv_b", "loss")
    g.update(dict(zip(names, small)))
    return grad_x, g, sums


ANY = pl.BlockSpec(memory_space=pl.ANY)


def _position():
    return lax.axis_index("x"), lax.axis_index("y"), lax.axis_index("c")


def _all_gather(blocks, name):
    nb = len(blocks)

    def body(*refs):
        start, forward, finish = _gather_phases(refs[:nb], refs[nb:2 * nb], *refs[2 * nb:])
        start()
        forward()
        finish()

    return pl.pallas_call(
        body, name=name, out_shape=_gather_shapes(blocks), in_specs=[ANY] * nb, out_specs=[ANY] * nb,
        scratch_shapes=_gather_sems(nb),
    )(*blocks)


def _hosting(body, n_in, n_out, n_scratch, n_x, make_phases, grid):
    n_steps = math.prod(grid)

    def hosted(*refs):
        a = n_in + n_x
        b = a + n_out + n_x
        ins, xs = refs[:n_in], refs[n_in:a]
        outs, x_outs = refs[a:a + n_out], refs[a + n_out:b]
        scratch, sems = refs[b:b + n_scratch], refs[b + n_scratch:]
        step = 0
        for ax, n in enumerate(grid):
            step = step * n + pl.program_id(ax)
        phases = make_phases(xs, x_outs, *sems)
        pl.when(step == 0)(phases[0])
        for ph in phases[1:-1]:
            pl.when(step == n_steps // 2)(ph)
        body(*ins, *outs, *scratch)
        pl.when(step == n_steps - 1)(phases[-1])

    return hosted


def _gather_shapes(blocks):
    return [S((N_DEV,) + b.shape, b.dtype) for b in blocks]


def _gather_sems(nb):
    return [pltpu.SemaphoreType.DMA((7 * nb,)), pltpu.SemaphoreType.DMA((7 * nb,)), pltpu.SemaphoreType.DMA((nb,))]


def _gather_phases(x_refs, out_refs, send_sems, recv_sems, local_sems):
    nb = len(x_refs)
    x, y, c = _position()
    me, sibling = (x, y, c), (x, y, 1 - c)
    chips = [(1 - x, y), (x, 1 - y), (1 - x, 1 - y)]

    def copy(i, k, blk, to, own=False):
        px, py, pc = blk
        slot = out_refs[i].at[4 * px + 2 * py + pc]
        return pltpu.make_async_remote_copy(
            src_ref=x_refs[i] if own else slot, dst_ref=slot, send_sem=send_sems.at[7 * i + k],
            recv_sem=recv_sems.at[7 * i + k], device_id=to, device_id_type=MESH)

    def mine(i):
        return pltpu.make_async_copy(x_refs[i], out_refs[i].at[4 * x + 2 * y + c], local_sems.at[i])

    def first(i):
        return [copy(i, 0, me, sibling, own=True)] + [copy(i, 1 + j, me, (*chip, c), own=True)
                                                     for j, chip in enumerate(chips)]

    def passed(i, j):
        return copy(i, 4 + j, (*chips[j], c), sibling)

    def start():
        for i in range(nb):
            mine(i).start()
            for cp in first(i):
                cp.start()

    def forward():
        for i in range(nb):
            for j, chip in enumerate(chips):
                copy(i, 1 + j, (*chip, c), me).wait_recv()
                passed(i, j).start()

    def finish():
        for i in range(nb):
            copy(i, 0, sibling, me).wait_recv()
            for j, chip in enumerate(chips):
                copy(i, 4 + j, (*chip, 1 - c), me).wait_recv()
        for i in range(nb):
            for cp in first(i) + [passed(i, j) for j in range(3)]:
                cp.wait_send()
            mine(i).wait()

    return start, forward, finish


def _swap_with_sibling(pks, name):
    nb = len(pks)

    def body(*refs):
        start, finish = _sibling_swap_phases(refs[:nb], refs[nb:2 * nb], *refs[2 * nb:])
        start()
        finish()

    return pl.pallas_call(
        body, name=name, out_shape=_sibling_swap_shapes(pks), in_specs=[ANY] * nb, out_specs=[ANY] * nb,
        scratch_shapes=_sibling_swap_sems(nb),
    )(*pks)


def _sibling_swap_shapes(pks):
    return [S((4,) + p.shape[1:], p.dtype) for p in pks]


def _sibling_swap_sems(nb):
    return [pltpu.SemaphoreType.DMA((4 * nb,)), pltpu.SemaphoreType.DMA((4 * nb,))]


def _sibling_swap_phases(pk_refs, out_refs, send_sems, recv_sems):
    nb = len(pk_refs)
    x, y, c = _position()

    def copies():
        return [pltpu.make_async_remote_copy(
            src_ref=pk_refs[i].at[2 * k + 1 - c], dst_ref=out_refs[i].at[k], send_sem=send_sems.at[4 * i + k],
            recv_sem=recv_sems.at[4 * i + k], device_id=(x, y, 1 - c), device_id_type=MESH)
            for i in range(nb) for k in range(4)]

    def start():
        for cp in copies():
            cp.start()

    def finish():
        for cp in copies():
            cp.wait()

    return start, finish


def _swap_between_chips(pbs, name):
    nb = len(pbs)

    def body(*refs):
        start, finish = _chip_swap_phases(refs[:nb], refs[nb:2 * nb], *refs[2 * nb:])
        start()
        finish()

    return pl.pallas_call(
        body, name=name, out_shape=[S(p.shape, p.dtype) for p in pbs], in_specs=[ANY] * nb, out_specs=[ANY] * nb,
        scratch_shapes=_chip_swap_sems(nb),
    )(*pbs)


def _chip_swap_sems(nb):
    return [pltpu.SemaphoreType.DMA((3 * nb,)), pltpu.SemaphoreType.DMA((3 * nb,)), pltpu.SemaphoreType.DMA((nb,))]


def _chip_swap_phases(pb_refs, out_refs, send_sems, recv_sems, local_sems):
    nb = len(pb_refs)
    x, y, c = _position()
    me = 2 * x + y
    chips = [(1 - x, y), (x, 1 - y), (1 - x, 1 - y)]

    def local(i):
        return pltpu.make_async_copy(pb_refs[i].at[me], out_refs[i].at[me], local_sems.at[i])

    def send(i, j):
        cx, cy = chips[j]
        return pltpu.make_async_remote_copy(
            src_ref=pb_refs[i].at[2 * cx + cy], dst_ref=out_refs[i].at[me], send_sem=send_sems.at[3 * i + j],
            recv_sem=recv_sems.at[3 * i + j], device_id=(cx, cy, c), device_id_type=MESH)

    def arrival(i, j):
        cx, cy = chips[j]
        return pltpu.make_async_remote_copy(
            src_ref=pb_refs[i].at[me], dst_ref=out_refs[i].at[2 * cx + cy], send_sem=send_sems.at[3 * i + j],
            recv_sem=recv_sems.at[3 * i + j], device_id=(cx, cy, c), device_id_type=MESH)

    def start():
        for i in range(nb):
            local(i).start()
            for j in range(3):
                send(i, j).start()

    def finish():
        for i in range(nb):
            for j in range(3):
                arrival(i, j).wait_recv()
        for i in range(nb):
            for j in range(3):
                send(i, j).wait_send()
            local(i).wait()

    return start, finish


def _row_tile(r):
    return max(t for t in range(16, min(r, 512) + 1, 16) if r % t == 0)


def _pair_sum_cast(pk, recv, core, name):
    _, r, l = pk.shape
    tr = _row_tile(r)

    def body(c_ref, a_ref, b_ref, o_ref):
        o_ref[...] = (a_ref[...] + b_ref[...]).astype(BF16)

    return pl.pallas_call(
        body, name=name,
        grid_spec=pltpu.PrefetchScalarGridSpec(
            num_scalar_prefetch=1, grid=(4, r // tr),
            in_specs=[pl.BlockSpec((None, tr, l), lambda k, i, c: (2 * k + c[0], i, 0)),
                      pl.BlockSpec((None, tr, l), lambda k, i, c: (k, i, 0))],
            out_specs=pl.BlockSpec((None, tr, l), lambda k, i, c: (k, i, 0))),
        out_shape=S((4, r, l), BF16), compiler_params=_cp(("parallel", "parallel")),
    )(core, pk, recv)


def _final_sum(pk, recv_sib, recv_chips, slot, chip, name):
    _, r, l = pk.shape
    tr = _row_tile(r)

    def body(s_ref, k_ref, a_ref, b_ref, rc_ref, o_ref):
        base = a_ref[...] + b_ref[...]
        acc = jnp.zeros_like(base)
        for j in range(4):
            acc = acc + jnp.where(k_ref[0] == j, base, rc_ref[j].astype(F32))
        o_ref[...] = acc

    return pl.pallas_call(
        body, name=name,
        grid_spec=pltpu.PrefetchScalarGridSpec(
            num_scalar_prefetch=2, grid=(r // tr,),
            in_specs=[pl.BlockSpec((None, tr, l), lambda i, s, k: (s[0], i, 0)),
                      pl.BlockSpec((None, tr, l), lambda i, s, k: (k[0], i, 0)),
                      pl.BlockSpec((4, tr, l), lambda i, s, k: (0, i, 0))],
            out_specs=pl.BlockSpec((tr, l), lambda i, s, k: (i, 0))),
        out_shape=S((r, l), F32), compiler_params=_cp(("parallel",)),
    )(slot, chip, pk, recv_sib, recv_chips)


def _adamw_math(w, g, m, v):
    m = ADAM_B1 * m + (1.0 - ADAM_B1) * g
    v = ADAM_B2 * v + (1.0 - ADAM_B2) * (g * g)
    m_hat = m / (1.0 - ADAM_B1 ** ADAM_STEP)
    v_hat = v / (1.0 - ADAM_B2 ** ADAM_STEP)
    return -ADAM_LR * (m_hat / (jnp.sqrt(v_hat) + ADAM_EPS) + ADAM_WD * w), m, v


def _adamw(w, g, m, v, name):
    r, c = w.shape
    tr = 256 if r % 256 == 0 else r

    def body(w_ref, g_ref, m_ref, v_ref, d_ref, nm_ref, nv_ref):
        d_ref[...], nm_ref[...], nv_ref[...] = _adamw_math(w_ref[...], g_ref[...], m_ref[...], v_ref[...])

    tile = pl.BlockSpec((tr, c), lambda i: (i, 0))
    return pl.pallas_call(
        body, name=name, grid=(r // tr,), in_specs=[tile] * 4, out_specs=[tile] * 3, out_shape=[S((r, c), F32)] * 3,
        compiler_params=_cp(("parallel",)),
    )(w, g, m, v)


def _small_update(gathered, w, m, v):
    def body(ga_ref, w_ref, m_ref, v_ref, g_ref, d_ref, nm_ref, nv_ref):
        g = ga_ref[0]
        for k in range(1, N_DEV):
            g = g + ga_ref[k]
        g_ref[...] = g
        d_ref[...], nm_ref[...], nv_ref[...] = _adamw_math(w_ref[...], g, m_ref[...], v_ref[...])

    return pl.pallas_call(body, name="small_update", out_shape=[S(w.shape, F32)] * 4, compiler_params=_cp())(
        gathered, w, m, v)


BIG = ("w_in", "mem_kv_w", "w_br_hgrn", "w_br_fox", "w_br_mem", "w_out", "ffn_w_up", "ffn_conv_w", "ffn_w_down")
GROUP_ROWS = ("w_out", "ffn_w_down")
GROUP_LANE = ("w_br_hgrn", "w_br_fox", "w_br_mem")
LANE_GROUP_ROWS = 224
SMALL = ("norm_mix_g", "norm_mem_g", "hgrn_lb_logits", "hgrn_norm_g", "fox_f_bias", "fox_q_norm_g", "fox_k_norm_g",
         "mem_q_norm_g", "mem_k_norm_g", "norm_ffn_g", "ffn_conv_b")


def _rows_of(n_elems):
    return -(-n_elems // LANE)


def _to_rows(a, lead=0):
    flat = a.reshape(a.shape[:lead] + (-1,))
    pad = (-flat.shape[-1]) % LANE
    if pad:
        flat = jnp.pad(flat, [(0, 0)] * lead + [(0, pad)])
    return flat.reshape(a.shape[:lead] + (-1, LANE))


def _stack_rows(parts, lead, total_rows):
    buf = jnp.concatenate(parts, axis=lead)
    pad = total_rows - buf.shape[lead]
    return jnp.pad(buf, [(0, 0)] * lead + [(0, pad), (0, 0)])


def _round_up(n, k):
    return -(-n // k) * k


def _from_rows(rows, shape, lead=0):
    n = math.prod(shape)
    return rows.reshape(rows.shape[:lead] + (-1,))[..., :n].reshape(rows.shape[:lead] + tuple(shape))


def _blocks_to_full(blocks, kind):
    n, a, b = blocks.shape
    return blocks.transpose(1, 0, 2).reshape(a, n * b) if kind == "col" else blocks.reshape(n * a, b)


def _full_to_blocks(full, kind, n=N_DEV):
    a, b = full.shape
    return full.reshape(a, n, b // n).transpose(1, 0, 2) if kind == "col" else full.reshape(n, a // n, b)


def _lane_group_rows(shard):
    n_lane = sum(shard[n].shape[0] for n in GROUP_LANE)
    n_cw = shard["ffn_conv_w"].size
    return n_lane, _rows_of(3 * n_cw), _rows_of(n_cw), _round_up(n_lane + _rows_of(3 * n_cw), LANE_GROUP_ROWS)


def _split_bf16x3(x):
    hi = x.astype(BF16)
    r1 = x - hi.astype(F32)
    mid = r1.astype(BF16)
    return jnp.stack([hi, mid, (r1 - mid.astype(F32)).astype(BF16)])


class _Exchange:
    def __init__(self, shard):
        self.shard = shard
        xi, yi, ci = _position()
        self.core = ci.astype(jnp.int32).reshape(1)
        self.chip = (2 * xi + yi).astype(jnp.int32).reshape(1)
        self.n_lane, self.r_pieces, self.r_vals, self.r_lane = _lane_group_rows(shard)

    def first_blocks(self):
        return [self.shard["w_in"].astype(BF16), self.shard["mem_kv_w"].astype(BF16)]

    def unpack_first(self, gathered):
        return {"w_in": _perm_from_blocks(gathered[0]), "mem_kv_w": _blocks_to_full(gathered[1], "row")}

    def late_blocks(self):
        sh = self.shard
        lane_rows = [sh[n].astype(BF16) for n in GROUP_LANE] + [_to_rows(_split_bf16x3(sh["ffn_conv_w"]))]
        return [sh[n].astype(BF16) for n in GROUP_ROWS] + [sh["ffn_w_up"].astype(BF16),
                                                           _stack_rows(lane_rows, 0, self.r_lane)]

    def unpack_late(self, gathered):
        *rows, gc, gd = gathered
        sh = self.shard
        W = {"ffn_w_up": _blocks_to_full(gc, "col")}
        for n, blocks in zip(GROUP_ROWS, rows):
            W[n] = _blocks_to_full(blocks, "row")
        r0 = 0
        for n in GROUP_LANE:
            W[n] = _blocks_to_full(gd[:, r0:r0 + sh[n].shape[0]], "col")
            r0 += sh[n].shape[0]
        cw = _from_rows(gd[:, self.n_lane:self.n_lane + self.r_pieces], (3,) + sh["ffn_conv_w"].shape, lead=1).astype(F32)
        W["ffn_conv_w"] = _blocks_to_full(cw[:, 0] + cw[:, 1] + cw[:, 2], "col")
        return W

    def early_grads(self, g):
        cw_rows = _to_rows(_full_to_blocks(g["ffn_conv_w"], "col"), lead=1)
        return [_full_to_blocks(g[n], "row") for n in GROUP_ROWS] + [
            jnp.concatenate([_full_to_blocks(h, "col", N_DEV // 2) for h in g["ffn_w_up"]], axis=0),
            _stack_rows([_full_to_blocks(g[n], "col") for n in GROUP_LANE] + [cw_rows], 1, self.r_lane)]

    def last_grads(self, g):
        return [_unperm_blocks(g["w_in"], N_DEV), _full_to_blocks(g["mem_kv_w"], "row")]

    def pair_sums(self, pks, recv_sib, tag):
        return [_pair_sum_cast(p, r, self.core, "rs_pair_sum_%s%d" % (tag, i))
                for i, (p, r) in enumerate(zip(pks, recv_sib))]

    def final_sums(self, pks, recv_sib, recv_chips, tag):
        return [_final_sum(p, rs, rc, 2 * self.chip + self.core, self.chip, "rs_final_sum_%s%d" % (tag, i))
                for i, (p, rs, rc) in enumerate(zip(pks, recv_sib, recv_chips))]

    def unpack_grads(self, early, last):
        sh = self.shard
        *rows, g_up, g_lane = early
        g_shard = {"w_in": last[0], "mem_kv_w": last[1], "ffn_w_up": g_up, **dict(zip(GROUP_ROWS, rows))}
        r0 = 0
        for n in GROUP_LANE:
            g_shard[n] = g_lane[r0:r0 + sh[n].shape[0]]
            r0 += sh[n].shape[0]
        g_shard["ffn_conv_w"] = _from_rows(g_lane[self.n_lane:self.n_lane + self.r_vals], sh["ffn_conv_w"].shape)
        return g_shard


def kernel(x, mem, norm_mix_g, norm_mem_g, w_in, hgrn_lb_logits, hgrn_norm_g, fox_f_bias, fox_q_norm_g, fox_k_norm_g, mem_kv_w, mem_q_norm_g, mem_k_norm_g, w_br_hgrn, w_br_fox, w_br_mem, w_out, norm_ffn_g, ffn_w_up, ffn_conv_w, ffn_conv_b, ffn_w_down, loss_target, m_norm_mix_g, m_norm_mem_g, m_w_in, m_hgrn_lb_logits, m_hgrn_norm_g, m_fox_f_bias, m_fox_q_norm_g, m_fox_k_norm_g, m_mem_kv_w, m_mem_q_norm_g, m_mem_k_norm_g, m_w_br_hgrn, m_w_br_fox, m_w_br_mem, m_w_out, m_norm_ffn_g, m_ffn_w_up, m_ffn_conv_w, m_ffn_conv_b, m_ffn_w_down, v_norm_mix_g, v_norm_mem_g, v_w_in, v_hgrn_lb_logits, v_hgrn_norm_g, v_fox_f_bias, v_fox_q_norm_g, v_fox_k_norm_g, v_mem_kv_w, v_mem_q_norm_g, v_mem_k_norm_g, v_w_br_hgrn, v_w_br_fox, v_w_br_mem, v_w_out, v_norm_ffn_g, v_ffn_w_up, v_ffn_conv_w, v_ffn_conv_b, v_ffn_w_down):
    given = dict(locals())
    order = ("norm_mix_g", "norm_mem_g", "w_in", "hgrn_lb_logits", "hgrn_norm_g", "fox_f_bias", "fox_q_norm_g",
             "fox_k_norm_g", "mem_kv_w", "mem_q_norm_g", "mem_k_norm_g", "w_br_hgrn", "w_br_fox", "w_br_mem", "w_out",
             "norm_ffn_g", "ffn_w_up", "ffn_conv_w", "ffn_conv_b", "ffn_w_down")
    B, T, D = x.shape
    M = mem.shape[1]
    shard = {n: given[n][0] if n in BIG else given[n] for n in order}
    mom = {n: (given["m_" + n][0], given["v_" + n][0]) if n in BIG else (given["m_" + n], given["v_" + n])
           for n in order}
    shard["hgrn_lb_logits"] = hgrn_lb_logits
    for n in ("norm_mix_g", "norm_mem_g", "hgrn_norm_g", "fox_f_bias", "fox_q_norm_g", "fox_k_norm_g", "mem_q_norm_g",
              "mem_k_norm_g", "norm_ffn_g", "ffn_conv_b"):
        shard[n] = given[n].reshape(1, -1)

    ex = _Exchange(shard)
    W = ex.unpack_first(_all_gather(ex.first_blocks(), "ag_first"))

    sm = {n: shard[n] for n in SMALL}
    grad_x, g, sums = _local_step(x.reshape(B * T, D), mem.reshape(B * M, D), loss_target.reshape(B * T, D), sm, W,
                                  B, T, M, ex)
    g_shard = ex.unpack_grads(*sums)

    sg = {n: g[n] for n in SMALL}
    sg["fox_f_bias"] = g["fox_f_bias"][:, :FOX_H]
    sg["fox_q_norm_g"] = g["fox_q_norm_g"][:, :FOX_D]
    sg["fox_k_norm_g"] = g["fox_k_norm_g"][:, :FOX_D]
    slayout, row0 = {}, 0
    for n in SMALL:
        nr = _rows_of(shard[n].size)
        slayout[n] = (row0, nr)
        row0 += nr
    loss_row = row0
    r_small = _round_up(row0 + 1, 8)

    def pack_small(d, with_loss=None):
        rows = [_to_rows(d[n]) for n in SMALL]
        rows.append(with_loss if with_loss is not None else jnp.zeros((1, LANE), F32))
        return _stack_rows(rows, 0, r_small)

    sgath, = _all_gather([pack_small(sg, g["loss"])], "ag_small")
    s_g, s_d, s_m, s_v = _small_update(sgath, pack_small(shard), pack_small({n: mom[n][0].reshape(shard[n].shape) for n in SMALL}),
                                       pack_small({n: mom[n][1].reshape(shard[n].shape) for n in SMALL}))
    loss = s_g[loss_row, 0]

    grads, deltas, new_m, new_v = {}, {}, {}, {}
    for n in BIG:
        gn = g_shard[n]
        d, nm, nv = _adamw(shard[n], gn, mom[n][0], mom[n][1], "adamw_" + n)
        grads[n], deltas[n], new_m[n], new_v[n] = (a[None] for a in (gn, d, nm, nv))
    for n in SMALL:
        r0, nr = slayout[n]
        for dst, src in ((grads, s_g), (deltas, s_d), (new_m, s_m), (new_v, s_v)):
            dst[n] = _from_rows(src[r0:r0 + nr], given[n].shape)
    return (loss, grad_x.reshape(B, T, D), *[grads[n] for n in order], *[deltas[n] for n in order],
            *[new_m[n] for n in order], *[new_v[n] for n in order])
```

```python
import functools
import math

import jax
import jax.numpy as jnp
from jax import lax
from jax.experimental import pallas as pl
from jax.experimental.pallas import tpu as pltpu

F32, BF16 = jnp.float32, jnp.bfloat16
S = jax.ShapeDtypeStruct
MESH = pl.DeviceIdType.MESH

N_DEV = 8
EPS = 1e-6
LANE = 128
CHUNK = 64
SUB = 16
HG_H, HG_D = 4, 128
HG_GROUP_FWD = 4
HG_GROUP = 2
FOX_H, FOX_D = 8, 64
FOX_P = FOX_H // 2
MEM_H, MEM_D = 4, 128
NEG = -1e30
VMEM_LIMIT = 56 * 2**20

ADAM_LR, ADAM_B1, ADAM_B2, ADAM_EPS, ADAM_WD, ADAM_STEP = 0.001, 0.9, 0.999, 1e-08, 0.01, 10

C_FOX, C_MQ, C_HG, C_GATE, C_FF, C_END = 0, 1536, 2048, 4096, 7168, 7296


def _cp(sem=None):
    return pltpu.CompilerParams(dimension_semantics=sem, vmem_limit_bytes=VMEM_LIMIT)


def _dot(a, b, dims, prec=None):
    return lax.dot_general(a, b, (dims, ((), ())), preferred_element_type=F32, precision=prec)


def _nn(a, b, prec=None):
    return _dot(a, b, ((1,), (0,)), prec)


def _nt(a, b, prec=None):
    return _dot(a, b, ((1,), (1,)), prec)


def _tn(a, b, prec=None):
    return _dot(a, b, ((0,), (0,)), prec)


def _b(x):
    return x.astype(BF16)


def _mm3(fn, a, b):
    ah, bh = _b(a), _b(b)
    return fn(ah, bh) + fn(ah, _b(b - bh.astype(F32))) + fn(_b(a - ah.astype(F32)), bh)


def _iota(shape, dim):
    return lax.broadcasted_iota(jnp.int32, shape, dim)


def _rowsum8(x):
    r, d = x.shape
    return jnp.sum(x.reshape(r // 8, 8, d), axis=0)


def _rmsnorm_cast(x, g, name, tm=1024):
    n, d = x.shape

    def body(x_ref, g_ref, o_ref):
        v = x_ref[...]
        r = lax.rsqrt(jnp.mean(v * v, axis=-1, keepdims=True) + EPS)
        o_ref[...] = (v * r * g_ref[...]).astype(BF16)

    return pl.pallas_call(
        body, name=name, grid=(n // tm,),
        in_specs=[pl.BlockSpec((tm, d), lambda i: (i, 0)), pl.BlockSpec((1, d), lambda i: (0, 0))],
        out_specs=pl.BlockSpec((tm, d), lambda i: (i, 0)), out_shape=S((n, d), BF16), compiler_params=_cp(("parallel",)),
    )(x, g)


def _rmsnorm_bwd(dh, x, g, resid, name, tm=1024):
    n, d = x.shape
    has_res = resid is not None

    def body(*refs):
        if has_res:
            dh_ref, x_ref, g_ref, r_ref, dx_ref, dg_ref = refs
        else:
            dh_ref, x_ref, g_ref, dx_ref, dg_ref = refs
        v = x_ref[...]
        dhv = dh_ref[...].astype(F32)
        r = lax.rsqrt(jnp.mean(v * v, axis=-1, keepdims=True) + EPS)
        xh = v * r
        u = dhv * g_ref[...]
        dx = r * (u - xh * jnp.mean(u * xh, axis=-1, keepdims=True))
        if has_res:
            dx = dx + r_ref[...]
        dx_ref[...] = dx

        @pl.when(pl.program_id(0) == 0)
        def _():
            dg_ref[...] = jnp.zeros_like(dg_ref)

        dg_ref[...] += _rowsum8(dhv * xh)

    tile = pl.BlockSpec((tm, d), lambda i: (i, 0))
    ins = [tile, tile, pl.BlockSpec((1, d), lambda i: (0, 0))] + ([tile] if has_res else [])
    args = (dh, x, g) + ((resid,) if has_res else ())
    return pl.pallas_call(
        body, name=name, grid=(n // tm,), in_specs=ins,
        out_specs=[tile, pl.BlockSpec((8, d), lambda i: (0, 0))],
        out_shape=[S((n, d), F32), S((8, d), F32)], compiler_params=_cp(("arbitrary",)),
    )(*args)


def _mm_nn(a, b, out_dtype, name, tm, tn):
    m, k = a.shape
    n = b.shape[1]
    assert n % tn == 0 and m % tm == 0

    def body(a_ref, b_ref, o_ref):
        o_ref[...] = _nn(a_ref[...].astype(BF16), b_ref[...].astype(BF16)).astype(out_dtype)

    return pl.pallas_call(
        body, name=name, grid=(n // tn, m // tm),
        in_specs=[pl.BlockSpec((tm, k), lambda j, i: (i, 0)), pl.BlockSpec((k, tn), lambda j, i: (0, j))],
        out_specs=pl.BlockSpec((tm, tn), lambda j, i: (i, j)), out_shape=S((m, n), out_dtype),
        compiler_params=_cp(("parallel", "parallel")),
    )(a, b)


def _mm_nt_sum(parts, w, name, tm, swap=()):
    m = parts[0][0].shape[0]
    k = w.shape[0]
    assert m % tm == 0 and all(c % n == 0 and o % n == 0 for _, c, n, o in parts)
    np_ = len(parts)
    nsw = len(swap)
    n_steps = m // tm

    def body(*refs):
        o_ref = refs[2 * np_ + nsw]
        if nsw:
            start, finish = _chip_swap_phases(refs[2 * np_:2 * np_ + nsw], refs[2 * np_ + nsw + 1:2 * np_ + 2 * nsw + 1],
                                              *refs[2 * np_ + 2 * nsw + 1:])
            pl.when(pl.program_id(0) == 0)(start)
        acc = _nt(refs[0][...].astype(BF16), refs[np_][...].astype(BF16))
        for i in range(1, np_):
            acc = acc + _nt(refs[i][...].astype(BF16), refs[np_ + i][...].astype(BF16))
        o_ref[...] = acc
        if nsw:
            pl.when(pl.program_id(0) == n_steps - 1)(finish)

    dy_specs = [pl.BlockSpec((tm, n), functools.partial(lambda i, j: (i, j), j=c // n)) for _, c, n, _ in parts]
    w_specs = [pl.BlockSpec((k, n), functools.partial(lambda i, j: (0, j), j=o // n)) for _, _, n, o in parts]
    out = pl.pallas_call(
        body, name=name, grid=(n_steps,), in_specs=dy_specs + w_specs + [ANY] * nsw,
        out_specs=[pl.BlockSpec((tm, k), lambda i: (i, 0))] + [ANY] * nsw,
        out_shape=[S((m, k), F32)] + [S(p.shape, p.dtype) for p in swap],
        scratch_shapes=_chip_swap_sems(nsw) if nsw else [],
        compiler_params=_cp(("arbitrary",) if nsw else ("parallel",)),
    )(*([p[0] for p in parts] + [w] * np_ + list(swap)))
    return (out[0], out[1:]) if nsw else out[0]


def _mm_tn(x, dy, name, tm, tn):
    m, k = x.shape
    n = dy.shape[1]
    tm = min(tm, m)
    assert m % tm == 0 and n % tn == 0

    def body(x_ref, dy_ref, o_ref):
        part = _tn(x_ref[...].astype(BF16), dy_ref[...].astype(BF16))

        @pl.when(pl.program_id(1) == 0)
        def _():
            o_ref[...] = part

        @pl.when(pl.program_id(1) > 0)
        def _():
            o_ref[...] += part

    return pl.pallas_call(
        body, name=name, grid=(n // tn, m // tm),
        in_specs=[pl.BlockSpec((tm, k), lambda j, i: (i, 0)), pl.BlockSpec((tm, tn), lambda j, i: (i, j))],
        out_specs=pl.BlockSpec((k, tn), lambda j, i: (0, j)), out_shape=S((k, n), F32),
        compiler_params=_cp(("parallel", "arbitrary")),
    )(x, dy)


def _lower_bound(logits):
    e = jnp.exp(logits - jnp.max(logits, axis=0, keepdims=True))
    return e[0:1, :] / jnp.sum(e, axis=0, keepdims=True)


def _hg_gates(fl, lb):
    sig = jax.nn.sigmoid(fl)
    f = lb + (1.0 - lb) * sig
    k = (1.0 - lb) * (1.0 - sig)
    return sig, f, k, jnp.log(f)


def _silu_and_grad(x):
    s = jax.nn.sigmoid(x)
    return x * s, s * (1.0 + x * (1.0 - s))


def _hg_rowblocks(G):
    return [None] + [G[SUB * i - 1:SUB * i, :] for i in range(1, CHUNK // SUB)]


def _hg_intra_A(qs, k, G):
    refs = _hg_rowblocks(G)
    cols = _iota((SUB, LANE), 1)
    rows = _iota((SUB, LANE), 0)
    no_keys = jnp.zeros((LANE - CHUNK, HG_D), BF16)
    blocks = []
    for i in range(CHUNK // SUB):
        lo = SUB * i
        qb, Gb = qs[lo:lo + SUB, :], G[lo:lo + SUB, :]
        diag = jnp.zeros((SUB, LANE), F32)
        for s in range(SUB):
            e = jnp.exp(jnp.minimum(Gb - G[lo + s:lo + s + 1, :], 0.0))
            col = jnp.sum(qb * k[lo + s:lo + s + 1, :] * e, axis=-1, keepdims=True)
            diag = jnp.where(cols == lo + s, col, diag)
        a = jnp.where((cols >= lo) & (cols <= rows + lo), diag, 0.0)
        if i > 0:
            qr = qb * jnp.exp(Gb - refs[i])
            kr = k * jnp.exp(jnp.minimum(refs[i] - G, 0.0))
            a = jnp.where(cols < lo, _nt(_b(qr), jnp.concatenate([_b(kr), no_keys], axis=0)), a)
        blocks.append(a)
    return jnp.concatenate(blocks, axis=0)


def _hg_intra_bwd(dA, qs, k, G):
    refs = _hg_rowblocks(G)
    cols = _iota((SUB, CHUNK), 1)
    rows16 = _iota((SUB, HG_D), 0)
    dk = jnp.zeros((CHUNK, HG_D), F32)
    dq_blocks, dk_diag_blocks = [], []
    for i in range(CHUNK // SUB):
        lo = SUB * i
        qb, Gb = qs[lo:lo + SUB, :], G[lo:lo + SUB, :]
        dAb = dA[lo:lo + SUB, :]
        dq = jnp.zeros((SUB, HG_D), F32)
        dkb = jnp.zeros((SUB, HG_D), F32)
        for s in range(SUB):
            e = jnp.exp(jnp.minimum(Gb - G[lo + s:lo + s + 1, :], 0.0))
            e = jnp.where(rows16 >= s, e, 0.0)
            dcol = jnp.sum(jnp.where(cols == lo + s, dAb, 0.0), axis=-1, keepdims=True)
            w = dcol * e
            dq = dq + w * k[lo + s:lo + s + 1, :]
            dkb = jnp.where(rows16 == s, jnp.sum(w * qb, axis=0, keepdims=True), dkb)
        if i > 0:
            e1 = jnp.exp(Gb - refs[i])
            e2 = jnp.exp(jnp.minimum(refs[i] - G, 0.0))
            dA_off = jnp.where(cols < lo, dAb, 0.0)
            dq = dq + _mm3(_nn, dA_off, k * e2) * e1
            dk = dk + _mm3(_tn, dA_off, qb * e1) * e2
        dq_blocks.append(dq)
        dk_diag_blocks.append(dkb)
    return jnp.concatenate(dq_blocks, axis=0), dk + jnp.concatenate(dk_diag_blocks, axis=0)


def _tri(n, upper=False):
    r, c = _iota((n, n), 0), _iota((n, n), 1)
    return jnp.where((c >= r) if upper else (r >= c), 1.0, 0.0).astype(BF16)


def _prefix_mm(tri, x):
    hi = x.astype(BF16)
    r1 = x - hi.astype(F32)
    mid = r1.astype(BF16)
    lo = (r1 - mid.astype(F32)).astype(BF16)
    return _nn(tri, hi) + _nn(tri, mid) + _nn(tri, lo)


def _hgrn_fwd(z, lb, gn, B, T):
    N = B * T
    NC = T // CHUNK
    ng = HG_H // HG_GROUP_FWD

    def body(z_ref, lb_ref, gn_ref, y_ref, o_ref, st_ref, a_ref, s_scr):
        lbs = _lower_bound(lb_ref[...])
        tri = _tri(CHUNK)
        s_scr[...] = jnp.zeros_like(s_scr)

        def chunk(c, carry):
            r = pl.ds(pl.multiple_of(c * CHUNK, CHUNK), CHUNK)
            for hh in range(HG_GROUP_FWD):
                zc, oc = 4 * LANE * hh, LANE * hh
                ql, fl, il, gl = (z_ref[r, zc + LANE * j:zc + LANE * (j + 1)].astype(F32) for j in range(4))
                _, _, k, logf = _hg_gates(fl, lbs[:, oc:oc + LANE])
                G = _prefix_mm(tri, logf)
                qs = ql * jax.nn.sigmoid(ql)
                st = s_scr[hh]
                st_ref[hh * NC + c] = st
                g_last = G[CHUNK - 1:CHUNK, :]
                A = _b(_hg_intra_A(qs, k, G))
                a_ref[r, oc:oc + LANE] = A
                o = _nn(A[:, 0:CHUNK], _b(il)) + _nt(_b(qs * jnp.exp(G)), _b(st))
                s_scr[hh] = st * jnp.exp(g_last) + _mm3(_tn, il, k * jnp.exp(g_last - G))
                o_ref[r, oc:oc + LANE] = o
                rstd = lax.rsqrt(jnp.mean(o * o, axis=-1, keepdims=True) + EPS)
                y_ref[r, oc:oc + LANE] = (o * rstd * gn_ref[...] * (gl * jax.nn.sigmoid(gl))).astype(BF16)
            return carry

        lax.fori_loop(0, NC, chunk, 0)

    gw = HG_GROUP_FWD * LANE
    cb = C_HG // (4 * gw)
    return pl.pallas_call(
        body, name="hgrn_fwd", grid=(B, ng),
        in_specs=[pl.BlockSpec((T, 4 * gw), lambda b, h: (b, cb + h)), pl.BlockSpec((lb.shape[0], gw), lambda b, h: (0, h)),
                  pl.BlockSpec((1, LANE), lambda b, h: (0, 0))],
        out_specs=[pl.BlockSpec((T, gw), lambda b, h: (b, h)), pl.BlockSpec((T, gw), lambda b, h: (b, h)),
                   pl.BlockSpec((HG_GROUP_FWD * NC, HG_D, HG_D), lambda b, h: (b * ng + h, 0, 0)),
                   pl.BlockSpec((T, gw), lambda b, h: (b, h))],
        out_shape=[S((N, 512), BF16), S((N, 512), F32), S((B * HG_H * NC, HG_D, HG_D), F32), S((N, 512), BF16)],
        scratch_shapes=[pltpu.VMEM((HG_GROUP_FWD, HG_D, HG_D), F32)], compiler_params=_cp(("parallel", "parallel")),
    )(z, lb, gn)


def _hgrn_bwd(z, o_raw, states, a_mat, dy, lb, gn, B, T, swap_sibling=()):
    N = B * T
    NC = T // CHUNK
    ng = HG_H // HG_GROUP
    nsw = len(swap_sibling)

    def body(z_ref, o_ref, st_ref, a_ref, dy_ref, lb_ref, gn_ref, dz_ref, dlb_ref, dgn_ref, ds_scr, racc, dgn_acc):
        lbs = _lower_bound(lb_ref[...])
        gn_v = gn_ref[...]
        tri, triu = _tri(CHUNK), _tri(CHUNK, upper=True)
        cmask = _iota((CHUNK, CHUNK), 0) >= _iota((CHUNK, CHUNK), 1)
        for ref in (ds_scr, racc, dgn_acc, dlb_ref):
            ref[...] = jnp.zeros_like(ref)

        def chunk(ci, carry):
            c = NC - 1 - ci
            r = pl.ds(pl.multiple_of(c * CHUNK, CHUNK), CHUNK)
            for hh in range(HG_GROUP):
                zc, oc = 4 * LANE * hh, LANE * hh
                lb_v = lbs[:, oc:oc + LANE]
                ql, fl, il, gl = (z_ref[r, zc + LANE * j:zc + LANE * (j + 1)].astype(F32) for j in range(4))
                sig, f, k, logf = _hg_gates(fl, lb_v)
                G = _prefix_mm(tri, logf)
                qs, dsilu_q = _silu_and_grad(ql)
                gs, dsilu_g = _silu_and_grad(gl)
                o = o_ref[r, oc:oc + LANE]
                dyv = dy_ref[r, oc:oc + LANE]
                rstd = lax.rsqrt(jnp.mean(o * o, axis=-1, keepdims=True) + EPS)
                oh = o * rstd
                dgl = dyv * oh * gn_v * dsilu_g
                dn = dyv * gs
                dgn_acc[...] += _rowsum8(dn * oh)
                u = dn * gn_v
                do = rstd * (u - oh * jnp.mean(u * oh, axis=-1, keepdims=True))
                st = st_ref[hh * NC + c]
                dst = ds_scr[hh]
                eG = jnp.exp(G)
                g_last = G[CHUNK - 1:CHUNK, :]
                eL = jnp.exp(g_last - G)
                dA = jnp.where(cmask, _mm3(_nt, do, il), 0.0)
                dq_in, dk_in = _hg_intra_bwd(dA, qs, k, G)
                di = _tn(a_ref[r, oc:oc + LANE][:, 0:CHUNK], _b(do)) + _nt(_b(k * eL), _b(dst))
                dq = dq_in + _mm3(_nn, do, st) * eG
                dk = dk_in + _mm3(_nn, il, dst) * eL
                ds_scr[hh] = dst * jnp.exp(g_last) + _mm3(_tn, do, qs * eG)
                dd = qs * dq - k * dk
                dlogf = _prefix_mm(triu, dd) + racc[hh]
                racc[hh] += jnp.sum(dd, axis=0, keepdims=True)
                df = dlogf / f - dk
                dlb_ref[8 * hh:8 * (hh + 1), :] += _rowsum8(df * (1.0 - sig))
                dz_ref[r, zc:zc + LANE] = (dq * dsilu_q).astype(BF16)
                dz_ref[r, zc + LANE:zc + 2 * LANE] = (df * (1.0 - lb_v) * sig * (1.0 - sig)).astype(BF16)
                dz_ref[r, zc + 2 * LANE:zc + 3 * LANE] = di.astype(BF16)
                dz_ref[r, zc + 3 * LANE:zc + 4 * LANE] = dgl.astype(BF16)
            return carry

        lax.fori_loop(0, NC, chunk, 0)
        dgn_ref[...] = dgn_acc[...]

    gw = HG_GROUP * LANE
    cb = C_HG // (4 * gw)
    col = pl.BlockSpec((T, gw), lambda b, h: (b, h))
    if nsw:
        body = _hosting(body, 7, 3, 3, nsw, _sibling_swap_phases, (B, ng))
    return pl.pallas_call(
        body, name="hgrn_bwd", grid=(B, ng),
        in_specs=[pl.BlockSpec((T, 4 * gw), lambda b, h: (b, cb + h)), col,
                  pl.BlockSpec((HG_GROUP * NC, HG_D, HG_D), lambda b, h: (b * ng + h, 0, 0)), col, col,
                  pl.BlockSpec((lb.shape[0], gw), lambda b, h: (0, h)), pl.BlockSpec((1, LANE), lambda b, h: (0, 0))]
        + [ANY] * nsw,
        out_specs=[pl.BlockSpec((T, 4 * gw), lambda b, h: (b, h)),
                   pl.BlockSpec((8 * HG_GROUP, LANE), lambda b, h: (b * ng + h, 0)),
                   pl.BlockSpec((8, LANE), lambda b, h: (b * ng + h, 0))] + [ANY] * nsw,
        out_shape=[S((N, 2048), BF16), S((B * HG_H * 8, LANE), F32), S((B * ng * 8, LANE), F32)]
        + _sibling_swap_shapes(swap_sibling),
        scratch_shapes=[pltpu.VMEM((HG_GROUP, HG_D, HG_D), F32), pltpu.VMEM((HG_GROUP, 1, LANE), F32),
                        pltpu.VMEM((8, LANE), F32)] + (_sibling_swap_sems(nsw) if nsw else []),
        compiler_params=_cp(("arbitrary", "arbitrary") if nsw else ("parallel", "parallel")),
    )(z, o_raw, states, a_mat, dy, lb, gn, *swap_sibling)


def _pair_mean(x, lo_half):
    a = jnp.sum(jnp.where(lo_half, x, 0.0), axis=-1, keepdims=True)
    b = jnp.sum(jnp.where(lo_half, 0.0, x), axis=-1, keepdims=True)
    return jnp.where(lo_half, a, b) * (1.0 / FOX_D)


def _fox_gate_fwd(z, bias, B, T):
    N = B * T
    tb = LANE

    def body(z_ref, b_ref, fc_ref, fct_ref):
        tri = _tri(tb)

        def step(i, carry):
            r = pl.ds(pl.multiple_of(i * tb, tb), tb)
            cs = _prefix_mm(tri, jax.nn.log_sigmoid(z_ref[r, :].astype(F32) + b_ref[...])) + carry
            fc_ref[r, :] = cs
            fct_ref[0, :, r] = cs.T[0:8, :]
            return cs[tb - 1:tb, :]

        lax.fori_loop(0, T // tb, step, jnp.zeros((1, LANE), F32))

    return pl.pallas_call(
        body, name="fox_gate_fwd", grid=(B,),
        in_specs=[pl.BlockSpec((T, LANE), lambda b: (b, C_FF // LANE)), pl.BlockSpec((1, LANE), lambda b: (0, 0))],
        out_specs=[pl.BlockSpec((T, LANE), lambda b: (b, 0)), pl.BlockSpec((1, 8, T), lambda b: (b, 0, 0))],
        out_shape=[S((N, LANE), F32), S((B, 8, T), F32)], compiler_params=_cp(("parallel",)),
    )(z, bias)


def _fox_gate_bwd(dfc, z, bias, B, T):
    N = B * T
    tb = LANE
    nt = T // tb

    def body(d_ref, z_ref, b_ref, dz_ref, db_ref):
        triu = _tri(tb, upper=True)
        db_ref[...] = jnp.zeros_like(db_ref)

        def step(ii, carry):
            r = pl.ds(pl.multiple_of((nt - 1 - ii) * tb, tb), tb)
            d = d_ref[r, 0:LANE]
            for p in range(1, FOX_P):
                d = d + d_ref[r, LANE * p:LANE * (p + 1)]
            rc = _prefix_mm(triu, d) + carry
            dff = rc * jax.nn.sigmoid(-(z_ref[r, :].astype(F32) + b_ref[...]))
            dz_ref[r, :] = dff.astype(BF16)
            db_ref[...] += _rowsum8(dff)
            return carry + jnp.sum(d, axis=0, keepdims=True)

        lax.fori_loop(0, nt, step, jnp.zeros((1, LANE), F32))

    return pl.pallas_call(
        body, name="fox_gate_bwd", grid=(B,),
        in_specs=[pl.BlockSpec((T, 512), lambda b: (b, 0)), pl.BlockSpec((T, LANE), lambda b: (b, C_FF // LANE)),
                  pl.BlockSpec((1, LANE), lambda b: (0, 0))],
        out_specs=[pl.BlockSpec((T, LANE), lambda b: (b, 0)), pl.BlockSpec((8, LANE), lambda b: (b, 0))],
        out_shape=[S((N, LANE), BF16), S((B * 8, LANE), F32)], compiler_params=_cp(("parallel",)),
    )(dfc, z, bias)


def _fox_prep(z_ref, gq, gk, r, lo_half):
    q, k, v = (z_ref[r, LANE * j:LANE * (j + 1)].astype(F32) for j in range(3))
    rq = lax.rsqrt(_pair_mean(q * q, lo_half) + EPS)
    rk = lax.rsqrt(_pair_mean(k * k, lo_half) + EPS)
    qh, kh = q * rq, k * rk
    return qh * gq * (FOX_D ** -0.5), kh * gk, v, qh, kh, rq, rk


def _fox_fwd(z, fc, fct, gq, gk, B, T, tq=512, gather=()):
    N = B * T
    NQ = T // tq
    nga = len(gather)

    def body(z_ref, fc_ref, fct_ref, gq_ref, gk_ref, y_ref, lse_ref, qn_s, kn_s, v_s):
        p, qi = pl.program_id(1), pl.program_id(2)
        lo_half = _iota((1, LANE), 1) < FOX_D

        @pl.when(qi == 0)
        def _():
            def prep(i, carry):
                r = pl.ds(pl.multiple_of(i * tq, tq), tq)
                qn, kn, v = _fox_prep(z_ref, gq_ref[...], gk_ref[...], r, lo_half)[:3]
                qn_s[r, :], kn_s[r, :], v_s[r, :] = qn.astype(BF16), kn.astype(BF16), v.astype(BF16)
                return carry
            lax.fori_loop(0, NQ, prep, 0)

        rq = pl.ds(pl.multiple_of(qi * tq, tq), tq)
        qn = qn_s[rq, :]
        fcq = fc_ref[rq, :]
        lane = _iota((tq, LANE), 1)
        causal = _iota((tq, tq), 0) >= _iota((tq, tq), 1)
        qhs = [jnp.where(lo_half, qn, jnp.zeros_like(qn)), jnp.where(lo_half, jnp.zeros_like(qn), qn)]
        fqs = [jnp.sum(jnp.where(lane == 2 * p + hh, fcq, 0.0), axis=-1, keepdims=True) for hh in range(2)]

        def kv(j, carry, diagonal):
            rk = pl.ds(pl.multiple_of(j * tq, tq), tq)
            kj, vj = kn_s[rk, :], v_s[rk, :]
            one = jnp.ones_like(vj)
            new = []
            for hh in range(2):
                m, acc = carry[hh]
                s = _nt(qhs[hh], kj) + fqs[hh] - fct_ref[0, pl.ds(2 * p + hh, 1), rk]
                if diagonal:
                    s = jnp.where(causal, s, NEG)
                m_new = jnp.maximum(m, jnp.max(s, axis=-1, keepdims=True))
                pe = jnp.exp(s - m_new)
                v_aug = jnp.where(lo_half if hh == 0 else jnp.logical_not(lo_half), vj, one)
                new.append((m_new, jnp.exp(m - m_new) * acc + _nn(pe.astype(BF16), v_aug)))
            return tuple(new)

        init = tuple((jnp.full((tq, 1), NEG, F32), jnp.zeros((tq, LANE), F32)) for _ in range(2))
        carry = lax.fori_loop(0, qi, functools.partial(kv, diagonal=False), init)
        (m0, a0), (m1, a1) = kv(qi, carry, True)
        l0, l1 = a0[:, FOX_D:FOX_D + 1], a1[:, 0:1]
        y_ref[...] = jnp.where(lo_half, a0 / l0, a1 / l1).astype(BF16)
        lse_ref[...] = jnp.where(lo_half, m0 + jnp.log(l0), m1 + jnp.log(l1))

    vec = pl.BlockSpec((1, LANE), lambda b, p, q: (0, 0))
    tile = pl.BlockSpec((tq, LANE), lambda b, p, q: (b * NQ + q, p))
    if nga:
        body = _hosting(body, 5, 2, 3, nga, _gather_phases, (B, FOX_P, NQ))
    return pl.pallas_call(
        body, name="fox_fwd", grid=(B, FOX_P, NQ),
        in_specs=[pl.BlockSpec((T, 384), lambda b, p, q: (b, p)), pl.BlockSpec((T, LANE), lambda b, p, q: (b, 0)),
                  pl.BlockSpec((1, 8, T), lambda b, p, q: (b, 0, 0)), vec, vec] + [ANY] * nga,
        out_specs=[tile, tile] + [ANY] * nga, out_shape=[S((N, 512), BF16), S((N, 512), F32)] + _gather_shapes(gather),
        scratch_shapes=[pltpu.VMEM((T, LANE), BF16)] * 3 + (_gather_sems(nga) if nga else []),
        compiler_params=_cp(("arbitrary",) * 3 if nga else ("parallel", "parallel", "arbitrary")),
    )(z, fc, fct, gq, gk, *gather)


def _fox_bwd(z, dy, y, lse, fc, fct, gq, gk, B, T, tq=512, swap=()):
    N = B * T
    NQ = T // tq
    nsw = len(swap)

    def body(z_ref, dy_ref, y_ref, lse_ref, fc_ref, fct_ref, gq_ref, gk_ref, dz_ref, dfc_ref, dgq_ref, dgk_ref,
             qn_s, kn_s, v_s, do_s, delta_s, dq_s, dfk_s):
        p, kj = pl.program_id(1), pl.program_id(2)
        lo_half = _iota((1, LANE), 1) < FOX_D
        lane = _iota((tq, LANE), 1)
        gq_v, gk_v = gq_ref[...], gk_ref[...]

        @pl.when(kj == 0)
        def _():
            def prep(i, carry):
                r = pl.ds(pl.multiple_of(i * tq, tq), tq)
                qn, kn, v = _fox_prep(z_ref, gq_v, gk_v, r, lo_half)[:3]
                qn_s[r, :], kn_s[r, :], v_s[r, :] = qn.astype(BF16), kn.astype(BF16), v.astype(BF16)
                do = dy_ref[r, :]
                do_s[r, :] = do.astype(BF16)
                delta_s[r, :] = _pair_mean(do * y_ref[r, :].astype(F32), lo_half) * float(FOX_D)
                return carry
            lax.fori_loop(0, NQ, prep, 0)
            dq_s[...] = jnp.zeros_like(dq_s)
            dgq_ref[...] = jnp.zeros_like(dgq_ref)
            dgk_ref[...] = jnp.zeros_like(dgk_ref)

        rk = pl.ds(pl.multiple_of(kj * tq, tq), tq)
        kn, vv = kn_s[rk, :], v_s[rk, :]
        causal = _iota((tq, tq), 0) >= _iota((tq, tq), 1)
        zero, one = jnp.zeros_like(kn), jnp.ones_like(kn)
        hms = [lo_half, jnp.logical_not(lo_half)]
        kmasks = [jnp.where(hm, kn, zero) for hm in hms]
        kaugs = [jnp.where(hm, kn, one) for hm in hms]
        vmasks = [jnp.where(hm, vv, zero) for hm in hms]
        fks = [fct_ref[0, pl.ds(2 * p + hh, 1), rk] for hh in range(2)]

        def qloop(i, carry, diagonal):
            ri = pl.ds(pl.multiple_of(i * tq, tq), tq)
            qn = qn_s[ri, :]
            do = do_s[ri, :]
            fcq = fc_ref[ri, :]
            new = []
            for hh in range(2):
                dk_acc, dv_acc = carry[hh]
                c0 = FOX_D * hh
                fq = jnp.sum(jnp.where(lane == 2 * p + hh, fcq, 0.0), axis=-1, keepdims=True)
                pr = jnp.exp(_nt(qn, kmasks[hh]) + fq - fks[hh] - lse_ref[ri, c0:c0 + 1])
                if diagonal:
                    pr = jnp.where(causal, pr, 0.0)
                ds = (pr * (_nt(do, vmasks[hh]) - delta_s[ri, c0:c0 + 1])).astype(BF16)
                dq_s[hh, ri, :] += _nn(ds, kaugs[hh])
                new.append((dk_acc + _tn(jnp.where(hms[hh], qn, one), ds), dv_acc + _tn(do, pr.astype(BF16))))
            return tuple(new)

        init = tuple((jnp.zeros((LANE, tq), F32), jnp.zeros((LANE, tq), F32)) for _ in range(2))
        carry = qloop(kj, init, True)
        (dk0, dv0), (dk1, dv1) = lax.fori_loop(kj + 1, NQ, functools.partial(qloop, diagonal=False), carry)
        dks, dvs = [dk0.T, dk1.T], [dv0.T, dv1.T]

        dkn = jnp.where(lo_half, dks[0], dks[1])
        _, _, _, _, kh, _, rkk = _fox_prep(z_ref, gq_v, gk_v, rk, lo_half)
        u = dkn * gk_v
        dz_ref[rk, LANE:2 * LANE] = (rkk * (u - kh * _pair_mean(u * kh, lo_half))).astype(BF16)
        dz_ref[rk, 2 * LANE:3 * LANE] = jnp.where(lo_half, dvs[0], dvs[1]).astype(BF16)
        dgk_ref[...] += _rowsum8(dkn * kh)
        dfk_s[rk, :] = jnp.where(lane == 2 * p, -dks[0][:, FOX_D:FOX_D + 1],
                                 jnp.where(lane == 2 * p + 1, -dks[1][:, 0:1], 0.0))

        @pl.when(kj == NQ - 1)
        def _():
            def fin(i, carry):
                r = pl.ds(pl.multiple_of(i * tq, tq), tq)
                d0, d1 = dq_s[0, r, :], dq_s[1, r, :]
                dqn = jnp.where(lo_half, d0, d1)
                _, _, _, qh, _, rqq, _ = _fox_prep(z_ref, gq_v, gk_v, r, lo_half)
                u = dqn * gq_v * (FOX_D ** -0.5)
                dz_ref[r, 0:LANE] = (rqq * (u - qh * _pair_mean(u * qh, lo_half))).astype(BF16)
                dgq_ref[...] += _rowsum8(dqn * qh) * (FOX_D ** -0.5)
                dfc_ref[r, :] = dfk_s[r, :] + jnp.where(lane == 2 * p, d0[:, FOX_D:FOX_D + 1],
                                                        jnp.where(lane == 2 * p + 1, d1[:, 0:1], 0.0))
                return carry
            lax.fori_loop(0, NQ, fin, 0)

    vec = pl.BlockSpec((1, LANE), lambda b, p, k: (0, 0))
    col = pl.BlockSpec((T, LANE), lambda b, p, k: (b, p))
    part = pl.BlockSpec((8, LANE), lambda b, p, k: (b * FOX_P + p, 0))
    if nsw:
        body = _hosting(body, 8, 4, 7, nsw, _chip_swap_phases, (B, FOX_P, NQ))
    return pl.pallas_call(
        body, name="fox_bwd", grid=(B, FOX_P, NQ),
        in_specs=[pl.BlockSpec((T, 384), lambda b, p, k: (b, p)), col, col, col,
                  pl.BlockSpec((T, LANE), lambda b, p, k: (b, 0)), pl.BlockSpec((1, 8, T), lambda b, p, k: (b, 0, 0)),
                  vec, vec] + [ANY] * nsw,
        out_specs=[pl.BlockSpec((T, 384), lambda b, p, k: (b, p)), col, part, part] + [ANY] * nsw,
        out_shape=[S((N, 1536), BF16), S((N, 512), F32), S((B * FOX_P * 8, LANE), F32), S((B * FOX_P * 8, LANE), F32)]
        + [S(p.shape, p.dtype) for p in swap],
        scratch_shapes=[pltpu.VMEM((T, LANE), BF16)] * 4 + [pltpu.VMEM((T, LANE), F32), pltpu.VMEM((2, T, LANE), F32),
                                                            pltpu.VMEM((T, LANE), F32)]
        + (_chip_swap_sems(nsw) if nsw else []),
        compiler_params=_cp(("arbitrary",) * 3 if nsw else ("parallel", "parallel", "arbitrary")),
    )(z, dy, y, lse, fc, fct, gq, gk, *swap)


def _mem_scores(z_ref, kv_ref, gq, gk, h):
    c = slice(MEM_D * h, MEM_D * (h + 1))
    q, k = z_ref[:, c].astype(F32), kv_ref[:, c]
    rq = lax.rsqrt(jnp.mean(q * q, axis=-1, keepdims=True) + EPS)
    rk = lax.rsqrt(jnp.mean(k * k, axis=-1, keepdims=True) + EPS)
    qh, kh = q * rq, k * rk
    qn = (qh * gq * (MEM_D ** -0.5)).astype(BF16)
    kn = (kh * gk).astype(BF16)
    s = _nt(qn, kn)
    pe = jnp.exp(s - jnp.max(s, axis=-1, keepdims=True))
    pn = pe / jnp.sum(pe, axis=-1, keepdims=True)
    return pn, qn, kn, qh, kh, rq, rk


def _mem_fwd(z, memkv, gq, gk, B, T, M, tq=1024):
    N = B * T
    tq = min(tq, T)
    NQ = T // tq
    W = MEM_H * MEM_D

    def body(z_ref, kv_ref, gq_ref, gk_ref, y_ref):
        for h in range(MEM_H):
            pn = _mem_scores(z_ref, kv_ref, gq_ref[...], gk_ref[...], h)[0]
            v = kv_ref[:, W + MEM_D * h:W + MEM_D * (h + 1)].astype(BF16)
            y_ref[:, MEM_D * h:MEM_D * (h + 1)] = _nn(pn.astype(BF16), v).astype(BF16)

    vec = pl.BlockSpec((1, LANE), lambda b, q: (0, 0))
    return pl.pallas_call(
        body, name="mem_fwd", grid=(B, NQ),
        in_specs=[pl.BlockSpec((tq, W), lambda b, q: (b * NQ + q, C_MQ // W)),
                  pl.BlockSpec((M, 2 * W), lambda b, q: (b, 0)), vec, vec],
        out_specs=pl.BlockSpec((tq, W), lambda b, q: (b * NQ + q, 0)), out_shape=S((N, W), BF16),
        compiler_params=_cp(("parallel", "parallel")),
    )(z, memkv, gq, gk)


def _mem_bwd(z, memkv, dy, gq, gk, B, T, M, tq=1024):
    N = B * T
    tq = min(tq, T)
    NQ = T // tq
    W = MEM_H * MEM_D

    def body(z_ref, kv_ref, dy_ref, gq_ref, gk_ref, dz_ref, dkv_ref, dgq_ref, dgk_ref, acc):
        qi = pl.program_id(1)
        gq_v, gk_v = gq_ref[...], gk_ref[...]

        @pl.when(qi == 0)
        def _():
            acc[...] = jnp.zeros_like(acc)
            dgq_ref[...] = jnp.zeros_like(dgq_ref)
            dgk_ref[...] = jnp.zeros_like(dgk_ref)

        for h in range(MEM_H):
            c = slice(MEM_D * h, MEM_D * (h + 1))
            cv = slice(W + MEM_D * h, W + MEM_D * (h + 1))
            pn, qn, kn, qh, _, rq, _ = _mem_scores(z_ref, kv_ref, gq_v, gk_v, h)
            do = dy_ref[:, c].astype(BF16)
            dp = _nt(do, kv_ref[:, cv].astype(BF16))
            ds = (pn * (dp - jnp.sum(dp * pn, axis=-1, keepdims=True))).astype(BF16)
            dqn = _nn(ds, kn)
            acc[:, c] += _tn(ds, qn)
            acc[:, cv] += _tn(pn.astype(BF16), do)
            u = dqn * gq_v * (MEM_D ** -0.5)
            dz_ref[:, c] = (rq * (u - qh * jnp.mean(u * qh, axis=-1, keepdims=True))).astype(BF16)
            dgq_ref[...] += _rowsum8(dqn * qh) * (MEM_D ** -0.5)

        @pl.when(qi == NQ - 1)
        def _():
            for h in range(MEM_H):
                c = slice(MEM_D * h, MEM_D * (h + 1))
                cv = slice(W + MEM_D * h, W + MEM_D * (h + 1))
                k = kv_ref[:, c]
                rk = lax.rsqrt(jnp.mean(k * k, axis=-1, keepdims=True) + EPS)
                kh = k * rk
                dkn = acc[:, c]
                u = dkn * gk_v
                dkv_ref[:, c] = (rk * (u - kh * jnp.mean(u * kh, axis=-1, keepdims=True))).astype(BF16)
                dkv_ref[:, cv] = acc[:, cv].astype(BF16)
                dgk_ref[...] += _rowsum8(dkn * kh)

    vec = pl.BlockSpec((1, LANE), lambda b, q: (0, 0))
    part = pl.BlockSpec((8, LANE), lambda b, q: (b, 0))
    return pl.pallas_call(
        body, name="mem_bwd", grid=(B, NQ),
        in_specs=[pl.BlockSpec((tq, W), lambda b, q: (b * NQ + q, C_MQ // W)),
                  pl.BlockSpec((M, 2 * W), lambda b, q: (b, 0)), pl.BlockSpec((tq, W), lambda b, q: (b * NQ + q, 0)),
                  vec, vec],
        out_specs=[pl.BlockSpec((tq, W), lambda b, q: (b * NQ + q, 0)), pl.BlockSpec((M, 2 * W), lambda b, q: (b, 0)),
                   part, part],
        out_shape=[S((N, W), BF16), S((B * M, 2 * W), BF16), S((B * 8, LANE), F32), S((B * 8, LANE), F32)],
        scratch_shapes=[pltpu.VMEM((M, 2 * W), F32)], compiler_params=_cp(("parallel", "arbitrary")),
    )(z, memkv, dy, gq, gk)


def _merge_fwd(ya, yb, yc, z, x, wa, wb, wc, wo, g_next, tm=512):
    n, d = x.shape
    wdt = ya.shape[1]
    gb = C_GATE // d

    def body(ya_ref, yb_ref, yc_ref, g0_ref, g1_ref, g2_ref, x_ref, wa_ref, wb_ref, wc_ref, wo_ref, gn_ref,
             x1_ref, mg_ref, ua_ref, ub_ref, uc_ref, h_ref):
        merged = jnp.zeros((tm, d), F32)
        for y_ref, g_ref, w_ref, u_ref in ((ya_ref, g0_ref, wa_ref, ua_ref), (yb_ref, g1_ref, wb_ref, ub_ref),
                                           (yc_ref, g2_ref, wc_ref, uc_ref)):
            u = _nn(y_ref[...], w_ref[...])
            u_ref[...] = u.astype(BF16)
            merged = merged + jax.nn.sigmoid(g_ref[...].astype(F32)) * u
        mb = merged.astype(BF16)
        mg_ref[...] = mb
        x1 = x_ref[...] + _nn(mb, wo_ref[...])
        x1_ref[...] = x1
        h_ref[...] = (x1 * lax.rsqrt(jnp.mean(x1 * x1, axis=-1, keepdims=True) + EPS) * gn_ref[...]).astype(BF16)

    yt = pl.BlockSpec((tm, wdt), lambda i: (i, 0))
    xt = pl.BlockSpec((tm, d), lambda i: (i, 0))
    wbr = pl.BlockSpec((wdt, d), lambda i: (0, 0))
    gates = [pl.BlockSpec((tm, d), functools.partial(lambda i, k: (i, gb + k), k=k)) for k in range(3)]
    return pl.pallas_call(
        body, name="merge_fwd", grid=(n // tm,),
        in_specs=[yt, yt, yt] + gates + [xt, wbr, wbr, wbr, pl.BlockSpec((d, d), lambda i: (0, 0)),
                                         pl.BlockSpec((1, d), lambda i: (0, 0))],
        out_specs=[xt] * 6, out_shape=[S((n, d), F32)] + [S((n, d), BF16)] * 5, compiler_params=_cp(("parallel",)),
    )(ya, yb, yc, z, z, z, x, wa, wb, wc, wo, g_next)


def _merge_bwd(dx1, z, ua, ub, uc, wa, wb, wc, wo, tm=512):
    n, d = dx1.shape
    wdt = wa.shape[0]
    gb = C_GATE // d

    def body(dx_ref, g0_ref, g1_ref, g2_ref, ua_ref, ub_ref, uc_ref, wa_ref, wb_ref, wc_ref, wo_ref,
             dg_ref, dya_ref, dyb_ref, dyc_ref, dua_ref, dub_ref, duc_ref):
        dm = _nt(dx_ref[...].astype(BF16), wo_ref[...])
        for k, (g_ref, u_ref, w_ref, dy_ref, du_ref) in enumerate((
                (g0_ref, ua_ref, wa_ref, dya_ref, dua_ref), (g1_ref, ub_ref, wb_ref, dyb_ref, dub_ref),
                (g2_ref, uc_ref, wc_ref, dyc_ref, duc_ref))):
            g = jax.nn.sigmoid(g_ref[...].astype(F32))
            du = (dm * g).astype(BF16)
            du_ref[...] = du
            dg_ref[:, d * k:d * (k + 1)] = (dm * u_ref[...].astype(F32) * g * (1.0 - g)).astype(BF16)
            dy_ref[...] = _nt(du, w_ref[...])

    yt = pl.BlockSpec((tm, wdt), lambda i: (i, 0))
    xt = pl.BlockSpec((tm, d), lambda i: (i, 0))
    wbr = pl.BlockSpec((wdt, d), lambda i: (0, 0))
    gates = [pl.BlockSpec((tm, d), functools.partial(lambda i, k: (i, gb + k), k=k)) for k in range(3)]
    return pl.pallas_call(
        body, name="merge_bwd", grid=(n // tm,),
        in_specs=[xt] + gates + [xt, xt, xt, wbr, wbr, wbr, pl.BlockSpec((d, d), lambda i: (0, 0))],
        out_specs=[pl.BlockSpec((tm, 3 * d), lambda i: (i, 0)), yt, yt, yt, xt, xt, xt],
        out_shape=[S((n, 3 * d), BF16)] + [S((n, wdt), F32)] * 3 + [S((n, d), BF16)] * 3,
        compiler_params=_cp(("parallel",)),
    )(dx1, z, z, z, ua, ub, uc, wa, wb, wc, wo)


FFN_TN = 1408
TN_TM = 2048
INV_SQRT2 = 0.7071067811865476
INV_SQRT_2PI = 0.3989422804014327


def _conv_shifted(a, prev, first, tm):
    row = _iota(a.shape, 0)
    p7 = jnp.where(first, 0.0, prev[7:8, :])
    p6 = jnp.where(first, 0.0, prev[6:7, :])
    a1 = jnp.where(row == 0, p7, pltpu.roll(a, 1, 0))
    a2 = jnp.where(row == 0, p6, jnp.where(row == 1, p7, pltpu.roll(a, 2, 0)))
    return a1, a2


def _ffn_act_fwd(up, cw, cb, B, T, tm=1024):
    N = B * T
    tm = min(tm, T)
    dff = cw.shape[1]
    NT, NJ, tn = T // tm, dff // FFN_TN, FFN_TN

    def body(a_ref, v_ref, cw_ref, cb_ref, y_ref, c_ref, carry):
        t = pl.program_id(2)
        a = a_ref[...].astype(F32)
        a1, a2 = _conv_shifted(a, carry[...], t == 0, tm)
        w = cw_ref[...]
        ac = w[0:1, :] * a2 + w[1:2, :] * a1 + w[2:3, :] * a + cb_ref[...]
        cdf = 0.5 * (1.0 + lax.erf(ac * INV_SQRT2))
        y_ref[...] = (ac * cdf * v_ref[...].astype(F32)).astype(BF16)
        c_ref[...] = cdf.astype(BF16)
        carry[...] = a[tm - 8:tm, :]

    return pl.pallas_call(
        body, name="ffn_act_fwd", grid=(B, NJ, NT),
        in_specs=[pl.BlockSpec((tm, tn), lambda b, j, t: (b * NT + t, j)),
                  pl.BlockSpec((tm, tn), lambda b, j, t: (b * NT + t, NJ + j)),
                  pl.BlockSpec((3, tn), lambda b, j, t: (0, j)), pl.BlockSpec((1, tn), lambda b, j, t: (0, j))],
        out_specs=[pl.BlockSpec((tm, tn), lambda b, j, t: (b * NT + t, j))] * 2, out_shape=[S((N, dff), BF16)] * 2,
        scratch_shapes=[pltpu.VMEM((8, tn), F32)], compiler_params=_cp(("parallel", "parallel", "arbitrary")),
    )(up, up, cw, cb)


def _ffn_down_loss(y, wd, x1, tgt, tm=512):
    n, d = x1.shape
    kf = y.shape[1]

    def body(y_ref, w_ref, x_ref, t_ref, dx_ref, ls_ref):
        err = x_ref[...] + _nn(y_ref[...], w_ref[...]) - t_ref[...]
        dx_ref[...] = err * (1.0 / d)

        @pl.when(pl.program_id(0) == 0)
        def _():
            ls_ref[...] = jnp.zeros_like(ls_ref)

        ls_ref[...] += _rowsum8(err * err) * (0.5 / d)

    xt = pl.BlockSpec((tm, d), lambda i: (i, 0))
    return pl.pallas_call(
        body, name="ffn_down_loss", grid=(n // tm,),
        in_specs=[pl.BlockSpec((tm, kf), lambda i: (i, 0)), pl.BlockSpec((kf, d), lambda i: (0, 0)), xt, xt],
        out_specs=[xt, pl.BlockSpec((8, d), lambda i: (0, 0))], out_shape=[S((n, d), F32), S((8, d), F32)],
        compiler_params=_cp(("arbitrary",)),
    )(y, wd, x1, tgt)


def _ffn_act_bwd1(dx2, wd, up, cdf, cw, cb, B, T, tm=512):
    N = B * T
    d = dx2.shape[1]
    dff = cw.shape[1]
    NT, NJ, tn = T // tm, dff // FFN_TN, FFN_TN

    def body(dx_ref, w_ref, a_ref, v_ref, c_ref, cw_ref, cb_ref, dac_ref, dv_ref, dcw_ref, dcb_ref, carry):
        b, t = pl.program_id(1), pl.program_id(2)
        a = a_ref[...].astype(F32)
        a1, a2 = _conv_shifted(a, carry[...], t == 0, tm)
        carry[...] = a[tm - 8:tm, :]
        w = cw_ref[...]
        ac = w[0:1, :] * a2 + w[1:2, :] * a1 + w[2:3, :] * a + cb_ref[...]
        dy = _nt(dx_ref[...].astype(BF16), w_ref[...])
        cdf = c_ref[...].astype(F32)
        dv_ref[...] = (dy * ac * cdf).astype(BF16)
        dac = dy * v_ref[...].astype(F32) * (cdf + ac * jnp.exp(-0.5 * ac * ac) * INV_SQRT_2PI)
        dac_ref[...] = dac

        @pl.when((b == 0) & (t == 0))
        def _():
            dcw_ref[...] = jnp.zeros_like(dcw_ref)
            dcb_ref[...] = jnp.zeros_like(dcb_ref)

        dcw_ref[0:8, :] += _rowsum8(dac * a2)
        dcw_ref[8:16, :] += _rowsum8(dac * a1)
        dcw_ref[16:24, :] += _rowsum8(dac * a)
        dcb_ref[...] += _rowsum8(dac)

    return pl.pallas_call(
        body, name="ffn_act_bwd1", grid=(NJ, B, NT),
        in_specs=[pl.BlockSpec((tm, d), lambda j, b, t: (b * NT + t, 0)), pl.BlockSpec((tn, d), lambda j, b, t: (j, 0)),
                  pl.BlockSpec((tm, tn), lambda j, b, t: (b * NT + t, j)),
                  pl.BlockSpec((tm, tn), lambda j, b, t: (b * NT + t, NJ + j)),
                  pl.BlockSpec((tm, tn), lambda j, b, t: (b * NT + t, j)),
                  pl.BlockSpec((3, tn), lambda j, b, t: (0, j)), pl.BlockSpec((1, tn), lambda j, b, t: (0, j))],
        out_specs=[pl.BlockSpec((tm, tn), lambda j, b, t: (b * NT + t, j)),
                   pl.BlockSpec((tm, tn), lambda j, b, t: (b * NT + t, j)),
                   pl.BlockSpec((24, tn), lambda j, b, t: (0, j)), pl.BlockSpec((8, tn), lambda j, b, t: (0, j))],
        out_shape=[S((N, dff), F32), S((N, dff), BF16), S((24, dff), F32), S((8, dff), F32)],
        scratch_shapes=[pltpu.VMEM((8, tn), F32)], compiler_params=_cp(("parallel", "arbitrary", "arbitrary")),
    )(dx2, wd, up, up, cdf, cw, cb)


def _ffn_act_bwd2(dac, cw, B, T, tm=1024):
    N = B * T
    tm = min(tm, T)
    dff = cw.shape[1]
    NT, NJ, tn = T // tm, dff // FFN_TN, FFN_TN
    last8 = N // 8 - 1

    def body(d_ref, nx_ref, cw_ref, da_ref):
        t = pl.program_id(2)
        dd = d_ref[...]
        row = _iota(dd.shape, 0)
        last = t == NT - 1
        n0 = jnp.where(last, 0.0, nx_ref[0:1, :])
        n1 = jnp.where(last, 0.0, nx_ref[1:2, :])
        d1 = jnp.where(row == tm - 1, n0, pltpu.roll(dd, tm - 1, 0))
        d2 = jnp.where(row == tm - 1, n1, jnp.where(row == tm - 2, n0, pltpu.roll(dd, tm - 2, 0)))
        w = cw_ref[...]
        da_ref[...] = (w[2:3, :] * dd + w[1:2, :] * d1 + w[0:1, :] * d2).astype(BF16)

    return pl.pallas_call(
        body, name="ffn_act_bwd2", grid=(B, NJ, NT),
        in_specs=[pl.BlockSpec((tm, tn), lambda b, j, t: (b * NT + t, j)),
                  pl.BlockSpec((8, tn), lambda b, j, t: (jnp.minimum((b * NT + t + 1) * (tm // 8), last8), j)),
                  pl.BlockSpec((3, tn), lambda b, j, t: (0, j))],
        out_specs=pl.BlockSpec((tm, tn), lambda b, j, t: (b * NT + t, j)), out_shape=S((N, dff), BF16),
        compiler_params=_cp(("parallel", "parallel", "parallel")),
    )(dac, dac, cw)


def _fold_rows(p, name):
    r, c = p.shape[0] // 8, p.shape[1]

    def body(p_ref, o_ref):
        for j in range(r):
            o_ref[j:j + 1, :] = jnp.sum(p_ref[8 * j:8 * (j + 1), :], axis=0, keepdims=True)

    return pl.pallas_call(body, name=name, out_shape=S((r, c), F32), compiler_params=_cp())(p)


def _small_reduce(lbl, dg_mix, dg_mem, dlb_p, dgn_p, dfb_p, dgq_p, dgk_p, dmq_p, dmk_p, dg_ffn, dcb_p, loss_p):
    d, dff = dg_mix.shape[1], dcb_p.shape[1]
    nbh = dlb_p.shape[0] // (8 * HG_H)

    def colsum(ref):
        return jnp.sum(ref[...], axis=0, keepdims=True)

    def body(lbl_ref, mix_ref, mem_ref, dlb_ref, dgn_ref, dfb_ref, dgq_ref, dgk_ref, dmq_ref, dmk_ref, ffn_ref, dcb_ref,
             ls_ref, o_mix, o_mem, o_lb, o_hgn, o_fb, o_fq, o_fk, o_mq, o_mk, o_ffn, o_cb, o_loss):
        o_mix[...], o_mem[...], o_ffn[...], o_cb[...] = colsum(mix_ref), colsum(mem_ref), colsum(ffn_ref), colsum(dcb_ref)
        o_hgn[...], o_fb[...], o_mq[...], o_mk[...] = colsum(dgn_ref), colsum(dfb_ref), colsum(dmq_ref), colsum(dmk_ref)
        for src, dst in ((dgq_ref, o_fq), (dgk_ref, o_fk)):
            v = colsum(src)
            dst[...] = v + pltpu.roll(v, FOX_D, 1)
        o_loss[...] = jnp.zeros((1, LANE), F32) + jnp.sum(colsum(ls_ref), axis=-1, keepdims=True)
        logits = lbl_ref[...]
        e = jnp.exp(logits - jnp.max(logits, axis=0, keepdims=True))
        pr = e / jnp.sum(e, axis=0, keepdims=True)
        rows = _iota((8, LANE), 0)
        for h in range(HG_H):
            acc = jnp.zeros((8, LANE), F32)
            for b in range(nbh):
                acc = acc + dlb_ref[8 * (b * HG_H + h):8 * (b * HG_H + h + 1), :]
            dlb = jnp.sum(acc, axis=0, keepdims=True)
            c = slice(LANE * h, LANE * (h + 1))
            p0 = pr[0:1, c]
            first = _iota((logits.shape[0], LANE), 0) == 0
            o_lb[:, c] = pr[:, c] * (jnp.where(first, 1.0, 0.0) - p0) * dlb

    outs = [S((1, d), F32), S((1, d), F32), S(lbl.shape, F32)] + [S((1, LANE), F32)] * 6 + \
           [S((1, d), F32), S((1, dff), F32), S((1, LANE), F32)]
    return pl.pallas_call(body, name="small_reduce", out_shape=outs, compiler_params=_cp())(
        lbl, dg_mix, dg_mem, dlb_p, dgn_p, dfb_p, dgq_p, dgk_p, dmq_p, dmk_p, dg_ffn, dcb_p, loss_p)


def _in_col_pieces():
    hw, fw = HG_H * HG_D, FOX_H * FOX_D
    fox0, ff0 = 4 * hw, 4 * hw + 3 * fw
    mq0 = ff0 + FOX_H
    gate0 = mq0 + MEM_H * MEM_D
    pieces = []
    for p in range(FOX_P):
        pieces += [(fox0 + j * fw + LANE * p, LANE) for j in range(3)]
    pieces.append((mq0, MEM_H * MEM_D))
    for h in range(HG_H):
        pieces += [(j * hw + HG_D * h, HG_D) for j in range(4)]
    pieces.append((gate0, C_FF - C_GATE))
    pieces.append((ff0, FOX_H))
    return pieces


def _perm_from_blocks(blocks):
    n_blk, _, c = blocks.shape
    parts = []
    for s, n in _in_col_pieces():
        lo = s
        while lo < s + n:
            d = lo // c
            hi = min(s + n, (d + 1) * c)
            parts.append(blocks[d][:, lo - d * c:hi - d * c])
            lo = hi
    parts.append(jnp.zeros((blocks.shape[1], C_END - C_FF - FOX_H), blocks.dtype))
    return jnp.concatenate(parts, axis=1)


def _unperm_blocks(segs, n_blk):
    starts = [0]
    for a in segs:
        starts.append(starts[-1] + a.shape[1])
    new_start, placed = 0, []
    for s, n in _in_col_pieces():
        placed.append((s, new_start, n))
        new_start += n
    placed.sort()
    c = sum(n for _, _, n in placed) // n_blk
    blocks = []
    for d in range(n_blk):
        parts = []
        for s, ns, n in placed:
            lo, hi = max(s, d * c), min(s + n, (d + 1) * c)
            if lo < hi:
                i = max(j for j in range(len(segs)) if starts[j] <= ns)
                parts.append(segs[i][:, ns + lo - s - starts[i]:ns + hi - s - starts[i]])
        blocks.append(jnp.concatenate(parts, axis=1))
    return jnp.stack(blocks)


def _local_step(x2, mem2, tgt, sm, W, B, T, M, ex=None):
    fbias = jnp.pad(sm["fox_f_bias"], ((0, 0), (0, LANE - FOX_H)))
    gq2 = jnp.concatenate([sm["fox_q_norm_g"]] * 2, axis=1)
    gk2 = jnp.concatenate([sm["fox_k_norm_g"]] * 2, axis=1)
    lbl = sm["hgrn_lb_logits"]
    h = _rmsnorm_cast(x2, sm["norm_mix_g"], "norm_mix")
    z = _mm_nn(h, W["w_in"], BF16, "proj_in", 512, 2432)
    memn = _rmsnorm_cast(mem2, sm["norm_mem_g"], "norm_mem", tm=256)
    memkv = _mm_nn(memn, W["mem_kv_w"], F32, "proj_memkv", 256, 512)
    ya, o_raw, states, a_mat = _hgrn_fwd(z, lbl, sm["hgrn_norm_g"], B, T)
    fc, fct = _fox_gate_fwd(z, fbias, B, T)
    yb, lse, *late = _fox_fwd(z, fc, fct, gq2, gk2, B, T, gather=ex.late_blocks() if ex else ())
    if ex:
        W = {**W, **ex.unpack_late(late)}
    yc = _mem_fwd(z, memkv, sm["mem_q_norm_g"], sm["mem_k_norm_g"], B, T, M)
    x1, merged, ua, ub, uc, h2 = _merge_fwd(ya, yb, yc, z, x2, W["w_br_hgrn"], W["w_br_fox"], W["w_br_mem"], W["w_out"],
                                            sm["norm_ffn_g"])
    up = _mm_nn(h2, W["ffn_w_up"], BF16, "ffn_up", 512, FFN_TN)
    yf, cdf = _ffn_act_fwd(up, W["ffn_conv_w"], sm["ffn_conv_b"], B, T)
    dx2, loss_p = _ffn_down_loss(yf, W["ffn_w_down"], x1, tgt)
    dff = W["ffn_conv_w"].shape[1]
    dac, dv, dcw_p, dcb_p = _ffn_act_bwd1(dx2, W["ffn_w_down"], up, cdf, W["ffn_conv_w"], sm["ffn_conv_b"], B, T)
    da = _ffn_act_bwd2(dac, W["ffn_conv_w"], B, T)
    g = {"ffn_conv_w": _fold_rows(dcw_p, "g_conv_w")}
    g["ffn_w_down"] = _mm_tn(yf, dx2, "g_w_down", TN_TM, 512)
    dh2 = _mm_nt_sum([(da, 0, dff, 0), (dv, 0, dff, dff)], W["ffn_w_up"], "dh2", 512)
    g["ffn_w_up"] = [_mm_tn(h2, da, "g_w_up_a", TN_TM, FFN_TN), _mm_tn(h2, dv, "g_w_up_v", TN_TM, FFN_TN)]
    dx1, dg_ffn = _rmsnorm_bwd(dh2, x1, sm["norm_ffn_g"], dx2, "norm_ffn_bwd")
    g["w_out"] = _mm_tn(merged, dx1, "g_w_out", TN_TM, 512)
    dgate, dya, dyb, dyc, dua, dub, duc = _merge_bwd(dx1, z, ua, ub, uc, W["w_br_hgrn"], W["w_br_fox"], W["w_br_mem"],
                                                    W["w_out"])
    g["w_br_hgrn"] = _mm_tn(ya, dua, "g_w_br_hgrn", TN_TM, 512)
    g["w_br_fox"] = _mm_tn(yb, dub, "g_w_br_fox", TN_TM, 512)
    g["w_br_mem"] = _mm_tn(yc, duc, "g_w_br_mem", TN_TM, 512)
    early_pk = ex.early_grads(g) if ex else ()
    dz_hg, dlb_p, dgn_p, *early_sib = _hgrn_bwd(z, o_raw, states, a_mat, dya, lbl, sm["hgrn_norm_g"], B, T,
                                                swap_sibling=early_pk)
    dz_fox, dfc, dgq_p, dgk_p, *early_chips = _fox_bwd(z, dyb, yb, lse, fc, fct, gq2, gk2, B, T,
                                                       swap=ex.pair_sums(early_pk, early_sib, "early") if ex else ())
    dz_ff, dfb_p = _fox_gate_bwd(dfc, z, fbias, B, T)
    dz_mq, dkv, dmq_p, dmk_p = _mem_bwd(z, memkv, dyc, sm["mem_q_norm_g"], sm["mem_k_norm_g"], B, T, M)
    g["mem_kv_w"] = _mm_tn(memn, dkv, "g_mem_kv_w", 256, 512)
    dmemn = _mm_nt_sum([(dkv, 0, dkv.shape[1], 0)], W["mem_kv_w"], "d_memn", 256)
    _, dg_mem = _rmsnorm_bwd(dmemn, mem2, sm["norm_mem_g"], None, "norm_mem_bwd", tm=256)
    d = x2.shape[1]
    parts = [(dz_fox, 0, C_MQ - C_FOX, C_FOX), (dz_mq, 0, C_HG - C_MQ, C_MQ), (dz_hg, 0, C_GATE - C_HG, C_HG)]
    parts += [(dgate, d * k, d, C_GATE + d * k) for k in range(3)] + [(dz_ff, 0, C_END - C_FF, C_FF)]
    g["w_in"] = [_mm_tn(h, dzs, "g_w_in_%d" % i, 2 * TN_TM, min(512, dzs.shape[1]))
                 for i, dzs in enumerate((dz_fox, dz_mq, dz_hg, dgate, dz_ff))]
    sums = None
    if ex:
        last_pk = ex.last_grads(g)
        last_sib = _swap_with_sibling(last_pk, "rs_sibling_last")
        dh, last_chips = _mm_nt_sum(parts, W["w_in"], "dh", 512, swap=ex.pair_sums(last_pk, last_sib, "last"))
        sums = (ex.final_sums(early_pk, early_sib, early_chips, "early"),
                ex.final_sums(last_pk, last_sib, last_chips, "last"))
    else:
        dh = _mm_nt_sum(parts, W["w_in"], "dh", 512)
    grad_x, dg_mix = _rmsnorm_bwd(dh, x2, sm["norm_mix_g"], dx1, "norm_mix_bwd")
    small = _small_reduce(lbl, dg_mix, dg_mem, dlb_p, dgn_p, dfb_p, dgq_p, dgk_p, dmq_p, dmk_p, dg_ffn, dcb_p, loss_p)
    names = ("norm_mix_g", "norm_mem_g", "hgrn_lb_logits", "hgrn_norm_g", "fox_f_bias", "fox_q_norm_g", "fox_k_norm_g",
             "mem_q_norm_g", "mem_k_norm_g", "norm_ffn_g", "ffn_conv_b", "loss")
    g.update(dict(zip(names, small)))
    return grad_x, g, sums


ANY = pl.BlockSpec(memory_space=pl.ANY)


def _position():
    return lax.axis_index("x"), lax.axis_index("y"), lax.axis_index("c")


def _all_gather(blocks, name):
    nb = len(blocks)

    def body(*refs):
        start, forward, finish = _gather_phases(refs[:nb], refs[nb:2 * nb], *refs[2 * nb:])
        start()
        forward()
        finish()

    return pl.pallas_call(
        body, name=name, out_shape=_gather_shapes(blocks), in_specs=[ANY] * nb, out_specs=[ANY] * nb,
        scratch_shapes=_gather_sems(nb),
    )(*blocks)


def _hosting(body, n_in, n_out, n_scratch, n_x, make_phases, grid):
    n_steps = math.prod(grid)

    def hosted(*refs):
        a = n_in + n_x
        b = a + n_out + n_x
        ins, xs = refs[:n_in], refs[n_in:a]
        outs, x_outs = refs[a:a + n_out], refs[a + n_out:b]
        scratch, sems = refs[b:b + n_scratch], refs[b + n_scratch:]
        step = 0
        for ax, n in enumerate(grid):
            step = step * n + pl.program_id(ax)
        phases = make_phases(xs, x_outs, *sems)
        pl.when(step == 0)(phases[0])
        for ph in phases[1:-1]:
            pl.when(step == n_steps // 2)(ph)
        body(*ins, *outs, *scratch)
        pl.when(step == n_steps - 1)(phases[-1])

    return hosted


def _gather_shapes(blocks):
    return [S((N_DEV,) + b.shape, b.dtype) for b in blocks]


def _gather_sems(nb):
    return [pltpu.SemaphoreType.DMA((7 * nb,)), pltpu.SemaphoreType.DMA((7 * nb,)), pltpu.SemaphoreType.DMA((nb,))]


def _gather_phases(x_refs, out_refs, send_sems, recv_sems, local_sems):
    nb = len(x_refs)
    x, y, c = _position()
    me, sibling = (x, y, c), (x, y, 1 - c)
    chips = [(1 - x, y), (x, 1 - y), (1 - x, 1 - y)]

    def copy(i, k, blk, to, own=False):
        px, py, pc = blk
        slot = out_refs[i].at[4 * px + 2 * py + pc]
        return pltpu.make_async_remote_copy(
            src_ref=x_refs[i] if own else slot, dst_ref=slot, send_sem=send_sems.at[7 * i + k],
            recv_sem=recv_sems.at[7 * i + k], device_id=to, device_id_type=MESH)

    def mine(i):
        return pltpu.make_async_copy(x_refs[i], out_refs[i].at[4 * x + 2 * y + c], local_sems.at[i])

    def first(i):
        return [copy(i, 0, me, sibling, own=True)] + [copy(i, 1 + j, me, (*chip, c), own=True)
                                                     for j, chip in enumerate(chips)]

    def passed(i, j):
        return copy(i, 4 + j, (*chips[j], c), sibling)

    def start():
        for i in range(nb):
            mine(i).start()
            for cp in first(i):
                cp.start()

    def forward():
        for i in range(nb):
            for j, chip in enumerate(chips):
                copy(i, 1 + j, (*chip, c), me).wait_recv()
                passed(i, j).start()

    def finish():
        for i in range(nb):
            copy(i, 0, sibling, me).wait_recv()
            for j, chip in enumerate(chips):
                copy(i, 4 + j, (*chip, 1 - c), me).wait_recv()
        for i in range(nb):
            for cp in first(i) + [passed(i, j) for j in range(3)]:
                cp.wait_send()
            mine(i).wait()

    return start, forward, finish


def _swap_with_sibling(pks, name):
    nb = len(pks)

    def body(*refs):
        start, finish = _sibling_swap_phases(refs[:nb], refs[nb:2 * nb], *refs[2 * nb:])
        start()
        finish()

    return pl.pallas_call(
        body, name=name, out_shape=_sibling_swap_shapes(pks), in_specs=[ANY] * nb, out_specs=[ANY] * nb,
        scratch_shapes=_sibling_swap_sems(nb),
    )(*pks)


def _sibling_swap_shapes(pks):
    return [S((4,) + p.shape[1:], p.dtype) for p in pks]


def _sibling_swap_sems(nb):
    return [pltpu.SemaphoreType.DMA((4 * nb,)), pltpu.SemaphoreType.DMA((4 * nb,))]


def _sibling_swap_phases(pk_refs, out_refs, send_sems, recv_sems):
    nb = len(pk_refs)
    x, y, c = _position()

    def copies():
        return [pltpu.make_async_remote_copy(
            src_ref=pk_refs[i].at[2 * k + 1 - c], dst_ref=out_refs[i].at[k], send_sem=send_sems.at[4 * i + k],
            recv_sem=recv_sems.at[4 * i + k], device_id=(x, y, 1 - c), device_id_type=MESH)
            for i in range(nb) for k in range(4)]

    def start():
        for cp in copies():
            cp.start()

    def finish():
        for cp in copies():
            cp.wait()

    return start, finish


def _swap_between_chips(pbs, name):
    nb = len(pbs)

    def body(*refs):
        start, finish = _chip_swap_phases(refs[:nb], refs[nb:2 * nb], *refs[2 * nb:])
        start()
        finish()

    return pl.pallas_call(
        body, name=name, out_shape=[S(p.shape, p.dtype) for p in pbs], in_specs=[ANY] * nb, out_specs=[ANY] * nb,
        scratch_shapes=_chip_swap_sems(nb),
    )(*pbs)


def _chip_swap_sems(nb):
    return [pltpu.SemaphoreType.DMA((3 * nb,)), pltpu.SemaphoreType.DMA((3 * nb,)), pltpu.SemaphoreType.DMA((nb,))]


def _chip_swap_phases(pb_refs, out_refs, send_sems, recv_sems, local_sems):
    nb = len(pb_refs)
    x, y, c = _position()
    me = 2 * x + y
    chips = [(1 - x, y), (x, 1 - y), (1 - x, 1 - y)]

    def local(i):
        return pltpu.make_async_copy(pb_refs[i].at[me], out_refs[i].at[me], local_sems.at[i])

    def send(i, j):
        cx, cy = chips[j]
        return pltpu.make_async_remote_copy(
            src_ref=pb_refs[i].at[2 * cx + cy], dst_ref=out_refs[i].at[me], send_sem=send_sems.at[3 * i + j],
            recv_sem=recv_sems.at[3 * i + j], device_id=(cx, cy, c), device_id_type=MESH)

    def arrival(i, j):
        cx, cy = chips[j]
        return pltpu.make_async_remote_copy(
            src_ref=pb_refs[i].at[me], dst_ref=out_refs[i].at[2 * cx + cy], send_sem=send_sems.at[3 * i + j],
            recv_sem=recv_sems.at[3 * i + j], device_id=(cx, cy, c), device_id_type=MESH)

    def start():
        for i in range(nb):
            local(i).start()
            for j in range(3):
                send(i, j).start()

    def finish():
        for i in range(nb):
            for j in range(3):
                arrival(i, j).wait_recv()
        for i in range(nb):
            for j in range(3):
                send(i, j).wait_send()
            local(i).wait()

    return start, finish


def _row_tile(r):
    return max(t for t in range(16, min(r, 512) + 1, 16) if r % t == 0)


def _pair_sum_cast(pk, recv, core, name):
    _, r, l = pk.shape
    tr = _row_tile(r)

    def body(c_ref, a_ref, b_ref, o_ref):
        o_ref[...] = (a_ref[...] + b_ref[...]).astype(BF16)

    return pl.pallas_call(
        body, name=name,
        grid_spec=pltpu.PrefetchScalarGridSpec(
            num_scalar_prefetch=1, grid=(4, r // tr),
            in_specs=[pl.BlockSpec((None, tr, l), lambda k, i, c: (2 * k + c[0], i, 0)),
                      pl.BlockSpec((None, tr, l), lambda k, i, c: (k, i, 0))],
            out_specs=pl.BlockSpec((None, tr, l), lambda k, i, c: (k, i, 0))),
        out_shape=S((4, r, l), BF16), compiler_params=_cp(("parallel", "parallel")),
    )(core, pk, recv)


def _final_sum(pk, recv_sib, recv_chips, slot, chip, name):
    _, r, l = pk.shape
    tr = _row_tile(r)

    def body(s_ref, k_ref, a_ref, b_ref, rc_ref, o_ref):
        base = a_ref[...] + b_ref[...]
        acc = jnp.zeros_like(base)
        for j in range(4):
            acc = acc + jnp.where(k_ref[0] == j, base, rc_ref[j].astype(F32))
        o_ref[...] = acc

    return pl.pallas_call(
        body, name=name,
        grid_spec=pltpu.PrefetchScalarGridSpec(
            num_scalar_prefetch=2, grid=(r // tr,),
            in_specs=[pl.BlockSpec((None, tr, l), lambda i, s, k: (s[0], i, 0)),
                      pl.BlockSpec((None, tr, l), lambda i, s, k: (k[0], i, 0)),
                      pl.BlockSpec((4, tr, l), lambda i, s, k: (0, i, 0))],
            out_specs=pl.BlockSpec((tr, l), lambda i, s, k: (i, 0))),
        out_shape=S((r, l), F32), compiler_params=_cp(("parallel",)),
    )(slot, chip, pk, recv_sib, recv_chips)


def _adamw_math(w, g, m, v):
    m = ADAM_B1 * m + (1.0 - ADAM_B1) * g
    v = ADAM_B2 * v + (1.0 - ADAM_B2) * (g * g)
    m_hat = m / (1.0 - ADAM_B1 ** ADAM_STEP)
    v_hat = v / (1.0 - ADAM_B2 ** ADAM_STEP)
    return -ADAM_LR * (m_hat / (jnp.sqrt(v_hat) + ADAM_EPS) + ADAM_WD * w), m, v


def _adamw(w, g, m, v, name):
    r, c = w.shape
    tr = 256 if r % 256 == 0 else r

    def body(w_ref, g_ref, m_ref, v_ref, d_ref, nm_ref, nv_ref):
        d_ref[...], nm_ref[...], nv_ref[...] = _adamw_math(w_ref[...], g_ref[...], m_ref[...], v_ref[...])

    tile = pl.BlockSpec((tr, c), lambda i: (i, 0))
    return pl.pallas_call(
        body, name=name, grid=(r // tr,), in_specs=[tile] * 4, out_specs=[tile] * 3, out_shape=[S((r, c), F32)] * 3,
        compiler_params=_cp(("parallel",)),
    )(w, g, m, v)


def _small_update(gathered, w, m, v):
    def body(ga_ref, w_ref, m_ref, v_ref, g_ref, d_ref, nm_ref, nv_ref):
        g = ga_ref[0]
        for k in range(1, N_DEV):
            g = g + ga_ref[k]
        g_ref[...] = g
        d_ref[...], nm_ref[...], nv_ref[...] = _adamw_math(w_ref[...], g, m_ref[...], v_ref[...])

    return pl.pallas_call(body, name="small_update", out_shape=[S(w.shape, F32)] * 4, compiler_params=_cp())(
        gathered, w, m, v)


BIG = ("w_in", "mem_kv_w", "w_br_hgrn", "w_br_fox", "w_br_mem", "w_out", "ffn_w_up", "ffn_conv_w", "ffn_w_down")
GROUP_ROWS = ("w_out", "ffn_w_down")
GROUP_LANE = ("w_br_hgrn", "w_br_fox", "w_br_mem")
LANE_GROUP_ROWS = 224
SMALL = ("norm_mix_g", "norm_mem_g", "hgrn_lb_logits", "hgrn_norm_g", "fox_f_bias", "fox_q_norm_g", "fox_k_norm_g",
         "mem_q_norm_g", "mem_k_norm_g", "norm_ffn_g", "ffn_conv_b")


def _rows_of(n_elems):
    return -(-n_elems // LANE)


def _to_rows(a, lead=0):
    flat = a.reshape(a.shape[:lead] + (-1,))
    pad = (-flat.shape[-1]) % LANE
    if pad:
        flat = jnp.pad(flat, [(0, 0)] * lead + [(0, pad)])
    return flat.reshape(a.shape[:lead] + (-1, LANE))


def _stack_rows(parts, lead, total_rows):
    buf = jnp.concatenate(parts, axis=lead)
    pad = total_rows - buf.shape[lead]
    return jnp.pad(buf, [(0, 0)] * lead + [(0, pad), (0, 0)])


def _round_up(n, k):
    return -(-n // k) * k


def _from_rows(rows, shape, lead=0):
    n = math.prod(shape)
    return rows.reshape(rows.shape[:lead] + (-1,))[..., :n].reshape(rows.shape[:lead] + tuple(shape))


def _blocks_to_full(blocks, kind):
    n, a, b = blocks.shape
    return blocks.transpose(1, 0, 2).reshape(a, n * b) if kind == "col" else blocks.reshape(n * a, b)


def _full_to_blocks(full, kind, n=N_DEV):
    a, b = full.shape
    return full.reshape(a, n, b // n).transpose(1, 0, 2) if kind == "col" else full.reshape(n, a // n, b)


def _lane_group_rows(shard):
    n_lane = sum(shard[n].shape[0] for n in GROUP_LANE)
    n_cw = shard["ffn_conv_w"].size
    return n_lane, _rows_of(3 * n_cw), _rows_of(n_cw), _round_up(n_lane + _rows_of(3 * n_cw), LANE_GROUP_ROWS)


def _split_bf16x3(x):
    hi = x.astype(BF16)
    r1 = x - hi.astype(F32)
    mid = r1.astype(BF16)
    return jnp.stack([hi, mid, (r1 - mid.astype(F32)).astype(BF16)])


class _Exchange:
    def __init__(self, shard):
        self.shard = shard
        xi, yi, ci = _position()
        self.core = ci.astype(jnp.int32).reshape(1)
        self.chip = (2 * xi + yi).astype(jnp.int32).reshape(1)
        self.n_lane, self.r_pieces, self.r_vals, self.r_lane = _lane_group_rows(shard)

    def first_blocks(self):
        return [self.shard["w_in"].astype(BF16), self.shard["mem_kv_w"].astype(BF16)]

    def unpack_first(self, gathered):
        return {"w_in": _perm_from_blocks(gathered[0]), "mem_kv_w": _blocks_to_full(gathered[1], "row")}

    def late_blocks(self):
        sh = self.shard
        lane_rows = [sh[n].astype(BF16) for n in GROUP_LANE] + [_to_rows(_split_bf16x3(sh["ffn_conv_w"]))]
        return [sh[n].astype(BF16) for n in GROUP_ROWS] + [sh["ffn_w_up"].astype(BF16),
                                                           _stack_rows(lane_rows, 0, self.r_lane)]

    def unpack_late(self, gathered):
        *rows, gc, gd = gathered
        sh = self.shard
        W = {"ffn_w_up": _blocks_to_full(gc, "col")}
        for n, blocks in zip(GROUP_ROWS, rows):
            W[n] = _blocks_to_full(blocks, "row")
        r0 = 0
        for n in GROUP_LANE:
            W[n] = _blocks_to_full(gd[:, r0:r0 + sh[n].shape[0]], "col")
            r0 += sh[n].shape[0]
        cw = _from_rows(gd[:, self.n_lane:self.n_lane + self.r_pieces], (3,) + sh["ffn_conv_w"].shape, lead=1).astype(F32)
        W["ffn_conv_w"] = _blocks_to_full(cw[:, 0] + cw[:, 1] + cw[:, 2], "col")
        return W

    def early_grads(self, g):
        cw_rows = _to_rows(_full_to_blocks(g["ffn_conv_w"], "col"), lead=1)
        return [_full_to_blocks(g[n], "row") for n in GROUP_ROWS] + [
            jnp.concatenate([_full_to_blocks(h, "col", N_DEV // 2) for h in g["ffn_w_up"]], axis=0),
            _stack_rows([_full_to_blocks(g[n], "col") for n in GROUP_LANE] + [cw_rows], 1, self.r_lane)]

    def last_grads(self, g):
        return [_unperm_blocks(g["w_in"], N_DEV), _full_to_blocks(g["mem_kv_w"], "row")]

    def pair_sums(self, pks, recv_sib, tag):
        return [_pair_sum_cast(p, r, self.core, "rs_pair_sum_%s%d" % (tag, i))
                for i, (p, r) in enumerate(zip(pks, recv_sib))]

    def final_sums(self, pks, recv_sib, recv_chips, tag):
        return [_final_sum(p, rs, rc, 2 * self.chip + self.core, self.chip, "rs_final_sum_%s%d" % (tag, i))
                for i, (p, rs, rc) in enumerate(zip(pks, recv_sib, recv_chips))]

    def unpack_grads(self, early, last):
        sh = self.shard
        *rows, g_up, g_lane = early
        g_shard = {"w_in": last[0], "mem_kv_w": last[1], "ffn_w_up": g_up, **dict(zip(GROUP_ROWS, rows))}
        r0 = 0
        for n in GROUP_LANE:
            g_shard[n] = g_lane[r0:r0 + sh[n].shape[0]]
            r0 += sh[n].shape[0]
        g_shard["ffn_conv_w"] = _from_rows(g_lane[self.n_lane:self.n_lane + self.r_vals], sh["ffn_conv_w"].shape)
        return g_shard


def kernel(x, mem, norm_mix_g, norm_mem_g, w_in, hgrn_lb_logits, hgrn_norm_g, fox_f_bias, fox_q_norm_g, fox_k_norm_g, mem_kv_w, mem_q_norm_g, mem_k_norm_g, w_br_hgrn, w_br_fox, w_br_mem, w_out, norm_ffn_g, ffn_w_up, ffn_conv_w, ffn_conv_b, ffn_w_down, loss_target, m_norm_mix_g, m_norm_mem_g, m_w_in, m_hgrn_lb_logits, m_hgrn_norm_g, m_fox_f_bias, m_fox_q_norm_g, m_fox_k_norm_g, m_mem_kv_w, m_mem_q_norm_g, m_mem_k_norm_g, m_w_br_hgrn, m_w_br_fox, m_w_br_mem, m_w_out, m_norm_ffn_g, m_ffn_w_up, m_ffn_conv_w, m_ffn_conv_b, m_ffn_w_down, v_norm_mix_g, v_norm_mem_g, v_w_in, v_hgrn_lb_logits, v_hgrn_norm_g, v_fox_f_bias, v_fox_q_norm_g, v_fox_k_norm_g, v_mem_kv_w, v_mem_q_norm_g, v_mem_k_norm_g, v_w_br_hgrn, v_w_br_fox, v_w_br_mem, v_w_out, v_norm_ffn_g, v_ffn_w_up, v_ffn_conv_w, v_ffn_conv_b, v_ffn_w_down):
    given = dict(locals())
    order = ("norm_mix_g", "norm_mem_g", "w_in", "hgrn_lb_logits", "hgrn_norm_g", "fox_f_bias", "fox_q_norm_g",
             "fox_k_norm_g", "mem_kv_w", "mem_q_norm_g", "mem_k_norm_g", "w_br_hgrn", "w_br_fox", "w_br_mem", "w_out",
             "norm_ffn_g", "ffn_w_up", "ffn_conv_w", "ffn_conv_b", "ffn_w_down")
    B, T, D = x.shape
    M = mem.shape[1]
    shard = {n: given[n][0] if n in BIG else given[n] for n in order}
    mom = {n: (given["m_" + n][0], given["v_" + n][0]) if n in BIG else (given["m_" + n], given["v_" + n])
           for n in order}
    shard["hgrn_lb_logits"] = hgrn_lb_logits
    for n in ("norm_mix_g", "norm_mem_g", "hgrn_norm_g", "fox_f_bias", "fox_q_norm_g", "fox_k_norm_g", "mem_q_norm_g",
              "mem_k_norm_g", "norm_ffn_g", "ffn_conv_b"):
        shard[n] = given[n].reshape(1, -1)

    ex = _Exchange(shard)
    W = ex.unpack_first(_all_gather(ex.first_blocks(), "ag_first"))

    sm = {n: shard[n] for n in SMALL}
    grad_x, g, sums = _local_step(x.reshape(B * T, D), mem.reshape(B * M, D), loss_target.reshape(B * T, D), sm, W,
                                  B, T, M, ex)
    g_shard = ex.unpack_grads(*sums)

    sg = {n: g[n] for n in SMALL}
    sg["fox_f_bias"] = g["fox_f_bias"][:, :FOX_H]
    sg["fox_q_norm_g"] = g["fox_q_norm_g"][:, :FOX_D]
    sg["fox_k_norm_g"] = g["fox_k_norm_g"][:, :FOX_D]
    slayout, row0 = {}, 0
    for n in SMALL:
        nr = _rows_of(shard[n].size)
        slayout[n] = (row0, nr)
        row0 += nr
    loss_row = row0
    r_small = _round_up(row0 + 1, 8)

    def pack_small(d, with_loss=None):
        rows = [_to_rows(d[n]) for n in SMALL]
        rows.append(with_loss if with_loss is not None else jnp.zeros((1, LANE), F32))
        return _stack_rows(rows, 0, r_small)

    sgath, = _all_gather([pack_small(sg, g["loss"])], "ag_small")
    s_g, s_d, s_m, s_v = _small_update(sgath, pack_small(shard), pack_small({n: mom[n][0].reshape(shard[n].shape) for n in SMALL}),
                                       pack_small({n: mom[n][1].reshape(shard[n].shape) for n in SMALL}))
    loss = s_g[loss_row, 0]

    grads, deltas, new_m, new_v = {}, {}, {}, {}
    for n in BIG:
        gn = g_shard[n]
        d, nm, nv = _adamw(shard[n], gn, mom[n][0], mom[n][1], "adamw_" + n)
        grads[n], deltas[n], new_m[n], new_v[n] = (a[None] for a in (gn, d, nm, nv))
    for n in SMALL:
        r0, nr = slayout[n]
        for dst, src in ((grads, s_g), (deltas, s_d), (new_m, s_m), (new_v, s_v)):
            dst[n] = _from_rows(src[r0:r0 + nr], given[n].shape)
    return (loss, grad_x.reshape(B, T, D), *[grads[n] for n in order], *[deltas[n] for n in order],
            *[new_m[n] for n in order], *[new_v[n] for n in order])
```

```python
import functools
import math

import jax
import jax.numpy as jnp
from jax import lax
from jax.experimental import pallas as pl
from jax.experimental.pallas import tpu as pltpu

F32, BF16 = jnp.float32, jnp.bfloat16
S = jax.ShapeDtypeStruct
MESH = pl.DeviceIdType.MESH

N_DEV = 8
EPS = 1e-6
LANE = 128
CHUNK = 64
SUB = 16
HG_H, HG_D = 4, 128
HG_GROUP_FWD = 4
HG_GROUP = 2
FOX_H, FOX_D = 8, 64
FOX_P = FOX_H // 2
MEM_H, MEM_D = 4, 128
NEG = -1e30
VMEM_LIMIT = 56 * 2**20

ADAM_LR, ADAM_B1, ADAM_B2, ADAM_EPS, ADAM_WD, ADAM_STEP = 0.001, 0.9, 0.999, 1e-08, 0.01, 10

C_FOX, C_MQ, C_HG, C_GATE, C_FF, C_END = 0, 1536, 2048, 4096, 7168, 7296


def _cp(sem=None):
    return pltpu.CompilerParams(dimension_semantics=sem, vmem_limit_bytes=VMEM_LIMIT)


def _dot(a, b, dims, prec=None):
    return lax.dot_general(a, b, (dims, ((), ())), preferred_element_type=F32, precision=prec)


def _nn(a, b, prec=None):
    return _dot(a, b, ((1,), (0,)), prec)


def _nt(a, b, prec=None):
    return _dot(a, b, ((1,), (1,)), prec)


def _tn(a, b, prec=None):
    return _dot(a, b, ((0,), (0,)), prec)


def _b(x):
    return x.astype(BF16)


def _mm3(fn, a, b):
    ah, bh = _b(a), _b(b)
    return fn(ah, bh) + fn(ah, _b(b - bh.astype(F32))) + fn(_b(a - ah.astype(F32)), bh)


def _iota(shape, dim):
    return lax.broadcasted_iota(jnp.int32, shape, dim)


def _rowsum8(x):
    r, d = x.shape
    return jnp.sum(x.reshape(r // 8, 8, d), axis=0)


def _rmsnorm_cast(x, g, name, tm=1024):
    n, d = x.shape

    def body(x_ref, g_ref, o_ref):
        v = x_ref[...]
        r = lax.rsqrt(jnp.mean(v * v, axis=-1, keepdims=True) + EPS)
        o_ref[...] = (v * r * g_ref[...]).astype(BF16)

    return pl.pallas_call(
        body, name=name, grid=(n // tm,),
        in_specs=[pl.BlockSpec((tm, d), lambda i: (i, 0)), pl.BlockSpec((1, d), lambda i: (0, 0))],
        out_specs=pl.BlockSpec((tm, d), lambda i: (i, 0)), out_shape=S((n, d), BF16), compiler_params=_cp(("parallel",)),
    )(x, g)


def _rmsnorm_bwd(dh, x, g, resid, name, tm=1024):
    n, d = x.shape
    has_res = resid is not None

    def body(*refs):
        if has_res:
            dh_ref, x_ref, g_ref, r_ref, dx_ref, dg_ref = refs
        else:
            dh_ref, x_ref, g_ref, dx_ref, dg_ref = refs
        v = x_ref[...]
        dhv = dh_ref[...].astype(F32)
        r = lax.rsqrt(jnp.mean(v * v, axis=-1, keepdims=True) + EPS)
        xh = v * r
        u = dhv * g_ref[...]
        dx = r * (u - xh * jnp.mean(u * xh, axis=-1, keepdims=True))
        if has_res:
            dx = dx + r_ref[...]
        dx_ref[...] = dx

        @pl.when(pl.program_id(0) == 0)
        def _():
            dg_ref[...] = jnp.zeros_like(dg_ref)

        dg_ref[...] += _rowsum8(dhv * xh)

    tile = pl.BlockSpec((tm, d), lambda i: (i, 0))
    ins = [tile, tile, pl.BlockSpec((1, d), lambda i: (0, 0))] + ([tile] if has_res else [])
    args = (dh, x, g) + ((resid,) if has_res else ())
    return pl.pallas_call(
        body, name=name, grid=(n // tm,), in_specs=ins,
        out_specs=[tile, pl.BlockSpec((8, d), lambda i: (0, 0))],
        out_shape=[S((n, d), F32), S((8, d), F32)], compiler_params=_cp(("arbitrary",)),
    )(*args)


def _mm_nn(a, b, out_dtype, name, tm, tn):
    m, k = a.shape
    n = b.shape[1]
    assert n % tn == 0 and m % tm == 0

    def body(a_ref, b_ref, o_ref):
        o_ref[...] = _nn(a_ref[...].astype(BF16), b_ref[...].astype(BF16)).astype(out_dtype)

    return pl.pallas_call(
        body, name=name, grid=(n // tn, m // tm),
        in_specs=[pl.BlockSpec((tm, k), lambda j, i: (i, 0)), pl.BlockSpec((k, tn), lambda j, i: (0, j))],
        out_specs=pl.BlockSpec((tm, tn), lambda j, i: (i, j)), out_shape=S((m, n), out_dtype),
        compiler_params=_cp(("parallel", "parallel")),
    )(a, b)


def _mm_nt_sum(parts, w, name, tm, swap=()):
    m = parts[0][0].shape[0]
    k = w.shape[0]
    assert m % tm == 0 and all(c % n == 0 and o % n == 0 for _, c, n, o in parts)
    np_ = len(parts)
    nsw = len(swap)
    n_steps = m // tm

    def body(*refs):
        o_ref = refs[2 * np_ + nsw]
        if nsw:
            start, finish = _chip_swap_phases(refs[2 * np_:2 * np_ + nsw], refs[2 * np_ + nsw + 1:2 * np_ + 2 * nsw + 1],
                                              *refs[2 * np_ + 2 * nsw + 1:])
            pl.when(pl.program_id(0) == 0)(start)
        acc = _nt(refs[0][...].astype(BF16), refs[np_][...].astype(BF16))
        for i in range(1, np_):
            acc = acc + _nt(refs[i][...].astype(BF16), refs[np_ + i][...].astype(BF16))
        o_ref[...] = acc
        if nsw:
            pl.when(pl.program_id(0) == n_steps - 1)(finish)

    dy_specs = [pl.BlockSpec((tm, n), functools.partial(lambda i, j: (i, j), j=c // n)) for _, c, n, _ in parts]
    w_specs = [pl.BlockSpec((k, n), functools.partial(lambda i, j: (0, j), j=o // n)) for _, _, n, o in parts]
    out = pl.pallas_call(
        body, name=name, grid=(n_steps,), in_specs=dy_specs + w_specs + [ANY] * nsw,
        out_specs=[pl.BlockSpec((tm, k), lambda i: (i, 0))] + [ANY] * nsw,
        out_shape=[S((m, k), F32)] + [S(p.shape, p.dtype) for p in swap],
        scratch_shapes=_chip_swap_sems(nsw) if nsw else [],
        compiler_params=_cp(("arbitrary",) if nsw else ("parallel",)),
    )(*([p[0] for p in parts] + [w] * np_ + list(swap)))
    return (out[0], out[1:]) if nsw else out[0]


def _mm_tn(x, dy, name, tm, tn):
    m, k = x.shape
    n = dy.shape[1]
    tm = min(tm, m)
    assert m % tm == 0 and n % tn == 0

    def body(x_ref, dy_ref, o_ref):
        part = _tn(x_ref[...].astype(BF16), dy_ref[...].astype(BF16))

        @pl.when(pl.program_id(1) == 0)
        def _():
            o_ref[...] = part

        @pl.when(pl.program_id(1) > 0)
        def _():
            o_ref[...] += part

    return pl.pallas_call(
        body, name=name, grid=(n // tn, m // tm),
        in_specs=[pl.BlockSpec((tm, k), lambda j, i: (i, 0)), pl.BlockSpec((tm, tn), lambda j, i: (i, j))],
        out_specs=pl.BlockSpec((k, tn), lambda j, i: (0, j)), out_shape=S((k, n), F32),
        compiler_params=_cp(("parallel", "arbitrary")),
    )(x, dy)


def _lower_bound(logits):
    e = jnp.exp(logits - jnp.max(logits, axis=0, keepdims=True))
    return e[0:1, :] / jnp.sum(e, axis=0, keepdims=True)


def _hg_gates(fl, lb):
    sig = jax.nn.sigmoid(fl)
    f = lb + (1.0 - lb) * sig
    k = (1.0 - lb) * (1.0 - sig)
    return sig, f, k, jnp.log(f)


def _silu_and_grad(x):
    s = jax.nn.sigmoid(x)
    return x * s, s * (1.0 + x * (1.0 - s))


def _hg_rowblocks(G):
    return [None] + [G[SUB * i - 1:SUB * i, :] for i in range(1, CHUNK // SUB)]


def _hg_intra_A(qs, k, G):
    refs = _hg_rowblocks(G)
    cols = _iota((SUB, LANE), 1)
    rows = _iota((SUB, LANE), 0)
    no_keys = jnp.zeros((LANE - CHUNK, HG_D), BF16)
    blocks = []
    for i in range(CHUNK // SUB):
        lo = SUB * i
        qb, Gb = qs[lo:lo + SUB, :], G[lo:lo + SUB, :]
        diag = jnp.zeros((SUB, LANE), F32)
        for s in range(SUB):
            e = jnp.exp(jnp.minimum(Gb - G[lo + s:lo + s + 1, :], 0.0))
            col = jnp.sum(qb * k[lo + s:lo + s + 1, :] * e, axis=-1, keepdims=True)
            diag = jnp.where(cols == lo + s, col, diag)
        a = jnp.where((cols >= lo) & (cols <= rows + lo), diag, 0.0)
        if i > 0:
            qr = qb * jnp.exp(Gb - refs[i])
            kr = k * jnp.exp(jnp.minimum(refs[i] - G, 0.0))
            a = jnp.where(cols < lo, _nt(_b(qr), jnp.concatenate([_b(kr), no_keys], axis=0)), a)
        blocks.append(a)
    return jnp.concatenate(blocks, axis=0)


def _hg_intra_bwd(dA, qs, k, G):
    refs = _hg_rowblocks(G)
    cols = _iota((SUB, CHUNK), 1)
    rows16 = _iota((SUB, HG_D), 0)
    dk = jnp.zeros((CHUNK, HG_D), F32)
    dq_blocks, dk_diag_blocks = [], []
    for i in range(CHUNK // SUB):
        lo = SUB * i
        qb, Gb = qs[lo:lo + SUB, :], G[lo:lo + SUB, :]
        dAb = dA[lo:lo + SUB, :]
        dq = jnp.zeros((SUB, HG_D), F32)
        dkb = jnp.zeros((SUB, HG_D), F32)
        for s in range(SUB):
            e = jnp.exp(jnp.minimum(Gb - G[lo + s:lo + s + 1, :], 0.0))
            e = jnp.where(rows16 >= s, e, 0.0)
            dcol = jnp.sum(jnp.where(cols == lo + s, dAb, 0.0), axis=-1, keepdims=True)
            w = dcol * e
            dq = dq + w * k[lo + s:lo + s + 1, :]
            dkb = jnp.where(rows16 == s, jnp.sum(w * qb, axis=0, keepdims=True), dkb)
        if i > 0:
            e1 = jnp.exp(Gb - refs[i])
            e2 = jnp.exp(jnp.minimum(refs[i] - G, 0.0))
            dA_off = jnp.where(cols < lo, dAb, 0.0)
            dq = dq + _mm3(_nn, dA_off, k * e2) * e1
            dk = dk + _mm3(_tn, dA_off, qb * e1) * e2
        dq_blocks.append(dq)
        dk_diag_blocks.append(dkb)
    return jnp.concatenate(dq_blocks, axis=0), dk + jnp.concatenate(dk_diag_blocks, axis=0)


def _tri(n, upper=False):
    r, c = _iota((n, n), 0), _iota((n, n), 1)
    return jnp.where((c >= r) if upper else (r >= c), 1.0, 0.0).astype(BF16)


def _prefix_mm(tri, x):
    hi = x.astype(BF16)
    r1 = x - hi.astype(F32)
    mid = r1.astype(BF16)
    lo = (r1 - mid.astype(F32)).astype(BF16)
    return _nn(tri, hi) + _nn(tri, mid) + _nn(tri, lo)


def _hgrn_fwd(z, lb, gn, B, T):
    N = B * T
    NC = T // CHUNK
    ng = HG_H // HG_GROUP_FWD

    def body(z_ref, lb_ref, gn_ref, y_ref, o_ref, st_ref, a_ref, s_scr):
        lbs = _lower_bound(lb_ref[...])
        tri = _tri(CHUNK)
        s_scr[...] = jnp.zeros_like(s_scr)

        def chunk(c, carry):
            r = pl.ds(pl.multiple_of(c * CHUNK, CHUNK), CHUNK)
            for hh in range(HG_GROUP_FWD):
                zc, oc = 4 * LANE * hh, LANE * hh
                ql, fl, il, gl = (z_ref[r, zc + LANE * j:zc + LANE * (j + 1)].astype(F32) for j in range(4))
                _, _, k, logf = _hg_gates(fl, lbs[:, oc:oc + LANE])
                G = _prefix_mm(tri, logf)
                qs = ql * jax.nn.sigmoid(ql)
                st = s_scr[hh]
                st_ref[hh * NC + c] = st
                g_last = G[CHUNK - 1:CHUNK, :]
                A = _b(_hg_intra_A(qs, k, G))
                a_ref[r, oc:oc + LANE] = A
                o = _nn(A[:, 0:CHUNK], _b(il)) + _nt(_b(qs * jnp.exp(G)), _b(st))
                s_scr[hh] = st * jnp.exp(g_last) + _mm3(_tn, il, k * jnp.exp(g_last - G))
                o_ref[r, oc:oc + LANE] = o
                rstd = lax.rsqrt(jnp.mean(o * o, axis=-1, keepdims=True) + EPS)
                y_ref[r, oc:oc + LANE] = (o * rstd * gn_ref[...] * (gl * jax.nn.sigmoid(gl))).astype(BF16)
            return carry

        lax.fori_loop(0, NC, chunk, 0, unroll=2)

    gw = HG_GROUP_FWD * LANE
    cb = C_HG // (4 * gw)
    return pl.pallas_call(
        body, name="hgrn_fwd", grid=(B, ng),
        in_specs=[pl.BlockSpec((T, 4 * gw), lambda b, h: (b, cb + h)), pl.BlockSpec((lb.shape[0], gw), lambda b, h: (0, h)),
                  pl.BlockSpec((1, LANE), lambda b, h: (0, 0))],
        out_specs=[pl.BlockSpec((T, gw), lambda b, h: (b, h)), pl.BlockSpec((T, gw), lambda b, h: (b, h)),
                   pl.BlockSpec((HG_GROUP_FWD * NC, HG_D, HG_D), lambda b, h: (b * ng + h, 0, 0)),
                   pl.BlockSpec((T, gw), lambda b, h: (b, h))],
        out_shape=[S((N, 512), BF16), S((N, 512), F32), S((B * HG_H * NC, HG_D, HG_D), F32), S((N, 512), BF16)],
        scratch_shapes=[pltpu.VMEM((HG_GROUP_FWD, HG_D, HG_D), F32)], compiler_params=_cp(("parallel", "parallel")),
    )(z, lb, gn)


def _hgrn_bwd(z, o_raw, states, a_mat, dy, lb, gn, B, T, swap_sibling=()):
    N = B * T
    NC = T // CHUNK
    ng = HG_H // HG_GROUP
    nsw = len(swap_sibling)

    def body(z_ref, o_ref, st_ref, a_ref, dy_ref, lb_ref, gn_ref, dz_ref, dlb_ref, dgn_ref, ds_scr, racc, dgn_acc):
        lbs = _lower_bound(lb_ref[...])
        gn_v = gn_ref[...]
        tri, triu = _tri(CHUNK), _tri(CHUNK, upper=True)
        cmask = _iota((CHUNK, CHUNK), 0) >= _iota((CHUNK, CHUNK), 1)
        for ref in (ds_scr, racc, dgn_acc, dlb_ref):
            ref[...] = jnp.zeros_like(ref)

        def chunk(ci, carry):
            c = NC - 1 - ci
            r = pl.ds(pl.multiple_of(c * CHUNK, CHUNK), CHUNK)
            for hh in range(HG_GROUP):
                zc, oc = 4 * LANE * hh, LANE * hh
                lb_v = lbs[:, oc:oc + LANE]
                ql, fl, il, gl = (z_ref[r, zc + LANE * j:zc + LANE * (j + 1)].astype(F32) for j in range(4))
                sig, f, k, logf = _hg_gates(fl, lb_v)
                G = _prefix_mm(tri, logf)
                qs, dsilu_q = _silu_and_grad(ql)
                gs, dsilu_g = _silu_and_grad(gl)
                o = o_ref[r, oc:oc + LANE]
                dyv = dy_ref[r, oc:oc + LANE]
                rstd = lax.rsqrt(jnp.mean(o * o, axis=-1, keepdims=True) + EPS)
                oh = o * rstd
                dgl = dyv * oh * gn_v * dsilu_g
                dn = dyv * gs
                dgn_acc[...] += _rowsum8(dn * oh)
                u = dn * gn_v
                do = rstd * (u - oh * jnp.mean(u * oh, axis=-1, keepdims=True))
                st = st_ref[hh * NC + c]
                dst = ds_scr[hh]
                eG = jnp.exp(G)
                g_last = G[CHUNK - 1:CHUNK, :]
                eL = jnp.exp(g_last - G)
                dA = jnp.where(cmask, _mm3(_nt, do, il), 0.0)
                dq_in, dk_in = _hg_intra_bwd(dA, qs, k, G)
                di = _tn(a_ref[r, oc:oc + LANE][:, 0:CHUNK], _b(do)) + _nt(_b(k * eL), _b(dst))
                dq = dq_in + _mm3(_nn, do, st) * eG
                dk = dk_in + _mm3(_nn, il, dst) * eL
                ds_scr[hh] = dst * jnp.exp(g_last) + _mm3(_tn, do, qs * eG)
                dd = qs * dq - k * dk
                dlogf = _prefix_mm(triu, dd) + racc[hh]
                racc[hh] += jnp.sum(dd, axis=0, keepdims=True)
                df = dlogf / f - dk
                dlb_ref[8 * hh:8 * (hh + 1), :] += _rowsum8(df * (1.0 - sig))
                dz_ref[r, zc:zc + LANE] = (dq * dsilu_q).astype(BF16)
                dz_ref[r, zc + LANE:zc + 2 * LANE] = (df * (1.0 - lb_v) * sig * (1.0 - sig)).astype(BF16)
                dz_ref[r, zc + 2 * LANE:zc + 3 * LANE] = di.astype(BF16)
                dz_ref[r, zc + 3 * LANE:zc + 4 * LANE] = dgl.astype(BF16)
            return carry

        lax.fori_loop(0, NC, chunk, 0, unroll=2)
        dgn_ref[...] = dgn_acc[...]

    gw = HG_GROUP * LANE
    cb = C_HG // (4 * gw)
    col = pl.BlockSpec((T, gw), lambda b, h: (b, h))
    if nsw:
        body = _hosting(body, 7, 3, 3, nsw, _sibling_swap_phases, (B, ng))
    return pl.pallas_call(
        body, name="hgrn_bwd", grid=(B, ng),
        in_specs=[pl.BlockSpec((T, 4 * gw), lambda b, h: (b, cb + h)), col,
                  pl.BlockSpec((HG_GROUP * NC, HG_D, HG_D), lambda b, h: (b * ng + h, 0, 0)), col, col,
                  pl.BlockSpec((lb.shape[0], gw), lambda b, h: (0, h)), pl.BlockSpec((1, LANE), lambda b, h: (0, 0))]
        + [ANY] * nsw,
        out_specs=[pl.BlockSpec((T, 4 * gw), lambda b, h: (b, h)),
                   pl.BlockSpec((8 * HG_GROUP, LANE), lambda b, h: (b * ng + h, 0)),
                   pl.BlockSpec((8, LANE), lambda b, h: (b * ng + h, 0))] + [ANY] * nsw,
        out_shape=[S((N, 2048), BF16), S((B * HG_H * 8, LANE), F32), S((B * ng * 8, LANE), F32)]
        + _sibling_swap_shapes(swap_sibling),
        scratch_shapes=[pltpu.VMEM((HG_GROUP, HG_D, HG_D), F32), pltpu.VMEM((HG_GROUP, 1, LANE), F32),
                        pltpu.VMEM((8, LANE), F32)] + (_sibling_swap_sems(nsw) if nsw else []),
        compiler_params=_cp(("arbitrary", "arbitrary") if nsw else ("parallel", "parallel")),
    )(z, o_raw, states, a_mat, dy, lb, gn, *swap_sibling)


def _pair_mean(x, lo_half):
    a = jnp.sum(jnp.where(lo_half, x, 0.0), axis=-1, keepdims=True)
    b = jnp.sum(jnp.where(lo_half, 0.0, x), axis=-1, keepdims=True)
    return jnp.where(lo_half, a, b) * (1.0 / FOX_D)


def _fox_gate_fwd(z, bias, B, T):
    N = B * T
    tb = LANE

    def body(z_ref, b_ref, fc_ref, fct_ref):
        tri = _tri(tb)

        def step(i, carry):
            r = pl.ds(pl.multiple_of(i * tb, tb), tb)
            cs = _prefix_mm(tri, jax.nn.log_sigmoid(z_ref[r, :].astype(F32) + b_ref[...])) + carry
            fc_ref[r, :] = cs
            fct_ref[0, :, r] = cs.T[0:8, :]
            return cs[tb - 1:tb, :]

        lax.fori_loop(0, T // tb, step, jnp.zeros((1, LANE), F32))

    return pl.pallas_call(
        body, name="fox_gate_fwd", grid=(B,),
        in_specs=[pl.BlockSpec((T, LANE), lambda b: (b, C_FF // LANE)), pl.BlockSpec((1, LANE), lambda b: (0, 0))],
        out_specs=[pl.BlockSpec((T, LANE), lambda b: (b, 0)), pl.BlockSpec((1, 8, T), lambda b: (b, 0, 0))],
        out_shape=[S((N, LANE), F32), S((B, 8, T), F32)], compiler_params=_cp(("parallel",)),
    )(z, bias)


def _fox_gate_bwd(dfc, z, bias, B, T):
    N = B * T
    tb = LANE
    nt = T // tb

    def body(d_ref, z_ref, b_ref, dz_ref, db_ref):
        triu = _tri(tb, upper=True)
        db_ref[...] = jnp.zeros_like(db_ref)

        def step(ii, carry):
            r = pl.ds(pl.multiple_of((nt - 1 - ii) * tb, tb), tb)
            d = d_ref[r, 0:LANE]
            for p in range(1, FOX_P):
                d = d + d_ref[r, LANE * p:LANE * (p + 1)]
            rc = _prefix_mm(triu, d) + carry
            dff = rc * jax.nn.sigmoid(-(z_ref[r, :].astype(F32) + b_ref[...]))
            dz_ref[r, :] = dff.astype(BF16)
            db_ref[...] += _rowsum8(dff)
            return carry + jnp.sum(d, axis=0, keepdims=True)

        lax.fori_loop(0, nt, step, jnp.zeros((1, LANE), F32))

    return pl.pallas_call(
        body, name="fox_gate_bwd", grid=(B,),
        in_specs=[pl.BlockSpec((T, 512), lambda b: (b, 0)), pl.BlockSpec((T, LANE), lambda b: (b, C_FF // LANE)),
                  pl.BlockSpec((1, LANE), lambda b: (0, 0))],
        out_specs=[pl.BlockSpec((T, LANE), lambda b: (b, 0)), pl.BlockSpec((8, LANE), lambda b: (b, 0))],
        out_shape=[S((N, LANE), BF16), S((B * 8, LANE), F32)], compiler_params=_cp(("parallel",)),
    )(dfc, z, bias)


def _fox_prep(z_ref, gq, gk, r, lo_half):
    q, k, v = (z_ref[r, LANE * j:LANE * (j + 1)].astype(F32) for j in range(3))
    rq = lax.rsqrt(_pair_mean(q * q, lo_half) + EPS)
    rk = lax.rsqrt(_pair_mean(k * k, lo_half) + EPS)
    qh, kh = q * rq, k * rk
    return qh * gq * (FOX_D ** -0.5), kh * gk, v, qh, kh, rq, rk


def _fox_fwd(z, fc, fct, gq, gk, B, T, tq=512, gather=()):
    N = B * T
    NQ = T // tq
    nga = len(gather)

    def body(z_ref, fc_ref, fct_ref, gq_ref, gk_ref, y_ref, lse_ref, qn_s, kn_s, v_s):
        p, qi = pl.program_id(1), pl.program_id(2)
        lo_half = _iota((1, LANE), 1) < FOX_D

        @pl.when(qi == 0)
        def _():
            def prep(i, carry):
                r = pl.ds(pl.multiple_of(i * tq, tq), tq)
                qn, kn, v = _fox_prep(z_ref, gq_ref[...], gk_ref[...], r, lo_half)[:3]
                qn_s[r, :], kn_s[r, :], v_s[r, :] = qn.astype(BF16), kn.astype(BF16), v.astype(BF16)
                return carry
            lax.fori_loop(0, NQ, prep, 0)

        rq = pl.ds(pl.multiple_of(qi * tq, tq), tq)
        qn = qn_s[rq, :]
        fcq = fc_ref[rq, :]
        lane = _iota((tq, LANE), 1)
        causal = _iota((tq, tq), 0) >= _iota((tq, tq), 1)
        qhs = [jnp.where(lo_half, qn, jnp.zeros_like(qn)), jnp.where(lo_half, jnp.zeros_like(qn), qn)]
        fqs = [jnp.sum(jnp.where(lane == 2 * p + hh, fcq, 0.0), axis=-1, keepdims=True) for hh in range(2)]

        def kv(j, carry, diagonal):
            rk = pl.ds(pl.multiple_of(j * tq, tq), tq)
            kj, vj = kn_s[rk, :], v_s[rk, :]
            one = jnp.ones_like(vj)
            new = []
            for hh in range(2):
                m, acc = carry[hh]
                s = _nt(qhs[hh], kj) + fqs[hh] - fct_ref[0, pl.ds(2 * p + hh, 1), rk]
                if diagonal:
                    s = jnp.where(causal, s, NEG)
                m_new = jnp.maximum(m, jnp.max(s, axis=-1, keepdims=True))
                pe = jnp.exp(s - m_new)
                v_aug = jnp.where(lo_half if hh == 0 else jnp.logical_not(lo_half), vj, one)
                new.append((m_new, jnp.exp(m - m_new) * acc + _nn(pe.astype(BF16), v_aug)))
            return tuple(new)

        init = tuple((jnp.full((tq, 1), NEG, F32), jnp.zeros((tq, LANE), F32)) for _ in range(2))
        carry = lax.fori_loop(0, qi, functools.partial(kv, diagonal=False), init)
        (m0, a0), (m1, a1) = kv(qi, carry, True)
        l0, l1 = a0[:, FOX_D:FOX_D + 1], a1[:, 0:1]
        y_ref[...] = jnp.where(lo_half, a0 / l0, a1 / l1).astype(BF16)
        lse_ref[...] = jnp.where(lo_half, m0 + jnp.log(l0), m1 + jnp.log(l1))

    vec = pl.BlockSpec((1, LANE), lambda b, p, q: (0, 0))
    tile = pl.BlockSpec((tq, LANE), lambda b, p, q: (b * NQ + q, p))
    if nga:
        body = _hosting(body, 5, 2, 3, nga, _gather_phases, (B, FOX_P, NQ))
    return pl.pallas_call(
        body, name="fox_fwd", grid=(B, FOX_P, NQ),
        in_specs=[pl.BlockSpec((T, 384), lambda b, p, q: (b, p)), pl.BlockSpec((T, LANE), lambda b, p, q: (b, 0)),
                  pl.BlockSpec((1, 8, T), lambda b, p, q: (b, 0, 0)), vec, vec] + [ANY] * nga,
        out_specs=[tile, tile] + [ANY] * nga, out_shape=[S((N, 512), BF16), S((N, 512), F32)] + _gather_shapes(gather),
        scratch_shapes=[pltpu.VMEM((T, LANE), BF16)] * 3 + (_gather_sems(nga) if nga else []),
        compiler_params=_cp(("arbitrary",) * 3 if nga else ("parallel", "parallel", "arbitrary")),
    )(z, fc, fct, gq, gk, *gather)


def _fox_bwd(z, dy, y, lse, fc, fct, gq, gk, B, T, tq=512, swap=()):
    N = B * T
    NQ = T // tq
    nsw = len(swap)

    def body(z_ref, dy_ref, y_ref, lse_ref, fc_ref, fct_ref, gq_ref, gk_ref, dz_ref, dfc_ref, dgq_ref, dgk_ref,
             qn_s, kn_s, v_s, do_s, delta_s, dq_s, dfk_s):
        p, kj = pl.program_id(1), pl.program_id(2)
        lo_half = _iota((1, LANE), 1) < FOX_D
        lane = _iota((tq, LANE), 1)
        gq_v, gk_v = gq_ref[...], gk_ref[...]

        @pl.when(kj == 0)
        def _():
            def prep(i, carry):
                r = pl.ds(pl.multiple_of(i * tq, tq), tq)
                qn, kn, v = _fox_prep(z_ref, gq_v, gk_v, r, lo_half)[:3]
                qn_s[r, :], kn_s[r, :], v_s[r, :] = qn.astype(BF16), kn.astype(BF16), v.astype(BF16)
                do = dy_ref[r, :]
                do_s[r, :] = do.astype(BF16)
                delta_s[r, :] = _pair_mean(do * y_ref[r, :].astype(F32), lo_half) * float(FOX_D)
                return carry
            lax.fori_loop(0, NQ, prep, 0)
            dq_s[...] = jnp.zeros_like(dq_s)
            dgq_ref[...] = jnp.zeros_like(dgq_ref)
            dgk_ref[...] = jnp.zeros_like(dgk_ref)

        rk = pl.ds(pl.multiple_of(kj * tq, tq), tq)
        kn, vv = kn_s[rk, :], v_s[rk, :]
        causal = _iota((tq, tq), 0) >= _iota((tq, tq), 1)
        zero, one = jnp.zeros_like(kn), jnp.ones_like(kn)
        hms = [lo_half, jnp.logical_not(lo_half)]
        kmasks = [jnp.where(hm, kn, zero) for hm in hms]
        kaugs = [jnp.where(hm, kn, one) for hm in hms]
        vmasks = [jnp.where(hm, vv, zero) for hm in hms]
        fks = [fct_ref[0, pl.ds(2 * p + hh, 1), rk] for hh in range(2)]

        def qloop(i, carry, diagonal):
            ri = pl.ds(pl.multiple_of(i * tq, tq), tq)
            qn = qn_s[ri, :]
            do = do_s[ri, :]
            fcq = fc_ref[ri, :]
            new = []
            for hh in range(2):
                dk_acc, dv_acc = carry[hh]
                c0 = FOX_D * hh
                fq = jnp.sum(jnp.where(lane == 2 * p + hh, fcq, 0.0), axis=-1, keepdims=True)
                pr = jnp.exp(_nt(qn, kmasks[hh]) + fq - fks[hh] - lse_ref[ri, c0:c0 + 1])
                if diagonal:
                    pr = jnp.where(causal, pr, 0.0)
                ds = (pr * (_nt(do, vmasks[hh]) - delta_s[ri, c0:c0 + 1])).astype(BF16)
                dq_s[hh, ri, :] += _nn(ds, kaugs[hh])
                new.append((dk_acc + _tn(jnp.where(hms[hh], qn, one), ds), dv_acc + _tn(do, pr.astype(BF16))))
            return tuple(new)

        init = tuple((jnp.zeros((LANE, tq), F32), jnp.zeros((LANE, tq), F32)) for _ in range(2))
        carry = qloop(kj, init, True)
        (dk0, dv0), (dk1, dv1) = lax.fori_loop(kj + 1, NQ, functools.partial(qloop, diagonal=False), carry)
        dks, dvs = [dk0.T, dk1.T], [dv0.T, dv1.T]

        dkn = jnp.where(lo_half, dks[0], dks[1])
        _, _, _, _, kh, _, rkk = _fox_prep(z_ref, gq_v, gk_v, rk, lo_half)
        u = dkn * gk_v
        dz_ref[rk, LANE:2 * LANE] = (rkk * (u - kh * _pair_mean(u * kh, lo_half))).astype(BF16)
        dz_ref[rk, 2 * LANE:3 * LANE] = jnp.where(lo_half, dvs[0], dvs[1]).astype(BF16)
        dgk_ref[...] += _rowsum8(dkn * kh)
        dfk_s[rk, :] = jnp.where(lane == 2 * p, -dks[0][:, FOX_D:FOX_D + 1],
                                 jnp.where(lane == 2 * p + 1, -dks[1][:, 0:1], 0.0))

        @pl.when(kj == NQ - 1)
        def _():
            def fin(i, carry):
                r = pl.ds(pl.multiple_of(i * tq, tq), tq)
                d0, d1 = dq_s[0, r, :], dq_s[1, r, :]
                dqn = jnp.where(lo_half, d0, d1)
                _, _, _, qh, _, rqq, _ = _fox_prep(z_ref, gq_v, gk_v, r, lo_half)
                u = dqn * gq_v * (FOX_D ** -0.5)
                dz_ref[r, 0:LANE] = (rqq * (u - qh * _pair_mean(u * qh, lo_half))).astype(BF16)
                dgq_ref[...] += _rowsum8(dqn * qh) * (FOX_D ** -0.5)
                dfc_ref[r, :] = dfk_s[r, :] + jnp.where(lane == 2 * p, d0[:, FOX_D:FOX_D + 1],
                                                        jnp.where(lane == 2 * p + 1, d1[:, 0:1], 0.0))
                return carry
            lax.fori_loop(0, NQ, fin, 0)

    vec = pl.BlockSpec((1, LANE), lambda b, p, k: (0, 0))
    col = pl.BlockSpec((T, LANE), lambda b, p, k: (b, p))
    part = pl.BlockSpec((8, LANE), lambda b, p, k: (b * FOX_P + p, 0))
    if nsw:
        body = _hosting(body, 8, 4, 7, nsw, _chip_swap_phases, (B, FOX_P, NQ))
    return pl.pallas_call(
        body, name="fox_bwd", grid=(B, FOX_P, NQ),
        in_specs=[pl.BlockSpec((T, 384), lambda b, p, k: (b, p)), col, col, col,
                  pl.BlockSpec((T, LANE), lambda b, p, k: (b, 0)), pl.BlockSpec((1, 8, T), lambda b, p, k: (b, 0, 0)),
                  vec, vec] + [ANY] * nsw,
        out_specs=[pl.BlockSpec((T, 384), lambda b, p, k: (b, p)), col, part, part] + [ANY] * nsw,
        out_shape=[S((N, 1536), BF16), S((N, 512), F32), S((B * FOX_P * 8, LANE), F32), S((B * FOX_P * 8, LANE), F32)]
        + [S(p.shape, p.dtype) for p in swap],
        scratch_shapes=[pltpu.VMEM((T, LANE), BF16)] * 4 + [pltpu.VMEM((T, LANE), F32), pltpu.VMEM((2, T, LANE), F32),
                                                            pltpu.VMEM((T, LANE), F32)]
        + (_chip_swap_sems(nsw) if nsw else []),
        compiler_params=_cp(("arbitrary",) * 3 if nsw else ("parallel", "parallel", "arbitrary")),
    )(z, dy, y, lse, fc, fct, gq, gk, *swap)


def _mem_scores(z_ref, kv_ref, gq, gk, h):
    c = slice(MEM_D * h, MEM_D * (h + 1))
    q, k = z_ref[:, c].astype(F32), kv_ref[:, c]
    rq = lax.rsqrt(jnp.mean(q * q, axis=-1, keepdims=True) + EPS)
    rk = lax.rsqrt(jnp.mean(k * k, axis=-1, keepdims=True) + EPS)
    qh, kh = q * rq, k * rk
    qn = (qh * gq * (MEM_D ** -0.5)).astype(BF16)
    kn = (kh * gk).astype(BF16)
    s = _nt(qn, kn)
    pe = jnp.exp(s - jnp.max(s, axis=-1, keepdims=True))
    pn = pe / jnp.sum(pe, axis=-1, keepdims=True)
    return pn, qn, kn, qh, kh, rq, rk


def _mem_fwd(z, memkv, gq, gk, B, T, M, tq=1024):
    N = B * T
    tq = min(tq, T)
    NQ = T // tq
    W = MEM_H * MEM_D

    def body(z_ref, kv_ref, gq_ref, gk_ref, y_ref):
        for h in range(MEM_H):
            pn = _mem_scores(z_ref, kv_ref, gq_ref[...], gk_ref[...], h)[0]
            v = kv_ref[:, W + MEM_D * h:W + MEM_D * (h + 1)].astype(BF16)
            y_ref[:, MEM_D * h:MEM_D * (h + 1)] = _nn(pn.astype(BF16), v).astype(BF16)

    vec = pl.BlockSpec((1, LANE), lambda b, q: (0, 0))
    return pl.pallas_call(
        body, name="mem_fwd", grid=(B, NQ),
        in_specs=[pl.BlockSpec((tq, W), lambda b, q: (b * NQ + q, C_MQ // W)),
                  pl.BlockSpec((M, 2 * W), lambda b, q: (b, 0)), vec, vec],
        out_specs=pl.BlockSpec((tq, W), lambda b, q: (b * NQ + q, 0)), out_shape=S((N, W), BF16),
        compiler_params=_cp(("parallel", "parallel")),
    )(z, memkv, gq, gk)


def _mem_bwd(z, memkv, dy, gq, gk, B, T, M, tq=1024):
    N = B * T
    tq = min(tq, T)
    NQ = T // tq
    W = MEM_H * MEM_D

    def body(z_ref, kv_ref, dy_ref, gq_ref, gk_ref, dz_ref, dkv_ref, dgq_ref, dgk_ref, acc):
        qi = pl.program_id(1)
        gq_v, gk_v = gq_ref[...], gk_ref[...]

        @pl.when(qi == 0)
        def _():
            acc[...] = jnp.zeros_like(acc)
            dgq_ref[...] = jnp.zeros_like(dgq_ref)
            dgk_ref[...] = jnp.zeros_like(dgk_ref)

        for h in range(MEM_H):
            c = slice(MEM_D * h, MEM_D * (h + 1))
            cv = slice(W + MEM_D * h, W + MEM_D * (h + 1))
            pn, qn, kn, qh, _, rq, _ = _mem_scores(z_ref, kv_ref, gq_v, gk_v, h)
            do = dy_ref[:, c].astype(BF16)
            dp = _nt(do, kv_ref[:, cv].astype(BF16))
            ds = (pn * (dp - jnp.sum(dp * pn, axis=-1, keepdims=True))).astype(BF16)
            dqn = _nn(ds, kn)
            acc[:, c] += _tn(ds, qn)
            acc[:, cv] += _tn(pn.astype(BF16), do)
            u = dqn * gq_v * (MEM_D ** -0.5)
            dz_ref[:, c] = (rq * (u - qh * jnp.mean(u * qh, axis=-1, keepdims=True))).astype(BF16)
            dgq_ref[...] += _rowsum8(dqn * qh) * (MEM_D ** -0.5)

        @pl.when(qi == NQ - 1)
        def _():
            for h in range(MEM_H):
                c = slice(MEM_D * h, MEM_D * (h + 1))
                cv = slice(W + MEM_D * h, W + MEM_D * (h + 1))
                k = kv_ref[:, c]
                rk = lax.rsqrt(jnp.mean(k * k, axis=-1, keepdims=True) + EPS)
                kh = k * rk
                dkn = acc[:, c]
                u = dkn * gk_v
                dkv_ref[:, c] = (rk * (u - kh * jnp.mean(u * kh, axis=-1, keepdims=True))).astype(BF16)
                dkv_ref[:, cv] = acc[:, cv].astype(BF16)
                dgk_ref[...] += _rowsum8(dkn * kh)

    vec = pl.BlockSpec((1, LANE), lambda b, q: (0, 0))
    part = pl.BlockSpec((8, LANE), lambda b, q: (b, 0))
    return pl.pallas_call(
        body, name="mem_bwd", grid=(B, NQ),
        in_specs=[pl.BlockSpec((tq, W), lambda b, q: (b * NQ + q, C_MQ // W)),
                  pl.BlockSpec((M, 2 * W), lambda b, q: (b, 0)), pl.BlockSpec((tq, W), lambda b, q: (b * NQ + q, 0)),
                  vec, vec],
        out_specs=[pl.BlockSpec((tq, W), lambda b, q: (b * NQ + q, 0)), pl.BlockSpec((M, 2 * W), lambda b, q: (b, 0)),
                   part, part],
        out_shape=[S((N, W), BF16), S((B * M, 2 * W), BF16), S((B * 8, LANE), F32), S((B * 8, LANE), F32)],
        scratch_shapes=[pltpu.VMEM((M, 2 * W), F32)], compiler_params=_cp(("parallel", "arbitrary")),
    )(z, memkv, dy, gq, gk)


def _merge_fwd(ya, yb, yc, z, x, wa, wb, wc, wo, g_next, tm=512):
    n, d = x.shape
    wdt = ya.shape[1]
    gb = C_GATE // d

    def body(ya_ref, yb_ref, yc_ref, g0_ref, g1_ref, g2_ref, x_ref, wa_ref, wb_ref, wc_ref, wo_ref, gn_ref,
             x1_ref, mg_ref, ua_ref, ub_ref, uc_ref, h_ref):
        merged = jnp.zeros((tm, d), F32)
        for y_ref, g_ref, w_ref, u_ref in ((ya_ref, g0_ref, wa_ref, ua_ref), (yb_ref, g1_ref, wb_ref, ub_ref),
                                           (yc_ref, g2_ref, wc_ref, uc_ref)):
            u = _nn(y_ref[...], w_ref[...])
            u_ref[...] = u.astype(BF16)
            merged = merged + jax.nn.sigmoid(g_ref[...].astype(F32)) * u
        mb = merged.astype(BF16)
        mg_ref[...] = mb
        x1 = x_ref[...] + _nn(mb, wo_ref[...])
        x1_ref[...] = x1
        h_ref[...] = (x1 * lax.rsqrt(jnp.mean(x1 * x1, axis=-1, keepdims=True) + EPS) * gn_ref[...]).astype(BF16)

    yt = pl.BlockSpec((tm, wdt), lambda i: (i, 0))
    xt = pl.BlockSpec((tm, d), lambda i: (i, 0))
    wbr = pl.BlockSpec((wdt, d), lambda i: (0, 0))
    gates = [pl.BlockSpec((tm, d), functools.partial(lambda i, k: (i, gb + k), k=k)) for k in range(3)]
    return pl.pallas_call(
        body, name="merge_fwd", grid=(n // tm,),
        in_specs=[yt, yt, yt] + gates + [xt, wbr, wbr, wbr, pl.BlockSpec((d, d), lambda i: (0, 0)),
                                         pl.BlockSpec((1, d), lambda i: (0, 0))],
        out_specs=[xt] * 6, out_shape=[S((n, d), F32)] + [S((n, d), BF16)] * 5, compiler_params=_cp(("parallel",)),
    )(ya, yb, yc, z, z, z, x, wa, wb, wc, wo, g_next)


def _merge_bwd(dx1, z, ua, ub, uc, wa, wb, wc, wo, tm=512):
    n, d = dx1.shape
    wdt = wa.shape[0]
    gb = C_GATE // d

    def body(dx_ref, g0_ref, g1_ref, g2_ref, ua_ref, ub_ref, uc_ref, wa_ref, wb_ref, wc_ref, wo_ref,
             dg_ref, dya_ref, dyb_ref, dyc_ref, dua_ref, dub_ref, duc_ref):
        dm = _nt(dx_ref[...].astype(BF16), wo_ref[...])
        for k, (g_ref, u_ref, w_ref, dy_ref, du_ref) in enumerate((
                (g0_ref, ua_ref, wa_ref, dya_ref, dua_ref), (g1_ref, ub_ref, wb_ref, dyb_ref, dub_ref),
                (g2_ref, uc_ref, wc_ref, dyc_ref, duc_ref))):
            g = jax.nn.sigmoid(g_ref[...].astype(F32))
            du = (dm * g).astype(BF16)
            du_ref[...] = du
            dg_ref[:, d * k:d * (k + 1)] = (dm * u_ref[...].astype(F32) * g * (1.0 - g)).astype(BF16)
            dy_ref[...] = _nt(du, w_ref[...])

    yt = pl.BlockSpec((tm, wdt), lambda i: (i, 0))
    xt = pl.BlockSpec((tm, d), lambda i: (i, 0))
    wbr = pl.BlockSpec((wdt, d), lambda i: (0, 0))
    gates = [pl.BlockSpec((tm, d), functools.partial(lambda i, k: (i, gb + k), k=k)) for k in range(3)]
    return pl.pallas_call(
        body, name="merge_bwd", grid=(n // tm,),
        in_specs=[xt] + gates + [xt, xt, xt, wbr, wbr, wbr, pl.BlockSpec((d, d), lambda i: (0, 0))],
        out_specs=[pl.BlockSpec((tm, 3 * d), lambda i: (i, 0)), yt, yt, yt, xt, xt, xt],
        out_shape=[S((n, 3 * d), BF16)] + [S((n, wdt), F32)] * 3 + [S((n, d), BF16)] * 3,
        compiler_params=_cp(("parallel",)),
    )(dx1, z, z, z, ua, ub, uc, wa, wb, wc, wo)


FFN_TN = 1408
TN_TM = 2048
INV_SQRT2 = 0.7071067811865476
INV_SQRT_2PI = 0.3989422804014327


def _conv_shifted(a, prev, first, tm):
    row = _iota(a.shape, 0)
    p7 = jnp.where(first, 0.0, prev[7:8, :])
    p6 = jnp.where(first, 0.0, prev[6:7, :])
    a1 = jnp.where(row == 0, p7, pltpu.roll(a, 1, 0))
    a2 = jnp.where(row == 0, p6, jnp.where(row == 1, p7, pltpu.roll(a, 2, 0)))
    return a1, a2


def _ffn_act_fwd(up, cw, cb, B, T, tm=1024):
    N = B * T
    tm = min(tm, T)
    dff = cw.shape[1]
    NT, NJ, tn = T // tm, dff // FFN_TN, FFN_TN

    def body(a_ref, v_ref, cw_ref, cb_ref, y_ref, c_ref, carry):
        t = pl.program_id(2)
        a = a_ref[...].astype(F32)
        a1, a2 = _conv_shifted(a, carry[...], t == 0, tm)
        w = cw_ref[...]
        ac = w[0:1, :] * a2 + w[1:2, :] * a1 + w[2:3, :] * a + cb_ref[...]
        cdf = 0.5 * (1.0 + lax.erf(ac * INV_SQRT2))
        y_ref[...] = (ac * cdf * v_ref[...].astype(F32)).astype(BF16)
        c_ref[...] = cdf.astype(BF16)
        carry[...] = a[tm - 8:tm, :]

    return pl.pallas_call(
        body, name="ffn_act_fwd", grid=(B, NJ, NT),
        in_specs=[pl.BlockSpec((tm, tn), lambda b, j, t: (b * NT + t, j)),
                  pl.BlockSpec((tm, tn), lambda b, j, t: (b * NT + t, NJ + j)),
                  pl.BlockSpec((3, tn), lambda b, j, t: (0, j)), pl.BlockSpec((1, tn), lambda b, j, t: (0, j))],
        out_specs=[pl.BlockSpec((tm, tn), lambda b, j, t: (b * NT + t, j))] * 2, out_shape=[S((N, dff), BF16)] * 2,
        scratch_shapes=[pltpu.VMEM((8, tn), F32)], compiler_params=_cp(("parallel", "parallel", "arbitrary")),
    )(up, up, cw, cb)


def _ffn_down_loss(y, wd, x1, tgt, tm=512):
    n, d = x1.shape
    kf = y.shape[1]

    def body(y_ref, w_ref, x_ref, t_ref, dx_ref, ls_ref):
        err = x_ref[...] + _nn(y_ref[...], w_ref[...]) - t_ref[...]
        dx_ref[...] = err * (1.0 / d)

        @pl.when(pl.program_id(0) == 0)
        def _():
            ls_ref[...] = jnp.zeros_like(ls_ref)

        ls_ref[...] += _rowsum8(err * err) * (0.5 / d)

    xt = pl.BlockSpec((tm, d), lambda i: (i, 0))
    return pl.pallas_call(
        body, name="ffn_down_loss", grid=(n // tm,),
        in_specs=[pl.BlockSpec((tm, kf), lambda i: (i, 0)), pl.BlockSpec((kf, d), lambda i: (0, 0)), xt, xt],
        out_specs=[xt, pl.BlockSpec((8, d), lambda i: (0, 0))], out_shape=[S((n, d), F32), S((8, d), F32)],
        compiler_params=_cp(("arbitrary",)),
    )(y, wd, x1, tgt)


def _ffn_act_bwd1(dx2, wd, up, cdf, cw, cb, B, T, tm=512):
    N = B * T
    d = dx2.shape[1]
    dff = cw.shape[1]
    NT, NJ, tn = T // tm, dff // FFN_TN, FFN_TN

    def body(dx_ref, w_ref, a_ref, v_ref, c_ref, cw_ref, cb_ref, dac_ref, dv_ref, dcw_ref, dcb_ref, carry):
        b, t = pl.program_id(1), pl.program_id(2)
        a = a_ref[...].astype(F32)
        a1, a2 = _conv_shifted(a, carry[...], t == 0, tm)
        carry[...] = a[tm - 8:tm, :]
        w = cw_ref[...]
        ac = w[0:1, :] * a2 + w[1:2, :] * a1 + w[2:3, :] * a + cb_ref[...]
        dy = _nt(dx_ref[...].astype(BF16), w_ref[...])
        cdf = c_ref[...].astype(F32)
        dv_ref[...] = (dy * ac * cdf).astype(BF16)
        dac = dy * v_ref[...].astype(F32) * (cdf + ac * jnp.exp(-0.5 * ac * ac) * INV_SQRT_2PI)
        dac_ref[...] = dac

        @pl.when((b == 0) & (t == 0))
        def _():
            dcw_ref[...] = jnp.zeros_like(dcw_ref)
            dcb_ref[...] = jnp.zeros_like(dcb_ref)

        dcw_ref[0:8, :] += _rowsum8(dac * a2)
        dcw_ref[8:16, :] += _rowsum8(dac * a1)
        dcw_ref[16:24, :] += _rowsum8(dac * a)
        dcb_ref[...] += _rowsum8(dac)

    return pl.pallas_call(
        body, name="ffn_act_bwd1", grid=(NJ, B, NT),
        in_specs=[pl.BlockSpec((tm, d), lambda j, b, t: (b * NT + t, 0)), pl.BlockSpec((tn, d), lambda j, b, t: (j, 0)),
                  pl.BlockSpec((tm, tn), lambda j, b, t: (b * NT + t, j)),
                  pl.BlockSpec((tm, tn), lambda j, b, t: (b * NT + t, NJ + j)),
                  pl.BlockSpec((tm, tn), lambda j, b, t: (b * NT + t, j)),
                  pl.BlockSpec((3, tn), lambda j, b, t: (0, j)), pl.BlockSpec((1, tn), lambda j, b, t: (0, j))],
        out_specs=[pl.BlockSpec((tm, tn), lambda j, b, t: (b * NT + t, j)),
                   pl.BlockSpec((tm, tn), lambda j, b, t: (b * NT + t, j)),
                   pl.BlockSpec((24, tn), lambda j, b, t: (0, j)), pl.BlockSpec((8, tn), lambda j, b, t: (0, j))],
        out_shape=[S((N, dff), F32), S((N, dff), BF16), S((24, dff), F32), S((8, dff), F32)],
        scratch_shapes=[pltpu.VMEM((8, tn), F32)], compiler_params=_cp(("parallel", "arbitrary", "arbitrary")),
    )(dx2, wd, up, up, cdf, cw, cb)


def _ffn_act_bwd2(dac, cw, B, T, tm=1024):
    N = B * T
    tm = min(tm, T)
    dff = cw.shape[1]
    NT, NJ, tn = T // tm, dff // FFN_TN, FFN_TN
    last8 = N // 8 - 1

    def body(d_ref, nx_ref, cw_ref, da_ref):
        t = pl.program_id(2)
        dd = d_ref[...]
        row = _iota(dd.shape, 0)
        last = t == NT - 1
        n0 = jnp.where(last, 0.0, nx_ref[0:1, :])
        n1 = jnp.where(last, 0.0, nx_ref[1:2, :])
        d1 = jnp.where(row == tm - 1, n0, pltpu.roll(dd, tm - 1, 0))
        d2 = jnp.where(row == tm - 1, n1, jnp.where(row == tm - 2, n0, pltpu.roll(dd, tm - 2, 0)))
        w = cw_ref[...]
        da_ref[...] = (w[2:3, :] * dd + w[1:2, :] * d1 + w[0:1, :] * d2).astype(BF16)

    return pl.pallas_call(
        body, name="ffn_act_bwd2", grid=(B, NJ, NT),
        in_specs=[pl.BlockSpec((tm, tn), lambda b, j, t: (b * NT + t, j)),
                  pl.BlockSpec((8, tn), lambda b, j, t: (jnp.minimum((b * NT + t + 1) * (tm // 8), last8), j)),
                  pl.BlockSpec((3, tn), lambda b, j, t: (0, j))],
        out_specs=pl.BlockSpec((tm, tn), lambda b, j, t: (b * NT + t, j)), out_shape=S((N, dff), BF16),
        compiler_params=_cp(("parallel", "parallel", "parallel")),
    )(dac, dac, cw)


def _fold_rows(p, name):
    r, c = p.shape[0] // 8, p.shape[1]

    def body(p_ref, o_ref):
        for j in range(r):
            o_ref[j:j + 1, :] = jnp.sum(p_ref[8 * j:8 * (j + 1), :], axis=0, keepdims=True)

    return pl.pallas_call(body, name=name, out_shape=S((r, c), F32), compiler_params=_cp())(p)


def _small_reduce(lbl, dg_mix, dg_mem, dlb_p, dgn_p, dfb_p, dgq_p, dgk_p, dmq_p, dmk_p, dg_ffn, dcb_p, loss_p):
    d, dff = dg_mix.shape[1], dcb_p.shape[1]
    nbh = dlb_p.shape[0] // (8 * HG_H)

    def colsum(ref):
        return jnp.sum(ref[...], axis=0, keepdims=True)

    def body(lbl_ref, mix_ref, mem_ref, dlb_ref, dgn_ref, dfb_ref, dgq_ref, dgk_ref, dmq_ref, dmk_ref, ffn_ref, dcb_ref,
             ls_ref, o_mix, o_mem, o_lb, o_hgn, o_fb, o_fq, o_fk, o_mq, o_mk, o_ffn, o_cb, o_loss):
        o_mix[...], o_mem[...], o_ffn[...], o_cb[...] = colsum(mix_ref), colsum(mem_ref), colsum(ffn_ref), colsum(dcb_ref)
        o_hgn[...], o_fb[...], o_mq[...], o_mk[...] = colsum(dgn_ref), colsum(dfb_ref), colsum(dmq_ref), colsum(dmk_ref)
        for src, dst in ((dgq_ref, o_fq), (dgk_ref, o_fk)):
            v = colsum(src)
            dst[...] = v + pltpu.roll(v, FOX_D, 1)
        o_loss[...] = jnp.zeros((1, LANE), F32) + jnp.sum(colsum(ls_ref), axis=-1, keepdims=True)
        logits = lbl_ref[...]
        e = jnp.exp(logits - jnp.max(logits, axis=0, keepdims=True))
        pr = e / jnp.sum(e, axis=0, keepdims=True)
        rows = _iota((8, LANE), 0)
        for h in range(HG_H):
            acc = jnp.zeros((8, LANE), F32)
            for b in range(nbh):
                acc = acc + dlb_ref[8 * (b * HG_H + h):8 * (b * HG_H + h + 1), :]
            dlb = jnp.sum(acc, axis=0, keepdims=True)
            c = slice(LANE * h, LANE * (h + 1))
            p0 = pr[0:1, c]
            first = _iota((logits.shape[0], LANE), 0) == 0
            o_lb[:, c] = pr[:, c] * (jnp.where(first, 1.0, 0.0) - p0) * dlb

    outs = [S((1, d), F32), S((1, d), F32), S(lbl.shape, F32)] + [S((1, LANE), F32)] * 6 + \
           [S((1, d), F32), S((1, dff), F32), S((1, LANE), F32)]
    return pl.pallas_call(body, name="small_reduce", out_shape=outs, compiler_params=_cp())(
        lbl, dg_mix, dg_mem, dlb_p, dgn_p, dfb_p, dgq_p, dgk_p, dmq_p, dmk_p, dg_ffn, dcb_p, loss_p)


def _in_col_pieces():
    hw, fw = HG_H * HG_D, FOX_H * FOX_D
    fox0, ff0 = 4 * hw, 4 * hw + 3 * fw
    mq0 = ff0 + FOX_H
    gate0 = mq0 + MEM_H * MEM_D
    pieces = []
    for p in range(FOX_P):
        pieces += [(fox0 + j * fw + LANE * p, LANE) for j in range(3)]
    pieces.append((mq0, MEM_H * MEM_D))
    for h in range(HG_H):
        pieces += [(j * hw + HG_D * h, HG_D) for j in range(4)]
    pieces.append((gate0, C_FF - C_GATE))
    pieces.append((ff0, FOX_H))
    return pieces


def _perm_from_blocks(blocks):
    n_blk, _, c = blocks.shape
    parts = []
    for s, n in _in_col_pieces():
        lo = s
        while lo < s + n:
            d = lo // c
            hi = min(s + n, (d + 1) * c)
            parts.append(blocks[d][:, lo - d * c:hi - d * c])
            lo = hi
    parts.append(jnp.zeros((blocks.shape[1], C_END - C_FF - FOX_H), blocks.dtype))
    return jnp.concatenate(parts, axis=1)


def _unperm_blocks(segs, n_blk):
    starts = [0]
    for a in segs:
        starts.append(starts[-1] + a.shape[1])
    new_start, placed = 0, []
    for s, n in _in_col_pieces():
        placed.append((s, new_start, n))
        new_start += n
    placed.sort()
    c = sum(n for _, _, n in placed) // n_blk
    blocks = []
    for d in range(n_blk):
        parts = []
        for s, ns, n in placed:
            lo, hi = max(s, d * c), min(s + n, (d + 1) * c)
            if lo < hi:
                i = max(j for j in range(len(segs)) if starts[j] <= ns)
                parts.append(segs[i][:, ns + lo - s - starts[i]:ns + hi - s - starts[i]])
        blocks.append(jnp.concatenate(parts, axis=1))
    return jnp.stack(blocks)


def _local_step(x2, mem2, tgt, sm, W, B, T, M, ex=None):
    fbias = jnp.pad(sm["fox_f_bias"], ((0, 0), (0, LANE - FOX_H)))
    gq2 = jnp.concatenate([sm["fox_q_norm_g"]] * 2, axis=1)
    gk2 = jnp.concatenate([sm["fox_k_norm_g"]] * 2, axis=1)
    lbl = sm["hgrn_lb_logits"]
    h = _rmsnorm_cast(x2, sm["norm_mix_g"], "norm_mix")
    z = _mm_nn(h, W["w_in"], BF16, "proj_in", 512, 2432)
    memn = _rmsnorm_cast(mem2, sm["norm_mem_g"], "norm_mem", tm=256)
    memkv = _mm_nn(memn, W["mem_kv_w"], F32, "proj_memkv", 256, 512)
    ya, o_raw, states, a_mat = _hgrn_fwd(z, lbl, sm["hgrn_norm_g"], B, T)
    fc, fct = _fox_gate_fwd(z, fbias, B, T)
    yb, lse, *late = _fox_fwd(z, fc, fct, gq2, gk2, B, T, gather=ex.late_blocks() if ex else ())
    if ex:
        W = {**W, **ex.unpack_late(late)}
    yc = _mem_fwd(z, memkv, sm["mem_q_norm_g"], sm["mem_k_norm_g"], B, T, M)
    x1, merged, ua, ub, uc, h2 = _merge_fwd(ya, yb, yc, z, x2, W["w_br_hgrn"], W["w_br_fox"], W["w_br_mem"], W["w_out"],
                                            sm["norm_ffn_g"])
    up = _mm_nn(h2, W["ffn_w_up"], BF16, "ffn_up", 512, FFN_TN)
    yf, cdf = _ffn_act_fwd(up, W["ffn_conv_w"], sm["ffn_conv_b"], B, T)
    dx2, loss_p = _ffn_down_loss(yf, W["ffn_w_down"], x1, tgt)
    dff = W["ffn_conv_w"].shape[1]
    dac, dv, dcw_p, dcb_p = _ffn_act_bwd1(dx2, W["ffn_w_down"], up, cdf, W["ffn_conv_w"], sm["ffn_conv_b"], B, T)
    da = _ffn_act_bwd2(dac, W["ffn_conv_w"], B, T)
    g = {"ffn_conv_w": _fold_rows(dcw_p, "g_conv_w")}
    g["ffn_w_down"] = _mm_tn(yf, dx2, "g_w_down", TN_TM, 512)
    dh2 = _mm_nt_sum([(da, 0, dff, 0), (dv, 0, dff, dff)], W["ffn_w_up"], "dh2", 512)
    g["ffn_w_up"] = [_mm_tn(h2, da, "g_w_up_a", TN_TM, FFN_TN), _mm_tn(h2, dv, "g_w_up_v", TN_TM, FFN_TN)]
    dx1, dg_ffn = _rmsnorm_bwd(dh2, x1, sm["norm_ffn_g"], dx2, "norm_ffn_bwd")
    g["w_out"] = _mm_tn(merged, dx1, "g_w_out", TN_TM, 512)
    dgate, dya, dyb, dyc, dua, dub, duc = _merge_bwd(dx1, z, ua, ub, uc, W["w_br_hgrn"], W["w_br_fox"], W["w_br_mem"],
                                                    W["w_out"])
    g["w_br_hgrn"] = _mm_tn(ya, dua, "g_w_br_hgrn", TN_TM, 512)
    g["w_br_fox"] = _mm_tn(yb, dub, "g_w_br_fox", TN_TM, 512)
    g["w_br_mem"] = _mm_tn(yc, duc, "g_w_br_mem", TN_TM, 512)
    early_pk = ex.early_grads(g) if ex else ()
    dz_hg, dlb_p, dgn_p, *early_sib = _hgrn_bwd(z, o_raw, states, a_mat, dya, lbl, sm["hgrn_norm_g"], B, T,
                                                swap_sibling=early_pk)
    dz_fox, dfc, dgq_p, dgk_p, *early_chips = _fox_bwd(z, dyb, yb, lse, fc, fct, gq2, gk2, B, T,
                                                       swap=ex.pair_sums(early_pk, early_sib, "early") if ex else ())
    dz_ff, dfb_p = _fox_gate_bwd(dfc, z, fbias, B, T)
    dz_mq, dkv, dmq_p, dmk_p = _mem_bwd(z, memkv, dyc, sm["mem_q_norm_g"], sm["mem_k_norm_g"], B, T, M)
    g["mem_kv_w"] = _mm_tn(memn, dkv, "g_mem_kv_w", 256, 512)
    dmemn = _mm_nt_sum([(dkv, 0, dkv.shape[1], 0)], W["mem_kv_w"], "d_memn", 256)
    _, dg_mem = _rmsnorm_bwd(dmemn, mem2, sm["norm_mem_g"], None, "norm_mem_bwd", tm=256)
    d = x2.shape[1]
    parts = [(dz_fox, 0, C_MQ - C_FOX, C_FOX), (dz_mq, 0, C_HG - C_MQ, C_MQ), (dz_hg, 0, C_GATE - C_HG, C_HG)]
    parts += [(dgate, d * k, d, C_GATE + d * k) for k in range(3)] + [(dz_ff, 0, C_END - C_FF, C_FF)]
    g["w_in"] = [_mm_tn(h, dzs, "g_w_in_%d" % i, 2 * TN_TM, min(512, dzs.shape[1]))
                 for i, dzs in enumerate((dz_fox, dz_mq, dz_hg, dgate, dz_ff))]
    sums = None
    if ex:
        last_pk = ex.last_grads(g)
        last_sib = _swap_with_sibling(last_pk, "rs_sibling_last")
        dh, last_chips = _mm_nt_sum(parts, W["w_in"], "dh", 512, swap=ex.pair_sums(last_pk, last_sib, "last"))
        sums = (ex.final_sums(early_pk, early_sib, early_chips, "early"),
                ex.final_sums(last_pk, last_sib, last_chips, "last"))
    else:
        dh = _mm_nt_sum(parts, W["w_in"], "dh", 512)
    grad_x, dg_mix = _rmsnorm_bwd(dh, x2, sm["norm_mix_g"], dx1, "norm_mix_bwd")
    small = _small_reduce(lbl, dg_mix, dg_mem, dlb_p, dgn_p, dfb_p, dgq_p, dgk_p, dmq_p, dmk_p, dg_ffn, dcb_p, loss_p)
    names = ("norm_mix_g", "norm_mem_g", "hgrn_lb_logits", "hgrn_norm_g", "fox_f_bias", "fox_q_norm_g", "fox_k_norm_g",
             "mem_q_norm_g", "mem_k_norm_g", "norm_ffn_g", "ffn_conv_b", "loss")
    g.update(dict(zip(names, small)))
    return grad_x, g, sums


ANY = pl.BlockSpec(memory_space=pl.ANY)


def _position():
    return lax.axis_index("x"), lax.axis_index("y"), lax.axis_index("c")


def _all_gather(blocks, name):
    nb = len(blocks)

    def body(*refs):
        start, forward, finish = _gather_phases(refs[:nb], refs[nb:2 * nb], *refs[2 * nb:])
        start()
        forward()
        finish()

    return pl.pallas_call(
        body, name=name, out_shape=_gather_shapes(blocks), in_specs=[ANY] * nb, out_specs=[ANY] * nb,
        scratch_shapes=_gather_sems(nb),
    )(*blocks)


def _hosting(body, n_in, n_out, n_scratch, n_x, make_phases, grid):
    n_steps = math.prod(grid)

    def hosted(*refs):
        a = n_in + n_x
        b = a + n_out + n_x
        ins, xs = refs[:n_in], refs[n_in:a]
        outs, x_outs = refs[a:a + n_out], refs[a + n_out:b]
        scratch, sems = refs[b:b + n_scratch], refs[b + n_scratch:]
        step = 0
        for ax, n in enumerate(grid):
            step = step * n + pl.program_id(ax)
        phases = make_phases(xs, x_outs, *sems)
        pl.when(step == 0)(phases[0])
        for ph in phases[1:-1]:
            pl.when(step == n_steps // 2)(ph)
        body(*ins, *outs, *scratch)
        pl.when(step == n_steps - 1)(phases[-1])

    return hosted


def _gather_shapes(blocks):
    return [S((N_DEV,) + b.shape, b.dtype) for b in blocks]


def _gather_sems(nb):
    return [pltpu.SemaphoreType.DMA((7 * nb,)), pltpu.SemaphoreType.DMA((7 * nb,)), pltpu.SemaphoreType.DMA((nb,))]


def _gather_phases(x_refs, out_refs, send_sems, recv_sems, local_sems):
    nb = len(x_refs)
    x, y, c = _position()
    me, sibling = (x, y, c), (x, y, 1 - c)
    chips = [(1 - x, y), (x, 1 - y), (1 - x, 1 - y)]

    def copy(i, k, blk, to, own=False):
        px, py, pc = blk
        slot = out_refs[i].at[4 * px + 2 * py + pc]
        return pltpu.make_async_remote_copy(
            src_ref=x_refs[i] if own else slot, dst_ref=slot, send_sem=send_sems.at[7 * i + k],
            recv_sem=recv_sems.at[7 * i + k], device_id=to, device_id_type=MESH)

    def mine(i):
        return pltpu.make_async_copy(x_refs[i], out_refs[i].at[4 * x + 2 * y + c], local_sems.at[i])

    def first(i):
        return [copy(i, 0, me, sibling, own=True)] + [copy(i, 1 + j, me, (*chip, c), own=True)
                                                     for j, chip in enumerate(chips)]

    def passed(i, j):
        return copy(i, 4 + j, (*chips[j], c), sibling)

    def start():
        for i in range(nb):
            mine(i).start()
            for cp in first(i):
                cp.start()

    def forward():
        for i in range(nb):
            for j, chip in enumerate(chips):
                copy(i, 1 + j, (*chip, c), me).wait_recv()
                passed(i, j).start()

    def finish():
        for i in range(nb):
            copy(i, 0, sibling, me).wait_recv()
            for j, chip in enumerate(chips):
                copy(i, 4 + j, (*chip, 1 - c), me).wait_recv()
        for i in range(nb):
            for cp in first(i) + [passed(i, j) for j in range(3)]:
                cp.wait_send()
            mine(i).wait()

    return start, forward, finish


def _swap_with_sibling(pks, name):
    nb = len(pks)

    def body(*refs):
        start, finish = _sibling_swap_phases(refs[:nb], refs[nb:2 * nb], *refs[2 * nb:])
        start()
        finish()

    return pl.pallas_call(
        body, name=name, out_shape=_sibling_swap_shapes(pks), in_specs=[ANY] * nb, out_specs=[ANY] * nb,
        scratch_shapes=_sibling_swap_sems(nb),
    )(*pks)


def _sibling_swap_shapes(pks):
    return [S((4,) + p.shape[1:], p.dtype) for p in pks]


def _sibling_swap_sems(nb):
    return [pltpu.SemaphoreType.DMA((4 * nb,)), pltpu.SemaphoreType.DMA((4 * nb,))]


def _sibling_swap_phases(pk_refs, out_refs, send_sems, recv_sems):
    nb = len(pk_refs)
    x, y, c = _position()

    def copies():
        return [pltpu.make_async_remote_copy(
            src_ref=pk_refs[i].at[2 * k + 1 - c], dst_ref=out_refs[i].at[k], send_sem=send_sems.at[4 * i + k],
            recv_sem=recv_sems.at[4 * i + k], device_id=(x, y, 1 - c), device_id_type=MESH)
            for i in range(nb) for k in range(4)]

    def start():
        for cp in copies():
            cp.start()

    def finish():
        for cp in copies():
            cp.wait()

    return start, finish


def _swap_between_chips(pbs, name):
    nb = len(pbs)

    def body(*refs):
        start, finish = _chip_swap_phases(refs[:nb], refs[nb:2 * nb], *refs[2 * nb:])
        start()
        finish()

    return pl.pallas_call(
        body, name=name, out_shape=[S(p.shape, p.dtype) for p in pbs], in_specs=[ANY] * nb, out_specs=[ANY] * nb,
        scratch_shapes=_chip_swap_sems(nb),
    )(*pbs)


def _chip_swap_sems(nb):
    return [pltpu.SemaphoreType.DMA((3 * nb,)), pltpu.SemaphoreType.DMA((3 * nb,)), pltpu.SemaphoreType.DMA((nb,))]


def _chip_swap_phases(pb_refs, out_refs, send_sems, recv_sems, local_sems):
    nb = len(pb_refs)
    x, y, c = _position()
    me = 2 * x + y
    chips = [(1 - x, y), (x, 1 - y), (1 - x, 1 - y)]

    def local(i):
        return pltpu.make_async_copy(pb_refs[i].at[me], out_refs[i].at[me], local_sems.at[i])

    def send(i, j):
        cx, cy = chips[j]
        return pltpu.make_async_remote_copy(
            src_ref=pb_refs[i].at[2 * cx + cy], dst_ref=out_refs[i].at[me], send_sem=send_sems.at[3 * i + j],
            recv_sem=recv_sems.at[3 * i + j], device_id=(cx, cy, c), device_id_type=MESH)

    def arrival(i, j):
        cx, cy = chips[j]
        return pltpu.make_async_remote_copy(
            src_ref=pb_refs[i].at[me], dst_ref=out_refs[i].at[2 * cx + cy], send_sem=send_sems.at[3 * i + j],
            recv_sem=recv_sems.at[3 * i + j], device_id=(cx, cy, c), device_id_type=MESH)

    def start():
        for i in range(nb):
            local(i).start()
            for j in range(3):
                send(i, j).start()

    def finish():
        for i in range(nb):
            for j in range(3):
                arrival(i, j).wait_recv()
        for i in range(nb):
            for j in range(3):
                send(i, j).wait_send()
            local(i).wait()

    return start, finish


def _row_tile(r):
    return max(t for t in range(16, min(r, 512) + 1, 16) if r % t == 0)


def _pair_sum_cast(pk, recv, core, name):
    _, r, l = pk.shape
    tr = _row_tile(r)

    def body(c_ref, a_ref, b_ref, o_ref):
        o_ref[...] = (a_ref[...] + b_ref[...]).astype(BF16)

    return pl.pallas_call(
        body, name=name,
        grid_spec=pltpu.PrefetchScalarGridSpec(
            num_scalar_prefetch=1, grid=(4, r // tr),
            in_specs=[pl.BlockSpec((None, tr, l), lambda k, i, c: (2 * k + c[0], i, 0)),
                      pl.BlockSpec((None, tr, l), lambda k, i, c: (k, i, 0))],
            out_specs=pl.BlockSpec((None, tr, l), lambda k, i, c: (k, i, 0))),
        out_shape=S((4, r, l), BF16), compiler_params=_cp(("parallel", "parallel")),
    )(core, pk, recv)


def _final_sum(pk, recv_sib, recv_chips, slot, chip, name):
    _, r, l = pk.shape
    tr = _row_tile(r)

    def body(s_ref, k_ref, a_ref, b_ref, rc_ref, o_ref):
        base = a_ref[...] + b_ref[...]
        acc = jnp.zeros_like(base)
        for j in range(4):
            acc = acc + jnp.where(k_ref[0] == j, base, rc_ref[j].astype(F32))
        o_ref[...] = acc

    return pl.pallas_call(
        body, name=name,
        grid_spec=pltpu.PrefetchScalarGridSpec(
            num_scalar_prefetch=2, grid=(r // tr,),
            in_specs=[pl.BlockSpec((None, tr, l), lambda i, s, k: (s[0], i, 0)),
                      pl.BlockSpec((None, tr, l), lambda i, s, k: (k[0], i, 0)),
                      pl.BlockSpec((4, tr, l), lambda i, s, k: (0, i, 0))],
            out_specs=pl.BlockSpec((tr, l), lambda i, s, k: (i, 0))),
        out_shape=S((r, l), F32), compiler_params=_cp(("parallel",)),
    )(slot, chip, pk, recv_sib, recv_chips)


def _adamw_math(w, g, m, v):
    m = ADAM_B1 * m + (1.0 - ADAM_B1) * g
    v = ADAM_B2 * v + (1.0 - ADAM_B2) * (g * g)
    m_hat = m / (1.0 - ADAM_B1 ** ADAM_STEP)
    v_hat = v / (1.0 - ADAM_B2 ** ADAM_STEP)
    return -ADAM_LR * (m_hat / (jnp.sqrt(v_hat) + ADAM_EPS) + ADAM_WD * w), m, v


def _adamw(w, g, m, v, name):
    r, c = w.shape
    tr = 256 if r % 256 == 0 else r

    def body(w_ref, g_ref, m_ref, v_ref, d_ref, nm_ref, nv_ref):
        d_ref[...], nm_ref[...], nv_ref[...] = _adamw_math(w_ref[...], g_ref[...], m_ref[...], v_ref[...])

    tile = pl.BlockSpec((tr, c), lambda i: (i, 0))
    return pl.pallas_call(
        body, name=name, grid=(r // tr,), in_specs=[tile] * 4, out_specs=[tile] * 3, out_shape=[S((r, c), F32)] * 3,
        compiler_params=_cp(("parallel",)),
    )(w, g, m, v)


def _small_update(gathered, w, m, v):
    def body(ga_ref, w_ref, m_ref, v_ref, g_ref, d_ref, nm_ref, nv_ref):
        g = ga_ref[0]
        for k in range(1, N_DEV):
            g = g + ga_ref[k]
        g_ref[...] = g
        d_ref[...], nm_ref[...], nv_ref[...] = _adamw_math(w_ref[...], g, m_ref[...], v_ref[...])

    return pl.pallas_call(body, name="small_update", out_shape=[S(w.shape, F32)] * 4, compiler_params=_cp())(
        gathered, w, m, v)


BIG = ("w_in", "mem_kv_w", "w_br_hgrn", "w_br_fox", "w_br_mem", "w_out", "ffn_w_up", "ffn_conv_w", "ffn_w_down")
GROUP_ROWS = ("w_out", "ffn_w_down")
GROUP_LANE = ("w_br_hgrn", "w_br_fox", "w_br_mem")
LANE_GROUP_ROWS = 224
SMALL = ("norm_mix_g", "norm_mem_g", "hgrn_lb_logits", "hgrn_norm_g", "fox_f_bias", "fox_q_norm_g", "fox_k_norm_g",
         "mem_q_norm_g", "mem_k_norm_g", "norm_ffn_g", "ffn_conv_b")


def _rows_of(n_elems):
    return -(-n_elems // LANE)


def _to_rows(a, lead=0):
    flat = a.reshape(a.shape[:lead] + (-1,))
    pad = (-flat.shape[-1]) % LANE
    if pad:
        flat = jnp.pad(flat, [(0, 0)] * lead + [(0, pad)])
    return flat.reshape(a.shape[:lead] + (-1, LANE))


def _stack_rows(parts, lead, total_rows):
    buf = jnp.concatenate(parts, axis=lead)
    pad = total_rows - buf.shape[lead]
    return jnp.pad(buf, [(0, 0)] * lead + [(0, pad), (0, 0)])


def _round_up(n, k):
    return -(-n // k) * k


def _from_rows(rows, shape, lead=0):
    n = math.prod(shape)
    return rows.reshape(rows.shape[:lead] + (-1,))[..., :n].reshape(rows.shape[:lead] + tuple(shape))


def _blocks_to_full(blocks, kind):
    n, a, b = blocks.shape
    return blocks.transpose(1, 0, 2).reshape(a, n * b) if kind == "col" else blocks.reshape(n * a, b)


def _full_to_blocks(full, kind, n=N_DEV):
    a, b = full.shape
    return full.reshape(a, n, b // n).transpose(1, 0, 2) if kind == "col" else full.reshape(n, a // n, b)


def _lane_group_rows(shard):
    n_lane = sum(shard[n].shape[0] for n in GROUP_LANE)
    n_cw = shard["ffn_conv_w"].size
    return n_lane, _rows_of(3 * n_cw), _rows_of(n_cw), _round_up(n_lane + _rows_of(3 * n_cw), LANE_GROUP_ROWS)


def _split_bf16x3(x):
    hi = x.astype(BF16)
    r1 = x - hi.astype(F32)
    mid = r1.astype(BF16)
    return jnp.stack([hi, mid, (r1 - mid.astype(F32)).astype(BF16)])


class _Exchange:
    def __init__(self, shard):
        self.shard = shard
        xi, yi, ci = _position()
        self.core = ci.astype(jnp.int32).reshape(1)
        self.chip = (2 * xi + yi).astype(jnp.int32).reshape(1)
        self.n_lane, self.r_pieces, self.r_vals, self.r_lane = _lane_group_rows(shard)

    def first_blocks(self):
        return [self.shard["w_in"].astype(BF16), self.shard["mem_kv_w"].astype(BF16)]

    def unpack_first(self, gathered):
        return {"w_in": _perm_from_blocks(gathered[0]), "mem_kv_w": _blocks_to_full(gathered[1], "row")}

    def late_blocks(self):
        sh = self.shard
        lane_rows = [sh[n].astype(BF16) for n in GROUP_LANE] + [_to_rows(_split_bf16x3(sh["ffn_conv_w"]))]
        return [sh[n].astype(BF16) for n in GROUP_ROWS] + [sh["ffn_w_up"].astype(BF16),
                                                           _stack_rows(lane_rows, 0, self.r_lane)]

    def unpack_late(self, gathered):
        *rows, gc, gd = gathered
        sh = self.shard
        W = {"ffn_w_up": _blocks_to_full(gc, "col")}
        for n, blocks in zip(GROUP_ROWS, rows):
            W[n] = _blocks_to_full(blocks, "row")
        r0 = 0
        for n in GROUP_LANE:
            W[n] = _blocks_to_full(gd[:, r0:r0 + sh[n].shape[0]], "col")
            r0 += sh[n].shape[0]
        cw = _from_rows(gd[:, self.n_lane:self.n_lane + self.r_pieces], (3,) + sh["ffn_conv_w"].shape, lead=1).astype(F32)
        W["ffn_conv_w"] = _blocks_to_full(cw[:, 0] + cw[:, 1] + cw[:, 2], "col")
        return W

    def early_grads(self, g):
        cw_rows = _to_rows(_full_to_blocks(g["ffn_conv_w"], "col"), lead=1)
        return [_full_to_blocks(g[n], "row") for n in GROUP_ROWS] + [
            jnp.concatenate([_full_to_blocks(h, "col", N_DEV // 2) for h in g["ffn_w_up"]], axis=0),
            _stack_rows([_full_to_blocks(g[n], "col") for n in GROUP_LANE] + [cw_rows], 1, self.r_lane)]

    def last_grads(self, g):
        return [_unperm_blocks(g["w_in"], N_DEV), _full_to_blocks(g["mem_kv_w"], "row")]

    def pair_sums(self, pks, recv_sib, tag):
        return [_pair_sum_cast(p, r, self.core, "rs_pair_sum_%s%d" % (tag, i))
                for i, (p, r) in enumerate(zip(pks, recv_sib))]

    def final_sums(self, pks, recv_sib, recv_chips, tag):
        return [_final_sum(p, rs, rc, 2 * self.chip + self.core, self.chip, "rs_final_sum_%s%d" % (tag, i))
                for i, (p, rs, rc) in enumerate(zip(pks, recv_sib, recv_chips))]

    def unpack_grads(self, early, last):
        sh = self.shard
        *rows, g_up, g_lane = early
        g_shard = {"w_in": last[0], "mem_kv_w": last[1], "ffn_w_up": g_up, **dict(zip(GROUP_ROWS, rows))}
        r0 = 0
        for n in GROUP_LANE:
            g_shard[n] = g_lane[r0:r0 + sh[n].shape[0]]
            r0 += sh[n].shape[0]
        g_shard["ffn_conv_w"] = _from_rows(g_lane[self.n_lane:self.n_lane + self.r_vals], sh["ffn_conv_w"].shape)
        return g_shard


def kernel(x, mem, norm_mix_g, norm_mem_g, w_in, hgrn_lb_logits, hgrn_norm_g, fox_f_bias, fox_q_norm_g, fox_k_norm_g, mem_kv_w, mem_q_norm_g, mem_k_norm_g, w_br_hgrn, w_br_fox, w_br_mem, w_out, norm_ffn_g, ffn_w_up, ffn_conv_w, ffn_conv_b, ffn_w_down, loss_target, m_norm_mix_g, m_norm_mem_g, m_w_in, m_hgrn_lb_logits, m_hgrn_norm_g, m_fox_f_bias, m_fox_q_norm_g, m_fox_k_norm_g, m_mem_kv_w, m_mem_q_norm_g, m_mem_k_norm_g, m_w_br_hgrn, m_w_br_fox, m_w_br_mem, m_w_out, m_norm_ffn_g, m_ffn_w_up, m_ffn_conv_w, m_ffn_conv_b, m_ffn_w_down, v_norm_mix_g, v_norm_mem_g, v_w_in, v_hgrn_lb_logits, v_hgrn_norm_g, v_fox_f_bias, v_fox_q_norm_g, v_fox_k_norm_g, v_mem_kv_w, v_mem_q_norm_g, v_mem_k_norm_g, v_w_br_hgrn, v_w_br_fox, v_w_br_mem, v_w_out, v_norm_ffn_g, v_ffn_w_up, v_ffn_conv_w, v_ffn_conv_b, v_ffn_w_down):
    given = dict(locals())
    order = ("norm_mix_g", "norm_mem_g", "w_in", "hgrn_lb_logits", "hgrn_norm_g", "fox_f_bias", "fox_q_norm_g",
             "fox_k_norm_g", "mem_kv_w", "mem_q_norm_g", "mem_k_norm_g", "w_br_hgrn", "w_br_fox", "w_br_mem", "w_out",
             "norm_ffn_g", "ffn_w_up", "ffn_conv_w", "ffn_conv_b", "ffn_w_down")
    B, T, D = x.shape
    M = mem.shape[1]
    shard = {n: given[n][0] if n in BIG else given[n] for n in order}
    mom = {n: (given["m_" + n][0], given["v_" + n][0]) if n in BIG else (given["m_" + n], given["v_" + n])
           for n in order}
    shard["hgrn_lb_logits"] = hgrn_lb_logits
    for n in ("norm_mix_g", "norm_mem_g", "hgrn_norm_g", "fox_f_bias", "fox_q_norm_g", "fox_k_norm_g", "mem_q_norm_g",
              "mem_k_norm_g", "norm_ffn_g", "ffn_conv_b"):
        shard[n] = given[n].reshape(1, -1)

    ex = _Exchange(shard)
    W = ex.unpack_first(_all_gather(ex.first_blocks(), "ag_first"))

    sm = {n: shard[n] for n in SMALL}
    grad_x, g, sums = _local_step(x.reshape(B * T, D), mem.reshape(B * M, D), loss_target.reshape(B * T, D), sm, W,
                                  B, T, M, ex)
    g_shard = ex.unpack_grads(*sums)

    sg = {n: g[n] for n in SMALL}
    sg["fox_f_bias"] = g["fox_f_bias"][:, :FOX_H]
    sg["fox_q_norm_g"] = g["fox_q_norm_g"][:, :FOX_D]
    sg["fox_k_norm_g"] = g["fox_k_norm_g"][:, :FOX_D]
    slayout, row0 = {}, 0
    for n in SMALL:
        nr = _rows_of(shard[n].size)
        slayout[n] = (row0, nr)
        row0 += nr
    loss_row = row0
    r_small = _round_up(row0 + 1, 8)

    def pack_small(d, with_loss=None):
        rows = [_to_rows(d[n]) for n in SMALL]
        rows.append(with_loss if with_loss is not None else jnp.zeros((1, LANE), F32))
        return _stack_rows(rows, 0, r_small)

    sgath, = _all_gather([pack_small(sg, g["loss"])], "ag_small")
    s_g, s_d, s_m, s_v = _small_update(sgath, pack_small(shard), pack_small({n: mom[n][0].reshape(shard[n].shape) for n in SMALL}),
                                       pack_small({n: mom[n][1].reshape(shard[n].shape) for n in SMALL}))
    loss = s_g[loss_row, 0]

    grads, deltas, new_m, new_v = {}, {}, {}, {}
    for n in BIG:
        gn = g_shard[n]
        d, nm, nv = _adamw(shard[n], gn, mom[n][0], mom[n][1], "adamw_" + n)
        grads[n], deltas[n], new_m[n], new_v[n] = (a[None] for a in (gn, d, nm, nv))
    for n in SMALL:
        r0, nr = slayout[n]
        for dst, src in ((grads, s_g), (deltas, s_d), (new_m, s_m), (new_v, s_v)):
            dst[n] = _from_rows(src[r0:r0 + nr], given[n].shape)
    return (loss, grad_x.reshape(B, T, D), *[grads[n] for n in order], *[deltas[n] for n in order],
            *[new_m[n] for n in order], *[new_v[n] for n in order])
```

```python
import functools
import math

import jax
import jax.numpy as jnp
from jax import lax
from jax.experimental import pallas as pl
from jax.experimental.pallas import tpu as pltpu

F32, BF16 = jnp.float32, jnp.bfloat16
S = jax.ShapeDtypeStruct
MESH = pl.DeviceIdType.MESH

N_DEV = 8
EPS = 1e-6
LANE = 128
CHUNK = 64
SUB = 16
HG_H, HG_D = 4, 128
HG_GROUP_FWD = 4
HG_GROUP = 2
FOX_H, FOX_D = 8, 64
FOX_P = FOX_H // 2
MEM_H, MEM_D = 4, 128
NEG = -1e30
VMEM_LIMIT = 56 * 2**20

ADAM_LR, ADAM_B1, ADAM_B2, ADAM_EPS, ADAM_WD, ADAM_STEP = 0.001, 0.9, 0.999, 1e-08, 0.01, 10

C_FOX, C_MQ, C_HG, C_GATE, C_FF, C_END = 0, 1536, 2048, 4096, 7168, 7296


def _cp(sem=None):
    return pltpu.CompilerParams(dimension_semantics=sem, vmem_limit_bytes=VMEM_LIMIT)


def _dot(a, b, dims, prec=None):
    return lax.dot_general(a, b, (dims, ((), ())), preferred_element_type=F32, precision=prec)


def _nn(a, b, prec=None):
    return _dot(a, b, ((1,), (0,)), prec)


def _nt(a, b, prec=None):
    return _dot(a, b, ((1,), (1,)), prec)


def _tn(a, b, prec=None):
    return _dot(a, b, ((0,), (0,)), prec)


def _b(x):
    return x.astype(BF16)


def _mm3(fn, a, b):
    ah, bh = _b(a), _b(b)
    return fn(ah, bh) + fn(ah, _b(b - bh.astype(F32))) + fn(_b(a - ah.astype(F32)), bh)


def _iota(shape, dim):
    return lax.broadcasted_iota(jnp.int32, shape, dim)


def _rowsum8(x):
    r, d = x.shape
    return jnp.sum(x.reshape(r // 8, 8, d), axis=0)


def _rmsnorm_cast(x, g, name, tm=1024, gather=()):
    n, d = x.shape
    nga = len(gather)

    def body(x_ref, g_ref, o_ref):
        v = x_ref[...]
        r = lax.rsqrt(jnp.mean(v * v, axis=-1, keepdims=True) + EPS)
        o_ref[...] = (v * r * g_ref[...]).astype(BF16)

    if nga:
        body = _hosting(body, 2, 1, 0, nga, _gather_phases, (n // tm,))
    out = pl.pallas_call(
        body, name=name, grid=(n // tm,),
        in_specs=[pl.BlockSpec((tm, d), lambda i: (i, 0)), pl.BlockSpec((1, d), lambda i: (0, 0))] + [ANY] * nga,
        out_specs=[pl.BlockSpec((tm, d), lambda i: (i, 0))] + [ANY] * nga,
        out_shape=[S((n, d), BF16)] + _gather_shapes(gather), scratch_shapes=_gather_sems(nga) if nga else [],
        compiler_params=_cp(("arbitrary",) if nga else ("parallel",)),
    )(x, g, *gather)
    return out if nga else out[0]


def _rmsnorm_bwd(dh, x, g, resid, name, tm=1024):
    n, d = x.shape
    has_res = resid is not None

    def body(*refs):
        if has_res:
            dh_ref, x_ref, g_ref, r_ref, dx_ref, dg_ref = refs
        else:
            dh_ref, x_ref, g_ref, dx_ref, dg_ref = refs
        v = x_ref[...]
        dhv = dh_ref[...].astype(F32)
        r = lax.rsqrt(jnp.mean(v * v, axis=-1, keepdims=True) + EPS)
        xh = v * r
        u = dhv * g_ref[...]
        dx = r * (u - xh * jnp.mean(u * xh, axis=-1, keepdims=True))
        if has_res:
            dx = dx + r_ref[...]
        dx_ref[...] = dx

        @pl.when(pl.program_id(0) == 0)
        def _():
            dg_ref[...] = jnp.zeros_like(dg_ref)

        dg_ref[...] += _rowsum8(dhv * xh)

    tile = pl.BlockSpec((tm, d), lambda i: (i, 0))
    ins = [tile, tile, pl.BlockSpec((1, d), lambda i: (0, 0))] + ([tile] if has_res else [])
    args = (dh, x, g) + ((resid,) if has_res else ())
    return pl.pallas_call(
        body, name=name, grid=(n // tm,), in_specs=ins,
        out_specs=[tile, pl.BlockSpec((8, d), lambda i: (0, 0))],
        out_shape=[S((n, d), F32), S((8, d), F32)], compiler_params=_cp(("arbitrary",)),
    )(*args)


def _mm_nn(a, b, out_dtype, name, tm, tn):
    m, k = a.shape
    n = b.shape[1]
    assert n % tn == 0 and m % tm == 0

    def body(a_ref, b_ref, o_ref):
        o_ref[...] = _nn(a_ref[...].astype(BF16), b_ref[...].astype(BF16)).astype(out_dtype)

    return pl.pallas_call(
        body, name=name, grid=(n // tn, m // tm),
        in_specs=[pl.BlockSpec((tm, k), lambda j, i: (i, 0)), pl.BlockSpec((k, tn), lambda j, i: (0, j))],
        out_specs=pl.BlockSpec((tm, tn), lambda j, i: (i, j)), out_shape=S((m, n), out_dtype),
        compiler_params=_cp(("parallel", "parallel")),
    )(a, b)


def _mm_nt_sum(parts, w, name, tm, swap=()):
    m = parts[0][0].shape[0]
    k = w.shape[0]
    assert m % tm == 0 and all(c % n == 0 and o % n == 0 for _, c, n, o in parts)
    np_ = len(parts)
    nsw = len(swap)
    n_steps = m // tm

    def body(*refs):
        o_ref = refs[2 * np_ + nsw]
        if nsw:
            start, finish = _chip_swap_phases(refs[2 * np_:2 * np_ + nsw], refs[2 * np_ + nsw + 1:2 * np_ + 2 * nsw + 1],
                                              *refs[2 * np_ + 2 * nsw + 1:])
            pl.when(pl.program_id(0) == 0)(start)
        acc = _nt(refs[0][...].astype(BF16), refs[np_][...].astype(BF16))
        for i in range(1, np_):
            acc = acc + _nt(refs[i][...].astype(BF16), refs[np_ + i][...].astype(BF16))
        o_ref[...] = acc
        if nsw:
            pl.when(pl.program_id(0) == n_steps - 1)(finish)

    dy_specs = [pl.BlockSpec((tm, n), functools.partial(lambda i, j: (i, j), j=c // n)) for _, c, n, _ in parts]
    w_specs = [pl.BlockSpec((k, n), functools.partial(lambda i, j: (0, j), j=o // n)) for _, _, n, o in parts]
    out = pl.pallas_call(
        body, name=name, grid=(n_steps,), in_specs=dy_specs + w_specs + [ANY] * nsw,
        out_specs=[pl.BlockSpec((tm, k), lambda i: (i, 0))] + [ANY] * nsw,
        out_shape=[S((m, k), F32)] + [S(p.shape, p.dtype) for p in swap],
        scratch_shapes=_chip_swap_sems(nsw) if nsw else [],
        compiler_params=_cp(("arbitrary",) if nsw else ("parallel",)),
    )(*([p[0] for p in parts] + [w] * np_ + list(swap)))
    return (out[0], out[1:]) if nsw else out[0]


def _mm_tn(x, dy, name, tm, tn):
    m, k = x.shape
    n = dy.shape[1]
    tm = min(tm, m)
    assert m % tm == 0 and n % tn == 0

    def body(x_ref, dy_ref, o_ref):
        part = _tn(x_ref[...].astype(BF16), dy_ref[...].astype(BF16))

        @pl.when(pl.program_id(1) == 0)
        def _():
            o_ref[...] = part

        @pl.when(pl.program_id(1) > 0)
        def _():
            o_ref[...] += part

    return pl.pallas_call(
        body, name=name, grid=(n // tn, m // tm),
        in_specs=[pl.BlockSpec((tm, k), lambda j, i: (i, 0)), pl.BlockSpec((tm, tn), lambda j, i: (i, j))],
        out_specs=pl.BlockSpec((k, tn), lambda j, i: (0, j)), out_shape=S((k, n), F32),
        compiler_params=_cp(("parallel", "arbitrary")),
    )(x, dy)


def _lower_bound(logits):
    e = jnp.exp(logits - jnp.max(logits, axis=0, keepdims=True))
    return e[0:1, :] / jnp.sum(e, axis=0, keepdims=True)


def _hg_gates(fl, lb):
    sig = jax.nn.sigmoid(fl)
    f = lb + (1.0 - lb) * sig
    k = (1.0 - lb) * (1.0 - sig)
    return sig, f, k, jnp.log(f)


def _silu_and_grad(x):
    s = jax.nn.sigmoid(x)
    return x * s, s * (1.0 + x * (1.0 - s))


def _hg_rowblocks(G):
    return [None] + [G[SUB * i - 1:SUB * i, :] for i in range(1, CHUNK // SUB)]


def _hg_intra_A(qs, k, G):
    refs = _hg_rowblocks(G)
    cols = _iota((SUB, LANE), 1)
    rows = _iota((SUB, LANE), 0)
    no_keys = jnp.zeros((LANE - CHUNK, HG_D), BF16)
    blocks = []
    for i in range(CHUNK // SUB):
        lo = SUB * i
        qb, Gb = qs[lo:lo + SUB, :], G[lo:lo + SUB, :]
        diag = jnp.zeros((SUB, LANE), F32)
        for s in range(SUB):
            e = jnp.exp(jnp.minimum(Gb - G[lo + s:lo + s + 1, :], 0.0))
            col = jnp.sum(qb * k[lo + s:lo + s + 1, :] * e, axis=-1, keepdims=True)
            diag = jnp.where(cols == lo + s, col, diag)
        a = jnp.where((cols >= lo) & (cols <= rows + lo), diag, 0.0)
        if i > 0:
            qr = qb * jnp.exp(Gb - refs[i])
            kr = k * jnp.exp(jnp.minimum(refs[i] - G, 0.0))
            a = jnp.where(cols < lo, _nt(_b(qr), jnp.concatenate([_b(kr), no_keys], axis=0)), a)
        blocks.append(a)
    return jnp.concatenate(blocks, axis=0)


def _hg_intra_bwd(dA, qs, k, G):
    refs = _hg_rowblocks(G)
    cols = _iota((SUB, CHUNK), 1)
    rows16 = _iota((SUB, HG_D), 0)
    dk = jnp.zeros((CHUNK, HG_D), F32)
    dq_blocks, dk_diag_blocks = [], []
    for i in range(CHUNK // SUB):
        lo = SUB * i
        qb, Gb = qs[lo:lo + SUB, :], G[lo:lo + SUB, :]
        dAb = dA[lo:lo + SUB, :]
        dq = jnp.zeros((SUB, HG_D), F32)
        dkb = jnp.zeros((SUB, HG_D), F32)
        for s in range(SUB):
            e = jnp.exp(jnp.minimum(Gb - G[lo + s:lo + s + 1, :], 0.0))
            e = jnp.where(rows16 >= s, e, 0.0)
            dcol = jnp.sum(jnp.where(cols == lo + s, dAb, 0.0), axis=-1, keepdims=True)
            w = dcol * e
            dq = dq + w * k[lo + s:lo + s + 1, :]
            dkb = jnp.where(rows16 == s, jnp.sum(w * qb, axis=0, keepdims=True), dkb)
        if i > 0:
            e1 = jnp.exp(Gb - refs[i])
            e2 = jnp.exp(jnp.minimum(refs[i] - G, 0.0))
            dA_off = jnp.where(cols < lo, dAb, 0.0)
            dq = dq + _mm3(_nn, dA_off, k * e2) * e1
            dk = dk + _mm3(_tn, dA_off, qb * e1) * e2
        dq_blocks.append(dq)
        dk_diag_blocks.append(dkb)
    return jnp.concatenate(dq_blocks, axis=0), dk + jnp.concatenate(dk_diag_blocks, axis=0)


def _tri(n, upper=False):
    r, c = _iota((n, n), 0), _iota((n, n), 1)
    return jnp.where((c >= r) if upper else (r >= c), 1.0, 0.0).astype(BF16)


def _prefix_mm(tri, x):
    hi = x.astype(BF16)
    r1 = x - hi.astype(F32)
    mid = r1.astype(BF16)
    lo = (r1 - mid.astype(F32)).astype(BF16)
    return _nn(tri, hi) + _nn(tri, mid) + _nn(tri, lo)


def _hgrn_fwd(z, lb, gn, B, T):
    N = B * T
    NC = T // CHUNK
    ng = HG_H // HG_GROUP_FWD

    def body(z_ref, lb_ref, gn_ref, y_ref, o_ref, st_ref, a_ref, s_scr):
        lbs = _lower_bound(lb_ref[...])
        tri = _tri(CHUNK)
        s_scr[...] = jnp.zeros_like(s_scr)

        def chunk(c, carry):
            r = pl.ds(pl.multiple_of(c * CHUNK, CHUNK), CHUNK)
            for hh in range(HG_GROUP_FWD):
                zc, oc = 4 * LANE * hh, LANE * hh
                ql, fl, il, gl = (z_ref[r, zc + LANE * j:zc + LANE * (j + 1)].astype(F32) for j in range(4))
                _, _, k, logf = _hg_gates(fl, lbs[:, oc:oc + LANE])
                G = _prefix_mm(tri, logf)
                qs = ql * jax.nn.sigmoid(ql)
                st = s_scr[hh]
                st_ref[hh * NC + c] = st
                g_last = G[CHUNK - 1:CHUNK, :]
                A = _b(_hg_intra_A(qs, k, G))
                a_ref[r, oc:oc + LANE] = A
                o = _nn(A[:, 0:CHUNK], _b(il)) + _nt(_b(qs * jnp.exp(G)), _b(st))
                s_scr[hh] = st * jnp.exp(g_last) + _mm3(_tn, il, k * jnp.exp(g_last - G))
                o_ref[r, oc:oc + LANE] = o
                rstd = lax.rsqrt(jnp.mean(o * o, axis=-1, keepdims=True) + EPS)
                y_ref[r, oc:oc + LANE] = (o * rstd * gn_ref[...] * (gl * jax.nn.sigmoid(gl))).astype(BF16)
            return carry

        lax.fori_loop(0, NC, chunk, 0, unroll=4)

    gw = HG_GROUP_FWD * LANE
    cb = C_HG // (4 * gw)
    return pl.pallas_call(
        body, name="hgrn_fwd", grid=(B, ng),
        in_specs=[pl.BlockSpec((T, 4 * gw), lambda b, h: (b, cb + h)), pl.BlockSpec((lb.shape[0], gw), lambda b, h: (0, h)),
                  pl.BlockSpec((1, LANE), lambda b, h: (0, 0))],
        out_specs=[pl.BlockSpec((T, gw), lambda b, h: (b, h)), pl.BlockSpec((T, gw), lambda b, h: (b, h)),
                   pl.BlockSpec((HG_GROUP_FWD * NC, HG_D, HG_D), lambda b, h: (b * ng + h, 0, 0)),
                   pl.BlockSpec((T, gw), lambda b, h: (b, h))],
        out_shape=[S((N, 512), BF16), S((N, 512), F32), S((B * HG_H * NC, HG_D, HG_D), F32), S((N, 512), BF16)],
        scratch_shapes=[pltpu.VMEM((HG_GROUP_FWD, HG_D, HG_D), F32)], compiler_params=_cp(("parallel", "parallel")),
    )(z, lb, gn)


def _hgrn_bwd(z, o_raw, states, a_mat, dy, lb, gn, B, T, swap_sibling=()):
    N = B * T
    NC = T // CHUNK
    ng = HG_H // HG_GROUP
    nsw = len(swap_sibling)

    def body(z_ref, o_ref, st_ref, a_ref, dy_ref, lb_ref, gn_ref, dz_ref, dlb_ref, dgn_ref, ds_scr, racc, dgn_acc):
        lbs = _lower_bound(lb_ref[...])
        gn_v = gn_ref[...]
        tri, triu = _tri(CHUNK), _tri(CHUNK, upper=True)
        cmask = _iota((CHUNK, CHUNK), 0) >= _iota((CHUNK, CHUNK), 1)
        for ref in (ds_scr, racc, dgn_acc, dlb_ref):
            ref[...] = jnp.zeros_like(ref)

        def chunk(ci, carry):
            c = NC - 1 - ci
            r = pl.ds(pl.multiple_of(c * CHUNK, CHUNK), CHUNK)
            for hh in range(HG_GROUP):
                zc, oc = 4 * LANE * hh, LANE * hh
                lb_v = lbs[:, oc:oc + LANE]
                ql, fl, il, gl = (z_ref[r, zc + LANE * j:zc + LANE * (j + 1)].astype(F32) for j in range(4))
                sig, f, k, logf = _hg_gates(fl, lb_v)
                G = _prefix_mm(tri, logf)
                qs, dsilu_q = _silu_and_grad(ql)
                gs, dsilu_g = _silu_and_grad(gl)
                o = o_ref[r, oc:oc + LANE]
                dyv = dy_ref[r, oc:oc + LANE]
                rstd = lax.rsqrt(jnp.mean(o * o, axis=-1, keepdims=True) + EPS)
                oh = o * rstd
                dgl = dyv * oh * gn_v * dsilu_g
                dn = dyv * gs
                dgn_acc[...] += _rowsum8(dn * oh)
                u = dn * gn_v
                do = rstd * (u - oh * jnp.mean(u * oh, axis=-1, keepdims=True))
                st = st_ref[hh * NC + c]
                dst = ds_scr[hh]
                eG = jnp.exp(G)
                g_last = G[CHUNK - 1:CHUNK, :]
                eL = jnp.exp(g_last - G)
                dA = jnp.where(cmask, _mm3(_nt, do, il), 0.0)
                dq_in, dk_in = _hg_intra_bwd(dA, qs, k, G)
                di = _tn(a_ref[r, oc:oc + LANE][:, 0:CHUNK], _b(do)) + _nt(_b(k * eL), _b(dst))
                dq = dq_in + _mm3(_nn, do, st) * eG
                dk = dk_in + _mm3(_nn, il, dst) * eL
                ds_scr[hh] = dst * jnp.exp(g_last) + _mm3(_tn, do, qs * eG)
                dd = qs * dq - k * dk
                dlogf = _prefix_mm(triu, dd) + racc[hh]
                racc[hh] += jnp.sum(dd, axis=0, keepdims=True)
                df = dlogf / f - dk
                dlb_ref[8 * hh:8 * (hh + 1), :] += _rowsum8(df * (1.0 - sig))
                dz_ref[r, zc:zc + LANE] = (dq * dsilu_q).astype(BF16)
                dz_ref[r, zc + LANE:zc + 2 * LANE] = (df * (1.0 - lb_v) * sig * (1.0 - sig)).astype(BF16)
                dz_ref[r, zc + 2 * LANE:zc + 3 * LANE] = di.astype(BF16)
                dz_ref[r, zc + 3 * LANE:zc + 4 * LANE] = dgl.astype(BF16)
            return carry

        lax.fori_loop(0, NC, chunk, 0, unroll=4)
        dgn_ref[...] = dgn_acc[...]

    gw = HG_GROUP * LANE
    cb = C_HG // (4 * gw)
    col = pl.BlockSpec((T, gw), lambda b, h: (b, h))
    if nsw:
        body = _hosting(body, 7, 3, 3, nsw, _sibling_swap_phases, (B, ng))
    return pl.pallas_call(
        body, name="hgrn_bwd", grid=(B, ng),
        in_specs=[pl.BlockSpec((T, 4 * gw), lambda b, h: (b, cb + h)), col,
                  pl.BlockSpec((HG_GROUP * NC, HG_D, HG_D), lambda b, h: (b * ng + h, 0, 0)), col, col,
                  pl.BlockSpec((lb.shape[0], gw), lambda b, h: (0, h)), pl.BlockSpec((1, LANE), lambda b, h: (0, 0))]
        + [ANY] * nsw,
        out_specs=[pl.BlockSpec((T, 4 * gw), lambda b, h: (b, h)),
                   pl.BlockSpec((8 * HG_GROUP, LANE), lambda b, h: (b * ng + h, 0)),
                   pl.BlockSpec((8, LANE), lambda b, h: (b * ng + h, 0))] + [ANY] * nsw,
        out_shape=[S((N, 2048), BF16), S((B * HG_H * 8, LANE), F32), S((B * ng * 8, LANE), F32)]
        + _sibling_swap_shapes(swap_sibling),
        scratch_shapes=[pltpu.VMEM((HG_GROUP, HG_D, HG_D), F32), pltpu.VMEM((HG_GROUP, 1, LANE), F32),
                        pltpu.VMEM((8, LANE), F32)] + (_sibling_swap_sems(nsw) if nsw else []),
        compiler_params=_cp(("arbitrary", "arbitrary") if nsw else ("parallel", "parallel")),
    )(z, o_raw, states, a_mat, dy, lb, gn, *swap_sibling)


def _pair_mean(x, lo_half):
    a = jnp.sum(jnp.where(lo_half, x, 0.0), axis=-1, keepdims=True)
    b = jnp.sum(jnp.where(lo_half, 0.0, x), axis=-1, keepdims=True)
    return jnp.where(lo_half, a, b) * (1.0 / FOX_D)


def _fox_gate_fwd(z, bias, B, T):
    N = B * T
    tb = LANE

    def body(z_ref, b_ref, fc_ref, fct_ref):
        tri = _tri(tb)

        def step(i, carry):
            r = pl.ds(pl.multiple_of(i * tb, tb), tb)
            cs = _prefix_mm(tri, jax.nn.log_sigmoid(z_ref[r, :].astype(F32) + b_ref[...])) + carry
            fc_ref[r, :] = cs
            fct_ref[0, :, r] = cs.T[0:8, :]
            return cs[tb - 1:tb, :]

        lax.fori_loop(0, T // tb, step, jnp.zeros((1, LANE), F32))

    return pl.pallas_call(
        body, name="fox_gate_fwd", grid=(B,),
        in_specs=[pl.BlockSpec((T, LANE), lambda b: (b, C_FF // LANE)), pl.BlockSpec((1, LANE), lambda b: (0, 0))],
        out_specs=[pl.BlockSpec((T, LANE), lambda b: (b, 0)), pl.BlockSpec((1, 8, T), lambda b: (b, 0, 0))],
        out_shape=[S((N, LANE), F32), S((B, 8, T), F32)], compiler_params=_cp(("parallel",)),
    )(z, bias)


def _fox_gate_bwd(dfc, z, bias, B, T):
    N = B * T
    tb = LANE
    nt = T // tb

    def body(d_ref, z_ref, b_ref, dz_ref, db_ref):
        triu = _tri(tb, upper=True)
        db_ref[...] = jnp.zeros_like(db_ref)

        def step(ii, carry):
            r = pl.ds(pl.multiple_of((nt - 1 - ii) * tb, tb), tb)
            d = d_ref[r, 0:LANE]
            for p in range(1, FOX_P):
                d = d + d_ref[r, LANE * p:LANE * (p + 1)]
            rc = _prefix_mm(triu, d) + carry
            dff = rc * jax.nn.sigmoid(-(z_ref[r, :].astype(F32) + b_ref[...]))
            dz_ref[r, :] = dff.astype(BF16)
            db_ref[...] += _rowsum8(dff)
            return carry + jnp.sum(d, axis=0, keepdims=True)

        lax.fori_loop(0, nt, step, jnp.zeros((1, LANE), F32))

    return pl.pallas_call(
        body, name="fox_gate_bwd", grid=(B,),
        in_specs=[pl.BlockSpec((T, 512), lambda b: (b, 0)), pl.BlockSpec((T, LANE), lambda b: (b, C_FF // LANE)),
                  pl.BlockSpec((1, LANE), lambda b: (0, 0))],
        out_specs=[pl.BlockSpec((T, LANE), lambda b: (b, 0)), pl.BlockSpec((8, LANE), lambda b: (b, 0))],
        out_shape=[S((N, LANE), BF16), S((B * 8, LANE), F32)], compiler_params=_cp(("parallel",)),
    )(dfc, z, bias)


def _fox_prep(z_ref, gq, gk, r, lo_half):
    q, k, v = (z_ref[r, LANE * j:LANE * (j + 1)].astype(F32) for j in range(3))
    rq = lax.rsqrt(_pair_mean(q * q, lo_half) + EPS)
    rk = lax.rsqrt(_pair_mean(k * k, lo_half) + EPS)
    qh, kh = q * rq, k * rk
    return qh * gq * (FOX_D ** -0.5), kh * gk, v, qh, kh, rq, rk


def _fox_fwd(z, fc, fct, gq, gk, B, T, tq=512, gather=()):
    N = B * T
    NQ = T // tq
    nga = len(gather)

    def body(z_ref, fc_ref, fct_ref, gq_ref, gk_ref, y_ref, lse_ref, qn_s, kn_s, v_s):
        p, qi = pl.program_id(1), pl.program_id(2)
        lo_half = _iota((1, LANE), 1) < FOX_D

        @pl.when(qi == 0)
        def _():
            def prep(i, carry):
                r = pl.ds(pl.multiple_of(i * tq, tq), tq)
                qn, kn, v = _fox_prep(z_ref, gq_ref[...], gk_ref[...], r, lo_half)[:3]
                qn_s[r, :], kn_s[r, :], v_s[r, :] = qn.astype(BF16), kn.astype(BF16), v.astype(BF16)
                return carry
            lax.fori_loop(0, NQ, prep, 0)

        rq = pl.ds(pl.multiple_of(qi * tq, tq), tq)
        qn = qn_s[rq, :]
        fcq = fc_ref[rq, :]
        lane = _iota((tq, LANE), 1)
        causal = _iota((tq, tq), 0) >= _iota((tq, tq), 1)
        qhs = [jnp.where(lo_half, qn, jnp.zeros_like(qn)), jnp.where(lo_half, jnp.zeros_like(qn), qn)]
        fqs = [jnp.sum(jnp.where(lane == 2 * p + hh, fcq, 0.0), axis=-1, keepdims=True) for hh in range(2)]

        def kv(j, carry, diagonal):
            rk = pl.ds(pl.multiple_of(j * tq, tq), tq)
            kj, vj = kn_s[rk, :], v_s[rk, :]
            one = jnp.ones_like(vj)
            new = []
            for hh in range(2):
                m, acc = carry[hh]
                s = _nt(qhs[hh], kj) + fqs[hh] - fct_ref[0, pl.ds(2 * p + hh, 1), rk]
                if diagonal:
                    s = jnp.where(causal, s, NEG)
                m_new = jnp.maximum(m, jnp.max(s, axis=-1, keepdims=True))
                pe = jnp.exp(s - m_new)
                v_aug = jnp.where(lo_half if hh == 0 else jnp.logical_not(lo_half), vj, one)
                new.append((m_new, jnp.exp(m - m_new) * acc + _nn(pe.astype(BF16), v_aug)))
            return tuple(new)

        init = tuple((jnp.full((tq, 1), NEG, F32), jnp.zeros((tq, LANE), F32)) for _ in range(2))
        carry = lax.fori_loop(0, qi, functools.partial(kv, diagonal=False), init)
        (m0, a0), (m1, a1) = kv(qi, carry, True)
        l0, l1 = a0[:, FOX_D:FOX_D + 1], a1[:, 0:1]
        y_ref[...] = jnp.where(lo_half, a0 / l0, a1 / l1).astype(BF16)
        lse_ref[...] = jnp.where(lo_half, m0 + jnp.log(l0), m1 + jnp.log(l1))

    vec = pl.BlockSpec((1, LANE), lambda b, p, q: (0, 0))
    tile = pl.BlockSpec((tq, LANE), lambda b, p, q: (b * NQ + q, p))
    if nga:
        body = _hosting(body, 5, 2, 3, nga, _gather_phases, (B, FOX_P, NQ))
    return pl.pallas_call(
        body, name="fox_fwd", grid=(B, FOX_P, NQ),
        in_specs=[pl.BlockSpec((T, 384), lambda b, p, q: (b, p)), pl.BlockSpec((T, LANE), lambda b, p, q: (b, 0)),
                  pl.BlockSpec((1, 8, T), lambda b, p, q: (b, 0, 0)), vec, vec] + [ANY] * nga,
        out_specs=[tile, tile] + [ANY] * nga, out_shape=[S((N, 512), BF16), S((N, 512), F32)] + _gather_shapes(gather),
        scratch_shapes=[pltpu.VMEM((T, LANE), BF16)] * 3 + (_gather_sems(nga) if nga else []),
        compiler_params=_cp(("arbitrary",) * 3 if nga else ("parallel", "parallel", "arbitrary")),
    )(z, fc, fct, gq, gk, *gather)


def _fox_bwd(z, dy, y, lse, fc, fct, gq, gk, B, T, tq=512, swap=()):
    N = B * T
    NQ = T // tq
    nsw = len(swap)

    def body(z_ref, dy_ref, y_ref, lse_ref, fc_ref, fct_ref, gq_ref, gk_ref, dz_ref, dfc_ref, dgq_ref, dgk_ref,
             qn_s, kn_s, v_s, do_s, delta_s, dq_s, dfk_s):
        p, kj = pl.program_id(1), pl.program_id(2)
        lo_half = _iota((1, LANE), 1) < FOX_D
        lane = _iota((tq, LANE), 1)
        gq_v, gk_v = gq_ref[...], gk_ref[...]

        @pl.when(kj == 0)
        def _():
            def prep(i, carry):
                r = pl.ds(pl.multiple_of(i * tq, tq), tq)
                qn, kn, v = _fox_prep(z_ref, gq_v, gk_v, r, lo_half)[:3]
                qn_s[r, :], kn_s[r, :], v_s[r, :] = qn.astype(BF16), kn.astype(BF16), v.astype(BF16)
                do = dy_ref[r, :]
                do_s[r, :] = do.astype(BF16)
                delta_s[r, :] = _pair_mean(do * y_ref[r, :].astype(F32), lo_half) * float(FOX_D)
                return carry
            lax.fori_loop(0, NQ, prep, 0)
            dq_s[...] = jnp.zeros_like(dq_s)
            dgq_ref[...] = jnp.zeros_like(dgq_ref)
            dgk_ref[...] = jnp.zeros_like(dgk_ref)

        rk = pl.ds(pl.multiple_of(kj * tq, tq), tq)
        kn, vv = kn_s[rk, :], v_s[rk, :]
        causal = _iota((tq, tq), 0) >= _iota((tq, tq), 1)
        zero, one = jnp.zeros_like(kn), jnp.ones_like(kn)
        hms = [lo_half, jnp.logical_not(lo_half)]
        kmasks = [jnp.where(hm, kn, zero) for hm in hms]
        kaugs = [jnp.where(hm, kn, one) for hm in hms]
        vmasks = [jnp.where(hm, vv, zero) for hm in hms]
        fks = [fct_ref[0, pl.ds(2 * p + hh, 1), rk] for hh in range(2)]

        def qloop(i, carry, diagonal):
            ri = pl.ds(pl.multiple_of(i * tq, tq), tq)
            qn = qn_s[ri, :]
            do = do_s[ri, :]
            fcq = fc_ref[ri, :]
            new = []
            for hh in range(2):
                dk_acc, dv_acc = carry[hh]
                c0 = FOX_D * hh
                fq = jnp.sum(jnp.where(lane == 2 * p + hh, fcq, 0.0), axis=-1, keepdims=True)
                pr = jnp.exp(_nt(qn, kmasks[hh]) + fq - fks[hh] - lse_ref[ri, c0:c0 + 1])
                if diagonal:
                    pr = jnp.where(causal, pr, 0.0)
                ds = (pr * (_nt(do, vmasks[hh]) - delta_s[ri, c0:c0 + 1])).astype(BF16)
                dq_s[hh, ri, :] += _nn(ds, kaugs[hh])
                new.append((dk_acc + _tn(jnp.where(hms[hh], qn, one), ds), dv_acc + _tn(do, pr.astype(BF16))))
            return tuple(new)

        init = tuple((jnp.zeros((LANE, tq), F32), jnp.zeros((LANE, tq), F32)) for _ in range(2))
        carry = qloop(kj, init, True)
        (dk0, dv0), (dk1, dv1) = lax.fori_loop(kj + 1, NQ, functools.partial(qloop, diagonal=False), carry)
        dks, dvs = [dk0.T, dk1.T], [dv0.T, dv1.T]

        dkn = jnp.where(lo_half, dks[0], dks[1])
        _, _, _, _, kh, _, rkk = _fox_prep(z_ref, gq_v, gk_v, rk, lo_half)
        u = dkn * gk_v
        dz_ref[rk, LANE:2 * LANE] = (rkk * (u - kh * _pair_mean(u * kh, lo_half))).astype(BF16)
        dz_ref[rk, 2 * LANE:3 * LANE] = jnp.where(lo_half, dvs[0], dvs[1]).astype(BF16)
        dgk_ref[...] += _rowsum8(dkn * kh)
        dfk_s[rk, :] = jnp.where(lane == 2 * p, -dks[0][:, FOX_D:FOX_D + 1],
                                 jnp.where(lane == 2 * p + 1, -dks[1][:, 0:1], 0.0))

        @pl.when(kj == NQ - 1)
        def _():
            def fin(i, carry):
                r = pl.ds(pl.multiple_of(i * tq, tq), tq)
                d0, d1 = dq_s[0, r, :], dq_s[1, r, :]
                dqn = jnp.where(lo_half, d0, d1)
                _, _, _, qh, _, rqq, _ = _fox_prep(z_ref, gq_v, gk_v, r, lo_half)
                u = dqn * gq_v * (FOX_D ** -0.5)
                dz_ref[r, 0:LANE] = (rqq * (u - qh * _pair_mean(u * qh, lo_half))).astype(BF16)
                dgq_ref[...] += _rowsum8(dqn * qh) * (FOX_D ** -0.5)
                dfc_ref[r, :] = dfk_s[r, :] + jnp.where(lane == 2 * p, d0[:, FOX_D:FOX_D + 1],
                                                        jnp.where(lane == 2 * p + 1, d1[:, 0:1], 0.0))
                return carry
            lax.fori_loop(0, NQ, fin, 0)

    vec = pl.BlockSpec((1, LANE), lambda b, p, k: (0, 0))
    col = pl.BlockSpec((T, LANE), lambda b, p, k: (b, p))
    part = pl.BlockSpec((8, LANE), lambda b, p, k: (b * FOX_P + p, 0))
    if nsw:
        body = _hosting(body, 8, 4, 7, nsw, _chip_swap_phases, (B, FOX_P, NQ))
    return pl.pallas_call(
        body, name="fox_bwd", grid=(B, FOX_P, NQ),
        in_specs=[pl.BlockSpec((T, 384), lambda b, p, k: (b, p)), col, col, col,
                  pl.BlockSpec((T, LANE), lambda b, p, k: (b, 0)), pl.BlockSpec((1, 8, T), lambda b, p, k: (b, 0, 0)),
                  vec, vec] + [ANY] * nsw,
        out_specs=[pl.BlockSpec((T, 384), lambda b, p, k: (b, p)), col, part, part] + [ANY] * nsw,
        out_shape=[S((N, 1536), BF16), S((N, 512), F32), S((B * FOX_P * 8, LANE), F32), S((B * FOX_P * 8, LANE), F32)]
        + [S(p.shape, p.dtype) for p in swap],
        scratch_shapes=[pltpu.VMEM((T, LANE), BF16)] * 4 + [pltpu.VMEM((T, LANE), F32), pltpu.VMEM((2, T, LANE), F32),
                                                            pltpu.VMEM((T, LANE), F32)]
        + (_chip_swap_sems(nsw) if nsw else []),
        compiler_params=_cp(("arbitrary",) * 3 if nsw else ("parallel", "parallel", "arbitrary")),
    )(z, dy, y, lse, fc, fct, gq, gk, *swap)


def _mem_scores(z_ref, kv_ref, gq, gk, h):
    c = slice(MEM_D * h, MEM_D * (h + 1))
    q, k = z_ref[:, c].astype(F32), kv_ref[:, c]
    rq = lax.rsqrt(jnp.mean(q * q, axis=-1, keepdims=True) + EPS)
    rk = lax.rsqrt(jnp.mean(k * k, axis=-1, keepdims=True) + EPS)
    qh, kh = q * rq, k * rk
    qn = (qh * gq * (MEM_D ** -0.5)).astype(BF16)
    kn = (kh * gk).astype(BF16)
    s = _nt(qn, kn)
    pe = jnp.exp(s - jnp.max(s, axis=-1, keepdims=True))
    pn = pe / jnp.sum(pe, axis=-1, keepdims=True)
    return pn, qn, kn, qh, kh, rq, rk


def _mem_fwd(z, memkv, gq, gk, B, T, M, tq=1024):
    N = B * T
    tq = min(tq, T)
    NQ = T // tq
    W = MEM_H * MEM_D

    def body(z_ref, kv_ref, gq_ref, gk_ref, y_ref):
        for h in range(MEM_H):
            pn = _mem_scores(z_ref, kv_ref, gq_ref[...], gk_ref[...], h)[0]
            v = kv_ref[:, W + MEM_D * h:W + MEM_D * (h + 1)].astype(BF16)
            y_ref[:, MEM_D * h:MEM_D * (h + 1)] = _nn(pn.astype(BF16), v).astype(BF16)

    vec = pl.BlockSpec((1, LANE), lambda b, q: (0, 0))
    return pl.pallas_call(
        body, name="mem_fwd", grid=(B, NQ),
        in_specs=[pl.BlockSpec((tq, W), lambda b, q: (b * NQ + q, C_MQ // W)),
                  pl.BlockSpec((M, 2 * W), lambda b, q: (b, 0)), vec, vec],
        out_specs=pl.BlockSpec((tq, W), lambda b, q: (b * NQ + q, 0)), out_shape=S((N, W), BF16),
        compiler_params=_cp(("parallel", "parallel")),
    )(z, memkv, gq, gk)


def _mem_bwd(z, memkv, dy, gq, gk, B, T, M, tq=1024):
    N = B * T
    tq = min(tq, T)
    NQ = T // tq
    W = MEM_H * MEM_D

    def body(z_ref, kv_ref, dy_ref, gq_ref, gk_ref, dz_ref, dkv_ref, dgq_ref, dgk_ref, acc):
        qi = pl.program_id(1)
        gq_v, gk_v = gq_ref[...], gk_ref[...]

        @pl.when(qi == 0)
        def _():
            acc[...] = jnp.zeros_like(acc)
            dgq_ref[...] = jnp.zeros_like(dgq_ref)
            dgk_ref[...] = jnp.zeros_like(dgk_ref)

        for h in range(MEM_H):
            c = slice(MEM_D * h, MEM_D * (h + 1))
            cv = slice(W + MEM_D * h, W + MEM_D * (h + 1))
            pn, qn, kn, qh, _, rq, _ = _mem_scores(z_ref, kv_ref, gq_v, gk_v, h)
            do = dy_ref[:, c].astype(BF16)
            dp = _nt(do, kv_ref[:, cv].astype(BF16))
            ds = (pn * (dp - jnp.sum(dp * pn, axis=-1, keepdims=True))).astype(BF16)
            dqn = _nn(ds, kn)
            acc[:, c] += _tn(ds, qn)
            acc[:, cv] += _tn(pn.astype(BF16), do)
            u = dqn * gq_v * (MEM_D ** -0.5)
            dz_ref[:, c] = (rq * (u - qh * jnp.mean(u * qh, axis=-1, keepdims=True))).astype(BF16)
            dgq_ref[...] += _rowsum8(dqn * qh) * (MEM_D ** -0.5)

        @pl.when(qi == NQ - 1)
        def _():
            for h in range(MEM_H):
                c = slice(MEM_D * h, MEM_D * (h + 1))
                cv = slice(W + MEM_D * h, W + MEM_D * (h + 1))
                k = kv_ref[:, c]
                rk = lax.rsqrt(jnp.mean(k * k, axis=-1, keepdims=True) + EPS)
                kh = k * rk
                dkn = acc[:, c]
                u = dkn * gk_v
                dkv_ref[:, c] = (rk * (u - kh * jnp.mean(u * kh, axis=-1, keepdims=True))).astype(BF16)
                dkv_ref[:, cv] = acc[:, cv].astype(BF16)
                dgk_ref[...] += _rowsum8(dkn * kh)

    vec = pl.BlockSpec((1, LANE), lambda b, q: (0, 0))
    part = pl.BlockSpec((8, LANE), lambda b, q: (b, 0))
    return pl.pallas_call(
        body, name="mem_bwd", grid=(B, NQ),
        in_specs=[pl.BlockSpec((tq, W), lambda b, q: (b * NQ + q, C_MQ // W)),
                  pl.BlockSpec((M, 2 * W), lambda b, q: (b, 0)), pl.BlockSpec((tq, W), lambda b, q: (b * NQ + q, 0)),
                  vec, vec],
        out_specs=[pl.BlockSpec((tq, W), lambda b, q: (b * NQ + q, 0)), pl.BlockSpec((M, 2 * W), lambda b, q: (b, 0)),
                   part, part],
        out_shape=[S((N, W), BF16), S((B * M, 2 * W), BF16), S((B * 8, LANE), F32), S((B * 8, LANE), F32)],
        scratch_shapes=[pltpu.VMEM((M, 2 * W), F32)], compiler_params=_cp(("parallel", "arbitrary")),
    )(z, memkv, dy, gq, gk)


def _merge_fwd(ya, yb, yc, z, x, wa, wb, wc, wo, g_next, tm=512):
    n, d = x.shape
    wdt = ya.shape[1]
    gb = C_GATE // d

    def body(ya_ref, yb_ref, yc_ref, g0_ref, g1_ref, g2_ref, x_ref, wa_ref, wb_ref, wc_ref, wo_ref, gn_ref,
             x1_ref, mg_ref, ua_ref, ub_ref, uc_ref, h_ref):
        merged = jnp.zeros((tm, d), F32)
        for y_ref, g_ref, w_ref, u_ref in ((ya_ref, g0_ref, wa_ref, ua_ref), (yb_ref, g1_ref, wb_ref, ub_ref),
                                           (yc_ref, g2_ref, wc_ref, uc_ref)):
            u = _nn(y_ref[...], w_ref[...])
            u_ref[...] = u.astype(BF16)
            merged = merged + jax.nn.sigmoid(g_ref[...].astype(F32)) * u
        mb = merged.astype(BF16)
        mg_ref[...] = mb
        x1 = x_ref[...] + _nn(mb, wo_ref[...])
        x1_ref[...] = x1
        h_ref[...] = (x1 * lax.rsqrt(jnp.mean(x1 * x1, axis=-1, keepdims=True) + EPS) * gn_ref[...]).astype(BF16)

    yt = pl.BlockSpec((tm, wdt), lambda i: (i, 0))
    xt = pl.BlockSpec((tm, d), lambda i: (i, 0))
    wbr = pl.BlockSpec((wdt, d), lambda i: (0, 0))
    gates = [pl.BlockSpec((tm, d), functools.partial(lambda i, k: (i, gb + k), k=k)) for k in range(3)]
    return pl.pallas_call(
        body, name="merge_fwd", grid=(n // tm,),
        in_specs=[yt, yt, yt] + gates + [xt, wbr, wbr, wbr, pl.BlockSpec((d, d), lambda i: (0, 0)),
                                         pl.BlockSpec((1, d), lambda i: (0, 0))],
        out_specs=[xt] * 6, out_shape=[S((n, d), F32)] + [S((n, d), BF16)] * 5, compiler_params=_cp(("parallel",)),
    )(ya, yb, yc, z, z, z, x, wa, wb, wc, wo, g_next)


def _merge_bwd(dx1, z, ua, ub, uc, wa, wb, wc, wo, tm=512):
    n, d = dx1.shape
    wdt = wa.shape[0]
    gb = C_GATE // d

    def body(dx_ref, g0_ref, g1_ref, g2_ref, ua_ref, ub_ref, uc_ref, wa_ref, wb_ref, wc_ref, wo_ref,
             dg_ref, dya_ref, dyb_ref, dyc_ref, dua_ref, dub_ref, duc_ref):
        dm = _nt(dx_ref[...].astype(BF16), wo_ref[...])
        for k, (g_ref, u_ref, w_ref, dy_ref, du_ref) in enumerate((
                (g0_ref, ua_ref, wa_ref, dya_ref, dua_ref), (g1_ref, ub_ref, wb_ref, dyb_ref, dub_ref),
                (g2_ref, uc_ref, wc_ref, dyc_ref, duc_ref))):
            g = jax.nn.sigmoid(g_ref[...].astype(F32))
            du = (dm * g).astype(BF16)
            du_ref[...] = du
            dg_ref[:, d * k:d * (k + 1)] = (dm * u_ref[...].astype(F32) * g * (1.0 - g)).astype(BF16)
            dy_ref[...] = _nt(du, w_ref[...])

    yt = pl.BlockSpec((tm, wdt), lambda i: (i, 0))
    xt = pl.BlockSpec((tm, d), lambda i: (i, 0))
    wbr = pl.BlockSpec((wdt, d), lambda i: (0, 0))
    gates = [pl.BlockSpec((tm, d), functools.partial(lambda i, k: (i, gb + k), k=k)) for k in range(3)]
    return pl.pallas_call(
        body, name="merge_bwd", grid=(n // tm,),
        in_specs=[xt] + gates + [xt, xt, xt, wbr, wbr, wbr, pl.BlockSpec((d, d), lambda i: (0, 0))],
        out_specs=[pl.BlockSpec((tm, 3 * d), lambda i: (i, 0)), yt, yt, yt, xt, xt, xt],
        out_shape=[S((n, 3 * d), BF16)] + [S((n, wdt), F32)] * 3 + [S((n, d), BF16)] * 3,
        compiler_params=_cp(("parallel",)),
    )(dx1, z, z, z, ua, ub, uc, wa, wb, wc, wo)


FFN_TN = 1408
TN_TM = 2048
INV_SQRT2 = 0.7071067811865476
INV_SQRT_2PI = 0.3989422804014327


def _conv_shifted(a, prev, first, tm):
    row = _iota(a.shape, 0)
    p7 = jnp.where(first, 0.0, prev[7:8, :])
    p6 = jnp.where(first, 0.0, prev[6:7, :])
    a1 = jnp.where(row == 0, p7, pltpu.roll(a, 1, 0))
    a2 = jnp.where(row == 0, p6, jnp.where(row == 1, p7, pltpu.roll(a, 2, 0)))
    return a1, a2


def _ffn_act_fwd(up, cw, cb, B, T, tm=1024):
    N = B * T
    tm = min(tm, T)
    dff = cw.shape[1]
    NT, NJ, tn = T // tm, dff // FFN_TN, FFN_TN

    def body(a_ref, v_ref, cw_ref, cb_ref, y_ref, c_ref, carry):
        t = pl.program_id(2)
        a = a_ref[...].astype(F32)
        a1, a2 = _conv_shifted(a, carry[...], t == 0, tm)
        w = cw_ref[...]
        ac = w[0:1, :] * a2 + w[1:2, :] * a1 + w[2:3, :] * a + cb_ref[...]
        cdf = 0.5 * (1.0 + lax.erf(ac * INV_SQRT2))
        y_ref[...] = (ac * cdf * v_ref[...].astype(F32)).astype(BF16)
        c_ref[...] = cdf.astype(BF16)
        carry[...] = a[tm - 8:tm, :]

    return pl.pallas_call(
        body, name="ffn_act_fwd", grid=(B, NJ, NT),
        in_specs=[pl.BlockSpec((tm, tn), lambda b, j, t: (b * NT + t, j)),
                  pl.BlockSpec((tm, tn), lambda b, j, t: (b * NT + t, NJ + j)),
                  pl.BlockSpec((3, tn), lambda b, j, t: (0, j)), pl.BlockSpec((1, tn), lambda b, j, t: (0, j))],
        out_specs=[pl.BlockSpec((tm, tn), lambda b, j, t: (b * NT + t, j))] * 2, out_shape=[S((N, dff), BF16)] * 2,
        scratch_shapes=[pltpu.VMEM((8, tn), F32)], compiler_params=_cp(("parallel", "parallel", "arbitrary")),
    )(up, up, cw, cb)


def _ffn_down_loss(y, wd, x1, tgt, tm=512):
    n, d = x1.shape
    kf = y.shape[1]

    def body(y_ref, w_ref, x_ref, t_ref, dx_ref, ls_ref):
        err = x_ref[...] + _nn(y_ref[...], w_ref[...]) - t_ref[...]
        dx_ref[...] = err * (1.0 / d)

        @pl.when(pl.program_id(0) == 0)
        def _():
            ls_ref[...] = jnp.zeros_like(ls_ref)

        ls_ref[...] += _rowsum8(err * err) * (0.5 / d)

    xt = pl.BlockSpec((tm, d), lambda i: (i, 0))
    return pl.pallas_call(
        body, name="ffn_down_loss", grid=(n // tm,),
        in_specs=[pl.BlockSpec((tm, kf), lambda i: (i, 0)), pl.BlockSpec((kf, d), lambda i: (0, 0)), xt, xt],
        out_specs=[xt, pl.BlockSpec((8, d), lambda i: (0, 0))], out_shape=[S((n, d), F32), S((8, d), F32)],
        compiler_params=_cp(("arbitrary",)),
    )(y, wd, x1, tgt)


def _ffn_act_bwd1(dx2, wd, up, cdf, cw, cb, B, T, tm=512):
    N = B * T
    tm = min(tm, T)
    d = dx2.shape[1]
    dff = cw.shape[1]
    NT, NJ, tn = T // tm, dff // FFN_TN, FFN_TN

    def body(dx_ref, w_ref, a_ref, v_ref, c_ref, cw_ref, cb_ref, dac_ref, dv_ref, dcw_ref, dcb_ref, carry):
        b, t = pl.program_id(1), pl.program_id(2)
        a = a_ref[...].astype(F32)
        a1, a2 = _conv_shifted(a, carry[...], t == 0, tm)
        carry[...] = a[tm - 8:tm, :]
        w = cw_ref[...]
        ac = w[0:1, :] * a2 + w[1:2, :] * a1 + w[2:3, :] * a + cb_ref[...]
        dy = _nt(dx_ref[...].astype(BF16), w_ref[...])
        cdf = c_ref[...].astype(F32)
        dv_ref[...] = (dy * ac * cdf).astype(BF16)
        dac = dy * v_ref[...].astype(F32) * (cdf + ac * jnp.exp(-0.5 * ac * ac) * INV_SQRT_2PI)
        dac_ref[...] = dac

        @pl.when((b == 0) & (t == 0))
        def _():
            dcw_ref[...] = jnp.zeros_like(dcw_ref)
            dcb_ref[...] = jnp.zeros_like(dcb_ref)

        dcw_ref[0:8, :] += _rowsum8(dac * a2)
        dcw_ref[8:16, :] += _rowsum8(dac * a1)
        dcw_ref[16:24, :] += _rowsum8(dac * a)
        dcb_ref[...] += _rowsum8(dac)

    return pl.pallas_call(
        body, name="ffn_act_bwd1", grid=(NJ, B, NT),
        in_specs=[pl.BlockSpec((tm, d), lambda j, b, t: (b * NT + t, 0)), pl.BlockSpec((tn, d), lambda j, b, t: (j, 0)),
                  pl.BlockSpec((tm, tn), lambda j, b, t: (b * NT + t, j)),
                  pl.BlockSpec((tm, tn), lambda j, b, t: (b * NT + t, NJ + j)),
                  pl.BlockSpec((tm, tn), lambda j, b, t: (b * NT + t, j)),
                  pl.BlockSpec((3, tn), lambda j, b, t: (0, j)), pl.BlockSpec((1, tn), lambda j, b, t: (0, j))],
        out_specs=[pl.BlockSpec((tm, tn), lambda j, b, t: (b * NT + t, j)),
                   pl.BlockSpec((tm, tn), lambda j, b, t: (b * NT + t, j)),
                   pl.BlockSpec((24, tn), lambda j, b, t: (0, j)), pl.BlockSpec((8, tn), lambda j, b, t: (0, j))],
        out_shape=[S((N, dff), F32), S((N, dff), BF16), S((24, dff), F32), S((8, dff), F32)],
        scratch_shapes=[pltpu.VMEM((8, tn), F32)], compiler_params=_cp(("parallel", "arbitrary", "arbitrary")),
    )(dx2, wd, up, up, cdf, cw, cb)


def _ffn_act_bwd2(dac, cw, B, T, tm=1024):
    N = B * T
    tm = min(tm, T)
    dff = cw.shape[1]
    NT, NJ, tn = T // tm, dff // FFN_TN, FFN_TN
    last8 = N // 8 - 1

    def body(d_ref, nx_ref, cw_ref, da_ref):
        t = pl.program_id(2)
        dd = d_ref[...]
        row = _iota(dd.shape, 0)
        last = t == NT - 1
        n0 = jnp.where(last, 0.0, nx_ref[0:1, :])
        n1 = jnp.where(last, 0.0, nx_ref[1:2, :])
        d1 = jnp.where(row == tm - 1, n0, pltpu.roll(dd, tm - 1, 0))
        d2 = jnp.where(row == tm - 1, n1, jnp.where(row == tm - 2, n0, pltpu.roll(dd, tm - 2, 0)))
        w = cw_ref[...]
        da_ref[...] = (w[2:3, :] * dd + w[1:2, :] * d1 + w[0:1, :] * d2).astype(BF16)

    return pl.pallas_call(
        body, name="ffn_act_bwd2", grid=(B, NJ, NT),
        in_specs=[pl.BlockSpec((tm, tn), lambda b, j, t: (b * NT + t, j)),
                  pl.BlockSpec((8, tn), lambda b, j, t: (jnp.minimum((b * NT + t + 1) * (tm // 8), last8), j)),
                  pl.BlockSpec((3, tn), lambda b, j, t: (0, j))],
        out_specs=pl.BlockSpec((tm, tn), lambda b, j, t: (b * NT + t, j)), out_shape=S((N, dff), BF16),
        compiler_params=_cp(("parallel", "parallel", "parallel")),
    )(dac, dac, cw)


def _fold_rows(p, name):
    r, c = p.shape[0] // 8, p.shape[1]

    def body(p_ref, o_ref):
        for j in range(r):
            o_ref[j:j + 1, :] = jnp.sum(p_ref[8 * j:8 * (j + 1), :], axis=0, keepdims=True)

    return pl.pallas_call(body, name=name, out_shape=S((r, c), F32), compiler_params=_cp())(p)


def _small_reduce(lbl, dg_mix, dg_mem, dlb_p, dgn_p, dfb_p, dgq_p, dgk_p, dmq_p, dmk_p, dg_ffn, dcb_p, loss_p):
    d, dff = dg_mix.shape[1], dcb_p.shape[1]
    nbh = dlb_p.shape[0] // (8 * HG_H)

    def colsum(ref):
        return jnp.sum(ref[...], axis=0, keepdims=True)

    def body(lbl_ref, mix_ref, mem_ref, dlb_ref, dgn_ref, dfb_ref, dgq_ref, dgk_ref, dmq_ref, dmk_ref, ffn_ref, dcb_ref,
             ls_ref, o_mix, o_mem, o_lb, o_hgn, o_fb, o_fq, o_fk, o_mq, o_mk, o_ffn, o_cb, o_loss):
        o_mix[...], o_mem[...], o_ffn[...], o_cb[...] = colsum(mix_ref), colsum(mem_ref), colsum(ffn_ref), colsum(dcb_ref)
        o_hgn[...], o_fb[...], o_mq[...], o_mk[...] = colsum(dgn_ref), colsum(dfb_ref), colsum(dmq_ref), colsum(dmk_ref)
        for src, dst in ((dgq_ref, o_fq), (dgk_ref, o_fk)):
            v = colsum(src)
            dst[...] = v + pltpu.roll(v, FOX_D, 1)
        o_loss[...] = jnp.zeros((1, LANE), F32) + jnp.sum(colsum(ls_ref), axis=-1, keepdims=True)
        logits = lbl_ref[...]
        e = jnp.exp(logits - jnp.max(logits, axis=0, keepdims=True))
        pr = e / jnp.sum(e, axis=0, keepdims=True)
        rows = _iota((8, LANE), 0)
        for h in range(HG_H):
            acc = jnp.zeros((8, LANE), F32)
            for b in range(nbh):
                acc = acc + dlb_ref[8 * (b * HG_H + h):8 * (b * HG_H + h + 1), :]
            dlb = jnp.sum(acc, axis=0, keepdims=True)
            c = slice(LANE * h, LANE * (h + 1))
            p0 = pr[0:1, c]
            first = _iota((logits.shape[0], LANE), 0) == 0
            o_lb[:, c] = pr[:, c] * (jnp.where(first, 1.0, 0.0) - p0) * dlb

    outs = [S((1, d), F32), S((1, d), F32), S(lbl.shape, F32)] + [S((1, LANE), F32)] * 6 + \
           [S((1, d), F32), S((1, dff), F32), S((1, LANE), F32)]
    return pl.pallas_call(body, name="small_reduce", out_shape=outs, compiler_params=_cp())(
        lbl, dg_mix, dg_mem, dlb_p, dgn_p, dfb_p, dgq_p, dgk_p, dmq_p, dmk_p, dg_ffn, dcb_p, loss_p)


def _in_col_pieces():
    hw, fw = HG_H * HG_D, FOX_H * FOX_D
    fox0, ff0 = 4 * hw, 4 * hw + 3 * fw
    mq0 = ff0 + FOX_H
    gate0 = mq0 + MEM_H * MEM_D
    pieces = []
    for p in range(FOX_P):
        pieces += [(fox0 + j * fw + LANE * p, LANE) for j in range(3)]
    pieces.append((mq0, MEM_H * MEM_D))
    for h in range(HG_H):
        pieces += [(j * hw + HG_D * h, HG_D) for j in range(4)]
    pieces.append((gate0, C_FF - C_GATE))
    pieces.append((ff0, FOX_H))
    return pieces


def _perm_from_blocks(blocks):
    n_blk, _, c = blocks.shape
    parts = []
    for s, n in _in_col_pieces():
        lo = s
        while lo < s + n:
            d = lo // c
            hi = min(s + n, (d + 1) * c)
            parts.append(blocks[d][:, lo - d * c:hi - d * c])
            lo = hi
    parts.append(jnp.zeros((blocks.shape[1], C_END - C_FF - FOX_H), blocks.dtype))
    return jnp.concatenate(parts, axis=1)


def _unperm_blocks(segs, n_blk):
    starts = [0]
    for a in segs:
        starts.append(starts[-1] + a.shape[1])
    new_start, placed = 0, []
    for s, n in _in_col_pieces():
        placed.append((s, new_start, n))
        new_start += n
    placed.sort()
    c = sum(n for _, _, n in placed) // n_blk
    blocks = []
    for d in range(n_blk):
        parts = []
        for s, ns, n in placed:
            lo, hi = max(s, d * c), min(s + n, (d + 1) * c)
            if lo < hi:
                i = max(j for j in range(len(segs)) if starts[j] <= ns)
                parts.append(segs[i][:, ns + lo - s - starts[i]:ns + hi - s - starts[i]])
        blocks.append(jnp.concatenate(parts, axis=1))
    return jnp.stack(blocks)


def _local_step(x2, mem2, tgt, sm, W, B, T, M, ex=None):
    fbias = jnp.pad(sm["fox_f_bias"], ((0, 0), (0, LANE - FOX_H)))
    gq2 = jnp.concatenate([sm["fox_q_norm_g"]] * 2, axis=1)
    gk2 = jnp.concatenate([sm["fox_k_norm_g"]] * 2, axis=1)
    lbl = sm["hgrn_lb_logits"]
    if ex:
        h, *first = _rmsnorm_cast(x2, sm["norm_mix_g"], "norm_mix", gather=ex.first_blocks())
        W = ex.unpack_first(first)
    else:
        h = _rmsnorm_cast(x2, sm["norm_mix_g"], "norm_mix")
    z = _mm_nn(h, W["w_in"], BF16, "proj_in", 512, 2432)
    memn = _rmsnorm_cast(mem2, sm["norm_mem_g"], "norm_mem", tm=256)
    memkv = _mm_nn(memn, W["mem_kv_w"], F32, "proj_memkv", 256, 512)
    ya, o_raw, states, a_mat = _hgrn_fwd(z, lbl, sm["hgrn_norm_g"], B, T)
    fc, fct = _fox_gate_fwd(z, fbias, B, T)
    yb, lse, *late = _fox_fwd(z, fc, fct, gq2, gk2, B, T, gather=ex.late_blocks() if ex else ())
    if ex:
        W = {**W, **ex.unpack_late(late)}
    yc = _mem_fwd(z, memkv, sm["mem_q_norm_g"], sm["mem_k_norm_g"], B, T, M)
    x1, merged, ua, ub, uc, h2 = _merge_fwd(ya, yb, yc, z, x2, W["w_br_hgrn"], W["w_br_fox"], W["w_br_mem"], W["w_out"],
                                            sm["norm_ffn_g"])
    up = _mm_nn(h2, W["ffn_w_up"], BF16, "ffn_up", 512, FFN_TN)
    yf, cdf = _ffn_act_fwd(up, W["ffn_conv_w"], sm["ffn_conv_b"], B, T)
    dx2, loss_p = _ffn_down_loss(yf, W["ffn_w_down"], x1, tgt)
    dff = W["ffn_conv_w"].shape[1]
    dac, dv, dcw_p, dcb_p = _ffn_act_bwd1(dx2, W["ffn_w_down"], up, cdf, W["ffn_conv_w"], sm["ffn_conv_b"], B, T)
    da = _ffn_act_bwd2(dac, W["ffn_conv_w"], B, T)
    g = {"ffn_conv_w": _fold_rows(dcw_p, "g_conv_w")}
    g["ffn_w_down"] = _mm_tn(yf, dx2, "g_w_down", TN_TM, 512)
    dh2 = _mm_nt_sum([(da, 0, dff, 0), (dv, 0, dff, dff)], W["ffn_w_up"], "dh2", 512)
    g["ffn_w_up"] = [_mm_tn(h2, da, "g_w_up_a", TN_TM, FFN_TN), _mm_tn(h2, dv, "g_w_up_v", TN_TM, FFN_TN)]
    dx1, dg_ffn = _rmsnorm_bwd(dh2, x1, sm["norm_ffn_g"], dx2, "norm_ffn_bwd")
    g["w_out"] = _mm_tn(merged, dx1, "g_w_out", TN_TM, 512)
    dgate, dya, dyb, dyc, dua, dub, duc = _merge_bwd(dx1, z, ua, ub, uc, W["w_br_hgrn"], W["w_br_fox"], W["w_br_mem"],
                                                    W["w_out"])
    g["w_br_hgrn"] = _mm_tn(ya, dua, "g_w_br_hgrn", TN_TM, 512)
    g["w_br_fox"] = _mm_tn(yb, dub, "g_w_br_fox", TN_TM, 512)
    g["w_br_mem"] = _mm_tn(yc, duc, "g_w_br_mem", TN_TM, 512)
    early_pk = ex.early_grads(g) if ex else ()
    dz_hg, dlb_p, dgn_p, *early_sib = _hgrn_bwd(z, o_raw, states, a_mat, dya, lbl, sm["hgrn_norm_g"], B, T,
                                                swap_sibling=early_pk)
    dz_fox, dfc, dgq_p, dgk_p, *early_chips = _fox_bwd(z, dyb, yb, lse, fc, fct, gq2, gk2, B, T,
                                                       swap=ex.pair_sums(early_pk, early_sib, "early") if ex else ())
    dz_ff, dfb_p = _fox_gate_bwd(dfc, z, fbias, B, T)
    dz_mq, dkv, dmq_p, dmk_p = _mem_bwd(z, memkv, dyc, sm["mem_q_norm_g"], sm["mem_k_norm_g"], B, T, M)
    g["mem_kv_w"] = _mm_tn(memn, dkv, "g_mem_kv_w", 256, 512)
    dmemn = _mm_nt_sum([(dkv, 0, dkv.shape[1], 0)], W["mem_kv_w"], "d_memn", 256)
    _, dg_mem = _rmsnorm_bwd(dmemn, mem2, sm["norm_mem_g"], None, "norm_mem_bwd", tm=256)
    d = x2.shape[1]
    parts = [(dz_fox, 0, C_MQ - C_FOX, C_FOX), (dz_mq, 0, C_HG - C_MQ, C_MQ), (dz_hg, 0, C_GATE - C_HG, C_HG)]
    parts += [(dgate, d * k, d, C_GATE + d * k) for k in range(3)] + [(dz_ff, 0, C_END - C_FF, C_FF)]
    g["w_in"] = [_mm_tn(h, dzs, "g_w_in_%d" % i, 2 * TN_TM, min(512, dzs.shape[1]))
                 for i, dzs in enumerate((dz_fox, dz_mq, dz_hg, dgate, dz_ff))]
    sums = None
    if ex:
        last_pk = ex.last_grads(g)
        last_sib = _swap_with_sibling(last_pk, "rs_sibling_last")
        dh, last_chips = _mm_nt_sum(parts, W["w_in"], "dh", 512, swap=ex.pair_sums(last_pk, last_sib, "last"))
        sums = (ex.final_sums(early_pk, early_sib, early_chips, "early"),
                ex.final_sums(last_pk, last_sib, last_chips, "last"))
    else:
        dh = _mm_nt_sum(parts, W["w_in"], "dh", 512)
    grad_x, dg_mix = _rmsnorm_bwd(dh, x2, sm["norm_mix_g"], dx1, "norm_mix_bwd")
    small = _small_reduce(lbl, dg_mix, dg_mem, dlb_p, dgn_p, dfb_p, dgq_p, dgk_p, dmq_p, dmk_p, dg_ffn, dcb_p, loss_p)
    names = ("norm_mix_g", "norm_mem_g", "hgrn_lb_logits", "hgrn_norm_g", "fox_f_bias", "fox_q_norm_g", "fox_k_norm_g",
             "mem_q_norm_g", "mem_k_norm_g", "norm_ffn_g", "ffn_conv_b", "loss")
    g.update(dict(zip(names, small)))
    return grad_x, g, sums


ANY = pl.BlockSpec(memory_space=pl.ANY)


def _position():
    return lax.axis_index("x"), lax.axis_index("y"), lax.axis_index("c")


def _all_gather(blocks, name):
    nb = len(blocks)

    def body(*refs):
        start, forward, finish = _gather_phases(refs[:nb], refs[nb:2 * nb], *refs[2 * nb:])
        start()
        forward()
        finish()

    return pl.pallas_call(
        body, name=name, out_shape=_gather_shapes(blocks), in_specs=[ANY] * nb, out_specs=[ANY] * nb,
        scratch_shapes=_gather_sems(nb),
    )(*blocks)


def _hosting(body, n_in, n_out, n_scratch, n_x, make_phases, grid):
    n_steps = math.prod(grid)

    def hosted(*refs):
        a = n_in + n_x
        b = a + n_out + n_x
        ins, xs = refs[:n_in], refs[n_in:a]
        outs, x_outs = refs[a:a + n_out], refs[a + n_out:b]
        scratch, sems = refs[b:b + n_scratch], refs[b + n_scratch:]
        step = 0
        for ax, n in enumerate(grid):
            step = step * n + pl.program_id(ax)
        phases = make_phases(xs, x_outs, *sems)
        pl.when(step == 0)(phases[0])
        for ph in phases[1:-1]:
            pl.when(step == n_steps // 2)(ph)
        body(*ins, *outs, *scratch)
        pl.when(step == n_steps - 1)(phases[-1])

    return hosted


def _gather_shapes(blocks):
    return [S((N_DEV,) + b.shape, b.dtype) for b in blocks]


def _gather_sems(nb):
    return [pltpu.SemaphoreType.DMA((7 * nb,)), pltpu.SemaphoreType.DMA((7 * nb,)), pltpu.SemaphoreType.DMA((nb,))]


def _gather_phases(x_refs, out_refs, send_sems, recv_sems, local_sems):
    nb = len(x_refs)
    x, y, c = _position()
    me, sibling = (x, y, c), (x, y, 1 - c)
    chips = [(1 - x, y), (x, 1 - y), (1 - x, 1 - y)]

    def copy(i, k, blk, to, own=False):
        px, py, pc = blk
        slot = out_refs[i].at[4 * px + 2 * py + pc]
        return pltpu.make_async_remote_copy(
            src_ref=x_refs[i] if own else slot, dst_ref=slot, send_sem=send_sems.at[7 * i + k],
            recv_sem=recv_sems.at[7 * i + k], device_id=to, device_id_type=MESH)

    def mine(i):
        return pltpu.make_async_copy(x_refs[i], out_refs[i].at[4 * x + 2 * y + c], local_sems.at[i])

    def first(i):
        return [copy(i, 0, me, sibling, own=True)] + [copy(i, 1 + j, me, (*chip, c), own=True)
                                                     for j, chip in enumerate(chips)]

    def passed(i, j):
        return copy(i, 4 + j, (*chips[j], c), sibling)

    def start():
        for i in range(nb):
            mine(i).start()
            for cp in first(i):
                cp.start()

    def forward():
        for i in range(nb):
            for j, chip in enumerate(chips):
                copy(i, 1 + j, (*chip, c), me).wait_recv()
                passed(i, j).start()

    def finish():
        for i in range(nb):
            copy(i, 0, sibling, me).wait_recv()
            for j, chip in enumerate(chips):
                copy(i, 4 + j, (*chip, 1 - c), me).wait_recv()
        for i in range(nb):
            for cp in first(i) + [passed(i, j) for j in range(3)]:
                cp.wait_send()
            mine(i).wait()

    return start, forward, finish


def _swap_with_sibling(pks, name):
    nb = len(pks)

    def body(*refs):
        start, finish = _sibling_swap_phases(refs[:nb], refs[nb:2 * nb], *refs[2 * nb:])
        start()
        finish()

    return pl.pallas_call(
        body, name=name, out_shape=_sibling_swap_shapes(pks), in_specs=[ANY] * nb, out_specs=[ANY] * nb,
        scratch_shapes=_sibling_swap_sems(nb),
    )(*pks)


def _sibling_swap_shapes(pks):
    return [S((4,) + p.shape[1:], p.dtype) for p in pks]


def _sibling_swap_sems(nb):
    return [pltpu.SemaphoreType.DMA((4 * nb,)), pltpu.SemaphoreType.DMA((4 * nb,))]


def _sibling_swap_phases(pk_refs, out_refs, send_sems, recv_sems):
    nb = len(pk_refs)
    x, y, c = _position()

    def copies():
        return [pltpu.make_async_remote_copy(
            src_ref=pk_refs[i].at[2 * k + 1 - c], dst_ref=out_refs[i].at[k], send_sem=send_sems.at[4 * i + k],
            recv_sem=recv_sems.at[4 * i + k], device_id=(x, y, 1 - c), device_id_type=MESH)
            for i in range(nb) for k in range(4)]

    def start():
        for cp in copies():
            cp.start()

    def finish():
        for cp in copies():
            cp.wait()

    return start, finish


def _swap_between_chips(pbs, name):
    nb = len(pbs)

    def body(*refs):
        start, finish = _chip_swap_phases(refs[:nb], refs[nb:2 * nb], *refs[2 * nb:])
        start()
        finish()

    return pl.pallas_call(
        body, name=name, out_shape=[S(p.shape, p.dtype) for p in pbs], in_specs=[ANY] * nb, out_specs=[ANY] * nb,
        scratch_shapes=_chip_swap_sems(nb),
    )(*pbs)


def _chip_swap_sems(nb):
    return [pltpu.SemaphoreType.DMA((3 * nb,)), pltpu.SemaphoreType.DMA((3 * nb,)), pltpu.SemaphoreType.DMA((nb,))]


def _chip_swap_phases(pb_refs, out_refs, send_sems, recv_sems, local_sems):
    nb = len(pb_refs)
    x, y, c = _position()
    me = 2 * x + y
    chips = [(1 - x, y), (x, 1 - y), (1 - x, 1 - y)]

    def local(i):
        return pltpu.make_async_copy(pb_refs[i].at[me], out_refs[i].at[me], local_sems.at[i])

    def send(i, j):
        cx, cy = chips[j]
        return pltpu.make_async_remote_copy(
            src_ref=pb_refs[i].at[2 * cx + cy], dst_ref=out_refs[i].at[me], send_sem=send_sems.at[3 * i + j],
            recv_sem=recv_sems.at[3 * i + j], device_id=(cx, cy, c), device_id_type=MESH)

    def arrival(i, j):
        cx, cy = chips[j]
        return pltpu.make_async_remote_copy(
            src_ref=pb_refs[i].at[me], dst_ref=out_refs[i].at[2 * cx + cy], send_sem=send_sems.at[3 * i + j],
            recv_sem=recv_sems.at[3 * i + j], device_id=(cx, cy, c), device_id_type=MESH)

    def start():
        for i in range(nb):
            local(i).start()
            for j in range(3):
                send(i, j).start()

    def finish():
        for i in range(nb):
            for j in range(3):
                arrival(i, j).wait_recv()
        for i in range(nb):
            for j in range(3):
                send(i, j).wait_send()
            local(i).wait()

    return start, finish


def _row_tile(r):
    return max(t for t in range(16, min(r, 512) + 1, 16) if r % t == 0)


def _pair_sum_cast(pk, recv, core, name):
    _, r, l = pk.shape
    tr = _row_tile(r)

    def body(c_ref, a_ref, b_ref, o_ref):
        o_ref[...] = (a_ref[...] + b_ref[...]).astype(BF16)

    return pl.pallas_call(
        body, name=name,
        grid_spec=pltpu.PrefetchScalarGridSpec(
            num_scalar_prefetch=1, grid=(4, r // tr),
            in_specs=[pl.BlockSpec((None, tr, l), lambda k, i, c: (2 * k + c[0], i, 0)),
                      pl.BlockSpec((None, tr, l), lambda k, i, c: (k, i, 0))],
            out_specs=pl.BlockSpec((None, tr, l), lambda k, i, c: (k, i, 0))),
        out_shape=S((4, r, l), BF16), compiler_params=_cp(("parallel", "parallel")),
    )(core, pk, recv)


def _final_sum(pk, recv_sib, recv_chips, slot, chip, name):
    _, r, l = pk.shape
    tr = _row_tile(r)

    def body(s_ref, k_ref, a_ref, b_ref, rc_ref, o_ref):
        base = a_ref[...] + b_ref[...]
        acc = jnp.zeros_like(base)
        for j in range(4):
            acc = acc + jnp.where(k_ref[0] == j, base, rc_ref[j].astype(F32))
        o_ref[...] = acc

    return pl.pallas_call(
        body, name=name,
        grid_spec=pltpu.PrefetchScalarGridSpec(
            num_scalar_prefetch=2, grid=(r // tr,),
            in_specs=[pl.BlockSpec((None, tr, l), lambda i, s, k: (s[0], i, 0)),
                      pl.BlockSpec((None, tr, l), lambda i, s, k: (k[0], i, 0)),
                      pl.BlockSpec((4, tr, l), lambda i, s, k: (0, i, 0))],
            out_specs=pl.BlockSpec((tr, l), lambda i, s, k: (i, 0))),
        out_shape=S((r, l), F32), compiler_params=_cp(("parallel",)),
    )(slot, chip, pk, recv_sib, recv_chips)


def _adamw_math(w, g, m, v):
    m = ADAM_B1 * m + (1.0 - ADAM_B1) * g
    v = ADAM_B2 * v + (1.0 - ADAM_B2) * (g * g)
    m_hat = m / (1.0 - ADAM_B1 ** ADAM_STEP)
    v_hat = v / (1.0 - ADAM_B2 ** ADAM_STEP)
    return -ADAM_LR * (m_hat / (jnp.sqrt(v_hat) + ADAM_EPS) + ADAM_WD * w), m, v


def _adamw(w, g, m, v, name):
    r, c = w.shape
    tr = 256 if r % 256 == 0 else r

    def body(w_ref, g_ref, m_ref, v_ref, d_ref, nm_ref, nv_ref):
        d_ref[...], nm_ref[...], nv_ref[...] = _adamw_math(w_ref[...], g_ref[...], m_ref[...], v_ref[...])

    tile = pl.BlockSpec((tr, c), lambda i: (i, 0))
    return pl.pallas_call(
        body, name=name, grid=(r // tr,), in_specs=[tile] * 4, out_specs=[tile] * 3, out_shape=[S((r, c), F32)] * 3,
        compiler_params=_cp(("parallel",)),
    )(w, g, m, v)


def _small_update(gathered, w, m, v):
    def body(ga_ref, w_ref, m_ref, v_ref, g_ref, d_ref, nm_ref, nv_ref):
        g = ga_ref[0]
        for k in range(1, N_DEV):
            g = g + ga_ref[k]
        g_ref[...] = g
        d_ref[...], nm_ref[...], nv_ref[...] = _adamw_math(w_ref[...], g, m_ref[...], v_ref[...])

    return pl.pallas_call(body, name="small_update", out_shape=[S(w.shape, F32)] * 4, compiler_params=_cp())(
        gathered, w, m, v)


BIG = ("w_in", "mem_kv_w", "w_br_hgrn", "w_br_fox", "w_br_mem", "w_out", "ffn_w_up", "ffn_conv_w", "ffn_w_down")
GROUP_ROWS = ("w_out", "ffn_w_down")
GROUP_LANE = ("w_br_hgrn", "w_br_fox", "w_br_mem")
LANE_GROUP_ROWS = 224
SMALL = ("norm_mix_g", "norm_mem_g", "hgrn_lb_logits", "hgrn_norm_g", "fox_f_bias", "fox_q_norm_g", "fox_k_norm_g",
         "mem_q_norm_g", "mem_k_norm_g", "norm_ffn_g", "ffn_conv_b")


def _rows_of(n_elems):
    return -(-n_elems // LANE)


def _to_rows(a, lead=0):
    flat = a.reshape(a.shape[:lead] + (-1,))
    pad = (-flat.shape[-1]) % LANE
    if pad:
        flat = jnp.pad(flat, [(0, 0)] * lead + [(0, pad)])
    return flat.reshape(a.shape[:lead] + (-1, LANE))


def _stack_rows(parts, lead, total_rows):
    buf = jnp.concatenate(parts, axis=lead)
    pad = total_rows - buf.shape[lead]
    return jnp.pad(buf, [(0, 0)] * lead + [(0, pad), (0, 0)])


def _round_up(n, k):
    return -(-n // k) * k


def _from_rows(rows, shape, lead=0):
    n = math.prod(shape)
    return rows.reshape(rows.shape[:lead] + (-1,))[..., :n].reshape(rows.shape[:lead] + tuple(shape))


def _blocks_to_full(blocks, kind):
    n, a, b = blocks.shape
    return blocks.transpose(1, 0, 2).reshape(a, n * b) if kind == "col" else blocks.reshape(n * a, b)


def _full_to_blocks(full, kind, n=N_DEV):
    a, b = full.shape
    return full.reshape(a, n, b // n).transpose(1, 0, 2) if kind == "col" else full.reshape(n, a // n, b)


def _lane_group_rows(shard):
    n_lane = sum(shard[n].shape[0] for n in GROUP_LANE)
    n_cw = shard["ffn_conv_w"].size
    return n_lane, _rows_of(3 * n_cw), _rows_of(n_cw), _round_up(n_lane + _rows_of(3 * n_cw), LANE_GROUP_ROWS)


def _split_bf16x3(x):
    hi = x.astype(BF16)
    r1 = x - hi.astype(F32)
    mid = r1.astype(BF16)
    return jnp.stack([hi, mid, (r1 - mid.astype(F32)).astype(BF16)])


class _Exchange:
    def __init__(self, shard):
        self.shard = shard
        xi, yi, ci = _position()
        self.core = ci.astype(jnp.int32).reshape(1)
        self.chip = (2 * xi + yi).astype(jnp.int32).reshape(1)
        self.n_lane, self.r_pieces, self.r_vals, self.r_lane = _lane_group_rows(shard)

    def first_blocks(self):
        return [self.shard["w_in"].astype(BF16), self.shard["mem_kv_w"].astype(BF16)]

    def unpack_first(self, gathered):
        return {"w_in": _perm_from_blocks(gathered[0]), "mem_kv_w": _blocks_to_full(gathered[1], "row")}

    def late_blocks(self):
        sh = self.shard
        lane_rows = [sh[n].astype(BF16) for n in GROUP_LANE] + [_to_rows(_split_bf16x3(sh["ffn_conv_w"]))]
        return [sh[n].astype(BF16) for n in GROUP_ROWS] + [sh["ffn_w_up"].astype(BF16),
                                                           _stack_rows(lane_rows, 0, self.r_lane)]

    def unpack_late(self, gathered):
        *rows, gc, gd = gathered
        sh = self.shard
        W = {"ffn_w_up": _blocks_to_full(gc, "col")}
        for n, blocks in zip(GROUP_ROWS, rows):
            W[n] = _blocks_to_full(blocks, "row")
        r0 = 0
        for n in GROUP_LANE:
            W[n] = _blocks_to_full(gd[:, r0:r0 + sh[n].shape[0]], "col")
            r0 += sh[n].shape[0]
        cw = _from_rows(gd[:, self.n_lane:self.n_lane + self.r_pieces], (3,) + sh["ffn_conv_w"].shape, lead=1).astype(F32)
        W["ffn_conv_w"] = _blocks_to_full(cw[:, 0] + cw[:, 1] + cw[:, 2], "col")
        return W

    def early_grads(self, g):
        cw_rows = _to_rows(_full_to_blocks(g["ffn_conv_w"], "col"), lead=1)
        return [_full_to_blocks(g[n], "row") for n in GROUP_ROWS] + [
            jnp.concatenate([_full_to_blocks(h, "col", N_DEV // 2) for h in g["ffn_w_up"]], axis=0),
            _stack_rows([_full_to_blocks(g[n], "col") for n in GROUP_LANE] + [cw_rows], 1, self.r_lane)]

    def last_grads(self, g):
        return [_unperm_blocks(g["w_in"], N_DEV), _full_to_blocks(g["mem_kv_w"], "row")]

    def pair_sums(self, pks, recv_sib, tag):
        return [_pair_sum_cast(p, r, self.core, "rs_pair_sum_%s%d" % (tag, i))
                for i, (p, r) in enumerate(zip(pks, recv_sib))]

    def final_sums(self, pks, recv_sib, recv_chips, tag):
        return [_final_sum(p, rs, rc, 2 * self.chip + self.core, self.chip, "rs_final_sum_%s%d" % (tag, i))
                for i, (p, rs, rc) in enumerate(zip(pks, recv_sib, recv_chips))]

    def unpack_grads(self, early, last):
        sh = self.shard
        *rows, g_up, g_lane = early
        g_shard = {"w_in": last[0], "mem_kv_w": last[1], "ffn_w_up": g_up, **dict(zip(GROUP_ROWS, rows))}
        r0 = 0
        for n in GROUP_LANE:
            g_shard[n] = g_lane[r0:r0 + sh[n].shape[0]]
            r0 += sh[n].shape[0]
        g_shard["ffn_conv_w"] = _from_rows(g_lane[self.n_lane:self.n_lane + self.r_vals], sh["ffn_conv_w"].shape)
        return g_shard


def kernel(x, mem, norm_mix_g, norm_mem_g, w_in, hgrn_lb_logits, hgrn_norm_g, fox_f_bias, fox_q_norm_g, fox_k_norm_g, mem_kv_w, mem_q_norm_g, mem_k_norm_g, w_br_hgrn, w_br_fox, w_br_mem, w_out, norm_ffn_g, ffn_w_up, ffn_conv_w, ffn_conv_b, ffn_w_down, loss_target, m_norm_mix_g, m_norm_mem_g, m_w_in, m_hgrn_lb_logits, m_hgrn_norm_g, m_fox_f_bias, m_fox_q_norm_g, m_fox_k_norm_g, m_mem_kv_w, m_mem_q_norm_g, m_mem_k_norm_g, m_w_br_hgrn, m_w_br_fox, m_w_br_mem, m_w_out, m_norm_ffn_g, m_ffn_w_up, m_ffn_conv_w, m_ffn_conv_b, m_ffn_w_down, v_norm_mix_g, v_norm_mem_g, v_w_in, v_hgrn_lb_logits, v_hgrn_norm_g, v_fox_f_bias, v_fox_q_norm_g, v_fox_k_norm_g, v_mem_kv_w, v_mem_q_norm_g, v_mem_k_norm_g, v_w_br_hgrn, v_w_br_fox, v_w_br_mem, v_w_out, v_norm_ffn_g, v_ffn_w_up, v_ffn_conv_w, v_ffn_conv_b, v_ffn_w_down):
    given = dict(locals())
    order = ("norm_mix_g", "norm_mem_g", "w_in", "hgrn_lb_logits", "hgrn_norm_g", "fox_f_bias", "fox_q_norm_g",
             "fox_k_norm_g", "mem_kv_w", "mem_q_norm_g", "mem_k_norm_g", "w_br_hgrn", "w_br_fox", "w_br_mem", "w_out",
             "norm_ffn_g", "ffn_w_up", "ffn_conv_w", "ffn_conv_b", "ffn_w_down")
    B, T, D = x.shape
    M = mem.shape[1]
    shard = {n: given[n][0] if n in BIG else given[n] for n in order}
    mom = {n: (given["m_" + n][0], given["v_" + n][0]) if n in BIG else (given["m_" + n], given["v_" + n])
           for n in order}
    shard["hgrn_lb_logits"] = hgrn_lb_logits
    for n in ("norm_mix_g", "norm_mem_g", "hgrn_norm_g", "fox_f_bias", "fox_q_norm_g", "fox_k_norm_g", "mem_q_norm_g",
              "mem_k_norm_g", "norm_ffn_g", "ffn_conv_b"):
        shard[n] = given[n].reshape(1, -1)

    ex = _Exchange(shard)
    sm = {n: shard[n] for n in SMALL}
    grad_x, g, sums = _local_step(x.reshape(B * T, D), mem.reshape(B * M, D), loss_target.reshape(B * T, D), sm, None,
                                  B, T, M, ex)
    g_shard = ex.unpack_grads(*sums)

    sg = {n: g[n] for n in SMALL}
    sg["fox_f_bias"] = g["fox_f_bias"][:, :FOX_H]
    sg["fox_q_norm_g"] = g["fox_q_norm_g"][:, :FOX_D]
    sg["fox_k_norm_g"] = g["fox_k_norm_g"][:, :FOX_D]
    slayout, row0 = {}, 0
    for n in SMALL:
        nr = _rows_of(shard[n].size)
        slayout[n] = (row0, nr)
        row0 += nr
    loss_row = row0
    r_small = _round_up(row0 + 1, 8)

    def pack_small(d, with_loss=None):
        rows = [_to_rows(d[n]) for n in SMALL]
        rows.append(with_loss if with_loss is not None else jnp.zeros((1, LANE), F32))
        return _stack_rows(rows, 0, r_small)

    sgath, = _all_gather([pack_small(sg, g["loss"])], "ag_small")
    s_g, s_d, s_m, s_v = _small_update(sgath, pack_small(shard), pack_small({n: mom[n][0].reshape(shard[n].shape) for n in SMALL}),
                                       pack_small({n: mom[n][1].reshape(shard[n].shape) for n in SMALL}))
    loss = s_g[loss_row, 0]

    grads, deltas, new_m, new_v = {}, {}, {}, {}
    for n in BIG:
        gn = g_shard[n]
        d, nm, nv = _adamw(shard[n], gn, mom[n][0], mom[n][1], "adamw_" + n)
        grads[n], deltas[n], new_m[n], new_v[n] = (a[None] for a in (gn, d, nm, nv))
    for n in SMALL:
        r0, nr = slayout[n]
        for dst, src in ((grads, s_g), (deltas, s_d), (new_m, s_m), (new_v, s_v)):
            dst[n] = _from_rows(src[r0:r0 + nr], given[n].shape)
    return (loss, grad_x.reshape(B, T, D), *[grads[n] for n in order], *[deltas[n] for n in order],
            *[new_m[n] for n in order], *[new_v[n] for n in order])
```

```python
import functools
import math

import jax
import jax.numpy as jnp
from jax import lax
from jax.experimental import pallas as pl
from jax.experimental.pallas import tpu as pltpu

F32, BF16 = jnp.float32, jnp.bfloat16
S = jax.ShapeDtypeStruct
MESH = pl.DeviceIdType.MESH

N_DEV = 8
EPS = 1e-6
LANE = 128
CHUNK = 64
SUB = 16
HG_H, HG_D = 4, 128
HG_GROUP_FWD = 4
HG_GROUP = 2
FOX_H, FOX_D = 8, 64
FOX_P = FOX_H // 2
MEM_H, MEM_D = 4, 128
NEG = -1e30
VMEM_LIMIT = 56 * 2**20

ADAM_LR, ADAM_B1, ADAM_B2, ADAM_EPS, ADAM_WD, ADAM_STEP = 0.001, 0.9, 0.999, 1e-08, 0.01, 10

C_FOX, C_MQ, C_HG, C_GATE, C_FF, C_END = 0, 1536, 2048, 4096, 7168, 7296


def _cp(sem=None):
    return pltpu.CompilerParams(dimension_semantics=sem, vmem_limit_bytes=VMEM_LIMIT)


def _dot(a, b, dims, prec=None):
    return lax.dot_general(a, b, (dims, ((), ())), preferred_element_type=F32, precision=prec)


def _nn(a, b, prec=None):
    return _dot(a, b, ((1,), (0,)), prec)


def _nt(a, b, prec=None):
    return _dot(a, b, ((1,), (1,)), prec)


def _tn(a, b, prec=None):
    return _dot(a, b, ((0,), (0,)), prec)


def _b(x):
    return x.astype(BF16)


def _mm3(fn, a, b):
    ah, bh = _b(a), _b(b)
    return fn(ah, bh) + fn(ah, _b(b - bh.astype(F32))) + fn(_b(a - ah.astype(F32)), bh)


def _iota(shape, dim):
    return lax.broadcasted_iota(jnp.int32, shape, dim)


def _rowsum8(x):
    r, d = x.shape
    return jnp.sum(x.reshape(r // 8, 8, d), axis=0)


def _rmsnorm_cast(x, g, name, tm=1024, gather=()):
    n, d = x.shape
    nga = len(gather)

    def body(x_ref, g_ref, o_ref):
        v = x_ref[...]
        r = lax.rsqrt(jnp.mean(v * v, axis=-1, keepdims=True) + EPS)
        o_ref[...] = (v * r * g_ref[...]).astype(BF16)

    if nga:
        body = _hosting(body, 2, 1, 0, nga, _gather_phases, (n // tm,))
    out = pl.pallas_call(
        body, name=name, grid=(n // tm,),
        in_specs=[pl.BlockSpec((tm, d), lambda i: (i, 0)), pl.BlockSpec((1, d), lambda i: (0, 0))] + [ANY] * nga,
        out_specs=[pl.BlockSpec((tm, d), lambda i: (i, 0))] + [ANY] * nga,
        out_shape=[S((n, d), BF16)] + _gather_shapes(gather), scratch_shapes=_gather_sems(nga) if nga else [],
        compiler_params=_cp(("arbitrary",) if nga else ("parallel",)),
    )(x, g, *gather)
    return out if nga else out[0]


def _rmsnorm_bwd(dh, x, g, resid, name, tm=1024):
    n, d = x.shape
    has_res = resid is not None

    def body(*refs):
        if has_res:
            dh_ref, x_ref, g_ref, r_ref, dx_ref, dg_ref = refs
        else:
            dh_ref, x_ref, g_ref, dx_ref, dg_ref = refs
        v = x_ref[...]
        dhv = dh_ref[...].astype(F32)
        r = lax.rsqrt(jnp.mean(v * v, axis=-1, keepdims=True) + EPS)
        xh = v * r
        u = dhv * g_ref[...]
        dx = r * (u - xh * jnp.mean(u * xh, axis=-1, keepdims=True))
        if has_res:
            dx = dx + r_ref[...]
        dx_ref[...] = dx

        @pl.when(pl.program_id(0) == 0)
        def _():
            dg_ref[...] = jnp.zeros_like(dg_ref)

        dg_ref[...] += _rowsum8(dhv * xh)

    tile = pl.BlockSpec((tm, d), lambda i: (i, 0))
    ins = [tile, tile, pl.BlockSpec((1, d), lambda i: (0, 0))] + ([tile] if has_res else [])
    args = (dh, x, g) + ((resid,) if has_res else ())
    return pl.pallas_call(
        body, name=name, grid=(n // tm,), in_specs=ins,
        out_specs=[tile, pl.BlockSpec((8, d), lambda i: (0, 0))],
        out_shape=[S((n, d), F32), S((8, d), F32)], compiler_params=_cp(("arbitrary",)),
    )(*args)


def _mm_nn(a, b, out_dtype, name, tm, tn):
    m, k = a.shape
    n = b.shape[1]
    assert n % tn == 0 and m % tm == 0

    def body(a_ref, b_ref, o_ref):
        o_ref[...] = _nn(a_ref[...].astype(BF16), b_ref[...].astype(BF16)).astype(out_dtype)

    return pl.pallas_call(
        body, name=name, grid=(n // tn, m // tm),
        in_specs=[pl.BlockSpec((tm, k), lambda j, i: (i, 0)), pl.BlockSpec((k, tn), lambda j, i: (0, j))],
        out_specs=pl.BlockSpec((tm, tn), lambda j, i: (i, j)), out_shape=S((m, n), out_dtype),
        compiler_params=_cp(("parallel", "parallel")),
    )(a, b)


def _mm_nt_sum(parts, w, name, tm, swap=()):
    m = parts[0][0].shape[0]
    k = w.shape[0]
    assert m % tm == 0 and all(c % n == 0 and o % n == 0 for _, c, n, o in parts)
    np_ = len(parts)
    nsw = len(swap)
    n_steps = m // tm

    def body(*refs):
        o_ref = refs[2 * np_ + nsw]
        if nsw:
            start, finish = _chip_swap_phases(refs[2 * np_:2 * np_ + nsw], refs[2 * np_ + nsw + 1:2 * np_ + 2 * nsw + 1],
                                              *refs[2 * np_ + 2 * nsw + 1:])
            pl.when(pl.program_id(0) == 0)(start)
        acc = _nt(refs[0][...].astype(BF16), refs[np_][...].astype(BF16))
        for i in range(1, np_):
            acc = acc + _nt(refs[i][...].astype(BF16), refs[np_ + i][...].astype(BF16))
        o_ref[...] = acc
        if nsw:
            pl.when(pl.program_id(0) == n_steps - 1)(finish)

    dy_specs = [pl.BlockSpec((tm, n), functools.partial(lambda i, j: (i, j), j=c // n)) for _, c, n, _ in parts]
    w_specs = [pl.BlockSpec((k, n), functools.partial(lambda i, j: (0, j), j=o // n)) for _, _, n, o in parts]
    out = pl.pallas_call(
        body, name=name, grid=(n_steps,), in_specs=dy_specs + w_specs + [ANY] * nsw,
        out_specs=[pl.BlockSpec((tm, k), lambda i: (i, 0))] + [ANY] * nsw,
        out_shape=[S((m, k), F32)] + [S(p.shape, p.dtype) for p in swap],
        scratch_shapes=_chip_swap_sems(nsw) if nsw else [],
        compiler_params=_cp(("arbitrary",) if nsw else ("parallel",)),
    )(*([p[0] for p in parts] + [w] * np_ + list(swap)))
    return (out[0], out[1:]) if nsw else out[0]


def _mm_tn(x, dy, name, tm, tn):
    m, k = x.shape
    n = dy.shape[1]
    tm = min(tm, m)
    assert m % tm == 0 and n % tn == 0

    def body(x_ref, dy_ref, o_ref):
        part = _tn(x_ref[...].astype(BF16), dy_ref[...].astype(BF16))

        @pl.when(pl.program_id(1) == 0)
        def _():
            o_ref[...] = part

        @pl.when(pl.program_id(1) > 0)
        def _():
            o_ref[...] += part

    return pl.pallas_call(
        body, name=name, grid=(n // tn, m // tm),
        in_specs=[pl.BlockSpec((tm, k), lambda j, i: (i, 0)), pl.BlockSpec((tm, tn), lambda j, i: (i, j))],
        out_specs=pl.BlockSpec((k, tn), lambda j, i: (0, j)), out_shape=S((k, n), F32),
        compiler_params=_cp(("parallel", "arbitrary")),
    )(x, dy)


def _lower_bound(logits):
    e = jnp.exp(logits - jnp.max(logits, axis=0, keepdims=True))
    return e[0:1, :] / jnp.sum(e, axis=0, keepdims=True)


def _hg_gates(fl, lb):
    sig = jax.nn.sigmoid(fl)
    f = lb + (1.0 - lb) * sig
    k = (1.0 - lb) * (1.0 - sig)
    return sig, f, k, jnp.log(f)


def _silu_and_grad(x):
    s = jax.nn.sigmoid(x)
    return x * s, s * (1.0 + x * (1.0 - s))


def _hg_rowblocks(G):
    return [None] + [G[SUB * i - 1:SUB * i, :] for i in range(1, CHUNK // SUB)]


def _hg_intra_A(qs, k, G):
    refs = _hg_rowblocks(G)
    cols = _iota((SUB, LANE), 1)
    rows = _iota((SUB, LANE), 0)
    no_keys = jnp.zeros((LANE - CHUNK, HG_D), BF16)
    blocks = []
    for i in range(CHUNK // SUB):
        lo = SUB * i
        qb, Gb = qs[lo:lo + SUB, :], G[lo:lo + SUB, :]
        diag = jnp.zeros((SUB, LANE), F32)
        for s in range(SUB):
            e = jnp.exp(jnp.minimum(Gb - G[lo + s:lo + s + 1, :], 0.0))
            col = jnp.sum(qb * k[lo + s:lo + s + 1, :] * e, axis=-1, keepdims=True)
            diag = jnp.where(cols == lo + s, col, diag)
        a = jnp.where((cols >= lo) & (cols <= rows + lo), diag, 0.0)
        if i > 0:
            qr = qb * jnp.exp(Gb - refs[i])
            kr = k * jnp.exp(jnp.minimum(refs[i] - G, 0.0))
            a = jnp.where(cols < lo, _nt(_b(qr), jnp.concatenate([_b(kr), no_keys], axis=0)), a)
        blocks.append(a)
    return jnp.concatenate(blocks, axis=0)


def _hg_intra_bwd(dA, qs, k, G):
    refs = _hg_rowblocks(G)
    cols = _iota((SUB, CHUNK), 1)
    rows16 = _iota((SUB, HG_D), 0)
    dk = jnp.zeros((CHUNK, HG_D), F32)
    dq_blocks, dk_diag_blocks = [], []
    for i in range(CHUNK // SUB):
        lo = SUB * i
        qb, Gb = qs[lo:lo + SUB, :], G[lo:lo + SUB, :]
        dAb = dA[lo:lo + SUB, :]
        dq = jnp.zeros((SUB, HG_D), F32)
        dkb = jnp.zeros((SUB, HG_D), F32)
        for s in range(SUB):
            e = jnp.exp(jnp.minimum(Gb - G[lo + s:lo + s + 1, :], 0.0))
            e = jnp.where(rows16 >= s, e, 0.0)
            dcol = jnp.sum(jnp.where(cols == lo + s, dAb, 0.0), axis=-1, keepdims=True)
            w = dcol * e
            dq = dq + w * k[lo + s:lo + s + 1, :]
            dkb = jnp.where(rows16 == s, jnp.sum(w * qb, axis=0, keepdims=True), dkb)
        if i > 0:
            e1 = jnp.exp(Gb - refs[i])
            e2 = jnp.exp(jnp.minimum(refs[i] - G, 0.0))
            dA_off = jnp.where(cols < lo, dAb, 0.0)
            dq = dq + _mm3(_nn, dA_off, k * e2) * e1
            dk = dk + _mm3(_tn, dA_off, qb * e1) * e2
        dq_blocks.append(dq)
        dk_diag_blocks.append(dkb)
    return jnp.concatenate(dq_blocks, axis=0), dk + jnp.concatenate(dk_diag_blocks, axis=0)


def _tri(n, upper=False):
    r, c = _iota((n, n), 0), _iota((n, n), 1)
    return jnp.where((c >= r) if upper else (r >= c), 1.0, 0.0).astype(BF16)


def _prefix_mm(tri, x):
    hi = x.astype(BF16)
    r1 = x - hi.astype(F32)
    mid = r1.astype(BF16)
    lo = (r1 - mid.astype(F32)).astype(BF16)
    return _nn(tri, hi) + _nn(tri, mid) + _nn(tri, lo)


def _hgrn_fwd(z, lb, gn, B, T):
    N = B * T
    NC = T // CHUNK
    ng = HG_H // HG_GROUP_FWD

    def body(z_ref, lb_ref, gn_ref, y_ref, o_ref, st_ref, a_ref, s_scr):
        lbs = _lower_bound(lb_ref[...])
        tri = _tri(CHUNK)
        s_scr[...] = jnp.zeros_like(s_scr)

        def chunk(c, carry):
            r = pl.ds(pl.multiple_of(c * CHUNK, CHUNK), CHUNK)
            for hh in range(HG_GROUP_FWD):
                zc, oc = 4 * LANE * hh, LANE * hh
                ql, fl, il, gl = (z_ref[r, zc + LANE * j:zc + LANE * (j + 1)].astype(F32) for j in range(4))
                _, _, k, logf = _hg_gates(fl, lbs[:, oc:oc + LANE])
                G = _prefix_mm(tri, logf)
                qs = ql * jax.nn.sigmoid(ql)
                st = s_scr[hh]
                st_ref[hh * NC + c] = st
                g_last = G[CHUNK - 1:CHUNK, :]
                A = _b(_hg_intra_A(qs, k, G))
                a_ref[r, oc:oc + LANE] = A
                o = _nn(A[:, 0:CHUNK], _b(il)) + _nt(_b(qs * jnp.exp(G)), _b(st))
                s_scr[hh] = st * jnp.exp(g_last) + _mm3(_tn, il, k * jnp.exp(g_last - G))
                o_ref[r, oc:oc + LANE] = o
                rstd = lax.rsqrt(jnp.mean(o * o, axis=-1, keepdims=True) + EPS)
                y_ref[r, oc:oc + LANE] = (o * rstd * gn_ref[...] * (gl * jax.nn.sigmoid(gl))).astype(BF16)
            return carry

        lax.fori_loop(0, NC, chunk, 0, unroll=4)

    gw = HG_GROUP_FWD * LANE
    cb = C_HG // (4 * gw)
    return pl.pallas_call(
        body, name="hgrn_fwd", grid=(B, ng),
        in_specs=[pl.BlockSpec((T, 4 * gw), lambda b, h: (b, cb + h)), pl.BlockSpec((lb.shape[0], gw), lambda b, h: (0, h)),
                  pl.BlockSpec((1, LANE), lambda b, h: (0, 0))],
        out_specs=[pl.BlockSpec((T, gw), lambda b, h: (b, h)), pl.BlockSpec((T, gw), lambda b, h: (b, h)),
                   pl.BlockSpec((HG_GROUP_FWD * NC, HG_D, HG_D), lambda b, h: (b * ng + h, 0, 0)),
                   pl.BlockSpec((T, gw), lambda b, h: (b, h))],
        out_shape=[S((N, 512), BF16), S((N, 512), F32), S((B * HG_H * NC, HG_D, HG_D), F32), S((N, 512), BF16)],
        scratch_shapes=[pltpu.VMEM((HG_GROUP_FWD, HG_D, HG_D), F32)], compiler_params=_cp(("parallel", "parallel")),
    )(z, lb, gn)


def _hgrn_bwd(z, o_raw, states, a_mat, dy, lb, gn, B, T, swap_sibling=()):
    N = B * T
    NC = T // CHUNK
    ng = HG_H // HG_GROUP
    nsw = len(swap_sibling)

    def body(z_ref, o_ref, st_ref, a_ref, dy_ref, lb_ref, gn_ref, dz_ref, dlb_ref, dgn_ref, ds_scr, racc, dgn_acc):
        lbs = _lower_bound(lb_ref[...])
        gn_v = gn_ref[...]
        tri, triu = _tri(CHUNK), _tri(CHUNK, upper=True)
        cmask = _iota((CHUNK, CHUNK), 0) >= _iota((CHUNK, CHUNK), 1)
        for ref in (ds_scr, racc, dgn_acc, dlb_ref):
            ref[...] = jnp.zeros_like(ref)

        def chunk(ci, carry):
            c = NC - 1 - ci
            r = pl.ds(pl.multiple_of(c * CHUNK, CHUNK), CHUNK)
            for hh in range(HG_GROUP):
                zc, oc = 4 * LANE * hh, LANE * hh
                lb_v = lbs[:, oc:oc + LANE]
                ql, fl, il, gl = (z_ref[r, zc + LANE * j:zc + LANE * (j + 1)].astype(F32) for j in range(4))
                sig, f, k, logf = _hg_gates(fl, lb_v)
                G = _prefix_mm(tri, logf)
                qs, dsilu_q = _silu_and_grad(ql)
                gs, dsilu_g = _silu_and_grad(gl)
                o = o_ref[r, oc:oc + LANE]
                dyv = dy_ref[r, oc:oc + LANE]
                rstd = lax.rsqrt(jnp.mean(o * o, axis=-1, keepdims=True) + EPS)
                oh = o * rstd
                dgl = dyv * oh * gn_v * dsilu_g
                dn = dyv * gs
                dgn_acc[...] += _rowsum8(dn * oh)
                u = dn * gn_v
                do = rstd * (u - oh * jnp.mean(u * oh, axis=-1, keepdims=True))
                st = st_ref[hh * NC + c]
                dst = ds_scr[hh]
                eG = jnp.exp(G)
                g_last = G[CHUNK - 1:CHUNK, :]
                eL = jnp.exp(g_last - G)
                dA = jnp.where(cmask, _mm3(_nt, do, il), 0.0)
                dq_in, dk_in = _hg_intra_bwd(dA, qs, k, G)
                di = _tn(a_ref[r, oc:oc + LANE][:, 0:CHUNK], _b(do)) + _nt(_b(k * eL), _b(dst))
                dq = dq_in + _mm3(_nn, do, st) * eG
                dk = dk_in + _mm3(_nn, il, dst) * eL
                ds_scr[hh] = dst * jnp.exp(g_last) + _mm3(_tn, do, qs * eG)
                dd = qs * dq - k * dk
                dlogf = _prefix_mm(triu, dd) + racc[hh]
                racc[hh] += jnp.sum(dd, axis=0, keepdims=True)
                df = dlogf / f - dk
                dlb_ref[8 * hh:8 * (hh + 1), :] += _rowsum8(df * (1.0 - sig))
                dz_ref[r, zc:zc + LANE] = (dq * dsilu_q).astype(BF16)
                dz_ref[r, zc + LANE:zc + 2 * LANE] = (df * (1.0 - lb_v) * sig * (1.0 - sig)).astype(BF16)
                dz_ref[r, zc + 2 * LANE:zc + 3 * LANE] = di.astype(BF16)
                dz_ref[r, zc + 3 * LANE:zc + 4 * LANE] = dgl.astype(BF16)
            return carry

        lax.fori_loop(0, NC, chunk, 0, unroll=4)
        dgn_ref[...] = dgn_acc[...]

    gw = HG_GROUP * LANE
    cb = C_HG // (4 * gw)
    col = pl.BlockSpec((T, gw), lambda b, h: (b, h))
    if nsw:
        body = _hosting(body, 7, 3, 3, nsw, _sibling_swap_phases, (B, ng))
    return pl.pallas_call(
        body, name="hgrn_bwd", grid=(B, ng),
        in_specs=[pl.BlockSpec((T, 4 * gw), lambda b, h: (b, cb + h)), col,
                  pl.BlockSpec((HG_GROUP * NC, HG_D, HG_D), lambda b, h: (b * ng + h, 0, 0)), col, col,
                  pl.BlockSpec((lb.shape[0], gw), lambda b, h: (0, h)), pl.BlockSpec((1, LANE), lambda b, h: (0, 0))]
        + [ANY] * nsw,
        out_specs=[pl.BlockSpec((T, 4 * gw), lambda b, h: (b, h)),
                   pl.BlockSpec((8 * HG_GROUP, LANE), lambda b, h: (b * ng + h, 0)),
                   pl.BlockSpec((8, LANE), lambda b, h: (b * ng + h, 0))] + [ANY] * nsw,
        out_shape=[S((N, 2048), BF16), S((B * HG_H * 8, LANE), F32), S((B * ng * 8, LANE), F32)]
        + _sibling_swap_shapes(swap_sibling),
        scratch_shapes=[pltpu.VMEM((HG_GROUP, HG_D, HG_D), F32), pltpu.VMEM((HG_GROUP, 1, LANE), F32),
                        pltpu.VMEM((8, LANE), F32)] + (_sibling_swap_sems(nsw) if nsw else []),
        compiler_params=_cp(("arbitrary", "arbitrary") if nsw else ("parallel", "parallel")),
    )(z, o_raw, states, a_mat, dy, lb, gn, *swap_sibling)


def _pair_mean(x, lo_half):
    a = jnp.sum(jnp.where(lo_half, x, 0.0), axis=-1, keepdims=True)
    b = jnp.sum(jnp.where(lo_half, 0.0, x), axis=-1, keepdims=True)
    return jnp.where(lo_half, a, b) * (1.0 / FOX_D)


def _fox_gate_fwd(z, bias, B, T):
    N = B * T
    tb = LANE

    def body(z_ref, b_ref, fc_ref, fct_ref):
        tri = _tri(tb)

        def step(i, carry):
            r = pl.ds(pl.multiple_of(i * tb, tb), tb)
            cs = _prefix_mm(tri, jax.nn.log_sigmoid(z_ref[r, :].astype(F32) + b_ref[...])) + carry
            fc_ref[r, :] = cs
            fct_ref[0, :, r] = cs.T[0:8, :]
            return cs[tb - 1:tb, :]

        lax.fori_loop(0, T // tb, step, jnp.zeros((1, LANE), F32))

    return pl.pallas_call(
        body, name="fox_gate_fwd", grid=(B,),
        in_specs=[pl.BlockSpec((T, LANE), lambda b: (b, C_FF // LANE)), pl.BlockSpec((1, LANE), lambda b: (0, 0))],
        out_specs=[pl.BlockSpec((T, LANE), lambda b: (b, 0)), pl.BlockSpec((1, 8, T), lambda b: (b, 0, 0))],
        out_shape=[S((N, LANE), F32), S((B, 8, T), F32)], compiler_params=_cp(("parallel",)),
    )(z, bias)


def _fox_gate_bwd(dfc, z, bias, B, T):
    N = B * T
    tb = LANE
    nt = T // tb

    def body(d_ref, z_ref, b_ref, dz_ref, db_ref):
        triu = _tri(tb, upper=True)
        db_ref[...] = jnp.zeros_like(db_ref)

        def step(ii, carry):
            r = pl.ds(pl.multiple_of((nt - 1 - ii) * tb, tb), tb)
            d = d_ref[r, 0:LANE]
            for p in range(1, FOX_P):
                d = d + d_ref[r, LANE * p:LANE * (p + 1)]
            rc = _prefix_mm(triu, d) + carry
            dff = rc * jax.nn.sigmoid(-(z_ref[r, :].astype(F32) + b_ref[...]))
            dz_ref[r, :] = dff.astype(BF16)
            db_ref[...] += _rowsum8(dff)
            return carry + jnp.sum(d, axis=0, keepdims=True)

        lax.fori_loop(0, nt, step, jnp.zeros((1, LANE), F32))

    return pl.pallas_call(
        body, name="fox_gate_bwd", grid=(B,),
        in_specs=[pl.BlockSpec((T, 512), lambda b: (b, 0)), pl.BlockSpec((T, LANE), lambda b: (b, C_FF // LANE)),
                  pl.BlockSpec((1, LANE), lambda b: (0, 0))],
        out_specs=[pl.BlockSpec((T, LANE), lambda b: (b, 0)), pl.BlockSpec((8, LANE), lambda b: (b, 0))],
        out_shape=[S((N, LANE), BF16), S((B * 8, LANE), F32)], compiler_params=_cp(("parallel",)),
    )(dfc, z, bias)


def _fox_prep(z_ref, gq, gk, r, lo_half):
    q, k, v = (z_ref[r, LANE * j:LANE * (j + 1)].astype(F32) for j in range(3))
    rq = lax.rsqrt(_pair_mean(q * q, lo_half) + EPS)
    rk = lax.rsqrt(_pair_mean(k * k, lo_half) + EPS)
    qh, kh = q * rq, k * rk
    return qh * gq * (FOX_D ** -0.5), kh * gk, v, qh, kh, rq, rk


def _fox_fwd(z, fc, fct, gq, gk, B, T, tq=512, gather=()):
    N = B * T
    NQ = T // tq
    nga = len(gather)

    def body(z_ref, fc_ref, fct_ref, gq_ref, gk_ref, y_ref, lse_ref, qn_s, kn_s, v_s):
        p, qi = pl.program_id(1), pl.program_id(2)
        lo_half = _iota((1, LANE), 1) < FOX_D

        @pl.when(qi == 0)
        def _():
            def prep(i, carry):
                r = pl.ds(pl.multiple_of(i * tq, tq), tq)
                qn, kn, v = _fox_prep(z_ref, gq_ref[...], gk_ref[...], r, lo_half)[:3]
                qn_s[r, :], kn_s[r, :], v_s[r, :] = qn.astype(BF16), kn.astype(BF16), v.astype(BF16)
                return carry
            lax.fori_loop(0, NQ, prep, 0)

        rq = pl.ds(pl.multiple_of(qi * tq, tq), tq)
        qn = qn_s[rq, :]
        fcq = fc_ref[rq, :]
        lane = _iota((tq, LANE), 1)
        causal = _iota((tq, tq), 0) >= _iota((tq, tq), 1)
        qhs = [jnp.where(lo_half, qn, jnp.zeros_like(qn)), jnp.where(lo_half, jnp.zeros_like(qn), qn)]
        fqs = [jnp.sum(jnp.where(lane == 2 * p + hh, fcq, 0.0), axis=-1, keepdims=True) for hh in range(2)]

        def kv(j, carry, diagonal):
            rk = pl.ds(pl.multiple_of(j * tq, tq), tq)
            kj, vj = kn_s[rk, :], v_s[rk, :]
            one = jnp.ones_like(vj)
            new = []
            for hh in range(2):
                m, acc = carry[hh]
                s = _nt(qhs[hh], kj) + fqs[hh] - fct_ref[0, pl.ds(2 * p + hh, 1), rk]
                if diagonal:
                    s = jnp.where(causal, s, NEG)
                m_new = jnp.maximum(m, jnp.max(s, axis=-1, keepdims=True))
                pe = jnp.exp(s - m_new)
                v_aug = jnp.where(lo_half if hh == 0 else jnp.logical_not(lo_half), vj, one)
                new.append((m_new, jnp.exp(m - m_new) * acc + _nn(pe.astype(BF16), v_aug)))
            return tuple(new)

        init = tuple((jnp.full((tq, 1), NEG, F32), jnp.zeros((tq, LANE), F32)) for _ in range(2))
        carry = lax.fori_loop(0, qi, functools.partial(kv, diagonal=False), init)
        (m0, a0), (m1, a1) = kv(qi, carry, True)
        l0, l1 = a0[:, FOX_D:FOX_D + 1], a1[:, 0:1]
        y_ref[...] = jnp.where(lo_half, a0 / l0, a1 / l1).astype(BF16)
        lse_ref[...] = jnp.where(lo_half, m0 + jnp.log(l0), m1 + jnp.log(l1))

    vec = pl.BlockSpec((1, LANE), lambda b, p, q: (0, 0))
    tile = pl.BlockSpec((tq, LANE), lambda b, p, q: (b * NQ + q, p))
    if nga:
        body = _hosting(body, 5, 2, 3, nga, _gather_phases, (B, FOX_P, NQ))
    return pl.pallas_call(
        body, name="fox_fwd", grid=(B, FOX_P, NQ),
        in_specs=[pl.BlockSpec((T, 384), lambda b, p, q: (b, p)), pl.BlockSpec((T, LANE), lambda b, p, q: (b, 0)),
                  pl.BlockSpec((1, 8, T), lambda b, p, q: (b, 0, 0)), vec, vec] + [ANY] * nga,
        out_specs=[tile, tile] + [ANY] * nga, out_shape=[S((N, 512), BF16), S((N, 512), F32)] + _gather_shapes(gather),
        scratch_shapes=[pltpu.VMEM((T, LANE), BF16)] * 3 + (_gather_sems(nga) if nga else []),
        compiler_params=_cp(("arbitrary",) * 3 if nga else ("parallel", "parallel", "arbitrary")),
    )(z, fc, fct, gq, gk, *gather)


def _fox_bwd(z, dy, y, lse, fc, fct, gq, gk, B, T, tq=512, swap=()):
    N = B * T
    NQ = T // tq
    nsw = len(swap)

    def body(z_ref, dy_ref, y_ref, lse_ref, fc_ref, fct_ref, gq_ref, gk_ref, dz_ref, dfc_ref, dgq_ref, dgk_ref,
             qn_s, kn_s, v_s, do_s, delta_s, dq_s, dfk_s):
        p, kj = pl.program_id(1), pl.program_id(2)
        lo_half = _iota((1, LANE), 1) < FOX_D
        lane = _iota((tq, LANE), 1)
        gq_v, gk_v = gq_ref[...], gk_ref[...]

        @pl.when(kj == 0)
        def _():
            def prep(i, carry):
                r = pl.ds(pl.multiple_of(i * tq, tq), tq)
                qn, kn, v = _fox_prep(z_ref, gq_v, gk_v, r, lo_half)[:3]
                qn_s[r, :], kn_s[r, :], v_s[r, :] = qn.astype(BF16), kn.astype(BF16), v.astype(BF16)
                do = dy_ref[r, :]
                do_s[r, :] = do.astype(BF16)
                delta_s[r, :] = _pair_mean(do * y_ref[r, :].astype(F32), lo_half) * float(FOX_D)
                return carry
            lax.fori_loop(0, NQ, prep, 0)
            dq_s[...] = jnp.zeros_like(dq_s)
            dgq_ref[...] = jnp.zeros_like(dgq_ref)
            dgk_ref[...] = jnp.zeros_like(dgk_ref)

        rk = pl.ds(pl.multiple_of(kj * tq, tq), tq)
        kn, vv = kn_s[rk, :], v_s[rk, :]
        causal = _iota((tq, tq), 0) >= _iota((tq, tq), 1)
        zero, one = jnp.zeros_like(kn), jnp.ones_like(kn)
        hms = [lo_half, jnp.logical_not(lo_half)]
        kmasks = [jnp.where(hm, kn, zero) for hm in hms]
        kaugs = [jnp.where(hm, kn, one) for hm in hms]
        vmasks = [jnp.where(hm, vv, zero) for hm in hms]
        fks = [fct_ref[0, pl.ds(2 * p + hh, 1), rk] for hh in range(2)]

        def qloop(i, carry, diagonal):
            ri = pl.ds(pl.multiple_of(i * tq, tq), tq)
            qn = qn_s[ri, :]
            do = do_s[ri, :]
            fcq = fc_ref[ri, :]
            new = []
            for hh in range(2):
                dk_acc, dv_acc = carry[hh]
                c0 = FOX_D * hh
                fq = jnp.sum(jnp.where(lane == 2 * p + hh, fcq, 0.0), axis=-1, keepdims=True)
                pr = jnp.exp(_nt(qn, kmasks[hh]) + fq - fks[hh] - lse_ref[ri, c0:c0 + 1])
                if diagonal:
                    pr = jnp.where(causal, pr, 0.0)
                ds = (pr * (_nt(do, vmasks[hh]) - delta_s[ri, c0:c0 + 1])).astype(BF16)
                dq_s[hh, ri, :] += _nn(ds, kaugs[hh])
                new.append((dk_acc + _tn(jnp.where(hms[hh], qn, one), ds), dv_acc + _tn(do, pr.astype(BF16))))
            return tuple(new)

        init = tuple((jnp.zeros((LANE, tq), F32), jnp.zeros((LANE, tq), F32)) for _ in range(2))
        carry = qloop(kj, init, True)
        (dk0, dv0), (dk1, dv1) = lax.fori_loop(kj + 1, NQ, functools.partial(qloop, diagonal=False), carry)
        dks, dvs = [dk0.T, dk1.T], [dv0.T, dv1.T]

        dkn = jnp.where(lo_half, dks[0], dks[1])
        _, _, _, _, kh, _, rkk = _fox_prep(z_ref, gq_v, gk_v, rk, lo_half)
        u = dkn * gk_v
        dz_ref[rk, LANE:2 * LANE] = (rkk * (u - kh * _pair_mean(u * kh, lo_half))).astype(BF16)
        dz_ref[rk, 2 * LANE:3 * LANE] = jnp.where(lo_half, dvs[0], dvs[1]).astype(BF16)
        dgk_ref[...] += _rowsum8(dkn * kh)
        dfk_s[rk, :] = jnp.where(lane == 2 * p, -dks[0][:, FOX_D:FOX_D + 1],
                                 jnp.where(lane == 2 * p + 1, -dks[1][:, 0:1], 0.0))

        @pl.when(kj == NQ - 1)
        def _():
            def fin(i, carry):
                r = pl.ds(pl.multiple_of(i * tq, tq), tq)
                d0, d1 = dq_s[0, r, :], dq_s[1, r, :]
                dqn = jnp.where(lo_half, d0, d1)
                _, _, _, qh, _, rqq, _ = _fox_prep(z_ref, gq_v, gk_v, r, lo_half)
                u = dqn * gq_v * (FOX_D ** -0.5)
                dz_ref[r, 0:LANE] = (rqq * (u - qh * _pair_mean(u * qh, lo_half))).astype(BF16)
                dgq_ref[...] += _rowsum8(dqn * qh) * (FOX_D ** -0.5)
                dfc_ref[r, :] = dfk_s[r, :] + jnp.where(lane == 2 * p, d0[:, FOX_D:FOX_D + 1],
                                                        jnp.where(lane == 2 * p + 1, d1[:, 0:1], 0.0))
                return carry
            lax.fori_loop(0, NQ, fin, 0)

    vec = pl.BlockSpec((1, LANE), lambda b, p, k: (0, 0))
    col = pl.BlockSpec((T, LANE), lambda b, p, k: (b, p))
    part = pl.BlockSpec((8, LANE), lambda b, p, k: (b * FOX_P + p, 0))
    if nsw:
        body = _hosting(body, 8, 4, 7, nsw, _chip_swap_phases, (B, FOX_P, NQ))
    return pl.pallas_call(
        body, name="fox_bwd", grid=(B, FOX_P, NQ),
        in_specs=[pl.BlockSpec((T, 384), lambda b, p, k: (b, p)), col, col, col,
                  pl.BlockSpec((T, LANE), lambda b, p, k: (b, 0)), pl.BlockSpec((1, 8, T), lambda b, p, k: (b, 0, 0)),
                  vec, vec] + [ANY] * nsw,
        out_specs=[pl.BlockSpec((T, 384), lambda b, p, k: (b, p)), col, part, part] + [ANY] * nsw,
        out_shape=[S((N, 1536), BF16), S((N, 512), F32), S((B * FOX_P * 8, LANE), F32), S((B * FOX_P * 8, LANE), F32)]
        + [S(p.shape, p.dtype) for p in swap],
        scratch_shapes=[pltpu.VMEM((T, LANE), BF16)] * 4 + [pltpu.VMEM((T, LANE), F32), pltpu.VMEM((2, T, LANE), F32),
                                                            pltpu.VMEM((T, LANE), F32)]
        + (_chip_swap_sems(nsw) if nsw else []),
        compiler_params=_cp(("arbitrary",) * 3 if nsw else ("parallel", "parallel", "arbitrary")),
    )(z, dy, y, lse, fc, fct, gq, gk, *swap)


def _mem_scores(z_ref, kv_ref, gq, gk, h):
    c = slice(MEM_D * h, MEM_D * (h + 1))
    q, k = z_ref[:, c].astype(F32), kv_ref[:, c]
    rq = lax.rsqrt(jnp.mean(q * q, axis=-1, keepdims=True) + EPS)
    rk = lax.rsqrt(jnp.mean(k * k, axis=-1, keepdims=True) + EPS)
    qh, kh = q * rq, k * rk
    qn = (qh * gq * (MEM_D ** -0.5)).astype(BF16)
    kn = (kh * gk).astype(BF16)
    s = _nt(qn, kn)
    pe = jnp.exp(s - jnp.max(s, axis=-1, keepdims=True))
    pn = pe / jnp.sum(pe, axis=-1, keepdims=True)
    return pn, qn, kn, qh, kh, rq, rk


def _mem_fwd(z, memkv, gq, gk, B, T, M, tq=1024):
    N = B * T
    tq = min(tq, T)
    NQ = T // tq
    W = MEM_H * MEM_D

    def body(z_ref, kv_ref, gq_ref, gk_ref, y_ref):
        for h in range(MEM_H):
            pn = _mem_scores(z_ref, kv_ref, gq_ref[...], gk_ref[...], h)[0]
            v = kv_ref[:, W + MEM_D * h:W + MEM_D * (h + 1)].astype(BF16)
            y_ref[:, MEM_D * h:MEM_D * (h + 1)] = _nn(pn.astype(BF16), v).astype(BF16)

    vec = pl.BlockSpec((1, LANE), lambda b, q: (0, 0))
    return pl.pallas_call(
        body, name="mem_fwd", grid=(B, NQ),
        in_specs=[pl.BlockSpec((tq, W), lambda b, q: (b * NQ + q, C_MQ // W)),
                  pl.BlockSpec((M, 2 * W), lambda b, q: (b, 0)), vec, vec],
        out_specs=pl.BlockSpec((tq, W), lambda b, q: (b * NQ + q, 0)), out_shape=S((N, W), BF16),
        compiler_params=_cp(("parallel", "parallel")),
    )(z, memkv, gq, gk)


def _mem_bwd(z, memkv, dy, gq, gk, B, T, M, tq=1024):
    N = B * T
    tq = min(tq, T)
    NQ = T // tq
    W = MEM_H * MEM_D

    def body(z_ref, kv_ref, dy_ref, gq_ref, gk_ref, dz_ref, dkv_ref, dgq_ref, dgk_ref, acc):
        qi = pl.program_id(1)
        gq_v, gk_v = gq_ref[...], gk_ref[...]

        @pl.when(qi == 0)
        def _():
            acc[...] = jnp.zeros_like(acc)
            dgq_ref[...] = jnp.zeros_like(dgq_ref)
            dgk_ref[...] = jnp.zeros_like(dgk_ref)

        for h in range(MEM_H):
            c = slice(MEM_D * h, MEM_D * (h + 1))
            cv = slice(W + MEM_D * h, W + MEM_D * (h + 1))
            pn, qn, kn, qh, _, rq, _ = _mem_scores(z_ref, kv_ref, gq_v, gk_v, h)
            do = dy_ref[:, c].astype(BF16)
            dp = _nt(do, kv_ref[:, cv].astype(BF16))
            ds = (pn * (dp - jnp.sum(dp * pn, axis=-1, keepdims=True))).astype(BF16)
            dqn = _nn(ds, kn)
            acc[:, c] += _tn(ds, qn)
            acc[:, cv] += _tn(pn.astype(BF16), do)
            u = dqn * gq_v * (MEM_D ** -0.5)
            dz_ref[:, c] = (rq * (u - qh * jnp.mean(u * qh, axis=-1, keepdims=True))).astype(BF16)
            dgq_ref[...] += _rowsum8(dqn * qh) * (MEM_D ** -0.5)

        @pl.when(qi == NQ - 1)
        def _():
            for h in range(MEM_H):
                c = slice(MEM_D * h, MEM_D * (h + 1))
                cv = slice(W + MEM_D * h, W + MEM_D * (h + 1))
                k = kv_ref[:, c]
                rk = lax.rsqrt(jnp.mean(k * k, axis=-1, keepdims=True) + EPS)
                kh = k * rk
                dkn = acc[:, c]
                u = dkn * gk_v
                dkv_ref[:, c] = (rk * (u - kh * jnp.mean(u * kh, axis=-1, keepdims=True))).astype(BF16)
                dkv_ref[:, cv] = acc[:, cv].astype(BF16)
                dgk_ref[...] += _rowsum8(dkn * kh)

    vec = pl.BlockSpec((1, LANE), lambda b, q: (0, 0))
    part = pl.BlockSpec((8, LANE), lambda b, q: (b, 0))
    return pl.pallas_call(
        body, name="mem_bwd", grid=(B, NQ),
        in_specs=[pl.BlockSpec((tq, W), lambda b, q: (b * NQ + q, C_MQ // W)),
                  pl.BlockSpec((M, 2 * W), lambda b, q: (b, 0)), pl.BlockSpec((tq, W), lambda b, q: (b * NQ + q, 0)),
                  vec, vec],
        out_specs=[pl.BlockSpec((tq, W), lambda b, q: (b * NQ + q, 0)), pl.BlockSpec((M, 2 * W), lambda b, q: (b, 0)),
                   part, part],
        out_shape=[S((N, W), BF16), S((B * M, 2 * W), BF16), S((B * 8, LANE), F32), S((B * 8, LANE), F32)],
        scratch_shapes=[pltpu.VMEM((M, 2 * W), F32)], compiler_params=_cp(("parallel", "arbitrary")),
    )(z, memkv, dy, gq, gk)


def _merge_fwd(ya, yb, yc, z, x, wa, wb, wc, wo, g_next, tm=512):
    n, d = x.shape
    wdt = ya.shape[1]
    gb = C_GATE // d

    def body(ya_ref, yb_ref, yc_ref, g0_ref, g1_ref, g2_ref, x_ref, wa_ref, wb_ref, wc_ref, wo_ref, gn_ref,
             x1_ref, mg_ref, ua_ref, ub_ref, uc_ref, h_ref):
        merged = jnp.zeros((tm, d), F32)
        for y_ref, g_ref, w_ref, u_ref in ((ya_ref, g0_ref, wa_ref, ua_ref), (yb_ref, g1_ref, wb_ref, ub_ref),
                                           (yc_ref, g2_ref, wc_ref, uc_ref)):
            u = _nn(y_ref[...], w_ref[...])
            u_ref[...] = u.astype(BF16)
            merged = merged + jax.nn.sigmoid(g_ref[...].astype(F32)) * u
        mb = merged.astype(BF16)
        mg_ref[...] = mb
        x1 = x_ref[...] + _nn(mb, wo_ref[...])
        x1_ref[...] = x1
        h_ref[...] = (x1 * lax.rsqrt(jnp.mean(x1 * x1, axis=-1, keepdims=True) + EPS) * gn_ref[...]).astype(BF16)

    yt = pl.BlockSpec((tm, wdt), lambda i: (i, 0))
    xt = pl.BlockSpec((tm, d), lambda i: (i, 0))
    wbr = pl.BlockSpec((wdt, d), lambda i: (0, 0))
    gates = [pl.BlockSpec((tm, d), functools.partial(lambda i, k: (i, gb + k), k=k)) for k in range(3)]
    return pl.pallas_call(
        body, name="merge_fwd", grid=(n // tm,),
        in_specs=[yt, yt, yt] + gates + [xt, wbr, wbr, wbr, pl.BlockSpec((d, d), lambda i: (0, 0)),
                                         pl.BlockSpec((1, d), lambda i: (0, 0))],
        out_specs=[xt] * 6, out_shape=[S((n, d), F32)] + [S((n, d), BF16)] * 5, compiler_params=_cp(("parallel",)),
    )(ya, yb, yc, z, z, z, x, wa, wb, wc, wo, g_next)


def _merge_bwd(dx1, z, ua, ub, uc, wa, wb, wc, wo, tm=512):
    n, d = dx1.shape
    wdt = wa.shape[0]
    gb = C_GATE // d

    def body(dx_ref, g0_ref, g1_ref, g2_ref, ua_ref, ub_ref, uc_ref, wa_ref, wb_ref, wc_ref, wo_ref,
             dg_ref, dya_ref, dyb_ref, dyc_ref, dua_ref, dub_ref, duc_ref):
        dm = _nt(dx_ref[...].astype(BF16), wo_ref[...])
        for k, (g_ref, u_ref, w_ref, dy_ref, du_ref) in enumerate((
                (g0_ref, ua_ref, wa_ref, dya_ref, dua_ref), (g1_ref, ub_ref, wb_ref, dyb_ref, dub_ref),
                (g2_ref, uc_ref, wc_ref, dyc_ref, duc_ref))):
            g = jax.nn.sigmoid(g_ref[...].astype(F32))
            du = (dm * g).astype(BF16)
            du_ref[...] = du
            dg_ref[:, d * k:d * (k + 1)] = (dm * u_ref[...].astype(F32) * g * (1.0 - g)).astype(BF16)
            dy_ref[...] = _nt(du, w_ref[...])

    yt = pl.BlockSpec((tm, wdt), lambda i: (i, 0))
    xt = pl.BlockSpec((tm, d), lambda i: (i, 0))
    wbr = pl.BlockSpec((wdt, d), lambda i: (0, 0))
    gates = [pl.BlockSpec((tm, d), functools.partial(lambda i, k: (i, gb + k), k=k)) for k in range(3)]
    return pl.pallas_call(
        body, name="merge_bwd", grid=(n // tm,),
        in_specs=[xt] + gates + [xt, xt, xt, wbr, wbr, wbr, pl.BlockSpec((d, d), lambda i: (0, 0))],
        out_specs=[pl.BlockSpec((tm, 3 * d), lambda i: (i, 0)), yt, yt, yt, xt, xt, xt],
        out_shape=[S((n, 3 * d), BF16)] + [S((n, wdt), F32)] * 3 + [S((n, d), BF16)] * 3,
        compiler_params=_cp(("parallel",)),
    )(dx1, z, z, z, ua, ub, uc, wa, wb, wc, wo)


FFN_TN = 1408
TN_TM = 2048
INV_SQRT2 = 0.7071067811865476
INV_SQRT_2PI = 0.3989422804014327


def _conv_shifted(a, prev, first, tm):
    row = _iota(a.shape, 0)
    p7 = jnp.where(first, 0.0, prev[7:8, :])
    p6 = jnp.where(first, 0.0, prev[6:7, :])
    a1 = jnp.where(row == 0, p7, pltpu.roll(a, 1, 0))
    a2 = jnp.where(row == 0, p6, jnp.where(row == 1, p7, pltpu.roll(a, 2, 0)))
    return a1, a2


def _ffn_act_fwd(up, cw, cb, B, T, tm=1024):
    N = B * T
    tm = min(tm, T)
    dff = cw.shape[1]
    NT, NJ, tn = T // tm, dff // FFN_TN, FFN_TN

    def body(a_ref, v_ref, cw_ref, cb_ref, y_ref, c_ref, carry):
        t = pl.program_id(2)
        a = a_ref[...].astype(F32)
        a1, a2 = _conv_shifted(a, carry[...], t == 0, tm)
        w = cw_ref[...]
        ac = w[0:1, :] * a2 + w[1:2, :] * a1 + w[2:3, :] * a + cb_ref[...]
        cdf = 0.5 * (1.0 + lax.erf(ac * INV_SQRT2))
        y_ref[...] = (ac * cdf * v_ref[...].astype(F32)).astype(BF16)
        c_ref[...] = cdf.astype(BF16)
        carry[...] = a[tm - 8:tm, :]

    return pl.pallas_call(
        body, name="ffn_act_fwd", grid=(B, NJ, NT),
        in_specs=[pl.BlockSpec((tm, tn), lambda b, j, t: (b * NT + t, j)),
                  pl.BlockSpec((tm, tn), lambda b, j, t: (b * NT + t, NJ + j)),
                  pl.BlockSpec((3, tn), lambda b, j, t: (0, j)), pl.BlockSpec((1, tn), lambda b, j, t: (0, j))],
        out_specs=[pl.BlockSpec((tm, tn), lambda b, j, t: (b * NT + t, j))] * 2, out_shape=[S((N, dff), BF16)] * 2,
        scratch_shapes=[pltpu.VMEM((8, tn), F32)], compiler_params=_cp(("parallel", "parallel", "arbitrary")),
    )(up, up, cw, cb)


def _ffn_down_loss(y, wd, x1, tgt, tm=512):
    n, d = x1.shape
    kf = y.shape[1]

    def body(y_ref, w_ref, x_ref, t_ref, dx_ref, ls_ref):
        err = x_ref[...] + _nn(y_ref[...], w_ref[...]) - t_ref[...]
        dx_ref[...] = err * (1.0 / d)

        @pl.when(pl.program_id(0) == 0)
        def _():
            ls_ref[...] = jnp.zeros_like(ls_ref)

        ls_ref[...] += _rowsum8(err * err) * (0.5 / d)

    xt = pl.BlockSpec((tm, d), lambda i: (i, 0))
    return pl.pallas_call(
        body, name="ffn_down_loss", grid=(n // tm,),
        in_specs=[pl.BlockSpec((tm, kf), lambda i: (i, 0)), pl.BlockSpec((kf, d), lambda i: (0, 0)), xt, xt],
        out_specs=[xt, pl.BlockSpec((8, d), lambda i: (0, 0))], out_shape=[S((n, d), F32), S((8, d), F32)],
        compiler_params=_cp(("arbitrary",)),
    )(y, wd, x1, tgt)


def _ffn_act_bwd1(dx2, wd, up, cdf, cw, cb, B, T, tm=512):
    N = B * T
    tm = min(tm, T)
    d = dx2.shape[1]
    dff = cw.shape[1]
    NT, NJ, tn = T // tm, dff // FFN_TN, FFN_TN

    def body(dx_ref, w_ref, a_ref, v_ref, c_ref, cw_ref, cb_ref, dac_ref, dv_ref, dcw_ref, dcb_ref, carry):
        b, t = pl.program_id(1), pl.program_id(2)
        a = a_ref[...].astype(F32)
        a1, a2 = _conv_shifted(a, carry[...], t == 0, tm)
        carry[...] = a[tm - 8:tm, :]
        w = cw_ref[...]
        ac = w[0:1, :] * a2 + w[1:2, :] * a1 + w[2:3, :] * a + cb_ref[...]
        dy = _nt(dx_ref[...].astype(BF16), w_ref[...])
        cdf = c_ref[...].astype(F32)
        dv_ref[...] = (dy * ac * cdf).astype(BF16)
        dac = dy * v_ref[...].astype(F32) * (cdf + ac * jnp.exp(-0.5 * ac * ac) * INV_SQRT_2PI)
        dac_ref[...] = dac

        @pl.when((b == 0) & (t == 0))
        def _():
            dcw_ref[...] = jnp.zeros_like(dcw_ref)
            dcb_ref[...] = jnp.zeros_like(dcb_ref)

        dcw_ref[0:8, :] += _rowsum8(dac * a2)
        dcw_ref[8:16, :] += _rowsum8(dac * a1)
        dcw_ref[16:24, :] += _rowsum8(dac * a)
        dcb_ref[...] += _rowsum8(dac)

    return pl.pallas_call(
        body, name="ffn_act_bwd1", grid=(NJ, B, NT),
        in_specs=[pl.BlockSpec((tm, d), lambda j, b, t: (b * NT + t, 0)), pl.BlockSpec((tn, d), lambda j, b, t: (j, 0)),
                  pl.BlockSpec((tm, tn), lambda j, b, t: (b * NT + t, j)),
                  pl.BlockSpec((tm, tn), lambda j, b, t: (b * NT + t, NJ + j)),
                  pl.BlockSpec((tm, tn), lambda j, b, t: (b * NT + t, j)),
                  pl.BlockSpec((3, tn), lambda j, b, t: (0, j)), pl.BlockSpec((1, tn), lambda j, b, t: (0, j))],
        out_specs=[pl.BlockSpec((tm, tn), lambda j, b, t: (b * NT + t, j)),
                   pl.BlockSpec((tm, tn), lambda j, b, t: (b * NT + t, j)),
                   pl.BlockSpec((24, tn), lambda j, b, t: (0, j)), pl.BlockSpec((8, tn), lambda j, b, t: (0, j))],
        out_shape=[S((N, dff), F32), S((N, dff), BF16), S((24, dff), F32), S((8, dff), F32)],
        scratch_shapes=[pltpu.VMEM((8, tn), F32)], compiler_params=_cp(("parallel", "arbitrary", "arbitrary")),
    )(dx2, wd, up, up, cdf, cw, cb)


def _ffn_act_bwd2(dac, cw, B, T, tm=1024):
    N = B * T
    tm = min(tm, T)
    dff = cw.shape[1]
    NT, NJ, tn = T // tm, dff // FFN_TN, FFN_TN
    last8 = N // 8 - 1

    def body(d_ref, nx_ref, cw_ref, da_ref):
        t = pl.program_id(2)
        dd = d_ref[...]
        row = _iota(dd.shape, 0)
        last = t == NT - 1
        n0 = jnp.where(last, 0.0, nx_ref[0:1, :])
        n1 = jnp.where(last, 0.0, nx_ref[1:2, :])
        d1 = jnp.where(row == tm - 1, n0, pltpu.roll(dd, tm - 1, 0))
        d2 = jnp.where(row == tm - 1, n1, jnp.where(row == tm - 2, n0, pltpu.roll(dd, tm - 2, 0)))
        w = cw_ref[...]
        da_ref[...] = (w[2:3, :] * dd + w[1:2, :] * d1 + w[0:1, :] * d2).astype(BF16)

    return pl.pallas_call(
        body, name="ffn_act_bwd2", grid=(B, NJ, NT),
        in_specs=[pl.BlockSpec((tm, tn), lambda b, j, t: (b * NT + t, j)),
                  pl.BlockSpec((8, tn), lambda b, j, t: (jnp.minimum((b * NT + t + 1) * (tm // 8), last8), j)),
                  pl.BlockSpec((3, tn), lambda b, j, t: (0, j))],
        out_specs=pl.BlockSpec((tm, tn), lambda b, j, t: (b * NT + t, j)), out_shape=S((N, dff), BF16),
        compiler_params=_cp(("parallel", "parallel", "parallel")),
    )(dac, dac, cw)


def _fold_rows(p, name):
    r, c = p.shape[0] // 8, p.shape[1]

    def body(p_ref, o_ref):
        for j in range(r):
            o_ref[j:j + 1, :] = jnp.sum(p_ref[8 * j:8 * (j + 1), :], axis=0, keepdims=True)

    return pl.pallas_call(body, name=name, out_shape=S((r, c), F32), compiler_params=_cp())(p)


def _small_reduce(lbl, dg_mix, dg_mem, dlb_p, dgn_p, dfb_p, dgq_p, dgk_p, dmq_p, dmk_p, dg_ffn, dcb_p, loss_p):
    d, dff = dg_mix.shape[1], dcb_p.shape[1]
    nbh = dlb_p.shape[0] // (8 * HG_H)

    def colsum(ref):
        return jnp.sum(ref[...], axis=0, keepdims=True)

    def body(lbl_ref, mix_ref, mem_ref, dlb_ref, dgn_ref, dfb_ref, dgq_ref, dgk_ref, dmq_ref, dmk_ref, ffn_ref, dcb_ref,
             ls_ref, o_mix, o_mem, o_lb, o_hgn, o_fb, o_fq, o_fk, o_mq, o_mk, o_ffn, o_cb, o_loss):
        o_mix[...], o_mem[...], o_ffn[...], o_cb[...] = colsum(mix_ref), colsum(mem_ref), colsum(ffn_ref), colsum(dcb_ref)
        o_hgn[...], o_fb[...], o_mq[...], o_mk[...] = colsum(dgn_ref), colsum(dfb_ref), colsum(dmq_ref), colsum(dmk_ref)
        for src, dst in ((dgq_ref, o_fq), (dgk_ref, o_fk)):
            v = colsum(src)
            dst[...] = v + pltpu.roll(v, FOX_D, 1)
        o_loss[...] = jnp.zeros((1, LANE), F32) + jnp.sum(colsum(ls_ref), axis=-1, keepdims=True)
        logits = lbl_ref[...]
        e = jnp.exp(logits - jnp.max(logits, axis=0, keepdims=True))
        pr = e / jnp.sum(e, axis=0, keepdims=True)
        rows = _iota((8, LANE), 0)
        for h in range(HG_H):
            acc = jnp.zeros((8, LANE), F32)
            for b in range(nbh):
                acc = acc + dlb_ref[8 * (b * HG_H + h):8 * (b * HG_H + h + 1), :]
            dlb = jnp.sum(acc, axis=0, keepdims=True)
            c = slice(LANE * h, LANE * (h + 1))
            p0 = pr[0:1, c]
            first = _iota((logits.shape[0], LANE), 0) == 0
            o_lb[:, c] = pr[:, c] * (jnp.where(first, 1.0, 0.0) - p0) * dlb

    outs = [S((1, d), F32), S((1, d), F32), S(lbl.shape, F32)] + [S((1, LANE), F32)] * 6 + \
           [S((1, d), F32), S((1, dff), F32), S((1, LANE), F32)]
    return pl.pallas_call(body, name="small_reduce", out_shape=outs, compiler_params=_cp())(
        lbl, dg_mix, dg_mem, dlb_p, dgn_p, dfb_p, dgq_p, dgk_p, dmq_p, dmk_p, dg_ffn, dcb_p, loss_p)


def _in_col_pieces():
    hw, fw = HG_H * HG_D, FOX_H * FOX_D
    fox0, ff0 = 4 * hw, 4 * hw + 3 * fw
    mq0 = ff0 + FOX_H
    gate0 = mq0 + MEM_H * MEM_D
    pieces = []
    for p in range(FOX_P):
        pieces += [(fox0 + j * fw + LANE * p, LANE) for j in range(3)]
    pieces.append((mq0, MEM_H * MEM_D))
    for h in range(HG_H):
        pieces += [(j * hw + HG_D * h, HG_D) for j in range(4)]
    pieces.append((gate0, C_FF - C_GATE))
    pieces.append((ff0, FOX_H))
    return pieces


def _perm_from_blocks(blocks):
    n_blk, _, c = blocks.shape
    parts = []
    for s, n in _in_col_pieces():
        lo = s
        while lo < s + n:
            d = lo // c
            hi = min(s + n, (d + 1) * c)
            parts.append(blocks[d][:, lo - d * c:hi - d * c])
            lo = hi
    parts.append(jnp.zeros((blocks.shape[1], C_END - C_FF - FOX_H), blocks.dtype))
    return jnp.concatenate(parts, axis=1)


def _unperm_blocks(segs, n_blk):
    starts = [0]
    for a in segs:
        starts.append(starts[-1] + a.shape[1])
    new_start, placed = 0, []
    for s, n in _in_col_pieces():
        placed.append((s, new_start, n))
        new_start += n
    placed.sort()
    c = sum(n for _, _, n in placed) // n_blk
    blocks = []
    for d in range(n_blk):
        parts = []
        for s, ns, n in placed:
            lo, hi = max(s, d * c), min(s + n, (d + 1) * c)
            if lo < hi:
                i = max(j for j in range(len(segs)) if starts[j] <= ns)
                parts.append(segs[i][:, ns + lo - s - starts[i]:ns + hi - s - starts[i]])
        blocks.append(jnp.concatenate(parts, axis=1))
    return jnp.stack(blocks)


def _local_step(x2, mem2, tgt, sm, W, B, T, M, ex=None):
    fbias = jnp.pad(sm["fox_f_bias"], ((0, 0), (0, LANE - FOX_H)))
    gq2 = jnp.concatenate([sm["fox_q_norm_g"]] * 2, axis=1)
    gk2 = jnp.concatenate([sm["fox_k_norm_g"]] * 2, axis=1)
    lbl = sm["hgrn_lb_logits"]
    if ex:
        h, *first = _rmsnorm_cast(x2, sm["norm_mix_g"], "norm_mix", gather=ex.first_blocks())
        W = ex.unpack_first(first)
    else:
        h = _rmsnorm_cast(x2, sm["norm_mix_g"], "norm_mix")
    z = _mm_nn(h, W["w_in"], BF16, "proj_in", 512, 2432)
    memn = _rmsnorm_cast(mem2, sm["norm_mem_g"], "norm_mem", tm=256)
    memkv = _mm_nn(memn, W["mem_kv_w"], F32, "proj_memkv", 256, 512)
    ya, o_raw, states, a_mat = _hgrn_fwd(z, lbl, sm["hgrn_norm_g"], B, T)
    fc, fct = _fox_gate_fwd(z, fbias, B, T)
    yb, lse, *late = _fox_fwd(z, fc, fct, gq2, gk2, B, T, gather=ex.late_blocks() if ex else ())
    if ex:
        W = {**W, **ex.unpack_late(late)}
    yc = _mem_fwd(z, memkv, sm["mem_q_norm_g"], sm["mem_k_norm_g"], B, T, M)
    x1, merged, ua, ub, uc, h2 = _merge_fwd(ya, yb, yc, z, x2, W["w_br_hgrn"], W["w_br_fox"], W["w_br_mem"], W["w_out"],
                                            sm["norm_ffn_g"])
    up = _mm_nn(h2, W["ffn_w_up"], BF16, "ffn_up", 512, FFN_TN)
    yf, cdf = _ffn_act_fwd(up, W["ffn_conv_w"], sm["ffn_conv_b"], B, T)
    dx2, loss_p = _ffn_down_loss(yf, W["ffn_w_down"], x1, tgt)
    dff = W["ffn_conv_w"].shape[1]
    dac, dv, dcw_p, dcb_p = _ffn_act_bwd1(dx2, W["ffn_w_down"], up, cdf, W["ffn_conv_w"], sm["ffn_conv_b"], B, T)
    da = _ffn_act_bwd2(dac, W["ffn_conv_w"], B, T)
    g = {"ffn_conv_w": _fold_rows(dcw_p, "g_conv_w")}
    g["ffn_w_down"] = _mm_tn(yf, dx2, "g_w_down", TN_TM, 512)
    dh2 = _mm_nt_sum([(da, 0, dff, 0), (dv, 0, dff, dff)], W["ffn_w_up"], "dh2", 512)
    g["ffn_w_up"] = [_mm_tn(h2, da, "g_w_up_a", TN_TM, FFN_TN), _mm_tn(h2, dv, "g_w_up_v", TN_TM, FFN_TN)]
    dx1, dg_ffn = _rmsnorm_bwd(dh2, x1, sm["norm_ffn_g"], dx2, "norm_ffn_bwd")
    g["w_out"] = _mm_tn(merged, dx1, "g_w_out", TN_TM, 512)
    dgate, dya, dyb, dyc, dua, dub, duc = _merge_bwd(dx1, z, ua, ub, uc, W["w_br_hgrn"], W["w_br_fox"], W["w_br_mem"],
                                                    W["w_out"])
    g["w_br_hgrn"] = _mm_tn(ya, dua, "g_w_br_hgrn", TN_TM, 512)
    g["w_br_fox"] = _mm_tn(yb, dub, "g_w_br_fox", TN_TM, 512)
    g["w_br_mem"] = _mm_tn(yc, duc, "g_w_br_mem", TN_TM, 512)
    early_pk = ex.early_grads(g) if ex else ()
    dz_hg, dlb_p, dgn_p, *early_sib = _hgrn_bwd(z, o_raw, states, a_mat, dya, lbl, sm["hgrn_norm_g"], B, T,
                                                swap_sibling=early_pk)
    dz_fox, dfc, dgq_p, dgk_p, *early_chips = _fox_bwd(z, dyb, yb, lse, fc, fct, gq2, gk2, B, T,
                                                       swap=ex.pair_sums(early_pk, early_sib, "early") if ex else ())
    dz_ff, dfb_p = _fox_gate_bwd(dfc, z, fbias, B, T)
    dz_mq, dkv, dmq_p, dmk_p = _mem_bwd(z, memkv, dyc, sm["mem_q_norm_g"], sm["mem_k_norm_g"], B, T, M)
    g["mem_kv_w"] = _mm_tn(memn, dkv, "g_mem_kv_w", 256, 512)
    dmemn = _mm_nt_sum([(dkv, 0, dkv.shape[1], 0)], W["mem_kv_w"], "d_memn", 256)
    _, dg_mem = _rmsnorm_bwd(dmemn, mem2, sm["norm_mem_g"], None, "norm_mem_bwd", tm=256)
    d = x2.shape[1]
    parts = [(dz_fox, 0, C_MQ - C_FOX, C_FOX), (dz_mq, 0, C_HG - C_MQ, C_MQ), (dz_hg, 0, C_GATE - C_HG, C_HG)]
    parts += [(dgate, d * k, d, C_GATE + d * k) for k in range(3)] + [(dz_ff, 0, C_END - C_FF, C_FF)]
    g["w_in"] = [_mm_tn(h, dzs, "g_w_in_%d" % i, 2 * TN_TM, min(512, dzs.shape[1]))
                 for i, dzs in enumerate((dz_fox, dz_mq, dz_hg, dgate, dz_ff))]
    sums = None
    if ex:
        last_pk = ex.last_grads(g)
        last_sib = _swap_with_sibling(last_pk, "rs_sibling_last")
        dh, last_chips = _mm_nt_sum(parts, W["w_in"], "dh", 512, swap=ex.pair_sums(last_pk, last_sib, "last"))
        sums = (ex.final_sums(early_pk, early_sib, early_chips, "early"),
                ex.final_sums(last_pk, last_sib, last_chips, "last"))
    else:
        dh = _mm_nt_sum(parts, W["w_in"], "dh", 512)
    grad_x, dg_mix = _rmsnorm_bwd(dh, x2, sm["norm_mix_g"], dx1, "norm_mix_bwd")
    small = _small_reduce(lbl, dg_mix, dg_mem, dlb_p, dgn_p, dfb_p, dgq_p, dgk_p, dmq_p, dmk_p, dg_ffn, dcb_p, loss_p)
    names = ("norm_mix_g", "norm_mem_g", "hgrn_lb_logits", "hgrn_norm_g", "fox_f_bias", "fox_q_norm_g", "fox_k_norm_g",
             "mem_q_norm_g", "mem_k_norm_g", "norm_ffn_g", "ffn_conv_b", "loss")
    g.update(dict(zip(names, small)))
    return grad_x, g, sums


ANY = pl.BlockSpec(memory_space=pl.ANY)


def _position():
    return lax.axis_index("x"), lax.axis_index("y"), lax.axis_index("c")


def _all_gather(blocks, name):
    nb = len(blocks)

    def body(*refs):
        start, forward, finish = _gather_phases(refs[:nb], refs[nb:2 * nb], *refs[2 * nb:])
        start()
        forward()
        finish()

    return pl.pallas_call(
        body, name=name, out_shape=_gather_shapes(blocks), in_specs=[ANY] * nb, out_specs=[ANY] * nb,
        scratch_shapes=_gather_sems(nb),
    )(*blocks)


def _hosting(body, n_in, n_out, n_scratch, n_x, make_phases, grid):
    n_steps = math.prod(grid)

    def hosted(*refs):
        a = n_in + n_x
        b = a + n_out + n_x
        ins, xs = refs[:n_in], refs[n_in:a]
        outs, x_outs = refs[a:a + n_out], refs[a + n_out:b]
        scratch, sems = refs[b:b + n_scratch], refs[b + n_scratch:]
        step = 0
        for ax, n in enumerate(grid):
            step = step * n + pl.program_id(ax)
        phases = make_phases(xs, x_outs, *sems)
        pl.when(step == 0)(phases[0])
        for ph in phases[1:-1]:
            pl.when(step == n_steps // 2)(ph)
        body(*ins, *outs, *scratch)
        pl.when(step == n_steps - 1)(phases[-1])

    return hosted


def _gather_shapes(blocks):
    return [S((N_DEV,) + b.shape, b.dtype) for b in blocks]


def _gather_sems(nb):
    return [pltpu.SemaphoreType.DMA((7 * nb,)), pltpu.SemaphoreType.DMA((7 * nb,)), pltpu.SemaphoreType.DMA((nb,))]


def _gather_phases(x_refs, out_refs, send_sems, recv_sems, local_sems):
    nb = len(x_refs)
    x, y, c = _position()
    me, sibling = (x, y, c), (x, y, 1 - c)
    chips = [(1 - x, y), (x, 1 - y), (1 - x, 1 - y)]

    def copy(i, k, blk, to, own=False):
        px, py, pc = blk
        slot = out_refs[i].at[4 * px + 2 * py + pc]
        return pltpu.make_async_remote_copy(
            src_ref=x_refs[i] if own else slot, dst_ref=slot, send_sem=send_sems.at[7 * i + k],
            recv_sem=recv_sems.at[7 * i + k], device_id=to, device_id_type=MESH)

    def mine(i):
        return pltpu.make_async_copy(x_refs[i], out_refs[i].at[4 * x + 2 * y + c], local_sems.at[i])

    def first(i):
        return [copy(i, 0, me, sibling, own=True)] + [copy(i, 1 + j, me, (*chip, c), own=True)
                                                     for j, chip in enumerate(chips)]

    def passed(i, j):
        return copy(i, 4 + j, (*chips[j], c), sibling)

    def start():
        for i in range(nb):
            mine(i).start()
            for cp in first(i):
                cp.start()

    def forward():
        for i in range(nb):
            for j, chip in enumerate(chips):
                copy(i, 1 + j, (*chip, c), me).wait_recv()
                passed(i, j).start()

    def finish():
        for i in range(nb):
            copy(i, 0, sibling, me).wait_recv()
            for j, chip in enumerate(chips):
                copy(i, 4 + j, (*chip, 1 - c), me).wait_recv()
        for i in range(nb):
            for cp in first(i) + [passed(i, j) for j in range(3)]:
                cp.wait_send()
            mine(i).wait()

    return start, forward, finish


def _swap_with_sibling(pks, name):
    nb = len(pks)

    def body(*refs):
        start, finish = _sibling_swap_phases(refs[:nb], refs[nb:2 * nb], *refs[2 * nb:])
        start()
        finish()

    return pl.pallas_call(
        body, name=name, out_shape=_sibling_swap_shapes(pks), in_specs=[ANY] * nb, out_specs=[ANY] * nb,
        scratch_shapes=_sibling_swap_sems(nb),
    )(*pks)


def _sibling_swap_shapes(pks):
    return [S((4,) + p.shape[1:], p.dtype) for p in pks]


def _sibling_swap_sems(nb):
    return [pltpu.SemaphoreType.DMA((4 * nb,)), pltpu.SemaphoreType.DMA((4 * nb,))]


def _sibling_swap_phases(pk_refs, out_refs, send_sems, recv_sems):
    nb = len(pk_refs)
    x, y, c = _position()

    def copies():
        return [pltpu.make_async_remote_copy(
            src_ref=pk_refs[i].at[2 * k + 1 - c], dst_ref=out_refs[i].at[k], send_sem=send_sems.at[4 * i + k],
            recv_sem=recv_sems.at[4 * i + k], device_id=(x, y, 1 - c), device_id_type=MESH)
            for i in range(nb) for k in range(4)]

    def start():
        for cp in copies():
            cp.start()

    def finish():
        for cp in copies():
            cp.wait()

    return start, finish


def _swap_between_chips(pbs, name):
    nb = len(pbs)

    def body(*refs):
        start, finish = _chip_swap_phases(refs[:nb], refs[nb:2 * nb], *refs[2 * nb:])
        start()
        finish()

    return pl.pallas_call(
        body, name=name, out_shape=[S(p.shape, p.dtype) for p in pbs], in_specs=[ANY] * nb, out_specs=[ANY] * nb,
        scratch_shapes=_chip_swap_sems(nb),
    )(*pbs)


def _chip_swap_sems(nb):
    return [pltpu.SemaphoreType.DMA((3 * nb,)), pltpu.SemaphoreType.DMA((3 * nb,)), pltpu.SemaphoreType.DMA((nb,))]


def _chip_swap_phases(pb_refs, out_refs, send_sems, recv_sems, local_sems):
    nb = len(pb_refs)
    x, y, c = _position()
    me = 2 * x + y
    chips = [(1 - x, y), (x, 1 - y), (1 - x, 1 - y)]

    def local(i):
        return pltpu.make_async_copy(pb_refs[i].at[me], out_refs[i].at[me], local_sems.at[i])

    def send(i, j):
        cx, cy = chips[j]
        return pltpu.make_async_remote_copy(
            src_ref=pb_refs[i].at[2 * cx + cy], dst_ref=out_refs[i].at[me], send_sem=send_sems.at[3 * i + j],
            recv_sem=recv_sems.at[3 * i + j], device_id=(cx, cy, c), device_id_type=MESH)

    def arrival(i, j):
        cx, cy = chips[j]
        return pltpu.make_async_remote_copy(
            src_ref=pb_refs[i].at[me], dst_ref=out_refs[i].at[2 * cx + cy], send_sem=send_sems.at[3 * i + j],
            recv_sem=recv_sems.at[3 * i + j], device_id=(cx, cy, c), device_id_type=MESH)

    def start():
        for i in range(nb):
            local(i).start()
            for j in range(3):
                send(i, j).start()

    def finish():
        for i in range(nb):
            for j in range(3):
                arrival(i, j).wait_recv()
        for i in range(nb):
            for j in range(3):
                send(i, j).wait_send()
            local(i).wait()

    return start, finish


def _row_tile(r):
    return max(t for t in range(16, min(r, 1024) + 1, 16) if r % t == 0)


def _pair_sum_cast(pk, recv, core, name):
    _, r, l = pk.shape
    tr = _row_tile(r)

    def body(c_ref, a_ref, b_ref, o_ref):
        o_ref[...] = (a_ref[...] + b_ref[...]).astype(BF16)

    return pl.pallas_call(
        body, name=name,
        grid_spec=pltpu.PrefetchScalarGridSpec(
            num_scalar_prefetch=1, grid=(4, r // tr),
            in_specs=[pl.BlockSpec((None, tr, l), lambda k, i, c: (2 * k + c[0], i, 0)),
                      pl.BlockSpec((None, tr, l), lambda k, i, c: (k, i, 0))],
            out_specs=pl.BlockSpec((None, tr, l), lambda k, i, c: (k, i, 0))),
        out_shape=S((4, r, l), BF16), compiler_params=_cp(("parallel", "parallel")),
    )(core, pk, recv)


def _final_sum(pk, recv_sib, recv_chips, slot, chip, name):
    _, r, l = pk.shape
    tr = _row_tile(r)

    def body(s_ref, k_ref, a_ref, b_ref, rc_ref, o_ref):
        base = a_ref[...] + b_ref[...]
        acc = jnp.zeros_like(base)
        for j in range(4):
            acc = acc + jnp.where(k_ref[0] == j, base, rc_ref[j].astype(F32))
        o_ref[...] = acc

    return pl.pallas_call(
        body, name=name,
        grid_spec=pltpu.PrefetchScalarGridSpec(
            num_scalar_prefetch=2, grid=(r // tr,),
            in_specs=[pl.BlockSpec((None, tr, l), lambda i, s, k: (s[0], i, 0)),
                      pl.BlockSpec((None, tr, l), lambda i, s, k: (k[0], i, 0)),
                      pl.BlockSpec((4, tr, l), lambda i, s, k: (0, i, 0))],
            out_specs=pl.BlockSpec((tr, l), lambda i, s, k: (i, 0))),
        out_shape=S((r, l), F32), compiler_params=_cp(("parallel",)),
    )(slot, chip, pk, recv_sib, recv_chips)


def _adamw_math(w, g, m, v):
    m = ADAM_B1 * m + (1.0 - ADAM_B1) * g
    v = ADAM_B2 * v + (1.0 - ADAM_B2) * (g * g)
    m_hat = m / (1.0 - ADAM_B1 ** ADAM_STEP)
    v_hat = v / (1.0 - ADAM_B2 ** ADAM_STEP)
    return -ADAM_LR * (m_hat / (jnp.sqrt(v_hat) + ADAM_EPS) + ADAM_WD * w), m, v


def _adamw(w, g, m, v, name):
    r, c = w.shape
    tr = 512 if r % 512 == 0 else r

    def body(w_ref, g_ref, m_ref, v_ref, d_ref, nm_ref, nv_ref):
        d_ref[...], nm_ref[...], nv_ref[...] = _adamw_math(w_ref[...], g_ref[...], m_ref[...], v_ref[...])

    tile = pl.BlockSpec((tr, c), lambda i: (i, 0))
    return pl.pallas_call(
        body, name=name, grid=(r // tr,), in_specs=[tile] * 4, out_specs=[tile] * 3, out_shape=[S((r, c), F32)] * 3,
        compiler_params=_cp(("parallel",)),
    )(w, g, m, v)


def _small_update(gathered, w, m, v):
    def body(ga_ref, w_ref, m_ref, v_ref, g_ref, d_ref, nm_ref, nv_ref):
        g = ga_ref[0]
        for k in range(1, N_DEV):
            g = g + ga_ref[k]
        g_ref[...] = g
        d_ref[...], nm_ref[...], nv_ref[...] = _adamw_math(w_ref[...], g, m_ref[...], v_ref[...])

    return pl.pallas_call(body, name="small_update", out_shape=[S(w.shape, F32)] * 4, compiler_params=_cp())(
        gathered, w, m, v)


BIG = ("w_in", "mem_kv_w", "w_br_hgrn", "w_br_fox", "w_br_mem", "w_out", "ffn_w_up", "ffn_conv_w", "ffn_w_down")
GROUP_ROWS = ("w_out", "ffn_w_down")
GROUP_LANE = ("w_br_hgrn", "w_br_fox", "w_br_mem")
LANE_GROUP_ROWS = 224
SMALL = ("norm_mix_g", "norm_mem_g", "hgrn_lb_logits", "hgrn_norm_g", "fox_f_bias", "fox_q_norm_g", "fox_k_norm_g",
         "mem_q_norm_g", "mem_k_norm_g", "norm_ffn_g", "ffn_conv_b")


def _rows_of(n_elems):
    return -(-n_elems // LANE)


def _to_rows(a, lead=0):
    flat = a.reshape(a.shape[:lead] + (-1,))
    pad = (-flat.shape[-1]) % LANE
    if pad:
        flat = jnp.pad(flat, [(0, 0)] * lead + [(0, pad)])
    return flat.reshape(a.shape[:lead] + (-1, LANE))


def _stack_rows(parts, lead, total_rows):
    buf = jnp.concatenate(parts, axis=lead)
    pad = total_rows - buf.shape[lead]
    return jnp.pad(buf, [(0, 0)] * lead + [(0, pad), (0, 0)])


def _round_up(n, k):
    return -(-n // k) * k


def _from_rows(rows, shape, lead=0):
    n = math.prod(shape)
    return rows.reshape(rows.shape[:lead] + (-1,))[..., :n].reshape(rows.shape[:lead] + tuple(shape))


def _blocks_to_full(blocks, kind):
    n, a, b = blocks.shape
    return blocks.transpose(1, 0, 2).reshape(a, n * b) if kind == "col" else blocks.reshape(n * a, b)


def _full_to_blocks(full, kind, n=N_DEV):
    a, b = full.shape
    return full.reshape(a, n, b // n).transpose(1, 0, 2) if kind == "col" else full.reshape(n, a // n, b)


def _lane_group_rows(shard):
    n_lane = sum(shard[n].shape[0] for n in GROUP_LANE)
    n_cw = shard["ffn_conv_w"].size
    return n_lane, _rows_of(3 * n_cw), _rows_of(n_cw), _round_up(n_lane + _rows_of(3 * n_cw), LANE_GROUP_ROWS)


def _split_bf16x3(x):
    hi = x.astype(BF16)
    r1 = x - hi.astype(F32)
    mid = r1.astype(BF16)
    return jnp.stack([hi, mid, (r1 - mid.astype(F32)).astype(BF16)])


class _Exchange:
    def __init__(self, shard):
        self.shard = shard
        xi, yi, ci = _position()
        self.core = ci.astype(jnp.int32).reshape(1)
        self.chip = (2 * xi + yi).astype(jnp.int32).reshape(1)
        self.n_lane, self.r_pieces, self.r_vals, self.r_lane = _lane_group_rows(shard)

    def first_blocks(self):
        return [self.shard["w_in"].astype(BF16), self.shard["mem_kv_w"].astype(BF16)]

    def unpack_first(self, gathered):
        return {"w_in": _perm_from_blocks(gathered[0]), "mem_kv_w": _blocks_to_full(gathered[1], "row")}

    def late_blocks(self):
        sh = self.shard
        lane_rows = [sh[n].astype(BF16) for n in GROUP_LANE] + [_to_rows(_split_bf16x3(sh["ffn_conv_w"]))]
        return [sh[n].astype(BF16) for n in GROUP_ROWS] + [sh["ffn_w_up"].astype(BF16),
                                                           _stack_rows(lane_rows, 0, self.r_lane)]

    def unpack_late(self, gathered):
        *rows, gc, gd = gathered
        sh = self.shard
        W = {"ffn_w_up": _blocks_to_full(gc, "col")}
        for n, blocks in zip(GROUP_ROWS, rows):
            W[n] = _blocks_to_full(blocks, "row")
        r0 = 0
        for n in GROUP_LANE:
            W[n] = _blocks_to_full(gd[:, r0:r0 + sh[n].shape[0]], "col")
            r0 += sh[n].shape[0]
        cw = _from_rows(gd[:, self.n_lane:self.n_lane + self.r_pieces], (3,) + sh["ffn_conv_w"].shape, lead=1).astype(F32)
        W["ffn_conv_w"] = _blocks_to_full(cw[:, 0] + cw[:, 1] + cw[:, 2], "col")
        return W

    def early_grads(self, g):
        cw_rows = _to_rows(_full_to_blocks(g["ffn_conv_w"], "col"), lead=1)
        return [_full_to_blocks(g[n], "row") for n in GROUP_ROWS] + [
            jnp.concatenate([_full_to_blocks(h, "col", N_DEV // 2) for h in g["ffn_w_up"]], axis=0),
            _stack_rows([_full_to_blocks(g[n], "col") for n in GROUP_LANE] + [cw_rows], 1, self.r_lane)]

    def last_grads(self, g):
        return [_unperm_blocks(g["w_in"], N_DEV), _full_to_blocks(g["mem_kv_w"], "row")]

    def pair_sums(self, pks, recv_sib, tag):
        return [_pair_sum_cast(p, r, self.core, "rs_pair_sum_%s%d" % (tag, i))
                for i, (p, r) in enumerate(zip(pks, recv_sib))]

    def final_sums(self, pks, recv_sib, recv_chips, tag):
        return [_final_sum(p, rs, rc, 2 * self.chip + self.core, self.chip, "rs_final_sum_%s%d" % (tag, i))
                for i, (p, rs, rc) in enumerate(zip(pks, recv_sib, recv_chips))]

    def unpack_grads(self, early, last):
        sh = self.shard
        *rows, g_up, g_lane = early
        g_shard = {"w_in": last[0], "mem_kv_w": last[1], "ffn_w_up": g_up, **dict(zip(GROUP_ROWS, rows))}
        r0 = 0
        for n in GROUP_LANE:
            g_shard[n] = g_lane[r0:r0 + sh[n].shape[0]]
            r0 += sh[n].shape[0]
        g_shard["ffn_conv_w"] = _from_rows(g_lane[self.n_lane:self.n_lane + self.r_vals], sh["ffn_conv_w"].shape)
        return g_shard


def kernel(x, mem, norm_mix_g, norm_mem_g, w_in, hgrn_lb_logits, hgrn_norm_g, fox_f_bias, fox_q_norm_g, fox_k_norm_g, mem_kv_w, mem_q_norm_g, mem_k_norm_g, w_br_hgrn, w_br_fox, w_br_mem, w_out, norm_ffn_g, ffn_w_up, ffn_conv_w, ffn_conv_b, ffn_w_down, loss_target, m_norm_mix_g, m_norm_mem_g, m_w_in, m_hgrn_lb_logits, m_hgrn_norm_g, m_fox_f_bias, m_fox_q_norm_g, m_fox_k_norm_g, m_mem_kv_w, m_mem_q_norm_g, m_mem_k_norm_g, m_w_br_hgrn, m_w_br_fox, m_w_br_mem, m_w_out, m_norm_ffn_g, m_ffn_w_up, m_ffn_conv_w, m_ffn_conv_b, m_ffn_w_down, v_norm_mix_g, v_norm_mem_g, v_w_in, v_hgrn_lb_logits, v_hgrn_norm_g, v_fox_f_bias, v_fox_q_norm_g, v_fox_k_norm_g, v_mem_kv_w, v_mem_q_norm_g, v_mem_k_norm_g, v_w_br_hgrn, v_w_br_fox, v_w_br_mem, v_w_out, v_norm_ffn_g, v_ffn_w_up, v_ffn_conv_w, v_ffn_conv_b, v_ffn_w_down):
    given = dict(locals())
    order = ("norm_mix_g", "norm_mem_g", "w_in", "hgrn_lb_logits", "hgrn_norm_g", "fox_f_bias", "fox_q_norm_g",
             "fox_k_norm_g", "mem_kv_w", "mem_q_norm_g", "mem_k_norm_g", "w_br_hgrn", "w_br_fox", "w_br_mem", "w_out",
             "norm_ffn_g", "ffn_w_up", "ffn_conv_w", "ffn_conv_b", "ffn_w_down")
    B, T, D = x.shape
    M = mem.shape[1]
    shard = {n: given[n][0] if n in BIG else given[n] for n in order}
    mom = {n: (given["m_" + n][0], given["v_" + n][0]) if n in BIG else (given["m_" + n], given["v_" + n])
           for n in order}
    shard["hgrn_lb_logits"] = hgrn_lb_logits
    for n in ("norm_mix_g", "norm_mem_g", "hgrn_norm_g", "fox_f_bias", "fox_q_norm_g", "fox_k_norm_g", "mem_q_norm_g",
              "mem_k_norm_g", "norm_ffn_g", "ffn_conv_b"):
        shard[n] = given[n].reshape(1, -1)

    ex = _Exchange(shard)
    sm = {n: shard[n] for n in SMALL}
    grad_x, g, sums = _local_step(x.reshape(B * T, D), mem.reshape(B * M, D), loss_target.reshape(B * T, D), sm, None,
                                  B, T, M, ex)
    g_shard = ex.unpack_grads(*sums)

    sg = {n: g[n] for n in SMALL}
    sg["fox_f_bias"] = g["fox_f_bias"][:, :FOX_H]
    sg["fox_q_norm_g"] = g["fox_q_norm_g"][:, :FOX_D]
    sg["fox_k_norm_g"] = g["fox_k_norm_g"][:, :FOX_D]
    slayout, row0 = {}, 0
    for n in SMALL:
        nr = _rows_of(shard[n].size)
        slayout[n] = (row0, nr)
        row0 += nr
    loss_row = row0
    r_small = _round_up(row0 + 1, 8)

    def pack_small(d, with_loss=None):
        rows = [_to_rows(d[n]) for n in SMALL]
        rows.append(with_loss if with_loss is not None else jnp.zeros((1, LANE), F32))
        return _stack_rows(rows, 0, r_small)

    sgath, = _all_gather([pack_small(sg, g["loss"])], "ag_small")
    s_g, s_d, s_m, s_v = _small_update(sgath, pack_small(shard), pack_small({n: mom[n][0].reshape(shard[n].shape) for n in SMALL}),
                                       pack_small({n: mom[n][1].reshape(shard[n].shape) for n in SMALL}))
    loss = s_g[loss_row, 0]

    grads, deltas, new_m, new_v = {}, {}, {}, {}
    for n in BIG:
        gn = g_shard[n]
        d, nm, nv = _adamw(shard[n], gn, mom[n][0], mom[n][1], "adamw_" + n)
        grads[n], deltas[n], new_m[n], new_v[n] = (a[None] for a in (gn, d, nm, nv))
    for n in SMALL:
        r0, nr = slayout[n]
        for dst, src in ((grads, s_g), (deltas, s_d), (new_m, s_m), (new_v, s_v)):
            dst[n] = _from_rows(src[r0:r0 + nr], given[n].shape)
    return (loss, grad_x.reshape(B, T, D), *[grads[n] for n in order], *[deltas[n] for n in order],
            *[new_m[n] for n in order], *[new_v[n] for n in order])
```

```python
import functools
import math

import jax
import jax.numpy as jnp
from jax import lax
from jax.experimental import pallas as pl
from jax.experimental.pallas import tpu as pltpu

F32, BF16 = jnp.float32, jnp.bfloat16
S = jax.ShapeDtypeStruct
MESH = pl.DeviceIdType.MESH

N_DEV = 8
EPS = 1e-6
LANE = 128
CHUNK = 64
SUB = 16
HG_H, HG_D = 4, 128
HG_GROUP_FWD = 4
HG_GROUP = 2
FOX_H, FOX_D = 8, 64
FOX_P = FOX_H // 2
MEM_H, MEM_D = 4, 128
NEG = -1e30
VMEM_LIMIT = 56 * 2**20

ADAM_LR, ADAM_B1, ADAM_B2, ADAM_EPS, ADAM_WD, ADAM_STEP = 0.001, 0.9, 0.999, 1e-08, 0.01, 10

C_FOX, C_MQ, C_HG, C_GATE, C_FF, C_END = 0, 1536, 2048, 4096, 7168, 7296


def _cp(sem=None):
    return pltpu.CompilerParams(dimension_semantics=sem, vmem_limit_bytes=VMEM_LIMIT)


def _dot(a, b, dims, prec=None):
    return lax.dot_general(a, b, (dims, ((), ())), preferred_element_type=F32, precision=prec)


def _nn(a, b, prec=None):
    return _dot(a, b, ((1,), (0,)), prec)


def _nt(a, b, prec=None):
    return _dot(a, b, ((1,), (1,)), prec)


def _tn(a, b, prec=None):
    return _dot(a, b, ((0,), (0,)), prec)


def _b(x):
    return x.astype(BF16)


def _mm3(fn, a, b):
    ah, bh = _b(a), _b(b)
    return fn(ah, bh) + fn(ah, _b(b - bh.astype(F32))) + fn(_b(a - ah.astype(F32)), bh)


def _iota(shape, dim):
    return lax.broadcasted_iota(jnp.int32, shape, dim)


def _rowsum8(x):
    r, d = x.shape
    return jnp.sum(x.reshape(r // 8, 8, d), axis=0)


def _rmsnorm_cast(x, g, name, tm=1024, gather=()):
    n, d = x.shape
    nga = len(gather)

    def body(x_ref, g_ref, o_ref):
        v = x_ref[...]
        r = lax.rsqrt(jnp.mean(v * v, axis=-1, keepdims=True) + EPS)
        o_ref[...] = (v * r * g_ref[...]).astype(BF16)

    if nga:
        body = _hosting(body, 2, 1, 0, nga, _gather_phases, (n // tm,))
    out = pl.pallas_call(
        body, name=name, grid=(n // tm,),
        in_specs=[pl.BlockSpec((tm, d), lambda i: (i, 0)), pl.BlockSpec((1, d), lambda i: (0, 0))] + [ANY] * nga,
        out_specs=[pl.BlockSpec((tm, d), lambda i: (i, 0))] + [ANY] * nga,
        out_shape=[S((n, d), BF16)] + _gather_shapes(gather), scratch_shapes=_gather_sems(nga) if nga else [],
        compiler_params=_cp(("arbitrary",) if nga else ("parallel",)),
    )(x, g, *gather)
    return out if nga else out[0]


def _rmsnorm_bwd(dh, x, g, resid, name, tm=1024):
    n, d = x.shape
    has_res = resid is not None

    def body(*refs):
        if has_res:
            dh_ref, x_ref, g_ref, r_ref, dx_ref, dg_ref = refs
        else:
            dh_ref, x_ref, g_ref, dx_ref, dg_ref = refs
        v = x_ref[...]
        dhv = dh_ref[...].astype(F32)
        r = lax.rsqrt(jnp.mean(v * v, axis=-1, keepdims=True) + EPS)
        xh = v * r
        u = dhv * g_ref[...]
        dx = r * (u - xh * jnp.mean(u * xh, axis=-1, keepdims=True))
        if has_res:
            dx = dx + r_ref[...]
        dx_ref[...] = dx

        @pl.when(pl.program_id(0) == 0)
        def _():
            dg_ref[...] = jnp.zeros_like(dg_ref)

        dg_ref[...] += _rowsum8(dhv * xh)

    tile = pl.BlockSpec((tm, d), lambda i: (i, 0))
    ins = [tile, tile, pl.BlockSpec((1, d), lambda i: (0, 0))] + ([tile] if has_res else [])
    args = (dh, x, g) + ((resid,) if has_res else ())
    return pl.pallas_call(
        body, name=name, grid=(n // tm,), in_specs=ins,
        out_specs=[tile, pl.BlockSpec((8, d), lambda i: (0, 0))],
        out_shape=[S((n, d), F32), S((8, d), F32)], compiler_params=_cp(("arbitrary",)),
    )(*args)


def _mm_nn(a, b, out_dtype, name, tm, tn):
    m, k = a.shape
    n = b.shape[1]
    assert n % tn == 0 and m % tm == 0

    def body(a_ref, b_ref, o_ref):
        o_ref[...] = _nn(a_ref[...].astype(BF16), b_ref[...].astype(BF16)).astype(out_dtype)

    return pl.pallas_call(
        body, name=name, grid=(n // tn, m // tm),
        in_specs=[pl.BlockSpec((tm, k), lambda j, i: (i, 0)), pl.BlockSpec((k, tn), lambda j, i: (0, j))],
        out_specs=pl.BlockSpec((tm, tn), lambda j, i: (i, j)), out_shape=S((m, n), out_dtype),
        compiler_params=_cp(("parallel", "parallel")),
    )(a, b)


def _mm_nt_sum(parts, w, name, tm, swap=()):
    m = parts[0][0].shape[0]
    k = w.shape[0]
    assert m % tm == 0 and all(c % n == 0 and o % n == 0 for _, c, n, o in parts)
    np_ = len(parts)
    nsw = len(swap)
    n_steps = m // tm

    def body(*refs):
        o_ref = refs[2 * np_ + nsw]
        if nsw:
            start, finish = _chip_swap_phases(refs[2 * np_:2 * np_ + nsw], refs[2 * np_ + nsw + 1:2 * np_ + 2 * nsw + 1],
                                              *refs[2 * np_ + 2 * nsw + 1:])
            pl.when(pl.program_id(0) == 0)(start)
        acc = _nt(refs[0][...].astype(BF16), refs[np_][...].astype(BF16))
        for i in range(1, np_):
            acc = acc + _nt(refs[i][...].astype(BF16), refs[np_ + i][...].astype(BF16))
        o_ref[...] = acc
        if nsw:
            pl.when(pl.program_id(0) == n_steps - 1)(finish)

    dy_specs = [pl.BlockSpec((tm, n), functools.partial(lambda i, j: (i, j), j=c // n)) for _, c, n, _ in parts]
    w_specs = [pl.BlockSpec((k, n), functools.partial(lambda i, j: (0, j), j=o // n)) for _, _, n, o in parts]
    out = pl.pallas_call(
        body, name=name, grid=(n_steps,), in_specs=dy_specs + w_specs + [ANY] * nsw,
        out_specs=[pl.BlockSpec((tm, k), lambda i: (i, 0))] + [ANY] * nsw,
        out_shape=[S((m, k), F32)] + [S(p.shape, p.dtype) for p in swap],
        scratch_shapes=_chip_swap_sems(nsw) if nsw else [],
        compiler_params=_cp(("arbitrary",) if nsw else ("parallel",)),
    )(*([p[0] for p in parts] + [w] * np_ + list(swap)))
    return (out[0], out[1:]) if nsw else out[0]


def _mm_tn(x, dy, name, tm, tn):
    m, k = x.shape
    n = dy.shape[1]
    tm = min(tm, m)
    assert m % tm == 0 and n % tn == 0

    def body(x_ref, dy_ref, o_ref):
        part = _tn(x_ref[...].astype(BF16), dy_ref[...].astype(BF16))

        @pl.when(pl.program_id(1) == 0)
        def _():
            o_ref[...] = part

        @pl.when(pl.program_id(1) > 0)
        def _():
            o_ref[...] += part

    return pl.pallas_call(
        body, name=name, grid=(n // tn, m // tm),
        in_specs=[pl.BlockSpec((tm, k), lambda j, i: (i, 0)), pl.BlockSpec((tm, tn), lambda j, i: (i, j))],
        out_specs=pl.BlockSpec((k, tn), lambda j, i: (0, j)), out_shape=S((k, n), F32),
        compiler_params=_cp(("parallel", "arbitrary")),
    )(x, dy)


def _lower_bound(logits):
    e = jnp.exp(logits - jnp.max(logits, axis=0, keepdims=True))
    return e[0:1, :] / jnp.sum(e, axis=0, keepdims=True)


def _hg_gates(fl, lb):
    sig = jax.nn.sigmoid(fl)
    f = lb + (1.0 - lb) * sig
    k = (1.0 - lb) * (1.0 - sig)
    return sig, f, k, jnp.log(f)


def _silu_and_grad(x):
    s = jax.nn.sigmoid(x)
    return x * s, s * (1.0 + x * (1.0 - s))


def _hg_rowblocks(G):
    return [None] + [G[SUB * i - 1:SUB * i, :] for i in range(1, CHUNK // SUB)]


def _hg_intra_A(qs, k, G):
    refs = _hg_rowblocks(G)
    cols = _iota((SUB, LANE), 1)
    rows = _iota((SUB, LANE), 0)
    no_keys = jnp.zeros((LANE - CHUNK, HG_D), BF16)
    blocks = []
    for i in range(CHUNK // SUB):
        lo = SUB * i
        qb, Gb = qs[lo:lo + SUB, :], G[lo:lo + SUB, :]
        diag = jnp.zeros((SUB, LANE), F32)
        for s in range(SUB):
            e = jnp.exp(jnp.minimum(Gb - G[lo + s:lo + s + 1, :], 0.0))
            col = jnp.sum(qb * k[lo + s:lo + s + 1, :] * e, axis=-1, keepdims=True)
            diag = jnp.where(cols == lo + s, col, diag)
        a = jnp.where((cols >= lo) & (cols <= rows + lo), diag, 0.0)
        if i > 0:
            qr = qb * jnp.exp(Gb - refs[i])
            kr = k * jnp.exp(jnp.minimum(refs[i] - G, 0.0))
            a = jnp.where(cols < lo, _nt(_b(qr), jnp.concatenate([_b(kr), no_keys], axis=0)), a)
        blocks.append(a)
    return jnp.concatenate(blocks, axis=0)


def _hg_intra_bwd(dA, qs, k, G):
    refs = _hg_rowblocks(G)
    cols = _iota((SUB, CHUNK), 1)
    rows16 = _iota((SUB, HG_D), 0)
    dk = jnp.zeros((CHUNK, HG_D), F32)
    dq_blocks, dk_diag_blocks = [], []
    for i in range(CHUNK // SUB):
        lo = SUB * i
        qb, Gb = qs[lo:lo + SUB, :], G[lo:lo + SUB, :]
        dAb = dA[lo:lo + SUB, :]
        dq = jnp.zeros((SUB, HG_D), F32)
        dkb = jnp.zeros((SUB, HG_D), F32)
        for s in range(SUB):
            e = jnp.exp(jnp.minimum(Gb - G[lo + s:lo + s + 1, :], 0.0))
            e = jnp.where(rows16 >= s, e, 0.0)
            dcol = jnp.sum(jnp.where(cols == lo + s, dAb, 0.0), axis=-1, keepdims=True)
            w = dcol * e
            dq = dq + w * k[lo + s:lo + s + 1, :]
            dkb = jnp.where(rows16 == s, jnp.sum(w * qb, axis=0, keepdims=True), dkb)
        if i > 0:
            e1 = jnp.exp(Gb - refs[i])
            e2 = jnp.exp(jnp.minimum(refs[i] - G, 0.0))
            dA_off = jnp.where(cols < lo, dAb, 0.0)
            dq = dq + _mm3(_nn, dA_off, k * e2) * e1
            dk = dk + _mm3(_tn, dA_off, qb * e1) * e2
        dq_blocks.append(dq)
        dk_diag_blocks.append(dkb)
    return jnp.concatenate(dq_blocks, axis=0), dk + jnp.concatenate(dk_diag_blocks, axis=0)


def _tri(n, upper=False):
    r, c = _iota((n, n), 0), _iota((n, n), 1)
    return jnp.where((c >= r) if upper else (r >= c), 1.0, 0.0).astype(BF16)


def _prefix_mm(tri, x):
    hi = x.astype(BF16)
    r1 = x - hi.astype(F32)
    mid = r1.astype(BF16)
    lo = (r1 - mid.astype(F32)).astype(BF16)
    return _nn(tri, hi) + _nn(tri, mid) + _nn(tri, lo)


def _hgrn_fwd(z, lb, gn, B, T):
    N = B * T
    NC = T // CHUNK
    ng = HG_H // HG_GROUP_FWD

    def body(z_ref, lb_ref, gn_ref, y_ref, o_ref, st_ref, a_ref, s_scr):
        lbs = _lower_bound(lb_ref[...])
        tri = _tri(CHUNK)
        s_scr[...] = jnp.zeros_like(s_scr)

        def chunk(c, carry):
            r = pl.ds(pl.multiple_of(c * CHUNK, CHUNK), CHUNK)
            for hh in range(HG_GROUP_FWD):
                zc, oc = 4 * LANE * hh, LANE * hh
                ql, fl, il, gl = (z_ref[r, zc + LANE * j:zc + LANE * (j + 1)].astype(F32) for j in range(4))
                _, _, k, logf = _hg_gates(fl, lbs[:, oc:oc + LANE])
                G = _prefix_mm(tri, logf)
                qs = ql * jax.nn.sigmoid(ql)
                st = s_scr[hh]
                st_ref[hh * NC + c] = st
                g_last = G[CHUNK - 1:CHUNK, :]
                A = _b(_hg_intra_A(qs, k, G))
                a_ref[r, oc:oc + LANE] = A
                o = _nn(A[:, 0:CHUNK], _b(il)) + _nt(_b(qs * jnp.exp(G)), _b(st))
                s_scr[hh] = st * jnp.exp(g_last) + _mm3(_tn, il, k * jnp.exp(g_last - G))
                o_ref[r, oc:oc + LANE] = o
                rstd = lax.rsqrt(jnp.mean(o * o, axis=-1, keepdims=True) + EPS)
                y_ref[r, oc:oc + LANE] = (o * rstd * gn_ref[...] * (gl * jax.nn.sigmoid(gl))).astype(BF16)
            return carry

        lax.fori_loop(0, NC, chunk, 0, unroll=4)

    gw = HG_GROUP_FWD * LANE
    cb = C_HG // (4 * gw)
    return pl.pallas_call(
        body, name="hgrn_fwd", grid=(B, ng),
        in_specs=[pl.BlockSpec((T, 4 * gw), lambda b, h: (b, cb + h)), pl.BlockSpec((lb.shape[0], gw), lambda b, h: (0, h)),
                  pl.BlockSpec((1, LANE), lambda b, h: (0, 0))],
        out_specs=[pl.BlockSpec((T, gw), lambda b, h: (b, h)), pl.BlockSpec((T, gw), lambda b, h: (b, h)),
                   pl.BlockSpec((HG_GROUP_FWD * NC, HG_D, HG_D), lambda b, h: (b * ng + h, 0, 0)),
                   pl.BlockSpec((T, gw), lambda b, h: (b, h))],
        out_shape=[S((N, 512), BF16), S((N, 512), F32), S((B * HG_H * NC, HG_D, HG_D), F32), S((N, 512), BF16)],
        scratch_shapes=[pltpu.VMEM((HG_GROUP_FWD, HG_D, HG_D), F32)], compiler_params=_cp(("parallel", "parallel")),
    )(z, lb, gn)


def _hgrn_bwd(z, o_raw, states, a_mat, dy, lb, gn, B, T, swap_sibling=()):
    N = B * T
    NC = T // CHUNK
    ng = HG_H // HG_GROUP
    nsw = len(swap_sibling)

    def body(z_ref, o_ref, st_ref, a_ref, dy_ref, lb_ref, gn_ref, dz_ref, dlb_ref, dgn_ref, ds_scr, racc, dgn_acc):
        lbs = _lower_bound(lb_ref[...])
        gn_v = gn_ref[...]
        tri, triu = _tri(CHUNK), _tri(CHUNK, upper=True)
        cmask = _iota((CHUNK, CHUNK), 0) >= _iota((CHUNK, CHUNK), 1)
        for ref in (ds_scr, racc, dgn_acc, dlb_ref):
            ref[...] = jnp.zeros_like(ref)

        def chunk(ci, carry):
            c = NC - 1 - ci
            r = pl.ds(pl.multiple_of(c * CHUNK, CHUNK), CHUNK)
            for hh in range(HG_GROUP):
                zc, oc = 4 * LANE * hh, LANE * hh
                lb_v = lbs[:, oc:oc + LANE]
                ql, fl, il, gl = (z_ref[r, zc + LANE * j:zc + LANE * (j + 1)].astype(F32) for j in range(4))
                sig, f, k, logf = _hg_gates(fl, lb_v)
                G = _prefix_mm(tri, logf)
                qs, dsilu_q = _silu_and_grad(ql)
                gs, dsilu_g = _silu_and_grad(gl)
                o = o_ref[r, oc:oc + LANE]
                dyv = dy_ref[r, oc:oc + LANE]
                rstd = lax.rsqrt(jnp.mean(o * o, axis=-1, keepdims=True) + EPS)
                oh = o * rstd
                dgl = dyv * oh * gn_v * dsilu_g
                dn = dyv * gs
                dgn_acc[...] += _rowsum8(dn * oh)
                u = dn * gn_v
                do = rstd * (u - oh * jnp.mean(u * oh, axis=-1, keepdims=True))
                st = st_ref[hh * NC + c]
                dst = ds_scr[hh]
                eG = jnp.exp(G)
                g_last = G[CHUNK - 1:CHUNK, :]
                eL = jnp.exp(g_last - G)
                dA = jnp.where(cmask, _mm3(_nt, do, il), 0.0)
                dq_in, dk_in = _hg_intra_bwd(dA, qs, k, G)
                di = _tn(a_ref[r, oc:oc + LANE][:, 0:CHUNK], _b(do)) + _nt(_b(k * eL), _b(dst))
                dq = dq_in + _mm3(_nn, do, st) * eG
                dk = dk_in + _mm3(_nn, il, dst) * eL
                ds_scr[hh] = dst * jnp.exp(g_last) + _mm3(_tn, do, qs * eG)
                dd = qs * dq - k * dk
                dlogf = _prefix_mm(triu, dd) + racc[hh]
                racc[hh] += jnp.sum(dd, axis=0, keepdims=True)
                df = dlogf / f - dk
                dlb_ref[8 * hh:8 * (hh + 1), :] += _rowsum8(df * (1.0 - sig))
                dz_ref[r, zc:zc + LANE] = (dq * dsilu_q).astype(BF16)
                dz_ref[r, zc + LANE:zc + 2 * LANE] = (df * (1.0 - lb_v) * sig * (1.0 - sig)).astype(BF16)
                dz_ref[r, zc + 2 * LANE:zc + 3 * LANE] = di.astype(BF16)
                dz_ref[r, zc + 3 * LANE:zc + 4 * LANE] = dgl.astype(BF16)
            return carry

        lax.fori_loop(0, NC, chunk, 0, unroll=4)
        dgn_ref[...] = dgn_acc[...]

    gw = HG_GROUP * LANE
    cb = C_HG // (4 * gw)
    col = pl.BlockSpec((T, gw), lambda b, h: (b, h))
    if nsw:
        body = _hosting(body, 7, 3, 3, nsw, _sibling_swap_phases, (B, ng))
    return pl.pallas_call(
        body, name="hgrn_bwd", grid=(B, ng),
        in_specs=[pl.BlockSpec((T, 4 * gw), lambda b, h: (b, cb + h)), col,
                  pl.BlockSpec((HG_GROUP * NC, HG_D, HG_D), lambda b, h: (b * ng + h, 0, 0)), col, col,
                  pl.BlockSpec((lb.shape[0], gw), lambda b, h: (0, h)), pl.BlockSpec((1, LANE), lambda b, h: (0, 0))]
        + [ANY] * nsw,
        out_specs=[pl.BlockSpec((T, 4 * gw), lambda b, h: (b, h)),
                   pl.BlockSpec((8 * HG_GROUP, LANE), lambda b, h: (b * ng + h, 0)),
                   pl.BlockSpec((8, LANE), lambda b, h: (b * ng + h, 0))] + [ANY] * nsw,
        out_shape=[S((N, 2048), BF16), S((B * HG_H * 8, LANE), F32), S((B * ng * 8, LANE), F32)]
        + _sibling_swap_shapes(swap_sibling),
        scratch_shapes=[pltpu.VMEM((HG_GROUP, HG_D, HG_D), F32), pltpu.VMEM((HG_GROUP, 1, LANE), F32),
                        pltpu.VMEM((8, LANE), F32)] + (_sibling_swap_sems(nsw) if nsw else []),
        compiler_params=_cp(("arbitrary", "arbitrary") if nsw else ("parallel", "parallel")),
    )(z, o_raw, states, a_mat, dy, lb, gn, *swap_sibling)


def _pair_mean(x, lo_half):
    a = jnp.sum(jnp.where(lo_half, x, 0.0), axis=-1, keepdims=True)
    b = jnp.sum(jnp.where(lo_half, 0.0, x), axis=-1, keepdims=True)
    return jnp.where(lo_half, a, b) * (1.0 / FOX_D)


def _fox_gate_fwd(z, bias, B, T):
    N = B * T
    tb = LANE

    def body(z_ref, b_ref, fc_ref, fct_ref):
        tri = _tri(tb)

        def step(i, carry):
            r = pl.ds(pl.multiple_of(i * tb, tb), tb)
            cs = _prefix_mm(tri, jax.nn.log_sigmoid(z_ref[r, :].astype(F32) + b_ref[...])) + carry
            fc_ref[r, :] = cs
            fct_ref[0, :, r] = cs.T[0:8, :]
            return cs[tb - 1:tb, :]

        lax.fori_loop(0, T // tb, step, jnp.zeros((1, LANE), F32))

    return pl.pallas_call(
        body, name="fox_gate_fwd", grid=(B,),
        in_specs=[pl.BlockSpec((T, LANE), lambda b: (b, C_FF // LANE)), pl.BlockSpec((1, LANE), lambda b: (0, 0))],
        out_specs=[pl.BlockSpec((T, LANE), lambda b: (b, 0)), pl.BlockSpec((1, 8, T), lambda b: (b, 0, 0))],
        out_shape=[S((N, LANE), F32), S((B, 8, T), F32)], compiler_params=_cp(("parallel",)),
    )(z, bias)


def _fox_gate_bwd(dfc, z, bias, B, T):
    N = B * T
    tb = LANE
    nt = T // tb

    def body(d_ref, z_ref, b_ref, dz_ref, db_ref):
        triu = _tri(tb, upper=True)
        db_ref[...] = jnp.zeros_like(db_ref)

        def step(ii, carry):
            r = pl.ds(pl.multiple_of((nt - 1 - ii) * tb, tb), tb)
            d = d_ref[r, 0:LANE]
            for p in range(1, FOX_P):
                d = d + d_ref[r, LANE * p:LANE * (p + 1)]
            rc = _prefix_mm(triu, d) + carry
            dff = rc * jax.nn.sigmoid(-(z_ref[r, :].astype(F32) + b_ref[...]))
            dz_ref[r, :] = dff.astype(BF16)
            db_ref[...] += _rowsum8(dff)
            return carry + jnp.sum(d, axis=0, keepdims=True)

        lax.fori_loop(0, nt, step, jnp.zeros((1, LANE), F32))

    return pl.pallas_call(
        body, name="fox_gate_bwd", grid=(B,),
        in_specs=[pl.BlockSpec((T, 512), lambda b: (b, 0)), pl.BlockSpec((T, LANE), lambda b: (b, C_FF // LANE)),
                  pl.BlockSpec((1, LANE), lambda b: (0, 0))],
        out_specs=[pl.BlockSpec((T, LANE), lambda b: (b, 0)), pl.BlockSpec((8, LANE), lambda b: (b, 0))],
        out_shape=[S((N, LANE), BF16), S((B * 8, LANE), F32)], compiler_params=_cp(("parallel",)),
    )(dfc, z, bias)


def _fox_prep(z_ref, gq, gk, r, lo_half):
    q, k, v = (z_ref[r, LANE * j:LANE * (j + 1)].astype(F32) for j in range(3))
    rq = lax.rsqrt(_pair_mean(q * q, lo_half) + EPS)
    rk = lax.rsqrt(_pair_mean(k * k, lo_half) + EPS)
    qh, kh = q * rq, k * rk
    return qh * gq * (FOX_D ** -0.5), kh * gk, v, qh, kh, rq, rk


def _fox_fwd(z, fc, fct, gq, gk, B, T, tq=512, gather=()):
    N = B * T
    NQ = T // tq
    nga = len(gather)

    def body(z_ref, fc_ref, fct_ref, gq_ref, gk_ref, y_ref, lse_ref, qn_s, kn_s, v_s):
        p, qi = pl.program_id(1), pl.program_id(2)
        lo_half = _iota((1, LANE), 1) < FOX_D

        @pl.when(qi == 0)
        def _():
            def prep(i, carry):
                r = pl.ds(pl.multiple_of(i * tq, tq), tq)
                qn, kn, v = _fox_prep(z_ref, gq_ref[...], gk_ref[...], r, lo_half)[:3]
                qn_s[r, :], kn_s[r, :], v_s[r, :] = qn.astype(BF16), kn.astype(BF16), v.astype(BF16)
                return carry
            lax.fori_loop(0, NQ, prep, 0)

        rq = pl.ds(pl.multiple_of(qi * tq, tq), tq)
        qn = qn_s[rq, :]
        fcq = fc_ref[rq, :]
        lane = _iota((tq, LANE), 1)
        causal = _iota((tq, tq), 0) >= _iota((tq, tq), 1)
        qhs = [jnp.where(lo_half, qn, jnp.zeros_like(qn)), jnp.where(lo_half, jnp.zeros_like(qn), qn)]
        fqs = [jnp.sum(jnp.where(lane == 2 * p + hh, fcq, 0.0), axis=-1, keepdims=True) for hh in range(2)]

        def kv(j, carry, diagonal):
            rk = pl.ds(pl.multiple_of(j * tq, tq), tq)
            kj, vj = kn_s[rk, :], v_s[rk, :]
            one = jnp.ones_like(vj)
            new = []
            for hh in range(2):
                m, acc = carry[hh]
                s = _nt(qhs[hh], kj) + fqs[hh] - fct_ref[0, pl.ds(2 * p + hh, 1), rk]
                if diagonal:
                    s = jnp.where(causal, s, NEG)
                m_new = jnp.maximum(m, jnp.max(s, axis=-1, keepdims=True))
                pe = jnp.exp(s - m_new)
                v_aug = jnp.where(lo_half if hh == 0 else jnp.logical_not(lo_half), vj, one)
                new.append((m_new, jnp.exp(m - m_new) * acc + _nn(pe.astype(BF16), v_aug)))
            return tuple(new)

        init = tuple((jnp.full((tq, 1), NEG, F32), jnp.zeros((tq, LANE), F32)) for _ in range(2))
        carry = lax.fori_loop(0, qi, functools.partial(kv, diagonal=False), init)
        (m0, a0), (m1, a1) = kv(qi, carry, True)
        l0, l1 = a0[:, FOX_D:FOX_D + 1], a1[:, 0:1]
        y_ref[...] = jnp.where(lo_half, a0 / l0, a1 / l1).astype(BF16)
        lse_ref[...] = jnp.where(lo_half, m0 + jnp.log(l0), m1 + jnp.log(l1))

    vec = pl.BlockSpec((1, LANE), lambda b, p, q: (0, 0))
    tile = pl.BlockSpec((tq, LANE), lambda b, p, q: (b * NQ + q, p))
    if nga:
        body = _hosting(body, 5, 2, 3, nga, _gather_phases, (B, FOX_P, NQ))
    return pl.pallas_call(
        body, name="fox_fwd", grid=(B, FOX_P, NQ),
        in_specs=[pl.BlockSpec((T, 384), lambda b, p, q: (b, p)), pl.BlockSpec((T, LANE), lambda b, p, q: (b, 0)),
                  pl.BlockSpec((1, 8, T), lambda b, p, q: (b, 0, 0)), vec, vec] + [ANY] * nga,
        out_specs=[tile, tile] + [ANY] * nga, out_shape=[S((N, 512), BF16), S((N, 512), F32)] + _gather_shapes(gather),
        scratch_shapes=[pltpu.VMEM((T, LANE), BF16)] * 3 + (_gather_sems(nga) if nga else []),
        compiler_params=_cp(("arbitrary",) * 3 if nga else ("parallel", "parallel", "arbitrary")),
    )(z, fc, fct, gq, gk, *gather)


def _fox_bwd(z, dy, y, lse, fc, fct, gq, gk, B, T, tq=512, swap=()):
    N = B * T
    NQ = T // tq
    nsw = len(swap)

    def body(z_ref, dy_ref, y_ref, lse_ref, fc_ref, fct_ref, gq_ref, gk_ref, dz_ref, dfc_ref, dgq_ref, dgk_ref,
             qn_s, kn_s, v_s, do_s, delta_s, dq_s, dfk_s):
        p, kj = pl.program_id(1), pl.program_id(2)
        lo_half = _iota((1, LANE), 1) < FOX_D
        lane = _iota((tq, LANE), 1)
        gq_v, gk_v = gq_ref[...], gk_ref[...]

        @pl.when(kj == 0)
        def _():
            def prep(i, carry):
                r = pl.ds(pl.multiple_of(i * tq, tq), tq)
                qn, kn, v = _fox_prep(z_ref, gq_v, gk_v, r, lo_half)[:3]
                qn_s[r, :], kn_s[r, :], v_s[r, :] = qn.astype(BF16), kn.astype(BF16), v.astype(BF16)
                do = dy_ref[r, :]
                do_s[r, :] = do.astype(BF16)
                delta_s[r, :] = _pair_mean(do * y_ref[r, :].astype(F32), lo_half) * float(FOX_D)
                return carry
            lax.fori_loop(0, NQ, prep, 0)
            dq_s[...] = jnp.zeros_like(dq_s)
            dgq_ref[...] = jnp.zeros_like(dgq_ref)
            dgk_ref[...] = jnp.zeros_like(dgk_ref)

        rk = pl.ds(pl.multiple_of(kj * tq, tq), tq)
        kn, vv = kn_s[rk, :], v_s[rk, :]
        causal = _iota((tq, tq), 0) >= _iota((tq, tq), 1)
        zero, one = jnp.zeros_like(kn), jnp.ones_like(kn)
        hms = [lo_half, jnp.logical_not(lo_half)]
        kmasks = [jnp.where(hm, kn, zero) for hm in hms]
        kaugs = [jnp.where(hm, kn, one) for hm in hms]
        vmasks = [jnp.where(hm, vv, zero) for hm in hms]
        fks = [fct_ref[0, pl.ds(2 * p + hh, 1), rk] for hh in range(2)]

        def qloop(i, carry, diagonal):
            ri = pl.ds(pl.multiple_of(i * tq, tq), tq)
            qn = qn_s[ri, :]
            do = do_s[ri, :]
            fcq = fc_ref[ri, :]
            new = []
            for hh in range(2):
                dk_acc, dv_acc = carry[hh]
                c0 = FOX_D * hh
                fq = jnp.sum(jnp.where(lane == 2 * p + hh, fcq, 0.0), axis=-1, keepdims=True)
                pr = jnp.exp(_nt(qn, kmasks[hh]) + fq - fks[hh] - lse_ref[ri, c0:c0 + 1])
                if diagonal:
                    pr = jnp.where(causal, pr, 0.0)
                ds = (pr * (_nt(do, vmasks[hh]) - delta_s[ri, c0:c0 + 1])).astype(BF16)
                dq_s[hh, ri, :] += _nn(ds, kaugs[hh])
                new.append((dk_acc + _tn(jnp.where(hms[hh], qn, one), ds), dv_acc + _tn(do, pr.astype(BF16))))
            return tuple(new)

        init = tuple((jnp.zeros((LANE, tq), F32), jnp.zeros((LANE, tq), F32)) for _ in range(2))
        carry = qloop(kj, init, True)
        (dk0, dv0), (dk1, dv1) = lax.fori_loop(kj + 1, NQ, functools.partial(qloop, diagonal=False), carry)
        dks, dvs = [dk0.T, dk1.T], [dv0.T, dv1.T]

        dkn = jnp.where(lo_half, dks[0], dks[1])
        _, _, _, _, kh, _, rkk = _fox_prep(z_ref, gq_v, gk_v, rk, lo_half)
        u = dkn * gk_v
        dz_ref[rk, LANE:2 * LANE] = (rkk * (u - kh * _pair_mean(u * kh, lo_half))).astype(BF16)
        dz_ref[rk, 2 * LANE:3 * LANE] = jnp.where(lo_half, dvs[0], dvs[1]).astype(BF16)
        dgk_ref[...] += _rowsum8(dkn * kh)
        dfk_s[rk, :] = jnp.where(lane == 2 * p, -dks[0][:, FOX_D:FOX_D + 1],
                                 jnp.where(lane == 2 * p + 1, -dks[1][:, 0:1], 0.0))

        @pl.when(kj == NQ - 1)
        def _():
            def fin(i, carry):
                r = pl.ds(pl.multiple_of(i * tq, tq), tq)
                d0, d1 = dq_s[0, r, :], dq_s[1, r, :]
                dqn = jnp.where(lo_half, d0, d1)
                _, _, _, qh, _, rqq, _ = _fox_prep(z_ref, gq_v, gk_v, r, lo_half)
                u = dqn * gq_v * (FOX_D ** -0.5)
                dz_ref[r, 0:LANE] = (rqq * (u - qh * _pair_mean(u * qh, lo_half))).astype(BF16)
                dgq_ref[...] += _rowsum8(dqn * qh) * (FOX_D ** -0.5)
                dfc_ref[r, :] = dfk_s[r, :] + jnp.where(lane == 2 * p, d0[:, FOX_D:FOX_D + 1],
                                                        jnp.where(lane == 2 * p + 1, d1[:, 0:1], 0.0))
                return carry
            lax.fori_loop(0, NQ, fin, 0)

    vec = pl.BlockSpec((1, LANE), lambda b, p, k: (0, 0))
    col = pl.BlockSpec((T, LANE), lambda b, p, k: (b, p))
    part = pl.BlockSpec((8, LANE), lambda b, p, k: (b * FOX_P + p, 0))
    if nsw:
        body = _hosting(body, 8, 4, 7, nsw, _chip_swap_phases, (B, FOX_P, NQ))
    return pl.pallas_call(
        body, name="fox_bwd", grid=(B, FOX_P, NQ),
        in_specs=[pl.BlockSpec((T, 384), lambda b, p, k: (b, p)), col, col, col,
                  pl.BlockSpec((T, LANE), lambda b, p, k: (b, 0)), pl.BlockSpec((1, 8, T), lambda b, p, k: (b, 0, 0)),
                  vec, vec] + [ANY] * nsw,
        out_specs=[pl.BlockSpec((T, 384), lambda b, p, k: (b, p)), col, part, part] + [ANY] * nsw,
        out_shape=[S((N, 1536), BF16), S((N, 512), F32), S((B * FOX_P * 8, LANE), F32), S((B * FOX_P * 8, LANE), F32)]
        + [S(p.shape, p.dtype) for p in swap],
        scratch_shapes=[pltpu.VMEM((T, LANE), BF16)] * 4 + [pltpu.VMEM((T, LANE), F32), pltpu.VMEM((2, T, LANE), F32),
                                                            pltpu.VMEM((T, LANE), F32)]
        + (_chip_swap_sems(nsw) if nsw else []),
        compiler_params=_cp(("arbitrary",) * 3 if nsw else ("parallel", "parallel", "arbitrary")),
    )(z, dy, y, lse, fc, fct, gq, gk, *swap)


def _mem_scores(z_ref, kv_ref, gq, gk, h):
    c = slice(MEM_D * h, MEM_D * (h + 1))
    q, k = z_ref[:, c].astype(F32), kv_ref[:, c]
    rq = lax.rsqrt(jnp.mean(q * q, axis=-1, keepdims=True) + EPS)
    rk = lax.rsqrt(jnp.mean(k * k, axis=-1, keepdims=True) + EPS)
    qh, kh = q * rq, k * rk
    qn = (qh * gq * (MEM_D ** -0.5)).astype(BF16)
    kn = (kh * gk).astype(BF16)
    s = _nt(qn, kn)
    pe = jnp.exp(s - jnp.max(s, axis=-1, keepdims=True))
    pn = pe / jnp.sum(pe, axis=-1, keepdims=True)
    return pn, qn, kn, qh, kh, rq, rk


def _mem_fwd(z, memkv, gq, gk, B, T, M, tq=1024):
    N = B * T
    tq = min(tq, T)
    NQ = T // tq
    W = MEM_H * MEM_D

    def body(z_ref, kv_ref, gq_ref, gk_ref, y_ref):
        for h in range(MEM_H):
            pn = _mem_scores(z_ref, kv_ref, gq_ref[...], gk_ref[...], h)[0]
            v = kv_ref[:, W + MEM_D * h:W + MEM_D * (h + 1)].astype(BF16)
            y_ref[:, MEM_D * h:MEM_D * (h + 1)] = _nn(pn.astype(BF16), v).astype(BF16)

    vec = pl.BlockSpec((1, LANE), lambda b, q: (0, 0))
    return pl.pallas_call(
        body, name="mem_fwd", grid=(B, NQ),
        in_specs=[pl.BlockSpec((tq, W), lambda b, q: (b * NQ + q, C_MQ // W)),
                  pl.BlockSpec((M, 2 * W), lambda b, q: (b, 0)), vec, vec],
        out_specs=pl.BlockSpec((tq, W), lambda b, q: (b * NQ + q, 0)), out_shape=S((N, W), BF16),
        compiler_params=_cp(("parallel", "parallel")),
    )(z, memkv, gq, gk)


def _mem_bwd(z, memkv, dy, gq, gk, B, T, M, tq=1024):
    N = B * T
    tq = min(tq, T)
    NQ = T // tq
    W = MEM_H * MEM_D

    def body(z_ref, kv_ref, dy_ref, gq_ref, gk_ref, dz_ref, dkv_ref, dgq_ref, dgk_ref, acc):
        qi = pl.program_id(1)
        gq_v, gk_v = gq_ref[...], gk_ref[...]

        @pl.when(qi == 0)
        def _():
            acc[...] = jnp.zeros_like(acc)
            dgq_ref[...] = jnp.zeros_like(dgq_ref)
            dgk_ref[...] = jnp.zeros_like(dgk_ref)

        for h in range(MEM_H):
            c = slice(MEM_D * h, MEM_D * (h + 1))
            cv = slice(W + MEM_D * h, W + MEM_D * (h + 1))
            pn, qn, kn, qh, _, rq, _ = _mem_scores(z_ref, kv_ref, gq_v, gk_v, h)
            do = dy_ref[:, c].astype(BF16)
            dp = _nt(do, kv_ref[:, cv].astype(BF16))
            ds = (pn * (dp - jnp.sum(dp * pn, axis=-1, keepdims=True))).astype(BF16)
            dqn = _nn(ds, kn)
            acc[:, c] += _tn(ds, qn)
            acc[:, cv] += _tn(pn.astype(BF16), do)
            u = dqn * gq_v * (MEM_D ** -0.5)
            dz_ref[:, c] = (rq * (u - qh * jnp.mean(u * qh, axis=-1, keepdims=True))).astype(BF16)
            dgq_ref[...] += _rowsum8(dqn * qh) * (MEM_D ** -0.5)

        @pl.when(qi == NQ - 1)
        def _():
            for h in range(MEM_H):
                c = slice(MEM_D * h, MEM_D * (h + 1))
                cv = slice(W + MEM_D * h, W + MEM_D * (h + 1))
                k = kv_ref[:, c]
                rk = lax.rsqrt(jnp.mean(k * k, axis=-1, keepdims=True) + EPS)
                kh = k * rk
                dkn = acc[:, c]
                u = dkn * gk_v
                dkv_ref[:, c] = (rk * (u - kh * jnp.mean(u * kh, axis=-1, keepdims=True))).astype(BF16)
                dkv_ref[:, cv] = acc[:, cv].astype(BF16)
                dgk_ref[...] += _rowsum8(dkn * kh)

    vec = pl.BlockSpec((1, LANE), lambda b, q: (0, 0))
    part = pl.BlockSpec((8, LANE), lambda b, q: (b, 0))
    return pl.pallas_call(
        body, name="mem_bwd", grid=(B, NQ),
        in_specs=[pl.BlockSpec((tq, W), lambda b, q: (b * NQ + q, C_MQ // W)),
                  pl.BlockSpec((M, 2 * W), lambda b, q: (b, 0)), pl.BlockSpec((tq, W), lambda b, q: (b * NQ + q, 0)),
                  vec, vec],
        out_specs=[pl.BlockSpec((tq, W), lambda b, q: (b * NQ + q, 0)), pl.BlockSpec((M, 2 * W), lambda b, q: (b, 0)),
                   part, part],
        out_shape=[S((N, W), BF16), S((B * M, 2 * W), BF16), S((B * 8, LANE), F32), S((B * 8, LANE), F32)],
        scratch_shapes=[pltpu.VMEM((M, 2 * W), F32)], compiler_params=_cp(("parallel", "arbitrary")),
    )(z, memkv, dy, gq, gk)


def _merge_fwd(ya, yb, yc, z, x, wa, wb, wc, wo, g_next, tm=512):
    n, d = x.shape
    wdt = ya.shape[1]
    gb = C_GATE // d

    def body(ya_ref, yb_ref, yc_ref, g0_ref, g1_ref, g2_ref, x_ref, wa_ref, wb_ref, wc_ref, wo_ref, gn_ref,
             x1_ref, mg_ref, ua_ref, ub_ref, uc_ref, h_ref):
        merged = jnp.zeros((tm, d), F32)
        for y_ref, g_ref, w_ref, u_ref in ((ya_ref, g0_ref, wa_ref, ua_ref), (yb_ref, g1_ref, wb_ref, ub_ref),
                                           (yc_ref, g2_ref, wc_ref, uc_ref)):
            u = _nn(y_ref[...], w_ref[...])
            u_ref[...] = u.astype(BF16)
            merged = merged + jax.nn.sigmoid(g_ref[...].astype(F32)) * u
        mb = merged.astype(BF16)
        mg_ref[...] = mb
        x1 = x_ref[...] + _nn(mb, wo_ref[...])
        x1_ref[...] = x1
        h_ref[...] = (x1 * lax.rsqrt(jnp.mean(x1 * x1, axis=-1, keepdims=True) + EPS) * gn_ref[...]).astype(BF16)

    yt = pl.BlockSpec((tm, wdt), lambda i: (i, 0))
    xt = pl.BlockSpec((tm, d), lambda i: (i, 0))
    wbr = pl.BlockSpec((wdt, d), lambda i: (0, 0))
    gates = [pl.BlockSpec((tm, d), functools.partial(lambda i, k: (i, gb + k), k=k)) for k in range(3)]
    return pl.pallas_call(
        body, name="merge_fwd", grid=(n // tm,),
        in_specs=[yt, yt, yt] + gates + [xt, wbr, wbr, wbr, pl.BlockSpec((d, d), lambda i: (0, 0)),
                                         pl.BlockSpec((1, d), lambda i: (0, 0))],
        out_specs=[xt] * 6, out_shape=[S((n, d), F32)] + [S((n, d), BF16)] * 5, compiler_params=_cp(("parallel",)),
    )(ya, yb, yc, z, z, z, x, wa, wb, wc, wo, g_next)


def _merge_bwd(dx1, z, ua, ub, uc, wa, wb, wc, wo, tm=512):
    n, d = dx1.shape
    wdt = wa.shape[0]
    gb = C_GATE // d

    def body(dx_ref, g0_ref, g1_ref, g2_ref, ua_ref, ub_ref, uc_ref, wa_ref, wb_ref, wc_ref, wo_ref,
             dg_ref, dya_ref, dyb_ref, dyc_ref, dua_ref, dub_ref, duc_ref):
        dm = _nt(dx_ref[...].astype(BF16), wo_ref[...])
        for k, (g_ref, u_ref, w_ref, dy_ref, du_ref) in enumerate((
                (g0_ref, ua_ref, wa_ref, dya_ref, dua_ref), (g1_ref, ub_ref, wb_ref, dyb_ref, dub_ref),
                (g2_ref, uc_ref, wc_ref, dyc_ref, duc_ref))):
            g = jax.nn.sigmoid(g_ref[...].astype(F32))
            du = (dm * g).astype(BF16)
            du_ref[...] = du
            dg_ref[:, d * k:d * (k + 1)] = (dm * u_ref[...].astype(F32) * g * (1.0 - g)).astype(BF16)
            dy_ref[...] = _nt(du, w_ref[...])

    yt = pl.BlockSpec((tm, wdt), lambda i: (i, 0))
    xt = pl.BlockSpec((tm, d), lambda i: (i, 0))
    wbr = pl.BlockSpec((wdt, d), lambda i: (0, 0))
    gates = [pl.BlockSpec((tm, d), functools.partial(lambda i, k: (i, gb + k), k=k)) for k in range(3)]
    return pl.pallas_call(
        body, name="merge_bwd", grid=(n // tm,),
        in_specs=[xt] + gates + [xt, xt, xt, wbr, wbr, wbr, pl.BlockSpec((d, d), lambda i: (0, 0))],
        out_specs=[pl.BlockSpec((tm, 3 * d), lambda i: (i, 0)), yt, yt, yt, xt, xt, xt],
        out_shape=[S((n, 3 * d), BF16)] + [S((n, wdt), F32)] * 3 + [S((n, d), BF16)] * 3,
        compiler_params=_cp(("parallel",)),
    )(dx1, z, z, z, ua, ub, uc, wa, wb, wc, wo)


FFN_TN = 1408
TN_TM = 2048
INV_SQRT2 = 0.7071067811865476
INV_SQRT_2PI = 0.3989422804014327


def _conv_shifted(a, prev, first, tm):
    row = _iota(a.shape, 0)
    p7 = jnp.where(first, 0.0, prev[7:8, :])
    p6 = jnp.where(first, 0.0, prev[6:7, :])
    a1 = jnp.where(row == 0, p7, pltpu.roll(a, 1, 0))
    a2 = jnp.where(row == 0, p6, jnp.where(row == 1, p7, pltpu.roll(a, 2, 0)))
    return a1, a2


def _ffn_act_fwd(up, cw, cb, B, T, tm=1024):
    N = B * T
    tm = min(tm, T)
    dff = cw.shape[1]
    NT, NJ, tn = T // tm, dff // FFN_TN, FFN_TN

    def body(a_ref, v_ref, cw_ref, cb_ref, y_ref, c_ref, carry):
        t = pl.program_id(2)
        a = a_ref[...].astype(F32)
        a1, a2 = _conv_shifted(a, carry[...], t == 0, tm)
        w = cw_ref[...]
        ac = w[0:1, :] * a2 + w[1:2, :] * a1 + w[2:3, :] * a + cb_ref[...]
        cdf = 0.5 * (1.0 + lax.erf(ac * INV_SQRT2))
        y_ref[...] = (ac * cdf * v_ref[...].astype(F32)).astype(BF16)
        c_ref[...] = cdf.astype(BF16)
        carry[...] = a[tm - 8:tm, :]

    return pl.pallas_call(
        body, name="ffn_act_fwd", grid=(B, NJ, NT),
        in_specs=[pl.BlockSpec((tm, tn), lambda b, j, t: (b * NT + t, j)),
                  pl.BlockSpec((tm, tn), lambda b, j, t: (b * NT + t, NJ + j)),
                  pl.BlockSpec((3, tn), lambda b, j, t: (0, j)), pl.BlockSpec((1, tn), lambda b, j, t: (0, j))],
        out_specs=[pl.BlockSpec((tm, tn), lambda b, j, t: (b * NT + t, j))] * 2, out_shape=[S((N, dff), BF16)] * 2,
        scratch_shapes=[pltpu.VMEM((8, tn), F32)], compiler_params=_cp(("parallel", "parallel", "arbitrary")),
    )(up, up, cw, cb)


def _ffn_down_loss(y, wd, x1, tgt, tm=512):
    n, d = x1.shape
    kf = y.shape[1]

    def body(y_ref, w_ref, x_ref, t_ref, dx_ref, ls_ref):
        err = x_ref[...] + _nn(y_ref[...], w_ref[...]) - t_ref[...]
        dx_ref[...] = err * (1.0 / d)

        @pl.when(pl.program_id(0) == 0)
        def _():
            ls_ref[...] = jnp.zeros_like(ls_ref)

        ls_ref[...] += _rowsum8(err * err) * (0.5 / d)

    xt = pl.BlockSpec((tm, d), lambda i: (i, 0))
    return pl.pallas_call(
        body, name="ffn_down_loss", grid=(n // tm,),
        in_specs=[pl.BlockSpec((tm, kf), lambda i: (i, 0)), pl.BlockSpec((kf, d), lambda i: (0, 0)), xt, xt],
        out_specs=[xt, pl.BlockSpec((8, d), lambda i: (0, 0))], out_shape=[S((n, d), F32), S((8, d), F32)],
        compiler_params=_cp(("arbitrary",)),
    )(y, wd, x1, tgt)


def _ffn_act_bwd1(dx2, wd, up, cdf, cw, cb, B, T, tm=512):
    N = B * T
    tm = min(tm, T)
    d = dx2.shape[1]
    dff = cw.shape[1]
    NT, NJ, tn = T // tm, dff // FFN_TN, FFN_TN

    def body(dx_ref, w_ref, a_ref, v_ref, c_ref, cw_ref, cb_ref, dac_ref, dv_ref, dcw_ref, dcb_ref, carry):
        b, t = pl.program_id(1), pl.program_id(2)
        a = a_ref[...].astype(F32)
        a1, a2 = _conv_shifted(a, carry[...], t == 0, tm)
        carry[...] = a[tm - 8:tm, :]
        w = cw_ref[...]
        ac = w[0:1, :] * a2 + w[1:2, :] * a1 + w[2:3, :] * a + cb_ref[...]
        dy = _nt(dx_ref[...].astype(BF16), w_ref[...])
        cdf = c_ref[...].astype(F32)
        dv_ref[...] = (dy * ac * cdf).astype(BF16)
        dac = dy * v_ref[...].astype(F32) * (cdf + ac * jnp.exp(-0.5 * ac * ac) * INV_SQRT_2PI)
        dac_ref[...] = dac

        @pl.when((b == 0) & (t == 0))
        def _():
            dcw_ref[...] = jnp.zeros_like(dcw_ref)
            dcb_ref[...] = jnp.zeros_like(dcb_ref)

        dcw_ref[0:8, :] += _rowsum8(dac * a2)
        dcw_ref[8:16, :] += _rowsum8(dac * a1)
        dcw_ref[16:24, :] += _rowsum8(dac * a)
        dcb_ref[...] += _rowsum8(dac)

    return pl.pallas_call(
        body, name="ffn_act_bwd1", grid=(NJ, B, NT),
        in_specs=[pl.BlockSpec((tm, d), lambda j, b, t: (b * NT + t, 0)), pl.BlockSpec((tn, d), lambda j, b, t: (j, 0)),
                  pl.BlockSpec((tm, tn), lambda j, b, t: (b * NT + t, j)),
                  pl.BlockSpec((tm, tn), lambda j, b, t: (b * NT + t, NJ + j)),
                  pl.BlockSpec((tm, tn), lambda j, b, t: (b * NT + t, j)),
                  pl.BlockSpec((3, tn), lambda j, b, t: (0, j)), pl.BlockSpec((1, tn), lambda j, b, t: (0, j))],
        out_specs=[pl.BlockSpec((tm, tn), lambda j, b, t: (b * NT + t, j)),
                   pl.BlockSpec((tm, tn), lambda j, b, t: (b * NT + t, j)),
                   pl.BlockSpec((24, tn), lambda j, b, t: (0, j)), pl.BlockSpec((8, tn), lambda j, b, t: (0, j))],
        out_shape=[S((N, dff), F32), S((N, dff), BF16), S((24, dff), F32), S((8, dff), F32)],
        scratch_shapes=[pltpu.VMEM((8, tn), F32)], compiler_params=_cp(("parallel", "arbitrary", "arbitrary")),
    )(dx2, wd, up, up, cdf, cw, cb)


def _ffn_act_bwd2(dac, cw, B, T, tm=1024):
    N = B * T
    tm = min(tm, T)
    dff = cw.shape[1]
    NT, NJ, tn = T // tm, dff // FFN_TN, FFN_TN
    last8 = N // 8 - 1

    def body(d_ref, nx_ref, cw_ref, da_ref):
        t = pl.program_id(2)
        dd = d_ref[...]
        row = _iota(dd.shape, 0)
        last = t == NT - 1
        n0 = jnp.where(last, 0.0, nx_ref[0:1, :])
        n1 = jnp.where(last, 0.0, nx_ref[1:2, :])
        d1 = jnp.where(row == tm - 1, n0, pltpu.roll(dd, tm - 1, 0))
        d2 = jnp.where(row == tm - 1, n1, jnp.where(row == tm - 2, n0, pltpu.roll(dd, tm - 2, 0)))
        w = cw_ref[...]
        da_ref[...] = (w[2:3, :] * dd + w[1:2, :] * d1 + w[0:1, :] * d2).astype(BF16)

    return pl.pallas_call(
        body, name="ffn_act_bwd2", grid=(B, NJ, NT),
        in_specs=[pl.BlockSpec((tm, tn), lambda b, j, t: (b * NT + t, j)),
                  pl.BlockSpec((8, tn), lambda b, j, t: (jnp.minimum((b * NT + t + 1) * (tm // 8), last8), j)),
                  pl.BlockSpec((3, tn), lambda b, j, t: (0, j))],
        out_specs=pl.BlockSpec((tm, tn), lambda b, j, t: (b * NT + t, j)), out_shape=S((N, dff), BF16),
        compiler_params=_cp(("parallel", "parallel", "parallel")),
    )(dac, dac, cw)


def _fold_rows(p, name):
    r, c = p.shape[0] // 8, p.shape[1]

    def body(p_ref, o_ref):
        for j in range(r):
            o_ref[j:j + 1, :] = jnp.sum(p_ref[8 * j:8 * (j + 1), :], axis=0, keepdims=True)

    return pl.pallas_call(body, name=name, out_shape=S((r, c), F32), compiler_params=_cp())(p)


def _small_reduce(lbl, dg_mix, dg_mem, dlb_p, dgn_p, dfb_p, dgq_p, dgk_p, dmq_p, dmk_p, dg_ffn, dcb_p, loss_p):
    d, dff = dg_mix.shape[1], dcb_p.shape[1]
    nbh = dlb_p.shape[0] // (8 * HG_H)

    def colsum(ref):
        return jnp.sum(ref[...], axis=0, keepdims=True)

    def body(lbl_ref, mix_ref, mem_ref, dlb_ref, dgn_ref, dfb_ref, dgq_ref, dgk_ref, dmq_ref, dmk_ref, ffn_ref, dcb_ref,
             ls_ref, o_mix, o_mem, o_lb, o_hgn, o_fb, o_fq, o_fk, o_mq, o_mk, o_ffn, o_cb, o_loss):
        o_mix[...], o_mem[...], o_ffn[...], o_cb[...] = colsum(mix_ref), colsum(mem_ref), colsum(ffn_ref), colsum(dcb_ref)
        o_hgn[...], o_fb[...], o_mq[...], o_mk[...] = colsum(dgn_ref), colsum(dfb_ref), colsum(dmq_ref), colsum(dmk_ref)
        for src, dst in ((dgq_ref, o_fq), (dgk_ref, o_fk)):
            v = colsum(src)
            dst[...] = v + pltpu.roll(v, FOX_D, 1)
        o_loss[...] = jnp.zeros((1, LANE), F32) + jnp.sum(colsum(ls_ref), axis=-1, keepdims=True)
        logits = lbl_ref[...]
        e = jnp.exp(logits - jnp.max(logits, axis=0, keepdims=True))
        pr = e / jnp.sum(e, axis=0, keepdims=True)
        rows = _iota((8, LANE), 0)
        for h in range(HG_H):
            acc = jnp.zeros((8, LANE), F32)
            for b in range(nbh):
                acc = acc + dlb_ref[8 * (b * HG_H + h):8 * (b * HG_H + h + 1), :]
            dlb = jnp.sum(acc, axis=0, keepdims=True)
            c = slice(LANE * h, LANE * (h + 1))
            p0 = pr[0:1, c]
            first = _iota((logits.shape[0], LANE), 0) == 0
            o_lb[:, c] = pr[:, c] * (jnp.where(first, 1.0, 0.0) - p0) * dlb

    outs = [S((1, d), F32), S((1, d), F32), S(lbl.shape, F32)] + [S((1, LANE), F32)] * 6 + \
           [S((1, d), F32), S((1, dff), F32), S((1, LANE), F32)]
    return pl.pallas_call(body, name="small_reduce", out_shape=outs, compiler_params=_cp())(
        lbl, dg_mix, dg_mem, dlb_p, dgn_p, dfb_p, dgq_p, dgk_p, dmq_p, dmk_p, dg_ffn, dcb_p, loss_p)


def _in_col_pieces():
    hw, fw = HG_H * HG_D, FOX_H * FOX_D
    fox0, ff0 = 4 * hw, 4 * hw + 3 * fw
    mq0 = ff0 + FOX_H
    gate0 = mq0 + MEM_H * MEM_D
    pieces = []
    for p in range(FOX_P):
        pieces += [(fox0 + j * fw + LANE * p, LANE) for j in range(3)]
    pieces.append((mq0, MEM_H * MEM_D))
    for h in range(HG_H):
        pieces += [(j * hw + HG_D * h, HG_D) for j in range(4)]
    pieces.append((gate0, C_FF - C_GATE))
    pieces.append((ff0, FOX_H))
    return pieces


def _perm_from_blocks(blocks):
    n_blk, _, c = blocks.shape
    parts = []
    for s, n in _in_col_pieces():
        lo = s
        while lo < s + n:
            d = lo // c
            hi = min(s + n, (d + 1) * c)
            parts.append(blocks[d][:, lo - d * c:hi - d * c])
            lo = hi
    parts.append(jnp.zeros((blocks.shape[1], C_END - C_FF - FOX_H), blocks.dtype))
    return jnp.concatenate(parts, axis=1)


def _unperm_blocks(segs, n_blk):
    starts = [0]
    for a in segs:
        starts.append(starts[-1] + a.shape[1])
    new_start, placed = 0, []
    for s, n in _in_col_pieces():
        placed.append((s, new_start, n))
        new_start += n
    placed.sort()
    c = sum(n for _, _, n in placed) // n_blk
    blocks = []
    for d in range(n_blk):
        parts = []
        for s, ns, n in placed:
            lo, hi = max(s, d * c), min(s + n, (d + 1) * c)
            if lo < hi:
                i = max(j for j in range(len(segs)) if starts[j] <= ns)
                parts.append(segs[i][:, ns + lo - s - starts[i]:ns + hi - s - starts[i]])
        blocks.append(jnp.concatenate(parts, axis=1))
    return jnp.stack(blocks)


def _local_step(x2, mem2, tgt, sm, W, B, T, M, ex=None):
    fbias = jnp.pad(sm["fox_f_bias"], ((0, 0), (0, LANE - FOX_H)))
    gq2 = jnp.concatenate([sm["fox_q_norm_g"]] * 2, axis=1)
    gk2 = jnp.concatenate([sm["fox_k_norm_g"]] * 2, axis=1)
    lbl = sm["hgrn_lb_logits"]
    if ex:
        h, *first = _rmsnorm_cast(x2, sm["norm_mix_g"], "norm_mix", gather=ex.first_blocks())
        W = ex.unpack_first(first)
    else:
        h = _rmsnorm_cast(x2, sm["norm_mix_g"], "norm_mix")
    z = _mm_nn(h, W["w_in"], BF16, "proj_in", 512, C_END)
    memn = _rmsnorm_cast(mem2, sm["norm_mem_g"], "norm_mem", tm=256)
    memkv = _mm_nn(memn, W["mem_kv_w"], F32, "proj_memkv", 256, 512)
    ya, o_raw, states, a_mat = _hgrn_fwd(z, lbl, sm["hgrn_norm_g"], B, T)
    fc, fct = _fox_gate_fwd(z, fbias, B, T)
    yb, lse, *late = _fox_fwd(z, fc, fct, gq2, gk2, B, T, gather=ex.late_blocks() if ex else ())
    if ex:
        W = {**W, **ex.unpack_late(late)}
    yc = _mem_fwd(z, memkv, sm["mem_q_norm_g"], sm["mem_k_norm_g"], B, T, M)
    x1, merged, ua, ub, uc, h2 = _merge_fwd(ya, yb, yc, z, x2, W["w_br_hgrn"], W["w_br_fox"], W["w_br_mem"], W["w_out"],
                                            sm["norm_ffn_g"])
    up = _mm_nn(h2, W["ffn_w_up"], BF16, "ffn_up", 512, FFN_TN)
    yf, cdf = _ffn_act_fwd(up, W["ffn_conv_w"], sm["ffn_conv_b"], B, T)
    dx2, loss_p = _ffn_down_loss(yf, W["ffn_w_down"], x1, tgt)
    dff = W["ffn_conv_w"].shape[1]
    dac, dv, dcw_p, dcb_p = _ffn_act_bwd1(dx2, W["ffn_w_down"], up, cdf, W["ffn_conv_w"], sm["ffn_conv_b"], B, T)
    da = _ffn_act_bwd2(dac, W["ffn_conv_w"], B, T)
    g = {"ffn_conv_w": _fold_rows(dcw_p, "g_conv_w")}
    g["ffn_w_down"] = _mm_tn(yf, dx2, "g_w_down", TN_TM, 512)
    dh2 = _mm_nt_sum([(da, 0, dff, 0), (dv, 0, dff, dff)], W["ffn_w_up"], "dh2", 512)
    g["ffn_w_up"] = [_mm_tn(h2, da, "g_w_up_a", TN_TM, FFN_TN), _mm_tn(h2, dv, "g_w_up_v", TN_TM, FFN_TN)]
    dx1, dg_ffn = _rmsnorm_bwd(dh2, x1, sm["norm_ffn_g"], dx2, "norm_ffn_bwd")
    g["w_out"] = _mm_tn(merged, dx1, "g_w_out", TN_TM, 512)
    dgate, dya, dyb, dyc, dua, dub, duc = _merge_bwd(dx1, z, ua, ub, uc, W["w_br_hgrn"], W["w_br_fox"], W["w_br_mem"],
                                                    W["w_out"])
    g["w_br_hgrn"] = _mm_tn(ya, dua, "g_w_br_hgrn", TN_TM, 512)
    g["w_br_fox"] = _mm_tn(yb, dub, "g_w_br_fox", TN_TM, 512)
    g["w_br_mem"] = _mm_tn(yc, duc, "g_w_br_mem", TN_TM, 512)
    early_pk = ex.early_grads(g) if ex else ()
    dz_hg, dlb_p, dgn_p, *early_sib = _hgrn_bwd(z, o_raw, states, a_mat, dya, lbl, sm["hgrn_norm_g"], B, T,
                                                swap_sibling=early_pk)
    dz_fox, dfc, dgq_p, dgk_p, *early_chips = _fox_bwd(z, dyb, yb, lse, fc, fct, gq2, gk2, B, T,
                                                       swap=ex.pair_sums(early_pk, early_sib, "early") if ex else ())
    dz_ff, dfb_p = _fox_gate_bwd(dfc, z, fbias, B, T)
    dz_mq, dkv, dmq_p, dmk_p = _mem_bwd(z, memkv, dyc, sm["mem_q_norm_g"], sm["mem_k_norm_g"], B, T, M)
    g["mem_kv_w"] = _mm_tn(memn, dkv, "g_mem_kv_w", 256, 512)
    dmemn = _mm_nt_sum([(dkv, 0, dkv.shape[1], 0)], W["mem_kv_w"], "d_memn", 256)
    _, dg_mem = _rmsnorm_bwd(dmemn, mem2, sm["norm_mem_g"], None, "norm_mem_bwd", tm=256)
    d = x2.shape[1]
    parts = [(dz_fox, 0, C_MQ - C_FOX, C_FOX), (dz_mq, 0, C_HG - C_MQ, C_MQ), (dz_hg, 0, C_GATE - C_HG, C_HG)]
    parts += [(dgate, d * k, d, C_GATE + d * k) for k in range(3)] + [(dz_ff, 0, C_END - C_FF, C_FF)]
    g["w_in"] = [_mm_tn(h, dzs, "g_w_in_%d" % i, 2 * TN_TM,
                        max(t for t in (1024, 768, 512, LANE) if dzs.shape[1] % t == 0))
                 for i, dzs in enumerate((dz_fox, dz_mq, dz_hg, dgate, dz_ff))]
    sums = None
    if ex:
        last_pk = ex.last_grads(g)
        last_sib = _swap_with_sibling(last_pk, "rs_sibling_last")
        dh, last_chips = _mm_nt_sum(parts, W["w_in"], "dh", 512, swap=ex.pair_sums(last_pk, last_sib, "last"))
        sums = (ex.final_sums(early_pk, early_sib, early_chips, "early"),
                ex.final_sums(last_pk, last_sib, last_chips, "last"))
    else:
        dh = _mm_nt_sum(parts, W["w_in"], "dh", 512)
    grad_x, dg_mix = _rmsnorm_bwd(dh, x2, sm["norm_mix_g"], dx1, "norm_mix_bwd")
    small = _small_reduce(lbl, dg_mix, dg_mem, dlb_p, dgn_p, dfb_p, dgq_p, dgk_p, dmq_p, dmk_p, dg_ffn, dcb_p, loss_p)
    names = ("norm_mix_g", "norm_mem_g", "hgrn_lb_logits", "hgrn_norm_g", "fox_f_bias", "fox_q_norm_g", "fox_k_norm_g",
             "mem_q_norm_g", "mem_k_norm_g", "norm_ffn_g", "ffn_conv_b", "loss")
    g.update(dict(zip(names, small)))
    return grad_x, g, sums


ANY = pl.BlockSpec(memory_space=pl.ANY)


def _position():
    return lax.axis_index("x"), lax.axis_index("y"), lax.axis_index("c")


def _all_gather(blocks, name):
    nb = len(blocks)

    def body(*refs):
        start, forward, finish = _gather_phases(refs[:nb], refs[nb:2 * nb], *refs[2 * nb:])
        start()
        forward()
        finish()

    return pl.pallas_call(
        body, name=name, out_shape=_gather_shapes(blocks), in_specs=[ANY] * nb, out_specs=[ANY] * nb,
        scratch_shapes=_gather_sems(nb),
    )(*blocks)


def _hosting(body, n_in, n_out, n_scratch, n_x, make_phases, grid):
    n_steps = math.prod(grid)

    def hosted(*refs):
        a = n_in + n_x
        b = a + n_out + n_x
        ins, xs = refs[:n_in], refs[n_in:a]
        outs, x_outs = refs[a:a + n_out], refs[a + n_out:b]
        scratch, sems = refs[b:b + n_scratch], refs[b + n_scratch:]
        step = 0
        for ax, n in enumerate(grid):
            step = step * n + pl.program_id(ax)
        phases = make_phases(xs, x_outs, *sems)
        pl.when(step == 0)(phases[0])
        for ph in phases[1:-1]:
            pl.when(step == n_steps // 2)(ph)
        body(*ins, *outs, *scratch)
        pl.when(step == n_steps - 1)(phases[-1])

    return hosted


def _gather_shapes(blocks):
    return [S((N_DEV,) + b.shape, b.dtype) for b in blocks]


def _gather_sems(nb):
    return [pltpu.SemaphoreType.DMA((7 * nb,)), pltpu.SemaphoreType.DMA((7 * nb,)), pltpu.SemaphoreType.DMA((nb,))]


def _gather_phases(x_refs, out_refs, send_sems, recv_sems, local_sems):
    nb = len(x_refs)
    x, y, c = _position()
    me, sibling = (x, y, c), (x, y, 1 - c)
    chips = [(1 - x, y), (x, 1 - y), (1 - x, 1 - y)]

    def copy(i, k, blk, to, own=False):
        px, py, pc = blk
        slot = out_refs[i].at[4 * px + 2 * py + pc]
        return pltpu.make_async_remote_copy(
            src_ref=x_refs[i] if own else slot, dst_ref=slot, send_sem=send_sems.at[7 * i + k],
            recv_sem=recv_sems.at[7 * i + k], device_id=to, device_id_type=MESH)

    def mine(i):
        return pltpu.make_async_copy(x_refs[i], out_refs[i].at[4 * x + 2 * y + c], local_sems.at[i])

    def first(i):
        return [copy(i, 0, me, sibling, own=True)] + [copy(i, 1 + j, me, (*chip, c), own=True)
                                                     for j, chip in enumerate(chips)]

    def passed(i, j):
        return copy(i, 4 + j, (*chips[j], c), sibling)

    def start():
        for i in range(nb):
            mine(i).start()
            for cp in first(i):
                cp.start()

    def forward():
        for i in range(nb):
            for j, chip in enumerate(chips):
                copy(i, 1 + j, (*chip, c), me).wait_recv()
                passed(i, j).start()

    def finish():
        for i in range(nb):
            copy(i, 0, sibling, me).wait_recv()
            for j, chip in enumerate(chips):
                copy(i, 4 + j, (*chip, 1 - c), me).wait_recv()
        for i in range(nb):
            for cp in first(i) + [passed(i, j) for j in range(3)]:
                cp.wait_send()
            mine(i).wait()

    return start, forward, finish


def _swap_with_sibling(pks, name):
    nb = len(pks)

    def body(*refs):
        start, finish = _sibling_swap_phases(refs[:nb], refs[nb:2 * nb], *refs[2 * nb:])
        start()
        finish()

    return pl.pallas_call(
        body, name=name, out_shape=_sibling_swap_shapes(pks), in_specs=[ANY] * nb, out_specs=[ANY] * nb,
        scratch_shapes=_sibling_swap_sems(nb),
    )(*pks)


def _sibling_swap_shapes(pks):
    return [S((4,) + p.shape[1:], p.dtype) for p in pks]


def _sibling_swap_sems(nb):
    return [pltpu.SemaphoreType.DMA((4 * nb,)), pltpu.SemaphoreType.DMA((4 * nb,))]


def _sibling_swap_phases(pk_refs, out_refs, send_sems, recv_sems):
    nb = len(pk_refs)
    x, y, c = _position()

    def copies():
        return [pltpu.make_async_remote_copy(
            src_ref=pk_refs[i].at[2 * k + 1 - c], dst_ref=out_refs[i].at[k], send_sem=send_sems.at[4 * i + k],
            recv_sem=recv_sems.at[4 * i + k], device_id=(x, y, 1 - c), device_id_type=MESH)
            for i in range(nb) for k in range(4)]

    def start():
        for cp in copies():
            cp.start()

    def finish():
        for cp in copies():
            cp.wait()

    return start, finish


def _swap_between_chips(pbs, name):
    nb = len(pbs)

    def body(*refs):
        start, finish = _chip_swap_phases(refs[:nb], refs[nb:2 * nb], *refs[2 * nb:])
        start()
        finish()

    return pl.pallas_call(
        body, name=name, out_shape=[S(p.shape, p.dtype) for p in pbs], in_specs=[ANY] * nb, out_specs=[ANY] * nb,
        scratch_shapes=_chip_swap_sems(nb),
    )(*pbs)


def _chip_swap_sems(nb):
    return [pltpu.SemaphoreType.DMA((3 * nb,)), pltpu.SemaphoreType.DMA((3 * nb,)), pltpu.SemaphoreType.DMA((nb,))]


def _chip_swap_phases(pb_refs, out_refs, send_sems, recv_sems, local_sems):
    nb = len(pb_refs)
    x, y, c = _position()
    me = 2 * x + y
    chips = [(1 - x, y), (x, 1 - y), (1 - x, 1 - y)]

    def local(i):
        return pltpu.make_async_copy(pb_refs[i].at[me], out_refs[i].at[me], local_sems.at[i])

    def send(i, j):
        cx, cy = chips[j]
        return pltpu.make_async_remote_copy(
            src_ref=pb_refs[i].at[2 * cx + cy], dst_ref=out_refs[i].at[me], send_sem=send_sems.at[3 * i + j],
            recv_sem=recv_sems.at[3 * i + j], device_id=(cx, cy, c), device_id_type=MESH)

    def arrival(i, j):
        cx, cy = chips[j]
        return pltpu.make_async_remote_copy(
            src_ref=pb_refs[i].at[me], dst_ref=out_refs[i].at[2 * cx + cy], send_sem=send_sems.at[3 * i + j],
            recv_sem=recv_sems.at[3 * i + j], device_id=(cx, cy, c), device_id_type=MESH)

    def start():
        for i in range(nb):
            local(i).start()
            for j in range(3):
                send(i, j).start()

    def finish():
        for i in range(nb):
            for j in range(3):
                arrival(i, j).wait_recv()
        for i in range(nb):
            for j in range(3):
                send(i, j).wait_send()
            local(i).wait()

    return start, finish


def _row_tile(r):
    return max(t for t in range(16, min(r, 1024) + 1, 16) if r % t == 0)


def _pair_sum_cast(pk, recv, core, name):
    _, r, l = pk.shape
    tr = _row_tile(r)

    def body(c_ref, a_ref, b_ref, o_ref):
        o_ref[...] = (a_ref[...] + b_ref[...]).astype(BF16)

    return pl.pallas_call(
        body, name=name,
        grid_spec=pltpu.PrefetchScalarGridSpec(
            num_scalar_prefetch=1, grid=(4, r // tr),
            in_specs=[pl.BlockSpec((None, tr, l), lambda k, i, c: (2 * k + c[0], i, 0)),
                      pl.BlockSpec((None, tr, l), lambda k, i, c: (k, i, 0))],
            out_specs=pl.BlockSpec((None, tr, l), lambda k, i, c: (k, i, 0))),
        out_shape=S((4, r, l), BF16), compiler_params=_cp(("parallel", "parallel")),
    )(core, pk, recv)


def _final_sum(pk, recv_sib, recv_chips, slot, chip, name):
    _, r, l = pk.shape
    tr = _row_tile(r)

    def body(s_ref, k_ref, a_ref, b_ref, rc_ref, o_ref):
        base = a_ref[...] + b_ref[...]
        acc = jnp.zeros_like(base)
        for j in range(4):
            acc = acc + jnp.where(k_ref[0] == j, base, rc_ref[j].astype(F32))
        o_ref[...] = acc

    return pl.pallas_call(
        body, name=name,
        grid_spec=pltpu.PrefetchScalarGridSpec(
            num_scalar_prefetch=2, grid=(r // tr,),
            in_specs=[pl.BlockSpec((None, tr, l), lambda i, s, k: (s[0], i, 0)),
                      pl.BlockSpec((None, tr, l), lambda i, s, k: (k[0], i, 0)),
                      pl.BlockSpec((4, tr, l), lambda i, s, k: (0, i, 0))],
            out_specs=pl.BlockSpec((tr, l), lambda i, s, k: (i, 0))),
        out_shape=S((r, l), F32), compiler_params=_cp(("parallel",)),
    )(slot, chip, pk, recv_sib, recv_chips)


def _adamw_math(w, g, m, v):
    m = ADAM_B1 * m + (1.0 - ADAM_B1) * g
    v = ADAM_B2 * v + (1.0 - ADAM_B2) * (g * g)
    m_hat = m / (1.0 - ADAM_B1 ** ADAM_STEP)
    v_hat = v / (1.0 - ADAM_B2 ** ADAM_STEP)
    return -ADAM_LR * (m_hat / (jnp.sqrt(v_hat) + ADAM_EPS) + ADAM_WD * w), m, v


def _adamw(w, g, m, v, name):
    r, c = w.shape
    tr = 512 if r % 512 == 0 else r

    def body(w_ref, g_ref, m_ref, v_ref, d_ref, nm_ref, nv_ref):
        d_ref[...], nm_ref[...], nv_ref[...] = _adamw_math(w_ref[...], g_ref[...], m_ref[...], v_ref[...])

    tile = pl.BlockSpec((tr, c), lambda i: (i, 0))
    return pl.pallas_call(
        body, name=name, grid=(r // tr,), in_specs=[tile] * 4, out_specs=[tile] * 3, out_shape=[S((r, c), F32)] * 3,
        compiler_params=_cp(("parallel",)),
    )(w, g, m, v)


def _small_update(gathered, w, m, v):
    def body(ga_ref, w_ref, m_ref, v_ref, g_ref, d_ref, nm_ref, nv_ref):
        g = ga_ref[0]
        for k in range(1, N_DEV):
            g = g + ga_ref[k]
        g_ref[...] = g
        d_ref[...], nm_ref[...], nv_ref[...] = _adamw_math(w_ref[...], g, m_ref[...], v_ref[...])

    return pl.pallas_call(body, name="small_update", out_shape=[S(w.shape, F32)] * 4, compiler_params=_cp())(
        gathered, w, m, v)


BIG = ("w_in", "mem_kv_w", "w_br_hgrn", "w_br_fox", "w_br_mem", "w_out", "ffn_w_up", "ffn_conv_w", "ffn_w_down")
GROUP_ROWS = ("w_out", "ffn_w_down")
GROUP_LANE = ("w_br_hgrn", "w_br_fox", "w_br_mem")
LANE_GROUP_ROWS = 224
SMALL = ("norm_mix_g", "norm_mem_g", "hgrn_lb_logits", "hgrn_norm_g", "fox_f_bias", "fox_q_norm_g", "fox_k_norm_g",
         "mem_q_norm_g", "mem_k_norm_g", "norm_ffn_g", "ffn_conv_b")


def _rows_of(n_elems):
    return -(-n_elems // LANE)


def _to_rows(a, lead=0):
    flat = a.reshape(a.shape[:lead] + (-1,))
    pad = (-flat.shape[-1]) % LANE
    if pad:
        flat = jnp.pad(flat, [(0, 0)] * lead + [(0, pad)])
    return flat.reshape(a.shape[:lead] + (-1, LANE))


def _stack_rows(parts, lead, total_rows):
    buf = jnp.concatenate(parts, axis=lead)
    pad = total_rows - buf.shape[lead]
    return jnp.pad(buf, [(0, 0)] * lead + [(0, pad), (0, 0)])


def _round_up(n, k):
    return -(-n // k) * k


def _from_rows(rows, shape, lead=0):
    n = math.prod(shape)
    return rows.reshape(rows.shape[:lead] + (-1,))[..., :n].reshape(rows.shape[:lead] + tuple(shape))


def _blocks_to_full(blocks, kind):
    n, a, b = blocks.shape
    return blocks.transpose(1, 0, 2).reshape(a, n * b) if kind == "col" else blocks.reshape(n * a, b)


def _full_to_blocks(full, kind, n=N_DEV):
    a, b = full.shape
    return full.reshape(a, n, b // n).transpose(1, 0, 2) if kind == "col" else full.reshape(n, a // n, b)


def _lane_group_rows(shard):
    n_lane = sum(shard[n].shape[0] for n in GROUP_LANE)
    n_cw = shard["ffn_conv_w"].size
    return n_lane, _rows_of(3 * n_cw), _rows_of(n_cw), _round_up(n_lane + _rows_of(3 * n_cw), LANE_GROUP_ROWS)


def _split_bf16x3(x):
    hi = x.astype(BF16)
    r1 = x - hi.astype(F32)
    mid = r1.astype(BF16)
    return jnp.stack([hi, mid, (r1 - mid.astype(F32)).astype(BF16)])


class _Exchange:
    def __init__(self, shard):
        self.shard = shard
        xi, yi, ci = _position()
        self.core = ci.astype(jnp.int32).reshape(1)
        self.chip = (2 * xi + yi).astype(jnp.int32).reshape(1)
        self.n_lane, self.r_pieces, self.r_vals, self.r_lane = _lane_group_rows(shard)

    def first_blocks(self):
        return [self.shard["w_in"].astype(BF16), self.shard["mem_kv_w"].astype(BF16)]

    def unpack_first(self, gathered):
        return {"w_in": _perm_from_blocks(gathered[0]), "mem_kv_w": _blocks_to_full(gathered[1], "row")}

    def late_blocks(self):
        sh = self.shard
        lane_rows = [sh[n].astype(BF16) for n in GROUP_LANE] + [_to_rows(_split_bf16x3(sh["ffn_conv_w"]))]
        return [sh[n].astype(BF16) for n in GROUP_ROWS] + [sh["ffn_w_up"].astype(BF16),
                                                           _stack_rows(lane_rows, 0, self.r_lane)]

    def unpack_late(self, gathered):
        *rows, gc, gd = gathered
        sh = self.shard
        W = {"ffn_w_up": _blocks_to_full(gc, "col")}
        for n, blocks in zip(GROUP_ROWS, rows):
            W[n] = _blocks_to_full(blocks, "row")
        r0 = 0
        for n in GROUP_LANE:
            W[n] = _blocks_to_full(gd[:, r0:r0 + sh[n].shape[0]], "col")
            r0 += sh[n].shape[0]
        cw = _from_rows(gd[:, self.n_lane:self.n_lane + self.r_pieces], (3,) + sh["ffn_conv_w"].shape, lead=1).astype(F32)
        W["ffn_conv_w"] = _blocks_to_full(cw[:, 0] + cw[:, 1] + cw[:, 2], "col")
        return W

    def early_grads(self, g):
        cw_rows = _to_rows(_full_to_blocks(g["ffn_conv_w"], "col"), lead=1)
        return [_full_to_blocks(g[n], "row") for n in GROUP_ROWS] + [
            jnp.concatenate([_full_to_blocks(h, "col", N_DEV // 2) for h in g["ffn_w_up"]], axis=0),
            _stack_rows([_full_to_blocks(g[n], "col") for n in GROUP_LANE] + [cw_rows], 1, self.r_lane)]

    def last_grads(self, g):
        return [_unperm_blocks(g["w_in"], N_DEV), _full_to_blocks(g["mem_kv_w"], "row")]

    def pair_sums(self, pks, recv_sib, tag):
        return [_pair_sum_cast(p, r, self.core, "rs_pair_sum_%s%d" % (tag, i))
                for i, (p, r) in enumerate(zip(pks, recv_sib))]

    def final_sums(self, pks, recv_sib, recv_chips, tag):
        return [_final_sum(p, rs, rc, 2 * self.chip + self.core, self.chip, "rs_final_sum_%s%d" % (tag, i))
                for i, (p, rs, rc) in enumerate(zip(pks, recv_sib, recv_chips))]

    def unpack_grads(self, early, last):
        sh = self.shard
        *rows, g_up, g_lane = early
        g_shard = {"w_in": last[0], "mem_kv_w": last[1], "ffn_w_up": g_up, **dict(zip(GROUP_ROWS, rows))}
        r0 = 0
        for n in GROUP_LANE:
            g_shard[n] = g_lane[r0:r0 + sh[n].shape[0]]
            r0 += sh[n].shape[0]
        g_shard["ffn_conv_w"] = _from_rows(g_lane[self.n_lane:self.n_lane + self.r_vals], sh["ffn_conv_w"].shape)
        return g_shard


def kernel(x, mem, norm_mix_g, norm_mem_g, w_in, hgrn_lb_logits, hgrn_norm_g, fox_f_bias, fox_q_norm_g, fox_k_norm_g, mem_kv_w, mem_q_norm_g, mem_k_norm_g, w_br_hgrn, w_br_fox, w_br_mem, w_out, norm_ffn_g, ffn_w_up, ffn_conv_w, ffn_conv_b, ffn_w_down, loss_target, m_norm_mix_g, m_norm_mem_g, m_w_in, m_hgrn_lb_logits, m_hgrn_norm_g, m_fox_f_bias, m_fox_q_norm_g, m_fox_k_norm_g, m_mem_kv_w, m_mem_q_norm_g, m_mem_k_norm_g, m_w_br_hgrn, m_w_br_fox, m_w_br_mem, m_w_out, m_norm_ffn_g, m_ffn_w_up, m_ffn_conv_w, m_ffn_conv_b, m_ffn_w_down, v_norm_mix_g, v_norm_mem_g, v_w_in, v_hgrn_lb_logits, v_hgrn_norm_g, v_fox_f_bias, v_fox_q_norm_g, v_fox_k_norm_g, v_mem_kv_w, v_mem_q_norm_g, v_mem_k_norm_g, v_w_br_hgrn, v_w_br_fox, v_w_br_mem, v_w_out, v_norm_ffn_g, v_ffn_w_up, v_ffn_conv_w, v_ffn_conv_b, v_ffn_w_down):
    given = dict(locals())
    order = ("norm_mix_g", "norm_mem_g", "w_in", "hgrn_lb_logits", "hgrn_norm_g", "fox_f_bias", "fox_q_norm_g",
             "fox_k_norm_g", "mem_kv_w", "mem_q_norm_g", "mem_k_norm_g", "w_br_hgrn", "w_br_fox", "w_br_mem", "w_out",
             "norm_ffn_g", "ffn_w_up", "ffn_conv_w", "ffn_conv_b", "ffn_w_down")
    B, T, D = x.shape
    M = mem.shape[1]
    shard = {n: given[n][0] if n in BIG else given[n] for n in order}
    mom = {n: (given["m_" + n][0], given["v_" + n][0]) if n in BIG else (given["m_" + n], given["v_" + n])
           for n in order}
    shard["hgrn_lb_logits"] = hgrn_lb_logits
    for n in ("norm_mix_g", "norm_mem_g", "hgrn_norm_g", "fox_f_bias", "fox_q_norm_g", "fox_k_norm_g", "mem_q_norm_g",
              "mem_k_norm_g", "norm_ffn_g", "ffn_conv_b"):
        shard[n] = given[n].reshape(1, -1)

    ex = _Exchange(shard)
    sm = {n: shard[n] for n in SMALL}
    grad_x, g, sums = _local_step(x.reshape(B * T, D), mem.reshape(B * M, D), loss_target.reshape(B * T, D), sm, None,
                                  B, T, M, ex)
    g_shard = ex.unpack_grads(*sums)

    sg = {n: g[n] for n in SMALL}
    sg["fox_f_bias"] = g["fox_f_bias"][:, :FOX_H]
    sg["fox_q_norm_g"] = g["fox_q_norm_g"][:, :FOX_D]
    sg["fox_k_norm_g"] = g["fox_k_norm_g"][:, :FOX_D]
    slayout, row0 = {}, 0
    for n in SMALL:
        nr = _rows_of(shard[n].size)
        slayout[n] = (row0, nr)
        row0 += nr
    loss_row = row0
    r_small = _round_up(row0 + 1, 8)

    def pack_small(d, with_loss=None):
        rows = [_to_rows(d[n]) for n in SMALL]
        rows.append(with_loss if with_loss is not None else jnp.zeros((1, LANE), F32))
        return _stack_rows(rows, 0, r_small)

    sgath, = _all_gather([pack_small(sg, g["loss"])], "ag_small")
    s_g, s_d, s_m, s_v = _small_update(sgath, pack_small(shard), pack_small({n: mom[n][0].reshape(shard[n].shape) for n in SMALL}),
                                       pack_small({n: mom[n][1].reshape(shard[n].shape) for n in SMALL}))
    loss = s_g[loss_row, 0]

    grads, deltas, new_m, new_v = {}, {}, {}, {}
    for n in BIG:
        gn = g_shard[n]
        d, nm, nv = _adamw(shard[n], gn, mom[n][0], mom[n][1], "adamw_" + n)
        grads[n], deltas[n], new_m[n], new_v[n] = (a[None] for a in (gn, d, nm, nv))
    for n in SMALL:
        r0, nr = slayout[n]
        for dst, src in ((grads, s_g), (deltas, s_d), (new_m, s_m), (new_v, s_v)):
            dst[n] = _from_rows(src[r0:r0 + nr], given[n].shape)
    return (loss, grad_x.reshape(B, T, D), *[grads[n] for n in order], *[deltas[n] for n in order],
            *[new_m[n] for n in order], *[new_v[n] for n in order])
```

```python
import functools
import math

import jax
import jax.numpy as jnp
from jax import lax
from jax.experimental import pallas as pl
from jax.experimental.pallas import tpu as pltpu

F32, BF16 = jnp.float32, jnp.bfloat16
S = jax.ShapeDtypeStruct
MESH = pl.DeviceIdType.MESH

N_DEV = 8
EPS = 1e-6
LANE = 128
CHUNK = 64
SUB = 16
HG_H, HG_D = 4, 128
HG_GROUP_FWD = 4
HG_GROUP = 2
FOX_H, FOX_D = 8, 64
FOX_P = FOX_H // 2
MEM_H, MEM_D = 4, 128
NEG = -1e30
VMEM_LIMIT = 56 * 2**20

ADAM_LR, ADAM_B1, ADAM_B2, ADAM_EPS, ADAM_WD, ADAM_STEP = 0.001, 0.9, 0.999, 1e-08, 0.01, 10

C_FOX, C_MQ, C_HG, C_GATE, C_FF, C_END = 0, 1536, 2048, 4096, 7168, 7296


def _cp(sem=None):
    return pltpu.CompilerParams(dimension_semantics=sem, vmem_limit_bytes=VMEM_LIMIT)


def _dot(a, b, dims, prec=None):
    return lax.dot_general(a, b, (dims, ((), ())), preferred_element_type=F32, precision=prec)


def _nn(a, b, prec=None):
    return _dot(a, b, ((1,), (0,)), prec)


def _nt(a, b, prec=None):
    return _dot(a, b, ((1,), (1,)), prec)


def _tn(a, b, prec=None):
    return _dot(a, b, ((0,), (0,)), prec)


def _b(x):
    return x.astype(BF16)


def _mm3(fn, a, b):
    ah, bh = _b(a), _b(b)
    return fn(ah, bh) + fn(ah, _b(b - bh.astype(F32))) + fn(_b(a - ah.astype(F32)), bh)


def _iota(shape, dim):
    return lax.broadcasted_iota(jnp.int32, shape, dim)


def _rowsum8(x):
    r, d = x.shape
    return jnp.sum(x.reshape(r // 8, 8, d), axis=0)


def _rmsnorm_cast(x, g, name, tm=1024, gather=()):
    n, d = x.shape
    nga = len(gather)

    def body(x_ref, g_ref, o_ref):
        v = x_ref[...]
        r = lax.rsqrt(jnp.mean(v * v, axis=-1, keepdims=True) + EPS)
        o_ref[...] = (v * r * g_ref[...]).astype(BF16)

    if nga:
        body = _hosting(body, 2, 1, 0, nga, _gather_phases, (n // tm,))
    out = pl.pallas_call(
        body, name=name, grid=(n // tm,),
        in_specs=[pl.BlockSpec((tm, d), lambda i: (i, 0)), pl.BlockSpec((1, d), lambda i: (0, 0))] + [ANY] * nga,
        out_specs=[pl.BlockSpec((tm, d), lambda i: (i, 0))] + [ANY] * nga,
        out_shape=[S((n, d), BF16)] + _gather_shapes(gather), scratch_shapes=_gather_sems(nga) if nga else [],
        compiler_params=_cp(("arbitrary",) if nga else ("parallel",)),
    )(x, g, *gather)
    return out if nga else out[0]


def _rmsnorm_bwd(dh, x, g, resid, name, tm=1024):
    n, d = x.shape
    has_res = resid is not None

    def body(*refs):
        if has_res:
            dh_ref, x_ref, g_ref, r_ref, dx_ref, dg_ref = refs
        else:
            dh_ref, x_ref, g_ref, dx_ref, dg_ref = refs
        v = x_ref[...]
        dhv = dh_ref[...].astype(F32)
        r = lax.rsqrt(jnp.mean(v * v, axis=-1, keepdims=True) + EPS)
        xh = v * r
        u = dhv * g_ref[...]
        dx = r * (u - xh * jnp.mean(u * xh, axis=-1, keepdims=True))
        if has_res:
            dx = dx + r_ref[...]
        dx_ref[...] = dx

        @pl.when(pl.program_id(0) == 0)
        def _():
            dg_ref[...] = jnp.zeros_like(dg_ref)

        dg_ref[...] += _rowsum8(dhv * xh)

    tile = pl.BlockSpec((tm, d), lambda i: (i, 0))
    ins = [tile, tile, pl.BlockSpec((1, d), lambda i: (0, 0))] + ([tile] if has_res else [])
    args = (dh, x, g) + ((resid,) if has_res else ())
    return pl.pallas_call(
        body, name=name, grid=(n // tm,), in_specs=ins,
        out_specs=[tile, pl.BlockSpec((8, d), lambda i: (0, 0))],
        out_shape=[S((n, d), F32), S((8, d), F32)], compiler_params=_cp(("arbitrary",)),
    )(*args)


def _mm_nn(a, b, out_dtype, name, tm, tn):
    m, k = a.shape
    n = b.shape[1]
    assert n % tn == 0 and m % tm == 0

    def body(a_ref, b_ref, o_ref):
        o_ref[...] = _nn(a_ref[...].astype(BF16), b_ref[...].astype(BF16)).astype(out_dtype)

    return pl.pallas_call(
        body, name=name, grid=(n // tn, m // tm),
        in_specs=[pl.BlockSpec((tm, k), lambda j, i: (i, 0)), pl.BlockSpec((k, tn), lambda j, i: (0, j))],
        out_specs=pl.BlockSpec((tm, tn), lambda j, i: (i, j)), out_shape=S((m, n), out_dtype),
        compiler_params=_cp(("parallel", "parallel")),
    )(a, b)


def _mm_nt_sum(parts, w, name, tm, swap=()):
    m = parts[0][0].shape[0]
    k = w.shape[0]
    assert m % tm == 0 and all(c % n == 0 and o % n == 0 for _, c, n, o in parts)
    np_ = len(parts)
    nsw = len(swap)
    n_steps = m // tm

    def body(*refs):
        o_ref = refs[2 * np_ + nsw]
        if nsw:
            start, finish = _chip_swap_phases(refs[2 * np_:2 * np_ + nsw], refs[2 * np_ + nsw + 1:2 * np_ + 2 * nsw + 1],
                                              *refs[2 * np_ + 2 * nsw + 1:])
            pl.when(pl.program_id(0) == 0)(start)
        acc = _nt(refs[0][...].astype(BF16), refs[np_][...].astype(BF16))
        for i in range(1, np_):
            acc = acc + _nt(refs[i][...].astype(BF16), refs[np_ + i][...].astype(BF16))
        o_ref[...] = acc
        if nsw:
            pl.when(pl.program_id(0) == n_steps - 1)(finish)

    dy_specs = [pl.BlockSpec((tm, n), functools.partial(lambda i, j: (i, j), j=c // n)) for _, c, n, _ in parts]
    w_specs = [pl.BlockSpec((k, n), functools.partial(lambda i, j: (0, j), j=o // n)) for _, _, n, o in parts]
    out = pl.pallas_call(
        body, name=name, grid=(n_steps,), in_specs=dy_specs + w_specs + [ANY] * nsw,
        out_specs=[pl.BlockSpec((tm, k), lambda i: (i, 0))] + [ANY] * nsw,
        out_shape=[S((m, k), F32)] + [S(p.shape, p.dtype) for p in swap],
        scratch_shapes=_chip_swap_sems(nsw) if nsw else [],
        compiler_params=_cp(("arbitrary",) if nsw else ("parallel",)),
    )(*([p[0] for p in parts] + [w] * np_ + list(swap)))
    return (out[0], out[1:]) if nsw else out[0]


def _mm_tn(x, dy, name, tm, tn):
    m, k = x.shape
    n = dy.shape[1]
    tm = min(tm, m)
    assert m % tm == 0 and n % tn == 0

    def body(x_ref, dy_ref, o_ref):
        part = _tn(x_ref[...].astype(BF16), dy_ref[...].astype(BF16))

        @pl.when(pl.program_id(1) == 0)
        def _():
            o_ref[...] = part

        @pl.when(pl.program_id(1) > 0)
        def _():
            o_ref[...] += part

    return pl.pallas_call(
        body, name=name, grid=(n // tn, m // tm),
        in_specs=[pl.BlockSpec((tm, k), lambda j, i: (i, 0)), pl.BlockSpec((tm, tn), lambda j, i: (i, j))],
        out_specs=pl.BlockSpec((k, tn), lambda j, i: (0, j)), out_shape=S((k, n), F32),
        compiler_params=_cp(("parallel", "arbitrary")),
    )(x, dy)


def _lower_bound(logits):
    e = jnp.exp(logits - jnp.max(logits, axis=0, keepdims=True))
    return e[0:1, :] / jnp.sum(e, axis=0, keepdims=True)


def _hg_gates(fl, lb):
    sig = jax.nn.sigmoid(fl)
    f = lb + (1.0 - lb) * sig
    k = (1.0 - lb) * (1.0 - sig)
    return sig, f, k, jnp.log(f)


def _silu_and_grad(x):
    s = jax.nn.sigmoid(x)
    return x * s, s * (1.0 + x * (1.0 - s))


def _hg_rowblocks(G):
    return [None] + [G[SUB * i - 1:SUB * i, :] for i in range(1, CHUNK // SUB)]


def _hg_intra_A(qs, k, G):
    refs = _hg_rowblocks(G)
    cols = _iota((SUB, LANE), 1)
    rows = _iota((SUB, LANE), 0)
    no_keys = jnp.zeros((LANE - CHUNK, HG_D), BF16)
    blocks = []
    for i in range(CHUNK // SUB):
        lo = SUB * i
        qb, Gb = qs[lo:lo + SUB, :], G[lo:lo + SUB, :]
        diag = jnp.zeros((SUB, LANE), F32)
        for s in range(SUB):
            e = jnp.exp(jnp.minimum(Gb - G[lo + s:lo + s + 1, :], 0.0))
            col = jnp.sum(qb * k[lo + s:lo + s + 1, :] * e, axis=-1, keepdims=True)
            diag = jnp.where(cols == lo + s, col, diag)
        a = jnp.where((cols >= lo) & (cols <= rows + lo), diag, 0.0)
        if i > 0:
            qr = qb * jnp.exp(Gb - refs[i])
            kr = k * jnp.exp(jnp.minimum(refs[i] - G, 0.0))
            a = jnp.where(cols < lo, _nt(_b(qr), jnp.concatenate([_b(kr), no_keys], axis=0)), a)
        blocks.append(a)
    return jnp.concatenate(blocks, axis=0)


def _hg_intra_bwd(dA, qs, k, G):
    refs = _hg_rowblocks(G)
    cols = _iota((SUB, CHUNK), 1)
    rows16 = _iota((SUB, HG_D), 0)
    dk = jnp.zeros((CHUNK, HG_D), F32)
    dq_blocks, dk_diag_blocks = [], []
    for i in range(CHUNK // SUB):
        lo = SUB * i
        qb, Gb = qs[lo:lo + SUB, :], G[lo:lo + SUB, :]
        dAb = dA[lo:lo + SUB, :]
        dq = jnp.zeros((SUB, HG_D), F32)
        dkb = jnp.zeros((SUB, HG_D), F32)
        for s in range(SUB):
            e = jnp.exp(jnp.minimum(Gb - G[lo + s:lo + s + 1, :], 0.0))
            e = jnp.where(rows16 >= s, e, 0.0)
            dcol = jnp.sum(jnp.where(cols == lo + s, dAb, 0.0), axis=-1, keepdims=True)
            w = dcol * e
            dq = dq + w * k[lo + s:lo + s + 1, :]
            dkb = jnp.where(rows16 == s, jnp.sum(w * qb, axis=0, keepdims=True), dkb)
        if i > 0:
            e1 = jnp.exp(Gb - refs[i])
            e2 = jnp.exp(jnp.minimum(refs[i] - G, 0.0))
            dA_off = jnp.where(cols < lo, dAb, 0.0)
            dq = dq + _mm3(_nn, dA_off, k * e2) * e1
            dk = dk + _mm3(_tn, dA_off, qb * e1) * e2
        dq_blocks.append(dq)
        dk_diag_blocks.append(dkb)
    return jnp.concatenate(dq_blocks, axis=0), dk + jnp.concatenate(dk_diag_blocks, axis=0)


def _tri(n, upper=False):
    r, c = _iota((n, n), 0), _iota((n, n), 1)
    return jnp.where((c >= r) if upper else (r >= c), 1.0, 0.0).astype(BF16)


def _prefix_mm(tri, x):
    hi = x.astype(BF16)
    r1 = x - hi.astype(F32)
    mid = r1.astype(BF16)
    lo = (r1 - mid.astype(F32)).astype(BF16)
    return _nn(tri, hi) + _nn(tri, mid) + _nn(tri, lo)


def _hgrn_fwd(z, lb, gn, B, T):
    N = B * T
    NC = T // CHUNK
    ng = HG_H // HG_GROUP_FWD

    def body(z_ref, lb_ref, gn_ref, y_ref, o_ref, st_ref, a_ref, s_scr):
        lbs = _lower_bound(lb_ref[...])
        tri = _tri(CHUNK)
        s_scr[...] = jnp.zeros_like(s_scr)

        def chunk(c, carry):
            r = pl.ds(pl.multiple_of(c * CHUNK, CHUNK), CHUNK)
            for hh in range(HG_GROUP_FWD):
                zc, oc = 4 * LANE * hh, LANE * hh
                ql, fl, il, gl = (z_ref[r, zc + LANE * j:zc + LANE * (j + 1)].astype(F32) for j in range(4))
                _, _, k, logf = _hg_gates(fl, lbs[:, oc:oc + LANE])
                G = _prefix_mm(tri, logf)
                qs = ql * jax.nn.sigmoid(ql)
                st = s_scr[hh]
                st_ref[hh * NC + c] = st
                g_last = G[CHUNK - 1:CHUNK, :]
                A = _b(_hg_intra_A(qs, k, G))
                a_ref[r, oc:oc + LANE] = A
                o = _nn(A[:, 0:CHUNK], _b(il)) + _nt(_b(qs * jnp.exp(G)), _b(st))
                s_scr[hh] = st * jnp.exp(g_last) + _mm3(_tn, il, k * jnp.exp(g_last - G))
                o_ref[r, oc:oc + LANE] = o
                rstd = lax.rsqrt(jnp.mean(o * o, axis=-1, keepdims=True) + EPS)
                y_ref[r, oc:oc + LANE] = (o * rstd * gn_ref[...] * (gl * jax.nn.sigmoid(gl))).astype(BF16)
            return carry

        lax.fori_loop(0, NC, chunk, 0, unroll=4)

    gw = HG_GROUP_FWD * LANE
    cb = C_HG // (4 * gw)
    return pl.pallas_call(
        body, name="hgrn_fwd", grid=(B, ng),
        in_specs=[pl.BlockSpec((T, 4 * gw), lambda b, h: (b, cb + h)), pl.BlockSpec((lb.shape[0], gw), lambda b, h: (0, h)),
                  pl.BlockSpec((1, LANE), lambda b, h: (0, 0))],
        out_specs=[pl.BlockSpec((T, gw), lambda b, h: (b, h)), pl.BlockSpec((T, gw), lambda b, h: (b, h)),
                   pl.BlockSpec((HG_GROUP_FWD * NC, HG_D, HG_D), lambda b, h: (b * ng + h, 0, 0)),
                   pl.BlockSpec((T, gw), lambda b, h: (b, h))],
        out_shape=[S((N, 512), BF16), S((N, 512), F32), S((B * HG_H * NC, HG_D, HG_D), F32), S((N, 512), BF16)],
        scratch_shapes=[pltpu.VMEM((HG_GROUP_FWD, HG_D, HG_D), F32)], compiler_params=_cp(("parallel", "parallel")),
    )(z, lb, gn)


def _hgrn_bwd(z, o_raw, states, a_mat, dy, lb, gn, B, T, swap_sibling=()):
    N = B * T
    NC = T // CHUNK
    ng = HG_H // HG_GROUP
    nsw = len(swap_sibling)

    def body(z_ref, o_ref, st_ref, a_ref, dy_ref, lb_ref, gn_ref, dz_ref, dlb_ref, dgn_ref, ds_scr, racc, dgn_acc):
        lbs = _lower_bound(lb_ref[...])
        gn_v = gn_ref[...]
        tri, triu = _tri(CHUNK), _tri(CHUNK, upper=True)
        cmask = _iota((CHUNK, CHUNK), 0) >= _iota((CHUNK, CHUNK), 1)
        for ref in (ds_scr, racc, dgn_acc, dlb_ref):
            ref[...] = jnp.zeros_like(ref)

        def chunk(ci, carry):
            c = NC - 1 - ci
            r = pl.ds(pl.multiple_of(c * CHUNK, CHUNK), CHUNK)
            for hh in range(HG_GROUP):
                zc, oc = 4 * LANE * hh, LANE * hh
                lb_v = lbs[:, oc:oc + LANE]
                ql, fl, il, gl = (z_ref[r, zc + LANE * j:zc + LANE * (j + 1)].astype(F32) for j in range(4))
                sig, f, k, logf = _hg_gates(fl, lb_v)
                G = _prefix_mm(tri, logf)
                qs, dsilu_q = _silu_and_grad(ql)
                gs, dsilu_g = _silu_and_grad(gl)
                o = o_ref[r, oc:oc + LANE]
                dyv = dy_ref[r, oc:oc + LANE]
                rstd = lax.rsqrt(jnp.mean(o * o, axis=-1, keepdims=True) + EPS)
                oh = o * rstd
                dgl = dyv * oh * gn_v * dsilu_g
                dn = dyv * gs
                dgn_acc[...] += _rowsum8(dn * oh)
                u = dn * gn_v
                do = rstd * (u - oh * jnp.mean(u * oh, axis=-1, keepdims=True))
                st = st_ref[hh * NC + c]
                dst = ds_scr[hh]
                eG = jnp.exp(G)
                g_last = G[CHUNK - 1:CHUNK, :]
                eL = jnp.exp(g_last - G)
                dA = jnp.where(cmask, _mm3(_nt, do, il), 0.0)
                dq_in, dk_in = _hg_intra_bwd(dA, qs, k, G)
                di = _tn(a_ref[r, oc:oc + LANE][:, 0:CHUNK], _b(do)) + _nt(_b(k * eL), _b(dst))
                dq = dq_in + _mm3(_nn, do, st) * eG
                dk = dk_in + _mm3(_nn, il, dst) * eL
                ds_scr[hh] = dst * jnp.exp(g_last) + _mm3(_tn, do, qs * eG)
                dd = qs * dq - k * dk
                dlogf = _prefix_mm(triu, dd) + racc[hh]
                racc[hh] += jnp.sum(dd, axis=0, keepdims=True)
                df = dlogf / f - dk
                dlb_ref[8 * hh:8 * (hh + 1), :] += _rowsum8(df * (1.0 - sig))
                dz_ref[r, zc:zc + LANE] = (dq * dsilu_q).astype(BF16)
                dz_ref[r, zc + LANE:zc + 2 * LANE] = (df * (1.0 - lb_v) * sig * (1.0 - sig)).astype(BF16)
                dz_ref[r, zc + 2 * LANE:zc + 3 * LANE] = di.astype(BF16)
                dz_ref[r, zc + 3 * LANE:zc + 4 * LANE] = dgl.astype(BF16)
            return carry

        lax.fori_loop(0, NC, chunk, 0, unroll=4)
        dgn_ref[...] = dgn_acc[...]

    gw = HG_GROUP * LANE
    cb = C_HG // (4 * gw)
    col = pl.BlockSpec((T, gw), lambda b, h: (b, h))
    if nsw:
        body = _hosting(body, 7, 3, 3, nsw, _sibling_swap_phases, (B, ng))
    return pl.pallas_call(
        body, name="hgrn_bwd", grid=(B, ng),
        in_specs=[pl.BlockSpec((T, 4 * gw), lambda b, h: (b, cb + h)), col,
                  pl.BlockSpec((HG_GROUP * NC, HG_D, HG_D), lambda b, h: (b * ng + h, 0, 0)), col, col,
                  pl.BlockSpec((lb.shape[0], gw), lambda b, h: (0, h)), pl.BlockSpec((1, LANE), lambda b, h: (0, 0))]
        + [ANY] * nsw,
        out_specs=[pl.BlockSpec((T, 4 * gw), lambda b, h: (b, h)),
                   pl.BlockSpec((8 * HG_GROUP, LANE), lambda b, h: (b * ng + h, 0)),
                   pl.BlockSpec((8, LANE), lambda b, h: (b * ng + h, 0))] + [ANY] * nsw,
        out_shape=[S((N, 2048), BF16), S((B * HG_H * 8, LANE), F32), S((B * ng * 8, LANE), F32)]
        + _sibling_swap_shapes(swap_sibling),
        scratch_shapes=[pltpu.VMEM((HG_GROUP, HG_D, HG_D), F32), pltpu.VMEM((HG_GROUP, 1, LANE), F32),
                        pltpu.VMEM((8, LANE), F32)] + (_sibling_swap_sems(nsw) if nsw else []),
        compiler_params=_cp(("arbitrary", "arbitrary") if nsw else ("parallel", "parallel")),
    )(z, o_raw, states, a_mat, dy, lb, gn, *swap_sibling)


def _pair_mean(x, lo_half):
    a = jnp.sum(jnp.where(lo_half, x, 0.0), axis=-1, keepdims=True)
    b = jnp.sum(jnp.where(lo_half, 0.0, x), axis=-1, keepdims=True)
    return jnp.where(lo_half, a, b) * (1.0 / FOX_D)


def _fox_gate_fwd(z, bias, B, T):
    N = B * T
    tb = LANE

    def body(z_ref, b_ref, fc_ref, fct_ref):
        tri = _tri(tb)

        def step(i, carry):
            r = pl.ds(pl.multiple_of(i * tb, tb), tb)
            cs = _prefix_mm(tri, jax.nn.log_sigmoid(z_ref[r, :].astype(F32) + b_ref[...])) + carry
            fc_ref[r, :] = cs
            fct_ref[0, :, r] = cs.T[0:8, :]
            return cs[tb - 1:tb, :]

        lax.fori_loop(0, T // tb, step, jnp.zeros((1, LANE), F32))

    return pl.pallas_call(
        body, name="fox_gate_fwd", grid=(B,),
        in_specs=[pl.BlockSpec((T, LANE), lambda b: (b, C_FF // LANE)), pl.BlockSpec((1, LANE), lambda b: (0, 0))],
        out_specs=[pl.BlockSpec((T, LANE), lambda b: (b, 0)), pl.BlockSpec((1, 8, T), lambda b: (b, 0, 0))],
        out_shape=[S((N, LANE), F32), S((B, 8, T), F32)], compiler_params=_cp(("parallel",)),
    )(z, bias)


def _fox_gate_bwd(dfc, z, bias, B, T):
    N = B * T
    tb = LANE
    nt = T // tb

    def body(d_ref, z_ref, b_ref, dz_ref, db_ref):
        triu = _tri(tb, upper=True)
        db_ref[...] = jnp.zeros_like(db_ref)

        def step(ii, carry):
            r = pl.ds(pl.multiple_of((nt - 1 - ii) * tb, tb), tb)
            d = d_ref[r, 0:LANE]
            for p in range(1, FOX_P):
                d = d + d_ref[r, LANE * p:LANE * (p + 1)]
            rc = _prefix_mm(triu, d) + carry
            dff = rc * jax.nn.sigmoid(-(z_ref[r, :].astype(F32) + b_ref[...]))
            dz_ref[r, :] = dff.astype(BF16)
            db_ref[...] += _rowsum8(dff)
            return carry + jnp.sum(d, axis=0, keepdims=True)

        lax.fori_loop(0, nt, step, jnp.zeros((1, LANE), F32))

    return pl.pallas_call(
        body, name="fox_gate_bwd", grid=(B,),
        in_specs=[pl.BlockSpec((T, 512), lambda b: (b, 0)), pl.BlockSpec((T, LANE), lambda b: (b, C_FF // LANE)),
                  pl.BlockSpec((1, LANE), lambda b: (0, 0))],
        out_specs=[pl.BlockSpec((T, LANE), lambda b: (b, 0)), pl.BlockSpec((8, LANE), lambda b: (b, 0))],
        out_shape=[S((N, LANE), BF16), S((B * 8, LANE), F32)], compiler_params=_cp(("parallel",)),
    )(dfc, z, bias)


def _fox_prep(z_ref, gq, gk, r, lo_half):
    q, k, v = (z_ref[r, LANE * j:LANE * (j + 1)].astype(F32) for j in range(3))
    rq = lax.rsqrt(_pair_mean(q * q, lo_half) + EPS)
    rk = lax.rsqrt(_pair_mean(k * k, lo_half) + EPS)
    qh, kh = q * rq, k * rk
    return qh * gq * (FOX_D ** -0.5), kh * gk, v, qh, kh, rq, rk


def _fox_fwd(z, fc, fct, gq, gk, B, T, tq=512, gather=()):
    N = B * T
    NQ = T // tq
    nga = len(gather)

    def body(z_ref, fc_ref, fct_ref, gq_ref, gk_ref, y_ref, lse_ref, qn_s, kn_s, v_s):
        p, qi = pl.program_id(1), pl.program_id(2)
        lo_half = _iota((1, LANE), 1) < FOX_D

        @pl.when(qi == 0)
        def _():
            def prep(i, carry):
                r = pl.ds(pl.multiple_of(i * tq, tq), tq)
                qn, kn, v = _fox_prep(z_ref, gq_ref[...], gk_ref[...], r, lo_half)[:3]
                qn_s[r, :], kn_s[r, :], v_s[r, :] = qn.astype(BF16), kn.astype(BF16), v.astype(BF16)
                return carry
            lax.fori_loop(0, NQ, prep, 0)

        rq = pl.ds(pl.multiple_of(qi * tq, tq), tq)
        qn = qn_s[rq, :]
        fcq = fc_ref[rq, :]
        lane = _iota((tq, LANE), 1)
        causal = _iota((tq, tq), 0) >= _iota((tq, tq), 1)
        qhs = [jnp.where(lo_half, qn, jnp.zeros_like(qn)), jnp.where(lo_half, jnp.zeros_like(qn), qn)]
        fqs = [jnp.sum(jnp.where(lane == 2 * p + hh, fcq, 0.0), axis=-1, keepdims=True) for hh in range(2)]

        def kv(j, carry, diagonal):
            rk = pl.ds(pl.multiple_of(j * tq, tq), tq)
            kj, vj = kn_s[rk, :], v_s[rk, :]
            one = jnp.ones_like(vj)
            new = []
            for hh in range(2):
                m, acc = carry[hh]
                s = _nt(qhs[hh], kj) + fqs[hh] - fct_ref[0, pl.ds(2 * p + hh, 1), rk]
                if diagonal:
                    s = jnp.where(causal, s, NEG)
                m_new = jnp.maximum(m, jnp.max(s, axis=-1, keepdims=True))
                pe = jnp.exp(s - m_new)
                v_aug = jnp.where(lo_half if hh == 0 else jnp.logical_not(lo_half), vj, one)
                new.append((m_new, jnp.exp(m - m_new) * acc + _nn(pe.astype(BF16), v_aug)))
            return tuple(new)

        init = tuple((jnp.full((tq, 1), NEG, F32), jnp.zeros((tq, LANE), F32)) for _ in range(2))
        carry = lax.fori_loop(0, qi, functools.partial(kv, diagonal=False), init)
        (m0, a0), (m1, a1) = kv(qi, carry, True)
        l0, l1 = a0[:, FOX_D:FOX_D + 1], a1[:, 0:1]
        y_ref[...] = jnp.where(lo_half, a0 / l0, a1 / l1).astype(BF16)
        lse_ref[...] = jnp.where(lo_half, m0 + jnp.log(l0), m1 + jnp.log(l1))

    vec = pl.BlockSpec((1, LANE), lambda b, p, q: (0, 0))
    tile = pl.BlockSpec((tq, LANE), lambda b, p, q: (b * NQ + q, p))
    if nga:
        body = _hosting(body, 5, 2, 3, nga, _gather_phases, (B, FOX_P, NQ))
    return pl.pallas_call(
        body, name="fox_fwd", grid=(B, FOX_P, NQ),
        in_specs=[pl.BlockSpec((T, 384), lambda b, p, q: (b, p)), pl.BlockSpec((T, LANE), lambda b, p, q: (b, 0)),
                  pl.BlockSpec((1, 8, T), lambda b, p, q: (b, 0, 0)), vec, vec] + [ANY] * nga,
        out_specs=[tile, tile] + [ANY] * nga, out_shape=[S((N, 512), BF16), S((N, 512), F32)] + _gather_shapes(gather),
        scratch_shapes=[pltpu.VMEM((T, LANE), BF16)] * 3 + (_gather_sems(nga) if nga else []),
        compiler_params=_cp(("arbitrary",) * 3 if nga else ("parallel", "parallel", "arbitrary")),
    )(z, fc, fct, gq, gk, *gather)


def _fox_bwd(z, dy, y, lse, fc, fct, gq, gk, B, T, tq=512, swap=()):
    N = B * T
    NQ = T // tq
    nsw = len(swap)

    def body(z_ref, dy_ref, y_ref, lse_ref, fc_ref, fct_ref, gq_ref, gk_ref, dz_ref, dfc_ref, dgq_ref, dgk_ref,
             qn_s, kn_s, v_s, do_s, delta_s, dq_s, dfk_s):
        p, kj = pl.program_id(1), pl.program_id(2)
        lo_half = _iota((1, LANE), 1) < FOX_D
        lane = _iota((tq, LANE), 1)
        gq_v, gk_v = gq_ref[...], gk_ref[...]

        @pl.when(kj == 0)
        def _():
            def prep(i, carry):
                r = pl.ds(pl.multiple_of(i * tq, tq), tq)
                qn, kn, v = _fox_prep(z_ref, gq_v, gk_v, r, lo_half)[:3]
                qn_s[r, :], kn_s[r, :], v_s[r, :] = qn.astype(BF16), kn.astype(BF16), v.astype(BF16)
                do = dy_ref[r, :]
                do_s[r, :] = do.astype(BF16)
                delta_s[r, :] = _pair_mean(do * y_ref[r, :].astype(F32), lo_half) * float(FOX_D)
                return carry
            lax.fori_loop(0, NQ, prep, 0)
            dq_s[...] = jnp.zeros_like(dq_s)
            dgq_ref[...] = jnp.zeros_like(dgq_ref)
            dgk_ref[...] = jnp.zeros_like(dgk_ref)

        rk = pl.ds(pl.multiple_of(kj * tq, tq), tq)
        kn, vv = kn_s[rk, :], v_s[rk, :]
        causal = _iota((tq, tq), 0) >= _iota((tq, tq), 1)
        zero, one = jnp.zeros_like(kn), jnp.ones_like(kn)
        hms = [lo_half, jnp.logical_not(lo_half)]
        kmasks = [jnp.where(hm, kn, zero) for hm in hms]
        kaugs = [jnp.where(hm, kn, one) for hm in hms]
        vmasks = [jnp.where(hm, vv, zero) for hm in hms]
        fks = [fct_ref[0, pl.ds(2 * p + hh, 1), rk] for hh in range(2)]

        def qloop(i, carry, diagonal):
            ri = pl.ds(pl.multiple_of(i * tq, tq), tq)
            qn = qn_s[ri, :]
            do = do_s[ri, :]
            fcq = fc_ref[ri, :]
            new = []
            for hh in range(2):
                dk_acc, dv_acc = carry[hh]
                c0 = FOX_D * hh
                fq = jnp.sum(jnp.where(lane == 2 * p + hh, fcq, 0.0), axis=-1, keepdims=True)
                pr = jnp.exp(_nt(qn, kmasks[hh]) + fq - fks[hh] - lse_ref[ri, c0:c0 + 1])
                if diagonal:
                    pr = jnp.where(causal, pr, 0.0)
                ds = (pr * (_nt(do, vmasks[hh]) - delta_s[ri, c0:c0 + 1])).astype(BF16)
                dq_s[hh, ri, :] += _nn(ds, kaugs[hh])
                new.append((dk_acc + _tn(jnp.where(hms[hh], qn, one), ds), dv_acc + _tn(do, pr.astype(BF16))))
            return tuple(new)

        init = tuple((jnp.zeros((LANE, tq), F32), jnp.zeros((LANE, tq), F32)) for _ in range(2))
        carry = qloop(kj, init, True)
        (dk0, dv0), (dk1, dv1) = lax.fori_loop(kj + 1, NQ, functools.partial(qloop, diagonal=False), carry)
        dks, dvs = [dk0.T, dk1.T], [dv0.T, dv1.T]

        dkn = jnp.where(lo_half, dks[0], dks[1])
        _, _, _, _, kh, _, rkk = _fox_prep(z_ref, gq_v, gk_v, rk, lo_half)
        u = dkn * gk_v
        dz_ref[rk, LANE:2 * LANE] = (rkk * (u - kh * _pair_mean(u * kh, lo_half))).astype(BF16)
        dz_ref[rk, 2 * LANE:3 * LANE] = jnp.where(lo_half, dvs[0], dvs[1]).astype(BF16)
        dgk_ref[...] += _rowsum8(dkn * kh)
        dfk_s[rk, :] = jnp.where(lane == 2 * p, -dks[0][:, FOX_D:FOX_D + 1],
                                 jnp.where(lane == 2 * p + 1, -dks[1][:, 0:1], 0.0))

        @pl.when(kj == NQ - 1)
        def _():
            def fin(i, carry):
                r = pl.ds(pl.multiple_of(i * tq, tq), tq)
                d0, d1 = dq_s[0, r, :], dq_s[1, r, :]
                dqn = jnp.where(lo_half, d0, d1)
                _, _, _, qh, _, rqq, _ = _fox_prep(z_ref, gq_v, gk_v, r, lo_half)
                u = dqn * gq_v * (FOX_D ** -0.5)
                dz_ref[r, 0:LANE] = (rqq * (u - qh * _pair_mean(u * qh, lo_half))).astype(BF16)
                dgq_ref[...] += _rowsum8(dqn * qh) * (FOX_D ** -0.5)
                dfc_ref[r, :] = dfk_s[r, :] + jnp.where(lane == 2 * p, d0[:, FOX_D:FOX_D + 1],
                                                        jnp.where(lane == 2 * p + 1, d1[:, 0:1], 0.0))
                return carry
            lax.fori_loop(0, NQ, fin, 0)

    vec = pl.BlockSpec((1, LANE), lambda b, p, k: (0, 0))
    col = pl.BlockSpec((T, LANE), lambda b, p, k: (b, p))
    part = pl.BlockSpec((8, LANE), lambda b, p, k: (b * FOX_P + p, 0))
    if nsw:
        body = _hosting(body, 8, 4, 7, nsw, _chip_swap_phases, (B, FOX_P, NQ))
    return pl.pallas_call(
        body, name="fox_bwd", grid=(B, FOX_P, NQ),
        in_specs=[pl.BlockSpec((T, 384), lambda b, p, k: (b, p)), col, col, col,
                  pl.BlockSpec((T, LANE), lambda b, p, k: (b, 0)), pl.BlockSpec((1, 8, T), lambda b, p, k: (b, 0, 0)),
                  vec, vec] + [ANY] * nsw,
        out_specs=[pl.BlockSpec((T, 384), lambda b, p, k: (b, p)), col, part, part] + [ANY] * nsw,
        out_shape=[S((N, 1536), BF16), S((N, 512), F32), S((B * FOX_P * 8, LANE), F32), S((B * FOX_P * 8, LANE), F32)]
        + [S(p.shape, p.dtype) for p in swap],
        scratch_shapes=[pltpu.VMEM((T, LANE), BF16)] * 4 + [pltpu.VMEM((T, LANE), F32), pltpu.VMEM((2, T, LANE), F32),
                                                            pltpu.VMEM((T, LANE), F32)]
        + (_chip_swap_sems(nsw) if nsw else []),
        compiler_params=_cp(("arbitrary",) * 3 if nsw else ("parallel", "parallel", "arbitrary")),
    )(z, dy, y, lse, fc, fct, gq, gk, *swap)


def _mem_scores(z_ref, kv_ref, gq, gk, h):
    c = slice(MEM_D * h, MEM_D * (h + 1))
    q, k = z_ref[:, c].astype(F32), kv_ref[:, c]
    rq = lax.rsqrt(jnp.mean(q * q, axis=-1, keepdims=True) + EPS)
    rk = lax.rsqrt(jnp.mean(k * k, axis=-1, keepdims=True) + EPS)
    qh, kh = q * rq, k * rk
    qn = (qh * gq * (MEM_D ** -0.5)).astype(BF16)
    kn = (kh * gk).astype(BF16)
    s = _nt(qn, kn)
    pe = jnp.exp(s - jnp.max(s, axis=-1, keepdims=True))
    pn = pe / jnp.sum(pe, axis=-1, keepdims=True)
    return pn, qn, kn, qh, kh, rq, rk


def _mem_fwd(z, memkv, gq, gk, B, T, M, tq=1024):
    N = B * T
    tq = min(tq, T)
    NQ = T // tq
    W = MEM_H * MEM_D

    def body(z_ref, kv_ref, gq_ref, gk_ref, y_ref):
        for h in range(MEM_H):
            pn = _mem_scores(z_ref, kv_ref, gq_ref[...], gk_ref[...], h)[0]
            v = kv_ref[:, W + MEM_D * h:W + MEM_D * (h + 1)].astype(BF16)
            y_ref[:, MEM_D * h:MEM_D * (h + 1)] = _nn(pn.astype(BF16), v).astype(BF16)

    vec = pl.BlockSpec((1, LANE), lambda b, q: (0, 0))
    return pl.pallas_call(
        body, name="mem_fwd", grid=(B, NQ),
        in_specs=[pl.BlockSpec((tq, W), lambda b, q: (b * NQ + q, C_MQ // W)),
                  pl.BlockSpec((M, 2 * W), lambda b, q: (b, 0)), vec, vec],
        out_specs=pl.BlockSpec((tq, W), lambda b, q: (b * NQ + q, 0)), out_shape=S((N, W), BF16),
        compiler_params=_cp(("parallel", "parallel")),
    )(z, memkv, gq, gk)


def _mem_bwd(z, memkv, dy, gq, gk, B, T, M, tq=1024):
    N = B * T
    tq = min(tq, T)
    NQ = T // tq
    W = MEM_H * MEM_D

    def body(z_ref, kv_ref, dy_ref, gq_ref, gk_ref, dz_ref, dkv_ref, dgq_ref, dgk_ref, acc):
        qi = pl.program_id(1)
        gq_v, gk_v = gq_ref[...], gk_ref[...]

        @pl.when(qi == 0)
        def _():
            acc[...] = jnp.zeros_like(acc)
            dgq_ref[...] = jnp.zeros_like(dgq_ref)
            dgk_ref[...] = jnp.zeros_like(dgk_ref)

        for h in range(MEM_H):
            c = slice(MEM_D * h, MEM_D * (h + 1))
            cv = slice(W + MEM_D * h, W + MEM_D * (h + 1))
            pn, qn, kn, qh, _, rq, _ = _mem_scores(z_ref, kv_ref, gq_v, gk_v, h)
            do = dy_ref[:, c].astype(BF16)
            dp = _nt(do, kv_ref[:, cv].astype(BF16))
            ds = (pn * (dp - jnp.sum(dp * pn, axis=-1, keepdims=True))).astype(BF16)
            dqn = _nn(ds, kn)
            acc[:, c] += _tn(ds, qn)
            acc[:, cv] += _tn(pn.astype(BF16), do)
            u = dqn * gq_v * (MEM_D ** -0.5)
            dz_ref[:, c] = (rq * (u - qh * jnp.mean(u * qh, axis=-1, keepdims=True))).astype(BF16)
            dgq_ref[...] += _rowsum8(dqn * qh) * (MEM_D ** -0.5)

        @pl.when(qi == NQ - 1)
        def _():
            for h in range(MEM_H):
                c = slice(MEM_D * h, MEM_D * (h + 1))
                cv = slice(W + MEM_D * h, W + MEM_D * (h + 1))
                k = kv_ref[:, c]
                rk = lax.rsqrt(jnp.mean(k * k, axis=-1, keepdims=True) + EPS)
                kh = k * rk
                dkn = acc[:, c]
                u = dkn * gk_v
                dkv_ref[:, c] = (rk * (u - kh * jnp.mean(u * kh, axis=-1, keepdims=True))).astype(BF16)
                dkv_ref[:, cv] = acc[:, cv].astype(BF16)
                dgk_ref[...] += _rowsum8(dkn * kh)

    vec = pl.BlockSpec((1, LANE), lambda b, q: (0, 0))
    part = pl.BlockSpec((8, LANE), lambda b, q: (b, 0))
    return pl.pallas_call(
        body, name="mem_bwd", grid=(B, NQ),
        in_specs=[pl.BlockSpec((tq, W), lambda b, q: (b * NQ + q, C_MQ // W)),
                  pl.BlockSpec((M, 2 * W), lambda b, q: (b, 0)), pl.BlockSpec((tq, W), lambda b, q: (b * NQ + q, 0)),
                  vec, vec],
        out_specs=[pl.BlockSpec((tq, W), lambda b, q: (b * NQ + q, 0)), pl.BlockSpec((M, 2 * W), lambda b, q: (b, 0)),
                   part, part],
        out_shape=[S((N, W), BF16), S((B * M, 2 * W), BF16), S((B * 8, LANE), F32), S((B * 8, LANE), F32)],
        scratch_shapes=[pltpu.VMEM((M, 2 * W), F32)], compiler_params=_cp(("parallel", "arbitrary")),
    )(z, memkv, dy, gq, gk)


def _merge_fwd(ya, yb, yc, z, x, wa, wb, wc, wo, g_next, tm=512):
    n, d = x.shape
    wdt = ya.shape[1]
    gb = C_GATE // d

    def body(ya_ref, yb_ref, yc_ref, g0_ref, g1_ref, g2_ref, x_ref, wa_ref, wb_ref, wc_ref, wo_ref, gn_ref,
             x1_ref, mg_ref, ua_ref, ub_ref, uc_ref, h_ref):
        merged = jnp.zeros((tm, d), F32)
        for y_ref, g_ref, w_ref, u_ref in ((ya_ref, g0_ref, wa_ref, ua_ref), (yb_ref, g1_ref, wb_ref, ub_ref),
                                           (yc_ref, g2_ref, wc_ref, uc_ref)):
            u = _nn(y_ref[...], w_ref[...])
            u_ref[...] = u.astype(BF16)
            merged = merged + jax.nn.sigmoid(g_ref[...].astype(F32)) * u
        mb = merged.astype(BF16)
        mg_ref[...] = mb
        x1 = x_ref[...] + _nn(mb, wo_ref[...])
        x1_ref[...] = x1
        h_ref[...] = (x1 * lax.rsqrt(jnp.mean(x1 * x1, axis=-1, keepdims=True) + EPS) * gn_ref[...]).astype(BF16)

    yt = pl.BlockSpec((tm, wdt), lambda i: (i, 0))
    xt = pl.BlockSpec((tm, d), lambda i: (i, 0))
    wbr = pl.BlockSpec((wdt, d), lambda i: (0, 0))
    gates = [pl.BlockSpec((tm, d), functools.partial(lambda i, k: (i, gb + k), k=k)) for k in range(3)]
    return pl.pallas_call(
        body, name="merge_fwd", grid=(n // tm,),
        in_specs=[yt, yt, yt] + gates + [xt, wbr, wbr, wbr, pl.BlockSpec((d, d), lambda i: (0, 0)),
                                         pl.BlockSpec((1, d), lambda i: (0, 0))],
        out_specs=[xt] * 6, out_shape=[S((n, d), F32)] + [S((n, d), BF16)] * 5, compiler_params=_cp(("parallel",)),
    )(ya, yb, yc, z, z, z, x, wa, wb, wc, wo, g_next)


def _merge_bwd(dx1, z, ua, ub, uc, wa, wb, wc, wo, tm=512):
    n, d = dx1.shape
    wdt = wa.shape[0]
    gb = C_GATE // d

    def body(dx_ref, g0_ref, g1_ref, g2_ref, ua_ref, ub_ref, uc_ref, wa_ref, wb_ref, wc_ref, wo_ref,
             dg_ref, dya_ref, dyb_ref, dyc_ref, dua_ref, dub_ref, duc_ref):
        dm = _nt(dx_ref[...].astype(BF16), wo_ref[...])
        for k, (g_ref, u_ref, w_ref, dy_ref, du_ref) in enumerate((
                (g0_ref, ua_ref, wa_ref, dya_ref, dua_ref), (g1_ref, ub_ref, wb_ref, dyb_ref, dub_ref),
                (g2_ref, uc_ref, wc_ref, dyc_ref, duc_ref))):
            g = jax.nn.sigmoid(g_ref[...].astype(F32))
            du = (dm * g).astype(BF16)
            du_ref[...] = du
            dg_ref[:, d * k:d * (k + 1)] = (dm * u_ref[...].astype(F32) * g * (1.0 - g)).astype(BF16)
            dy_ref[...] = _nt(du, w_ref[...])

    yt = pl.BlockSpec((tm, wdt), lambda i: (i, 0))
    xt = pl.BlockSpec((tm, d), lambda i: (i, 0))
    wbr = pl.BlockSpec((wdt, d), lambda i: (0, 0))
    gates = [pl.BlockSpec((tm, d), functools.partial(lambda i, k: (i, gb + k), k=k)) for k in range(3)]
    return pl.pallas_call(
        body, name="merge_bwd", grid=(n // tm,),
        in_specs=[xt] + gates + [xt, xt, xt, wbr, wbr, wbr, pl.BlockSpec((d, d), lambda i: (0, 0))],
        out_specs=[pl.BlockSpec((tm, 3 * d), lambda i: (i, 0)), yt, yt, yt, xt, xt, xt],
        out_shape=[S((n, 3 * d), BF16)] + [S((n, wdt), F32)] * 3 + [S((n, d), BF16)] * 3,
        compiler_params=_cp(("parallel",)),
    )(dx1, z, z, z, ua, ub, uc, wa, wb, wc, wo)


FFN_TN = 1408
TN_TM = 2048
INV_SQRT2 = 0.7071067811865476
INV_SQRT_2PI = 0.3989422804014327


def _conv_shifted(a, prev, first, tm):
    row = _iota(a.shape, 0)
    p7 = jnp.where(first, 0.0, prev[7:8, :])
    p6 = jnp.where(first, 0.0, prev[6:7, :])
    a1 = jnp.where(row == 0, p7, pltpu.roll(a, 1, 0))
    a2 = jnp.where(row == 0, p6, jnp.where(row == 1, p7, pltpu.roll(a, 2, 0)))
    return a1, a2


def _ffn_act_fwd(up, cw, cb, B, T, tm=1024):
    N = B * T
    tm = min(tm, T)
    dff = cw.shape[1]
    NT, NJ, tn = T // tm, dff // FFN_TN, FFN_TN

    def body(a_ref, v_ref, cw_ref, cb_ref, y_ref, c_ref, carry):
        t = pl.program_id(2)
        a = a_ref[...].astype(F32)
        a1, a2 = _conv_shifted(a, carry[...], t == 0, tm)
        w = cw_ref[...]
        ac = w[0:1, :] * a2 + w[1:2, :] * a1 + w[2:3, :] * a + cb_ref[...]
        cdf = 0.5 * (1.0 + lax.erf(ac * INV_SQRT2))
        y_ref[...] = (ac * cdf * v_ref[...].astype(F32)).astype(BF16)
        c_ref[...] = cdf.astype(BF16)
        carry[...] = a[tm - 8:tm, :]

    return pl.pallas_call(
        body, name="ffn_act_fwd", grid=(B, NJ, NT),
        in_specs=[pl.BlockSpec((tm, tn), lambda b, j, t: (b * NT + t, j)),
                  pl.BlockSpec((tm, tn), lambda b, j, t: (b * NT + t, NJ + j)),
                  pl.BlockSpec((3, tn), lambda b, j, t: (0, j)), pl.BlockSpec((1, tn), lambda b, j, t: (0, j))],
        out_specs=[pl.BlockSpec((tm, tn), lambda b, j, t: (b * NT + t, j))] * 2, out_shape=[S((N, dff), BF16)] * 2,
        scratch_shapes=[pltpu.VMEM((8, tn), F32)], compiler_params=_cp(("parallel", "parallel", "arbitrary")),
    )(up, up, cw, cb)


def _ffn_down_loss(y, wd, x1, tgt, tm=512):
    n, d = x1.shape
    kf = y.shape[1]

    def body(y_ref, w_ref, x_ref, t_ref, dx_ref, ls_ref):
        err = x_ref[...] + _nn(y_ref[...], w_ref[...]) - t_ref[...]
        dx_ref[...] = err * (1.0 / d)

        @pl.when(pl.program_id(0) == 0)
        def _():
            ls_ref[...] = jnp.zeros_like(ls_ref)

        ls_ref[...] += _rowsum8(err * err) * (0.5 / d)

    xt = pl.BlockSpec((tm, d), lambda i: (i, 0))
    return pl.pallas_call(
        body, name="ffn_down_loss", grid=(n // tm,),
        in_specs=[pl.BlockSpec((tm, kf), lambda i: (i, 0)), pl.BlockSpec((kf, d), lambda i: (0, 0)), xt, xt],
        out_specs=[xt, pl.BlockSpec((8, d), lambda i: (0, 0))], out_shape=[S((n, d), F32), S((8, d), F32)],
        compiler_params=_cp(("arbitrary",)),
    )(y, wd, x1, tgt)


def _ffn_act_bwd1(dx2, wd, up, cdf, cw, cb, B, T, tm=512):
    N = B * T
    tm = min(tm, T)
    d = dx2.shape[1]
    dff = cw.shape[1]
    NT, NJ, tn = T // tm, dff // FFN_TN, FFN_TN

    def body(dx_ref, w_ref, a_ref, v_ref, c_ref, cw_ref, cb_ref, dac_ref, dv_ref, dcw_ref, dcb_ref, carry):
        b, t = pl.program_id(1), pl.program_id(2)
        a = a_ref[...].astype(F32)
        a1, a2 = _conv_shifted(a, carry[...], t == 0, tm)
        carry[...] = a[tm - 8:tm, :]
        w = cw_ref[...]
        ac = w[0:1, :] * a2 + w[1:2, :] * a1 + w[2:3, :] * a + cb_ref[...]
        dy = _nt(dx_ref[...].astype(BF16), w_ref[...])
        cdf = c_ref[...].astype(F32)
        dv_ref[...] = (dy * ac * cdf).astype(BF16)
        dac = dy * v_ref[...].astype(F32) * (cdf + ac * jnp.exp(-0.5 * ac * ac) * INV_SQRT_2PI)
        dac_ref[...] = dac

        @pl.when((b == 0) & (t == 0))
        def _():
            dcw_ref[...] = jnp.zeros_like(dcw_ref)
            dcb_ref[...] = jnp.zeros_like(dcb_ref)

        dcw_ref[0:8, :] += _rowsum8(dac * a2)
        dcw_ref[8:16, :] += _rowsum8(dac * a1)
        dcw_ref[16:24, :] += _rowsum8(dac * a)
        dcb_ref[...] += _rowsum8(dac)

    return pl.pallas_call(
        body, name="ffn_act_bwd1", grid=(NJ, B, NT),
        in_specs=[pl.BlockSpec((tm, d), lambda j, b, t: (b * NT + t, 0)), pl.BlockSpec((tn, d), lambda j, b, t: (j, 0)),
                  pl.BlockSpec((tm, tn), lambda j, b, t: (b * NT + t, j)),
                  pl.BlockSpec((tm, tn), lambda j, b, t: (b * NT + t, NJ + j)),
                  pl.BlockSpec((tm, tn), lambda j, b, t: (b * NT + t, j)),
                  pl.BlockSpec((3, tn), lambda j, b, t: (0, j)), pl.BlockSpec((1, tn), lambda j, b, t: (0, j))],
        out_specs=[pl.BlockSpec((tm, tn), lambda j, b, t: (b * NT + t, j)),
                   pl.BlockSpec((tm, tn), lambda j, b, t: (b * NT + t, j)),
                   pl.BlockSpec((24, tn), lambda j, b, t: (0, j)), pl.BlockSpec((8, tn), lambda j, b, t: (0, j))],
        out_shape=[S((N, dff), F32), S((N, dff), BF16), S((24, dff), F32), S((8, dff), F32)],
        scratch_shapes=[pltpu.VMEM((8, tn), F32)], compiler_params=_cp(("parallel", "arbitrary", "arbitrary")),
    )(dx2, wd, up, up, cdf, cw, cb)


def _ffn_act_bwd2(dac, cw, B, T, tm=1024):
    N = B * T
    tm = min(tm, T)
    dff = cw.shape[1]
    NT, NJ, tn = T // tm, dff // FFN_TN, FFN_TN
    last8 = N // 8 - 1

    def body(d_ref, nx_ref, cw_ref, da_ref):
        t = pl.program_id(2)
        dd = d_ref[...]
        row = _iota(dd.shape, 0)
        last = t == NT - 1
        n0 = jnp.where(last, 0.0, nx_ref[0:1, :])
        n1 = jnp.where(last, 0.0, nx_ref[1:2, :])
        d1 = jnp.where(row == tm - 1, n0, pltpu.roll(dd, tm - 1, 0))
        d2 = jnp.where(row == tm - 1, n1, jnp.where(row == tm - 2, n0, pltpu.roll(dd, tm - 2, 0)))
        w = cw_ref[...]
        da_ref[...] = (w[2:3, :] * dd + w[1:2, :] * d1 + w[0:1, :] * d2).astype(BF16)

    return pl.pallas_call(
        body, name="ffn_act_bwd2", grid=(B, NJ, NT),
        in_specs=[pl.BlockSpec((tm, tn), lambda b, j, t: (b * NT + t, j)),
                  pl.BlockSpec((8, tn), lambda b, j, t: (jnp.minimum((b * NT + t + 1) * (tm // 8), last8), j)),
                  pl.BlockSpec((3, tn), lambda b, j, t: (0, j))],
        out_specs=pl.BlockSpec((tm, tn), lambda b, j, t: (b * NT + t, j)), out_shape=S((N, dff), BF16),
        compiler_params=_cp(("parallel", "parallel", "parallel")),
    )(dac, dac, cw)


def _fold_rows(p, name):
    r, c = p.shape[0] // 8, p.shape[1]

    def body(p_ref, o_ref):
        for j in range(r):
            o_ref[j:j + 1, :] = jnp.sum(p_ref[8 * j:8 * (j + 1), :], axis=0, keepdims=True)

    return pl.pallas_call(body, name=name, out_shape=S((r, c), F32), compiler_params=_cp())(p)


def _small_reduce(lbl, dg_mix, dg_mem, dlb_p, dgn_p, dfb_p, dgq_p, dgk_p, dmq_p, dmk_p, dg_ffn, dcb_p, loss_p):
    d, dff = dg_mix.shape[1], dcb_p.shape[1]
    nbh = dlb_p.shape[0] // (8 * HG_H)

    def colsum(ref):
        return jnp.sum(ref[...], axis=0, keepdims=True)

    def body(lbl_ref, mix_ref, mem_ref, dlb_ref, dgn_ref, dfb_ref, dgq_ref, dgk_ref, dmq_ref, dmk_ref, ffn_ref, dcb_ref,
             ls_ref, o_mix, o_mem, o_lb, o_hgn, o_fb, o_fq, o_fk, o_mq, o_mk, o_ffn, o_cb, o_loss):
        o_mix[...], o_mem[...], o_ffn[...], o_cb[...] = colsum(mix_ref), colsum(mem_ref), colsum(ffn_ref), colsum(dcb_ref)
        o_hgn[...], o_fb[...], o_mq[...], o_mk[...] = colsum(dgn_ref), colsum(dfb_ref), colsum(dmq_ref), colsum(dmk_ref)
        for src, dst in ((dgq_ref, o_fq), (dgk_ref, o_fk)):
            v = colsum(src)
            dst[...] = v + pltpu.roll(v, FOX_D, 1)
        o_loss[...] = jnp.zeros((1, LANE), F32) + jnp.sum(colsum(ls_ref), axis=-1, keepdims=True)
        logits = lbl_ref[...]
        e = jnp.exp(logits - jnp.max(logits, axis=0, keepdims=True))
        pr = e / jnp.sum(e, axis=0, keepdims=True)
        rows = _iota((8, LANE), 0)
        for h in range(HG_H):
            acc = jnp.zeros((8, LANE), F32)
            for b in range(nbh):
                acc = acc + dlb_ref[8 * (b * HG_H + h):8 * (b * HG_H + h + 1), :]
            dlb = jnp.sum(acc, axis=0, keepdims=True)
            c = slice(LANE * h, LANE * (h + 1))
            p0 = pr[0:1, c]
            first = _iota((logits.shape[0], LANE), 0) == 0
            o_lb[:, c] = pr[:, c] * (jnp.where(first, 1.0, 0.0) - p0) * dlb

    outs = [S((1, d), F32), S((1, d), F32), S(lbl.shape, F32)] + [S((1, LANE), F32)] * 6 + \
           [S((1, d), F32), S((1, dff), F32), S((1, LANE), F32)]
    return pl.pallas_call(body, name="small_reduce", out_shape=outs, compiler_params=_cp())(
        lbl, dg_mix, dg_mem, dlb_p, dgn_p, dfb_p, dgq_p, dgk_p, dmq_p, dmk_p, dg_ffn, dcb_p, loss_p)


def _in_col_pieces():
    hw, fw = HG_H * HG_D, FOX_H * FOX_D
    fox0, ff0 = 4 * hw, 4 * hw + 3 * fw
    mq0 = ff0 + FOX_H
    gate0 = mq0 + MEM_H * MEM_D
    pieces = []
    for p in range(FOX_P):
        pieces += [(fox0 + j * fw + LANE * p, LANE) for j in range(3)]
    pieces.append((mq0, MEM_H * MEM_D))
    for h in range(HG_H):
        pieces += [(j * hw + HG_D * h, HG_D) for j in range(4)]
    pieces.append((gate0, C_FF - C_GATE))
    pieces.append((ff0, FOX_H))
    return pieces


def _perm_from_blocks(blocks):
    n_blk, _, c = blocks.shape
    parts = []
    for s, n in _in_col_pieces():
        lo = s
        while lo < s + n:
            d = lo // c
            hi = min(s + n, (d + 1) * c)
            parts.append(blocks[d][:, lo - d * c:hi - d * c])
            lo = hi
    parts.append(jnp.zeros((blocks.shape[1], C_END - C_FF - FOX_H), blocks.dtype))
    return jnp.concatenate(parts, axis=1)


def _unperm_blocks(segs, n_blk):
    starts = [0]
    for a in segs:
        starts.append(starts[-1] + a.shape[1])
    new_start, placed = 0, []
    for s, n in _in_col_pieces():
        placed.append((s, new_start, n))
        new_start += n
    placed.sort()
    c = sum(n for _, _, n in placed) // n_blk
    blocks = []
    for d in range(n_blk):
        parts = []
        for s, ns, n in placed:
            lo, hi = max(s, d * c), min(s + n, (d + 1) * c)
            if lo < hi:
                i = max(j for j in range(len(segs)) if starts[j] <= ns)
                parts.append(segs[i][:, ns + lo - s - starts[i]:ns + hi - s - starts[i]])
        blocks.append(jnp.concatenate(parts, axis=1))
    return jnp.stack(blocks)


def _local_step(x2, mem2, tgt, sm, W, B, T, M, ex=None):
    fbias = jnp.pad(sm["fox_f_bias"], ((0, 0), (0, LANE - FOX_H)))
    gq2 = jnp.concatenate([sm["fox_q_norm_g"]] * 2, axis=1)
    gk2 = jnp.concatenate([sm["fox_k_norm_g"]] * 2, axis=1)
    lbl = sm["hgrn_lb_logits"]
    if ex:
        h, *first = _rmsnorm_cast(x2, sm["norm_mix_g"], "norm_mix", gather=ex.first_blocks())
        W = ex.unpack_first(first)
    else:
        h = _rmsnorm_cast(x2, sm["norm_mix_g"], "norm_mix")
    z = _mm_nn(h, W["w_in"], BF16, "proj_in", 512, C_END)
    memn = _rmsnorm_cast(mem2, sm["norm_mem_g"], "norm_mem", tm=256)
    memkv = _mm_nn(memn, W["mem_kv_w"], F32, "proj_memkv", 256, 512)
    ya, o_raw, states, a_mat = _hgrn_fwd(z, lbl, sm["hgrn_norm_g"], B, T)
    fc, fct = _fox_gate_fwd(z, fbias, B, T)
    yb, lse, *late = _fox_fwd(z, fc, fct, gq2, gk2, B, T, gather=ex.late_blocks() if ex else ())
    if ex:
        W = {**W, **ex.unpack_late(late)}
    yc = _mem_fwd(z, memkv, sm["mem_q_norm_g"], sm["mem_k_norm_g"], B, T, M)
    x1, merged, ua, ub, uc, h2 = _merge_fwd(ya, yb, yc, z, x2, W["w_br_hgrn"], W["w_br_fox"], W["w_br_mem"], W["w_out"],
                                            sm["norm_ffn_g"])
    up = _mm_nn(h2, W["ffn_w_up"], BF16, "ffn_up", 512, 2 * FFN_TN)
    yf, cdf = _ffn_act_fwd(up, W["ffn_conv_w"], sm["ffn_conv_b"], B, T)
    dx2, loss_p = _ffn_down_loss(yf, W["ffn_w_down"], x1, tgt)
    dff = W["ffn_conv_w"].shape[1]
    dac, dv, dcw_p, dcb_p = _ffn_act_bwd1(dx2, W["ffn_w_down"], up, cdf, W["ffn_conv_w"], sm["ffn_conv_b"], B, T)
    da = _ffn_act_bwd2(dac, W["ffn_conv_w"], B, T)
    g = {"ffn_conv_w": _fold_rows(dcw_p, "g_conv_w")}
    g["ffn_w_down"] = _mm_tn(yf, dx2, "g_w_down", TN_TM, 512)
    dh2 = _mm_nt_sum([(da, 0, dff, 0), (dv, 0, dff, dff)], W["ffn_w_up"], "dh2", 512)
    g["ffn_w_up"] = [_mm_tn(h2, da, "g_w_up_a", TN_TM, FFN_TN), _mm_tn(h2, dv, "g_w_up_v", TN_TM, FFN_TN)]
    dx1, dg_ffn = _rmsnorm_bwd(dh2, x1, sm["norm_ffn_g"], dx2, "norm_ffn_bwd")
    g["w_out"] = _mm_tn(merged, dx1, "g_w_out", TN_TM, 1024)
    dgate, dya, dyb, dyc, dua, dub, duc = _merge_bwd(dx1, z, ua, ub, uc, W["w_br_hgrn"], W["w_br_fox"], W["w_br_mem"],
                                                    W["w_out"])
    g["w_br_hgrn"] = _mm_tn(ya, dua, "g_w_br_hgrn", TN_TM, 1024)
    g["w_br_fox"] = _mm_tn(yb, dub, "g_w_br_fox", TN_TM, 1024)
    g["w_br_mem"] = _mm_tn(yc, duc, "g_w_br_mem", TN_TM, 1024)
    early_pk = ex.early_grads(g) if ex else ()
    dz_hg, dlb_p, dgn_p, *early_sib = _hgrn_bwd(z, o_raw, states, a_mat, dya, lbl, sm["hgrn_norm_g"], B, T,
                                                swap_sibling=early_pk)
    dz_fox, dfc, dgq_p, dgk_p, *early_chips = _fox_bwd(z, dyb, yb, lse, fc, fct, gq2, gk2, B, T,
                                                       swap=ex.pair_sums(early_pk, early_sib, "early") if ex else ())
    dz_ff, dfb_p = _fox_gate_bwd(dfc, z, fbias, B, T)
    dz_mq, dkv, dmq_p, dmk_p = _mem_bwd(z, memkv, dyc, sm["mem_q_norm_g"], sm["mem_k_norm_g"], B, T, M)
    g["mem_kv_w"] = _mm_tn(memn, dkv, "g_mem_kv_w", 256, 512)
    dmemn = _mm_nt_sum([(dkv, 0, dkv.shape[1], 0)], W["mem_kv_w"], "d_memn", 256)
    _, dg_mem = _rmsnorm_bwd(dmemn, mem2, sm["norm_mem_g"], None, "norm_mem_bwd", tm=256)
    d = x2.shape[1]
    parts = [(dz_fox, 0, C_MQ - C_FOX, C_FOX), (dz_mq, 0, C_HG - C_MQ, C_MQ), (dz_hg, 0, C_GATE - C_HG, C_HG)]
    parts += [(dgate, d * k, d, C_GATE + d * k) for k in range(3)] + [(dz_ff, 0, C_END - C_FF, C_FF)]
    g["w_in"] = [_mm_tn(h, dzs, "g_w_in_%d" % i, 2 * TN_TM,
                        max(t for t in (1024, 768, 512, LANE) if dzs.shape[1] % t == 0))
                 for i, dzs in enumerate((dz_fox, dz_mq, dz_hg, dgate, dz_ff))]
    sums = None
    if ex:
        last_pk = ex.last_grads(g)
        last_sib = _swap_with_sibling(last_pk, "rs_sibling_last")
        dh, last_chips = _mm_nt_sum(parts, W["w_in"], "dh", 512, swap=ex.pair_sums(last_pk, last_sib, "last"))
        sums = (ex.final_sums(early_pk, early_sib, early_chips, "early"),
                ex.final_sums(last_pk, last_sib, last_chips, "last"))
    else:
        dh = _mm_nt_sum(parts, W["w_in"], "dh", 512)
    grad_x, dg_mix = _rmsnorm_bwd(dh, x2, sm["norm_mix_g"], dx1, "norm_mix_bwd")
    small = _small_reduce(lbl, dg_mix, dg_mem, dlb_p, dgn_p, dfb_p, dgq_p, dgk_p, dmq_p, dmk_p, dg_ffn, dcb_p, loss_p)
    names = ("norm_mix_g", "norm_mem_g", "hgrn_lb_logits", "hgrn_norm_g", "fox_f_bias", "fox_q_norm_g", "fox_k_norm_g",
             "mem_q_norm_g", "mem_k_norm_g", "norm_ffn_g", "ffn_conv_b", "loss")
    g.update(dict(zip(names, small)))
    return grad_x, g, sums


ANY = pl.BlockSpec(memory_space=pl.ANY)


def _position():
    return lax.axis_index("x"), lax.axis_index("y"), lax.axis_index("c")


def _all_gather(blocks, name):
    nb = len(blocks)

    def body(*refs):
        start, forward, finish = _gather_phases(refs[:nb], refs[nb:2 * nb], *refs[2 * nb:])
        start()
        forward()
        finish()

    return pl.pallas_call(
        body, name=name, out_shape=_gather_shapes(blocks), in_specs=[ANY] * nb, out_specs=[ANY] * nb,
        scratch_shapes=_gather_sems(nb),
    )(*blocks)


def _hosting(body, n_in, n_out, n_scratch, n_x, make_phases, grid):
    n_steps = math.prod(grid)

    def hosted(*refs):
        a = n_in + n_x
        b = a + n_out + n_x
        ins, xs = refs[:n_in], refs[n_in:a]
        outs, x_outs = refs[a:a + n_out], refs[a + n_out:b]
        scratch, sems = refs[b:b + n_scratch], refs[b + n_scratch:]
        step = 0
        for ax, n in enumerate(grid):
            step = step * n + pl.program_id(ax)
        phases = make_phases(xs, x_outs, *sems)
        pl.when(step == 0)(phases[0])
        for ph in phases[1:-1]:
            pl.when(step == n_steps // 2)(ph)
        body(*ins, *outs, *scratch)
        pl.when(step == n_steps - 1)(phases[-1])

    return hosted


def _gather_shapes(blocks):
    return [S((N_DEV,) + b.shape, b.dtype) for b in blocks]


def _gather_sems(nb):
    return [pltpu.SemaphoreType.DMA((7 * nb,)), pltpu.SemaphoreType.DMA((7 * nb,)), pltpu.SemaphoreType.DMA((nb,))]


def _gather_phases(x_refs, out_refs, send_sems, recv_sems, local_sems):
    nb = len(x_refs)
    x, y, c = _position()
    me, sibling = (x, y, c), (x, y, 1 - c)
    chips = [(1 - x, y), (x, 1 - y), (1 - x, 1 - y)]

    def copy(i, k, blk, to, own=False):
        px, py, pc = blk
        slot = out_refs[i].at[4 * px + 2 * py + pc]
        return pltpu.make_async_remote_copy(
            src_ref=x_refs[i] if own else slot, dst_ref=slot, send_sem=send_sems.at[7 * i + k],
            recv_sem=recv_sems.at[7 * i + k], device_id=to, device_id_type=MESH)

    def mine(i):
        return pltpu.make_async_copy(x_refs[i], out_refs[i].at[4 * x + 2 * y + c], local_sems.at[i])

    def first(i):
        return [copy(i, 0, me, sibling, own=True)] + [copy(i, 1 + j, me, (*chip, c), own=True)
                                                     for j, chip in enumerate(chips)]

    def passed(i, j):
        return copy(i, 4 + j, (*chips[j], c), sibling)

    def start():
        for i in range(nb):
            mine(i).start()
            for cp in first(i):
                cp.start()

    def forward():
        for i in range(nb):
            for j, chip in enumerate(chips):
                copy(i, 1 + j, (*chip, c), me).wait_recv()
                passed(i, j).start()

    def finish():
        for i in range(nb):
            copy(i, 0, sibling, me).wait_recv()
            for j, chip in enumerate(chips):
                copy(i, 4 + j, (*chip, 1 - c), me).wait_recv()
        for i in range(nb):
            for cp in first(i) + [passed(i, j) for j in range(3)]:
                cp.wait_send()
            mine(i).wait()

    return start, forward, finish


def _swap_with_sibling(pks, name):
    nb = len(pks)

    def body(*refs):
        start, finish = _sibling_swap_phases(refs[:nb], refs[nb:2 * nb], *refs[2 * nb:])
        start()
        finish()

    return pl.pallas_call(
        body, name=name, out_shape=_sibling_swap_shapes(pks), in_specs=[ANY] * nb, out_specs=[ANY] * nb,
        scratch_shapes=_sibling_swap_sems(nb),
    )(*pks)


def _sibling_swap_shapes(pks):
    return [S((4,) + p.shape[1:], p.dtype) for p in pks]


def _sibling_swap_sems(nb):
    return [pltpu.SemaphoreType.DMA((4 * nb,)), pltpu.SemaphoreType.DMA((4 * nb,))]


def _sibling_swap_phases(pk_refs, out_refs, send_sems, recv_sems):
    nb = len(pk_refs)
    x, y, c = _position()

    def copies():
        return [pltpu.make_async_remote_copy(
            src_ref=pk_refs[i].at[2 * k + 1 - c], dst_ref=out_refs[i].at[k], send_sem=send_sems.at[4 * i + k],
            recv_sem=recv_sems.at[4 * i + k], device_id=(x, y, 1 - c), device_id_type=MESH)
            for i in range(nb) for k in range(4)]

    def start():
        for cp in copies():
            cp.start()

    def finish():
        for cp in copies():
            cp.wait()

    return start, finish


def _swap_between_chips(pbs, name):
    nb = len(pbs)

    def body(*refs):
        start, finish = _chip_swap_phases(refs[:nb], refs[nb:2 * nb], *refs[2 * nb:])
        start()
        finish()

    return pl.pallas_call(
        body, name=name, out_shape=[S(p.shape, p.dtype) for p in pbs], in_specs=[ANY] * nb, out_specs=[ANY] * nb,
        scratch_shapes=_chip_swap_sems(nb),
    )(*pbs)


def _chip_swap_sems(nb):
    return [pltpu.SemaphoreType.DMA((3 * nb,)), pltpu.SemaphoreType.DMA((3 * nb,)), pltpu.SemaphoreType.DMA((nb,))]


def _chip_swap_phases(pb_refs, out_refs, send_sems, recv_sems, local_sems):
    nb = len(pb_refs)
    x, y, c = _position()
    me = 2 * x + y
    chips = [(1 - x, y), (x, 1 - y), (1 - x, 1 - y)]

    def local(i):
        return pltpu.make_async_copy(pb_refs[i].at[me], out_refs[i].at[me], local_sems.at[i])

    def send(i, j):
        cx, cy = chips[j]
        return pltpu.make_async_remote_copy(
            src_ref=pb_refs[i].at[2 * cx + cy], dst_ref=out_refs[i].at[me], send_sem=send_sems.at[3 * i + j],
            recv_sem=recv_sems.at[3 * i + j], device_id=(cx, cy, c), device_id_type=MESH)

    def arrival(i, j):
        cx, cy = chips[j]
        return pltpu.make_async_remote_copy(
            src_ref=pb_refs[i].at[me], dst_ref=out_refs[i].at[2 * cx + cy], send_sem=send_sems.at[3 * i + j],
            recv_sem=recv_sems.at[3 * i + j], device_id=(cx, cy, c), device_id_type=MESH)

    def start():
        for i in range(nb):
            local(i).start()
            for j in range(3):
                send(i, j).start()

    def finish():
        for i in range(nb):
            for j in range(3):
                arrival(i, j).wait_recv()
        for i in range(nb):
            for j in range(3):
                send(i, j).wait_send()
            local(i).wait()

    return start, finish


def _row_tile(r):
    return max(t for t in range(16, min(r, 1024) + 1, 16) if r % t == 0)


def _pair_sum_cast(pk, recv, core, name):
    _, r, l = pk.shape
    tr = _row_tile(r)

    def body(c_ref, a_ref, b_ref, o_ref):
        o_ref[...] = (a_ref[...] + b_ref[...]).astype(BF16)

    return pl.pallas_call(
        body, name=name,
        grid_spec=pltpu.PrefetchScalarGridSpec(
            num_scalar_prefetch=1, grid=(4, r // tr),
            in_specs=[pl.BlockSpec((None, tr, l), lambda k, i, c: (2 * k + c[0], i, 0)),
                      pl.BlockSpec((None, tr, l), lambda k, i, c: (k, i, 0))],
            out_specs=pl.BlockSpec((None, tr, l), lambda k, i, c: (k, i, 0))),
        out_shape=S((4, r, l), BF16), compiler_params=_cp(("parallel", "parallel")),
    )(core, pk, recv)


def _final_sum(pk, recv_sib, recv_chips, slot, chip, name):
    _, r, l = pk.shape
    tr = _row_tile(r)

    def body(s_ref, k_ref, a_ref, b_ref, rc_ref, o_ref):
        base = a_ref[...] + b_ref[...]
        acc = jnp.zeros_like(base)
        for j in range(4):
            acc = acc + jnp.where(k_ref[0] == j, base, rc_ref[j].astype(F32))
        o_ref[...] = acc

    return pl.pallas_call(
        body, name=name,
        grid_spec=pltpu.PrefetchScalarGridSpec(
            num_scalar_prefetch=2, grid=(r // tr,),
            in_specs=[pl.BlockSpec((None, tr, l), lambda i, s, k: (s[0], i, 0)),
                      pl.BlockSpec((None, tr, l), lambda i, s, k: (k[0], i, 0)),
                      pl.BlockSpec((4, tr, l), lambda i, s, k: (0, i, 0))],
            out_specs=pl.BlockSpec((tr, l), lambda i, s, k: (i, 0))),
        out_shape=S((r, l), F32), compiler_params=_cp(("parallel",)),
    )(slot, chip, pk, recv_sib, recv_chips)


def _adamw_math(w, g, m, v):
    m = ADAM_B1 * m + (1.0 - ADAM_B1) * g
    v = ADAM_B2 * v + (1.0 - ADAM_B2) * (g * g)
    m_hat = m / (1.0 - ADAM_B1 ** ADAM_STEP)
    v_hat = v / (1.0 - ADAM_B2 ** ADAM_STEP)
    return -ADAM_LR * (m_hat / (jnp.sqrt(v_hat) + ADAM_EPS) + ADAM_WD * w), m, v


def _adamw(w, g, m, v, name):
    r, c = w.shape
    tr = 512 if r % 512 == 0 else r

    def body(w_ref, g_ref, m_ref, v_ref, d_ref, nm_ref, nv_ref):
        d_ref[...], nm_ref[...], nv_ref[...] = _adamw_math(w_ref[...], g_ref[...], m_ref[...], v_ref[...])

    tile = pl.BlockSpec((tr, c), lambda i: (i, 0))
    return pl.pallas_call(
        body, name=name, grid=(r // tr,), in_specs=[tile] * 4, out_specs=[tile] * 3, out_shape=[S((r, c), F32)] * 3,
        compiler_params=_cp(("parallel",)),
    )(w, g, m, v)


def _small_update(gathered, w, m, v):
    def body(ga_ref, w_ref, m_ref, v_ref, g_ref, d_ref, nm_ref, nv_ref):
        g = ga_ref[0]
        for k in range(1, N_DEV):
            g = g + ga_ref[k]
        g_ref[...] = g
        d_ref[...], nm_ref[...], nv_ref[...] = _adamw_math(w_ref[...], g, m_ref[...], v_ref[...])

    return pl.pallas_call(body, name="small_update", out_shape=[S(w.shape, F32)] * 4, compiler_params=_cp())(
        gathered, w, m, v)


BIG = ("w_in", "mem_kv_w", "w_br_hgrn", "w_br_fox", "w_br_mem", "w_out", "ffn_w_up", "ffn_conv_w", "ffn_w_down")
GROUP_ROWS = ("w_out", "ffn_w_down")
GROUP_LANE = ("w_br_hgrn", "w_br_fox", "w_br_mem")
LANE_GROUP_ROWS = 224
SMALL = ("norm_mix_g", "norm_mem_g", "hgrn_lb_logits", "hgrn_norm_g", "fox_f_bias", "fox_q_norm_g", "fox_k_norm_g",
         "mem_q_norm_g", "mem_k_norm_g", "norm_ffn_g", "ffn_conv_b")


def _rows_of(n_elems):
    return -(-n_elems // LANE)


def _to_rows(a, lead=0):
    flat = a.reshape(a.shape[:lead] + (-1,))
    pad = (-flat.shape[-1]) % LANE
    if pad:
        flat = jnp.pad(flat, [(0, 0)] * lead + [(0, pad)])
    return flat.reshape(a.shape[:lead] + (-1, LANE))


def _stack_rows(parts, lead, total_rows):
    buf = jnp.concatenate(parts, axis=lead)
    pad = total_rows - buf.shape[lead]
    return jnp.pad(buf, [(0, 0)] * lead + [(0, pad), (0, 0)])


def _round_up(n, k):
    return -(-n // k) * k


def _from_rows(rows, shape, lead=0):
    n = math.prod(shape)
    return rows.reshape(rows.shape[:lead] + (-1,))[..., :n].reshape(rows.shape[:lead] + tuple(shape))


def _blocks_to_full(blocks, kind):
    n, a, b = blocks.shape
    return blocks.transpose(1, 0, 2).reshape(a, n * b) if kind == "col" else blocks.reshape(n * a, b)


def _full_to_blocks(full, kind, n=N_DEV):
    a, b = full.shape
    return full.reshape(a, n, b // n).transpose(1, 0, 2) if kind == "col" else full.reshape(n, a // n, b)


def _lane_group_rows(shard):
    n_lane = sum(shard[n].shape[0] for n in GROUP_LANE)
    n_cw = shard["ffn_conv_w"].size
    return n_lane, _rows_of(3 * n_cw), _rows_of(n_cw), _round_up(n_lane + _rows_of(3 * n_cw), LANE_GROUP_ROWS)


def _split_bf16x3(x):
    hi = x.astype(BF16)
    r1 = x - hi.astype(F32)
    mid = r1.astype(BF16)
    return jnp.stack([hi, mid, (r1 - mid.astype(F32)).astype(BF16)])


class _Exchange:
    def __init__(self, shard):
        self.shard = shard
        xi, yi, ci = _position()
        self.core = ci.astype(jnp.int32).reshape(1)
        self.chip = (2 * xi + yi).astype(jnp.int32).reshape(1)
        self.n_lane, self.r_pieces, self.r_vals, self.r_lane = _lane_group_rows(shard)

    def first_blocks(self):
        return [self.shard["w_in"].astype(BF16), self.shard["mem_kv_w"].astype(BF16)]

    def unpack_first(self, gathered):
        return {"w_in": _perm_from_blocks(gathered[0]), "mem_kv_w": _blocks_to_full(gathered[1], "row")}

    def late_blocks(self):
        sh = self.shard
        lane_rows = [sh[n].astype(BF16) for n in GROUP_LANE] + [_to_rows(_split_bf16x3(sh["ffn_conv_w"]))]
        return [sh[n].astype(BF16) for n in GROUP_ROWS] + [sh["ffn_w_up"].astype(BF16),
                                                           _stack_rows(lane_rows, 0, self.r_lane)]

    def unpack_late(self, gathered):
        *rows, gc, gd = gathered
        sh = self.shard
        W = {"ffn_w_up": _blocks_to_full(gc, "col")}
        for n, blocks in zip(GROUP_ROWS, rows):
            W[n] = _blocks_to_full(blocks, "row")
        r0 = 0
        for n in GROUP_LANE:
            W[n] = _blocks_to_full(gd[:, r0:r0 + sh[n].shape[0]], "col")
            r0 += sh[n].shape[0]
        cw = _from_rows(gd[:, self.n_lane:self.n_lane + self.r_pieces], (3,) + sh["ffn_conv_w"].shape, lead=1).astype(F32)
        W["ffn_conv_w"] = _blocks_to_full(cw[:, 0] + cw[:, 1] + cw[:, 2], "col")
        return W

    def early_grads(self, g):
        cw_rows = _to_rows(_full_to_blocks(g["ffn_conv_w"], "col"), lead=1)
        return [_full_to_blocks(g[n], "row") for n in GROUP_ROWS] + [
            jnp.concatenate([_full_to_blocks(h, "col", N_DEV // 2) for h in g["ffn_w_up"]], axis=0),
            _stack_rows([_full_to_blocks(g[n], "col") for n in GROUP_LANE] + [cw_rows], 1, self.r_lane)]

    def last_grads(self, g):
        return [_unperm_blocks(g["w_in"], N_DEV), _full_to_blocks(g["mem_kv_w"], "row")]

    def pair_sums(self, pks, recv_sib, tag):
        return [_pair_sum_cast(p, r, self.core, "rs_pair_sum_%s%d" % (tag, i))
                for i, (p, r) in enumerate(zip(pks, recv_sib))]

    def final_sums(self, pks, recv_sib, recv_chips, tag):
        return [_final_sum(p, rs, rc, 2 * self.chip + self.core, self.chip, "rs_final_sum_%s%d" % (tag, i))
                for i, (p, rs, rc) in enumerate(zip(pks, recv_sib, recv_chips))]

    def unpack_grads(self, early, last):
        sh = self.shard
        *rows, g_up, g_lane = early
        g_shard = {"w_in": last[0], "mem_kv_w": last[1], "ffn_w_up": g_up, **dict(zip(GROUP_ROWS, rows))}
        r0 = 0
        for n in GROUP_LANE:
            g_shard[n] = g_lane[r0:r0 + sh[n].shape[0]]
            r0 += sh[n].shape[0]
        g_shard["ffn_conv_w"] = _from_rows(g_lane[self.n_lane:self.n_lane + self.r_vals], sh["ffn_conv_w"].shape)
        return g_shard


def kernel(x, mem, norm_mix_g, norm_mem_g, w_in, hgrn_lb_logits, hgrn_norm_g, fox_f_bias, fox_q_norm_g, fox_k_norm_g, mem_kv_w, mem_q_norm_g, mem_k_norm_g, w_br_hgrn, w_br_fox, w_br_mem, w_out, norm_ffn_g, ffn_w_up, ffn_conv_w, ffn_conv_b, ffn_w_down, loss_target, m_norm_mix_g, m_norm_mem_g, m_w_in, m_hgrn_lb_logits, m_hgrn_norm_g, m_fox_f_bias, m_fox_q_norm_g, m_fox_k_norm_g, m_mem_kv_w, m_mem_q_norm_g, m_mem_k_norm_g, m_w_br_hgrn, m_w_br_fox, m_w_br_mem, m_w_out, m_norm_ffn_g, m_ffn_w_up, m_ffn_conv_w, m_ffn_conv_b, m_ffn_w_down, v_norm_mix_g, v_norm_mem_g, v_w_in, v_hgrn_lb_logits, v_hgrn_norm_g, v_fox_f_bias, v_fox_q_norm_g, v_fox_k_norm_g, v_mem_kv_w, v_mem_q_norm_g, v_mem_k_norm_g, v_w_br_hgrn, v_w_br_fox, v_w_br_mem, v_w_out, v_norm_ffn_g, v_ffn_w_up, v_ffn_conv_w, v_ffn_conv_b, v_ffn_w_down):
    given = dict(locals())
    order = ("norm_mix_g", "norm_mem_g", "w_in", "hgrn_lb_logits", "hgrn_norm_g", "fox_f_bias", "fox_q_norm_g",
             "fox_k_norm_g", "mem_kv_w", "mem_q_norm_g", "mem_k_norm_g", "w_br_hgrn", "w_br_fox", "w_br_mem", "w_out",
             "norm_ffn_g", "ffn_w_up", "ffn_conv_w", "ffn_conv_b", "ffn_w_down")
    B, T, D = x.shape
    M = mem.shape[1]
    shard = {n: given[n][0] if n in BIG else given[n] for n in order}
    mom = {n: (given["m_" + n][0], given["v_" + n][0]) if n in BIG else (given["m_" + n], given["v_" + n])
           for n in order}
    shard["hgrn_lb_logits"] = hgrn_lb_logits
    for n in ("norm_mix_g", "norm_mem_g", "hgrn_norm_g", "fox_f_bias", "fox_q_norm_g", "fox_k_norm_g", "mem_q_norm_g",
              "mem_k_norm_g", "norm_ffn_g", "ffn_conv_b"):
        shard[n] = given[n].reshape(1, -1)

    ex = _Exchange(shard)
    sm = {n: shard[n] for n in SMALL}
    grad_x, g, sums = _local_step(x.reshape(B * T, D), mem.reshape(B * M, D), loss_target.reshape(B * T, D), sm, None,
                                  B, T, M, ex)
    g_shard = ex.unpack_grads(*sums)

    sg = {n: g[n] for n in SMALL}
    sg["fox_f_bias"] = g["fox_f_bias"][:, :FOX_H]
    sg["fox_q_norm_g"] = g["fox_q_norm_g"][:, :FOX_D]
    sg["fox_k_norm_g"] = g["fox_k_norm_g"][:, :FOX_D]
    slayout, row0 = {}, 0
    for n in SMALL:
        nr = _rows_of(shard[n].size)
        slayout[n] = (row0, nr)
        row0 += nr
    loss_row = row0
    r_small = _round_up(row0 + 1, 8)

    def pack_small(d, with_loss=None):
        rows = [_to_rows(d[n]) for n in SMALL]
        rows.append(with_loss if with_loss is not None else jnp.zeros((1, LANE), F32))
        return _stack_rows(rows, 0, r_small)

    sgath, = _all_gather([pack_small(sg, g["loss"])], "ag_small")
    s_g, s_d, s_m, s_v = _small_update(sgath, pack_small(shard), pack_small({n: mom[n][0].reshape(shard[n].shape) for n in SMALL}),
                                       pack_small({n: mom[n][1].reshape(shard[n].shape) for n in SMALL}))
    loss = s_g[loss_row, 0]

    grads, deltas, new_m, new_v = {}, {}, {}, {}
    for n in BIG:
        gn = g_shard[n]
        d, nm, nv = _adamw(shard[n], gn, mom[n][0], mom[n][1], "adamw_" + n)
        grads[n], deltas[n], new_m[n], new_v[n] = (a[None] for a in (gn, d, nm, nv))
    for n in SMALL:
        r0, nr = slayout[n]
        for dst, src in ((grads, s_g), (deltas, s_d), (new_m, s_m), (new_v, s_v)):
            dst[n] = _from_rows(src[r0:r0 + nr], given[n].shape)
    return (loss, grad_x.reshape(B, T, D), *[grads[n] for n in order], *[deltas[n] for n in order],
            *[new_m[n] for n in order], *[new_v[n] for n in order])
```

```python
import functools
import math

import jax
import jax.numpy as jnp
from jax import lax
from jax.experimental import pallas as pl
from jax.experimental.pallas import tpu as pltpu

F32, BF16 = jnp.float32, jnp.bfloat16
S = jax.ShapeDtypeStruct
MESH = pl.DeviceIdType.MESH

N_DEV = 8
EPS = 1e-6
LANE = 128
CHUNK = 64
SUB = 16
HG_H, HG_D = 4, 128
HG_GROUP_FWD = 4
HG_GROUP = 2
FOX_H, FOX_D = 8, 64
FOX_P = FOX_H // 2
MEM_H, MEM_D = 4, 128
NEG = -1e30
VMEM_LIMIT = 56 * 2**20

ADAM_LR, ADAM_B1, ADAM_B2, ADAM_EPS, ADAM_WD, ADAM_STEP = 0.001, 0.9, 0.999, 1e-08, 0.01, 10

C_FOX, C_MQ, C_HG, C_GATE, C_FF, C_END = 0, 1536, 2048, 4096, 7168, 7296


def _cp(sem=None):
    return pltpu.CompilerParams(dimension_semantics=sem, vmem_limit_bytes=VMEM_LIMIT)


def _dot(a, b, dims, prec=None):
    return lax.dot_general(a, b, (dims, ((), ())), preferred_element_type=F32, precision=prec)


def _nn(a, b, prec=None):
    return _dot(a, b, ((1,), (0,)), prec)


def _nt(a, b, prec=None):
    return _dot(a, b, ((1,), (1,)), prec)


def _tn(a, b, prec=None):
    return _dot(a, b, ((0,), (0,)), prec)


def _b(x):
    return x.astype(BF16)


def _mm3(fn, a, b):
    ah, bh = _b(a), _b(b)
    return fn(ah, bh) + fn(ah, _b(b - bh.astype(F32))) + fn(_b(a - ah.astype(F32)), bh)


def _iota(shape, dim):
    return lax.broadcasted_iota(jnp.int32, shape, dim)


def _rowsum8(x):
    r, d = x.shape
    return jnp.sum(x.reshape(r // 8, 8, d), axis=0)


def _rmsnorm_cast(x, g, name, tm=1024, gather=()):
    n, d = x.shape
    nga = len(gather)

    def body(x_ref, g_ref, o_ref):
        v = x_ref[...]
        r = lax.rsqrt(jnp.mean(v * v, axis=-1, keepdims=True) + EPS)
        o_ref[...] = (v * r * g_ref[...]).astype(BF16)

    if nga:
        body = _hosting(body, 2, 1, 0, nga, _gather_phases, (n // tm,))
    out = pl.pallas_call(
        body, name=name, grid=(n // tm,),
        in_specs=[pl.BlockSpec((tm, d), lambda i: (i, 0)), pl.BlockSpec((1, d), lambda i: (0, 0))] + [ANY] * nga,
        out_specs=[pl.BlockSpec((tm, d), lambda i: (i, 0))] + [ANY] * nga,
        out_shape=[S((n, d), BF16)] + _gather_shapes(gather), scratch_shapes=_gather_sems(nga) if nga else [],
        compiler_params=_cp(("arbitrary",) if nga else ("parallel",)),
    )(x, g, *gather)
    return out if nga else out[0]


def _rmsnorm_bwd(dh, x, g, resid, name, tm=1024):
    n, d = x.shape
    has_res = resid is not None

    def body(*refs):
        if has_res:
            dh_ref, x_ref, g_ref, r_ref, dx_ref, dg_ref = refs
        else:
            dh_ref, x_ref, g_ref, dx_ref, dg_ref = refs
        v = x_ref[...]
        dhv = dh_ref[...].astype(F32)
        r = lax.rsqrt(jnp.mean(v * v, axis=-1, keepdims=True) + EPS)
        xh = v * r
        u = dhv * g_ref[...]
        dx = r * (u - xh * jnp.mean(u * xh, axis=-1, keepdims=True))
        if has_res:
            dx = dx + r_ref[...]
        dx_ref[...] = dx

        @pl.when(pl.program_id(0) == 0)
        def _():
            dg_ref[...] = jnp.zeros_like(dg_ref)

        dg_ref[...] += _rowsum8(dhv * xh)

    tile = pl.BlockSpec((tm, d), lambda i: (i, 0))
    ins = [tile, tile, pl.BlockSpec((1, d), lambda i: (0, 0))] + ([tile] if has_res else [])
    args = (dh, x, g) + ((resid,) if has_res else ())
    return pl.pallas_call(
        body, name=name, grid=(n // tm,), in_specs=ins,
        out_specs=[tile, pl.BlockSpec((8, d), lambda i: (0, 0))],
        out_shape=[S((n, d), F32), S((8, d), F32)], compiler_params=_cp(("arbitrary",)),
    )(*args)


def _mm_nn(a, b, out_dtype, name, tm, tn):
    m, k = a.shape
    n = b.shape[1]
    assert n % tn == 0 and m % tm == 0

    def body(a_ref, b_ref, o_ref):
        o_ref[...] = _nn(a_ref[...].astype(BF16), b_ref[...].astype(BF16)).astype(out_dtype)

    return pl.pallas_call(
        body, name=name, grid=(n // tn, m // tm),
        in_specs=[pl.BlockSpec((tm, k), lambda j, i: (i, 0)), pl.BlockSpec((k, tn), lambda j, i: (0, j))],
        out_specs=pl.BlockSpec((tm, tn), lambda j, i: (i, j)), out_shape=S((m, n), out_dtype),
        compiler_params=_cp(("parallel", "parallel")),
    )(a, b)


def _mm_nt_sum(parts, w, name, tm, swap=()):
    m = parts[0][0].shape[0]
    k = w.shape[0]
    assert m % tm == 0 and all(c % n == 0 and o % n == 0 for _, c, n, o in parts)
    np_ = len(parts)
    nsw = len(swap)
    n_steps = m // tm

    def body(*refs):
        o_ref = refs[2 * np_ + nsw]
        if nsw:
            start, finish = _chip_swap_phases(refs[2 * np_:2 * np_ + nsw], refs[2 * np_ + nsw + 1:2 * np_ + 2 * nsw + 1],
                                              *refs[2 * np_ + 2 * nsw + 1:])
            pl.when(pl.program_id(0) == 0)(start)
        acc = _nt(refs[0][...].astype(BF16), refs[np_][...].astype(BF16))
        for i in range(1, np_):
            acc = acc + _nt(refs[i][...].astype(BF16), refs[np_ + i][...].astype(BF16))
        o_ref[...] = acc
        if nsw:
            pl.when(pl.program_id(0) == n_steps - 1)(finish)

    dy_specs = [pl.BlockSpec((tm, n), functools.partial(lambda i, j: (i, j), j=c // n)) for _, c, n, _ in parts]
    w_specs = [pl.BlockSpec((k, n), functools.partial(lambda i, j: (0, j), j=o // n)) for _, _, n, o in parts]
    out = pl.pallas_call(
        body, name=name, grid=(n_steps,), in_specs=dy_specs + w_specs + [ANY] * nsw,
        out_specs=[pl.BlockSpec((tm, k), lambda i: (i, 0))] + [ANY] * nsw,
        out_shape=[S((m, k), F32)] + [S(p.shape, p.dtype) for p in swap],
        scratch_shapes=_chip_swap_sems(nsw) if nsw else [],
        compiler_params=_cp(("arbitrary",) if nsw else ("parallel",)),
    )(*([p[0] for p in parts] + [w] * np_ + list(swap)))
    return (out[0], out[1:]) if nsw else out[0]


def _mm_tn(x, dy, name, tm, tn):
    m, k = x.shape
    n = dy.shape[1]
    tm = min(tm, m)
    assert m % tm == 0 and n % tn == 0

    def body(x_ref, dy_ref, o_ref):
        part = _tn(x_ref[...].astype(BF16), dy_ref[...].astype(BF16))

        @pl.when(pl.program_id(1) == 0)
        def _():
            o_ref[...] = part

        @pl.when(pl.program_id(1) > 0)
        def _():
            o_ref[...] += part

    return pl.pallas_call(
        body, name=name, grid=(n // tn, m // tm),
        in_specs=[pl.BlockSpec((tm, k), lambda j, i: (i, 0)), pl.BlockSpec((tm, tn), lambda j, i: (i, j))],
        out_specs=pl.BlockSpec((k, tn), lambda j, i: (0, j)), out_shape=S((k, n), F32),
        compiler_params=_cp(("parallel", "arbitrary")),
    )(x, dy)


def _lower_bound(logits):
    e = jnp.exp(logits - jnp.max(logits, axis=0, keepdims=True))
    return e[0:1, :] / jnp.sum(e, axis=0, keepdims=True)


def _hg_gates(fl, lb):
    sig = jax.nn.sigmoid(fl)
    f = lb + (1.0 - lb) * sig
    k = (1.0 - lb) * (1.0 - sig)
    return sig, f, k, jnp.log(f)


def _silu_and_grad(x):
    s = jax.nn.sigmoid(x)
    return x * s, s * (1.0 + x * (1.0 - s))


def _hg_rowblocks(G):
    return [None] + [G[SUB * i - 1:SUB * i, :] for i in range(1, CHUNK // SUB)]


def _hg_intra_A(qs, k, G):
    refs = _hg_rowblocks(G)
    cols = _iota((SUB, LANE), 1)
    rows = _iota((SUB, LANE), 0)
    no_keys = jnp.zeros((LANE - CHUNK, HG_D), BF16)
    blocks = []
    for i in range(CHUNK // SUB):
        lo = SUB * i
        qb, Gb = qs[lo:lo + SUB, :], G[lo:lo + SUB, :]
        diag = jnp.zeros((SUB, LANE), F32)
        for s in range(SUB):
            e = jnp.exp(jnp.minimum(Gb - G[lo + s:lo + s + 1, :], 0.0))
            col = jnp.sum(qb * k[lo + s:lo + s + 1, :] * e, axis=-1, keepdims=True)
            diag = jnp.where(cols == lo + s, col, diag)
        a = jnp.where((cols >= lo) & (cols <= rows + lo), diag, 0.0)
        if i > 0:
            qr = qb * jnp.exp(Gb - refs[i])
            kr = k * jnp.exp(jnp.minimum(refs[i] - G, 0.0))
            a = jnp.where(cols < lo, _nt(_b(qr), jnp.concatenate([_b(kr), no_keys], axis=0)), a)
        blocks.append(a)
    return jnp.concatenate(blocks, axis=0)


def _hg_intra_bwd(dA, qs, k, G):
    refs = _hg_rowblocks(G)
    cols = _iota((SUB, CHUNK), 1)
    rows16 = _iota((SUB, HG_D), 0)
    dk = jnp.zeros((CHUNK, HG_D), F32)
    dq_blocks, dk_diag_blocks = [], []
    for i in range(CHUNK // SUB):
        lo = SUB * i
        qb, Gb = qs[lo:lo + SUB, :], G[lo:lo + SUB, :]
        dAb = dA[lo:lo + SUB, :]
        dq = jnp.zeros((SUB, HG_D), F32)
        dkb = jnp.zeros((SUB, HG_D), F32)
        for s in range(SUB):
            e = jnp.exp(jnp.minimum(Gb - G[lo + s:lo + s + 1, :], 0.0))
            e = jnp.where(rows16 >= s, e, 0.0)
            dcol = jnp.sum(jnp.where(cols == lo + s, dAb, 0.0), axis=-1, keepdims=True)
            w = dcol * e
            dq = dq + w * k[lo + s:lo + s + 1, :]
            dkb = jnp.where(rows16 == s, jnp.sum(w * qb, axis=0, keepdims=True), dkb)
        if i > 0:
            e1 = jnp.exp(Gb - refs[i])
            e2 = jnp.exp(jnp.minimum(refs[i] - G, 0.0))
            dA_off = jnp.where(cols < lo, dAb, 0.0)
            dq = dq + _mm3(_nn, dA_off, k * e2) * e1
            dk = dk + _mm3(_tn, dA_off, qb * e1) * e2
        dq_blocks.append(dq)
        dk_diag_blocks.append(dkb)
    return jnp.concatenate(dq_blocks, axis=0), dk + jnp.concatenate(dk_diag_blocks, axis=0)


def _tri(n, upper=False):
    r, c = _iota((n, n), 0), _iota((n, n), 1)
    return jnp.where((c >= r) if upper else (r >= c), 1.0, 0.0).astype(BF16)


def _prefix_mm(tri, x):
    hi = x.astype(BF16)
    r1 = x - hi.astype(F32)
    mid = r1.astype(BF16)
    lo = (r1 - mid.astype(F32)).astype(BF16)
    return _nn(tri, hi) + _nn(tri, mid) + _nn(tri, lo)


def _hgrn_fwd(z, lb, gn, B, T):
    N = B * T
    NC = T // CHUNK
    ng = HG_H // HG_GROUP_FWD

    def body(z_ref, lb_ref, gn_ref, y_ref, o_ref, st_ref, a_ref, s_scr):
        lbs = _lower_bound(lb_ref[...])
        tri = _tri(CHUNK)
        s_scr[...] = jnp.zeros_like(s_scr)

        def chunk(c, carry):
            r = pl.ds(pl.multiple_of(c * CHUNK, CHUNK), CHUNK)
            for hh in range(HG_GROUP_FWD):
                zc, oc = 4 * LANE * hh, LANE * hh
                ql, fl, il, gl = (z_ref[r, zc + LANE * j:zc + LANE * (j + 1)].astype(F32) for j in range(4))
                _, _, k, logf = _hg_gates(fl, lbs[:, oc:oc + LANE])
                G = _prefix_mm(tri, logf)
                qs = ql * jax.nn.sigmoid(ql)
                st = s_scr[hh]
                st_ref[hh * NC + c] = st
                g_last = G[CHUNK - 1:CHUNK, :]
                A = _b(_hg_intra_A(qs, k, G))
                a_ref[r, oc:oc + LANE] = A
                o = _nn(A[:, 0:CHUNK], _b(il)) + _nt(_b(qs * jnp.exp(G)), _b(st))
                s_scr[hh] = st * jnp.exp(g_last) + _mm3(_tn, il, k * jnp.exp(g_last - G))
                o_ref[r, oc:oc + LANE] = o
                rstd = lax.rsqrt(jnp.mean(o * o, axis=-1, keepdims=True) + EPS)
                y_ref[r, oc:oc + LANE] = (o * rstd * gn_ref[...] * (gl * jax.nn.sigmoid(gl))).astype(BF16)
            return carry

        lax.fori_loop(0, NC, chunk, 0, unroll=4)

    gw = HG_GROUP_FWD * LANE
    cb = C_HG // (4 * gw)
    return pl.pallas_call(
        body, name="hgrn_fwd", grid=(B, ng),
        in_specs=[pl.BlockSpec((T, 4 * gw), lambda b, h: (b, cb + h)), pl.BlockSpec((lb.shape[0], gw), lambda b, h: (0, h)),
                  pl.BlockSpec((1, LANE), lambda b, h: (0, 0))],
        out_specs=[pl.BlockSpec((T, gw), lambda b, h: (b, h)), pl.BlockSpec((T, gw), lambda b, h: (b, h)),
                   pl.BlockSpec((HG_GROUP_FWD * NC, HG_D, HG_D), lambda b, h: (b * ng + h, 0, 0)),
                   pl.BlockSpec((T, gw), lambda b, h: (b, h))],
        out_shape=[S((N, 512), BF16), S((N, 512), F32), S((B * HG_H * NC, HG_D, HG_D), F32), S((N, 512), BF16)],
        scratch_shapes=[pltpu.VMEM((HG_GROUP_FWD, HG_D, HG_D), F32)], compiler_params=_cp(("parallel", "parallel")),
    )(z, lb, gn)


def _hgrn_bwd(z, o_raw, states, a_mat, dy, lb, gn, B, T, swap_sibling=()):
    N = B * T
    NC = T // CHUNK
    ng = HG_H // HG_GROUP
    nsw = len(swap_sibling)

    def body(z_ref, o_ref, st_ref, a_ref, dy_ref, lb_ref, gn_ref, dz_ref, dlb_ref, dgn_ref, ds_scr, racc, dgn_acc):
        lbs = _lower_bound(lb_ref[...])
        gn_v = gn_ref[...]
        tri, triu = _tri(CHUNK), _tri(CHUNK, upper=True)
        cmask = _iota((CHUNK, CHUNK), 0) >= _iota((CHUNK, CHUNK), 1)
        for ref in (ds_scr, racc, dgn_acc, dlb_ref):
            ref[...] = jnp.zeros_like(ref)

        def chunk(ci, carry):
            c = NC - 1 - ci
            r = pl.ds(pl.multiple_of(c * CHUNK, CHUNK), CHUNK)
            for hh in range(HG_GROUP):
                zc, oc = 4 * LANE * hh, LANE * hh
                lb_v = lbs[:, oc:oc + LANE]
                ql, fl, il, gl = (z_ref[r, zc + LANE * j:zc + LANE * (j + 1)].astype(F32) for j in range(4))
                sig, f, k, logf = _hg_gates(fl, lb_v)
                G = _prefix_mm(tri, logf)
                qs, dsilu_q = _silu_and_grad(ql)
                gs, dsilu_g = _silu_and_grad(gl)
                o = o_ref[r, oc:oc + LANE]
                dyv = dy_ref[r, oc:oc + LANE]
                rstd = lax.rsqrt(jnp.mean(o * o, axis=-1, keepdims=True) + EPS)
                oh = o * rstd
                dgl = dyv * oh * gn_v * dsilu_g
                dn = dyv * gs
                dgn_acc[...] += _rowsum8(dn * oh)
                u = dn * gn_v
                do = rstd * (u - oh * jnp.mean(u * oh, axis=-1, keepdims=True))
                st = st_ref[hh * NC + c]
                dst = ds_scr[hh]
                eG = jnp.exp(G)
                g_last = G[CHUNK - 1:CHUNK, :]
                eL = jnp.exp(g_last - G)
                dA = jnp.where(cmask, _mm3(_nt, do, il), 0.0)
                dq_in, dk_in = _hg_intra_bwd(dA, qs, k, G)
                di = _tn(a_ref[r, oc:oc + LANE][:, 0:CHUNK], _b(do)) + _nt(_b(k * eL), _b(dst))
                dq = dq_in + _mm3(_nn, do, st) * eG
                dk = dk_in + _mm3(_nn, il, dst) * eL
                ds_scr[hh] = dst * jnp.exp(g_last) + _mm3(_tn, do, qs * eG)
                dd = qs * dq - k * dk
                dlogf = _prefix_mm(triu, dd) + racc[hh]
                racc[hh] += jnp.sum(dd, axis=0, keepdims=True)
                df = dlogf / f - dk
                dlb_ref[8 * hh:8 * (hh + 1), :] += _rowsum8(df * (1.0 - sig))
                dz_ref[r, zc:zc + LANE] = (dq * dsilu_q).astype(BF16)
                dz_ref[r, zc + LANE:zc + 2 * LANE] = (df * (1.0 - lb_v) * sig * (1.0 - sig)).astype(BF16)
                dz_ref[r, zc + 2 * LANE:zc + 3 * LANE] = di.astype(BF16)
                dz_ref[r, zc + 3 * LANE:zc + 4 * LANE] = dgl.astype(BF16)
            return carry

        lax.fori_loop(0, NC, chunk, 0, unroll=4)
        dgn_ref[...] = dgn_acc[...]

    gw = HG_GROUP * LANE
    cb = C_HG // (4 * gw)
    col = pl.BlockSpec((T, gw), lambda b, h: (b, h))
    if nsw:
        body = _hosting(body, 7, 3, 3, nsw, _sibling_swap_phases, (B, ng))
    return pl.pallas_call(
        body, name="hgrn_bwd", grid=(B, ng),
        in_specs=[pl.BlockSpec((T, 4 * gw), lambda b, h: (b, cb + h)), col,
                  pl.BlockSpec((HG_GROUP * NC, HG_D, HG_D), lambda b, h: (b * ng + h, 0, 0)), col, col,
                  pl.BlockSpec((lb.shape[0], gw), lambda b, h: (0, h)), pl.BlockSpec((1, LANE), lambda b, h: (0, 0))]
        + [ANY] * nsw,
        out_specs=[pl.BlockSpec((T, 4 * gw), lambda b, h: (b, h)),
                   pl.BlockSpec((8 * HG_GROUP, LANE), lambda b, h: (b * ng + h, 0)),
                   pl.BlockSpec((8, LANE), lambda b, h: (b * ng + h, 0))] + [ANY] * nsw,
        out_shape=[S((N, 2048), BF16), S((B * HG_H * 8, LANE), F32), S((B * ng * 8, LANE), F32)]
        + _sibling_swap_shapes(swap_sibling),
        scratch_shapes=[pltpu.VMEM((HG_GROUP, HG_D, HG_D), F32), pltpu.VMEM((HG_GROUP, 1, LANE), F32),
                        pltpu.VMEM((8, LANE), F32)] + (_sibling_swap_sems(nsw) if nsw else []),
        compiler_params=_cp(("arbitrary", "arbitrary") if nsw else ("parallel", "parallel")),
    )(z, o_raw, states, a_mat, dy, lb, gn, *swap_sibling)


def _pair_mean(x, lo_half):
    a = jnp.sum(jnp.where(lo_half, x, 0.0), axis=-1, keepdims=True)
    b = jnp.sum(jnp.where(lo_half, 0.0, x), axis=-1, keepdims=True)
    return jnp.where(lo_half, a, b) * (1.0 / FOX_D)


def _fox_gate_fwd(z, bias, B, T):
    N = B * T
    tb = LANE

    def body(z_ref, b_ref, fc_ref, fct_ref):
        tri = _tri(tb)

        def step(i, carry):
            r = pl.ds(pl.multiple_of(i * tb, tb), tb)
            cs = _prefix_mm(tri, jax.nn.log_sigmoid(z_ref[r, :].astype(F32) + b_ref[...])) + carry
            fc_ref[r, :] = cs
            fct_ref[0, :, r] = cs.T[0:8, :]
            return cs[tb - 1:tb, :]

        lax.fori_loop(0, T // tb, step, jnp.zeros((1, LANE), F32))

    return pl.pallas_call(
        body, name="fox_gate_fwd", grid=(B,),
        in_specs=[pl.BlockSpec((T, LANE), lambda b: (b, C_FF // LANE)), pl.BlockSpec((1, LANE), lambda b: (0, 0))],
        out_specs=[pl.BlockSpec((T, LANE), lambda b: (b, 0)), pl.BlockSpec((1, 8, T), lambda b: (b, 0, 0))],
        out_shape=[S((N, LANE), F32), S((B, 8, T), F32)], compiler_params=_cp(("parallel",)),
    )(z, bias)


def _fox_gate_bwd(dfc, z, bias, B, T):
    N = B * T
    tb = LANE
    nt = T // tb

    def body(d_ref, z_ref, b_ref, dz_ref, db_ref):
        triu = _tri(tb, upper=True)
        db_ref[...] = jnp.zeros_like(db_ref)

        def step(ii, carry):
            r = pl.ds(pl.multiple_of((nt - 1 - ii) * tb, tb), tb)
            d = d_ref[r, 0:LANE]
            for p in range(1, FOX_P):
                d = d + d_ref[r, LANE * p:LANE * (p + 1)]
            rc = _prefix_mm(triu, d) + carry
            dff = rc * jax.nn.sigmoid(-(z_ref[r, :].astype(F32) + b_ref[...]))
            dz_ref[r, :] = dff.astype(BF16)
            db_ref[...] += _rowsum8(dff)
            return carry + jnp.sum(d, axis=0, keepdims=True)

        lax.fori_loop(0, nt, step, jnp.zeros((1, LANE), F32))

    return pl.pallas_call(
        body, name="fox_gate_bwd", grid=(B,),
        in_specs=[pl.BlockSpec((T, 512), lambda b: (b, 0)), pl.BlockSpec((T, LANE), lambda b: (b, C_FF // LANE)),
                  pl.BlockSpec((1, LANE), lambda b: (0, 0))],
        out_specs=[pl.BlockSpec((T, LANE), lambda b: (b, 0)), pl.BlockSpec((8, LANE), lambda b: (b, 0))],
        out_shape=[S((N, LANE), BF16), S((B * 8, LANE), F32)], compiler_params=_cp(("parallel",)),
    )(dfc, z, bias)


def _fox_prep(z_ref, gq, gk, r, lo_half):
    q, k, v = (z_ref[r, LANE * j:LANE * (j + 1)].astype(F32) for j in range(3))
    rq = lax.rsqrt(_pair_mean(q * q, lo_half) + EPS)
    rk = lax.rsqrt(_pair_mean(k * k, lo_half) + EPS)
    qh, kh = q * rq, k * rk
    return qh * gq * (FOX_D ** -0.5), kh * gk, v, qh, kh, rq, rk


def _fox_fwd(z, fc, fct, gq, gk, B, T, tq=512, gather=()):
    N = B * T
    NQ = T // tq
    nga = len(gather)

    def body(z_ref, fc_ref, fct_ref, gq_ref, gk_ref, y_ref, lse_ref, qn_s, kn_s, v_s):
        p, qi = pl.program_id(1), pl.program_id(2)
        lo_half = _iota((1, LANE), 1) < FOX_D

        @pl.when(qi == 0)
        def _():
            def prep(i, carry):
                r = pl.ds(pl.multiple_of(i * tq, tq), tq)
                qn, kn, v = _fox_prep(z_ref, gq_ref[...], gk_ref[...], r, lo_half)[:3]
                qn_s[r, :], kn_s[r, :], v_s[r, :] = qn.astype(BF16), kn.astype(BF16), v.astype(BF16)
                return carry
            lax.fori_loop(0, NQ, prep, 0)

        rq = pl.ds(pl.multiple_of(qi * tq, tq), tq)
        qn = qn_s[rq, :]
        fcq = fc_ref[rq, :]
        lane = _iota((tq, LANE), 1)
        causal = _iota((tq, tq), 0) >= _iota((tq, tq), 1)
        qhs = [jnp.where(lo_half, qn, jnp.zeros_like(qn)), jnp.where(lo_half, jnp.zeros_like(qn), qn)]
        fqs = [jnp.sum(jnp.where(lane == 2 * p + hh, fcq, 0.0), axis=-1, keepdims=True) for hh in range(2)]

        def kv(j, carry, diagonal):
            rk = pl.ds(pl.multiple_of(j * tq, tq), tq)
            kj, vj = kn_s[rk, :], v_s[rk, :]
            one = jnp.ones_like(vj)
            new = []
            for hh in range(2):
                m, acc = carry[hh]
                s = _nt(qhs[hh], kj) + fqs[hh] - fct_ref[0, pl.ds(2 * p + hh, 1), rk]
                if diagonal:
                    s = jnp.where(causal, s, NEG)
                m_new = jnp.maximum(m, jnp.max(s, axis=-1, keepdims=True))
                pe = jnp.exp(s - m_new)
                v_aug = jnp.where(lo_half if hh == 0 else jnp.logical_not(lo_half), vj, one)
                new.append((m_new, jnp.exp(m - m_new) * acc + _nn(pe.astype(BF16), v_aug)))
            return tuple(new)

        init = tuple((jnp.full((tq, 1), NEG, F32), jnp.zeros((tq, LANE), F32)) for _ in range(2))
        carry = lax.fori_loop(0, qi, functools.partial(kv, diagonal=False), init)
        (m0, a0), (m1, a1) = kv(qi, carry, True)
        l0, l1 = a0[:, FOX_D:FOX_D + 1], a1[:, 0:1]
        y_ref[...] = jnp.where(lo_half, a0 / l0, a1 / l1).astype(BF16)
        lse_ref[...] = jnp.where(lo_half, m0 + jnp.log(l0), m1 + jnp.log(l1))

    vec = pl.BlockSpec((1, LANE), lambda b, p, q: (0, 0))
    tile = pl.BlockSpec((tq, LANE), lambda b, p, q: (b * NQ + q, p))
    if nga:
        body = _hosting(body, 5, 2, 3, nga, _gather_phases, (B, FOX_P, NQ))
    return pl.pallas_call(
        body, name="fox_fwd", grid=(B, FOX_P, NQ),
        in_specs=[pl.BlockSpec((T, 384), lambda b, p, q: (b, p)), pl.BlockSpec((T, LANE), lambda b, p, q: (b, 0)),
                  pl.BlockSpec((1, 8, T), lambda b, p, q: (b, 0, 0)), vec, vec] + [ANY] * nga,
        out_specs=[tile, tile] + [ANY] * nga, out_shape=[S((N, 512), BF16), S((N, 512), F32)] + _gather_shapes(gather),
        scratch_shapes=[pltpu.VMEM((T, LANE), BF16)] * 3 + (_gather_sems(nga) if nga else []),
        compiler_params=_cp(("arbitrary",) * 3 if nga else ("parallel", "parallel", "arbitrary")),
    )(z, fc, fct, gq, gk, *gather)


def _fox_bwd(z, dy, y, lse, fc, fct, gq, gk, B, T, tq=512, swap=()):
    N = B * T
    NQ = T // tq
    nsw = len(swap)

    def body(z_ref, dy_ref, y_ref, lse_ref, fc_ref, fct_ref, gq_ref, gk_ref, dz_ref, dfc_ref, dgq_ref, dgk_ref,
             qn_s, kn_s, v_s, do_s, delta_s, dq_s, dfk_s):
        p, kj = pl.program_id(1), pl.program_id(2)
        lo_half = _iota((1, LANE), 1) < FOX_D
        lane = _iota((tq, LANE), 1)
        gq_v, gk_v = gq_ref[...], gk_ref[...]

        @pl.when(kj == 0)
        def _():
            def prep(i, carry):
                r = pl.ds(pl.multiple_of(i * tq, tq), tq)
                qn, kn, v = _fox_prep(z_ref, gq_v, gk_v, r, lo_half)[:3]
                qn_s[r, :], kn_s[r, :], v_s[r, :] = qn.astype(BF16), kn.astype(BF16), v.astype(BF16)
                do = dy_ref[r, :]
                do_s[r, :] = do.astype(BF16)
                delta_s[r, :] = _pair_mean(do * y_ref[r, :].astype(F32), lo_half) * float(FOX_D)
                return carry
            lax.fori_loop(0, NQ, prep, 0)
            dq_s[...] = jnp.zeros_like(dq_s)
            dgq_ref[...] = jnp.zeros_like(dgq_ref)
            dgk_ref[...] = jnp.zeros_like(dgk_ref)

        rk = pl.ds(pl.multiple_of(kj * tq, tq), tq)
        kn, vv = kn_s[rk, :], v_s[rk, :]
        causal = _iota((tq, tq), 0) >= _iota((tq, tq), 1)
        zero, one = jnp.zeros_like(kn), jnp.ones_like(kn)
        hms = [lo_half, jnp.logical_not(lo_half)]
        kmasks = [jnp.where(hm, kn, zero) for hm in hms]
        kaugs = [jnp.where(hm, kn, one) for hm in hms]
        vmasks = [jnp.where(hm, vv, zero) for hm in hms]
        fks = [fct_ref[0, pl.ds(2 * p + hh, 1), rk] for hh in range(2)]

        def qloop(i, carry, diagonal):
            ri = pl.ds(pl.multiple_of(i * tq, tq), tq)
            qn = qn_s[ri, :]
            do = do_s[ri, :]
            fcq = fc_ref[ri, :]
            new = []
            for hh in range(2):
                dk_acc, dv_acc = carry[hh]
                c0 = FOX_D * hh
                fq = jnp.sum(jnp.where(lane == 2 * p + hh, fcq, 0.0), axis=-1, keepdims=True)
                pr = jnp.exp(_nt(qn, kmasks[hh]) + fq - fks[hh] - lse_ref[ri, c0:c0 + 1])
                if diagonal:
                    pr = jnp.where(causal, pr, 0.0)
                ds = (pr * (_nt(do, vmasks[hh]) - delta_s[ri, c0:c0 + 1])).astype(BF16)
                dq_s[hh, ri, :] += _nn(ds, kaugs[hh])
                new.append((dk_acc + _tn(jnp.where(hms[hh], qn, one), ds), dv_acc + _tn(do, pr.astype(BF16))))
            return tuple(new)

        init = tuple((jnp.zeros((LANE, tq), F32), jnp.zeros((LANE, tq), F32)) for _ in range(2))
        carry = qloop(kj, init, True)
        (dk0, dv0), (dk1, dv1) = lax.fori_loop(kj + 1, NQ, functools.partial(qloop, diagonal=False), carry)
        dks, dvs = [dk0.T, dk1.T], [dv0.T, dv1.T]

        dkn = jnp.where(lo_half, dks[0], dks[1])
        _, _, _, _, kh, _, rkk = _fox_prep(z_ref, gq_v, gk_v, rk, lo_half)
        u = dkn * gk_v
        dz_ref[rk, LANE:2 * LANE] = (rkk * (u - kh * _pair_mean(u * kh, lo_half))).astype(BF16)
        dz_ref[rk, 2 * LANE:3 * LANE] = jnp.where(lo_half, dvs[0], dvs[1]).astype(BF16)
        dgk_ref[...] += _rowsum8(dkn * kh)
        dfk_s[rk, :] = jnp.where(lane == 2 * p, -dks[0][:, FOX_D:FOX_D + 1],
                                 jnp.where(lane == 2 * p + 1, -dks[1][:, 0:1], 0.0))

        @pl.when(kj == NQ - 1)
        def _():
            def fin(i, carry):
                r = pl.ds(pl.multiple_of(i * tq, tq), tq)
                d0, d1 = dq_s[0, r, :], dq_s[1, r, :]
                dqn = jnp.where(lo_half, d0, d1)
                _, _, _, qh, _, rqq, _ = _fox_prep(z_ref, gq_v, gk_v, r, lo_half)
                u = dqn * gq_v * (FOX_D ** -0.5)
                dz_ref[r, 0:LANE] = (rqq * (u - qh * _pair_mean(u * qh, lo_half))).astype(BF16)
                dgq_ref[...] += _rowsum8(dqn * qh) * (FOX_D ** -0.5)
                dfc_ref[r, :] = dfk_s[r, :] + jnp.where(lane == 2 * p, d0[:, FOX_D:FOX_D + 1],
                                                        jnp.where(lane == 2 * p + 1, d1[:, 0:1], 0.0))
                return carry
            lax.fori_loop(0, NQ, fin, 0)

    vec = pl.BlockSpec((1, LANE), lambda b, p, k: (0, 0))
    col = pl.BlockSpec((T, LANE), lambda b, p, k: (b, p))
    part = pl.BlockSpec((8, LANE), lambda b, p, k: (b * FOX_P + p, 0))
    if nsw:
        body = _hosting(body, 8, 4, 7, nsw, _chip_swap_phases, (B, FOX_P, NQ))
    return pl.pallas_call(
        body, name="fox_bwd", grid=(B, FOX_P, NQ),
        in_specs=[pl.BlockSpec((T, 384), lambda b, p, k: (b, p)), col, col, col,
                  pl.BlockSpec((T, LANE), lambda b, p, k: (b, 0)), pl.BlockSpec((1, 8, T), lambda b, p, k: (b, 0, 0)),
                  vec, vec] + [ANY] * nsw,
        out_specs=[pl.BlockSpec((T, 384), lambda b, p, k: (b, p)), col, part, part] + [ANY] * nsw,
        out_shape=[S((N, 1536), BF16), S((N, 512), F32), S((B * FOX_P * 8, LANE), F32), S((B * FOX_P * 8, LANE), F32)]
        + [S(p.shape, p.dtype) for p in swap],
        scratch_shapes=[pltpu.VMEM((T, LANE), BF16)] * 4 + [pltpu.VMEM((T, LANE), F32), pltpu.VMEM((2, T, LANE), F32),
                                                            pltpu.VMEM((T, LANE), F32)]
        + (_chip_swap_sems(nsw) if nsw else []),
        compiler_params=_cp(("arbitrary",) * 3 if nsw else ("parallel", "parallel", "arbitrary")),
    )(z, dy, y, lse, fc, fct, gq, gk, *swap)


def _mem_scores(z_ref, kv_ref, gq, gk, h):
    c = slice(MEM_D * h, MEM_D * (h + 1))
    q, k = z_ref[:, c].astype(F32), kv_ref[:, c]
    rq = lax.rsqrt(jnp.mean(q * q, axis=-1, keepdims=True) + EPS)
    rk = lax.rsqrt(jnp.mean(k * k, axis=-1, keepdims=True) + EPS)
    qh, kh = q * rq, k * rk
    qn = (qh * gq * (MEM_D ** -0.5)).astype(BF16)
    kn = (kh * gk).astype(BF16)
    s = _nt(qn, kn)
    pe = jnp.exp(s - jnp.max(s, axis=-1, keepdims=True))
    pn = pe / jnp.sum(pe, axis=-1, keepdims=True)
    return pn, qn, kn, qh, kh, rq, rk


def _mem_fwd(z, memkv, gq, gk, B, T, M, tq=1024):
    N = B * T
    tq = min(tq, T)
    NQ = T // tq
    W = MEM_H * MEM_D

    def body(z_ref, kv_ref, gq_ref, gk_ref, y_ref):
        for h in range(MEM_H):
            pn = _mem_scores(z_ref, kv_ref, gq_ref[...], gk_ref[...], h)[0]
            v = kv_ref[:, W + MEM_D * h:W + MEM_D * (h + 1)].astype(BF16)
            y_ref[:, MEM_D * h:MEM_D * (h + 1)] = _nn(pn.astype(BF16), v).astype(BF16)

    vec = pl.BlockSpec((1, LANE), lambda b, q: (0, 0))
    return pl.pallas_call(
        body, name="mem_fwd", grid=(B, NQ),
        in_specs=[pl.BlockSpec((tq, W), lambda b, q: (b * NQ + q, C_MQ // W)),
                  pl.BlockSpec((M, 2 * W), lambda b, q: (b, 0)), vec, vec],
        out_specs=pl.BlockSpec((tq, W), lambda b, q: (b * NQ + q, 0)), out_shape=S((N, W), BF16),
        compiler_params=_cp(("parallel", "parallel")),
    )(z, memkv, gq, gk)


def _mem_bwd(z, memkv, dy, gq, gk, B, T, M, tq=1024):
    N = B * T
    tq = min(tq, T)
    NQ = T // tq
    W = MEM_H * MEM_D

    def body(z_ref, kv_ref, dy_ref, gq_ref, gk_ref, dz_ref, dkv_ref, dgq_ref, dgk_ref, acc):
        qi = pl.program_id(1)
        gq_v, gk_v = gq_ref[...], gk_ref[...]

        @pl.when(qi == 0)
        def _():
            acc[...] = jnp.zeros_like(acc)
            dgq_ref[...] = jnp.zeros_like(dgq_ref)
            dgk_ref[...] = jnp.zeros_like(dgk_ref)

        for h in range(MEM_H):
            c = slice(MEM_D * h, MEM_D * (h + 1))
            cv = slice(W + MEM_D * h, W + MEM_D * (h + 1))
            pn, qn, kn, qh, _, rq, _ = _mem_scores(z_ref, kv_ref, gq_v, gk_v, h)
            do = dy_ref[:, c].astype(BF16)
            dp = _nt(do, kv_ref[:, cv].astype(BF16))
            ds = (pn * (dp - jnp.sum(dp * pn, axis=-1, keepdims=True))).astype(BF16)
            dqn = _nn(ds, kn)
            acc[:, c] += _tn(ds, qn)
            acc[:, cv] += _tn(pn.astype(BF16), do)
            u = dqn * gq_v * (MEM_D ** -0.5)
            dz_ref[:, c] = (rq * (u - qh * jnp.mean(u * qh, axis=-1, keepdims=True))).astype(BF16)
            dgq_ref[...] += _rowsum8(dqn * qh) * (MEM_D ** -0.5)

        @pl.when(qi == NQ - 1)
        def _():
            for h in range(MEM_H):
                c = slice(MEM_D * h, MEM_D * (h + 1))
                cv = slice(W + MEM_D * h, W + MEM_D * (h + 1))
                k = kv_ref[:, c]
                rk = lax.rsqrt(jnp.mean(k * k, axis=-1, keepdims=True) + EPS)
                kh = k * rk
                dkn = acc[:, c]
                u = dkn * gk_v
                dkv_ref[:, c] = (rk * (u - kh * jnp.mean(u * kh, axis=-1, keepdims=True))).astype(BF16)
                dkv_ref[:, cv] = acc[:, cv].astype(BF16)
                dgk_ref[...] += _rowsum8(dkn * kh)

    vec = pl.BlockSpec((1, LANE), lambda b, q: (0, 0))
    part = pl.BlockSpec((8, LANE), lambda b, q: (b, 0))
    return pl.pallas_call(
        body, name="mem_bwd", grid=(B, NQ),
        in_specs=[pl.BlockSpec((tq, W), lambda b, q: (b * NQ + q, C_MQ // W)),
                  pl.BlockSpec((M, 2 * W), lambda b, q: (b, 0)), pl.BlockSpec((tq, W), lambda b, q: (b * NQ + q, 0)),
                  vec, vec],
        out_specs=[pl.BlockSpec((tq, W), lambda b, q: (b * NQ + q, 0)), pl.BlockSpec((M, 2 * W), lambda b, q: (b, 0)),
                   part, part],
        out_shape=[S((N, W), BF16), S((B * M, 2 * W), BF16), S((B * 8, LANE), F32), S((B * 8, LANE), F32)],
        scratch_shapes=[pltpu.VMEM((M, 2 * W), F32)], compiler_params=_cp(("parallel", "arbitrary")),
    )(z, memkv, dy, gq, gk)


def _merge_fwd(ya, yb, yc, z, x, wa, wb, wc, wo, g_next, tm=512):
    n, d = x.shape
    wdt = ya.shape[1]
    gb = C_GATE // d

    def body(ya_ref, yb_ref, yc_ref, g0_ref, g1_ref, g2_ref, x_ref, wa_ref, wb_ref, wc_ref, wo_ref, gn_ref,
             x1_ref, mg_ref, ua_ref, ub_ref, uc_ref, h_ref):
        merged = jnp.zeros((tm, d), F32)
        for y_ref, g_ref, w_ref, u_ref in ((ya_ref, g0_ref, wa_ref, ua_ref), (yb_ref, g1_ref, wb_ref, ub_ref),
                                           (yc_ref, g2_ref, wc_ref, uc_ref)):
            u = _nn(y_ref[...], w_ref[...])
            u_ref[...] = u.astype(BF16)
            merged = merged + jax.nn.sigmoid(g_ref[...].astype(F32)) * u
        mb = merged.astype(BF16)
        mg_ref[...] = mb
        x1 = x_ref[...] + _nn(mb, wo_ref[...])
        x1_ref[...] = x1
        h_ref[...] = (x1 * lax.rsqrt(jnp.mean(x1 * x1, axis=-1, keepdims=True) + EPS) * gn_ref[...]).astype(BF16)

    yt = pl.BlockSpec((tm, wdt), lambda i: (i, 0))
    xt = pl.BlockSpec((tm, d), lambda i: (i, 0))
    wbr = pl.BlockSpec((wdt, d), lambda i: (0, 0))
    gates = [pl.BlockSpec((tm, d), functools.partial(lambda i, k: (i, gb + k), k=k)) for k in range(3)]
    return pl.pallas_call(
        body, name="merge_fwd", grid=(n // tm,),
        in_specs=[yt, yt, yt] + gates + [xt, wbr, wbr, wbr, pl.BlockSpec((d, d), lambda i: (0, 0)),
                                         pl.BlockSpec((1, d), lambda i: (0, 0))],
        out_specs=[xt] * 6, out_shape=[S((n, d), F32)] + [S((n, d), BF16)] * 5, compiler_params=_cp(("parallel",)),
    )(ya, yb, yc, z, z, z, x, wa, wb, wc, wo, g_next)


def _merge_bwd(dx1, z, ua, ub, uc, wa, wb, wc, wo, tm=512):
    n, d = dx1.shape
    wdt = wa.shape[0]
    gb = C_GATE // d

    def body(dx_ref, g0_ref, g1_ref, g2_ref, ua_ref, ub_ref, uc_ref, wa_ref, wb_ref, wc_ref, wo_ref,
             dg_ref, dya_ref, dyb_ref, dyc_ref, dua_ref, dub_ref, duc_ref):
        dm = _nt(dx_ref[...].astype(BF16), wo_ref[...])
        for k, (g_ref, u_ref, w_ref, dy_ref, du_ref) in enumerate((
                (g0_ref, ua_ref, wa_ref, dya_ref, dua_ref), (g1_ref, ub_ref, wb_ref, dyb_ref, dub_ref),
                (g2_ref, uc_ref, wc_ref, dyc_ref, duc_ref))):
            g = jax.nn.sigmoid(g_ref[...].astype(F32))
            du = (dm * g).astype(BF16)
            du_ref[...] = du
            dg_ref[:, d * k:d * (k + 1)] = (dm * u_ref[...].astype(F32) * g * (1.0 - g)).astype(BF16)
            dy_ref[...] = _nt(du, w_ref[...])

    yt = pl.BlockSpec((tm, wdt), lambda i: (i, 0))
    xt = pl.BlockSpec((tm, d), lambda i: (i, 0))
    wbr = pl.BlockSpec((wdt, d), lambda i: (0, 0))
    gates = [pl.BlockSpec((tm, d), functools.partial(lambda i, k: (i, gb + k), k=k)) for k in range(3)]
    return pl.pallas_call(
        body, name="merge_bwd", grid=(n // tm,),
        in_specs=[xt] + gates + [xt, xt, xt, wbr, wbr, wbr, pl.BlockSpec((d, d), lambda i: (0, 0))],
        out_specs=[pl.BlockSpec((tm, 3 * d), lambda i: (i, 0)), yt, yt, yt, xt, xt, xt],
        out_shape=[S((n, 3 * d), BF16)] + [S((n, wdt), F32)] * 3 + [S((n, d), BF16)] * 3,
        compiler_params=_cp(("parallel",)),
    )(dx1, z, z, z, ua, ub, uc, wa, wb, wc, wo)


FFN_TN = 1408
TN_TM = 2048
INV_SQRT2 = 0.7071067811865476
INV_SQRT_2PI = 0.3989422804014327


def _conv_shifted(a, prev, first, tm):
    row = _iota(a.shape, 0)
    p7 = jnp.where(first, 0.0, prev[7:8, :])
    p6 = jnp.where(first, 0.0, prev[6:7, :])
    a1 = jnp.where(row == 0, p7, pltpu.roll(a, 1, 0))
    a2 = jnp.where(row == 0, p6, jnp.where(row == 1, p7, pltpu.roll(a, 2, 0)))
    return a1, a2


def _ffn_act_fwd(up, cw, cb, B, T, tm=1024):
    N = B * T
    tm = min(tm, T)
    dff = cw.shape[1]
    NT, NJ, tn = T // tm, dff // FFN_TN, FFN_TN

    def body(a_ref, v_ref, cw_ref, cb_ref, y_ref, c_ref, carry):
        t = pl.program_id(2)
        a = a_ref[...].astype(F32)
        a1, a2 = _conv_shifted(a, carry[...], t == 0, tm)
        w = cw_ref[...]
        ac = w[0:1, :] * a2 + w[1:2, :] * a1 + w[2:3, :] * a + cb_ref[...]
        cdf = 0.5 * (1.0 + lax.erf(ac * INV_SQRT2))
        y_ref[...] = (ac * cdf * v_ref[...].astype(F32)).astype(BF16)
        c_ref[...] = cdf.astype(BF16)
        carry[...] = a[tm - 8:tm, :]

    return pl.pallas_call(
        body, name="ffn_act_fwd", grid=(B, NJ, NT),
        in_specs=[pl.BlockSpec((tm, tn), lambda b, j, t: (b * NT + t, j)),
                  pl.BlockSpec((tm, tn), lambda b, j, t: (b * NT + t, NJ + j)),
                  pl.BlockSpec((3, tn), lambda b, j, t: (0, j)), pl.BlockSpec((1, tn), lambda b, j, t: (0, j))],
        out_specs=[pl.BlockSpec((tm, tn), lambda b, j, t: (b * NT + t, j))] * 2, out_shape=[S((N, dff), BF16)] * 2,
        scratch_shapes=[pltpu.VMEM((8, tn), F32)], compiler_params=_cp(("parallel", "parallel", "arbitrary")),
    )(up, up, cw, cb)


def _ffn_down_loss(y, wd, x1, tgt, tm=512):
    n, d = x1.shape
    kf = y.shape[1]

    def body(y_ref, w_ref, x_ref, t_ref, dx_ref, ls_ref):
        err = x_ref[...] + _nn(y_ref[...], w_ref[...]) - t_ref[...]
        dx_ref[...] = err * (1.0 / d)

        @pl.when(pl.program_id(0) == 0)
        def _():
            ls_ref[...] = jnp.zeros_like(ls_ref)

        ls_ref[...] += _rowsum8(err * err) * (0.5 / d)

    xt = pl.BlockSpec((tm, d), lambda i: (i, 0))
    return pl.pallas_call(
        body, name="ffn_down_loss", grid=(n // tm,),
        in_specs=[pl.BlockSpec((tm, kf), lambda i: (i, 0)), pl.BlockSpec((kf, d), lambda i: (0, 0)), xt, xt],
        out_specs=[xt, pl.BlockSpec((8, d), lambda i: (0, 0))], out_shape=[S((n, d), F32), S((8, d), F32)],
        compiler_params=_cp(("arbitrary",)),
    )(y, wd, x1, tgt)


def _ffn_act_bwd1(dx2, wd, up, cdf, cw, cb, B, T, tm=512):
    N = B * T
    tm = min(tm, T)
    d = dx2.shape[1]
    dff = cw.shape[1]
    NT, NJ, tn = T // tm, dff // FFN_TN, FFN_TN

    def body(dx_ref, w_ref, a_ref, v_ref, c_ref, cw_ref, cb_ref, dac_ref, dv_ref, dcw_ref, dcb_ref, carry):
        b, t = pl.program_id(1), pl.program_id(2)
        a = a_ref[...].astype(F32)
        a1, a2 = _conv_shifted(a, carry[...], t == 0, tm)
        carry[...] = a[tm - 8:tm, :]
        w = cw_ref[...]
        ac = w[0:1, :] * a2 + w[1:2, :] * a1 + w[2:3, :] * a + cb_ref[...]
        dy = _nt(dx_ref[...].astype(BF16), w_ref[...])
        cdf = c_ref[...].astype(F32)
        dv_ref[...] = (dy * ac * cdf).astype(BF16)
        dac = dy * v_ref[...].astype(F32) * (cdf + ac * jnp.exp(-0.5 * ac * ac) * INV_SQRT_2PI)
        dac_ref[...] = dac

        @pl.when((b == 0) & (t == 0))
        def _():
            dcw_ref[...] = jnp.zeros_like(dcw_ref)
            dcb_ref[...] = jnp.zeros_like(dcb_ref)

        dcw_ref[0:8, :] += _rowsum8(dac * a2)
        dcw_ref[8:16, :] += _rowsum8(dac * a1)
        dcw_ref[16:24, :] += _rowsum8(dac * a)
        dcb_ref[...] += _rowsum8(dac)

    return pl.pallas_call(
        body, name="ffn_act_bwd1", grid=(NJ, B, NT),
        in_specs=[pl.BlockSpec((tm, d), lambda j, b, t: (b * NT + t, 0)), pl.BlockSpec((tn, d), lambda j, b, t: (j, 0)),
                  pl.BlockSpec((tm, tn), lambda j, b, t: (b * NT + t, j)),
                  pl.BlockSpec((tm, tn), lambda j, b, t: (b * NT + t, NJ + j)),
                  pl.BlockSpec((tm, tn), lambda j, b, t: (b * NT + t, j)),
                  pl.BlockSpec((3, tn), lambda j, b, t: (0, j)), pl.BlockSpec((1, tn), lambda j, b, t: (0, j))],
        out_specs=[pl.BlockSpec((tm, tn), lambda j, b, t: (b * NT + t, j)),
                   pl.BlockSpec((tm, tn), lambda j, b, t: (b * NT + t, j)),
                   pl.BlockSpec((24, tn), lambda j, b, t: (0, j)), pl.BlockSpec((8, tn), lambda j, b, t: (0, j))],
        out_shape=[S((N, dff), F32), S((N, dff), BF16), S((24, dff), F32), S((8, dff), F32)],
        scratch_shapes=[pltpu.VMEM((8, tn), F32)], compiler_params=_cp(("parallel", "arbitrary", "arbitrary")),
    )(dx2, wd, up, up, cdf, cw, cb)


def _ffn_act_bwd2(dac, cw, B, T, tm=1024):
    N = B * T
    tm = min(tm, T)
    dff = cw.shape[1]
    NT, NJ, tn = T // tm, dff // FFN_TN, FFN_TN
    last8 = N // 8 - 1

    def body(d_ref, nx_ref, cw_ref, da_ref):
        t = pl.program_id(2)
        dd = d_ref[...]
        row = _iota(dd.shape, 0)
        last = t == NT - 1
        n0 = jnp.where(last, 0.0, nx_ref[0:1, :])
        n1 = jnp.where(last, 0.0, nx_ref[1:2, :])
        d1 = jnp.where(row == tm - 1, n0, pltpu.roll(dd, tm - 1, 0))
        d2 = jnp.where(row == tm - 1, n1, jnp.where(row == tm - 2, n0, pltpu.roll(dd, tm - 2, 0)))
        w = cw_ref[...]
        da_ref[...] = (w[2:3, :] * dd + w[1:2, :] * d1 + w[0:1, :] * d2).astype(BF16)

    return pl.pallas_call(
        body, name="ffn_act_bwd2", grid=(B, NJ, NT),
        in_specs=[pl.BlockSpec((tm, tn), lambda b, j, t: (b * NT + t, j)),
                  pl.BlockSpec((8, tn), lambda b, j, t: (jnp.minimum((b * NT + t + 1) * (tm // 8), last8), j)),
                  pl.BlockSpec((3, tn), lambda b, j, t: (0, j))],
        out_specs=pl.BlockSpec((tm, tn), lambda b, j, t: (b * NT + t, j)), out_shape=S((N, dff), BF16),
        compiler_params=_cp(("parallel", "parallel", "parallel")),
    )(dac, dac, cw)


def _fold_rows(p, name):
    r, c = p.shape[0] // 8, p.shape[1]

    def body(p_ref, o_ref):
        for j in range(r):
            o_ref[j:j + 1, :] = jnp.sum(p_ref[8 * j:8 * (j + 1), :], axis=0, keepdims=True)

    return pl.pallas_call(body, name=name, out_shape=S((r, c), F32), compiler_params=_cp())(p)


def _small_reduce(lbl, dg_mix, dg_mem, dlb_p, dgn_p, dfb_p, dgq_p, dgk_p, dmq_p, dmk_p, dg_ffn, dcb_p, loss_p):
    d, dff = dg_mix.shape[1], dcb_p.shape[1]
    nbh = dlb_p.shape[0] // (8 * HG_H)

    def colsum(ref):
        return jnp.sum(ref[...], axis=0, keepdims=True)

    def body(lbl_ref, mix_ref, mem_ref, dlb_ref, dgn_ref, dfb_ref, dgq_ref, dgk_ref, dmq_ref, dmk_ref, ffn_ref, dcb_ref,
             ls_ref, o_mix, o_mem, o_lb, o_hgn, o_fb, o_fq, o_fk, o_mq, o_mk, o_ffn, o_cb, o_loss):
        o_mix[...], o_mem[...], o_ffn[...], o_cb[...] = colsum(mix_ref), colsum(mem_ref), colsum(ffn_ref), colsum(dcb_ref)
        o_hgn[...], o_fb[...], o_mq[...], o_mk[...] = colsum(dgn_ref), colsum(dfb_ref), colsum(dmq_ref), colsum(dmk_ref)
        for src, dst in ((dgq_ref, o_fq), (dgk_ref, o_fk)):
            v = colsum(src)
            dst[...] = v + pltpu.roll(v, FOX_D, 1)
        o_loss[...] = jnp.zeros((1, LANE), F32) + jnp.sum(colsum(ls_ref), axis=-1, keepdims=True)
        logits = lbl_ref[...]
        e = jnp.exp(logits - jnp.max(logits, axis=0, keepdims=True))
        pr = e / jnp.sum(e, axis=0, keepdims=True)
        rows = _iota((8, LANE), 0)
        for h in range(HG_H):
            acc = jnp.zeros((8, LANE), F32)
            for b in range(nbh):
                acc = acc + dlb_ref[8 * (b * HG_H + h):8 * (b * HG_H + h + 1), :]
            dlb = jnp.sum(acc, axis=0, keepdims=True)
            c = slice(LANE * h, LANE * (h + 1))
            p0 = pr[0:1, c]
            first = _iota((logits.shape[0], LANE), 0) == 0
            o_lb[:, c] = pr[:, c] * (jnp.where(first, 1.0, 0.0) - p0) * dlb

    outs = [S((1, d), F32), S((1, d), F32), S(lbl.shape, F32)] + [S((1, LANE), F32)] * 6 + \
           [S((1, d), F32), S((1, dff), F32), S((1, LANE), F32)]
    return pl.pallas_call(body, name="small_reduce", out_shape=outs, compiler_params=_cp())(
        lbl, dg_mix, dg_mem, dlb_p, dgn_p, dfb_p, dgq_p, dgk_p, dmq_p, dmk_p, dg_ffn, dcb_p, loss_p)


def _in_col_pieces():
    hw, fw = HG_H * HG_D, FOX_H * FOX_D
    fox0, ff0 = 4 * hw, 4 * hw + 3 * fw
    mq0 = ff0 + FOX_H
    gate0 = mq0 + MEM_H * MEM_D
    pieces = []
    for p in range(FOX_P):
        pieces += [(fox0 + j * fw + LANE * p, LANE) for j in range(3)]
    pieces.append((mq0, MEM_H * MEM_D))
    for h in range(HG_H):
        pieces += [(j * hw + HG_D * h, HG_D) for j in range(4)]
    pieces.append((gate0, C_FF - C_GATE))
    pieces.append((ff0, FOX_H))
    return pieces


def _perm_from_blocks(blocks):
    n_blk, _, c = blocks.shape
    parts = []
    for s, n in _in_col_pieces():
        lo = s
        while lo < s + n:
            d = lo // c
            hi = min(s + n, (d + 1) * c)
            parts.append(blocks[d][:, lo - d * c:hi - d * c])
            lo = hi
    parts.append(jnp.zeros((blocks.shape[1], C_END - C_FF - FOX_H), blocks.dtype))
    return jnp.concatenate(parts, axis=1)


def _unperm_blocks(segs, n_blk):
    starts = [0]
    for a in segs:
        starts.append(starts[-1] + a.shape[1])
    new_start, placed = 0, []
    for s, n in _in_col_pieces():
        placed.append((s, new_start, n))
        new_start += n
    placed.sort()
    c = sum(n for _, _, n in placed) // n_blk
    blocks = []
    for d in range(n_blk):
        parts = []
        for s, ns, n in placed:
            lo, hi = max(s, d * c), min(s + n, (d + 1) * c)
            if lo < hi:
                i = max(j for j in range(len(segs)) if starts[j] <= ns)
                parts.append(segs[i][:, ns + lo - s - starts[i]:ns + hi - s - starts[i]])
        blocks.append(jnp.concatenate(parts, axis=1))
    return jnp.stack(blocks)


def _local_step(x2, mem2, tgt, sm, W, B, T, M, ex=None):
    fbias = jnp.pad(sm["fox_f_bias"], ((0, 0), (0, LANE - FOX_H)))
    gq2 = jnp.concatenate([sm["fox_q_norm_g"]] * 2, axis=1)
    gk2 = jnp.concatenate([sm["fox_k_norm_g"]] * 2, axis=1)
    lbl = sm["hgrn_lb_logits"]
    if ex:
        h, *first = _rmsnorm_cast(x2, sm["norm_mix_g"], "norm_mix", gather=ex.first_blocks())
        W = ex.unpack_first(first)
    else:
        h = _rmsnorm_cast(x2, sm["norm_mix_g"], "norm_mix")
    z = _mm_nn(h, W["w_in"], BF16, "proj_in", 512, C_END)
    memn = _rmsnorm_cast(mem2, sm["norm_mem_g"], "norm_mem", tm=256)
    memkv = _mm_nn(memn, W["mem_kv_w"], F32, "proj_memkv", 256, 512)
    ya, o_raw, states, a_mat = _hgrn_fwd(z, lbl, sm["hgrn_norm_g"], B, T)
    fc, fct = _fox_gate_fwd(z, fbias, B, T)
    yb, lse, *late = _fox_fwd(z, fc, fct, gq2, gk2, B, T, gather=ex.late_blocks() if ex else ())
    if ex:
        W = {**W, **ex.unpack_late(late)}
    yc = _mem_fwd(z, memkv, sm["mem_q_norm_g"], sm["mem_k_norm_g"], B, T, M)
    x1, merged, ua, ub, uc, h2 = _merge_fwd(ya, yb, yc, z, x2, W["w_br_hgrn"], W["w_br_fox"], W["w_br_mem"], W["w_out"],
                                            sm["norm_ffn_g"])
    up = _mm_nn(h2, W["ffn_w_up"], BF16, "ffn_up", 512, 4 * FFN_TN)
    yf, cdf = _ffn_act_fwd(up, W["ffn_conv_w"], sm["ffn_conv_b"], B, T)
    dx2, loss_p = _ffn_down_loss(yf, W["ffn_w_down"], x1, tgt)
    dff = W["ffn_conv_w"].shape[1]
    dac, dv, dcw_p, dcb_p = _ffn_act_bwd1(dx2, W["ffn_w_down"], up, cdf, W["ffn_conv_w"], sm["ffn_conv_b"], B, T)
    da = _ffn_act_bwd2(dac, W["ffn_conv_w"], B, T)
    g = {"ffn_conv_w": _fold_rows(dcw_p, "g_conv_w")}
    g["ffn_w_down"] = _mm_tn(yf, dx2, "g_w_down", 1024, 1024)
    dh2 = _mm_nt_sum([(da, 0, dff, 0), (dv, 0, dff, dff)], W["ffn_w_up"], "dh2", 512)
    g["ffn_w_up"] = [_mm_tn(h2, da, "g_w_up_a", 1024, dff), _mm_tn(h2, dv, "g_w_up_v", 1024, dff)]
    dx1, dg_ffn = _rmsnorm_bwd(dh2, x1, sm["norm_ffn_g"], dx2, "norm_ffn_bwd")
    g["w_out"] = _mm_tn(merged, dx1, "g_w_out", TN_TM, 1024)
    dgate, dya, dyb, dyc, dua, dub, duc = _merge_bwd(dx1, z, ua, ub, uc, W["w_br_hgrn"], W["w_br_fox"], W["w_br_mem"],
                                                    W["w_out"])
    g["w_br_hgrn"] = _mm_tn(ya, dua, "g_w_br_hgrn", TN_TM, 1024)
    g["w_br_fox"] = _mm_tn(yb, dub, "g_w_br_fox", TN_TM, 1024)
    g["w_br_mem"] = _mm_tn(yc, duc, "g_w_br_mem", TN_TM, 1024)
    early_pk = ex.early_grads(g) if ex else ()
    dz_hg, dlb_p, dgn_p, *early_sib = _hgrn_bwd(z, o_raw, states, a_mat, dya, lbl, sm["hgrn_norm_g"], B, T,
                                                swap_sibling=early_pk)
    dz_fox, dfc, dgq_p, dgk_p, *early_chips = _fox_bwd(z, dyb, yb, lse, fc, fct, gq2, gk2, B, T,
                                                       swap=ex.pair_sums(early_pk, early_sib, "early") if ex else ())
    dz_ff, dfb_p = _fox_gate_bwd(dfc, z, fbias, B, T)
    dz_mq, dkv, dmq_p, dmk_p = _mem_bwd(z, memkv, dyc, sm["mem_q_norm_g"], sm["mem_k_norm_g"], B, T, M)
    g["mem_kv_w"] = _mm_tn(memn, dkv, "g_mem_kv_w", 256, 512)
    dmemn = _mm_nt_sum([(dkv, 0, dkv.shape[1], 0)], W["mem_kv_w"], "d_memn", 256)
    _, dg_mem = _rmsnorm_bwd(dmemn, mem2, sm["norm_mem_g"], None, "norm_mem_bwd", tm=256)
    d = x2.shape[1]
    parts = [(dz_fox, 0, C_MQ - C_FOX, C_FOX), (dz_mq, 0, C_HG - C_MQ, C_MQ), (dz_hg, 0, C_GATE - C_HG, C_HG)]
    parts += [(dgate, d * k, d, C_GATE + d * k) for k in range(3)] + [(dz_ff, 0, C_END - C_FF, C_FF)]
    g["w_in"] = [_mm_tn(h, dzs, "g_w_in_%d" % i, TN_TM,
                        max(t for t in (1536, 1024, 512, LANE) if dzs.shape[1] % t == 0))
                 for i, dzs in enumerate((dz_fox, dz_mq, dz_hg, dgate, dz_ff))]
    sums = None
    if ex:
        last_pk = ex.last_grads(g)
        last_sib = _swap_with_sibling(last_pk, "rs_sibling_last")
        dh, last_chips = _mm_nt_sum(parts, W["w_in"], "dh", 512, swap=ex.pair_sums(last_pk, last_sib, "last"))
        sums = (ex.final_sums(early_pk, early_sib, early_chips, "early"),
                ex.final_sums(last_pk, last_sib, last_chips, "last"))
    else:
        dh = _mm_nt_sum(parts, W["w_in"], "dh", 512)
    grad_x, dg_mix = _rmsnorm_bwd(dh, x2, sm["norm_mix_g"], dx1, "norm_mix_bwd")
    small = _small_reduce(lbl, dg_mix, dg_mem, dlb_p, dgn_p, dfb_p, dgq_p, dgk_p, dmq_p, dmk_p, dg_ffn, dcb_p, loss_p)
    names = ("norm_mix_g", "norm_mem_g", "hgrn_lb_logits", "hgrn_norm_g", "fox_f_bias", "fox_q_norm_g", "fox_k_norm_g",
             "mem_q_norm_g", "mem_k_norm_g", "norm_ffn_g", "ffn_conv_b", "loss")
    g.update(dict(zip(names, small)))
    return grad_x, g, sums


ANY = pl.BlockSpec(memory_space=pl.ANY)


def _position():
    return lax.axis_index("x"), lax.axis_index("y"), lax.axis_index("c")


def _all_gather(blocks, name):
    nb = len(blocks)

    def body(*refs):
        start, forward, finish = _gather_phases(refs[:nb], refs[nb:2 * nb], *refs[2 * nb:])
        start()
        forward()
        finish()

    return pl.pallas_call(
        body, name=name, out_shape=_gather_shapes(blocks), in_specs=[ANY] * nb, out_specs=[ANY] * nb,
        scratch_shapes=_gather_sems(nb),
    )(*blocks)


def _hosting(body, n_in, n_out, n_scratch, n_x, make_phases, grid):
    n_steps = math.prod(grid)

    def hosted(*refs):
        a = n_in + n_x
        b = a + n_out + n_x
        ins, xs = refs[:n_in], refs[n_in:a]
        outs, x_outs = refs[a:a + n_out], refs[a + n_out:b]
        scratch, sems = refs[b:b + n_scratch], refs[b + n_scratch:]
        step = 0
        for ax, n in enumerate(grid):
            step = step * n + pl.program_id(ax)
        phases = make_phases(xs, x_outs, *sems)
        pl.when(step == 0)(phases[0])
        for ph in phases[1:-1]:
            pl.when(step == n_steps // 2)(ph)
        body(*ins, *outs, *scratch)
        pl.when(step == n_steps - 1)(phases[-1])

    return hosted


def _gather_shapes(blocks):
    return [S((N_DEV,) + b.shape, b.dtype) for b in blocks]


def _gather_sems(nb):
    return [pltpu.SemaphoreType.DMA((7 * nb,)), pltpu.SemaphoreType.DMA((7 * nb,)), pltpu.SemaphoreType.DMA((nb,))]


def _gather_phases(x_refs, out_refs, send_sems, recv_sems, local_sems):
    nb = len(x_refs)
    x, y, c = _position()
    me, sibling = (x, y, c), (x, y, 1 - c)
    chips = [(1 - x, y), (x, 1 - y), (1 - x, 1 - y)]

    def copy(i, k, blk, to, own=False):
        px, py, pc = blk
        slot = out_refs[i].at[4 * px + 2 * py + pc]
        return pltpu.make_async_remote_copy(
            src_ref=x_refs[i] if own else slot, dst_ref=slot, send_sem=send_sems.at[7 * i + k],
            recv_sem=recv_sems.at[7 * i + k], device_id=to, device_id_type=MESH)

    def mine(i):
        return pltpu.make_async_copy(x_refs[i], out_refs[i].at[4 * x + 2 * y + c], local_sems.at[i])

    def first(i):
        return [copy(i, 0, me, sibling, own=True)] + [copy(i, 1 + j, me, (*chip, c), own=True)
                                                     for j, chip in enumerate(chips)]

    def passed(i, j):
        return copy(i, 4 + j, (*chips[j], c), sibling)

    def start():
        for i in range(nb):
            mine(i).start()
            for cp in first(i):
                cp.start()

    def forward():
        for i in range(nb):
            for j, chip in enumerate(chips):
                copy(i, 1 + j, (*chip, c), me).wait_recv()
                passed(i, j).start()

    def finish():
        for i in range(nb):
            copy(i, 0, sibling, me).wait_recv()
            for j, chip in enumerate(chips):
                copy(i, 4 + j, (*chip, 1 - c), me).wait_recv()
        for i in range(nb):
            for cp in first(i) + [passed(i, j) for j in range(3)]:
                cp.wait_send()
            mine(i).wait()

    return start, forward, finish


def _swap_with_sibling(pks, name):
    nb = len(pks)

    def body(*refs):
        start, finish = _sibling_swap_phases(refs[:nb], refs[nb:2 * nb], *refs[2 * nb:])
        start()
        finish()

    return pl.pallas_call(
        body, name=name, out_shape=_sibling_swap_shapes(pks), in_specs=[ANY] * nb, out_specs=[ANY] * nb,
        scratch_shapes=_sibling_swap_sems(nb),
    )(*pks)


def _sibling_swap_shapes(pks):
    return [S((4,) + p.shape[1:], p.dtype) for p in pks]


def _sibling_swap_sems(nb):
    return [pltpu.SemaphoreType.DMA((4 * nb,)), pltpu.SemaphoreType.DMA((4 * nb,))]


def _sibling_swap_phases(pk_refs, out_refs, send_sems, recv_sems):
    nb = len(pk_refs)
    x, y, c = _position()

    def copies():
        return [pltpu.make_async_remote_copy(
            src_ref=pk_refs[i].at[2 * k + 1 - c], dst_ref=out_refs[i].at[k], send_sem=send_sems.at[4 * i + k],
            recv_sem=recv_sems.at[4 * i + k], device_id=(x, y, 1 - c), device_id_type=MESH)
            for i in range(nb) for k in range(4)]

    def start():
        for cp in copies():
            cp.start()

    def finish():
        for cp in copies():
            cp.wait()

    return start, finish


def _swap_between_chips(pbs, name):
    nb = len(pbs)

    def body(*refs):
        start, finish = _chip_swap_phases(refs[:nb], refs[nb:2 * nb], *refs[2 * nb:])
        start()
        finish()

    return pl.pallas_call(
        body, name=name, out_shape=[S(p.shape, p.dtype) for p in pbs], in_specs=[ANY] * nb, out_specs=[ANY] * nb,
        scratch_shapes=_chip_swap_sems(nb),
    )(*pbs)


def _chip_swap_sems(nb):
    return [pltpu.SemaphoreType.DMA((3 * nb,)), pltpu.SemaphoreType.DMA((3 * nb,)), pltpu.SemaphoreType.DMA((nb,))]


def _chip_swap_phases(pb_refs, out_refs, send_sems, recv_sems, local_sems):
    nb = len(pb_refs)
    x, y, c = _position()
    me = 2 * x + y
    chips = [(1 - x, y), (x, 1 - y), (1 - x, 1 - y)]

    def local(i):
        return pltpu.make_async_copy(pb_refs[i].at[me], out_refs[i].at[me], local_sems.at[i])

    def send(i, j):
        cx, cy = chips[j]
        return pltpu.make_async_remote_copy(
            src_ref=pb_refs[i].at[2 * cx + cy], dst_ref=out_refs[i].at[me], send_sem=send_sems.at[3 * i + j],
            recv_sem=recv_sems.at[3 * i + j], device_id=(cx, cy, c), device_id_type=MESH)

    def arrival(i, j):
        cx, cy = chips[j]
        return pltpu.make_async_remote_copy(
            src_ref=pb_refs[i].at[me], dst_ref=out_refs[i].at[2 * cx + cy], send_sem=send_sems.at[3 * i + j],
            recv_sem=recv_sems.at[3 * i + j], device_id=(cx, cy, c), device_id_type=MESH)

    def start():
        for i in range(nb):
            local(i).start()
            for j in range(3):
                send(i, j).start()

    def finish():
        for i in range(nb):
            for j in range(3):
                arrival(i, j).wait_recv()
        for i in range(nb):
            for j in range(3):
                send(i, j).wait_send()
            local(i).wait()

    return start, finish


def _row_tile(r):
    return max(t for t in range(16, min(r, 1024) + 1, 16) if r % t == 0)


def _pair_sum_cast(pk, recv, core, name):
    _, r, l = pk.shape
    tr = _row_tile(r)

    def body(c_ref, a_ref, b_ref, o_ref):
        o_ref[...] = (a_ref[...] + b_ref[...]).astype(BF16)

    return pl.pallas_call(
        body, name=name,
        grid_spec=pltpu.PrefetchScalarGridSpec(
            num_scalar_prefetch=1, grid=(4, r // tr),
            in_specs=[pl.BlockSpec((None, tr, l), lambda k, i, c: (2 * k + c[0], i, 0)),
                      pl.BlockSpec((None, tr, l), lambda k, i, c: (k, i, 0))],
            out_specs=pl.BlockSpec((None, tr, l), lambda k, i, c: (k, i, 0))),
        out_shape=S((4, r, l), BF16), compiler_params=_cp(("parallel", "parallel")),
    )(core, pk, recv)


def _final_sum(pk, recv_sib, recv_chips, slot, chip, name):
    _, r, l = pk.shape
    tr = _row_tile(r)

    def body(s_ref, k_ref, a_ref, b_ref, rc_ref, o_ref):
        base = a_ref[...] + b_ref[...]
        acc = jnp.zeros_like(base)
        for j in range(4):
            acc = acc + jnp.where(k_ref[0] == j, base, rc_ref[j].astype(F32))
        o_ref[...] = acc

    return pl.pallas_call(
        body, name=name,
        grid_spec=pltpu.PrefetchScalarGridSpec(
            num_scalar_prefetch=2, grid=(r // tr,),
            in_specs=[pl.BlockSpec((None, tr, l), lambda i, s, k: (s[0], i, 0)),
                      pl.BlockSpec((None, tr, l), lambda i, s, k: (k[0], i, 0)),
                      pl.BlockSpec((4, tr, l), lambda i, s, k: (0, i, 0))],
            out_specs=pl.BlockSpec((tr, l), lambda i, s, k: (i, 0))),
        out_shape=S((r, l), F32), compiler_params=_cp(("parallel",)),
    )(slot, chip, pk, recv_sib, recv_chips)


def _adamw_math(w, g, m, v):
    m = ADAM_B1 * m + (1.0 - ADAM_B1) * g
    v = ADAM_B2 * v + (1.0 - ADAM_B2) * (g * g)
    m_hat = m / (1.0 - ADAM_B1 ** ADAM_STEP)
    v_hat = v / (1.0 - ADAM_B2 ** ADAM_STEP)
    return -ADAM_LR * (m_hat / (jnp.sqrt(v_hat) + ADAM_EPS) + ADAM_WD * w), m, v


def _adamw(w, g, m, v, name):
    r, c = w.shape
    tr = 512 if r % 512 == 0 else r

    def body(w_ref, g_ref, m_ref, v_ref, d_ref, nm_ref, nv_ref):
        d_ref[...], nm_ref[...], nv_ref[...] = _adamw_math(w_ref[...], g_ref[...], m_ref[...], v_ref[...])

    tile = pl.BlockSpec((tr, c), lambda i: (i, 0))
    return pl.pallas_call(
        body, name=name, grid=(r // tr,), in_specs=[tile] * 4, out_specs=[tile] * 3, out_shape=[S((r, c), F32)] * 3,
        compiler_params=_cp(("parallel",)),
    )(w, g, m, v)


def _small_update(gathered, w, m, v):
    def body(ga_ref, w_ref, m_ref, v_ref, g_ref, d_ref, nm_ref, nv_ref):
        g = ga_ref[0]
        for k in range(1, N_DEV):
            g = g + ga_ref[k]
        g_ref[...] = g
        d_ref[...], nm_ref[...], nv_ref[...] = _adamw_math(w_ref[...], g, m_ref[...], v_ref[...])

    return pl.pallas_call(body, name="small_update", out_shape=[S(w.shape, F32)] * 4, compiler_params=_cp())(
        gathered, w, m, v)


BIG = ("w_in", "mem_kv_w", "w_br_hgrn", "w_br_fox", "w_br_mem", "w_out", "ffn_w_up", "ffn_conv_w", "ffn_w_down")
GROUP_ROWS = ("w_out", "ffn_w_down")
GROUP_LANE = ("w_br_hgrn", "w_br_fox", "w_br_mem")
LANE_GROUP_ROWS = 224
SMALL = ("norm_mix_g", "norm_mem_g", "hgrn_lb_logits", "hgrn_norm_g", "fox_f_bias", "fox_q_norm_g", "fox_k_norm_g",
         "mem_q_norm_g", "mem_k_norm_g", "norm_ffn_g", "ffn_conv_b")


def _rows_of(n_elems):
    return -(-n_elems // LANE)


def _to_rows(a, lead=0):
    flat = a.reshape(a.shape[:lead] + (-1,))
    pad = (-flat.shape[-1]) % LANE
    if pad:
        flat = jnp.pad(flat, [(0, 0)] * lead + [(0, pad)])
    return flat.reshape(a.shape[:lead] + (-1, LANE))


def _stack_rows(parts, lead, total_rows):
    buf = jnp.concatenate(parts, axis=lead)
    pad = total_rows - buf.shape[lead]
    return jnp.pad(buf, [(0, 0)] * lead + [(0, pad), (0, 0)])


def _round_up(n, k):
    return -(-n // k) * k


def _from_rows(rows, shape, lead=0):
    n = math.prod(shape)
    return rows.reshape(rows.shape[:lead] + (-1,))[..., :n].reshape(rows.shape[:lead] + tuple(shape))


def _blocks_to_full(blocks, kind):
    n, a, b = blocks.shape
    return blocks.transpose(1, 0, 2).reshape(a, n * b) if kind == "col" else blocks.reshape(n * a, b)


def _full_to_blocks(full, kind, n=N_DEV):
    a, b = full.shape
    return full.reshape(a, n, b // n).transpose(1, 0, 2) if kind == "col" else full.reshape(n, a // n, b)


def _lane_group_rows(shard):
    n_lane = sum(shard[n].shape[0] for n in GROUP_LANE)
    n_cw = shard["ffn_conv_w"].size
    return n_lane, _rows_of(3 * n_cw), _rows_of(n_cw), _round_up(n_lane + _rows_of(3 * n_cw), LANE_GROUP_ROWS)


def _split_bf16x3(x):
    hi = x.astype(BF16)
    r1 = x - hi.astype(F32)
    mid = r1.astype(BF16)
    return jnp.stack([hi, mid, (r1 - mid.astype(F32)).astype(BF16)])


class _Exchange:
    def __init__(self, shard):
        self.shard = shard
        xi, yi, ci = _position()
        self.core = ci.astype(jnp.int32).reshape(1)
        self.chip = (2 * xi + yi).astype(jnp.int32).reshape(1)
        self.n_lane, self.r_pieces, self.r_vals, self.r_lane = _lane_group_rows(shard)

    def first_blocks(self):
        return [self.shard["w_in"].astype(BF16), self.shard["mem_kv_w"].astype(BF16)]

    def unpack_first(self, gathered):
        return {"w_in": _perm_from_blocks(gathered[0]), "mem_kv_w": _blocks_to_full(gathered[1], "row")}

    def late_blocks(self):
        sh = self.shard
        lane_rows = [sh[n].astype(BF16) for n in GROUP_LANE] + [_to_rows(_split_bf16x3(sh["ffn_conv_w"]))]
        return [sh[n].astype(BF16) for n in GROUP_ROWS] + [sh["ffn_w_up"].astype(BF16),
                                                           _stack_rows(lane_rows, 0, self.r_lane)]

    def unpack_late(self, gathered):
        *rows, gc, gd = gathered
        sh = self.shard
        W = {"ffn_w_up": _blocks_to_full(gc, "col")}
        for n, blocks in zip(GROUP_ROWS, rows):
            W[n] = _blocks_to_full(blocks, "row")
        r0 = 0
        for n in GROUP_LANE:
            W[n] = _blocks_to_full(gd[:, r0:r0 + sh[n].shape[0]], "col")
            r0 += sh[n].shape[0]
        cw = _from_rows(gd[:, self.n_lane:self.n_lane + self.r_pieces], (3,) + sh["ffn_conv_w"].shape, lead=1).astype(F32)
        W["ffn_conv_w"] = _blocks_to_full(cw[:, 0] + cw[:, 1] + cw[:, 2], "col")
        return W

    def early_grads(self, g):
        cw_rows = _to_rows(_full_to_blocks(g["ffn_conv_w"], "col"), lead=1)
        return [_full_to_blocks(g[n], "row") for n in GROUP_ROWS] + [
            jnp.concatenate([_full_to_blocks(h, "col", N_DEV // 2) for h in g["ffn_w_up"]], axis=0),
            _stack_rows([_full_to_blocks(g[n], "col") for n in GROUP_LANE] + [cw_rows], 1, self.r_lane)]

    def last_grads(self, g):
        return [_unperm_blocks(g["w_in"], N_DEV), _full_to_blocks(g["mem_kv_w"], "row")]

    def pair_sums(self, pks, recv_sib, tag):
        return [_pair_sum_cast(p, r, self.core, "rs_pair_sum_%s%d" % (tag, i))
                for i, (p, r) in enumerate(zip(pks, recv_sib))]

    def final_sums(self, pks, recv_sib, recv_chips, tag):
        return [_final_sum(p, rs, rc, 2 * self.chip + self.core, self.chip, "rs_final_sum_%s%d" % (tag, i))
                for i, (p, rs, rc) in enumerate(zip(pks, recv_sib, recv_chips))]

    def unpack_grads(self, early, last):
        sh = self.shard
        *rows, g_up, g_lane = early
        g_shard = {"w_in": last[0], "mem_kv_w": last[1], "ffn_w_up": g_up, **dict(zip(GROUP_ROWS, rows))}
        r0 = 0
        for n in GROUP_LANE:
            g_shard[n] = g_lane[r0:r0 + sh[n].shape[0]]
            r0 += sh[n].shape[0]
        g_shard["ffn_conv_w"] = _from_rows(g_lane[self.n_lane:self.n_lane + self.r_vals], sh["ffn_conv_w"].shape)
        return g_shard


def kernel(x, mem, norm_mix_g, norm_mem_g, w_in, hgrn_lb_logits, hgrn_norm_g, fox_f_bias, fox_q_norm_g, fox_k_norm_g, mem_kv_w, mem_q_norm_g, mem_k_norm_g, w_br_hgrn, w_br_fox, w_br_mem, w_out, norm_ffn_g, ffn_w_up, ffn_conv_w, ffn_conv_b, ffn_w_down, loss_target, m_norm_mix_g, m_norm_mem_g, m_w_in, m_hgrn_lb_logits, m_hgrn_norm_g, m_fox_f_bias, m_fox_q_norm_g, m_fox_k_norm_g, m_mem_kv_w, m_mem_q_norm_g, m_mem_k_norm_g, m_w_br_hgrn, m_w_br_fox, m_w_br_mem, m_w_out, m_norm_ffn_g, m_ffn_w_up, m_ffn_conv_w, m_ffn_conv_b, m_ffn_w_down, v_norm_mix_g, v_norm_mem_g, v_w_in, v_hgrn_lb_logits, v_hgrn_norm_g, v_fox_f_bias, v_fox_q_norm_g, v_fox_k_norm_g, v_mem_kv_w, v_mem_q_norm_g, v_mem_k_norm_g, v_w_br_hgrn, v_w_br_fox, v_w_br_mem, v_w_out, v_norm_ffn_g, v_ffn_w_up, v_ffn_conv_w, v_ffn_conv_b, v_ffn_w_down):
    given = dict(locals())
    order = ("norm_mix_g", "norm_mem_g", "w_in", "hgrn_lb_logits", "hgrn_norm_g", "fox_f_bias", "fox_q_norm_g",
             "fox_k_norm_g", "mem_kv_w", "mem_q_norm_g", "mem_k_norm_g", "w_br_hgrn", "w_br_fox", "w_br_mem", "w_out",
             "norm_ffn_g", "ffn_w_up", "ffn_conv_w", "ffn_conv_b", "ffn_w_down")
    B, T, D = x.shape
    M = mem.shape[1]
    shard = {n: given[n][0] if n in BIG else given[n] for n in order}
    mom = {n: (given["m_" + n][0], given["v_" + n][0]) if n in BIG else (given["m_" + n], given["v_" + n])
           for n in order}
    shard["hgrn_lb_logits"] = hgrn_lb_logits
    for n in ("norm_mix_g", "norm_mem_g", "hgrn_norm_g", "fox_f_bias", "fox_q_norm_g", "fox_k_norm_g", "mem_q_norm_g",
              "mem_k_norm_g", "norm_ffn_g", "ffn_conv_b"):
        shard[n] = given[n].reshape(1, -1)

    ex = _Exchange(shard)
    sm = {n: shard[n] for n in SMALL}
    grad_x, g, sums = _local_step(x.reshape(B * T, D), mem.reshape(B * M, D), loss_target.reshape(B * T, D), sm, None,
                                  B, T, M, ex)
    g_shard = ex.unpack_grads(*sums)

    sg = {n: g[n] for n in SMALL}
    sg["fox_f_bias"] = g["fox_f_bias"][:, :FOX_H]
    sg["fox_q_norm_g"] = g["fox_q_norm_g"][:, :FOX_D]
    sg["fox_k_norm_g"] = g["fox_k_norm_g"][:, :FOX_D]
    slayout, row0 = {}, 0
    for n in SMALL:
        nr = _rows_of(shard[n].size)
        slayout[n] = (row0, nr)
        row0 += nr
    loss_row = row0
    r_small = _round_up(row0 + 1, 8)

    def pack_small(d, with_loss=None):
        rows = [_to_rows(d[n]) for n in SMALL]
        rows.append(with_loss if with_loss is not None else jnp.zeros((1, LANE), F32))
        return _stack_rows(rows, 0, r_small)

    sgath, = _all_gather([pack_small(sg, g["loss"])], "ag_small")
    s_g, s_d, s_m, s_v = _small_update(sgath, pack_small(shard), pack_small({n: mom[n][0].reshape(shard[n].shape) for n in SMALL}),
                                       pack_small({n: mom[n][1].reshape(shard[n].shape) for n in SMALL}))
    loss = s_g[loss_row, 0]

    grads, deltas, new_m, new_v = {}, {}, {}, {}
    for n in BIG:
        gn = g_shard[n]
        d, nm, nv = _adamw(shard[n], gn, mom[n][0], mom[n][1], "adamw_" + n)
        grads[n], deltas[n], new_m[n], new_v[n] = (a[None] for a in (gn, d, nm, nv))
    for n in SMALL:
        r0, nr = slayout[n]
        for dst, src in ((grads, s_g), (deltas, s_d), (new_m, s_m), (new_v, s_v)):
            dst[n] = _from_rows(src[r0:r0 + nr], given[n].shape)
    return (loss, grad_x.reshape(B, T, D), *[grads[n] for n in order], *[deltas[n] for n in order],
            *[new_m[n] for n in order], *[new_v[n] for n in order])
```

```python
import functools
import math

import jax
import jax.numpy as jnp
from jax import lax
from jax.experimental import pallas as pl
from jax.experimental.pallas import tpu as pltpu

F32, BF16 = jnp.float32, jnp.bfloat16
S = jax.ShapeDtypeStruct
MESH = pl.DeviceIdType.MESH

N_DEV = 8
EPS = 1e-6
LANE = 128
CHUNK = 64
SUB = 16
HG_H, HG_D = 4, 128
HG_GROUP_FWD = 4
HG_GROUP = 2
FOX_H, FOX_D = 8, 64
FOX_P = FOX_H // 2
MEM_H, MEM_D = 4, 128
NEG = -1e30
VMEM_LIMIT = 56 * 2**20

ADAM_LR, ADAM_B1, ADAM_B2, ADAM_EPS, ADAM_WD, ADAM_STEP = 0.001, 0.9, 0.999, 1e-08, 0.01, 10

C_FOX, C_MQ, C_HG, C_GATE, C_FF, C_END = 0, 1536, 2048, 4096, 7168, 7296


def _cp(sem=None):
    return pltpu.CompilerParams(dimension_semantics=sem, vmem_limit_bytes=VMEM_LIMIT)


def _dot(a, b, dims, prec=None):
    return lax.dot_general(a, b, (dims, ((), ())), preferred_element_type=F32, precision=prec)


def _nn(a, b, prec=None):
    return _dot(a, b, ((1,), (0,)), prec)


def _nt(a, b, prec=None):
    return _dot(a, b, ((1,), (1,)), prec)


def _tn(a, b, prec=None):
    return _dot(a, b, ((0,), (0,)), prec)


def _b(x):
    return x.astype(BF16)


def _mm3(fn, a, b):
    ah, bh = _b(a), _b(b)
    return fn(ah, bh) + fn(ah, _b(b - bh.astype(F32))) + fn(_b(a - ah.astype(F32)), bh)


def _iota(shape, dim):
    return lax.broadcasted_iota(jnp.int32, shape, dim)


def _rowsum8(x):
    r, d = x.shape
    return jnp.sum(x.reshape(r // 8, 8, d), axis=0)


def _rmsnorm_cast(x, g, name, tm=1024, gather=()):
    n, d = x.shape
    nga = len(gather)

    def body(x_ref, g_ref, o_ref):
        v = x_ref[...]
        r = lax.rsqrt(jnp.mean(v * v, axis=-1, keepdims=True) + EPS)
        o_ref[...] = (v * r * g_ref[...]).astype(BF16)

    if nga:
        body = _hosting(body, 2, 1, 0, nga, _gather_phases, (n // tm,))
    out = pl.pallas_call(
        body, name=name, grid=(n // tm,),
        in_specs=[pl.BlockSpec((tm, d), lambda i: (i, 0)), pl.BlockSpec((1, d), lambda i: (0, 0))] + [ANY] * nga,
        out_specs=[pl.BlockSpec((tm, d), lambda i: (i, 0))] + [ANY] * nga,
        out_shape=[S((n, d), BF16)] + _gather_shapes(gather), scratch_shapes=_gather_sems(nga) if nga else [],
        compiler_params=_cp(("arbitrary",) if nga else ("parallel",)),
    )(x, g, *gather)
    return out if nga else out[0]


def _rmsnorm_bwd(dh, x, g, resid, name, tm=1024):
    n, d = x.shape
    has_res = resid is not None

    def body(*refs):
        if has_res:
            dh_ref, x_ref, g_ref, r_ref, dx_ref, dg_ref = refs
        else:
            dh_ref, x_ref, g_ref, dx_ref, dg_ref = refs
        v = x_ref[...]
        dhv = dh_ref[...].astype(F32)
        r = lax.rsqrt(jnp.mean(v * v, axis=-1, keepdims=True) + EPS)
        xh = v * r
        u = dhv * g_ref[...]
        dx = r * (u - xh * jnp.mean(u * xh, axis=-1, keepdims=True))
        if has_res:
            dx = dx + r_ref[...]
        dx_ref[...] = dx

        @pl.when(pl.program_id(0) == 0)
        def _():
            dg_ref[...] = jnp.zeros_like(dg_ref)

        dg_ref[...] += _rowsum8(dhv * xh)

    tile = pl.BlockSpec((tm, d), lambda i: (i, 0))
    ins = [tile, tile, pl.BlockSpec((1, d), lambda i: (0, 0))] + ([tile] if has_res else [])
    args = (dh, x, g) + ((resid,) if has_res else ())
    return pl.pallas_call(
        body, name=name, grid=(n // tm,), in_specs=ins,
        out_specs=[tile, pl.BlockSpec((8, d), lambda i: (0, 0))],
        out_shape=[S((n, d), F32), S((8, d), F32)], compiler_params=_cp(("arbitrary",)),
    )(*args)


def _mm_nn(a, b, out_dtype, name, tm, tn):
    m, k = a.shape
    n = b.shape[1]
    assert n % tn == 0 and m % tm == 0

    def body(a_ref, b_ref, o_ref):
        o_ref[...] = _nn(a_ref[...].astype(BF16), b_ref[...].astype(BF16)).astype(out_dtype)

    return pl.pallas_call(
        body, name=name, grid=(n // tn, m // tm),
        in_specs=[pl.BlockSpec((tm, k), lambda j, i: (i, 0)), pl.BlockSpec((k, tn), lambda j, i: (0, j))],
        out_specs=pl.BlockSpec((tm, tn), lambda j, i: (i, j)), out_shape=S((m, n), out_dtype),
        compiler_params=_cp(("parallel", "parallel")),
    )(a, b)


def _mm_nt_sum(parts, w, name, tm, swap=()):
    m = parts[0][0].shape[0]
    k = w.shape[0]
    assert m % tm == 0 and all(c % n == 0 and o % n == 0 for _, c, n, o in parts)
    np_ = len(parts)
    nsw = len(swap)
    n_steps = m // tm

    def body(*refs):
        o_ref = refs[2 * np_ + nsw]
        if nsw:
            start, finish = _chip_swap_phases(refs[2 * np_:2 * np_ + nsw], refs[2 * np_ + nsw + 1:2 * np_ + 2 * nsw + 1],
                                              *refs[2 * np_ + 2 * nsw + 1:])
            pl.when(pl.program_id(0) == 0)(start)
        acc = _nt(refs[0][...].astype(BF16), refs[np_][...].astype(BF16))
        for i in range(1, np_):
            acc = acc + _nt(refs[i][...].astype(BF16), refs[np_ + i][...].astype(BF16))
        o_ref[...] = acc
        if nsw:
            pl.when(pl.program_id(0) == n_steps - 1)(finish)

    dy_specs = [pl.BlockSpec((tm, n), functools.partial(lambda i, j: (i, j), j=c // n)) for _, c, n, _ in parts]
    w_specs = [pl.BlockSpec((k, n), functools.partial(lambda i, j: (0, j), j=o // n)) for _, _, n, o in parts]
    out = pl.pallas_call(
        body, name=name, grid=(n_steps,), in_specs=dy_specs + w_specs + [ANY] * nsw,
        out_specs=[pl.BlockSpec((tm, k), lambda i: (i, 0))] + [ANY] * nsw,
        out_shape=[S((m, k), F32)] + [S(p.shape, p.dtype) for p in swap],
        scratch_shapes=_chip_swap_sems(nsw) if nsw else [],
        compiler_params=_cp(("arbitrary",) if nsw else ("parallel",)),
    )(*([p[0] for p in parts] + [w] * np_ + list(swap)))
    return (out[0], out[1:]) if nsw else out[0]


def _mm_tn(x, dy, name, tm, tn):
    m, k = x.shape
    n = dy.shape[1]
    tm = min(tm, m)
    assert m % tm == 0 and n % tn == 0

    def body(x_ref, dy_ref, o_ref):
        part = _tn(x_ref[...].astype(BF16), dy_ref[...].astype(BF16))

        @pl.when(pl.program_id(1) == 0)
        def _():
            o_ref[...] = part

        @pl.when(pl.program_id(1) > 0)
        def _():
            o_ref[...] += part

    return pl.pallas_call(
        body, name=name, grid=(n // tn, m // tm),
        in_specs=[pl.BlockSpec((tm, k), lambda j, i: (i, 0)), pl.BlockSpec((tm, tn), lambda j, i: (i, j))],
        out_specs=pl.BlockSpec((k, tn), lambda j, i: (0, j)), out_shape=S((k, n), F32),
        compiler_params=_cp(("parallel", "arbitrary")),
    )(x, dy)


def _lower_bound(logits):
    e = jnp.exp(logits - jnp.max(logits, axis=0, keepdims=True))
    return e[0:1, :] / jnp.sum(e, axis=0, keepdims=True)


def _hg_gates(fl, lb):
    sig = jax.nn.sigmoid(fl)
    f = lb + (1.0 - lb) * sig
    k = (1.0 - lb) * (1.0 - sig)
    return sig, f, k, jnp.log(f)


def _silu_and_grad(x):
    s = jax.nn.sigmoid(x)
    return x * s, s * (1.0 + x * (1.0 - s))


def _hg_rowblocks(G):
    return [None] + [G[SUB * i - 1:SUB * i, :] for i in range(1, CHUNK // SUB)]


def _hg_intra_A(qs, k, G):
    refs = _hg_rowblocks(G)
    cols = _iota((SUB, LANE), 1)
    rows = _iota((SUB, LANE), 0)
    no_keys = jnp.zeros((LANE - CHUNK, HG_D), BF16)
    blocks = []
    for i in range(CHUNK // SUB):
        lo = SUB * i
        qb, Gb = qs[lo:lo + SUB, :], G[lo:lo + SUB, :]
        diag = jnp.zeros((SUB, LANE), F32)
        for s in range(SUB):
            e = jnp.exp(jnp.minimum(Gb - G[lo + s:lo + s + 1, :], 0.0))
            col = jnp.sum(qb * k[lo + s:lo + s + 1, :] * e, axis=-1, keepdims=True)
            diag = jnp.where(cols == lo + s, col, diag)
        a = jnp.where((cols >= lo) & (cols <= rows + lo), diag, 0.0)
        if i > 0:
            qr = qb * jnp.exp(Gb - refs[i])
            kr = k * jnp.exp(jnp.minimum(refs[i] - G, 0.0))
            a = jnp.where(cols < lo, _nt(_b(qr), jnp.concatenate([_b(kr), no_keys], axis=0)), a)
        blocks.append(a)
    return jnp.concatenate(blocks, axis=0)


def _hg_intra_bwd(dA, qs, k, G):
    refs = _hg_rowblocks(G)
    cols = _iota((SUB, CHUNK), 1)
    rows16 = _iota((SUB, HG_D), 0)
    dk = jnp.zeros((CHUNK, HG_D), F32)
    dq_blocks, dk_diag_blocks = [], []
    for i in range(CHUNK // SUB):
        lo = SUB * i
        qb, Gb = qs[lo:lo + SUB, :], G[lo:lo + SUB, :]
        dAb = dA[lo:lo + SUB, :]
        dq = jnp.zeros((SUB, HG_D), F32)
        dkb = jnp.zeros((SUB, HG_D), F32)
        for s in range(SUB):
            e = jnp.exp(jnp.minimum(Gb - G[lo + s:lo + s + 1, :], 0.0))
            e = jnp.where(rows16 >= s, e, 0.0)
            dcol = jnp.sum(jnp.where(cols == lo + s, dAb, 0.0), axis=-1, keepdims=True)
            w = dcol * e
            dq = dq + w * k[lo + s:lo + s + 1, :]
            dkb = jnp.where(rows16 == s, jnp.sum(w * qb, axis=0, keepdims=True), dkb)
        if i > 0:
            e1 = jnp.exp(Gb - refs[i])
            e2 = jnp.exp(jnp.minimum(refs[i] - G, 0.0))
            dA_off = jnp.where(cols < lo, dAb, 0.0)
            dq = dq + _mm3(_nn, dA_off, k * e2) * e1
            dk = dk + _mm3(_tn, dA_off, qb * e1) * e2
        dq_blocks.append(dq)
        dk_diag_blocks.append(dkb)
    return jnp.concatenate(dq_blocks, axis=0), dk + jnp.concatenate(dk_diag_blocks, axis=0)


def _tri(n, upper=False):
    r, c = _iota((n, n), 0), _iota((n, n), 1)
    return jnp.where((c >= r) if upper else (r >= c), 1.0, 0.0).astype(BF16)


def _prefix_mm(tri, x):
    hi = x.astype(BF16)
    r1 = x - hi.astype(F32)
    mid = r1.astype(BF16)
    lo = (r1 - mid.astype(F32)).astype(BF16)
    return _nn(tri, hi) + _nn(tri, mid) + _nn(tri, lo)


def _hgrn_fwd(z, lb, gn, B, T):
    N = B * T
    NC = T // CHUNK
    ng = HG_H // HG_GROUP_FWD

    def body(z_ref, lb_ref, gn_ref, y_ref, o_ref, st_ref, a_ref, s_scr):
        lbs = _lower_bound(lb_ref[...])
        tri = _tri(CHUNK)
        s_scr[...] = jnp.zeros_like(s_scr)

        def chunk(c, carry):
            r = pl.ds(pl.multiple_of(c * CHUNK, CHUNK), CHUNK)
            for hh in range(HG_GROUP_FWD):
                zc, oc = 4 * LANE * hh, LANE * hh
                ql, fl, il, gl = (z_ref[r, zc + LANE * j:zc + LANE * (j + 1)].astype(F32) for j in range(4))
                _, _, k, logf = _hg_gates(fl, lbs[:, oc:oc + LANE])
                G = _prefix_mm(tri, logf)
                qs = ql * jax.nn.sigmoid(ql)
                st = s_scr[hh]
                st_ref[hh * NC + c] = st
                g_last = G[CHUNK - 1:CHUNK, :]
                A = _b(_hg_intra_A(qs, k, G))
                a_ref[r, oc:oc + LANE] = A
                o = _nn(A[:, 0:CHUNK], _b(il)) + _nt(_b(qs * jnp.exp(G)), _b(st))
                s_scr[hh] = st * jnp.exp(g_last) + _mm3(_tn, il, k * jnp.exp(g_last - G))
                o_ref[r, oc:oc + LANE] = o
                rstd = lax.rsqrt(jnp.mean(o * o, axis=-1, keepdims=True) + EPS)
                y_ref[r, oc:oc + LANE] = (o * rstd * gn_ref[...] * (gl * jax.nn.sigmoid(gl))).astype(BF16)
            return carry

        lax.fori_loop(0, NC, chunk, 0, unroll=4)

    gw = HG_GROUP_FWD * LANE
    cb = C_HG // (4 * gw)
    return pl.pallas_call(
        body, name="hgrn_fwd", grid=(B, ng),
        in_specs=[pl.BlockSpec((T, 4 * gw), lambda b, h: (b, cb + h)), pl.BlockSpec((lb.shape[0], gw), lambda b, h: (0, h)),
                  pl.BlockSpec((1, LANE), lambda b, h: (0, 0))],
        out_specs=[pl.BlockSpec((T, gw), lambda b, h: (b, h)), pl.BlockSpec((T, gw), lambda b, h: (b, h)),
                   pl.BlockSpec((HG_GROUP_FWD * NC, HG_D, HG_D), lambda b, h: (b * ng + h, 0, 0)),
                   pl.BlockSpec((T, gw), lambda b, h: (b, h))],
        out_shape=[S((N, 512), BF16), S((N, 512), F32), S((B * HG_H * NC, HG_D, HG_D), F32), S((N, 512), BF16)],
        scratch_shapes=[pltpu.VMEM((HG_GROUP_FWD, HG_D, HG_D), F32)], compiler_params=_cp(("parallel", "parallel")),
    )(z, lb, gn)


def _hgrn_bwd(z, o_raw, states, a_mat, dy, lb, gn, B, T, swap_sibling=()):
    N = B * T
    NC = T // CHUNK
    ng = HG_H // HG_GROUP
    nsw = len(swap_sibling)

    def body(z_ref, o_ref, st_ref, a_ref, dy_ref, lb_ref, gn_ref, dz_ref, dlb_ref, dgn_ref, ds_scr, racc, dgn_acc):
        lbs = _lower_bound(lb_ref[...])
        gn_v = gn_ref[...]
        tri, triu = _tri(CHUNK), _tri(CHUNK, upper=True)
        cmask = _iota((CHUNK, CHUNK), 0) >= _iota((CHUNK, CHUNK), 1)
        for ref in (ds_scr, racc, dgn_acc, dlb_ref):
            ref[...] = jnp.zeros_like(ref)

        def chunk(ci, carry):
            c = NC - 1 - ci
            r = pl.ds(pl.multiple_of(c * CHUNK, CHUNK), CHUNK)
            for hh in range(HG_GROUP):
                zc, oc = 4 * LANE * hh, LANE * hh
                lb_v = lbs[:, oc:oc + LANE]
                ql, fl, il, gl = (z_ref[r, zc + LANE * j:zc + LANE * (j + 1)].astype(F32) for j in range(4))
                sig, f, k, logf = _hg_gates(fl, lb_v)
                G = _prefix_mm(tri, logf)
                qs, dsilu_q = _silu_and_grad(ql)
                gs, dsilu_g = _silu_and_grad(gl)
                o = o_ref[r, oc:oc + LANE]
                dyv = dy_ref[r, oc:oc + LANE]
                rstd = lax.rsqrt(jnp.mean(o * o, axis=-1, keepdims=True) + EPS)
                oh = o * rstd
                dgl = dyv * oh * gn_v * dsilu_g
                dn = dyv * gs
                dgn_acc[...] += _rowsum8(dn * oh)
                u = dn * gn_v
                do = rstd * (u - oh * jnp.mean(u * oh, axis=-1, keepdims=True))
                st = st_ref[hh * NC + c]
                dst = ds_scr[hh]
                eG = jnp.exp(G)
                g_last = G[CHUNK - 1:CHUNK, :]
                eL = jnp.exp(g_last - G)
                dA = jnp.where(cmask, _mm3(_nt, do, il), 0.0)
                dq_in, dk_in = _hg_intra_bwd(dA, qs, k, G)
                di = _tn(a_ref[r, oc:oc + LANE][:, 0:CHUNK], _b(do)) + _nt(_b(k * eL), _b(dst))
                dq = dq_in + _mm3(_nn, do, st) * eG
                dk = dk_in + _mm3(_nn, il, dst) * eL
                ds_scr[hh] = dst * jnp.exp(g_last) + _mm3(_tn, do, qs * eG)
                dd = qs * dq - k * dk
                dlogf = _prefix_mm(triu, dd) + racc[hh]
                racc[hh] += jnp.sum(dd, axis=0, keepdims=True)
                df = dlogf / f - dk
                dlb_ref[8 * hh:8 * (hh + 1), :] += _rowsum8(df * (1.0 - sig))
                dz_ref[r, zc:zc + LANE] = (dq * dsilu_q).astype(BF16)
                dz_ref[r, zc + LANE:zc + 2 * LANE] = (df * (1.0 - lb_v) * sig * (1.0 - sig)).astype(BF16)
                dz_ref[r, zc + 2 * LANE:zc + 3 * LANE] = di.astype(BF16)
                dz_ref[r, zc + 3 * LANE:zc + 4 * LANE] = dgl.astype(BF16)
            return carry

        lax.fori_loop(0, NC, chunk, 0, unroll=4)
        dgn_ref[...] = dgn_acc[...]

    gw = HG_GROUP * LANE
    cb = C_HG // (4 * gw)
    col = pl.BlockSpec((T, gw), lambda b, h: (b, h))
    if nsw:
        body = _hosting(body, 7, 3, 3, nsw, _sibling_swap_phases, (B, ng))
    return pl.pallas_call(
        body, name="hgrn_bwd", grid=(B, ng),
        in_specs=[pl.BlockSpec((T, 4 * gw), lambda b, h: (b, cb + h)), col,
                  pl.BlockSpec((HG_GROUP * NC, HG_D, HG_D), lambda b, h: (b * ng + h, 0, 0)), col, col,
                  pl.BlockSpec((lb.shape[0], gw), lambda b, h: (0, h)), pl.BlockSpec((1, LANE), lambda b, h: (0, 0))]
        + [ANY] * nsw,
        out_specs=[pl.BlockSpec((T, 4 * gw), lambda b, h: (b, h)),
                   pl.BlockSpec((8 * HG_GROUP, LANE), lambda b, h: (b * ng + h, 0)),
                   pl.BlockSpec((8, LANE), lambda b, h: (b * ng + h, 0))] + [ANY] * nsw,
        out_shape=[S((N, 2048), BF16), S((B * HG_H * 8, LANE), F32), S((B * ng * 8, LANE), F32)]
        + _sibling_swap_shapes(swap_sibling),
        scratch_shapes=[pltpu.VMEM((HG_GROUP, HG_D, HG_D), F32), pltpu.VMEM((HG_GROUP, 1, LANE), F32),
                        pltpu.VMEM((8, LANE), F32)] + (_sibling_swap_sems(nsw) if nsw else []),
        compiler_params=_cp(("arbitrary", "arbitrary") if nsw else ("parallel", "parallel")),
    )(z, o_raw, states, a_mat, dy, lb, gn, *swap_sibling)


def _pair_mean(x, lo_half):
    a = jnp.sum(jnp.where(lo_half, x, 0.0), axis=-1, keepdims=True)
    b = jnp.sum(jnp.where(lo_half, 0.0, x), axis=-1, keepdims=True)
    return jnp.where(lo_half, a, b) * (1.0 / FOX_D)


def _fox_gate_fwd(z, bias, B, T):
    N = B * T
    tb = LANE

    def body(z_ref, b_ref, fc_ref, fct_ref):
        tri = _tri(tb)

        def step(i, carry):
            r = pl.ds(pl.multiple_of(i * tb, tb), tb)
            cs = _prefix_mm(tri, jax.nn.log_sigmoid(z_ref[r, :].astype(F32) + b_ref[...])) + carry
            fc_ref[r, :] = cs
            fct_ref[0, :, r] = cs.T[0:8, :]
            return cs[tb - 1:tb, :]

        lax.fori_loop(0, T // tb, step, jnp.zeros((1, LANE), F32))

    return pl.pallas_call(
        body, name="fox_gate_fwd", grid=(B,),
        in_specs=[pl.BlockSpec((T, LANE), lambda b: (b, C_FF // LANE)), pl.BlockSpec((1, LANE), lambda b: (0, 0))],
        out_specs=[pl.BlockSpec((T, LANE), lambda b: (b, 0)), pl.BlockSpec((1, 8, T), lambda b: (b, 0, 0))],
        out_shape=[S((N, LANE), F32), S((B, 8, T), F32)], compiler_params=_cp(("parallel",)),
    )(z, bias)


def _fox_gate_bwd(dfc, z, bias, B, T):
    N = B * T
    tb = LANE
    nt = T // tb

    def body(d_ref, z_ref, b_ref, dz_ref, db_ref):
        triu = _tri(tb, upper=True)
        db_ref[...] = jnp.zeros_like(db_ref)

        def step(ii, carry):
            r = pl.ds(pl.multiple_of((nt - 1 - ii) * tb, tb), tb)
            d = d_ref[r, 0:LANE]
            for p in range(1, FOX_P):
                d = d + d_ref[r, LANE * p:LANE * (p + 1)]
            rc = _prefix_mm(triu, d) + carry
            dff = rc * jax.nn.sigmoid(-(z_ref[r, :].astype(F32) + b_ref[...]))
            dz_ref[r, :] = dff.astype(BF16)
            db_ref[...] += _rowsum8(dff)
            return carry + jnp.sum(d, axis=0, keepdims=True)

        lax.fori_loop(0, nt, step, jnp.zeros((1, LANE), F32))

    return pl.pallas_call(
        body, name="fox_gate_bwd", grid=(B,),
        in_specs=[pl.BlockSpec((T, 512), lambda b: (b, 0)), pl.BlockSpec((T, LANE), lambda b: (b, C_FF // LANE)),
                  pl.BlockSpec((1, LANE), lambda b: (0, 0))],
        out_specs=[pl.BlockSpec((T, LANE), lambda b: (b, 0)), pl.BlockSpec((8, LANE), lambda b: (b, 0))],
        out_shape=[S((N, LANE), BF16), S((B * 8, LANE), F32)], compiler_params=_cp(("parallel",)),
    )(dfc, z, bias)


def _fox_prep(z_ref, gq, gk, r, lo_half):
    q, k, v = (z_ref[r, LANE * j:LANE * (j + 1)].astype(F32) for j in range(3))
    rq = lax.rsqrt(_pair_mean(q * q, lo_half) + EPS)
    rk = lax.rsqrt(_pair_mean(k * k, lo_half) + EPS)
    qh, kh = q * rq, k * rk
    return qh * gq * (FOX_D ** -0.5), kh * gk, v, qh, kh, rq, rk


def _fox_fwd(z, fc, fct, gq, gk, B, T, tq=512, gather=()):
    N = B * T
    NQ = T // tq
    nga = len(gather)

    def body(z_ref, fc_ref, fct_ref, gq_ref, gk_ref, y_ref, lse_ref, qn_s, kn_s, v_s):
        p, qi = pl.program_id(1), pl.program_id(2)
        lo_half = _iota((1, LANE), 1) < FOX_D

        @pl.when(qi == 0)
        def _():
            def prep(i, carry):
                r = pl.ds(pl.multiple_of(i * tq, tq), tq)
                qn, kn, v = _fox_prep(z_ref, gq_ref[...], gk_ref[...], r, lo_half)[:3]
                qn_s[r, :], kn_s[r, :], v_s[r, :] = qn.astype(BF16), kn.astype(BF16), v.astype(BF16)
                return carry
            lax.fori_loop(0, NQ, prep, 0)

        rq = pl.ds(pl.multiple_of(qi * tq, tq), tq)
        qn = qn_s[rq, :]
        fcq = fc_ref[rq, :]
        lane = _iota((tq, LANE), 1)
        causal = _iota((tq, tq), 0) >= _iota((tq, tq), 1)
        qhs = [jnp.where(lo_half, qn, jnp.zeros_like(qn)), jnp.where(lo_half, jnp.zeros_like(qn), qn)]
        fqs = [jnp.sum(jnp.where(lane == 2 * p + hh, fcq, 0.0), axis=-1, keepdims=True) for hh in range(2)]

        def kv(j, carry, diagonal):
            rk = pl.ds(pl.multiple_of(j * tq, tq), tq)
            kj, vj = kn_s[rk, :], v_s[rk, :]
            one = jnp.ones_like(vj)
            new = []
            for hh in range(2):
                m, acc = carry[hh]
                s = _nt(qhs[hh], kj) + fqs[hh] - fct_ref[0, pl.ds(2 * p + hh, 1), rk]
                if diagonal:
                    s = jnp.where(causal, s, NEG)
                m_new = jnp.maximum(m, jnp.max(s, axis=-1, keepdims=True))
                pe = jnp.exp(s - m_new)
                v_aug = jnp.where(lo_half if hh == 0 else jnp.logical_not(lo_half), vj, one)
                new.append((m_new, jnp.exp(m - m_new) * acc + _nn(pe.astype(BF16), v_aug)))
            return tuple(new)

        init = tuple((jnp.full((tq, 1), NEG, F32), jnp.zeros((tq, LANE), F32)) for _ in range(2))
        carry = lax.fori_loop(0, qi, functools.partial(kv, diagonal=False), init)
        (m0, a0), (m1, a1) = kv(qi, carry, True)
        l0, l1 = a0[:, FOX_D:FOX_D + 1], a1[:, 0:1]
        y_ref[...] = jnp.where(lo_half, a0 / l0, a1 / l1).astype(BF16)
        lse_ref[...] = jnp.where(lo_half, m0 + jnp.log(l0), m1 + jnp.log(l1))

    vec = pl.BlockSpec((1, LANE), lambda b, p, q: (0, 0))
    tile = pl.BlockSpec((tq, LANE), lambda b, p, q: (b * NQ + q, p))
    if nga:
        body = _hosting(body, 5, 2, 3, nga, _gather_phases, (B, FOX_P, NQ))
    return pl.pallas_call(
        body, name="fox_fwd", grid=(B, FOX_P, NQ),
        in_specs=[pl.BlockSpec((T, 384), lambda b, p, q: (b, p)), pl.BlockSpec((T, LANE), lambda b, p, q: (b, 0)),
                  pl.BlockSpec((1, 8, T), lambda b, p, q: (b, 0, 0)), vec, vec] + [ANY] * nga,
        out_specs=[tile, tile] + [ANY] * nga, out_shape=[S((N, 512), BF16), S((N, 512), F32)] + _gather_shapes(gather),
        scratch_shapes=[pltpu.VMEM((T, LANE), BF16)] * 3 + (_gather_sems(nga) if nga else []),
        compiler_params=_cp(("arbitrary",) * 3 if nga else ("parallel", "parallel", "arbitrary")),
    )(z, fc, fct, gq, gk, *gather)


def _fox_bwd(z, dy, y, lse, fc, fct, gq, gk, B, T, tq=512, swap=()):
    N = B * T
    NQ = T // tq
    nsw = len(swap)

    def body(z_ref, dy_ref, y_ref, lse_ref, fc_ref, fct_ref, gq_ref, gk_ref, dz_ref, dfc_ref, dgq_ref, dgk_ref,
             qn_s, kn_s, v_s, do_s, delta_s, dq_s, dfk_s):
        p, kj = pl.program_id(1), pl.program_id(2)
        lo_half = _iota((1, LANE), 1) < FOX_D
        lane = _iota((tq, LANE), 1)
        gq_v, gk_v = gq_ref[...], gk_ref[...]

        @pl.when(kj == 0)
        def _():
            def prep(i, carry):
                r = pl.ds(pl.multiple_of(i * tq, tq), tq)
                qn, kn, v = _fox_prep(z_ref, gq_v, gk_v, r, lo_half)[:3]
                qn_s[r, :], kn_s[r, :], v_s[r, :] = qn.astype(BF16), kn.astype(BF16), v.astype(BF16)
                do = dy_ref[r, :]
                do_s[r, :] = do.astype(BF16)
                delta_s[r, :] = _pair_mean(do * y_ref[r, :].astype(F32), lo_half) * float(FOX_D)
                return carry
            lax.fori_loop(0, NQ, prep, 0)
            dq_s[...] = jnp.zeros_like(dq_s)
            dgq_ref[...] = jnp.zeros_like(dgq_ref)
            dgk_ref[...] = jnp.zeros_like(dgk_ref)

        rk = pl.ds(pl.multiple_of(kj * tq, tq), tq)
        kn, vv = kn_s[rk, :], v_s[rk, :]
        causal = _iota((tq, tq), 0) >= _iota((tq, tq), 1)
        zero, one = jnp.zeros_like(kn), jnp.ones_like(kn)
        hms = [lo_half, jnp.logical_not(lo_half)]
        kmasks = [jnp.where(hm, kn, zero) for hm in hms]
        kaugs = [jnp.where(hm, kn, one) for hm in hms]
        vmasks = [jnp.where(hm, vv, zero) for hm in hms]
        fks = [fct_ref[0, pl.ds(2 * p + hh, 1), rk] for hh in range(2)]

        def qloop(i, carry, diagonal):
            ri = pl.ds(pl.multiple_of(i * tq, tq), tq)
            qn = qn_s[ri, :]
            do = do_s[ri, :]
            fcq = fc_ref[ri, :]
            new = []
            for hh in range(2):
                dk_acc, dv_acc = carry[hh]
                c0 = FOX_D * hh
                fq = jnp.sum(jnp.where(lane == 2 * p + hh, fcq, 0.0), axis=-1, keepdims=True)
                pr = jnp.exp(_nt(qn, kmasks[hh]) + fq - fks[hh] - lse_ref[ri, c0:c0 + 1])
                if diagonal:
                    pr = jnp.where(causal, pr, 0.0)
                ds = (pr * (_nt(do, vmasks[hh]) - delta_s[ri, c0:c0 + 1])).astype(BF16)
                dq_s[hh, ri, :] += _nn(ds, kaugs[hh])
                new.append((dk_acc + _tn(jnp.where(hms[hh], qn, one), ds), dv_acc + _tn(do, pr.astype(BF16))))
            return tuple(new)

        init = tuple((jnp.zeros((LANE, tq), F32), jnp.zeros((LANE, tq), F32)) for _ in range(2))
        carry = qloop(kj, init, True)
        (dk0, dv0), (dk1, dv1) = lax.fori_loop(kj + 1, NQ, functools.partial(qloop, diagonal=False), carry)
        dks, dvs = [dk0.T, dk1.T], [dv0.T, dv1.T]

        dkn = jnp.where(lo_half, dks[0], dks[1])
        _, _, _, _, kh, _, rkk = _fox_prep(z_ref, gq_v, gk_v, rk, lo_half)
        u = dkn * gk_v
        dz_ref[rk, LANE:2 * LANE] = (rkk * (u - kh * _pair_mean(u * kh, lo_half))).astype(BF16)
        dz_ref[rk, 2 * LANE:3 * LANE] = jnp.where(lo_half, dvs[0], dvs[1]).astype(BF16)
        dgk_ref[...] += _rowsum8(dkn * kh)
        dfk_s[rk, :] = jnp.where(lane == 2 * p, -dks[0][:, FOX_D:FOX_D + 1],
                                 jnp.where(lane == 2 * p + 1, -dks[1][:, 0:1], 0.0))

        @pl.when(kj == NQ - 1)
        def _():
            def fin(i, carry):
                r = pl.ds(pl.multiple_of(i * tq, tq), tq)
                d0, d1 = dq_s[0, r, :], dq_s[1, r, :]
                dqn = jnp.where(lo_half, d0, d1)
                _, _, _, qh, _, rqq, _ = _fox_prep(z_ref, gq_v, gk_v, r, lo_half)
                u = dqn * gq_v * (FOX_D ** -0.5)
                dz_ref[r, 0:LANE] = (rqq * (u - qh * _pair_mean(u * qh, lo_half))).astype(BF16)
                dgq_ref[...] += _rowsum8(dqn * qh) * (FOX_D ** -0.5)
                dfc_ref[r, :] = dfk_s[r, :] + jnp.where(lane == 2 * p, d0[:, FOX_D:FOX_D + 1],
                                                        jnp.where(lane == 2 * p + 1, d1[:, 0:1], 0.0))
                return carry
            lax.fori_loop(0, NQ, fin, 0)

    vec = pl.BlockSpec((1, LANE), lambda b, p, k: (0, 0))
    col = pl.BlockSpec((T, LANE), lambda b, p, k: (b, p))
    part = pl.BlockSpec((8, LANE), lambda b, p, k: (b * FOX_P + p, 0))
    if nsw:
        body = _hosting(body, 8, 4, 7, nsw, _chip_swap_phases, (B, FOX_P, NQ))
    return pl.pallas_call(
        body, name="fox_bwd", grid=(B, FOX_P, NQ),
        in_specs=[pl.BlockSpec((T, 384), lambda b, p, k: (b, p)), col, col, col,
                  pl.BlockSpec((T, LANE), lambda b, p, k: (b, 0)), pl.BlockSpec((1, 8, T), lambda b, p, k: (b, 0, 0)),
                  vec, vec] + [ANY] * nsw,
        out_specs=[pl.BlockSpec((T, 384), lambda b, p, k: (b, p)), col, part, part] + [ANY] * nsw,
        out_shape=[S((N, 1536), BF16), S((N, 512), F32), S((B * FOX_P * 8, LANE), F32), S((B * FOX_P * 8, LANE), F32)]
        + [S(p.shape, p.dtype) for p in swap],
        scratch_shapes=[pltpu.VMEM((T, LANE), BF16)] * 4 + [pltpu.VMEM((T, LANE), F32), pltpu.VMEM((2, T, LANE), F32),
                                                            pltpu.VMEM((T, LANE), F32)]
        + (_chip_swap_sems(nsw) if nsw else []),
        compiler_params=_cp(("arbitrary",) * 3 if nsw else ("parallel", "parallel", "arbitrary")),
    )(z, dy, y, lse, fc, fct, gq, gk, *swap)


def _mem_scores(z_ref, kv_ref, gq, gk, h):
    c = slice(MEM_D * h, MEM_D * (h + 1))
    q, k = z_ref[:, c].astype(F32), kv_ref[:, c]
    rq = lax.rsqrt(jnp.mean(q * q, axis=-1, keepdims=True) + EPS)
    rk = lax.rsqrt(jnp.mean(k * k, axis=-1, keepdims=True) + EPS)
    qh, kh = q * rq, k * rk
    qn = (qh * gq * (MEM_D ** -0.5)).astype(BF16)
    kn = (kh * gk).astype(BF16)
    s = _nt(qn, kn)
    pe = jnp.exp(s - jnp.max(s, axis=-1, keepdims=True))
    pn = pe / jnp.sum(pe, axis=-1, keepdims=True)
    return pn, qn, kn, qh, kh, rq, rk


def _mem_fwd(z, memkv, gq, gk, B, T, M, tq=1024):
    N = B * T
    tq = min(tq, T)
    NQ = T // tq
    W = MEM_H * MEM_D

    def body(z_ref, kv_ref, gq_ref, gk_ref, y_ref):
        for h in range(MEM_H):
            pn = _mem_scores(z_ref, kv_ref, gq_ref[...], gk_ref[...], h)[0]
            v = kv_ref[:, W + MEM_D * h:W + MEM_D * (h + 1)].astype(BF16)
            y_ref[:, MEM_D * h:MEM_D * (h + 1)] = _nn(pn.astype(BF16), v).astype(BF16)

    vec = pl.BlockSpec((1, LANE), lambda b, q: (0, 0))
    return pl.pallas_call(
        body, name="mem_fwd", grid=(B, NQ),
        in_specs=[pl.BlockSpec((tq, W), lambda b, q: (b * NQ + q, C_MQ // W)),
                  pl.BlockSpec((M, 2 * W), lambda b, q: (b, 0)), vec, vec],
        out_specs=pl.BlockSpec((tq, W), lambda b, q: (b * NQ + q, 0)), out_shape=S((N, W), BF16),
        compiler_params=_cp(("parallel", "parallel")),
    )(z, memkv, gq, gk)


def _mem_bwd(z, memkv, dy, gq, gk, B, T, M, tq=1024):
    N = B * T
    tq = min(tq, T)
    NQ = T // tq
    W = MEM_H * MEM_D

    def body(z_ref, kv_ref, dy_ref, gq_ref, gk_ref, dz_ref, dkv_ref, dgq_ref, dgk_ref, acc):
        qi = pl.program_id(1)
        gq_v, gk_v = gq_ref[...], gk_ref[...]

        @pl.when(qi == 0)
        def _():
            acc[...] = jnp.zeros_like(acc)
            dgq_ref[...] = jnp.zeros_like(dgq_ref)
            dgk_ref[...] = jnp.zeros_like(dgk_ref)

        for h in range(MEM_H):
            c = slice(MEM_D * h, MEM_D * (h + 1))
            cv = slice(W + MEM_D * h, W + MEM_D * (h + 1))
            pn, qn, kn, qh, _, rq, _ = _mem_scores(z_ref, kv_ref, gq_v, gk_v, h)
            do = dy_ref[:, c].astype(BF16)
            dp = _nt(do, kv_ref[:, cv].astype(BF16))
            ds = (pn * (dp - jnp.sum(dp * pn, axis=-1, keepdims=True))).astype(BF16)
            dqn = _nn(ds, kn)
            acc[:, c] += _tn(ds, qn)
            acc[:, cv] += _tn(pn.astype(BF16), do)
            u = dqn * gq_v * (MEM_D ** -0.5)
            dz_ref[:, c] = (rq * (u - qh * jnp.mean(u * qh, axis=-1, keepdims=True))).astype(BF16)
            dgq_ref[...] += _rowsum8(dqn * qh) * (MEM_D ** -0.5)

        @pl.when(qi == NQ - 1)
        def _():
            for h in range(MEM_H):
                c = slice(MEM_D * h, MEM_D * (h + 1))
                cv = slice(W + MEM_D * h, W + MEM_D * (h + 1))
                k = kv_ref[:, c]
                rk = lax.rsqrt(jnp.mean(k * k, axis=-1, keepdims=True) + EPS)
                kh = k * rk
                dkn = acc[:, c]
                u = dkn * gk_v
                dkv_ref[:, c] = (rk * (u - kh * jnp.mean(u * kh, axis=-1, keepdims=True))).astype(BF16)
                dkv_ref[:, cv] = acc[:, cv].astype(BF16)
                dgk_ref[...] += _rowsum8(dkn * kh)

    vec = pl.BlockSpec((1, LANE), lambda b, q: (0, 0))
    part = pl.BlockSpec((8, LANE), lambda b, q: (b, 0))
    return pl.pallas_call(
        body, name="mem_bwd", grid=(B, NQ),
        in_specs=[pl.BlockSpec((tq, W), lambda b, q: (b * NQ + q, C_MQ // W)),
                  pl.BlockSpec((M, 2 * W), lambda b, q: (b, 0)), pl.BlockSpec((tq, W), lambda b, q: (b * NQ + q, 0)),
                  vec, vec],
        out_specs=[pl.BlockSpec((tq, W), lambda b, q: (b * NQ + q, 0)), pl.BlockSpec((M, 2 * W), lambda b, q: (b, 0)),
                   part, part],
        out_shape=[S((N, W), BF16), S((B * M, 2 * W), BF16), S((B * 8, LANE), F32), S((B * 8, LANE), F32)],
        scratch_shapes=[pltpu.VMEM((M, 2 * W), F32)], compiler_params=_cp(("parallel", "arbitrary")),
    )(z, memkv, dy, gq, gk)


def _merge_fwd(ya, yb, yc, z, x, wa, wb, wc, wo, g_next, tm=512):
    n, d = x.shape
    wdt = ya.shape[1]
    gb = C_GATE // d

    def body(ya_ref, yb_ref, yc_ref, g0_ref, g1_ref, g2_ref, x_ref, wa_ref, wb_ref, wc_ref, wo_ref, gn_ref,
             x1_ref, mg_ref, ua_ref, ub_ref, uc_ref, h_ref):
        merged = jnp.zeros((tm, d), F32)
        for y_ref, g_ref, w_ref, u_ref in ((ya_ref, g0_ref, wa_ref, ua_ref), (yb_ref, g1_ref, wb_ref, ub_ref),
                                           (yc_ref, g2_ref, wc_ref, uc_ref)):
            u = _nn(y_ref[...], w_ref[...])
            u_ref[...] = u.astype(BF16)
            merged = merged + jax.nn.sigmoid(g_ref[...].astype(F32)) * u
        mb = merged.astype(BF16)
        mg_ref[...] = mb
        x1 = x_ref[...] + _nn(mb, wo_ref[...])
        x1_ref[...] = x1
        h_ref[...] = (x1 * lax.rsqrt(jnp.mean(x1 * x1, axis=-1, keepdims=True) + EPS) * gn_ref[...]).astype(BF16)

    yt = pl.BlockSpec((tm, wdt), lambda i: (i, 0))
    xt = pl.BlockSpec((tm, d), lambda i: (i, 0))
    wbr = pl.BlockSpec((wdt, d), lambda i: (0, 0))
    gates = [pl.BlockSpec((tm, d), functools.partial(lambda i, k: (i, gb + k), k=k)) for k in range(3)]
    return pl.pallas_call(
        body, name="merge_fwd", grid=(n // tm,),
        in_specs=[yt, yt, yt] + gates + [xt, wbr, wbr, wbr, pl.BlockSpec((d, d), lambda i: (0, 0)),
                                         pl.BlockSpec((1, d), lambda i: (0, 0))],
        out_specs=[xt] * 6, out_shape=[S((n, d), F32)] + [S((n, d), BF16)] * 5, compiler_params=_cp(("parallel",)),
    )(ya, yb, yc, z, z, z, x, wa, wb, wc, wo, g_next)


def _merge_bwd(dx1, z, ua, ub, uc, wa, wb, wc, wo, tm=512):
    n, d = dx1.shape
    wdt = wa.shape[0]
    gb = C_GATE // d

    def body(dx_ref, g0_ref, g1_ref, g2_ref, ua_ref, ub_ref, uc_ref, wa_ref, wb_ref, wc_ref, wo_ref,
             dg_ref, dya_ref, dyb_ref, dyc_ref, dua_ref, dub_ref, duc_ref):
        dm = _nt(dx_ref[...].astype(BF16), wo_ref[...])
        for k, (g_ref, u_ref, w_ref, dy_ref, du_ref) in enumerate((
                (g0_ref, ua_ref, wa_ref, dya_ref, dua_ref), (g1_ref, ub_ref, wb_ref, dyb_ref, dub_ref),
                (g2_ref, uc_ref, wc_ref, dyc_ref, duc_ref))):
            g = jax.nn.sigmoid(g_ref[...].astype(F32))
            du = (dm * g).astype(BF16)
            du_ref[...] = du
            dg_ref[:, d * k:d * (k + 1)] = (dm * u_ref[...].astype(F32) * g * (1.0 - g)).astype(BF16)
            dy_ref[...] = _nt(du, w_ref[...])

    yt = pl.BlockSpec((tm, wdt), lambda i: (i, 0))
    xt = pl.BlockSpec((tm, d), lambda i: (i, 0))
    wbr = pl.BlockSpec((wdt, d), lambda i: (0, 0))
    gates = [pl.BlockSpec((tm, d), functools.partial(lambda i, k: (i, gb + k), k=k)) for k in range(3)]
    return pl.pallas_call(
        body, name="merge_bwd", grid=(n // tm,),
        in_specs=[xt] + gates + [xt, xt, xt, wbr, wbr, wbr, pl.BlockSpec((d, d), lambda i: (0, 0))],
        out_specs=[pl.BlockSpec((tm, 3 * d), lambda i: (i, 0)), yt, yt, yt, xt, xt, xt],
        out_shape=[S((n, 3 * d), BF16)] + [S((n, wdt), F32)] * 3 + [S((n, d), BF16)] * 3,
        compiler_params=_cp(("parallel",)),
    )(dx1, z, z, z, ua, ub, uc, wa, wb, wc, wo)


FFN_TN = 1408
TN_TM = 2048
INV_SQRT2 = 0.7071067811865476
INV_SQRT_2PI = 0.3989422804014327


def _conv_shifted(a, prev, first, tm):
    row = _iota(a.shape, 0)
    p7 = jnp.where(first, 0.0, prev[7:8, :])
    p6 = jnp.where(first, 0.0, prev[6:7, :])
    a1 = jnp.where(row == 0, p7, pltpu.roll(a, 1, 0))
    a2 = jnp.where(row == 0, p6, jnp.where(row == 1, p7, pltpu.roll(a, 2, 0)))
    return a1, a2


def _ffn_act_fwd(up, cw, cb, B, T, tm=1024):
    N = B * T
    tm = min(tm, T)
    dff = cw.shape[1]
    NT, NJ, tn = T // tm, dff // FFN_TN, FFN_TN

    def body(a_ref, v_ref, cw_ref, cb_ref, y_ref, c_ref, carry):
        t = pl.program_id(2)
        a = a_ref[...].astype(F32)
        a1, a2 = _conv_shifted(a, carry[...], t == 0, tm)
        w = cw_ref[...]
        ac = w[0:1, :] * a2 + w[1:2, :] * a1 + w[2:3, :] * a + cb_ref[...]
        cdf = 0.5 * (1.0 + lax.erf(ac * INV_SQRT2))
        y_ref[...] = (ac * cdf * v_ref[...].astype(F32)).astype(BF16)
        c_ref[...] = cdf.astype(BF16)
        carry[...] = a[tm - 8:tm, :]

    return pl.pallas_call(
        body, name="ffn_act_fwd", grid=(B, NJ, NT),
        in_specs=[pl.BlockSpec((tm, tn), lambda b, j, t: (b * NT + t, j)),
                  pl.BlockSpec((tm, tn), lambda b, j, t: (b * NT + t, NJ + j)),
                  pl.BlockSpec((3, tn), lambda b, j, t: (0, j)), pl.BlockSpec((1, tn), lambda b, j, t: (0, j))],
        out_specs=[pl.BlockSpec((tm, tn), lambda b, j, t: (b * NT + t, j))] * 2, out_shape=[S((N, dff), BF16)] * 2,
        scratch_shapes=[pltpu.VMEM((8, tn), F32)], compiler_params=_cp(("parallel", "parallel", "arbitrary")),
    )(up, up, cw, cb)


def _ffn_down_loss(y, wd, x1, tgt, tm=512):
    n, d = x1.shape
    kf = y.shape[1]

    def body(y_ref, w_ref, x_ref, t_ref, dx_ref, ls_ref):
        err = x_ref[...] + _nn(y_ref[...], w_ref[...]) - t_ref[...]
        dx_ref[...] = err * (1.0 / d)

        @pl.when(pl.program_id(0) == 0)
        def _():
            ls_ref[...] = jnp.zeros_like(ls_ref)

        ls_ref[...] += _rowsum8(err * err) * (0.5 / d)

    xt = pl.BlockSpec((tm, d), lambda i: (i, 0))
    return pl.pallas_call(
        body, name="ffn_down_loss", grid=(n // tm,),
        in_specs=[pl.BlockSpec((tm, kf), lambda i: (i, 0)), pl.BlockSpec((kf, d), lambda i: (0, 0)), xt, xt],
        out_specs=[xt, pl.BlockSpec((8, d), lambda i: (0, 0))], out_shape=[S((n, d), F32), S((8, d), F32)],
        compiler_params=_cp(("arbitrary",)),
    )(y, wd, x1, tgt)


def _ffn_act_bwd1(dx2, wd, up, cdf, cw, cb, B, T, tm=512):
    N = B * T
    tm = min(tm, T)
    d = dx2.shape[1]
    dff = cw.shape[1]
    NT, NJ, tn = T // tm, dff // FFN_TN, FFN_TN

    def body(dx_ref, w_ref, a_ref, v_ref, c_ref, cw_ref, cb_ref, dac_ref, dv_ref, dcw_ref, dcb_ref, carry):
        b, t = pl.program_id(1), pl.program_id(2)
        a = a_ref[...].astype(F32)
        a1, a2 = _conv_shifted(a, carry[...], t == 0, tm)
        carry[...] = a[tm - 8:tm, :]
        w = cw_ref[...]
        ac = w[0:1, :] * a2 + w[1:2, :] * a1 + w[2:3, :] * a + cb_ref[...]
        dy = _nt(dx_ref[...].astype(BF16), w_ref[...])
        cdf = c_ref[...].astype(F32)
        dv_ref[...] = (dy * ac * cdf).astype(BF16)
        dac = dy * v_ref[...].astype(F32) * (cdf + ac * jnp.exp(-0.5 * ac * ac) * INV_SQRT_2PI)
        dac_ref[...] = dac

        @pl.when((b == 0) & (t == 0))
        def _():
            dcw_ref[...] = jnp.zeros_like(dcw_ref)
            dcb_ref[...] = jnp.zeros_like(dcb_ref)

        dcw_ref[0:8, :] += _rowsum8(dac * a2)
        dcw_ref[8:16, :] += _rowsum8(dac * a1)
        dcw_ref[16:24, :] += _rowsum8(dac * a)
        dcb_ref[...] += _rowsum8(dac)

    return pl.pallas_call(
        body, name="ffn_act_bwd1", grid=(NJ, B, NT),
        in_specs=[pl.BlockSpec((tm, d), lambda j, b, t: (b * NT + t, 0)), pl.BlockSpec((tn, d), lambda j, b, t: (j, 0)),
                  pl.BlockSpec((tm, tn), lambda j, b, t: (b * NT + t, j)),
                  pl.BlockSpec((tm, tn), lambda j, b, t: (b * NT + t, NJ + j)),
                  pl.BlockSpec((tm, tn), lambda j, b, t: (b * NT + t, j)),
                  pl.BlockSpec((3, tn), lambda j, b, t: (0, j)), pl.BlockSpec((1, tn), lambda j, b, t: (0, j))],
        out_specs=[pl.BlockSpec((tm, tn), lambda j, b, t: (b * NT + t, j)),
                   pl.BlockSpec((tm, tn), lambda j, b, t: (b * NT + t, j)),
                   pl.BlockSpec((24, tn), lambda j, b, t: (0, j)), pl.BlockSpec((8, tn), lambda j, b, t: (0, j))],
        out_shape=[S((N, dff), F32), S((N, dff), BF16), S((24, dff), F32), S((8, dff), F32)],
        scratch_shapes=[pltpu.VMEM((8, tn), F32)], compiler_params=_cp(("parallel", "arbitrary", "arbitrary")),
    )(dx2, wd, up, up, cdf, cw, cb)


def _ffn_act_bwd2(dac, cw, B, T, tm=1024):
    N = B * T
    tm = min(tm, T)
    dff = cw.shape[1]
    NT, NJ, tn = T // tm, dff // FFN_TN, FFN_TN
    last8 = N // 8 - 1

    def body(d_ref, nx_ref, cw_ref, da_ref):
        t = pl.program_id(2)
        dd = d_ref[...]
        row = _iota(dd.shape, 0)
        last = t == NT - 1
        n0 = jnp.where(last, 0.0, nx_ref[0:1, :])
        n1 = jnp.where(last, 0.0, nx_ref[1:2, :])
        d1 = jnp.where(row == tm - 1, n0, pltpu.roll(dd, tm - 1, 0))
        d2 = jnp.where(row == tm - 1, n1, jnp.where(row == tm - 2, n0, pltpu.roll(dd, tm - 2, 0)))
        w = cw_ref[...]
        da_ref[...] = (w[2:3, :] * dd + w[1:2, :] * d1 + w[0:1, :] * d2).astype(BF16)

    return pl.pallas_call(
        body, name="ffn_act_bwd2", grid=(B, NJ, NT),
        in_specs=[pl.BlockSpec((tm, tn), lambda b, j, t: (b * NT + t, j)),
                  pl.BlockSpec((8, tn), lambda b, j, t: (jnp.minimum((b * NT + t + 1) * (tm // 8), last8), j)),
                  pl.BlockSpec((3, tn), lambda b, j, t: (0, j))],
        out_specs=pl.BlockSpec((tm, tn), lambda b, j, t: (b * NT + t, j)), out_shape=S((N, dff), BF16),
        compiler_params=_cp(("parallel", "parallel", "parallel")),
    )(dac, dac, cw)


def _fold_rows(p, name):
    r, c = p.shape[0] // 8, p.shape[1]

    def body(p_ref, o_ref):
        for j in range(r):
            o_ref[j:j + 1, :] = jnp.sum(p_ref[8 * j:8 * (j + 1), :], axis=0, keepdims=True)

    return pl.pallas_call(body, name=name, out_shape=S((r, c), F32), compiler_params=_cp())(p)


def _small_reduce(lbl, dg_mix, dg_mem, dlb_p, dgn_p, dfb_p, dgq_p, dgk_p, dmq_p, dmk_p, dg_ffn, dcb_p, loss_p):
    d, dff = dg_mix.shape[1], dcb_p.shape[1]
    nbh = dlb_p.shape[0] // (8 * HG_H)

    def colsum(ref):
        return jnp.sum(ref[...], axis=0, keepdims=True)

    def body(lbl_ref, mix_ref, mem_ref, dlb_ref, dgn_ref, dfb_ref, dgq_ref, dgk_ref, dmq_ref, dmk_ref, ffn_ref, dcb_ref,
             ls_ref, o_mix, o_mem, o_lb, o_hgn, o_fb, o_fq, o_fk, o_mq, o_mk, o_ffn, o_cb, o_loss):
        o_mix[...], o_mem[...], o_ffn[...], o_cb[...] = colsum(mix_ref), colsum(mem_ref), colsum(ffn_ref), colsum(dcb_ref)
        o_hgn[...], o_fb[...], o_mq[...], o_mk[...] = colsum(dgn_ref), colsum(dfb_ref), colsum(dmq_ref), colsum(dmk_ref)
        for src, dst in ((dgq_ref, o_fq), (dgk_ref, o_fk)):
            v = colsum(src)
            dst[...] = v + pltpu.roll(v, FOX_D, 1)
        o_loss[...] = jnp.zeros((1, LANE), F32) + jnp.sum(colsum(ls_ref), axis=-1, keepdims=True)
        logits = lbl_ref[...]
        e = jnp.exp(logits - jnp.max(logits, axis=0, keepdims=True))
        pr = e / jnp.sum(e, axis=0, keepdims=True)
        rows = _iota((8, LANE), 0)
        for h in range(HG_H):
            acc = jnp.zeros((8, LANE), F32)
            for b in range(nbh):
                acc = acc + dlb_ref[8 * (b * HG_H + h):8 * (b * HG_H + h + 1), :]
            dlb = jnp.sum(acc, axis=0, keepdims=True)
            c = slice(LANE * h, LANE * (h + 1))
            p0 = pr[0:1, c]
            first = _iota((logits.shape[0], LANE), 0) == 0
            o_lb[:, c] = pr[:, c] * (jnp.where(first, 1.0, 0.0) - p0) * dlb

    outs = [S((1, d), F32), S((1, d), F32), S(lbl.shape, F32)] + [S((1, LANE), F32)] * 6 + \
           [S((1, d), F32), S((1, dff), F32), S((1, LANE), F32)]
    return pl.pallas_call(body, name="small_reduce", out_shape=outs, compiler_params=_cp())(
        lbl, dg_mix, dg_mem, dlb_p, dgn_p, dfb_p, dgq_p, dgk_p, dmq_p, dmk_p, dg_ffn, dcb_p, loss_p)


def _in_col_pieces():
    hw, fw = HG_H * HG_D, FOX_H * FOX_D
    fox0, ff0 = 4 * hw, 4 * hw + 3 * fw
    mq0 = ff0 + FOX_H
    gate0 = mq0 + MEM_H * MEM_D
    pieces = []
    for p in range(FOX_P):
        pieces += [(fox0 + j * fw + LANE * p, LANE) for j in range(3)]
    pieces.append((mq0, MEM_H * MEM_D))
    for h in range(HG_H):
        pieces += [(j * hw + HG_D * h, HG_D) for j in range(4)]
    pieces.append((gate0, C_FF - C_GATE))
    pieces.append((ff0, FOX_H))
    return pieces


def _perm_from_blocks(blocks):
    n_blk, _, c = blocks.shape
    parts = []
    for s, n in _in_col_pieces():
        lo = s
        while lo < s + n:
            d = lo // c
            hi = min(s + n, (d + 1) * c)
            parts.append(blocks[d][:, lo - d * c:hi - d * c])
            lo = hi
    parts.append(jnp.zeros((blocks.shape[1], C_END - C_FF - FOX_H), blocks.dtype))
    return jnp.concatenate(parts, axis=1)


def _unperm_blocks(segs, n_blk):
    starts = [0]
    for a in segs:
        starts.append(starts[-1] + a.shape[1])
    new_start, placed = 0, []
    for s, n in _in_col_pieces():
        placed.append((s, new_start, n))
        new_start += n
    placed.sort()
    c = sum(n for _, _, n in placed) // n_blk
    blocks = []
    for d in range(n_blk):
        parts = []
        for s, ns, n in placed:
            lo, hi = max(s, d * c), min(s + n, (d + 1) * c)
            if lo < hi:
                i = max(j for j in range(len(segs)) if starts[j] <= ns)
                parts.append(segs[i][:, ns + lo - s - starts[i]:ns + hi - s - starts[i]])
        blocks.append(jnp.concatenate(parts, axis=1))
    return jnp.stack(blocks)


def _local_step(x2, mem2, tgt, sm, W, B, T, M, ex=None):
    fbias = jnp.pad(sm["fox_f_bias"], ((0, 0), (0, LANE - FOX_H)))
    gq2 = jnp.concatenate([sm["fox_q_norm_g"]] * 2, axis=1)
    gk2 = jnp.concatenate([sm["fox_k_norm_g"]] * 2, axis=1)
    lbl = sm["hgrn_lb_logits"]
    if ex:
        h, *first = _rmsnorm_cast(x2, sm["norm_mix_g"], "norm_mix", gather=ex.first_blocks())
        W = ex.unpack_first(first)
    else:
        h = _rmsnorm_cast(x2, sm["norm_mix_g"], "norm_mix")
    z = _mm_nn(h, W["w_in"], BF16, "proj_in", 512, C_END)
    memn = _rmsnorm_cast(mem2, sm["norm_mem_g"], "norm_mem", tm=256)
    memkv = _mm_nn(memn, W["mem_kv_w"], F32, "proj_memkv", 256, 512)
    ya, o_raw, states, a_mat = _hgrn_fwd(z, lbl, sm["hgrn_norm_g"], B, T)
    fc, fct = _fox_gate_fwd(z, fbias, B, T)
    yb, lse, *late = _fox_fwd(z, fc, fct, gq2, gk2, B, T, gather=ex.late_blocks() if ex else ())
    if ex:
        W = {**W, **ex.unpack_late(late)}
    yc = _mem_fwd(z, memkv, sm["mem_q_norm_g"], sm["mem_k_norm_g"], B, T, M)
    x1, merged, ua, ub, uc, h2 = _merge_fwd(ya, yb, yc, z, x2, W["w_br_hgrn"], W["w_br_fox"], W["w_br_mem"], W["w_out"],
                                            sm["norm_ffn_g"])
    up = _mm_nn(h2, W["ffn_w_up"], BF16, "ffn_up", 512, 4 * FFN_TN)
    yf, cdf = _ffn_act_fwd(up, W["ffn_conv_w"], sm["ffn_conv_b"], B, T)
    dx2, loss_p = _ffn_down_loss(yf, W["ffn_w_down"], x1, tgt)
    dff = W["ffn_conv_w"].shape[1]
    dac, dv, dcw_p, dcb_p = _ffn_act_bwd1(dx2, W["ffn_w_down"], up, cdf, W["ffn_conv_w"], sm["ffn_conv_b"], B, T)
    da = _ffn_act_bwd2(dac, W["ffn_conv_w"], B, T)
    g = {"ffn_conv_w": _fold_rows(dcw_p, "g_conv_w")}
    g["ffn_w_down"] = _mm_tn(yf, dx2, "g_w_down", TN_TM, 512)
    dh2 = _mm_nt_sum([(da, 0, dff, 0), (dv, 0, dff, dff)], W["ffn_w_up"], "dh2", 512)
    g["ffn_w_up"] = [_mm_tn(h2, da, "g_w_up_a", TN_TM, FFN_TN), _mm_tn(h2, dv, "g_w_up_v", TN_TM, FFN_TN)]
    dx1, dg_ffn = _rmsnorm_bwd(dh2, x1, sm["norm_ffn_g"], dx2, "norm_ffn_bwd")
    g["w_out"] = _mm_tn(merged, dx1, "g_w_out", TN_TM, 1024)
    dgate, dya, dyb, dyc, dua, dub, duc = _merge_bwd(dx1, z, ua, ub, uc, W["w_br_hgrn"], W["w_br_fox"], W["w_br_mem"],
                                                    W["w_out"])
    g["w_br_hgrn"] = _mm_tn(ya, dua, "g_w_br_hgrn", 2 * TN_TM, 1024)
    g["w_br_fox"] = _mm_tn(yb, dub, "g_w_br_fox", 2 * TN_TM, 1024)
    g["w_br_mem"] = _mm_tn(yc, duc, "g_w_br_mem", 2 * TN_TM, 1024)
    early_pk = ex.early_grads(g) if ex else ()
    dz_hg, dlb_p, dgn_p, *early_sib = _hgrn_bwd(z, o_raw, states, a_mat, dya, lbl, sm["hgrn_norm_g"], B, T,
                                                swap_sibling=early_pk)
    dz_fox, dfc, dgq_p, dgk_p, *early_chips = _fox_bwd(z, dyb, yb, lse, fc, fct, gq2, gk2, B, T,
                                                       swap=ex.pair_sums(early_pk, early_sib, "early") if ex else ())
    dz_ff, dfb_p = _fox_gate_bwd(dfc, z, fbias, B, T)
    dz_mq, dkv, dmq_p, dmk_p = _mem_bwd(z, memkv, dyc, sm["mem_q_norm_g"], sm["mem_k_norm_g"], B, T, M)
    g["mem_kv_w"] = _mm_tn(memn, dkv, "g_mem_kv_w", 256, 512)
    dmemn = _mm_nt_sum([(dkv, 0, dkv.shape[1], 0)], W["mem_kv_w"], "d_memn", 256)
    _, dg_mem = _rmsnorm_bwd(dmemn, mem2, sm["norm_mem_g"], None, "norm_mem_bwd", tm=256)
    d = x2.shape[1]
    parts = [(dz_fox, 0, C_MQ - C_FOX, C_FOX), (dz_mq, 0, C_HG - C_MQ, C_MQ), (dz_hg, 0, C_GATE - C_HG, C_HG)]
    parts += [(dgate, d * k, d, C_GATE + d * k) for k in range(3)] + [(dz_ff, 0, C_END - C_FF, C_FF)]
    g["w_in"] = [_mm_tn(h, dzs, "g_w_in_%d" % i, 2 * TN_TM,
                        max(t for t in (1024, 768, 512, LANE) if dzs.shape[1] % t == 0))
                 for i, dzs in enumerate((dz_fox, dz_mq, dz_hg, dgate, dz_ff))]
    sums = None
    if ex:
        last_pk = ex.last_grads(g)
        last_sib = _swap_with_sibling(last_pk, "rs_sibling_last")
        dh, last_chips = _mm_nt_sum(parts, W["w_in"], "dh", 512, swap=ex.pair_sums(last_pk, last_sib, "last"))
        sums = (ex.final_sums(early_pk, early_sib, early_chips, "early"),
                ex.final_sums(last_pk, last_sib, last_chips, "last"))
    else:
        dh = _mm_nt_sum(parts, W["w_in"], "dh", 512)
    grad_x, dg_mix = _rmsnorm_bwd(dh, x2, sm["norm_mix_g"], dx1, "norm_mix_bwd")
    small = _small_reduce(lbl, dg_mix, dg_mem, dlb_p, dgn_p, dfb_p, dgq_p, dgk_p, dmq_p, dmk_p, dg_ffn, dcb_p, loss_p)
    names = ("norm_mix_g", "norm_mem_g", "hgrn_lb_logits", "hgrn_norm_g", "fox_f_bias", "fox_q_norm_g", "fox_k_norm_g",
             "mem_q_norm_g", "mem_k_norm_g", "norm_ffn_g", "ffn_conv_b", "loss")
    g.update(dict(zip(names, small)))
    return grad_x, g, sums


ANY = pl.BlockSpec(memory_space=pl.ANY)


def _position():
    return lax.axis_index("x"), lax.axis_index("y"), lax.axis_index("c")


def _all_gather(blocks, name):
    nb = len(blocks)

    def body(*refs):
        start, forward, finish = _gather_phases(refs[:nb], refs[nb:2 * nb], *refs[2 * nb:])
        start()
        forward()
        finish()

    return pl.pallas_call(
        body, name=name, out_shape=_gather_shapes(blocks), in_specs=[ANY] * nb, out_specs=[ANY] * nb,
        scratch_shapes=_gather_sems(nb),
    )(*blocks)


def _hosting(body, n_in, n_out, n_scratch, n_x, make_phases, grid):
    n_steps = math.prod(grid)

    def hosted(*refs):
        a = n_in + n_x
        b = a + n_out + n_x
        ins, xs = refs[:n_in], refs[n_in:a]
        outs, x_outs = refs[a:a + n_out], refs[a + n_out:b]
        scratch, sems = refs[b:b + n_scratch], refs[b + n_scratch:]
        step = 0
        for ax, n in enumerate(grid):
            step = step * n + pl.program_id(ax)
        phases = make_phases(xs, x_outs, *sems)
        pl.when(step == 0)(phases[0])
        for ph in phases[1:-1]:
            pl.when(step == n_steps // 2)(ph)
        body(*ins, *outs, *scratch)
        pl.when(step == n_steps - 1)(phases[-1])

    return hosted


def _gather_shapes(blocks):
    return [S((N_DEV,) + b.shape, b.dtype) for b in blocks]


def _gather_sems(nb):
    return [pltpu.SemaphoreType.DMA((7 * nb,)), pltpu.SemaphoreType.DMA((7 * nb,)), pltpu.SemaphoreType.DMA((nb,))]


def _gather_phases(x_refs, out_refs, send_sems, recv_sems, local_sems):
    nb = len(x_refs)
    x, y, c = _position()
    me, sibling = (x, y, c), (x, y, 1 - c)
    chips = [(1 - x, y), (x, 1 - y), (1 - x, 1 - y)]

    def copy(i, k, blk, to, own=False):
        px, py, pc = blk
        slot = out_refs[i].at[4 * px + 2 * py + pc]
        return pltpu.make_async_remote_copy(
            src_ref=x_refs[i] if own else slot, dst_ref=slot, send_sem=send_sems.at[7 * i + k],
            recv_sem=recv_sems.at[7 * i + k], device_id=to, device_id_type=MESH)

    def mine(i):
        return pltpu.make_async_copy(x_refs[i], out_refs[i].at[4 * x + 2 * y + c], local_sems.at[i])

    def first(i):
        return [copy(i, 0, me, sibling, own=True)] + [copy(i, 1 + j, me, (*chip, c), own=True)
                                                     for j, chip in enumerate(chips)]

    def passed(i, j):
        return copy(i, 4 + j, (*chips[j], c), sibling)

    def start():
        for i in range(nb):
            mine(i).start()
            for cp in first(i):
                cp.start()

    def forward():
        for i in range(nb):
            for j, chip in enumerate(chips):
                copy(i, 1 + j, (*chip, c), me).wait_recv()
                passed(i, j).start()

    def finish():
        for i in range(nb):
            copy(i, 0, sibling, me).wait_recv()
            for j, chip in enumerate(chips):
                copy(i, 4 + j, (*chip, 1 - c), me).wait_recv()
        for i in range(nb):
            for cp in first(i) + [passed(i, j) for j in range(3)]:
                cp.wait_send()
            mine(i).wait()

    return start, forward, finish


def _swap_with_sibling(pks, name):
    nb = len(pks)

    def body(*refs):
        start, finish = _sibling_swap_phases(refs[:nb], refs[nb:2 * nb], *refs[2 * nb:])
        start()
        finish()

    return pl.pallas_call(
        body, name=name, out_shape=_sibling_swap_shapes(pks), in_specs=[ANY] * nb, out_specs=[ANY] * nb,
        scratch_shapes=_sibling_swap_sems(nb),
    )(*pks)


def _sibling_swap_shapes(pks):
    return [S((4,) + p.shape[1:], p.dtype) for p in pks]


def _sibling_swap_sems(nb):
    return [pltpu.SemaphoreType.DMA((4 * nb,)), pltpu.SemaphoreType.DMA((4 * nb,))]


def _sibling_swap_phases(pk_refs, out_refs, send_sems, recv_sems):
    nb = len(pk_refs)
    x, y, c = _position()

    def copies():
        return [pltpu.make_async_remote_copy(
            src_ref=pk_refs[i].at[2 * k + 1 - c], dst_ref=out_refs[i].at[k], send_sem=send_sems.at[4 * i + k],
            recv_sem=recv_sems.at[4 * i + k], device_id=(x, y, 1 - c), device_id_type=MESH)
            for i in range(nb) for k in range(4)]

    def start():
        for cp in copies():
            cp.start()

    def finish():
        for cp in copies():
            cp.wait()

    return start, finish


def _swap_between_chips(pbs, name):
    nb = len(pbs)

    def body(*refs):
        start, finish = _chip_swap_phases(refs[:nb], refs[nb:2 * nb], *refs[2 * nb:])
        start()
        finish()

    return pl.pallas_call(
        body, name=name, out_shape=[S(p.shape, p.dtype) for p in pbs], in_specs=[ANY] * nb, out_specs=[ANY] * nb,
        scratch_shapes=_chip_swap_sems(nb),
    )(*pbs)


def _chip_swap_sems(nb):
    return [pltpu.SemaphoreType.DMA((3 * nb,)), pltpu.SemaphoreType.DMA((3 * nb,)), pltpu.SemaphoreType.DMA((nb,))]


def _chip_swap_phases(pb_refs, out_refs, send_sems, recv_sems, local_sems):
    nb = len(pb_refs)
    x, y, c = _position()
    me = 2 * x + y
    chips = [(1 - x, y), (x, 1 - y), (1 - x, 1 - y)]

    def local(i):
        return pltpu.make_async_copy(pb_refs[i].at[me], out_refs[i].at[me], local_sems.at[i])

    def send(i, j):
        cx, cy = chips[j]
        return pltpu.make_async_remote_copy(
            src_ref=pb_refs[i].at[2 * cx + cy], dst_ref=out_refs[i].at[me], send_sem=send_sems.at[3 * i + j],
            recv_sem=recv_sems.at[3 * i + j], device_id=(cx, cy, c), device_id_type=MESH)

    def arrival(i, j):
        cx, cy = chips[j]
        return pltpu.make_async_remote_copy(
            src_ref=pb_refs[i].at[me], dst_ref=out_refs[i].at[2 * cx + cy], send_sem=send_sems.at[3 * i + j],
            recv_sem=recv_sems.at[3 * i + j], device_id=(cx, cy, c), device_id_type=MESH)

    def start():
        for i in range(nb):
            local(i).start()
            for j in range(3):
                send(i, j).start()

    def finish():
        for i in range(nb):
            for j in range(3):
                arrival(i, j).wait_recv()
        for i in range(nb):
            for j in range(3):
                send(i, j).wait_send()
            local(i).wait()

    return start, finish


def _row_tile(r):
    return max(t for t in range(16, min(r, 1024) + 1, 16) if r % t == 0)


def _pair_sum_cast(pk, recv, core, name):
    _, r, l = pk.shape
    tr = _row_tile(r)

    def body(c_ref, a_ref, b_ref, o_ref):
        o_ref[...] = (a_ref[...] + b_ref[...]).astype(BF16)

    return pl.pallas_call(
        body, name=name,
        grid_spec=pltpu.PrefetchScalarGridSpec(
            num_scalar_prefetch=1, grid=(4, r // tr),
            in_specs=[pl.BlockSpec((None, tr, l), lambda k, i, c: (2 * k + c[0], i, 0)),
                      pl.BlockSpec((None, tr, l), lambda k, i, c: (k, i, 0))],
            out_specs=pl.BlockSpec((None, tr, l), lambda k, i, c: (k, i, 0))),
        out_shape=S((4, r, l), BF16), compiler_params=_cp(("parallel", "parallel")),
    )(core, pk, recv)


def _final_sum(pk, recv_sib, recv_chips, slot, chip, name):
    _, r, l = pk.shape
    tr = _row_tile(r)

    def body(s_ref, k_ref, a_ref, b_ref, rc_ref, o_ref):
        base = a_ref[...] + b_ref[...]
        acc = jnp.zeros_like(base)
        for j in range(4):
            acc = acc + jnp.where(k_ref[0] == j, base, rc_ref[j].astype(F32))
        o_ref[...] = acc

    return pl.pallas_call(
        body, name=name,
        grid_spec=pltpu.PrefetchScalarGridSpec(
            num_scalar_prefetch=2, grid=(r // tr,),
            in_specs=[pl.BlockSpec((None, tr, l), lambda i, s, k: (s[0], i, 0)),
                      pl.BlockSpec((None, tr, l), lambda i, s, k: (k[0], i, 0)),
                      pl.BlockSpec((4, tr, l), lambda i, s, k: (0, i, 0))],
            out_specs=pl.BlockSpec((tr, l), lambda i, s, k: (i, 0))),
        out_shape=S((r, l), F32), compiler_params=_cp(("parallel",)),
    )(slot, chip, pk, recv_sib, recv_chips)


def _adamw_math(w, g, m, v):
    m = ADAM_B1 * m + (1.0 - ADAM_B1) * g
    v = ADAM_B2 * v + (1.0 - ADAM_B2) * (g * g)
    m_hat = m / (1.0 - ADAM_B1 ** ADAM_STEP)
    v_hat = v / (1.0 - ADAM_B2 ** ADAM_STEP)
    return -ADAM_LR * (m_hat / (jnp.sqrt(v_hat) + ADAM_EPS) + ADAM_WD * w), m, v


def _adamw(w, g, m, v, name):
    r, c = w.shape
    tr = 512 if r % 512 == 0 else r

    def body(w_ref, g_ref, m_ref, v_ref, d_ref, nm_ref, nv_ref):
        d_ref[...], nm_ref[...], nv_ref[...] = _adamw_math(w_ref[...], g_ref[...], m_ref[...], v_ref[...])

    tile = pl.BlockSpec((tr, c), lambda i: (i, 0))
    return pl.pallas_call(
        body, name=name, grid=(r // tr,), in_specs=[tile] * 4, out_specs=[tile] * 3, out_shape=[S((r, c), F32)] * 3,
        compiler_params=_cp(("parallel",)),
    )(w, g, m, v)


def _small_update(gathered, w, m, v):
    def body(ga_ref, w_ref, m_ref, v_ref, g_ref, d_ref, nm_ref, nv_ref):
        g = ga_ref[0]
        for k in range(1, N_DEV):
            g = g + ga_ref[k]
        g_ref[...] = g
        d_ref[...], nm_ref[...], nv_ref[...] = _adamw_math(w_ref[...], g, m_ref[...], v_ref[...])

    return pl.pallas_call(body, name="small_update", out_shape=[S(w.shape, F32)] * 4, compiler_params=_cp())(
        gathered, w, m, v)


BIG = ("w_in", "mem_kv_w", "w_br_hgrn", "w_br_fox", "w_br_mem", "w_out", "ffn_w_up", "ffn_conv_w", "ffn_w_down")
GROUP_ROWS = ("w_out", "ffn_w_down")
GROUP_LANE = ("w_br_hgrn", "w_br_fox", "w_br_mem")
LANE_GROUP_ROWS = 224
SMALL = ("norm_mix_g", "norm_mem_g", "hgrn_lb_logits", "hgrn_norm_g", "fox_f_bias", "fox_q_norm_g", "fox_k_norm_g",
         "mem_q_norm_g", "mem_k_norm_g", "norm_ffn_g", "ffn_conv_b")


def _rows_of(n_elems):
    return -(-n_elems // LANE)


def _to_rows(a, lead=0):
    flat = a.reshape(a.shape[:lead] + (-1,))
    pad = (-flat.shape[-1]) % LANE
    if pad:
        flat = jnp.pad(flat, [(0, 0)] * lead + [(0, pad)])
    return flat.reshape(a.shape[:lead] + (-1, LANE))


def _stack_rows(parts, lead, total_rows):
    buf = jnp.concatenate(parts, axis=lead)
    pad = total_rows - buf.shape[lead]
    return jnp.pad(buf, [(0, 0)] * lead + [(0, pad), (0, 0)])


def _round_up(n, k):
    return -(-n // k) * k


def _from_rows(rows, shape, lead=0):
    n = math.prod(shape)
    return rows.reshape(rows.shape[:lead] + (-1,))[..., :n].reshape(rows.shape[:lead] + tuple(shape))


def _blocks_to_full(blocks, kind):
    n, a, b = blocks.shape
    return blocks.transpose(1, 0, 2).reshape(a, n * b) if kind == "col" else blocks.reshape(n * a, b)


def _full_to_blocks(full, kind, n=N_DEV):
    a, b = full.shape
    return full.reshape(a, n, b // n).transpose(1, 0, 2) if kind == "col" else full.reshape(n, a // n, b)


def _lane_group_rows(shard):
    n_lane = sum(shard[n].shape[0] for n in GROUP_LANE)
    n_cw = shard["ffn_conv_w"].size
    return n_lane, _rows_of(3 * n_cw), _rows_of(n_cw), _round_up(n_lane + _rows_of(3 * n_cw), LANE_GROUP_ROWS)


def _split_bf16x3(x):
    hi = x.astype(BF16)
    r1 = x - hi.astype(F32)
    mid = r1.astype(BF16)
    return jnp.stack([hi, mid, (r1 - mid.astype(F32)).astype(BF16)])


class _Exchange:
    def __init__(self, shard):
        self.shard = shard
        xi, yi, ci = _position()
        self.core = ci.astype(jnp.int32).reshape(1)
        self.chip = (2 * xi + yi).astype(jnp.int32).reshape(1)
        self.n_lane, self.r_pieces, self.r_vals, self.r_lane = _lane_group_rows(shard)

    def first_blocks(self):
        return [self.shard["w_in"].astype(BF16), self.shard["mem_kv_w"].astype(BF16)]

    def unpack_first(self, gathered):
        return {"w_in": _perm_from_blocks(gathered[0]), "mem_kv_w": _blocks_to_full(gathered[1], "row")}

    def late_blocks(self):
        sh = self.shard
        lane_rows = [sh[n].astype(BF16) for n in GROUP_LANE] + [_to_rows(_split_bf16x3(sh["ffn_conv_w"]))]
        return [sh[n].astype(BF16) for n in GROUP_ROWS] + [sh["ffn_w_up"].astype(BF16),
                                                           _stack_rows(lane_rows, 0, self.r_lane)]

    def unpack_late(self, gathered):
        *rows, gc, gd = gathered
        sh = self.shard
        W = {"ffn_w_up": _blocks_to_full(gc, "col")}
        for n, blocks in zip(GROUP_ROWS, rows):
            W[n] = _blocks_to_full(blocks, "row")
        r0 = 0
        for n in GROUP_LANE:
            W[n] = _blocks_to_full(gd[:, r0:r0 + sh[n].shape[0]], "col")
            r0 += sh[n].shape[0]
        cw = _from_rows(gd[:, self.n_lane:self.n_lane + self.r_pieces], (3,) + sh["ffn_conv_w"].shape, lead=1).astype(F32)
        W["ffn_conv_w"] = _blocks_to_full(cw[:, 0] + cw[:, 1] + cw[:, 2], "col")
        return W

    def early_grads(self, g):
        cw_rows = _to_rows(_full_to_blocks(g["ffn_conv_w"], "col"), lead=1)
        return [_full_to_blocks(g[n], "row") for n in GROUP_ROWS] + [
            jnp.concatenate([_full_to_blocks(h, "col", N_DEV // 2) for h in g["ffn_w_up"]], axis=0),
            _stack_rows([_full_to_blocks(g[n], "col") for n in GROUP_LANE] + [cw_rows], 1, self.r_lane)]

    def last_grads(self, g):
        return [_unperm_blocks(g["w_in"], N_DEV), _full_to_blocks(g["mem_kv_w"], "row")]

    def pair_sums(self, pks, recv_sib, tag):
        return [_pair_sum_cast(p, r, self.core, "rs_pair_sum_%s%d" % (tag, i))
                for i, (p, r) in enumerate(zip(pks, recv_sib))]

    def final_sums(self, pks, recv_sib, recv_chips, tag):
        return [_final_sum(p, rs, rc, 2 * self.chip + self.core, self.chip, "rs_final_sum_%s%d" % (tag, i))
                for i, (p, rs, rc) in enumerate(zip(pks, recv_sib, recv_chips))]

    def unpack_grads(self, early, last):
        sh = self.shard
        *rows, g_up, g_lane = early
        g_shard = {"w_in": last[0], "mem_kv_w": last[1], "ffn_w_up": g_up, **dict(zip(GROUP_ROWS, rows))}
        r0 = 0
        for n in GROUP_LANE:
            g_shard[n] = g_lane[r0:r0 + sh[n].shape[0]]
            r0 += sh[n].shape[0]
        g_shard["ffn_conv_w"] = _from_rows(g_lane[self.n_lane:self.n_lane + self.r_vals], sh["ffn_conv_w"].shape)
        return g_shard


def kernel(x, mem, norm_mix_g, norm_mem_g, w_in, hgrn_lb_logits, hgrn_norm_g, fox_f_bias, fox_q_norm_g, fox_k_norm_g, mem_kv_w, mem_q_norm_g, mem_k_norm_g, w_br_hgrn, w_br_fox, w_br_mem, w_out, norm_ffn_g, ffn_w_up, ffn_conv_w, ffn_conv_b, ffn_w_down, loss_target, m_norm_mix_g, m_norm_mem_g, m_w_in, m_hgrn_lb_logits, m_hgrn_norm_g, m_fox_f_bias, m_fox_q_norm_g, m_fox_k_norm_g, m_mem_kv_w, m_mem_q_norm_g, m_mem_k_norm_g, m_w_br_hgrn, m_w_br_fox, m_w_br_mem, m_w_out, m_norm_ffn_g, m_ffn_w_up, m_ffn_conv_w, m_ffn_conv_b, m_ffn_w_down, v_norm_mix_g, v_norm_mem_g, v_w_in, v_hgrn_lb_logits, v_hgrn_norm_g, v_fox_f_bias, v_fox_q_norm_g, v_fox_k_norm_g, v_mem_kv_w, v_mem_q_norm_g, v_mem_k_norm_g, v_w_br_hgrn, v_w_br_fox, v_w_br_mem, v_w_out, v_norm_ffn_g, v_ffn_w_up, v_ffn_conv_w, v_ffn_conv_b, v_ffn_w_down):
    given = dict(locals())
    order = ("norm_mix_g", "norm_mem_g", "w_in", "hgrn_lb_logits", "hgrn_norm_g", "fox_f_bias", "fox_q_norm_g",
             "fox_k_norm_g", "mem_kv_w", "mem_q_norm_g", "mem_k_norm_g", "w_br_hgrn", "w_br_fox", "w_br_mem", "w_out",
             "norm_ffn_g", "ffn_w_up", "ffn_conv_w", "ffn_conv_b", "ffn_w_down")
    B, T, D = x.shape
    M = mem.shape[1]
    shard = {n: given[n][0] if n in BIG else given[n] for n in order}
    mom = {n: (given["m_" + n][0], given["v_" + n][0]) if n in BIG else (given["m_" + n], given["v_" + n])
           for n in order}
    shard["hgrn_lb_logits"] = hgrn_lb_logits
    for n in ("norm_mix_g", "norm_mem_g", "hgrn_norm_g", "fox_f_bias", "fox_q_norm_g", "fox_k_norm_g", "mem_q_norm_g",
              "mem_k_norm_g", "norm_ffn_g", "ffn_conv_b"):
        shard[n] = given[n].reshape(1, -1)

    ex = _Exchange(shard)
    sm = {n: shard[n] for n in SMALL}
    grad_x, g, sums = _local_step(x.reshape(B * T, D), mem.reshape(B * M, D), loss_target.reshape(B * T, D), sm, None,
                                  B, T, M, ex)
    g_shard = ex.unpack_grads(*sums)

    sg = {n: g[n] for n in SMALL}
    sg["fox_f_bias"] = g["fox_f_bias"][:, :FOX_H]
    sg["fox_q_norm_g"] = g["fox_q_norm_g"][:, :FOX_D]
    sg["fox_k_norm_g"] = g["fox_k_norm_g"][:, :FOX_D]
    slayout, row0 = {}, 0
    for n in SMALL:
        nr = _rows_of(shard[n].size)
        slayout[n] = (row0, nr)
        row0 += nr
    loss_row = row0
    r_small = _round_up(row0 + 1, 8)

    def pack_small(d, with_loss=None):
        rows = [_to_rows(d[n]) for n in SMALL]
        rows.append(with_loss if with_loss is not None else jnp.zeros((1, LANE), F32))
        return _stack_rows(rows, 0, r_small)

    sgath, = _all_gather([pack_small(sg, g["loss"])], "ag_small")
    s_g, s_d, s_m, s_v = _small_update(sgath, pack_small(shard), pack_small({n: mom[n][0].reshape(shard[n].shape) for n in SMALL}),
                                       pack_small({n: mom[n][1].reshape(shard[n].shape) for n in SMALL}))
    loss = s_g[loss_row, 0]

    grads, deltas, new_m, new_v = {}, {}, {}, {}
    for n in BIG:
        gn = g_shard[n]
        d, nm, nv = _adamw(shard[n], gn, mom[n][0], mom[n][1], "adamw_" + n)
        grads[n], deltas[n], new_m[n], new_v[n] = (a[None] for a in (gn, d, nm, nv))
    for n in SMALL:
        r0, nr = slayout[n]
        for dst, src in ((grads, s_g), (deltas, s_d), (new_m, s_m), (new_v, s_v)):
            dst[n] = _from_rows(src[r0:r0 + nr], given[n].shape)
    return (loss, grad_x.reshape(B, T, D), *[grads[n] for n in order], *[deltas[n] for n in order],
            *[new_m[n] for n in order], *[new_v[n] for n in order])
```

```python
import functools
import math

import jax
import jax.numpy as jnp
from jax import lax
from jax.experimental import pallas as pl
from jax.experimental.pallas import tpu as pltpu

F32, BF16 = jnp.float32, jnp.bfloat16
S = jax.ShapeDtypeStruct
MESH = pl.DeviceIdType.MESH

N_DEV = 8
EPS = 1e-6
LANE = 128
CHUNK = 64
SUB = 16
HG_H, HG_D = 4, 128
HG_GROUP_FWD = 4
HG_GROUP = 2
FOX_H, FOX_D = 8, 64
FOX_P = FOX_H // 2
MEM_H, MEM_D = 4, 128
NEG = -1e30
VMEM_LIMIT = 56 * 2**20

ADAM_LR, ADAM_B1, ADAM_B2, ADAM_EPS, ADAM_WD, ADAM_STEP = 0.001, 0.9, 0.999, 1e-08, 0.01, 10

C_FOX, C_MQ, C_HG, C_GATE, C_FF, C_END = 0, 1536, 2048, 4096, 7168, 7296


def _cp(sem=None):
    return pltpu.CompilerParams(dimension_semantics=sem, vmem_limit_bytes=VMEM_LIMIT)


def _dot(a, b, dims, prec=None):
    return lax.dot_general(a, b, (dims, ((), ())), preferred_element_type=F32, precision=prec)


def _nn(a, b, prec=None):
    return _dot(a, b, ((1,), (0,)), prec)


def _nt(a, b, prec=None):
    return _dot(a, b, ((1,), (1,)), prec)


def _tn(a, b, prec=None):
    return _dot(a, b, ((0,), (0,)), prec)


def _b(x):
    return x.astype(BF16)


def _mm3(fn, a, b):
    ah, bh = _b(a), _b(b)
    return fn(ah, bh) + fn(ah, _b(b - bh.astype(F32))) + fn(_b(a - ah.astype(F32)), bh)


def _iota(shape, dim):
    return lax.broadcasted_iota(jnp.int32, shape, dim)


def _rowsum8(x):
    r, d = x.shape
    return jnp.sum(x.reshape(r // 8, 8, d), axis=0)


def _rmsnorm_cast(x, g, name, tm=1024, gather=()):
    n, d = x.shape
    nga = len(gather)

    def body(x_ref, g_ref, o_ref):
        v = x_ref[...]
        r = lax.rsqrt(jnp.mean(v * v, axis=-1, keepdims=True) + EPS)
        o_ref[...] = (v * r * g_ref[...]).astype(BF16)

    if nga:
        body = _hosting(body, 2, 1, 0, nga, _gather_phases, (n // tm,))
    out = pl.pallas_call(
        body, name=name, grid=(n // tm,),
        in_specs=[pl.BlockSpec((tm, d), lambda i: (i, 0)), pl.BlockSpec((1, d), lambda i: (0, 0))] + [ANY] * nga,
        out_specs=[pl.BlockSpec((tm, d), lambda i: (i, 0))] + [ANY] * nga,
        out_shape=[S((n, d), BF16)] + _gather_shapes(gather), scratch_shapes=_gather_sems(nga) if nga else [],
        compiler_params=_cp(("arbitrary",) if nga else ("parallel",)),
    )(x, g, *gather)
    return out if nga else out[0]


def _rmsnorm_bwd(dh, x, g, resid, name, tm=1024):
    n, d = x.shape
    has_res = resid is not None

    def body(*refs):
        if has_res:
            dh_ref, x_ref, g_ref, r_ref, dx_ref, dg_ref = refs
        else:
            dh_ref, x_ref, g_ref, dx_ref, dg_ref = refs
        v = x_ref[...]
        dhv = dh_ref[...].astype(F32)
        r = lax.rsqrt(jnp.mean(v * v, axis=-1, keepdims=True) + EPS)
        xh = v * r
        u = dhv * g_ref[...]
        dx = r * (u - xh * jnp.mean(u * xh, axis=-1, keepdims=True))
        if has_res:
            dx = dx + r_ref[...]
        dx_ref[...] = dx

        @pl.when(pl.program_id(0) == 0)
        def _():
            dg_ref[...] = jnp.zeros_like(dg_ref)

        dg_ref[...] += _rowsum8(dhv * xh)

    tile = pl.BlockSpec((tm, d), lambda i: (i, 0))
    ins = [tile, tile, pl.BlockSpec((1, d), lambda i: (0, 0))] + ([tile] if has_res else [])
    args = (dh, x, g) + ((resid,) if has_res else ())
    return pl.pallas_call(
        body, name=name, grid=(n // tm,), in_specs=ins,
        out_specs=[tile, pl.BlockSpec((8, d), lambda i: (0, 0))],
        out_shape=[S((n, d), F32), S((8, d), F32)], compiler_params=_cp(("arbitrary",)),
    )(*args)


def _mm_nn(a, b, out_dtype, name, tm, tn):
    m, k = a.shape
    n = b.shape[1]
    assert n % tn == 0 and m % tm == 0

    def body(a_ref, b_ref, o_ref):
        o_ref[...] = _nn(a_ref[...].astype(BF16), b_ref[...].astype(BF16)).astype(out_dtype)

    return pl.pallas_call(
        body, name=name, grid=(n // tn, m // tm),
        in_specs=[pl.BlockSpec((tm, k), lambda j, i: (i, 0)), pl.BlockSpec((k, tn), lambda j, i: (0, j))],
        out_specs=pl.BlockSpec((tm, tn), lambda j, i: (i, j)), out_shape=S((m, n), out_dtype),
        compiler_params=_cp(("parallel", "parallel")),
    )(a, b)


def _mm_nt_sum(parts, w, name, tm, swap=()):
    m = parts[0][0].shape[0]
    k = w.shape[0]
    assert m % tm == 0 and all(c % n == 0 and o % n == 0 for _, c, n, o in parts)
    np_ = len(parts)
    nsw = len(swap)
    n_steps = m // tm

    def body(*refs):
        o_ref = refs[2 * np_ + nsw]
        if nsw:
            start, finish = _chip_swap_phases(refs[2 * np_:2 * np_ + nsw], refs[2 * np_ + nsw + 1:2 * np_ + 2 * nsw + 1],
                                              *refs[2 * np_ + 2 * nsw + 1:])
            pl.when(pl.program_id(0) == 0)(start)
        acc = _nt(refs[0][...].astype(BF16), refs[np_][...].astype(BF16))
        for i in range(1, np_):
            acc = acc + _nt(refs[i][...].astype(BF16), refs[np_ + i][...].astype(BF16))
        o_ref[...] = acc
        if nsw:
            pl.when(pl.program_id(0) == n_steps - 1)(finish)

    dy_specs = [pl.BlockSpec((tm, n), functools.partial(lambda i, j: (i, j), j=c // n)) for _, c, n, _ in parts]
    w_specs = [pl.BlockSpec((k, n), functools.partial(lambda i, j: (0, j), j=o // n)) for _, _, n, o in parts]
    out = pl.pallas_call(
        body, name=name, grid=(n_steps,), in_specs=dy_specs + w_specs + [ANY] * nsw,
        out_specs=[pl.BlockSpec((tm, k), lambda i: (i, 0))] + [ANY] * nsw,
        out_shape=[S((m, k), F32)] + [S(p.shape, p.dtype) for p in swap],
        scratch_shapes=_chip_swap_sems(nsw) if nsw else [],
        compiler_params=_cp(("arbitrary",) if nsw else ("parallel",)),
    )(*([p[0] for p in parts] + [w] * np_ + list(swap)))
    return (out[0], out[1:]) if nsw else out[0]


MM_TN_PIECE = 1536


def _mm_tn(x, dy, name, tm, tn):
    m, k = x.shape
    n = dy.shape[1]
    tm = min(tm, m)
    assert m % tm == 0 and n % tn == 0

    cuts = list(range(0, tn, MM_TN_PIECE)) + [tn]

    def body(x_ref, dy_ref, o_ref):
        xb = x_ref[...].astype(BF16)
        for c0, c1 in zip(cuts[:-1], cuts[1:]):
            part = _tn(xb, dy_ref[:, c0:c1].astype(BF16))

            @pl.when(pl.program_id(1) == 0)
            def _():
                o_ref[:, c0:c1] = part

            @pl.when(pl.program_id(1) > 0)
            def _():
                o_ref[:, c0:c1] += part

    out_mode = dict(pipeline_mode=pl.Buffered(1)) if tn == n else {}
    return pl.pallas_call(
        body, name=name, grid=(n // tn, m // tm),
        in_specs=[pl.BlockSpec((tm, k), lambda j, i: (i, 0)), pl.BlockSpec((tm, tn), lambda j, i: (i, j))],
        out_specs=pl.BlockSpec((k, tn), lambda j, i: (0, j), **out_mode), out_shape=S((k, n), F32),
        compiler_params=_cp(("parallel", "arbitrary")),
    )(x, dy)


def _lower_bound(logits):
    e = jnp.exp(logits - jnp.max(logits, axis=0, keepdims=True))
    return e[0:1, :] / jnp.sum(e, axis=0, keepdims=True)


def _hg_gates(fl, lb):
    sig = jax.nn.sigmoid(fl)
    f = lb + (1.0 - lb) * sig
    k = (1.0 - lb) * (1.0 - sig)
    return sig, f, k, jnp.log(f)


def _silu_and_grad(x):
    s = jax.nn.sigmoid(x)
    return x * s, s * (1.0 + x * (1.0 - s))


def _hg_rowblocks(G):
    return [None] + [G[SUB * i - 1:SUB * i, :] for i in range(1, CHUNK // SUB)]


def _hg_intra_A(qs, k, G):
    refs = _hg_rowblocks(G)
    cols = _iota((SUB, LANE), 1)
    rows = _iota((SUB, LANE), 0)
    no_keys = jnp.zeros((LANE - CHUNK, HG_D), BF16)
    blocks = []
    for i in range(CHUNK // SUB):
        lo = SUB * i
        qb, Gb = qs[lo:lo + SUB, :], G[lo:lo + SUB, :]
        diag = jnp.zeros((SUB, LANE), F32)
        for s in range(SUB):
            e = jnp.exp(jnp.minimum(Gb - G[lo + s:lo + s + 1, :], 0.0))
            col = jnp.sum(qb * k[lo + s:lo + s + 1, :] * e, axis=-1, keepdims=True)
            diag = jnp.where(cols == lo + s, col, diag)
        a = jnp.where((cols >= lo) & (cols <= rows + lo), diag, 0.0)
        if i > 0:
            qr = qb * jnp.exp(Gb - refs[i])
            kr = k * jnp.exp(jnp.minimum(refs[i] - G, 0.0))
            a = jnp.where(cols < lo, _nt(_b(qr), jnp.concatenate([_b(kr), no_keys], axis=0)), a)
        blocks.append(a)
    return jnp.concatenate(blocks, axis=0)


def _hg_intra_bwd(dA, qs, k, G):
    refs = _hg_rowblocks(G)
    cols = _iota((SUB, CHUNK), 1)
    rows16 = _iota((SUB, HG_D), 0)
    dk = jnp.zeros((CHUNK, HG_D), F32)
    dq_blocks, dk_diag_blocks = [], []
    for i in range(CHUNK // SUB):
        lo = SUB * i
        qb, Gb = qs[lo:lo + SUB, :], G[lo:lo + SUB, :]
        dAb = dA[lo:lo + SUB, :]
        dq = jnp.zeros((SUB, HG_D), F32)
        dkb = jnp.zeros((SUB, HG_D), F32)
        for s in range(SUB):
            e = jnp.exp(jnp.minimum(Gb - G[lo + s:lo + s + 1, :], 0.0))
            e = jnp.where(rows16 >= s, e, 0.0)
            dcol = jnp.sum(jnp.where(cols == lo + s, dAb, 0.0), axis=-1, keepdims=True)
            w = dcol * e
            dq = dq + w * k[lo + s:lo + s + 1, :]
            dkb = jnp.where(rows16 == s, jnp.sum(w * qb, axis=0, keepdims=True), dkb)
        if i > 0:
            e1 = jnp.exp(Gb - refs[i])
            e2 = jnp.exp(jnp.minimum(refs[i] - G, 0.0))
            dA_off = jnp.where(cols < lo, dAb, 0.0)
            dq = dq + _mm3(_nn, dA_off, k * e2) * e1
            dk = dk + _mm3(_tn, dA_off, qb * e1) * e2
        dq_blocks.append(dq)
        dk_diag_blocks.append(dkb)
    return jnp.concatenate(dq_blocks, axis=0), dk + jnp.concatenate(dk_diag_blocks, axis=0)


def _tri(n, upper=False):
    r, c = _iota((n, n), 0), _iota((n, n), 1)
    return jnp.where((c >= r) if upper else (r >= c), 1.0, 0.0).astype(BF16)


def _prefix_mm(tri, x):
    hi = x.astype(BF16)
    r1 = x - hi.astype(F32)
    mid = r1.astype(BF16)
    lo = (r1 - mid.astype(F32)).astype(BF16)
    return _nn(tri, hi) + _nn(tri, mid) + _nn(tri, lo)


def _hgrn_fwd(z, lb, gn, B, T):
    N = B * T
    NC = T // CHUNK
    ng = HG_H // HG_GROUP_FWD

    def body(z_ref, lb_ref, gn_ref, y_ref, o_ref, st_ref, a_ref, s_scr):
        lbs = _lower_bound(lb_ref[...])
        tri = _tri(CHUNK)
        s_scr[...] = jnp.zeros_like(s_scr)

        def chunk(c, carry):
            r = pl.ds(pl.multiple_of(c * CHUNK, CHUNK), CHUNK)
            for hh in range(HG_GROUP_FWD):
                zc, oc = 4 * LANE * hh, LANE * hh
                ql, fl, il, gl = (z_ref[r, zc + LANE * j:zc + LANE * (j + 1)].astype(F32) for j in range(4))
                _, _, k, logf = _hg_gates(fl, lbs[:, oc:oc + LANE])
                G = _prefix_mm(tri, logf)
                qs = ql * jax.nn.sigmoid(ql)
                st = s_scr[hh]
                st_ref[hh * NC + c] = st
                g_last = G[CHUNK - 1:CHUNK, :]
                A = _b(_hg_intra_A(qs, k, G))
                a_ref[r, oc:oc + LANE] = A
                o = _nn(A[:, 0:CHUNK], _b(il)) + _nt(_b(qs * jnp.exp(G)), _b(st))
                s_scr[hh] = st * jnp.exp(g_last) + _mm3(_tn, il, k * jnp.exp(g_last - G))
                o_ref[r, oc:oc + LANE] = o
                rstd = lax.rsqrt(jnp.mean(o * o, axis=-1, keepdims=True) + EPS)
                y_ref[r, oc:oc + LANE] = (o * rstd * gn_ref[...] * (gl * jax.nn.sigmoid(gl))).astype(BF16)
            return carry

        lax.fori_loop(0, NC, chunk, 0, unroll=4)

    gw = HG_GROUP_FWD * LANE
    cb = C_HG // (4 * gw)
    return pl.pallas_call(
        body, name="hgrn_fwd", grid=(B, ng),
        in_specs=[pl.BlockSpec((T, 4 * gw), lambda b, h: (b, cb + h)), pl.BlockSpec((lb.shape[0], gw), lambda b, h: (0, h)),
                  pl.BlockSpec((1, LANE), lambda b, h: (0, 0))],
        out_specs=[pl.BlockSpec((T, gw), lambda b, h: (b, h)), pl.BlockSpec((T, gw), lambda b, h: (b, h)),
                   pl.BlockSpec((HG_GROUP_FWD * NC, HG_D, HG_D), lambda b, h: (b * ng + h, 0, 0)),
                   pl.BlockSpec((T, gw), lambda b, h: (b, h))],
        out_shape=[S((N, 512), BF16), S((N, 512), F32), S((B * HG_H * NC, HG_D, HG_D), F32), S((N, 512), BF16)],
        scratch_shapes=[pltpu.VMEM((HG_GROUP_FWD, HG_D, HG_D), F32)], compiler_params=_cp(("parallel", "parallel")),
    )(z, lb, gn)


def _hgrn_bwd(z, o_raw, states, a_mat, dy, lb, gn, B, T, swap_sibling=()):
    N = B * T
    NC = T // CHUNK
    ng = HG_H // HG_GROUP
    nsw = len(swap_sibling)

    def body(z_ref, o_ref, st_ref, a_ref, dy_ref, lb_ref, gn_ref, dz_ref, dlb_ref, dgn_ref, ds_scr, racc, dgn_acc):
        lbs = _lower_bound(lb_ref[...])
        gn_v = gn_ref[...]
        tri, triu = _tri(CHUNK), _tri(CHUNK, upper=True)
        cmask = _iota((CHUNK, CHUNK), 0) >= _iota((CHUNK, CHUNK), 1)
        for ref in (ds_scr, racc, dgn_acc, dlb_ref):
            ref[...] = jnp.zeros_like(ref)

        def chunk(ci, carry):
            c = NC - 1 - ci
            r = pl.ds(pl.multiple_of(c * CHUNK, CHUNK), CHUNK)
            for hh in range(HG_GROUP):
                zc, oc = 4 * LANE * hh, LANE * hh
                lb_v = lbs[:, oc:oc + LANE]
                ql, fl, il, gl = (z_ref[r, zc + LANE * j:zc + LANE * (j + 1)].astype(F32) for j in range(4))
                sig, f, k, logf = _hg_gates(fl, lb_v)
                G = _prefix_mm(tri, logf)
                qs, dsilu_q = _silu_and_grad(ql)
                gs, dsilu_g = _silu_and_grad(gl)
                o = o_ref[r, oc:oc + LANE]
                dyv = dy_ref[r, oc:oc + LANE]
                rstd = lax.rsqrt(jnp.mean(o * o, axis=-1, keepdims=True) + EPS)
                oh = o * rstd
                dgl = dyv * oh * gn_v * dsilu_g
                dn = dyv * gs
                dgn_acc[...] += _rowsum8(dn * oh)
                u = dn * gn_v
                do = rstd * (u - oh * jnp.mean(u * oh, axis=-1, keepdims=True))
                st = st_ref[hh * NC + c]
                dst = ds_scr[hh]
                eG = jnp.exp(G)
                g_last = G[CHUNK - 1:CHUNK, :]
                eL = jnp.exp(g_last - G)
                dA = jnp.where(cmask, _mm3(_nt, do, il), 0.0)
                dq_in, dk_in = _hg_intra_bwd(dA, qs, k, G)
                di = _tn(a_ref[r, oc:oc + LANE][:, 0:CHUNK], _b(do)) + _nt(_b(k * eL), _b(dst))
                dq = dq_in + _mm3(_nn, do, st) * eG
                dk = dk_in + _mm3(_nn, il, dst) * eL
                ds_scr[hh] = dst * jnp.exp(g_last) + _mm3(_tn, do, qs * eG)
                dd = qs * dq - k * dk
                dlogf = _prefix_mm(triu, dd) + racc[hh]
                racc[hh] += jnp.sum(dd, axis=0, keepdims=True)
                df = dlogf / f - dk
                dlb_ref[8 * hh:8 * (hh + 1), :] += _rowsum8(df * (1.0 - sig))
                dz_ref[r, zc:zc + LANE] = (dq * dsilu_q).astype(BF16)
                dz_ref[r, zc + LANE:zc + 2 * LANE] = (df * (1.0 - lb_v) * sig * (1.0 - sig)).astype(BF16)
                dz_ref[r, zc + 2 * LANE:zc + 3 * LANE] = di.astype(BF16)
                dz_ref[r, zc + 3 * LANE:zc + 4 * LANE] = dgl.astype(BF16)
            return carry

        lax.fori_loop(0, NC, chunk, 0, unroll=4)
        dgn_ref[...] = dgn_acc[...]

    gw = HG_GROUP * LANE
    cb = C_HG // (4 * gw)
    col = pl.BlockSpec((T, gw), lambda b, h: (b, h))
    if nsw:
        body = _hosting(body, 7, 3, 3, nsw, _sibling_swap_phases, (B, ng))
    return pl.pallas_call(
        body, name="hgrn_bwd", grid=(B, ng),
        in_specs=[pl.BlockSpec((T, 4 * gw), lambda b, h: (b, cb + h)), col,
                  pl.BlockSpec((HG_GROUP * NC, HG_D, HG_D), lambda b, h: (b * ng + h, 0, 0)), col, col,
                  pl.BlockSpec((lb.shape[0], gw), lambda b, h: (0, h)), pl.BlockSpec((1, LANE), lambda b, h: (0, 0))]
        + [ANY] * nsw,
        out_specs=[pl.BlockSpec((T, 4 * gw), lambda b, h: (b, h)),
                   pl.BlockSpec((8 * HG_GROUP, LANE), lambda b, h: (b * ng + h, 0)),
                   pl.BlockSpec((8, LANE), lambda b, h: (b * ng + h, 0))] + [ANY] * nsw,
        out_shape=[S((N, 2048), BF16), S((B * HG_H * 8, LANE), F32), S((B * ng * 8, LANE), F32)]
        + _sibling_swap_shapes(swap_sibling),
        scratch_shapes=[pltpu.VMEM((HG_GROUP, HG_D, HG_D), F32), pltpu.VMEM((HG_GROUP, 1, LANE), F32),
                        pltpu.VMEM((8, LANE), F32)] + (_sibling_swap_sems(nsw) if nsw else []),
        compiler_params=_cp(("arbitrary", "arbitrary") if nsw else ("parallel", "parallel")),
    )(z, o_raw, states, a_mat, dy, lb, gn, *swap_sibling)


def _pair_mean(x, lo_half):
    a = jnp.sum(jnp.where(lo_half, x, 0.0), axis=-1, keepdims=True)
    b = jnp.sum(jnp.where(lo_half, 0.0, x), axis=-1, keepdims=True)
    return jnp.where(lo_half, a, b) * (1.0 / FOX_D)


def _fox_gate_fwd(z, bias, B, T):
    N = B * T
    tb = LANE

    def body(z_ref, b_ref, fc_ref, fct_ref):
        tri = _tri(tb)

        def step(i, carry):
            r = pl.ds(pl.multiple_of(i * tb, tb), tb)
            cs = _prefix_mm(tri, jax.nn.log_sigmoid(z_ref[r, :].astype(F32) + b_ref[...])) + carry
            fc_ref[r, :] = cs
            fct_ref[0, :, r] = cs.T[0:8, :]
            return cs[tb - 1:tb, :]

        lax.fori_loop(0, T // tb, step, jnp.zeros((1, LANE), F32))

    return pl.pallas_call(
        body, name="fox_gate_fwd", grid=(B,),
        in_specs=[pl.BlockSpec((T, LANE), lambda b: (b, C_FF // LANE)), pl.BlockSpec((1, LANE), lambda b: (0, 0))],
        out_specs=[pl.BlockSpec((T, LANE), lambda b: (b, 0)), pl.BlockSpec((1, 8, T), lambda b: (b, 0, 0))],
        out_shape=[S((N, LANE), F32), S((B, 8, T), F32)], compiler_params=_cp(("parallel",)),
    )(z, bias)


def _fox_gate_bwd(dfc, z, bias, B, T):
    N = B * T
    tb = LANE
    nt = T // tb

    def body(d_ref, z_ref, b_ref, dz_ref, db_ref):
        triu = _tri(tb, upper=True)
        db_ref[...] = jnp.zeros_like(db_ref)

        def step(ii, carry):
            r = pl.ds(pl.multiple_of((nt - 1 - ii) * tb, tb), tb)
            d = d_ref[r, 0:LANE]
            for p in range(1, FOX_P):
                d = d + d_ref[r, LANE * p:LANE * (p + 1)]
            rc = _prefix_mm(triu, d) + carry
            dff = rc * jax.nn.sigmoid(-(z_ref[r, :].astype(F32) + b_ref[...]))
            dz_ref[r, :] = dff.astype(BF16)
            db_ref[...] += _rowsum8(dff)
            return carry + jnp.sum(d, axis=0, keepdims=True)

        lax.fori_loop(0, nt, step, jnp.zeros((1, LANE), F32))

    return pl.pallas_call(
        body, name="fox_gate_bwd", grid=(B,),
        in_specs=[pl.BlockSpec((T, 512), lambda b: (b, 0)), pl.BlockSpec((T, LANE), lambda b: (b, C_FF // LANE)),
                  pl.BlockSpec((1, LANE), lambda b: (0, 0))],
        out_specs=[pl.BlockSpec((T, LANE), lambda b: (b, 0)), pl.BlockSpec((8, LANE), lambda b: (b, 0))],
        out_shape=[S((N, LANE), BF16), S((B * 8, LANE), F32)], compiler_params=_cp(("parallel",)),
    )(dfc, z, bias)


def _fox_prep(z_ref, gq, gk, r, lo_half):
    q, k, v = (z_ref[r, LANE * j:LANE * (j + 1)].astype(F32) for j in range(3))
    rq = lax.rsqrt(_pair_mean(q * q, lo_half) + EPS)
    rk = lax.rsqrt(_pair_mean(k * k, lo_half) + EPS)
    qh, kh = q * rq, k * rk
    return qh * gq * (FOX_D ** -0.5), kh * gk, v, qh, kh, rq, rk


def _fox_fwd(z, fc, fct, gq, gk, B, T, tq=512, gather=()):
    N = B * T
    NQ = T // tq
    nga = len(gather)

    def body(z_ref, fc_ref, fct_ref, gq_ref, gk_ref, y_ref, lse_ref, qn_s, kn_s, v_s):
        p, qi = pl.program_id(1), pl.program_id(2)
        lo_half = _iota((1, LANE), 1) < FOX_D

        @pl.when(qi == 0)
        def _():
            def prep(i, carry):
                r = pl.ds(pl.multiple_of(i * tq, tq), tq)
                qn, kn, v = _fox_prep(z_ref, gq_ref[...], gk_ref[...], r, lo_half)[:3]
                qn_s[r, :], kn_s[r, :], v_s[r, :] = qn.astype(BF16), kn.astype(BF16), v.astype(BF16)
                return carry
            lax.fori_loop(0, NQ, prep, 0)

        rq = pl.ds(pl.multiple_of(qi * tq, tq), tq)
        qn = qn_s[rq, :]
        fcq = fc_ref[rq, :]
        lane = _iota((tq, LANE), 1)
        causal = _iota((tq, tq), 0) >= _iota((tq, tq), 1)
        qhs = [jnp.where(lo_half, qn, jnp.zeros_like(qn)), jnp.where(lo_half, jnp.zeros_like(qn), qn)]
        fqs = [jnp.sum(jnp.where(lane == 2 * p + hh, fcq, 0.0), axis=-1, keepdims=True) for hh in range(2)]

        def kv(j, carry, diagonal):
            rk = pl.ds(pl.multiple_of(j * tq, tq), tq)
            kj, vj = kn_s[rk, :], v_s[rk, :]
            one = jnp.ones_like(vj)
            new = []
            for hh in range(2):
                m, acc = carry[hh]
                s = _nt(qhs[hh], kj) + fqs[hh] - fct_ref[0, pl.ds(2 * p + hh, 1), rk]
                if diagonal:
                    s = jnp.where(causal, s, NEG)
                m_new = jnp.maximum(m, jnp.max(s, axis=-1, keepdims=True))
                pe = jnp.exp(s - m_new)
                v_aug = jnp.where(lo_half if hh == 0 else jnp.logical_not(lo_half), vj, one)
                new.append((m_new, jnp.exp(m - m_new) * acc + _nn(pe.astype(BF16), v_aug)))
            return tuple(new)

        init = tuple((jnp.full((tq, 1), NEG, F32), jnp.zeros((tq, LANE), F32)) for _ in range(2))
        carry = lax.fori_loop(0, qi, functools.partial(kv, diagonal=False), init)
        (m0, a0), (m1, a1) = kv(qi, carry, True)
        l0, l1 = a0[:, FOX_D:FOX_D + 1], a1[:, 0:1]
        y_ref[...] = jnp.where(lo_half, a0 / l0, a1 / l1).astype(BF16)
        lse_ref[...] = jnp.where(lo_half, m0 + jnp.log(l0), m1 + jnp.log(l1))

    vec = pl.BlockSpec((1, LANE), lambda b, p, q: (0, 0))
    tile = pl.BlockSpec((tq, LANE), lambda b, p, q: (b * NQ + q, p))
    if nga:
        body = _hosting(body, 5, 2, 3, nga, _gather_phases, (B, FOX_P, NQ))
    return pl.pallas_call(
        body, name="fox_fwd", grid=(B, FOX_P, NQ),
        in_specs=[pl.BlockSpec((T, 384), lambda b, p, q: (b, p)), pl.BlockSpec((T, LANE), lambda b, p, q: (b, 0)),
                  pl.BlockSpec((1, 8, T), lambda b, p, q: (b, 0, 0)), vec, vec] + [ANY] * nga,
        out_specs=[tile, tile] + [ANY] * nga, out_shape=[S((N, 512), BF16), S((N, 512), F32)] + _gather_shapes(gather),
        scratch_shapes=[pltpu.VMEM((T, LANE), BF16)] * 3 + (_gather_sems(nga) if nga else []),
        compiler_params=_cp(("arbitrary",) * 3 if nga else ("parallel", "parallel", "arbitrary")),
    )(z, fc, fct, gq, gk, *gather)


def _fox_bwd(z, dy, y, lse, fc, fct, gq, gk, B, T, tq=512, swap=()):
    N = B * T
    NQ = T // tq
    nsw = len(swap)

    def body(z_ref, dy_ref, y_ref, lse_ref, fc_ref, fct_ref, gq_ref, gk_ref, dz_ref, dfc_ref, dgq_ref, dgk_ref,
             qn_s, kn_s, v_s, do_s, delta_s, dq_s, dfk_s):
        p, kj = pl.program_id(1), pl.program_id(2)
        lo_half = _iota((1, LANE), 1) < FOX_D
        lane = _iota((tq, LANE), 1)
        gq_v, gk_v = gq_ref[...], gk_ref[...]

        @pl.when(kj == 0)
        def _():
            def prep(i, carry):
                r = pl.ds(pl.multiple_of(i * tq, tq), tq)
                qn, kn, v = _fox_prep(z_ref, gq_v, gk_v, r, lo_half)[:3]
                qn_s[r, :], kn_s[r, :], v_s[r, :] = qn.astype(BF16), kn.astype(BF16), v.astype(BF16)
                do = dy_ref[r, :]
                do_s[r, :] = do.astype(BF16)
                delta_s[r, :] = _pair_mean(do * y_ref[r, :].astype(F32), lo_half) * float(FOX_D)
                return carry
            lax.fori_loop(0, NQ, prep, 0)
            dq_s[...] = jnp.zeros_like(dq_s)
            dgq_ref[...] = jnp.zeros_like(dgq_ref)
            dgk_ref[...] = jnp.zeros_like(dgk_ref)

        rk = pl.ds(pl.multiple_of(kj * tq, tq), tq)
        kn, vv = kn_s[rk, :], v_s[rk, :]
        causal = _iota((tq, tq), 0) >= _iota((tq, tq), 1)
        zero, one = jnp.zeros_like(kn), jnp.ones_like(kn)
        hms = [lo_half, jnp.logical_not(lo_half)]
        kmasks = [jnp.where(hm, kn, zero) for hm in hms]
        kaugs = [jnp.where(hm, kn, one) for hm in hms]
        vmasks = [jnp.where(hm, vv, zero) for hm in hms]
        fks = [fct_ref[0, pl.ds(2 * p + hh, 1), rk] for hh in range(2)]

        def qloop(i, carry, diagonal):
            ri = pl.ds(pl.multiple_of(i * tq, tq), tq)
            qn = qn_s[ri, :]
            do = do_s[ri, :]
            fcq = fc_ref[ri, :]
            new = []
            for hh in range(2):
                dk_acc, dv_acc = carry[hh]
                c0 = FOX_D * hh
                fq = jnp.sum(jnp.where(lane == 2 * p + hh, fcq, 0.0), axis=-1, keepdims=True)
                pr = jnp.exp(_nt(qn, kmasks[hh]) + fq - fks[hh] - lse_ref[ri, c0:c0 + 1])
                if diagonal:
                    pr = jnp.where(causal, pr, 0.0)
                ds = (pr * (_nt(do, vmasks[hh]) - delta_s[ri, c0:c0 + 1])).astype(BF16)
                dq_s[hh, ri, :] += _nn(ds, kaugs[hh])
                new.append((dk_acc + _tn(jnp.where(hms[hh], qn, one), ds), dv_acc + _tn(do, pr.astype(BF16))))
            return tuple(new)

        init = tuple((jnp.zeros((LANE, tq), F32), jnp.zeros((LANE, tq), F32)) for _ in range(2))
        carry = qloop(kj, init, True)
        (dk0, dv0), (dk1, dv1) = lax.fori_loop(kj + 1, NQ, functools.partial(qloop, diagonal=False), carry)
        dks, dvs = [dk0.T, dk1.T], [dv0.T, dv1.T]

        dkn = jnp.where(lo_half, dks[0], dks[1])
        _, _, _, _, kh, _, rkk = _fox_prep(z_ref, gq_v, gk_v, rk, lo_half)
        u = dkn * gk_v
        dz_ref[rk, LANE:2 * LANE] = (rkk * (u - kh * _pair_mean(u * kh, lo_half))).astype(BF16)
        dz_ref[rk, 2 * LANE:3 * LANE] = jnp.where(lo_half, dvs[0], dvs[1]).astype(BF16)
        dgk_ref[...] += _rowsum8(dkn * kh)
        dfk_s[rk, :] = jnp.where(lane == 2 * p, -dks[0][:, FOX_D:FOX_D + 1],
                                 jnp.where(lane == 2 * p + 1, -dks[1][:, 0:1], 0.0))

        @pl.when(kj == NQ - 1)
        def _():
            def fin(i, carry):
                r = pl.ds(pl.multiple_of(i * tq, tq), tq)
                d0, d1 = dq_s[0, r, :], dq_s[1, r, :]
                dqn = jnp.where(lo_half, d0, d1)
                _, _, _, qh, _, rqq, _ = _fox_prep(z_ref, gq_v, gk_v, r, lo_half)
                u = dqn * gq_v * (FOX_D ** -0.5)
                dz_ref[r, 0:LANE] = (rqq * (u - qh * _pair_mean(u * qh, lo_half))).astype(BF16)
                dgq_ref[...] += _rowsum8(dqn * qh) * (FOX_D ** -0.5)
                dfc_ref[r, :] = dfk_s[r, :] + jnp.where(lane == 2 * p, d0[:, FOX_D:FOX_D + 1],
                                                        jnp.where(lane == 2 * p + 1, d1[:, 0:1], 0.0))
                return carry
            lax.fori_loop(0, NQ, fin, 0)

    vec = pl.BlockSpec((1, LANE), lambda b, p, k: (0, 0))
    col = pl.BlockSpec((T, LANE), lambda b, p, k: (b, p))
    part = pl.BlockSpec((8, LANE), lambda b, p, k: (b * FOX_P + p, 0))
    if nsw:
        body = _hosting(body, 8, 4, 7, nsw, _chip_swap_phases, (B, FOX_P, NQ))
    return pl.pallas_call(
        body, name="fox_bwd", grid=(B, FOX_P, NQ),
        in_specs=[pl.BlockSpec((T, 384), lambda b, p, k: (b, p)), col, col, col,
                  pl.BlockSpec((T, LANE), lambda b, p, k: (b, 0)), pl.BlockSpec((1, 8, T), lambda b, p, k: (b, 0, 0)),
                  vec, vec] + [ANY] * nsw,
        out_specs=[pl.BlockSpec((T, 384), lambda b, p, k: (b, p)), col, part, part] + [ANY] * nsw,
        out_shape=[S((N, 1536), BF16), S((N, 512), F32), S((B * FOX_P * 8, LANE), F32), S((B * FOX_P * 8, LANE), F32)]
        + [S(p.shape, p.dtype) for p in swap],
        scratch_shapes=[pltpu.VMEM((T, LANE), BF16)] * 4 + [pltpu.VMEM((T, LANE), F32), pltpu.VMEM((2, T, LANE), F32),
                                                            pltpu.VMEM((T, LANE), F32)]
        + (_chip_swap_sems(nsw) if nsw else []),
        compiler_params=_cp(("arbitrary",) * 3 if nsw else ("parallel", "parallel", "arbitrary")),
    )(z, dy, y, lse, fc, fct, gq, gk, *swap)


def _mem_scores(z_ref, kv_ref, gq, gk, h):
    c = slice(MEM_D * h, MEM_D * (h + 1))
    q, k = z_ref[:, c].astype(F32), kv_ref[:, c]
    rq = lax.rsqrt(jnp.mean(q * q, axis=-1, keepdims=True) + EPS)
    rk = lax.rsqrt(jnp.mean(k * k, axis=-1, keepdims=True) + EPS)
    qh, kh = q * rq, k * rk
    qn = (qh * gq * (MEM_D ** -0.5)).astype(BF16)
    kn = (kh * gk).astype(BF16)
    s = _nt(qn, kn)
    pe = jnp.exp(s - jnp.max(s, axis=-1, keepdims=True))
    pn = pe / jnp.sum(pe, axis=-1, keepdims=True)
    return pn, qn, kn, qh, kh, rq, rk


def _mem_fwd(z, memkv, gq, gk, B, T, M, tq=1024):
    N = B * T
    tq = min(tq, T)
    NQ = T // tq
    W = MEM_H * MEM_D

    def body(z_ref, kv_ref, gq_ref, gk_ref, y_ref):
        for h in range(MEM_H):
            pn = _mem_scores(z_ref, kv_ref, gq_ref[...], gk_ref[...], h)[0]
            v = kv_ref[:, W + MEM_D * h:W + MEM_D * (h + 1)].astype(BF16)
            y_ref[:, MEM_D * h:MEM_D * (h + 1)] = _nn(pn.astype(BF16), v).astype(BF16)

    vec = pl.BlockSpec((1, LANE), lambda b, q: (0, 0))
    return pl.pallas_call(
        body, name="mem_fwd", grid=(B, NQ),
        in_specs=[pl.BlockSpec((tq, W), lambda b, q: (b * NQ + q, C_MQ // W)),
                  pl.BlockSpec((M, 2 * W), lambda b, q: (b, 0)), vec, vec],
        out_specs=pl.BlockSpec((tq, W), lambda b, q: (b * NQ + q, 0)), out_shape=S((N, W), BF16),
        compiler_params=_cp(("parallel", "parallel")),
    )(z, memkv, gq, gk)


def _mem_bwd(z, memkv, dy, gq, gk, B, T, M, tq=1024):
    N = B * T
    tq = min(tq, T)
    NQ = T // tq
    W = MEM_H * MEM_D

    def body(z_ref, kv_ref, dy_ref, gq_ref, gk_ref, dz_ref, dkv_ref, dgq_ref, dgk_ref, acc):
        qi = pl.program_id(1)
        gq_v, gk_v = gq_ref[...], gk_ref[...]

        @pl.when(qi == 0)
        def _():
            acc[...] = jnp.zeros_like(acc)
            dgq_ref[...] = jnp.zeros_like(dgq_ref)
            dgk_ref[...] = jnp.zeros_like(dgk_ref)

        for h in range(MEM_H):
            c = slice(MEM_D * h, MEM_D * (h + 1))
            cv = slice(W + MEM_D * h, W + MEM_D * (h + 1))
            pn, qn, kn, qh, _, rq, _ = _mem_scores(z_ref, kv_ref, gq_v, gk_v, h)
            do = dy_ref[:, c].astype(BF16)
            dp = _nt(do, kv_ref[:, cv].astype(BF16))
            ds = (pn * (dp - jnp.sum(dp * pn, axis=-1, keepdims=True))).astype(BF16)
            dqn = _nn(ds, kn)
            acc[:, c] += _tn(ds, qn)
            acc[:, cv] += _tn(pn.astype(BF16), do)
            u = dqn * gq_v * (MEM_D ** -0.5)
            dz_ref[:, c] = (rq * (u - qh * jnp.mean(u * qh, axis=-1, keepdims=True))).astype(BF16)
            dgq_ref[...] += _rowsum8(dqn * qh) * (MEM_D ** -0.5)

        @pl.when(qi == NQ - 1)
        def _():
            for h in range(MEM_H):
                c = slice(MEM_D * h, MEM_D * (h + 1))
                cv = slice(W + MEM_D * h, W + MEM_D * (h + 1))
                k = kv_ref[:, c]
                rk = lax.rsqrt(jnp.mean(k * k, axis=-1, keepdims=True) + EPS)
                kh = k * rk
                dkn = acc[:, c]
                u = dkn * gk_v
                dkv_ref[:, c] = (rk * (u - kh * jnp.mean(u * kh, axis=-1, keepdims=True))).astype(BF16)
                dkv_ref[:, cv] = acc[:, cv].astype(BF16)
                dgk_ref[...] += _rowsum8(dkn * kh)

    vec = pl.BlockSpec((1, LANE), lambda b, q: (0, 0))
    part = pl.BlockSpec((8, LANE), lambda b, q: (b, 0))
    return pl.pallas_call(
        body, name="mem_bwd", grid=(B, NQ),
        in_specs=[pl.BlockSpec((tq, W), lambda b, q: (b * NQ + q, C_MQ // W)),
                  pl.BlockSpec((M, 2 * W), lambda b, q: (b, 0)), pl.BlockSpec((tq, W), lambda b, q: (b * NQ + q, 0)),
                  vec, vec],
        out_specs=[pl.BlockSpec((tq, W), lambda b, q: (b * NQ + q, 0)), pl.BlockSpec((M, 2 * W), lambda b, q: (b, 0)),
                   part, part],
        out_shape=[S((N, W), BF16), S((B * M, 2 * W), BF16), S((B * 8, LANE), F32), S((B * 8, LANE), F32)],
        scratch_shapes=[pltpu.VMEM((M, 2 * W), F32)], compiler_params=_cp(("parallel", "arbitrary")),
    )(z, memkv, dy, gq, gk)


def _merge_fwd(ya, yb, yc, z, x, wa, wb, wc, wo, g_next, tm=512):
    n, d = x.shape
    wdt = ya.shape[1]
    gb = C_GATE // d

    def body(ya_ref, yb_ref, yc_ref, g0_ref, g1_ref, g2_ref, x_ref, wa_ref, wb_ref, wc_ref, wo_ref, gn_ref,
             x1_ref, mg_ref, ua_ref, ub_ref, uc_ref, h_ref):
        merged = jnp.zeros((tm, d), F32)
        for y_ref, g_ref, w_ref, u_ref in ((ya_ref, g0_ref, wa_ref, ua_ref), (yb_ref, g1_ref, wb_ref, ub_ref),
                                           (yc_ref, g2_ref, wc_ref, uc_ref)):
            u = _nn(y_ref[...], w_ref[...])
            u_ref[...] = u.astype(BF16)
            merged = merged + jax.nn.sigmoid(g_ref[...].astype(F32)) * u
        mb = merged.astype(BF16)
        mg_ref[...] = mb
        x1 = x_ref[...] + _nn(mb, wo_ref[...])
        x1_ref[...] = x1
        h_ref[...] = (x1 * lax.rsqrt(jnp.mean(x1 * x1, axis=-1, keepdims=True) + EPS) * gn_ref[...]).astype(BF16)

    yt = pl.BlockSpec((tm, wdt), lambda i: (i, 0))
    xt = pl.BlockSpec((tm, d), lambda i: (i, 0))
    wbr = pl.BlockSpec((wdt, d), lambda i: (0, 0))
    gates = [pl.BlockSpec((tm, d), functools.partial(lambda i, k: (i, gb + k), k=k)) for k in range(3)]
    return pl.pallas_call(
        body, name="merge_fwd", grid=(n // tm,),
        in_specs=[yt, yt, yt] + gates + [xt, wbr, wbr, wbr, pl.BlockSpec((d, d), lambda i: (0, 0)),
                                         pl.BlockSpec((1, d), lambda i: (0, 0))],
        out_specs=[xt] * 6, out_shape=[S((n, d), F32)] + [S((n, d), BF16)] * 5, compiler_params=_cp(("parallel",)),
    )(ya, yb, yc, z, z, z, x, wa, wb, wc, wo, g_next)


def _merge_bwd(dx1, z, ua, ub, uc, wa, wb, wc, wo, tm=512):
    n, d = dx1.shape
    wdt = wa.shape[0]
    gb = C_GATE // d

    def body(dx_ref, g0_ref, g1_ref, g2_ref, ua_ref, ub_ref, uc_ref, wa_ref, wb_ref, wc_ref, wo_ref,
             dg_ref, dya_ref, dyb_ref, dyc_ref, dua_ref, dub_ref, duc_ref):
        dm = _nt(dx_ref[...].astype(BF16), wo_ref[...])
        for k, (g_ref, u_ref, w_ref, dy_ref, du_ref) in enumerate((
                (g0_ref, ua_ref, wa_ref, dya_ref, dua_ref), (g1_ref, ub_ref, wb_ref, dyb_ref, dub_ref),
                (g2_ref, uc_ref, wc_ref, dyc_ref, duc_ref))):
            g = jax.nn.sigmoid(g_ref[...].astype(F32))
            du = (dm * g).astype(BF16)
            du_ref[...] = du
            dg_ref[:, d * k:d * (k + 1)] = (dm * u_ref[...].astype(F32) * g * (1.0 - g)).astype(BF16)
            dy_ref[...] = _nt(du, w_ref[...])

    yt = pl.BlockSpec((tm, wdt), lambda i: (i, 0))
    xt = pl.BlockSpec((tm, d), lambda i: (i, 0))
    wbr = pl.BlockSpec((wdt, d), lambda i: (0, 0))
    gates = [pl.BlockSpec((tm, d), functools.partial(lambda i, k: (i, gb + k), k=k)) for k in range(3)]
    return pl.pallas_call(
        body, name="merge_bwd", grid=(n // tm,),
        in_specs=[xt] + gates + [xt, xt, xt, wbr, wbr, wbr, pl.BlockSpec((d, d), lambda i: (0, 0))],
        out_specs=[pl.BlockSpec((tm, 3 * d), lambda i: (i, 0)), yt, yt, yt, xt, xt, xt],
        out_shape=[S((n, 3 * d), BF16)] + [S((n, wdt), F32)] * 3 + [S((n, d), BF16)] * 3,
        compiler_params=_cp(("parallel",)),
    )(dx1, z, z, z, ua, ub, uc, wa, wb, wc, wo)


FFN_TN = 1408
TN_TM = 2048
INV_SQRT2 = 0.7071067811865476
INV_SQRT_2PI = 0.3989422804014327


def _conv_shifted(a, prev, first, tm):
    row = _iota(a.shape, 0)
    p7 = jnp.where(first, 0.0, prev[7:8, :])
    p6 = jnp.where(first, 0.0, prev[6:7, :])
    a1 = jnp.where(row == 0, p7, pltpu.roll(a, 1, 0))
    a2 = jnp.where(row == 0, p6, jnp.where(row == 1, p7, pltpu.roll(a, 2, 0)))
    return a1, a2


def _ffn_act_fwd(up, cw, cb, B, T, tm=1024):
    N = B * T
    tm = min(tm, T)
    dff = cw.shape[1]
    NT, NJ, tn = T // tm, dff // FFN_TN, FFN_TN

    def body(a_ref, v_ref, cw_ref, cb_ref, y_ref, c_ref, carry):
        t = pl.program_id(2)
        a = a_ref[...].astype(F32)
        a1, a2 = _conv_shifted(a, carry[...], t == 0, tm)
        w = cw_ref[...]
        ac = w[0:1, :] * a2 + w[1:2, :] * a1 + w[2:3, :] * a + cb_ref[...]
        cdf = 0.5 * (1.0 + lax.erf(ac * INV_SQRT2))
        y_ref[...] = (ac * cdf * v_ref[...].astype(F32)).astype(BF16)
        c_ref[...] = cdf.astype(BF16)
        carry[...] = a[tm - 8:tm, :]

    return pl.pallas_call(
        body, name="ffn_act_fwd", grid=(B, NJ, NT),
        in_specs=[pl.BlockSpec((tm, tn), lambda b, j, t: (b * NT + t, j)),
                  pl.BlockSpec((tm, tn), lambda b, j, t: (b * NT + t, NJ + j)),
                  pl.BlockSpec((3, tn), lambda b, j, t: (0, j)), pl.BlockSpec((1, tn), lambda b, j, t: (0, j))],
        out_specs=[pl.BlockSpec((tm, tn), lambda b, j, t: (b * NT + t, j))] * 2, out_shape=[S((N, dff), BF16)] * 2,
        scratch_shapes=[pltpu.VMEM((8, tn), F32)], compiler_params=_cp(("parallel", "parallel", "arbitrary")),
    )(up, up, cw, cb)


def _ffn_down_loss(y, wd, x1, tgt, tm=512):
    n, d = x1.shape
    kf = y.shape[1]

    def body(y_ref, w_ref, x_ref, t_ref, dx_ref, ls_ref):
        err = x_ref[...] + _nn(y_ref[...], w_ref[...]) - t_ref[...]
        dx_ref[...] = err * (1.0 / d)

        @pl.when(pl.program_id(0) == 0)
        def _():
            ls_ref[...] = jnp.zeros_like(ls_ref)

        ls_ref[...] += _rowsum8(err * err) * (0.5 / d)

    xt = pl.BlockSpec((tm, d), lambda i: (i, 0))
    return pl.pallas_call(
        body, name="ffn_down_loss", grid=(n // tm,),
        in_specs=[pl.BlockSpec((tm, kf), lambda i: (i, 0)), pl.BlockSpec((kf, d), lambda i: (0, 0)), xt, xt],
        out_specs=[xt, pl.BlockSpec((8, d), lambda i: (0, 0))], out_shape=[S((n, d), F32), S((8, d), F32)],
        compiler_params=_cp(("arbitrary",)),
    )(y, wd, x1, tgt)


def _ffn_act_bwd1(dx2, wd, up, cdf, cw, cb, B, T, tm=512):
    N = B * T
    tm = min(tm, T)
    d = dx2.shape[1]
    dff = cw.shape[1]
    NT, NJ, tn = T // tm, dff // FFN_TN, FFN_TN

    def body(dx_ref, w_ref, a_ref, v_ref, c_ref, cw_ref, cb_ref, dac_ref, dv_ref, dcw_ref, dcb_ref, carry):
        b, t = pl.program_id(1), pl.program_id(2)
        a = a_ref[...].astype(F32)
        a1, a2 = _conv_shifted(a, carry[...], t == 0, tm)
        carry[...] = a[tm - 8:tm, :]
        w = cw_ref[...]
        ac = w[0:1, :] * a2 + w[1:2, :] * a1 + w[2:3, :] * a + cb_ref[...]
        dy = _nt(dx_ref[...].astype(BF16), w_ref[...])
        cdf = c_ref[...].astype(F32)
        dv_ref[...] = (dy * ac * cdf).astype(BF16)
        dac = dy * v_ref[...].astype(F32) * (cdf + ac * jnp.exp(-0.5 * ac * ac) * INV_SQRT_2PI)
        dac_ref[...] = dac

        @pl.when((b == 0) & (t == 0))
        def _():
            dcw_ref[...] = jnp.zeros_like(dcw_ref)
            dcb_ref[...] = jnp.zeros_like(dcb_ref)

        dcw_ref[0:8, :] += _rowsum8(dac * a2)
        dcw_ref[8:16, :] += _rowsum8(dac * a1)
        dcw_ref[16:24, :] += _rowsum8(dac * a)
        dcb_ref[...] += _rowsum8(dac)

    return pl.pallas_call(
        body, name="ffn_act_bwd1", grid=(NJ, B, NT),
        in_specs=[pl.BlockSpec((tm, d), lambda j, b, t: (b * NT + t, 0)), pl.BlockSpec((tn, d), lambda j, b, t: (j, 0)),
                  pl.BlockSpec((tm, tn), lambda j, b, t: (b * NT + t, j)),
                  pl.BlockSpec((tm, tn), lambda j, b, t: (b * NT + t, NJ + j)),
                  pl.BlockSpec((tm, tn), lambda j, b, t: (b * NT + t, j)),
                  pl.BlockSpec((3, tn), lambda j, b, t: (0, j)), pl.BlockSpec((1, tn), lambda j, b, t: (0, j))],
        out_specs=[pl.BlockSpec((tm, tn), lambda j, b, t: (b * NT + t, j)),
                   pl.BlockSpec((tm, tn), lambda j, b, t: (b * NT + t, j)),
                   pl.BlockSpec((24, tn), lambda j, b, t: (0, j)), pl.BlockSpec((8, tn), lambda j, b, t: (0, j))],
        out_shape=[S((N, dff), F32), S((N, dff), BF16), S((24, dff), F32), S((8, dff), F32)],
        scratch_shapes=[pltpu.VMEM((8, tn), F32)], compiler_params=_cp(("parallel", "arbitrary", "arbitrary")),
    )(dx2, wd, up, up, cdf, cw, cb)


def _ffn_act_bwd2(dac, cw, B, T, tm=1024):
    N = B * T
    tm = min(tm, T)
    dff = cw.shape[1]
    NT, NJ, tn = T // tm, dff // FFN_TN, FFN_TN
    last8 = N // 8 - 1

    def body(d_ref, nx_ref, cw_ref, da_ref):
        t = pl.program_id(2)
        dd = d_ref[...]
        row = _iota(dd.shape, 0)
        last = t == NT - 1
        n0 = jnp.where(last, 0.0, nx_ref[0:1, :])
        n1 = jnp.where(last, 0.0, nx_ref[1:2, :])
        d1 = jnp.where(row == tm - 1, n0, pltpu.roll(dd, tm - 1, 0))
        d2 = jnp.where(row == tm - 1, n1, jnp.where(row == tm - 2, n0, pltpu.roll(dd, tm - 2, 0)))
        w = cw_ref[...]
        da_ref[...] = (w[2:3, :] * dd + w[1:2, :] * d1 + w[0:1, :] * d2).astype(BF16)

    return pl.pallas_call(
        body, name="ffn_act_bwd2", grid=(B, NJ, NT),
        in_specs=[pl.BlockSpec((tm, tn), lambda b, j, t: (b * NT + t, j)),
                  pl.BlockSpec((8, tn), lambda b, j, t: (jnp.minimum((b * NT + t + 1) * (tm // 8), last8), j)),
                  pl.BlockSpec((3, tn), lambda b, j, t: (0, j))],
        out_specs=pl.BlockSpec((tm, tn), lambda b, j, t: (b * NT + t, j)), out_shape=S((N, dff), BF16),
        compiler_params=_cp(("parallel", "parallel", "parallel")),
    )(dac, dac, cw)


def _fold_rows(p, name):
    r, c = p.shape[0] // 8, p.shape[1]

    def body(p_ref, o_ref):
        for j in range(r):
            o_ref[j:j + 1, :] = jnp.sum(p_ref[8 * j:8 * (j + 1), :], axis=0, keepdims=True)

    return pl.pallas_call(body, name=name, out_shape=S((r, c), F32), compiler_params=_cp())(p)


def _small_reduce(lbl, dg_mix, dg_mem, dlb_p, dgn_p, dfb_p, dgq_p, dgk_p, dmq_p, dmk_p, dg_ffn, dcb_p, loss_p):
    d, dff = dg_mix.shape[1], dcb_p.shape[1]
    nbh = dlb_p.shape[0] // (8 * HG_H)

    def colsum(ref):
        return jnp.sum(ref[...], axis=0, keepdims=True)

    def body(lbl_ref, mix_ref, mem_ref, dlb_ref, dgn_ref, dfb_ref, dgq_ref, dgk_ref, dmq_ref, dmk_ref, ffn_ref, dcb_ref,
             ls_ref, o_mix, o_mem, o_lb, o_hgn, o_fb, o_fq, o_fk, o_mq, o_mk, o_ffn, o_cb, o_loss):
        o_mix[...], o_mem[...], o_ffn[...], o_cb[...] = colsum(mix_ref), colsum(mem_ref), colsum(ffn_ref), colsum(dcb_ref)
        o_hgn[...], o_fb[...], o_mq[...], o_mk[...] = colsum(dgn_ref), colsum(dfb_ref), colsum(dmq_ref), colsum(dmk_ref)
        for src, dst in ((dgq_ref, o_fq), (dgk_ref, o_fk)):
            v = colsum(src)
            dst[...] = v + pltpu.roll(v, FOX_D, 1)
        o_loss[...] = jnp.zeros((1, LANE), F32) + jnp.sum(colsum(ls_ref), axis=-1, keepdims=True)
        logits = lbl_ref[...]
        e = jnp.exp(logits - jnp.max(logits, axis=0, keepdims=True))
        pr = e / jnp.sum(e, axis=0, keepdims=True)
        rows = _iota((8, LANE), 0)
        for h in range(HG_H):
            acc = jnp.zeros((8, LANE), F32)
            for b in range(nbh):
                acc = acc + dlb_ref[8 * (b * HG_H + h):8 * (b * HG_H + h + 1), :]
            dlb = jnp.sum(acc, axis=0, keepdims=True)
            c = slice(LANE * h, LANE * (h + 1))
            p0 = pr[0:1, c]
            first = _iota((logits.shape[0], LANE), 0) == 0
            o_lb[:, c] = pr[:, c] * (jnp.where(first, 1.0, 0.0) - p0) * dlb

    outs = [S((1, d), F32), S((1, d), F32), S(lbl.shape, F32)] + [S((1, LANE), F32)] * 6 + \
           [S((1, d), F32), S((1, dff), F32), S((1, LANE), F32)]
    return pl.pallas_call(body, name="small_reduce", out_shape=outs, compiler_params=_cp())(
        lbl, dg_mix, dg_mem, dlb_p, dgn_p, dfb_p, dgq_p, dgk_p, dmq_p, dmk_p, dg_ffn, dcb_p, loss_p)


def _in_col_pieces():
    hw, fw = HG_H * HG_D, FOX_H * FOX_D
    fox0, ff0 = 4 * hw, 4 * hw + 3 * fw
    mq0 = ff0 + FOX_H
    gate0 = mq0 + MEM_H * MEM_D
    pieces = []
    for p in range(FOX_P):
        pieces += [(fox0 + j * fw + LANE * p, LANE) for j in range(3)]
    pieces.append((mq0, MEM_H * MEM_D))
    for h in range(HG_H):
        pieces += [(j * hw + HG_D * h, HG_D) for j in range(4)]
    pieces.append((gate0, C_FF - C_GATE))
    pieces.append((ff0, FOX_H))
    return pieces


def _perm_from_blocks(blocks):
    n_blk, _, c = blocks.shape
    parts = []
    for s, n in _in_col_pieces():
        lo = s
        while lo < s + n:
            d = lo // c
            hi = min(s + n, (d + 1) * c)
            parts.append(blocks[d][:, lo - d * c:hi - d * c])
            lo = hi
    parts.append(jnp.zeros((blocks.shape[1], C_END - C_FF - FOX_H), blocks.dtype))
    return jnp.concatenate(parts, axis=1)


def _unperm_blocks(segs, n_blk):
    starts = [0]
    for a in segs:
        starts.append(starts[-1] + a.shape[1])
    new_start, placed = 0, []
    for s, n in _in_col_pieces():
        placed.append((s, new_start, n))
        new_start += n
    placed.sort()
    c = sum(n for _, _, n in placed) // n_blk
    blocks = []
    for d in range(n_blk):
        parts = []
        for s, ns, n in placed:
            lo, hi = max(s, d * c), min(s + n, (d + 1) * c)
            if lo < hi:
                i = max(j for j in range(len(segs)) if starts[j] <= ns)
                parts.append(segs[i][:, ns + lo - s - starts[i]:ns + hi - s - starts[i]])
        blocks.append(jnp.concatenate(parts, axis=1))
    return jnp.stack(blocks)


def _local_step(x2, mem2, tgt, sm, W, B, T, M, ex=None):
    fbias = jnp.pad(sm["fox_f_bias"], ((0, 0), (0, LANE - FOX_H)))
    gq2 = jnp.concatenate([sm["fox_q_norm_g"]] * 2, axis=1)
    gk2 = jnp.concatenate([sm["fox_k_norm_g"]] * 2, axis=1)
    lbl = sm["hgrn_lb_logits"]
    if ex:
        h, *first = _rmsnorm_cast(x2, sm["norm_mix_g"], "norm_mix", gather=ex.first_blocks())
        W = ex.unpack_first(first)
    else:
        h = _rmsnorm_cast(x2, sm["norm_mix_g"], "norm_mix")
    z = _mm_nn(h, W["w_in"], BF16, "proj_in", 512, C_END)
    memn = _rmsnorm_cast(mem2, sm["norm_mem_g"], "norm_mem", tm=256)
    memkv = _mm_nn(memn, W["mem_kv_w"], F32, "proj_memkv", 256, 512)
    ya, o_raw, states, a_mat = _hgrn_fwd(z, lbl, sm["hgrn_norm_g"], B, T)
    fc, fct = _fox_gate_fwd(z, fbias, B, T)
    yb, lse, *late = _fox_fwd(z, fc, fct, gq2, gk2, B, T, gather=ex.late_blocks() if ex else ())
    if ex:
        W = {**W, **ex.unpack_late(late)}
    yc = _mem_fwd(z, memkv, sm["mem_q_norm_g"], sm["mem_k_norm_g"], B, T, M)
    x1, merged, ua, ub, uc, h2 = _merge_fwd(ya, yb, yc, z, x2, W["w_br_hgrn"], W["w_br_fox"], W["w_br_mem"], W["w_out"],
                                            sm["norm_ffn_g"])
    up = _mm_nn(h2, W["ffn_w_up"], BF16, "ffn_up", 512, 2 * FFN_TN)
    yf, cdf = _ffn_act_fwd(up, W["ffn_conv_w"], sm["ffn_conv_b"], B, T)
    dx2, loss_p = _ffn_down_loss(yf, W["ffn_w_down"], x1, tgt)
    dff = W["ffn_conv_w"].shape[1]
    dac, dv, dcw_p, dcb_p = _ffn_act_bwd1(dx2, W["ffn_w_down"], up, cdf, W["ffn_conv_w"], sm["ffn_conv_b"], B, T)
    da = _ffn_act_bwd2(dac, W["ffn_conv_w"], B, T)
    g = {"ffn_conv_w": _fold_rows(dcw_p, "g_conv_w")}
    g["ffn_w_down"] = _mm_tn(yf, dx2, "g_w_down", TN_TM, 512)
    dh2 = _mm_nt_sum([(da, 0, dff, 0), (dv, 0, dff, dff)], W["ffn_w_up"], "dh2", 512)
    g["ffn_w_up"] = [_mm_tn(h2, da, "g_w_up_a", TN_TM, dff), _mm_tn(h2, dv, "g_w_up_v", TN_TM, dff)]
    dx1, dg_ffn = _rmsnorm_bwd(dh2, x1, sm["norm_ffn_g"], dx2, "norm_ffn_bwd")
    g["w_out"] = _mm_tn(merged, dx1, "g_w_out", TN_TM, 1024)
    dgate, dya, dyb, dyc, dua, dub, duc = _merge_bwd(dx1, z, ua, ub, uc, W["w_br_hgrn"], W["w_br_fox"], W["w_br_mem"],
                                                    W["w_out"])
    g["w_br_hgrn"] = _mm_tn(ya, dua, "g_w_br_hgrn", TN_TM, 1024)
    g["w_br_fox"] = _mm_tn(yb, dub, "g_w_br_fox", TN_TM, 1024)
    g["w_br_mem"] = _mm_tn(yc, duc, "g_w_br_mem", TN_TM, 1024)
    early_pk = ex.early_grads(g) if ex else ()
    dz_hg, dlb_p, dgn_p, *early_sib = _hgrn_bwd(z, o_raw, states, a_mat, dya, lbl, sm["hgrn_norm_g"], B, T,
                                                swap_sibling=early_pk)
    dz_fox, dfc, dgq_p, dgk_p, *early_chips = _fox_bwd(z, dyb, yb, lse, fc, fct, gq2, gk2, B, T,
                                                       swap=ex.pair_sums(early_pk, early_sib, "early") if ex else ())
    dz_ff, dfb_p = _fox_gate_bwd(dfc, z, fbias, B, T)
    dz_mq, dkv, dmq_p, dmk_p = _mem_bwd(z, memkv, dyc, sm["mem_q_norm_g"], sm["mem_k_norm_g"], B, T, M)
    g["mem_kv_w"] = _mm_tn(memn, dkv, "g_mem_kv_w", 256, 512)
    dmemn = _mm_nt_sum([(dkv, 0, dkv.shape[1], 0)], W["mem_kv_w"], "d_memn", 256)
    _, dg_mem = _rmsnorm_bwd(dmemn, mem2, sm["norm_mem_g"], None, "norm_mem_bwd", tm=256)
    d = x2.shape[1]
    parts = [(dz_fox, 0, C_MQ - C_FOX, C_FOX), (dz_mq, 0, C_HG - C_MQ, C_MQ), (dz_hg, 0, C_GATE - C_HG, C_HG)]
    parts += [(dgate, d * k, d, C_GATE + d * k) for k in range(3)] + [(dz_ff, 0, C_END - C_FF, C_FF)]
    g["w_in"] = [_mm_tn(h, dzs, "g_w_in_%d" % i, 2 * TN_TM,
                        max(t for t in (1024, 768, 512, LANE) if dzs.shape[1] % t == 0))
                 for i, dzs in enumerate((dz_fox, dz_mq, dz_hg, dgate, dz_ff))]
    sums = None
    if ex:
        last_pk = ex.last_grads(g)
        last_sib = _swap_with_sibling(last_pk, "rs_sibling_last")
        dh, last_chips = _mm_nt_sum(parts, W["w_in"], "dh", 512, swap=ex.pair_sums(last_pk, last_sib, "last"))
        sums = (ex.final_sums(early_pk, early_sib, early_chips, "early"),
                ex.final_sums(last_pk, last_sib, last_chips, "last"))
    else:
        dh = _mm_nt_sum(parts, W["w_in"], "dh", 512)
    grad_x, dg_mix = _rmsnorm_bwd(dh, x2, sm["norm_mix_g"], dx1, "norm_mix_bwd")
    small = _small_reduce(lbl, dg_mix, dg_mem, dlb_p, dgn_p, dfb_p, dgq_p, dgk_p, dmq_p, dmk_p, dg_ffn, dcb_p, loss_p)
    names = ("norm_mix_g", "norm_mem_g", "hgrn_lb_logits", "hgrn_norm_g", "fox_f_bias", "fox_q_norm_g", "fox_k_norm_g",
             "mem_q_norm_g", "mem_k_norm_g", "norm_ffn_g", "ffn_conv_b", "loss")
    g.update(dict(zip(names, small)))
    return grad_x, g, sums


ANY = pl.BlockSpec(memory_space=pl.ANY)


def _position():
    return lax.axis_index("x"), lax.axis_index("y"), lax.axis_index("c")


def _all_gather(blocks, name):
    nb = len(blocks)

    def body(*refs):
        start, forward, finish = _gather_phases(refs[:nb], refs[nb:2 * nb], *refs[2 * nb:])
        start()
        forward()
        finish()

    return pl.pallas_call(
        body, name=name, out_shape=_gather_shapes(blocks), in_specs=[ANY] * nb, out_specs=[ANY] * nb,
        scratch_shapes=_gather_sems(nb),
    )(*blocks)


def _hosting(body, n_in, n_out, n_scratch, n_x, make_phases, grid):
    n_steps = math.prod(grid)

    def hosted(*refs):
        a = n_in + n_x
        b = a + n_out + n_x
        ins, xs = refs[:n_in], refs[n_in:a]
        outs, x_outs = refs[a:a + n_out], refs[a + n_out:b]
        scratch, sems = refs[b:b + n_scratch], refs[b + n_scratch:]
        step = 0
        for ax, n in enumerate(grid):
            step = step * n + pl.program_id(ax)
        phases = make_phases(xs, x_outs, *sems)
        pl.when(step == 0)(phases[0])
        for ph in phases[1:-1]:
            pl.when(step == n_steps // 2)(ph)
        body(*ins, *outs, *scratch)
        pl.when(step == n_steps - 1)(phases[-1])

    return hosted


def _gather_shapes(blocks):
    return [S((N_DEV,) + b.shape, b.dtype) for b in blocks]


def _gather_sems(nb):
    return [pltpu.SemaphoreType.DMA((7 * nb,)), pltpu.SemaphoreType.DMA((7 * nb,)), pltpu.SemaphoreType.DMA((nb,))]


def _gather_phases(x_refs, out_refs, send_sems, recv_sems, local_sems):
    nb = len(x_refs)
    x, y, c = _position()
    me, sibling = (x, y, c), (x, y, 1 - c)
    chips = [(1 - x, y), (x, 1 - y), (1 - x, 1 - y)]

    def copy(i, k, blk, to, own=False):
        px, py, pc = blk
        slot = out_refs[i].at[4 * px + 2 * py + pc]
        return pltpu.make_async_remote_copy(
            src_ref=x_refs[i] if own else slot, dst_ref=slot, send_sem=send_sems.at[7 * i + k],
            recv_sem=recv_sems.at[7 * i + k], device_id=to, device_id_type=MESH)

    def mine(i):
        return pltpu.make_async_copy(x_refs[i], out_refs[i].at[4 * x + 2 * y + c], local_sems.at[i])

    def first(i):
        return [copy(i, 0, me, sibling, own=True)] + [copy(i, 1 + j, me, (*chip, c), own=True)
                                                     for j, chip in enumerate(chips)]

    def passed(i, j):
        return copy(i, 4 + j, (*chips[j], c), sibling)

    def start():
        for i in range(nb):
            mine(i).start()
            for cp in first(i):
                cp.start()

    def forward():
        for i in range(nb):
            for j, chip in enumerate(chips):
                copy(i, 1 + j, (*chip, c), me).wait_recv()
                passed(i, j).start()

    def finish():
        for i in range(nb):
            copy(i, 0, sibling, me).wait_recv()
            for j, chip in enumerate(chips):
                copy(i, 4 + j, (*chip, 1 - c), me).wait_recv()
        for i in range(nb):
            for cp in first(i) + [passed(i, j) for j in range(3)]:
                cp.wait_send()
            mine(i).wait()

    return start, forward, finish


def _swap_with_sibling(pks, name):
    nb = len(pks)

    def body(*refs):
        start, finish = _sibling_swap_phases(refs[:nb], refs[nb:2 * nb], *refs[2 * nb:])
        start()
        finish()

    return pl.pallas_call(
        body, name=name, out_shape=_sibling_swap_shapes(pks), in_specs=[ANY] * nb, out_specs=[ANY] * nb,
        scratch_shapes=_sibling_swap_sems(nb),
    )(*pks)


def _sibling_swap_shapes(pks):
    return [S((4,) + p.shape[1:], p.dtype) for p in pks]


def _sibling_swap_sems(nb):
    return [pltpu.SemaphoreType.DMA((4 * nb,)), pltpu.SemaphoreType.DMA((4 * nb,))]


def _sibling_swap_phases(pk_refs, out_refs, send_sems, recv_sems):
    nb = len(pk_refs)
    x, y, c = _position()

    def copies():
        return [pltpu.make_async_remote_copy(
            src_ref=pk_refs[i].at[2 * k + 1 - c], dst_ref=out_refs[i].at[k], send_sem=send_sems.at[4 * i + k],
            recv_sem=recv_sems.at[4 * i + k], device_id=(x, y, 1 - c), device_id_type=MESH)
            for i in range(nb) for k in range(4)]

    def start():
        for cp in copies():
            cp.start()

    def finish():
        for cp in copies():
            cp.wait()

    return start, finish


def _swap_between_chips(pbs, name):
    nb = len(pbs)

    def body(*refs):
        start, finish = _chip_swap_phases(refs[:nb], refs[nb:2 * nb], *refs[2 * nb:])
        start()
        finish()

    return pl.pallas_call(
        body, name=name, out_shape=[S(p.shape, p.dtype) for p in pbs], in_specs=[ANY] * nb, out_specs=[ANY] * nb,
        scratch_shapes=_chip_swap_sems(nb),
    )(*pbs)


def _chip_swap_sems(nb):
    return [pltpu.SemaphoreType.DMA((3 * nb,)), pltpu.SemaphoreType.DMA((3 * nb,)), pltpu.SemaphoreType.DMA((nb,))]


def _chip_swap_phases(pb_refs, out_refs, send_sems, recv_sems, local_sems):
    nb = len(pb_refs)
    x, y, c = _position()
    me = 2 * x + y
    chips = [(1 - x, y), (x, 1 - y), (1 - x, 1 - y)]

    def local(i):
        return pltpu.make_async_copy(pb_refs[i].at[me], out_refs[i].at[me], local_sems.at[i])

    def send(i, j):
        cx, cy = chips[j]
        return pltpu.make_async_remote_copy(
            src_ref=pb_refs[i].at[2 * cx + cy], dst_ref=out_refs[i].at[me], send_sem=send_sems.at[3 * i + j],
            recv_sem=recv_sems.at[3 * i + j], device_id=(cx, cy, c), device_id_type=MESH)

    def arrival(i, j):
        cx, cy = chips[j]
        return pltpu.make_async_remote_copy(
            src_ref=pb_refs[i].at[me], dst_ref=out_refs[i].at[2 * cx + cy], send_sem=send_sems.at[3 * i + j],
            recv_sem=recv_sems.at[3 * i + j], device_id=(cx, cy, c), device_id_type=MESH)

    def start():
        for i in range(nb):
            local(i).start()
            for j in range(3):
                send(i, j).start()

    def finish():
        for i in range(nb):
            for j in range(3):
                arrival(i, j).wait_recv()
        for i in range(nb):
            for j in range(3):
                send(i, j).wait_send()
            local(i).wait()

    return start, finish


def _row_tile(r):
    return max(t for t in range(16, min(r, 1024) + 1, 16) if r % t == 0)


def _pair_sum_cast(pk, recv, core, name):
    _, r, l = pk.shape
    tr = _row_tile(r)

    def body(c_ref, a_ref, b_ref, o_ref):
        o_ref[...] = (a_ref[...] + b_ref[...]).astype(BF16)

    return pl.pallas_call(
        body, name=name,
        grid_spec=pltpu.PrefetchScalarGridSpec(
            num_scalar_prefetch=1, grid=(4, r // tr),
            in_specs=[pl.BlockSpec((None, tr, l), lambda k, i, c: (2 * k + c[0], i, 0)),
                      pl.BlockSpec((None, tr, l), lambda k, i, c: (k, i, 0))],
            out_specs=pl.BlockSpec((None, tr, l), lambda k, i, c: (k, i, 0))),
        out_shape=S((4, r, l), BF16), compiler_params=_cp(("parallel", "parallel")),
    )(core, pk, recv)


def _final_sum(pk, recv_sib, recv_chips, slot, chip, name):
    _, r, l = pk.shape
    tr = _row_tile(r)

    def body(s_ref, k_ref, a_ref, b_ref, rc_ref, o_ref):
        base = a_ref[...] + b_ref[...]
        acc = jnp.zeros_like(base)
        for j in range(4):
            acc = acc + jnp.where(k_ref[0] == j, base, rc_ref[j].astype(F32))
        o_ref[...] = acc

    return pl.pallas_call(
        body, name=name,
        grid_spec=pltpu.PrefetchScalarGridSpec(
            num_scalar_prefetch=2, grid=(r // tr,),
            in_specs=[pl.BlockSpec((None, tr, l), lambda i, s, k: (s[0], i, 0)),
                      pl.BlockSpec((None, tr, l), lambda i, s, k: (k[0], i, 0)),
                      pl.BlockSpec((4, tr, l), lambda i, s, k: (0, i, 0))],
            out_specs=pl.BlockSpec((tr, l), lambda i, s, k: (i, 0))),
        out_shape=S((r, l), F32), compiler_params=_cp(("parallel",)),
    )(slot, chip, pk, recv_sib, recv_chips)


def _adamw_math(w, g, m, v):
    m = ADAM_B1 * m + (1.0 - ADAM_B1) * g
    v = ADAM_B2 * v + (1.0 - ADAM_B2) * (g * g)
    m_hat = m / (1.0 - ADAM_B1 ** ADAM_STEP)
    v_hat = v / (1.0 - ADAM_B2 ** ADAM_STEP)
    return -ADAM_LR * (m_hat / (jnp.sqrt(v_hat) + ADAM_EPS) + ADAM_WD * w), m, v


def _adamw(w, g, m, v, name):
    r, c = w.shape
    tr = 512 if r % 512 == 0 else r

    def body(w_ref, g_ref, m_ref, v_ref, d_ref, nm_ref, nv_ref):
        d_ref[...], nm_ref[...], nv_ref[...] = _adamw_math(w_ref[...], g_ref[...], m_ref[...], v_ref[...])

    tile = pl.BlockSpec((tr, c), lambda i: (i, 0))
    return pl.pallas_call(
        body, name=name, grid=(r // tr,), in_specs=[tile] * 4, out_specs=[tile] * 3, out_shape=[S((r, c), F32)] * 3,
        compiler_params=_cp(("parallel",)),
    )(w, g, m, v)


def _small_update(gathered, w, m, v):
    def body(ga_ref, w_ref, m_ref, v_ref, g_ref, d_ref, nm_ref, nv_ref):
        g = ga_ref[0]
        for k in range(1, N_DEV):
            g = g + ga_ref[k]
        g_ref[...] = g
        d_ref[...], nm_ref[...], nv_ref[...] = _adamw_math(w_ref[...], g, m_ref[...], v_ref[...])

    return pl.pallas_call(body, name="small_update", out_shape=[S(w.shape, F32)] * 4, compiler_params=_cp())(
        gathered, w, m, v)


BIG = ("w_in", "mem_kv_w", "w_br_hgrn", "w_br_fox", "w_br_mem", "w_out", "ffn_w_up", "ffn_conv_w", "ffn_w_down")
GROUP_ROWS = ("w_out", "ffn_w_down")
GROUP_LANE = ("w_br_hgrn", "w_br_fox", "w_br_mem")
LANE_GROUP_ROWS = 224
SMALL = ("norm_mix_g", "norm_mem_g", "hgrn_lb_logits", "hgrn_norm_g", "fox_f_bias", "fox_q_norm_g", "fox_k_norm_g",
         "mem_q_norm_g", "mem_k_norm_g", "norm_ffn_g", "ffn_conv_b")


def _rows_of(n_elems):
    return -(-n_elems // LANE)


def _to_rows(a, lead=0):
    flat = a.reshape(a.shape[:lead] + (-1,))
    pad = (-flat.shape[-1]) % LANE
    if pad:
        flat = jnp.pad(flat, [(0, 0)] * lead + [(0, pad)])
    return flat.reshape(a.shape[:lead] + (-1, LANE))


def _stack_rows(parts, lead, total_rows):
    buf = jnp.concatenate(parts, axis=lead)
    pad = total_rows - buf.shape[lead]
    return jnp.pad(buf, [(0, 0)] * lead + [(0, pad), (0, 0)])


def _round_up(n, k):
    return -(-n // k) * k


def _from_rows(rows, shape, lead=0):
    n = math.prod(shape)
    return rows.reshape(rows.shape[:lead] + (-1,))[..., :n].reshape(rows.shape[:lead] + tuple(shape))


def _blocks_to_full(blocks, kind):
    n, a, b = blocks.shape
    return blocks.transpose(1, 0, 2).reshape(a, n * b) if kind == "col" else blocks.reshape(n * a, b)


def _full_to_blocks(full, kind, n=N_DEV):
    a, b = full.shape
    return full.reshape(a, n, b // n).transpose(1, 0, 2) if kind == "col" else full.reshape(n, a // n, b)


def _lane_group_rows(shard):
    n_lane = sum(shard[n].shape[0] for n in GROUP_LANE)
    n_cw = shard["ffn_conv_w"].size
    return n_lane, _rows_of(3 * n_cw), _rows_of(n_cw), _round_up(n_lane + _rows_of(3 * n_cw), LANE_GROUP_ROWS)


def _split_bf16x3(x):
    hi = x.astype(BF16)
    r1 = x - hi.astype(F32)
    mid = r1.astype(BF16)
    return jnp.stack([hi, mid, (r1 - mid.astype(F32)).astype(BF16)])


class _Exchange:
    def __init__(self, shard):
        self.shard = shard
        xi, yi, ci = _position()
        self.core = ci.astype(jnp.int32).reshape(1)
        self.chip = (2 * xi + yi).astype(jnp.int32).reshape(1)
        self.n_lane, self.r_pieces, self.r_vals, self.r_lane = _lane_group_rows(shard)

    def first_blocks(self):
        return [self.shard["w_in"].astype(BF16), self.shard["mem_kv_w"].astype(BF16)]

    def unpack_first(self, gathered):
        return {"w_in": _perm_from_blocks(gathered[0]), "mem_kv_w": _blocks_to_full(gathered[1], "row")}

    def late_blocks(self):
        sh = self.shard
        lane_rows = [sh[n].astype(BF16) for n in GROUP_LANE] + [_to_rows(_split_bf16x3(sh["ffn_conv_w"]))]
        return [sh[n].astype(BF16) for n in GROUP_ROWS] + [sh["ffn_w_up"].astype(BF16),
                                                           _stack_rows(lane_rows, 0, self.r_lane)]

    def unpack_late(self, gathered):
        *rows, gc, gd = gathered
        sh = self.shard
        W = {"ffn_w_up": _blocks_to_full(gc, "col")}
        for n, blocks in zip(GROUP_ROWS, rows):
            W[n] = _blocks_to_full(blocks, "row")
        r0 = 0
        for n in GROUP_LANE:
            W[n] = _blocks_to_full(gd[:, r0:r0 + sh[n].shape[0]], "col")
            r0 += sh[n].shape[0]
        cw = _from_rows(gd[:, self.n_lane:self.n_lane + self.r_pieces], (3,) + sh["ffn_conv_w"].shape, lead=1).astype(F32)
        W["ffn_conv_w"] = _blocks_to_full(cw[:, 0] + cw[:, 1] + cw[:, 2], "col")
        return W

    def early_grads(self, g):
        cw_rows = _to_rows(_full_to_blocks(g["ffn_conv_w"], "col"), lead=1)
        return [_full_to_blocks(g[n], "row") for n in GROUP_ROWS] + [
            jnp.concatenate([_full_to_blocks(h, "col", N_DEV // 2) for h in g["ffn_w_up"]], axis=0),
            _stack_rows([_full_to_blocks(g[n], "col") for n in GROUP_LANE] + [cw_rows], 1, self.r_lane)]

    def last_grads(self, g):
        return [_unperm_blocks(g["w_in"], N_DEV), _full_to_blocks(g["mem_kv_w"], "row")]

    def pair_sums(self, pks, recv_sib, tag):
        return [_pair_sum_cast(p, r, self.core, "rs_pair_sum_%s%d" % (tag, i))
                for i, (p, r) in enumerate(zip(pks, recv_sib))]

    def final_sums(self, pks, recv_sib, recv_chips, tag):
        return [_final_sum(p, rs, rc, 2 * self.chip + self.core, self.chip, "rs_final_sum_%s%d" % (tag, i))
                for i, (p, rs, rc) in enumerate(zip(pks, recv_sib, recv_chips))]

    def unpack_grads(self, early, last):
        sh = self.shard
        *rows, g_up, g_lane = early
        g_shard = {"w_in": last[0], "mem_kv_w": last[1], "ffn_w_up": g_up, **dict(zip(GROUP_ROWS, rows))}
        r0 = 0
        for n in GROUP_LANE:
            g_shard[n] = g_lane[r0:r0 + sh[n].shape[0]]
            r0 += sh[n].shape[0]
        g_shard["ffn_conv_w"] = _from_rows(g_lane[self.n_lane:self.n_lane + self.r_vals], sh["ffn_conv_w"].shape)
        return g_shard


def kernel(x, mem, norm_mix_g, norm_mem_g, w_in, hgrn_lb_logits, hgrn_norm_g, fox_f_bias, fox_q_norm_g, fox_k_norm_g, mem_kv_w, mem_q_norm_g, mem_k_norm_g, w_br_hgrn, w_br_fox, w_br_mem, w_out, norm_ffn_g, ffn_w_up, ffn_conv_w, ffn_conv_b, ffn_w_down, loss_target, m_norm_mix_g, m_norm_mem_g, m_w_in, m_hgrn_lb_logits, m_hgrn_norm_g, m_fox_f_bias, m_fox_q_norm_g, m_fox_k_norm_g, m_mem_kv_w, m_mem_q_norm_g, m_mem_k_norm_g, m_w_br_hgrn, m_w_br_fox, m_w_br_mem, m_w_out, m_norm_ffn_g, m_ffn_w_up, m_ffn_conv_w, m_ffn_conv_b, m_ffn_w_down, v_norm_mix_g, v_norm_mem_g, v_w_in, v_hgrn_lb_logits, v_hgrn_norm_g, v_fox_f_bias, v_fox_q_norm_g, v_fox_k_norm_g, v_mem_kv_w, v_mem_q_norm_g, v_mem_k_norm_g, v_w_br_hgrn, v_w_br_fox, v_w_br_mem, v_w_out, v_norm_ffn_g, v_ffn_w_up, v_ffn_conv_w, v_ffn_conv_b, v_ffn_w_down):
    given = dict(locals())
    order = ("norm_mix_g", "norm_mem_g", "w_in", "hgrn_lb_logits", "hgrn_norm_g", "fox_f_bias", "fox_q_norm_g",
             "fox_k_norm_g", "mem_kv_w", "mem_q_norm_g", "mem_k_norm_g", "w_br_hgrn", "w_br_fox", "w_br_mem", "w_out",
             "norm_ffn_g", "ffn_w_up", "ffn_conv_w", "ffn_conv_b", "ffn_w_down")
    B, T, D = x.shape
    M = mem.shape[1]
    shard = {n: given[n][0] if n in BIG else given[n] for n in order}
    mom = {n: (given["m_" + n][0], given["v_" + n][0]) if n in BIG else (given["m_" + n], given["v_" + n])
           for n in order}
    shard["hgrn_lb_logits"] = hgrn_lb_logits
    for n in ("norm_mix_g", "norm_mem_g", "hgrn_norm_g", "fox_f_bias", "fox_q_norm_g", "fox_k_norm_g", "mem_q_norm_g",
              "mem_k_norm_g", "norm_ffn_g", "ffn_conv_b"):
        shard[n] = given[n].reshape(1, -1)

    ex = _Exchange(shard)
    sm = {n: shard[n] for n in SMALL}
    grad_x, g, sums = _local_step(x.reshape(B * T, D), mem.reshape(B * M, D), loss_target.reshape(B * T, D), sm, None,
                                  B, T, M, ex)
    g_shard = ex.unpack_grads(*sums)

    sg = {n: g[n] for n in SMALL}
    sg["fox_f_bias"] = g["fox_f_bias"][:, :FOX_H]
    sg["fox_q_norm_g"] = g["fox_q_norm_g"][:, :FOX_D]
    sg["fox_k_norm_g"] = g["fox_k_norm_g"][:, :FOX_D]
    slayout, row0 = {}, 0
    for n in SMALL:
        nr = _rows_of(shard[n].size)
        slayout[n] = (row0, nr)
        row0 += nr
    loss_row = row0
    r_small = _round_up(row0 + 1, 8)

    def pack_small(d, with_loss=None):
        rows = [_to_rows(d[n]) for n in SMALL]
        rows.append(with_loss if with_loss is not None else jnp.zeros((1, LANE), F32))
        return _stack_rows(rows, 0, r_small)

    sgath, = _all_gather([pack_small(sg, g["loss"])], "ag_small")
    s_g, s_d, s_m, s_v = _small_update(sgath, pack_small(shard), pack_small({n: mom[n][0].reshape(shard[n].shape) for n in SMALL}),
                                       pack_small({n: mom[n][1].reshape(shard[n].shape) for n in SMALL}))
    loss = s_g[loss_row, 0]

    grads, deltas, new_m, new_v = {}, {}, {}, {}
    for n in BIG:
        gn = g_shard[n]
        d, nm, nv = _adamw(shard[n], gn, mom[n][0], mom[n][1], "adamw_" + n)
        grads[n], deltas[n], new_m[n], new_v[n] = (a[None] for a in (gn, d, nm, nv))
    for n in SMALL:
        r0, nr = slayout[n]
        for dst, src in ((grads, s_g), (deltas, s_d), (new_m, s_m), (new_v, s_v)):
            dst[n] = _from_rows(src[r0:r0 + nr], given[n].shape)
    return (loss, grad_x.reshape(B, T, D), *[grads[n] for n in order], *[deltas[n] for n in order],
            *[new_m[n] for n in order], *[new_v[n] for n in order])
```

```python
import functools
import math

import jax
import jax.numpy as jnp
from jax import lax
from jax.experimental import pallas as pl
from jax.experimental.pallas import tpu as pltpu

F32, BF16 = jnp.float32, jnp.bfloat16
S = jax.ShapeDtypeStruct
MESH = pl.DeviceIdType.MESH

N_DEV = 8
EPS = 1e-6
LANE = 128
CHUNK = 64
SUB = 16
HG_H, HG_D = 4, 128
HG_GROUP_FWD = 4
HG_GROUP = 2
FOX_H, FOX_D = 8, 64
FOX_P = FOX_H // 2
MEM_H, MEM_D = 4, 128
NEG = -1e30
VMEM_LIMIT = 56 * 2**20

ADAM_LR, ADAM_B1, ADAM_B2, ADAM_EPS, ADAM_WD, ADAM_STEP = 0.001, 0.9, 0.999, 1e-08, 0.01, 10

C_FOX, C_MQ, C_HG, C_GATE, C_FF, C_END = 0, 1536, 2048, 4096, 7168, 7296


def _cp(sem=None):
    return pltpu.CompilerParams(dimension_semantics=sem, vmem_limit_bytes=VMEM_LIMIT)


def _dot(a, b, dims, prec=None):
    return lax.dot_general(a, b, (dims, ((), ())), preferred_element_type=F32, precision=prec)


def _nn(a, b, prec=None):
    return _dot(a, b, ((1,), (0,)), prec)


def _nt(a, b, prec=None):
    return _dot(a, b, ((1,), (1,)), prec)


def _tn(a, b, prec=None):
    return _dot(a, b, ((0,), (0,)), prec)


def _b(x):
    return x.astype(BF16)


def _mm3(fn, a, b):
    ah, bh = _b(a), _b(b)
    return fn(ah, bh) + fn(ah, _b(b - bh.astype(F32))) + fn(_b(a - ah.astype(F32)), bh)


def _iota(shape, dim):
    return lax.broadcasted_iota(jnp.int32, shape, dim)


def _rowsum8(x):
    r, d = x.shape
    return jnp.sum(x.reshape(r // 8, 8, d), axis=0)


def _rmsnorm_cast(x, g, name, tm=1024, gather=()):
    n, d = x.shape
    nga = len(gather)

    def body(x_ref, g_ref, o_ref):
        v = x_ref[...]
        r = lax.rsqrt(jnp.mean(v * v, axis=-1, keepdims=True) + EPS)
        o_ref[...] = (v * r * g_ref[...]).astype(BF16)

    if nga:
        body = _hosting(body, 2, 1, 0, nga, _gather_phases, (n // tm,))
    out = pl.pallas_call(
        body, name=name, grid=(n // tm,),
        in_specs=[pl.BlockSpec((tm, d), lambda i: (i, 0)), pl.BlockSpec((1, d), lambda i: (0, 0))] + [ANY] * nga,
        out_specs=[pl.BlockSpec((tm, d), lambda i: (i, 0))] + [ANY] * nga,
        out_shape=[S((n, d), BF16)] + _gather_shapes(gather), scratch_shapes=_gather_sems(nga) if nga else [],
        compiler_params=_cp(("arbitrary",) if nga else ("parallel",)),
    )(x, g, *gather)
    return out if nga else out[0]


def _rmsnorm_bwd(dh, x, g, resid, name, tm=1024):
    n, d = x.shape
    has_res = resid is not None

    def body(*refs):
        if has_res:
            dh_ref, x_ref, g_ref, r_ref, dx_ref, dg_ref = refs
        else:
            dh_ref, x_ref, g_ref, dx_ref, dg_ref = refs
        v = x_ref[...]
        dhv = dh_ref[...].astype(F32)
        r = lax.rsqrt(jnp.mean(v * v, axis=-1, keepdims=True) + EPS)
        xh = v * r
        u = dhv * g_ref[...]
        dx = r * (u - xh * jnp.mean(u * xh, axis=-1, keepdims=True))
        if has_res:
            dx = dx + r_ref[...]
        dx_ref[...] = dx

        @pl.when(pl.program_id(0) == 0)
        def _():
            dg_ref[...] = jnp.zeros_like(dg_ref)

        dg_ref[...] += _rowsum8(dhv * xh)

    tile = pl.BlockSpec((tm, d), lambda i: (i, 0))
    ins = [tile, tile, pl.BlockSpec((1, d), lambda i: (0, 0))] + ([tile] if has_res else [])
    args = (dh, x, g) + ((resid,) if has_res else ())
    return pl.pallas_call(
        body, name=name, grid=(n // tm,), in_specs=ins,
        out_specs=[tile, pl.BlockSpec((8, d), lambda i: (0, 0))],
        out_shape=[S((n, d), F32), S((8, d), F32)], compiler_params=_cp(("arbitrary",)),
    )(*args)


def _mm_nn(a, b, out_dtype, name, tm, tn):
    m, k = a.shape
    n = b.shape[1]
    assert n % tn == 0 and m % tm == 0

    def body(a_ref, b_ref, o_ref):
        o_ref[...] = _nn(a_ref[...].astype(BF16), b_ref[...].astype(BF16)).astype(out_dtype)

    return pl.pallas_call(
        body, name=name, grid=(n // tn, m // tm),
        in_specs=[pl.BlockSpec((tm, k), lambda j, i: (i, 0)), pl.BlockSpec((k, tn), lambda j, i: (0, j))],
        out_specs=pl.BlockSpec((tm, tn), lambda j, i: (i, j)), out_shape=S((m, n), out_dtype),
        compiler_params=_cp(("parallel", "parallel")),
    )(a, b)


def _mm_nt_sum(parts, w, name, tm, swap=()):
    m = parts[0][0].shape[0]
    k = w.shape[0]
    assert m % tm == 0 and all(c % n == 0 and o % n == 0 for _, c, n, o in parts)
    np_ = len(parts)
    nsw = len(swap)
    n_steps = m // tm

    def body(*refs):
        o_ref = refs[2 * np_ + nsw]
        if nsw:
            start, finish = _chip_swap_phases(refs[2 * np_:2 * np_ + nsw], refs[2 * np_ + nsw + 1:2 * np_ + 2 * nsw + 1],
                                              *refs[2 * np_ + 2 * nsw + 1:])
            pl.when(pl.program_id(0) == 0)(start)
        acc = _nt(refs[0][...].astype(BF16), refs[np_][...].astype(BF16))
        for i in range(1, np_):
            acc = acc + _nt(refs[i][...].astype(BF16), refs[np_ + i][...].astype(BF16))
        o_ref[...] = acc
        if nsw:
            pl.when(pl.program_id(0) == n_steps - 1)(finish)

    dy_specs = [pl.BlockSpec((tm, n), functools.partial(lambda i, j: (i, j), j=c // n)) for _, c, n, _ in parts]
    w_specs = [pl.BlockSpec((k, n), functools.partial(lambda i, j: (0, j), j=o // n)) for _, _, n, o in parts]
    out = pl.pallas_call(
        body, name=name, grid=(n_steps,), in_specs=dy_specs + w_specs + [ANY] * nsw,
        out_specs=[pl.BlockSpec((tm, k), lambda i: (i, 0))] + [ANY] * nsw,
        out_shape=[S((m, k), F32)] + [S(p.shape, p.dtype) for p in swap],
        scratch_shapes=_chip_swap_sems(nsw) if nsw else [],
        compiler_params=_cp(("arbitrary",) if nsw else ("parallel",)),
    )(*([p[0] for p in parts] + [w] * np_ + list(swap)))
    return (out[0], out[1:]) if nsw else out[0]


MM_TN_PIECE = 3072


def _mm_tn(x, dy, name, tm, tn):
    m, k = x.shape
    n = dy.shape[1]
    tm = min(tm, m)
    assert m % tm == 0 and n % tn == 0

    cuts = list(range(0, tn, MM_TN_PIECE)) + [tn]

    def body(x_ref, dy_ref, o_ref):
        xb = x_ref[...].astype(BF16)
        for c0, c1 in zip(cuts[:-1], cuts[1:]):
            part = _tn(xb, dy_ref[:, c0:c1].astype(BF16))

            @pl.when(pl.program_id(1) == 0)
            def _():
                o_ref[:, c0:c1] = part

            @pl.when(pl.program_id(1) > 0)
            def _():
                o_ref[:, c0:c1] += part

    out_mode = dict(pipeline_mode=pl.Buffered(1)) if tn == n else {}
    return pl.pallas_call(
        body, name=name, grid=(n // tn, m // tm),
        in_specs=[pl.BlockSpec((tm, k), lambda j, i: (i, 0)), pl.BlockSpec((tm, tn), lambda j, i: (i, j))],
        out_specs=pl.BlockSpec((k, tn), lambda j, i: (0, j), **out_mode), out_shape=S((k, n), F32),
        compiler_params=_cp(("parallel", "arbitrary")),
    )(x, dy)


def _lower_bound(logits):
    e = jnp.exp(logits - jnp.max(logits, axis=0, keepdims=True))
    return e[0:1, :] / jnp.sum(e, axis=0, keepdims=True)


def _hg_gates(fl, lb):
    sig = jax.nn.sigmoid(fl)
    f = lb + (1.0 - lb) * sig
    k = (1.0 - lb) * (1.0 - sig)
    return sig, f, k, jnp.log(f)


def _silu_and_grad(x):
    s = jax.nn.sigmoid(x)
    return x * s, s * (1.0 + x * (1.0 - s))


def _hg_rowblocks(G):
    return [None] + [G[SUB * i - 1:SUB * i, :] for i in range(1, CHUNK // SUB)]


def _hg_intra_A(qs, k, G):
    refs = _hg_rowblocks(G)
    cols = _iota((SUB, LANE), 1)
    rows = _iota((SUB, LANE), 0)
    no_keys = jnp.zeros((LANE - CHUNK, HG_D), BF16)
    blocks = []
    for i in range(CHUNK // SUB):
        lo = SUB * i
        qb, Gb = qs[lo:lo + SUB, :], G[lo:lo + SUB, :]
        diag = jnp.zeros((SUB, LANE), F32)
        for s in range(SUB):
            e = jnp.exp(jnp.minimum(Gb - G[lo + s:lo + s + 1, :], 0.0))
            col = jnp.sum(qb * k[lo + s:lo + s + 1, :] * e, axis=-1, keepdims=True)
            diag = jnp.where(cols == lo + s, col, diag)
        a = jnp.where((cols >= lo) & (cols <= rows + lo), diag, 0.0)
        if i > 0:
            qr = qb * jnp.exp(Gb - refs[i])
            kr = k * jnp.exp(jnp.minimum(refs[i] - G, 0.0))
            a = jnp.where(cols < lo, _nt(_b(qr), jnp.concatenate([_b(kr), no_keys], axis=0)), a)
        blocks.append(a)
    return jnp.concatenate(blocks, axis=0)


def _hg_intra_bwd(dA, qs, k, G):
    refs = _hg_rowblocks(G)
    cols = _iota((SUB, CHUNK), 1)
    rows16 = _iota((SUB, HG_D), 0)
    dk = jnp.zeros((CHUNK, HG_D), F32)
    dq_blocks, dk_diag_blocks = [], []
    for i in range(CHUNK // SUB):
        lo = SUB * i
        qb, Gb = qs[lo:lo + SUB, :], G[lo:lo + SUB, :]
        dAb = dA[lo:lo + SUB, :]
        dq = jnp.zeros((SUB, HG_D), F32)
        dkb = jnp.zeros((SUB, HG_D), F32)
        for s in range(SUB):
            e = jnp.exp(jnp.minimum(Gb - G[lo + s:lo + s + 1, :], 0.0))
            e = jnp.where(rows16 >= s, e, 0.0)
            dcol = jnp.sum(jnp.where(cols == lo + s, dAb, 0.0), axis=-1, keepdims=True)
            w = dcol * e
            dq = dq + w * k[lo + s:lo + s + 1, :]
            dkb = jnp.where(rows16 == s, jnp.sum(w * qb, axis=0, keepdims=True), dkb)
        if i > 0:
            e1 = jnp.exp(Gb - refs[i])
            e2 = jnp.exp(jnp.minimum(refs[i] - G, 0.0))
            dA_off = jnp.where(cols < lo, dAb, 0.0)
            dq = dq + _mm3(_nn, dA_off, k * e2) * e1
            dk = dk + _mm3(_tn, dA_off, qb * e1) * e2
        dq_blocks.append(dq)
        dk_diag_blocks.append(dkb)
    return jnp.concatenate(dq_blocks, axis=0), dk + jnp.concatenate(dk_diag_blocks, axis=0)


def _tri(n, upper=False):
    r, c = _iota((n, n), 0), _iota((n, n), 1)
    return jnp.where((c >= r) if upper else (r >= c), 1.0, 0.0).astype(BF16)


def _prefix_mm(tri, x):
    hi = x.astype(BF16)
    r1 = x - hi.astype(F32)
    mid = r1.astype(BF16)
    lo = (r1 - mid.astype(F32)).astype(BF16)
    return _nn(tri, hi) + _nn(tri, mid) + _nn(tri, lo)


def _hgrn_fwd(z, lb, gn, B, T):
    N = B * T
    NC = T // CHUNK
    ng = HG_H // HG_GROUP_FWD

    def body(z_ref, lb_ref, gn_ref, y_ref, o_ref, st_ref, a_ref, s_scr):
        lbs = _lower_bound(lb_ref[...])
        tri = _tri(CHUNK)
        s_scr[...] = jnp.zeros_like(s_scr)

        def chunk(c, carry):
            r = pl.ds(pl.multiple_of(c * CHUNK, CHUNK), CHUNK)
            for hh in range(HG_GROUP_FWD):
                zc, oc = 4 * LANE * hh, LANE * hh
                ql, fl, il, gl = (z_ref[r, zc + LANE * j:zc + LANE * (j + 1)].astype(F32) for j in range(4))
                _, _, k, logf = _hg_gates(fl, lbs[:, oc:oc + LANE])
                G = _prefix_mm(tri, logf)
                qs = ql * jax.nn.sigmoid(ql)
                st = s_scr[hh]
                st_ref[hh * NC + c] = st
                g_last = G[CHUNK - 1:CHUNK, :]
                A = _b(_hg_intra_A(qs, k, G))
                a_ref[r, oc:oc + LANE] = A
                o = _nn(A[:, 0:CHUNK], _b(il)) + _nt(_b(qs * jnp.exp(G)), _b(st))
                s_scr[hh] = st * jnp.exp(g_last) + _mm3(_tn, il, k * jnp.exp(g_last - G))
                o_ref[r, oc:oc + LANE] = o
                rstd = lax.rsqrt(jnp.mean(o * o, axis=-1, keepdims=True) + EPS)
                y_ref[r, oc:oc + LANE] = (o * rstd * gn_ref[...] * (gl * jax.nn.sigmoid(gl))).astype(BF16)
            return carry

        lax.fori_loop(0, NC, chunk, 0, unroll=4)

    gw = HG_GROUP_FWD * LANE
    cb = C_HG // (4 * gw)
    return pl.pallas_call(
        body, name="hgrn_fwd", grid=(B, ng),
        in_specs=[pl.BlockSpec((T, 4 * gw), lambda b, h: (b, cb + h)), pl.BlockSpec((lb.shape[0], gw), lambda b, h: (0, h)),
                  pl.BlockSpec((1, LANE), lambda b, h: (0, 0))],
        out_specs=[pl.BlockSpec((T, gw), lambda b, h: (b, h)), pl.BlockSpec((T, gw), lambda b, h: (b, h)),
                   pl.BlockSpec((HG_GROUP_FWD * NC, HG_D, HG_D), lambda b, h: (b * ng + h, 0, 0)),
                   pl.BlockSpec((T, gw), lambda b, h: (b, h))],
        out_shape=[S((N, 512), BF16), S((N, 512), F32), S((B * HG_H * NC, HG_D, HG_D), F32), S((N, 512), BF16)],
        scratch_shapes=[pltpu.VMEM((HG_GROUP_FWD, HG_D, HG_D), F32)], compiler_params=_cp(("parallel", "parallel")),
    )(z, lb, gn)


def _hgrn_bwd(z, o_raw, states, a_mat, dy, lb, gn, B, T, swap_sibling=()):
    N = B * T
    NC = T // CHUNK
    ng = HG_H // HG_GROUP
    nsw = len(swap_sibling)

    def body(z_ref, o_ref, st_ref, a_ref, dy_ref, lb_ref, gn_ref, dz_ref, dlb_ref, dgn_ref, ds_scr, racc, dgn_acc):
        lbs = _lower_bound(lb_ref[...])
        gn_v = gn_ref[...]
        tri, triu = _tri(CHUNK), _tri(CHUNK, upper=True)
        cmask = _iota((CHUNK, CHUNK), 0) >= _iota((CHUNK, CHUNK), 1)
        for ref in (ds_scr, racc, dgn_acc, dlb_ref):
            ref[...] = jnp.zeros_like(ref)

        def chunk(ci, carry):
            c = NC - 1 - ci
            r = pl.ds(pl.multiple_of(c * CHUNK, CHUNK), CHUNK)
            for hh in range(HG_GROUP):
                zc, oc = 4 * LANE * hh, LANE * hh
                lb_v = lbs[:, oc:oc + LANE]
                ql, fl, il, gl = (z_ref[r, zc + LANE * j:zc + LANE * (j + 1)].astype(F32) for j in range(4))
                sig, f, k, logf = _hg_gates(fl, lb_v)
                G = _prefix_mm(tri, logf)
                qs, dsilu_q = _silu_and_grad(ql)
                gs, dsilu_g = _silu_and_grad(gl)
                o = o_ref[r, oc:oc + LANE]
                dyv = dy_ref[r, oc:oc + LANE]
                rstd = lax.rsqrt(jnp.mean(o * o, axis=-1, keepdims=True) + EPS)
                oh = o * rstd
                dgl = dyv * oh * gn_v * dsilu_g
                dn = dyv * gs
                dgn_acc[...] += _rowsum8(dn * oh)
                u = dn * gn_v
                do = rstd * (u - oh * jnp.mean(u * oh, axis=-1, keepdims=True))
                st = st_ref[hh * NC + c]
                dst = ds_scr[hh]
                eG = jnp.exp(G)
                g_last = G[CHUNK - 1:CHUNK, :]
                eL = jnp.exp(g_last - G)
                dA = jnp.where(cmask, _mm3(_nt, do, il), 0.0)
                dq_in, dk_in = _hg_intra_bwd(dA, qs, k, G)
                di = _tn(a_ref[r, oc:oc + LANE][:, 0:CHUNK], _b(do)) + _nt(_b(k * eL), _b(dst))
                dq = dq_in + _mm3(_nn, do, st) * eG
                dk = dk_in + _mm3(_nn, il, dst) * eL
                ds_scr[hh] = dst * jnp.exp(g_last) + _mm3(_tn, do, qs * eG)
                dd = qs * dq - k * dk
                dlogf = _prefix_mm(triu, dd) + racc[hh]
                racc[hh] += jnp.sum(dd, axis=0, keepdims=True)
                df = dlogf / f - dk
                dlb_ref[8 * hh:8 * (hh + 1), :] += _rowsum8(df * (1.0 - sig))
                dz_ref[r, zc:zc + LANE] = (dq * dsilu_q).astype(BF16)
                dz_ref[r, zc + LANE:zc + 2 * LANE] = (df * (1.0 - lb_v) * sig * (1.0 - sig)).astype(BF16)
                dz_ref[r, zc + 2 * LANE:zc + 3 * LANE] = di.astype(BF16)
                dz_ref[r, zc + 3 * LANE:zc + 4 * LANE] = dgl.astype(BF16)
            return carry

        lax.fori_loop(0, NC, chunk, 0, unroll=4)
        dgn_ref[...] = dgn_acc[...]

    gw = HG_GROUP * LANE
    cb = C_HG // (4 * gw)
    col = pl.BlockSpec((T, gw), lambda b, h: (b, h))
    if nsw:
        body = _hosting(body, 7, 3, 3, nsw, _sibling_swap_phases, (B, ng))
    return pl.pallas_call(
        body, name="hgrn_bwd", grid=(B, ng),
        in_specs=[pl.BlockSpec((T, 4 * gw), lambda b, h: (b, cb + h)), col,
                  pl.BlockSpec((HG_GROUP * NC, HG_D, HG_D), lambda b, h: (b * ng + h, 0, 0)), col, col,
                  pl.BlockSpec((lb.shape[0], gw), lambda b, h: (0, h)), pl.BlockSpec((1, LANE), lambda b, h: (0, 0))]
        + [ANY] * nsw,
        out_specs=[pl.BlockSpec((T, 4 * gw), lambda b, h: (b, h)),
                   pl.BlockSpec((8 * HG_GROUP, LANE), lambda b, h: (b * ng + h, 0)),
                   pl.BlockSpec((8, LANE), lambda b, h: (b * ng + h, 0))] + [ANY] * nsw,
        out_shape=[S((N, 2048), BF16), S((B * HG_H * 8, LANE), F32), S((B * ng * 8, LANE), F32)]
        + _sibling_swap_shapes(swap_sibling),
        scratch_shapes=[pltpu.VMEM((HG_GROUP, HG_D, HG_D), F32), pltpu.VMEM((HG_GROUP, 1, LANE), F32),
                        pltpu.VMEM((8, LANE), F32)] + (_sibling_swap_sems(nsw) if nsw else []),
        compiler_params=_cp(("arbitrary", "arbitrary") if nsw else ("parallel", "parallel")),
    )(z, o_raw, states, a_mat, dy, lb, gn, *swap_sibling)


def _pair_mean(x, lo_half):
    a = jnp.sum(jnp.where(lo_half, x, 0.0), axis=-1, keepdims=True)
    b = jnp.sum(jnp.where(lo_half, 0.0, x), axis=-1, keepdims=True)
    return jnp.where(lo_half, a, b) * (1.0 / FOX_D)


def _fox_gate_fwd(z, bias, B, T):
    N = B * T
    tb = LANE

    def body(z_ref, b_ref, fc_ref, fct_ref):
        tri = _tri(tb)

        def step(i, carry):
            r = pl.ds(pl.multiple_of(i * tb, tb), tb)
            cs = _prefix_mm(tri, jax.nn.log_sigmoid(z_ref[r, :].astype(F32) + b_ref[...])) + carry
            fc_ref[r, :] = cs
            fct_ref[0, :, r] = cs.T[0:8, :]
            return cs[tb - 1:tb, :]

        lax.fori_loop(0, T // tb, step, jnp.zeros((1, LANE), F32))

    return pl.pallas_call(
        body, name="fox_gate_fwd", grid=(B,),
        in_specs=[pl.BlockSpec((T, LANE), lambda b: (b, C_FF // LANE)), pl.BlockSpec((1, LANE), lambda b: (0, 0))],
        out_specs=[pl.BlockSpec((T, LANE), lambda b: (b, 0)), pl.BlockSpec((1, 8, T), lambda b: (b, 0, 0))],
        out_shape=[S((N, LANE), F32), S((B, 8, T), F32)], compiler_params=_cp(("parallel",)),
    )(z, bias)


def _fox_gate_bwd(dfc, z, bias, B, T):
    N = B * T
    tb = LANE
    nt = T // tb

    def body(d_ref, z_ref, b_ref, dz_ref, db_ref):
        triu = _tri(tb, upper=True)
        db_ref[...] = jnp.zeros_like(db_ref)

        def step(ii, carry):
            r = pl.ds(pl.multiple_of((nt - 1 - ii) * tb, tb), tb)
            d = d_ref[r, 0:LANE]
            for p in range(1, FOX_P):
                d = d + d_ref[r, LANE * p:LANE * (p + 1)]
            rc = _prefix_mm(triu, d) + carry
            dff = rc * jax.nn.sigmoid(-(z_ref[r, :].astype(F32) + b_ref[...]))
            dz_ref[r, :] = dff.astype(BF16)
            db_ref[...] += _rowsum8(dff)
            return carry + jnp.sum(d, axis=0, keepdims=True)

        lax.fori_loop(0, nt, step, jnp.zeros((1, LANE), F32))

    return pl.pallas_call(
        body, name="fox_gate_bwd", grid=(B,),
        in_specs=[pl.BlockSpec((T, 512), lambda b: (b, 0)), pl.BlockSpec((T, LANE), lambda b: (b, C_FF // LANE)),
                  pl.BlockSpec((1, LANE), lambda b: (0, 0))],
        out_specs=[pl.BlockSpec((T, LANE), lambda b: (b, 0)), pl.BlockSpec((8, LANE), lambda b: (b, 0))],
        out_shape=[S((N, LANE), BF16), S((B * 8, LANE), F32)], compiler_params=_cp(("parallel",)),
    )(dfc, z, bias)


def _fox_prep(z_ref, gq, gk, r, lo_half):
    q, k, v = (z_ref[r, LANE * j:LANE * (j + 1)].astype(F32) for j in range(3))
    rq = lax.rsqrt(_pair_mean(q * q, lo_half) + EPS)
    rk = lax.rsqrt(_pair_mean(k * k, lo_half) + EPS)
    qh, kh = q * rq, k * rk
    return qh * gq * (FOX_D ** -0.5), kh * gk, v, qh, kh, rq, rk


def _fox_fwd(z, fc, fct, gq, gk, B, T, tq=512, gather=()):
    N = B * T
    NQ = T // tq
    nga = len(gather)

    def body(z_ref, fc_ref, fct_ref, gq_ref, gk_ref, y_ref, lse_ref, qn_s, kn_s, v_s):
        p, qi = pl.program_id(1), pl.program_id(2)
        lo_half = _iota((1, LANE), 1) < FOX_D

        @pl.when(qi == 0)
        def _():
            def prep(i, carry):
                r = pl.ds(pl.multiple_of(i * tq, tq), tq)
                qn, kn, v = _fox_prep(z_ref, gq_ref[...], gk_ref[...], r, lo_half)[:3]
                qn_s[r, :], kn_s[r, :], v_s[r, :] = qn.astype(BF16), kn.astype(BF16), v.astype(BF16)
                return carry
            lax.fori_loop(0, NQ, prep, 0)

        rq = pl.ds(pl.multiple_of(qi * tq, tq), tq)
        qn = qn_s[rq, :]
        fcq = fc_ref[rq, :]
        lane = _iota((tq, LANE), 1)
        causal = _iota((tq, tq), 0) >= _iota((tq, tq), 1)
        qhs = [jnp.where(lo_half, qn, jnp.zeros_like(qn)), jnp.where(lo_half, jnp.zeros_like(qn), qn)]
        fqs = [jnp.sum(jnp.where(lane == 2 * p + hh, fcq, 0.0), axis=-1, keepdims=True) for hh in range(2)]

        def kv(j, carry, diagonal):
            rk = pl.ds(pl.multiple_of(j * tq, tq), tq)
            kj, vj = kn_s[rk, :], v_s[rk, :]
            one = jnp.ones_like(vj)
            new = []
            for hh in range(2):
                m, acc = carry[hh]
                s = _nt(qhs[hh], kj) + fqs[hh] - fct_ref[0, pl.ds(2 * p + hh, 1), rk]
                if diagonal:
                    s = jnp.where(causal, s, NEG)
                m_new = jnp.maximum(m, jnp.max(s, axis=-1, keepdims=True))
                pe = jnp.exp(s - m_new)
                v_aug = jnp.where(lo_half if hh == 0 else jnp.logical_not(lo_half), vj, one)
                new.append((m_new, jnp.exp(m - m_new) * acc + _nn(pe.astype(BF16), v_aug)))
            return tuple(new)

        init = tuple((jnp.full((tq, 1), NEG, F32), jnp.zeros((tq, LANE), F32)) for _ in range(2))
        carry = lax.fori_loop(0, qi, functools.partial(kv, diagonal=False), init)
        (m0, a0), (m1, a1) = kv(qi, carry, True)
        l0, l1 = a0[:, FOX_D:FOX_D + 1], a1[:, 0:1]
        y_ref[...] = jnp.where(lo_half, a0 / l0, a1 / l1).astype(BF16)
        lse_ref[...] = jnp.where(lo_half, m0 + jnp.log(l0), m1 + jnp.log(l1))

    vec = pl.BlockSpec((1, LANE), lambda b, p, q: (0, 0))
    tile = pl.BlockSpec((tq, LANE), lambda b, p, q: (b * NQ + q, p))
    if nga:
        body = _hosting(body, 5, 2, 3, nga, _gather_phases, (B, FOX_P, NQ))
    return pl.pallas_call(
        body, name="fox_fwd", grid=(B, FOX_P, NQ),
        in_specs=[pl.BlockSpec((T, 384), lambda b, p, q: (b, p)), pl.BlockSpec((T, LANE), lambda b, p, q: (b, 0)),
                  pl.BlockSpec((1, 8, T), lambda b, p, q: (b, 0, 0)), vec, vec] + [ANY] * nga,
        out_specs=[tile, tile] + [ANY] * nga, out_shape=[S((N, 512), BF16), S((N, 512), F32)] + _gather_shapes(gather),
        scratch_shapes=[pltpu.VMEM((T, LANE), BF16)] * 3 + (_gather_sems(nga) if nga else []),
        compiler_params=_cp(("arbitrary",) * 3 if nga else ("parallel", "parallel", "arbitrary")),
    )(z, fc, fct, gq, gk, *gather)


def _fox_bwd(z, dy, y, lse, fc, fct, gq, gk, B, T, tq=512, swap=()):
    N = B * T
    NQ = T // tq
    nsw = len(swap)

    def body(z_ref, dy_ref, y_ref, lse_ref, fc_ref, fct_ref, gq_ref, gk_ref, dz_ref, dfc_ref, dgq_ref, dgk_ref,
             qn_s, kn_s, v_s, do_s, delta_s, dq_s, dfk_s):
        p, kj = pl.program_id(1), pl.program_id(2)
        lo_half = _iota((1, LANE), 1) < FOX_D
        lane = _iota((tq, LANE), 1)
        gq_v, gk_v = gq_ref[...], gk_ref[...]

        @pl.when(kj == 0)
        def _():
            def prep(i, carry):
                r = pl.ds(pl.multiple_of(i * tq, tq), tq)
                qn, kn, v = _fox_prep(z_ref, gq_v, gk_v, r, lo_half)[:3]
                qn_s[r, :], kn_s[r, :], v_s[r, :] = qn.astype(BF16), kn.astype(BF16), v.astype(BF16)
                do = dy_ref[r, :]
                do_s[r, :] = do.astype(BF16)
                delta_s[r, :] = _pair_mean(do * y_ref[r, :].astype(F32), lo_half) * float(FOX_D)
                return carry
            lax.fori_loop(0, NQ, prep, 0)
            dq_s[...] = jnp.zeros_like(dq_s)
            dgq_ref[...] = jnp.zeros_like(dgq_ref)
            dgk_ref[...] = jnp.zeros_like(dgk_ref)

        rk = pl.ds(pl.multiple_of(kj * tq, tq), tq)
        kn, vv = kn_s[rk, :], v_s[rk, :]
        causal = _iota((tq, tq), 0) >= _iota((tq, tq), 1)
        zero, one = jnp.zeros_like(kn), jnp.ones_like(kn)
        hms = [lo_half, jnp.logical_not(lo_half)]
        kmasks = [jnp.where(hm, kn, zero) for hm in hms]
        kaugs = [jnp.where(hm, kn, one) for hm in hms]
        vmasks = [jnp.where(hm, vv, zero) for hm in hms]
        fks = [fct_ref[0, pl.ds(2 * p + hh, 1), rk] for hh in range(2)]

        def qloop(i, carry, diagonal):
            ri = pl.ds(pl.multiple_of(i * tq, tq), tq)
            qn = qn_s[ri, :]
            do = do_s[ri, :]
            fcq = fc_ref[ri, :]
            new = []
            for hh in range(2):
                dk_acc, dv_acc = carry[hh]
                c0 = FOX_D * hh
                fq = jnp.sum(jnp.where(lane == 2 * p + hh, fcq, 0.0), axis=-1, keepdims=True)
                pr = jnp.exp(_nt(qn, kmasks[hh]) + fq - fks[hh] - lse_ref[ri, c0:c0 + 1])
                if diagonal:
                    pr = jnp.where(causal, pr, 0.0)
                ds = (pr * (_nt(do, vmasks[hh]) - delta_s[ri, c0:c0 + 1])).astype(BF16)
                dq_s[hh, ri, :] += _nn(ds, kaugs[hh])
                new.append((dk_acc + _tn(jnp.where(hms[hh], qn, one), ds), dv_acc + _tn(do, pr.astype(BF16))))
            return tuple(new)

        init = tuple((jnp.zeros((LANE, tq), F32), jnp.zeros((LANE, tq), F32)) for _ in range(2))
        carry = qloop(kj, init, True)
        (dk0, dv0), (dk1, dv1) = lax.fori_loop(kj + 1, NQ, functools.partial(qloop, diagonal=False), carry)
        dks, dvs = [dk0.T, dk1.T], [dv0.T, dv1.T]

        dkn = jnp.where(lo_half, dks[0], dks[1])
        _, _, _, _, kh, _, rkk = _fox_prep(z_ref, gq_v, gk_v, rk, lo_half)
        u = dkn * gk_v
        dz_ref[rk, LANE:2 * LANE] = (rkk * (u - kh * _pair_mean(u * kh, lo_half))).astype(BF16)
        dz_ref[rk, 2 * LANE:3 * LANE] = jnp.where(lo_half, dvs[0], dvs[1]).astype(BF16)
        dgk_ref[...] += _rowsum8(dkn * kh)
        dfk_s[rk, :] = jnp.where(lane == 2 * p, -dks[0][:, FOX_D:FOX_D + 1],
                                 jnp.where(lane == 2 * p + 1, -dks[1][:, 0:1], 0.0))

        @pl.when(kj == NQ - 1)
        def _():
            def fin(i, carry):
                r = pl.ds(pl.multiple_of(i * tq, tq), tq)
                d0, d1 = dq_s[0, r, :], dq_s[1, r, :]
                dqn = jnp.where(lo_half, d0, d1)
                _, _, _, qh, _, rqq, _ = _fox_prep(z_ref, gq_v, gk_v, r, lo_half)
                u = dqn * gq_v * (FOX_D ** -0.5)
                dz_ref[r, 0:LANE] = (rqq * (u - qh * _pair_mean(u * qh, lo_half))).astype(BF16)
                dgq_ref[...] += _rowsum8(dqn * qh) * (FOX_D ** -0.5)
                dfc_ref[r, :] = dfk_s[r, :] + jnp.where(lane == 2 * p, d0[:, FOX_D:FOX_D + 1],
                                                        jnp.where(lane == 2 * p + 1, d1[:, 0:1], 0.0))
                return carry
            lax.fori_loop(0, NQ, fin, 0)

    vec = pl.BlockSpec((1, LANE), lambda b, p, k: (0, 0))
    col = pl.BlockSpec((T, LANE), lambda b, p, k: (b, p))
    part = pl.BlockSpec((8, LANE), lambda b, p, k: (b * FOX_P + p, 0))
    if nsw:
        body = _hosting(body, 8, 4, 7, nsw, _chip_swap_phases, (B, FOX_P, NQ))
    return pl.pallas_call(
        body, name="fox_bwd", grid=(B, FOX_P, NQ),
        in_specs=[pl.BlockSpec((T, 384), lambda b, p, k: (b, p)), col, col, col,
                  pl.BlockSpec((T, LANE), lambda b, p, k: (b, 0)), pl.BlockSpec((1, 8, T), lambda b, p, k: (b, 0, 0)),
                  vec, vec] + [ANY] * nsw,
        out_specs=[pl.BlockSpec((T, 384), lambda b, p, k: (b, p)), col, part, part] + [ANY] * nsw,
        out_shape=[S((N, 1536), BF16), S((N, 512), F32), S((B * FOX_P * 8, LANE), F32), S((B * FOX_P * 8, LANE), F32)]
        + [S(p.shape, p.dtype) for p in swap],
        scratch_shapes=[pltpu.VMEM((T, LANE), BF16)] * 4 + [pltpu.VMEM((T, LANE), F32), pltpu.VMEM((2, T, LANE), F32),
                                                            pltpu.VMEM((T, LANE), F32)]
        + (_chip_swap_sems(nsw) if nsw else []),
        compiler_params=_cp(("arbitrary",) * 3 if nsw else ("parallel", "parallel", "arbitrary")),
    )(z, dy, y, lse, fc, fct, gq, gk, *swap)


def _mem_scores(z_ref, kv_ref, gq, gk, h):
    c = slice(MEM_D * h, MEM_D * (h + 1))
    q, k = z_ref[:, c].astype(F32), kv_ref[:, c]
    rq = lax.rsqrt(jnp.mean(q * q, axis=-1, keepdims=True) + EPS)
    rk = lax.rsqrt(jnp.mean(k * k, axis=-1, keepdims=True) + EPS)
    qh, kh = q * rq, k * rk
    qn = (qh * gq * (MEM_D ** -0.5)).astype(BF16)
    kn = (kh * gk).astype(BF16)
    s = _nt(qn, kn)
    pe = jnp.exp(s - jnp.max(s, axis=-1, keepdims=True))
    pn = pe / jnp.sum(pe, axis=-1, keepdims=True)
    return pn, qn, kn, qh, kh, rq, rk


def _mem_fwd(z, memkv, gq, gk, B, T, M, tq=1024):
    N = B * T
    tq = min(tq, T)
    NQ = T // tq
    W = MEM_H * MEM_D

    def body(z_ref, kv_ref, gq_ref, gk_ref, y_ref):
        for h in range(MEM_H):
            pn = _mem_scores(z_ref, kv_ref, gq_ref[...], gk_ref[...], h)[0]
            v = kv_ref[:, W + MEM_D * h:W + MEM_D * (h + 1)].astype(BF16)
            y_ref[:, MEM_D * h:MEM_D * (h + 1)] = _nn(pn.astype(BF16), v).astype(BF16)

    vec = pl.BlockSpec((1, LANE), lambda b, q: (0, 0))
    return pl.pallas_call(
        body, name="mem_fwd", grid=(B, NQ),
        in_specs=[pl.BlockSpec((tq, W), lambda b, q: (b * NQ + q, C_MQ // W)),
                  pl.BlockSpec((M, 2 * W), lambda b, q: (b, 0)), vec, vec],
        out_specs=pl.BlockSpec((tq, W), lambda b, q: (b * NQ + q, 0)), out_shape=S((N, W), BF16),
        compiler_params=_cp(("parallel", "parallel")),
    )(z, memkv, gq, gk)


def _mem_bwd(z, memkv, dy, gq, gk, B, T, M, tq=1024):
    N = B * T
    tq = min(tq, T)
    NQ = T // tq
    W = MEM_H * MEM_D

    def body(z_ref, kv_ref, dy_ref, gq_ref, gk_ref, dz_ref, dkv_ref, dgq_ref, dgk_ref, acc):
        qi = pl.program_id(1)
        gq_v, gk_v = gq_ref[...], gk_ref[...]

        @pl.when(qi == 0)
        def _():
            acc[...] = jnp.zeros_like(acc)
            dgq_ref[...] = jnp.zeros_like(dgq_ref)
            dgk_ref[...] = jnp.zeros_like(dgk_ref)

        for h in range(MEM_H):
            c = slice(MEM_D * h, MEM_D * (h + 1))
            cv = slice(W + MEM_D * h, W + MEM_D * (h + 1))
            pn, qn, kn, qh, _, rq, _ = _mem_scores(z_ref, kv_ref, gq_v, gk_v, h)
            do = dy_ref[:, c].astype(BF16)
            dp = _nt(do, kv_ref[:, cv].astype(BF16))
            ds = (pn * (dp - jnp.sum(dp * pn, axis=-1, keepdims=True))).astype(BF16)
            dqn = _nn(ds, kn)
            acc[:, c] += _tn(ds, qn)
            acc[:, cv] += _tn(pn.astype(BF16), do)
            u = dqn * gq_v * (MEM_D ** -0.5)
            dz_ref[:, c] = (rq * (u - qh * jnp.mean(u * qh, axis=-1, keepdims=True))).astype(BF16)
            dgq_ref[...] += _rowsum8(dqn * qh) * (MEM_D ** -0.5)

        @pl.when(qi == NQ - 1)
        def _():
            for h in range(MEM_H):
                c = slice(MEM_D * h, MEM_D * (h + 1))
                cv = slice(W + MEM_D * h, W + MEM_D * (h + 1))
                k = kv_ref[:, c]
                rk = lax.rsqrt(jnp.mean(k * k, axis=-1, keepdims=True) + EPS)
                kh = k * rk
                dkn = acc[:, c]
                u = dkn * gk_v
                dkv_ref[:, c] = (rk * (u - kh * jnp.mean(u * kh, axis=-1, keepdims=True))).astype(BF16)
                dkv_ref[:, cv] = acc[:, cv].astype(BF16)
                dgk_ref[...] += _rowsum8(dkn * kh)

    vec = pl.BlockSpec((1, LANE), lambda b, q: (0, 0))
    part = pl.BlockSpec((8, LANE), lambda b, q: (b, 0))
    return pl.pallas_call(
        body, name="mem_bwd", grid=(B, NQ),
        in_specs=[pl.BlockSpec((tq, W), lambda b, q: (b * NQ + q, C_MQ // W)),
                  pl.BlockSpec((M, 2 * W), lambda b, q: (b, 0)), pl.BlockSpec((tq, W), lambda b, q: (b * NQ + q, 0)),
                  vec, vec],
        out_specs=[pl.BlockSpec((tq, W), lambda b, q: (b * NQ + q, 0)), pl.BlockSpec((M, 2 * W), lambda b, q: (b, 0)),
                   part, part],
        out_shape=[S((N, W), BF16), S((B * M, 2 * W), BF16), S((B * 8, LANE), F32), S((B * 8, LANE), F32)],
        scratch_shapes=[pltpu.VMEM((M, 2 * W), F32)], compiler_params=_cp(("parallel", "arbitrary")),
    )(z, memkv, dy, gq, gk)


def _merge_fwd(ya, yb, yc, z, x, wa, wb, wc, wo, g_next, tm=512):
    n, d = x.shape
    wdt = ya.shape[1]
    gb = C_GATE // d

    def body(ya_ref, yb_ref, yc_ref, g0_ref, g1_ref, g2_ref, x_ref, wa_ref, wb_ref, wc_ref, wo_ref, gn_ref,
             x1_ref, mg_ref, ua_ref, ub_ref, uc_ref, h_ref):
        merged = jnp.zeros((tm, d), F32)
        for y_ref, g_ref, w_ref, u_ref in ((ya_ref, g0_ref, wa_ref, ua_ref), (yb_ref, g1_ref, wb_ref, ub_ref),
                                           (yc_ref, g2_ref, wc_ref, uc_ref)):
            u = _nn(y_ref[...], w_ref[...])
            u_ref[...] = u.astype(BF16)
            merged = merged + jax.nn.sigmoid(g_ref[...].astype(F32)) * u
        mb = merged.astype(BF16)
        mg_ref[...] = mb
        x1 = x_ref[...] + _nn(mb, wo_ref[...])
        x1_ref[...] = x1
        h_ref[...] = (x1 * lax.rsqrt(jnp.mean(x1 * x1, axis=-1, keepdims=True) + EPS) * gn_ref[...]).astype(BF16)

    yt = pl.BlockSpec((tm, wdt), lambda i: (i, 0))
    xt = pl.BlockSpec((tm, d), lambda i: (i, 0))
    wbr = pl.BlockSpec((wdt, d), lambda i: (0, 0))
    gates = [pl.BlockSpec((tm, d), functools.partial(lambda i, k: (i, gb + k), k=k)) for k in range(3)]
    return pl.pallas_call(
        body, name="merge_fwd", grid=(n // tm,),
        in_specs=[yt, yt, yt] + gates + [xt, wbr, wbr, wbr, pl.BlockSpec((d, d), lambda i: (0, 0)),
                                         pl.BlockSpec((1, d), lambda i: (0, 0))],
        out_specs=[xt] * 6, out_shape=[S((n, d), F32)] + [S((n, d), BF16)] * 5, compiler_params=_cp(("parallel",)),
    )(ya, yb, yc, z, z, z, x, wa, wb, wc, wo, g_next)


def _merge_bwd(dx1, z, ua, ub, uc, wa, wb, wc, wo, tm=512):
    n, d = dx1.shape
    wdt = wa.shape[0]
    gb = C_GATE // d

    def body(dx_ref, g0_ref, g1_ref, g2_ref, ua_ref, ub_ref, uc_ref, wa_ref, wb_ref, wc_ref, wo_ref,
             dg_ref, dya_ref, dyb_ref, dyc_ref, dua_ref, dub_ref, duc_ref):
        dm = _nt(dx_ref[...].astype(BF16), wo_ref[...])
        for k, (g_ref, u_ref, w_ref, dy_ref, du_ref) in enumerate((
                (g0_ref, ua_ref, wa_ref, dya_ref, dua_ref), (g1_ref, ub_ref, wb_ref, dyb_ref, dub_ref),
                (g2_ref, uc_ref, wc_ref, dyc_ref, duc_ref))):
            g = jax.nn.sigmoid(g_ref[...].astype(F32))
            du = (dm * g).astype(BF16)
            du_ref[...] = du
            dg_ref[:, d * k:d * (k + 1)] = (dm * u_ref[...].astype(F32) * g * (1.0 - g)).astype(BF16)
            dy_ref[...] = _nt(du, w_ref[...])

    yt = pl.BlockSpec((tm, wdt), lambda i: (i, 0))
    xt = pl.BlockSpec((tm, d), lambda i: (i, 0))
    wbr = pl.BlockSpec((wdt, d), lambda i: (0, 0))
    gates = [pl.BlockSpec((tm, d), functools.partial(lambda i, k: (i, gb + k), k=k)) for k in range(3)]
    return pl.pallas_call(
        body, name="merge_bwd", grid=(n // tm,),
        in_specs=[xt] + gates + [xt, xt, xt, wbr, wbr, wbr, pl.BlockSpec((d, d), lambda i: (0, 0))],
        out_specs=[pl.BlockSpec((tm, 3 * d), lambda i: (i, 0)), yt, yt, yt, xt, xt, xt],
        out_shape=[S((n, 3 * d), BF16)] + [S((n, wdt), F32)] * 3 + [S((n, d), BF16)] * 3,
        compiler_params=_cp(("parallel",)),
    )(dx1, z, z, z, ua, ub, uc, wa, wb, wc, wo)


FFN_TN = 1408
TN_TM = 2048
INV_SQRT2 = 0.7071067811865476
INV_SQRT_2PI = 0.3989422804014327


def _conv_shifted(a, prev, first, tm):
    row = _iota(a.shape, 0)
    p7 = jnp.where(first, 0.0, prev[7:8, :])
    p6 = jnp.where(first, 0.0, prev[6:7, :])
    a1 = jnp.where(row == 0, p7, pltpu.roll(a, 1, 0))
    a2 = jnp.where(row == 0, p6, jnp.where(row == 1, p7, pltpu.roll(a, 2, 0)))
    return a1, a2


def _ffn_act_fwd(up, cw, cb, B, T, tm=1024):
    N = B * T
    tm = min(tm, T)
    dff = cw.shape[1]
    NT, NJ, tn = T // tm, dff // FFN_TN, FFN_TN

    def body(a_ref, v_ref, cw_ref, cb_ref, y_ref, c_ref, carry):
        t = pl.program_id(2)
        a = a_ref[...].astype(F32)
        a1, a2 = _conv_shifted(a, carry[...], t == 0, tm)
        w = cw_ref[...]
        ac = w[0:1, :] * a2 + w[1:2, :] * a1 + w[2:3, :] * a + cb_ref[...]
        cdf = 0.5 * (1.0 + lax.erf(ac * INV_SQRT2))
        y_ref[...] = (ac * cdf * v_ref[...].astype(F32)).astype(BF16)
        c_ref[...] = cdf.astype(BF16)
        carry[...] = a[tm - 8:tm, :]

    return pl.pallas_call(
        body, name="ffn_act_fwd", grid=(B, NJ, NT),
        in_specs=[pl.BlockSpec((tm, tn), lambda b, j, t: (b * NT + t, j)),
                  pl.BlockSpec((tm, tn), lambda b, j, t: (b * NT + t, NJ + j)),
                  pl.BlockSpec((3, tn), lambda b, j, t: (0, j)), pl.BlockSpec((1, tn), lambda b, j, t: (0, j))],
        out_specs=[pl.BlockSpec((tm, tn), lambda b, j, t: (b * NT + t, j))] * 2, out_shape=[S((N, dff), BF16)] * 2,
        scratch_shapes=[pltpu.VMEM((8, tn), F32)], compiler_params=_cp(("parallel", "parallel", "arbitrary")),
    )(up, up, cw, cb)


def _ffn_down_loss(y, wd, x1, tgt, tm=512):
    n, d = x1.shape
    kf = y.shape[1]

    def body(y_ref, w_ref, x_ref, t_ref, dx_ref, ls_ref):
        err = x_ref[...] + _nn(y_ref[...], w_ref[...]) - t_ref[...]
        dx_ref[...] = err * (1.0 / d)

        @pl.when(pl.program_id(0) == 0)
        def _():
            ls_ref[...] = jnp.zeros_like(ls_ref)

        ls_ref[...] += _rowsum8(err * err) * (0.5 / d)

    xt = pl.BlockSpec((tm, d), lambda i: (i, 0))
    return pl.pallas_call(
        body, name="ffn_down_loss", grid=(n // tm,),
        in_specs=[pl.BlockSpec((tm, kf), lambda i: (i, 0)), pl.BlockSpec((kf, d), lambda i: (0, 0)), xt, xt],
        out_specs=[xt, pl.BlockSpec((8, d), lambda i: (0, 0))], out_shape=[S((n, d), F32), S((8, d), F32)],
        compiler_params=_cp(("arbitrary",)),
    )(y, wd, x1, tgt)


def _ffn_act_bwd1(dx2, wd, up, cdf, cw, cb, B, T, tm=512):
    N = B * T
    tm = min(tm, T)
    d = dx2.shape[1]
    dff = cw.shape[1]
    NT, NJ, tn = T // tm, dff // FFN_TN, FFN_TN

    def body(dx_ref, w_ref, a_ref, v_ref, c_ref, cw_ref, cb_ref, dac_ref, dv_ref, dcw_ref, dcb_ref, carry):
        b, t = pl.program_id(1), pl.program_id(2)
        a = a_ref[...].astype(F32)
        a1, a2 = _conv_shifted(a, carry[...], t == 0, tm)
        carry[...] = a[tm - 8:tm, :]
        w = cw_ref[...]
        ac = w[0:1, :] * a2 + w[1:2, :] * a1 + w[2:3, :] * a + cb_ref[...]
        dy = _nt(dx_ref[...].astype(BF16), w_ref[...])
        cdf = c_ref[...].astype(F32)
        dv_ref[...] = (dy * ac * cdf).astype(BF16)
        dac = dy * v_ref[...].astype(F32) * (cdf + ac * jnp.exp(-0.5 * ac * ac) * INV_SQRT_2PI)
        dac_ref[...] = dac

        @pl.when((b == 0) & (t == 0))
        def _():
            dcw_ref[...] = jnp.zeros_like(dcw_ref)
            dcb_ref[...] = jnp.zeros_like(dcb_ref)

        dcw_ref[0:8, :] += _rowsum8(dac * a2)
        dcw_ref[8:16, :] += _rowsum8(dac * a1)
        dcw_ref[16:24, :] += _rowsum8(dac * a)
        dcb_ref[...] += _rowsum8(dac)

    return pl.pallas_call(
        body, name="ffn_act_bwd1", grid=(NJ, B, NT),
        in_specs=[pl.BlockSpec((tm, d), lambda j, b, t: (b * NT + t, 0)), pl.BlockSpec((tn, d), lambda j, b, t: (j, 0)),
                  pl.BlockSpec((tm, tn), lambda j, b, t: (b * NT + t, j)),
                  pl.BlockSpec((tm, tn), lambda j, b, t: (b * NT + t, NJ + j)),
                  pl.BlockSpec((tm, tn), lambda j, b, t: (b * NT + t, j)),
                  pl.BlockSpec((3, tn), lambda j, b, t: (0, j)), pl.BlockSpec((1, tn), lambda j, b, t: (0, j))],
        out_specs=[pl.BlockSpec((tm, tn), lambda j, b, t: (b * NT + t, j)),
                   pl.BlockSpec((tm, tn), lambda j, b, t: (b * NT + t, j)),
                   pl.BlockSpec((24, tn), lambda j, b, t: (0, j)), pl.BlockSpec((8, tn), lambda j, b, t: (0, j))],
        out_shape=[S((N, dff), F32), S((N, dff), BF16), S((24, dff), F32), S((8, dff), F32)],
        scratch_shapes=[pltpu.VMEM((8, tn), F32)], compiler_params=_cp(("parallel", "arbitrary", "arbitrary")),
    )(dx2, wd, up, up, cdf, cw, cb)


def _ffn_act_bwd2(dac, cw, B, T, tm=1024):
    N = B * T
    tm = min(tm, T)
    dff = cw.shape[1]
    NT, NJ, tn = T // tm, dff // FFN_TN, FFN_TN
    last8 = N // 8 - 1

    def body(d_ref, nx_ref, cw_ref, da_ref):
        t = pl.program_id(2)
        dd = d_ref[...]
        row = _iota(dd.shape, 0)
        last = t == NT - 1
        n0 = jnp.where(last, 0.0, nx_ref[0:1, :])
        n1 = jnp.where(last, 0.0, nx_ref[1:2, :])
        d1 = jnp.where(row == tm - 1, n0, pltpu.roll(dd, tm - 1, 0))
        d2 = jnp.where(row == tm - 1, n1, jnp.where(row == tm - 2, n0, pltpu.roll(dd, tm - 2, 0)))
        w = cw_ref[...]
        da_ref[...] = (w[2:3, :] * dd + w[1:2, :] * d1 + w[0:1, :] * d2).astype(BF16)

    return pl.pallas_call(
        body, name="ffn_act_bwd2", grid=(B, NJ, NT),
        in_specs=[pl.BlockSpec((tm, tn), lambda b, j, t: (b * NT + t, j)),
                  pl.BlockSpec((8, tn), lambda b, j, t: (jnp.minimum((b * NT + t + 1) * (tm // 8), last8), j)),
                  pl.BlockSpec((3, tn), lambda b, j, t: (0, j))],
        out_specs=pl.BlockSpec((tm, tn), lambda b, j, t: (b * NT + t, j)), out_shape=S((N, dff), BF16),
        compiler_params=_cp(("parallel", "parallel", "parallel")),
    )(dac, dac, cw)


def _fold_rows(p, name):
    r, c = p.shape[0] // 8, p.shape[1]

    def body(p_ref, o_ref):
        for j in range(r):
            o_ref[j:j + 1, :] = jnp.sum(p_ref[8 * j:8 * (j + 1), :], axis=0, keepdims=True)

    return pl.pallas_call(body, name=name, out_shape=S((r, c), F32), compiler_params=_cp())(p)


def _small_reduce(lbl, dg_mix, dg_mem, dlb_p, dgn_p, dfb_p, dgq_p, dgk_p, dmq_p, dmk_p, dg_ffn, dcb_p, loss_p):
    d, dff = dg_mix.shape[1], dcb_p.shape[1]
    nbh = dlb_p.shape[0] // (8 * HG_H)

    def colsum(ref):
        return jnp.sum(ref[...], axis=0, keepdims=True)

    def body(lbl_ref, mix_ref, mem_ref, dlb_ref, dgn_ref, dfb_ref, dgq_ref, dgk_ref, dmq_ref, dmk_ref, ffn_ref, dcb_ref,
             ls_ref, o_mix, o_mem, o_lb, o_hgn, o_fb, o_fq, o_fk, o_mq, o_mk, o_ffn, o_cb, o_loss):
        o_mix[...], o_mem[...], o_ffn[...], o_cb[...] = colsum(mix_ref), colsum(mem_ref), colsum(ffn_ref), colsum(dcb_ref)
        o_hgn[...], o_fb[...], o_mq[...], o_mk[...] = colsum(dgn_ref), colsum(dfb_ref), colsum(dmq_ref), colsum(dmk_ref)
        for src, dst in ((dgq_ref, o_fq), (dgk_ref, o_fk)):
            v = colsum(src)
            dst[...] = v + pltpu.roll(v, FOX_D, 1)
        o_loss[...] = jnp.zeros((1, LANE), F32) + jnp.sum(colsum(ls_ref), axis=-1, keepdims=True)
        logits = lbl_ref[...]
        e = jnp.exp(logits - jnp.max(logits, axis=0, keepdims=True))
        pr = e / jnp.sum(e, axis=0, keepdims=True)
        rows = _iota((8, LANE), 0)
        for h in range(HG_H):
            acc = jnp.zeros((8, LANE), F32)
            for b in range(nbh):
                acc = acc + dlb_ref[8 * (b * HG_H + h):8 * (b * HG_H + h + 1), :]
            dlb = jnp.sum(acc, axis=0, keepdims=True)
            c = slice(LANE * h, LANE * (h + 1))
            p0 = pr[0:1, c]
            first = _iota((logits.shape[0], LANE), 0) == 0
            o_lb[:, c] = pr[:, c] * (jnp.where(first, 1.0, 0.0) - p0) * dlb

    outs = [S((1, d), F32), S((1, d), F32), S(lbl.shape, F32)] + [S((1, LANE), F32)] * 6 + \
           [S((1, d), F32), S((1, dff), F32), S((1, LANE), F32)]
    return pl.pallas_call(body, name="small_reduce", out_shape=outs, compiler_params=_cp())(
        lbl, dg_mix, dg_mem, dlb_p, dgn_p, dfb_p, dgq_p, dgk_p, dmq_p, dmk_p, dg_ffn, dcb_p, loss_p)


def _in_col_pieces():
    hw, fw = HG_H * HG_D, FOX_H * FOX_D
    fox0, ff0 = 4 * hw, 4 * hw + 3 * fw
    mq0 = ff0 + FOX_H
    gate0 = mq0 + MEM_H * MEM_D
    pieces = []
    for p in range(FOX_P):
        pieces += [(fox0 + j * fw + LANE * p, LANE) for j in range(3)]
    pieces.append((mq0, MEM_H * MEM_D))
    for h in range(HG_H):
        pieces += [(j * hw + HG_D * h, HG_D) for j in range(4)]
    pieces.append((gate0, C_FF - C_GATE))
    pieces.append((ff0, FOX_H))
    return pieces


def _perm_from_blocks(blocks):
    n_blk, _, c = blocks.shape
    parts = []
    for s, n in _in_col_pieces():
        lo = s
        while lo < s + n:
            d = lo // c
            hi = min(s + n, (d + 1) * c)
            parts.append(blocks[d][:, lo - d * c:hi - d * c])
            lo = hi
    parts.append(jnp.zeros((blocks.shape[1], C_END - C_FF - FOX_H), blocks.dtype))
    return jnp.concatenate(parts, axis=1)


def _unperm_blocks(segs, n_blk):
    starts = [0]
    for a in segs:
        starts.append(starts[-1] + a.shape[1])
    new_start, placed = 0, []
    for s, n in _in_col_pieces():
        placed.append((s, new_start, n))
        new_start += n
    placed.sort()
    c = sum(n for _, _, n in placed) // n_blk
    blocks = []
    for d in range(n_blk):
        parts = []
        for s, ns, n in placed:
            lo, hi = max(s, d * c), min(s + n, (d + 1) * c)
            if lo < hi:
                i = max(j for j in range(len(segs)) if starts[j] <= ns)
                parts.append(segs[i][:, ns + lo - s - starts[i]:ns + hi - s - starts[i]])
        blocks.append(jnp.concatenate(parts, axis=1))
    return jnp.stack(blocks)


def _local_step(x2, mem2, tgt, sm, W, B, T, M, ex=None):
    fbias = jnp.pad(sm["fox_f_bias"], ((0, 0), (0, LANE - FOX_H)))
    gq2 = jnp.concatenate([sm["fox_q_norm_g"]] * 2, axis=1)
    gk2 = jnp.concatenate([sm["fox_k_norm_g"]] * 2, axis=1)
    lbl = sm["hgrn_lb_logits"]
    if ex:
        h, *first = _rmsnorm_cast(x2, sm["norm_mix_g"], "norm_mix", gather=ex.first_blocks())
        W = ex.unpack_first(first)
    else:
        h = _rmsnorm_cast(x2, sm["norm_mix_g"], "norm_mix")
    z = _mm_nn(h, W["w_in"], BF16, "proj_in", 512, C_END)
    memn = _rmsnorm_cast(mem2, sm["norm_mem_g"], "norm_mem", tm=256)
    memkv = _mm_nn(memn, W["mem_kv_w"], F32, "proj_memkv", 256, 512)
    ya, o_raw, states, a_mat = _hgrn_fwd(z, lbl, sm["hgrn_norm_g"], B, T)
    fc, fct = _fox_gate_fwd(z, fbias, B, T)
    yb, lse, *late = _fox_fwd(z, fc, fct, gq2, gk2, B, T, gather=ex.late_blocks() if ex else ())
    if ex:
        W = {**W, **ex.unpack_late(late)}
    yc = _mem_fwd(z, memkv, sm["mem_q_norm_g"], sm["mem_k_norm_g"], B, T, M)
    x1, merged, ua, ub, uc, h2 = _merge_fwd(ya, yb, yc, z, x2, W["w_br_hgrn"], W["w_br_fox"], W["w_br_mem"], W["w_out"],
                                            sm["norm_ffn_g"])
    up = _mm_nn(h2, W["ffn_w_up"], BF16, "ffn_up", 512, 2 * FFN_TN)
    yf, cdf = _ffn_act_fwd(up, W["ffn_conv_w"], sm["ffn_conv_b"], B, T)
    dx2, loss_p = _ffn_down_loss(yf, W["ffn_w_down"], x1, tgt)
    dff = W["ffn_conv_w"].shape[1]
    dac, dv, dcw_p, dcb_p = _ffn_act_bwd1(dx2, W["ffn_w_down"], up, cdf, W["ffn_conv_w"], sm["ffn_conv_b"], B, T)
    da = _ffn_act_bwd2(dac, W["ffn_conv_w"], B, T)
    g = {"ffn_conv_w": _fold_rows(dcw_p, "g_conv_w")}
    g["ffn_w_down"] = _mm_tn(yf, dx2, "g_w_down", TN_TM, 512)
    dh2 = _mm_nt_sum([(da, 0, dff, 0), (dv, 0, dff, dff)], W["ffn_w_up"], "dh2", 512)
    g["ffn_w_up"] = [_mm_tn(h2, da, "g_w_up_a", TN_TM, dff), _mm_tn(h2, dv, "g_w_up_v", TN_TM, dff)]
    dx1, dg_ffn = _rmsnorm_bwd(dh2, x1, sm["norm_ffn_g"], dx2, "norm_ffn_bwd")
    g["w_out"] = _mm_tn(merged, dx1, "g_w_out", TN_TM, 1024)
    dgate, dya, dyb, dyc, dua, dub, duc = _merge_bwd(dx1, z, ua, ub, uc, W["w_br_hgrn"], W["w_br_fox"], W["w_br_mem"],
                                                    W["w_out"])
    g["w_br_hgrn"] = _mm_tn(ya, dua, "g_w_br_hgrn", TN_TM, 1024)
    g["w_br_fox"] = _mm_tn(yb, dub, "g_w_br_fox", TN_TM, 1024)
    g["w_br_mem"] = _mm_tn(yc, duc, "g_w_br_mem", TN_TM, 1024)
    early_pk = ex.early_grads(g) if ex else ()
    dz_hg, dlb_p, dgn_p, *early_sib = _hgrn_bwd(z, o_raw, states, a_mat, dya, lbl, sm["hgrn_norm_g"], B, T,
                                                swap_sibling=early_pk)
    dz_fox, dfc, dgq_p, dgk_p, *early_chips = _fox_bwd(z, dyb, yb, lse, fc, fct, gq2, gk2, B, T,
                                                       swap=ex.pair_sums(early_pk, early_sib, "early") if ex else ())
    dz_ff, dfb_p = _fox_gate_bwd(dfc, z, fbias, B, T)
    dz_mq, dkv, dmq_p, dmk_p = _mem_bwd(z, memkv, dyc, sm["mem_q_norm_g"], sm["mem_k_norm_g"], B, T, M)
    g["mem_kv_w"] = _mm_tn(memn, dkv, "g_mem_kv_w", 256, 512)
    dmemn = _mm_nt_sum([(dkv, 0, dkv.shape[1], 0)], W["mem_kv_w"], "d_memn", 256)
    _, dg_mem = _rmsnorm_bwd(dmemn, mem2, sm["norm_mem_g"], None, "norm_mem_bwd", tm=256)
    d = x2.shape[1]
    parts = [(dz_fox, 0, C_MQ - C_FOX, C_FOX), (dz_mq, 0, C_HG - C_MQ, C_MQ), (dz_hg, 0, C_GATE - C_HG, C_HG)]
    parts += [(dgate, d * k, d, C_GATE + d * k) for k in range(3)] + [(dz_ff, 0, C_END - C_FF, C_FF)]
    g["w_in"] = [_mm_tn(h, dzs, "g_w_in_%d" % i, 2 * TN_TM,
                        max(t for t in (1024, 768, 512, LANE) if dzs.shape[1] % t == 0))
                 for i, dzs in enumerate((dz_fox, dz_mq, dz_hg, dgate, dz_ff))]
    sums = None
    if ex:
        last_pk = ex.last_grads(g)
        last_sib = _swap_with_sibling(last_pk, "rs_sibling_last")
        dh, last_chips = _mm_nt_sum(parts, W["w_in"], "dh", 512, swap=ex.pair_sums(last_pk, last_sib, "last"))
        sums = (ex.final_sums(early_pk, early_sib, early_chips, "early"),
                ex.final_sums(last_pk, last_sib, last_chips, "last"))
    else:
        dh = _mm_nt_sum(parts, W["w_in"], "dh", 512)
    grad_x, dg_mix = _rmsnorm_bwd(dh, x2, sm["norm_mix_g"], dx1, "norm_mix_bwd")
    small = _small_reduce(lbl, dg_mix, dg_mem, dlb_p, dgn_p, dfb_p, dgq_p, dgk_p, dmq_p, dmk_p, dg_ffn, dcb_p, loss_p)
    names = ("norm_mix_g", "norm_mem_g", "hgrn_lb_logits", "hgrn_norm_g", "fox_f_bias", "fox_q_norm_g", "fox_k_norm_g",
             "mem_q_norm_g", "mem_k_norm_g", "norm_ffn_g", "ffn_conv_b", "loss")
    g.update(dict(zip(names, small)))
    return grad_x, g, sums


ANY = pl.BlockSpec(memory_space=pl.ANY)


def _position():
    return lax.axis_index("x"), lax.axis_index("y"), lax.axis_index("c")


def _all_gather(blocks, name):
    nb = len(blocks)

    def body(*refs):
        start, forward, finish = _gather_phases(refs[:nb], refs[nb:2 * nb], *refs[2 * nb:])
        start()
        forward()
        finish()

    return pl.pallas_call(
        body, name=name, out_shape=_gather_shapes(blocks), in_specs=[ANY] * nb, out_specs=[ANY] * nb,
        scratch_shapes=_gather_sems(nb),
    )(*blocks)


def _hosting(body, n_in, n_out, n_scratch, n_x, make_phases, grid):
    n_steps = math.prod(grid)

    def hosted(*refs):
        a = n_in + n_x
        b = a + n_out + n_x
        ins, xs = refs[:n_in], refs[n_in:a]
        outs, x_outs = refs[a:a + n_out], refs[a + n_out:b]
        scratch, sems = refs[b:b + n_scratch], refs[b + n_scratch:]
        step = 0
        for ax, n in enumerate(grid):
            step = step * n + pl.program_id(ax)
        phases = make_phases(xs, x_outs, *sems)
        pl.when(step == 0)(phases[0])
        for ph in phases[1:-1]:
            pl.when(step == n_steps // 2)(ph)
        body(*ins, *outs, *scratch)
        pl.when(step == n_steps - 1)(phases[-1])

    return hosted


def _gather_shapes(blocks):
    return [S((N_DEV,) + b.shape, b.dtype) for b in blocks]


def _gather_sems(nb):
    return [pltpu.SemaphoreType.DMA((7 * nb,)), pltpu.SemaphoreType.DMA((7 * nb,)), pltpu.SemaphoreType.DMA((nb,))]


def _gather_phases(x_refs, out_refs, send_sems, recv_sems, local_sems):
    nb = len(x_refs)
    x, y, c = _position()
    me, sibling = (x, y, c), (x, y, 1 - c)
    chips = [(1 - x, y), (x, 1 - y), (1 - x, 1 - y)]

    def copy(i, k, blk, to, own=False):
        px, py, pc = blk
        slot = out_refs[i].at[4 * px + 2 * py + pc]
        return pltpu.make_async_remote_copy(
            src_ref=x_refs[i] if own else slot, dst_ref=slot, send_sem=send_sems.at[7 * i + k],
            recv_sem=recv_sems.at[7 * i + k], device_id=to, device_id_type=MESH)

    def mine(i):
        return pltpu.make_async_copy(x_refs[i], out_refs[i].at[4 * x + 2 * y + c], local_sems.at[i])

    def first(i):
        return [copy(i, 0, me, sibling, own=True)] + [copy(i, 1 + j, me, (*chip, c), own=True)
                                                     for j, chip in enumerate(chips)]

    def passed(i, j):
        return copy(i, 4 + j, (*chips[j], c), sibling)

    def start():
        for i in range(nb):
            mine(i).start()
            for cp in first(i):
                cp.start()

    def forward():
        for i in range(nb):
            for j, chip in enumerate(chips):
                copy(i, 1 + j, (*chip, c), me).wait_recv()
                passed(i, j).start()

    def finish():
        for i in range(nb):
            copy(i, 0, sibling, me).wait_recv()
            for j, chip in enumerate(chips):
                copy(i, 4 + j, (*chip, 1 - c), me).wait_recv()
        for i in range(nb):
            for cp in first(i) + [passed(i, j) for j in range(3)]:
                cp.wait_send()
            mine(i).wait()

    return start, forward, finish


def _swap_with_sibling(pks, name):
    nb = len(pks)

    def body(*refs):
        start, finish = _sibling_swap_phases(refs[:nb], refs[nb:2 * nb], *refs[2 * nb:])
        start()
        finish()

    return pl.pallas_call(
        body, name=name, out_shape=_sibling_swap_shapes(pks), in_specs=[ANY] * nb, out_specs=[ANY] * nb,
        scratch_shapes=_sibling_swap_sems(nb),
    )(*pks)


def _sibling_swap_shapes(pks):
    return [S((4,) + p.shape[1:], p.dtype) for p in pks]


def _sibling_swap_sems(nb):
    return [pltpu.SemaphoreType.DMA((4 * nb,)), pltpu.SemaphoreType.DMA((4 * nb,))]


def _sibling_swap_phases(pk_refs, out_refs, send_sems, recv_sems):
    nb = len(pk_refs)
    x, y, c = _position()

    def copies():
        return [pltpu.make_async_remote_copy(
            src_ref=pk_refs[i].at[2 * k + 1 - c], dst_ref=out_refs[i].at[k], send_sem=send_sems.at[4 * i + k],
            recv_sem=recv_sems.at[4 * i + k], device_id=(x, y, 1 - c), device_id_type=MESH)
            for i in range(nb) for k in range(4)]

    def start():
        for cp in copies():
            cp.start()

    def finish():
        for cp in copies():
            cp.wait()

    return start, finish


def _swap_between_chips(pbs, name):
    nb = len(pbs)

    def body(*refs):
        start, finish = _chip_swap_phases(refs[:nb], refs[nb:2 * nb], *refs[2 * nb:])
        start()
        finish()

    return pl.pallas_call(
        body, name=name, out_shape=[S(p.shape, p.dtype) for p in pbs], in_specs=[ANY] * nb, out_specs=[ANY] * nb,
        scratch_shapes=_chip_swap_sems(nb),
    )(*pbs)


def _chip_swap_sems(nb):
    return [pltpu.SemaphoreType.DMA((3 * nb,)), pltpu.SemaphoreType.DMA((3 * nb,)), pltpu.SemaphoreType.DMA((nb,))]


def _chip_swap_phases(pb_refs, out_refs, send_sems, recv_sems, local_sems):
    nb = len(pb_refs)
    x, y, c = _position()
    me = 2 * x + y
    chips = [(1 - x, y), (x, 1 - y), (1 - x, 1 - y)]

    def local(i):
        return pltpu.make_async_copy(pb_refs[i].at[me], out_refs[i].at[me], local_sems.at[i])

    def send(i, j):
        cx, cy = chips[j]
        return pltpu.make_async_remote_copy(
            src_ref=pb_refs[i].at[2 * cx + cy], dst_ref=out_refs[i].at[me], send_sem=send_sems.at[3 * i + j],
            recv_sem=recv_sems.at[3 * i + j], device_id=(cx, cy, c), device_id_type=MESH)

    def arrival(i, j):
        cx, cy = chips[j]
        return pltpu.make_async_remote_copy(
            src_ref=pb_refs[i].at[me], dst_ref=out_refs[i].at[2 * cx + cy], send_sem=send_sems.at[3 * i + j],
            recv_sem=recv_sems.at[3 * i + j], device_id=(cx, cy, c), device_id_type=MESH)

    def start():
        for i in range(nb):
            local(i).start()
            for j in range(3):
                send(i, j).start()

    def finish():
        for i in range(nb):
            for j in range(3):
                arrival(i, j).wait_recv()
        for i in range(nb):
            for j in range(3):
                send(i, j).wait_send()
            local(i).wait()

    return start, finish


def _row_tile(r):
    return max(t for t in range(16, min(r, 1024) + 1, 16) if r % t == 0)


def _pair_sum_cast(pk, recv, core, name):
    _, r, l = pk.shape
    tr = _row_tile(r)

    def body(c_ref, a_ref, b_ref, o_ref):
        o_ref[...] = (a_ref[...] + b_ref[...]).astype(BF16)

    return pl.pallas_call(
        body, name=name,
        grid_spec=pltpu.PrefetchScalarGridSpec(
            num_scalar_prefetch=1, grid=(4, r // tr),
            in_specs=[pl.BlockSpec((None, tr, l), lambda k, i, c: (2 * k + c[0], i, 0)),
                      pl.BlockSpec((None, tr, l), lambda k, i, c: (k, i, 0))],
            out_specs=pl.BlockSpec((None, tr, l), lambda k, i, c: (k, i, 0))),
        out_shape=S((4, r, l), BF16), compiler_params=_cp(("parallel", "parallel")),
    )(core, pk, recv)


def _final_sum(pk, recv_sib, recv_chips, slot, chip, name):
    _, r, l = pk.shape
    tr = _row_tile(r)

    def body(s_ref, k_ref, a_ref, b_ref, rc_ref, o_ref):
        base = a_ref[...] + b_ref[...]
        acc = jnp.zeros_like(base)
        for j in range(4):
            acc = acc + jnp.where(k_ref[0] == j, base, rc_ref[j].astype(F32))
        o_ref[...] = acc

    return pl.pallas_call(
        body, name=name,
        grid_spec=pltpu.PrefetchScalarGridSpec(
            num_scalar_prefetch=2, grid=(r // tr,),
            in_specs=[pl.BlockSpec((None, tr, l), lambda i, s, k: (s[0], i, 0)),
                      pl.BlockSpec((None, tr, l), lambda i, s, k: (k[0], i, 0)),
                      pl.BlockSpec((4, tr, l), lambda i, s, k: (0, i, 0))],
            out_specs=pl.BlockSpec((tr, l), lambda i, s, k: (i, 0))),
        out_shape=S((r, l), F32), compiler_params=_cp(("parallel",)),
    )(slot, chip, pk, recv_sib, recv_chips)


def _adamw_math(w, g, m, v):
    m = ADAM_B1 * m + (1.0 - ADAM_B1) * g
    v = ADAM_B2 * v + (1.0 - ADAM_B2) * (g * g)
    m_hat = m / (1.0 - ADAM_B1 ** ADAM_STEP)
    v_hat = v / (1.0 - ADAM_B2 ** ADAM_STEP)
    return -ADAM_LR * (m_hat / (jnp.sqrt(v_hat) + ADAM_EPS) + ADAM_WD * w), m, v


def _adamw(w, g, m, v, name):
    r, c = w.shape
    tr = 512 if r % 512 == 0 else r

    def body(w_ref, g_ref, m_ref, v_ref, d_ref, nm_ref, nv_ref):
        d_ref[...], nm_ref[...], nv_ref[...] = _adamw_math(w_ref[...], g_ref[...], m_ref[...], v_ref[...])

    tile = pl.BlockSpec((tr, c), lambda i: (i, 0))
    return pl.pallas_call(
        body, name=name, grid=(r // tr,), in_specs=[tile] * 4, out_specs=[tile] * 3, out_shape=[S((r, c), F32)] * 3,
        compiler_params=_cp(("parallel",)),
    )(w, g, m, v)


def _small_update(gathered, w, m, v):
    def body(ga_ref, w_ref, m_ref, v_ref, g_ref, d_ref, nm_ref, nv_ref):
        g = ga_ref[0]
        for k in range(1, N_DEV):
            g = g + ga_ref[k]
        g_ref[...] = g
        d_ref[...], nm_ref[...], nv_ref[...] = _adamw_math(w_ref[...], g, m_ref[...], v_ref[...])

    return pl.pallas_call(body, name="small_update", out_shape=[S(w.shape, F32)] * 4, compiler_params=_cp())(
        gathered, w, m, v)


BIG = ("w_in", "mem_kv_w", "w_br_hgrn", "w_br_fox", "w_br_mem", "w_out", "ffn_w_up", "ffn_conv_w", "ffn_w_down")
GROUP_ROWS = ("w_out", "ffn_w_down")
GROUP_LANE = ("w_br_hgrn", "w_br_fox", "w_br_mem")
LANE_GROUP_ROWS = 224
SMALL = ("norm_mix_g", "norm_mem_g", "hgrn_lb_logits", "hgrn_norm_g", "fox_f_bias", "fox_q_norm_g", "fox_k_norm_g",
         "mem_q_norm_g", "mem_k_norm_g", "norm_ffn_g", "ffn_conv_b")


def _rows_of(n_elems):
    return -(-n_elems // LANE)


def _to_rows(a, lead=0):
    flat = a.reshape(a.shape[:lead] + (-1,))
    pad = (-flat.shape[-1]) % LANE
    if pad:
        flat = jnp.pad(flat, [(0, 0)] * lead + [(0, pad)])
    return flat.reshape(a.shape[:lead] + (-1, LANE))


def _stack_rows(parts, lead, total_rows):
    buf = jnp.concatenate(parts, axis=lead)
    pad = total_rows - buf.shape[lead]
    return jnp.pad(buf, [(0, 0)] * lead + [(0, pad), (0, 0)])


def _round_up(n, k):
    return -(-n // k) * k


def _from_rows(rows, shape, lead=0):
    n = math.prod(shape)
    return rows.reshape(rows.shape[:lead] + (-1,))[..., :n].reshape(rows.shape[:lead] + tuple(shape))


def _blocks_to_full(blocks, kind):
    n, a, b = blocks.shape
    return blocks.transpose(1, 0, 2).reshape(a, n * b) if kind == "col" else blocks.reshape(n * a, b)


def _full_to_blocks(full, kind, n=N_DEV):
    a, b = full.shape
    return full.reshape(a, n, b // n).transpose(1, 0, 2) if kind == "col" else full.reshape(n, a // n, b)


def _lane_group_rows(shard):
    n_lane = sum(shard[n].shape[0] for n in GROUP_LANE)
    n_cw = shard["ffn_conv_w"].size
    return n_lane, _rows_of(3 * n_cw), _rows_of(n_cw), _round_up(n_lane + _rows_of(3 * n_cw), LANE_GROUP_ROWS)


def _split_bf16x3(x):
    hi = x.astype(BF16)
    r1 = x - hi.astype(F32)
    mid = r1.astype(BF16)
    return jnp.stack([hi, mid, (r1 - mid.astype(F32)).astype(BF16)])


class _Exchange:
    def __init__(self, shard):
        self.shard = shard
        xi, yi, ci = _position()
        self.core = ci.astype(jnp.int32).reshape(1)
        self.chip = (2 * xi + yi).astype(jnp.int32).reshape(1)
        self.n_lane, self.r_pieces, self.r_vals, self.r_lane = _lane_group_rows(shard)

    def first_blocks(self):
        return [self.shard["w_in"].astype(BF16), self.shard["mem_kv_w"].astype(BF16)]

    def unpack_first(self, gathered):
        return {"w_in": _perm_from_blocks(gathered[0]), "mem_kv_w": _blocks_to_full(gathered[1], "row")}

    def late_blocks(self):
        sh = self.shard
        lane_rows = [sh[n].astype(BF16) for n in GROUP_LANE] + [_to_rows(_split_bf16x3(sh["ffn_conv_w"]))]
        return [sh[n].astype(BF16) for n in GROUP_ROWS] + [sh["ffn_w_up"].astype(BF16),
                                                           _stack_rows(lane_rows, 0, self.r_lane)]

    def unpack_late(self, gathered):
        *rows, gc, gd = gathered
        sh = self.shard
        W = {"ffn_w_up": _blocks_to_full(gc, "col")}
        for n, blocks in zip(GROUP_ROWS, rows):
            W[n] = _blocks_to_full(blocks, "row")
        r0 = 0
        for n in GROUP_LANE:
            W[n] = _blocks_to_full(gd[:, r0:r0 + sh[n].shape[0]], "col")
            r0 += sh[n].shape[0]
        cw = _from_rows(gd[:, self.n_lane:self.n_lane + self.r_pieces], (3,) + sh["ffn_conv_w"].shape, lead=1).astype(F32)
        W["ffn_conv_w"] = _blocks_to_full(cw[:, 0] + cw[:, 1] + cw[:, 2], "col")
        return W

    def early_grads(self, g):
        cw_rows = _to_rows(_full_to_blocks(g["ffn_conv_w"], "col"), lead=1)
        return [_full_to_blocks(g[n], "row") for n in GROUP_ROWS] + [
            jnp.concatenate([_full_to_blocks(h, "col", N_DEV // 2) for h in g["ffn_w_up"]], axis=0),
            _stack_rows([_full_to_blocks(g[n], "col") for n in GROUP_LANE] + [cw_rows], 1, self.r_lane)]

    def last_grads(self, g):
        return [_unperm_blocks(g["w_in"], N_DEV), _full_to_blocks(g["mem_kv_w"], "row")]

    def pair_sums(self, pks, recv_sib, tag):
        return [_pair_sum_cast(p, r, self.core, "rs_pair_sum_%s%d" % (tag, i))
                for i, (p, r) in enumerate(zip(pks, recv_sib))]

    def final_sums(self, pks, recv_sib, recv_chips, tag):
        return [_final_sum(p, rs, rc, 2 * self.chip + self.core, self.chip, "rs_final_sum_%s%d" % (tag, i))
                for i, (p, rs, rc) in enumerate(zip(pks, recv_sib, recv_chips))]

    def unpack_grads(self, early, last):
        sh = self.shard
        *rows, g_up, g_lane = early
        g_shard = {"w_in": last[0], "mem_kv_w": last[1], "ffn_w_up": g_up, **dict(zip(GROUP_ROWS, rows))}
        r0 = 0
        for n in GROUP_LANE:
            g_shard[n] = g_lane[r0:r0 + sh[n].shape[0]]
            r0 += sh[n].shape[0]
        g_shard["ffn_conv_w"] = _from_rows(g_lane[self.n_lane:self.n_lane + self.r_vals], sh["ffn_conv_w"].shape)
        return g_shard


def kernel(x, mem, norm_mix_g, norm_mem_g, w_in, hgrn_lb_logits, hgrn_norm_g, fox_f_bias, fox_q_norm_g, fox_k_norm_g, mem_kv_w, mem_q_norm_g, mem_k_norm_g, w_br_hgrn, w_br_fox, w_br_mem, w_out, norm_ffn_g, ffn_w_up, ffn_conv_w, ffn_conv_b, ffn_w_down, loss_target, m_norm_mix_g, m_norm_mem_g, m_w_in, m_hgrn_lb_logits, m_hgrn_norm_g, m_fox_f_bias, m_fox_q_norm_g, m_fox_k_norm_g, m_mem_kv_w, m_mem_q_norm_g, m_mem_k_norm_g, m_w_br_hgrn, m_w_br_fox, m_w_br_mem, m_w_out, m_norm_ffn_g, m_ffn_w_up, m_ffn_conv_w, m_ffn_conv_b, m_ffn_w_down, v_norm_mix_g, v_norm_mem_g, v_w_in, v_hgrn_lb_logits, v_hgrn_norm_g, v_fox_f_bias, v_fox_q_norm_g, v_fox_k_norm_g, v_mem_kv_w, v_mem_q_norm_g, v_mem_k_norm_g, v_w_br_hgrn, v_w_br_fox, v_w_br_mem, v_w_out, v_norm_ffn_g, v_ffn_w_up, v_ffn_conv_w, v_ffn_conv_b, v_ffn_w_down):
    given = dict(locals())
    order = ("norm_mix_g", "norm_mem_g", "w_in", "hgrn_lb_logits", "hgrn_norm_g", "fox_f_bias", "fox_q_norm_g",
             "fox_k_norm_g", "mem_kv_w", "mem_q_norm_g", "mem_k_norm_g", "w_br_hgrn", "w_br_fox", "w_br_mem", "w_out",
             "norm_ffn_g", "ffn_w_up", "ffn_conv_w", "ffn_conv_b", "ffn_w_down")
    B, T, D = x.shape
    M = mem.shape[1]
    shard = {n: given[n][0] if n in BIG else given[n] for n in order}
    mom = {n: (given["m_" + n][0], given["v_" + n][0]) if n in BIG else (given["m_" + n], given["v_" + n])
           for n in order}
    shard["hgrn_lb_logits"] = hgrn_lb_logits
    for n in ("norm_mix_g", "norm_mem_g", "hgrn_norm_g", "fox_f_bias", "fox_q_norm_g", "fox_k_norm_g", "mem_q_norm_g",
              "mem_k_norm_g", "norm_ffn_g", "ffn_conv_b"):
        shard[n] = given[n].reshape(1, -1)

    ex = _Exchange(shard)
    sm = {n: shard[n] for n in SMALL}
    grad_x, g, sums = _local_step(x.reshape(B * T, D), mem.reshape(B * M, D), loss_target.reshape(B * T, D), sm, None,
                                  B, T, M, ex)
    g_shard = ex.unpack_grads(*sums)

    sg = {n: g[n] for n in SMALL}
    sg["fox_f_bias"] = g["fox_f_bias"][:, :FOX_H]
    sg["fox_q_norm_g"] = g["fox_q_norm_g"][:, :FOX_D]
    sg["fox_k_norm_g"] = g["fox_k_norm_g"][:, :FOX_D]
    slayout, row0 = {}, 0
    for n in SMALL:
        nr = _rows_of(shard[n].size)
        slayout[n] = (row0, nr)
        row0 += nr
    loss_row = row0
    r_small = _round_up(row0 + 1, 8)

    def pack_small(d, with_loss=None):
        rows = [_to_rows(d[n]) for n in SMALL]
        rows.append(with_loss if with_loss is not None else jnp.zeros((1, LANE), F32))
        return _stack_rows(rows, 0, r_small)

    sgath, = _all_gather([pack_small(sg, g["loss"])], "ag_small")
    s_g, s_d, s_m, s_v = _small_update(sgath, pack_small(shard), pack_small({n: mom[n][0].reshape(shard[n].shape) for n in SMALL}),
                                       pack_small({n: mom[n][1].reshape(shard[n].shape) for n in SMALL}))
    loss = s_g[loss_row, 0]

    grads, deltas, new_m, new_v = {}, {}, {}, {}
    for n in BIG:
        gn = g_shard[n]
        d, nm, nv = _adamw(shard[n], gn, mom[n][0], mom[n][1], "adamw_" + n)
        grads[n], deltas[n], new_m[n], new_v[n] = (a[None] for a in (gn, d, nm, nv))
    for n in SMALL:
        r0, nr = slayout[n]
        for dst, src in ((grads, s_g), (deltas, s_d), (new_m, s_m), (new_v, s_v)):
            dst[n] = _from_rows(src[r0:r0 + nr], given[n].shape)
    return (loss, grad_x.reshape(B, T, D), *[grads[n] for n in order], *[deltas[n] for n in order],
            *[new_m[n] for n in order], *[new_v[n] for n in order])
```
